```python
import math
import jax, jax.numpy as jnp
from jax import lax
import numpy as np

D_MODEL = 1024
BATCH = 8
SEQ = 4096
DEPTH = 4

HEAD_DIM = 64
SWA_Q_HEADS = 8
SWA_KV_HEADS = 2
SWA_GROUP = SWA_Q_HEADS // SWA_KV_HEADS
WINDOW = 128
FOX_HEADS = 4
MLA_HEADS = 4
MLA_Q_RANK = 256
MLA_KV_RANK = 128
MLA_NOPE_DIM = 64
MLA_ROPE_DIM = 32
MLA_V_DIM = 64
ROPE_THETA = 10000.0
REL_BUCKETS = 32
REL_MAX_DIST = 128
D_FF = 2816
CONV_WIDTH = 3
Q_BLOCK = 128
EPS = 1e-6
NEG_INF = -1e30

SWA_WIDTH = SWA_Q_HEADS * HEAD_DIM
FOX_WIDTH = FOX_HEADS * HEAD_DIM
MLA_WIDTH = MLA_HEADS * MLA_V_DIM
MIX_WIDTH = SWA_WIDTH + FOX_WIDTH + MLA_WIDTH
SWA_COLS = (SWA_Q_HEADS + 2 * SWA_KV_HEADS) * HEAD_DIM
FOX_COLS = 3 * FOX_HEADS * HEAD_DIM + FOX_HEADS
MLA_COLS = MLA_Q_RANK + MLA_KV_RANK + MLA_ROPE_DIM
IN_COLS = SWA_COLS + FOX_COLS + MLA_COLS
MLA_QK_DIM = MLA_NOPE_DIM + MLA_ROPE_DIM

kernel_name = "hymba_swa_fox_mla_convffn_trunk"


def rmsnorm(x, g):
    xf = x.astype(jnp.float32)
    y = xf * lax.rsqrt(jnp.mean(xf * xf, axis=-1, keepdims=True) + EPS) * g.astype(jnp.float32)
    return y.astype(x.dtype)


def t5_causal_bucket(dist):
    max_exact = REL_BUCKETS // 2
    d = jnp.maximum(dist, 0)
    log_ratio = jnp.log(jnp.maximum(d, 1).astype(jnp.float32) / max_exact) / math.log(REL_MAX_DIST / max_exact)
    large = max_exact + (log_ratio * (REL_BUCKETS - max_exact)).astype(jnp.int32)
    large = jnp.minimum(large, REL_BUCKETS - 1)
    return jnp.where(d < max_exact, d, large)


def apply_rope(t, cos, sin):
    t1, t2 = jnp.split(t, 2, axis=-1)
    return jnp.concatenate([t1 * cos - t2 * sin, t1 * sin + t2 * cos], axis=-1)


def swa_sink_attention(q, k, v, sinks, rel_bias):
    B, S = q.shape[0], q.shape[1]
    nb = S // WINDOW
    qb = q.reshape(B, nb, WINDOW, SWA_KV_HEADS, SWA_GROUP, HEAD_DIM)

    def band(t):
        tb = t.reshape(B, nb, WINDOW, SWA_KV_HEADS, HEAD_DIM)
        prev = jnp.pad(tb, ((0, 0), (1, 0), (0, 0), (0, 0), (0, 0)))[:, :-1]
        return jnp.concatenate([prev, tb], axis=2)

    kb, vb = band(k), band(v)
    qi = jnp.arange(WINDOW, dtype=jnp.int32)[:, None] + WINDOW
    kj = jnp.arange(2 * WINDOW, dtype=jnp.int32)[None, :]
    dist = qi - kj
    in_band = (dist >= 0) & (dist < WINDOW)
    valid_key = (jnp.arange(nb)[:, None, None] > 0) | (kj >= WINDOW)[None]
    mask = in_band[None] & valid_key
    bias = rel_bias.astype(jnp.float32)[t5_causal_bucket(dist)]
    bias = bias.transpose(2, 0, 1).reshape(SWA_KV_HEADS, SWA_GROUP, WINDOW, 2 * WINDOW)
    s = jnp.einsum('bnqhgd,bnkhd->bnhgqk', qb, kb, preferred_element_type=jnp.float32)
    s = s * (HEAD_DIM ** -0.5) + bias
    s = jnp.where(mask[None, :, None, None], s, NEG_INF)
    sink = sinks.astype(jnp.float32).reshape(1, 1, SWA_KV_HEADS, SWA_GROUP, 1, 1)
    sink = jnp.broadcast_to(sink, s.shape[:-1] + (1,))
    p = jax.nn.softmax(jnp.concatenate([s, sink], axis=-1), axis=-1)[..., :-1]
    o = jnp.einsum('bnhgqk,bnkhd->bnqhgd', p.astype(v.dtype), vb)
    return o.reshape(B, S, SWA_Q_HEADS * HEAD_DIM)


def blocked_causal_attention(q, k, v, scale, log_forget_cum=None):
    B, S, H = q.shape[0], q.shape[1], q.shape[2]
    nb = S // Q_BLOCK
    q_blocks = q.reshape(B, nb, Q_BLOCK, H, q.shape[-1]).swapaxes(0, 1)
    k_pos = jnp.arange(S, dtype=jnp.int32)
    if log_forget_cum is None:
        f_blocks, f_keys = None, None
    else:
        f_t = log_forget_cum.transpose(0, 2, 1)
        f_blocks = f_t.reshape(B, H, nb, Q_BLOCK).transpose(2, 0, 1, 3)
        f_keys = f_t

    def one_block(args):
        qb, fb, i = args
        s = jnp.einsum('bqhd,bkhd->bhqk', qb, k, preferred_element_type=jnp.float32) * scale
        if fb is not None:
            s = s + (fb[..., :, None] - f_keys[..., None, :])
        q_pos = i * Q_BLOCK + jnp.arange(Q_BLOCK, dtype=jnp.int32)
        s = jnp.where(k_pos[None, :] <= q_pos[:, None], s, NEG_INF)
        p = jax.nn.softmax(s, axis=-1)
        return jnp.einsum('bhqk,bkhd->bqhd', p.astype(v.dtype), v)

    out = lax.map(one_block, (q_blocks, f_blocks, jnp.arange(nb, dtype=jnp.int32)))
    return out.swapaxes(0, 1).reshape(B, S, H * v.shape[-1])


def causal_depthwise_conv(u, w, b):
    S = u.shape[1]
    up = jnp.pad(u, ((0, 0), (CONV_WIDTH - 1, 0), (0, 0)))
    y = b.astype(u.dtype)
    for tap in range(CONV_WIDTH):
        y = y + w[tap].astype(u.dtype) * up[:, tap:tap + S]
    return y


def _fwd_setup_inputs(seed: int = 0) -> dict:
    key = jax.random.key(seed)
    ks = jax.random.split(key, 20)
    L, D = DEPTH, D_MODEL
    f32 = jnp.float32

    def nrm(k, shape, scale):
        return jax.random.normal(k, shape, f32) * scale

    def gain(k, shape):
        return 1.0 + 0.05 * jax.random.normal(k, shape, f32)

    return {
        "x": jax.random.normal(ks[0], (BATCH, SEQ, D), f32),
        "attn_pre_norm": gain(ks[1], (L, D)),
        "w_in": nrm(ks[2], (L, D, IN_COLS), D ** -0.5),
        "forget_bias": 2.0 + 0.5 * jax.random.normal(ks[3], (L, FOX_HEADS), f32),
        "swa_sinks": nrm(ks[4], (L, SWA_Q_HEADS), 0.5),
        "rel_bias": nrm(ks[5], (REL_BUCKETS, SWA_Q_HEADS), 0.5),
        "q_latent_norm": gain(ks[6], (L, MLA_Q_RANK)),
        "w_uq": nrm(ks[7], (L, MLA_Q_RANK, MLA_HEADS * MLA_QK_DIM), MLA_Q_RANK ** -0.5),
        "kv_latent_norm": gain(ks[8], (L, MLA_KV_RANK)),
        "w_ukv": nrm(ks[9], (L, MLA_KV_RANK, MLA_HEADS * (MLA_NOPE_DIM + MLA_V_DIM)), MLA_KV_RANK ** -0.5),
        "group_norm": gain(ks[10], (L, MIX_WIDTH)),
        "w_out": nrm(ks[11], (L, MIX_WIDTH, D), MIX_WIDTH ** -0.5),
        "attn_post_norm": gain(ks[12], (L, D)),
        "ffn_pre_norm": gain(ks[13], (L, D)),
        "w_up": nrm(ks[14], (L, D, 2 * D_FF), D ** -0.5),
        "conv_w": nrm(ks[15], (L, CONV_WIDTH, 2 * D_FF), CONV_WIDTH ** -0.5),
        "conv_b": nrm(ks[16], (L, 2 * D_FF), 0.02),
        "w_down": nrm(ks[17], (L, D_FF, D), D_FF ** -0.5),
        "ffn_post_norm": gain(ks[18], (L, D)),
    }


def _fwd_reference(x, attn_pre_norm, w_in, forget_bias, swa_sinks, rel_bias, q_latent_norm, w_uq,
              kv_latent_norm, w_ukv, group_norm, w_out, attn_post_norm, ffn_pre_norm, w_up,
              conv_w, conv_b, w_down, ffn_post_norm):
    B, S, _ = x.shape
    pos = jnp.arange(S, dtype=jnp.float32)
    inv_freq = ROPE_THETA ** (-(jnp.arange(MLA_ROPE_DIM // 2, dtype=jnp.float32) * 2.0 / MLA_ROPE_DIM))
    ang = pos[:, None] * inv_freq[None, :]
    cos = jnp.cos(ang)[:, None, :].astype(x.dtype)
    sin = jnp.sin(ang)[:, None, :].astype(x.dtype)

    for l in range(DEPTH):
        h = rmsnorm(x, attn_pre_norm[l])
        proj = h @ w_in[l]
        a_cols, f_cols, m_cols = jnp.split(proj, [SWA_COLS, SWA_COLS + FOX_COLS], axis=-1)

        qa, ka, va = jnp.split(a_cols, [SWA_WIDTH, SWA_WIDTH + SWA_KV_HEADS * HEAD_DIM], axis=-1)
        out_a = swa_sink_attention(qa.reshape(B, S, SWA_Q_HEADS, HEAD_DIM),
                                   ka.reshape(B, S, SWA_KV_HEADS, HEAD_DIM),
                                   va.reshape(B, S, SWA_KV_HEADS, HEAD_DIM),
                                   swa_sinks[l], rel_bias)

        qf, kf, vf, f_logit = jnp.split(f_cols, [FOX_WIDTH, 2 * FOX_WIDTH, 3 * FOX_WIDTH], axis=-1)
        log_f = jax.nn.log_sigmoid(f_logit.astype(jnp.float32) + forget_bias[l].astype(jnp.float32))
        F = jnp.cumsum(log_f, axis=1)
        out_b = blocked_causal_attention(qf.reshape(B, S, FOX_HEADS, HEAD_DIM),
                                         kf.reshape(B, S, FOX_HEADS, HEAD_DIM),
                                         vf.reshape(B, S, FOX_HEADS, HEAD_DIM),
                                         HEAD_DIM ** -0.5, log_forget_cum=F)

        c_q, c_kv, k_rope = jnp.split(m_cols, [MLA_Q_RANK, MLA_Q_RANK + MLA_KV_RANK], axis=-1)
        qm = (rmsnorm(c_q, q_latent_norm[l]) @ w_uq[l]).reshape(B, S, MLA_HEADS, MLA_QK_DIM)
        q_nope, q_rope = jnp.split(qm, [MLA_NOPE_DIM], axis=-1)
        kv = (rmsnorm(c_kv, kv_latent_norm[l]) @ w_ukv[l]).reshape(B, S, MLA_HEADS, MLA_NOPE_DIM + MLA_V_DIM)
        k_nope, vm = jnp.split(kv, [MLA_NOPE_DIM], axis=-1)
        q_rope = apply_rope(q_rope, cos, sin)
        k_rope = jnp.broadcast_to(apply_rope(k_rope[:, :, None, :], cos, sin), (B, S, MLA_HEADS, MLA_ROPE_DIM))
        out_c = blocked_causal_attention(jnp.concatenate([q_nope, q_rope], axis=-1),
                                         jnp.concatenate([k_nope, k_rope], axis=-1),
                                         vm, MLA_QK_DIM ** -0.5)

        g_a, g_b, g_c = jnp.split(group_norm[l], [SWA_WIDTH, SWA_WIDTH + FOX_WIDTH])
        mixed = jnp.concatenate([rmsnorm(out_a, g_a), rmsnorm(out_b, g_b), rmsnorm(out_c, g_c)], axis=-1)
        x = x + rmsnorm(mixed @ w_out[l], attn_post_norm[l])

        h = rmsnorm(x, ffn_pre_norm[l])
        u = causal_depthwise_conv(h @ w_up[l], conv_w[l], conv_b[l])
        gate, up = jnp.split(u, 2, axis=-1)
        y = (jax.nn.gelu(gate, approximate=True) * up) @ w_down[l]
        x = x + rmsnorm(y, ffn_post_norm[l])
    return x


import jax as _jax
import jax.numpy as _jnp

TWIN_FORMAT = 'train_step'
FWD_PARAMS = ['x', 'attn_pre_norm', 'w_in', 'forget_bias', 'swa_sinks', 'rel_bias', 'q_latent_norm', 'w_uq', 'kv_latent_norm', 'w_ukv', 'group_norm', 'w_out', 'attn_post_norm', 'ffn_pre_norm', 'w_up', 'conv_w', 'conv_b', 'w_down', 'ffn_post_norm']
TWIN_WEIGHTS = ['attn_pre_norm', 'w_in', 'forget_bias', 'swa_sinks', 'rel_bias', 'q_latent_norm', 'w_uq', 'kv_latent_norm', 'w_ukv', 'group_norm', 'w_out', 'attn_post_norm', 'ffn_pre_norm', 'w_up', 'conv_w', 'conv_b', 'w_down', 'ffn_post_norm']
TWIN_DIFF_INPUT = 'x'
TWIN_INPUTS = ['x', 'attn_pre_norm', 'w_in', 'forget_bias', 'swa_sinks', 'rel_bias', 'q_latent_norm', 'w_uq', 'kv_latent_norm', 'w_ukv', 'group_norm', 'w_out', 'attn_post_norm', 'ffn_pre_norm', 'w_up', 'conv_w', 'conv_b', 'w_down', 'ffn_post_norm', 'loss_target', 'm_attn_pre_norm', 'm_w_in', 'm_forget_bias', 'm_swa_sinks', 'm_rel_bias', 'm_q_latent_norm', 'm_w_uq', 'm_kv_latent_norm', 'm_w_ukv', 'm_group_norm', 'm_w_out', 'm_attn_post_norm', 'm_ffn_pre_norm', 'm_w_up', 'm_conv_w', 'm_conv_b', 'm_w_down', 'm_ffn_post_norm', 'v_attn_pre_norm', 'v_w_in', 'v_forget_bias', 'v_swa_sinks', 'v_rel_bias', 'v_q_latent_norm', 'v_w_uq', 'v_kv_latent_norm', 'v_w_ukv', 'v_group_norm', 'v_w_out', 'v_attn_post_norm', 'v_ffn_pre_norm', 'v_w_up', 'v_conv_w', 'v_conv_b', 'v_w_down', 'v_ffn_post_norm']
TWIN_OUTPUTS = ['loss', 'grad_x', 'grad_attn_pre_norm', 'grad_w_in', 'grad_forget_bias', 'grad_swa_sinks', 'grad_rel_bias', 'grad_q_latent_norm', 'grad_w_uq', 'grad_kv_latent_norm', 'grad_w_ukv', 'grad_group_norm', 'grad_w_out', 'grad_attn_post_norm', 'grad_ffn_pre_norm', 'grad_w_up', 'grad_conv_w', 'grad_conv_b', 'grad_w_down', 'grad_ffn_post_norm', 'delta_attn_pre_norm', 'delta_w_in', 'delta_forget_bias', 'delta_swa_sinks', 'delta_rel_bias', 'delta_q_latent_norm', 'delta_w_uq', 'delta_kv_latent_norm', 'delta_w_ukv', 'delta_group_norm', 'delta_w_out', 'delta_attn_post_norm', 'delta_ffn_pre_norm', 'delta_w_up', 'delta_conv_w', 'delta_conv_b', 'delta_w_down', 'delta_ffn_post_norm', 'new_m_attn_pre_norm', 'new_m_w_in', 'new_m_forget_bias', 'new_m_swa_sinks', 'new_m_rel_bias', 'new_m_q_latent_norm', 'new_m_w_uq', 'new_m_kv_latent_norm', 'new_m_w_ukv', 'new_m_group_norm', 'new_m_w_out', 'new_m_attn_post_norm', 'new_m_ffn_pre_norm', 'new_m_w_up', 'new_m_conv_w', 'new_m_conv_b', 'new_m_w_down', 'new_m_ffn_post_norm', 'new_v_attn_pre_norm', 'new_v_w_in', 'new_v_forget_bias', 'new_v_swa_sinks', 'new_v_rel_bias', 'new_v_q_latent_norm', 'new_v_w_uq', 'new_v_kv_latent_norm', 'new_v_w_ukv', 'new_v_group_norm', 'new_v_w_out', 'new_v_attn_post_norm', 'new_v_ffn_pre_norm', 'new_v_w_up', 'new_v_conv_w', 'new_v_conv_b', 'new_v_w_down', 'new_v_ffn_post_norm']
TWIN_LEAF_KINDS = {'loss': 'loss', 'grad_x': 'grad_x', 'grad_attn_pre_norm': 'grad_w', 'grad_w_in': 'grad_w', 'grad_forget_bias': 'grad_w', 'grad_swa_sinks': 'grad_w', 'grad_rel_bias': 'grad_w', 'grad_q_latent_norm': 'grad_w', 'grad_w_uq': 'grad_w', 'grad_kv_latent_norm': 'grad_w', 'grad_w_ukv': 'grad_w', 'grad_group_norm': 'grad_w', 'grad_w_out': 'grad_w', 'grad_attn_post_norm': 'grad_w', 'grad_ffn_pre_norm': 'grad_w', 'grad_w_up': 'grad_w', 'grad_conv_w': 'grad_w', 'grad_conv_b': 'grad_w', 'grad_w_down': 'grad_w', 'grad_ffn_post_norm': 'grad_w', 'delta_attn_pre_norm': 'delta_w', 'delta_w_in': 'delta_w', 'delta_forget_bias': 'delta_w', 'delta_swa_sinks': 'delta_w', 'delta_rel_bias': 'delta_w', 'delta_q_latent_norm': 'delta_w', 'delta_w_uq': 'delta_w', 'delta_kv_latent_norm': 'delta_w', 'delta_w_ukv': 'delta_w', 'delta_group_norm': 'delta_w', 'delta_w_out': 'delta_w', 'delta_attn_post_norm': 'delta_w', 'delta_ffn_pre_norm': 'delta_w', 'delta_w_up': 'delta_w', 'delta_conv_w': 'delta_w', 'delta_conv_b': 'delta_w', 'delta_w_down': 'delta_w', 'delta_ffn_post_norm': 'delta_w', 'new_m_attn_pre_norm': 'new_m', 'new_m_w_in': 'new_m', 'new_m_forget_bias': 'new_m', 'new_m_swa_sinks': 'new_m', 'new_m_rel_bias': 'new_m', 'new_m_q_latent_norm': 'new_m', 'new_m_w_uq': 'new_m', 'new_m_kv_latent_norm': 'new_m', 'new_m_w_ukv': 'new_m', 'new_m_group_norm': 'new_m', 'new_m_w_out': 'new_m', 'new_m_attn_post_norm': 'new_m', 'new_m_ffn_pre_norm': 'new_m', 'new_m_w_up': 'new_m', 'new_m_conv_w': 'new_m', 'new_m_conv_b': 'new_m', 'new_m_w_down': 'new_m', 'new_m_ffn_post_norm': 'new_m', 'new_v_attn_pre_norm': 'new_v', 'new_v_w_in': 'new_v', 'new_v_forget_bias': 'new_v', 'new_v_swa_sinks': 'new_v', 'new_v_rel_bias': 'new_v', 'new_v_q_latent_norm': 'new_v', 'new_v_w_uq': 'new_v', 'new_v_kv_latent_norm': 'new_v', 'new_v_w_ukv': 'new_v', 'new_v_group_norm': 'new_v', 'new_v_w_out': 'new_v', 'new_v_attn_post_norm': 'new_v', 'new_v_ffn_pre_norm': 'new_v', 'new_v_w_up': 'new_v', 'new_v_conv_w': 'new_v', 'new_v_conv_b': 'new_v', 'new_v_w_down': 'new_v', 'new_v_ffn_post_norm': 'new_v'}


def _forward(args):
    return _fwd_reference(*[args[k] for k in FWD_PARAMS])


def _output_shape():
    out = _jax.eval_shape(lambda: _forward(_fwd_setup_inputs(0)))
    return out.shape, out.dtype

N_MICROBATCH = 1
ADAM_LR = 0.001
ADAM_B1 = 0.9
ADAM_B2 = 0.999
ADAM_EPS = 1e-08
ADAM_WD = 0.01
ADAM_STEP = 10
PER_EXAMPLE_BATCH_AXIS = {'x': 0, 'loss_target': 0}
SHARED_INPUTS = []
_WEIGHT_DTYPES = {'attn_pre_norm': _jnp.float32, 'w_in': _jnp.float32, 'forget_bias': _jnp.float32, 'swa_sinks': _jnp.float32, 'rel_bias': _jnp.float32, 'q_latent_norm': _jnp.float32, 'w_uq': _jnp.float32, 'kv_latent_norm': _jnp.float32, 'w_ukv': _jnp.float32, 'group_norm': _jnp.float32, 'w_out': _jnp.float32, 'attn_post_norm': _jnp.float32, 'ffn_pre_norm': _jnp.float32, 'w_up': _jnp.float32, 'conv_w': _jnp.float32, 'conv_b': _jnp.float32, 'w_down': _jnp.float32, 'ffn_post_norm': _jnp.float32}
MOMENT_SCALE = {'attn_pre_norm': 1.095540e+01, 'w_in': 7.556884e+00, 'forget_bias': 6.177516e+00, 'swa_sinks': 1.407223e+00, 'rel_bias': 3.092591e+00, 'q_latent_norm': 1.553896e+00, 'w_uq': 1.452902e+00, 'kv_latent_norm': 2.225485e+01, 'w_ukv': 1.137336e+01, 'group_norm': 1.103480e+01, 'w_out': 1.065297e+01, 'attn_post_norm': 3.076471e+01, 'ffn_pre_norm': 4.032317e+00, 'w_up': 1.653127e+00, 'conv_w': 1.966603e+00, 'conv_b': 8.482903e+00, 'w_down': 3.394440e+00, 'ffn_post_norm': 3.179719e+01}


def _to_microbatches(a, axis):
    t = _jnp.moveaxis(a, axis, 0)
    t = t.reshape((N_MICROBATCH, t.shape[0] // N_MICROBATCH) + t.shape[1:])
    return _jnp.moveaxis(t, 1, axis + 1)


def setup_inputs(seed: int = 0) -> dict:
    inp = _fwd_setup_inputs(seed)
    key = _jax.random.fold_in(_jax.random.key(seed), 7919)
    shape, _ = _output_shape()
    out = dict(inp)
    out["loss_target"] = _jax.random.normal(_jax.random.fold_in(key, 0), shape, _jnp.float32)
    for i, name in enumerate(TWIN_WEIGHTS):
        w = inp[name].astype(_jnp.float32)
        if MOMENT_SCALE is None:
            s = _jnp.sqrt(_jnp.mean(_jnp.square(w)) + 1e-30)
        else:
            s = MOMENT_SCALE[name]
        km, kv = _jax.random.split(_jax.random.fold_in(key, i + 1))
        out[name] = w
        out["m_" + name] = s * _jax.random.normal(km, w.shape, _jnp.float32)
        out["v_" + name] = (s * s) * _jax.random.uniform(kv, w.shape, _jnp.float32, 0.5, 1.5)
    if N_MICROBATCH > 1:
        for name, axis in PER_EXAMPLE_BATCH_AXIS.items():
            out[name] = _to_microbatches(out[name], axis)
    return {'x': out['x'], 'attn_pre_norm': out['attn_pre_norm'], 'w_in': out['w_in'], 'forget_bias': out['forget_bias'], 'swa_sinks': out['swa_sinks'], 'rel_bias': out['rel_bias'], 'q_latent_norm': out['q_latent_norm'], 'w_uq': out['w_uq'], 'kv_latent_norm': out['kv_latent_norm'], 'w_ukv': out['w_ukv'], 'group_norm': out['group_norm'], 'w_out': out['w_out'], 'attn_post_norm': out['attn_post_norm'], 'ffn_pre_norm': out['ffn_pre_norm'], 'w_up': out['w_up'], 'conv_w': out['conv_w'], 'conv_b': out['conv_b'], 'w_down': out['w_down'], 'ffn_post_norm': out['ffn_post_norm'], 'loss_target': out['loss_target'], 'm_attn_pre_norm': out['m_attn_pre_norm'], 'm_w_in': out['m_w_in'], 'm_forget_bias': out['m_forget_bias'], 'm_swa_sinks': out['m_swa_sinks'], 'm_rel_bias': out['m_rel_bias'], 'm_q_latent_norm': out['m_q_latent_norm'], 'm_w_uq': out['m_w_uq'], 'm_kv_latent_norm': out['m_kv_latent_norm'], 'm_w_ukv': out['m_w_ukv'], 'm_group_norm': out['m_group_norm'], 'm_w_out': out['m_w_out'], 'm_attn_post_norm': out['m_attn_post_norm'], 'm_ffn_pre_norm': out['m_ffn_pre_norm'], 'm_w_up': out['m_w_up'], 'm_conv_w': out['m_conv_w'], 'm_conv_b': out['m_conv_b'], 'm_w_down': out['m_w_down'], 'm_ffn_post_norm': out['m_ffn_post_norm'], 'v_attn_pre_norm': out['v_attn_pre_norm'], 'v_w_in': out['v_w_in'], 'v_forget_bias': out['v_forget_bias'], 'v_swa_sinks': out['v_swa_sinks'], 'v_rel_bias': out['v_rel_bias'], 'v_q_latent_norm': out['v_q_latent_norm'], 'v_w_uq': out['v_w_uq'], 'v_kv_latent_norm': out['v_kv_latent_norm'], 'v_w_ukv': out['v_w_ukv'], 'v_group_norm': out['v_group_norm'], 'v_w_out': out['v_w_out'], 'v_attn_post_norm': out['v_attn_post_norm'], 'v_ffn_pre_norm': out['v_ffn_pre_norm'], 'v_w_up': out['v_w_up'], 'v_conv_w': out['v_conv_w'], 'v_conv_b': out['v_conv_b'], 'v_w_down': out['v_w_down'], 'v_ffn_post_norm': out['v_ffn_post_norm']}


def _loss(weights, diff, rest, loss_target):
    with _jax.named_scope("forward"):
        args = {**rest, TWIN_DIFF_INPUT: diff, **{k: w.astype(_WEIGHT_DTYPES[k]) for k, w in weights.items()}}
        y = _forward(args)
    with _jax.named_scope("loss_head"):
        err = _jnp.square(y.astype(_jnp.float32) - loss_target)
        return 0.5 * _jnp.sum(_jnp.mean(err, axis=-1)) if err.ndim else 0.5 * err


def _adamw(w, g, m, v):
    m = ADAM_B1 * m + (1.0 - ADAM_B1) * g
    v = ADAM_B2 * v + (1.0 - ADAM_B2) * _jnp.square(g)
    m_hat = m / (1.0 - ADAM_B1 ** ADAM_STEP)
    v_hat = v / (1.0 - ADAM_B2 ** ADAM_STEP)
    delta = -ADAM_LR * (m_hat / (_jnp.sqrt(v_hat) + ADAM_EPS) + ADAM_WD * w)
    return delta, m, v


def reference(x, attn_pre_norm, w_in, forget_bias, swa_sinks, rel_bias, q_latent_norm, w_uq, kv_latent_norm, w_ukv, group_norm, w_out, attn_post_norm, ffn_pre_norm, w_up, conv_w, conv_b, w_down, ffn_post_norm, loss_target, m_attn_pre_norm, m_w_in, m_forget_bias, m_swa_sinks, m_rel_bias, m_q_latent_norm, m_w_uq, m_kv_latent_norm, m_w_ukv, m_group_norm, m_w_out, m_attn_post_norm, m_ffn_pre_norm, m_w_up, m_conv_w, m_conv_b, m_w_down, m_ffn_post_norm, v_attn_pre_norm, v_w_in, v_forget_bias, v_swa_sinks, v_rel_bias, v_q_latent_norm, v_w_uq, v_kv_latent_norm, v_w_ukv, v_group_norm, v_w_out, v_attn_post_norm, v_ffn_pre_norm, v_w_up, v_conv_w, v_conv_b, v_w_down, v_ffn_post_norm):
    given = dict(x=x, attn_pre_norm=attn_pre_norm, w_in=w_in, forget_bias=forget_bias, swa_sinks=swa_sinks, rel_bias=rel_bias, q_latent_norm=q_latent_norm, w_uq=w_uq, kv_latent_norm=kv_latent_norm, w_ukv=w_ukv, group_norm=group_norm, w_out=w_out, attn_post_norm=attn_post_norm, ffn_pre_norm=ffn_pre_norm, w_up=w_up, conv_w=conv_w, conv_b=conv_b, w_down=w_down, ffn_post_norm=ffn_post_norm, loss_target=loss_target, m_attn_pre_norm=m_attn_pre_norm, m_w_in=m_w_in, m_forget_bias=m_forget_bias, m_swa_sinks=m_swa_sinks, m_rel_bias=m_rel_bias, m_q_latent_norm=m_q_latent_norm, m_w_uq=m_w_uq, m_kv_latent_norm=m_kv_latent_norm, m_w_ukv=m_w_ukv, m_group_norm=m_group_norm, m_w_out=m_w_out, m_attn_post_norm=m_attn_post_norm, m_ffn_pre_norm=m_ffn_pre_norm, m_w_up=m_w_up, m_conv_w=m_conv_w, m_conv_b=m_conv_b, m_w_down=m_w_down, m_ffn_post_norm=m_ffn_post_norm, v_attn_pre_norm=v_attn_pre_norm, v_w_in=v_w_in, v_forget_bias=v_forget_bias, v_swa_sinks=v_swa_sinks, v_rel_bias=v_rel_bias, v_q_latent_norm=v_q_latent_norm, v_w_uq=v_w_uq, v_kv_latent_norm=v_kv_latent_norm, v_w_ukv=v_w_ukv, v_group_norm=v_group_norm, v_w_out=v_w_out, v_attn_post_norm=v_attn_post_norm, v_ffn_pre_norm=v_ffn_pre_norm, v_w_up=v_w_up, v_conv_w=v_conv_w, v_conv_b=v_conv_b, v_w_down=v_w_down, v_ffn_post_norm=v_ffn_post_norm)
    weights = {n: given[n] for n in TWIN_WEIGHTS}
    shared = {n: given[n] for n in SHARED_INPUTS}
    per_example = {n: given[n] for n in ['x']}
    grad_fn = _jax.value_and_grad(_loss, argnums=(0, 1))

    def one_microbatch(ex, loss_target):
        ex = dict(ex)
        diff = ex.pop(TWIN_DIFF_INPUT)
        return grad_fn(weights, diff, {**shared, **ex}, loss_target)

    if N_MICROBATCH == 1:
        loss, (grad_w, grad_x) = one_microbatch(per_example, given["loss_target"])
    else:
        def body(carry, xs):
            loss_sum, grad_sum = carry
            l_k, (gw_k, gx_k) = one_microbatch(xs[0], xs[1])
            with _jax.named_scope("update"):
                return (loss_sum + l_k, _jax.tree.map(_jnp.add, grad_sum, gw_k)), gx_k

        init = (_jnp.zeros((), _jnp.float32), _jax.tree.map(_jnp.zeros_like, weights))
        (loss, grad_w), grad_x = _jax.lax.scan(body, init, (per_example, given["loss_target"]))
    with _jax.named_scope("update"):
        delta_w, new_m, new_v = {}, {}, {}
        for n in TWIN_WEIGHTS:
            delta_w[n], new_m[n], new_v[n] = _adamw(weights[n], grad_w[n], given["m_" + n], given["v_" + n])
    return (loss, grad_x, *[grad_w[n] for n in TWIN_WEIGHTS], *[delta_w[n] for n in TWIN_WEIGHTS],
            *[new_m[n] for n in TWIN_WEIGHTS], *[new_v[n] for n in TWIN_WEIGHTS])
```

```python
import math

import numpy as np
import jax
import jax.numpy as jnp
from jax import lax
from jax.experimental import pallas as pl
from jax.experimental.pallas import tpu as pltpu

F32 = jnp.float32
MXU = jnp.bfloat16

N_DEV = 8
DEPTH = 4
D_MODEL = 1024
HEAD_DIM = 64
WINDOW = 128
SWA_Q_HEADS = 8
REL_BUCKETS = 32
REL_MAX_DIST = 128
MLA_QK_DIM = 96
ROPE_DIM = 32
ROPE_THETA = 10000.0
D_FF = 2816
EPS = 1e-6
NEG = -1e30
IN_COLS = 1956
IN_COLS_P = 2048
C_QA, C_KA, C_VA = 0, 512, 640
C_QF, C_KF, C_VF = 768, 1024, 1280
C_CQ, C_CKV, C_MISC = 1536, 1792, 1920
ROPE_LANE0 = 64
ADAM_LR, ADAM_B1, ADAM_B2, ADAM_EPS, ADAM_WD, ADAM_STEP = 0.001, 0.9, 0.999, 1e-08, 0.01, 10

VMEM_LIMIT = 56 * 1024 * 1024
PACK_COLS = 1024
PACK_ROW_ALIGN = 16


def _cparams(sem=None):
    return pltpu.CompilerParams(dimension_semantics=sem, vmem_limit_bytes=VMEM_LIMIT)


def _tile(n, pref):
    if n <= pref:
        return n
    t = pref - pref % 128
    while t >= 128:
        if n % t == 0:
            return t
        t -= 128
    return n


def _dot(a, b):
    return jnp.dot(a.astype(MXU), b.astype(MXU), preferred_element_type=F32)


def _dot_nt(a, b):
    return lax.dot_general(a.astype(MXU), b.astype(MXU), (((1,), (1,)), ((), ())),
                           preferred_element_type=F32)


def _rms_fwd(x, g):
    return x * lax.rsqrt(jnp.mean(x * x, axis=-1, keepdims=True) + EPS) * g


def _rms_bwd(dy, x, g, n=None):
    r = lax.rsqrt(jnp.mean(x * x, axis=-1, keepdims=True) + EPS)
    xh = x * r
    dg = jnp.sum(dy * xh, axis=0, keepdims=True)
    dxh = dy * g
    dx = r * (dxh - xh * jnp.mean(dxh * xh, axis=-1, keepdims=True))
    return dx, dg


def _acc_out(ref, val, first):
    @pl.when(first)
    def _():
        ref[...] = val

    @pl.when(jnp.logical_not(first))
    def _():
        ref[...] += val


def norm_matmul(x, g, w, name):
    T, K = x.shape
    N = w.shape[1]
    tm, tn = _tile(T, 512), _tile(N, 512)

    def body(x_ref, g_ref, w_ref, o_ref, hT_ref, h_sc):
        @pl.when(pl.program_id(1) == 0)
        def _():
            h = _rms_fwd(x_ref[...], g_ref[...])
            h_sc[...] = h.astype(MXU)
            hT_ref[...] = h.T.astype(MXU)

        o_ref[...] = jnp.dot(h_sc[...], w_ref[...], preferred_element_type=F32)

    return pl.pallas_call(
        body, name=name, grid=(T // tm, N // tn),
        in_specs=[pl.BlockSpec((tm, K), lambda i, j: (i, 0)),
                  pl.BlockSpec((1, K), lambda i, j: (0, 0)),
                  pl.BlockSpec((K, tn), lambda i, j: (0, j))],
        out_specs=[pl.BlockSpec((tm, tn), lambda i, j: (i, j)),
                   pl.BlockSpec((K, tm), lambda i, j: (0, i))],
        out_shape=[jax.ShapeDtypeStruct((T, N), F32), jax.ShapeDtypeStruct((K, T), MXU)],
        scratch_shapes=[pltpu.VMEM((tm, K), MXU)],
        compiler_params=_cparams(("parallel", "arbitrary")),
    )(x, g, w)


def matmul_nn(a, b, name, out_dtype=F32):
    M, K = a.shape
    N = b.shape[1]
    tm, tn, tk = _tile(M, 512), _tile(N, 512), _tile(K, 1024)
    nk = K // tk

    def body(a_ref, b_ref, o_ref, acc):
        k = pl.program_id(2)
        part = _dot(a_ref[...], b_ref[...])
        _acc_out(acc, part, k == 0)

        @pl.when(k == nk - 1)
        def _():
            o_ref[...] = acc[...].astype(out_dtype)

    return pl.pallas_call(
        body, name=name, grid=(M // tm, N // tn, nk),
        in_specs=[pl.BlockSpec((tm, tk), lambda i, j, k: (i, k)),
                  pl.BlockSpec((tk, tn), lambda i, j, k: (k, j))],
        out_specs=pl.BlockSpec((tm, tn), lambda i, j, k: (i, j)),
        out_shape=jax.ShapeDtypeStruct((M, N), out_dtype),
        scratch_shapes=[pltpu.VMEM((tm, tn), F32)],
        compiler_params=_cparams(("parallel", "parallel", "arbitrary")),
    )(a, b)


def matmul_nt_normbwd(dy, w, x, g, dres, name):
    T, N = dy.shape
    K = w.shape[0]
    tm, tn = _tile(T, 512), _tile(N, 512)
    nj = N // tn

    def body(dy_ref, w_ref, x_ref, g_ref, dres_ref, dx_ref, dg_ref, acc):
        i, j = pl.program_id(0), pl.program_id(1)
        _acc_out(acc, _dot_nt(dy_ref[...], w_ref[...]), j == 0)

        @pl.when(j == nj - 1)
        def _():
            dx, dg = _rms_bwd(acc[...], x_ref[...], g_ref[...])
            dx_ref[...] = dres_ref[...] + dx
            _acc_out(dg_ref, dg, i == 0)

    return pl.pallas_call(
        body, name=name, grid=(T // tm, nj),
        in_specs=[pl.BlockSpec((tm, tn), lambda i, j: (i, j)),
                  pl.BlockSpec((K, tn), lambda i, j: (0, j)),
                  pl.BlockSpec((tm, K), lambda i, j: (i, 0)),
                  pl.BlockSpec((1, K), lambda i, j: (0, 0)),
                  pl.BlockSpec((tm, K), lambda i, j: (i, 0))],
        out_specs=[pl.BlockSpec((tm, K), lambda i, j: (i, 0)),
                   pl.BlockSpec((1, K), lambda i, j: (0, 0))],
        out_shape=[jax.ShapeDtypeStruct((T, K), F32), jax.ShapeDtypeStruct((1, K), F32)],
        scratch_shapes=[pltpu.VMEM((tm, K), F32)],
        compiler_params=_cparams(("arbitrary", "arbitrary")),
    )(dy, w, x, g, dres)


def loss_kernel(y, tgt):
    T, D = y.shape
    tm = _tile(T, 512)

    def body(y_ref, t_ref, dy_ref, acc_ref):
        e = y_ref[...] - t_ref[...]
        dy_ref[...] = e * (1.0 / D)
        _acc_out(acc_ref, jnp.sum(e * e, axis=0, keepdims=True), pl.program_id(0) == 0)

    return pl.pallas_call(
        body, name="loss", grid=(T // tm,),
        in_specs=[pl.BlockSpec((tm, D), lambda i: (i, 0)), pl.BlockSpec((tm, D), lambda i: (i, 0))],
        out_specs=[pl.BlockSpec((tm, D), lambda i: (i, 0)), pl.BlockSpec((1, D), lambda i: (0, 0))],
        out_shape=[jax.ShapeDtypeStruct((T, D), F32), jax.ShapeDtypeStruct((1, D), F32)],
        compiler_params=_cparams(("arbitrary",)),
    )(y, tgt)


def _rope_partner(x):
    lane = lax.broadcasted_iota(jnp.int32, (1, 128), 1)
    return jnp.where(lane < ROPE_LANE0 + ROPE_DIM // 2, pltpu.roll(x, 128 - ROPE_DIM // 2, 1),
                     pltpu.roll(x, ROPE_DIM // 2, 1))


def _rope_apply(x, cos, sin_signed):
    return x * cos + _rope_partner(x) * sin_signed


def _rope_apply_bwd(dy, cos, sin_signed):
    lane = lax.broadcasted_iota(jnp.int32, (1, 128), 1)
    rotary = (lane >= ROPE_LANE0) & (lane < ROPE_LANE0 + ROPE_DIM)
    return dy * cos + jnp.where(rotary, _rope_partner(dy * sin_signed), 0.0)


def rope_tables(T):
    pos = jnp.arange(T, dtype=F32)
    inv_freq = ROPE_THETA ** (-(jnp.arange(ROPE_DIM // 2, dtype=F32) * 2.0 / ROPE_DIM))
    ang = pos[:, None] * inv_freq[None, :]
    cos, sin = jnp.cos(ang), jnp.sin(ang)
    z = jnp.zeros((T, ROPE_LANE0), F32)
    z2 = jnp.zeros((T, 128 - ROPE_LANE0 - ROPE_DIM), F32)
    cosr = jnp.concatenate([z, cos, cos, z2], axis=1)
    sinr = jnp.concatenate([z, -sin, sin, z2], axis=1)
    return cosr, sinr


def mla_prep(proj, gq, gkv, wuq, wukv, cosr, sinr):
    T = proj.shape[0]
    tm = _tile(T, 512)

    def body(cq_ref, ckv_ref, misc_ref, gq_ref, gkv_ref, wuq_ref, wukv_ref, cos_ref, sin_ref,
             q_ref, k_ref, v_ref, cqT_ref, ckvT_ref):
        lane = lax.broadcasted_iota(jnp.int32, (1, 128), 1)
        cosr_, sinr_ = cos_ref[...], sin_ref[...]
        cosq = cosr_ + jnp.where(lane < ROPE_LANE0, 1.0, 0.0)
        cqn = _rms_fwd(cq_ref[...], gq_ref[...])
        cqT_ref[...] = cqn.T.astype(MXU)
        qm = _dot(cqn, wuq_ref[...])
        q_ref[...] = jnp.concatenate(
            [_rope_apply(qm[:, 128 * h:128 * (h + 1)], cosq, sinr_) for h in range(4)], axis=1).astype(MXU)
        ckvn = _rms_fwd(ckv_ref[...], gkv_ref[...])
        ckvT_ref[...] = ckvn.T.astype(MXU)
        kv = _dot(ckvn, wukv_ref[...])
        kr = _rope_apply(misc_ref[...], cosr_, sinr_)
        k_ref[...] = jnp.concatenate(
            [kv[:, 128 * h:128 * (h + 1)] + kr for h in range(4)], axis=1).astype(MXU)
        v_ref[...] = kv[:, 512:768].astype(MXU)

    row = lambda i: (i, 0)
    const = lambda i: (0, 0)
    return pl.pallas_call(
        body, name="mla_prep", grid=(T // tm,),
        in_specs=[pl.BlockSpec((tm, 256), lambda i: (i, C_CQ // 256)),
                  pl.BlockSpec((tm, 128), lambda i: (i, C_CKV // 128)),
                  pl.BlockSpec((tm, 128), lambda i: (i, C_MISC // 128)),
                  pl.BlockSpec((1, 256), const), pl.BlockSpec((1, 128), const),
                  pl.BlockSpec((256, 512), const), pl.BlockSpec((128, 768), const),
                  pl.BlockSpec((tm, 128), row), pl.BlockSpec((tm, 128), row)],
        out_specs=[pl.BlockSpec((tm, 512), row), pl.BlockSpec((tm, 512), row), pl.BlockSpec((tm, 256), row),
                   pl.BlockSpec((256, tm), lambda i: (0, i)), pl.BlockSpec((128, tm), lambda i: (0, i))],
        out_shape=[jax.ShapeDtypeStruct((T, 512), MXU), jax.ShapeDtypeStruct((T, 512), MXU),
                   jax.ShapeDtypeStruct((T, 256), MXU),
                   jax.ShapeDtypeStruct((256, T), MXU), jax.ShapeDtypeStruct((128, T), MXU)],
        compiler_params=_cparams(("parallel",)),
    )(proj, proj, proj, gq, gkv, wuq, wukv, cosr, sinr)


def mla_prep_bwd(dq, dk, dv, proj, gq, gkv, wuq, wukv, cosr, sinr):
    T = proj.shape[0]
    tm = _tile(T, 512)

    def body(dq_ref, dk_ref, dv_ref, cq_ref, ckv_ref, gq_ref, gkv_ref, wuq_ref, wukv_ref, cos_ref, sin_ref,
             dqm_ref, dkv_ref, dcq_ref, dckv_ref, dmisc_ref, dgq_ref, dgkv_ref):
        first = pl.program_id(0) == 0
        lane = lax.broadcasted_iota(jnp.int32, (1, 128), 1)
        cosr_, sinr_ = cos_ref[...], sin_ref[...]
        cosq = cosr_ + jnp.where(lane < ROPE_LANE0, 1.0, 0.0)
        dqv = dq_ref[...]
        dqm = jnp.concatenate(
            [_rope_apply_bwd(dqv[:, 128 * h:128 * (h + 1)], cosq, sinr_) for h in range(4)], axis=1)
        dqm_ref[...] = dqm.astype(MXU)
        dcq, dgq = _rms_bwd(_dot_nt(dqm, wuq_ref[...]), cq_ref[...], gq_ref[...])
        dcq_ref[...] = dcq
        _acc_out(dgq_ref, dgq, first)
        dkv_ = dk_ref[...]
        heads = [dkv_[:, 128 * h:128 * (h + 1)] for h in range(4)]
        dkr = heads[0] + heads[1] + heads[2] + heads[3]
        dmisc_ref[...] = _rope_apply_bwd(dkr, cosr_, sinr_)
        dkvm = jnp.concatenate([jnp.where(lane < ROPE_LANE0, hd, 0.0) for hd in heads] + [dv_ref[...]], axis=1)
        dkv_ref[...] = dkvm.astype(MXU)
        dckv, dgkv = _rms_bwd(_dot_nt(dkvm, wukv_ref[...]), ckv_ref[...], gkv_ref[...])
        dckv_ref[...] = dckv
        _acc_out(dgkv_ref, dgkv, first)

    row = lambda i: (i, 0)
    const = lambda i: (0, 0)
    return pl.pallas_call(
        body, name="mla_prep_bwd", grid=(T // tm,),
        in_specs=[pl.BlockSpec((tm, 512), row), pl.BlockSpec((tm, 512), row), pl.BlockSpec((tm, 256), row),
                  pl.BlockSpec((tm, 256), lambda i: (i, C_CQ // 256)),
                  pl.BlockSpec((tm, 128), lambda i: (i, C_CKV // 128)),
                  pl.BlockSpec((1, 256), const), pl.BlockSpec((1, 128), const),
                  pl.BlockSpec((256, 512), const), pl.BlockSpec((128, 768), const),
                  pl.BlockSpec((tm, 128), row), pl.BlockSpec((tm, 128), row)],
        out_specs=[pl.BlockSpec((tm, 512), row), pl.BlockSpec((tm, 768), row), pl.BlockSpec((tm, 256), row),
                   pl.BlockSpec((tm, 128), row), pl.BlockSpec((tm, 128), row),
                   pl.BlockSpec((1, 256), const), pl.BlockSpec((1, 128), const)],
        out_shape=[jax.ShapeDtypeStruct((T, 512), MXU), jax.ShapeDtypeStruct((T, 768), MXU),
                   jax.ShapeDtypeStruct((T, 256), F32), jax.ShapeDtypeStruct((T, 128), F32),
                   jax.ShapeDtypeStruct((T, 128), F32),
                   jax.ShapeDtypeStruct((1, 256), F32), jax.ShapeDtypeStruct((1, 128), F32)],
        compiler_params=_cparams(("arbitrary",)),
    )(dq, dk, dv, proj, proj, gq, gkv, wuq, wukv, cosr, sinr)


def _split3(x):
    hi = x.astype(MXU)
    r1 = x - hi.astype(F32)
    mid = r1.astype(MXU)
    lo = (r1 - mid.astype(F32)).astype(MXU)
    return hi, mid, lo


def _tri_matmul(tri, x):
    hi, mid, lo = _split3(x)
    d = lambda p: jnp.dot(tri, p, preferred_element_type=F32)
    return d(hi) + d(mid) + d(lo)


def _log_sigmoid(z):
    return jnp.minimum(z, 0.0) - jnp.log(1.0 + jnp.exp(-jnp.abs(z)))


def fox_gate(proj, fbias):
    T = proj.shape[0]
    tb = _tile(T, 512)

    def body(misc_ref, b_ref, fc_ref, fr_ref, carry):
        @pl.when(pl.program_id(0) == 0)
        def _():
            carry[...] = jnp.zeros_like(carry)

        lane = lax.broadcasted_iota(jnp.int32, (1, 128), 1)
        lf = jnp.where(lane < 4, _log_sigmoid(misc_ref[...] + b_ref[...]), 0.0)
        r = lax.broadcasted_iota(jnp.int32, (tb, tb), 0)
        c = lax.broadcasted_iota(jnp.int32, (tb, tb), 1)
        tri = jnp.where(r >= c, 1.0, 0.0).astype(MXU)
        F = _tri_matmul(tri, lf) + carry[...]
        carry[...] = carry[...] + jnp.sum(lf, axis=0, keepdims=True)
        fc_ref[0] = F
        fc_ref[1] = pltpu.roll(F, 126, 1)
        ft = F.T[0:8, :]
        fr_ref[0] = ft
        fr_ref[1] = pltpu.roll(ft, 6, 0)

    return pl.pallas_call(
        body, name="fox_gate", grid=(T // tb,),
        in_specs=[pl.BlockSpec((tb, 128), lambda i: (i, C_MISC // 128)), pl.BlockSpec((1, 128), lambda i: (0, 0))],
        out_specs=[pl.BlockSpec((2, tb, 128), lambda i: (0, i, 0)), pl.BlockSpec((2, 8, tb), lambda i: (0, 0, i))],
        out_shape=[jax.ShapeDtypeStruct((2, T, 128), F32), jax.ShapeDtypeStruct((2, 8, T), F32)],
        scratch_shapes=[pltpu.VMEM((1, 128), F32)],
        compiler_params=_cparams(("arbitrary",)),
    )(proj, fbias)


def fox_gate_bwd(dFq, dFk, proj, fbias):
    T = proj.shape[0]
    tb = _tile(T, 512)
    nb = T // tb

    def body(dq_ref, dk_ref, misc_ref, b_ref, dm_ref, db_ref, carry):
        first = pl.program_id(0) == 0

        @pl.when(first)
        def _():
            carry[...] = jnp.zeros_like(carry)

        lane = lax.broadcasted_iota(jnp.int32, (1, 128), 1)
        dF = jnp.where(lane < 4, (dq_ref[0] + dk_ref[0]) + pltpu.roll(dq_ref[1] + dk_ref[1], 2, 1), 0.0)
        r = lax.broadcasted_iota(jnp.int32, (tb, tb), 0)
        c = lax.broadcasted_iota(jnp.int32, (tb, tb), 1)
        tri = jnp.where(r <= c, 1.0, 0.0).astype(MXU)
        dlf = _tri_matmul(tri, dF) + carry[...]
        carry[...] = carry[...] + jnp.sum(dF, axis=0, keepdims=True)
        z = misc_ref[...] + b_ref[...]
        dz = jnp.where(lane < 4, dlf * (1.0 / (1.0 + jnp.exp(z))), 0.0)
        dm_ref[...] = dz
        _acc_out(db_ref, jnp.sum(dz, axis=0, keepdims=True), first)

    return pl.pallas_call(
        body, name="fox_gate_bwd", grid=(nb,),
        in_specs=[pl.BlockSpec((2, tb, 128), lambda i: (0, nb - 1 - i, 0)),
                  pl.BlockSpec((2, tb, 128), lambda i: (0, nb - 1 - i, 0)),
                  pl.BlockSpec((tb, 128), lambda i: (nb - 1 - i, C_MISC // 128)),
                  pl.BlockSpec((1, 128), lambda i: (0, 0))],
        out_specs=[pl.BlockSpec((tb, 128), lambda i: (nb - 1 - i, 0)), pl.BlockSpec((1, 128), lambda i: (0, 0))],
        out_shape=[jax.ShapeDtypeStruct((T, 128), F32), jax.ShapeDtypeStruct((1, 128), F32)],
        scratch_shapes=[pltpu.VMEM((1, 128), F32)],
        compiler_params=_cparams(("arbitrary",)),
    )(dFq, dFk, proj, fbias)


def _row_stat_tile(a, b, n):
    at = jnp.broadcast_to(a, (n, 128)).T[0:8, :]
    bt = jnp.broadcast_to(b, (n, 128)).T[0:8, :]
    sub = lax.broadcasted_iota(jnp.int32, (8, 1), 0)
    return jnp.where(sub == 0, at, jnp.where(sub == 1, bt, 0.0))


def _col_stat_tile(a, b):
    lane = lax.broadcasted_iota(jnp.int32, (1, 128), 1)
    return jnp.where(lane == 0, a, jnp.where(lane == 1, b, 0.0))


def _lane_pick(x, h):
    lane = lax.broadcasted_iota(jnp.int32, (1, 128), 1)
    return jnp.sum(jnp.where(lane == h, x, 0.0), axis=1, keepdims=True)


def _half_mask(h):
    lane = lax.broadcasted_iota(jnp.int32, (1, 128), 1)
    return (lane // HEAD_DIM) == h


def flash_fwd(q, k, v, fcol, frow, *, qblk, kblk, vblk, nq, scale, name):
    T = q.shape[0]
    tq = tk = _tile(T, 256)
    wq = 128 * nq
    has_f = fcol is not None

    def body(*refs):
        if has_f:
            q_ref, k_ref, v_ref, fc_ref, fr_ref, o_ref, lc_ref, lr_ref, m_sc, l_sc, acc_sc = refs
        else:
            q_ref, k_ref, v_ref, o_ref, lc_ref, lr_ref, m_sc, l_sc, acc_sc = refs
        i = pl.program_id(1)
        rows = i * tq + lax.broadcasted_iota(jnp.int32, (tq, 1), 0)
        qb = q_ref[...]
        outs, lses = [], []
        for h in range(2):
            if nq == 1:
                qh = jnp.where(_half_mask(h), qb, 0).astype(MXU)
            else:
                qh = qb[:, 128 * h:128 * (h + 1)].astype(MXU)
            if has_f:
                fq = _lane_pick(fc_ref[0], h)
            m_sc[...] = jnp.full((tq, 1), NEG, F32)
            l_sc[...] = jnp.zeros((tq, 1), F32)
            acc_sc[...] = jnp.zeros((tq, 128), F32)

            def step(j, carry):
                off = pl.multiple_of(j * tk, tk)
                ks = k_ref[pl.ds(off, tk), :]
                kh = ks if nq == 1 else ks[:, 128 * h:128 * (h + 1)]
                s = _dot_nt(qh, kh) * scale
                if has_f:
                    s = s + (fq - fr_ref[0, h:h + 1, pl.ds(off, tk)])
                cols = j * tk + lax.broadcasted_iota(jnp.int32, (1, tk), 1)
                s = jnp.where(cols <= rows, s, NEG)
                m_prev = m_sc[...]
                m_new = jnp.maximum(m_prev, jnp.max(s, axis=1, keepdims=True))
                alpha = jnp.exp(m_prev - m_new)
                pm = jnp.exp(s - m_new)
                l_sc[...] = alpha * l_sc[...] + jnp.sum(pm, axis=1, keepdims=True)
                acc_sc[...] = alpha * acc_sc[...] + _dot(pm, v_ref[pl.ds(off, tk), :])
                m_sc[...] = m_new
                return carry

            lax.fori_loop(0, i + 1, step, 0)
            l = l_sc[...]
            outs.append(acc_sc[...] / l)
            lses.append(m_sc[...] + jnp.log(l))
        o_ref[...] = jnp.where(_half_mask(0), outs[0], outs[1])
        lc_ref[0] = _col_stat_tile(lses[0], lses[1])
        lr_ref[0] = _row_stat_tile(lses[0], lses[1], tq)

    in_specs = [pl.BlockSpec((tq, wq), lambda p, i: (i, qblk + p)),
                pl.BlockSpec((T, wq), lambda p, i: (0, kblk + p)),
                pl.BlockSpec((T, 128), lambda p, i: (0, vblk + p))]
    args = [q, k, v]
    if has_f:
        in_specs += [pl.BlockSpec((1, tq, 128), lambda p, i: (p, i, 0)),
                     pl.BlockSpec((1, 8, T), lambda p, i: (p, 0, 0))]
        args += [fcol, frow]
    return pl.pallas_call(
        body, name=name, grid=(2, T // tq), in_specs=in_specs,
        out_specs=[pl.BlockSpec((tq, 128), lambda p, i: (i, p)),
                   pl.BlockSpec((1, tq, 128), lambda p, i: (p, i, 0)),
                   pl.BlockSpec((1, 8, tq), lambda p, i: (p, 0, i))],
        out_shape=[jax.ShapeDtypeStruct((T, 256), F32), jax.ShapeDtypeStruct((2, T, 128), F32),
                   jax.ShapeDtypeStruct((2, 8, T), F32)],
        scratch_shapes=[pltpu.VMEM((tq, 1), F32), pltpu.VMEM((tq, 1), F32), pltpu.VMEM((tq, 128), F32)],
        compiler_params=_cparams(("parallel", "parallel")),
    )(*args)


def flash_dq(q, k, v, do, o, lcol, fcol, frow, *, qblk, kblk, vblk, doblk, nq, scale, name):
    T = q.shape[0]
    tq = tk = _tile(T, 256)
    wq = 128 * nq
    has_f = fcol is not None

    def body(*refs):
        if has_f:
            q_ref, k_ref, v_ref, do_ref, o_ref, lc_ref, fc_ref, fr_ref, dq_ref, dr_ref, df_ref, acc_sc, df_sc = refs
        else:
            q_ref, k_ref, v_ref, do_ref, o_ref, lc_ref, dq_ref, dr_ref, acc_sc = refs
        i = pl.program_id(1)
        rows = i * tq + lax.broadcasted_iota(jnp.int32, (tq, 1), 0)
        qb = q_ref[...]
        dob = do_ref[...]
        prod = dob * o_ref[...]
        dqs, Ds, dfs = [], [], []
        for h in range(2):
            hm = _half_mask(h)
            if nq == 1:
                qh = jnp.where(hm, qb, 0).astype(MXU)
            else:
                qh = qb[:, 128 * h:128 * (h + 1)].astype(MXU)
            doh = jnp.where(hm, dob, 0.0).astype(MXU)
            D = jnp.sum(jnp.where(hm, prod, 0.0), axis=1, keepdims=True)
            lse = _lane_pick(lc_ref[0], h)
            if has_f:
                fq = _lane_pick(fc_ref[0], h)
                df_sc[...] = jnp.zeros((tq, 128), F32)
            acc_sc[...] = jnp.zeros((tq, 128), F32)

            def step(j, carry):
                off = pl.multiple_of(j * tk, tk)
                ks = k_ref[pl.ds(off, tk), :]
                kh = ks if nq == 1 else ks[:, 128 * h:128 * (h + 1)]
                s = _dot_nt(qh, kh) * scale
                if has_f:
                    s = s + (fq - fr_ref[0, h:h + 1, pl.ds(off, tk)])
                cols = j * tk + lax.broadcasted_iota(jnp.int32, (1, tk), 1)
                pmat = jnp.where(cols <= rows, jnp.exp(s - lse), 0.0)
                dp = _dot_nt(doh, v_ref[pl.ds(off, tk), :])
                ds = pmat * (dp - D)
                kk = jnp.where(hm, kh, 0) if nq == 1 else kh
                acc_sc[...] += _dot(ds, kk)
                if has_f:
                    part = ds[:, 0:128]
                    for c in range(1, tk // 128):
                        part = part + ds[:, 128 * c:128 * (c + 1)]
                    df_sc[...] += part
                return carry

            lax.fori_loop(0, i + 1, step, 0)
            dqs.append(acc_sc[...] * scale)
            Ds.append(D)
            if has_f:
                dfs.append(jnp.sum(df_sc[...], axis=1, keepdims=True))
        if nq == 1:
            dq_ref[...] = dqs[0] + dqs[1]
        else:
            dq_ref[...] = jnp.concatenate(dqs, axis=1)
        dr_ref[0] = _row_stat_tile(Ds[0], Ds[1], tq)
        if has_f:
            df_ref[0] = _col_stat_tile(dfs[0], dfs[1])

    in_specs = [pl.BlockSpec((tq, wq), lambda p, i: (i, qblk + p)),
                pl.BlockSpec((T, wq), lambda p, i: (0, kblk + p)),
                pl.BlockSpec((T, 128), lambda p, i: (0, vblk + p)),
                pl.BlockSpec((tq, 128), lambda p, i: (i, doblk + p)),
                pl.BlockSpec((tq, 128), lambda p, i: (i, p)),
                pl.BlockSpec((1, tq, 128), lambda p, i: (p, i, 0))]
    args = [q, k, v, do, o, lcol]
    out_specs = [pl.BlockSpec((tq, wq), lambda p, i: (i, p)), pl.BlockSpec((1, 8, tq), lambda p, i: (p, 0, i))]
    out_shape = [jax.ShapeDtypeStruct((T, 2 * wq), F32), jax.ShapeDtypeStruct((2, 8, T), F32)]
    scratch = [pltpu.VMEM((tq, 128), F32)]
    if has_f:
        in_specs += [pl.BlockSpec((1, tq, 128), lambda p, i: (p, i, 0)),
                     pl.BlockSpec((1, 8, T), lambda p, i: (p, 0, 0))]
        args += [fcol, frow]
        out_specs.append(pl.BlockSpec((1, tq, 128), lambda p, i: (p, i, 0)))
        out_shape.append(jax.ShapeDtypeStruct((2, T, 128), F32))
        scratch.append(pltpu.VMEM((tq, 128), F32))
    return pl.pallas_call(
        body, name=name, grid=(2, T // tq), in_specs=in_specs, out_specs=out_specs, out_shape=out_shape,
        scratch_shapes=scratch, compiler_params=_cparams(("parallel", "parallel")),
    )(*args)


def flash_dkv(q, k, v, do, lrow, drow, fcol, frow, *, qblk, kblk, vblk, doblk, nq, scale, name):
    T = q.shape[0]
    tq = tk = _tile(T, 256)
    wq = 128 * nq
    nqb = T // tq
    has_f = fcol is not None

    def body(*refs):
        if has_f:
            (q_ref, k_ref, v_ref, do_ref, lr_ref, dr_ref, fc_ref, fr_ref,
             dk_ref, dv_ref, df_ref, dk_sc, dv_sc, df_sc) = refs
        else:
            q_ref, k_ref, v_ref, do_ref, lr_ref, dr_ref, dk_ref, dv_ref, dk_sc, dv_sc = refs
        j = pl.program_id(1)
        krows = j * tk + lax.broadcasted_iota(jnp.int32, (tk, 1), 0)
        kb = k_ref[...]
        vb = v_ref[...]
        dv_sc[...] = jnp.zeros((tk, 128), F32)
        dks, dfs = [], []
        for h in range(2):
            hm = _half_mask(h)
            if nq == 1:
                kh = jnp.where(hm, kb, 0).astype(MXU)
            else:
                kh = kb[:, 128 * h:128 * (h + 1)].astype(MXU)
            vh = jnp.where(hm, vb, 0).astype(MXU)
            if has_f:
                fk = _lane_pick(fc_ref[0], h)
                df_sc[...] = jnp.zeros((tk, 128), F32)
            if nq == 2 or h == 0:
                dk_sc[...] = jnp.zeros((tk, 128), F32)

            def step(i, carry):
                off = pl.multiple_of(i * tq, tq)
                qs = q_ref[pl.ds(off, tq), :]
                qh = qs if nq == 1 else qs[:, 128 * h:128 * (h + 1)]
                sT = _dot_nt(kh, qh) * scale
                if has_f:
                    sT = sT + (fr_ref[0, h:h + 1, pl.ds(off, tq)] - fk)
                qcols = i * tq + lax.broadcasted_iota(jnp.int32, (1, tq), 1)
                pT = jnp.where(krows <= qcols, jnp.exp(sT - lr_ref[0, h:h + 1, pl.ds(off, tq)]), 0.0)
                dos = do_ref[pl.ds(off, tq), :]
                dpT = _dot_nt(vh, dos)
                dsT = pT * (dpT - dr_ref[0, h:h + 1, pl.ds(off, tq)])
                dv_sc[...] += _dot(pT, jnp.where(hm, dos, 0))
                qq = jnp.where(hm, qs, 0) if nq == 1 else qh
                dk_sc[...] += _dot(dsT, qq)
                if has_f:
                    part = dsT[:, 0:128]
                    for c in range(1, tq // 128):
                        part = part + dsT[:, 128 * c:128 * (c + 1)]
                    df_sc[...] += part
                return carry

            lax.fori_loop(j, nqb, step, 0)
            if nq == 2:
                dks.append(dk_sc[...] * scale)
            if has_f:
                dfs.append(-jnp.sum(df_sc[...], axis=1, keepdims=True))
        if nq == 1:
            dk_ref[...] = dk_sc[...] * scale
        else:
            dk_ref[...] = jnp.concatenate(dks, axis=1)
        dv_ref[...] = dv_sc[...]
        if has_f:
            df_ref[0] = _col_stat_tile(dfs[0], dfs[1])

    in_specs = [pl.BlockSpec((T, wq), lambda p, j: (0, qblk + p)),
                pl.BlockSpec((tk, wq), lambda p, j: (j, kblk + p)),
                pl.BlockSpec((tk, 128), lambda p, j: (j, vblk + p)),
                pl.BlockSpec((T, 128), lambda p, j: (0, doblk + p)),
                pl.BlockSpec((1, 8, T), lambda p, j: (p, 0, 0)),
                pl.BlockSpec((1, 8, T), lambda p, j: (p, 0, 0))]
    args = [q, k, v, do, lrow, drow]
    out_specs = [pl.BlockSpec((tk, wq), lambda p, j: (j, p)), pl.BlockSpec((tk, 128), lambda p, j: (j, p))]
    out_shape = [jax.ShapeDtypeStruct((T, 2 * wq), F32), jax.ShapeDtypeStruct((T, 256), F32)]
    scratch = [pltpu.VMEM((tk, 128), F32), pltpu.VMEM((tk, 128), F32)]
    if has_f:
        in_specs += [pl.BlockSpec((1, tk, 128), lambda p, j: (p, j, 0)),
                     pl.BlockSpec((1, 8, T), lambda p, j: (p, 0, 0))]
        args += [fcol, frow]
        out_specs.append(pl.BlockSpec((1, tk, 128), lambda p, j: (p, j, 0)))
        out_shape.append(jax.ShapeDtypeStruct((2, T, 128), F32))
        scratch.append(pltpu.VMEM((tk, 128), F32))
    return pl.pallas_call(
        body, name=name, grid=(2, T // tk), in_specs=in_specs, out_specs=out_specs, out_shape=out_shape,
        scratch_shapes=scratch, compiler_params=_cparams(("parallel", "parallel")),
    )(*args)


def _swa_align(pair, e, h):
    sel = jnp.where(_half_mask(e), pair, 0.0)
    if e == h:
        return sel
    return pltpu.roll(sel, HEAD_DIM, 1)


def _swa_mask(n):
    W = WINDOW
    qi = lax.broadcasted_iota(jnp.int32, (W, 2 * W), 0) + W
    kj = lax.broadcasted_iota(jnp.int32, (W, 2 * W), 1)
    dist = qi - kj
    return (dist >= 0) & (dist < W) & ((n > 0) | (kj >= W))


def swa_fwd(proj, bias, sinks):
    T = proj.shape[0]
    W = WINDOW
    nb = T // W
    scale = HEAD_DIM ** -0.5

    def body(sink_ref, q_ref, kp_ref, kc_ref, vp_ref, vc_ref, b_ref, o_ref, l_ref):
        n = pl.program_id(0)
        mask = _swa_mask(n)
        kband = jnp.concatenate([kp_ref[...], kc_ref[...]], axis=0).astype(MXU)
        vband = jnp.concatenate([vp_ref[...], vc_ref[...]], axis=0).astype(MXU)
        lane = lax.broadcasted_iota(jnp.int32, (1, 128), 1)
        lse_tile = jnp.zeros((W, 128), F32)
        pairs = []
        for h in range(2):
            full = []
            for g in range(4):
                hq = 4 * h + g
                qa = _swa_align(q_ref[:, 128 * (hq // 2):128 * (hq // 2 + 1)], hq % 2, h)
                s = _dot_nt(qa, kband) * scale + b_ref[hq]
                s = jnp.where(mask, s, NEG)
                sink = sink_ref[hq]
                m = jnp.maximum(jnp.max(s, axis=1, keepdims=True), sink)
                e = jnp.exp(s - m)
                l = jnp.sum(e, axis=1, keepdims=True) + jnp.exp(sink - m)
                r = jnp.where(_half_mask(h), _dot(e, vband), 0.0) / l
                full.append(r + pltpu.roll(r, HEAD_DIM, 1))
                lse_tile = jnp.where(lane == hq, m + jnp.log(l), lse_tile)
            pairs.append(jnp.where(_half_mask(0), full[0], full[1]))
            pairs.append(jnp.where(_half_mask(0), full[2], full[3]))
        o_ref[...] = jnp.concatenate(pairs, axis=1)
        l_ref[...] = lse_tile

    prev = lambda n: (jnp.maximum(n - 1, 0), C_KA // 128)
    cur = lambda n: (n, C_KA // 128)
    prev_v = lambda n: (jnp.maximum(n - 1, 0), C_VA // 128)
    cur_v = lambda n: (n, C_VA // 128)
    return pl.pallas_call(
        body, name="swa_fwd", grid=(nb,),
        in_specs=[pl.BlockSpec(memory_space=pltpu.SMEM),
                  pl.BlockSpec((W, 512), lambda n: (n, 0)),
                  pl.BlockSpec((W, 128), prev), pl.BlockSpec((W, 128), cur),
                  pl.BlockSpec((W, 128), prev_v), pl.BlockSpec((W, 128), cur_v),
                  pl.BlockSpec((8, W, 2 * W), lambda n: (0, 0, 0))],
        out_specs=[pl.BlockSpec((W, 512), lambda n: (n, 0)), pl.BlockSpec((W, 128), lambda n: (n, 0))],
        out_shape=[jax.ShapeDtypeStruct((T, 512), F32), jax.ShapeDtypeStruct((T, 128), F32)],
        compiler_params=_cparams(("parallel",)),
    )(sinks, proj, proj, proj, proj, proj, bias)


def swa_bwd(proj, bias, sinks, do, o, lse):
    T = proj.shape[0]
    W = WINDOW
    nb = T // W
    scale = HEAD_DIM ** -0.5

    def body(sink_ref, q_ref, kp_ref, kc_ref, vp_ref, vc_ref, b_ref, do_ref, o_ref, l_ref,
             dq_ref, dk_ref, dv_ref, db_ref, dsk_ref, ck, cv):
        n = pl.program_id(0)

        @pl.when(n == 0)
        def _():
            ck[...] = jnp.zeros_like(ck)
            cv[...] = jnp.zeros_like(cv)
            db_ref[...] = jnp.zeros_like(db_ref)
            dsk_ref[...] = jnp.zeros_like(dsk_ref)

        @pl.when(n < nb)
        def _():
            mask = _swa_mask(n)
            kb32 = jnp.concatenate([kp_ref[...], kc_ref[...]], axis=0)
            vb32 = jnp.concatenate([vp_ref[...], vc_ref[...]], axis=0)
            kband = kb32.astype(MXU)
            sub = lax.broadcasted_iota(jnp.int32, (8, 1), 0)
            dk_band = jnp.zeros((2 * W, 128), F32)
            dv_band = jnp.zeros((2 * W, 128), F32)
            dsk = jnp.zeros((8, 128), F32)
            dq_pairs = []
            for h in range(2):
                hm = _half_mask(h)
                km = jnp.where(hm, kb32, 0.0).astype(MXU)
                vm = jnp.where(hm, vb32, 0.0).astype(MXU)
                full = []
                for g in range(4):
                    hq = 4 * h + g
                    pb = slice(128 * (hq // 2), 128 * (hq // 2 + 1))
                    e = hq % 2
                    qa = _swa_align(q_ref[:, pb], e, h)
                    dop = do_ref[:, pb]
                    doa = _swa_align(dop, e, h)
                    D = jnp.sum(jnp.where(_half_mask(e), dop * o_ref[:, pb], 0.0), axis=1, keepdims=True)
                    lse_h = _lane_pick(l_ref[...], hq)
                    s = _dot_nt(qa, kband) * scale + b_ref[hq]
                    p = jnp.where(mask, jnp.exp(s - lse_h), 0.0)
                    psink = jnp.exp(sink_ref[hq] - lse_h)
                    dsk = dsk + jnp.where(sub == hq, -jnp.sum(psink * D, axis=0, keepdims=True), 0.0)
                    dp = _dot_nt(doa, vm)
                    ds = p * (dp - D)
                    db_ref[hq] += ds
                    dq = _dot(ds, km) * scale
                    full.append(dq + pltpu.roll(dq, HEAD_DIM, 1))
                    dk_band = dk_band + _dot(ds.T, qa) * scale
                    dv_band = dv_band + _dot(p.T, doa)
                dq_pairs.append(jnp.where(_half_mask(0), full[0], full[1]))
                dq_pairs.append(jnp.where(_half_mask(0), full[2], full[3]))
            dq_ref[...] = jnp.concatenate(dq_pairs, axis=1)
            dsk_ref[...] += dsk
            dk_ref[...] = ck[...] + dk_band[0:W]
            dv_ref[...] = cv[...] + dv_band[0:W]
            ck[...] = dk_band[W:2 * W]
            cv[...] = dv_band[W:2 * W]

        @pl.when(n == nb)
        def _():
            dk_ref[...] = ck[...]
            dv_ref[...] = cv[...]

    cl = lambda n: jnp.minimum(n, nb - 1)
    pv = lambda n: jnp.maximum(jnp.minimum(n, nb - 1) - 1, 0)
    return pl.pallas_call(
        body, name="swa_bwd", grid=(nb + 1,),
        in_specs=[pl.BlockSpec(memory_space=pltpu.SMEM),
                  pl.BlockSpec((W, 512), lambda n: (cl(n), 0)),
                  pl.BlockSpec((W, 128), lambda n: (pv(n), C_KA // 128)),
                  pl.BlockSpec((W, 128), lambda n: (cl(n), C_KA // 128)),
                  pl.BlockSpec((W, 128), lambda n: (pv(n), C_VA // 128)),
                  pl.BlockSpec((W, 128), lambda n: (cl(n), C_VA // 128)),
                  pl.BlockSpec((8, W, 2 * W), lambda n: (0, 0, 0)),
                  pl.BlockSpec((W, 512), lambda n: (cl(n), 0)),
                  pl.BlockSpec((W, 512), lambda n: (cl(n), 0)),
                  pl.BlockSpec((W, 128), lambda n: (cl(n), 0))],
        out_specs=[pl.BlockSpec((W, 512), lambda n: (cl(n), 0)),
                   pl.BlockSpec((W, 128), lambda n: (jnp.maximum(n - 1, 0), 0)),
                   pl.BlockSpec((W, 128), lambda n: (jnp.maximum(n - 1, 0), 0)),
                   pl.BlockSpec((8, W, 2 * W), lambda n: (0, 0, 0)),
                   pl.BlockSpec((8, 128), lambda n: (0, 0))],
        out_shape=[jax.ShapeDtypeStruct((T, 512), F32), jax.ShapeDtypeStruct((T, 128), F32),
                   jax.ShapeDtypeStruct((T, 128), F32), jax.ShapeDtypeStruct((8, W, 2 * W), F32),
                   jax.ShapeDtypeStruct((8, 128), F32)],
        scratch_shapes=[pltpu.VMEM((W, 128), F32), pltpu.VMEM((W, 128), F32)],
        compiler_params=_cparams(("arbitrary",)),
    )(sinks, proj, proj, proj, proj, proj, bias, do, o, lse)


def swa_bias_table(rel_bias):
    W = WINDOW
    qi = jnp.arange(W, dtype=jnp.int32)[:, None] + W
    kj = jnp.arange(2 * W, dtype=jnp.int32)[None, :]
    dist = qi - kj
    max_exact = REL_BUCKETS // 2
    d = jnp.maximum(dist, 0)
    log_ratio = jnp.log(jnp.maximum(d, 1).astype(F32) / max_exact) / math.log(REL_MAX_DIST / max_exact)
    large = jnp.minimum(max_exact + (log_ratio * (REL_BUCKETS - max_exact)).astype(jnp.int32), REL_BUCKETS - 1)
    bucket = jnp.where(d < max_exact, d, large)
    bucket = bucket.reshape(-1)
    onehot = (bucket[None, :] == jnp.arange(REL_BUCKETS, dtype=jnp.int32)[:, None]).astype(F32)
    bias = jnp.dot(rel_bias.astype(F32).T, onehot, precision=lax.Precision.HIGHEST)
    return bias.reshape(SWA_Q_HEADS, W, 2 * W), bucket


def attn_out(oa, ob, oc, gn, wout, gpost, x):
    T = x.shape[0]
    tm = _tile(T, 512)

    def body(oa_ref, ob_ref, oc_ref, gn_ref, w_ref, gp_ref, x_ref, x2_ref, y_ref, mT_ref):
        g = gn_ref[...]
        mixed = jnp.concatenate([_rms_fwd(oa_ref[...], g[:, 0:512]), _rms_fwd(ob_ref[...], g[:, 512:768]),
                                 _rms_fwd(oc_ref[...], g[:, 768:1024])], axis=1)
        mT_ref[...] = mixed.T.astype(MXU)
        y = _dot(mixed, w_ref[...])
        y_ref[...] = y
        x2_ref[...] = x_ref[...] + _rms_fwd(y, gp_ref[...])

    row = lambda i: (i, 0)
    const = lambda i: (0, 0)
    return pl.pallas_call(
        body, name="attn_out", grid=(T // tm,),
        in_specs=[pl.BlockSpec((tm, 512), row), pl.BlockSpec((tm, 256), row), pl.BlockSpec((tm, 256), row),
                  pl.BlockSpec((1, 1024), const), pl.BlockSpec((1024, 1024), const), pl.BlockSpec((1, 1024), const),
                  pl.BlockSpec((tm, 1024), row)],
        out_specs=[pl.BlockSpec((tm, 1024), row), pl.BlockSpec((tm, 1024), row),
                   pl.BlockSpec((1024, tm), lambda i: (0, i))],
        out_shape=[jax.ShapeDtypeStruct((T, 1024), F32), jax.ShapeDtypeStruct((T, 1024), F32),
                   jax.ShapeDtypeStruct((1024, T), MXU)],
        compiler_params=_cparams(("parallel",)),
    )(oa, ob, oc, gn, wout, gpost, x)


def attn_out_bwd(dx2, y, oa, ob, oc, gn, wout, gpost):
    T = dx2.shape[0]
    tm = _tile(T, 512)

    def body(dx_ref, y_ref, oa_ref, ob_ref, oc_ref, gn_ref, w_ref, gp_ref,
             dy_ref, da_ref, db_ref, dc_ref, dgn_ref, dgp_ref):
        first = pl.program_id(0) == 0
        dy, dgp = _rms_bwd(dx_ref[...], y_ref[...], gp_ref[...])
        dy_ref[...] = dy.astype(MXU)
        _acc_out(dgp_ref, dgp, first)
        dm = _dot_nt(dy, w_ref[...])
        g = gn_ref[...]
        da, dga = _rms_bwd(dm[:, 0:512], oa_ref[...], g[:, 0:512])
        db, dgb = _rms_bwd(dm[:, 512:768], ob_ref[...], g[:, 512:768])
        dc, dgc = _rms_bwd(dm[:, 768:1024], oc_ref[...], g[:, 768:1024])
        da_ref[...] = da
        db_ref[...] = db
        dc_ref[...] = dc
        _acc_out(dgn_ref, jnp.concatenate([dga, dgb, dgc], axis=1), first)

    row = lambda i: (i, 0)
    const = lambda i: (0, 0)
    return pl.pallas_call(
        body, name="attn_out_bwd", grid=(T // tm,),
        in_specs=[pl.BlockSpec((tm, 1024), row), pl.BlockSpec((tm, 1024), row),
                  pl.BlockSpec((tm, 512), row), pl.BlockSpec((tm, 256), row), pl.BlockSpec((tm, 256), row),
                  pl.BlockSpec((1, 1024), const), pl.BlockSpec((1024, 1024), const), pl.BlockSpec((1, 1024), const)],
        out_specs=[pl.BlockSpec((tm, 1024), row), pl.BlockSpec((tm, 512), row), pl.BlockSpec((tm, 256), row),
                   pl.BlockSpec((tm, 256), row), pl.BlockSpec((1, 1024), const), pl.BlockSpec((1, 1024), const)],
        out_shape=[jax.ShapeDtypeStruct((T, 1024), MXU), jax.ShapeDtypeStruct((T, 512), F32),
                   jax.ShapeDtypeStruct((T, 256), F32), jax.ShapeDtypeStruct((T, 256), F32),
                   jax.ShapeDtypeStruct((1, 1024), F32), jax.ShapeDtypeStruct((1, 1024), F32)],
        compiler_params=_cparams(("arbitrary",)),
    )(dx2, y, oa, ob, oc, gn, wout, gpost)


FF_TILE = 256
_GELU_C = math.sqrt(2.0 / math.pi)


def _gelu(x):
    return 0.5 * x * (1.0 + jnp.tanh(_GELU_C * (x + 0.044715 * x * x * x)))


def _gelu_grad(x):
    t = jnp.tanh(_GELU_C * (x + 0.044715 * x * x * x))
    return 0.5 * (1.0 + t) + 0.5 * x * (1.0 - t * t) * _GELU_C * (1.0 + 3 * 0.044715 * x * x)


def _conv_taps(u, hal_ref, first):
    row = lax.broadcasted_iota(jnp.int32, (u.shape[0], 1), 0)
    h6 = jnp.where(first, 0.0, hal_ref[6:7, :])
    h7 = jnp.where(first, 0.0, hal_ref[7:8, :])
    r1 = jnp.where(row == 0, h7, pltpu.roll(u, 1, 0))
    r2 = jnp.where(row == 0, h6, jnp.where(row == 1, h7, pltpu.roll(u, 2, 0)))
    return r1, r2


def ffn_fwd(u0, convw, convb, wdown, gpost, x2):
    T = x2.shape[0]
    tm, tn = _tile(T, 512), FF_TILE
    nj = D_FF // tn

    def body(ug_ref, uu_ref, hg_ref, hu_ref, wg_ref, wu_ref, bg_ref, bu_ref, wd_ref, gp_ref, x_ref,
             x3_ref, y_ref, aT_ref, acc):
        i, j = pl.program_id(0), pl.program_id(1)
        first = i == 0

        def conv(u_ref, h_ref, w_ref, b_ref):
            u = u_ref[...]
            r1, r2 = _conv_taps(u, h_ref, first)
            return b_ref[...] + w_ref[0:1, :] * r2 + w_ref[1:2, :] * r1 + w_ref[2:3, :] * u

        a = _gelu(conv(ug_ref, hg_ref, wg_ref, bg_ref)) * conv(uu_ref, hu_ref, wu_ref, bu_ref)
        aT_ref[...] = a.T.astype(MXU)
        _acc_out(acc, _dot(a, wd_ref[...]), j == 0)

        @pl.when(j == nj - 1)
        def _():
            y = acc[...]
            y_ref[...] = y
            x3_ref[...] = x_ref[...] + _rms_fwd(y, gp_ref[...])

    halo = lambda off: (lambda i, j: (jnp.maximum(i * (tm // 8) - 1, 0), off + j))
    return pl.pallas_call(
        body, name="ffn_fwd", grid=(T // tm, nj),
        in_specs=[pl.BlockSpec((tm, tn), lambda i, j: (i, j)), pl.BlockSpec((tm, tn), lambda i, j: (i, nj + j)),
                  pl.BlockSpec((8, tn), halo(0)), pl.BlockSpec((8, tn), halo(nj)),
                  pl.BlockSpec((3, tn), lambda i, j: (0, j)), pl.BlockSpec((3, tn), lambda i, j: (0, nj + j)),
                  pl.BlockSpec((1, tn), lambda i, j: (0, j)), pl.BlockSpec((1, tn), lambda i, j: (0, nj + j)),
                  pl.BlockSpec((tn, 1024), lambda i, j: (j, 0)),
                  pl.BlockSpec((1, 1024), lambda i, j: (0, 0)),
                  pl.BlockSpec((tm, 1024), lambda i, j: (i, 0))],
        out_specs=[pl.BlockSpec((tm, 1024), lambda i, j: (i, 0)), pl.BlockSpec((tm, 1024), lambda i, j: (i, 0)),
                   pl.BlockSpec((tn, tm), lambda i, j: (j, i))],
        out_shape=[jax.ShapeDtypeStruct((T, 1024), F32), jax.ShapeDtypeStruct((T, 1024), F32),
                   jax.ShapeDtypeStruct((D_FF, T), MXU)],
        scratch_shapes=[pltpu.VMEM((tm, 1024), F32)],
        compiler_params=_cparams(("parallel", "arbitrary")),
    )(u0, u0, u0, u0, convw, convw, convb, convb, wdown, gpost, x2)


def ffn_bwd(dx3, y, u0, convw, convb, wdown, gpost):
    T = dx3.shape[0]
    tm, tn = _tile(T, 512), FF_TILE
    nj = D_FF // tn
    ni = T // tm

    def body(dx_ref, y_ref, ug_ref, uu_ref, hg_ref, hu_ref, wg_ref, wu_ref, bg_ref, bu_ref, wd_ref, gp_ref,
             dy_ref, dug_ref, duu_ref, dcg_ref, dcu_ref, dgp_ref, dy_sc, cg, cu, ag, au):
        s, j = pl.program_id(0), pl.program_id(1)
        i = ni - 1 - s
        first_tok = i == 0
        row = lax.broadcasted_iota(jnp.int32, (tm, 1), 0)
        sub = lax.broadcasted_iota(jnp.int32, (8, 1), 0)

        @pl.when(j == 0)
        def _():
            dy, dgp = _rms_bwd(dx_ref[...], y_ref[...], gp_ref[...])
            dy_sc[...] = dy.astype(MXU)
            dy_ref[...] = dy.astype(MXU)
            _acc_out(dgp_ref, dgp, s == 0)

        @pl.when(s == 0)
        def _():
            cg[j] = jnp.zeros((8, tn), F32)
            cu[j] = jnp.zeros((8, tn), F32)
            ag[j] = jnp.zeros((8, tn), F32)
            au[j] = jnp.zeros((8, tn), F32)

        da = _dot_nt(dy_sc[...], wd_ref[...])

        def conv(u_ref, h_ref, w_ref, b_ref):
            u = u_ref[...]
            r1, r2 = _conv_taps(u, h_ref, first_tok)
            return b_ref[...] + w_ref[0:1, :] * r2 + w_ref[1:2, :] * r1 + w_ref[2:3, :] * u, u, r1, r2

        gate, ugv, g1, g2 = conv(ug_ref, hg_ref, wg_ref, bg_ref)
        up, uuv, u1, u2 = conv(uu_ref, hu_ref, wu_ref, bu_ref)
        gl = _gelu(gate)
        dup = da * gl
        dgate = da * up * _gelu_grad(gate)

        def conv_bwd(du, u, r1, r2, w_ref, c_ref, a_ref, du_ref):
            nxt = c_ref[j]
            n0, n1 = nxt[0:1, :], nxt[1:2, :]
            f1 = jnp.where(row == tm - 1, n0, pltpu.roll(du, tm - 1, 0))
            f2 = jnp.where(row == tm - 1, n1, jnp.where(row == tm - 2, n0, pltpu.roll(du, tm - 2, 0)))
            du_ref[...] = (w_ref[2:3, :] * du + w_ref[1:2, :] * f1 + w_ref[0:1, :] * f2).astype(MXU)
            c_ref[j] = du[0:8, :]
            red = lambda v: jnp.sum(v, axis=0, keepdims=True)
            part = jnp.where(sub == 0, red(du * r2), jnp.where(sub == 1, red(du * r1), jnp.where(
                sub == 2, red(du * u), jnp.where(sub == 3, red(du), 0.0))))
            a_ref[j] = a_ref[j] + part
            return a_ref[j]

        dcg_ref[0] = conv_bwd(dgate, ugv, g1, g2, wg_ref, cg, ag, dug_ref)
        dcu_ref[0] = conv_bwd(dup, uuv, u1, u2, wu_ref, cu, au, duu_ref)

    rev = lambda s: ni - 1 - s
    halo = lambda off: (lambda s, j: (jnp.maximum(rev(s) * (tm // 8) - 1, 0), off + j))
    return pl.pallas_call(
        body, name="ffn_bwd", grid=(ni, nj),
        in_specs=[pl.BlockSpec((tm, 1024), lambda s, j: (rev(s), 0)), pl.BlockSpec((tm, 1024), lambda s, j: (rev(s), 0)),
                  pl.BlockSpec((tm, tn), lambda s, j: (rev(s), j)), pl.BlockSpec((tm, tn), lambda s, j: (rev(s), nj + j)),
                  pl.BlockSpec((8, tn), halo(0)), pl.BlockSpec((8, tn), halo(nj)),
                  pl.BlockSpec((3, tn), lambda s, j: (0, j)), pl.BlockSpec((3, tn), lambda s, j: (0, nj + j)),
                  pl.BlockSpec((1, tn), lambda s, j: (0, j)), pl.BlockSpec((1, tn), lambda s, j: (0, nj + j)),
                  pl.BlockSpec((tn, 1024), lambda s, j: (j, 0)),
                  pl.BlockSpec((1, 1024), lambda s, j: (0, 0))],
        out_specs=[pl.BlockSpec((tm, 1024), lambda s, j: (rev(s), 0)),
                   pl.BlockSpec((tm, tn), lambda s, j: (rev(s), j)), pl.BlockSpec((tm, tn), lambda s, j: (rev(s), j)),
                   pl.BlockSpec((1, 8, tn), lambda s, j: (s, 0, j)), pl.BlockSpec((1, 8, tn), lambda s, j: (s, 0, j)),
                   pl.BlockSpec((1, 1024), lambda s, j: (0, 0))],
        out_shape=[jax.ShapeDtypeStruct((T, 1024), MXU), jax.ShapeDtypeStruct((T, D_FF), MXU),
                   jax.ShapeDtypeStruct((T, D_FF), MXU),
                   jax.ShapeDtypeStruct((ni, 8, D_FF), F32), jax.ShapeDtypeStruct((ni, 8, D_FF), F32),
                   jax.ShapeDtypeStruct((1, 1024), F32)],
        scratch_shapes=[pltpu.VMEM((tm, 1024), MXU)] + [pltpu.VMEM((nj, 8, tn), F32)] * 4,
        compiler_params=_cparams(("arbitrary", "arbitrary")),
    )(dx3, y, u0, u0, u0, u0, convw, convw, convb, convb, wdown, gpost)


ELEMS_PER_BLOCK = 512 * 1024


def _row_block(R, C):
    if R * C <= ELEMS_PER_BLOCK or R % 8:
        return R
    best = 8
    for t in range(8, R + 1, 8):
        if R % t == 0 and t * C <= ELEMS_PER_BLOCK:
            best = t
    return best


def adamw(w, g, m, v, name):
    R, C = w.shape
    tr = _row_block(R, C)
    c1 = 1.0 - ADAM_B1 ** ADAM_STEP
    c2 = 1.0 - ADAM_B2 ** ADAM_STEP

    def body(w_ref, g_ref, m_ref, v_ref, d_ref, nm_ref, nv_ref):
        gv = g_ref[...]
        nm = ADAM_B1 * m_ref[...] + (1.0 - ADAM_B1) * gv
        nv = ADAM_B2 * v_ref[...] + (1.0 - ADAM_B2) * (gv * gv)
        nm_ref[...] = nm
        nv_ref[...] = nv
        d_ref[...] = -ADAM_LR * ((nm / c1) / (jnp.sqrt(nv / c2) + ADAM_EPS) + ADAM_WD * w_ref[...])

    spec = pl.BlockSpec((tr, C), lambda i: (i, 0))
    return pl.pallas_call(
        body, name=name, grid=(R // tr,), in_specs=[spec] * 4, out_specs=[spec] * 3,
        out_shape=[jax.ShapeDtypeStruct((R, C), F32)] * 3,
        compiler_params=_cparams(("parallel",)),
    )(w, g, m, v)


def sum_devices(buf, name):
    _, R, C = buf.shape
    tr = _row_block(R, C * 4)

    def body(b_ref, o_ref):
        acc = b_ref[0].astype(F32)
        for d in range(1, N_DEV):
            acc = acc + b_ref[d].astype(F32)
        o_ref[...] = acc

    return pl.pallas_call(
        body, name=name, grid=(R // tr,),
        in_specs=[pl.BlockSpec((N_DEV, tr, C), lambda i: (0, i, 0))],
        out_specs=pl.BlockSpec((tr, C), lambda i: (i, 0)),
        out_shape=jax.ShapeDtypeStruct((R, C), F32),
        compiler_params=_cparams(("parallel",)),
    )(buf)


def exchange(src, name, gather):
    shape = src.shape if not gather else (N_DEV,) + src.shape

    def body(src_ref, out_ref, send_sems, recv_sems, local_sem):
        x, y, c = lax.axis_index("x"), lax.axis_index("y"), lax.axis_index("c")
        me = 4 * x + 2 * y + c
        flip = lambda a, bit: 1 - a if bit else a

        def copy(k):
            px, py, pc = flip(x, (k >> 2) & 1), flip(y, (k >> 1) & 1), flip(c, k & 1)
            peer = 4 * px + 2 * py + pc
            outgoing = pltpu.make_async_remote_copy(
                src_ref=src_ref if gather else src_ref.at[peer], dst_ref=out_ref.at[me],
                send_sem=send_sems.at[k - 1], recv_sem=recv_sems.at[k - 1],
                device_id=(px, py, pc), device_id_type=pl.DeviceIdType.MESH)
            incoming = pltpu.make_async_remote_copy(
                src_ref=src_ref if gather else src_ref.at[me], dst_ref=out_ref.at[peer],
                send_sem=send_sems.at[k - 1], recv_sem=recv_sems.at[k - 1],
                device_id=(px, py, pc), device_id_type=pl.DeviceIdType.MESH)
            return outgoing, incoming

        copies = [copy(k) for k in range(1, N_DEV)]
        for outgoing, _ in copies:
            outgoing.start()
        mine = pltpu.make_async_copy(src_ref if gather else src_ref.at[me], out_ref.at[me], local_sem)
        mine.start()
        for _, incoming in copies:
            incoming.wait_recv()
        for outgoing, _ in copies:
            outgoing.wait_send()
        mine.wait()

    return pl.pallas_call(
        body, name=name,
        in_specs=[pl.BlockSpec(memory_space=pl.ANY)], out_specs=pl.BlockSpec(memory_space=pl.ANY),
        out_shape=jax.ShapeDtypeStruct(shape, src.dtype),
        scratch_shapes=[pltpu.SemaphoreType.DMA((N_DEV - 1,)), pltpu.SemaphoreType.DMA((N_DEV - 1,)),
                        pltpu.SemaphoreType.DMA],
    )(src)


def _pack(parts, cols, row_align, dtype):
    flat = jnp.concatenate([p.astype(dtype) for p in parts], axis=-1)
    n = flat.shape[-1]
    block = cols * row_align
    total = -(-n // block) * block
    flat = jnp.pad(flat, [(0, 0)] * (flat.ndim - 1) + [(0, total - n)])
    return flat.reshape(flat.shape[:-1] + (total // cols, cols))


def _unpack(buf, shapes):
    lead = buf.shape[:-2]
    flat = buf.reshape(lead + (-1,))
    out, off = [], 0
    for s in shapes:
        n = int(np.prod(s))
        out.append(flat[..., off:off + n].reshape(lead + tuple(s)))
        off += n
    return out


SHARD_SHAPES = [(128, IN_COLS), (256, 48), (128, 64), (128, 1024), (1024, 704), (352, 1024)]
SHARDED = ["w_in", "w_uq", "w_ukv", "w_out", "w_up", "w_down"]


def _full_from_shards(name, s):
    if name in ("w_in", "w_out", "w_down"):
        return s.reshape((-1, s.shape[-1]))
    return s.transpose(1, 0, 2).reshape((s.shape[1], -1))


def _shards_from_full(name, f):
    if name in ("w_in", "w_out", "w_down"):
        return f.reshape((N_DEV, -1, f.shape[-1]))
    return f.reshape((f.shape[0], N_DEV, -1)).transpose(1, 0, 2)


def _perm_w_in(w):
    z = lambda n: jnp.zeros((w.shape[0], n), w.dtype)
    return jnp.concatenate([w[:, :1536], w[:, 1540:1924], w[:, 1536:1540], z(60), w[:, 1924:1956], z(32)], axis=1)


def _unperm_w_in(d):
    return jnp.concatenate([d[:, :1536], d[:, 1920:1924], d[:, 1536:1920], d[:, 1984:2016]], axis=1)


def _perm_w_uq(w):
    return jnp.pad(w.reshape(256, 4, MLA_QK_DIM), ((0, 0), (0, 0), (0, 128 - MLA_QK_DIM))).reshape(256, 512)


def _unperm_w_uq(d):
    return d.reshape(256, 4, 128)[:, :, :MLA_QK_DIM].reshape(256, 4 * MLA_QK_DIM)


def _perm_w_ukv(w):
    w4 = w.reshape(128, 4, 128)
    k = jnp.pad(w4[:, :, :64], ((0, 0), (0, 0), (0, 64))).reshape(128, 512)
    return jnp.concatenate([k, w4[:, :, 64:].reshape(128, 256)], axis=1)


def _unperm_w_ukv(d):
    dk = d[:, :512].reshape(128, 4, 128)[:, :, :64]
    dv = d[:, 512:].reshape(128, 4, 64)
    return jnp.concatenate([dk, dv], axis=-1).reshape(128, 512)


def _row(v, width=None):
    v = v.reshape(1, -1).astype(F32)
    if width is not None and v.shape[1] < width:
        v = jnp.pad(v, ((0, 0), (0, width - v.shape[1])))
    return v


def _layer_fwd(x, P, shared):
    cosr, sinr, bias = shared
    proj, hT = norm_matmul(x, P["g_pre"], P["w_in_p"], "in_proj")
    qm, km, vm, cqT, ckvT = mla_prep(proj, P["gq"], P["gkv"], P["w_uq_p"], P["w_ukv_p"], cosr, sinr)
    fcol, frow = fox_gate(proj, P["fbias"])
    oa, lse_a = swa_fwd(proj, bias, P["sinks"])
    ob, lcb, lrb = flash_fwd(proj, proj, proj, fcol, frow, qblk=C_QF // 128, kblk=C_KF // 128, vblk=C_VF // 128,
                             nq=1, scale=HEAD_DIM ** -0.5, name="fox_fwd")
    oc, lcc, lrc = flash_fwd(qm, km, vm, None, None, qblk=0, kblk=0, vblk=0, nq=2,
                             scale=MLA_QK_DIM ** -0.5, name="mla_fwd")
    x2, y1, mT = attn_out(oa, ob, oc, P["gn"], P["w_out"], P["g_apost"], x)
    u0, h2T = norm_matmul(x2, P["g_fpre"], P["w_up"], "up_proj")
    x3, y2, aT = ffn_fwd(u0, P["conv_w"], P["conv_b"], P["w_down"], P["g_fpost"], x2)
    S = dict(x=x, proj=proj, hT=hT, qm=qm, km=km, vm=vm, cqT=cqT, ckvT=ckvT, fcol=fcol, frow=frow,
             oa=oa, lse_a=lse_a, ob=ob, lcb=lcb, lrb=lrb, oc=oc, lcc=lcc, lrc=lrc,
             x2=x2, y1=y1, mT=mT, u0=u0, h2T=h2T, y2=y2, aT=aT)
    return x3, S


def _layer_bwd(dx3, P, S, shared):
    cosr, sinr, bias = shared
    proj = S["proj"]
    G = {}
    dy2, dug, duu, dcg, dcu, G["ffn_post_norm"] = ffn_bwd(dx3, S["y2"], S["u0"], P["conv_w"], P["conv_b"],
                                                         P["w_down"], P["g_fpost"])
    du0 = jnp.concatenate([dug, duu], axis=1)
    dconv = jnp.concatenate([dcg[-1], dcu[-1]], axis=1)
    G["conv_w"], G["conv_b"] = dconv[0:3], dconv[3]
    G["w_down"] = matmul_nn(S["aT"], dy2, "dw_down", MXU)
    G["w_up"] = matmul_nn(S["h2T"], du0, "dw_up", MXU)
    dx2, G["ffn_pre_norm"] = matmul_nt_normbwd(du0, P["w_up"], S["x2"], P["g_fpre"], dx3, "up_bwd")
    dy1, doa, dob, doc, G["group_norm"], G["attn_post_norm"] = attn_out_bwd(
        dx2, S["y1"], S["oa"], S["ob"], S["oc"], P["gn"], P["w_out"], P["g_apost"])
    G["w_out"] = matmul_nn(S["mT"], dy1, "dw_out", MXU)
    dqa, dka, dva, dbias, dsk = swa_bwd(proj, bias, P["sinks"], doa, S["oa"], S["lse_a"])
    G["swa_sinks"] = dsk[:, 0]
    fox = dict(qblk=C_QF // 128, kblk=C_KF // 128, vblk=C_VF // 128, doblk=0, nq=1, scale=HEAD_DIM ** -0.5)
    dqf, drow_f, dFq = flash_dq(proj, proj, proj, dob, S["ob"], S["lcb"], S["fcol"], S["frow"], name="fox_dq", **fox)
    dkf, dvf, dF = flash_dkv(proj, proj, proj, dob, S["lrb"], drow_f, S["fcol"], S["frow"], name="fox_dkv", **fox)
    dmisc_f, dfb = fox_gate_bwd(dFq, dF, proj, P["fbias"])
    G["forget_bias"] = dfb[0, 0:4]
    mla = dict(qblk=0, kblk=0, vblk=0, doblk=0, nq=2, scale=MLA_QK_DIM ** -0.5)
    dqm_, drow_c = flash_dq(S["qm"], S["km"], S["vm"], doc, S["oc"], S["lcc"], None, None, name="mla_dq", **mla)
    dkm_, dvm_ = flash_dkv(S["qm"], S["km"], S["vm"], doc, S["lrc"], drow_c, None, None, name="mla_dkv", **mla)
    dqm, dkv, dcq, dckv, dmisc_r, G["q_latent_norm"], G["kv_latent_norm"] = mla_prep_bwd(
        dqm_, dkm_, dvm_, proj, P["gq"], P["gkv"], P["w_uq_p"], P["w_ukv_p"], cosr, sinr)
    G["w_uq"] = _unperm_w_uq(matmul_nn(S["cqT"], dqm, "dw_uq", MXU))
    G["w_ukv"] = _unperm_w_ukv(matmul_nn(S["ckvT"], dkv, "dw_ukv", MXU))
    dproj = jnp.concatenate([dqa, dka, dva, dqf, dkf, dvf, dcq, dckv, dmisc_f + dmisc_r], axis=1).astype(MXU)
    G["w_in"] = _unperm_w_in(matmul_nn(S["hT"], dproj, "dw_in", MXU))
    dx, G["attn_pre_norm"] = matmul_nt_normbwd(dproj, P["w_in_p"], S["x"], P["g_pre"], dx2, "in_bwd")
    return dx, G, dbias


def _layer_params(l, full, small):
    return dict(
        g_pre=_row(small["attn_pre_norm"][l]), w_in_p=_perm_w_in(full["w_in"]),
        gq=_row(small["q_latent_norm"][l]), gkv=_row(small["kv_latent_norm"][l]),
        w_uq_p=_perm_w_uq(full["w_uq"]), w_ukv_p=_perm_w_ukv(full["w_ukv"]),
        fbias=_row(small["forget_bias"][l], 128), sinks=small["swa_sinks"][l].astype(F32),
        gn=_row(small["group_norm"][l]), w_out=full["w_out"], g_apost=_row(small["attn_post_norm"][l]),
        g_fpre=_row(small["ffn_pre_norm"][l]), w_up=full["w_up"], conv_w=full["conv_w"],
        conv_b=_row(small["conv_b"][l]), w_down=full["w_down"], g_fpost=_row(small["ffn_post_norm"][l]))


def _rel_bias_grad(dbias, bucket):
    flat = dbias.reshape(SWA_Q_HEADS, -1)
    hi = flat.astype(MXU)
    lo = (flat - hi.astype(F32)).astype(MXU)
    onehot = (bucket[:, None] == jnp.arange(128, dtype=jnp.int32)[None, :]).astype(MXU)
    r = matmul_nn(jnp.concatenate([hi, lo], axis=0), onehot, "rel_bias_grad")
    return (r[0:8] + r[8:16])[:, :REL_BUCKETS].T


def local_step(x, tgt, fulls, small):
    T = x.shape[0]
    cosr, sinr = rope_tables(T)
    bias, bucket = swa_bias_table(small["rel_bias"])
    shared = (cosr, sinr, bias)
    Ps, Ss = [], []
    h = x
    for l in range(DEPTH):
        P = _layer_params(l, fulls[l], small)
        h, S = _layer_fwd(h, P, shared)
        Ps.append(P)
        Ss.append(S)
    dh, sq = loss_kernel(h, tgt)
    grads = [None] * DEPTH
    dbias_sum = None
    for l in reversed(range(DEPTH)):
        dh, grads[l], dbias = _layer_bwd(dh, Ps[l], Ss[l], shared)
        dbias_sum = dbias if dbias_sum is None else dbias_sum + dbias
    return sq, dh, grads, _rel_bias_grad(dbias_sum, bucket)


WEIGHTS = ['attn_pre_norm', 'w_in', 'forget_bias', 'swa_sinks', 'rel_bias', 'q_latent_norm', 'w_uq',
           'kv_latent_norm', 'w_ukv', 'group_norm', 'w_out', 'attn_post_norm', 'ffn_pre_norm', 'w_up', 'conv_w',
           'conv_b', 'w_down', 'ffn_post_norm']
SMALL_PER_LAYER = ['attn_pre_norm', 'forget_bias', 'swa_sinks', 'q_latent_norm', 'kv_latent_norm', 'group_norm',
                   'attn_post_norm', 'ffn_pre_norm', 'conv_b', 'ffn_post_norm', 'conv_w']


def kernel(x, attn_pre_norm, w_in, forget_bias, swa_sinks, rel_bias, q_latent_norm, w_uq, kv_latent_norm, w_ukv, group_norm, w_out, attn_post_norm, ffn_pre_norm, w_up, conv_w, conv_b, w_down, ffn_post_norm, loss_target, m_attn_pre_norm, m_w_in, m_forget_bias, m_swa_sinks, m_rel_bias, m_q_latent_norm, m_w_uq, m_kv_latent_norm, m_w_ukv, m_group_norm, m_w_out, m_attn_post_norm, m_ffn_pre_norm, m_w_up, m_conv_w, m_conv_b, m_w_down, m_ffn_post_norm, v_attn_pre_norm, v_w_in, v_forget_bias, v_swa_sinks, v_rel_bias, v_q_latent_norm, v_w_uq, v_kv_latent_norm, v_w_ukv, v_group_norm, v_w_out, v_attn_post_norm, v_ffn_pre_norm, v_w_up, v_conv_w, v_conv_b, v_w_down, v_ffn_post_norm):
    W = dict(attn_pre_norm=attn_pre_norm, w_in=w_in, forget_bias=forget_bias, swa_sinks=swa_sinks, rel_bias=rel_bias,
             q_latent_norm=q_latent_norm, w_uq=w_uq, kv_latent_norm=kv_latent_norm, w_ukv=w_ukv,
             group_norm=group_norm, w_out=w_out, attn_post_norm=attn_post_norm, ffn_pre_norm=ffn_pre_norm,
             w_up=w_up, conv_w=conv_w, conv_b=conv_b, w_down=w_down, ffn_post_norm=ffn_post_norm)
    M = dict(attn_pre_norm=m_attn_pre_norm, w_in=m_w_in, forget_bias=m_forget_bias, swa_sinks=m_swa_sinks,
             rel_bias=m_rel_bias, q_latent_norm=m_q_latent_norm, w_uq=m_w_uq, kv_latent_norm=m_kv_latent_norm,
             w_ukv=m_w_ukv, group_norm=m_group_norm, w_out=m_w_out, attn_post_norm=m_attn_post_norm,
             ffn_pre_norm=m_ffn_pre_norm, w_up=m_w_up, conv_w=m_conv_w, conv_b=m_conv_b, w_down=m_w_down,
             ffn_post_norm=m_ffn_post_norm)
    V = dict(attn_pre_norm=v_attn_pre_norm, w_in=v_w_in, forget_bias=v_forget_bias, swa_sinks=v_swa_sinks,
             rel_bias=v_rel_bias, q_latent_norm=v_q_latent_norm, w_uq=v_w_uq, kv_latent_norm=v_kv_latent_norm,
             w_ukv=v_w_ukv, group_norm=v_group_norm, w_out=v_w_out, attn_post_norm=v_attn_post_norm,
             ffn_pre_norm=v_ffn_pre_norm, w_up=v_w_up, conv_w=v_conv_w, conv_b=v_conv_b, w_down=v_w_down,
             ffn_post_norm=v_ffn_post_norm)
    me = 4 * lax.axis_index("x") + 2 * lax.axis_index("y") + lax.axis_index("c")

    fulls = []
    for l in range(DEPTH):
        pack = _pack([W[n][l].reshape(-1) for n in SHARDED], PACK_COLS, PACK_ROW_ALIGN, MXU)
        got = _unpack(exchange(pack, "gather_weights", True), SHARD_SHAPES)
        fulls.append({n: _full_from_shards(n, s) for n, s in zip(SHARDED, got)})
    cw = _unpack(exchange(_pack([conv_w.reshape(-1)], 128, 8, F32), "gather_conv_w", True), [(DEPTH, 3, 704)])[0]
    cw = cw.transpose(1, 2, 0, 3).reshape(DEPTH, 3, 2 * D_FF)
    for l in range(DEPTH):
        fulls[l]["conv_w"] = cw[l]

    sq, dx, grads, drel = local_step(x[0], loss_target[0], fulls, W)

    gshard = {n: [] for n in SHARDED}
    for l in range(DEPTH):
        pack = _pack([_shards_from_full(n, grads[l][n]).reshape(N_DEV, -1) for n in SHARDED],
                     PACK_COLS, PACK_ROW_ALIGN, MXU)
        red = sum_devices(exchange(pack, "scatter_grads", False), "sum_grads")
        for n, g in zip(SHARDED, _unpack(red, SHARD_SHAPES)):
            gshard[n].append(g)
    G = {n: jnp.stack(gshard[n]) for n in SHARDED}

    parts, shapes = [], []
    for l in range(DEPTH):
        for n in SMALL_PER_LAYER:
            parts.append(grads[l][n].astype(F32).reshape(-1))
            shapes.append(grads[l][n].shape)
    parts += [drel.reshape(-1), jnp.sum(sq).reshape(1) * (0.5 / D_MODEL)]
    shapes += [drel.shape, (1,)]
    red = _unpack(sum_devices(exchange(_pack(parts, 128, 8, F32), "gather_small", True), "sum_small"), shapes)
    k = 0
    per = {n: [] for n in SMALL_PER_LAYER}
    for l in range(DEPTH):
        for n in SMALL_PER_LAYER:
            per[n].append(red[k])
            k += 1
    for n in SMALL_PER_LAYER:
        G[n] = jnp.stack(per[n]).reshape((DEPTH, 3, 2 * D_FF) if n == "conv_w" else W[n].shape)
    G["rel_bias"] = red[k]
    loss = red[k + 1][0]
    G["conv_w"] = lax.dynamic_slice_in_dim(G["conv_w"], me * 704, 704, axis=2)

    delta, new_m, new_v = {}, {}, {}
    for n in WEIGHTS:
        shp = W[n].shape
        v2 = lambda a: a.reshape(-1, shp[-1])
        d, nm, nv = adamw(v2(W[n]), v2(G[n]), v2(M[n]), v2(V[n]), "adamw_" + n)
        delta[n], new_m[n], new_v[n] = d.reshape(shp), nm.reshape(shp), nv.reshape(shp)
    return (loss, dx[None], *[G[n] for n in WEIGHTS], *[delta[n] for n in WEIGHTS],
            *[new_m[n] for n in WEIGHTS], *[new_v[n] for n in WEIGHTS])
```

```python
import math

import numpy as np
import jax
import jax.numpy as jnp
from jax import lax
from jax.experimental import pallas as pl
from jax.experimental.pallas import tpu as pltpu

F32 = jnp.float32
MXU = jnp.bfloat16

N_DEV = 8
DEPTH = 4
D_MODEL = 1024
HEAD_DIM = 64
WINDOW = 128
SWA_Q_HEADS = 8
REL_BUCKETS = 32
REL_MAX_DIST = 128
MLA_QK_DIM = 96
ROPE_DIM = 32
ROPE_THETA = 10000.0
D_FF = 2816
EPS = 1e-6
NEG = -1e30
IN_COLS = 1956
IN_COLS_P = 2048
C_QA, C_KA, C_VA = 0, 512, 640
C_QF, C_KF, C_VF = 768, 1024, 1280
C_CQ, C_CKV, C_MISC = 1536, 1792, 1920
ROPE_LANE0 = 64
ADAM_LR, ADAM_B1, ADAM_B2, ADAM_EPS, ADAM_WD, ADAM_STEP = 0.001, 0.9, 0.999, 1e-08, 0.01, 10

VMEM_LIMIT = 56 * 1024 * 1024
PACK_COLS = 1024
PACK_ROW_ALIGN = 16


def _cparams(sem=None):
    return pltpu.CompilerParams(dimension_semantics=sem, vmem_limit_bytes=VMEM_LIMIT)


def _tile(n, pref):
    if n <= pref:
        return n
    t = pref - pref % 128
    while t >= 128:
        if n % t == 0:
            return t
        t -= 128
    return n


def _dot(a, b):
    return jnp.dot(a.astype(MXU), b.astype(MXU), preferred_element_type=F32)


def _dot_nt(a, b):
    return lax.dot_general(a.astype(MXU), b.astype(MXU), (((1,), (1,)), ((), ())),
                           preferred_element_type=F32)


def _rms_fwd(x, g):
    return x * lax.rsqrt(jnp.mean(x * x, axis=-1, keepdims=True) + EPS) * g


def _rms_bwd(dy, x, g, n=None):
    r = lax.rsqrt(jnp.mean(x * x, axis=-1, keepdims=True) + EPS)
    xh = x * r
    dg = jnp.sum(dy * xh, axis=0, keepdims=True)
    dxh = dy * g
    dx = r * (dxh - xh * jnp.mean(dxh * xh, axis=-1, keepdims=True))
    return dx, dg


def _acc_out(ref, val, first):
    @pl.when(first)
    def _():
        ref[...] = val

    @pl.when(jnp.logical_not(first))
    def _():
        ref[...] += val


def norm_matmul(x, g, w, name, lo_tiles=0):
    T, K = x.shape
    N = w.shape[1]
    tm, tn = _tile(T, 512), _tile(N, 512)

    def body(x_ref, g_ref, w_ref, o_ref, hT_ref, *rest):
        h_sc = rest[-1]
        j = pl.program_id(1)

        @pl.when(j == 0)
        def _():
            h = _rms_fwd(x_ref[...], g_ref[...])
            h_sc[...] = h.astype(MXU)
            hT_ref[...] = h.T.astype(MXU)

        r = jnp.dot(h_sc[...], w_ref[...], preferred_element_type=F32)
        o_ref[...] = r
        if lo_tiles:
            @pl.when(j < lo_tiles)
            def _():
                rest[0][...] = r.astype(MXU)

    out_specs = [pl.BlockSpec((tm, tn), lambda i, j: (i, j)), pl.BlockSpec((K, tm), lambda i, j: (0, i))]
    out_shape = [jax.ShapeDtypeStruct((T, N), F32), jax.ShapeDtypeStruct((K, T), MXU)]
    if lo_tiles:
        out_specs.append(pl.BlockSpec((tm, tn), lambda i, j: (i, jnp.minimum(j, lo_tiles - 1))))
        out_shape.append(jax.ShapeDtypeStruct((T, lo_tiles * tn), MXU))
    return pl.pallas_call(
        body, name=name, grid=(T // tm, N // tn),
        in_specs=[pl.BlockSpec((tm, K), lambda i, j: (i, 0)),
                  pl.BlockSpec((1, K), lambda i, j: (0, 0)),
                  pl.BlockSpec((K, tn), lambda i, j: (0, j))],
        out_specs=out_specs, out_shape=out_shape,
        scratch_shapes=[pltpu.VMEM((tm, K), MXU)],
        compiler_params=_cparams(("parallel", "arbitrary")),
    )(x, g, w)


def matmul_nn(a, b, name, out_dtype=F32):
    M, K = a.shape
    N = b.shape[1]
    tm, tn, tk = _tile(M, 512), _tile(N, 512), _tile(K, 1024)
    nk = K // tk

    def body(a_ref, b_ref, o_ref, acc):
        k = pl.program_id(2)
        part = _dot(a_ref[...], b_ref[...])
        _acc_out(acc, part, k == 0)

        @pl.when(k == nk - 1)
        def _():
            o_ref[...] = acc[...].astype(out_dtype)

    return pl.pallas_call(
        body, name=name, grid=(M // tm, N // tn, nk),
        in_specs=[pl.BlockSpec((tm, tk), lambda i, j, k: (i, k)),
                  pl.BlockSpec((tk, tn), lambda i, j, k: (k, j))],
        out_specs=pl.BlockSpec((tm, tn), lambda i, j, k: (i, j)),
        out_shape=jax.ShapeDtypeStruct((M, N), out_dtype),
        scratch_shapes=[pltpu.VMEM((tm, tn), F32)],
        compiler_params=_cparams(("parallel", "parallel", "arbitrary")),
    )(a, b)


def matmul_nt_normbwd(dy, w, x, g, dres, name):
    T, N = dy.shape
    K = w.shape[0]
    tm, tn = _tile(T, 512), _tile(N, 512)
    nj = N // tn

    def body(dy_ref, w_ref, x_ref, g_ref, dres_ref, dx_ref, dg_ref, acc):
        i, j = pl.program_id(0), pl.program_id(1)
        _acc_out(acc, _dot_nt(dy_ref[...], w_ref[...]), j == 0)

        @pl.when(j == nj - 1)
        def _():
            dx, dg = _rms_bwd(acc[...], x_ref[...], g_ref[...])
            dx_ref[...] = dres_ref[...] + dx
            _acc_out(dg_ref, dg, i == 0)

    return pl.pallas_call(
        body, name=name, grid=(T // tm, nj),
        in_specs=[pl.BlockSpec((tm, tn), lambda i, j: (i, j)),
                  pl.BlockSpec((K, tn), lambda i, j: (0, j)),
                  pl.BlockSpec((tm, K), lambda i, j: (i, 0)),
                  pl.BlockSpec((1, K), lambda i, j: (0, 0)),
                  pl.BlockSpec((tm, K), lambda i, j: (i, 0))],
        out_specs=[pl.BlockSpec((tm, K), lambda i, j: (i, 0)),
                   pl.BlockSpec((1, K), lambda i, j: (0, 0))],
        out_shape=[jax.ShapeDtypeStruct((T, K), F32), jax.ShapeDtypeStruct((1, K), F32)],
        scratch_shapes=[pltpu.VMEM((tm, K), F32)],
        compiler_params=_cparams(("arbitrary", "arbitrary")),
    )(dy, w, x, g, dres)


def loss_kernel(y, tgt):
    T, D = y.shape
    tm = _tile(T, 512)

    def body(y_ref, t_ref, dy_ref, acc_ref):
        e = y_ref[...] - t_ref[...]
        dy_ref[...] = e * (1.0 / D)
        _acc_out(acc_ref, jnp.sum(e * e, axis=0, keepdims=True), pl.program_id(0) == 0)

    return pl.pallas_call(
        body, name="loss", grid=(T // tm,),
        in_specs=[pl.BlockSpec((tm, D), lambda i: (i, 0)), pl.BlockSpec((tm, D), lambda i: (i, 0))],
        out_specs=[pl.BlockSpec((tm, D), lambda i: (i, 0)), pl.BlockSpec((1, D), lambda i: (0, 0))],
        out_shape=[jax.ShapeDtypeStruct((T, D), F32), jax.ShapeDtypeStruct((1, D), F32)],
        compiler_params=_cparams(("arbitrary",)),
    )(y, tgt)


def _rope_partner(x):
    lane = lax.broadcasted_iota(jnp.int32, (1, 128), 1)
    return jnp.where(lane < ROPE_LANE0 + ROPE_DIM // 2, pltpu.roll(x, 128 - ROPE_DIM // 2, 1),
                     pltpu.roll(x, ROPE_DIM // 2, 1))


def _rope_apply(x, cos, sin_signed):
    return x * cos + _rope_partner(x) * sin_signed


def _rope_apply_bwd(dy, cos, sin_signed):
    lane = lax.broadcasted_iota(jnp.int32, (1, 128), 1)
    rotary = (lane >= ROPE_LANE0) & (lane < ROPE_LANE0 + ROPE_DIM)
    return dy * cos + jnp.where(rotary, _rope_partner(dy * sin_signed), 0.0)


def rope_tables(T):
    pos = jnp.arange(T, dtype=F32)
    inv_freq = ROPE_THETA ** (-(jnp.arange(ROPE_DIM // 2, dtype=F32) * 2.0 / ROPE_DIM))
    ang = pos[:, None] * inv_freq[None, :]
    cos, sin = jnp.cos(ang), jnp.sin(ang)
    z = jnp.zeros((T, ROPE_LANE0), F32)
    z2 = jnp.zeros((T, 128 - ROPE_LANE0 - ROPE_DIM), F32)
    cosr = jnp.concatenate([z, cos, cos, z2], axis=1)
    sinr = jnp.concatenate([z, -sin, sin, z2], axis=1)
    return cosr, sinr


def mla_prep(proj, gq, gkv, wuq, wukv, cosr, sinr):
    T = proj.shape[0]
    tm = _tile(T, 512)

    def body(cq_ref, ckv_ref, misc_ref, gq_ref, gkv_ref, wuq_ref, wukv_ref, cos_ref, sin_ref,
             q_ref, k_ref, v_ref, cqT_ref, ckvT_ref):
        lane = lax.broadcasted_iota(jnp.int32, (1, 128), 1)
        cosr_, sinr_ = cos_ref[...], sin_ref[...]
        cosq = cosr_ + jnp.where(lane < ROPE_LANE0, 1.0, 0.0)
        cqn = _rms_fwd(cq_ref[...], gq_ref[...])
        cqT_ref[...] = cqn.T.astype(MXU)
        qm = _dot(cqn, wuq_ref[...])
        q_ref[...] = jnp.concatenate(
            [_rope_apply(qm[:, 128 * h:128 * (h + 1)], cosq, sinr_) for h in range(4)], axis=1).astype(MXU)
        ckvn = _rms_fwd(ckv_ref[...], gkv_ref[...])
        ckvT_ref[...] = ckvn.T.astype(MXU)
        kv = _dot(ckvn, wukv_ref[...])
        kr = _rope_apply(misc_ref[...], cosr_, sinr_)
        k_ref[...] = jnp.concatenate(
            [kv[:, 128 * h:128 * (h + 1)] + kr for h in range(4)], axis=1).astype(MXU)
        v_ref[...] = kv[:, 512:768].astype(MXU)

    row = lambda i: (i, 0)
    const = lambda i: (0, 0)
    return pl.pallas_call(
        body, name="mla_prep", grid=(T // tm,),
        in_specs=[pl.BlockSpec((tm, 256), lambda i: (i, C_CQ // 256)),
                  pl.BlockSpec((tm, 128), lambda i: (i, C_CKV // 128)),
                  pl.BlockSpec((tm, 128), lambda i: (i, C_MISC // 128)),
                  pl.BlockSpec((1, 256), const), pl.BlockSpec((1, 128), const),
                  pl.BlockSpec((256, 512), const), pl.BlockSpec((128, 768), const),
                  pl.BlockSpec((tm, 128), row), pl.BlockSpec((tm, 128), row)],
        out_specs=[pl.BlockSpec((tm, 512), row), pl.BlockSpec((tm, 512), row), pl.BlockSpec((tm, 256), row),
                   pl.BlockSpec((256, tm), lambda i: (0, i)), pl.BlockSpec((128, tm), lambda i: (0, i))],
        out_shape=[jax.ShapeDtypeStruct((T, 512), MXU), jax.ShapeDtypeStruct((T, 512), MXU),
                   jax.ShapeDtypeStruct((T, 256), MXU),
                   jax.ShapeDtypeStruct((256, T), MXU), jax.ShapeDtypeStruct((128, T), MXU)],
        compiler_params=_cparams(("parallel",)),
    )(proj, proj, proj, gq, gkv, wuq, wukv, cosr, sinr)


def mla_prep_bwd(dq, dk, dv, proj, gq, gkv, wuq, wukv, cosr, sinr):
    T = proj.shape[0]
    tm = _tile(T, 512)

    def body(dq_ref, dk_ref, dv_ref, cq_ref, ckv_ref, gq_ref, gkv_ref, wuq_ref, wukv_ref, cos_ref, sin_ref,
             dqm_ref, dkv_ref, dcq_ref, dckv_ref, dmisc_ref, dgq_ref, dgkv_ref):
        first = pl.program_id(0) == 0
        lane = lax.broadcasted_iota(jnp.int32, (1, 128), 1)
        cosr_, sinr_ = cos_ref[...], sin_ref[...]
        cosq = cosr_ + jnp.where(lane < ROPE_LANE0, 1.0, 0.0)
        dqv = dq_ref[...]
        dqm = jnp.concatenate(
            [_rope_apply_bwd(dqv[:, 128 * h:128 * (h + 1)], cosq, sinr_) for h in range(4)], axis=1)
        dqm_ref[...] = dqm.astype(MXU)
        dcq, dgq = _rms_bwd(_dot_nt(dqm, wuq_ref[...]), cq_ref[...], gq_ref[...])
        dcq_ref[...] = dcq
        _acc_out(dgq_ref, dgq, first)
        dkv_ = dk_ref[...]
        heads = [dkv_[:, 128 * h:128 * (h + 1)] for h in range(4)]
        dkr = heads[0] + heads[1] + heads[2] + heads[3]
        dmisc_ref[...] = _rope_apply_bwd(dkr, cosr_, sinr_)
        dkvm = jnp.concatenate([jnp.where(lane < ROPE_LANE0, hd, 0.0) for hd in heads] + [dv_ref[...]], axis=1)
        dkv_ref[...] = dkvm.astype(MXU)
        dckv, dgkv = _rms_bwd(_dot_nt(dkvm, wukv_ref[...]), ckv_ref[...], gkv_ref[...])
        dckv_ref[...] = dckv
        _acc_out(dgkv_ref, dgkv, first)

    row = lambda i: (i, 0)
    const = lambda i: (0, 0)
    return pl.pallas_call(
        body, name="mla_prep_bwd", grid=(T // tm,),
        in_specs=[pl.BlockSpec((tm, 512), row), pl.BlockSpec((tm, 512), row), pl.BlockSpec((tm, 256), row),
                  pl.BlockSpec((tm, 256), lambda i: (i, C_CQ // 256)),
                  pl.BlockSpec((tm, 128), lambda i: (i, C_CKV // 128)),
                  pl.BlockSpec((1, 256), const), pl.BlockSpec((1, 128), const),
                  pl.BlockSpec((256, 512), const), pl.BlockSpec((128, 768), const),
                  pl.BlockSpec((tm, 128), row), pl.BlockSpec((tm, 128), row)],
        out_specs=[pl.BlockSpec((tm, 512), row), pl.BlockSpec((tm, 768), row), pl.BlockSpec((tm, 256), row),
                   pl.BlockSpec((tm, 128), row), pl.BlockSpec((tm, 128), row),
                   pl.BlockSpec((1, 256), const), pl.BlockSpec((1, 128), const)],
        out_shape=[jax.ShapeDtypeStruct((T, 512), MXU), jax.ShapeDtypeStruct((T, 768), MXU),
                   jax.ShapeDtypeStruct((T, 256), F32), jax.ShapeDtypeStruct((T, 128), F32),
                   jax.ShapeDtypeStruct((T, 128), F32),
                   jax.ShapeDtypeStruct((1, 256), F32), jax.ShapeDtypeStruct((1, 128), F32)],
        compiler_params=_cparams(("arbitrary",)),
    )(dq, dk, dv, proj, proj, gq, gkv, wuq, wukv, cosr, sinr)


def _split3(x):
    hi = x.astype(MXU)
    r1 = x - hi.astype(F32)
    mid = r1.astype(MXU)
    lo = (r1 - mid.astype(F32)).astype(MXU)
    return hi, mid, lo


def _tri_matmul(tri, x):
    hi, mid, lo = _split3(x)
    d = lambda p: jnp.dot(tri, p, preferred_element_type=F32)
    return d(hi) + d(mid) + d(lo)


def _log_sigmoid(z):
    return jnp.minimum(z, 0.0) - jnp.log(1.0 + jnp.exp(-jnp.abs(z)))


def fox_gate(proj, fbias):
    T = proj.shape[0]
    tb = _tile(T, 512)

    def body(misc_ref, b_ref, fc_ref, fr_ref, carry):
        @pl.when(pl.program_id(0) == 0)
        def _():
            carry[...] = jnp.zeros_like(carry)

        lane = lax.broadcasted_iota(jnp.int32, (1, 128), 1)
        lf = jnp.where(lane < 4, _log_sigmoid(misc_ref[...] + b_ref[...]), 0.0)
        r = lax.broadcasted_iota(jnp.int32, (tb, tb), 0)
        c = lax.broadcasted_iota(jnp.int32, (tb, tb), 1)
        tri = jnp.where(r >= c, 1.0, 0.0).astype(MXU)
        F = _tri_matmul(tri, lf) + carry[...]
        carry[...] = carry[...] + jnp.sum(lf, axis=0, keepdims=True)
        fc_ref[0] = F
        fc_ref[1] = pltpu.roll(F, 126, 1)
        ft = F.T[0:8, :]
        fr_ref[0] = ft
        fr_ref[1] = pltpu.roll(ft, 6, 0)

    return pl.pallas_call(
        body, name="fox_gate", grid=(T // tb,),
        in_specs=[pl.BlockSpec((tb, 128), lambda i: (i, C_MISC // 128)), pl.BlockSpec((1, 128), lambda i: (0, 0))],
        out_specs=[pl.BlockSpec((2, tb, 128), lambda i: (0, i, 0)), pl.BlockSpec((2, 8, tb), lambda i: (0, 0, i))],
        out_shape=[jax.ShapeDtypeStruct((2, T, 128), F32), jax.ShapeDtypeStruct((2, 8, T), F32)],
        scratch_shapes=[pltpu.VMEM((1, 128), F32)],
        compiler_params=_cparams(("arbitrary",)),
    )(proj, fbias)


def fox_gate_bwd(dFq, dFk, proj, fbias):
    T = proj.shape[0]
    tb = _tile(T, 512)
    nb = T // tb

    def body(dq_ref, dk_ref, misc_ref, b_ref, dm_ref, db_ref, carry):
        first = pl.program_id(0) == 0

        @pl.when(first)
        def _():
            carry[...] = jnp.zeros_like(carry)

        lane = lax.broadcasted_iota(jnp.int32, (1, 128), 1)
        dF = jnp.where(lane < 4, (dq_ref[0] + dk_ref[0]) + pltpu.roll(dq_ref[1] + dk_ref[1], 2, 1), 0.0)
        r = lax.broadcasted_iota(jnp.int32, (tb, tb), 0)
        c = lax.broadcasted_iota(jnp.int32, (tb, tb), 1)
        tri = jnp.where(r <= c, 1.0, 0.0).astype(MXU)
        dlf = _tri_matmul(tri, dF) + carry[...]
        carry[...] = carry[...] + jnp.sum(dF, axis=0, keepdims=True)
        z = misc_ref[...] + b_ref[...]
        dz = jnp.where(lane < 4, dlf * (1.0 / (1.0 + jnp.exp(z))), 0.0)
        dm_ref[...] = dz
        _acc_out(db_ref, jnp.sum(dz, axis=0, keepdims=True), first)

    return pl.pallas_call(
        body, name="fox_gate_bwd", grid=(nb,),
        in_specs=[pl.BlockSpec((2, tb, 128), lambda i: (0, nb - 1 - i, 0)),
                  pl.BlockSpec((2, tb, 128), lambda i: (0, nb - 1 - i, 0)),
                  pl.BlockSpec((tb, 128), lambda i: (nb - 1 - i, C_MISC // 128)),
                  pl.BlockSpec((1, 128), lambda i: (0, 0))],
        out_specs=[pl.BlockSpec((tb, 128), lambda i: (nb - 1 - i, 0)), pl.BlockSpec((1, 128), lambda i: (0, 0))],
        out_shape=[jax.ShapeDtypeStruct((T, 128), F32), jax.ShapeDtypeStruct((1, 128), F32)],
        scratch_shapes=[pltpu.VMEM((1, 128), F32)],
        compiler_params=_cparams(("arbitrary",)),
    )(dFq, dFk, proj, fbias)


def _row_stat_tile(a, b, n):
    at = jnp.broadcast_to(a, (n, 128)).T[0:8, :]
    bt = jnp.broadcast_to(b, (n, 128)).T[0:8, :]
    sub = lax.broadcasted_iota(jnp.int32, (8, 1), 0)
    return jnp.where(sub == 0, at, jnp.where(sub == 1, bt, 0.0))


def _col_stat_tile(a, b):
    lane = lax.broadcasted_iota(jnp.int32, (1, 128), 1)
    return jnp.where(lane == 0, a, jnp.where(lane == 1, b, 0.0))


def _lane_pick(x, h):
    lane = lax.broadcasted_iota(jnp.int32, (1, 128), 1)
    return jnp.sum(jnp.where(lane == h, x, 0.0), axis=1, keepdims=True)


def _half_mask(h):
    lane = lax.broadcasted_iota(jnp.int32, (1, 128), 1)
    return (lane // HEAD_DIM) == h


def flash_fwd(q, k, v, fcol, frow, *, qblk, kblk, vblk, nq, scale, name):
    T = q.shape[0]
    tq = tk = _tile(T, 256)
    wq = 128 * nq
    has_f = fcol is not None

    def body(*refs):
        if has_f:
            q_ref, k_ref, v_ref, fc_ref, fr_ref, o_ref, lc_ref, lr_ref, m_sc, acc_sc = refs
        else:
            q_ref, k_ref, v_ref, o_ref, lc_ref, lr_ref, m_sc, acc_sc = refs
        i = pl.program_id(1)
        diag = lax.broadcasted_iota(jnp.int32, (1, tk), 1) <= lax.broadcasted_iota(jnp.int32, (tq, 1), 0)
        qb = q_ref[...]
        if nq == 1:
            qhs = [jnp.where(_half_mask(h), qb, 0).astype(MXU) for h in range(2)]
        else:
            qhs = [qb[:, 128 * h:128 * (h + 1)].astype(MXU) for h in range(2)]
        fqs = [_lane_pick(fc_ref[0], h) for h in range(2)] if has_f else None
        for h in range(2):
            m_sc[h] = jnp.full((tq, 1), NEG, F32)
            acc_sc[h] = jnp.zeros((tq, 128), F32)

        def make_step(masked):
            def step(j, carry):
                off = pl.multiple_of(j * tk, tk)
                ks = k_ref[pl.ds(off, tk), :]
                vs = v_ref[pl.ds(off, tk), :]
                for h in range(2):
                    kh = ks if nq == 1 else ks[:, 128 * h:128 * (h + 1)]
                    s = _dot_nt(qhs[h], kh) * scale
                    if has_f:
                        s = s + (fqs[h] - fr_ref[0, h:h + 1, pl.ds(off, tk)])
                    if masked:
                        s = jnp.where(diag, s, NEG)
                    m_prev = m_sc[h]
                    m_new = jnp.maximum(m_prev, jnp.max(s, axis=1, keepdims=True))
                    alpha = jnp.exp(m_prev - m_new)
                    pm = jnp.exp(s - m_new)
                    vh = jnp.where(_half_mask(h), vs, jnp.ones_like(vs))
                    acc_sc[h] = alpha * acc_sc[h] + _dot(pm, vh)
                    m_sc[h] = m_new
                return carry
            return step

        lax.fori_loop(0, i, make_step(False), 0)
        make_step(True)(i, 0)
        outs, lses = [], []
        for h in range(2):
            acc = acc_sc[h]
            outs.append(acc / pltpu.roll(acc, HEAD_DIM, 1))
            l = jnp.sum(jnp.where(lax.broadcasted_iota(jnp.int32, (1, 128), 1) == HEAD_DIM * (1 - h), acc, 0.0),
                        axis=1, keepdims=True)
            lses.append(m_sc[h] + jnp.log(l))
        o_ref[...] = jnp.where(_half_mask(0), outs[0], outs[1])
        lc_ref[0] = _col_stat_tile(lses[0], lses[1])
        lr_ref[0] = _row_stat_tile(lses[0], lses[1], tq)

    in_specs = [pl.BlockSpec((tq, wq), lambda p, i: (i, qblk + p)),
                pl.BlockSpec((T, wq), lambda p, i: (0, kblk + p)),
                pl.BlockSpec((T, 128), lambda p, i: (0, vblk + p))]
    args = [q, k, v]
    if has_f:
        in_specs += [pl.BlockSpec((1, tq, 128), lambda p, i: (p, i, 0)),
                     pl.BlockSpec((1, 8, T), lambda p, i: (p, 0, 0))]
        args += [fcol, frow]
    return pl.pallas_call(
        body, name=name, grid=(2, T // tq), in_specs=in_specs,
        out_specs=[pl.BlockSpec((tq, 128), lambda p, i: (i, p)),
                   pl.BlockSpec((1, tq, 128), lambda p, i: (p, i, 0)),
                   pl.BlockSpec((1, 8, tq), lambda p, i: (p, 0, i))],
        out_shape=[jax.ShapeDtypeStruct((T, 256), F32), jax.ShapeDtypeStruct((2, T, 128), F32),
                   jax.ShapeDtypeStruct((2, 8, T), F32)],
        scratch_shapes=[pltpu.VMEM((2, tq, 1), F32), pltpu.VMEM((2, tq, 128), F32)],
        compiler_params=_cparams(("parallel", "parallel")),
    )(*args)


def flash_dq(q, k, v, do, o, lcol, fcol, frow, *, qblk, kblk, vblk, doblk, nq, scale, name):
    T = q.shape[0]
    tq = tk = _tile(T, 256)
    wq = 128 * nq
    has_f = fcol is not None

    def body(*refs):
        if has_f:
            q_ref, k_ref, v_ref, do_ref, o_ref, lc_ref, fc_ref, fr_ref, dq_ref, dr_ref, df_ref, acc_sc, df_sc = refs
        else:
            q_ref, k_ref, v_ref, do_ref, o_ref, lc_ref, dq_ref, dr_ref, acc_sc = refs
        i = pl.program_id(1)
        diag = lax.broadcasted_iota(jnp.int32, (1, tk), 1) <= lax.broadcasted_iota(jnp.int32, (tq, 1), 0)
        qb = q_ref[...]
        dob = do_ref[...]
        prod = dob * o_ref[...]
        hms = [_half_mask(h) for h in range(2)]
        if nq == 1:
            qhs = [jnp.where(hms[h], qb, 0).astype(MXU) for h in range(2)]
        else:
            qhs = [qb[:, 128 * h:128 * (h + 1)].astype(MXU) for h in range(2)]
        dohs = [jnp.where(hms[h], dob, 0.0).astype(MXU) for h in range(2)]
        Ds = [jnp.sum(jnp.where(hms[h], prod, 0.0), axis=1, keepdims=True) for h in range(2)]
        lses = [_lane_pick(lc_ref[0], h) for h in range(2)]
        fqs = [_lane_pick(fc_ref[0], h) for h in range(2)] if has_f else None
        acc_sc[...] = jnp.zeros_like(acc_sc)
        if has_f:
            df_sc[...] = jnp.zeros_like(df_sc)

        def make_step(masked):
            def step(j, carry):
                off = pl.multiple_of(j * tk, tk)
                ks = k_ref[pl.ds(off, tk), :]
                vs = v_ref[pl.ds(off, tk), :]
                for h in range(2):
                    kh = ks if nq == 1 else ks[:, 128 * h:128 * (h + 1)]
                    s = _dot_nt(qhs[h], kh) * scale
                    if has_f:
                        s = s + (fqs[h] - fr_ref[0, h:h + 1, pl.ds(off, tk)])
                    pmat = jnp.exp(s - lses[h])
                    if masked:
                        pmat = jnp.where(diag, pmat, 0.0)
                    ds = pmat * (_dot_nt(dohs[h], vs) - Ds[h])
                    kk = jnp.where(hms[h], kh, 0) if nq == 1 else kh
                    acc_sc[h if nq == 2 else 0] += _dot(ds, kk)
                    if has_f:
                        part = ds[:, 0:128]
                        for c in range(1, tk // 128):
                            part = part + ds[:, 128 * c:128 * (c + 1)]
                        df_sc[h] += part
                return carry
            return step

        lax.fori_loop(0, i, make_step(False), 0)
        make_step(True)(i, 0)
        if nq == 1:
            dq_ref[...] = acc_sc[0] * scale
        else:
            dq_ref[...] = jnp.concatenate([acc_sc[0], acc_sc[1]], axis=1) * scale
        dr_ref[0] = _row_stat_tile(Ds[0], Ds[1], tq)
        if has_f:
            df_ref[0] = _col_stat_tile(jnp.sum(df_sc[0], axis=1, keepdims=True),
                                       jnp.sum(df_sc[1], axis=1, keepdims=True))

    in_specs = [pl.BlockSpec((tq, wq), lambda p, i: (i, qblk + p)),
                pl.BlockSpec((T, wq), lambda p, i: (0, kblk + p)),
                pl.BlockSpec((T, 128), lambda p, i: (0, vblk + p)),
                pl.BlockSpec((tq, 128), lambda p, i: (i, doblk + p)),
                pl.BlockSpec((tq, 128), lambda p, i: (i, p)),
                pl.BlockSpec((1, tq, 128), lambda p, i: (p, i, 0))]
    args = [q, k, v, do, o, lcol]
    out_specs = [pl.BlockSpec((tq, wq), lambda p, i: (i, p)), pl.BlockSpec((1, 8, tq), lambda p, i: (p, 0, i))]
    out_shape = [jax.ShapeDtypeStruct((T, 2 * wq), F32), jax.ShapeDtypeStruct((2, 8, T), F32)]
    scratch = [pltpu.VMEM((nq, tq, 128), F32)]
    if has_f:
        in_specs += [pl.BlockSpec((1, tq, 128), lambda p, i: (p, i, 0)),
                     pl.BlockSpec((1, 8, T), lambda p, i: (p, 0, 0))]
        args += [fcol, frow]
        out_specs.append(pl.BlockSpec((1, tq, 128), lambda p, i: (p, i, 0)))
        out_shape.append(jax.ShapeDtypeStruct((2, T, 128), F32))
        scratch.append(pltpu.VMEM((2, tq, 128), F32))
    return pl.pallas_call(
        body, name=name, grid=(2, T // tq), in_specs=in_specs, out_specs=out_specs, out_shape=out_shape,
        scratch_shapes=scratch, compiler_params=_cparams(("parallel", "parallel")),
    )(*args)


def flash_dkv(q, k, v, do, lrow, drow, fcol, frow, *, qblk, kblk, vblk, doblk, nq, scale, name):
    T = q.shape[0]
    tq = tk = _tile(T, 256)
    wq = 128 * nq
    nqb = T // tq
    has_f = fcol is not None

    def body(*refs):
        if has_f:
            (q_ref, k_ref, v_ref, do_ref, lr_ref, dr_ref, fc_ref, fr_ref,
             dk_ref, dv_ref, df_ref, dk_sc, dv_sc, df_sc) = refs
        else:
            q_ref, k_ref, v_ref, do_ref, lr_ref, dr_ref, dk_ref, dv_ref, dk_sc, dv_sc = refs
        j = pl.program_id(1)
        diag = lax.broadcasted_iota(jnp.int32, (tk, 1), 0) <= lax.broadcasted_iota(jnp.int32, (1, tq), 1)
        kb = k_ref[...]
        vb = v_ref[...]
        hms = [_half_mask(h) for h in range(2)]
        if nq == 1:
            khs = [jnp.where(hms[h], kb, 0).astype(MXU) for h in range(2)]
        else:
            khs = [kb[:, 128 * h:128 * (h + 1)].astype(MXU) for h in range(2)]
        vhs = [jnp.where(hms[h], vb, 0).astype(MXU) for h in range(2)]
        fks = [_lane_pick(fc_ref[0], h) for h in range(2)] if has_f else None
        dv_sc[...] = jnp.zeros_like(dv_sc)
        dk_sc[...] = jnp.zeros_like(dk_sc)
        if has_f:
            df_sc[...] = jnp.zeros_like(df_sc)

        def make_step(masked):
            def step(i, carry):
                off = pl.multiple_of(i * tq, tq)
                qs = q_ref[pl.ds(off, tq), :]
                dos = do_ref[pl.ds(off, tq), :]
                for h in range(2):
                    qh = qs if nq == 1 else qs[:, 128 * h:128 * (h + 1)]
                    sT = _dot_nt(khs[h], qh) * scale
                    if has_f:
                        sT = sT + (fr_ref[0, h:h + 1, pl.ds(off, tq)] - fks[h])
                    pT = jnp.exp(sT - lr_ref[0, h:h + 1, pl.ds(off, tq)])
                    if masked:
                        pT = jnp.where(diag, pT, 0.0)
                    dsT = pT * (_dot_nt(vhs[h], dos) - dr_ref[0, h:h + 1, pl.ds(off, tq)])
                    dv_sc[...] += _dot(pT, jnp.where(hms[h], dos, 0))
                    qq = jnp.where(hms[h], qs, 0) if nq == 1 else qh
                    dk_sc[h if nq == 2 else 0] += _dot(dsT, qq)
                    if has_f:
                        part = dsT[:, 0:128]
                        for c in range(1, tq // 128):
                            part = part + dsT[:, 128 * c:128 * (c + 1)]
                        df_sc[h] += part
                return carry
            return step

        make_step(True)(j, 0)
        lax.fori_loop(j + 1, nqb, make_step(False), 0)
        if nq == 1:
            dk_ref[...] = dk_sc[0] * scale
        else:
            dk_ref[...] = jnp.concatenate([dk_sc[0], dk_sc[1]], axis=1) * scale
        dv_ref[...] = dv_sc[...]
        if has_f:
            df_ref[0] = _col_stat_tile(-jnp.sum(df_sc[0], axis=1, keepdims=True),
                                       -jnp.sum(df_sc[1], axis=1, keepdims=True))

    in_specs = [pl.BlockSpec((T, wq), lambda p, j: (0, qblk + p)),
                pl.BlockSpec((tk, wq), lambda p, j: (j, kblk + p)),
                pl.BlockSpec((tk, 128), lambda p, j: (j, vblk + p)),
                pl.BlockSpec((T, 128), lambda p, j: (0, doblk + p)),
                pl.BlockSpec((1, 8, T), lambda p, j: (p, 0, 0)),
                pl.BlockSpec((1, 8, T), lambda p, j: (p, 0, 0))]
    args = [q, k, v, do, lrow, drow]
    out_specs = [pl.BlockSpec((tk, wq), lambda p, j: (j, p)), pl.BlockSpec((tk, 128), lambda p, j: (j, p))]
    out_shape = [jax.ShapeDtypeStruct((T, 2 * wq), F32), jax.ShapeDtypeStruct((T, 256), F32)]
    scratch = [pltpu.VMEM((nq, tk, 128), F32), pltpu.VMEM((tk, 128), F32)]
    if has_f:
        in_specs += [pl.BlockSpec((1, tk, 128), lambda p, j: (p, j, 0)),
                     pl.BlockSpec((1, 8, T), lambda p, j: (p, 0, 0))]
        args += [fcol, frow]
        out_specs.append(pl.BlockSpec((1, tk, 128), lambda p, j: (p, j, 0)))
        out_shape.append(jax.ShapeDtypeStruct((2, T, 128), F32))
        scratch.append(pltpu.VMEM((2, tk, 128), F32))
    return pl.pallas_call(
        body, name=name, grid=(2, T // tk), in_specs=in_specs, out_specs=out_specs, out_shape=out_shape,
        scratch_shapes=scratch, compiler_params=_cparams(("parallel", "parallel")),
    )(*args)


def _swa_align(pair, e, h):
    sel = jnp.where(_half_mask(e), pair, 0.0)
    if e == h:
        return sel
    return pltpu.roll(sel, HEAD_DIM, 1)


def _swa_mask(n):
    W = WINDOW
    qi = lax.broadcasted_iota(jnp.int32, (W, 2 * W), 0) + W
    kj = lax.broadcasted_iota(jnp.int32, (W, 2 * W), 1)
    dist = qi - kj
    return (dist >= 0) & (dist < W) & ((n > 0) | (kj >= W))


def swa_fwd(proj, bias, sinks):
    T = proj.shape[0]
    W = WINDOW
    nb = T // W
    scale = HEAD_DIM ** -0.5

    def body(sink_ref, q_ref, kp_ref, kc_ref, vp_ref, vc_ref, b_ref, o_ref, l_ref):
        n = pl.program_id(0)
        mask = _swa_mask(n)
        kband = jnp.concatenate([kp_ref[...], kc_ref[...]], axis=0).astype(MXU)
        vband = jnp.concatenate([vp_ref[...], vc_ref[...]], axis=0).astype(MXU)
        lane = lax.broadcasted_iota(jnp.int32, (1, 128), 1)
        lse_tile = jnp.zeros((W, 128), F32)
        pairs = []
        for h in range(2):
            full = []
            for g in range(4):
                hq = 4 * h + g
                qa = _swa_align(q_ref[:, 128 * (hq // 2):128 * (hq // 2 + 1)], hq % 2, h)
                s = _dot_nt(qa, kband) * scale + b_ref[hq]
                s = jnp.where(mask, s, NEG)
                sink = sink_ref[hq]
                m = jnp.maximum(jnp.max(s, axis=1, keepdims=True), sink)
                e = jnp.exp(s - m)
                l = jnp.sum(e, axis=1, keepdims=True) + jnp.exp(sink - m)
                r = jnp.where(_half_mask(h), _dot(e, vband), 0.0) / l
                full.append(r + pltpu.roll(r, HEAD_DIM, 1))
                lse_tile = jnp.where(lane == hq, m + jnp.log(l), lse_tile)
            pairs.append(jnp.where(_half_mask(0), full[0], full[1]))
            pairs.append(jnp.where(_half_mask(0), full[2], full[3]))
        o_ref[...] = jnp.concatenate(pairs, axis=1)
        l_ref[...] = lse_tile

    prev = lambda n: (jnp.maximum(n - 1, 0), C_KA // 128)
    cur = lambda n: (n, C_KA // 128)
    prev_v = lambda n: (jnp.maximum(n - 1, 0), C_VA // 128)
    cur_v = lambda n: (n, C_VA // 128)
    return pl.pallas_call(
        body, name="swa_fwd", grid=(nb,),
        in_specs=[pl.BlockSpec(memory_space=pltpu.SMEM),
                  pl.BlockSpec((W, 512), lambda n: (n, 0)),
                  pl.BlockSpec((W, 128), prev), pl.BlockSpec((W, 128), cur),
                  pl.BlockSpec((W, 128), prev_v), pl.BlockSpec((W, 128), cur_v),
                  pl.BlockSpec((8, W, 2 * W), lambda n: (0, 0, 0))],
        out_specs=[pl.BlockSpec((W, 512), lambda n: (n, 0)), pl.BlockSpec((W, 128), lambda n: (n, 0))],
        out_shape=[jax.ShapeDtypeStruct((T, 512), F32), jax.ShapeDtypeStruct((T, 128), F32)],
        compiler_params=_cparams(("parallel",)),
    )(sinks, proj, proj, proj, proj, proj, bias)


def swa_bwd(proj, bias, sinks, do, o, lse):
    T = proj.shape[0]
    W = WINDOW
    nb = T // W
    scale = HEAD_DIM ** -0.5

    def body(sink_ref, q_ref, kp_ref, kc_ref, vp_ref, vc_ref, b_ref, do_ref, o_ref, l_ref,
             dq_ref, dk_ref, dv_ref, db_ref, dsk_ref, ck, cv):
        n = pl.program_id(0)

        @pl.when(n == 0)
        def _():
            ck[...] = jnp.zeros_like(ck)
            cv[...] = jnp.zeros_like(cv)
            db_ref[...] = jnp.zeros_like(db_ref)
            dsk_ref[...] = jnp.zeros_like(dsk_ref)

        @pl.when(n < nb)
        def _():
            mask = _swa_mask(n)
            kb32 = jnp.concatenate([kp_ref[...], kc_ref[...]], axis=0)
            vb32 = jnp.concatenate([vp_ref[...], vc_ref[...]], axis=0)
            kband = kb32.astype(MXU)
            sub = lax.broadcasted_iota(jnp.int32, (8, 1), 0)
            dk_band = jnp.zeros((2 * W, 128), F32)
            dv_band = jnp.zeros((2 * W, 128), F32)
            dsk = jnp.zeros((8, 128), F32)
            dq_pairs = []
            for h in range(2):
                hm = _half_mask(h)
                km = jnp.where(hm, kb32, 0.0).astype(MXU)
                vm = jnp.where(hm, vb32, 0.0).astype(MXU)
                full = []
                for g in range(4):
                    hq = 4 * h + g
                    pb = slice(128 * (hq // 2), 128 * (hq // 2 + 1))
                    e = hq % 2
                    qa = _swa_align(q_ref[:, pb], e, h)
                    dop = do_ref[:, pb]
                    doa = _swa_align(dop, e, h)
                    D = jnp.sum(jnp.where(_half_mask(e), dop * o_ref[:, pb], 0.0), axis=1, keepdims=True)
                    lse_h = _lane_pick(l_ref[...], hq)
                    s = _dot_nt(qa, kband) * scale + b_ref[hq]
                    p = jnp.where(mask, jnp.exp(s - lse_h), 0.0)
                    psink = jnp.exp(sink_ref[hq] - lse_h)
                    dsk = dsk + jnp.where(sub == hq, -jnp.sum(psink * D, axis=0, keepdims=True), 0.0)
                    dp = _dot_nt(doa, vm)
                    ds = p * (dp - D)
                    db_ref[hq] += ds
                    dq = _dot(ds, km) * scale
                    full.append(dq + pltpu.roll(dq, HEAD_DIM, 1))
                    dk_band = dk_band + _dot(ds.T, qa) * scale
                    dv_band = dv_band + _dot(p.T, doa)
                dq_pairs.append(jnp.where(_half_mask(0), full[0], full[1]))
                dq_pairs.append(jnp.where(_half_mask(0), full[2], full[3]))
            dq_ref[...] = jnp.concatenate(dq_pairs, axis=1)
            dsk_ref[...] += dsk
            dk_ref[...] = ck[...] + dk_band[0:W]
            dv_ref[...] = cv[...] + dv_band[0:W]
            ck[...] = dk_band[W:2 * W]
            cv[...] = dv_band[W:2 * W]

        @pl.when(n == nb)
        def _():
            dk_ref[...] = ck[...]
            dv_ref[...] = cv[...]

    cl = lambda n: jnp.minimum(n, nb - 1)
    pv = lambda n: jnp.maximum(jnp.minimum(n, nb - 1) - 1, 0)
    return pl.pallas_call(
        body, name="swa_bwd", grid=(nb + 1,),
        in_specs=[pl.BlockSpec(memory_space=pltpu.SMEM),
                  pl.BlockSpec((W, 512), lambda n: (cl(n), 0)),
                  pl.BlockSpec((W, 128), lambda n: (pv(n), C_KA // 128)),
                  pl.BlockSpec((W, 128), lambda n: (cl(n), C_KA // 128)),
                  pl.BlockSpec((W, 128), lambda n: (pv(n), C_VA // 128)),
                  pl.BlockSpec((W, 128), lambda n: (cl(n), C_VA // 128)),
                  pl.BlockSpec((8, W, 2 * W), lambda n: (0, 0, 0)),
                  pl.BlockSpec((W, 512), lambda n: (cl(n), 0)),
                  pl.BlockSpec((W, 512), lambda n: (cl(n), 0)),
                  pl.BlockSpec((W, 128), lambda n: (cl(n), 0))],
        out_specs=[pl.BlockSpec((W, 512), lambda n: (cl(n), 0)),
                   pl.BlockSpec((W, 128), lambda n: (jnp.maximum(n - 1, 0), 0)),
                   pl.BlockSpec((W, 128), lambda n: (jnp.maximum(n - 1, 0), 0)),
                   pl.BlockSpec((8, W, 2 * W), lambda n: (0, 0, 0)),
                   pl.BlockSpec((8, 128), lambda n: (0, 0))],
        out_shape=[jax.ShapeDtypeStruct((T, 512), F32), jax.ShapeDtypeStruct((T, 128), F32),
                   jax.ShapeDtypeStruct((T, 128), F32), jax.ShapeDtypeStruct((8, W, 2 * W), F32),
                   jax.ShapeDtypeStruct((8, 128), F32)],
        scratch_shapes=[pltpu.VMEM((W, 128), F32), pltpu.VMEM((W, 128), F32)],
        compiler_params=_cparams(("arbitrary",)),
    )(sinks, proj, proj, proj, proj, proj, bias, do, o, lse)


def swa_bias_table(rel_bias):
    W = WINDOW
    qi = jnp.arange(W, dtype=jnp.int32)[:, None] + W
    kj = jnp.arange(2 * W, dtype=jnp.int32)[None, :]
    dist = qi - kj
    max_exact = REL_BUCKETS // 2
    d = jnp.maximum(dist, 0)
    log_ratio = jnp.log(jnp.maximum(d, 1).astype(F32) / max_exact) / math.log(REL_MAX_DIST / max_exact)
    large = jnp.minimum(max_exact + (log_ratio * (REL_BUCKETS - max_exact)).astype(jnp.int32), REL_BUCKETS - 1)
    bucket = jnp.where(d < max_exact, d, large)
    bucket = bucket.reshape(-1)
    onehot = (bucket[None, :] == jnp.arange(REL_BUCKETS, dtype=jnp.int32)[:, None]).astype(F32)
    bias = jnp.dot(rel_bias.astype(F32).T, onehot, precision=lax.Precision.HIGHEST)
    return bias.reshape(SWA_Q_HEADS, W, 2 * W), bucket


def attn_out(oa, ob, oc, gn, wout, gpost, x):
    T = x.shape[0]
    tm = _tile(T, 512)

    def body(oa_ref, ob_ref, oc_ref, gn_ref, w_ref, gp_ref, x_ref, x2_ref, y_ref, mT_ref):
        g = gn_ref[...]
        mixed = jnp.concatenate([_rms_fwd(oa_ref[...], g[:, 0:512]), _rms_fwd(ob_ref[...], g[:, 512:768]),
                                 _rms_fwd(oc_ref[...], g[:, 768:1024])], axis=1)
        mT_ref[...] = mixed.T.astype(MXU)
        y = _dot(mixed, w_ref[...])
        y_ref[...] = y
        x2_ref[...] = x_ref[...] + _rms_fwd(y, gp_ref[...])

    row = lambda i: (i, 0)
    const = lambda i: (0, 0)
    return pl.pallas_call(
        body, name="attn_out", grid=(T // tm,),
        in_specs=[pl.BlockSpec((tm, 512), row), pl.BlockSpec((tm, 256), row), pl.BlockSpec((tm, 256), row),
                  pl.BlockSpec((1, 1024), const), pl.BlockSpec((1024, 1024), const), pl.BlockSpec((1, 1024), const),
                  pl.BlockSpec((tm, 1024), row)],
        out_specs=[pl.BlockSpec((tm, 1024), row), pl.BlockSpec((tm, 1024), row),
                   pl.BlockSpec((1024, tm), lambda i: (0, i))],
        out_shape=[jax.ShapeDtypeStruct((T, 1024), F32), jax.ShapeDtypeStruct((T, 1024), F32),
                   jax.ShapeDtypeStruct((1024, T), MXU)],
        compiler_params=_cparams(("parallel",)),
    )(oa, ob, oc, gn, wout, gpost, x)


def attn_out_bwd(dx2, y, oa, ob, oc, gn, wout, gpost):
    T = dx2.shape[0]
    tm = _tile(T, 512)

    def body(dx_ref, y_ref, oa_ref, ob_ref, oc_ref, gn_ref, w_ref, gp_ref,
             dy_ref, da_ref, db_ref, dc_ref, dgn_ref, dgp_ref):
        first = pl.program_id(0) == 0
        dy, dgp = _rms_bwd(dx_ref[...], y_ref[...], gp_ref[...])
        dy_ref[...] = dy.astype(MXU)
        _acc_out(dgp_ref, dgp, first)
        dm = _dot_nt(dy, w_ref[...])
        g = gn_ref[...]
        da, dga = _rms_bwd(dm[:, 0:512], oa_ref[...], g[:, 0:512])
        db, dgb = _rms_bwd(dm[:, 512:768], ob_ref[...], g[:, 512:768])
        dc, dgc = _rms_bwd(dm[:, 768:1024], oc_ref[...], g[:, 768:1024])
        da_ref[...] = da
        db_ref[...] = db
        dc_ref[...] = dc
        _acc_out(dgn_ref, jnp.concatenate([dga, dgb, dgc], axis=1), first)

    row = lambda i: (i, 0)
    const = lambda i: (0, 0)
    return pl.pallas_call(
        body, name="attn_out_bwd", grid=(T // tm,),
        in_specs=[pl.BlockSpec((tm, 1024), row), pl.BlockSpec((tm, 1024), row),
                  pl.BlockSpec((tm, 512), row), pl.BlockSpec((tm, 256), row), pl.BlockSpec((tm, 256), row),
                  pl.BlockSpec((1, 1024), const), pl.BlockSpec((1024, 1024), const), pl.BlockSpec((1, 1024), const)],
        out_specs=[pl.BlockSpec((tm, 1024), row), pl.BlockSpec((tm, 512), row), pl.BlockSpec((tm, 256), row),
                   pl.BlockSpec((tm, 256), row), pl.BlockSpec((1, 1024), const), pl.BlockSpec((1, 1024), const)],
        out_shape=[jax.ShapeDtypeStruct((T, 1024), MXU), jax.ShapeDtypeStruct((T, 512), F32),
                   jax.ShapeDtypeStruct((T, 256), F32), jax.ShapeDtypeStruct((T, 256), F32),
                   jax.ShapeDtypeStruct((1, 1024), F32), jax.ShapeDtypeStruct((1, 1024), F32)],
        compiler_params=_cparams(("arbitrary",)),
    )(dx2, y, oa, ob, oc, gn, wout, gpost)


FF_TILE = 256
_GELU_C = math.sqrt(2.0 / math.pi)


def _gelu(x):
    return 0.5 * x * (1.0 + jnp.tanh(_GELU_C * (x + 0.044715 * x * x * x)))


def _gelu_grad(x):
    t = jnp.tanh(_GELU_C * (x + 0.044715 * x * x * x))
    return 0.5 * (1.0 + t) + 0.5 * x * (1.0 - t * t) * _GELU_C * (1.0 + 3 * 0.044715 * x * x)


def _conv_taps(u, hal_ref, first):
    row = lax.broadcasted_iota(jnp.int32, (u.shape[0], 1), 0)
    h6 = jnp.where(first, 0.0, hal_ref[6:7, :])
    h7 = jnp.where(first, 0.0, hal_ref[7:8, :])
    r1 = jnp.where(row == 0, h7, pltpu.roll(u, 1, 0))
    r2 = jnp.where(row == 0, h6, jnp.where(row == 1, h7, pltpu.roll(u, 2, 0)))
    return r1, r2


def ffn_fwd(u0, convw, convb, wdown, gpost, x2):
    T = x2.shape[0]
    tm, tn = _tile(T, 512), FF_TILE
    nj = D_FF // tn

    def body(ug_ref, uu_ref, hg_ref, hu_ref, wg_ref, wu_ref, bg_ref, bu_ref, wd_ref, gp_ref, x_ref,
             x3_ref, y_ref, aT_ref, acc):
        i, j = pl.program_id(0), pl.program_id(1)
        first = i == 0

        def conv(u_ref, h_ref, w_ref, b_ref):
            u = u_ref[...]
            r1, r2 = _conv_taps(u, h_ref, first)
            return b_ref[...] + w_ref[0:1, :] * r2 + w_ref[1:2, :] * r1 + w_ref[2:3, :] * u

        a = _gelu(conv(ug_ref, hg_ref, wg_ref, bg_ref)) * conv(uu_ref, hu_ref, wu_ref, bu_ref)
        aT_ref[...] = a.T.astype(MXU)
        _acc_out(acc, _dot(a, wd_ref[...]), j == 0)

        @pl.when(j == nj - 1)
        def _():
            y = acc[...]
            y_ref[...] = y
            x3_ref[...] = x_ref[...] + _rms_fwd(y, gp_ref[...])

    halo = lambda off: (lambda i, j: (jnp.maximum(i * (tm // 8) - 1, 0), off + j))
    return pl.pallas_call(
        body, name="ffn_fwd", grid=(T // tm, nj),
        in_specs=[pl.BlockSpec((tm, tn), lambda i, j: (i, j)), pl.BlockSpec((tm, tn), lambda i, j: (i, nj + j)),
                  pl.BlockSpec((8, tn), halo(0)), pl.BlockSpec((8, tn), halo(nj)),
                  pl.BlockSpec((3, tn), lambda i, j: (0, j)), pl.BlockSpec((3, tn), lambda i, j: (0, nj + j)),
                  pl.BlockSpec((1, tn), lambda i, j: (0, j)), pl.BlockSpec((1, tn), lambda i, j: (0, nj + j)),
                  pl.BlockSpec((tn, 1024), lambda i, j: (j, 0)),
                  pl.BlockSpec((1, 1024), lambda i, j: (0, 0)),
                  pl.BlockSpec((tm, 1024), lambda i, j: (i, 0))],
        out_specs=[pl.BlockSpec((tm, 1024), lambda i, j: (i, 0)), pl.BlockSpec((tm, 1024), lambda i, j: (i, 0)),
                   pl.BlockSpec((tn, tm), lambda i, j: (j, i))],
        out_shape=[jax.ShapeDtypeStruct((T, 1024), F32), jax.ShapeDtypeStruct((T, 1024), F32),
                   jax.ShapeDtypeStruct((D_FF, T), MXU)],
        scratch_shapes=[pltpu.VMEM((tm, 1024), F32)],
        compiler_params=_cparams(("parallel", "arbitrary")),
    )(u0, u0, u0, u0, convw, convw, convb, convb, wdown, gpost, x2)


def ffn_bwd(dx3, y, u0, convw, convb, wdown, gpost):
    T = dx3.shape[0]
    tm, tn = _tile(T, 512), FF_TILE
    nj = D_FF // tn
    ni = T // tm

    def body(dx_ref, y_ref, ug_ref, uu_ref, hg_ref, hu_ref, wg_ref, wu_ref, bg_ref, bu_ref, wd_ref, gp_ref,
             dy_ref, dug_ref, duu_ref, dcg_ref, dcu_ref, dgp_ref, dy_sc, cg, cu, ag, au):
        s, j = pl.program_id(0), pl.program_id(1)
        i = ni - 1 - s
        first_tok = i == 0
        row = lax.broadcasted_iota(jnp.int32, (tm, 1), 0)
        sub = lax.broadcasted_iota(jnp.int32, (8, 1), 0)

        @pl.when(j == 0)
        def _():
            dy, dgp = _rms_bwd(dx_ref[...], y_ref[...], gp_ref[...])
            dy_sc[...] = dy.astype(MXU)
            dy_ref[...] = dy.astype(MXU)
            _acc_out(dgp_ref, dgp, s == 0)

        @pl.when(s == 0)
        def _():
            cg[j] = jnp.zeros((8, tn), F32)
            cu[j] = jnp.zeros((8, tn), F32)
            ag[j] = jnp.zeros((8, tn), F32)
            au[j] = jnp.zeros((8, tn), F32)

        da = _dot_nt(dy_sc[...], wd_ref[...])

        def conv(u_ref, h_ref, w_ref, b_ref):
            u = u_ref[...]
            r1, r2 = _conv_taps(u, h_ref, first_tok)
            return b_ref[...] + w_ref[0:1, :] * r2 + w_ref[1:2, :] * r1 + w_ref[2:3, :] * u, u, r1, r2

        gate, ugv, g1, g2 = conv(ug_ref, hg_ref, wg_ref, bg_ref)
        up, uuv, u1, u2 = conv(uu_ref, hu_ref, wu_ref, bu_ref)
        gl = _gelu(gate)
        dup = da * gl
        dgate = da * up * _gelu_grad(gate)

        def conv_bwd(du, u, r1, r2, w_ref, c_ref, a_ref, du_ref):
            nxt = c_ref[j]
            n0, n1 = nxt[0:1, :], nxt[1:2, :]
            f1 = jnp.where(row == tm - 1, n0, pltpu.roll(du, tm - 1, 0))
            f2 = jnp.where(row == tm - 1, n1, jnp.where(row == tm - 2, n0, pltpu.roll(du, tm - 2, 0)))
            du_ref[...] = (w_ref[2:3, :] * du + w_ref[1:2, :] * f1 + w_ref[0:1, :] * f2).astype(MXU)
            c_ref[j] = du[0:8, :]
            red = lambda v: jnp.sum(v, axis=0, keepdims=True)
            part = jnp.where(sub == 0, red(du * r2), jnp.where(sub == 1, red(du * r1), jnp.where(
                sub == 2, red(du * u), jnp.where(sub == 3, red(du), 0.0))))
            a_ref[j] = a_ref[j] + part
            return a_ref[j]

        dcg_ref[0] = conv_bwd(dgate, ugv, g1, g2, wg_ref, cg, ag, dug_ref)
        dcu_ref[0] = conv_bwd(dup, uuv, u1, u2, wu_ref, cu, au, duu_ref)

    rev = lambda s: ni - 1 - s
    halo = lambda off: (lambda s, j: (jnp.maximum(rev(s) * (tm // 8) - 1, 0), off + j))
    return pl.pallas_call(
        body, name="ffn_bwd", grid=(ni, nj),
        in_specs=[pl.BlockSpec((tm, 1024), lambda s, j: (rev(s), 0)), pl.BlockSpec((tm, 1024), lambda s, j: (rev(s), 0)),
                  pl.BlockSpec((tm, tn), lambda s, j: (rev(s), j)), pl.BlockSpec((tm, tn), lambda s, j: (rev(s), nj + j)),
                  pl.BlockSpec((8, tn), halo(0)), pl.BlockSpec((8, tn), halo(nj)),
                  pl.BlockSpec((3, tn), lambda s, j: (0, j)), pl.BlockSpec((3, tn), lambda s, j: (0, nj + j)),
                  pl.BlockSpec((1, tn), lambda s, j: (0, j)), pl.BlockSpec((1, tn), lambda s, j: (0, nj + j)),
                  pl.BlockSpec((tn, 1024), lambda s, j: (j, 0)),
                  pl.BlockSpec((1, 1024), lambda s, j: (0, 0))],
        out_specs=[pl.BlockSpec((tm, 1024), lambda s, j: (rev(s), 0)),
                   pl.BlockSpec((tm, tn), lambda s, j: (rev(s), j)), pl.BlockSpec((tm, tn), lambda s, j: (rev(s), j)),
                   pl.BlockSpec((1, 8, tn), lambda s, j: (s, 0, j)), pl.BlockSpec((1, 8, tn), lambda s, j: (s, 0, j)),
                   pl.BlockSpec((1, 1024), lambda s, j: (0, 0))],
        out_shape=[jax.ShapeDtypeStruct((T, 1024), MXU), jax.ShapeDtypeStruct((T, D_FF), MXU),
                   jax.ShapeDtypeStruct((T, D_FF), MXU),
                   jax.ShapeDtypeStruct((ni, 8, D_FF), F32), jax.ShapeDtypeStruct((ni, 8, D_FF), F32),
                   jax.ShapeDtypeStruct((1, 1024), F32)],
        scratch_shapes=[pltpu.VMEM((tm, 1024), MXU)] + [pltpu.VMEM((nj, 8, tn), F32)] * 4,
        compiler_params=_cparams(("arbitrary", "arbitrary")),
    )(dx3, y, u0, u0, u0, u0, convw, convw, convb, convb, wdown, gpost)


ELEMS_PER_BLOCK = 512 * 1024


def _row_block(R, C):
    if R * C <= ELEMS_PER_BLOCK or R % 8:
        return R
    best = 8
    for t in range(8, R + 1, 8):
        if R % t == 0 and t * C <= ELEMS_PER_BLOCK:
            best = t
    return best


def adamw(w, g, m, v, name):
    R, C = w.shape
    partials = g.ndim == 3
    tr = _row_block(R, 2 * C)
    c1 = 1.0 - ADAM_B1 ** ADAM_STEP
    c2 = 1.0 - ADAM_B2 ** ADAM_STEP

    def body(w_ref, g_ref, m_ref, v_ref, g_out, d_ref, nm_ref, nv_ref):
        if partials:
            gv = g_ref[0].astype(F32)
            for d in range(1, N_DEV):
                gv = gv + g_ref[d].astype(F32)
        else:
            gv = g_ref[...]
        g_out[...] = gv
        nm = ADAM_B1 * m_ref[...] + (1.0 - ADAM_B1) * gv
        nv = ADAM_B2 * v_ref[...] + (1.0 - ADAM_B2) * (gv * gv)
        nm_ref[...] = nm
        nv_ref[...] = nv
        d_ref[...] = -ADAM_LR * ((nm / c1) / (jnp.sqrt(nv / c2) + ADAM_EPS) + ADAM_WD * w_ref[...])

    spec = pl.BlockSpec((tr, C), lambda i: (i, 0))
    gspec = pl.BlockSpec((N_DEV, tr, C), lambda i: (0, i, 0)) if partials else spec
    return pl.pallas_call(
        body, name=name, grid=(R // tr,), in_specs=[spec, gspec, spec, spec], out_specs=[spec] * 4,
        out_shape=[jax.ShapeDtypeStruct((R, C), F32)] * 4,
        compiler_params=_cparams(("parallel",)),
    )(w, g, m, v)


def sum_devices(buf, name):
    _, R, C = buf.shape
    tr = _row_block(R, C * 4)

    def body(b_ref, o_ref):
        acc = b_ref[0].astype(F32)
        for d in range(1, N_DEV):
            acc = acc + b_ref[d].astype(F32)
        o_ref[...] = acc

    return pl.pallas_call(
        body, name=name, grid=(R // tr,),
        in_specs=[pl.BlockSpec((N_DEV, tr, C), lambda i: (0, i, 0))],
        out_specs=pl.BlockSpec((tr, C), lambda i: (i, 0)),
        out_shape=jax.ShapeDtypeStruct((R, C), F32),
        compiler_params=_cparams(("parallel",)),
    )(buf)


def exchange(srcs, name, gather):
    n = len(srcs)
    if gather:
        shapes = [(N_DEV,) + s.shape for s in srcs]
    else:
        shapes = [(N_DEV, s.shape[0]) + s.shape[2:] for s in srcs]

    def body(*refs):
        src_refs, out_refs = refs[:n], refs[n:2 * n]
        send_sems, recv_sems, local_sems = refs[2 * n:]
        x, y, c = lax.axis_index("x"), lax.axis_index("y"), lax.axis_index("c")
        me = 4 * x + 2 * y + c
        flip = lambda a, bit: 1 - a if bit else a
        part = lambda ref, d: ref if gather else ref.at[:, d]

        def copy(t, k):
            px, py, pc = flip(x, (k >> 2) & 1), flip(y, (k >> 1) & 1), flip(c, k & 1)
            peer = 4 * px + 2 * py + pc
            sem = t * (N_DEV - 1) + k - 1
            outgoing = pltpu.make_async_remote_copy(
                src_ref=part(src_refs[t], peer), dst_ref=out_refs[t].at[me],
                send_sem=send_sems.at[sem], recv_sem=recv_sems.at[sem],
                device_id=(px, py, pc), device_id_type=pl.DeviceIdType.MESH)
            incoming = pltpu.make_async_remote_copy(
                src_ref=part(src_refs[t], me), dst_ref=out_refs[t].at[peer],
                send_sem=send_sems.at[sem], recv_sem=recv_sems.at[sem],
                device_id=(px, py, pc), device_id_type=pl.DeviceIdType.MESH)
            return outgoing, incoming

        copies = [copy(t, k) for k in range(1, N_DEV) for t in range(n)]
        for outgoing, _ in copies:
            outgoing.start()
        mine = [pltpu.make_async_copy(part(src_refs[t], me), out_refs[t].at[me], local_sems.at[t]) for t in range(n)]
        for cp in mine:
            cp.start()
        for _, incoming in copies:
            incoming.wait_recv()
        for outgoing, _ in copies:
            outgoing.wait_send()
        for cp in mine:
            cp.wait()

    return pl.pallas_call(
        body, name=name,
        in_specs=[pl.BlockSpec(memory_space=pl.ANY)] * n, out_specs=[pl.BlockSpec(memory_space=pl.ANY)] * n,
        out_shape=[jax.ShapeDtypeStruct(shp, s.dtype) for shp, s in zip(shapes, srcs)],
        scratch_shapes=[pltpu.SemaphoreType.DMA((n * (N_DEV - 1),)), pltpu.SemaphoreType.DMA((n * (N_DEV - 1),)),
                        pltpu.SemaphoreType.DMA((n,))],
    )(*srcs)


def _pack(parts, cols, row_align, dtype):
    flat = jnp.concatenate([p.astype(dtype) for p in parts], axis=-1)
    n = flat.shape[-1]
    block = cols * row_align
    total = -(-n // block) * block
    flat = jnp.pad(flat, [(0, 0)] * (flat.ndim - 1) + [(0, total - n)])
    return flat.reshape(flat.shape[:-1] + (total // cols, cols))


def _unpack(buf, shapes):
    lead = buf.shape[:-2]
    flat = buf.reshape(lead + (-1,))
    out, off = [], 0
    for s in shapes:
        n = int(np.prod(s))
        out.append(flat[..., off:off + n].reshape(lead + tuple(s)))
        off += n
    return out


SHARD_SHAPES = [(128, IN_COLS), (256, 48), (128, 64), (128, 1024), (1024, 704), (352, 1024)]
SHARDED = ["w_in", "w_uq", "w_ukv", "w_out", "w_up", "w_down"]


def _full_from_shards(name, s):
    if name in ("w_in", "w_out", "w_down"):
        return s.reshape((-1, s.shape[-1]))
    return s.transpose(1, 0, 2).reshape((s.shape[1], -1))


def _shards_from_full(name, f):
    if name in ("w_in", "w_out", "w_down"):
        return f.reshape((N_DEV, -1, f.shape[-1]))
    return f.reshape((f.shape[0], N_DEV, -1)).transpose(1, 0, 2)


def _perm_w_in(w):
    z = lambda n: jnp.zeros((w.shape[0], n), w.dtype)
    return jnp.concatenate([w[:, :1536], w[:, 1540:1924], w[:, 1536:1540], z(60), w[:, 1924:1956], z(32)], axis=1)


def _unperm_w_in(d):
    return jnp.concatenate([d[:, :1536], d[:, 1920:1924], d[:, 1536:1920], d[:, 1984:2016]], axis=1)


def _perm_w_uq(w):
    return jnp.pad(w.reshape(256, 4, MLA_QK_DIM), ((0, 0), (0, 0), (0, 128 - MLA_QK_DIM))).reshape(256, 512)


def _unperm_w_uq(d):
    return d.reshape(256, 4, 128)[:, :, :MLA_QK_DIM].reshape(256, 4 * MLA_QK_DIM)


def _perm_w_ukv(w):
    w4 = w.reshape(128, 4, 128)
    k = jnp.pad(w4[:, :, :64], ((0, 0), (0, 0), (0, 64))).reshape(128, 512)
    return jnp.concatenate([k, w4[:, :, 64:].reshape(128, 256)], axis=1)


def _unperm_w_ukv(d):
    dk = d[:, :512].reshape(128, 4, 128)[:, :, :64]
    dv = d[:, 512:].reshape(128, 4, 64)
    return jnp.concatenate([dk, dv], axis=-1).reshape(128, 512)


def _row(v, width=None):
    v = v.reshape(1, -1).astype(F32)
    if width is not None and v.shape[1] < width:
        v = jnp.pad(v, ((0, 0), (0, width - v.shape[1])))
    return v


def _layer_fwd(x, P, shared):
    cosr, sinr, bias = shared
    proj, hT, projb = norm_matmul(x, P["g_pre"], P["w_in_p"], "in_proj", lo_tiles=C_CQ // 512)
    qm, km, vm, cqT, ckvT = mla_prep(proj, P["gq"], P["gkv"], P["w_uq_p"], P["w_ukv_p"], cosr, sinr)
    fcol, frow = fox_gate(proj, P["fbias"])
    oa, lse_a = swa_fwd(proj, bias, P["sinks"])
    ob, lcb, lrb = flash_fwd(projb, projb, projb, fcol, frow, qblk=C_QF // 128, kblk=C_KF // 128, vblk=C_VF // 128,
                             nq=1, scale=HEAD_DIM ** -0.5, name="fox_fwd")
    oc, lcc, lrc = flash_fwd(qm, km, vm, None, None, qblk=0, kblk=0, vblk=0, nq=2,
                             scale=MLA_QK_DIM ** -0.5, name="mla_fwd")
    x2, y1, mT = attn_out(oa, ob, oc, P["gn"], P["w_out"], P["g_apost"], x)
    u0, h2T = norm_matmul(x2, P["g_fpre"], P["w_up"], "up_proj")
    x3, y2, aT = ffn_fwd(u0, P["conv_w"], P["conv_b"], P["w_down"], P["g_fpost"], x2)
    S = dict(x=x, proj=proj, projb=projb, hT=hT, qm=qm, km=km, vm=vm, cqT=cqT, ckvT=ckvT, fcol=fcol, frow=frow,
             oa=oa, lse_a=lse_a, ob=ob, lcb=lcb, lrb=lrb, oc=oc, lcc=lcc, lrc=lrc,
             x2=x2, y1=y1, mT=mT, u0=u0, h2T=h2T, y2=y2, aT=aT)
    return x3, S


def _layer_bwd(dx3, P, S, shared):
    cosr, sinr, bias = shared
    proj = S["proj"]
    G = {}
    dy2, dug, duu, dcg, dcu, G["ffn_post_norm"] = ffn_bwd(dx3, S["y2"], S["u0"], P["conv_w"], P["conv_b"],
                                                         P["w_down"], P["g_fpost"])
    du0 = jnp.concatenate([dug, duu], axis=1)
    dconv = jnp.concatenate([dcg[-1], dcu[-1]], axis=1)
    G["conv_w"], G["conv_b"] = dconv[0:3], dconv[3]
    G["w_down"] = matmul_nn(S["aT"], dy2, "dw_down", MXU)
    G["w_up"] = matmul_nn(S["h2T"], du0, "dw_up", MXU)
    dx2, G["ffn_pre_norm"] = matmul_nt_normbwd(du0, P["w_up"], S["x2"], P["g_fpre"], dx3, "up_bwd")
    dy1, doa, dob, doc, G["group_norm"], G["attn_post_norm"] = attn_out_bwd(
        dx2, S["y1"], S["oa"], S["ob"], S["oc"], P["gn"], P["w_out"], P["g_apost"])
    G["w_out"] = matmul_nn(S["mT"], dy1, "dw_out", MXU)
    dqa, dka, dva, dbias, dsk = swa_bwd(proj, bias, P["sinks"], doa, S["oa"], S["lse_a"])
    G["swa_sinks"] = dsk[:, 0]
    fox = dict(qblk=C_QF // 128, kblk=C_KF // 128, vblk=C_VF // 128, doblk=0, nq=1, scale=HEAD_DIM ** -0.5)
    pb = S["projb"]
    dqf, drow_f, dFq = flash_dq(pb, pb, pb, dob, S["ob"], S["lcb"], S["fcol"], S["frow"], name="fox_dq", **fox)
    dkf, dvf, dF = flash_dkv(pb, pb, pb, dob, S["lrb"], drow_f, S["fcol"], S["frow"], name="fox_dkv", **fox)
    dmisc_f, dfb = fox_gate_bwd(dFq, dF, proj, P["fbias"])
    G["forget_bias"] = dfb[0, 0:4]
    mla = dict(qblk=0, kblk=0, vblk=0, doblk=0, nq=2, scale=MLA_QK_DIM ** -0.5)
    dqm_, drow_c = flash_dq(S["qm"], S["km"], S["vm"], doc, S["oc"], S["lcc"], None, None, name="mla_dq", **mla)
    dkm_, dvm_ = flash_dkv(S["qm"], S["km"], S["vm"], doc, S["lrc"], drow_c, None, None, name="mla_dkv", **mla)
    dqm, dkv, dcq, dckv, dmisc_r, G["q_latent_norm"], G["kv_latent_norm"] = mla_prep_bwd(
        dqm_, dkm_, dvm_, proj, P["gq"], P["gkv"], P["w_uq_p"], P["w_ukv_p"], cosr, sinr)
    G["w_uq"] = _unperm_w_uq(matmul_nn(S["cqT"], dqm, "dw_uq", MXU))
    G["w_ukv"] = _unperm_w_ukv(matmul_nn(S["ckvT"], dkv, "dw_ukv", MXU))
    dproj = jnp.concatenate([dqa, dka, dva, dqf, dkf, dvf, dcq, dckv, dmisc_f + dmisc_r], axis=1).astype(MXU)
    G["w_in"] = _unperm_w_in(matmul_nn(S["hT"], dproj, "dw_in", MXU))
    dx, G["attn_pre_norm"] = matmul_nt_normbwd(dproj, P["w_in_p"], S["x"], P["g_pre"], dx2, "in_bwd")
    return dx, G, dbias


def _layer_params(l, full, small):
    return dict(
        g_pre=_row(small["attn_pre_norm"][l]), w_in_p=_perm_w_in(full["w_in"]),
        gq=_row(small["q_latent_norm"][l]), gkv=_row(small["kv_latent_norm"][l]),
        w_uq_p=_perm_w_uq(full["w_uq"]), w_ukv_p=_perm_w_ukv(full["w_ukv"]),
        fbias=_row(small["forget_bias"][l], 128), sinks=small["swa_sinks"][l].astype(F32),
        gn=_row(small["group_norm"][l]), w_out=full["w_out"], g_apost=_row(small["attn_post_norm"][l]),
        g_fpre=_row(small["ffn_pre_norm"][l]), w_up=full["w_up"], conv_w=full["conv_w"],
        conv_b=_row(small["conv_b"][l]), w_down=full["w_down"], g_fpost=_row(small["ffn_post_norm"][l]))


def _rel_bias_grad(dbias, bucket):
    flat = dbias.reshape(SWA_Q_HEADS, -1)
    hi = flat.astype(MXU)
    lo = (flat - hi.astype(F32)).astype(MXU)
    onehot = (bucket[:, None] == jnp.arange(128, dtype=jnp.int32)[None, :]).astype(MXU)
    r = matmul_nn(jnp.concatenate([hi, lo], axis=0), onehot, "rel_bias_grad")
    return (r[0:8] + r[8:16])[:, :REL_BUCKETS].T


def local_step(x, tgt, fulls, small):
    T = x.shape[0]
    cosr, sinr = rope_tables(T)
    bias, bucket = swa_bias_table(small["rel_bias"])
    shared = (cosr, sinr, bias)
    Ps, Ss = [], []
    h = x
    for l in range(DEPTH):
        P = _layer_params(l, fulls[l], small)
        h, S = _layer_fwd(h, P, shared)
        Ps.append(P)
        Ss.append(S)
    dh, sq = loss_kernel(h, tgt)
    grads = [None] * DEPTH
    dbias_sum = None
    for l in reversed(range(DEPTH)):
        dh, grads[l], dbias = _layer_bwd(dh, Ps[l], Ss[l], shared)
        dbias_sum = dbias if dbias_sum is None else dbias_sum + dbias
    return sq, dh, grads, _rel_bias_grad(dbias_sum, bucket)


WEIGHTS = ['attn_pre_norm', 'w_in', 'forget_bias', 'swa_sinks', 'rel_bias', 'q_latent_norm', 'w_uq',
           'kv_latent_norm', 'w_ukv', 'group_norm', 'w_out', 'attn_post_norm', 'ffn_pre_norm', 'w_up', 'conv_w',
           'conv_b', 'w_down', 'ffn_post_norm']
SMALL_PER_LAYER = ['attn_pre_norm', 'forget_bias', 'swa_sinks', 'q_latent_norm', 'kv_latent_norm', 'group_norm',
                   'attn_post_norm', 'ffn_pre_norm', 'conv_b', 'ffn_post_norm', 'conv_w']


def kernel(x, attn_pre_norm, w_in, forget_bias, swa_sinks, rel_bias, q_latent_norm, w_uq, kv_latent_norm, w_ukv, group_norm, w_out, attn_post_norm, ffn_pre_norm, w_up, conv_w, conv_b, w_down, ffn_post_norm, loss_target, m_attn_pre_norm, m_w_in, m_forget_bias, m_swa_sinks, m_rel_bias, m_q_latent_norm, m_w_uq, m_kv_latent_norm, m_w_ukv, m_group_norm, m_w_out, m_attn_post_norm, m_ffn_pre_norm, m_w_up, m_conv_w, m_conv_b, m_w_down, m_ffn_post_norm, v_attn_pre_norm, v_w_in, v_forget_bias, v_swa_sinks, v_rel_bias, v_q_latent_norm, v_w_uq, v_kv_latent_norm, v_w_ukv, v_group_norm, v_w_out, v_attn_post_norm, v_ffn_pre_norm, v_w_up, v_conv_w, v_conv_b, v_w_down, v_ffn_post_norm):
    W = dict(attn_pre_norm=attn_pre_norm, w_in=w_in, forget_bias=forget_bias, swa_sinks=swa_sinks, rel_bias=rel_bias,
             q_latent_norm=q_latent_norm, w_uq=w_uq, kv_latent_norm=kv_latent_norm, w_ukv=w_ukv,
             group_norm=group_norm, w_out=w_out, attn_post_norm=attn_post_norm, ffn_pre_norm=ffn_pre_norm,
             w_up=w_up, conv_w=conv_w, conv_b=conv_b, w_down=w_down, ffn_post_norm=ffn_post_norm)
    M = dict(attn_pre_norm=m_attn_pre_norm, w_in=m_w_in, forget_bias=m_forget_bias, swa_sinks=m_swa_sinks,
             rel_bias=m_rel_bias, q_latent_norm=m_q_latent_norm, w_uq=m_w_uq, kv_latent_norm=m_kv_latent_norm,
             w_ukv=m_w_ukv, group_norm=m_group_norm, w_out=m_w_out, attn_post_norm=m_attn_post_norm,
             ffn_pre_norm=m_ffn_pre_norm, w_up=m_w_up, conv_w=m_conv_w, conv_b=m_conv_b, w_down=m_w_down,
             ffn_post_norm=m_ffn_post_norm)
    V = dict(attn_pre_norm=v_attn_pre_norm, w_in=v_w_in, forget_bias=v_forget_bias, swa_sinks=v_swa_sinks,
             rel_bias=v_rel_bias, q_latent_norm=v_q_latent_norm, w_uq=v_w_uq, kv_latent_norm=v_kv_latent_norm,
             w_ukv=v_w_ukv, group_norm=v_group_norm, w_out=v_w_out, attn_post_norm=v_attn_post_norm,
             ffn_pre_norm=v_ffn_pre_norm, w_up=v_w_up, conv_w=v_conv_w, conv_b=v_conv_b, w_down=v_w_down,
             ffn_post_norm=v_ffn_post_norm)
    me = 4 * lax.axis_index("x") + 2 * lax.axis_index("y") + lax.axis_index("c")

    fulls = []
    for l in range(DEPTH):
        got = exchange([W[n][l].astype(MXU) for n in SHARDED] + [conv_w[l]], "gather_weights", True)
        full = {n: _full_from_shards(n, s) for n, s in zip(SHARDED, got[:-1])}
        full["conv_w"] = got[-1].transpose(1, 0, 2).reshape(3, 2 * D_FF)
        fulls.append(full)

    sq, dx, grads, drel = local_step(x[0], loss_target[0], fulls, W)

    stacked = [jnp.stack([_shards_from_full(n, grads[l][n]) for l in range(DEPTH)]) for n in SHARDED]
    G = dict(zip(SHARDED, exchange(stacked, "scatter_grads", False)))

    parts, shapes = [], []
    for l in range(DEPTH):
        for n in SMALL_PER_LAYER:
            parts.append(grads[l][n].astype(F32).reshape(-1))
            shapes.append(grads[l][n].shape)
    parts += [drel.reshape(-1), jnp.sum(sq).reshape(1) * (0.5 / D_MODEL)]
    shapes += [drel.shape, (1,)]
    red = _unpack(sum_devices(exchange([_pack(parts, 128, 8, F32)], "gather_small", True)[0], "sum_small"), shapes)
    k = 0
    per = {n: [] for n in SMALL_PER_LAYER}
    for l in range(DEPTH):
        for n in SMALL_PER_LAYER:
            per[n].append(red[k])
            k += 1
    for n in SMALL_PER_LAYER:
        G[n] = jnp.stack(per[n]).reshape((DEPTH, 3, 2 * D_FF) if n == "conv_w" else W[n].shape)
    G["rel_bias"] = red[k]
    loss = red[k + 1][0]
    G["conv_w"] = lax.dynamic_slice_in_dim(G["conv_w"], me * 704, 704, axis=2)

    delta, new_m, new_v = {}, {}, {}
    for n in WEIGHTS:
        shp = W[n].shape
        v2 = lambda a: a.reshape(-1, shp[-1])
        g = G[n].reshape(N_DEV, -1, shp[-1]) if n in SHARDED else v2(G[n])
        g, d, nm, nv = adamw(v2(W[n]), g, v2(M[n]), v2(V[n]), "adamw_" + n)
        G[n], delta[n], new_m[n], new_v[n] = g.reshape(shp), d.reshape(shp), nm.reshape(shp), nv.reshape(shp)
    return (loss, dx[None], *[G[n] for n in WEIGHTS], *[delta[n] for n in WEIGHTS],
            *[new_m[n] for n in WEIGHTS], *[new_v[n] for n in WEIGHTS])
```

```python
import math

import numpy as np
import jax
import jax.numpy as jnp
from jax import lax
from jax.experimental import pallas as pl
from jax.experimental.pallas import tpu as pltpu

F32 = jnp.float32
MXU = jnp.bfloat16

N_DEV = 8
DEPTH = 4
D_MODEL = 1024
HEAD_DIM = 64
WINDOW = 128
SWA_Q_HEADS = 8
REL_BUCKETS = 32
REL_MAX_DIST = 128
MLA_QK_DIM = 96
ROPE_DIM = 32
ROPE_THETA = 10000.0
D_FF = 2816
EPS = 1e-6
NEG = -1e30
IN_COLS = 1956
IN_COLS_P = 2048
C_QA, C_KA, C_VA = 0, 512, 640
C_QF, C_KF, C_VF = 768, 1024, 1280
C_CQ, C_CKV, C_MISC = 1536, 1792, 1920
ROPE_LANE0 = 64
ADAM_LR, ADAM_B1, ADAM_B2, ADAM_EPS, ADAM_WD, ADAM_STEP = 0.001, 0.9, 0.999, 1e-08, 0.01, 10

VMEM_LIMIT = 56 * 1024 * 1024
PACK_COLS = 1024
PACK_ROW_ALIGN = 16


def _cparams(sem=None):
    return pltpu.CompilerParams(dimension_semantics=sem, vmem_limit_bytes=VMEM_LIMIT)


def _tile(n, pref):
    if n <= pref:
        return n
    t = pref - pref % 128
    while t >= 128:
        if n % t == 0:
            return t
        t -= 128
    return n


def _dot(a, b):
    return jnp.dot(a.astype(MXU), b.astype(MXU), preferred_element_type=F32)


def _dot_nt(a, b):
    return lax.dot_general(a.astype(MXU), b.astype(MXU), (((1,), (1,)), ((), ())),
                           preferred_element_type=F32)


def _rms_fwd(x, g):
    return x * lax.rsqrt(jnp.mean(x * x, axis=-1, keepdims=True) + EPS) * g


def _rms_bwd(dy, x, g, n=None):
    r = lax.rsqrt(jnp.mean(x * x, axis=-1, keepdims=True) + EPS)
    xh = x * r
    dg = jnp.sum(dy * xh, axis=0, keepdims=True)
    dxh = dy * g
    dx = r * (dxh - xh * jnp.mean(dxh * xh, axis=-1, keepdims=True))
    return dx, dg


def _acc_out(ref, val, first):
    @pl.when(first)
    def _():
        ref[...] = val

    @pl.when(jnp.logical_not(first))
    def _():
        ref[...] += val


def norm_matmul(x, g, w, name, lo_tiles=0, tn_pref=512):
    T, K = x.shape
    N = w.shape[1]
    tm, tn = _tile(T, 1024), _tile(N, tn_pref)

    def body(x_ref, g_ref, w_ref, o_ref, hT_ref, *rest):
        h_sc = rest[-1]
        j = pl.program_id(1)

        @pl.when(j == 0)
        def _():
            h = _rms_fwd(x_ref[...], g_ref[...])
            h_sc[...] = h.astype(MXU)
            hT_ref[...] = h.T.astype(MXU)

        r = jnp.dot(h_sc[...], w_ref[...], preferred_element_type=F32)
        o_ref[...] = r
        if lo_tiles:
            @pl.when(j < lo_tiles)
            def _():
                rest[0][...] = r.astype(MXU)

    out_specs = [pl.BlockSpec((tm, tn), lambda i, j: (i, j)), pl.BlockSpec((K, tm), lambda i, j: (0, i))]
    out_shape = [jax.ShapeDtypeStruct((T, N), F32), jax.ShapeDtypeStruct((K, T), MXU)]
    if lo_tiles:
        out_specs.append(pl.BlockSpec((tm, tn), lambda i, j: (i, jnp.minimum(j, lo_tiles - 1))))
        out_shape.append(jax.ShapeDtypeStruct((T, lo_tiles * tn), MXU))
    return pl.pallas_call(
        body, name=name, grid=(T // tm, N // tn),
        in_specs=[pl.BlockSpec((tm, K), lambda i, j: (i, 0)),
                  pl.BlockSpec((1, K), lambda i, j: (0, 0)),
                  pl.BlockSpec((K, tn), lambda i, j: (0, j))],
        out_specs=out_specs, out_shape=out_shape,
        scratch_shapes=[pltpu.VMEM((tm, K), MXU)],
        compiler_params=_cparams(("parallel", "arbitrary")),
    )(x, g, w)


def matmul_nn(a, b, name, out_dtype=F32):
    M, K = a.shape
    N = b.shape[1]
    tm, tn, tk = _tile(M, 1024), _tile(N, 1536), _tile(K, 1024)
    nk = K // tk

    def body(a_ref, b_ref, o_ref, acc):
        k = pl.program_id(2)
        part = _dot(a_ref[...], b_ref[...])
        _acc_out(acc, part, k == 0)

        @pl.when(k == nk - 1)
        def _():
            o_ref[...] = acc[...].astype(out_dtype)

    return pl.pallas_call(
        body, name=name, grid=(M // tm, N // tn, nk),
        in_specs=[pl.BlockSpec((tm, tk), lambda i, j, k: (i, k)),
                  pl.BlockSpec((tk, tn), lambda i, j, k: (k, j))],
        out_specs=pl.BlockSpec((tm, tn), lambda i, j, k: (i, j)),
        out_shape=jax.ShapeDtypeStruct((M, N), out_dtype),
        scratch_shapes=[pltpu.VMEM((tm, tn), F32)],
        compiler_params=_cparams(("parallel", "parallel", "arbitrary")),
    )(a, b)


def matmul_nt_normbwd(dy, w, x, g, dres, name):
    T, N = dy.shape
    K = w.shape[0]
    tm, tn = _tile(T, 1024), _tile(N, 1536)
    nj = N // tn

    def body(dy_ref, w_ref, x_ref, g_ref, dres_ref, dx_ref, dg_ref, acc):
        i, j = pl.program_id(0), pl.program_id(1)
        _acc_out(acc, _dot_nt(dy_ref[...], w_ref[...]), j == 0)

        @pl.when(j == nj - 1)
        def _():
            dx, dg = _rms_bwd(acc[...], x_ref[...], g_ref[...])
            dx_ref[...] = dres_ref[...] + dx
            _acc_out(dg_ref, dg, i == 0)

    return pl.pallas_call(
        body, name=name, grid=(T // tm, nj),
        in_specs=[pl.BlockSpec((tm, tn), lambda i, j: (i, j)),
                  pl.BlockSpec((K, tn), lambda i, j: (0, j)),
                  pl.BlockSpec((tm, K), lambda i, j: (i, 0)),
                  pl.BlockSpec((1, K), lambda i, j: (0, 0)),
                  pl.BlockSpec((tm, K), lambda i, j: (i, 0))],
        out_specs=[pl.BlockSpec((tm, K), lambda i, j: (i, 0)),
                   pl.BlockSpec((1, K), lambda i, j: (0, 0))],
        out_shape=[jax.ShapeDtypeStruct((T, K), F32), jax.ShapeDtypeStruct((1, K), F32)],
        scratch_shapes=[pltpu.VMEM((tm, K), F32)],
        compiler_params=_cparams(("arbitrary", "arbitrary")),
    )(dy, w, x, g, dres)


def loss_kernel(y, tgt):
    T, D = y.shape
    tm = _tile(T, 512)

    def body(y_ref, t_ref, dy_ref, acc_ref):
        e = y_ref[...] - t_ref[...]
        dy_ref[...] = e * (1.0 / D)
        _acc_out(acc_ref, jnp.sum(e * e, axis=0, keepdims=True), pl.program_id(0) == 0)

    return pl.pallas_call(
        body, name="loss", grid=(T // tm,),
        in_specs=[pl.BlockSpec((tm, D), lambda i: (i, 0)), pl.BlockSpec((tm, D), lambda i: (i, 0))],
        out_specs=[pl.BlockSpec((tm, D), lambda i: (i, 0)), pl.BlockSpec((1, D), lambda i: (0, 0))],
        out_shape=[jax.ShapeDtypeStruct((T, D), F32), jax.ShapeDtypeStruct((1, D), F32)],
        compiler_params=_cparams(("arbitrary",)),
    )(y, tgt)


def _rope_partner(x):
    lane = lax.broadcasted_iota(jnp.int32, (1, 128), 1)
    return jnp.where(lane < ROPE_LANE0 + ROPE_DIM // 2, pltpu.roll(x, 128 - ROPE_DIM // 2, 1),
                     pltpu.roll(x, ROPE_DIM // 2, 1))


def _rope_apply(x, cos, sin_signed):
    return x * cos + _rope_partner(x) * sin_signed


def _rope_apply_bwd(dy, cos, sin_signed):
    lane = lax.broadcasted_iota(jnp.int32, (1, 128), 1)
    rotary = (lane >= ROPE_LANE0) & (lane < ROPE_LANE0 + ROPE_DIM)
    return dy * cos + jnp.where(rotary, _rope_partner(dy * sin_signed), 0.0)


def rope_tables(T):
    pos = jnp.arange(T, dtype=F32)
    inv_freq = ROPE_THETA ** (-(jnp.arange(ROPE_DIM // 2, dtype=F32) * 2.0 / ROPE_DIM))
    ang = pos[:, None] * inv_freq[None, :]
    cos, sin = jnp.cos(ang), jnp.sin(ang)
    z = jnp.zeros((T, ROPE_LANE0), F32)
    z2 = jnp.zeros((T, 128 - ROPE_LANE0 - ROPE_DIM), F32)
    cosr = jnp.concatenate([z, cos, cos, z2], axis=1)
    sinr = jnp.concatenate([z, -sin, sin, z2], axis=1)
    return cosr, sinr


def mla_prep(proj, gq, gkv, wuq, wukv, cosr, sinr):
    T = proj.shape[0]
    tm = _tile(T, 512)

    def body(cq_ref, ckv_ref, misc_ref, gq_ref, gkv_ref, wuq_ref, wukv_ref, cos_ref, sin_ref,
             q_ref, k_ref, v_ref, cqT_ref, ckvT_ref):
        lane = lax.broadcasted_iota(jnp.int32, (1, 128), 1)
        cosr_, sinr_ = cos_ref[...], sin_ref[...]
        cosq = cosr_ + jnp.where(lane < ROPE_LANE0, 1.0, 0.0)
        cqn = _rms_fwd(cq_ref[...], gq_ref[...])
        cqT_ref[...] = cqn.T.astype(MXU)
        qm = _dot(cqn, wuq_ref[...])
        q_ref[...] = jnp.concatenate(
            [_rope_apply(qm[:, 128 * h:128 * (h + 1)], cosq, sinr_) for h in range(4)], axis=1).astype(MXU)
        ckvn = _rms_fwd(ckv_ref[...], gkv_ref[...])
        ckvT_ref[...] = ckvn.T.astype(MXU)
        kv = _dot(ckvn, wukv_ref[...])
        kr = _rope_apply(misc_ref[...], cosr_, sinr_)
        k_ref[...] = jnp.concatenate(
            [kv[:, 128 * h:128 * (h + 1)] + kr for h in range(4)], axis=1).astype(MXU)
        v_ref[...] = kv[:, 512:768].astype(MXU)

    row = lambda i: (i, 0)
    const = lambda i: (0, 0)
    return pl.pallas_call(
        body, name="mla_prep", grid=(T // tm,),
        in_specs=[pl.BlockSpec((tm, 256), lambda i: (i, C_CQ // 256)),
                  pl.BlockSpec((tm, 128), lambda i: (i, C_CKV // 128)),
                  pl.BlockSpec((tm, 128), lambda i: (i, C_MISC // 128)),
                  pl.BlockSpec((1, 256), const), pl.BlockSpec((1, 128), const),
                  pl.BlockSpec((256, 512), const), pl.BlockSpec((128, 768), const),
                  pl.BlockSpec((tm, 128), row), pl.BlockSpec((tm, 128), row)],
        out_specs=[pl.BlockSpec((tm, 512), row), pl.BlockSpec((tm, 512), row), pl.BlockSpec((tm, 256), row),
                   pl.BlockSpec((256, tm), lambda i: (0, i)), pl.BlockSpec((128, tm), lambda i: (0, i))],
        out_shape=[jax.ShapeDtypeStruct((T, 512), MXU), jax.ShapeDtypeStruct((T, 512), MXU),
                   jax.ShapeDtypeStruct((T, 256), MXU),
                   jax.ShapeDtypeStruct((256, T), MXU), jax.ShapeDtypeStruct((128, T), MXU)],
        compiler_params=_cparams(("parallel",)),
    )(proj, proj, proj, gq, gkv, wuq, wukv, cosr, sinr)


def mla_prep_bwd(dq, dk, dv, proj, gq, gkv, wuq, wukv, cosr, sinr):
    T = proj.shape[0]
    tm = _tile(T, 512)

    def body(dq_ref, dk_ref, dv_ref, cq_ref, ckv_ref, gq_ref, gkv_ref, wuq_ref, wukv_ref, cos_ref, sin_ref,
             dqm_ref, dkv_ref, dcq_ref, dckv_ref, dmisc_ref, dgq_ref, dgkv_ref):
        first = pl.program_id(0) == 0
        lane = lax.broadcasted_iota(jnp.int32, (1, 128), 1)
        cosr_, sinr_ = cos_ref[...], sin_ref[...]
        cosq = cosr_ + jnp.where(lane < ROPE_LANE0, 1.0, 0.0)
        dqv = dq_ref[...]
        dqm = jnp.concatenate(
            [_rope_apply_bwd(dqv[:, 128 * h:128 * (h + 1)], cosq, sinr_) for h in range(4)], axis=1)
        dqm_ref[...] = dqm.astype(MXU)
        dcq, dgq = _rms_bwd(_dot_nt(dqm, wuq_ref[...]), cq_ref[...], gq_ref[...])
        dcq_ref[...] = dcq
        _acc_out(dgq_ref, dgq, first)
        dkv_ = dk_ref[...]
        heads = [dkv_[:, 128 * h:128 * (h + 1)] for h in range(4)]
        dkr = heads[0] + heads[1] + heads[2] + heads[3]
        dmisc_ref[...] = _rope_apply_bwd(dkr, cosr_, sinr_)
        dkvm = jnp.concatenate([jnp.where(lane < ROPE_LANE0, hd, 0.0) for hd in heads] + [dv_ref[...]], axis=1)
        dkv_ref[...] = dkvm.astype(MXU)
        dckv, dgkv = _rms_bwd(_dot_nt(dkvm, wukv_ref[...]), ckv_ref[...], gkv_ref[...])
        dckv_ref[...] = dckv
        _acc_out(dgkv_ref, dgkv, first)

    row = lambda i: (i, 0)
    const = lambda i: (0, 0)
    return pl.pallas_call(
        body, name="mla_prep_bwd", grid=(T // tm,),
        in_specs=[pl.BlockSpec((tm, 512), row), pl.BlockSpec((tm, 512), row), pl.BlockSpec((tm, 256), row),
                  pl.BlockSpec((tm, 256), lambda i: (i, C_CQ // 256)),
                  pl.BlockSpec((tm, 128), lambda i: (i, C_CKV // 128)),
                  pl.BlockSpec((1, 256), const), pl.BlockSpec((1, 128), const),
                  pl.BlockSpec((256, 512), const), pl.BlockSpec((128, 768), const),
                  pl.BlockSpec((tm, 128), row), pl.BlockSpec((tm, 128), row)],
        out_specs=[pl.BlockSpec((tm, 512), row), pl.BlockSpec((tm, 768), row), pl.BlockSpec((tm, 256), row),
                   pl.BlockSpec((tm, 128), row), pl.BlockSpec((tm, 128), row),
                   pl.BlockSpec((1, 256), const), pl.BlockSpec((1, 128), const)],
        out_shape=[jax.ShapeDtypeStruct((T, 512), MXU), jax.ShapeDtypeStruct((T, 768), MXU),
                   jax.ShapeDtypeStruct((T, 256), F32), jax.ShapeDtypeStruct((T, 128), F32),
                   jax.ShapeDtypeStruct((T, 128), F32),
                   jax.ShapeDtypeStruct((1, 256), F32), jax.ShapeDtypeStruct((1, 128), F32)],
        compiler_params=_cparams(("arbitrary",)),
    )(dq, dk, dv, proj, proj, gq, gkv, wuq, wukv, cosr, sinr)


def _split3(x):
    hi = x.astype(MXU)
    r1 = x - hi.astype(F32)
    mid = r1.astype(MXU)
    lo = (r1 - mid.astype(F32)).astype(MXU)
    return hi, mid, lo


def _tri_matmul(tri, x):
    hi, mid, lo = _split3(x)
    d = lambda p: jnp.dot(tri, p, preferred_element_type=F32)
    return d(hi) + d(mid) + d(lo)


def _log_sigmoid(z):
    return jnp.minimum(z, 0.0) - jnp.log(1.0 + jnp.exp(-jnp.abs(z)))


def fox_gate(proj, fbias):
    T = proj.shape[0]
    tb = _tile(T, 512)

    def body(misc_ref, b_ref, fc_ref, fr_ref, frep_ref, carry):
        @pl.when(pl.program_id(0) == 0)
        def _():
            carry[...] = jnp.zeros_like(carry)

        lane = lax.broadcasted_iota(jnp.int32, (1, 128), 1)
        lf = jnp.where(lane < 4, _log_sigmoid(misc_ref[...] + b_ref[...]), 0.0)
        r = lax.broadcasted_iota(jnp.int32, (tb, tb), 0)
        c = lax.broadcasted_iota(jnp.int32, (tb, tb), 1)
        tri = jnp.where(r >= c, 1.0, 0.0).astype(MXU)
        F = _tri_matmul(tri, lf) + carry[...]
        carry[...] = carry[...] + jnp.sum(lf, axis=0, keepdims=True)
        fc_ref[0] = F
        fc_ref[1] = pltpu.roll(F, 126, 1)
        ft = F.T[0:8, :]
        fr_ref[0] = ft
        fr_ref[1] = pltpu.roll(ft, 6, 0)
        for h in range(4):
            frep_ref[h] = jnp.broadcast_to(_lane_pick(F, h), (tb, 128))

    return pl.pallas_call(
        body, name="fox_gate", grid=(T // tb,),
        in_specs=[pl.BlockSpec((tb, 128), lambda i: (i, C_MISC // 128)), pl.BlockSpec((1, 128), lambda i: (0, 0))],
        out_specs=[pl.BlockSpec((2, tb, 128), lambda i: (0, i, 0)), pl.BlockSpec((2, 8, tb), lambda i: (0, 0, i)),
                   pl.BlockSpec((4, tb, 128), lambda i: (0, i, 0))],
        out_shape=[jax.ShapeDtypeStruct((2, T, 128), F32), jax.ShapeDtypeStruct((2, 8, T), F32),
                   jax.ShapeDtypeStruct((4, T, 128), F32)],
        scratch_shapes=[pltpu.VMEM((1, 128), F32)],
        compiler_params=_cparams(("arbitrary",)),
    )(proj, fbias)


def fox_gate_bwd(dFq, dFk, proj, fbias):
    T = proj.shape[0]
    tb = _tile(T, 512)
    nb = T // tb

    def body(dq_ref, dk_ref, misc_ref, b_ref, dm_ref, db_ref, carry):
        first = pl.program_id(0) == 0

        @pl.when(first)
        def _():
            carry[...] = jnp.zeros_like(carry)

        lane = lax.broadcasted_iota(jnp.int32, (1, 128), 1)
        dF = jnp.where(lane < 4, (dq_ref[0] + dk_ref[0]) + pltpu.roll(dq_ref[1] + dk_ref[1], 2, 1), 0.0)
        r = lax.broadcasted_iota(jnp.int32, (tb, tb), 0)
        c = lax.broadcasted_iota(jnp.int32, (tb, tb), 1)
        tri = jnp.where(r <= c, 1.0, 0.0).astype(MXU)
        dlf = _tri_matmul(tri, dF) + carry[...]
        carry[...] = carry[...] + jnp.sum(dF, axis=0, keepdims=True)
        z = misc_ref[...] + b_ref[...]
        dz = jnp.where(lane < 4, dlf * (1.0 / (1.0 + jnp.exp(z))), 0.0)
        dm_ref[...] = dz
        _acc_out(db_ref, jnp.sum(dz, axis=0, keepdims=True), first)

    return pl.pallas_call(
        body, name="fox_gate_bwd", grid=(nb,),
        in_specs=[pl.BlockSpec((2, tb, 128), lambda i: (0, nb - 1 - i, 0)),
                  pl.BlockSpec((2, tb, 128), lambda i: (0, nb - 1 - i, 0)),
                  pl.BlockSpec((tb, 128), lambda i: (nb - 1 - i, C_MISC // 128)),
                  pl.BlockSpec((1, 128), lambda i: (0, 0))],
        out_specs=[pl.BlockSpec((tb, 128), lambda i: (nb - 1 - i, 0)), pl.BlockSpec((1, 128), lambda i: (0, 0))],
        out_shape=[jax.ShapeDtypeStruct((T, 128), F32), jax.ShapeDtypeStruct((1, 128), F32)],
        scratch_shapes=[pltpu.VMEM((1, 128), F32)],
        compiler_params=_cparams(("arbitrary",)),
    )(dFq, dFk, proj, fbias)


def _row_stat_tile(a, b, n):
    at = jnp.broadcast_to(a, (n, 128)).T[0:8, :]
    bt = jnp.broadcast_to(b, (n, 128)).T[0:8, :]
    sub = lax.broadcasted_iota(jnp.int32, (8, 1), 0)
    return jnp.where(sub == 0, at, jnp.where(sub == 1, bt, 0.0))


def _col_stat_tile(a, b):
    lane = lax.broadcasted_iota(jnp.int32, (1, 128), 1)
    return jnp.where(lane == 0, a, jnp.where(lane == 1, b, 0.0))


def _lane_pick(x, h):
    lane = lax.broadcasted_iota(jnp.int32, (1, 128), 1)
    return jnp.sum(jnp.where(lane == h, x, 0.0), axis=1, keepdims=True)


def _half_mask(h):
    lane = lax.broadcasted_iota(jnp.int32, (1, 128), 1)
    return (lane // HEAD_DIM) == h


def flash_fwd(q, k, v, frep, frow, *, qblk, kblk, vblk, nq, scale, name):
    T = q.shape[0]
    tq = tk = _tile(T, 256)
    wq = 128 * nq
    has_f = frep is not None

    def body(*refs):
        if has_f:
            q_ref, k_ref, v_ref, fk_ref, fr_ref, o_ref, lr_ref, vT_sc, m_sc, acc_sc = refs
        else:
            q_ref, k_ref, v_ref, o_ref, lr_ref, vT_sc, m_sc, acc_sc = refs
        i = pl.program_id(1)

        @pl.when(i == 0)
        def _():
            vT_sc[...] = v_ref[...].astype(F32).T.astype(MXU)

        diag = lax.broadcasted_iota(jnp.int32, (tk, 1), 0) <= lax.broadcasted_iota(jnp.int32, (1, tq), 1)
        row_half = lax.broadcasted_iota(jnp.int32, (128, 1), 0) // HEAD_DIM
        qb = q_ref[...]
        if nq == 1:
            qhs = [jnp.where(_half_mask(h), qb, 0).astype(MXU) for h in range(2)]
        else:
            qhs = [qb[:, 128 * h:128 * (h + 1)].astype(MXU) for h in range(2)]
        for h in range(2):
            m_sc[h] = jnp.full((1, tq), NEG, F32)
            acc_sc[h] = jnp.zeros((128, tq), F32)

        def make_step(masked):
            def step(j, carry):
                off = pl.multiple_of(j * tk, tk)
                ks = k_ref[pl.ds(off, tk), :]
                vT = vT_sc[:, pl.ds(off, tk)]
                for h in range(2):
                    kh = ks if nq == 1 else ks[:, 128 * h:128 * (h + 1)]
                    sT = _dot_nt(kh, qhs[h]) * scale
                    if has_f:
                        fk = fk_ref[h, pl.ds(off, tk), :]
                        sT = sT + (fr_ref[0, h:h + 1, :] - jnp.concatenate([fk] * (tq // 128), axis=1))
                    if masked:
                        sT = jnp.where(diag, sT, NEG)
                    m_prev = m_sc[h]
                    m_new = jnp.maximum(m_prev, jnp.max(sT, axis=0, keepdims=True))
                    alpha = jnp.exp(m_prev - m_new)
                    pT = jnp.exp(sT - m_new)
                    vTh = jnp.where(row_half == h, vT, jnp.ones_like(vT))
                    acc_sc[h] = alpha * acc_sc[h] + _dot(vTh, pT)
                    m_sc[h] = m_new
                return carry
            return step

        lax.fori_loop(0, i, make_step(False), 0)
        make_step(True)(i, 0)
        outs, lses = [], []
        for h in range(2):
            acc = acc_sc[h]
            outs.append(acc / pltpu.roll(acc, HEAD_DIM, 0))
            l = acc_sc[h, HEAD_DIM * (1 - h):HEAD_DIM * (1 - h) + 1, :]
            lses.append(m_sc[h] + jnp.log(l))
        o_ref[...] = jnp.where(row_half == 0, outs[0], outs[1]).T
        sub = lax.broadcasted_iota(jnp.int32, (8, 1), 0)
        lr_ref[0] = jnp.where(sub == 0, lses[0], jnp.where(sub == 1, lses[1], 0.0))

    in_specs = [pl.BlockSpec((tq, wq), lambda p, i: (i, qblk + p)),
                pl.BlockSpec((T, wq), lambda p, i: (0, kblk + p)),
                pl.BlockSpec((T, 128), lambda p, i: (0, vblk + p))]
    args = [q, k, v]
    if has_f:
        in_specs += [pl.BlockSpec((2, T, 128), lambda p, i: (p, 0, 0)),
                     pl.BlockSpec((1, 8, tq), lambda p, i: (p, 0, i))]
        args += [frep, frow]
    return pl.pallas_call(
        body, name=name, grid=(2, T // tq), in_specs=in_specs,
        out_specs=[pl.BlockSpec((tq, 128), lambda p, i: (i, p)),
                   pl.BlockSpec((1, 8, tq), lambda p, i: (p, 0, i))],
        out_shape=[jax.ShapeDtypeStruct((T, 256), F32), jax.ShapeDtypeStruct((2, 8, T), F32)],
        scratch_shapes=[pltpu.VMEM((128, T), MXU), pltpu.VMEM((2, 1, tq), F32), pltpu.VMEM((2, 128, tq), F32)],
        compiler_params=_cparams(("arbitrary", "arbitrary")),
    )(*args)


def flash_bwd(q, k, v, do, o, lrow, fcol, frow, *, qblk, kblk, vblk, nq, scale, name):
    T = q.shape[0]
    tq = tk = _tile(T, 256)
    wq = 128 * nq
    nqb = T // tq
    has_f = fcol is not None

    def body(*refs):
        if has_f:
            (q_ref, k_ref, v_ref, do_ref, o_ref, lr_ref, fc_ref, fr_ref,
             dq_ref, dk_ref, dv_ref, df_ref, dfq_ref, dk_sc, dv_sc, dqT_sc, d_sc, df_sc, dfq_sc) = refs
        else:
            q_ref, k_ref, v_ref, do_ref, o_ref, lr_ref, dq_ref, dk_ref, dv_ref, dk_sc, dv_sc, dqT_sc, d_sc = refs
        j = pl.program_id(1)
        diag = lax.broadcasted_iota(jnp.int32, (tk, 1), 0) <= lax.broadcasted_iota(jnp.int32, (1, tq), 1)
        hms = [_half_mask(h) for h in range(2)]

        @pl.when(j == 0)
        def _():
            dqT_sc[...] = jnp.zeros_like(dqT_sc)
            if has_f:
                dfq_sc[...] = jnp.zeros_like(dfq_sc)

            def delta(b, carry):
                off = pl.multiple_of(b * tq, tq)
                prod = do_ref[pl.ds(off, tq), :] * o_ref[pl.ds(off, tq), :]
                Ds = [jnp.sum(jnp.where(hms[h], prod, 0.0), axis=1, keepdims=True) for h in range(2)]
                d_sc[:, pl.ds(off, tq)] = _row_stat_tile(Ds[0], Ds[1], tq)
                return carry

            lax.fori_loop(0, nqb, delta, 0)

        kb = k_ref[...]
        vb = v_ref[...]
        if nq == 1:
            khs = [jnp.where(hms[h], kb, 0).astype(MXU) for h in range(2)]
        else:
            khs = [kb[:, 128 * h:128 * (h + 1)].astype(MXU) for h in range(2)]
        kTs = [kh.astype(F32).T.astype(MXU) for kh in khs]
        vhs = [jnp.where(hms[h], vb, 0).astype(MXU) for h in range(2)]
        fks = [_lane_pick(fc_ref[0], h) for h in range(2)] if has_f else None
        dv_sc[...] = jnp.zeros_like(dv_sc)
        dk_sc[...] = jnp.zeros_like(dk_sc)
        if has_f:
            df_sc[...] = jnp.zeros_like(df_sc)

        def make_step(masked):
            def step(i, carry):
                off = pl.multiple_of(i * tq, tq)
                qs = q_ref[pl.ds(off, tq), :]
                dos = do_ref[pl.ds(off, tq), :]
                for h in range(2):
                    qh = qs if nq == 1 else qs[:, 128 * h:128 * (h + 1)]
                    sT = _dot_nt(khs[h], qh) * scale
                    if has_f:
                        sT = sT + (fr_ref[0, h:h + 1, pl.ds(off, tq)] - fks[h])
                    pT = jnp.exp(sT - lr_ref[0, h:h + 1, pl.ds(off, tq)])
                    if masked:
                        pT = jnp.where(diag, pT, 0.0)
                    dsT = pT * (_dot_nt(vhs[h], dos) - d_sc[h:h + 1, pl.ds(off, tq)])
                    dv_sc[...] += _dot(pT, jnp.where(hms[h], dos, 0))
                    qq = jnp.where(hms[h], qs, 0) if nq == 1 else qh
                    dk_sc[h if nq == 2 else 0] += _dot(dsT, qq)
                    dqT_sc[h if nq == 2 else 0, :, pl.ds(off, tq)] += _dot(kTs[h], dsT)
                    if has_f:
                        part = dsT[:, 0:128]
                        for c in range(1, tq // 128):
                            part = part + dsT[:, 128 * c:128 * (c + 1)]
                        df_sc[h] += part
                        dfq_sc[h:h + 1, pl.ds(off, tq)] += jnp.sum(dsT, axis=0, keepdims=True)
                return carry
            return step

        make_step(True)(j, 0)
        lax.fori_loop(j + 1, nqb, make_step(False), 0)
        if nq == 1:
            dk_ref[...] = dk_sc[0] * scale
        else:
            dk_ref[...] = jnp.concatenate([dk_sc[0], dk_sc[1]], axis=1) * scale
        dv_ref[...] = dv_sc[...]
        if has_f:
            df_ref[0] = _col_stat_tile(-jnp.sum(df_sc[0], axis=1, keepdims=True),
                                       -jnp.sum(df_sc[1], axis=1, keepdims=True))

        @pl.when(j == nqb - 1)
        def _():
            if nq == 1:
                dq_ref[...] = dqT_sc[0].T * scale
            else:
                dq_ref[...] = jnp.concatenate([dqT_sc[0].T, dqT_sc[1].T], axis=1) * scale
            if has_f:
                sub = lax.broadcasted_iota(jnp.int32, (128, 1), 0)
                rows = jnp.where(sub == 0, dfq_sc[0:1, :], jnp.where(sub == 1, dfq_sc[1:2, :], 0.0))
                dfq_ref[0] = rows.T

    in_specs = [pl.BlockSpec((T, wq), lambda p, j: (0, qblk + p)),
                pl.BlockSpec((tk, wq), lambda p, j: (j, kblk + p)),
                pl.BlockSpec((tk, 128), lambda p, j: (j, vblk + p)),
                pl.BlockSpec((T, 128), lambda p, j: (0, p)),
                pl.BlockSpec((T, 128), lambda p, j: (0, p)),
                pl.BlockSpec((1, 8, T), lambda p, j: (p, 0, 0))]
    args = [q, k, v, do, o, lrow]
    out_specs = [pl.BlockSpec((T, wq), lambda p, j: (0, p)),
                 pl.BlockSpec((tk, wq), lambda p, j: (j, p)), pl.BlockSpec((tk, 128), lambda p, j: (j, p))]
    out_shape = [jax.ShapeDtypeStruct((T, 2 * wq), F32), jax.ShapeDtypeStruct((T, 2 * wq), F32),
                 jax.ShapeDtypeStruct((T, 256), F32)]
    scratch = [pltpu.VMEM((nq, tk, 128), F32), pltpu.VMEM((tk, 128), F32), pltpu.VMEM((nq, 128, T), F32),
               pltpu.VMEM((8, T), F32)]
    if has_f:
        in_specs += [pl.BlockSpec((1, tk, 128), lambda p, j: (p, j, 0)),
                     pl.BlockSpec((1, 8, T), lambda p, j: (p, 0, 0))]
        args += [fcol, frow]
        out_specs += [pl.BlockSpec((1, tk, 128), lambda p, j: (p, j, 0)),
                      pl.BlockSpec((1, T, 128), lambda p, j: (p, 0, 0))]
        out_shape += [jax.ShapeDtypeStruct((2, T, 128), F32), jax.ShapeDtypeStruct((2, T, 128), F32)]
        scratch += [pltpu.VMEM((2, tk, 128), F32), pltpu.VMEM((8, T), F32)]
    return pl.pallas_call(
        body, name=name, grid=(2, T // tk), in_specs=in_specs, out_specs=out_specs, out_shape=out_shape,
        scratch_shapes=scratch, compiler_params=_cparams(("arbitrary", "arbitrary")),
    )(*args)


def _swa_align(pair, e, h):
    sel = jnp.where(_half_mask(e), pair, 0.0)
    if e == h:
        return sel
    return pltpu.roll(sel, HEAD_DIM, 1)


def _swa_mask(n):
    W = WINDOW
    qi = lax.broadcasted_iota(jnp.int32, (W, 2 * W), 0) + W
    kj = lax.broadcasted_iota(jnp.int32, (W, 2 * W), 1)
    dist = qi - kj
    return (dist >= 0) & (dist < W) & ((n > 0) | (kj >= W))


def swa_fwd(proj, bias, sinks):
    T = proj.shape[0]
    W = WINDOW
    nb = T // W
    scale = HEAD_DIM ** -0.5

    def body(sink_ref, q_ref, kp_ref, kc_ref, vp_ref, vc_ref, b_ref, o_ref, l_ref):
        n = pl.program_id(0)
        mask = _swa_mask(n)
        kband = jnp.concatenate([kp_ref[...], kc_ref[...]], axis=0).astype(MXU)
        vband = jnp.concatenate([vp_ref[...], vc_ref[...]], axis=0).astype(MXU)
        lane = lax.broadcasted_iota(jnp.int32, (1, 128), 1)
        lse_tile = jnp.zeros((W, 128), F32)
        pairs = []
        for h in range(2):
            full = []
            for g in range(4):
                hq = 4 * h + g
                qa = _swa_align(q_ref[:, 128 * (hq // 2):128 * (hq // 2 + 1)], hq % 2, h)
                s = _dot_nt(qa, kband) * scale + b_ref[hq]
                s = jnp.where(mask, s, NEG)
                sink = sink_ref[hq]
                m = jnp.maximum(jnp.max(s, axis=1, keepdims=True), sink)
                e = jnp.exp(s - m)
                l = jnp.sum(e, axis=1, keepdims=True) + jnp.exp(sink - m)
                r = jnp.where(_half_mask(h), _dot(e, vband), 0.0) / l
                full.append(r + pltpu.roll(r, HEAD_DIM, 1))
                lse_tile = jnp.where(lane == hq, m + jnp.log(l), lse_tile)
            pairs.append(jnp.where(_half_mask(0), full[0], full[1]))
            pairs.append(jnp.where(_half_mask(0), full[2], full[3]))
        o_ref[...] = jnp.concatenate(pairs, axis=1)
        l_ref[...] = lse_tile

    prev = lambda n: (jnp.maximum(n - 1, 0), C_KA // 128)
    cur = lambda n: (n, C_KA // 128)
    prev_v = lambda n: (jnp.maximum(n - 1, 0), C_VA // 128)
    cur_v = lambda n: (n, C_VA // 128)
    return pl.pallas_call(
        body, name="swa_fwd", grid=(nb,),
        in_specs=[pl.BlockSpec(memory_space=pltpu.SMEM),
                  pl.BlockSpec((W, 512), lambda n: (n, 0)),
                  pl.BlockSpec((W, 128), prev), pl.BlockSpec((W, 128), cur),
                  pl.BlockSpec((W, 128), prev_v), pl.BlockSpec((W, 128), cur_v),
                  pl.BlockSpec((8, W, 2 * W), lambda n: (0, 0, 0))],
        out_specs=[pl.BlockSpec((W, 512), lambda n: (n, 0)), pl.BlockSpec((W, 128), lambda n: (n, 0))],
        out_shape=[jax.ShapeDtypeStruct((T, 512), F32), jax.ShapeDtypeStruct((T, 128), F32)],
        compiler_params=_cparams(("parallel",)),
    )(sinks, proj, proj, proj, proj, proj, bias)


def swa_bwd(proj, bias, sinks, do, o, lse):
    T = proj.shape[0]
    W = WINDOW
    nb = T // W
    scale = HEAD_DIM ** -0.5

    def body(sink_ref, q_ref, kp_ref, kc_ref, vp_ref, vc_ref, b_ref, do_ref, o_ref, l_ref,
             dq_ref, dk_ref, dv_ref, db_ref, dsk_ref, ck, cv):
        n = pl.program_id(0)

        @pl.when(n == 0)
        def _():
            ck[...] = jnp.zeros_like(ck)
            cv[...] = jnp.zeros_like(cv)
            db_ref[...] = jnp.zeros_like(db_ref)
            dsk_ref[...] = jnp.zeros_like(dsk_ref)

        @pl.when(n < nb)
        def _():
            mask = _swa_mask(n)
            kb32 = jnp.concatenate([kp_ref[...], kc_ref[...]], axis=0)
            vb32 = jnp.concatenate([vp_ref[...], vc_ref[...]], axis=0)
            kband = kb32.astype(MXU)
            sub = lax.broadcasted_iota(jnp.int32, (8, 1), 0)
            dk_band = jnp.zeros((2 * W, 128), F32)
            dv_band = jnp.zeros((2 * W, 128), F32)
            dsk = jnp.zeros((8, 128), F32)
            dq_pairs = []
            for h in range(2):
                hm = _half_mask(h)
                km = jnp.where(hm, kb32, 0.0).astype(MXU)
                vm = jnp.where(hm, vb32, 0.0).astype(MXU)
                full = []
                for g in range(4):
                    hq = 4 * h + g
                    pb = slice(128 * (hq // 2), 128 * (hq // 2 + 1))
                    e = hq % 2
                    qa = _swa_align(q_ref[:, pb], e, h)
                    dop = do_ref[:, pb]
                    doa = _swa_align(dop, e, h)
                    D = jnp.sum(jnp.where(_half_mask(e), dop * o_ref[:, pb], 0.0), axis=1, keepdims=True)
                    lse_h = _lane_pick(l_ref[...], hq)
                    s = _dot_nt(qa, kband) * scale + b_ref[hq]
                    p = jnp.where(mask, jnp.exp(s - lse_h), 0.0)
                    psink = jnp.exp(sink_ref[hq] - lse_h)
                    dsk = dsk + jnp.where(sub == hq, -jnp.sum(psink * D, axis=0, keepdims=True), 0.0)
                    dp = _dot_nt(doa, vm)
                    ds = p * (dp - D)
                    db_ref[hq] += ds
                    dq = _dot(ds, km) * scale
                    full.append(dq + pltpu.roll(dq, HEAD_DIM, 1))
                    dk_band = dk_band + _dot(ds.T, qa) * scale
                    dv_band = dv_band + _dot(p.T, doa)
                dq_pairs.append(jnp.where(_half_mask(0), full[0], full[1]))
                dq_pairs.append(jnp.where(_half_mask(0), full[2], full[3]))
            dq_ref[...] = jnp.concatenate(dq_pairs, axis=1)
            dsk_ref[...] += dsk
            dk_ref[...] = ck[...] + dk_band[0:W]
            dv_ref[...] = cv[...] + dv_band[0:W]
            ck[...] = dk_band[W:2 * W]
            cv[...] = dv_band[W:2 * W]

        @pl.when(n == nb)
        def _():
            dk_ref[...] = ck[...]
            dv_ref[...] = cv[...]

    cl = lambda n: jnp.minimum(n, nb - 1)
    pv = lambda n: jnp.maximum(jnp.minimum(n, nb - 1) - 1, 0)
    return pl.pallas_call(
        body, name="swa_bwd", grid=(nb + 1,),
        in_specs=[pl.BlockSpec(memory_space=pltpu.SMEM),
                  pl.BlockSpec((W, 512), lambda n: (cl(n), 0)),
                  pl.BlockSpec((W, 128), lambda n: (pv(n), C_KA // 128)),
                  pl.BlockSpec((W, 128), lambda n: (cl(n), C_KA // 128)),
                  pl.BlockSpec((W, 128), lambda n: (pv(n), C_VA // 128)),
                  pl.BlockSpec((W, 128), lambda n: (cl(n), C_VA // 128)),
                  pl.BlockSpec((8, W, 2 * W), lambda n: (0, 0, 0)),
                  pl.BlockSpec((W, 512), lambda n: (cl(n), 0)),
                  pl.BlockSpec((W, 512), lambda n: (cl(n), 0)),
                  pl.BlockSpec((W, 128), lambda n: (cl(n), 0))],
        out_specs=[pl.BlockSpec((W, 512), lambda n: (cl(n), 0)),
                   pl.BlockSpec((W, 128), lambda n: (jnp.maximum(n - 1, 0), 0)),
                   pl.BlockSpec((W, 128), lambda n: (jnp.maximum(n - 1, 0), 0)),
                   pl.BlockSpec((8, W, 2 * W), lambda n: (0, 0, 0)),
                   pl.BlockSpec((8, 128), lambda n: (0, 0))],
        out_shape=[jax.ShapeDtypeStruct((T, 512), F32), jax.ShapeDtypeStruct((T, 128), F32),
                   jax.ShapeDtypeStruct((T, 128), F32), jax.ShapeDtypeStruct((8, W, 2 * W), F32),
                   jax.ShapeDtypeStruct((8, 128), F32)],
        scratch_shapes=[pltpu.VMEM((W, 128), F32), pltpu.VMEM((W, 128), F32)],
        compiler_params=_cparams(("arbitrary",)),
    )(sinks, proj, proj, proj, proj, proj, bias, do, o, lse)


def swa_bias_table(rel_bias):
    W = WINDOW
    qi = jnp.arange(W, dtype=jnp.int32)[:, None] + W
    kj = jnp.arange(2 * W, dtype=jnp.int32)[None, :]
    dist = qi - kj
    max_exact = REL_BUCKETS // 2
    d = jnp.maximum(dist, 0)
    log_ratio = jnp.log(jnp.maximum(d, 1).astype(F32) / max_exact) / math.log(REL_MAX_DIST / max_exact)
    large = jnp.minimum(max_exact + (log_ratio * (REL_BUCKETS - max_exact)).astype(jnp.int32), REL_BUCKETS - 1)
    bucket = jnp.where(d < max_exact, d, large)
    bucket = bucket.reshape(-1)
    onehot = (bucket[None, :] == jnp.arange(REL_BUCKETS, dtype=jnp.int32)[:, None]).astype(F32)
    bias = jnp.dot(rel_bias.astype(F32).T, onehot, precision=lax.Precision.HIGHEST)
    return bias.reshape(SWA_Q_HEADS, W, 2 * W), bucket


def attn_out(oa, ob, oc, gn, wout, gpost, x):
    T = x.shape[0]
    tm = _tile(T, 512)

    def body(oa_ref, ob_ref, oc_ref, gn_ref, w_ref, gp_ref, x_ref, x2_ref, y_ref, mT_ref):
        g = gn_ref[...]
        mixed = jnp.concatenate([_rms_fwd(oa_ref[...], g[:, 0:512]), _rms_fwd(ob_ref[...], g[:, 512:768]),
                                 _rms_fwd(oc_ref[...], g[:, 768:1024])], axis=1)
        mT_ref[...] = mixed.T.astype(MXU)
        y = _dot(mixed, w_ref[...])
        y_ref[...] = y
        x2_ref[...] = x_ref[...] + _rms_fwd(y, gp_ref[...])

    row = lambda i: (i, 0)
    const = lambda i: (0, 0)
    return pl.pallas_call(
        body, name="attn_out", grid=(T // tm,),
        in_specs=[pl.BlockSpec((tm, 512), row), pl.BlockSpec((tm, 256), row), pl.BlockSpec((tm, 256), row),
                  pl.BlockSpec((1, 1024), const), pl.BlockSpec((1024, 1024), const), pl.BlockSpec((1, 1024), const),
                  pl.BlockSpec((tm, 1024), row)],
        out_specs=[pl.BlockSpec((tm, 1024), row), pl.BlockSpec((tm, 1024), row),
                   pl.BlockSpec((1024, tm), lambda i: (0, i))],
        out_shape=[jax.ShapeDtypeStruct((T, 1024), F32), jax.ShapeDtypeStruct((T, 1024), F32),
                   jax.ShapeDtypeStruct((1024, T), MXU)],
        compiler_params=_cparams(("parallel",)),
    )(oa, ob, oc, gn, wout, gpost, x)


def attn_out_bwd(dx2, y, oa, ob, oc, gn, wout, gpost):
    T = dx2.shape[0]
    tm = _tile(T, 512)

    def body(dx_ref, y_ref, oa_ref, ob_ref, oc_ref, gn_ref, w_ref, gp_ref,
             dy_ref, da_ref, db_ref, dc_ref, dgn_ref, dgp_ref):
        first = pl.program_id(0) == 0
        dy, dgp = _rms_bwd(dx_ref[...], y_ref[...], gp_ref[...])
        dy_ref[...] = dy.astype(MXU)
        _acc_out(dgp_ref, dgp, first)
        dm = _dot_nt(dy, w_ref[...])
        g = gn_ref[...]
        da, dga = _rms_bwd(dm[:, 0:512], oa_ref[...], g[:, 0:512])
        db, dgb = _rms_bwd(dm[:, 512:768], ob_ref[...], g[:, 512:768])
        dc, dgc = _rms_bwd(dm[:, 768:1024], oc_ref[...], g[:, 768:1024])
        da_ref[...] = da
        db_ref[...] = db
        dc_ref[...] = dc
        _acc_out(dgn_ref, jnp.concatenate([dga, dgb, dgc], axis=1), first)

    row = lambda i: (i, 0)
    const = lambda i: (0, 0)
    return pl.pallas_call(
        body, name="attn_out_bwd", grid=(T // tm,),
        in_specs=[pl.BlockSpec((tm, 1024), row), pl.BlockSpec((tm, 1024), row),
                  pl.BlockSpec((tm, 512), row), pl.BlockSpec((tm, 256), row), pl.BlockSpec((tm, 256), row),
                  pl.BlockSpec((1, 1024), const), pl.BlockSpec((1024, 1024), const), pl.BlockSpec((1, 1024), const)],
        out_specs=[pl.BlockSpec((tm, 1024), row), pl.BlockSpec((tm, 512), row), pl.BlockSpec((tm, 256), row),
                   pl.BlockSpec((tm, 256), row), pl.BlockSpec((1, 1024), const), pl.BlockSpec((1, 1024), const)],
        out_shape=[jax.ShapeDtypeStruct((T, 1024), MXU), jax.ShapeDtypeStruct((T, 512), F32),
                   jax.ShapeDtypeStruct((T, 256), F32), jax.ShapeDtypeStruct((T, 256), F32),
                   jax.ShapeDtypeStruct((1, 1024), F32), jax.ShapeDtypeStruct((1, 1024), F32)],
        compiler_params=_cparams(("arbitrary",)),
    )(dx2, y, oa, ob, oc, gn, wout, gpost)


FF_TILE = 256
_GELU_C = math.sqrt(2.0 / math.pi)


def _gelu(x):
    return 0.5 * x * (1.0 + jnp.tanh(_GELU_C * (x + 0.044715 * x * x * x)))


def _gelu_grad(x):
    t = jnp.tanh(_GELU_C * (x + 0.044715 * x * x * x))
    return 0.5 * (1.0 + t) + 0.5 * x * (1.0 - t * t) * _GELU_C * (1.0 + 3 * 0.044715 * x * x)


def _conv_taps(u, hal_ref, first):
    row = lax.broadcasted_iota(jnp.int32, (u.shape[0], 1), 0)
    h6 = jnp.where(first, 0.0, hal_ref[6:7, :])
    h7 = jnp.where(first, 0.0, hal_ref[7:8, :])
    r1 = jnp.where(row == 0, h7, pltpu.roll(u, 1, 0))
    r2 = jnp.where(row == 0, h6, jnp.where(row == 1, h7, pltpu.roll(u, 2, 0)))
    return r1, r2


def ffn_fwd(u0, convw, convb, wdown, gpost, x2):
    T = x2.shape[0]
    tm, tn = _tile(T, 1024), FF_TILE
    nj = D_FF // tn

    def body(ug_ref, uu_ref, hg_ref, hu_ref, wg_ref, wu_ref, bg_ref, bu_ref, wd_ref, gp_ref, x_ref,
             x3_ref, y_ref, aT_ref, acc):
        i, j = pl.program_id(0), pl.program_id(1)
        first = i == 0

        def conv(u_ref, h_ref, w_ref, b_ref):
            u = u_ref[...]
            r1, r2 = _conv_taps(u, h_ref, first)
            return b_ref[...] + w_ref[0:1, :] * r2 + w_ref[1:2, :] * r1 + w_ref[2:3, :] * u

        a = _gelu(conv(ug_ref, hg_ref, wg_ref, bg_ref)) * conv(uu_ref, hu_ref, wu_ref, bu_ref)
        aT_ref[...] = a.T.astype(MXU)
        _acc_out(acc, _dot(a, wd_ref[...]), j == 0)

        @pl.when(j == nj - 1)
        def _():
            y = acc[...]
            y_ref[...] = y
            x3_ref[...] = x_ref[...] + _rms_fwd(y, gp_ref[...])

    halo = lambda off: (lambda i, j: (jnp.maximum(i * (tm // 8) - 1, 0), off + j))
    return pl.pallas_call(
        body, name="ffn_fwd", grid=(T // tm, nj),
        in_specs=[pl.BlockSpec((tm, tn), lambda i, j: (i, j)), pl.BlockSpec((tm, tn), lambda i, j: (i, nj + j)),
                  pl.BlockSpec((8, tn), halo(0)), pl.BlockSpec((8, tn), halo(nj)),
                  pl.BlockSpec((3, tn), lambda i, j: (0, j)), pl.BlockSpec((3, tn), lambda i, j: (0, nj + j)),
                  pl.BlockSpec((1, tn), lambda i, j: (0, j)), pl.BlockSpec((1, tn), lambda i, j: (0, nj + j)),
                  pl.BlockSpec((tn, 1024), lambda i, j: (j, 0)),
                  pl.BlockSpec((1, 1024), lambda i, j: (0, 0)),
                  pl.BlockSpec((tm, 1024), lambda i, j: (i, 0))],
        out_specs=[pl.BlockSpec((tm, 1024), lambda i, j: (i, 0)), pl.BlockSpec((tm, 1024), lambda i, j: (i, 0)),
                   pl.BlockSpec((tn, tm), lambda i, j: (j, i))],
        out_shape=[jax.ShapeDtypeStruct((T, 1024), F32), jax.ShapeDtypeStruct((T, 1024), F32),
                   jax.ShapeDtypeStruct((D_FF, T), MXU)],
        scratch_shapes=[pltpu.VMEM((tm, 1024), F32)],
        compiler_params=_cparams(("parallel", "arbitrary")),
    )(u0, u0, u0, u0, convw, convw, convb, convb, wdown, gpost, x2)


def ffn_bwd(dx3, y, u0, convw, convb, wdown, gpost):
    T = dx3.shape[0]
    tm, tn = _tile(T, 1024), FF_TILE
    nj = D_FF // tn
    ni = T // tm

    def body(dx_ref, y_ref, ug_ref, uu_ref, hg_ref, hu_ref, wg_ref, wu_ref, bg_ref, bu_ref, wd_ref, gp_ref,
             dy_ref, dug_ref, duu_ref, dcg_ref, dcu_ref, dgp_ref, dy_sc, cg, cu, ag, au):
        s, j = pl.program_id(0), pl.program_id(1)
        i = ni - 1 - s
        first_tok = i == 0
        row = lax.broadcasted_iota(jnp.int32, (tm, 1), 0)
        sub = lax.broadcasted_iota(jnp.int32, (8, 1), 0)

        @pl.when(j == 0)
        def _():
            dy, dgp = _rms_bwd(dx_ref[...], y_ref[...], gp_ref[...])
            dy_sc[...] = dy.astype(MXU)
            dy_ref[...] = dy.astype(MXU)
            _acc_out(dgp_ref, dgp, s == 0)

        @pl.when(s == 0)
        def _():
            cg[j] = jnp.zeros((8, tn), F32)
            cu[j] = jnp.zeros((8, tn), F32)
            ag[j] = jnp.zeros((8, tn), F32)
            au[j] = jnp.zeros((8, tn), F32)

        da = _dot_nt(dy_sc[...], wd_ref[...])

        def conv(u_ref, h_ref, w_ref, b_ref):
            u = u_ref[...]
            r1, r2 = _conv_taps(u, h_ref, first_tok)
            return b_ref[...] + w_ref[0:1, :] * r2 + w_ref[1:2, :] * r1 + w_ref[2:3, :] * u, u, r1, r2

        gate, ugv, g1, g2 = conv(ug_ref, hg_ref, wg_ref, bg_ref)
        up, uuv, u1, u2 = conv(uu_ref, hu_ref, wu_ref, bu_ref)
        gl = _gelu(gate)
        dup = da * gl
        dgate = da * up * _gelu_grad(gate)

        def conv_bwd(du, u, r1, r2, w_ref, c_ref, a_ref, du_ref):
            nxt = c_ref[j]
            n0, n1 = nxt[0:1, :], nxt[1:2, :]
            f1 = jnp.where(row == tm - 1, n0, pltpu.roll(du, tm - 1, 0))
            f2 = jnp.where(row == tm - 1, n1, jnp.where(row == tm - 2, n0, pltpu.roll(du, tm - 2, 0)))
            du_ref[...] = (w_ref[2:3, :] * du + w_ref[1:2, :] * f1 + w_ref[0:1, :] * f2).astype(MXU)
            c_ref[j] = du[0:8, :]
            red = lambda v: jnp.sum(v, axis=0, keepdims=True)
            part = jnp.where(sub == 0, red(du * r2), jnp.where(sub == 1, red(du * r1), jnp.where(
                sub == 2, red(du * u), jnp.where(sub == 3, red(du), 0.0))))
            a_ref[j] = a_ref[j] + part
            return a_ref[j]

        dcg_ref[0] = conv_bwd(dgate, ugv, g1, g2, wg_ref, cg, ag, dug_ref)
        dcu_ref[0] = conv_bwd(dup, uuv, u1, u2, wu_ref, cu, au, duu_ref)

    rev = lambda s: ni - 1 - s
    halo = lambda off: (lambda s, j: (jnp.maximum(rev(s) * (tm // 8) - 1, 0), off + j))
    return pl.pallas_call(
        body, name="ffn_bwd", grid=(ni, nj),
        in_specs=[pl.BlockSpec((tm, 1024), lambda s, j: (rev(s), 0)), pl.BlockSpec((tm, 1024), lambda s, j: (rev(s), 0)),
                  pl.BlockSpec((tm, tn), lambda s, j: (rev(s), j)), pl.BlockSpec((tm, tn), lambda s, j: (rev(s), nj + j)),
                  pl.BlockSpec((8, tn), halo(0)), pl.BlockSpec((8, tn), halo(nj)),
                  pl.BlockSpec((3, tn), lambda s, j: (0, j)), pl.BlockSpec((3, tn), lambda s, j: (0, nj + j)),
                  pl.BlockSpec((1, tn), lambda s, j: (0, j)), pl.BlockSpec((1, tn), lambda s, j: (0, nj + j)),
                  pl.BlockSpec((tn, 1024), lambda s, j: (j, 0)),
                  pl.BlockSpec((1, 1024), lambda s, j: (0, 0))],
        out_specs=[pl.BlockSpec((tm, 1024), lambda s, j: (rev(s), 0)),
                   pl.BlockSpec((tm, tn), lambda s, j: (rev(s), j)), pl.BlockSpec((tm, tn), lambda s, j: (rev(s), j)),
                   pl.BlockSpec((1, 8, tn), lambda s, j: (s, 0, j)), pl.BlockSpec((1, 8, tn), lambda s, j: (s, 0, j)),
                   pl.BlockSpec((1, 1024), lambda s, j: (0, 0))],
        out_shape=[jax.ShapeDtypeStruct((T, 1024), MXU), jax.ShapeDtypeStruct((T, D_FF), MXU),
                   jax.ShapeDtypeStruct((T, D_FF), MXU),
                   jax.ShapeDtypeStruct((ni, 8, D_FF), F32), jax.ShapeDtypeStruct((ni, 8, D_FF), F32),
                   jax.ShapeDtypeStruct((1, 1024), F32)],
        scratch_shapes=[pltpu.VMEM((tm, 1024), MXU)] + [pltpu.VMEM((nj, 8, tn), F32)] * 4,
        compiler_params=_cparams(("arbitrary", "arbitrary")),
    )(dx3, y, u0, u0, u0, u0, convw, convw, convb, convb, wdown, gpost)


ELEMS_PER_BLOCK = 512 * 1024


def _row_block(R, C):
    if R * C <= ELEMS_PER_BLOCK or R % 8:
        return R
    best = 8
    for t in range(8, R + 1, 8):
        if R % t == 0 and t * C <= ELEMS_PER_BLOCK:
            best = t
    return best


def adamw(w, g, m, v, name):
    R, C = w.shape
    partials = g.ndim == 3
    tr = _row_block(R, 2 * C)
    c1 = 1.0 - ADAM_B1 ** ADAM_STEP
    c2 = 1.0 - ADAM_B2 ** ADAM_STEP

    def body(w_ref, g_ref, m_ref, v_ref, g_out, d_ref, nm_ref, nv_ref):
        if partials:
            gv = g_ref[0].astype(F32)
            for d in range(1, N_DEV):
                gv = gv + g_ref[d].astype(F32)
        else:
            gv = g_ref[...]
        g_out[...] = gv
        nm = ADAM_B1 * m_ref[...] + (1.0 - ADAM_B1) * gv
        nv = ADAM_B2 * v_ref[...] + (1.0 - ADAM_B2) * (gv * gv)
        nm_ref[...] = nm
        nv_ref[...] = nv
        d_ref[...] = -ADAM_LR * ((nm / c1) / (jnp.sqrt(nv / c2) + ADAM_EPS) + ADAM_WD * w_ref[...])

    spec = pl.BlockSpec((tr, C), lambda i: (i, 0))
    gspec = pl.BlockSpec((N_DEV, tr, C), lambda i: (0, i, 0)) if partials else spec
    return pl.pallas_call(
        body, name=name, grid=(R // tr,), in_specs=[spec, gspec, spec, spec], out_specs=[spec] * 4,
        out_shape=[jax.ShapeDtypeStruct((R, C), F32)] * 4,
        compiler_params=_cparams(("parallel",)),
    )(w, g, m, v)


def sum_devices(buf, name):
    _, R, C = buf.shape
    tr = _row_block(R, C * 4)

    def body(b_ref, o_ref):
        acc = b_ref[0].astype(F32)
        for d in range(1, N_DEV):
            acc = acc + b_ref[d].astype(F32)
        o_ref[...] = acc

    return pl.pallas_call(
        body, name=name, grid=(R // tr,),
        in_specs=[pl.BlockSpec((N_DEV, tr, C), lambda i: (0, i, 0))],
        out_specs=pl.BlockSpec((tr, C), lambda i: (i, 0)),
        out_shape=jax.ShapeDtypeStruct((R, C), F32),
        compiler_params=_cparams(("parallel",)),
    )(buf)


def exchange(srcs, name, gather):
    n = len(srcs)
    if gather:
        shapes = [(N_DEV,) + s.shape for s in srcs]
    else:
        shapes = [(N_DEV, s.shape[0]) + s.shape[2:] for s in srcs]

    def body(*refs):
        src_refs, out_refs = refs[:n], refs[n:2 * n]
        send_sems, recv_sems, local_sems = refs[2 * n:]
        x, y, c = lax.axis_index("x"), lax.axis_index("y"), lax.axis_index("c")
        me = 4 * x + 2 * y + c
        flip = lambda a, bit: 1 - a if bit else a
        part = lambda ref, d: ref if gather else ref.at[:, d]

        def copy(t, k):
            px, py, pc = flip(x, (k >> 2) & 1), flip(y, (k >> 1) & 1), flip(c, k & 1)
            peer = 4 * px + 2 * py + pc
            sem = t * (N_DEV - 1) + k - 1
            outgoing = pltpu.make_async_remote_copy(
                src_ref=part(src_refs[t], peer), dst_ref=out_refs[t].at[me],
                send_sem=send_sems.at[sem], recv_sem=recv_sems.at[sem],
                device_id=(px, py, pc), device_id_type=pl.DeviceIdType.MESH)
            incoming = pltpu.make_async_remote_copy(
                src_ref=part(src_refs[t], me), dst_ref=out_refs[t].at[peer],
                send_sem=send_sems.at[sem], recv_sem=recv_sems.at[sem],
                device_id=(px, py, pc), device_id_type=pl.DeviceIdType.MESH)
            return outgoing, incoming

        copies = [copy(t, k) for k in range(1, N_DEV) for t in range(n)]
        for outgoing, _ in copies:
            outgoing.start()
        mine = [pltpu.make_async_copy(part(src_refs[t], me), out_refs[t].at[me], local_sems.at[t]) for t in range(n)]
        for cp in mine:
            cp.start()
        for _, incoming in copies:
            incoming.wait_recv()
        for outgoing, _ in copies:
            outgoing.wait_send()
        for cp in mine:
            cp.wait()

    return pl.pallas_call(
        body, name=name,
        in_specs=[pl.BlockSpec(memory_space=pl.ANY)] * n, out_specs=[pl.BlockSpec(memory_space=pl.ANY)] * n,
        out_shape=[jax.ShapeDtypeStruct(shp, s.dtype) for shp, s in zip(shapes, srcs)],
        scratch_shapes=[pltpu.SemaphoreType.DMA((n * (N_DEV - 1),)), pltpu.SemaphoreType.DMA((n * (N_DEV - 1),)),
                        pltpu.SemaphoreType.DMA((n,))],
    )(*srcs)


def _pack(parts, cols, row_align, dtype):
    flat = jnp.concatenate([p.astype(dtype) for p in parts], axis=-1)
    n = flat.shape[-1]
    block = cols * row_align
    total = -(-n // block) * block
    flat = jnp.pad(flat, [(0, 0)] * (flat.ndim - 1) + [(0, total - n)])
    return flat.reshape(flat.shape[:-1] + (total // cols, cols))


def _unpack(buf, shapes):
    lead = buf.shape[:-2]
    flat = buf.reshape(lead + (-1,))
    out, off = [], 0
    for s in shapes:
        n = int(np.prod(s))
        out.append(flat[..., off:off + n].reshape(lead + tuple(s)))
        off += n
    return out


SHARD_SHAPES = [(128, IN_COLS), (256, 48), (128, 64), (128, 1024), (1024, 704), (352, 1024)]
SHARDED = ["w_in", "w_uq", "w_ukv", "w_out", "w_up", "w_down"]


def _full_from_shards(name, s):
    if name in ("w_in", "w_out", "w_down"):
        return s.reshape((-1, s.shape[-1]))
    return s.transpose(1, 0, 2).reshape((s.shape[1], -1))


def _shards_from_full(name, f):
    if name in ("w_in", "w_out", "w_down"):
        return f.reshape((N_DEV, -1, f.shape[-1]))
    return f.reshape((f.shape[0], N_DEV, -1)).transpose(1, 0, 2)


def _perm_w_in(w):
    z = lambda n: jnp.zeros((w.shape[0], n), w.dtype)
    return jnp.concatenate([w[:, :1536], w[:, 1540:1924], w[:, 1536:1540], z(60), w[:, 1924:1956], z(32)], axis=1)


def _unperm_w_in(d):
    return jnp.concatenate([d[:, :1536], d[:, 1920:1924], d[:, 1536:1920], d[:, 1984:2016]], axis=1)


def _perm_w_uq(w):
    return jnp.pad(w.reshape(256, 4, MLA_QK_DIM), ((0, 0), (0, 0), (0, 128 - MLA_QK_DIM))).reshape(256, 512)


def _unperm_w_uq(d):
    return d.reshape(256, 4, 128)[:, :, :MLA_QK_DIM].reshape(256, 4 * MLA_QK_DIM)


def _perm_w_ukv(w):
    w4 = w.reshape(128, 4, 128)
    k = jnp.pad(w4[:, :, :64], ((0, 0), (0, 0), (0, 64))).reshape(128, 512)
    return jnp.concatenate([k, w4[:, :, 64:].reshape(128, 256)], axis=1)


def _unperm_w_ukv(d):
    dk = d[:, :512].reshape(128, 4, 128)[:, :, :64]
    dv = d[:, 512:].reshape(128, 4, 64)
    return jnp.concatenate([dk, dv], axis=-1).reshape(128, 512)


def _row(v, width=None):
    v = v.reshape(1, -1).astype(F32)
    if width is not None and v.shape[1] < width:
        v = jnp.pad(v, ((0, 0), (0, width - v.shape[1])))
    return v


def _layer_fwd(x, P, shared):
    cosr, sinr, bias = shared
    proj, hT, projb = norm_matmul(x, P["g_pre"], P["w_in_p"], "in_proj", lo_tiles=C_CQ // 512)
    qm, km, vm, cqT, ckvT = mla_prep(proj, P["gq"], P["gkv"], P["w_uq_p"], P["w_ukv_p"], cosr, sinr)
    fcol, frow, frep = fox_gate(proj, P["fbias"])
    oa, lse_a = swa_fwd(proj, bias, P["sinks"])
    ob, lrb = flash_fwd(projb, projb, projb, frep, frow, qblk=C_QF // 128, kblk=C_KF // 128, vblk=C_VF // 128,
                             nq=1, scale=HEAD_DIM ** -0.5, name="fox_fwd")
    oc, lrc = flash_fwd(qm, km, vm, None, None, qblk=0, kblk=0, vblk=0, nq=2,
                             scale=MLA_QK_DIM ** -0.5, name="mla_fwd")
    x2, y1, mT = attn_out(oa, ob, oc, P["gn"], P["w_out"], P["g_apost"], x)
    u0, h2T = norm_matmul(x2, P["g_fpre"], P["w_up"], "up_proj", tn_pref=1536)
    x3, y2, aT = ffn_fwd(u0, P["conv_w"], P["conv_b"], P["w_down"], P["g_fpost"], x2)
    S = dict(x=x, proj=proj, projb=projb, hT=hT, qm=qm, km=km, vm=vm, cqT=cqT, ckvT=ckvT, fcol=fcol, frow=frow,
             oa=oa, lse_a=lse_a, ob=ob, lrb=lrb, oc=oc, lrc=lrc,
             x2=x2, y1=y1, mT=mT, u0=u0, h2T=h2T, y2=y2, aT=aT)
    return x3, S


def _layer_bwd(dx3, P, S, shared):
    cosr, sinr, bias = shared
    proj = S["proj"]
    G = {}
    dy2, dug, duu, dcg, dcu, G["ffn_post_norm"] = ffn_bwd(dx3, S["y2"], S["u0"], P["conv_w"], P["conv_b"],
                                                         P["w_down"], P["g_fpost"])
    du0 = jnp.concatenate([dug, duu], axis=1)
    dconv = jnp.concatenate([dcg[-1], dcu[-1]], axis=1)
    G["conv_w"], G["conv_b"] = dconv[0:3], dconv[3]
    G["w_down"] = matmul_nn(S["aT"], dy2, "dw_down", MXU)
    G["w_up"] = matmul_nn(S["h2T"], du0, "dw_up", MXU)
    dx2, G["ffn_pre_norm"] = matmul_nt_normbwd(du0, P["w_up"], S["x2"], P["g_fpre"], dx3, "up_bwd")
    dy1, doa, dob, doc, G["group_norm"], G["attn_post_norm"] = attn_out_bwd(
        dx2, S["y1"], S["oa"], S["ob"], S["oc"], P["gn"], P["w_out"], P["g_apost"])
    G["w_out"] = matmul_nn(S["mT"], dy1, "dw_out", MXU)
    dqa, dka, dva, dbias, dsk = swa_bwd(proj, bias, P["sinks"], doa, S["oa"], S["lse_a"])
    G["swa_sinks"] = dsk[:, 0]
    pb = S["projb"]
    dqf, dkf, dvf, dFk, dFq = flash_bwd(pb, pb, pb, dob, S["ob"], S["lrb"], S["fcol"], S["frow"], name="fox_bwd",
                                        qblk=C_QF // 128, kblk=C_KF // 128, vblk=C_VF // 128, nq=1,
                                        scale=HEAD_DIM ** -0.5)
    dmisc_f, dfb = fox_gate_bwd(dFq, dFk, proj, P["fbias"])
    G["forget_bias"] = dfb[0, 0:4]
    dqm_, dkm_, dvm_ = flash_bwd(S["qm"], S["km"], S["vm"], doc, S["oc"], S["lrc"], None, None, name="mla_bwd",
                                 qblk=0, kblk=0, vblk=0, nq=2, scale=MLA_QK_DIM ** -0.5)
    dqm, dkv, dcq, dckv, dmisc_r, G["q_latent_norm"], G["kv_latent_norm"] = mla_prep_bwd(
        dqm_, dkm_, dvm_, proj, P["gq"], P["gkv"], P["w_uq_p"], P["w_ukv_p"], cosr, sinr)
    G["w_uq"] = _unperm_w_uq(matmul_nn(S["cqT"], dqm, "dw_uq", MXU))
    G["w_ukv"] = _unperm_w_ukv(matmul_nn(S["ckvT"], dkv, "dw_ukv", MXU))
    dproj = jnp.concatenate([dqa, dka, dva, dqf, dkf, dvf, dcq, dckv, dmisc_f + dmisc_r], axis=1).astype(MXU)
    G["w_in"] = _unperm_w_in(matmul_nn(S["hT"], dproj, "dw_in", MXU))
    dx, G["attn_pre_norm"] = matmul_nt_normbwd(dproj, P["w_in_p"], S["x"], P["g_pre"], dx2, "in_bwd")
    return dx, G, dbias


def _layer_params(l, full, small):
    return dict(
        g_pre=_row(small["attn_pre_norm"][l]), w_in_p=_perm_w_in(full["w_in"]),
        gq=_row(small["q_latent_norm"][l]), gkv=_row(small["kv_latent_norm"][l]),
        w_uq_p=_perm_w_uq(full["w_uq"]), w_ukv_p=_perm_w_ukv(full["w_ukv"]),
        fbias=_row(small["forget_bias"][l], 128), sinks=small["swa_sinks"][l].astype(F32),
        gn=_row(small["group_norm"][l]), w_out=full["w_out"], g_apost=_row(small["attn_post_norm"][l]),
        g_fpre=_row(small["ffn_pre_norm"][l]), w_up=full["w_up"], conv_w=full["conv_w"],
        conv_b=_row(small["conv_b"][l]), w_down=full["w_down"], g_fpost=_row(small["ffn_post_norm"][l]))


def _rel_bias_grad(dbias, bucket):
    flat = dbias.reshape(SWA_Q_HEADS, -1)
    hi = flat.astype(MXU)
    lo = (flat - hi.astype(F32)).astype(MXU)
    onehot = (bucket[:, None] == jnp.arange(128, dtype=jnp.int32)[None, :]).astype(MXU)
    r = matmul_nn(jnp.concatenate([hi, lo], axis=0), onehot, "rel_bias_grad")
    return (r[0:8] + r[8:16])[:, :REL_BUCKETS].T


def local_step(x, tgt, fulls, small):
    T = x.shape[0]
    cosr, sinr = rope_tables(T)
    bias, bucket = swa_bias_table(small["rel_bias"])
    shared = (cosr, sinr, bias)
    Ps, Ss = [], []
    h = x
    for l in range(DEPTH):
        P = _layer_params(l, fulls[l], small)
        h, S = _layer_fwd(h, P, shared)
        Ps.append(P)
        Ss.append(S)
    dh, sq = loss_kernel(h, tgt)
    grads = [None] * DEPTH
    dbias_sum = None
    for l in reversed(range(DEPTH)):
        dh, grads[l], dbias = _layer_bwd(dh, Ps[l], Ss[l], shared)
        dbias_sum = dbias if dbias_sum is None else dbias_sum + dbias
    return sq, dh, grads, _rel_bias_grad(dbias_sum, bucket)


WEIGHTS = ['attn_pre_norm', 'w_in', 'forget_bias', 'swa_sinks', 'rel_bias', 'q_latent_norm', 'w_uq',
           'kv_latent_norm', 'w_ukv', 'group_norm', 'w_out', 'attn_post_norm', 'ffn_pre_norm', 'w_up', 'conv_w',
           'conv_b', 'w_down', 'ffn_post_norm']
SMALL_PER_LAYER = ['attn_pre_norm', 'forget_bias', 'swa_sinks', 'q_latent_norm', 'kv_latent_norm', 'group_norm',
                   'attn_post_norm', 'ffn_pre_norm', 'conv_b', 'ffn_post_norm', 'conv_w']


def kernel(x, attn_pre_norm, w_in, forget_bias, swa_sinks, rel_bias, q_latent_norm, w_uq, kv_latent_norm, w_ukv, group_norm, w_out, attn_post_norm, ffn_pre_norm, w_up, conv_w, conv_b, w_down, ffn_post_norm, loss_target, m_attn_pre_norm, m_w_in, m_forget_bias, m_swa_sinks, m_rel_bias, m_q_latent_norm, m_w_uq, m_kv_latent_norm, m_w_ukv, m_group_norm, m_w_out, m_attn_post_norm, m_ffn_pre_norm, m_w_up, m_conv_w, m_conv_b, m_w_down, m_ffn_post_norm, v_attn_pre_norm, v_w_in, v_forget_bias, v_swa_sinks, v_rel_bias, v_q_latent_norm, v_w_uq, v_kv_latent_norm, v_w_ukv, v_group_norm, v_w_out, v_attn_post_norm, v_ffn_pre_norm, v_w_up, v_conv_w, v_conv_b, v_w_down, v_ffn_post_norm):
    W = dict(attn_pre_norm=attn_pre_norm, w_in=w_in, forget_bias=forget_bias, swa_sinks=swa_sinks, rel_bias=rel_bias,
             q_latent_norm=q_latent_norm, w_uq=w_uq, kv_latent_norm=kv_latent_norm, w_ukv=w_ukv,
             group_norm=group_norm, w_out=w_out, attn_post_norm=attn_post_norm, ffn_pre_norm=ffn_pre_norm,
             w_up=w_up, conv_w=conv_w, conv_b=conv_b, w_down=w_down, ffn_post_norm=ffn_post_norm)
    M = dict(attn_pre_norm=m_attn_pre_norm, w_in=m_w_in, forget_bias=m_forget_bias, swa_sinks=m_swa_sinks,
             rel_bias=m_rel_bias, q_latent_norm=m_q_latent_norm, w_uq=m_w_uq, kv_latent_norm=m_kv_latent_norm,
             w_ukv=m_w_ukv, group_norm=m_group_norm, w_out=m_w_out, attn_post_norm=m_attn_post_norm,
             ffn_pre_norm=m_ffn_pre_norm, w_up=m_w_up, conv_w=m_conv_w, conv_b=m_conv_b, w_down=m_w_down,
             ffn_post_norm=m_ffn_post_norm)
    V = dict(attn_pre_norm=v_attn_pre_norm, w_in=v_w_in, forget_bias=v_forget_bias, swa_sinks=v_swa_sinks,
             rel_bias=v_rel_bias, q_latent_norm=v_q_latent_norm, w_uq=v_w_uq, kv_latent_norm=v_kv_latent_norm,
             w_ukv=v_w_ukv, group_norm=v_group_norm, w_out=v_w_out, attn_post_norm=v_attn_post_norm,
             ffn_pre_norm=v_ffn_pre_norm, w_up=v_w_up, conv_w=v_conv_w, conv_b=v_conv_b, w_down=v_w_down,
             ffn_post_norm=v_ffn_post_norm)
    me = 4 * lax.axis_index("x") + 2 * lax.axis_index("y") + lax.axis_index("c")

    fulls = []
    for l in range(DEPTH):
        got = exchange([W[n][l].astype(MXU) for n in SHARDED] + [conv_w[l]], "gather_weights", True)
        full = {n: _full_from_shards(n, s) for n, s in zip(SHARDED, got[:-1])}
        full["conv_w"] = got[-1].transpose(1, 0, 2).reshape(3, 2 * D_FF)
        fulls.append(full)

    sq, dx, grads, drel = local_step(x[0], loss_target[0], fulls, W)

    stacked = [jnp.stack([_shards_from_full(n, grads[l][n]) for l in range(DEPTH)]) for n in SHARDED]
    G = dict(zip(SHARDED, exchange(stacked, "scatter_grads", False)))

    parts, shapes = [], []
    for l in range(DEPTH):
        for n in SMALL_PER_LAYER:
            parts.append(grads[l][n].astype(F32).reshape(-1))
            shapes.append(grads[l][n].shape)
    parts += [drel.reshape(-1), jnp.sum(sq).reshape(1) * (0.5 / D_MODEL)]
    shapes += [drel.shape, (1,)]
    red = _unpack(sum_devices(exchange([_pack(parts, 128, 8, F32)], "gather_small", True)[0], "sum_small"), shapes)
    k = 0
    per = {n: [] for n in SMALL_PER_LAYER}
    for l in range(DEPTH):
        for n in SMALL_PER_LAYER:
            per[n].append(red[k])
            k += 1
    for n in SMALL_PER_LAYER:
        G[n] = jnp.stack(per[n]).reshape((DEPTH, 3, 2 * D_FF) if n == "conv_w" else W[n].shape)
    G["rel_bias"] = red[k]
    loss = red[k + 1][0]
    G["conv_w"] = lax.dynamic_slice_in_dim(G["conv_w"], me * 704, 704, axis=2)

    delta, new_m, new_v = {}, {}, {}
    for n in WEIGHTS:
        shp = W[n].shape
        v2 = lambda a: a.reshape(-1, shp[-1])
        g = G[n].reshape(N_DEV, -1, shp[-1]) if n in SHARDED else v2(G[n])
        g, d, nm, nv = adamw(v2(W[n]), g, v2(M[n]), v2(V[n]), "adamw_" + n)
        G[n], delta[n], new_m[n], new_v[n] = g.reshape(shp), d.reshape(shp), nm.reshape(shp), nv.reshape(shp)
    return (loss, dx[None], *[G[n] for n in WEIGHTS], *[delta[n] for n in WEIGHTS],
            *[new_m[n] for n in WEIGHTS], *[new_v[n] for n in WEIGHTS])
```

```python
import functools
import math

import numpy as np
import jax
import jax.numpy as jnp
from jax import lax
from jax.experimental import pallas as pl
from jax.experimental.pallas import tpu as pltpu

F32 = jnp.float32
MXU = jnp.bfloat16

N_DEV = 8
DEPTH = 4
D_MODEL = 1024
HEAD_DIM = 64
WINDOW = 128
SWA_Q_HEADS = 8
REL_BUCKETS = 32
REL_MAX_DIST = 128
MLA_QK_DIM = 96
ROPE_DIM = 32
ROPE_THETA = 10000.0
D_FF = 2816
EPS = 1e-6
NEG = -1e30
IN_COLS = 1956
IN_COLS_P = 2048
C_QA, C_KA, C_VA = 0, 512, 640
C_QF, C_KF, C_VF = 768, 1024, 1280
C_CQ, C_CKV, C_MISC = 1536, 1792, 1920
ROPE_LANE0 = 64
ADAM_LR, ADAM_B1, ADAM_B2, ADAM_EPS, ADAM_WD, ADAM_STEP = 0.001, 0.9, 0.999, 1e-08, 0.01, 10

VMEM_LIMIT = 56 * 1024 * 1024
PACK_COLS = 1024
PACK_ROW_ALIGN = 16


def _cparams(sem=None):
    return pltpu.CompilerParams(dimension_semantics=sem, vmem_limit_bytes=VMEM_LIMIT)


def _tile(n, pref):
    if n <= pref:
        return n
    t = pref - pref % 128
    while t >= 128:
        if n % t == 0:
            return t
        t -= 128
    return n


def _dot(a, b):
    return jnp.dot(a.astype(MXU), b.astype(MXU), preferred_element_type=F32)


def _dot_nt(a, b):
    return lax.dot_general(a.astype(MXU), b.astype(MXU), (((1,), (1,)), ((), ())),
                           preferred_element_type=F32)


def _rms_fwd(x, g):
    return x * lax.rsqrt(jnp.mean(x * x, axis=-1, keepdims=True) + EPS) * g


def _rms_bwd(dy, x, g, n=None):
    r = lax.rsqrt(jnp.mean(x * x, axis=-1, keepdims=True) + EPS)
    xh = x * r
    dg = jnp.sum(dy * xh, axis=0, keepdims=True)
    dxh = dy * g
    dx = r * (dxh - xh * jnp.mean(dxh * xh, axis=-1, keepdims=True))
    return dx, dg


def _acc_out(ref, val, first):
    @pl.when(first)
    def _():
        ref[...] = val

    @pl.when(jnp.logical_not(first))
    def _():
        ref[...] += val


def norm_matmul(x, g, w, name, lo_tiles=0, tn_pref=512):
    T, K = x.shape
    N = w.shape[1]
    tm, tn = _tile(T, 1024), _tile(N, tn_pref)

    def body(x_ref, g_ref, w_ref, o_ref, hT_ref, *rest):
        h_sc = rest[-1]
        j = pl.program_id(1)

        @pl.when(j == 0)
        def _():
            h = _rms_fwd(x_ref[...], g_ref[...])
            h_sc[...] = h.astype(MXU)
            hT_ref[...] = h.T.astype(MXU)

        r = jnp.dot(h_sc[...], w_ref[...], preferred_element_type=F32)
        o_ref[...] = r
        if lo_tiles:
            @pl.when(j < lo_tiles)
            def _():
                rest[0][...] = r.astype(MXU)

    out_specs = [pl.BlockSpec((tm, tn), lambda i, j: (i, j)), pl.BlockSpec((K, tm), lambda i, j: (0, i))]
    out_shape = [jax.ShapeDtypeStruct((T, N), F32), jax.ShapeDtypeStruct((K, T), MXU)]
    if lo_tiles:
        out_specs.append(pl.BlockSpec((tm, tn), lambda i, j: (i, jnp.minimum(j, lo_tiles - 1))))
        out_shape.append(jax.ShapeDtypeStruct((T, lo_tiles * tn), MXU))
    return pl.pallas_call(
        body, name=name, grid=(T // tm, N // tn),
        in_specs=[pl.BlockSpec((tm, K), lambda i, j: (i, 0)),
                  pl.BlockSpec((1, K), lambda i, j: (0, 0)),
                  pl.BlockSpec((K, tn), lambda i, j: (0, j))],
        out_specs=out_specs, out_shape=out_shape,
        scratch_shapes=[pltpu.VMEM((tm, K), MXU)],
        compiler_params=_cparams(("parallel", "arbitrary")),
    )(x, g, w)


def matmul_nn(a, b, name, out_dtype=F32):
    M, K = a.shape
    N = b.shape[1]
    tm, tn, tk = _tile(M, 1024), _tile(N, 1536), _tile(K, 1024)
    nk = K // tk

    def body(a_ref, b_ref, o_ref, acc):
        k = pl.program_id(2)
        part = _dot(a_ref[...], b_ref[...])
        _acc_out(acc, part, k == 0)

        @pl.when(k == nk - 1)
        def _():
            o_ref[...] = acc[...].astype(out_dtype)

    return pl.pallas_call(
        body, name=name, grid=(M // tm, N // tn, nk),
        in_specs=[pl.BlockSpec((tm, tk), lambda i, j, k: (i, k)),
                  pl.BlockSpec((tk, tn), lambda i, j, k: (k, j))],
        out_specs=pl.BlockSpec((tm, tn), lambda i, j, k: (i, j)),
        out_shape=jax.ShapeDtypeStruct((M, N), out_dtype),
        scratch_shapes=[pltpu.VMEM((tm, tn), F32)],
        compiler_params=_cparams(("parallel", "parallel", "arbitrary")),
    )(a, b)


def matmul_nt_normbwd(dy, w, x, g, dres, name):
    T, N = dy.shape
    K = w.shape[0]
    tm, tn = _tile(T, 1024), _tile(N, 1536)
    nj = N // tn

    def body(dy_ref, w_ref, x_ref, g_ref, dres_ref, dx_ref, dg_ref, acc):
        i, j = pl.program_id(0), pl.program_id(1)
        _acc_out(acc, _dot_nt(dy_ref[...], w_ref[...]), j == 0)

        @pl.when(j == nj - 1)
        def _():
            dx, dg = _rms_bwd(acc[...], x_ref[...], g_ref[...])
            dx_ref[...] = dres_ref[...] + dx
            _acc_out(dg_ref, dg, i == 0)

    return pl.pallas_call(
        body, name=name, grid=(T // tm, nj),
        in_specs=[pl.BlockSpec((tm, tn), lambda i, j: (i, j)),
                  pl.BlockSpec((K, tn), lambda i, j: (0, j)),
                  pl.BlockSpec((tm, K), lambda i, j: (i, 0)),
                  pl.BlockSpec((1, K), lambda i, j: (0, 0)),
                  pl.BlockSpec((tm, K), lambda i, j: (i, 0))],
        out_specs=[pl.BlockSpec((tm, K), lambda i, j: (i, 0)),
                   pl.BlockSpec((1, K), lambda i, j: (0, 0))],
        out_shape=[jax.ShapeDtypeStruct((T, K), F32), jax.ShapeDtypeStruct((1, K), F32)],
        scratch_shapes=[pltpu.VMEM((tm, K), F32)],
        compiler_params=_cparams(("arbitrary", "arbitrary")),
    )(dy, w, x, g, dres)


def loss_kernel(y, tgt):
    T, D = y.shape
    tm = _tile(T, 512)

    def body(y_ref, t_ref, dy_ref, acc_ref):
        e = y_ref[...] - t_ref[...]
        dy_ref[...] = e * (1.0 / D)
        _acc_out(acc_ref, jnp.sum(e * e, axis=0, keepdims=True), pl.program_id(0) == 0)

    return pl.pallas_call(
        body, name="loss", grid=(T // tm,),
        in_specs=[pl.BlockSpec((tm, D), lambda i: (i, 0)), pl.BlockSpec((tm, D), lambda i: (i, 0))],
        out_specs=[pl.BlockSpec((tm, D), lambda i: (i, 0)), pl.BlockSpec((1, D), lambda i: (0, 0))],
        out_shape=[jax.ShapeDtypeStruct((T, D), F32), jax.ShapeDtypeStruct((1, D), F32)],
        compiler_params=_cparams(("arbitrary",)),
    )(y, tgt)


def _rope_partner(x):
    lane = lax.broadcasted_iota(jnp.int32, (1, 128), 1)
    return jnp.where(lane < ROPE_LANE0 + ROPE_DIM // 2, pltpu.roll(x, 128 - ROPE_DIM // 2, 1),
                     pltpu.roll(x, ROPE_DIM // 2, 1))


def _rope_apply(x, cos, sin_signed):
    return x * cos + _rope_partner(x) * sin_signed


def _rope_apply_bwd(dy, cos, sin_signed):
    lane = lax.broadcasted_iota(jnp.int32, (1, 128), 1)
    rotary = (lane >= ROPE_LANE0) & (lane < ROPE_LANE0 + ROPE_DIM)
    return dy * cos + jnp.where(rotary, _rope_partner(dy * sin_signed), 0.0)


def rope_tables(T):
    pos = jnp.arange(T, dtype=F32)
    inv_freq = ROPE_THETA ** (-(jnp.arange(ROPE_DIM // 2, dtype=F32) * 2.0 / ROPE_DIM))
    ang = pos[:, None] * inv_freq[None, :]
    cos, sin = jnp.cos(ang), jnp.sin(ang)
    z = jnp.zeros((T, ROPE_LANE0), F32)
    z2 = jnp.zeros((T, 128 - ROPE_LANE0 - ROPE_DIM), F32)
    cosr = jnp.concatenate([z, cos, cos, z2], axis=1)
    sinr = jnp.concatenate([z, -sin, sin, z2], axis=1)
    return cosr, sinr


def mla_prep(proj, gq, gkv, wuq, wukv, cosr, sinr):
    T = proj.shape[0]
    tm = _tile(T, 512)

    def body(cq_ref, ckv_ref, misc_ref, gq_ref, gkv_ref, wuq_ref, wukv_ref, cos_ref, sin_ref,
             q_ref, k_ref, v_ref, cqT_ref, ckvT_ref):
        lane = lax.broadcasted_iota(jnp.int32, (1, 128), 1)
        cosr_, sinr_ = cos_ref[...], sin_ref[...]
        cosq = cosr_ + jnp.where(lane < ROPE_LANE0, 1.0, 0.0)
        cqn = _rms_fwd(cq_ref[...], gq_ref[...])
        cqT_ref[...] = cqn.T.astype(MXU)
        qm = _dot(cqn, wuq_ref[...])
        q_ref[...] = jnp.concatenate(
            [_rope_apply(qm[:, 128 * h:128 * (h + 1)], cosq, sinr_) for h in range(4)], axis=1).astype(MXU)
        ckvn = _rms_fwd(ckv_ref[...], gkv_ref[...])
        ckvT_ref[...] = ckvn.T.astype(MXU)
        kv = _dot(ckvn, wukv_ref[...])
        kr = _rope_apply(misc_ref[...], cosr_, sinr_)
        k_ref[...] = jnp.concatenate(
            [kv[:, 128 * h:128 * (h + 1)] + kr for h in range(4)], axis=1).astype(MXU)
        v_ref[...] = kv[:, 512:768].astype(MXU)

    row = lambda i: (i, 0)
    const = lambda i: (0, 0)
    return pl.pallas_call(
        body, name="mla_prep", grid=(T // tm,),
        in_specs=[pl.BlockSpec((tm, 256), lambda i: (i, C_CQ // 256)),
                  pl.BlockSpec((tm, 128), lambda i: (i, C_CKV // 128)),
                  pl.BlockSpec((tm, 128), lambda i: (i, C_MISC // 128)),
                  pl.BlockSpec((1, 256), const), pl.BlockSpec((1, 128), const),
                  pl.BlockSpec((256, 512), const), pl.BlockSpec((128, 768), const),
                  pl.BlockSpec((tm, 128), row), pl.BlockSpec((tm, 128), row)],
        out_specs=[pl.BlockSpec((tm, 512), row), pl.BlockSpec((tm, 512), row), pl.BlockSpec((tm, 256), row),
                   pl.BlockSpec((256, tm), lambda i: (0, i)), pl.BlockSpec((128, tm), lambda i: (0, i))],
        out_shape=[jax.ShapeDtypeStruct((T, 512), MXU), jax.ShapeDtypeStruct((T, 512), MXU),
                   jax.ShapeDtypeStruct((T, 256), MXU),
                   jax.ShapeDtypeStruct((256, T), MXU), jax.ShapeDtypeStruct((128, T), MXU)],
        compiler_params=_cparams(("parallel",)),
    )(proj, proj, proj, gq, gkv, wuq, wukv, cosr, sinr)


def mla_prep_bwd(dq, dk, dv, proj, gq, gkv, wuq, wukv, cosr, sinr):
    T = proj.shape[0]
    tm = _tile(T, 512)

    def body(dq_ref, dk_ref, dv_ref, cq_ref, ckv_ref, gq_ref, gkv_ref, wuq_ref, wukv_ref, cos_ref, sin_ref,
             dqm_ref, dkv_ref, dcq_ref, dckv_ref, dmisc_ref, dgq_ref, dgkv_ref):
        first = pl.program_id(0) == 0
        lane = lax.broadcasted_iota(jnp.int32, (1, 128), 1)
        cosr_, sinr_ = cos_ref[...], sin_ref[...]
        cosq = cosr_ + jnp.where(lane < ROPE_LANE0, 1.0, 0.0)
        dqv = dq_ref[...]
        dqm = jnp.concatenate(
            [_rope_apply_bwd(dqv[:, 128 * h:128 * (h + 1)], cosq, sinr_) for h in range(4)], axis=1)
        dqm_ref[...] = dqm.astype(MXU)
        dcq, dgq = _rms_bwd(_dot_nt(dqm, wuq_ref[...]), cq_ref[...], gq_ref[...])
        dcq_ref[...] = dcq
        _acc_out(dgq_ref, dgq, first)
        dkv_ = dk_ref[...]
        heads = [dkv_[:, 128 * h:128 * (h + 1)] for h in range(4)]
        dkr = heads[0] + heads[1] + heads[2] + heads[3]
        dmisc_ref[...] = _rope_apply_bwd(dkr, cosr_, sinr_)
        dkvm = jnp.concatenate([jnp.where(lane < ROPE_LANE0, hd, 0.0) for hd in heads] + [dv_ref[...]], axis=1)
        dkv_ref[...] = dkvm.astype(MXU)
        dckv, dgkv = _rms_bwd(_dot_nt(dkvm, wukv_ref[...]), ckv_ref[...], gkv_ref[...])
        dckv_ref[...] = dckv
        _acc_out(dgkv_ref, dgkv, first)

    row = lambda i: (i, 0)
    const = lambda i: (0, 0)
    return pl.pallas_call(
        body, name="mla_prep_bwd", grid=(T // tm,),
        in_specs=[pl.BlockSpec((tm, 512), row), pl.BlockSpec((tm, 512), row), pl.BlockSpec((tm, 256), row),
                  pl.BlockSpec((tm, 256), lambda i: (i, C_CQ // 256)),
                  pl.BlockSpec((tm, 128), lambda i: (i, C_CKV // 128)),
                  pl.BlockSpec((1, 256), const), pl.BlockSpec((1, 128), const),
                  pl.BlockSpec((256, 512), const), pl.BlockSpec((128, 768), const),
                  pl.BlockSpec((tm, 128), row), pl.BlockSpec((tm, 128), row)],
        out_specs=[pl.BlockSpec((tm, 512), row), pl.BlockSpec((tm, 768), row), pl.BlockSpec((tm, 256), row),
                   pl.BlockSpec((tm, 128), row), pl.BlockSpec((tm, 128), row),
                   pl.BlockSpec((1, 256), const), pl.BlockSpec((1, 128), const)],
        out_shape=[jax.ShapeDtypeStruct((T, 512), MXU), jax.ShapeDtypeStruct((T, 768), MXU),
                   jax.ShapeDtypeStruct((T, 256), F32), jax.ShapeDtypeStruct((T, 128), F32),
                   jax.ShapeDtypeStruct((T, 128), F32),
                   jax.ShapeDtypeStruct((1, 256), F32), jax.ShapeDtypeStruct((1, 128), F32)],
        compiler_params=_cparams(("arbitrary",)),
    )(dq, dk, dv, proj, proj, gq, gkv, wuq, wukv, cosr, sinr)


def _split3(x):
    hi = x.astype(MXU)
    r1 = x - hi.astype(F32)
    mid = r1.astype(MXU)
    lo = (r1 - mid.astype(F32)).astype(MXU)
    return hi, mid, lo


def _tri_matmul(tri, x):
    hi, mid, lo = _split3(x)
    d = lambda p: jnp.dot(tri, p, preferred_element_type=F32)
    return d(hi) + d(mid) + d(lo)


def _log_sigmoid(z):
    return jnp.minimum(z, 0.0) - jnp.log(1.0 + jnp.exp(-jnp.abs(z)))


def fox_gate(proj, fbias):
    T = proj.shape[0]
    tb = _tile(T, 512)

    def body(misc_ref, b_ref, fc_ref, fr_ref, frep_ref, carry):
        @pl.when(pl.program_id(0) == 0)
        def _():
            carry[...] = jnp.zeros_like(carry)

        lane = lax.broadcasted_iota(jnp.int32, (1, 128), 1)
        lf = jnp.where(lane < 4, _log_sigmoid(misc_ref[...] + b_ref[...]), 0.0)
        r = lax.broadcasted_iota(jnp.int32, (tb, tb), 0)
        c = lax.broadcasted_iota(jnp.int32, (tb, tb), 1)
        tri = jnp.where(r >= c, 1.0, 0.0).astype(MXU)
        F = _tri_matmul(tri, lf) + carry[...]
        carry[...] = carry[...] + jnp.sum(lf, axis=0, keepdims=True)
        fc_ref[0] = F
        fc_ref[1] = pltpu.roll(F, 126, 1)
        ft = F.T[0:8, :]
        fr_ref[0] = ft
        fr_ref[1] = pltpu.roll(ft, 6, 0)
        for h in range(4):
            frep_ref[h] = jnp.broadcast_to(_lane_pick(F, h), (tb, 128))

    return pl.pallas_call(
        body, name="fox_gate", grid=(T // tb,),
        in_specs=[pl.BlockSpec((tb, 128), lambda i: (i, C_MISC // 128)), pl.BlockSpec((1, 128), lambda i: (0, 0))],
        out_specs=[pl.BlockSpec((2, tb, 128), lambda i: (0, i, 0)), pl.BlockSpec((2, 8, tb), lambda i: (0, 0, i)),
                   pl.BlockSpec((4, tb, 128), lambda i: (0, i, 0))],
        out_shape=[jax.ShapeDtypeStruct((2, T, 128), F32), jax.ShapeDtypeStruct((2, 8, T), F32),
                   jax.ShapeDtypeStruct((4, T, 128), F32)],
        scratch_shapes=[pltpu.VMEM((1, 128), F32)],
        compiler_params=_cparams(("arbitrary",)),
    )(proj, fbias)


def fox_gate_bwd(dFq, dFk, proj, fbias):
    T = proj.shape[0]
    tb = _tile(T, 512)
    nb = T // tb

    def body(dq_ref, dk_ref, misc_ref, b_ref, dm_ref, db_ref, carry):
        first = pl.program_id(0) == 0

        @pl.when(first)
        def _():
            carry[...] = jnp.zeros_like(carry)

        lane = lax.broadcasted_iota(jnp.int32, (1, 128), 1)
        dF = jnp.where(lane < 4, (dq_ref[0] + dk_ref[0]) + pltpu.roll(dq_ref[1] + dk_ref[1], 2, 1), 0.0)
        r = lax.broadcasted_iota(jnp.int32, (tb, tb), 0)
        c = lax.broadcasted_iota(jnp.int32, (tb, tb), 1)
        tri = jnp.where(r <= c, 1.0, 0.0).astype(MXU)
        dlf = _tri_matmul(tri, dF) + carry[...]
        carry[...] = carry[...] + jnp.sum(dF, axis=0, keepdims=True)
        z = misc_ref[...] + b_ref[...]
        dz = jnp.where(lane < 4, dlf * (1.0 / (1.0 + jnp.exp(z))), 0.0)
        dm_ref[...] = dz
        _acc_out(db_ref, jnp.sum(dz, axis=0, keepdims=True), first)

    return pl.pallas_call(
        body, name="fox_gate_bwd", grid=(nb,),
        in_specs=[pl.BlockSpec((2, tb, 128), lambda i: (0, nb - 1 - i, 0)),
                  pl.BlockSpec((2, tb, 128), lambda i: (0, nb - 1 - i, 0)),
                  pl.BlockSpec((tb, 128), lambda i: (nb - 1 - i, C_MISC // 128)),
                  pl.BlockSpec((1, 128), lambda i: (0, 0))],
        out_specs=[pl.BlockSpec((tb, 128), lambda i: (nb - 1 - i, 0)), pl.BlockSpec((1, 128), lambda i: (0, 0))],
        out_shape=[jax.ShapeDtypeStruct((T, 128), F32), jax.ShapeDtypeStruct((1, 128), F32)],
        scratch_shapes=[pltpu.VMEM((1, 128), F32)],
        compiler_params=_cparams(("arbitrary",)),
    )(dFq, dFk, proj, fbias)


FLASH_TILE = 512


def _row_stat_tile(a, b, n):
    at = jnp.broadcast_to(a, (n, 128)).T[0:8, :]
    bt = jnp.broadcast_to(b, (n, 128)).T[0:8, :]
    sub = lax.broadcasted_iota(jnp.int32, (8, 1), 0)
    return jnp.where(sub == 0, at, jnp.where(sub == 1, bt, 0.0))


def _col_stat_tile(a, b):
    lane = lax.broadcasted_iota(jnp.int32, (1, 128), 1)
    return jnp.where(lane == 0, a, jnp.where(lane == 1, b, 0.0))


def _lane_pick(x, h):
    lane = lax.broadcasted_iota(jnp.int32, (1, 128), 1)
    return jnp.sum(jnp.where(lane == h, x, 0.0), axis=1, keepdims=True)


def _half_mask(h):
    lane = lax.broadcasted_iota(jnp.int32, (1, 128), 1)
    return (lane // HEAD_DIM) == h


def _call_hosting(body, name, grid, args, in_specs, out_specs, out_shape, scratch, exch):
    n_out = len(out_shape)
    if exch is not None:
        body, (xargs, xin, xout, xshape, xscratch) = hosted_exchange(
            body, len(args), n_out, len(scratch), grid, *exch)
        args, in_specs, out_specs = args + xargs, in_specs + xin, out_specs + xout
        out_shape, scratch = out_shape + xshape, scratch + xscratch
    res = pl.pallas_call(
        body, name=name, grid=grid, in_specs=in_specs, out_specs=out_specs, out_shape=out_shape,
        scratch_shapes=scratch, compiler_params=_cparams(("arbitrary",) * len(grid)),
    )(*args)
    return res[:n_out], res[n_out:]


def flash_fwd(q, k, v, frep, frow, *, qblk, kblk, vblk, nq, scale, name, exch=None):
    T = q.shape[0]
    tq = tk = _tile(T, FLASH_TILE)
    wq = 128 * nq
    has_f = frep is not None

    def body(*refs):
        if has_f:
            q_ref, k_ref, v_ref, fk_ref, fr_ref, o_ref, lr_ref, vT_sc, m_sc, acc_sc = refs
        else:
            q_ref, k_ref, v_ref, o_ref, lr_ref, vT_sc, m_sc, acc_sc = refs
        i = pl.program_id(1)

        @pl.when(i == 0)
        def _():
            vT_sc[...] = v_ref[...].astype(F32).T.astype(MXU)

        diag = lax.broadcasted_iota(jnp.int32, (tk, 1), 0) <= lax.broadcasted_iota(jnp.int32, (1, tq), 1)
        row_half = lax.broadcasted_iota(jnp.int32, (128, 1), 0) // HEAD_DIM
        qb = q_ref[...]
        if nq == 1:
            qhs = [jnp.where(_half_mask(h), qb, 0).astype(MXU) for h in range(2)]
        else:
            qhs = [qb[:, 128 * h:128 * (h + 1)].astype(MXU) for h in range(2)]
        for h in range(2):
            m_sc[h] = jnp.full((1, tq), NEG, F32)
            acc_sc[h] = jnp.zeros((128, tq), F32)

        def make_step(masked):
            def step(j, carry):
                off = pl.multiple_of(j * tk, tk)
                ks = k_ref[pl.ds(off, tk), :]
                vT = vT_sc[:, pl.ds(off, tk)]
                for h in range(2):
                    kh = ks if nq == 1 else ks[:, 128 * h:128 * (h + 1)]
                    sT = _dot_nt(kh, qhs[h]) * scale
                    if has_f:
                        fk = fk_ref[h, pl.ds(off, tk), :]
                        sT = sT + (fr_ref[0, h:h + 1, :] - jnp.concatenate([fk] * (tq // 128), axis=1))
                    if masked:
                        sT = jnp.where(diag, sT, NEG)
                    m_prev = m_sc[h]
                    m_new = jnp.maximum(m_prev, jnp.max(sT, axis=0, keepdims=True))
                    alpha = jnp.exp(m_prev - m_new)
                    pT = jnp.exp(sT - m_new)
                    vTh = jnp.where(row_half == h, vT, jnp.ones_like(vT))
                    acc_sc[h] = alpha * acc_sc[h] + _dot(vTh, pT)
                    m_sc[h] = m_new
                return carry
            return step

        lax.fori_loop(0, i, make_step(False), 0)
        make_step(True)(i, 0)
        outs, lses = [], []
        for h in range(2):
            acc = acc_sc[h]
            outs.append(acc / pltpu.roll(acc, HEAD_DIM, 0))
            l = acc_sc[h, HEAD_DIM * (1 - h):HEAD_DIM * (1 - h) + 1, :]
            lses.append(m_sc[h] + jnp.log(l))
        o_ref[...] = jnp.where(row_half == 0, outs[0], outs[1]).T
        sub = lax.broadcasted_iota(jnp.int32, (8, 1), 0)
        lr_ref[0] = jnp.where(sub == 0, lses[0], jnp.where(sub == 1, lses[1], 0.0))

    in_specs = [pl.BlockSpec((tq, wq), lambda p, i: (i, qblk + p)),
                pl.BlockSpec((T, wq), lambda p, i: (0, kblk + p)),
                pl.BlockSpec((T, 128), lambda p, i: (0, vblk + p))]
    args = [q, k, v]
    if has_f:
        in_specs += [pl.BlockSpec((2, T, 128), lambda p, i: (p, 0, 0)),
                     pl.BlockSpec((1, 8, tq), lambda p, i: (p, 0, i))]
        args += [frep, frow]
    out_specs = [pl.BlockSpec((tq, 128), lambda p, i: (i, p)), pl.BlockSpec((1, 8, tq), lambda p, i: (p, 0, i))]
    out_shape = [jax.ShapeDtypeStruct((T, 256), F32), jax.ShapeDtypeStruct((2, 8, T), F32)]
    scratch = [pltpu.VMEM((128, T), MXU), pltpu.VMEM((2, 1, tq), F32), pltpu.VMEM((2, 128, tq), F32)]
    return _call_hosting(body, name, (2, T // tq), args, in_specs, out_specs, out_shape, scratch, exch)


def flash_bwd(q, k, v, do, o, lrow, fcol, frow, *, qblk, kblk, vblk, nq, scale, name, exch=None):
    T = q.shape[0]
    tq = tk = _tile(T, FLASH_TILE)
    wq = 128 * nq
    nqb = T // tq
    has_f = fcol is not None

    def body(*refs):
        if has_f:
            (q_ref, k_ref, v_ref, do_ref, o_ref, lr_ref, fc_ref, fr_ref,
             dq_ref, dk_ref, dv_ref, df_ref, dfq_ref, dk_sc, dv_sc, dqT_sc, d_sc, df_sc, dfq_sc) = refs
        else:
            q_ref, k_ref, v_ref, do_ref, o_ref, lr_ref, dq_ref, dk_ref, dv_ref, dk_sc, dv_sc, dqT_sc, d_sc = refs
        j = pl.program_id(1)
        diag = lax.broadcasted_iota(jnp.int32, (tk, 1), 0) <= lax.broadcasted_iota(jnp.int32, (1, tq), 1)
        hms = [_half_mask(h) for h in range(2)]

        @pl.when(j == 0)
        def _():
            dqT_sc[...] = jnp.zeros_like(dqT_sc)
            if has_f:
                dfq_sc[...] = jnp.zeros_like(dfq_sc)

            def delta(b, carry):
                off = pl.multiple_of(b * tq, tq)
                prod = do_ref[pl.ds(off, tq), :] * o_ref[pl.ds(off, tq), :]
                Ds = [jnp.sum(jnp.where(hms[h], prod, 0.0), axis=1, keepdims=True) for h in range(2)]
                d_sc[:, pl.ds(off, tq)] = _row_stat_tile(Ds[0], Ds[1], tq)
                return carry

            lax.fori_loop(0, nqb, delta, 0)

        kb = k_ref[...]
        vb = v_ref[...]
        if nq == 1:
            khs = [jnp.where(hms[h], kb, 0).astype(MXU) for h in range(2)]
        else:
            khs = [kb[:, 128 * h:128 * (h + 1)].astype(MXU) for h in range(2)]
        kTs = [kh.astype(F32).T.astype(MXU) for kh in khs]
        vhs = [jnp.where(hms[h], vb, 0).astype(MXU) for h in range(2)]
        fks = [_lane_pick(fc_ref[0], h) for h in range(2)] if has_f else None
        dv_sc[...] = jnp.zeros_like(dv_sc)
        dk_sc[...] = jnp.zeros_like(dk_sc)
        if has_f:
            df_sc[...] = jnp.zeros_like(df_sc)

        def make_step(masked):
            def step(i, carry):
                off = pl.multiple_of(i * tq, tq)
                qs = q_ref[pl.ds(off, tq), :]
                dos = do_ref[pl.ds(off, tq), :]
                for h in range(2):
                    qh = qs if nq == 1 else qs[:, 128 * h:128 * (h + 1)]
                    sT = _dot_nt(khs[h], qh) * scale
                    if has_f:
                        sT = sT + (fr_ref[0, h:h + 1, pl.ds(off, tq)] - fks[h])
                    pT = jnp.exp(sT - lr_ref[0, h:h + 1, pl.ds(off, tq)])
                    if masked:
                        pT = jnp.where(diag, pT, 0.0)
                    dsT = pT * (_dot_nt(vhs[h], dos) - d_sc[h:h + 1, pl.ds(off, tq)])
                    dv_sc[...] += _dot(pT, jnp.where(hms[h], dos, 0))
                    qq = jnp.where(hms[h], qs, 0) if nq == 1 else qh
                    dk_sc[h if nq == 2 else 0] += _dot(dsT, qq)
                    dqT_sc[h if nq == 2 else 0, :, pl.ds(off, tq)] += _dot(kTs[h], dsT)
                    if has_f:
                        part = dsT[:, 0:128]
                        for c in range(1, tq // 128):
                            part = part + dsT[:, 128 * c:128 * (c + 1)]
                        df_sc[h] += part
                        dfq_sc[h:h + 1, pl.ds(off, tq)] += jnp.sum(dsT, axis=0, keepdims=True)
                return carry
            return step

        make_step(True)(j, 0)
        lax.fori_loop(j + 1, nqb, make_step(False), 0)
        if nq == 1:
            dk_ref[...] = dk_sc[0] * scale
        else:
            dk_ref[...] = jnp.concatenate([dk_sc[0], dk_sc[1]], axis=1) * scale
        dv_ref[...] = dv_sc[...]
        if has_f:
            df_ref[0] = _col_stat_tile(-jnp.sum(df_sc[0], axis=1, keepdims=True),
                                       -jnp.sum(df_sc[1], axis=1, keepdims=True))

        @pl.when(j == nqb - 1)
        def _():
            if nq == 1:
                dq_ref[...] = dqT_sc[0].T * scale
            else:
                dq_ref[...] = jnp.concatenate([dqT_sc[0].T, dqT_sc[1].T], axis=1) * scale
            if has_f:
                sub = lax.broadcasted_iota(jnp.int32, (128, 1), 0)
                rows = jnp.where(sub == 0, dfq_sc[0:1, :], jnp.where(sub == 1, dfq_sc[1:2, :], 0.0))
                dfq_ref[0] = rows.T

    in_specs = [pl.BlockSpec((T, wq), lambda p, j: (0, qblk + p)),
                pl.BlockSpec((tk, wq), lambda p, j: (j, kblk + p)),
                pl.BlockSpec((tk, 128), lambda p, j: (j, vblk + p)),
                pl.BlockSpec((T, 128), lambda p, j: (0, p)),
                pl.BlockSpec((T, 128), lambda p, j: (0, p)),
                pl.BlockSpec((1, 8, T), lambda p, j: (p, 0, 0))]
    args = [q, k, v, do, o, lrow]
    out_specs = [pl.BlockSpec((T, wq), lambda p, j: (0, p)),
                 pl.BlockSpec((tk, wq), lambda p, j: (j, p)), pl.BlockSpec((tk, 128), lambda p, j: (j, p))]
    out_shape = [jax.ShapeDtypeStruct((T, 2 * wq), F32), jax.ShapeDtypeStruct((T, 2 * wq), F32),
                 jax.ShapeDtypeStruct((T, 256), F32)]
    scratch = [pltpu.VMEM((nq, tk, 128), F32), pltpu.VMEM((tk, 128), F32), pltpu.VMEM((nq, 128, T), F32),
               pltpu.VMEM((8, T), F32)]
    if has_f:
        in_specs += [pl.BlockSpec((1, tk, 128), lambda p, j: (p, j, 0)),
                     pl.BlockSpec((1, 8, T), lambda p, j: (p, 0, 0))]
        args += [fcol, frow]
        out_specs += [pl.BlockSpec((1, tk, 128), lambda p, j: (p, j, 0)),
                      pl.BlockSpec((1, T, 128), lambda p, j: (p, 0, 0))]
        out_shape += [jax.ShapeDtypeStruct((2, T, 128), F32), jax.ShapeDtypeStruct((2, T, 128), F32)]
        scratch += [pltpu.VMEM((2, tk, 128), F32), pltpu.VMEM((8, T), F32)]
    return _call_hosting(body, name, (2, T // tk), args, in_specs, out_specs, out_shape, scratch, exch)


def _swa_align(pair, e, h):
    sel = jnp.where(_half_mask(e), pair, 0.0)
    if e == h:
        return sel
    return pltpu.roll(sel, HEAD_DIM, 1)


def _swa_mask(n):
    W = WINDOW
    qi = lax.broadcasted_iota(jnp.int32, (W, 2 * W), 0) + W
    kj = lax.broadcasted_iota(jnp.int32, (W, 2 * W), 1)
    dist = qi - kj
    return (dist >= 0) & (dist < W) & ((n > 0) | (kj >= W))


def swa_fwd(proj, bias, sinks):
    T = proj.shape[0]
    W = WINDOW
    nb = T // W
    scale = HEAD_DIM ** -0.5

    def body(sink_ref, q_ref, kp_ref, kc_ref, vp_ref, vc_ref, b_ref, o_ref, l_ref):
        n = pl.program_id(0)
        mask = _swa_mask(n)
        kband = jnp.concatenate([kp_ref[...], kc_ref[...]], axis=0).astype(MXU)
        vband = jnp.concatenate([vp_ref[...], vc_ref[...]], axis=0).astype(MXU)
        lane = lax.broadcasted_iota(jnp.int32, (1, 128), 1)
        lse_tile = jnp.zeros((W, 128), F32)
        pairs = []
        for h in range(2):
            full = []
            for g in range(4):
                hq = 4 * h + g
                qa = _swa_align(q_ref[:, 128 * (hq // 2):128 * (hq // 2 + 1)], hq % 2, h)
                s = _dot_nt(qa, kband) * scale + b_ref[hq]
                s = jnp.where(mask, s, NEG)
                sink = sink_ref[hq]
                m = jnp.maximum(jnp.max(s, axis=1, keepdims=True), sink)
                e = jnp.exp(s - m)
                l = jnp.sum(e, axis=1, keepdims=True) + jnp.exp(sink - m)
                r = jnp.where(_half_mask(h), _dot(e, vband), 0.0) / l
                full.append(r + pltpu.roll(r, HEAD_DIM, 1))
                lse_tile = jnp.where(lane == hq, m + jnp.log(l), lse_tile)
            pairs.append(jnp.where(_half_mask(0), full[0], full[1]))
            pairs.append(jnp.where(_half_mask(0), full[2], full[3]))
        o_ref[...] = jnp.concatenate(pairs, axis=1)
        l_ref[...] = lse_tile

    prev = lambda n: (jnp.maximum(n - 1, 0), C_KA // 128)
    cur = lambda n: (n, C_KA // 128)
    prev_v = lambda n: (jnp.maximum(n - 1, 0), C_VA // 128)
    cur_v = lambda n: (n, C_VA // 128)
    return pl.pallas_call(
        body, name="swa_fwd", grid=(nb,),
        in_specs=[pl.BlockSpec(memory_space=pltpu.SMEM),
                  pl.BlockSpec((W, 512), lambda n: (n, 0)),
                  pl.BlockSpec((W, 128), prev), pl.BlockSpec((W, 128), cur),
                  pl.BlockSpec((W, 128), prev_v), pl.BlockSpec((W, 128), cur_v),
                  pl.BlockSpec((8, W, 2 * W), lambda n: (0, 0, 0))],
        out_specs=[pl.BlockSpec((W, 512), lambda n: (n, 0)), pl.BlockSpec((W, 128), lambda n: (n, 0))],
        out_shape=[jax.ShapeDtypeStruct((T, 512), F32), jax.ShapeDtypeStruct((T, 128), F32)],
        compiler_params=_cparams(("parallel",)),
    )(sinks, proj, proj, proj, proj, proj, bias)


def swa_bwd(proj, bias, sinks, do, o, lse):
    T = proj.shape[0]
    W = WINDOW
    nb = T // W
    scale = HEAD_DIM ** -0.5

    def body(sink_ref, q_ref, kp_ref, kc_ref, vp_ref, vc_ref, b_ref, do_ref, o_ref, l_ref,
             dq_ref, dk_ref, dv_ref, db_ref, dsk_ref, ck, cv):
        n = pl.program_id(0)

        @pl.when(n == 0)
        def _():
            ck[...] = jnp.zeros_like(ck)
            cv[...] = jnp.zeros_like(cv)
            db_ref[...] = jnp.zeros_like(db_ref)
            dsk_ref[...] = jnp.zeros_like(dsk_ref)

        @pl.when(n < nb)
        def _():
            mask = _swa_mask(n)
            kb32 = jnp.concatenate([kp_ref[...], kc_ref[...]], axis=0)
            vb32 = jnp.concatenate([vp_ref[...], vc_ref[...]], axis=0)
            kband = kb32.astype(MXU)
            sub = lax.broadcasted_iota(jnp.int32, (8, 1), 0)
            dk_band = jnp.zeros((2 * W, 128), F32)
            dv_band = jnp.zeros((2 * W, 128), F32)
            dsk = jnp.zeros((8, 128), F32)
            dq_pairs = []
            for h in range(2):
                hm = _half_mask(h)
                km = jnp.where(hm, kb32, 0.0).astype(MXU)
                vm = jnp.where(hm, vb32, 0.0).astype(MXU)
                full = []
                for g in range(4):
                    hq = 4 * h + g
                    pb = slice(128 * (hq // 2), 128 * (hq // 2 + 1))
                    e = hq % 2
                    qa = _swa_align(q_ref[:, pb], e, h)
                    dop = do_ref[:, pb]
                    doa = _swa_align(dop, e, h)
                    D = jnp.sum(jnp.where(_half_mask(e), dop * o_ref[:, pb], 0.0), axis=1, keepdims=True)
                    lse_h = _lane_pick(l_ref[...], hq)
                    s = _dot_nt(qa, kband) * scale + b_ref[hq]
                    p = jnp.where(mask, jnp.exp(s - lse_h), 0.0)
                    psink = jnp.exp(sink_ref[hq] - lse_h)
                    dsk = dsk + jnp.where(sub == hq, -jnp.sum(psink * D, axis=0, keepdims=True), 0.0)
                    dp = _dot_nt(doa, vm)
                    ds = p * (dp - D)
                    db_ref[hq] += ds
                    dq = _dot(ds, km) * scale
                    full.append(dq + pltpu.roll(dq, HEAD_DIM, 1))
                    dk_band = dk_band + _dot(ds.T, qa) * scale
                    dv_band = dv_band + _dot(p.T, doa)
                dq_pairs.append(jnp.where(_half_mask(0), full[0], full[1]))
                dq_pairs.append(jnp.where(_half_mask(0), full[2], full[3]))
            dq_ref[...] = jnp.concatenate(dq_pairs, axis=1)
            dsk_ref[...] += dsk
            dk_ref[...] = ck[...] + dk_band[0:W]
            dv_ref[...] = cv[...] + dv_band[0:W]
            ck[...] = dk_band[W:2 * W]
            cv[...] = dv_band[W:2 * W]

        @pl.when(n == nb)
        def _():
            dk_ref[...] = ck[...]
            dv_ref[...] = cv[...]

    cl = lambda n: jnp.minimum(n, nb - 1)
    pv = lambda n: jnp.maximum(jnp.minimum(n, nb - 1) - 1, 0)
    return pl.pallas_call(
        body, name="swa_bwd", grid=(nb + 1,),
        in_specs=[pl.BlockSpec(memory_space=pltpu.SMEM),
                  pl.BlockSpec((W, 512), lambda n: (cl(n), 0)),
                  pl.BlockSpec((W, 128), lambda n: (pv(n), C_KA // 128)),
                  pl.BlockSpec((W, 128), lambda n: (cl(n), C_KA // 128)),
                  pl.BlockSpec((W, 128), lambda n: (pv(n), C_VA // 128)),
                  pl.BlockSpec((W, 128), lambda n: (cl(n), C_VA // 128)),
                  pl.BlockSpec((8, W, 2 * W), lambda n: (0, 0, 0)),
                  pl.BlockSpec((W, 512), lambda n: (cl(n), 0)),
                  pl.BlockSpec((W, 512), lambda n: (cl(n), 0)),
                  pl.BlockSpec((W, 128), lambda n: (cl(n), 0))],
        out_specs=[pl.BlockSpec((W, 512), lambda n: (cl(n), 0)),
                   pl.BlockSpec((W, 128), lambda n: (jnp.maximum(n - 1, 0), 0)),
                   pl.BlockSpec((W, 128), lambda n: (jnp.maximum(n - 1, 0), 0)),
                   pl.BlockSpec((8, W, 2 * W), lambda n: (0, 0, 0)),
                   pl.BlockSpec((8, 128), lambda n: (0, 0))],
        out_shape=[jax.ShapeDtypeStruct((T, 512), F32), jax.ShapeDtypeStruct((T, 128), F32),
                   jax.ShapeDtypeStruct((T, 128), F32), jax.ShapeDtypeStruct((8, W, 2 * W), F32),
                   jax.ShapeDtypeStruct((8, 128), F32)],
        scratch_shapes=[pltpu.VMEM((W, 128), F32), pltpu.VMEM((W, 128), F32)],
        compiler_params=_cparams(("arbitrary",)),
    )(sinks, proj, proj, proj, proj, proj, bias, do, o, lse)


def swa_bias_table(rel_bias):
    W = WINDOW
    qi = jnp.arange(W, dtype=jnp.int32)[:, None] + W
    kj = jnp.arange(2 * W, dtype=jnp.int32)[None, :]
    dist = qi - kj
    max_exact = REL_BUCKETS // 2
    d = jnp.maximum(dist, 0)
    log_ratio = jnp.log(jnp.maximum(d, 1).astype(F32) / max_exact) / math.log(REL_MAX_DIST / max_exact)
    large = jnp.minimum(max_exact + (log_ratio * (REL_BUCKETS - max_exact)).astype(jnp.int32), REL_BUCKETS - 1)
    bucket = jnp.where(d < max_exact, d, large)
    bucket = bucket.reshape(-1)
    onehot = (bucket[None, :] == jnp.arange(REL_BUCKETS, dtype=jnp.int32)[:, None]).astype(F32)
    bias = jnp.dot(rel_bias.astype(F32).T, onehot, precision=lax.Precision.HIGHEST)
    return bias.reshape(SWA_Q_HEADS, W, 2 * W), bucket


def attn_out(oa, ob, oc, gn, wout, gpost, x):
    T = x.shape[0]
    tm = _tile(T, 512)

    def body(oa_ref, ob_ref, oc_ref, gn_ref, w_ref, gp_ref, x_ref, x2_ref, y_ref, mT_ref):
        g = gn_ref[...]
        mixed = jnp.concatenate([_rms_fwd(oa_ref[...], g[:, 0:512]), _rms_fwd(ob_ref[...], g[:, 512:768]),
                                 _rms_fwd(oc_ref[...], g[:, 768:1024])], axis=1)
        mT_ref[...] = mixed.T.astype(MXU)
        y = _dot(mixed, w_ref[...])
        y_ref[...] = y
        x2_ref[...] = x_ref[...] + _rms_fwd(y, gp_ref[...])

    row = lambda i: (i, 0)
    const = lambda i: (0, 0)
    return pl.pallas_call(
        body, name="attn_out", grid=(T // tm,),
        in_specs=[pl.BlockSpec((tm, 512), row), pl.BlockSpec((tm, 256), row), pl.BlockSpec((tm, 256), row),
                  pl.BlockSpec((1, 1024), const), pl.BlockSpec((1024, 1024), const), pl.BlockSpec((1, 1024), const),
                  pl.BlockSpec((tm, 1024), row)],
        out_specs=[pl.BlockSpec((tm, 1024), row), pl.BlockSpec((tm, 1024), row),
                   pl.BlockSpec((1024, tm), lambda i: (0, i))],
        out_shape=[jax.ShapeDtypeStruct((T, 1024), F32), jax.ShapeDtypeStruct((T, 1024), F32),
                   jax.ShapeDtypeStruct((1024, T), MXU)],
        compiler_params=_cparams(("parallel",)),
    )(oa, ob, oc, gn, wout, gpost, x)


def attn_out_bwd(dx2, y, oa, ob, oc, gn, wout, gpost):
    T = dx2.shape[0]
    tm = _tile(T, 512)

    def body(dx_ref, y_ref, oa_ref, ob_ref, oc_ref, gn_ref, w_ref, gp_ref,
             dy_ref, da_ref, db_ref, dc_ref, dgn_ref, dgp_ref):
        first = pl.program_id(0) == 0
        dy, dgp = _rms_bwd(dx_ref[...], y_ref[...], gp_ref[...])
        dy_ref[...] = dy.astype(MXU)
        _acc_out(dgp_ref, dgp, first)
        dm = _dot_nt(dy, w_ref[...])
        g = gn_ref[...]
        da, dga = _rms_bwd(dm[:, 0:512], oa_ref[...], g[:, 0:512])
        db, dgb = _rms_bwd(dm[:, 512:768], ob_ref[...], g[:, 512:768])
        dc, dgc = _rms_bwd(dm[:, 768:1024], oc_ref[...], g[:, 768:1024])
        da_ref[...] = da
        db_ref[...] = db
        dc_ref[...] = dc
        _acc_out(dgn_ref, jnp.concatenate([dga, dgb, dgc], axis=1), first)

    row = lambda i: (i, 0)
    const = lambda i: (0, 0)
    return pl.pallas_call(
        body, name="attn_out_bwd", grid=(T // tm,),
        in_specs=[pl.BlockSpec((tm, 1024), row), pl.BlockSpec((tm, 1024), row),
                  pl.BlockSpec((tm, 512), row), pl.BlockSpec((tm, 256), row), pl.BlockSpec((tm, 256), row),
                  pl.BlockSpec((1, 1024), const), pl.BlockSpec((1024, 1024), const), pl.BlockSpec((1, 1024), const)],
        out_specs=[pl.BlockSpec((tm, 1024), row), pl.BlockSpec((tm, 512), row), pl.BlockSpec((tm, 256), row),
                   pl.BlockSpec((tm, 256), row), pl.BlockSpec((1, 1024), const), pl.BlockSpec((1, 1024), const)],
        out_shape=[jax.ShapeDtypeStruct((T, 1024), MXU), jax.ShapeDtypeStruct((T, 512), F32),
                   jax.ShapeDtypeStruct((T, 256), F32), jax.ShapeDtypeStruct((T, 256), F32),
                   jax.ShapeDtypeStruct((1, 1024), F32), jax.ShapeDtypeStruct((1, 1024), F32)],
        compiler_params=_cparams(("arbitrary",)),
    )(dx2, y, oa, ob, oc, gn, wout, gpost)


FF_TILE = 256
_GELU_C = math.sqrt(2.0 / math.pi)


def _gelu(x):
    return 0.5 * x * (1.0 + jnp.tanh(_GELU_C * (x + 0.044715 * x * x * x)))


def _gelu_grad(x):
    t = jnp.tanh(_GELU_C * (x + 0.044715 * x * x * x))
    return 0.5 * (1.0 + t) + 0.5 * x * (1.0 - t * t) * _GELU_C * (1.0 + 3 * 0.044715 * x * x)


def _conv_taps(u, hal_ref, first):
    row = lax.broadcasted_iota(jnp.int32, (u.shape[0], 1), 0)
    h6 = jnp.where(first, 0.0, hal_ref[6:7, :])
    h7 = jnp.where(first, 0.0, hal_ref[7:8, :])
    r1 = jnp.where(row == 0, h7, pltpu.roll(u, 1, 0))
    r2 = jnp.where(row == 0, h6, jnp.where(row == 1, h7, pltpu.roll(u, 2, 0)))
    return r1, r2


def ffn_fwd(u0, convw, convb, wdown, gpost, x2):
    T = x2.shape[0]
    tm, tn = _tile(T, 1024), FF_TILE
    nj = D_FF // tn

    def body(ug_ref, uu_ref, hg_ref, hu_ref, wg_ref, wu_ref, bg_ref, bu_ref, wd_ref, gp_ref, x_ref,
             x3_ref, y_ref, aT_ref, acc):
        i, j = pl.program_id(0), pl.program_id(1)
        first = i == 0

        def conv(u_ref, h_ref, w_ref, b_ref):
            u = u_ref[...]
            r1, r2 = _conv_taps(u, h_ref, first)
            return b_ref[...] + w_ref[0:1, :] * r2 + w_ref[1:2, :] * r1 + w_ref[2:3, :] * u

        a = _gelu(conv(ug_ref, hg_ref, wg_ref, bg_ref)) * conv(uu_ref, hu_ref, wu_ref, bu_ref)
        aT_ref[...] = a.T.astype(MXU)
        _acc_out(acc, _dot(a, wd_ref[...]), j == 0)

        @pl.when(j == nj - 1)
        def _():
            y = acc[...]
            y_ref[...] = y
            x3_ref[...] = x_ref[...] + _rms_fwd(y, gp_ref[...])

    halo = lambda off: (lambda i, j: (jnp.maximum(i * (tm // 8) - 1, 0), off + j))
    return pl.pallas_call(
        body, name="ffn_fwd", grid=(T // tm, nj),
        in_specs=[pl.BlockSpec((tm, tn), lambda i, j: (i, j)), pl.BlockSpec((tm, tn), lambda i, j: (i, nj + j)),
                  pl.BlockSpec((8, tn), halo(0)), pl.BlockSpec((8, tn), halo(nj)),
                  pl.BlockSpec((3, tn), lambda i, j: (0, j)), pl.BlockSpec((3, tn), lambda i, j: (0, nj + j)),
                  pl.BlockSpec((1, tn), lambda i, j: (0, j)), pl.BlockSpec((1, tn), lambda i, j: (0, nj + j)),
                  pl.BlockSpec((tn, 1024), lambda i, j: (j, 0)),
                  pl.BlockSpec((1, 1024), lambda i, j: (0, 0)),
                  pl.BlockSpec((tm, 1024), lambda i, j: (i, 0))],
        out_specs=[pl.BlockSpec((tm, 1024), lambda i, j: (i, 0)), pl.BlockSpec((tm, 1024), lambda i, j: (i, 0)),
                   pl.BlockSpec((tn, tm), lambda i, j: (j, i))],
        out_shape=[jax.ShapeDtypeStruct((T, 1024), F32), jax.ShapeDtypeStruct((T, 1024), F32),
                   jax.ShapeDtypeStruct((D_FF, T), MXU)],
        scratch_shapes=[pltpu.VMEM((tm, 1024), F32)],
        compiler_params=_cparams(("parallel", "arbitrary")),
    )(u0, u0, u0, u0, convw, convw, convb, convb, wdown, gpost, x2)


def ffn_bwd(dx3, y, u0, convw, convb, wdown, gpost):
    T = dx3.shape[0]
    tm, tn = _tile(T, 1024), FF_TILE
    nj = D_FF // tn
    ni = T // tm

    def body(dx_ref, y_ref, ug_ref, uu_ref, hg_ref, hu_ref, wg_ref, wu_ref, bg_ref, bu_ref, wd_ref, gp_ref,
             dy_ref, dug_ref, duu_ref, dcg_ref, dcu_ref, dgp_ref, dy_sc, cg, cu, ag, au):
        s, j = pl.program_id(0), pl.program_id(1)
        i = ni - 1 - s
        first_tok = i == 0
        row = lax.broadcasted_iota(jnp.int32, (tm, 1), 0)
        sub = lax.broadcasted_iota(jnp.int32, (8, 1), 0)

        @pl.when(j == 0)
        def _():
            dy, dgp = _rms_bwd(dx_ref[...], y_ref[...], gp_ref[...])
            dy_sc[...] = dy.astype(MXU)
            dy_ref[...] = dy.astype(MXU)
            _acc_out(dgp_ref, dgp, s == 0)

        @pl.when(s == 0)
        def _():
            cg[j] = jnp.zeros((8, tn), F32)
            cu[j] = jnp.zeros((8, tn), F32)
            ag[j] = jnp.zeros((8, tn), F32)
            au[j] = jnp.zeros((8, tn), F32)

        da = _dot_nt(dy_sc[...], wd_ref[...])

        def conv(u_ref, h_ref, w_ref, b_ref):
            u = u_ref[...]
            r1, r2 = _conv_taps(u, h_ref, first_tok)
            return b_ref[...] + w_ref[0:1, :] * r2 + w_ref[1:2, :] * r1 + w_ref[2:3, :] * u, u, r1, r2

        gate, ugv, g1, g2 = conv(ug_ref, hg_ref, wg_ref, bg_ref)
        up, uuv, u1, u2 = conv(uu_ref, hu_ref, wu_ref, bu_ref)
        gl = _gelu(gate)
        dup = da * gl
        dgate = da * up * _gelu_grad(gate)

        def conv_bwd(du, u, r1, r2, w_ref, c_ref, a_ref, du_ref):
            nxt = c_ref[j]
            n0, n1 = nxt[0:1, :], nxt[1:2, :]
            f1 = jnp.where(row == tm - 1, n0, pltpu.roll(du, tm - 1, 0))
            f2 = jnp.where(row == tm - 1, n1, jnp.where(row == tm - 2, n0, pltpu.roll(du, tm - 2, 0)))
            du_ref[...] = (w_ref[2:3, :] * du + w_ref[1:2, :] * f1 + w_ref[0:1, :] * f2).astype(MXU)
            c_ref[j] = du[0:8, :]
            red = lambda v: jnp.sum(v, axis=0, keepdims=True)
            part = jnp.where(sub == 0, red(du * r2), jnp.where(sub == 1, red(du * r1), jnp.where(
                sub == 2, red(du * u), jnp.where(sub == 3, red(du), 0.0))))
            a_ref[j] = a_ref[j] + part
            return a_ref[j]

        dcg_ref[0] = conv_bwd(dgate, ugv, g1, g2, wg_ref, cg, ag, dug_ref)
        dcu_ref[0] = conv_bwd(dup, uuv, u1, u2, wu_ref, cu, au, duu_ref)

    rev = lambda s: ni - 1 - s
    halo = lambda off: (lambda s, j: (jnp.maximum(rev(s) * (tm // 8) - 1, 0), off + j))
    return pl.pallas_call(
        body, name="ffn_bwd", grid=(ni, nj),
        in_specs=[pl.BlockSpec((tm, 1024), lambda s, j: (rev(s), 0)), pl.BlockSpec((tm, 1024), lambda s, j: (rev(s), 0)),
                  pl.BlockSpec((tm, tn), lambda s, j: (rev(s), j)), pl.BlockSpec((tm, tn), lambda s, j: (rev(s), nj + j)),
                  pl.BlockSpec((8, tn), halo(0)), pl.BlockSpec((8, tn), halo(nj)),
                  pl.BlockSpec((3, tn), lambda s, j: (0, j)), pl.BlockSpec((3, tn), lambda s, j: (0, nj + j)),
                  pl.BlockSpec((1, tn), lambda s, j: (0, j)), pl.BlockSpec((1, tn), lambda s, j: (0, nj + j)),
                  pl.BlockSpec((tn, 1024), lambda s, j: (j, 0)),
                  pl.BlockSpec((1, 1024), lambda s, j: (0, 0))],
        out_specs=[pl.BlockSpec((tm, 1024), lambda s, j: (rev(s), 0)),
                   pl.BlockSpec((tm, tn), lambda s, j: (rev(s), j)), pl.BlockSpec((tm, tn), lambda s, j: (rev(s), j)),
                   pl.BlockSpec((1, 8, tn), lambda s, j: (s, 0, j)), pl.BlockSpec((1, 8, tn), lambda s, j: (s, 0, j)),
                   pl.BlockSpec((1, 1024), lambda s, j: (0, 0))],
        out_shape=[jax.ShapeDtypeStruct((T, 1024), MXU), jax.ShapeDtypeStruct((T, D_FF), MXU),
                   jax.ShapeDtypeStruct((T, D_FF), MXU),
                   jax.ShapeDtypeStruct((ni, 8, D_FF), F32), jax.ShapeDtypeStruct((ni, 8, D_FF), F32),
                   jax.ShapeDtypeStruct((1, 1024), F32)],
        scratch_shapes=[pltpu.VMEM((tm, 1024), MXU)] + [pltpu.VMEM((nj, 8, tn), F32)] * 4,
        compiler_params=_cparams(("arbitrary", "arbitrary")),
    )(dx3, y, u0, u0, u0, u0, convw, convw, convb, convb, wdown, gpost)


ELEMS_PER_BLOCK = 512 * 1024


def _row_block(R, C):
    if R * C <= ELEMS_PER_BLOCK or R % 8:
        return R
    best = 8
    for t in range(8, R + 1, 8):
        if R % t == 0 and t * C <= ELEMS_PER_BLOCK:
            best = t
    return best


def adamw(w, g, m, v, name):
    R, C = w.shape
    partials = g.ndim == 3
    tr = _row_block(R, 2 * C)
    c1 = 1.0 - ADAM_B1 ** ADAM_STEP
    c2 = 1.0 - ADAM_B2 ** ADAM_STEP

    def body(w_ref, g_ref, m_ref, v_ref, g_out, d_ref, nm_ref, nv_ref):
        if partials:
            gv = g_ref[0].astype(F32)
            for d in range(1, N_DEV):
                gv = gv + g_ref[d].astype(F32)
        else:
            gv = g_ref[...]
        g_out[...] = gv
        nm = ADAM_B1 * m_ref[...] + (1.0 - ADAM_B1) * gv
        nv = ADAM_B2 * v_ref[...] + (1.0 - ADAM_B2) * (gv * gv)
        nm_ref[...] = nm
        nv_ref[...] = nv
        d_ref[...] = -ADAM_LR * ((nm / c1) / (jnp.sqrt(nv / c2) + ADAM_EPS) + ADAM_WD * w_ref[...])

    spec = pl.BlockSpec((tr, C), lambda i: (i, 0))
    gspec = pl.BlockSpec((N_DEV, tr, C), lambda i: (0, i, 0)) if partials else spec
    return pl.pallas_call(
        body, name=name, grid=(R // tr,), in_specs=[spec, gspec, spec, spec], out_specs=[spec] * 4,
        out_shape=[jax.ShapeDtypeStruct((R, C), F32)] * 4,
        compiler_params=_cparams(("parallel",)),
    )(w, g, m, v)


def sum_devices(buf, name):
    _, R, C = buf.shape
    tr = _row_block(R, C * 4)

    def body(b_ref, o_ref):
        acc = b_ref[0].astype(F32)
        for d in range(1, N_DEV):
            acc = acc + b_ref[d].astype(F32)
        o_ref[...] = acc

    return pl.pallas_call(
        body, name=name, grid=(R // tr,),
        in_specs=[pl.BlockSpec((N_DEV, tr, C), lambda i: (0, i, 0))],
        out_specs=pl.BlockSpec((tr, C), lambda i: (i, 0)),
        out_shape=jax.ShapeDtypeStruct((R, C), F32),
        compiler_params=_cparams(("parallel",)),
    )(buf)


def _exchange_copies(src_refs, out_refs, send_sems, recv_sems, gather):
    x, y, c = lax.axis_index("x"), lax.axis_index("y"), lax.axis_index("c")
    me = 4 * x + 2 * y + c
    flip = lambda a, bit: 1 - a if bit else a
    part = lambda ref, d: ref if gather else ref.at[d]
    copies = []
    for k in range(1, N_DEV):
        px, py, pc = flip(x, (k >> 2) & 1), flip(y, (k >> 1) & 1), flip(c, k & 1)
        peer = 4 * px + 2 * py + pc
        for t in range(len(src_refs)):
            sem = t * (N_DEV - 1) + k - 1
            mk = lambda s, d: pltpu.make_async_remote_copy(
                src_ref=s, dst_ref=d, send_sem=send_sems.at[sem], recv_sem=recv_sems.at[sem],
                device_id=(px, py, pc), device_id_type=pl.DeviceIdType.MESH)
            copies.append((mk(part(src_refs[t], peer), out_refs[t].at[me]),
                           mk(part(src_refs[t], me), out_refs[t].at[peer])))
    return me, copies


def exchange(srcs, name, gather):
    n = len(srcs)
    shapes = [(N_DEV,) + s.shape if gather else s.shape for s in srcs]

    def body(*refs):
        src_refs, out_refs = refs[:n], refs[n:2 * n]
        send_sems, recv_sems, local_sems = refs[2 * n:]
        me, copies = _exchange_copies(src_refs, out_refs, send_sems, recv_sems, gather)
        for outgoing, _ in copies:
            outgoing.start()
        mine = [pltpu.make_async_copy(src_refs[t] if gather else src_refs[t].at[me], out_refs[t].at[me],
                                      local_sems.at[t]) for t in range(n)]
        for cp in mine:
            cp.start()
        for _, incoming in copies:
            incoming.wait_recv()
        for outgoing, _ in copies:
            outgoing.wait_send()
        for cp in mine:
            cp.wait()

    return pl.pallas_call(
        body, name=name,
        in_specs=[pl.BlockSpec(memory_space=pl.ANY)] * n, out_specs=[pl.BlockSpec(memory_space=pl.ANY)] * n,
        out_shape=[jax.ShapeDtypeStruct(shp, s.dtype) for shp, s in zip(shapes, srcs)],
        scratch_shapes=[pltpu.SemaphoreType.DMA((n * (N_DEV - 1),)), pltpu.SemaphoreType.DMA((n * (N_DEV - 1),)),
                        pltpu.SemaphoreType.DMA((n,))],
    )(*srcs)


def hosted_exchange(body, n_in, n_out, n_scratch, grid, srcs, gather):
    n = len(srcs)
    shapes = [(N_DEV,) + s.shape if gather else s.shape for s in srcs]

    def wrapped(*refs):
        ins, xin = refs[:n_in], refs[n_in:n_in + n]
        outs = refs[n_in + n:n_in + n + n_out]
        xout = refs[n_in + n + n_out:n_in + 2 * n + n_out]
        rest = refs[n_in + 2 * n + n_out:]
        scratch, (send_sems, recv_sems, local_sems) = rest[:n_scratch], rest[n_scratch:]
        ids = [pl.program_id(a) for a in range(len(grid))]
        first = functools.reduce(jnp.logical_and, [i == 0 for i in ids])
        last = functools.reduce(jnp.logical_and, [i == g - 1 for i, g in zip(ids, grid)])
        me, copies = _exchange_copies(xin, xout, send_sems, recv_sems, gather)
        mine = [pltpu.make_async_copy(xin[t] if gather else xin[t].at[me], xout[t].at[me], local_sems.at[t])
                for t in range(n)]

        @pl.when(first)
        def _():
            for outgoing, _ in copies:
                outgoing.start()
            for cp in mine:
                cp.start()

        body(*ins, *outs, *scratch)

        @pl.when(last)
        def _():
            for _, incoming in copies:
                incoming.wait_recv()
            for outgoing, _ in copies:
                outgoing.wait_send()
            for cp in mine:
                cp.wait()

    any_spec = pl.BlockSpec(memory_space=pl.ANY)
    return wrapped, (list(srcs), [any_spec] * n, [any_spec] * n,
                     [jax.ShapeDtypeStruct(shp, s.dtype) for shp, s in zip(shapes, srcs)],
                     [pltpu.SemaphoreType.DMA((n * (N_DEV - 1),)), pltpu.SemaphoreType.DMA((n * (N_DEV - 1),)),
                      pltpu.SemaphoreType.DMA((n,))])


def _pack(parts, cols, row_align, dtype):
    flat = jnp.concatenate([p.astype(dtype) for p in parts], axis=-1)
    n = flat.shape[-1]
    block = cols * row_align
    total = -(-n // block) * block
    flat = jnp.pad(flat, [(0, 0)] * (flat.ndim - 1) + [(0, total - n)])
    return flat.reshape(flat.shape[:-1] + (total // cols, cols))


def _unpack(buf, shapes):
    lead = buf.shape[:-2]
    flat = buf.reshape(lead + (-1,))
    out, off = [], 0
    for s in shapes:
        n = int(np.prod(s))
        out.append(flat[..., off:off + n].reshape(lead + tuple(s)))
        off += n
    return out


SHARD_SHAPES = [(128, IN_COLS), (256, 48), (128, 64), (128, 1024), (1024, 704), (352, 1024)]
SHARDED = ["w_in", "w_uq", "w_ukv", "w_out", "w_up", "w_down"]
PART_A = ["w_in", "w_uq", "w_ukv", "w_out", "w_down"]


def _full_from_shards(name, s):
    if name in ("w_in", "w_out", "w_down"):
        return s.reshape((-1, s.shape[-1]))
    return s.transpose(1, 0, 2).reshape((s.shape[1], -1))


def _shards_from_full(name, f):
    if name in ("w_in", "w_out", "w_down"):
        return f.reshape((N_DEV, -1, f.shape[-1]))
    return f.reshape((f.shape[0], N_DEV, -1)).transpose(1, 0, 2)


def _perm_w_in(w):
    z = lambda n: jnp.zeros((w.shape[0], n), w.dtype)
    return jnp.concatenate([w[:, :1536], w[:, 1540:1924], w[:, 1536:1540], z(60), w[:, 1924:1956], z(32)], axis=1)


def _unperm_w_in(d):
    return jnp.concatenate([d[:, :1536], d[:, 1920:1924], d[:, 1536:1920], d[:, 1984:2016]], axis=1)


def _perm_w_uq(w):
    return jnp.pad(w.reshape(256, 4, MLA_QK_DIM), ((0, 0), (0, 0), (0, 128 - MLA_QK_DIM))).reshape(256, 512)


def _unperm_w_uq(d):
    return d.reshape(256, 4, 128)[:, :, :MLA_QK_DIM].reshape(256, 4 * MLA_QK_DIM)


def _perm_w_ukv(w):
    w4 = w.reshape(128, 4, 128)
    k = jnp.pad(w4[:, :, :64], ((0, 0), (0, 0), (0, 64))).reshape(128, 512)
    return jnp.concatenate([k, w4[:, :, 64:].reshape(128, 256)], axis=1)


def _unperm_w_ukv(d):
    dk = d[:, :512].reshape(128, 4, 128)[:, :, :64]
    dv = d[:, 512:].reshape(128, 4, 64)
    return jnp.concatenate([dk, dv], axis=-1).reshape(128, 512)


def _row(v, width=None):
    v = v.reshape(1, -1).astype(F32)
    if width is not None and v.shape[1] < width:
        v = jnp.pad(v, ((0, 0), (0, width - v.shape[1])))
    return v


def _layer_fwd(x, P, shared, send=None):
    cosr, sinr, bias = shared
    ex = (lambda part: None) if send is None else (lambda part: (send[part], True))
    proj, hT, projb = norm_matmul(x, P["g_pre"], P["w_in_p"], "in_proj", lo_tiles=C_CQ // 512)
    qm, km, vm, cqT, ckvT = mla_prep(proj, P["gq"], P["gkv"], P["w_uq_p"], P["w_ukv_p"], cosr, sinr)
    fcol, frow, frep = fox_gate(proj, P["fbias"])
    oa, lse_a = swa_fwd(proj, bias, P["sinks"])
    (ob, lrb), got_a = flash_fwd(projb, projb, projb, frep, frow, qblk=C_QF // 128, kblk=C_KF // 128,
                                 vblk=C_VF // 128, nq=1, scale=HEAD_DIM ** -0.5, name="fox_fwd", exch=ex(0))
    (oc, lrc), got_b = flash_fwd(qm, km, vm, None, None, qblk=0, kblk=0, vblk=0, nq=2,
                                 scale=MLA_QK_DIM ** -0.5, name="mla_fwd", exch=ex(1))
    x2, y1, mT = attn_out(oa, ob, oc, P["gn"], P["w_out"], P["g_apost"], x)
    u0, h2T = norm_matmul(x2, P["g_fpre"], P["w_up"], "up_proj", tn_pref=1536)
    x3, y2, aT = ffn_fwd(u0, P["conv_w"], P["conv_b"], P["w_down"], P["g_fpost"], x2)
    S = dict(x=x, proj=proj, projb=projb, hT=hT, qm=qm, km=km, vm=vm, cqT=cqT, ckvT=ckvT, fcol=fcol, frow=frow,
             oa=oa, lse_a=lse_a, ob=ob, lrb=lrb, oc=oc, lrc=lrc,
             x2=x2, y1=y1, mT=mT, u0=u0, h2T=h2T, y2=y2, aT=aT)
    return x3, S, (got_a, got_b)


def _layer_bwd(dx3, P, S, shared, send=None):
    cosr, sinr, bias = shared
    ex = (lambda part: None) if send is None else (lambda part: (send[part], False))
    proj = S["proj"]
    G = {}
    dy2, dug, duu, dcg, dcu, G["ffn_post_norm"] = ffn_bwd(dx3, S["y2"], S["u0"], P["conv_w"], P["conv_b"],
                                                         P["w_down"], P["g_fpost"])
    du0 = jnp.concatenate([dug, duu], axis=1)
    dconv = jnp.concatenate([dcg[-1], dcu[-1]], axis=1)
    G["conv_w"], G["conv_b"] = dconv[0:3], dconv[3]
    G["w_down"] = matmul_nn(S["aT"], dy2, "dw_down", MXU)
    G["w_up"] = matmul_nn(S["h2T"], du0, "dw_up", MXU)
    dx2, G["ffn_pre_norm"] = matmul_nt_normbwd(du0, P["w_up"], S["x2"], P["g_fpre"], dx3, "up_bwd")
    dy1, doa, dob, doc, G["group_norm"], G["attn_post_norm"] = attn_out_bwd(
        dx2, S["y1"], S["oa"], S["ob"], S["oc"], P["gn"], P["w_out"], P["g_apost"])
    G["w_out"] = matmul_nn(S["mT"], dy1, "dw_out", MXU)
    dqa, dka, dva, dbias, dsk = swa_bwd(proj, bias, P["sinks"], doa, S["oa"], S["lse_a"])
    G["swa_sinks"] = dsk[:, 0]
    pb = S["projb"]
    (dqf, dkf, dvf, dFk, dFq), got_a = flash_bwd(
        pb, pb, pb, dob, S["ob"], S["lrb"], S["fcol"], S["frow"], name="fox_bwd", qblk=C_QF // 128,
        kblk=C_KF // 128, vblk=C_VF // 128, nq=1, scale=HEAD_DIM ** -0.5, exch=ex(0))
    dmisc_f, dfb = fox_gate_bwd(dFq, dFk, proj, P["fbias"])
    G["forget_bias"] = dfb[0, 0:4]
    (dqm_, dkm_, dvm_), got_b = flash_bwd(
        S["qm"], S["km"], S["vm"], doc, S["oc"], S["lrc"], None, None, name="mla_bwd",
        qblk=0, kblk=0, vblk=0, nq=2, scale=MLA_QK_DIM ** -0.5, exch=ex(1))
    dqm, dkv, dcq, dckv, dmisc_r, G["q_latent_norm"], G["kv_latent_norm"] = mla_prep_bwd(
        dqm_, dkm_, dvm_, proj, P["gq"], P["gkv"], P["w_uq_p"], P["w_ukv_p"], cosr, sinr)
    G["w_uq"] = _unperm_w_uq(matmul_nn(S["cqT"], dqm, "dw_uq", MXU))
    G["w_ukv"] = _unperm_w_ukv(matmul_nn(S["ckvT"], dkv, "dw_ukv", MXU))
    dproj = jnp.concatenate([dqa, dka, dva, dqf, dkf, dvf, dcq, dckv, dmisc_f + dmisc_r], axis=1).astype(MXU)
    G["w_in"] = _unperm_w_in(matmul_nn(S["hT"], dproj, "dw_in", MXU))
    dx, G["attn_pre_norm"] = matmul_nt_normbwd(dproj, P["w_in_p"], S["x"], P["g_pre"], dx2, "in_bwd")
    return dx, G, dbias, (got_a, got_b)


def _layer_params(l, full, small):
    return dict(
        g_pre=_row(small["attn_pre_norm"][l]), w_in_p=_perm_w_in(full["w_in"]),
        gq=_row(small["q_latent_norm"][l]), gkv=_row(small["kv_latent_norm"][l]),
        w_uq_p=_perm_w_uq(full["w_uq"]), w_ukv_p=_perm_w_ukv(full["w_ukv"]),
        fbias=_row(small["forget_bias"][l], 128), sinks=small["swa_sinks"][l].astype(F32),
        gn=_row(small["group_norm"][l]), w_out=full["w_out"], g_apost=_row(small["attn_post_norm"][l]),
        g_fpre=_row(small["ffn_pre_norm"][l]), w_up=full["w_up"], conv_w=full["conv_w"],
        conv_b=_row(small["conv_b"][l]), w_down=full["w_down"], g_fpost=_row(small["ffn_post_norm"][l]))


def _rel_bias_grad(dbias, bucket):
    flat = dbias.reshape(SWA_Q_HEADS, -1)
    hi = flat.astype(MXU)
    lo = (flat - hi.astype(F32)).astype(MXU)
    onehot = (bucket[:, None] == jnp.arange(128, dtype=jnp.int32)[None, :]).astype(MXU)
    r = matmul_nn(jnp.concatenate([hi, lo], axis=0), onehot, "rel_bias_grad")
    return (r[0:8] + r[8:16])[:, :REL_BUCKETS].T


def local_step(x, tgt, fulls, small, comm=None):
    T = x.shape[0]
    cosr, sinr = rope_tables(T)
    bias, bucket = swa_bias_table(small["rel_bias"])
    shared = (cosr, sinr, bias)
    Ps, Ss = [], []
    h, full = x, fulls[0]
    for l in range(DEPTH):
        P = _layer_params(l, full, small)
        more = l + 1 < DEPTH
        h, S, got = _layer_fwd(h, P, shared, comm["weight_parts"](l + 1) if comm and more else None)
        if more:
            full = comm["full_from"](got) if comm else fulls[l + 1]
        Ps.append(P)
        Ss.append(S)
    dh, sq = loss_kernel(h, tgt)
    grads = [None] * DEPTH
    dbias_sum = None
    pending = None
    for l in reversed(range(DEPTH)):
        dh, grads[l], dbias, got = _layer_bwd(dh, Ps[l], Ss[l], shared, pending)
        dbias_sum = dbias if dbias_sum is None else dbias_sum + dbias
        if pending is not None:
            comm["grads_landed"](l + 1, got)
        pending = comm["grad_parts"](grads[l]) if comm else None
    return sq, dh, grads, _rel_bias_grad(dbias_sum, bucket), pending


WEIGHTS = ['attn_pre_norm', 'w_in', 'forget_bias', 'swa_sinks', 'rel_bias', 'q_latent_norm', 'w_uq',
           'kv_latent_norm', 'w_ukv', 'group_norm', 'w_out', 'attn_post_norm', 'ffn_pre_norm', 'w_up', 'conv_w',
           'conv_b', 'w_down', 'ffn_post_norm']
SMALL_PER_LAYER = ['attn_pre_norm', 'forget_bias', 'swa_sinks', 'q_latent_norm', 'kv_latent_norm', 'group_norm',
                   'attn_post_norm', 'ffn_pre_norm', 'conv_b', 'ffn_post_norm', 'conv_w']


def kernel(x, attn_pre_norm, w_in, forget_bias, swa_sinks, rel_bias, q_latent_norm, w_uq, kv_latent_norm, w_ukv, group_norm, w_out, attn_post_norm, ffn_pre_norm, w_up, conv_w, conv_b, w_down, ffn_post_norm, loss_target, m_attn_pre_norm, m_w_in, m_forget_bias, m_swa_sinks, m_rel_bias, m_q_latent_norm, m_w_uq, m_kv_latent_norm, m_w_ukv, m_group_norm, m_w_out, m_attn_post_norm, m_ffn_pre_norm, m_w_up, m_conv_w, m_conv_b, m_w_down, m_ffn_post_norm, v_attn_pre_norm, v_w_in, v_forget_bias, v_swa_sinks, v_rel_bias, v_q_latent_norm, v_w_uq, v_kv_latent_norm, v_w_ukv, v_group_norm, v_w_out, v_attn_post_norm, v_ffn_pre_norm, v_w_up, v_conv_w, v_conv_b, v_w_down, v_ffn_post_norm):
    W = dict(attn_pre_norm=attn_pre_norm, w_in=w_in, forget_bias=forget_bias, swa_sinks=swa_sinks, rel_bias=rel_bias,
             q_latent_norm=q_latent_norm, w_uq=w_uq, kv_latent_norm=kv_latent_norm, w_ukv=w_ukv,
             group_norm=group_norm, w_out=w_out, attn_post_norm=attn_post_norm, ffn_pre_norm=ffn_pre_norm,
             w_up=w_up, conv_w=conv_w, conv_b=conv_b, w_down=w_down, ffn_post_norm=ffn_post_norm)
    M = dict(attn_pre_norm=m_attn_pre_norm, w_in=m_w_in, forget_bias=m_forget_bias, swa_sinks=m_swa_sinks,
             rel_bias=m_rel_bias, q_latent_norm=m_q_latent_norm, w_uq=m_w_uq, kv_latent_norm=m_kv_latent_norm,
             w_ukv=m_w_ukv, group_norm=m_group_norm, w_out=m_w_out, attn_post_norm=m_attn_post_norm,
             ffn_pre_norm=m_ffn_pre_norm, w_up=m_w_up, conv_w=m_conv_w, conv_b=m_conv_b, w_down=m_w_down,
             ffn_post_norm=m_ffn_post_norm)
    V = dict(attn_pre_norm=v_attn_pre_norm, w_in=v_w_in, forget_bias=v_forget_bias, swa_sinks=v_swa_sinks,
             rel_bias=v_rel_bias, q_latent_norm=v_q_latent_norm, w_uq=v_w_uq, kv_latent_norm=v_kv_latent_norm,
             w_ukv=v_w_ukv, group_norm=v_group_norm, w_out=v_w_out, attn_post_norm=v_attn_post_norm,
             ffn_pre_norm=v_ffn_pre_norm, w_up=v_w_up, conv_w=v_conv_w, conv_b=v_conv_b, w_down=v_w_down,
             ffn_post_norm=v_ffn_post_norm)
    me = 4 * lax.axis_index("x") + 2 * lax.axis_index("y") + lax.axis_index("c")

    def weight_parts(l):
        return ([W[n][l].astype(MXU) for n in PART_A], [W["w_up"][l].astype(MXU), conv_w[l]])

    def full_from(got):
        full = {n: _full_from_shards(n, s) for n, s in zip(PART_A, got[0])}
        full["w_up"] = _full_from_shards("w_up", got[1][0])
        full["conv_w"] = got[1][1].transpose(1, 0, 2).reshape(3, 2 * D_FF)
        return full

    def grad_parts(g):
        return ([_shards_from_full(n, g[n]) for n in PART_A], [_shards_from_full("w_up", g["w_up"])])

    landed = [None] * DEPTH

    def grads_landed(l, got):
        landed[l] = dict(zip(PART_A, got[0]), w_up=got[1][0])

    comm = dict(weight_parts=weight_parts, full_from=full_from, grad_parts=grad_parts, grads_landed=grads_landed)
    a0, b0 = weight_parts(0)
    got = exchange(a0 + b0, "gather_weights", True)
    full0 = full_from((got[:len(a0)], got[len(a0):]))
    sq, dx, grads, drel, last = local_step(x[0], loss_target[0], [full0], W, comm)
    got = exchange(last[0] + last[1], "scatter_grads", False)
    grads_landed(0, (got[:len(last[0])], got[len(last[0]):]))
    G = {n: jnp.stack([landed[l][n] for l in range(DEPTH)], axis=1) for n in SHARDED}

    parts, shapes = [], []
    for l in range(DEPTH):
        for n in SMALL_PER_LAYER:
            parts.append(grads[l][n].astype(F32).reshape(-1))
            shapes.append(grads[l][n].shape)
    parts += [drel.reshape(-1), jnp.sum(sq).reshape(1) * (0.5 / D_MODEL)]
    shapes += [drel.shape, (1,)]
    red = _unpack(sum_devices(exchange([_pack(parts, 128, 8, F32)], "gather_small", True)[0], "sum_small"), shapes)
    k = 0
    per = {n: [] for n in SMALL_PER_LAYER}
    for l in range(DEPTH):
        for n in SMALL_PER_LAYER:
            per[n].append(red[k])
            k += 1
    for n in SMALL_PER_LAYER:
        G[n] = jnp.stack(per[n]).reshape((DEPTH, 3, 2 * D_FF) if n == "conv_w" else W[n].shape)
    G["rel_bias"] = red[k]
    loss = red[k + 1][0]
    G["conv_w"] = lax.dynamic_slice_in_dim(G["conv_w"], me * 704, 704, axis=2)

    delta, new_m, new_v = {}, {}, {}
    for n in WEIGHTS:
        shp = W[n].shape
        v2 = lambda a: a.reshape(-1, shp[-1])
        g = G[n].reshape(N_DEV, -1, shp[-1]) if n in SHARDED else v2(G[n])
        g, d, nm, nv = adamw(v2(W[n]), g, v2(M[n]), v2(V[n]), "adamw_" + n)
        G[n], delta[n], new_m[n], new_v[n] = g.reshape(shp), d.reshape(shp), nm.reshape(shp), nv.reshape(shp)
    return (loss, dx[None], *[G[n] for n in WEIGHTS], *[delta[n] for n in WEIGHTS],
            *[new_m[n] for n in WEIGHTS], *[new_v[n] for n in WEIGHTS])
```

```python
import functools
import math

import numpy as np
import jax
import jax.numpy as jnp
from jax import lax
from jax.experimental import pallas as pl
from jax.experimental.pallas import tpu as pltpu

F32 = jnp.float32
MXU = jnp.bfloat16

N_DEV = 8
DEPTH = 4
D_MODEL = 1024
HEAD_DIM = 64
WINDOW = 128
SWA_Q_HEADS = 8
REL_BUCKETS = 32
REL_MAX_DIST = 128
MLA_QK_DIM = 96
ROPE_DIM = 32
ROPE_THETA = 10000.0
D_FF = 2816
EPS = 1e-6
NEG = -1e30
IN_COLS = 1956
IN_COLS_P = 2048
C_QA, C_KA, C_VA = 0, 512, 640
C_QF, C_KF, C_VF = 768, 1024, 1280
C_CQ, C_CKV, C_MISC = 1536, 1792, 1920
ROPE_LANE0 = 64
ADAM_LR, ADAM_B1, ADAM_B2, ADAM_EPS, ADAM_WD, ADAM_STEP = 0.001, 0.9, 0.999, 1e-08, 0.01, 10

VMEM_LIMIT = 56 * 1024 * 1024
PACK_COLS = 1024
PACK_ROW_ALIGN = 16


def _cparams(sem=None):
    return pltpu.CompilerParams(dimension_semantics=sem, vmem_limit_bytes=VMEM_LIMIT)


def _tile(n, pref):
    if n <= pref:
        return n
    t = pref - pref % 128
    while t >= 128:
        if n % t == 0:
            return t
        t -= 128
    return n


def _dot(a, b):
    return jnp.dot(a.astype(MXU), b.astype(MXU), preferred_element_type=F32)


def _dot_nt(a, b):
    return lax.dot_general(a.astype(MXU), b.astype(MXU), (((1,), (1,)), ((), ())),
                           preferred_element_type=F32)


def _rms_fwd(x, g):
    return x * lax.rsqrt(jnp.mean(x * x, axis=-1, keepdims=True) + EPS) * g


def _rms_bwd(dy, x, g, n=None):
    r = lax.rsqrt(jnp.mean(x * x, axis=-1, keepdims=True) + EPS)
    xh = x * r
    dg = jnp.sum(dy * xh, axis=0, keepdims=True)
    dxh = dy * g
    dx = r * (dxh - xh * jnp.mean(dxh * xh, axis=-1, keepdims=True))
    return dx, dg


def _acc_out(ref, val, first):
    @pl.when(first)
    def _():
        ref[...] = val

    @pl.when(jnp.logical_not(first))
    def _():
        ref[...] += val


def norm_matmul(x, g, w, name, lo_tiles=0, tn_pref=512):
    T, K = x.shape
    N = w.shape[1]
    tm, tn = _tile(T, 1024), _tile(N, tn_pref)

    def body(x_ref, g_ref, w_ref, o_ref, hT_ref, *rest):
        h_sc = rest[-1]
        j = pl.program_id(1)

        @pl.when(j == 0)
        def _():
            h = _rms_fwd(x_ref[...], g_ref[...])
            h_sc[...] = h.astype(MXU)
            hT_ref[...] = h.T.astype(MXU)

        r = jnp.dot(h_sc[...], w_ref[...], preferred_element_type=F32)
        o_ref[...] = r
        if lo_tiles:
            @pl.when(j < lo_tiles)
            def _():
                rest[0][...] = r.astype(MXU)

    out_specs = [pl.BlockSpec((tm, tn), lambda i, j: (i, j)), pl.BlockSpec((K, tm), lambda i, j: (0, i))]
    out_shape = [jax.ShapeDtypeStruct((T, N), F32), jax.ShapeDtypeStruct((K, T), MXU)]
    if lo_tiles:
        out_specs.append(pl.BlockSpec((tm, tn), lambda i, j: (i, jnp.minimum(j, lo_tiles - 1))))
        out_shape.append(jax.ShapeDtypeStruct((T, lo_tiles * tn), MXU))
    return pl.pallas_call(
        body, name=name, grid=(T // tm, N // tn),
        in_specs=[pl.BlockSpec((tm, K), lambda i, j: (i, 0)),
                  pl.BlockSpec((1, K), lambda i, j: (0, 0)),
                  pl.BlockSpec((K, tn), lambda i, j: (0, j))],
        out_specs=out_specs, out_shape=out_shape,
        scratch_shapes=[pltpu.VMEM((tm, K), MXU)],
        compiler_params=_cparams(("parallel", "arbitrary")),
    )(x, g, w)


def matmul_nn(a, b, name, out_dtype=F32):
    M, K = a.shape
    N = b.shape[1]
    tm, tn, tk = _tile(M, 1408), _tile(N, 1536), _tile(K, 1024)
    nk = K // tk

    def body(a_ref, b_ref, o_ref, acc):
        k = pl.program_id(2)
        part = _dot(a_ref[...], b_ref[...])
        _acc_out(acc, part, k == 0)

        @pl.when(k == nk - 1)
        def _():
            o_ref[...] = acc[...].astype(out_dtype)

    return pl.pallas_call(
        body, name=name, grid=(M // tm, N // tn, nk),
        in_specs=[pl.BlockSpec((tm, tk), lambda i, j, k: (i, k)),
                  pl.BlockSpec((tk, tn), lambda i, j, k: (k, j))],
        out_specs=pl.BlockSpec((tm, tn), lambda i, j, k: (i, j)),
        out_shape=jax.ShapeDtypeStruct((M, N), out_dtype),
        scratch_shapes=[pltpu.VMEM((tm, tn), F32)],
        compiler_params=_cparams(("parallel", "parallel", "arbitrary")),
    )(a, b)


def matmul_nt_normbwd(dy, w, x, g, dres, name):
    T, N = dy.shape
    K = w.shape[0]
    tm, tn = _tile(T, 1024), _tile(N, 1536)
    nj = N // tn

    def body(dy_ref, w_ref, x_ref, g_ref, dres_ref, dx_ref, dg_ref, acc):
        i, j = pl.program_id(0), pl.program_id(1)
        _acc_out(acc, _dot_nt(dy_ref[...], w_ref[...]), j == 0)

        @pl.when(j == nj - 1)
        def _():
            dx, dg = _rms_bwd(acc[...], x_ref[...], g_ref[...])
            dx_ref[...] = dres_ref[...] + dx
            _acc_out(dg_ref, dg, i == 0)

    return pl.pallas_call(
        body, name=name, grid=(T // tm, nj),
        in_specs=[pl.BlockSpec((tm, tn), lambda i, j: (i, j)),
                  pl.BlockSpec((K, tn), lambda i, j: (0, j)),
                  pl.BlockSpec((tm, K), lambda i, j: (i, 0)),
                  pl.BlockSpec((1, K), lambda i, j: (0, 0)),
                  pl.BlockSpec((tm, K), lambda i, j: (i, 0))],
        out_specs=[pl.BlockSpec((tm, K), lambda i, j: (i, 0)),
                   pl.BlockSpec((1, K), lambda i, j: (0, 0))],
        out_shape=[jax.ShapeDtypeStruct((T, K), F32), jax.ShapeDtypeStruct((1, K), F32)],
        scratch_shapes=[pltpu.VMEM((tm, K), F32)],
        compiler_params=_cparams(("arbitrary", "arbitrary")),
    )(dy, w, x, g, dres)


def loss_kernel(y, tgt):
    T, D = y.shape
    tm = _tile(T, 512)

    def body(y_ref, t_ref, dy_ref, acc_ref):
        e = y_ref[...] - t_ref[...]
        dy_ref[...] = e * (1.0 / D)
        _acc_out(acc_ref, jnp.sum(e * e, axis=0, keepdims=True), pl.program_id(0) == 0)

    return pl.pallas_call(
        body, name="loss", grid=(T // tm,),
        in_specs=[pl.BlockSpec((tm, D), lambda i: (i, 0)), pl.BlockSpec((tm, D), lambda i: (i, 0))],
        out_specs=[pl.BlockSpec((tm, D), lambda i: (i, 0)), pl.BlockSpec((1, D), lambda i: (0, 0))],
        out_shape=[jax.ShapeDtypeStruct((T, D), F32), jax.ShapeDtypeStruct((1, D), F32)],
        compiler_params=_cparams(("arbitrary",)),
    )(y, tgt)


def _rope_partner(x):
    lane = lax.broadcasted_iota(jnp.int32, (1, 128), 1)
    return jnp.where(lane < ROPE_LANE0 + ROPE_DIM // 2, pltpu.roll(x, 128 - ROPE_DIM // 2, 1),
                     pltpu.roll(x, ROPE_DIM // 2, 1))


def _rope_apply(x, cos, sin_signed):
    return x * cos + _rope_partner(x) * sin_signed


def _rope_apply_bwd(dy, cos, sin_signed):
    lane = lax.broadcasted_iota(jnp.int32, (1, 128), 1)
    rotary = (lane >= ROPE_LANE0) & (lane < ROPE_LANE0 + ROPE_DIM)
    return dy * cos + jnp.where(rotary, _rope_partner(dy * sin_signed), 0.0)


def rope_tables(T):
    pos = jnp.arange(T, dtype=F32)
    inv_freq = ROPE_THETA ** (-(jnp.arange(ROPE_DIM // 2, dtype=F32) * 2.0 / ROPE_DIM))
    ang = pos[:, None] * inv_freq[None, :]
    cos, sin = jnp.cos(ang), jnp.sin(ang)
    z = jnp.zeros((T, ROPE_LANE0), F32)
    z2 = jnp.zeros((T, 128 - ROPE_LANE0 - ROPE_DIM), F32)
    cosr = jnp.concatenate([z, cos, cos, z2], axis=1)
    sinr = jnp.concatenate([z, -sin, sin, z2], axis=1)
    return cosr, sinr


def mla_prep(proj, gq, gkv, wuq, wukv, cosr, sinr):
    T = proj.shape[0]
    tm = _tile(T, 512)

    def body(cq_ref, ckv_ref, misc_ref, gq_ref, gkv_ref, wuq_ref, wukv_ref, cos_ref, sin_ref,
             q_ref, k_ref, v_ref, cqT_ref, ckvT_ref):
        lane = lax.broadcasted_iota(jnp.int32, (1, 128), 1)
        cosr_, sinr_ = cos_ref[...], sin_ref[...]
        cosq = cosr_ + jnp.where(lane < ROPE_LANE0, 1.0, 0.0)
        cqn = _rms_fwd(cq_ref[...], gq_ref[...])
        cqT_ref[...] = cqn.T.astype(MXU)
        qm = _dot(cqn, wuq_ref[...])
        q_ref[...] = jnp.concatenate(
            [_rope_apply(qm[:, 128 * h:128 * (h + 1)], cosq, sinr_) for h in range(4)], axis=1).astype(MXU)
        ckvn = _rms_fwd(ckv_ref[...], gkv_ref[...])
        ckvT_ref[...] = ckvn.T.astype(MXU)
        kv = _dot(ckvn, wukv_ref[...])
        kr = _rope_apply(misc_ref[...], cosr_, sinr_)
        k_ref[...] = jnp.concatenate(
            [kv[:, 128 * h:128 * (h + 1)] + kr for h in range(4)], axis=1).astype(MXU)
        v_ref[...] = kv[:, 512:768].astype(MXU)

    row = lambda i: (i, 0)
    const = lambda i: (0, 0)
    return pl.pallas_call(
        body, name="mla_prep", grid=(T // tm,),
        in_specs=[pl.BlockSpec((tm, 256), lambda i: (i, C_CQ // 256)),
                  pl.BlockSpec((tm, 128), lambda i: (i, C_CKV // 128)),
                  pl.BlockSpec((tm, 128), lambda i: (i, C_MISC // 128)),
                  pl.BlockSpec((1, 256), const), pl.BlockSpec((1, 128), const),
                  pl.BlockSpec((256, 512), const), pl.BlockSpec((128, 768), const),
                  pl.BlockSpec((tm, 128), row), pl.BlockSpec((tm, 128), row)],
        out_specs=[pl.BlockSpec((tm, 512), row), pl.BlockSpec((tm, 512), row), pl.BlockSpec((tm, 256), row),
                   pl.BlockSpec((256, tm), lambda i: (0, i)), pl.BlockSpec((128, tm), lambda i: (0, i))],
        out_shape=[jax.ShapeDtypeStruct((T, 512), MXU), jax.ShapeDtypeStruct((T, 512), MXU),
                   jax.ShapeDtypeStruct((T, 256), MXU),
                   jax.ShapeDtypeStruct((256, T), MXU), jax.ShapeDtypeStruct((128, T), MXU)],
        compiler_params=_cparams(("parallel",)),
    )(proj, proj, proj, gq, gkv, wuq, wukv, cosr, sinr)


def mla_prep_bwd(dq, dk, dv, proj, gq, gkv, wuq, wukv, cosr, sinr):
    T = proj.shape[0]
    tm = _tile(T, 512)

    def body(dq_ref, dk_ref, dv_ref, cq_ref, ckv_ref, gq_ref, gkv_ref, wuq_ref, wukv_ref, cos_ref, sin_ref,
             dqm_ref, dkv_ref, dcq_ref, dckv_ref, dmisc_ref, dgq_ref, dgkv_ref):
        first = pl.program_id(0) == 0
        lane = lax.broadcasted_iota(jnp.int32, (1, 128), 1)
        cosr_, sinr_ = cos_ref[...], sin_ref[...]
        cosq = cosr_ + jnp.where(lane < ROPE_LANE0, 1.0, 0.0)
        dqv = dq_ref[...]
        dqm = jnp.concatenate(
            [_rope_apply_bwd(dqv[:, 128 * h:128 * (h + 1)], cosq, sinr_) for h in range(4)], axis=1)
        dqm_ref[...] = dqm.astype(MXU)
        dcq, dgq = _rms_bwd(_dot_nt(dqm, wuq_ref[...]), cq_ref[...], gq_ref[...])
        dcq_ref[...] = dcq
        _acc_out(dgq_ref, dgq, first)
        dkv_ = dk_ref[...]
        heads = [dkv_[:, 128 * h:128 * (h + 1)] for h in range(4)]
        dkr = heads[0] + heads[1] + heads[2] + heads[3]
        dmisc_ref[...] = _rope_apply_bwd(dkr, cosr_, sinr_)
        dkvm = jnp.concatenate([jnp.where(lane < ROPE_LANE0, hd, 0.0) for hd in heads] + [dv_ref[...]], axis=1)
        dkv_ref[...] = dkvm.astype(MXU)
        dckv, dgkv = _rms_bwd(_dot_nt(dkvm, wukv_ref[...]), ckv_ref[...], gkv_ref[...])
        dckv_ref[...] = dckv
        _acc_out(dgkv_ref, dgkv, first)

    row = lambda i: (i, 0)
    const = lambda i: (0, 0)
    return pl.pallas_call(
        body, name="mla_prep_bwd", grid=(T // tm,),
        in_specs=[pl.BlockSpec((tm, 512), row), pl.BlockSpec((tm, 512), row), pl.BlockSpec((tm, 256), row),
                  pl.BlockSpec((tm, 256), lambda i: (i, C_CQ // 256)),
                  pl.BlockSpec((tm, 128), lambda i: (i, C_CKV // 128)),
                  pl.BlockSpec((1, 256), const), pl.BlockSpec((1, 128), const),
                  pl.BlockSpec((256, 512), const), pl.BlockSpec((128, 768), const),
                  pl.BlockSpec((tm, 128), row), pl.BlockSpec((tm, 128), row)],
        out_specs=[pl.BlockSpec((tm, 512), row), pl.BlockSpec((tm, 768), row), pl.BlockSpec((tm, 256), row),
                   pl.BlockSpec((tm, 128), row), pl.BlockSpec((tm, 128), row),
                   pl.BlockSpec((1, 256), const), pl.BlockSpec((1, 128), const)],
        out_shape=[jax.ShapeDtypeStruct((T, 512), MXU), jax.ShapeDtypeStruct((T, 768), MXU),
                   jax.ShapeDtypeStruct((T, 256), F32), jax.ShapeDtypeStruct((T, 128), F32),
                   jax.ShapeDtypeStruct((T, 128), F32),
                   jax.ShapeDtypeStruct((1, 256), F32), jax.ShapeDtypeStruct((1, 128), F32)],
        compiler_params=_cparams(("arbitrary",)),
    )(dq, dk, dv, proj, proj, gq, gkv, wuq, wukv, cosr, sinr)


def _split3(x):
    hi = x.astype(MXU)
    r1 = x - hi.astype(F32)
    mid = r1.astype(MXU)
    lo = (r1 - mid.astype(F32)).astype(MXU)
    return hi, mid, lo


def _tri_matmul(tri, x):
    hi, mid, lo = _split3(x)
    d = lambda p: jnp.dot(tri, p, preferred_element_type=F32)
    return d(hi) + d(mid) + d(lo)


def _log_sigmoid(z):
    return jnp.minimum(z, 0.0) - jnp.log(1.0 + jnp.exp(-jnp.abs(z)))


def fox_gate(proj, fbias):
    T = proj.shape[0]
    tb = _tile(T, 512)

    def body(misc_ref, b_ref, fc_ref, fr_ref, frep_ref, carry):
        @pl.when(pl.program_id(0) == 0)
        def _():
            carry[...] = jnp.zeros_like(carry)

        lane = lax.broadcasted_iota(jnp.int32, (1, 128), 1)
        lf = jnp.where(lane < 4, _log_sigmoid(misc_ref[...] + b_ref[...]), 0.0)
        r = lax.broadcasted_iota(jnp.int32, (tb, tb), 0)
        c = lax.broadcasted_iota(jnp.int32, (tb, tb), 1)
        tri = jnp.where(r >= c, 1.0, 0.0).astype(MXU)
        F = _tri_matmul(tri, lf) + carry[...]
        carry[...] = carry[...] + jnp.sum(lf, axis=0, keepdims=True)
        fc_ref[0] = F
        fc_ref[1] = pltpu.roll(F, 126, 1)
        ft = F.T[0:8, :]
        fr_ref[0] = ft
        fr_ref[1] = pltpu.roll(ft, 6, 0)
        for h in range(4):
            frep_ref[h] = jnp.broadcast_to(_lane_pick(F, h), (tb, 128))

    return pl.pallas_call(
        body, name="fox_gate", grid=(T // tb,),
        in_specs=[pl.BlockSpec((tb, 128), lambda i: (i, C_MISC // 128)), pl.BlockSpec((1, 128), lambda i: (0, 0))],
        out_specs=[pl.BlockSpec((2, tb, 128), lambda i: (0, i, 0)), pl.BlockSpec((2, 8, tb), lambda i: (0, 0, i)),
                   pl.BlockSpec((4, tb, 128), lambda i: (0, i, 0))],
        out_shape=[jax.ShapeDtypeStruct((2, T, 128), F32), jax.ShapeDtypeStruct((2, 8, T), F32),
                   jax.ShapeDtypeStruct((4, T, 128), F32)],
        scratch_shapes=[pltpu.VMEM((1, 128), F32)],
        compiler_params=_cparams(("arbitrary",)),
    )(proj, fbias)


def fox_gate_bwd(dFq, dFk, proj, fbias):
    T = proj.shape[0]
    tb = _tile(T, 512)
    nb = T // tb

    def body(dq_ref, dk_ref, misc_ref, b_ref, dm_ref, db_ref, carry):
        first = pl.program_id(0) == 0

        @pl.when(first)
        def _():
            carry[...] = jnp.zeros_like(carry)

        lane = lax.broadcasted_iota(jnp.int32, (1, 128), 1)
        dF = jnp.where(lane < 4, (dq_ref[0] + dk_ref[0]) + pltpu.roll(dq_ref[1] + dk_ref[1], 2, 1), 0.0)
        r = lax.broadcasted_iota(jnp.int32, (tb, tb), 0)
        c = lax.broadcasted_iota(jnp.int32, (tb, tb), 1)
        tri = jnp.where(r <= c, 1.0, 0.0).astype(MXU)
        dlf = _tri_matmul(tri, dF) + carry[...]
        carry[...] = carry[...] + jnp.sum(dF, axis=0, keepdims=True)
        z = misc_ref[...] + b_ref[...]
        dz = jnp.where(lane < 4, dlf * (1.0 / (1.0 + jnp.exp(z))), 0.0)
        dm_ref[...] = dz
        _acc_out(db_ref, jnp.sum(dz, axis=0, keepdims=True), first)

    return pl.pallas_call(
        body, name="fox_gate_bwd", grid=(nb,),
        in_specs=[pl.BlockSpec((2, tb, 128), lambda i: (0, nb - 1 - i, 0)),
                  pl.BlockSpec((2, tb, 128), lambda i: (0, nb - 1 - i, 0)),
                  pl.BlockSpec((tb, 128), lambda i: (nb - 1 - i, C_MISC // 128)),
                  pl.BlockSpec((1, 128), lambda i: (0, 0))],
        out_specs=[pl.BlockSpec((tb, 128), lambda i: (nb - 1 - i, 0)), pl.BlockSpec((1, 128), lambda i: (0, 0))],
        out_shape=[jax.ShapeDtypeStruct((T, 128), F32), jax.ShapeDtypeStruct((1, 128), F32)],
        scratch_shapes=[pltpu.VMEM((1, 128), F32)],
        compiler_params=_cparams(("arbitrary",)),
    )(dFq, dFk, proj, fbias)


FLASH_TILE = 512


def _row_stat_tile(a, b, n):
    at = jnp.broadcast_to(a, (n, 128)).T[0:8, :]
    bt = jnp.broadcast_to(b, (n, 128)).T[0:8, :]
    sub = lax.broadcasted_iota(jnp.int32, (8, 1), 0)
    return jnp.where(sub == 0, at, jnp.where(sub == 1, bt, 0.0))


def _col_stat_tile(a, b):
    lane = lax.broadcasted_iota(jnp.int32, (1, 128), 1)
    return jnp.where(lane == 0, a, jnp.where(lane == 1, b, 0.0))


def _lane_pick(x, h):
    lane = lax.broadcasted_iota(jnp.int32, (1, 128), 1)
    return jnp.sum(jnp.where(lane == h, x, 0.0), axis=1, keepdims=True)


def _half_mask(h):
    lane = lax.broadcasted_iota(jnp.int32, (1, 128), 1)
    return (lane // HEAD_DIM) == h


def _call_hosting(body, name, grid, args, in_specs, out_specs, out_shape, scratch, exch):
    n_out = len(out_shape)
    if exch is not None:
        body, (xargs, xin, xout, xshape, xscratch) = hosted_exchange(
            body, len(args), n_out, len(scratch), grid, *exch)
        args, in_specs, out_specs = args + xargs, in_specs + xin, out_specs + xout
        out_shape, scratch = out_shape + xshape, scratch + xscratch
    res = pl.pallas_call(
        body, name=name, grid=grid, in_specs=in_specs, out_specs=out_specs, out_shape=out_shape,
        scratch_shapes=scratch, compiler_params=_cparams(("arbitrary",) * len(grid)),
    )(*args)
    return res[:n_out], res[n_out:]


def flash_fwd(q, k, v, frep, frow, *, qblk, kblk, vblk, nq, scale, name, exch=None):
    T = q.shape[0]
    tq = tk = _tile(T, FLASH_TILE)
    wq = 128 * nq
    has_f = frep is not None

    def body(*refs):
        if has_f:
            q_ref, k_ref, v_ref, fk_ref, fr_ref, o_ref, lr_ref, vT_sc, m_sc, acc_sc = refs
        else:
            q_ref, k_ref, v_ref, o_ref, lr_ref, vT_sc, m_sc, acc_sc = refs
        i = pl.program_id(1)

        @pl.when(i == 0)
        def _():
            vT_sc[...] = v_ref[...].astype(F32).T.astype(MXU)

        diag = lax.broadcasted_iota(jnp.int32, (tk, 1), 0) <= lax.broadcasted_iota(jnp.int32, (1, tq), 1)
        row_half = lax.broadcasted_iota(jnp.int32, (128, 1), 0) // HEAD_DIM
        qb = q_ref[...].astype(F32) * scale
        if nq == 1:
            qhs = [jnp.where(_half_mask(h), qb, 0).astype(MXU) for h in range(2)]
        else:
            qhs = [qb[:, 128 * h:128 * (h + 1)].astype(MXU) for h in range(2)]
        for h in range(2):
            m_sc[h] = jnp.full((1, tq), NEG, F32)
            acc_sc[h] = jnp.zeros((128, tq), F32)

        def make_step(masked):
            def step(j, carry):
                off = pl.multiple_of(j * tk, tk)
                ks = k_ref[pl.ds(off, tk), :]
                vT = vT_sc[:, pl.ds(off, tk)]
                for h in range(2):
                    kh = ks if nq == 1 else ks[:, 128 * h:128 * (h + 1)]
                    sT = _dot_nt(kh, qhs[h])
                    if has_f:
                        fk = fk_ref[h, pl.ds(off, tk), :]
                        sT = sT + (fr_ref[0, h:h + 1, :] - jnp.concatenate([fk] * (tq // 128), axis=1))
                    if masked:
                        sT = jnp.where(diag, sT, NEG)
                    m_prev = m_sc[h]
                    m_new = jnp.maximum(m_prev, jnp.max(sT, axis=0, keepdims=True))
                    alpha = jnp.exp(m_prev - m_new)
                    pT = jnp.exp(sT - m_new)
                    vTh = jnp.where(row_half == h, vT, jnp.ones_like(vT))
                    acc_sc[h] = alpha * acc_sc[h] + _dot(vTh, pT)
                    m_sc[h] = m_new
                return carry
            return step

        lax.fori_loop(0, i, make_step(False), 0)
        make_step(True)(i, 0)
        outs, lses = [], []
        for h in range(2):
            acc = acc_sc[h]
            outs.append(acc / pltpu.roll(acc, HEAD_DIM, 0))
            l = acc_sc[h, HEAD_DIM * (1 - h):HEAD_DIM * (1 - h) + 1, :]
            lses.append(m_sc[h] + jnp.log(l))
        o_ref[...] = jnp.where(row_half == 0, outs[0], outs[1]).T
        sub = lax.broadcasted_iota(jnp.int32, (8, 1), 0)
        lr_ref[0] = jnp.where(sub == 0, lses[0], jnp.where(sub == 1, lses[1], 0.0))

    in_specs = [pl.BlockSpec((tq, wq), lambda p, i: (i, qblk + p)),
                pl.BlockSpec((T, wq), lambda p, i: (0, kblk + p)),
                pl.BlockSpec((T, 128), lambda p, i: (0, vblk + p))]
    args = [q, k, v]
    if has_f:
        in_specs += [pl.BlockSpec((2, T, 128), lambda p, i: (p, 0, 0)),
                     pl.BlockSpec((1, 8, tq), lambda p, i: (p, 0, i))]
        args += [frep, frow]
    out_specs = [pl.BlockSpec((tq, 128), lambda p, i: (i, p)), pl.BlockSpec((1, 8, tq), lambda p, i: (p, 0, i))]
    out_shape = [jax.ShapeDtypeStruct((T, 256), F32), jax.ShapeDtypeStruct((2, 8, T), F32)]
    scratch = [pltpu.VMEM((128, T), MXU), pltpu.VMEM((2, 1, tq), F32), pltpu.VMEM((2, 128, tq), F32)]
    return _call_hosting(body, name, (2, T // tq), args, in_specs, out_specs, out_shape, scratch, exch)


def flash_bwd(q, k, v, do, o, lrow, fcol, frow, *, qblk, kblk, vblk, nq, scale, name, exch=None):
    T = q.shape[0]
    tq = tk = _tile(T, FLASH_TILE)
    wq = 128 * nq
    nqb = T // tq
    has_f = fcol is not None

    def body(*refs):
        if has_f:
            (q_ref, k_ref, v_ref, do_ref, o_ref, lr_ref, fc_ref, fr_ref,
             dq_ref, dk_ref, dv_ref, df_ref, dfq_ref, dk_sc, dv_sc, dqT_sc, d_sc, df_sc, dfq_sc) = refs
        else:
            q_ref, k_ref, v_ref, do_ref, o_ref, lr_ref, dq_ref, dk_ref, dv_ref, dk_sc, dv_sc, dqT_sc, d_sc = refs
        j = pl.program_id(1)
        diag = lax.broadcasted_iota(jnp.int32, (tk, 1), 0) <= lax.broadcasted_iota(jnp.int32, (1, tq), 1)
        hms = [_half_mask(h) for h in range(2)]

        @pl.when(j == 0)
        def _():
            dqT_sc[...] = jnp.zeros_like(dqT_sc)
            if has_f:
                dfq_sc[...] = jnp.zeros_like(dfq_sc)

            def delta(b, carry):
                off = pl.multiple_of(b * tq, tq)
                prod = do_ref[pl.ds(off, tq), :] * o_ref[pl.ds(off, tq), :]
                Ds = [jnp.sum(jnp.where(hms[h], prod, 0.0), axis=1, keepdims=True) for h in range(2)]
                d_sc[:, pl.ds(off, tq)] = _row_stat_tile(Ds[0], Ds[1], tq)
                return carry

            lax.fori_loop(0, nqb, delta, 0)

        kb = k_ref[...]
        vb = v_ref[...]
        if nq == 1:
            khs = [jnp.where(hms[h], kb, 0).astype(MXU) for h in range(2)]
        else:
            khs = [kb[:, 128 * h:128 * (h + 1)].astype(MXU) for h in range(2)]
        kTs = [kh.astype(F32).T.astype(MXU) for kh in khs]
        kss = [(kh.astype(F32) * scale).astype(MXU) for kh in khs]
        vhs = [jnp.where(hms[h], vb, 0).astype(MXU) for h in range(2)]
        fks = [_lane_pick(fc_ref[0], h) for h in range(2)] if has_f else None
        dv_sc[...] = jnp.zeros_like(dv_sc)
        dk_sc[...] = jnp.zeros_like(dk_sc)
        if has_f:
            df_sc[...] = jnp.zeros_like(df_sc)

        def make_step(masked):
            def step(i, carry):
                off = pl.multiple_of(i * tq, tq)
                qs = q_ref[pl.ds(off, tq), :]
                dos = do_ref[pl.ds(off, tq), :]
                for h in range(2):
                    qh = qs if nq == 1 else qs[:, 128 * h:128 * (h + 1)]
                    sT = _dot_nt(kss[h], qh)
                    if has_f:
                        sT = sT + (fr_ref[0, h:h + 1, pl.ds(off, tq)] - fks[h])
                    pT = jnp.exp(sT - lr_ref[0, h:h + 1, pl.ds(off, tq)])
                    if masked:
                        pT = jnp.where(diag, pT, 0.0)
                    dsT = pT * (_dot_nt(vhs[h], dos) - d_sc[h:h + 1, pl.ds(off, tq)])
                    dv_sc[...] += _dot(pT, jnp.where(hms[h], dos, 0))
                    qq = jnp.where(hms[h], qs, 0) if nq == 1 else qh
                    dk_sc[h if nq == 2 else 0] += _dot(dsT, qq)
                    dqT_sc[h if nq == 2 else 0, :, pl.ds(off, tq)] += _dot(kTs[h], dsT)
                    if has_f:
                        part = dsT[:, 0:128]
                        for c in range(1, tq // 128):
                            part = part + dsT[:, 128 * c:128 * (c + 1)]
                        df_sc[h] += part
                        dfq_sc[h:h + 1, pl.ds(off, tq)] += jnp.sum(dsT, axis=0, keepdims=True)
                return carry
            return step

        make_step(True)(j, 0)
        lax.fori_loop(j + 1, nqb, make_step(False), 0)
        if nq == 1:
            dk_ref[...] = dk_sc[0] * scale
        else:
            dk_ref[...] = jnp.concatenate([dk_sc[0], dk_sc[1]], axis=1) * scale
        dv_ref[...] = dv_sc[...]
        if has_f:
            df_ref[0] = _col_stat_tile(-jnp.sum(df_sc[0], axis=1, keepdims=True),
                                       -jnp.sum(df_sc[1], axis=1, keepdims=True))

        @pl.when(j == nqb - 1)
        def _():
            if nq == 1:
                dq_ref[...] = dqT_sc[0].T * scale
            else:
                dq_ref[...] = jnp.concatenate([dqT_sc[0].T, dqT_sc[1].T], axis=1) * scale
            if has_f:
                sub = lax.broadcasted_iota(jnp.int32, (128, 1), 0)
                rows = jnp.where(sub == 0, dfq_sc[0:1, :], jnp.where(sub == 1, dfq_sc[1:2, :], 0.0))
                dfq_ref[0] = rows.T

    in_specs = [pl.BlockSpec((T, wq), lambda p, j: (0, qblk + p)),
                pl.BlockSpec((tk, wq), lambda p, j: (j, kblk + p)),
                pl.BlockSpec((tk, 128), lambda p, j: (j, vblk + p)),
                pl.BlockSpec((T, 128), lambda p, j: (0, p)),
                pl.BlockSpec((T, 128), lambda p, j: (0, p)),
                pl.BlockSpec((1, 8, T), lambda p, j: (p, 0, 0))]
    args = [q, k, v, do, o, lrow]
    out_specs = [pl.BlockSpec((T, wq), lambda p, j: (0, p)),
                 pl.BlockSpec((tk, wq), lambda p, j: (j, p)), pl.BlockSpec((tk, 128), lambda p, j: (j, p))]
    out_shape = [jax.ShapeDtypeStruct((T, 2 * wq), F32), jax.ShapeDtypeStruct((T, 2 * wq), F32),
                 jax.ShapeDtypeStruct((T, 256), F32)]
    scratch = [pltpu.VMEM((nq, tk, 128), F32), pltpu.VMEM((tk, 128), F32), pltpu.VMEM((nq, 128, T), F32),
               pltpu.VMEM((8, T), F32)]
    if has_f:
        in_specs += [pl.BlockSpec((1, tk, 128), lambda p, j: (p, j, 0)),
                     pl.BlockSpec((1, 8, T), lambda p, j: (p, 0, 0))]
        args += [fcol, frow]
        out_specs += [pl.BlockSpec((1, tk, 128), lambda p, j: (p, j, 0)),
                      pl.BlockSpec((1, T, 128), lambda p, j: (p, 0, 0))]
        out_shape += [jax.ShapeDtypeStruct((2, T, 128), F32), jax.ShapeDtypeStruct((2, T, 128), F32)]
        scratch += [pltpu.VMEM((2, tk, 128), F32), pltpu.VMEM((8, T), F32)]
    return _call_hosting(body, name, (2, T // tk), args, in_specs, out_specs, out_shape, scratch, exch)


def _swa_align(pair, e, h):
    sel = jnp.where(_half_mask(e), pair, 0.0)
    if e == h:
        return sel
    return pltpu.roll(sel, HEAD_DIM, 1)


def _swa_mask(n):
    W = WINDOW
    qi = lax.broadcasted_iota(jnp.int32, (W, 2 * W), 0) + W
    kj = lax.broadcasted_iota(jnp.int32, (W, 2 * W), 1)
    dist = qi - kj
    return (dist >= 0) & (dist < W) & ((n > 0) | (kj >= W))


def swa_fwd(proj, bias, sinks):
    T = proj.shape[0]
    W = WINDOW
    nb = T // W
    scale = HEAD_DIM ** -0.5

    def body(sink_ref, q_ref, kp_ref, kc_ref, vp_ref, vc_ref, b_ref, o_ref, l_ref):
        n = pl.program_id(0)
        mask = _swa_mask(n)
        kband = jnp.concatenate([kp_ref[...], kc_ref[...]], axis=0).astype(MXU)
        vband = jnp.concatenate([vp_ref[...], vc_ref[...]], axis=0).astype(MXU)
        lane = lax.broadcasted_iota(jnp.int32, (1, 128), 1)
        lse_tile = jnp.zeros((W, 128), F32)
        mask4 = jnp.concatenate([mask] * 4, axis=0)
        pairs = []
        for h in range(2):
            q4 = jnp.concatenate([_swa_align(q_ref[:, 128 * ((4 * h + g) // 2):128 * ((4 * h + g) // 2 + 1)],
                                             g % 2, h) for g in range(4)], axis=0)
            sink4 = jnp.concatenate([jnp.full((W, 1), sink_ref[4 * h + g], F32) for g in range(4)], axis=0)
            s = _dot_nt(q4, kband) * scale + b_ref[4 * h:4 * h + 4].reshape(4 * W, 2 * W)
            s = jnp.where(mask4, s, NEG)
            m = jnp.maximum(jnp.max(s, axis=1, keepdims=True), sink4)
            e = jnp.exp(s - m)
            l = jnp.sum(e, axis=1, keepdims=True) + jnp.exp(sink4 - m)
            r = jnp.where(_half_mask(h), _dot(e, vband), 0.0) / l
            r = r + pltpu.roll(r, HEAD_DIM, 1)
            lse4 = m + jnp.log(l)
            for g in range(4):
                lse_tile = jnp.where(lane == 4 * h + g, lse4[W * g:W * (g + 1)], lse_tile)
            pairs.append(jnp.where(_half_mask(0), r[0:W], r[W:2 * W]))
            pairs.append(jnp.where(_half_mask(0), r[2 * W:3 * W], r[3 * W:4 * W]))
        o_ref[...] = jnp.concatenate(pairs, axis=1)
        l_ref[...] = lse_tile

    prev = lambda n: (jnp.maximum(n - 1, 0), C_KA // 128)
    cur = lambda n: (n, C_KA // 128)
    prev_v = lambda n: (jnp.maximum(n - 1, 0), C_VA // 128)
    cur_v = lambda n: (n, C_VA // 128)
    return pl.pallas_call(
        body, name="swa_fwd", grid=(nb,),
        in_specs=[pl.BlockSpec(memory_space=pltpu.SMEM),
                  pl.BlockSpec((W, 512), lambda n: (n, 0)),
                  pl.BlockSpec((W, 128), prev), pl.BlockSpec((W, 128), cur),
                  pl.BlockSpec((W, 128), prev_v), pl.BlockSpec((W, 128), cur_v),
                  pl.BlockSpec((8, W, 2 * W), lambda n: (0, 0, 0))],
        out_specs=[pl.BlockSpec((W, 512), lambda n: (n, 0)), pl.BlockSpec((W, 128), lambda n: (n, 0))],
        out_shape=[jax.ShapeDtypeStruct((T, 512), F32), jax.ShapeDtypeStruct((T, 128), F32)],
        compiler_params=_cparams(("parallel",)),
    )(sinks, proj, proj, proj, proj, proj, bias)


def swa_bwd(proj, bias, sinks, do, o, lse):
    T = proj.shape[0]
    W = WINDOW
    nb = T // W
    scale = HEAD_DIM ** -0.5

    def body(sink_ref, q_ref, kp_ref, kc_ref, vp_ref, vc_ref, b_ref, do_ref, o_ref, l_ref,
             dq_ref, dk_ref, dv_ref, db_ref, dsk_ref, ck, cv):
        n = pl.program_id(0)

        @pl.when(n == 0)
        def _():
            ck[...] = jnp.zeros_like(ck)
            cv[...] = jnp.zeros_like(cv)
            db_ref[...] = jnp.zeros_like(db_ref)
            dsk_ref[...] = jnp.zeros_like(dsk_ref)

        @pl.when(n < nb)
        def _():
            mask = _swa_mask(n)
            kb32 = jnp.concatenate([kp_ref[...], kc_ref[...]], axis=0)
            vb32 = jnp.concatenate([vp_ref[...], vc_ref[...]], axis=0)
            kband = kb32.astype(MXU)
            sub = lax.broadcasted_iota(jnp.int32, (8, 1), 0)
            dk_band = jnp.zeros((2 * W, 128), F32)
            dv_band = jnp.zeros((2 * W, 128), F32)
            dsk = jnp.zeros((8, 128), F32)
            dq_pairs = []
            mask4 = jnp.concatenate([mask] * 4, axis=0)
            for h in range(2):
                hm = _half_mask(h)
                km = jnp.where(hm, kb32, 0.0).astype(MXU)
                vm = jnp.where(hm, vb32, 0.0).astype(MXU)
                pbs = [slice(128 * ((4 * h + g) // 2), 128 * ((4 * h + g) // 2 + 1)) for g in range(4)]
                q4 = jnp.concatenate([_swa_align(q_ref[:, pbs[g]], g % 2, h) for g in range(4)], axis=0)
                do4 = jnp.concatenate([_swa_align(do_ref[:, pbs[g]], g % 2, h) for g in range(4)], axis=0)
                D4 = jnp.concatenate(
                    [jnp.sum(jnp.where(_half_mask(g % 2), do_ref[:, pbs[g]] * o_ref[:, pbs[g]], 0.0), axis=1,
                             keepdims=True) for g in range(4)], axis=0)
                lse4 = jnp.concatenate([_lane_pick(l_ref[...], 4 * h + g) for g in range(4)], axis=0)
                sink4 = jnp.concatenate([jnp.full((W, 1), sink_ref[4 * h + g], F32) for g in range(4)], axis=0)
                s = _dot_nt(q4, kband) * scale + b_ref[4 * h:4 * h + 4].reshape(4 * W, 2 * W)
                p = jnp.where(mask4, jnp.exp(s - lse4), 0.0)
                sd = jnp.exp(sink4 - lse4) * D4
                for g in range(4):
                    dsk = dsk + jnp.where(sub == 4 * h + g,
                                          -jnp.sum(sd[W * g:W * (g + 1)], axis=0, keepdims=True), 0.0)
                ds = p * (_dot_nt(do4, vm) - D4)
                db_ref[4 * h:4 * h + 4] += ds.reshape(4, W, 2 * W)
                dq = _dot(ds, km) * scale
                dq = dq + pltpu.roll(dq, HEAD_DIM, 1)
                dk_band = dk_band + _dot(ds.T, q4) * scale
                dv_band = dv_band + _dot(p.T, do4)
                dq_pairs.append(jnp.where(_half_mask(0), dq[0:W], dq[W:2 * W]))
                dq_pairs.append(jnp.where(_half_mask(0), dq[2 * W:3 * W], dq[3 * W:4 * W]))
            dq_ref[...] = jnp.concatenate(dq_pairs, axis=1)
            dsk_ref[...] += dsk
            dk_ref[...] = ck[...] + dk_band[0:W]
            dv_ref[...] = cv[...] + dv_band[0:W]
            ck[...] = dk_band[W:2 * W]
            cv[...] = dv_band[W:2 * W]

        @pl.when(n == nb)
        def _():
            dk_ref[...] = ck[...]
            dv_ref[...] = cv[...]

    cl = lambda n: jnp.minimum(n, nb - 1)
    pv = lambda n: jnp.maximum(jnp.minimum(n, nb - 1) - 1, 0)
    return pl.pallas_call(
        body, name="swa_bwd", grid=(nb + 1,),
        in_specs=[pl.BlockSpec(memory_space=pltpu.SMEM),
                  pl.BlockSpec((W, 512), lambda n: (cl(n), 0)),
                  pl.BlockSpec((W, 128), lambda n: (pv(n), C_KA // 128)),
                  pl.BlockSpec((W, 128), lambda n: (cl(n), C_KA // 128)),
                  pl.BlockSpec((W, 128), lambda n: (pv(n), C_VA // 128)),
                  pl.BlockSpec((W, 128), lambda n: (cl(n), C_VA // 128)),
                  pl.BlockSpec((8, W, 2 * W), lambda n: (0, 0, 0)),
                  pl.BlockSpec((W, 512), lambda n: (cl(n), 0)),
                  pl.BlockSpec((W, 512), lambda n: (cl(n), 0)),
                  pl.BlockSpec((W, 128), lambda n: (cl(n), 0))],
        out_specs=[pl.BlockSpec((W, 512), lambda n: (cl(n), 0)),
                   pl.BlockSpec((W, 128), lambda n: (jnp.maximum(n - 1, 0), 0)),
                   pl.BlockSpec((W, 128), lambda n: (jnp.maximum(n - 1, 0), 0)),
                   pl.BlockSpec((8, W, 2 * W), lambda n: (0, 0, 0)),
                   pl.BlockSpec((8, 128), lambda n: (0, 0))],
        out_shape=[jax.ShapeDtypeStruct((T, 512), F32), jax.ShapeDtypeStruct((T, 128), F32),
                   jax.ShapeDtypeStruct((T, 128), F32), jax.ShapeDtypeStruct((8, W, 2 * W), F32),
                   jax.ShapeDtypeStruct((8, 128), F32)],
        scratch_shapes=[pltpu.VMEM((W, 128), F32), pltpu.VMEM((W, 128), F32)],
        compiler_params=_cparams(("arbitrary",)),
    )(sinks, proj, proj, proj, proj, proj, bias, do, o, lse)


def swa_bias_table(rel_bias):
    W = WINDOW
    qi = jnp.arange(W, dtype=jnp.int32)[:, None] + W
    kj = jnp.arange(2 * W, dtype=jnp.int32)[None, :]
    dist = qi - kj
    max_exact = REL_BUCKETS // 2
    d = jnp.maximum(dist, 0)
    log_ratio = jnp.log(jnp.maximum(d, 1).astype(F32) / max_exact) / math.log(REL_MAX_DIST / max_exact)
    large = jnp.minimum(max_exact + (log_ratio * (REL_BUCKETS - max_exact)).astype(jnp.int32), REL_BUCKETS - 1)
    bucket = jnp.where(d < max_exact, d, large)
    bucket = bucket.reshape(-1)
    onehot = (bucket[None, :] == jnp.arange(REL_BUCKETS, dtype=jnp.int32)[:, None]).astype(F32)
    bias = jnp.dot(rel_bias.astype(F32).T, onehot, precision=lax.Precision.HIGHEST)
    return bias.reshape(SWA_Q_HEADS, W, 2 * W), bucket


def attn_out(oa, ob, oc, gn, wout, gpost, x):
    T = x.shape[0]
    tm = _tile(T, 512)

    def body(oa_ref, ob_ref, oc_ref, gn_ref, w_ref, gp_ref, x_ref, x2_ref, y_ref, mT_ref):
        g = gn_ref[...]
        mixed = jnp.concatenate([_rms_fwd(oa_ref[...], g[:, 0:512]), _rms_fwd(ob_ref[...], g[:, 512:768]),
                                 _rms_fwd(oc_ref[...], g[:, 768:1024])], axis=1)
        mT_ref[...] = mixed.T.astype(MXU)
        y = _dot(mixed, w_ref[...])
        y_ref[...] = y
        x2_ref[...] = x_ref[...] + _rms_fwd(y, gp_ref[...])

    row = lambda i: (i, 0)
    const = lambda i: (0, 0)
    return pl.pallas_call(
        body, name="attn_out", grid=(T // tm,),
        in_specs=[pl.BlockSpec((tm, 512), row), pl.BlockSpec((tm, 256), row), pl.BlockSpec((tm, 256), row),
                  pl.BlockSpec((1, 1024), const), pl.BlockSpec((1024, 1024), const), pl.BlockSpec((1, 1024), const),
                  pl.BlockSpec((tm, 1024), row)],
        out_specs=[pl.BlockSpec((tm, 1024), row), pl.BlockSpec((tm, 1024), row),
                   pl.BlockSpec((1024, tm), lambda i: (0, i))],
        out_shape=[jax.ShapeDtypeStruct((T, 1024), F32), jax.ShapeDtypeStruct((T, 1024), F32),
                   jax.ShapeDtypeStruct((1024, T), MXU)],
        compiler_params=_cparams(("parallel",)),
    )(oa, ob, oc, gn, wout, gpost, x)


def attn_out_bwd(dx2, y, oa, ob, oc, gn, wout, gpost):
    T = dx2.shape[0]
    tm = _tile(T, 512)

    def body(dx_ref, y_ref, oa_ref, ob_ref, oc_ref, gn_ref, w_ref, gp_ref,
             dy_ref, da_ref, db_ref, dc_ref, dgn_ref, dgp_ref):
        first = pl.program_id(0) == 0
        dy, dgp = _rms_bwd(dx_ref[...], y_ref[...], gp_ref[...])
        dy_ref[...] = dy.astype(MXU)
        _acc_out(dgp_ref, dgp, first)
        dm = _dot_nt(dy, w_ref[...])
        g = gn_ref[...]
        da, dga = _rms_bwd(dm[:, 0:512], oa_ref[...], g[:, 0:512])
        db, dgb = _rms_bwd(dm[:, 512:768], ob_ref[...], g[:, 512:768])
        dc, dgc = _rms_bwd(dm[:, 768:1024], oc_ref[...], g[:, 768:1024])
        da_ref[...] = da
        db_ref[...] = db
        dc_ref[...] = dc
        _acc_out(dgn_ref, jnp.concatenate([dga, dgb, dgc], axis=1), first)

    row = lambda i: (i, 0)
    const = lambda i: (0, 0)
    return pl.pallas_call(
        body, name="attn_out_bwd", grid=(T // tm,),
        in_specs=[pl.BlockSpec((tm, 1024), row), pl.BlockSpec((tm, 1024), row),
                  pl.BlockSpec((tm, 512), row), pl.BlockSpec((tm, 256), row), pl.BlockSpec((tm, 256), row),
                  pl.BlockSpec((1, 1024), const), pl.BlockSpec((1024, 1024), const), pl.BlockSpec((1, 1024), const)],
        out_specs=[pl.BlockSpec((tm, 1024), row), pl.BlockSpec((tm, 512), row), pl.BlockSpec((tm, 256), row),
                   pl.BlockSpec((tm, 256), row), pl.BlockSpec((1, 1024), const), pl.BlockSpec((1, 1024), const)],
        out_shape=[jax.ShapeDtypeStruct((T, 1024), MXU), jax.ShapeDtypeStruct((T, 512), F32),
                   jax.ShapeDtypeStruct((T, 256), F32), jax.ShapeDtypeStruct((T, 256), F32),
                   jax.ShapeDtypeStruct((1, 1024), F32), jax.ShapeDtypeStruct((1, 1024), F32)],
        compiler_params=_cparams(("arbitrary",)),
    )(dx2, y, oa, ob, oc, gn, wout, gpost)


FF_TILE = 256
_GELU_C = math.sqrt(2.0 / math.pi)


def _gelu(x):
    return 0.5 * x * (1.0 + jnp.tanh(_GELU_C * (x + 0.044715 * x * x * x)))


def _gelu_with_grad(x):
    x2 = x * x
    t = jnp.tanh(_GELU_C * x * (1.0 + 0.044715 * x2))
    h = 0.5 * (1.0 + t)
    return x * h, h + (0.5 * _GELU_C) * x * (1.0 - t * t) * (1.0 + (3 * 0.044715) * x2)


def _conv_taps(u, hal_ref, first):
    row = lax.broadcasted_iota(jnp.int32, (u.shape[0], 1), 0)
    h6 = jnp.where(first, 0.0, hal_ref[6:7, :])
    h7 = jnp.where(first, 0.0, hal_ref[7:8, :])
    r1 = jnp.where(row == 0, h7, pltpu.roll(u, 1, 0))
    r2 = jnp.where(row == 0, h6, jnp.where(row == 1, h7, pltpu.roll(u, 2, 0)))
    return r1, r2


def ffn_fwd(u0, convw, convb, wdown, gpost, x2):
    T = x2.shape[0]
    tm, tn = _tile(T, 1024), FF_TILE
    nj = D_FF // tn

    def body(ug_ref, uu_ref, hg_ref, hu_ref, wg_ref, wu_ref, bg_ref, bu_ref, wd_ref, gp_ref, x_ref,
             x3_ref, y_ref, aT_ref, acc):
        i, j = pl.program_id(0), pl.program_id(1)
        first = i == 0

        def conv(u_ref, h_ref, w_ref, b_ref):
            u = u_ref[...]
            r1, r2 = _conv_taps(u, h_ref, first)
            return b_ref[...] + w_ref[0:1, :] * r2 + w_ref[1:2, :] * r1 + w_ref[2:3, :] * u

        a = _gelu(conv(ug_ref, hg_ref, wg_ref, bg_ref)) * conv(uu_ref, hu_ref, wu_ref, bu_ref)
        aT_ref[...] = a.T.astype(MXU)
        _acc_out(acc, _dot(a, wd_ref[...]), j == 0)

        @pl.when(j == nj - 1)
        def _():
            y = acc[...]
            y_ref[...] = y
            x3_ref[...] = x_ref[...] + _rms_fwd(y, gp_ref[...])

    halo = lambda off: (lambda i, j: (jnp.maximum(i * (tm // 8) - 1, 0), off + j))
    return pl.pallas_call(
        body, name="ffn_fwd", grid=(T // tm, nj),
        in_specs=[pl.BlockSpec((tm, tn), lambda i, j: (i, j)), pl.BlockSpec((tm, tn), lambda i, j: (i, nj + j)),
                  pl.BlockSpec((8, tn), halo(0)), pl.BlockSpec((8, tn), halo(nj)),
                  pl.BlockSpec((3, tn), lambda i, j: (0, j)), pl.BlockSpec((3, tn), lambda i, j: (0, nj + j)),
                  pl.BlockSpec((1, tn), lambda i, j: (0, j)), pl.BlockSpec((1, tn), lambda i, j: (0, nj + j)),
                  pl.BlockSpec((tn, 1024), lambda i, j: (j, 0)),
                  pl.BlockSpec((1, 1024), lambda i, j: (0, 0)),
                  pl.BlockSpec((tm, 1024), lambda i, j: (i, 0))],
        out_specs=[pl.BlockSpec((tm, 1024), lambda i, j: (i, 0)), pl.BlockSpec((tm, 1024), lambda i, j: (i, 0)),
                   pl.BlockSpec((tn, tm), lambda i, j: (j, i))],
        out_shape=[jax.ShapeDtypeStruct((T, 1024), F32), jax.ShapeDtypeStruct((T, 1024), F32),
                   jax.ShapeDtypeStruct((D_FF, T), MXU)],
        scratch_shapes=[pltpu.VMEM((tm, 1024), F32)],
        compiler_params=_cparams(("parallel", "arbitrary")),
    )(u0, u0, u0, u0, convw, convw, convb, convb, wdown, gpost, x2)


def ffn_bwd(dx3, y, u0, convw, convb, wdown, gpost):
    T = dx3.shape[0]
    tm, tn = _tile(T, 1024), FF_TILE
    nj = D_FF // tn
    ni = T // tm

    def body(dx_ref, y_ref, ug_ref, uu_ref, hg_ref, hu_ref, wg_ref, wu_ref, bg_ref, bu_ref, wd_ref, gp_ref,
             dy_ref, dug_ref, duu_ref, dcg_ref, dcu_ref, dgp_ref, dy_sc, cg, cu, ag, au):
        s, j = pl.program_id(0), pl.program_id(1)
        i = ni - 1 - s
        first_tok = i == 0
        row = lax.broadcasted_iota(jnp.int32, (tm, 1), 0)
        sub = lax.broadcasted_iota(jnp.int32, (8, 1), 0)

        @pl.when(j == 0)
        def _():
            dy, dgp = _rms_bwd(dx_ref[...], y_ref[...], gp_ref[...])
            dy_sc[...] = dy.astype(MXU)
            dy_ref[...] = dy.astype(MXU)
            _acc_out(dgp_ref, dgp, s == 0)

        @pl.when(s == 0)
        def _():
            cg[j] = jnp.zeros((8, tn), F32)
            cu[j] = jnp.zeros((8, tn), F32)
            ag[j] = jnp.zeros((8, tn), F32)
            au[j] = jnp.zeros((8, tn), F32)

        da = _dot_nt(dy_sc[...], wd_ref[...])

        def conv(u_ref, h_ref, w_ref, b_ref):
            u = u_ref[...]
            r1, r2 = _conv_taps(u, h_ref, first_tok)
            return b_ref[...] + w_ref[0:1, :] * r2 + w_ref[1:2, :] * r1 + w_ref[2:3, :] * u, u, r1, r2

        gate, ugv, g1, g2 = conv(ug_ref, hg_ref, wg_ref, bg_ref)
        up, uuv, u1, u2 = conv(uu_ref, hu_ref, wu_ref, bu_ref)
        gl, dgl = _gelu_with_grad(gate)
        dup = da * gl
        dgate = da * up * dgl

        def conv_bwd(du, u, r1, r2, w_ref, c_ref, a_ref, du_ref):
            nxt = c_ref[j]
            n0, n1 = nxt[0:1, :], nxt[1:2, :]
            f1 = jnp.where(row == tm - 1, n0, pltpu.roll(du, tm - 1, 0))
            f2 = jnp.where(row == tm - 1, n1, jnp.where(row == tm - 2, n0, pltpu.roll(du, tm - 2, 0)))
            du_ref[...] = (w_ref[2:3, :] * du + w_ref[1:2, :] * f1 + w_ref[0:1, :] * f2).astype(MXU)
            c_ref[j] = du[0:8, :]
            red = lambda v: jnp.sum(v, axis=0, keepdims=True)
            part = jnp.where(sub == 0, red(du * r2), jnp.where(sub == 1, red(du * r1), jnp.where(
                sub == 2, red(du * u), jnp.where(sub == 3, red(du), 0.0))))
            a_ref[j] = a_ref[j] + part
            return a_ref[j]

        dcg_ref[0] = conv_bwd(dgate, ugv, g1, g2, wg_ref, cg, ag, dug_ref)
        dcu_ref[0] = conv_bwd(dup, uuv, u1, u2, wu_ref, cu, au, duu_ref)

    rev = lambda s: ni - 1 - s
    halo = lambda off: (lambda s, j: (jnp.maximum(rev(s) * (tm // 8) - 1, 0), off + j))
    return pl.pallas_call(
        body, name="ffn_bwd", grid=(ni, nj),
        in_specs=[pl.BlockSpec((tm, 1024), lambda s, j: (rev(s), 0)), pl.BlockSpec((tm, 1024), lambda s, j: (rev(s), 0)),
                  pl.BlockSpec((tm, tn), lambda s, j: (rev(s), j)), pl.BlockSpec((tm, tn), lambda s, j: (rev(s), nj + j)),
                  pl.BlockSpec((8, tn), halo(0)), pl.BlockSpec((8, tn), halo(nj)),
                  pl.BlockSpec((3, tn), lambda s, j: (0, j)), pl.BlockSpec((3, tn), lambda s, j: (0, nj + j)),
                  pl.BlockSpec((1, tn), lambda s, j: (0, j)), pl.BlockSpec((1, tn), lambda s, j: (0, nj + j)),
                  pl.BlockSpec((tn, 1024), lambda s, j: (j, 0)),
                  pl.BlockSpec((1, 1024), lambda s, j: (0, 0))],
        out_specs=[pl.BlockSpec((tm, 1024), lambda s, j: (rev(s), 0)),
                   pl.BlockSpec((tm, tn), lambda s, j: (rev(s), j)), pl.BlockSpec((tm, tn), lambda s, j: (rev(s), j)),
                   pl.BlockSpec((1, 8, tn), lambda s, j: (s, 0, j)), pl.BlockSpec((1, 8, tn), lambda s, j: (s, 0, j)),
                   pl.BlockSpec((1, 1024), lambda s, j: (0, 0))],
        out_shape=[jax.ShapeDtypeStruct((T, 1024), MXU), jax.ShapeDtypeStruct((T, D_FF), MXU),
                   jax.ShapeDtypeStruct((T, D_FF), MXU),
                   jax.ShapeDtypeStruct((ni, 8, D_FF), F32), jax.ShapeDtypeStruct((ni, 8, D_FF), F32),
                   jax.ShapeDtypeStruct((1, 1024), F32)],
        scratch_shapes=[pltpu.VMEM((tm, 1024), MXU)] + [pltpu.VMEM((nj, 8, tn), F32)] * 4,
        compiler_params=_cparams(("arbitrary", "arbitrary")),
    )(dx3, y, u0, u0, u0, u0, convw, convw, convb, convb, wdown, gpost)


ELEMS_PER_BLOCK = 512 * 1024


def _row_block(R, C):
    if R * C <= ELEMS_PER_BLOCK or R % 8:
        return R
    best = 8
    for t in range(8, R + 1, 8):
        if R % t == 0 and t * C <= ELEMS_PER_BLOCK:
            best = t
    return best


def adamw(w, g, m, v, name):
    R, C = w.shape
    partials = g.ndim == 3
    tr = _row_block(R, 2 * C)
    c1 = 1.0 - ADAM_B1 ** ADAM_STEP
    c2 = 1.0 - ADAM_B2 ** ADAM_STEP

    def body(w_ref, g_ref, m_ref, v_ref, g_out, d_ref, nm_ref, nv_ref):
        if partials:
            gv = g_ref[0].astype(F32)
            for d in range(1, N_DEV):
                gv = gv + g_ref[d].astype(F32)
        else:
            gv = g_ref[...]
        g_out[...] = gv
        nm = ADAM_B1 * m_ref[...] + (1.0 - ADAM_B1) * gv
        nv = ADAM_B2 * v_ref[...] + (1.0 - ADAM_B2) * (gv * gv)
        nm_ref[...] = nm
        nv_ref[...] = nv
        d_ref[...] = -ADAM_LR * ((nm / c1) / (jnp.sqrt(nv / c2) + ADAM_EPS) + ADAM_WD * w_ref[...])

    spec = pl.BlockSpec((tr, C), lambda i: (i, 0))
    gspec = pl.BlockSpec((N_DEV, tr, C), lambda i: (0, i, 0)) if partials else spec
    return pl.pallas_call(
        body, name=name, grid=(R // tr,), in_specs=[spec, gspec, spec, spec], out_specs=[spec] * 4,
        out_shape=[jax.ShapeDtypeStruct((R, C), F32)] * 4,
        compiler_params=_cparams(("parallel",)),
    )(w, g, m, v)


def sum_devices(buf, name):
    _, R, C = buf.shape
    tr = _row_block(R, C * 4)

    def body(b_ref, o_ref):
        acc = b_ref[0].astype(F32)
        for d in range(1, N_DEV):
            acc = acc + b_ref[d].astype(F32)
        o_ref[...] = acc

    return pl.pallas_call(
        body, name=name, grid=(R // tr,),
        in_specs=[pl.BlockSpec((N_DEV, tr, C), lambda i: (0, i, 0))],
        out_specs=pl.BlockSpec((tr, C), lambda i: (i, 0)),
        out_shape=jax.ShapeDtypeStruct((R, C), F32),
        compiler_params=_cparams(("parallel",)),
    )(buf)


def _exchange_copies(src_refs, out_refs, send_sems, recv_sems, gather):
    x, y, c = lax.axis_index("x"), lax.axis_index("y"), lax.axis_index("c")
    me = 4 * x + 2 * y + c
    flip = lambda a, bit: 1 - a if bit else a
    part = lambda ref, d: ref if gather else ref.at[d]
    copies = []
    for k in range(1, N_DEV):
        px, py, pc = flip(x, (k >> 2) & 1), flip(y, (k >> 1) & 1), flip(c, k & 1)
        peer = 4 * px + 2 * py + pc
        for t in range(len(src_refs)):
            sem = t * (N_DEV - 1) + k - 1
            mk = lambda s, d: pltpu.make_async_remote_copy(
                src_ref=s, dst_ref=d, send_sem=send_sems.at[sem], recv_sem=recv_sems.at[sem],
                device_id=(px, py, pc), device_id_type=pl.DeviceIdType.MESH)
            copies.append((mk(part(src_refs[t], peer), out_refs[t].at[me]),
                           mk(part(src_refs[t], me), out_refs[t].at[peer])))
    return me, copies


def exchange(srcs, name, gather):
    n = len(srcs)
    shapes = [(N_DEV,) + s.shape if gather else s.shape for s in srcs]

    def body(*refs):
        src_refs, out_refs = refs[:n], refs[n:2 * n]
        send_sems, recv_sems, local_sems = refs[2 * n:]
        me, copies = _exchange_copies(src_refs, out_refs, send_sems, recv_sems, gather)
        for outgoing, _ in copies:
            outgoing.start()
        mine = [pltpu.make_async_copy(src_refs[t] if gather else src_refs[t].at[me], out_refs[t].at[me],
                                      local_sems.at[t]) for t in range(n)]
        for cp in mine:
            cp.start()
        for _, incoming in copies:
            incoming.wait_recv()
        for outgoing, _ in copies:
            outgoing.wait_send()
        for cp in mine:
            cp.wait()

    return pl.pallas_call(
        body, name=name,
        in_specs=[pl.BlockSpec(memory_space=pl.ANY)] * n, out_specs=[pl.BlockSpec(memory_space=pl.ANY)] * n,
        out_shape=[jax.ShapeDtypeStruct(shp, s.dtype) for shp, s in zip(shapes, srcs)],
        scratch_shapes=[pltpu.SemaphoreType.DMA((n * (N_DEV - 1),)), pltpu.SemaphoreType.DMA((n * (N_DEV - 1),)),
                        pltpu.SemaphoreType.DMA((n,))],
    )(*srcs)


def hosted_exchange(body, n_in, n_out, n_scratch, grid, srcs, gather):
    n = len(srcs)
    shapes = [(N_DEV,) + s.shape if gather else s.shape for s in srcs]

    def wrapped(*refs):
        ins, xin = refs[:n_in], refs[n_in:n_in + n]
        outs = refs[n_in + n:n_in + n + n_out]
        xout = refs[n_in + n + n_out:n_in + 2 * n + n_out]
        rest = refs[n_in + 2 * n + n_out:]
        scratch, (send_sems, recv_sems, local_sems) = rest[:n_scratch], rest[n_scratch:]
        ids = [pl.program_id(a) for a in range(len(grid))]
        first = functools.reduce(jnp.logical_and, [i == 0 for i in ids])
        last = functools.reduce(jnp.logical_and, [i == g - 1 for i, g in zip(ids, grid)])
        me, copies = _exchange_copies(xin, xout, send_sems, recv_sems, gather)
        mine = [pltpu.make_async_copy(xin[t] if gather else xin[t].at[me], xout[t].at[me], local_sems.at[t])
                for t in range(n)]

        @pl.when(first)
        def _():
            for outgoing, _ in copies:
                outgoing.start()
            for cp in mine:
                cp.start()

        body(*ins, *outs, *scratch)

        @pl.when(last)
        def _():
            for _, incoming in copies:
                incoming.wait_recv()
            for outgoing, _ in copies:
                outgoing.wait_send()
            for cp in mine:
                cp.wait()

    any_spec = pl.BlockSpec(memory_space=pl.ANY)
    return wrapped, (list(srcs), [any_spec] * n, [any_spec] * n,
                     [jax.ShapeDtypeStruct(shp, s.dtype) for shp, s in zip(shapes, srcs)],
                     [pltpu.SemaphoreType.DMA((n * (N_DEV - 1),)), pltpu.SemaphoreType.DMA((n * (N_DEV - 1),)),
                      pltpu.SemaphoreType.DMA((n,))])


def _pack(parts, cols, row_align, dtype):
    flat = jnp.concatenate([p.astype(dtype) for p in parts], axis=-1)
    n = flat.shape[-1]
    block = cols * row_align
    total = -(-n // block) * block
    flat = jnp.pad(flat, [(0, 0)] * (flat.ndim - 1) + [(0, total - n)])
    return flat.reshape(flat.shape[:-1] + (total // cols, cols))


def _unpack(buf, shapes):
    lead = buf.shape[:-2]
    flat = buf.reshape(lead + (-1,))
    out, off = [], 0
    for s in shapes:
        n = int(np.prod(s))
        out.append(flat[..., off:off + n].reshape(lead + tuple(s)))
        off += n
    return out


SHARD_SHAPES = [(128, IN_COLS), (256, 48), (128, 64), (128, 1024), (1024, 704), (352, 1024)]
SHARDED = ["w_in", "w_uq", "w_ukv", "w_out", "w_up", "w_down"]
ATTN_SIDE = ["w_in", "w_uq", "w_ukv", "w_out"]
FFN_SIDE = ["w_up", "conv_w", "w_down"]


def _full_from_shards(name, s):
    if name in ("w_in", "w_out", "w_down"):
        return s.reshape((-1, s.shape[-1]))
    return s.transpose(1, 0, 2).reshape((s.shape[1], -1))


def _shards_from_full(name, f):
    if name in ("w_in", "w_out", "w_down"):
        return f.reshape((N_DEV, -1, f.shape[-1]))
    return f.reshape((f.shape[0], N_DEV, -1)).transpose(1, 0, 2)


def _perm_w_in(w):
    z = lambda n: jnp.zeros((w.shape[0], n), w.dtype)
    return jnp.concatenate([w[:, :1536], w[:, 1540:1924], w[:, 1536:1540], z(60), w[:, 1924:1956], z(32)], axis=1)


def _unperm_w_in(d):
    return jnp.concatenate([d[:, :1536], d[:, 1920:1924], d[:, 1536:1920], d[:, 1984:2016]], axis=1)


def _perm_w_uq(w):
    return jnp.pad(w.reshape(256, 4, MLA_QK_DIM), ((0, 0), (0, 0), (0, 128 - MLA_QK_DIM))).reshape(256, 512)


def _unperm_w_uq(d):
    return d.reshape(256, 4, 128)[:, :, :MLA_QK_DIM].reshape(256, 4 * MLA_QK_DIM)


def _perm_w_ukv(w):
    w4 = w.reshape(128, 4, 128)
    k = jnp.pad(w4[:, :, :64], ((0, 0), (0, 0), (0, 64))).reshape(128, 512)
    return jnp.concatenate([k, w4[:, :, 64:].reshape(128, 256)], axis=1)


def _unperm_w_ukv(d):
    dk = d[:, :512].reshape(128, 4, 128)[:, :, :64]
    dv = d[:, 512:].reshape(128, 4, 64)
    return jnp.concatenate([dk, dv], axis=-1).reshape(128, 512)


def _row(v, width=None):
    v = v.reshape(1, -1).astype(F32)
    if width is not None and v.shape[1] < width:
        v = jnp.pad(v, ((0, 0), (0, width - v.shape[1])))
    return v


def _layer_fwd(x, P, shared, send=None, ffn_from=None):
    cosr, sinr, bias = shared
    ex = (lambda part: None) if send is None else (lambda part: (send[part], True))
    proj, hT, projb = norm_matmul(x, P["g_pre"], P["w_in_p"], "in_proj", lo_tiles=C_CQ // 512)
    qm, km, vm, cqT, ckvT = mla_prep(proj, P["gq"], P["gkv"], P["w_uq_p"], P["w_ukv_p"], cosr, sinr)
    fcol, frow, frep = fox_gate(proj, P["fbias"])
    oa, lse_a = swa_fwd(proj, bias, P["sinks"])
    (ob, lrb), got_a = flash_fwd(projb, projb, projb, frep, frow, qblk=C_QF // 128, kblk=C_KF // 128,
                                 vblk=C_VF // 128, nq=1, scale=HEAD_DIM ** -0.5, name="fox_fwd", exch=ex(0))
    (oc, lrc), got_b = flash_fwd(qm, km, vm, None, None, qblk=0, kblk=0, vblk=0, nq=2,
                                 scale=MLA_QK_DIM ** -0.5, name="mla_fwd", exch=ex(1))
    x2, y1, mT = attn_out(oa, ob, oc, P["gn"], P["w_out"], P["g_apost"], x)
    if ffn_from is not None:
        P = dict(P, **ffn_from(got_a, got_b))
    u0, h2T = norm_matmul(x2, P["g_fpre"], P["w_up"], "up_proj", tn_pref=1536)
    x3, y2, aT = ffn_fwd(u0, P["conv_w"], P["conv_b"], P["w_down"], P["g_fpost"], x2)
    S = dict(x=x, proj=proj, projb=projb, hT=hT, qm=qm, km=km, vm=vm, cqT=cqT, ckvT=ckvT, fcol=fcol, frow=frow,
             oa=oa, lse_a=lse_a, ob=ob, lrb=lrb, oc=oc, lrc=lrc,
             x2=x2, y1=y1, mT=mT, u0=u0, h2T=h2T, y2=y2, aT=aT)
    return x3, S, P, (got_a, got_b)


def _layer_bwd(dx3, P, S, shared, send_attn=None):
    cosr, sinr, bias = shared
    proj = S["proj"]
    G = {}
    dy2, dug, duu, dcg, dcu, G["ffn_post_norm"] = ffn_bwd(dx3, S["y2"], S["u0"], P["conv_w"], P["conv_b"],
                                                         P["w_down"], P["g_fpost"])
    du0 = jnp.concatenate([dug, duu], axis=1)
    dconv = jnp.concatenate([dcg[-1], dcu[-1]], axis=1)
    G["conv_w"], G["conv_b"] = dconv[0:3], dconv[3]
    G["w_down"] = matmul_nn(S["aT"], dy2, "dw_down", MXU)
    G["w_up"] = matmul_nn(S["h2T"], du0, "dw_up", MXU)
    dx2, G["ffn_pre_norm"] = matmul_nt_normbwd(du0, P["w_up"], S["x2"], P["g_fpre"], dx3, "up_bwd")
    dy1, doa, dob, doc, G["group_norm"], G["attn_post_norm"] = attn_out_bwd(
        dx2, S["y1"], S["oa"], S["ob"], S["oc"], P["gn"], P["w_out"], P["g_apost"])
    G["w_out"] = matmul_nn(S["mT"], dy1, "dw_out", MXU)
    dqa, dka, dva, dbias, dsk = swa_bwd(proj, bias, P["sinks"], doa, S["oa"], S["lse_a"])
    G["swa_sinks"] = dsk[:, 0]
    pb = S["projb"]
    if send_attn is None:
        ex = lambda part: None
    else:
        parts = ([_shards_from_full("w_down", G["w_down"])] + send_attn, [_shards_from_full("w_up", G["w_up"])])
        ex = lambda part: (parts[part], False)
    (dqf, dkf, dvf, dFk, dFq), got_a = flash_bwd(
        pb, pb, pb, dob, S["ob"], S["lrb"], S["fcol"], S["frow"], name="fox_bwd", qblk=C_QF // 128,
        kblk=C_KF // 128, vblk=C_VF // 128, nq=1, scale=HEAD_DIM ** -0.5, exch=ex(0))
    dmisc_f, dfb = fox_gate_bwd(dFq, dFk, proj, P["fbias"])
    G["forget_bias"] = dfb[0, 0:4]
    (dqm_, dkm_, dvm_), got_b = flash_bwd(
        S["qm"], S["km"], S["vm"], doc, S["oc"], S["lrc"], None, None, name="mla_bwd",
        qblk=0, kblk=0, vblk=0, nq=2, scale=MLA_QK_DIM ** -0.5, exch=ex(1))
    dqm, dkv, dcq, dckv, dmisc_r, G["q_latent_norm"], G["kv_latent_norm"] = mla_prep_bwd(
        dqm_, dkm_, dvm_, proj, P["gq"], P["gkv"], P["w_uq_p"], P["w_ukv_p"], cosr, sinr)
    G["w_uq"] = _unperm_w_uq(matmul_nn(S["cqT"], dqm, "dw_uq", MXU))
    G["w_ukv"] = _unperm_w_ukv(matmul_nn(S["ckvT"], dkv, "dw_ukv", MXU))
    dproj = jnp.concatenate([dqa, dka, dva, dqf, dkf, dvf, dcq, dckv, dmisc_f + dmisc_r], axis=1).astype(MXU)
    G["w_in"] = _unperm_w_in(matmul_nn(S["hT"], dproj, "dw_in", MXU))
    dx, G["attn_pre_norm"] = matmul_nt_normbwd(dproj, P["w_in_p"], S["x"], P["g_pre"], dx2, "in_bwd")
    return dx, G, dbias, (got_a, got_b)


def _layer_params(l, full, small):
    return dict(
        g_pre=_row(small["attn_pre_norm"][l]), w_in_p=_perm_w_in(full["w_in"]),
        gq=_row(small["q_latent_norm"][l]), gkv=_row(small["kv_latent_norm"][l]),
        w_uq_p=_perm_w_uq(full["w_uq"]), w_ukv_p=_perm_w_ukv(full["w_ukv"]),
        fbias=_row(small["forget_bias"][l], 128), sinks=small["swa_sinks"][l].astype(F32),
        gn=_row(small["group_norm"][l]), w_out=full["w_out"], g_apost=_row(small["attn_post_norm"][l]),
        g_fpre=_row(small["ffn_pre_norm"][l]), conv_b=_row(small["conv_b"][l]),
        g_fpost=_row(small["ffn_post_norm"][l]),
        **{n: full[n] for n in FFN_SIDE if n in full})


def _rel_bias_grad(dbias, bucket):
    flat = dbias.reshape(SWA_Q_HEADS, -1)
    hi = flat.astype(MXU)
    lo = (flat - hi.astype(F32)).astype(MXU)
    onehot = (bucket[:, None] == jnp.arange(128, dtype=jnp.int32)[None, :]).astype(MXU)
    r = matmul_nn(jnp.concatenate([hi, lo], axis=0), onehot, "rel_bias_grad")
    return (r[0:8] + r[8:16])[:, :REL_BUCKETS].T


def local_step(x, tgt, fulls, small, comm=None):
    T = x.shape[0]
    cosr, sinr = rope_tables(T)
    bias, bucket = swa_bias_table(small["rel_bias"])
    shared = (cosr, sinr, bias)
    Ps, Ss = [], []
    h, full = x, fulls[0]
    for l in range(DEPTH):
        P = _layer_params(l, full, small)
        if comm:
            h, S, P, got = _layer_fwd(h, P, shared, comm["weight_parts"](l), comm["ffn_from"])
            full = comm["attn_from"](got) if l + 1 < DEPTH else None
        else:
            h, S, P, _ = _layer_fwd(h, P, shared)
            full = fulls[l + 1] if l + 1 < DEPTH else None
        Ps.append(P)
        Ss.append(S)
    dh, sq = loss_kernel(h, tgt)
    grads = [None] * DEPTH
    dbias_sum = None
    pending = [] if comm else None
    for l in reversed(range(DEPTH)):
        dh, grads[l], dbias, got = _layer_bwd(dh, Ps[l], Ss[l], shared, pending)
        dbias_sum = dbias if dbias_sum is None else dbias_sum + dbias
        if comm:
            comm["landed"](l, ["w_down"], got[0][:1])
            comm["landed"](l, ["w_up"], got[1])
            if pending:
                comm["landed"](l + 1, ATTN_SIDE, got[0][1:])
            pending = [_shards_from_full(n, grads[l][n]) for n in ATTN_SIDE]
    return sq, dh, grads, _rel_bias_grad(dbias_sum, bucket), pending


WEIGHTS = ['attn_pre_norm', 'w_in', 'forget_bias', 'swa_sinks', 'rel_bias', 'q_latent_norm', 'w_uq',
           'kv_latent_norm', 'w_ukv', 'group_norm', 'w_out', 'attn_post_norm', 'ffn_pre_norm', 'w_up', 'conv_w',
           'conv_b', 'w_down', 'ffn_post_norm']
SMALL_PER_LAYER = ['attn_pre_norm', 'forget_bias', 'swa_sinks', 'q_latent_norm', 'kv_latent_norm', 'group_norm',
                   'attn_post_norm', 'ffn_pre_norm', 'conv_b', 'ffn_post_norm', 'conv_w']


def kernel(x, attn_pre_norm, w_in, forget_bias, swa_sinks, rel_bias, q_latent_norm, w_uq, kv_latent_norm, w_ukv, group_norm, w_out, attn_post_norm, ffn_pre_norm, w_up, conv_w, conv_b, w_down, ffn_post_norm, loss_target, m_attn_pre_norm, m_w_in, m_forget_bias, m_swa_sinks, m_rel_bias, m_q_latent_norm, m_w_uq, m_kv_latent_norm, m_w_ukv, m_group_norm, m_w_out, m_attn_post_norm, m_ffn_pre_norm, m_w_up, m_conv_w, m_conv_b, m_w_down, m_ffn_post_norm, v_attn_pre_norm, v_w_in, v_forget_bias, v_swa_sinks, v_rel_bias, v_q_latent_norm, v_w_uq, v_kv_latent_norm, v_w_ukv, v_group_norm, v_w_out, v_attn_post_norm, v_ffn_pre_norm, v_w_up, v_conv_w, v_conv_b, v_w_down, v_ffn_post_norm):
    W = dict(attn_pre_norm=attn_pre_norm, w_in=w_in, forget_bias=forget_bias, swa_sinks=swa_sinks, rel_bias=rel_bias,
             q_latent_norm=q_latent_norm, w_uq=w_uq, kv_latent_norm=kv_latent_norm, w_ukv=w_ukv,
             group_norm=group_norm, w_out=w_out, attn_post_norm=attn_post_norm, ffn_pre_norm=ffn_pre_norm,
             w_up=w_up, conv_w=conv_w, conv_b=conv_b, w_down=w_down, ffn_post_norm=ffn_post_norm)
    M = dict(attn_pre_norm=m_attn_pre_norm, w_in=m_w_in, forget_bias=m_forget_bias, swa_sinks=m_swa_sinks,
             rel_bias=m_rel_bias, q_latent_norm=m_q_latent_norm, w_uq=m_w_uq, kv_latent_norm=m_kv_latent_norm,
             w_ukv=m_w_ukv, group_norm=m_group_norm, w_out=m_w_out, attn_post_norm=m_attn_post_norm,
             ffn_pre_norm=m_ffn_pre_norm, w_up=m_w_up, conv_w=m_conv_w, conv_b=m_conv_b, w_down=m_w_down,
             ffn_post_norm=m_ffn_post_norm)
    V = dict(attn_pre_norm=v_attn_pre_norm, w_in=v_w_in, forget_bias=v_forget_bias, swa_sinks=v_swa_sinks,
             rel_bias=v_rel_bias, q_latent_norm=v_q_latent_norm, w_uq=v_w_uq, kv_latent_norm=v_kv_latent_norm,
             w_ukv=v_w_ukv, group_norm=v_group_norm, w_out=v_w_out, attn_post_norm=v_attn_post_norm,
             ffn_pre_norm=v_ffn_pre_norm, w_up=v_w_up, conv_w=v_conv_w, conv_b=v_conv_b, w_down=v_w_down,
             ffn_post_norm=v_ffn_post_norm)
    me = 4 * lax.axis_index("x") + 2 * lax.axis_index("y") + lax.axis_index("c")

    def attn_shards(l):
        return [W[n][l].astype(MXU) for n in ATTN_SIDE]

    def weight_parts(l):
        return ([W["w_down"][l].astype(MXU)] + (attn_shards(l + 1) if l + 1 < DEPTH else []),
                [W["w_up"][l].astype(MXU), conv_w[l]])

    def ffn_from(got_a, got_b):
        return dict(w_down=_full_from_shards("w_down", got_a[0]), w_up=_full_from_shards("w_up", got_b[0]),
                    conv_w=got_b[1].transpose(1, 0, 2).reshape(3, 2 * D_FF))

    def attn_from(got):
        return {n: _full_from_shards(n, s) for n, s in zip(ATTN_SIDE, got[0][1:])}

    landed = [{} for _ in range(DEPTH)]

    def on_landed(l, names, arrays):
        landed[l].update(zip(names, arrays))

    comm = dict(weight_parts=weight_parts, ffn_from=ffn_from, attn_from=attn_from, landed=on_landed)
    full0 = dict(zip(ATTN_SIDE, map(_full_from_shards, ATTN_SIDE, exchange(attn_shards(0), "gather_weights", True))))
    sq, dx, grads, drel, last = local_step(x[0], loss_target[0], [full0], W, comm)
    on_landed(0, ATTN_SIDE, exchange(last, "scatter_grads", False))
    G = {n: jnp.stack([landed[l][n] for l in range(DEPTH)], axis=1) for n in SHARDED}

    parts, shapes = [], []
    for l in range(DEPTH):
        for n in SMALL_PER_LAYER:
            parts.append(grads[l][n].astype(F32).reshape(-1))
            shapes.append(grads[l][n].shape)
    parts += [drel.reshape(-1), jnp.sum(sq).reshape(1) * (0.5 / D_MODEL)]
    shapes += [drel.shape, (1,)]
    red = _unpack(sum_devices(exchange([_pack(parts, 128, 8, F32)], "gather_small", True)[0], "sum_small"), shapes)
    k = 0
    per = {n: [] for n in SMALL_PER_LAYER}
    for l in range(DEPTH):
        for n in SMALL_PER_LAYER:
            per[n].append(red[k])
            k += 1
    for n in SMALL_PER_LAYER:
        G[n] = jnp.stack(per[n]).reshape((DEPTH, 3, 2 * D_FF) if n == "conv_w" else W[n].shape)
    G["rel_bias"] = red[k]
    loss = red[k + 1][0]
    G["conv_w"] = lax.dynamic_slice_in_dim(G["conv_w"], me * 704, 704, axis=2)

    delta, new_m, new_v = {}, {}, {}
    for n in WEIGHTS:
        shp = W[n].shape
        v2 = lambda a: a.reshape(-1, shp[-1])
        g = G[n].reshape(N_DEV, -1, shp[-1]) if n in SHARDED else v2(G[n])
        g, d, nm, nv = adamw(v2(W[n]), g, v2(M[n]), v2(V[n]), "adamw_" + n)
        G[n], delta[n], new_m[n], new_v[n] = g.reshape(shp), d.reshape(shp), nm.reshape(shp), nv.reshape(shp)
    return (loss, dx[None], *[G[n] for n in WEIGHTS], *[delta[n] for n in WEIGHTS],
            *[new_m[n] for n in WEIGHTS], *[new_v[n] for n in WEIGHTS])
```

```python
import functools
import math

import numpy as np
import jax
import jax.numpy as jnp
from jax import lax
from jax.experimental import pallas as pl
from jax.experimental.pallas import tpu as pltpu

F32 = jnp.float32
MXU = jnp.bfloat16

N_DEV = 8
DEPTH = 4
D_MODEL = 1024
HEAD_DIM = 64
WINDOW = 128
SWA_Q_HEADS = 8
REL_BUCKETS = 32
REL_MAX_DIST = 128
MLA_QK_DIM = 96
ROPE_DIM = 32
ROPE_THETA = 10000.0
D_FF = 2816
EPS = 1e-6
NEG = -1e30
IN_COLS = 1956
IN_COLS_P = 2048
C_QA, C_KA, C_VA = 0, 512, 640
C_QF, C_KF, C_VF = 768, 1024, 1280
C_CQ, C_CKV, C_MISC = 1536, 1792, 1920
ROPE_LANE0 = 64
ADAM_LR, ADAM_B1, ADAM_B2, ADAM_EPS, ADAM_WD, ADAM_STEP = 0.001, 0.9, 0.999, 1e-08, 0.01, 10

VMEM_LIMIT = 56 * 1024 * 1024
PACK_COLS = 1024
PACK_ROW_ALIGN = 16


def _cparams(sem=None):
    return pltpu.CompilerParams(dimension_semantics=sem, vmem_limit_bytes=VMEM_LIMIT)


def _tile(n, pref):
    if n <= pref:
        return n
    t = pref - pref % 128
    while t >= 128:
        if n % t == 0:
            return t
        t -= 128
    return n


def _dot(a, b):
    return jnp.dot(a.astype(MXU), b.astype(MXU), preferred_element_type=F32)


def _dot_nt(a, b):
    return lax.dot_general(a.astype(MXU), b.astype(MXU), (((1,), (1,)), ((), ())),
                           preferred_element_type=F32)


def _rms_fwd(x, g):
    return x * lax.rsqrt(jnp.mean(x * x, axis=-1, keepdims=True) + EPS) * g


def _rms_bwd(dy, x, g, n=None):
    r = lax.rsqrt(jnp.mean(x * x, axis=-1, keepdims=True) + EPS)
    xh = x * r
    dg = jnp.sum(dy * xh, axis=0, keepdims=True)
    dxh = dy * g
    dx = r * (dxh - xh * jnp.mean(dxh * xh, axis=-1, keepdims=True))
    return dx, dg


def _acc_out(ref, val, first):
    @pl.when(first)
    def _():
        ref[...] = val

    @pl.when(jnp.logical_not(first))
    def _():
        ref[...] += val


def norm_matmul(x, g, w, name, lo_tiles=0, tn_pref=512):
    T, K = x.shape
    N = w.shape[1]
    tm, tn = _tile(T, 1024), _tile(N, tn_pref)

    def body(x_ref, g_ref, w_ref, o_ref, hT_ref, *rest):
        h_sc = rest[-1]
        j = pl.program_id(1)

        @pl.when(j == 0)
        def _():
            h = _rms_fwd(x_ref[...], g_ref[...])
            h_sc[...] = h.astype(MXU)
            hT_ref[...] = h.T.astype(MXU)

        r = jnp.dot(h_sc[...], w_ref[...], preferred_element_type=F32)
        o_ref[...] = r
        if lo_tiles:
            @pl.when(j < lo_tiles)
            def _():
                rest[0][...] = r.astype(MXU)

    out_specs = [pl.BlockSpec((tm, tn), lambda i, j: (i, j)), pl.BlockSpec((K, tm), lambda i, j: (0, i))]
    out_shape = [jax.ShapeDtypeStruct((T, N), F32), jax.ShapeDtypeStruct((K, T), MXU)]
    if lo_tiles:
        out_specs.append(pl.BlockSpec((tm, tn), lambda i, j: (i, jnp.minimum(j, lo_tiles - 1))))
        out_shape.append(jax.ShapeDtypeStruct((T, lo_tiles * tn), MXU))
    return pl.pallas_call(
        body, name=name, grid=(T // tm, N // tn),
        in_specs=[pl.BlockSpec((tm, K), lambda i, j: (i, 0)),
                  pl.BlockSpec((1, K), lambda i, j: (0, 0)),
                  pl.BlockSpec((K, tn), lambda i, j: (0, j))],
        out_specs=out_specs, out_shape=out_shape,
        scratch_shapes=[pltpu.VMEM((tm, K), MXU)],
        compiler_params=_cparams(("parallel", "arbitrary")),
    )(x, g, w)


def matmul_nn(a, b, name, out_dtype=F32):
    M, K = a.shape
    N = b.shape[1]
    tm, tn, tk = _tile(M, 1408), _tile(N, 1536), _tile(K, 1024)
    nk = K // tk

    def body(a_ref, b_ref, o_ref, acc):
        k = pl.program_id(2)
        part = _dot(a_ref[...], b_ref[...])
        _acc_out(acc, part, k == 0)

        @pl.when(k == nk - 1)
        def _():
            o_ref[...] = acc[...].astype(out_dtype)

    return pl.pallas_call(
        body, name=name, grid=(M // tm, N // tn, nk),
        in_specs=[pl.BlockSpec((tm, tk), lambda i, j, k: (i, k)),
                  pl.BlockSpec((tk, tn), lambda i, j, k: (k, j))],
        out_specs=pl.BlockSpec((tm, tn), lambda i, j, k: (i, j)),
        out_shape=jax.ShapeDtypeStruct((M, N), out_dtype),
        scratch_shapes=[pltpu.VMEM((tm, tn), F32)],
        compiler_params=_cparams(("parallel", "parallel", "arbitrary")),
    )(a, b)


def matmul_nt_normbwd(dy, w, x, g, dres, name):
    T, N = dy.shape
    K = w.shape[0]
    tm, tn = _tile(T, 1024), _tile(N, 1536)
    nj = N // tn

    def body(dy_ref, w_ref, x_ref, g_ref, dres_ref, dx_ref, dg_ref, acc):
        i, j = pl.program_id(0), pl.program_id(1)
        _acc_out(acc, _dot_nt(dy_ref[...], w_ref[...]), j == 0)

        @pl.when(j == nj - 1)
        def _():
            dx, dg = _rms_bwd(acc[...], x_ref[...], g_ref[...])
            dx_ref[...] = dres_ref[...] + dx
            _acc_out(dg_ref, dg, i == 0)

    return pl.pallas_call(
        body, name=name, grid=(T // tm, nj),
        in_specs=[pl.BlockSpec((tm, tn), lambda i, j: (i, j)),
                  pl.BlockSpec((K, tn), lambda i, j: (0, j)),
                  pl.BlockSpec((tm, K), lambda i, j: (i, 0)),
                  pl.BlockSpec((1, K), lambda i, j: (0, 0)),
                  pl.BlockSpec((tm, K), lambda i, j: (i, 0))],
        out_specs=[pl.BlockSpec((tm, K), lambda i, j: (i, 0)),
                   pl.BlockSpec((1, K), lambda i, j: (0, 0))],
        out_shape=[jax.ShapeDtypeStruct((T, K), F32), jax.ShapeDtypeStruct((1, K), F32)],
        scratch_shapes=[pltpu.VMEM((tm, K), F32)],
        compiler_params=_cparams(("arbitrary", "arbitrary")),
    )(dy, w, x, g, dres)


def loss_kernel(y, tgt):
    T, D = y.shape
    tm = _tile(T, 512)

    def body(y_ref, t_ref, dy_ref, acc_ref):
        e = y_ref[...] - t_ref[...]
        dy_ref[...] = e * (1.0 / D)
        _acc_out(acc_ref, jnp.sum(e * e, axis=0, keepdims=True), pl.program_id(0) == 0)

    return pl.pallas_call(
        body, name="loss", grid=(T // tm,),
        in_specs=[pl.BlockSpec((tm, D), lambda i: (i, 0)), pl.BlockSpec((tm, D), lambda i: (i, 0))],
        out_specs=[pl.BlockSpec((tm, D), lambda i: (i, 0)), pl.BlockSpec((1, D), lambda i: (0, 0))],
        out_shape=[jax.ShapeDtypeStruct((T, D), F32), jax.ShapeDtypeStruct((1, D), F32)],
        compiler_params=_cparams(("arbitrary",)),
    )(y, tgt)


def _rope_partner(x):
    lane = lax.broadcasted_iota(jnp.int32, (1, 128), 1)
    return jnp.where(lane < ROPE_LANE0 + ROPE_DIM // 2, pltpu.roll(x, 128 - ROPE_DIM // 2, 1),
                     pltpu.roll(x, ROPE_DIM // 2, 1))


def _rope_apply(x, cos, sin_signed):
    return x * cos + _rope_partner(x) * sin_signed


def _rope_apply_bwd(dy, cos, sin_signed):
    lane = lax.broadcasted_iota(jnp.int32, (1, 128), 1)
    rotary = (lane >= ROPE_LANE0) & (lane < ROPE_LANE0 + ROPE_DIM)
    return dy * cos + jnp.where(rotary, _rope_partner(dy * sin_signed), 0.0)


def rope_tables(T):
    pos = jnp.arange(T, dtype=F32)
    inv_freq = ROPE_THETA ** (-(jnp.arange(ROPE_DIM // 2, dtype=F32) * 2.0 / ROPE_DIM))
    ang = pos[:, None] * inv_freq[None, :]
    cos, sin = jnp.cos(ang), jnp.sin(ang)
    z = jnp.zeros((T, ROPE_LANE0), F32)
    z2 = jnp.zeros((T, 128 - ROPE_LANE0 - ROPE_DIM), F32)
    cosr = jnp.concatenate([z, cos, cos, z2], axis=1)
    sinr = jnp.concatenate([z, -sin, sin, z2], axis=1)
    return cosr, sinr


def mla_prep(proj, gq, gkv, wuq, wukv, cosr, sinr):
    T = proj.shape[0]
    tm = _tile(T, 512)

    def body(cq_ref, ckv_ref, misc_ref, gq_ref, gkv_ref, wuq_ref, wukv_ref, cos_ref, sin_ref,
             q_ref, k_ref, v_ref, cqT_ref, ckvT_ref):
        lane = lax.broadcasted_iota(jnp.int32, (1, 128), 1)
        cosr_, sinr_ = cos_ref[...], sin_ref[...]
        cosq = cosr_ + jnp.where(lane < ROPE_LANE0, 1.0, 0.0)
        cqn = _rms_fwd(cq_ref[...], gq_ref[...])
        cqT_ref[...] = cqn.T.astype(MXU)
        qm = _dot(cqn, wuq_ref[...])
        q_ref[...] = jnp.concatenate(
            [_rope_apply(qm[:, 128 * h:128 * (h + 1)], cosq, sinr_) for h in range(4)], axis=1).astype(MXU)
        ckvn = _rms_fwd(ckv_ref[...], gkv_ref[...])
        ckvT_ref[...] = ckvn.T.astype(MXU)
        kv = _dot(ckvn, wukv_ref[...])
        kr = _rope_apply(misc_ref[...], cosr_, sinr_)
        k_ref[...] = jnp.concatenate(
            [kv[:, 128 * h:128 * (h + 1)] + kr for h in range(4)], axis=1).astype(MXU)
        v_ref[...] = kv[:, 512:768].astype(MXU)

    row = lambda i: (i, 0)
    const = lambda i: (0, 0)
    return pl.pallas_call(
        body, name="mla_prep", grid=(T // tm,),
        in_specs=[pl.BlockSpec((tm, 256), lambda i: (i, C_CQ // 256)),
                  pl.BlockSpec((tm, 128), lambda i: (i, C_CKV // 128)),
                  pl.BlockSpec((tm, 128), lambda i: (i, C_MISC // 128)),
                  pl.BlockSpec((1, 256), const), pl.BlockSpec((1, 128), const),
                  pl.BlockSpec((256, 512), const), pl.BlockSpec((128, 768), const),
                  pl.BlockSpec((tm, 128), row), pl.BlockSpec((tm, 128), row)],
        out_specs=[pl.BlockSpec((tm, 512), row), pl.BlockSpec((tm, 512), row), pl.BlockSpec((tm, 256), row),
                   pl.BlockSpec((256, tm), lambda i: (0, i)), pl.BlockSpec((128, tm), lambda i: (0, i))],
        out_shape=[jax.ShapeDtypeStruct((T, 512), MXU), jax.ShapeDtypeStruct((T, 512), MXU),
                   jax.ShapeDtypeStruct((T, 256), MXU),
                   jax.ShapeDtypeStruct((256, T), MXU), jax.ShapeDtypeStruct((128, T), MXU)],
        compiler_params=_cparams(("parallel",)),
    )(proj, proj, proj, gq, gkv, wuq, wukv, cosr, sinr)


def mla_prep_bwd(dq, dk, dv, proj, gq, gkv, wuq, wukv, cosr, sinr):
    T = proj.shape[0]
    tm = _tile(T, 512)

    def body(dq_ref, dk_ref, dv_ref, cq_ref, ckv_ref, gq_ref, gkv_ref, wuq_ref, wukv_ref, cos_ref, sin_ref,
             dqm_ref, dkv_ref, dcq_ref, dckv_ref, dmisc_ref, dgq_ref, dgkv_ref):
        first = pl.program_id(0) == 0
        lane = lax.broadcasted_iota(jnp.int32, (1, 128), 1)
        cosr_, sinr_ = cos_ref[...], sin_ref[...]
        cosq = cosr_ + jnp.where(lane < ROPE_LANE0, 1.0, 0.0)
        dqv = dq_ref[...]
        dqm = jnp.concatenate(
            [_rope_apply_bwd(dqv[:, 128 * h:128 * (h + 1)], cosq, sinr_) for h in range(4)], axis=1)
        dqm_ref[...] = dqm.astype(MXU)
        dcq, dgq = _rms_bwd(_dot_nt(dqm, wuq_ref[...]), cq_ref[...], gq_ref[...])
        dcq_ref[...] = dcq
        _acc_out(dgq_ref, dgq, first)
        dkv_ = dk_ref[...]
        heads = [dkv_[:, 128 * h:128 * (h + 1)] for h in range(4)]
        dkr = heads[0] + heads[1] + heads[2] + heads[3]
        dmisc_ref[...] = _rope_apply_bwd(dkr, cosr_, sinr_)
        dkvm = jnp.concatenate([jnp.where(lane < ROPE_LANE0, hd, 0.0) for hd in heads] + [dv_ref[...]], axis=1)
        dkv_ref[...] = dkvm.astype(MXU)
        dckv, dgkv = _rms_bwd(_dot_nt(dkvm, wukv_ref[...]), ckv_ref[...], gkv_ref[...])
        dckv_ref[...] = dckv
        _acc_out(dgkv_ref, dgkv, first)

    row = lambda i: (i, 0)
    const = lambda i: (0, 0)
    return pl.pallas_call(
        body, name="mla_prep_bwd", grid=(T // tm,),
        in_specs=[pl.BlockSpec((tm, 512), row), pl.BlockSpec((tm, 512), row), pl.BlockSpec((tm, 256), row),
                  pl.BlockSpec((tm, 256), lambda i: (i, C_CQ // 256)),
                  pl.BlockSpec((tm, 128), lambda i: (i, C_CKV // 128)),
                  pl.BlockSpec((1, 256), const), pl.BlockSpec((1, 128), const),
                  pl.BlockSpec((256, 512), const), pl.BlockSpec((128, 768), const),
                  pl.BlockSpec((tm, 128), row), pl.BlockSpec((tm, 128), row)],
        out_specs=[pl.BlockSpec((tm, 512), row), pl.BlockSpec((tm, 768), row), pl.BlockSpec((tm, 256), row),
                   pl.BlockSpec((tm, 128), row), pl.BlockSpec((tm, 128), row),
                   pl.BlockSpec((1, 256), const), pl.BlockSpec((1, 128), const)],
        out_shape=[jax.ShapeDtypeStruct((T, 512), MXU), jax.ShapeDtypeStruct((T, 768), MXU),
                   jax.ShapeDtypeStruct((T, 256), F32), jax.ShapeDtypeStruct((T, 128), F32),
                   jax.ShapeDtypeStruct((T, 128), F32),
                   jax.ShapeDtypeStruct((1, 256), F32), jax.ShapeDtypeStruct((1, 128), F32)],
        compiler_params=_cparams(("arbitrary",)),
    )(dq, dk, dv, proj, proj, gq, gkv, wuq, wukv, cosr, sinr)


def _split3(x):
    hi = x.astype(MXU)
    r1 = x - hi.astype(F32)
    mid = r1.astype(MXU)
    lo = (r1 - mid.astype(F32)).astype(MXU)
    return hi, mid, lo


def _tri_matmul(tri, x):
    hi, mid, lo = _split3(x)
    d = lambda p: jnp.dot(tri, p, preferred_element_type=F32)
    return d(hi) + d(mid) + d(lo)


def _log_sigmoid(z):
    return jnp.minimum(z, 0.0) - jnp.log(1.0 + jnp.exp(-jnp.abs(z)))


def fox_gate(proj, fbias):
    T = proj.shape[0]
    tb = _tile(T, 512)

    def body(misc_ref, b_ref, fc_ref, fr_ref, frep_ref, carry):
        @pl.when(pl.program_id(0) == 0)
        def _():
            carry[...] = jnp.zeros_like(carry)

        lane = lax.broadcasted_iota(jnp.int32, (1, 128), 1)
        lf = jnp.where(lane < 4, _log_sigmoid(misc_ref[...] + b_ref[...]), 0.0)
        r = lax.broadcasted_iota(jnp.int32, (tb, tb), 0)
        c = lax.broadcasted_iota(jnp.int32, (tb, tb), 1)
        tri = jnp.where(r >= c, 1.0, 0.0).astype(MXU)
        F = _tri_matmul(tri, lf) + carry[...]
        carry[...] = carry[...] + jnp.sum(lf, axis=0, keepdims=True)
        fc_ref[0] = F
        fc_ref[1] = pltpu.roll(F, 126, 1)
        ft = F.T[0:8, :]
        fr_ref[0] = ft
        fr_ref[1] = pltpu.roll(ft, 6, 0)
        for h in range(4):
            frep_ref[h] = jnp.broadcast_to(_lane_pick(F, h), (tb, 128))

    return pl.pallas_call(
        body, name="fox_gate", grid=(T // tb,),
        in_specs=[pl.BlockSpec((tb, 128), lambda i: (i, C_MISC // 128)), pl.BlockSpec((1, 128), lambda i: (0, 0))],
        out_specs=[pl.BlockSpec((2, tb, 128), lambda i: (0, i, 0)), pl.BlockSpec((2, 8, tb), lambda i: (0, 0, i)),
                   pl.BlockSpec((4, tb, 128), lambda i: (0, i, 0))],
        out_shape=[jax.ShapeDtypeStruct((2, T, 128), F32), jax.ShapeDtypeStruct((2, 8, T), F32),
                   jax.ShapeDtypeStruct((4, T, 128), F32)],
        scratch_shapes=[pltpu.VMEM((1, 128), F32)],
        compiler_params=_cparams(("arbitrary",)),
    )(proj, fbias)


def fox_gate_bwd(dFq, dFk, proj, fbias):
    T = proj.shape[0]
    tb = _tile(T, 512)
    nb = T // tb

    def body(dq_ref, dk_ref, misc_ref, b_ref, dm_ref, db_ref, carry):
        first = pl.program_id(0) == 0

        @pl.when(first)
        def _():
            carry[...] = jnp.zeros_like(carry)

        lane = lax.broadcasted_iota(jnp.int32, (1, 128), 1)
        dF = jnp.where(lane < 4, (dq_ref[0] + dk_ref[0]) + pltpu.roll(dq_ref[1] + dk_ref[1], 2, 1), 0.0)
        r = lax.broadcasted_iota(jnp.int32, (tb, tb), 0)
        c = lax.broadcasted_iota(jnp.int32, (tb, tb), 1)
        tri = jnp.where(r <= c, 1.0, 0.0).astype(MXU)
        dlf = _tri_matmul(tri, dF) + carry[...]
        carry[...] = carry[...] + jnp.sum(dF, axis=0, keepdims=True)
        z = misc_ref[...] + b_ref[...]
        dz = jnp.where(lane < 4, dlf * (1.0 / (1.0 + jnp.exp(z))), 0.0)
        dm_ref[...] = dz
        _acc_out(db_ref, jnp.sum(dz, axis=0, keepdims=True), first)

    return pl.pallas_call(
        body, name="fox_gate_bwd", grid=(nb,),
        in_specs=[pl.BlockSpec((2, tb, 128), lambda i: (0, nb - 1 - i, 0)),
                  pl.BlockSpec((2, tb, 128), lambda i: (0, nb - 1 - i, 0)),
                  pl.BlockSpec((tb, 128), lambda i: (nb - 1 - i, C_MISC // 128)),
                  pl.BlockSpec((1, 128), lambda i: (0, 0))],
        out_specs=[pl.BlockSpec((tb, 128), lambda i: (nb - 1 - i, 0)), pl.BlockSpec((1, 128), lambda i: (0, 0))],
        out_shape=[jax.ShapeDtypeStruct((T, 128), F32), jax.ShapeDtypeStruct((1, 128), F32)],
        scratch_shapes=[pltpu.VMEM((1, 128), F32)],
        compiler_params=_cparams(("arbitrary",)),
    )(dFq, dFk, proj, fbias)


FLASH_TILE = 512


def _row_stat_tile(a, b, n):
    at = jnp.broadcast_to(a, (n, 128)).T[0:8, :]
    bt = jnp.broadcast_to(b, (n, 128)).T[0:8, :]
    sub = lax.broadcasted_iota(jnp.int32, (8, 1), 0)
    return jnp.where(sub == 0, at, jnp.where(sub == 1, bt, 0.0))


def _col_stat_tile(a, b):
    lane = lax.broadcasted_iota(jnp.int32, (1, 128), 1)
    return jnp.where(lane == 0, a, jnp.where(lane == 1, b, 0.0))


def _lane_pick(x, h):
    lane = lax.broadcasted_iota(jnp.int32, (1, 128), 1)
    return jnp.sum(jnp.where(lane == h, x, 0.0), axis=1, keepdims=True)


def _half_mask(h):
    lane = lax.broadcasted_iota(jnp.int32, (1, 128), 1)
    return (lane // HEAD_DIM) == h


def _call_hosting(body, name, grid, args, in_specs, out_specs, out_shape, scratch, exch):
    n_out = len(out_shape)
    if exch is not None:
        body, (xargs, xin, xout, xshape, xscratch) = hosted_exchange(
            body, len(args), n_out, len(scratch), grid, *exch)
        args, in_specs, out_specs = args + xargs, in_specs + xin, out_specs + xout
        out_shape, scratch = out_shape + xshape, scratch + xscratch
    res = pl.pallas_call(
        body, name=name, grid=grid, in_specs=in_specs, out_specs=out_specs, out_shape=out_shape,
        scratch_shapes=scratch, compiler_params=_cparams(("arbitrary",) * len(grid)),
    )(*args)
    return res[:n_out], res[n_out:]


def flash_fwd(q, k, v, frep, frow, *, qblk, kblk, vblk, nq, scale, name, exch=None):
    T = q.shape[0]
    tq = tk = _tile(T, FLASH_TILE)
    wq = 128 * nq
    has_f = frep is not None

    def body(*refs):
        if has_f:
            q_ref, k_ref, v_ref, fk_ref, fr_ref, o_ref, lr_ref, vT_sc, m_sc, acc_sc = refs
        else:
            q_ref, k_ref, v_ref, o_ref, lr_ref, vT_sc, m_sc, acc_sc = refs
        i = pl.program_id(1)

        @pl.when(i == 0)
        def _():
            vT_sc[...] = v_ref[...].astype(F32).T.astype(MXU)

        diag = lax.broadcasted_iota(jnp.int32, (tk, 1), 0) <= lax.broadcasted_iota(jnp.int32, (1, tq), 1)
        row_half = lax.broadcasted_iota(jnp.int32, (128, 1), 0) // HEAD_DIM
        qb = q_ref[...].astype(F32) * scale
        if nq == 1:
            qhs = [jnp.where(_half_mask(h), qb, 0).astype(MXU) for h in range(2)]
        else:
            qhs = [qb[:, 128 * h:128 * (h + 1)].astype(MXU) for h in range(2)]
        for h in range(2):
            m_sc[h] = jnp.full((1, tq), NEG, F32)
            acc_sc[h] = jnp.zeros((128, tq), F32)

        def make_step(masked):
            def step(j, carry):
                off = pl.multiple_of(j * tk, tk)
                ks = k_ref[pl.ds(off, tk), :]
                vT = vT_sc[:, pl.ds(off, tk)]
                for h in range(2):
                    kh = ks if nq == 1 else ks[:, 128 * h:128 * (h + 1)]
                    sT = _dot_nt(kh, qhs[h])
                    if has_f:
                        fk = fk_ref[h, pl.ds(off, tk), :]
                        sT = sT + (fr_ref[0, h:h + 1, :] - jnp.concatenate([fk] * (tq // 128), axis=1))
                    if masked:
                        sT = jnp.where(diag, sT, NEG)
                    m_prev = m_sc[h]
                    m_new = jnp.maximum(m_prev, jnp.max(sT, axis=0, keepdims=True))
                    alpha = jnp.exp(m_prev - m_new)
                    pT = jnp.exp(sT - m_new)
                    vTh = jnp.where(row_half == h, vT, jnp.ones_like(vT))
                    acc_sc[h] = alpha * acc_sc[h] + _dot(vTh, pT)
                    m_sc[h] = m_new
                return carry
            return step

        lax.fori_loop(0, i, make_step(False), 0)
        make_step(True)(i, 0)
        outs, lses = [], []
        for h in range(2):
            acc = acc_sc[h]
            outs.append(acc / pltpu.roll(acc, HEAD_DIM, 0))
            l = acc_sc[h, HEAD_DIM * (1 - h):HEAD_DIM * (1 - h) + 1, :]
            lses.append(m_sc[h] + jnp.log(l))
        o_ref[...] = jnp.where(row_half == 0, outs[0], outs[1]).T
        sub = lax.broadcasted_iota(jnp.int32, (8, 1), 0)
        lr_ref[0] = jnp.where(sub == 0, lses[0], jnp.where(sub == 1, lses[1], 0.0))

    in_specs = [pl.BlockSpec((tq, wq), lambda p, i: (i, qblk + p)),
                pl.BlockSpec((T, wq), lambda p, i: (0, kblk + p)),
                pl.BlockSpec((T, 128), lambda p, i: (0, vblk + p))]
    args = [q, k, v]
    if has_f:
        in_specs += [pl.BlockSpec((2, T, 128), lambda p, i: (p, 0, 0)),
                     pl.BlockSpec((1, 8, tq), lambda p, i: (p, 0, i))]
        args += [frep, frow]
    out_specs = [pl.BlockSpec((tq, 128), lambda p, i: (i, p)), pl.BlockSpec((1, 8, tq), lambda p, i: (p, 0, i))]
    out_shape = [jax.ShapeDtypeStruct((T, 256), F32), jax.ShapeDtypeStruct((2, 8, T), F32)]
    scratch = [pltpu.VMEM((128, T), MXU), pltpu.VMEM((2, 1, tq), F32), pltpu.VMEM((2, 128, tq), F32)]
    return _call_hosting(body, name, (2, T // tq), args, in_specs, out_specs, out_shape, scratch, exch)


def flash_bwd(q, k, v, do, o, lrow, fcol, frow, *, qblk, kblk, vblk, nq, scale, name, exch=None):
    T = q.shape[0]
    tq = tk = _tile(T, FLASH_TILE)
    wq = 128 * nq
    nqb = T // tq
    has_f = fcol is not None

    def body(*refs):
        if has_f:
            (q_ref, k_ref, v_ref, do_ref, o_ref, lr_ref, fc_ref, fr_ref,
             dq_ref, dk_ref, dv_ref, df_ref, dfq_ref, dk_sc, dv_sc, dqT_sc, d_sc, df_sc, dfq_sc) = refs
        else:
            q_ref, k_ref, v_ref, do_ref, o_ref, lr_ref, dq_ref, dk_ref, dv_ref, dk_sc, dv_sc, dqT_sc, d_sc = refs
        j = pl.program_id(1)
        diag = lax.broadcasted_iota(jnp.int32, (tk, 1), 0) <= lax.broadcasted_iota(jnp.int32, (1, tq), 1)
        hms = [_half_mask(h) for h in range(2)]

        @pl.when(j == 0)
        def _():
            dqT_sc[...] = jnp.zeros_like(dqT_sc)
            if has_f:
                dfq_sc[...] = jnp.zeros_like(dfq_sc)

            def delta(b, carry):
                off = pl.multiple_of(b * tq, tq)
                prod = do_ref[pl.ds(off, tq), :] * o_ref[pl.ds(off, tq), :]
                Ds = [jnp.sum(jnp.where(hms[h], prod, 0.0), axis=1, keepdims=True) for h in range(2)]
                d_sc[:, pl.ds(off, tq)] = _row_stat_tile(Ds[0], Ds[1], tq)
                return carry

            lax.fori_loop(0, nqb, delta, 0)

        kb = k_ref[...]
        vb = v_ref[...]
        if nq == 1:
            khs = [jnp.where(hms[h], kb, 0).astype(MXU) for h in range(2)]
        else:
            khs = [kb[:, 128 * h:128 * (h + 1)].astype(MXU) for h in range(2)]
        kTs = [kh.astype(F32).T.astype(MXU) for kh in khs]
        kss = [(kh.astype(F32) * scale).astype(MXU) for kh in khs]
        vhs = [jnp.where(hms[h], vb, 0).astype(MXU) for h in range(2)]
        fks = [_lane_pick(fc_ref[0], h) for h in range(2)] if has_f else None
        dv_sc[...] = jnp.zeros_like(dv_sc)
        dk_sc[...] = jnp.zeros_like(dk_sc)
        if has_f:
            df_sc[...] = jnp.zeros_like(df_sc)

        def make_step(masked):
            def step(i, carry):
                off = pl.multiple_of(i * tq, tq)
                qs = q_ref[pl.ds(off, tq), :]
                dos = do_ref[pl.ds(off, tq), :]
                for h in range(2):
                    qh = qs if nq == 1 else qs[:, 128 * h:128 * (h + 1)]
                    sT = _dot_nt(kss[h], qh)
                    if has_f:
                        sT = sT + (fr_ref[0, h:h + 1, pl.ds(off, tq)] - fks[h])
                    pT = jnp.exp(sT - lr_ref[0, h:h + 1, pl.ds(off, tq)])
                    if masked:
                        pT = jnp.where(diag, pT, 0.0)
                    dsT = pT * (_dot_nt(vhs[h], dos) - d_sc[h:h + 1, pl.ds(off, tq)])
                    dv_sc[...] += _dot(pT, jnp.where(hms[h], dos, 0))
                    qq = jnp.where(hms[h], qs, 0) if nq == 1 else qh
                    dk_sc[h if nq == 2 else 0] += _dot(dsT, qq)
                    dqT_sc[h if nq == 2 else 0, :, pl.ds(off, tq)] += _dot(kTs[h], dsT)
                    if has_f:
                        part = dsT[:, 0:128]
                        for c in range(1, tq // 128):
                            part = part + dsT[:, 128 * c:128 * (c + 1)]
                        df_sc[h] += part
                        dfq_sc[h:h + 1, pl.ds(off, tq)] += jnp.sum(dsT, axis=0, keepdims=True)
                return carry
            return step

        make_step(True)(j, 0)
        lax.fori_loop(j + 1, nqb, make_step(False), 0)
        if nq == 1:
            dk_ref[...] = dk_sc[0] * scale
        else:
            dk_ref[...] = jnp.concatenate([dk_sc[0], dk_sc[1]], axis=1) * scale
        dv_ref[...] = dv_sc[...]
        if has_f:
            df_ref[0] = _col_stat_tile(-jnp.sum(df_sc[0], axis=1, keepdims=True),
                                       -jnp.sum(df_sc[1], axis=1, keepdims=True))

        @pl.when(j == nqb - 1)
        def _():
            if nq == 1:
                dq_ref[...] = dqT_sc[0].T * scale
            else:
                dq_ref[...] = jnp.concatenate([dqT_sc[0].T, dqT_sc[1].T], axis=1) * scale
            if has_f:
                sub = lax.broadcasted_iota(jnp.int32, (128, 1), 0)
                rows = jnp.where(sub == 0, dfq_sc[0:1, :], jnp.where(sub == 1, dfq_sc[1:2, :], 0.0))
                dfq_ref[0] = rows.T

    in_specs = [pl.BlockSpec((T, wq), lambda p, j: (0, qblk + p)),
                pl.BlockSpec((tk, wq), lambda p, j: (j, kblk + p)),
                pl.BlockSpec((tk, 128), lambda p, j: (j, vblk + p)),
                pl.BlockSpec((T, 128), lambda p, j: (0, p)),
                pl.BlockSpec((T, 128), lambda p, j: (0, p)),
                pl.BlockSpec((1, 8, T), lambda p, j: (p, 0, 0))]
    args = [q, k, v, do, o, lrow]
    out_specs = [pl.BlockSpec((T, wq), lambda p, j: (0, p)),
                 pl.BlockSpec((tk, wq), lambda p, j: (j, p)), pl.BlockSpec((tk, 128), lambda p, j: (j, p))]
    out_shape = [jax.ShapeDtypeStruct((T, 2 * wq), F32), jax.ShapeDtypeStruct((T, 2 * wq), F32),
                 jax.ShapeDtypeStruct((T, 256), F32)]
    scratch = [pltpu.VMEM((nq, tk, 128), F32), pltpu.VMEM((tk, 128), F32), pltpu.VMEM((nq, 128, T), F32),
               pltpu.VMEM((8, T), F32)]
    if has_f:
        in_specs += [pl.BlockSpec((1, tk, 128), lambda p, j: (p, j, 0)),
                     pl.BlockSpec((1, 8, T), lambda p, j: (p, 0, 0))]
        args += [fcol, frow]
        out_specs += [pl.BlockSpec((1, tk, 128), lambda p, j: (p, j, 0)),
                      pl.BlockSpec((1, T, 128), lambda p, j: (p, 0, 0))]
        out_shape += [jax.ShapeDtypeStruct((2, T, 128), F32), jax.ShapeDtypeStruct((2, T, 128), F32)]
        scratch += [pltpu.VMEM((2, tk, 128), F32), pltpu.VMEM((8, T), F32)]
    return _call_hosting(body, name, (2, T // tk), args, in_specs, out_specs, out_shape, scratch, exch)


def _swa_align(pair, e, h):
    sel = jnp.where(_half_mask(e), pair, 0.0)
    if e == h:
        return sel
    return pltpu.roll(sel, HEAD_DIM, 1)


def _swa_mask(n):
    W = WINDOW
    qi = lax.broadcasted_iota(jnp.int32, (W, 2 * W), 0) + W
    kj = lax.broadcasted_iota(jnp.int32, (W, 2 * W), 1)
    dist = qi - kj
    return (dist >= 0) & (dist < W) & ((n > 0) | (kj >= W))


def swa_fwd(proj, bias, sinks):
    T = proj.shape[0]
    W = WINDOW
    nb = T // W
    scale = HEAD_DIM ** -0.5

    def body(sink_ref, q_ref, kp_ref, kc_ref, vp_ref, vc_ref, b_ref, o_ref, l_ref):
        n = pl.program_id(0)
        mask = _swa_mask(n)
        kband = jnp.concatenate([kp_ref[...], kc_ref[...]], axis=0).astype(MXU)
        vband = jnp.concatenate([vp_ref[...], vc_ref[...]], axis=0).astype(MXU)
        lane = lax.broadcasted_iota(jnp.int32, (1, 128), 1)
        lse_tile = jnp.zeros((W, 128), F32)
        mask4 = jnp.concatenate([mask] * 4, axis=0)
        pairs = []
        for h in range(2):
            q4 = jnp.concatenate([_swa_align(q_ref[:, 128 * ((4 * h + g) // 2):128 * ((4 * h + g) // 2 + 1)],
                                             g % 2, h) for g in range(4)], axis=0)
            sink4 = jnp.concatenate([jnp.full((W, 1), sink_ref[4 * h + g], F32) for g in range(4)], axis=0)
            s = _dot_nt(q4, kband) * scale + b_ref[4 * h:4 * h + 4].reshape(4 * W, 2 * W)
            s = jnp.where(mask4, s, NEG)
            m = jnp.maximum(jnp.max(s, axis=1, keepdims=True), sink4)
            e = jnp.exp(s - m)
            l = jnp.sum(e, axis=1, keepdims=True) + jnp.exp(sink4 - m)
            r = jnp.where(_half_mask(h), _dot(e, vband), 0.0) / l
            r = r + pltpu.roll(r, HEAD_DIM, 1)
            lse4 = m + jnp.log(l)
            for g in range(4):
                lse_tile = jnp.where(lane == 4 * h + g, lse4[W * g:W * (g + 1)], lse_tile)
            pairs.append(jnp.where(_half_mask(0), r[0:W], r[W:2 * W]))
            pairs.append(jnp.where(_half_mask(0), r[2 * W:3 * W], r[3 * W:4 * W]))
        o_ref[...] = jnp.concatenate(pairs, axis=1)
        l_ref[...] = lse_tile

    prev = lambda n: (jnp.maximum(n - 1, 0), C_KA // 128)
    cur = lambda n: (n, C_KA // 128)
    prev_v = lambda n: (jnp.maximum(n - 1, 0), C_VA // 128)
    cur_v = lambda n: (n, C_VA // 128)
    return pl.pallas_call(
        body, name="swa_fwd", grid=(nb,),
        in_specs=[pl.BlockSpec(memory_space=pltpu.SMEM),
                  pl.BlockSpec((W, 512), lambda n: (n, 0)),
                  pl.BlockSpec((W, 128), prev), pl.BlockSpec((W, 128), cur),
                  pl.BlockSpec((W, 128), prev_v), pl.BlockSpec((W, 128), cur_v),
                  pl.BlockSpec((8, W, 2 * W), lambda n: (0, 0, 0))],
        out_specs=[pl.BlockSpec((W, 512), lambda n: (n, 0)), pl.BlockSpec((W, 128), lambda n: (n, 0))],
        out_shape=[jax.ShapeDtypeStruct((T, 512), F32), jax.ShapeDtypeStruct((T, 128), F32)],
        compiler_params=_cparams(("parallel",)),
    )(sinks, proj, proj, proj, proj, proj, bias)


def swa_bwd(proj, bias, sinks, do, o, lse):
    T = proj.shape[0]
    W = WINDOW
    nb = T // W
    scale = HEAD_DIM ** -0.5

    def body(sink_ref, q_ref, kp_ref, kc_ref, vp_ref, vc_ref, b_ref, do_ref, o_ref, l_ref,
             dq_ref, dk_ref, dv_ref, db_ref, dsk_ref, ck, cv):
        n = pl.program_id(0)

        @pl.when(n == 0)
        def _():
            ck[...] = jnp.zeros_like(ck)
            cv[...] = jnp.zeros_like(cv)
            db_ref[...] = jnp.zeros_like(db_ref)
            dsk_ref[...] = jnp.zeros_like(dsk_ref)

        @pl.when(n < nb)
        def _():
            mask = _swa_mask(n)
            kb32 = jnp.concatenate([kp_ref[...], kc_ref[...]], axis=0)
            vb32 = jnp.concatenate([vp_ref[...], vc_ref[...]], axis=0)
            kband = kb32.astype(MXU)
            sub = lax.broadcasted_iota(jnp.int32, (8, 1), 0)
            dk_band = jnp.zeros((2 * W, 128), F32)
            dv_band = jnp.zeros((2 * W, 128), F32)
            dsk = jnp.zeros((8, 128), F32)
            dq_pairs = []
            mask4 = jnp.concatenate([mask] * 4, axis=0)
            for h in range(2):
                hm = _half_mask(h)
                km = jnp.where(hm, kb32, 0.0).astype(MXU)
                vm = jnp.where(hm, vb32, 0.0).astype(MXU)
                pbs = [slice(128 * ((4 * h + g) // 2), 128 * ((4 * h + g) // 2 + 1)) for g in range(4)]
                q4 = jnp.concatenate([_swa_align(q_ref[:, pbs[g]], g % 2, h) for g in range(4)], axis=0)
                do4 = jnp.concatenate([_swa_align(do_ref[:, pbs[g]], g % 2, h) for g in range(4)], axis=0)
                D4 = jnp.concatenate(
                    [jnp.sum(jnp.where(_half_mask(g % 2), do_ref[:, pbs[g]] * o_ref[:, pbs[g]], 0.0), axis=1,
                             keepdims=True) for g in range(4)], axis=0)
                lse4 = jnp.concatenate([_lane_pick(l_ref[...], 4 * h + g) for g in range(4)], axis=0)
                sink4 = jnp.concatenate([jnp.full((W, 1), sink_ref[4 * h + g], F32) for g in range(4)], axis=0)
                s = _dot_nt(q4, kband) * scale + b_ref[4 * h:4 * h + 4].reshape(4 * W, 2 * W)
                p = jnp.where(mask4, jnp.exp(s - lse4), 0.0)
                sd = jnp.exp(sink4 - lse4) * D4
                for g in range(4):
                    dsk = dsk + jnp.where(sub == 4 * h + g,
                                          -jnp.sum(sd[W * g:W * (g + 1)], axis=0, keepdims=True), 0.0)
                ds = p * (_dot_nt(do4, vm) - D4)
                db_ref[4 * h:4 * h + 4] += ds.reshape(4, W, 2 * W)
                dq = _dot(ds, km) * scale
                dq = dq + pltpu.roll(dq, HEAD_DIM, 1)
                dk_band = dk_band + _dot(ds.T, q4) * scale
                dv_band = dv_band + _dot(p.T, do4)
                dq_pairs.append(jnp.where(_half_mask(0), dq[0:W], dq[W:2 * W]))
                dq_pairs.append(jnp.where(_half_mask(0), dq[2 * W:3 * W], dq[3 * W:4 * W]))
            dq_ref[...] = jnp.concatenate(dq_pairs, axis=1)
            dsk_ref[...] += dsk
            dk_ref[...] = ck[...] + dk_band[0:W]
            dv_ref[...] = cv[...] + dv_band[0:W]
            ck[...] = dk_band[W:2 * W]
            cv[...] = dv_band[W:2 * W]

        @pl.when(n == nb)
        def _():
            dk_ref[...] = ck[...]
            dv_ref[...] = cv[...]

    cl = lambda n: jnp.minimum(n, nb - 1)
    pv = lambda n: jnp.maximum(jnp.minimum(n, nb - 1) - 1, 0)
    return pl.pallas_call(
        body, name="swa_bwd", grid=(nb + 1,),
        in_specs=[pl.BlockSpec(memory_space=pltpu.SMEM),
                  pl.BlockSpec((W, 512), lambda n: (cl(n), 0)),
                  pl.BlockSpec((W, 128), lambda n: (pv(n), C_KA // 128)),
                  pl.BlockSpec((W, 128), lambda n: (cl(n), C_KA // 128)),
                  pl.BlockSpec((W, 128), lambda n: (pv(n), C_VA // 128)),
                  pl.BlockSpec((W, 128), lambda n: (cl(n), C_VA // 128)),
                  pl.BlockSpec((8, W, 2 * W), lambda n: (0, 0, 0)),
                  pl.BlockSpec((W, 512), lambda n: (cl(n), 0)),
                  pl.BlockSpec((W, 512), lambda n: (cl(n), 0)),
                  pl.BlockSpec((W, 128), lambda n: (cl(n), 0))],
        out_specs=[pl.BlockSpec((W, 512), lambda n: (cl(n), 0)),
                   pl.BlockSpec((W, 128), lambda n: (jnp.maximum(n - 1, 0), 0)),
                   pl.BlockSpec((W, 128), lambda n: (jnp.maximum(n - 1, 0), 0)),
                   pl.BlockSpec((8, W, 2 * W), lambda n: (0, 0, 0)),
                   pl.BlockSpec((8, 128), lambda n: (0, 0))],
        out_shape=[jax.ShapeDtypeStruct((T, 512), F32), jax.ShapeDtypeStruct((T, 128), F32),
                   jax.ShapeDtypeStruct((T, 128), F32), jax.ShapeDtypeStruct((8, W, 2 * W), F32),
                   jax.ShapeDtypeStruct((8, 128), F32)],
        scratch_shapes=[pltpu.VMEM((W, 128), F32), pltpu.VMEM((W, 128), F32)],
        compiler_params=_cparams(("arbitrary",)),
    )(sinks, proj, proj, proj, proj, proj, bias, do, o, lse)


def swa_bias_table(rel_bias):
    W = WINDOW
    qi = jnp.arange(W, dtype=jnp.int32)[:, None] + W
    kj = jnp.arange(2 * W, dtype=jnp.int32)[None, :]
    dist = qi - kj
    max_exact = REL_BUCKETS // 2
    d = jnp.maximum(dist, 0)
    log_ratio = jnp.log(jnp.maximum(d, 1).astype(F32) / max_exact) / math.log(REL_MAX_DIST / max_exact)
    large = jnp.minimum(max_exact + (log_ratio * (REL_BUCKETS - max_exact)).astype(jnp.int32), REL_BUCKETS - 1)
    bucket = jnp.where(d < max_exact, d, large)
    bucket = bucket.reshape(-1)
    onehot = (bucket[None, :] == jnp.arange(REL_BUCKETS, dtype=jnp.int32)[:, None]).astype(F32)
    bias = jnp.dot(rel_bias.astype(F32).T, onehot, precision=lax.Precision.HIGHEST)
    return bias.reshape(SWA_Q_HEADS, W, 2 * W), bucket


def attn_out(oa, ob, oc, gn, wout, gpost, x):
    T = x.shape[0]
    tm = _tile(T, 512)

    def body(oa_ref, ob_ref, oc_ref, gn_ref, w_ref, gp_ref, x_ref, x2_ref, y_ref, mT_ref):
        g = gn_ref[...]
        mixed = jnp.concatenate([_rms_fwd(oa_ref[...], g[:, 0:512]), _rms_fwd(ob_ref[...], g[:, 512:768]),
                                 _rms_fwd(oc_ref[...], g[:, 768:1024])], axis=1)
        mT_ref[...] = mixed.T.astype(MXU)
        y = _dot(mixed, w_ref[...])
        y_ref[...] = y
        x2_ref[...] = x_ref[...] + _rms_fwd(y, gp_ref[...])

    row = lambda i: (i, 0)
    const = lambda i: (0, 0)
    return pl.pallas_call(
        body, name="attn_out", grid=(T // tm,),
        in_specs=[pl.BlockSpec((tm, 512), row), pl.BlockSpec((tm, 256), row), pl.BlockSpec((tm, 256), row),
                  pl.BlockSpec((1, 1024), const), pl.BlockSpec((1024, 1024), const), pl.BlockSpec((1, 1024), const),
                  pl.BlockSpec((tm, 1024), row)],
        out_specs=[pl.BlockSpec((tm, 1024), row), pl.BlockSpec((tm, 1024), row),
                   pl.BlockSpec((1024, tm), lambda i: (0, i))],
        out_shape=[jax.ShapeDtypeStruct((T, 1024), F32), jax.ShapeDtypeStruct((T, 1024), F32),
                   jax.ShapeDtypeStruct((1024, T), MXU)],
        compiler_params=_cparams(("parallel",)),
    )(oa, ob, oc, gn, wout, gpost, x)


def attn_out_bwd(dx2, y, oa, ob, oc, gn, wout, gpost):
    T = dx2.shape[0]
    tm = _tile(T, 512)

    def body(dx_ref, y_ref, oa_ref, ob_ref, oc_ref, gn_ref, w_ref, gp_ref,
             dy_ref, da_ref, db_ref, dc_ref, dgn_ref, dgp_ref):
        first = pl.program_id(0) == 0
        dy, dgp = _rms_bwd(dx_ref[...], y_ref[...], gp_ref[...])
        dy_ref[...] = dy.astype(MXU)
        _acc_out(dgp_ref, dgp, first)
        dm = _dot_nt(dy, w_ref[...])
        g = gn_ref[...]
        da, dga = _rms_bwd(dm[:, 0:512], oa_ref[...], g[:, 0:512])
        db, dgb = _rms_bwd(dm[:, 512:768], ob_ref[...], g[:, 512:768])
        dc, dgc = _rms_bwd(dm[:, 768:1024], oc_ref[...], g[:, 768:1024])
        da_ref[...] = da
        db_ref[...] = db
        dc_ref[...] = dc
        _acc_out(dgn_ref, jnp.concatenate([dga, dgb, dgc], axis=1), first)

    row = lambda i: (i, 0)
    const = lambda i: (0, 0)
    return pl.pallas_call(
        body, name="attn_out_bwd", grid=(T // tm,),
        in_specs=[pl.BlockSpec((tm, 1024), row), pl.BlockSpec((tm, 1024), row),
                  pl.BlockSpec((tm, 512), row), pl.BlockSpec((tm, 256), row), pl.BlockSpec((tm, 256), row),
                  pl.BlockSpec((1, 1024), const), pl.BlockSpec((1024, 1024), const), pl.BlockSpec((1, 1024), const)],
        out_specs=[pl.BlockSpec((tm, 1024), row), pl.BlockSpec((tm, 512), row), pl.BlockSpec((tm, 256), row),
                   pl.BlockSpec((tm, 256), row), pl.BlockSpec((1, 1024), const), pl.BlockSpec((1, 1024), const)],
        out_shape=[jax.ShapeDtypeStruct((T, 1024), MXU), jax.ShapeDtypeStruct((T, 512), F32),
                   jax.ShapeDtypeStruct((T, 256), F32), jax.ShapeDtypeStruct((T, 256), F32),
                   jax.ShapeDtypeStruct((1, 1024), F32), jax.ShapeDtypeStruct((1, 1024), F32)],
        compiler_params=_cparams(("arbitrary",)),
    )(dx2, y, oa, ob, oc, gn, wout, gpost)


FF_TILE = 256
_GELU_C = math.sqrt(2.0 / math.pi)


def _gelu(x):
    return 0.5 * x * (1.0 + jnp.tanh(_GELU_C * (x + 0.044715 * x * x * x)))


def _gelu_with_grad(x):
    x2 = x * x
    t = jnp.tanh(_GELU_C * x * (1.0 + 0.044715 * x2))
    h = 0.5 * (1.0 + t)
    return x * h, h + (0.5 * _GELU_C) * x * (1.0 - t * t) * (1.0 + (3 * 0.044715) * x2)


def _conv_taps(u, hal_ref, first):
    row = lax.broadcasted_iota(jnp.int32, (8, 1), 0)
    h6 = jnp.where(first, 0.0, hal_ref[6:7, :])
    h7 = jnp.where(first, 0.0, hal_ref[7:8, :])
    r1, r2 = pltpu.roll(u, 1, 0), pltpu.roll(u, 2, 0)
    r1 = jnp.concatenate([jnp.where(row == 0, h7, r1[0:8]), r1[8:]], axis=0)
    r2 = jnp.concatenate([jnp.where(row == 0, h6, jnp.where(row == 1, h7, r2[0:8])), r2[8:]], axis=0)
    return r1, r2


def ffn_fwd(u0, convw, convb, wdown, gpost, x2):
    T = x2.shape[0]
    tm, tn = _tile(T, 1024), FF_TILE
    nj = D_FF // tn

    def body(ug_ref, uu_ref, hg_ref, hu_ref, wg_ref, wu_ref, bg_ref, bu_ref, wd_ref, gp_ref, x_ref,
             x3_ref, y_ref, aT_ref, acc):
        i, j = pl.program_id(0), pl.program_id(1)
        first = i == 0

        def conv(u_ref, h_ref, w_ref, b_ref):
            u = u_ref[...]
            r1, r2 = _conv_taps(u, h_ref, first)
            return b_ref[...] + w_ref[0:1, :] * r2 + w_ref[1:2, :] * r1 + w_ref[2:3, :] * u

        a = _gelu(conv(ug_ref, hg_ref, wg_ref, bg_ref)) * conv(uu_ref, hu_ref, wu_ref, bu_ref)
        aT_ref[...] = a.T.astype(MXU)
        _acc_out(acc, _dot(a, wd_ref[...]), j == 0)

        @pl.when(j == nj - 1)
        def _():
            y = acc[...]
            y_ref[...] = y
            x3_ref[...] = x_ref[...] + _rms_fwd(y, gp_ref[...])

    halo = lambda off: (lambda i, j: (jnp.maximum(i * (tm // 8) - 1, 0), off + j))
    return pl.pallas_call(
        body, name="ffn_fwd", grid=(T // tm, nj),
        in_specs=[pl.BlockSpec((tm, tn), lambda i, j: (i, j)), pl.BlockSpec((tm, tn), lambda i, j: (i, nj + j)),
                  pl.BlockSpec((8, tn), halo(0)), pl.BlockSpec((8, tn), halo(nj)),
                  pl.BlockSpec((3, tn), lambda i, j: (0, j)), pl.BlockSpec((3, tn), lambda i, j: (0, nj + j)),
                  pl.BlockSpec((1, tn), lambda i, j: (0, j)), pl.BlockSpec((1, tn), lambda i, j: (0, nj + j)),
                  pl.BlockSpec((tn, 1024), lambda i, j: (j, 0)),
                  pl.BlockSpec((1, 1024), lambda i, j: (0, 0)),
                  pl.BlockSpec((tm, 1024), lambda i, j: (i, 0))],
        out_specs=[pl.BlockSpec((tm, 1024), lambda i, j: (i, 0)), pl.BlockSpec((tm, 1024), lambda i, j: (i, 0)),
                   pl.BlockSpec((tn, tm), lambda i, j: (j, i))],
        out_shape=[jax.ShapeDtypeStruct((T, 1024), F32), jax.ShapeDtypeStruct((T, 1024), F32),
                   jax.ShapeDtypeStruct((D_FF, T), MXU)],
        scratch_shapes=[pltpu.VMEM((tm, 1024), F32)],
        compiler_params=_cparams(("parallel", "arbitrary")),
    )(u0, u0, u0, u0, convw, convw, convb, convb, wdown, gpost, x2)


def ffn_bwd(dx3, y, u0, convw, convb, wdown, gpost):
    T = dx3.shape[0]
    tm, tn = _tile(T, 1024), FF_TILE
    nj = D_FF // tn
    ni = T // tm

    def body(dx_ref, y_ref, ug_ref, uu_ref, hg_ref, hu_ref, wg_ref, wu_ref, bg_ref, bu_ref, wd_ref, gp_ref,
             dy_ref, dug_ref, duu_ref, dcg_ref, dcu_ref, dgp_ref, dy_sc, cg, cu, ag, au):
        s, j = pl.program_id(0), pl.program_id(1)
        i = ni - 1 - s
        first_tok = i == 0
        row = lax.broadcasted_iota(jnp.int32, (tm, 1), 0)
        sub = lax.broadcasted_iota(jnp.int32, (8, 1), 0)

        @pl.when(j == 0)
        def _():
            dy, dgp = _rms_bwd(dx_ref[...], y_ref[...], gp_ref[...])
            dy_sc[...] = dy.astype(MXU)
            dy_ref[...] = dy.astype(MXU)
            _acc_out(dgp_ref, dgp, s == 0)

        @pl.when(s == 0)
        def _():
            cg[j] = jnp.zeros((8, tn), F32)
            cu[j] = jnp.zeros((8, tn), F32)
            ag[j] = jnp.zeros((8, tn), F32)
            au[j] = jnp.zeros((8, tn), F32)

        da = _dot_nt(dy_sc[...], wd_ref[...])

        def conv(u_ref, h_ref, w_ref, b_ref):
            u = u_ref[...]
            r1, r2 = _conv_taps(u, h_ref, first_tok)
            return b_ref[...] + w_ref[0:1, :] * r2 + w_ref[1:2, :] * r1 + w_ref[2:3, :] * u, u, r1, r2

        gate, ugv, g1, g2 = conv(ug_ref, hg_ref, wg_ref, bg_ref)
        up, uuv, u1, u2 = conv(uu_ref, hu_ref, wu_ref, bu_ref)
        gl, dgl = _gelu_with_grad(gate)
        dup = da * gl
        dgate = da * up * dgl

        def conv_bwd(du, u, r1, r2, w_ref, c_ref, a_ref, du_ref):
            nxt = c_ref[j]
            n0, n1 = nxt[0:1, :], nxt[1:2, :]
            f1, f2 = pltpu.roll(du, tm - 1, 0), pltpu.roll(du, tm - 2, 0)
            f1 = jnp.concatenate([f1[:tm - 8], jnp.where(sub == 7, n0, f1[tm - 8:])], axis=0)
            f2 = jnp.concatenate([f2[:tm - 8], jnp.where(sub == 7, n1, jnp.where(sub == 6, n0, f2[tm - 8:]))], axis=0)
            du_ref[...] = (w_ref[2:3, :] * du + w_ref[1:2, :] * f1 + w_ref[0:1, :] * f2).astype(MXU)
            c_ref[j] = du[0:8, :]
            red = lambda v: jnp.sum(v, axis=0, keepdims=True)
            part = jnp.where(sub == 0, red(du * r2), jnp.where(sub == 1, red(du * r1), jnp.where(
                sub == 2, red(du * u), jnp.where(sub == 3, red(du), 0.0))))
            a_ref[j] = a_ref[j] + part
            return a_ref[j]

        dcg_ref[0] = conv_bwd(dgate, ugv, g1, g2, wg_ref, cg, ag, dug_ref)
        dcu_ref[0] = conv_bwd(dup, uuv, u1, u2, wu_ref, cu, au, duu_ref)

    rev = lambda s: ni - 1 - s
    halo = lambda off: (lambda s, j: (jnp.maximum(rev(s) * (tm // 8) - 1, 0), off + j))
    return pl.pallas_call(
        body, name="ffn_bwd", grid=(ni, nj),
        in_specs=[pl.BlockSpec((tm, 1024), lambda s, j: (rev(s), 0)), pl.BlockSpec((tm, 1024), lambda s, j: (rev(s), 0)),
                  pl.BlockSpec((tm, tn), lambda s, j: (rev(s), j)), pl.BlockSpec((tm, tn), lambda s, j: (rev(s), nj + j)),
                  pl.BlockSpec((8, tn), halo(0)), pl.BlockSpec((8, tn), halo(nj)),
                  pl.BlockSpec((3, tn), lambda s, j: (0, j)), pl.BlockSpec((3, tn), lambda s, j: (0, nj + j)),
                  pl.BlockSpec((1, tn), lambda s, j: (0, j)), pl.BlockSpec((1, tn), lambda s, j: (0, nj + j)),
                  pl.BlockSpec((tn, 1024), lambda s, j: (j, 0)),
                  pl.BlockSpec((1, 1024), lambda s, j: (0, 0))],
        out_specs=[pl.BlockSpec((tm, 1024), lambda s, j: (rev(s), 0)),
                   pl.BlockSpec((tm, tn), lambda s, j: (rev(s), j)), pl.BlockSpec((tm, tn), lambda s, j: (rev(s), j)),
                   pl.BlockSpec((1, 8, tn), lambda s, j: (s, 0, j)), pl.BlockSpec((1, 8, tn), lambda s, j: (s, 0, j)),
                   pl.BlockSpec((1, 1024), lambda s, j: (0, 0))],
        out_shape=[jax.ShapeDtypeStruct((T, 1024), MXU), jax.ShapeDtypeStruct((T, D_FF), MXU),
                   jax.ShapeDtypeStruct((T, D_FF), MXU),
                   jax.ShapeDtypeStruct((ni, 8, D_FF), F32), jax.ShapeDtypeStruct((ni, 8, D_FF), F32),
                   jax.ShapeDtypeStruct((1, 1024), F32)],
        scratch_shapes=[pltpu.VMEM((tm, 1024), MXU)] + [pltpu.VMEM((nj, 8, tn), F32)] * 4,
        compiler_params=_cparams(("arbitrary", "arbitrary")),
    )(dx3, y, u0, u0, u0, u0, convw, convw, convb, convb, wdown, gpost)


ELEMS_PER_BLOCK = 512 * 1024


def _row_block(R, C):
    if R * C <= ELEMS_PER_BLOCK or R % 8:
        return R
    best = 8
    for t in range(8, R + 1, 8):
        if R % t == 0 and t * C <= ELEMS_PER_BLOCK:
            best = t
    return best


def adamw(w, g, m, v, name):
    L, R, C = w.shape
    partials = isinstance(g, (list, tuple))
    tr = _row_block(R, 2 * C)
    c1 = 1.0 - ADAM_B1 ** ADAM_STEP
    c2 = 1.0 - ADAM_B2 ** ADAM_STEP

    def body(w_ref, *rest):
        g_refs, (m_ref, v_ref, g_out, d_ref, nm_ref, nv_ref) = rest[:-6], rest[-6:]

        def step(gv):
            g_out[0] = gv
            nm = ADAM_B1 * m_ref[0] + (1.0 - ADAM_B1) * gv
            nv = ADAM_B2 * v_ref[0] + (1.0 - ADAM_B2) * (gv * gv)
            nm_ref[0] = nm
            nv_ref[0] = nv
            d_ref[0] = -ADAM_LR * ((nm / c1) / (jnp.sqrt(nv / c2) + ADAM_EPS) + ADAM_WD * w_ref[0])

        if not partials:
            step(g_refs[0][0])
            return
        for k in range(L):
            @pl.when(pl.program_id(0) == k)
            def _(k=k):
                gv = g_refs[k][0].astype(F32)
                for d in range(1, N_DEV):
                    gv = gv + g_refs[k][d].astype(F32)
                step(gv)

    spec = pl.BlockSpec((1, tr, C), lambda l, i: (l, i, 0))
    if partials:
        gspecs = [pl.BlockSpec((N_DEV, tr, C), lambda l, i, k=k: (0, jnp.where(l == k, i, 0), 0)) for k in range(L)]
        gs = list(g)
    else:
        gspecs, gs = [spec], [g]
    return pl.pallas_call(
        body, name=name, grid=(L, R // tr), in_specs=[spec] + gspecs + [spec, spec], out_specs=[spec] * 4,
        out_shape=[jax.ShapeDtypeStruct((L, R, C), F32)] * 4,
        compiler_params=_cparams(("arbitrary", "arbitrary")),
    )(w, *gs, m, v)


def sum_devices(buf, name):
    _, R, C = buf.shape
    tr = _row_block(R, C * 4)

    def body(b_ref, o_ref):
        acc = b_ref[0].astype(F32)
        for d in range(1, N_DEV):
            acc = acc + b_ref[d].astype(F32)
        o_ref[...] = acc

    return pl.pallas_call(
        body, name=name, grid=(R // tr,),
        in_specs=[pl.BlockSpec((N_DEV, tr, C), lambda i: (0, i, 0))],
        out_specs=pl.BlockSpec((tr, C), lambda i: (i, 0)),
        out_shape=jax.ShapeDtypeStruct((R, C), F32),
        compiler_params=_cparams(("parallel",)),
    )(buf)


def _exchange_copies(src_refs, out_refs, send_sems, recv_sems, gather):
    x, y, c = lax.axis_index("x"), lax.axis_index("y"), lax.axis_index("c")
    me = 4 * x + 2 * y + c
    flip = lambda a, bit: 1 - a if bit else a
    part = lambda ref, d: ref if gather else ref.at[d]
    copies = []
    for k in range(1, N_DEV):
        px, py, pc = flip(x, (k >> 2) & 1), flip(y, (k >> 1) & 1), flip(c, k & 1)
        peer = 4 * px + 2 * py + pc
        for t in range(len(src_refs)):
            sem = t * (N_DEV - 1) + k - 1
            mk = lambda s, d: pltpu.make_async_remote_copy(
                src_ref=s, dst_ref=d, send_sem=send_sems.at[sem], recv_sem=recv_sems.at[sem],
                device_id=(px, py, pc), device_id_type=pl.DeviceIdType.MESH)
            copies.append((mk(part(src_refs[t], peer), out_refs[t].at[me]),
                           mk(part(src_refs[t], me), out_refs[t].at[peer])))
    return me, copies


def exchange(srcs, name, gather):
    n = len(srcs)
    shapes = [(N_DEV,) + s.shape if gather else s.shape for s in srcs]

    def body(*refs):
        src_refs, out_refs = refs[:n], refs[n:2 * n]
        send_sems, recv_sems, local_sems = refs[2 * n:]
        me, copies = _exchange_copies(src_refs, out_refs, send_sems, recv_sems, gather)
        for outgoing, _ in copies:
            outgoing.start()
        mine = [pltpu.make_async_copy(src_refs[t] if gather else src_refs[t].at[me], out_refs[t].at[me],
                                      local_sems.at[t]) for t in range(n)]
        for cp in mine:
            cp.start()
        for _, incoming in copies:
            incoming.wait_recv()
        for outgoing, _ in copies:
            outgoing.wait_send()
        for cp in mine:
            cp.wait()

    return pl.pallas_call(
        body, name=name,
        in_specs=[pl.BlockSpec(memory_space=pl.ANY)] * n, out_specs=[pl.BlockSpec(memory_space=pl.ANY)] * n,
        out_shape=[jax.ShapeDtypeStruct(shp, s.dtype) for shp, s in zip(shapes, srcs)],
        scratch_shapes=[pltpu.SemaphoreType.DMA((n * (N_DEV - 1),)), pltpu.SemaphoreType.DMA((n * (N_DEV - 1),)),
                        pltpu.SemaphoreType.DMA((n,))],
    )(*srcs)


def hosted_exchange(body, n_in, n_out, n_scratch, grid, srcs, gather):
    n = len(srcs)
    shapes = [(N_DEV,) + s.shape if gather else s.shape for s in srcs]

    def wrapped(*refs):
        ins, xin = refs[:n_in], refs[n_in:n_in + n]
        outs = refs[n_in + n:n_in + n + n_out]
        xout = refs[n_in + n + n_out:n_in + 2 * n + n_out]
        rest = refs[n_in + 2 * n + n_out:]
        scratch, (send_sems, recv_sems, local_sems) = rest[:n_scratch], rest[n_scratch:]
        ids = [pl.program_id(a) for a in range(len(grid))]
        first = functools.reduce(jnp.logical_and, [i == 0 for i in ids])
        last = functools.reduce(jnp.logical_and, [i == g - 1 for i, g in zip(ids, grid)])
        me, copies = _exchange_copies(xin, xout, send_sems, recv_sems, gather)
        mine = [pltpu.make_async_copy(xin[t] if gather else xin[t].at[me], xout[t].at[me], local_sems.at[t])
                for t in range(n)]

        @pl.when(first)
        def _():
            for outgoing, _ in copies:
                outgoing.start()
            for cp in mine:
                cp.start()

        body(*ins, *outs, *scratch)

        @pl.when(last)
        def _():
            for _, incoming in copies:
                incoming.wait_recv()
            for outgoing, _ in copies:
                outgoing.wait_send()
            for cp in mine:
                cp.wait()

    any_spec = pl.BlockSpec(memory_space=pl.ANY)
    return wrapped, (list(srcs), [any_spec] * n, [any_spec] * n,
                     [jax.ShapeDtypeStruct(shp, s.dtype) for shp, s in zip(shapes, srcs)],
                     [pltpu.SemaphoreType.DMA((n * (N_DEV - 1),)), pltpu.SemaphoreType.DMA((n * (N_DEV - 1),)),
                      pltpu.SemaphoreType.DMA((n,))])


def _pack(parts, cols, row_align, dtype):
    flat = jnp.concatenate([p.astype(dtype) for p in parts], axis=-1)
    n = flat.shape[-1]
    block = cols * row_align
    total = -(-n // block) * block
    flat = jnp.pad(flat, [(0, 0)] * (flat.ndim - 1) + [(0, total - n)])
    return flat.reshape(flat.shape[:-1] + (total // cols, cols))


def _unpack(buf, shapes):
    lead = buf.shape[:-2]
    flat = buf.reshape(lead + (-1,))
    out, off = [], 0
    for s in shapes:
        n = int(np.prod(s))
        out.append(flat[..., off:off + n].reshape(lead + tuple(s)))
        off += n
    return out


SHARD_SHAPES = [(128, IN_COLS), (256, 48), (128, 64), (128, 1024), (1024, 704), (352, 1024)]
SHARDED = ["w_in", "w_uq", "w_ukv", "w_out", "w_up", "w_down"]
ATTN_SENT = ["w_in_p", "w_uq", "w_ukv", "w_out"]
FFN_SIDE = ["w_up", "conv_w", "w_down"]


def _full_from_shards(name, s):
    if name in ("w_in", "w_in_p", "w_out", "w_down"):
        return s.reshape((-1, s.shape[-1]))
    return s.transpose(1, 0, 2).reshape((s.shape[1], -1))


def _shards_from_full(name, f):
    if name in ("w_in", "w_in_p", "w_out", "w_down"):
        return f.reshape((N_DEV, -1, f.shape[-1]))
    return f.reshape((f.shape[0], N_DEV, -1)).transpose(1, 0, 2)


def _perm_w_in(w):
    z = lambda n: jnp.zeros(w.shape[:-1] + (n,), w.dtype)
    return jnp.concatenate([w[..., :1536], w[..., 1540:1924], w[..., 1536:1540], z(60), w[..., 1924:1956], z(32)],
                           axis=-1)


def _unperm_w_in(d):
    return jnp.concatenate([d[..., :1536], d[..., 1920:1924], d[..., 1536:1920], d[..., 1984:2016]], axis=-1)


def _perm_w_uq(w):
    return jnp.pad(w.reshape(256, 4, MLA_QK_DIM), ((0, 0), (0, 0), (0, 128 - MLA_QK_DIM))).reshape(256, 512)


def _unperm_w_uq(d):
    return d.reshape(256, 4, 128)[:, :, :MLA_QK_DIM].reshape(256, 4 * MLA_QK_DIM)


def _perm_w_ukv(w):
    w4 = w.reshape(128, 4, 128)
    k = jnp.pad(w4[:, :, :64], ((0, 0), (0, 0), (0, 64))).reshape(128, 512)
    return jnp.concatenate([k, w4[:, :, 64:].reshape(128, 256)], axis=1)


def _unperm_w_ukv(d):
    dk = d[:, :512].reshape(128, 4, 128)[:, :, :64]
    dv = d[:, 512:].reshape(128, 4, 64)
    return jnp.concatenate([dk, dv], axis=-1).reshape(128, 512)


def _row(v, width=None):
    v = v.reshape(1, -1).astype(F32)
    if width is not None and v.shape[1] < width:
        v = jnp.pad(v, ((0, 0), (0, width - v.shape[1])))
    return v


def _layer_fwd(x, P, shared, send=None, ffn_from=None):
    cosr, sinr, bias = shared
    ex = (lambda part: None) if send is None else (lambda part: (send[part], True))
    proj, hT, projb = norm_matmul(x, P["g_pre"], P["w_in_p"], "in_proj", lo_tiles=C_CQ // 512)
    qm, km, vm, cqT, ckvT = mla_prep(proj, P["gq"], P["gkv"], P["w_uq_p"], P["w_ukv_p"], cosr, sinr)
    fcol, frow, frep = fox_gate(proj, P["fbias"])
    oa, lse_a = swa_fwd(proj, bias, P["sinks"])
    (ob, lrb), got_a = flash_fwd(projb, projb, projb, frep, frow, qblk=C_QF // 128, kblk=C_KF // 128,
                                 vblk=C_VF // 128, nq=1, scale=HEAD_DIM ** -0.5, name="fox_fwd", exch=ex(0))
    (oc, lrc), got_b = flash_fwd(qm, km, vm, None, None, qblk=0, kblk=0, vblk=0, nq=2,
                                 scale=MLA_QK_DIM ** -0.5, name="mla_fwd", exch=ex(1))
    x2, y1, mT = attn_out(oa, ob, oc, P["gn"], P["w_out"], P["g_apost"], x)
    if ffn_from is not None:
        P = dict(P, **ffn_from(got_a, got_b))
    u0, h2T = norm_matmul(x2, P["g_fpre"], P["w_up"], "up_proj", tn_pref=1536)
    x3, y2, aT = ffn_fwd(u0, P["conv_w"], P["conv_b"], P["w_down"], P["g_fpost"], x2)
    S = dict(x=x, proj=proj, projb=projb, hT=hT, qm=qm, km=km, vm=vm, cqT=cqT, ckvT=ckvT, fcol=fcol, frow=frow,
             oa=oa, lse_a=lse_a, ob=ob, lrb=lrb, oc=oc, lrc=lrc,
             x2=x2, y1=y1, mT=mT, u0=u0, h2T=h2T, y2=y2, aT=aT)
    return x3, S, P, (got_a, got_b)


def _layer_bwd(dx3, P, S, shared, send_attn=None):
    cosr, sinr, bias = shared
    proj = S["proj"]
    G = {}
    dy2, dug, duu, dcg, dcu, G["ffn_post_norm"] = ffn_bwd(dx3, S["y2"], S["u0"], P["conv_w"], P["conv_b"],
                                                         P["w_down"], P["g_fpost"])
    du0 = jnp.concatenate([dug, duu], axis=1)
    dconv = jnp.concatenate([dcg[-1], dcu[-1]], axis=1)
    G["conv_w"], G["conv_b"] = dconv[0:3], dconv[3]
    G["w_down"] = matmul_nn(S["aT"], dy2, "dw_down", MXU)
    G["w_up"] = matmul_nn(S["h2T"], du0, "dw_up", MXU)
    dx2, G["ffn_pre_norm"] = matmul_nt_normbwd(du0, P["w_up"], S["x2"], P["g_fpre"], dx3, "up_bwd")
    dy1, doa, dob, doc, G["group_norm"], G["attn_post_norm"] = attn_out_bwd(
        dx2, S["y1"], S["oa"], S["ob"], S["oc"], P["gn"], P["w_out"], P["g_apost"])
    G["w_out"] = matmul_nn(S["mT"], dy1, "dw_out", MXU)
    dqa, dka, dva, dbias, dsk = swa_bwd(proj, bias, P["sinks"], doa, S["oa"], S["lse_a"])
    G["swa_sinks"] = dsk[:, 0]
    pb = S["projb"]
    if send_attn is None:
        ex = lambda part: None
    else:
        parts = ([_shards_from_full("w_down", G["w_down"])] + send_attn, [_shards_from_full("w_up", G["w_up"])])
        ex = lambda part: (parts[part], False)
    (dqf, dkf, dvf, dFk, dFq), got_a = flash_bwd(
        pb, pb, pb, dob, S["ob"], S["lrb"], S["fcol"], S["frow"], name="fox_bwd", qblk=C_QF // 128,
        kblk=C_KF // 128, vblk=C_VF // 128, nq=1, scale=HEAD_DIM ** -0.5, exch=ex(0))
    dmisc_f, dfb = fox_gate_bwd(dFq, dFk, proj, P["fbias"])
    G["forget_bias"] = dfb[0, 0:4]
    (dqm_, dkm_, dvm_), got_b = flash_bwd(
        S["qm"], S["km"], S["vm"], doc, S["oc"], S["lrc"], None, None, name="mla_bwd",
        qblk=0, kblk=0, vblk=0, nq=2, scale=MLA_QK_DIM ** -0.5, exch=ex(1))
    dqm, dkv, dcq, dckv, dmisc_r, G["q_latent_norm"], G["kv_latent_norm"] = mla_prep_bwd(
        dqm_, dkm_, dvm_, proj, P["gq"], P["gkv"], P["w_uq_p"], P["w_ukv_p"], cosr, sinr)
    G["w_uq"] = _unperm_w_uq(matmul_nn(S["cqT"], dqm, "dw_uq", MXU))
    G["w_ukv"] = _unperm_w_ukv(matmul_nn(S["ckvT"], dkv, "dw_ukv", MXU))
    dproj = jnp.concatenate([dqa, dka, dva, dqf, dkf, dvf, dcq, dckv, dmisc_f + dmisc_r], axis=1).astype(MXU)
    G["w_in_p"] = matmul_nn(S["hT"], dproj, "dw_in", MXU)
    G["w_in"] = _unperm_w_in(G["w_in_p"])
    dx, G["attn_pre_norm"] = matmul_nt_normbwd(dproj, P["w_in_p"], S["x"], P["g_pre"], dx2, "in_bwd")
    return dx, G, dbias, (got_a, got_b)


def _layer_params(l, full, small):
    return dict(
        g_pre=_row(small["attn_pre_norm"][l]),
        w_in_p=full["w_in_p"] if "w_in_p" in full else _perm_w_in(full["w_in"]),
        gq=_row(small["q_latent_norm"][l]), gkv=_row(small["kv_latent_norm"][l]),
        w_uq_p=_perm_w_uq(full["w_uq"]), w_ukv_p=_perm_w_ukv(full["w_ukv"]),
        fbias=_row(small["forget_bias"][l], 128), sinks=small["swa_sinks"][l].astype(F32),
        gn=_row(small["group_norm"][l]), w_out=full["w_out"], g_apost=_row(small["attn_post_norm"][l]),
        g_fpre=_row(small["ffn_pre_norm"][l]), conv_b=_row(small["conv_b"][l]),
        g_fpost=_row(small["ffn_post_norm"][l]),
        **{n: full[n] for n in FFN_SIDE if n in full})


def _rel_bias_grad(dbias, bucket):
    flat = dbias.reshape(SWA_Q_HEADS, -1)
    hi = flat.astype(MXU)
    lo = (flat - hi.astype(F32)).astype(MXU)
    onehot = (bucket[:, None] == jnp.arange(128, dtype=jnp.int32)[None, :]).astype(MXU)
    r = matmul_nn(jnp.concatenate([hi, lo], axis=0), onehot, "rel_bias_grad")
    return (r[0:8] + r[8:16])[:, :REL_BUCKETS].T


def local_step(x, tgt, fulls, small, comm=None):
    T = x.shape[0]
    cosr, sinr = rope_tables(T)
    bias, bucket = swa_bias_table(small["rel_bias"])
    shared = (cosr, sinr, bias)
    Ps, Ss = [], []
    h, full = x, fulls[0]
    for l in range(DEPTH):
        P = _layer_params(l, full, small)
        if comm:
            h, S, P, got = _layer_fwd(h, P, shared, comm["weight_parts"](l), comm["ffn_from"])
            full = comm["attn_from"](got) if l + 1 < DEPTH else None
        else:
            h, S, P, _ = _layer_fwd(h, P, shared)
            full = fulls[l + 1] if l + 1 < DEPTH else None
        Ps.append(P)
        Ss.append(S)
    dh, sq = loss_kernel(h, tgt)
    grads = [None] * DEPTH
    dbias_sum = None
    pending = [] if comm else None
    for l in reversed(range(DEPTH)):
        dh, grads[l], dbias, got = _layer_bwd(dh, Ps[l], Ss[l], shared, pending)
        dbias_sum = dbias if dbias_sum is None else dbias_sum + dbias
        if comm:
            comm["landed"](l, ["w_down"], got[0][:1])
            comm["landed"](l, ["w_up"], got[1])
            if pending:
                comm["landed"](l + 1, ATTN_SENT, got[0][1:])
            pending = [_shards_from_full(n, grads[l][n]) for n in ATTN_SENT]
    return sq, dh, grads, _rel_bias_grad(dbias_sum, bucket), pending


WEIGHTS = ['attn_pre_norm', 'w_in', 'forget_bias', 'swa_sinks', 'rel_bias', 'q_latent_norm', 'w_uq',
           'kv_latent_norm', 'w_ukv', 'group_norm', 'w_out', 'attn_post_norm', 'ffn_pre_norm', 'w_up', 'conv_w',
           'conv_b', 'w_down', 'ffn_post_norm']
SMALL_PER_LAYER = ['attn_pre_norm', 'forget_bias', 'swa_sinks', 'q_latent_norm', 'kv_latent_norm', 'group_norm',
                   'attn_post_norm', 'ffn_pre_norm', 'conv_b', 'ffn_post_norm', 'conv_w']


def kernel(x, attn_pre_norm, w_in, forget_bias, swa_sinks, rel_bias, q_latent_norm, w_uq, kv_latent_norm, w_ukv, group_norm, w_out, attn_post_norm, ffn_pre_norm, w_up, conv_w, conv_b, w_down, ffn_post_norm, loss_target, m_attn_pre_norm, m_w_in, m_forget_bias, m_swa_sinks, m_rel_bias, m_q_latent_norm, m_w_uq, m_kv_latent_norm, m_w_ukv, m_group_norm, m_w_out, m_attn_post_norm, m_ffn_pre_norm, m_w_up, m_conv_w, m_conv_b, m_w_down, m_ffn_post_norm, v_attn_pre_norm, v_w_in, v_forget_bias, v_swa_sinks, v_rel_bias, v_q_latent_norm, v_w_uq, v_kv_latent_norm, v_w_ukv, v_group_norm, v_w_out, v_attn_post_norm, v_ffn_pre_norm, v_w_up, v_conv_w, v_conv_b, v_w_down, v_ffn_post_norm):
    W = dict(attn_pre_norm=attn_pre_norm, w_in=w_in, forget_bias=forget_bias, swa_sinks=swa_sinks, rel_bias=rel_bias,
             q_latent_norm=q_latent_norm, w_uq=w_uq, kv_latent_norm=kv_latent_norm, w_ukv=w_ukv,
             group_norm=group_norm, w_out=w_out, attn_post_norm=attn_post_norm, ffn_pre_norm=ffn_pre_norm,
             w_up=w_up, conv_w=conv_w, conv_b=conv_b, w_down=w_down, ffn_post_norm=ffn_post_norm)
    M = dict(attn_pre_norm=m_attn_pre_norm, w_in=m_w_in, forget_bias=m_forget_bias, swa_sinks=m_swa_sinks,
             rel_bias=m_rel_bias, q_latent_norm=m_q_latent_norm, w_uq=m_w_uq, kv_latent_norm=m_kv_latent_norm,
             w_ukv=m_w_ukv, group_norm=m_group_norm, w_out=m_w_out, attn_post_norm=m_attn_post_norm,
             ffn_pre_norm=m_ffn_pre_norm, w_up=m_w_up, conv_w=m_conv_w, conv_b=m_conv_b, w_down=m_w_down,
             ffn_post_norm=m_ffn_post_norm)
    V = dict(attn_pre_norm=v_attn_pre_norm, w_in=v_w_in, forget_bias=v_forget_bias, swa_sinks=v_swa_sinks,
             rel_bias=v_rel_bias, q_latent_norm=v_q_latent_norm, w_uq=v_w_uq, kv_latent_norm=v_kv_latent_norm,
             w_ukv=v_w_ukv, group_norm=v_group_norm, w_out=v_w_out, attn_post_norm=v_attn_post_norm,
             ffn_pre_norm=v_ffn_pre_norm, w_up=v_w_up, conv_w=v_conv_w, conv_b=v_conv_b, w_down=v_w_down,
             ffn_post_norm=v_ffn_post_norm)
    me = 4 * lax.axis_index("x") + 2 * lax.axis_index("y") + lax.axis_index("c")

    def attn_shards(l):
        return [_perm_w_in(w_in[l].astype(MXU))] + [W[n][l].astype(MXU) for n in ATTN_SENT[1:]]

    def weight_parts(l):
        return ([W["w_down"][l].astype(MXU)] + (attn_shards(l + 1) if l + 1 < DEPTH else []),
                [W["w_up"][l].astype(MXU), conv_w[l]])

    def ffn_from(got_a, got_b):
        return dict(w_down=_full_from_shards("w_down", got_a[0]), w_up=_full_from_shards("w_up", got_b[0]),
                    conv_w=got_b[1].transpose(1, 0, 2).reshape(3, 2 * D_FF))

    def attn_from(got):
        return {n: _full_from_shards(n, s) for n, s in zip(ATTN_SENT, got[0][1:])}

    landed = [{} for _ in range(DEPTH)]

    def on_landed(l, names, arrays):
        landed[l].update(zip(names, arrays))

    comm = dict(weight_parts=weight_parts, ffn_from=ffn_from, attn_from=attn_from, landed=on_landed)
    full0 = dict(zip(ATTN_SENT, map(_full_from_shards, ATTN_SENT, exchange(attn_shards(0), "gather_weights", True))))
    sq, dx, grads, drel, last = local_step(x[0], loss_target[0], [full0], W, comm)
    on_landed(0, ATTN_SENT, exchange(last, "scatter_grads", False))
    for l in range(DEPTH):
        landed[l]["w_in"] = _unperm_w_in(landed[l]["w_in_p"])
    G = {}

    parts, shapes = [], []
    for l in range(DEPTH):
        for n in SMALL_PER_LAYER:
            parts.append(grads[l][n].astype(F32).reshape(-1))
            shapes.append(grads[l][n].shape)
    parts += [drel.reshape(-1), jnp.sum(sq).reshape(1) * (0.5 / D_MODEL)]
    shapes += [drel.shape, (1,)]
    red = _unpack(sum_devices(exchange([_pack(parts, 128, 8, F32)], "gather_small", True)[0], "sum_small"), shapes)
    k = 0
    per = {n: [] for n in SMALL_PER_LAYER}
    for l in range(DEPTH):
        for n in SMALL_PER_LAYER:
            per[n].append(red[k])
            k += 1
    for n in SMALL_PER_LAYER:
        G[n] = jnp.stack(per[n]).reshape((DEPTH, 3, 2 * D_FF) if n == "conv_w" else W[n].shape)
    G["rel_bias"] = red[k]
    loss = red[k + 1][0]
    G["conv_w"] = lax.dynamic_slice_in_dim(G["conv_w"], me * 704, 704, axis=2)

    delta, new_m, new_v = {}, {}, {}
    for n in WEIGHTS:
        shp = W[n].shape
        v3 = lambda a: a.reshape(shp if len(shp) == 3 else (1,) + shp)
        g = [landed[l][n] for l in range(DEPTH)] if n in SHARDED else v3(G[n])
        g, d, nm, nv = adamw(v3(W[n]), g, v3(M[n]), v3(V[n]), "adamw_" + n)
        G[n], delta[n], new_m[n], new_v[n] = g.reshape(shp), d.reshape(shp), nm.reshape(shp), nv.reshape(shp)
    return (loss, dx[None], *[G[n] for n in WEIGHTS], *[delta[n] for n in WEIGHTS],
            *[new_m[n] for n in WEIGHTS], *[new_v[n] for n in WEIGHTS])
```

```python
import functools
import math

import numpy as np
import jax
import jax.numpy as jnp
from jax import lax
from jax.experimental import pallas as pl
from jax.experimental.pallas import tpu as pltpu

F32 = jnp.float32
MXU = jnp.bfloat16

N_DEV = 8
DEPTH = 4
D_MODEL = 1024
HEAD_DIM = 64
WINDOW = 128
SWA_Q_HEADS = 8
REL_BUCKETS = 32
REL_MAX_DIST = 128
MLA_QK_DIM = 96
ROPE_DIM = 32
ROPE_THETA = 10000.0
D_FF = 2816
EPS = 1e-6
NEG = -1e30
IN_COLS = 1956
IN_COLS_P = 2048
C_QA, C_KA, C_VA = 0, 512, 640
C_QF, C_KF, C_VF = 768, 1024, 1280
C_CQ, C_CKV, C_MISC = 1536, 1792, 1920
ROPE_LANE0 = 64
ADAM_LR, ADAM_B1, ADAM_B2, ADAM_EPS, ADAM_WD, ADAM_STEP = 0.001, 0.9, 0.999, 1e-08, 0.01, 10

VMEM_LIMIT = 56 * 1024 * 1024
PACK_COLS = 1024
PACK_ROW_ALIGN = 16


def _cparams(sem=None):
    return pltpu.CompilerParams(dimension_semantics=sem, vmem_limit_bytes=VMEM_LIMIT)


def _tile(n, pref):
    if n <= pref:
        return n
    t = pref - pref % 128
    while t >= 128:
        if n % t == 0:
            return t
        t -= 128
    return n


def _dot(a, b):
    return jnp.dot(a.astype(MXU), b.astype(MXU), preferred_element_type=F32)


def _dot_nt(a, b):
    return lax.dot_general(a.astype(MXU), b.astype(MXU), (((1,), (1,)), ((), ())),
                           preferred_element_type=F32)


def _rms_fwd(x, g):
    return x * lax.rsqrt(jnp.mean(x * x, axis=-1, keepdims=True) + EPS) * g


def _rms_bwd(dy, x, g, n=None):
    r = lax.rsqrt(jnp.mean(x * x, axis=-1, keepdims=True) + EPS)
    xh = x * r
    dg = jnp.sum(dy * xh, axis=0, keepdims=True)
    dxh = dy * g
    dx = r * (dxh - xh * jnp.mean(dxh * xh, axis=-1, keepdims=True))
    return dx, dg


def _acc_out(ref, val, first):
    @pl.when(first)
    def _():
        ref[...] = val

    @pl.when(jnp.logical_not(first))
    def _():
        ref[...] += val


def norm_matmul(x, g, w, name, lo_tiles=0, tn_pref=512, w_transposed=False, h_transposed=True):
    T, K = x.shape
    N = w.shape[0] if w_transposed else w.shape[1]
    tm, tn = _tile(T, 1024), _tile(N, tn_pref)

    def body(x_ref, g_ref, w_ref, o_ref, hT_ref, *rest):
        h_sc = rest[-1]
        j = pl.program_id(1)

        @pl.when(j == 0)
        def _():
            h = _rms_fwd(x_ref[...], g_ref[...])
            h_sc[...] = h.astype(MXU)
            hT_ref[...] = (h.T if h_transposed else h).astype(MXU)

        r = (_dot_nt if w_transposed else _dot)(h_sc[...], w_ref[...])
        o_ref[...] = r
        if lo_tiles:
            @pl.when(j < lo_tiles)
            def _():
                rest[0][...] = r.astype(MXU)

    h_spec = pl.BlockSpec((K, tm), lambda i, j: (0, i)) if h_transposed else pl.BlockSpec((tm, K), lambda i, j: (i, 0))
    out_specs = [pl.BlockSpec((tm, tn), lambda i, j: (i, j)), h_spec]
    out_shape = [jax.ShapeDtypeStruct((T, N), F32), jax.ShapeDtypeStruct((K, T) if h_transposed else (T, K), MXU)]
    if lo_tiles:
        out_specs.append(pl.BlockSpec((tm, tn), lambda i, j: (i, jnp.minimum(j, lo_tiles - 1))))
        out_shape.append(jax.ShapeDtypeStruct((T, lo_tiles * tn), MXU))
    return pl.pallas_call(
        body, name=name, grid=(T // tm, N // tn),
        in_specs=[pl.BlockSpec((tm, K), lambda i, j: (i, 0)),
                  pl.BlockSpec((1, K), lambda i, j: (0, 0)),
                  pl.BlockSpec((tn, K), lambda i, j: (j, 0)) if w_transposed else
                  pl.BlockSpec((K, tn), lambda i, j: (0, j))],
        out_specs=out_specs, out_shape=out_shape,
        scratch_shapes=[pltpu.VMEM((tm, K), MXU)],
        compiler_params=_cparams(("parallel", "arbitrary")),
    )(x, g, w)


def matmul_nn(a, b, name, out_dtype=F32):
    M, K = a.shape
    N = b.shape[1]
    tm, tn, tk = _tile(M, 1408), _tile(N, 1536), _tile(K, 1024)
    nk = K // tk

    def body(a_ref, b_ref, o_ref, acc):
        k = pl.program_id(2)
        part = _dot(a_ref[...], b_ref[...])
        _acc_out(acc, part, k == 0)

        @pl.when(k == nk - 1)
        def _():
            o_ref[...] = acc[...].astype(out_dtype)

    return pl.pallas_call(
        body, name=name, grid=(M // tm, N // tn, nk),
        in_specs=[pl.BlockSpec((tm, tk), lambda i, j, k: (i, k)),
                  pl.BlockSpec((tk, tn), lambda i, j, k: (k, j))],
        out_specs=pl.BlockSpec((tm, tn), lambda i, j, k: (i, j)),
        out_shape=jax.ShapeDtypeStruct((M, N), out_dtype),
        scratch_shapes=[pltpu.VMEM((tm, tn), F32)],
        compiler_params=_cparams(("parallel", "parallel", "arbitrary")),
    )(a, b)


def matmul_nt_normbwd(dy, w, x, g, dres, name):
    T, N = dy.shape
    K = w.shape[0]
    tm, tn = _tile(T, 1024), _tile(N, 1536)
    nj = N // tn

    def body(dy_ref, w_ref, x_ref, g_ref, dres_ref, dx_ref, dg_ref, acc):
        i, j = pl.program_id(0), pl.program_id(1)
        _acc_out(acc, _dot_nt(dy_ref[...], w_ref[...]), j == 0)

        @pl.when(j == nj - 1)
        def _():
            dx, dg = _rms_bwd(acc[...], x_ref[...], g_ref[...])
            dx_ref[...] = dres_ref[...] + dx
            _acc_out(dg_ref, dg, i == 0)

    return pl.pallas_call(
        body, name=name, grid=(T // tm, nj),
        in_specs=[pl.BlockSpec((tm, tn), lambda i, j: (i, j)),
                  pl.BlockSpec((K, tn), lambda i, j: (0, j)),
                  pl.BlockSpec((tm, K), lambda i, j: (i, 0)),
                  pl.BlockSpec((1, K), lambda i, j: (0, 0)),
                  pl.BlockSpec((tm, K), lambda i, j: (i, 0))],
        out_specs=[pl.BlockSpec((tm, K), lambda i, j: (i, 0)),
                   pl.BlockSpec((1, K), lambda i, j: (0, 0))],
        out_shape=[jax.ShapeDtypeStruct((T, K), F32), jax.ShapeDtypeStruct((1, K), F32)],
        scratch_shapes=[pltpu.VMEM((tm, K), F32)],
        compiler_params=_cparams(("arbitrary", "arbitrary")),
    )(dy, w, x, g, dres)


def loss_kernel(y, tgt):
    T, D = y.shape
    tm = _tile(T, 512)

    def body(y_ref, t_ref, dy_ref, acc_ref):
        e = y_ref[...] - t_ref[...]
        dy_ref[...] = e * (1.0 / D)
        _acc_out(acc_ref, jnp.sum(e * e, axis=0, keepdims=True), pl.program_id(0) == 0)

    return pl.pallas_call(
        body, name="loss", grid=(T // tm,),
        in_specs=[pl.BlockSpec((tm, D), lambda i: (i, 0)), pl.BlockSpec((tm, D), lambda i: (i, 0))],
        out_specs=[pl.BlockSpec((tm, D), lambda i: (i, 0)), pl.BlockSpec((1, D), lambda i: (0, 0))],
        out_shape=[jax.ShapeDtypeStruct((T, D), F32), jax.ShapeDtypeStruct((1, D), F32)],
        compiler_params=_cparams(("arbitrary",)),
    )(y, tgt)


def _rope_partner(x):
    lane = lax.broadcasted_iota(jnp.int32, (1, 128), 1)
    return jnp.where(lane < ROPE_LANE0 + ROPE_DIM // 2, pltpu.roll(x, 128 - ROPE_DIM // 2, 1),
                     pltpu.roll(x, ROPE_DIM // 2, 1))


def _rope_apply(x, cos, sin_signed):
    return x * cos + _rope_partner(x) * sin_signed


def _rope_apply_bwd(dy, cos, sin_signed):
    lane = lax.broadcasted_iota(jnp.int32, (1, 128), 1)
    rotary = (lane >= ROPE_LANE0) & (lane < ROPE_LANE0 + ROPE_DIM)
    return dy * cos + jnp.where(rotary, _rope_partner(dy * sin_signed), 0.0)


def rope_tables(T):
    pos = jnp.arange(T, dtype=F32)
    inv_freq = ROPE_THETA ** (-(jnp.arange(ROPE_DIM // 2, dtype=F32) * 2.0 / ROPE_DIM))
    ang = pos[:, None] * inv_freq[None, :]
    cos, sin = jnp.cos(ang), jnp.sin(ang)
    z = jnp.zeros((T, ROPE_LANE0), F32)
    z2 = jnp.zeros((T, 128 - ROPE_LANE0 - ROPE_DIM), F32)
    cosr = jnp.concatenate([z, cos, cos, z2], axis=1)
    sinr = jnp.concatenate([z, -sin, sin, z2], axis=1)
    return cosr, sinr


def mla_prep(proj, gq, gkv, wuq, wukv, cosr, sinr):
    T = proj.shape[0]
    tm = _tile(T, 512)

    def body(cq_ref, ckv_ref, misc_ref, gq_ref, gkv_ref, wuq_ref, wukv_ref, cos_ref, sin_ref,
             q_ref, k_ref, v_ref, cqT_ref, ckvT_ref):
        lane = lax.broadcasted_iota(jnp.int32, (1, 128), 1)
        cosr_, sinr_ = cos_ref[...], sin_ref[...]
        cosq = cosr_ + jnp.where(lane < ROPE_LANE0, 1.0, 0.0)
        cqn = _rms_fwd(cq_ref[...], gq_ref[...])
        cqT_ref[...] = cqn.T.astype(MXU)
        qm = _dot(cqn, wuq_ref[...])
        q_ref[...] = jnp.concatenate(
            [_rope_apply(qm[:, 128 * h:128 * (h + 1)], cosq, sinr_) for h in range(4)], axis=1).astype(MXU)
        ckvn = _rms_fwd(ckv_ref[...], gkv_ref[...])
        ckvT_ref[...] = ckvn.T.astype(MXU)
        kv = _dot(ckvn, wukv_ref[...])
        kr = _rope_apply(misc_ref[...], cosr_, sinr_)
        k_ref[...] = jnp.concatenate(
            [kv[:, 128 * h:128 * (h + 1)] + kr for h in range(4)], axis=1).astype(MXU)
        v_ref[...] = kv[:, 512:768].astype(MXU)

    row = lambda i: (i, 0)
    const = lambda i: (0, 0)
    return pl.pallas_call(
        body, name="mla_prep", grid=(T // tm,),
        in_specs=[pl.BlockSpec((tm, 256), lambda i: (i, C_CQ // 256)),
                  pl.BlockSpec((tm, 128), lambda i: (i, C_CKV // 128)),
                  pl.BlockSpec((tm, 128), lambda i: (i, C_MISC // 128)),
                  pl.BlockSpec((1, 256), const), pl.BlockSpec((1, 128), const),
                  pl.BlockSpec((256, 512), const), pl.BlockSpec((128, 768), const),
                  pl.BlockSpec((tm, 128), row), pl.BlockSpec((tm, 128), row)],
        out_specs=[pl.BlockSpec((tm, 512), row), pl.BlockSpec((tm, 512), row), pl.BlockSpec((tm, 256), row),
                   pl.BlockSpec((256, tm), lambda i: (0, i)), pl.BlockSpec((128, tm), lambda i: (0, i))],
        out_shape=[jax.ShapeDtypeStruct((T, 512), MXU), jax.ShapeDtypeStruct((T, 512), MXU),
                   jax.ShapeDtypeStruct((T, 256), MXU),
                   jax.ShapeDtypeStruct((256, T), MXU), jax.ShapeDtypeStruct((128, T), MXU)],
        compiler_params=_cparams(("parallel",)),
    )(proj, proj, proj, gq, gkv, wuq, wukv, cosr, sinr)


def mla_prep_bwd(dq, dk, dv, proj, gq, gkv, wuq, wukv, cosr, sinr):
    T = proj.shape[0]
    tm = _tile(T, 512)

    def body(dq_ref, dk_ref, dv_ref, cq_ref, ckv_ref, gq_ref, gkv_ref, wuq_ref, wukv_ref, cos_ref, sin_ref,
             dqm_ref, dkv_ref, dcq_ref, dckv_ref, dmisc_ref, dgq_ref, dgkv_ref):
        first = pl.program_id(0) == 0
        lane = lax.broadcasted_iota(jnp.int32, (1, 128), 1)
        cosr_, sinr_ = cos_ref[...], sin_ref[...]
        cosq = cosr_ + jnp.where(lane < ROPE_LANE0, 1.0, 0.0)
        dqv = dq_ref[...]
        dqm = jnp.concatenate(
            [_rope_apply_bwd(dqv[:, 128 * h:128 * (h + 1)], cosq, sinr_) for h in range(4)], axis=1)
        dqm_ref[...] = dqm.astype(MXU)
        dcq, dgq = _rms_bwd(_dot_nt(dqm, wuq_ref[...]), cq_ref[...], gq_ref[...])
        dcq_ref[...] = dcq
        _acc_out(dgq_ref, dgq, first)
        dkv_ = dk_ref[...]
        heads = [dkv_[:, 128 * h:128 * (h + 1)] for h in range(4)]
        dkr = heads[0] + heads[1] + heads[2] + heads[3]
        dmisc_ref[...] = _rope_apply_bwd(dkr, cosr_, sinr_)
        dkvm = jnp.concatenate([jnp.where(lane < ROPE_LANE0, hd, 0.0) for hd in heads] + [dv_ref[...]], axis=1)
        dkv_ref[...] = dkvm.astype(MXU)
        dckv, dgkv = _rms_bwd(_dot_nt(dkvm, wukv_ref[...]), ckv_ref[...], gkv_ref[...])
        dckv_ref[...] = dckv
        _acc_out(dgkv_ref, dgkv, first)

    row = lambda i: (i, 0)
    const = lambda i: (0, 0)
    return pl.pallas_call(
        body, name="mla_prep_bwd", grid=(T // tm,),
        in_specs=[pl.BlockSpec((tm, 512), row), pl.BlockSpec((tm, 512), row), pl.BlockSpec((tm, 256), row),
                  pl.BlockSpec((tm, 256), lambda i: (i, C_CQ // 256)),
                  pl.BlockSpec((tm, 128), lambda i: (i, C_CKV // 128)),
                  pl.BlockSpec((1, 256), const), pl.BlockSpec((1, 128), const),
                  pl.BlockSpec((256, 512), const), pl.BlockSpec((128, 768), const),
                  pl.BlockSpec((tm, 128), row), pl.BlockSpec((tm, 128), row)],
        out_specs=[pl.BlockSpec((tm, 512), row), pl.BlockSpec((tm, 768), row), pl.BlockSpec((tm, 256), row),
                   pl.BlockSpec((tm, 128), row), pl.BlockSpec((tm, 128), row),
                   pl.BlockSpec((1, 256), const), pl.BlockSpec((1, 128), const)],
        out_shape=[jax.ShapeDtypeStruct((T, 512), MXU), jax.ShapeDtypeStruct((T, 768), MXU),
                   jax.ShapeDtypeStruct((T, 256), F32), jax.ShapeDtypeStruct((T, 128), F32),
                   jax.ShapeDtypeStruct((T, 128), F32),
                   jax.ShapeDtypeStruct((1, 256), F32), jax.ShapeDtypeStruct((1, 128), F32)],
        compiler_params=_cparams(("arbitrary",)),
    )(dq, dk, dv, proj, proj, gq, gkv, wuq, wukv, cosr, sinr)


def _split3(x):
    hi = x.astype(MXU)
    r1 = x - hi.astype(F32)
    mid = r1.astype(MXU)
    lo = (r1 - mid.astype(F32)).astype(MXU)
    return hi, mid, lo


def _tri_matmul(tri, x):
    hi, mid, lo = _split3(x)
    d = lambda p: jnp.dot(tri, p, preferred_element_type=F32)
    return d(hi) + d(mid) + d(lo)


def _log_sigmoid(z):
    return jnp.minimum(z, 0.0) - jnp.log(1.0 + jnp.exp(-jnp.abs(z)))


def fox_gate(proj, fbias):
    T = proj.shape[0]
    tb = _tile(T, 512)

    def body(misc_ref, b_ref, fc_ref, fr_ref, frep_ref, carry):
        @pl.when(pl.program_id(0) == 0)
        def _():
            carry[...] = jnp.zeros_like(carry)

        lane = lax.broadcasted_iota(jnp.int32, (1, 128), 1)
        lf = jnp.where(lane < 4, _log_sigmoid(misc_ref[...] + b_ref[...]), 0.0)
        r = lax.broadcasted_iota(jnp.int32, (tb, tb), 0)
        c = lax.broadcasted_iota(jnp.int32, (tb, tb), 1)
        tri = jnp.where(r >= c, 1.0, 0.0).astype(MXU)
        F = _tri_matmul(tri, lf) + carry[...]
        carry[...] = carry[...] + jnp.sum(lf, axis=0, keepdims=True)
        fc_ref[0] = F
        fc_ref[1] = pltpu.roll(F, 126, 1)
        ft = F.T[0:8, :]
        fr_ref[0] = ft
        fr_ref[1] = pltpu.roll(ft, 6, 0)
        for h in range(4):
            frep_ref[h] = jnp.broadcast_to(_lane_pick(F, h), (tb, 128))

    return pl.pallas_call(
        body, name="fox_gate", grid=(T // tb,),
        in_specs=[pl.BlockSpec((tb, 128), lambda i: (i, C_MISC // 128)), pl.BlockSpec((1, 128), lambda i: (0, 0))],
        out_specs=[pl.BlockSpec((2, tb, 128), lambda i: (0, i, 0)), pl.BlockSpec((2, 8, tb), lambda i: (0, 0, i)),
                   pl.BlockSpec((4, tb, 128), lambda i: (0, i, 0))],
        out_shape=[jax.ShapeDtypeStruct((2, T, 128), F32), jax.ShapeDtypeStruct((2, 8, T), F32),
                   jax.ShapeDtypeStruct((4, T, 128), F32)],
        scratch_shapes=[pltpu.VMEM((1, 128), F32)],
        compiler_params=_cparams(("arbitrary",)),
    )(proj, fbias)


def fox_gate_bwd(dFq, dFk, proj, fbias):
    T = proj.shape[0]
    tb = _tile(T, 512)
    nb = T // tb

    def body(dq_ref, dk_ref, misc_ref, b_ref, dm_ref, db_ref, carry):
        first = pl.program_id(0) == 0

        @pl.when(first)
        def _():
            carry[...] = jnp.zeros_like(carry)

        lane = lax.broadcasted_iota(jnp.int32, (1, 128), 1)
        dF = jnp.where(lane < 4, (dq_ref[0] + dk_ref[0]) + pltpu.roll(dq_ref[1] + dk_ref[1], 2, 1), 0.0)
        r = lax.broadcasted_iota(jnp.int32, (tb, tb), 0)
        c = lax.broadcasted_iota(jnp.int32, (tb, tb), 1)
        tri = jnp.where(r <= c, 1.0, 0.0).astype(MXU)
        dlf = _tri_matmul(tri, dF) + carry[...]
        carry[...] = carry[...] + jnp.sum(dF, axis=0, keepdims=True)
        z = misc_ref[...] + b_ref[...]
        dz = jnp.where(lane < 4, dlf * (1.0 / (1.0 + jnp.exp(z))), 0.0)
        dm_ref[...] = dz
        _acc_out(db_ref, jnp.sum(dz, axis=0, keepdims=True), first)

    return pl.pallas_call(
        body, name="fox_gate_bwd", grid=(nb,),
        in_specs=[pl.BlockSpec((2, tb, 128), lambda i: (0, nb - 1 - i, 0)),
                  pl.BlockSpec((2, tb, 128), lambda i: (0, nb - 1 - i, 0)),
                  pl.BlockSpec((tb, 128), lambda i: (nb - 1 - i, C_MISC // 128)),
                  pl.BlockSpec((1, 128), lambda i: (0, 0))],
        out_specs=[pl.BlockSpec((tb, 128), lambda i: (nb - 1 - i, 0)), pl.BlockSpec((1, 128), lambda i: (0, 0))],
        out_shape=[jax.ShapeDtypeStruct((T, 128), F32), jax.ShapeDtypeStruct((1, 128), F32)],
        scratch_shapes=[pltpu.VMEM((1, 128), F32)],
        compiler_params=_cparams(("arbitrary",)),
    )(dFq, dFk, proj, fbias)


FLASH_TILE = 512


def _row_stat_tile(a, b, n):
    at = jnp.broadcast_to(a, (n, 128)).T[0:8, :]
    bt = jnp.broadcast_to(b, (n, 128)).T[0:8, :]
    sub = lax.broadcasted_iota(jnp.int32, (8, 1), 0)
    return jnp.where(sub == 0, at, jnp.where(sub == 1, bt, 0.0))


def _col_stat_tile(a, b):
    lane = lax.broadcasted_iota(jnp.int32, (1, 128), 1)
    return jnp.where(lane == 0, a, jnp.where(lane == 1, b, 0.0))


def _lane_pick(x, h):
    lane = lax.broadcasted_iota(jnp.int32, (1, 128), 1)
    return jnp.sum(jnp.where(lane == h, x, 0.0), axis=1, keepdims=True)


def _half_mask(h):
    lane = lax.broadcasted_iota(jnp.int32, (1, 128), 1)
    return (lane // HEAD_DIM) == h


def _call_hosting(body, name, grid, args, in_specs, out_specs, out_shape, scratch, exch):
    n_out = len(out_shape)
    if exch is not None:
        body, (xargs, xin, xout, xshape, xscratch) = hosted_exchange(
            body, len(args), n_out, len(scratch), grid, *exch)
        args, in_specs, out_specs = args + xargs, in_specs + xin, out_specs + xout
        out_shape, scratch = out_shape + xshape, scratch + xscratch
    res = pl.pallas_call(
        body, name=name, grid=grid, in_specs=in_specs, out_specs=out_specs, out_shape=out_shape,
        scratch_shapes=scratch, compiler_params=_cparams(("arbitrary",) * len(grid)),
    )(*args)
    return res[:n_out], res[n_out:]


def flash_fwd(q, k, v, frep, frow, *, qblk, kblk, vblk, nq, scale, name, exch=None):
    T = q.shape[0]
    tq = tk = _tile(T, FLASH_TILE)
    wq = 128 * nq
    has_f = frep is not None

    def body(*refs):
        if has_f:
            q_ref, k_ref, v_ref, fk_ref, fr_ref, o_ref, lr_ref, vT_sc, m_sc, acc_sc = refs
        else:
            q_ref, k_ref, v_ref, o_ref, lr_ref, vT_sc, m_sc, acc_sc = refs
        i = pl.program_id(1)

        @pl.when(i == 0)
        def _():
            vT_sc[...] = v_ref[...].astype(F32).T.astype(MXU)

        diag = lax.broadcasted_iota(jnp.int32, (tk, 1), 0) <= lax.broadcasted_iota(jnp.int32, (1, tq), 1)
        row_half = lax.broadcasted_iota(jnp.int32, (128, 1), 0) // HEAD_DIM
        qb = q_ref[...].astype(F32) * scale
        if nq == 1:
            qhs = [jnp.where(_half_mask(h), qb, 0).astype(MXU) for h in range(2)]
        else:
            qhs = [qb[:, 128 * h:128 * (h + 1)].astype(MXU) for h in range(2)]
        for h in range(2):
            m_sc[h] = jnp.full((1, tq), NEG, F32)
            acc_sc[h] = jnp.zeros((128, tq), F32)

        def make_step(masked):
            def step(j, carry):
                off = pl.multiple_of(j * tk, tk)
                ks = k_ref[pl.ds(off, tk), :]
                vT = vT_sc[:, pl.ds(off, tk)]
                for h in range(2):
                    kh = ks if nq == 1 else ks[:, 128 * h:128 * (h + 1)]
                    sT = _dot_nt(kh, qhs[h])
                    if has_f:
                        fk = fk_ref[h, pl.ds(off, tk), :]
                        sT = sT + (fr_ref[0, h:h + 1, :] - jnp.concatenate([fk] * (tq // 128), axis=1))
                    if masked:
                        sT = jnp.where(diag, sT, NEG)
                    m_prev = m_sc[h]
                    m_new = jnp.maximum(m_prev, jnp.max(sT, axis=0, keepdims=True))
                    alpha = jnp.exp(m_prev - m_new)
                    pT = jnp.exp(sT - m_new)
                    vTh = jnp.where(row_half == h, vT, jnp.ones_like(vT))
                    acc_sc[h] = alpha * acc_sc[h] + _dot(vTh, pT)
                    m_sc[h] = m_new
                return carry
            return step

        lax.fori_loop(0, i, make_step(False), 0)
        make_step(True)(i, 0)
        outs, lses = [], []
        for h in range(2):
            acc = acc_sc[h]
            outs.append(acc / pltpu.roll(acc, HEAD_DIM, 0))
            l = acc_sc[h, HEAD_DIM * (1 - h):HEAD_DIM * (1 - h) + 1, :]
            lses.append(m_sc[h] + jnp.log(l))
        o_ref[...] = jnp.where(row_half == 0, outs[0], outs[1]).T
        sub = lax.broadcasted_iota(jnp.int32, (8, 1), 0)
        lr_ref[0] = jnp.where(sub == 0, lses[0], jnp.where(sub == 1, lses[1], 0.0))

    in_specs = [pl.BlockSpec((tq, wq), lambda p, i: (i, qblk + p)),
                pl.BlockSpec((T, wq), lambda p, i: (0, kblk + p)),
                pl.BlockSpec((T, 128), lambda p, i: (0, vblk + p))]
    args = [q, k, v]
    if has_f:
        in_specs += [pl.BlockSpec((2, T, 128), lambda p, i: (p, 0, 0)),
                     pl.BlockSpec((1, 8, tq), lambda p, i: (p, 0, i))]
        args += [frep, frow]
    out_specs = [pl.BlockSpec((tq, 128), lambda p, i: (i, p)), pl.BlockSpec((1, 8, tq), lambda p, i: (p, 0, i))]
    out_shape = [jax.ShapeDtypeStruct((T, 256), F32), jax.ShapeDtypeStruct((2, 8, T), F32)]
    scratch = [pltpu.VMEM((128, T), MXU), pltpu.VMEM((2, 1, tq), F32), pltpu.VMEM((2, 128, tq), F32)]
    return _call_hosting(body, name, (2, T // tq), args, in_specs, out_specs, out_shape, scratch, exch)


def flash_bwd(q, k, v, do, o, lrow, fcol, frow, *, qblk, kblk, vblk, nq, scale, name, exch=None):
    T = q.shape[0]
    tq = tk = _tile(T, FLASH_TILE)
    wq = 128 * nq
    nqb = T // tq
    has_f = fcol is not None

    def body(*refs):
        if has_f:
            (q_ref, k_ref, v_ref, do_ref, o_ref, lr_ref, fc_ref, fr_ref,
             dq_ref, dk_ref, dv_ref, df_ref, dfq_ref, dk_sc, dv_sc, dqT_sc, d_sc, df_sc, dfq_sc) = refs
        else:
            q_ref, k_ref, v_ref, do_ref, o_ref, lr_ref, dq_ref, dk_ref, dv_ref, dk_sc, dv_sc, dqT_sc, d_sc = refs
        j = pl.program_id(1)
        diag = lax.broadcasted_iota(jnp.int32, (tk, 1), 0) <= lax.broadcasted_iota(jnp.int32, (1, tq), 1)
        hms = [_half_mask(h) for h in range(2)]

        @pl.when(j == 0)
        def _():
            dqT_sc[...] = jnp.zeros_like(dqT_sc)
            if has_f:
                dfq_sc[...] = jnp.zeros_like(dfq_sc)

            def delta(b, carry):
                off = pl.multiple_of(b * tq, tq)
                prod = do_ref[pl.ds(off, tq), :] * o_ref[pl.ds(off, tq), :]
                Ds = [jnp.sum(jnp.where(hms[h], prod, 0.0), axis=1, keepdims=True) for h in range(2)]
                d_sc[:, pl.ds(off, tq)] = _row_stat_tile(Ds[0], Ds[1], tq)
                return carry

            lax.fori_loop(0, nqb, delta, 0)

        kb = k_ref[...]
        vb = v_ref[...]
        if nq == 1:
            khs = [jnp.where(hms[h], kb, 0).astype(MXU) for h in range(2)]
        else:
            khs = [kb[:, 128 * h:128 * (h + 1)].astype(MXU) for h in range(2)]
        kTs = [kh.astype(F32).T.astype(MXU) for kh in khs]
        kss = [(kh.astype(F32) * scale).astype(MXU) for kh in khs]
        vhs = [jnp.where(hms[h], vb, 0).astype(MXU) for h in range(2)]
        fks = [_lane_pick(fc_ref[0], h) for h in range(2)] if has_f else None
        dv_sc[...] = jnp.zeros_like(dv_sc)
        dk_sc[...] = jnp.zeros_like(dk_sc)
        if has_f:
            df_sc[...] = jnp.zeros_like(df_sc)

        def make_step(masked):
            def step(i, carry):
                off = pl.multiple_of(i * tq, tq)
                qs = q_ref[pl.ds(off, tq), :]
                dos = do_ref[pl.ds(off, tq), :]
                for h in range(2):
                    qh = qs if nq == 1 else qs[:, 128 * h:128 * (h + 1)]
                    sT = _dot_nt(kss[h], qh)
                    if has_f:
                        sT = sT + (fr_ref[0, h:h + 1, pl.ds(off, tq)] - fks[h])
                    pT = jnp.exp(sT - lr_ref[0, h:h + 1, pl.ds(off, tq)])
                    if masked:
                        pT = jnp.where(diag, pT, 0.0)
                    dsT = pT * (_dot_nt(vhs[h], dos) - d_sc[h:h + 1, pl.ds(off, tq)])
                    dv_sc[...] += _dot(pT, jnp.where(hms[h], dos, 0))
                    qq = jnp.where(hms[h], qs, 0) if nq == 1 else qh
                    dk_sc[h if nq == 2 else 0] += _dot(dsT, qq)
                    dqT_sc[h if nq == 2 else 0, :, pl.ds(off, tq)] += _dot(kTs[h], dsT)
                    if has_f:
                        part = dsT[:, 0:128]
                        for c in range(1, tq // 128):
                            part = part + dsT[:, 128 * c:128 * (c + 1)]
                        df_sc[h] += part
                        dfq_sc[h:h + 1, pl.ds(off, tq)] += jnp.sum(dsT, axis=0, keepdims=True)
                return carry
            return step

        make_step(True)(j, 0)
        lax.fori_loop(j + 1, nqb, make_step(False), 0)
        if nq == 1:
            dk_ref[...] = dk_sc[0] * scale
        else:
            dk_ref[...] = jnp.concatenate([dk_sc[0], dk_sc[1]], axis=1) * scale
        dv_ref[...] = dv_sc[...]
        if has_f:
            df_ref[0] = _col_stat_tile(-jnp.sum(df_sc[0], axis=1, keepdims=True),
                                       -jnp.sum(df_sc[1], axis=1, keepdims=True))

        @pl.when(j == nqb - 1)
        def _():
            if nq == 1:
                dq_ref[...] = dqT_sc[0].T * scale
            else:
                dq_ref[...] = jnp.concatenate([dqT_sc[0].T, dqT_sc[1].T], axis=1) * scale
            if has_f:
                sub = lax.broadcasted_iota(jnp.int32, (128, 1), 0)
                rows = jnp.where(sub == 0, dfq_sc[0:1, :], jnp.where(sub == 1, dfq_sc[1:2, :], 0.0))
                dfq_ref[0] = rows.T

    in_specs = [pl.BlockSpec((T, wq), lambda p, j: (0, qblk + p)),
                pl.BlockSpec((tk, wq), lambda p, j: (j, kblk + p)),
                pl.BlockSpec((tk, 128), lambda p, j: (j, vblk + p)),
                pl.BlockSpec((T, 128), lambda p, j: (0, p)),
                pl.BlockSpec((T, 128), lambda p, j: (0, p)),
                pl.BlockSpec((1, 8, T), lambda p, j: (p, 0, 0))]
    args = [q, k, v, do, o, lrow]
    out_specs = [pl.BlockSpec((T, wq), lambda p, j: (0, p)),
                 pl.BlockSpec((tk, wq), lambda p, j: (j, p)), pl.BlockSpec((tk, 128), lambda p, j: (j, p))]
    out_shape = [jax.ShapeDtypeStruct((T, 2 * wq), F32), jax.ShapeDtypeStruct((T, 2 * wq), F32),
                 jax.ShapeDtypeStruct((T, 256), F32)]
    scratch = [pltpu.VMEM((nq, tk, 128), F32), pltpu.VMEM((tk, 128), F32), pltpu.VMEM((nq, 128, T), F32),
               pltpu.VMEM((8, T), F32)]
    if has_f:
        in_specs += [pl.BlockSpec((1, tk, 128), lambda p, j: (p, j, 0)),
                     pl.BlockSpec((1, 8, T), lambda p, j: (p, 0, 0))]
        args += [fcol, frow]
        out_specs += [pl.BlockSpec((1, tk, 128), lambda p, j: (p, j, 0)),
                      pl.BlockSpec((1, T, 128), lambda p, j: (p, 0, 0))]
        out_shape += [jax.ShapeDtypeStruct((2, T, 128), F32), jax.ShapeDtypeStruct((2, T, 128), F32)]
        scratch += [pltpu.VMEM((2, tk, 128), F32), pltpu.VMEM((8, T), F32)]
    return _call_hosting(body, name, (2, T // tk), args, in_specs, out_specs, out_shape, scratch, exch)


def _swa_align(pair, e, h):
    sel = jnp.where(_half_mask(e), pair, 0.0)
    if e == h:
        return sel
    return pltpu.roll(sel, HEAD_DIM, 1)


def _swa_mask(n):
    W = WINDOW
    qi = lax.broadcasted_iota(jnp.int32, (W, 2 * W), 0) + W
    kj = lax.broadcasted_iota(jnp.int32, (W, 2 * W), 1)
    dist = qi - kj
    return (dist >= 0) & (dist < W) & ((n > 0) | (kj >= W))


def swa_fwd(proj, bias, sinks):
    T = proj.shape[0]
    W = WINDOW
    nb = T // W
    scale = HEAD_DIM ** -0.5

    def body(sink_ref, q_ref, kp_ref, kc_ref, vp_ref, vc_ref, b_ref, o_ref, l_ref):
        n = pl.program_id(0)
        mask = _swa_mask(n)
        kband = jnp.concatenate([kp_ref[...], kc_ref[...]], axis=0).astype(MXU)
        vband = jnp.concatenate([vp_ref[...], vc_ref[...]], axis=0).astype(MXU)
        lane = lax.broadcasted_iota(jnp.int32, (1, 128), 1)
        lse_tile = jnp.zeros((W, 128), F32)
        mask4 = jnp.concatenate([mask] * 4, axis=0)
        pairs = []
        for h in range(2):
            q4 = jnp.concatenate([_swa_align(q_ref[:, 128 * ((4 * h + g) // 2):128 * ((4 * h + g) // 2 + 1)],
                                             g % 2, h) for g in range(4)], axis=0)
            sink4 = jnp.concatenate([jnp.full((W, 1), sink_ref[4 * h + g], F32) for g in range(4)], axis=0)
            s = _dot_nt(q4, kband) * scale + b_ref[4 * h:4 * h + 4].reshape(4 * W, 2 * W)
            s = jnp.where(mask4, s, NEG)
            m = jnp.maximum(jnp.max(s, axis=1, keepdims=True), sink4)
            e = jnp.exp(s - m)
            l = jnp.sum(e, axis=1, keepdims=True) + jnp.exp(sink4 - m)
            r = jnp.where(_half_mask(h), _dot(e, vband), 0.0) / l
            r = r + pltpu.roll(r, HEAD_DIM, 1)
            lse4 = m + jnp.log(l)
            for g in range(4):
                lse_tile = jnp.where(lane == 4 * h + g, lse4[W * g:W * (g + 1)], lse_tile)
            pairs.append(jnp.where(_half_mask(0), r[0:W], r[W:2 * W]))
            pairs.append(jnp.where(_half_mask(0), r[2 * W:3 * W], r[3 * W:4 * W]))
        o_ref[...] = jnp.concatenate(pairs, axis=1)
        l_ref[...] = lse_tile

    prev = lambda n: (jnp.maximum(n - 1, 0), C_KA // 128)
    cur = lambda n: (n, C_KA // 128)
    prev_v = lambda n: (jnp.maximum(n - 1, 0), C_VA // 128)
    cur_v = lambda n: (n, C_VA // 128)
    return pl.pallas_call(
        body, name="swa_fwd", grid=(nb,),
        in_specs=[pl.BlockSpec(memory_space=pltpu.SMEM),
                  pl.BlockSpec((W, 512), lambda n: (n, 0)),
                  pl.BlockSpec((W, 128), prev), pl.BlockSpec((W, 128), cur),
                  pl.BlockSpec((W, 128), prev_v), pl.BlockSpec((W, 128), cur_v),
                  pl.BlockSpec((8, W, 2 * W), lambda n: (0, 0, 0))],
        out_specs=[pl.BlockSpec((W, 512), lambda n: (n, 0)), pl.BlockSpec((W, 128), lambda n: (n, 0))],
        out_shape=[jax.ShapeDtypeStruct((T, 512), F32), jax.ShapeDtypeStruct((T, 128), F32)],
        compiler_params=_cparams(("parallel",)),
    )(sinks, proj, proj, proj, proj, proj, bias)


def swa_bwd(proj, bias, sinks, do, o, lse):
    T = proj.shape[0]
    W = WINDOW
    nb = T // W
    scale = HEAD_DIM ** -0.5

    def body(sink_ref, q_ref, kp_ref, kc_ref, vp_ref, vc_ref, b_ref, do_ref, o_ref, l_ref,
             dq_ref, dk_ref, dv_ref, db_ref, dsk_ref, ck, cv):
        n = pl.program_id(0)

        @pl.when(n == 0)
        def _():
            ck[...] = jnp.zeros_like(ck)
            cv[...] = jnp.zeros_like(cv)
            db_ref[...] = jnp.zeros_like(db_ref)
            dsk_ref[...] = jnp.zeros_like(dsk_ref)

        @pl.when(n < nb)
        def _():
            mask = _swa_mask(n)
            kb32 = jnp.concatenate([kp_ref[...], kc_ref[...]], axis=0)
            vb32 = jnp.concatenate([vp_ref[...], vc_ref[...]], axis=0)
            kband = kb32.astype(MXU)
            sub = lax.broadcasted_iota(jnp.int32, (8, 1), 0)
            dk_band = jnp.zeros((2 * W, 128), F32)
            dv_band = jnp.zeros((2 * W, 128), F32)
            dsk = jnp.zeros((8, 128), F32)
            dq_pairs = []
            mask4 = jnp.concatenate([mask] * 4, axis=0)
            for h in range(2):
                hm = _half_mask(h)
                km = jnp.where(hm, kb32, 0.0).astype(MXU)
                vm = jnp.where(hm, vb32, 0.0).astype(MXU)
                pbs = [slice(128 * ((4 * h + g) // 2), 128 * ((4 * h + g) // 2 + 1)) for g in range(4)]
                q4 = jnp.concatenate([_swa_align(q_ref[:, pbs[g]], g % 2, h) for g in range(4)], axis=0)
                do4 = jnp.concatenate([_swa_align(do_ref[:, pbs[g]], g % 2, h) for g in range(4)], axis=0)
                D4 = jnp.concatenate(
                    [jnp.sum(jnp.where(_half_mask(g % 2), do_ref[:, pbs[g]] * o_ref[:, pbs[g]], 0.0), axis=1,
                             keepdims=True) for g in range(4)], axis=0)
                lse4 = jnp.concatenate([_lane_pick(l_ref[...], 4 * h + g) for g in range(4)], axis=0)
                sink4 = jnp.concatenate([jnp.full((W, 1), sink_ref[4 * h + g], F32) for g in range(4)], axis=0)
                s = _dot_nt(q4, kband) * scale + b_ref[4 * h:4 * h + 4].reshape(4 * W, 2 * W)
                p = jnp.where(mask4, jnp.exp(s - lse4), 0.0)
                sd = jnp.exp(sink4 - lse4) * D4
                for g in range(4):
                    dsk = dsk + jnp.where(sub == 4 * h + g,
                                          -jnp.sum(sd[W * g:W * (g + 1)], axis=0, keepdims=True), 0.0)
                ds = p * (_dot_nt(do4, vm) - D4)
                db_ref[4 * h:4 * h + 4] += ds.reshape(4, W, 2 * W)
                dq = _dot(ds, km) * scale
                dq = dq + pltpu.roll(dq, HEAD_DIM, 1)
                dk_band = dk_band + _dot(ds.T, q4) * scale
                dv_band = dv_band + _dot(p.T, do4)
                dq_pairs.append(jnp.where(_half_mask(0), dq[0:W], dq[W:2 * W]))
                dq_pairs.append(jnp.where(_half_mask(0), dq[2 * W:3 * W], dq[3 * W:4 * W]))
            dq_ref[...] = jnp.concatenate(dq_pairs, axis=1)
            dsk_ref[...] += dsk
            dk_ref[...] = ck[...] + dk_band[0:W]
            dv_ref[...] = cv[...] + dv_band[0:W]
            ck[...] = dk_band[W:2 * W]
            cv[...] = dv_band[W:2 * W]

        @pl.when(n == nb)
        def _():
            dk_ref[...] = ck[...]
            dv_ref[...] = cv[...]

    cl = lambda n: jnp.minimum(n, nb - 1)
    pv = lambda n: jnp.maximum(jnp.minimum(n, nb - 1) - 1, 0)
    return pl.pallas_call(
        body, name="swa_bwd", grid=(nb + 1,),
        in_specs=[pl.BlockSpec(memory_space=pltpu.SMEM),
                  pl.BlockSpec((W, 512), lambda n: (cl(n), 0)),
                  pl.BlockSpec((W, 128), lambda n: (pv(n), C_KA // 128)),
                  pl.BlockSpec((W, 128), lambda n: (cl(n), C_KA // 128)),
                  pl.BlockSpec((W, 128), lambda n: (pv(n), C_VA // 128)),
                  pl.BlockSpec((W, 128), lambda n: (cl(n), C_VA // 128)),
                  pl.BlockSpec((8, W, 2 * W), lambda n: (0, 0, 0)),
                  pl.BlockSpec((W, 512), lambda n: (cl(n), 0)),
                  pl.BlockSpec((W, 512), lambda n: (cl(n), 0)),
                  pl.BlockSpec((W, 128), lambda n: (cl(n), 0))],
        out_specs=[pl.BlockSpec((W, 512), lambda n: (cl(n), 0)),
                   pl.BlockSpec((W, 128), lambda n: (jnp.maximum(n - 1, 0), 0)),
                   pl.BlockSpec((W, 128), lambda n: (jnp.maximum(n - 1, 0), 0)),
                   pl.BlockSpec((8, W, 2 * W), lambda n: (0, 0, 0)),
                   pl.BlockSpec((8, 128), lambda n: (0, 0))],
        out_shape=[jax.ShapeDtypeStruct((T, 512), F32), jax.ShapeDtypeStruct((T, 128), F32),
                   jax.ShapeDtypeStruct((T, 128), F32), jax.ShapeDtypeStruct((8, W, 2 * W), F32),
                   jax.ShapeDtypeStruct((8, 128), F32)],
        scratch_shapes=[pltpu.VMEM((W, 128), F32), pltpu.VMEM((W, 128), F32)],
        compiler_params=_cparams(("arbitrary",)),
    )(sinks, proj, proj, proj, proj, proj, bias, do, o, lse)


def swa_bias_table(rel_bias):
    W = WINDOW
    qi = jnp.arange(W, dtype=jnp.int32)[:, None] + W
    kj = jnp.arange(2 * W, dtype=jnp.int32)[None, :]
    dist = qi - kj
    max_exact = REL_BUCKETS // 2
    d = jnp.maximum(dist, 0)
    log_ratio = jnp.log(jnp.maximum(d, 1).astype(F32) / max_exact) / math.log(REL_MAX_DIST / max_exact)
    large = jnp.minimum(max_exact + (log_ratio * (REL_BUCKETS - max_exact)).astype(jnp.int32), REL_BUCKETS - 1)
    bucket = jnp.where(d < max_exact, d, large)
    bucket = bucket.reshape(-1)
    onehot = (bucket[None, :] == jnp.arange(REL_BUCKETS, dtype=jnp.int32)[:, None]).astype(F32)
    bias = jnp.dot(rel_bias.astype(F32).T, onehot, precision=lax.Precision.HIGHEST)
    return bias.reshape(SWA_Q_HEADS, W, 2 * W), bucket


def attn_out(oa, ob, oc, gn, wout, gpost, x):
    T = x.shape[0]
    tm = _tile(T, 512)

    def body(oa_ref, ob_ref, oc_ref, gn_ref, w_ref, gp_ref, x_ref, x2_ref, y_ref, mT_ref):
        g = gn_ref[...]
        mixed = jnp.concatenate([_rms_fwd(oa_ref[...], g[:, 0:512]), _rms_fwd(ob_ref[...], g[:, 512:768]),
                                 _rms_fwd(oc_ref[...], g[:, 768:1024])], axis=1)
        mT_ref[...] = mixed.T.astype(MXU)
        y = _dot(mixed, w_ref[...])
        y_ref[...] = y
        x2_ref[...] = x_ref[...] + _rms_fwd(y, gp_ref[...])

    row = lambda i: (i, 0)
    const = lambda i: (0, 0)
    return pl.pallas_call(
        body, name="attn_out", grid=(T // tm,),
        in_specs=[pl.BlockSpec((tm, 512), row), pl.BlockSpec((tm, 256), row), pl.BlockSpec((tm, 256), row),
                  pl.BlockSpec((1, 1024), const), pl.BlockSpec((1024, 1024), const), pl.BlockSpec((1, 1024), const),
                  pl.BlockSpec((tm, 1024), row)],
        out_specs=[pl.BlockSpec((tm, 1024), row), pl.BlockSpec((tm, 1024), row),
                   pl.BlockSpec((1024, tm), lambda i: (0, i))],
        out_shape=[jax.ShapeDtypeStruct((T, 1024), F32), jax.ShapeDtypeStruct((T, 1024), F32),
                   jax.ShapeDtypeStruct((1024, T), MXU)],
        compiler_params=_cparams(("parallel",)),
    )(oa, ob, oc, gn, wout, gpost, x)


def attn_out_bwd(dx2, y, oa, ob, oc, gn, wout, gpost):
    T = dx2.shape[0]
    tm = _tile(T, 512)

    def body(dx_ref, y_ref, oa_ref, ob_ref, oc_ref, gn_ref, w_ref, gp_ref,
             dy_ref, da_ref, db_ref, dc_ref, dgn_ref, dgp_ref):
        first = pl.program_id(0) == 0
        dy, dgp = _rms_bwd(dx_ref[...], y_ref[...], gp_ref[...])
        dy_ref[...] = dy.astype(MXU)
        _acc_out(dgp_ref, dgp, first)
        dm = _dot_nt(dy, w_ref[...])
        g = gn_ref[...]
        da, dga = _rms_bwd(dm[:, 0:512], oa_ref[...], g[:, 0:512])
        db, dgb = _rms_bwd(dm[:, 512:768], ob_ref[...], g[:, 512:768])
        dc, dgc = _rms_bwd(dm[:, 768:1024], oc_ref[...], g[:, 768:1024])
        da_ref[...] = da
        db_ref[...] = db
        dc_ref[...] = dc
        _acc_out(dgn_ref, jnp.concatenate([dga, dgb, dgc], axis=1), first)

    row = lambda i: (i, 0)
    const = lambda i: (0, 0)
    return pl.pallas_call(
        body, name="attn_out_bwd", grid=(T // tm,),
        in_specs=[pl.BlockSpec((tm, 1024), row), pl.BlockSpec((tm, 1024), row),
                  pl.BlockSpec((tm, 512), row), pl.BlockSpec((tm, 256), row), pl.BlockSpec((tm, 256), row),
                  pl.BlockSpec((1, 1024), const), pl.BlockSpec((1024, 1024), const), pl.BlockSpec((1, 1024), const)],
        out_specs=[pl.BlockSpec((tm, 1024), row), pl.BlockSpec((tm, 512), row), pl.BlockSpec((tm, 256), row),
                   pl.BlockSpec((tm, 256), row), pl.BlockSpec((1, 1024), const), pl.BlockSpec((1, 1024), const)],
        out_shape=[jax.ShapeDtypeStruct((T, 1024), MXU), jax.ShapeDtypeStruct((T, 512), F32),
                   jax.ShapeDtypeStruct((T, 256), F32), jax.ShapeDtypeStruct((T, 256), F32),
                   jax.ShapeDtypeStruct((1, 1024), F32), jax.ShapeDtypeStruct((1, 1024), F32)],
        compiler_params=_cparams(("arbitrary",)),
    )(dx2, y, oa, ob, oc, gn, wout, gpost)


FF_TILE = 256
_GELU_C = math.sqrt(2.0 / math.pi)


def _gelu(x):
    return 0.5 * x * (1.0 + jnp.tanh(_GELU_C * (x + 0.044715 * x * x * x)))


def _gelu_with_grad(x):
    x2 = x * x
    t = jnp.tanh(_GELU_C * x * (1.0 + 0.044715 * x2))
    h = 0.5 * (1.0 + t)
    return x * h, h + (0.5 * _GELU_C) * x * (1.0 - t * t) * (1.0 + (3 * 0.044715) * x2)


def _conv_taps(u, hal_ref, first):
    row = lax.broadcasted_iota(jnp.int32, (8, 1), 0)
    h6 = jnp.where(first, 0.0, hal_ref[6:7, :])
    h7 = jnp.where(first, 0.0, hal_ref[7:8, :])
    r1, r2 = pltpu.roll(u, 1, 0), pltpu.roll(u, 2, 0)
    r1 = jnp.concatenate([jnp.where(row == 0, h7, r1[0:8]), r1[8:]], axis=0)
    r2 = jnp.concatenate([jnp.where(row == 0, h6, jnp.where(row == 1, h7, r2[0:8])), r2[8:]], axis=0)
    return r1, r2


def ffn_fwd(u0, convw, convb, wdown, gpost, x2):
    T = x2.shape[0]
    tm, tn = _tile(T, 1024), FF_TILE
    nj = D_FF // tn

    def body(ug_ref, uu_ref, hg_ref, hu_ref, wg_ref, wu_ref, bg_ref, bu_ref, wd_ref, gp_ref, x_ref,
             x3_ref, y_ref, aT_ref, acc):
        i, j = pl.program_id(0), pl.program_id(1)
        first = i == 0

        def conv(u_ref, h_ref, w_ref, b_ref):
            u = u_ref[...]
            r1, r2 = _conv_taps(u, h_ref, first)
            return b_ref[...] + w_ref[0:1, :] * r2 + w_ref[1:2, :] * r1 + w_ref[2:3, :] * u

        a = _gelu(conv(ug_ref, hg_ref, wg_ref, bg_ref)) * conv(uu_ref, hu_ref, wu_ref, bu_ref)
        aT_ref[...] = a.T.astype(MXU)
        _acc_out(acc, _dot(a, wd_ref[...]), j == 0)

        @pl.when(j == nj - 1)
        def _():
            y = acc[...]
            y_ref[...] = y
            x3_ref[...] = x_ref[...] + _rms_fwd(y, gp_ref[...])

    halo = lambda off: (lambda i, j: (jnp.maximum(i * (tm // 8) - 1, 0), off + j))
    return pl.pallas_call(
        body, name="ffn_fwd", grid=(T // tm, nj),
        in_specs=[pl.BlockSpec((tm, tn), lambda i, j: (i, j)), pl.BlockSpec((tm, tn), lambda i, j: (i, nj + j)),
                  pl.BlockSpec((8, tn), halo(0)), pl.BlockSpec((8, tn), halo(nj)),
                  pl.BlockSpec((3, tn), lambda i, j: (0, j)), pl.BlockSpec((3, tn), lambda i, j: (0, nj + j)),
                  pl.BlockSpec((1, tn), lambda i, j: (0, j)), pl.BlockSpec((1, tn), lambda i, j: (0, nj + j)),
                  pl.BlockSpec((tn, 1024), lambda i, j: (j, 0)),
                  pl.BlockSpec((1, 1024), lambda i, j: (0, 0)),
                  pl.BlockSpec((tm, 1024), lambda i, j: (i, 0))],
        out_specs=[pl.BlockSpec((tm, 1024), lambda i, j: (i, 0)), pl.BlockSpec((tm, 1024), lambda i, j: (i, 0)),
                   pl.BlockSpec((tn, tm), lambda i, j: (j, i))],
        out_shape=[jax.ShapeDtypeStruct((T, 1024), F32), jax.ShapeDtypeStruct((T, 1024), F32),
                   jax.ShapeDtypeStruct((D_FF, T), MXU)],
        scratch_shapes=[pltpu.VMEM((tm, 1024), F32)],
        compiler_params=_cparams(("parallel", "arbitrary")),
    )(u0, u0, u0, u0, convw, convw, convb, convb, wdown, gpost, x2)


def ffn_bwd(dx3, y, u0, convw, convb, wdown, gpost, wupT, x2, gfpre):
    T = dx3.shape[0]
    tm, tn = _tile(T, 512), FF_TILE
    nj = D_FF // tn
    ni = T // tm

    def body(dx_ref, y_ref, ug_ref, uu_ref, hg_ref, hu_ref, wg_ref, wu_ref, bg_ref, bu_ref, wd_ref, gp_ref,
             wtg_ref, wtu_ref, x2_ref, gf_ref,
             dy_ref, dug_ref, duu_ref, dcg_ref, dcu_ref, dgp_ref, dx2_ref, dgf_ref, dy_sc, dh_sc, cg, cu, ag, au):
        s, j = pl.program_id(0), pl.program_id(1)
        i = ni - 1 - s
        first_tok = i == 0
        sub = lax.broadcasted_iota(jnp.int32, (8, 1), 0)

        @pl.when(j == 0)
        def _():
            dy, dgp = _rms_bwd(dx_ref[...], y_ref[...], gp_ref[...])
            dy_sc[...] = dy.astype(MXU)
            dy_ref[...] = dy.astype(MXU)
            _acc_out(dgp_ref, dgp, s == 0)
            dh_sc[...] = jnp.zeros_like(dh_sc)

        @pl.when(s == 0)
        def _():
            cg[j] = jnp.zeros((8, tn), F32)
            cu[j] = jnp.zeros((8, tn), F32)
            ag[j] = jnp.zeros((8, tn), F32)
            au[j] = jnp.zeros((8, tn), F32)

        da = _dot_nt(dy_sc[...], wd_ref[...])

        def conv(u_ref, h_ref, w_ref, b_ref):
            u = u_ref[...]
            r1, r2 = _conv_taps(u, h_ref, first_tok)
            return b_ref[...] + w_ref[0:1, :] * r2 + w_ref[1:2, :] * r1 + w_ref[2:3, :] * u, u, r1, r2

        gate, ugv, g1, g2 = conv(ug_ref, hg_ref, wg_ref, bg_ref)
        up, uuv, u1, u2 = conv(uu_ref, hu_ref, wu_ref, bu_ref)
        gl, dgl = _gelu_with_grad(gate)
        dup = da * gl
        dgate = da * up * dgl

        def conv_bwd(du, u, r1, r2, w_ref, c_ref, a_ref, duT_ref, wt_ref):
            nxt = c_ref[j]
            n0, n1 = nxt[0:1, :], nxt[1:2, :]
            f1, f2 = pltpu.roll(du, tm - 1, 0), pltpu.roll(du, tm - 2, 0)
            f1 = jnp.concatenate([f1[:tm - 8], jnp.where(sub == 7, n0, f1[tm - 8:])], axis=0)
            f2 = jnp.concatenate([f2[:tm - 8], jnp.where(sub == 7, n1, jnp.where(sub == 6, n0, f2[tm - 8:]))], axis=0)
            du0 = w_ref[2:3, :] * du + w_ref[1:2, :] * f1 + w_ref[0:1, :] * f2
            duT_ref[...] = du0.T.astype(MXU)
            dh_sc[...] += _dot(du0, wt_ref[...])
            c_ref[j] = du[0:8, :]
            red = lambda v: jnp.sum(v, axis=0, keepdims=True)
            part = jnp.where(sub == 0, red(du * r2), jnp.where(sub == 1, red(du * r1), jnp.where(
                sub == 2, red(du * u), jnp.where(sub == 3, red(du), 0.0))))
            a_ref[j] = a_ref[j] + part
            return a_ref[j]

        dcg_ref[0] = conv_bwd(dgate, ugv, g1, g2, wg_ref, cg, ag, dug_ref, wtg_ref)
        dcu_ref[0] = conv_bwd(dup, uuv, u1, u2, wu_ref, cu, au, duu_ref, wtu_ref)

        @pl.when(j == nj - 1)
        def _():
            dx, dgf = _rms_bwd(dh_sc[...], x2_ref[...], gf_ref[...])
            dx2_ref[...] = dx_ref[...] + dx
            _acc_out(dgf_ref, dgf, s == 0)

    rev = lambda s: ni - 1 - s
    halo = lambda off: (lambda s, j: (jnp.maximum(rev(s) * (tm // 8) - 1, 0), off + j))
    tok = pl.BlockSpec((tm, 1024), lambda s, j: (rev(s), 0))
    vec = pl.BlockSpec((1, 1024), lambda s, j: (0, 0))
    return pl.pallas_call(
        body, name="ffn_bwd", grid=(ni, nj),
        in_specs=[tok, tok,
                  pl.BlockSpec((tm, tn), lambda s, j: (rev(s), j)), pl.BlockSpec((tm, tn), lambda s, j: (rev(s), nj + j)),
                  pl.BlockSpec((8, tn), halo(0)), pl.BlockSpec((8, tn), halo(nj)),
                  pl.BlockSpec((3, tn), lambda s, j: (0, j)), pl.BlockSpec((3, tn), lambda s, j: (0, nj + j)),
                  pl.BlockSpec((1, tn), lambda s, j: (0, j)), pl.BlockSpec((1, tn), lambda s, j: (0, nj + j)),
                  pl.BlockSpec((tn, 1024), lambda s, j: (j, 0)), vec,
                  pl.BlockSpec((tn, 1024), lambda s, j: (j, 0)), pl.BlockSpec((tn, 1024), lambda s, j: (nj + j, 0)),
                  tok, vec],
        out_specs=[tok,
                   pl.BlockSpec((tn, tm), lambda s, j: (j, rev(s))), pl.BlockSpec((tn, tm), lambda s, j: (j, rev(s))),
                   pl.BlockSpec((1, 8, tn), lambda s, j: (s, 0, j)), pl.BlockSpec((1, 8, tn), lambda s, j: (s, 0, j)),
                   vec, tok, vec],
        out_shape=[jax.ShapeDtypeStruct((T, 1024), MXU), jax.ShapeDtypeStruct((D_FF, T), MXU),
                   jax.ShapeDtypeStruct((D_FF, T), MXU),
                   jax.ShapeDtypeStruct((ni, 8, D_FF), F32), jax.ShapeDtypeStruct((ni, 8, D_FF), F32),
                   jax.ShapeDtypeStruct((1, 1024), F32), jax.ShapeDtypeStruct((T, 1024), F32),
                   jax.ShapeDtypeStruct((1, 1024), F32)],
        scratch_shapes=[pltpu.VMEM((tm, 1024), MXU), pltpu.VMEM((tm, 1024), F32)] + [pltpu.VMEM((nj, 8, tn), F32)] * 4,
        compiler_params=_cparams(("arbitrary", "arbitrary")),
    )(dx3, y, u0, u0, u0, u0, convw, convw, convb, convb, wdown, gpost, wupT, wupT, x2, gfpre)


ELEMS_PER_BLOCK = 512 * 1024


def _row_block(R, C):
    if R * C <= ELEMS_PER_BLOCK or R % 8:
        return R
    best = 8
    for t in range(8, R + 1, 8):
        if R % t == 0 and t * C <= ELEMS_PER_BLOCK:
            best = t
    return best


def adamw(w, g, m, v, name):
    L, R, C = w.shape
    partials = isinstance(g, (list, tuple))
    tr = _row_block(R, 2 * C)
    c1 = 1.0 - ADAM_B1 ** ADAM_STEP
    c2 = 1.0 - ADAM_B2 ** ADAM_STEP

    def body(w_ref, *rest):
        g_refs, (m_ref, v_ref, g_out, d_ref, nm_ref, nv_ref) = rest[:-6], rest[-6:]

        def step(gv):
            g_out[0] = gv
            nm = ADAM_B1 * m_ref[0] + (1.0 - ADAM_B1) * gv
            nv = ADAM_B2 * v_ref[0] + (1.0 - ADAM_B2) * (gv * gv)
            nm_ref[0] = nm
            nv_ref[0] = nv
            d_ref[0] = -ADAM_LR * ((nm / c1) / (jnp.sqrt(nv / c2) + ADAM_EPS) + ADAM_WD * w_ref[0])

        if not partials:
            step(g_refs[0][0])
            return
        for k in range(L):
            @pl.when(pl.program_id(0) == k)
            def _(k=k):
                gv = g_refs[k][0].astype(F32)
                for d in range(1, N_DEV):
                    gv = gv + g_refs[k][d].astype(F32)
                step(gv)

    spec = pl.BlockSpec((1, tr, C), lambda l, i: (l, i, 0))
    if partials:
        gspecs = [pl.BlockSpec((N_DEV, tr, C), lambda l, i, k=k: (0, jnp.where(l == k, i, 0), 0)) for k in range(L)]
        gs = list(g)
    else:
        gspecs, gs = [spec], [g]
    return pl.pallas_call(
        body, name=name, grid=(L, R // tr), in_specs=[spec] + gspecs + [spec, spec], out_specs=[spec] * 4,
        out_shape=[jax.ShapeDtypeStruct((L, R, C), F32)] * 4,
        compiler_params=_cparams(("arbitrary", "arbitrary")),
    )(w, *gs, m, v)


def sum_devices(buf, name):
    _, R, C = buf.shape
    tr = _row_block(R, C * 4)

    def body(b_ref, o_ref):
        acc = b_ref[0].astype(F32)
        for d in range(1, N_DEV):
            acc = acc + b_ref[d].astype(F32)
        o_ref[...] = acc

    return pl.pallas_call(
        body, name=name, grid=(R // tr,),
        in_specs=[pl.BlockSpec((N_DEV, tr, C), lambda i: (0, i, 0))],
        out_specs=pl.BlockSpec((tr, C), lambda i: (i, 0)),
        out_shape=jax.ShapeDtypeStruct((R, C), F32),
        compiler_params=_cparams(("parallel",)),
    )(buf)


def _exchange_copies(src_refs, out_refs, send_sems, recv_sems, gather):
    x, y, c = lax.axis_index("x"), lax.axis_index("y"), lax.axis_index("c")
    me = 4 * x + 2 * y + c
    flip = lambda a, bit: 1 - a if bit else a
    part = lambda ref, d: ref if gather else ref.at[d]
    copies = []
    for k in range(1, N_DEV):
        px, py, pc = flip(x, (k >> 2) & 1), flip(y, (k >> 1) & 1), flip(c, k & 1)
        peer = 4 * px + 2 * py + pc
        for t in range(len(src_refs)):
            sem = t * (N_DEV - 1) + k - 1
            mk = lambda s, d: pltpu.make_async_remote_copy(
                src_ref=s, dst_ref=d, send_sem=send_sems.at[sem], recv_sem=recv_sems.at[sem],
                device_id=(px, py, pc), device_id_type=pl.DeviceIdType.MESH)
            copies.append((mk(part(src_refs[t], peer), out_refs[t].at[me]),
                           mk(part(src_refs[t], me), out_refs[t].at[peer])))
    return me, copies


def exchange(srcs, name, gather):
    n = len(srcs)
    shapes = [(N_DEV,) + s.shape if gather else s.shape for s in srcs]

    def body(*refs):
        src_refs, out_refs = refs[:n], refs[n:2 * n]
        send_sems, recv_sems, local_sems = refs[2 * n:]
        me, copies = _exchange_copies(src_refs, out_refs, send_sems, recv_sems, gather)
        for outgoing, _ in copies:
            outgoing.start()
        mine = [pltpu.make_async_copy(src_refs[t] if gather else src_refs[t].at[me], out_refs[t].at[me],
                                      local_sems.at[t]) for t in range(n)]
        for cp in mine:
            cp.start()
        for _, incoming in copies:
            incoming.wait_recv()
        for outgoing, _ in copies:
            outgoing.wait_send()
        for cp in mine:
            cp.wait()

    return pl.pallas_call(
        body, name=name,
        in_specs=[pl.BlockSpec(memory_space=pl.ANY)] * n, out_specs=[pl.BlockSpec(memory_space=pl.ANY)] * n,
        out_shape=[jax.ShapeDtypeStruct(shp, s.dtype) for shp, s in zip(shapes, srcs)],
        scratch_shapes=[pltpu.SemaphoreType.DMA((n * (N_DEV - 1),)), pltpu.SemaphoreType.DMA((n * (N_DEV - 1),)),
                        pltpu.SemaphoreType.DMA((n,))],
    )(*srcs)


def hosted_exchange(body, n_in, n_out, n_scratch, grid, srcs, gather):
    n = len(srcs)
    shapes = [(N_DEV,) + s.shape if gather else s.shape for s in srcs]

    def wrapped(*refs):
        ins, xin = refs[:n_in], refs[n_in:n_in + n]
        outs = refs[n_in + n:n_in + n + n_out]
        xout = refs[n_in + n + n_out:n_in + 2 * n + n_out]
        rest = refs[n_in + 2 * n + n_out:]
        scratch, (send_sems, recv_sems, local_sems) = rest[:n_scratch], rest[n_scratch:]
        ids = [pl.program_id(a) for a in range(len(grid))]
        first = functools.reduce(jnp.logical_and, [i == 0 for i in ids])
        last = functools.reduce(jnp.logical_and, [i == g - 1 for i, g in zip(ids, grid)])
        me, copies = _exchange_copies(xin, xout, send_sems, recv_sems, gather)
        mine = [pltpu.make_async_copy(xin[t] if gather else xin[t].at[me], xout[t].at[me], local_sems.at[t])
                for t in range(n)]

        @pl.when(first)
        def _():
            for outgoing, _ in copies:
                outgoing.start()
            for cp in mine:
                cp.start()

        body(*ins, *outs, *scratch)

        @pl.when(last)
        def _():
            for _, incoming in copies:
                incoming.wait_recv()
            for outgoing, _ in copies:
                outgoing.wait_send()
            for cp in mine:
                cp.wait()

    any_spec = pl.BlockSpec(memory_space=pl.ANY)
    return wrapped, (list(srcs), [any_spec] * n, [any_spec] * n,
                     [jax.ShapeDtypeStruct(shp, s.dtype) for shp, s in zip(shapes, srcs)],
                     [pltpu.SemaphoreType.DMA((n * (N_DEV - 1),)), pltpu.SemaphoreType.DMA((n * (N_DEV - 1),)),
                      pltpu.SemaphoreType.DMA((n,))])


def _pack(parts, cols, row_align, dtype):
    flat = jnp.concatenate([p.astype(dtype) for p in parts], axis=-1)
    n = flat.shape[-1]
    block = cols * row_align
    total = -(-n // block) * block
    flat = jnp.pad(flat, [(0, 0)] * (flat.ndim - 1) + [(0, total - n)])
    return flat.reshape(flat.shape[:-1] + (total // cols, cols))


def _unpack(buf, shapes):
    lead = buf.shape[:-2]
    flat = buf.reshape(lead + (-1,))
    out, off = [], 0
    for s in shapes:
        n = int(np.prod(s))
        out.append(flat[..., off:off + n].reshape(lead + tuple(s)))
        off += n
    return out


SHARD_SHAPES = [(128, IN_COLS), (256, 48), (128, 64), (128, 1024), (1024, 704), (352, 1024)]
SHARDED = ["w_in", "w_uq", "w_ukv", "w_out", "w_up", "w_down"]
ATTN_SENT = ["w_in_p", "w_uq", "w_ukv", "w_out"]
FFN_SIDE = ["w_upT", "conv_w", "w_down"]


def _full_from_shards(name, s):
    if name in ("w_in", "w_in_p", "w_out", "w_down", "w_upT"):
        return s.reshape((-1, s.shape[-1]))
    return s.transpose(1, 0, 2).reshape((s.shape[1], -1))


def _shards_from_full(name, f):
    if name in ("w_in", "w_in_p", "w_out", "w_down", "w_upT"):
        return f.reshape((N_DEV, -1, f.shape[-1]))
    return f.reshape((f.shape[0], N_DEV, -1)).transpose(1, 0, 2)


def _perm_w_in(w):
    z = lambda n: jnp.zeros(w.shape[:-1] + (n,), w.dtype)
    return jnp.concatenate([w[..., :1536], w[..., 1540:1924], w[..., 1536:1540], z(60), w[..., 1924:1956], z(32)],
                           axis=-1)


def _unperm_w_in(d):
    return jnp.concatenate([d[..., :1536], d[..., 1920:1924], d[..., 1536:1920], d[..., 1984:2016]], axis=-1)


def _perm_w_uq(w):
    return jnp.pad(w.reshape(256, 4, MLA_QK_DIM), ((0, 0), (0, 0), (0, 128 - MLA_QK_DIM))).reshape(256, 512)


def _unperm_w_uq(d):
    return d.reshape(256, 4, 128)[:, :, :MLA_QK_DIM].reshape(256, 4 * MLA_QK_DIM)


def _perm_w_ukv(w):
    w4 = w.reshape(128, 4, 128)
    k = jnp.pad(w4[:, :, :64], ((0, 0), (0, 0), (0, 64))).reshape(128, 512)
    return jnp.concatenate([k, w4[:, :, 64:].reshape(128, 256)], axis=1)


def _unperm_w_ukv(d):
    dk = d[:, :512].reshape(128, 4, 128)[:, :, :64]
    dv = d[:, 512:].reshape(128, 4, 64)
    return jnp.concatenate([dk, dv], axis=-1).reshape(128, 512)


def _row(v, width=None):
    v = v.reshape(1, -1).astype(F32)
    if width is not None and v.shape[1] < width:
        v = jnp.pad(v, ((0, 0), (0, width - v.shape[1])))
    return v


def _layer_fwd(x, P, shared, send=None, ffn_from=None):
    cosr, sinr, bias = shared
    ex = (lambda part: None) if send is None else (lambda part: (send[part], True))
    proj, hT, projb = norm_matmul(x, P["g_pre"], P["w_in_p"], "in_proj", lo_tiles=C_CQ // 512)
    qm, km, vm, cqT, ckvT = mla_prep(proj, P["gq"], P["gkv"], P["w_uq_p"], P["w_ukv_p"], cosr, sinr)
    fcol, frow, frep = fox_gate(proj, P["fbias"])
    oa, lse_a = swa_fwd(proj, bias, P["sinks"])
    (ob, lrb), got_a = flash_fwd(projb, projb, projb, frep, frow, qblk=C_QF // 128, kblk=C_KF // 128,
                                 vblk=C_VF // 128, nq=1, scale=HEAD_DIM ** -0.5, name="fox_fwd", exch=ex(0))
    (oc, lrc), got_b = flash_fwd(qm, km, vm, None, None, qblk=0, kblk=0, vblk=0, nq=2,
                                 scale=MLA_QK_DIM ** -0.5, name="mla_fwd", exch=ex(1))
    x2, y1, mT = attn_out(oa, ob, oc, P["gn"], P["w_out"], P["g_apost"], x)
    if ffn_from is not None:
        P = dict(P, **ffn_from(got_a, got_b))
    u0, h2 = norm_matmul(x2, P["g_fpre"], P["w_upT"], "up_proj", tn_pref=1536, w_transposed=True,
                         h_transposed=False)
    x3, y2, aT = ffn_fwd(u0, P["conv_w"], P["conv_b"], P["w_down"], P["g_fpost"], x2)
    S = dict(x=x, proj=proj, projb=projb, hT=hT, qm=qm, km=km, vm=vm, cqT=cqT, ckvT=ckvT, fcol=fcol, frow=frow,
             oa=oa, lse_a=lse_a, ob=ob, lrb=lrb, oc=oc, lrc=lrc,
             x2=x2, y1=y1, mT=mT, u0=u0, h2=h2, y2=y2, aT=aT)
    return x3, S, P, (got_a, got_b)


def _layer_bwd(dx3, P, S, shared, send_attn=None):
    cosr, sinr, bias = shared
    proj = S["proj"]
    G = {}
    dy2, dugT, duuT, dcg, dcu, G["ffn_post_norm"], dx2, G["ffn_pre_norm"] = ffn_bwd(
        dx3, S["y2"], S["u0"], P["conv_w"], P["conv_b"], P["w_down"], P["g_fpost"], P["w_upT"], S["x2"], P["g_fpre"])
    dconv = jnp.concatenate([dcg[-1], dcu[-1]], axis=1)
    G["conv_w"], G["conv_b"] = dconv[0:3], dconv[3]
    G["w_down"] = matmul_nn(S["aT"], dy2, "dw_down", MXU)
    G["w_upT"] = jnp.concatenate([matmul_nn(dugT, S["h2"], "dw_up_gate", MXU),
                                  matmul_nn(duuT, S["h2"], "dw_up_up", MXU)], axis=0)
    G["w_up"] = G["w_upT"].T
    dy1, doa, dob, doc, G["group_norm"], G["attn_post_norm"] = attn_out_bwd(
        dx2, S["y1"], S["oa"], S["ob"], S["oc"], P["gn"], P["w_out"], P["g_apost"])
    G["w_out"] = matmul_nn(S["mT"], dy1, "dw_out", MXU)
    dqa, dka, dva, dbias, dsk = swa_bwd(proj, bias, P["sinks"], doa, S["oa"], S["lse_a"])
    G["swa_sinks"] = dsk[:, 0]
    pb = S["projb"]
    if send_attn is None:
        ex = lambda part: None
    else:
        parts = ([_shards_from_full("w_down", G["w_down"])] + send_attn, [_shards_from_full("w_upT", G["w_upT"])])
        ex = lambda part: (parts[part], False)
    (dqf, dkf, dvf, dFk, dFq), got_a = flash_bwd(
        pb, pb, pb, dob, S["ob"], S["lrb"], S["fcol"], S["frow"], name="fox_bwd", qblk=C_QF // 128,
        kblk=C_KF // 128, vblk=C_VF // 128, nq=1, scale=HEAD_DIM ** -0.5, exch=ex(0))
    dmisc_f, dfb = fox_gate_bwd(dFq, dFk, proj, P["fbias"])
    G["forget_bias"] = dfb[0, 0:4]
    (dqm_, dkm_, dvm_), got_b = flash_bwd(
        S["qm"], S["km"], S["vm"], doc, S["oc"], S["lrc"], None, None, name="mla_bwd",
        qblk=0, kblk=0, vblk=0, nq=2, scale=MLA_QK_DIM ** -0.5, exch=ex(1))
    dqm, dkv, dcq, dckv, dmisc_r, G["q_latent_norm"], G["kv_latent_norm"] = mla_prep_bwd(
        dqm_, dkm_, dvm_, proj, P["gq"], P["gkv"], P["w_uq_p"], P["w_ukv_p"], cosr, sinr)
    G["w_uq"] = _unperm_w_uq(matmul_nn(S["cqT"], dqm, "dw_uq", MXU))
    G["w_ukv"] = _unperm_w_ukv(matmul_nn(S["ckvT"], dkv, "dw_ukv", MXU))
    dproj = jnp.concatenate([dqa, dka, dva, dqf, dkf, dvf, dcq, dckv, dmisc_f + dmisc_r], axis=1).astype(MXU)
    G["w_in_p"] = matmul_nn(S["hT"], dproj, "dw_in", MXU)
    G["w_in"] = _unperm_w_in(G["w_in_p"])
    dx, G["attn_pre_norm"] = matmul_nt_normbwd(dproj, P["w_in_p"], S["x"], P["g_pre"], dx2, "in_bwd")
    return dx, G, dbias, (got_a, got_b)


def _layer_params(l, full, small):
    return dict(
        g_pre=_row(small["attn_pre_norm"][l]),
        w_in_p=full["w_in_p"] if "w_in_p" in full else _perm_w_in(full["w_in"]),
        gq=_row(small["q_latent_norm"][l]), gkv=_row(small["kv_latent_norm"][l]),
        w_uq_p=_perm_w_uq(full["w_uq"]), w_ukv_p=_perm_w_ukv(full["w_ukv"]),
        fbias=_row(small["forget_bias"][l], 128), sinks=small["swa_sinks"][l].astype(F32),
        gn=_row(small["group_norm"][l]), w_out=full["w_out"], g_apost=_row(small["attn_post_norm"][l]),
        g_fpre=_row(small["ffn_pre_norm"][l]), conv_b=_row(small["conv_b"][l]),
        g_fpost=_row(small["ffn_post_norm"][l]),
        **{n: full[n] for n in FFN_SIDE if n in full},
        **({"w_upT": full["w_up"].T} if "w_up" in full else {}))


def _rel_bias_grad(dbias, bucket):
    flat = dbias.reshape(SWA_Q_HEADS, -1)
    hi = flat.astype(MXU)
    lo = (flat - hi.astype(F32)).astype(MXU)
    onehot = (bucket[:, None] == jnp.arange(128, dtype=jnp.int32)[None, :]).astype(MXU)
    r = matmul_nn(jnp.concatenate([hi, lo], axis=0), onehot, "rel_bias_grad")
    return (r[0:8] + r[8:16])[:, :REL_BUCKETS].T


def local_step(x, tgt, fulls, small, comm=None):
    T = x.shape[0]
    cosr, sinr = rope_tables(T)
    bias, bucket = swa_bias_table(small["rel_bias"])
    shared = (cosr, sinr, bias)
    Ps, Ss = [], []
    h, full = x, fulls[0]
    for l in range(DEPTH):
        P = _layer_params(l, full, small)
        if comm:
            h, S, P, got = _layer_fwd(h, P, shared, comm["weight_parts"](l), comm["ffn_from"])
            full = comm["attn_from"](got) if l + 1 < DEPTH else None
        else:
            h, S, P, _ = _layer_fwd(h, P, shared)
            full = fulls[l + 1] if l + 1 < DEPTH else None
        Ps.append(P)
        Ss.append(S)
    dh, sq = loss_kernel(h, tgt)
    grads = [None] * DEPTH
    dbias_sum = None
    pending = [] if comm else None
    for l in reversed(range(DEPTH)):
        dh, grads[l], dbias, got = _layer_bwd(dh, Ps[l], Ss[l], shared, pending)
        dbias_sum = dbias if dbias_sum is None else dbias_sum + dbias
        if comm:
            comm["landed"](l, ["w_down"], got[0][:1])
            comm["landed"](l, ["w_upT"], got[1])
            if pending:
                comm["landed"](l + 1, ATTN_SENT, got[0][1:])
            pending = [_shards_from_full(n, grads[l][n]) for n in ATTN_SENT]
    return sq, dh, grads, _rel_bias_grad(dbias_sum, bucket), pending


WEIGHTS = ['attn_pre_norm', 'w_in', 'forget_bias', 'swa_sinks', 'rel_bias', 'q_latent_norm', 'w_uq',
           'kv_latent_norm', 'w_ukv', 'group_norm', 'w_out', 'attn_post_norm', 'ffn_pre_norm', 'w_up', 'conv_w',
           'conv_b', 'w_down', 'ffn_post_norm']
SMALL_PER_LAYER = ['attn_pre_norm', 'forget_bias', 'swa_sinks', 'q_latent_norm', 'kv_latent_norm', 'group_norm',
                   'attn_post_norm', 'ffn_pre_norm', 'conv_b', 'ffn_post_norm', 'conv_w']


def kernel(x, attn_pre_norm, w_in, forget_bias, swa_sinks, rel_bias, q_latent_norm, w_uq, kv_latent_norm, w_ukv, group_norm, w_out, attn_post_norm, ffn_pre_norm, w_up, conv_w, conv_b, w_down, ffn_post_norm, loss_target, m_attn_pre_norm, m_w_in, m_forget_bias, m_swa_sinks, m_rel_bias, m_q_latent_norm, m_w_uq, m_kv_latent_norm, m_w_ukv, m_group_norm, m_w_out, m_attn_post_norm, m_ffn_pre_norm, m_w_up, m_conv_w, m_conv_b, m_w_down, m_ffn_post_norm, v_attn_pre_norm, v_w_in, v_forget_bias, v_swa_sinks, v_rel_bias, v_q_latent_norm, v_w_uq, v_kv_latent_norm, v_w_ukv, v_group_norm, v_w_out, v_attn_post_norm, v_ffn_pre_norm, v_w_up, v_conv_w, v_conv_b, v_w_down, v_ffn_post_norm):
    W = dict(attn_pre_norm=attn_pre_norm, w_in=w_in, forget_bias=forget_bias, swa_sinks=swa_sinks, rel_bias=rel_bias,
             q_latent_norm=q_latent_norm, w_uq=w_uq, kv_latent_norm=kv_latent_norm, w_ukv=w_ukv,
             group_norm=group_norm, w_out=w_out, attn_post_norm=attn_post_norm, ffn_pre_norm=ffn_pre_norm,
             w_up=w_up, conv_w=conv_w, conv_b=conv_b, w_down=w_down, ffn_post_norm=ffn_post_norm)
    M = dict(attn_pre_norm=m_attn_pre_norm, w_in=m_w_in, forget_bias=m_forget_bias, swa_sinks=m_swa_sinks,
             rel_bias=m_rel_bias, q_latent_norm=m_q_latent_norm, w_uq=m_w_uq, kv_latent_norm=m_kv_latent_norm,
             w_ukv=m_w_ukv, group_norm=m_group_norm, w_out=m_w_out, attn_post_norm=m_attn_post_norm,
             ffn_pre_norm=m_ffn_pre_norm, w_up=m_w_up, conv_w=m_conv_w, conv_b=m_conv_b, w_down=m_w_down,
             ffn_post_norm=m_ffn_post_norm)
    V = dict(attn_pre_norm=v_attn_pre_norm, w_in=v_w_in, forget_bias=v_forget_bias, swa_sinks=v_swa_sinks,
             rel_bias=v_rel_bias, q_latent_norm=v_q_latent_norm, w_uq=v_w_uq, kv_latent_norm=v_kv_latent_norm,
             w_ukv=v_w_ukv, group_norm=v_group_norm, w_out=v_w_out, attn_post_norm=v_attn_post_norm,
             ffn_pre_norm=v_ffn_pre_norm, w_up=v_w_up, conv_w=v_conv_w, conv_b=v_conv_b, w_down=v_w_down,
             ffn_post_norm=v_ffn_post_norm)
    me = 4 * lax.axis_index("x") + 2 * lax.axis_index("y") + lax.axis_index("c")

    def attn_shards(l):
        return [_perm_w_in(w_in[l].astype(MXU))] + [W[n][l].astype(MXU) for n in ATTN_SENT[1:]]

    def weight_parts(l):
        return ([W["w_down"][l].astype(MXU)] + (attn_shards(l + 1) if l + 1 < DEPTH else []),
                [jnp.swapaxes(W["w_up"][l], 0, 1).astype(MXU), conv_w[l]])

    def ffn_from(got_a, got_b):
        return dict(w_down=_full_from_shards("w_down", got_a[0]), w_upT=_full_from_shards("w_upT", got_b[0]),
                    conv_w=got_b[1].transpose(1, 0, 2).reshape(3, 2 * D_FF))

    def attn_from(got):
        return {n: _full_from_shards(n, s) for n, s in zip(ATTN_SENT, got[0][1:])}

    landed = [{} for _ in range(DEPTH)]

    def on_landed(l, names, arrays):
        landed[l].update(zip(names, arrays))

    comm = dict(weight_parts=weight_parts, ffn_from=ffn_from, attn_from=attn_from, landed=on_landed)
    full0 = dict(zip(ATTN_SENT, map(_full_from_shards, ATTN_SENT, exchange(attn_shards(0), "gather_weights", True))))
    sq, dx, grads, drel, last = local_step(x[0], loss_target[0], [full0], W, comm)
    on_landed(0, ATTN_SENT, exchange(last, "scatter_grads", False))
    for l in range(DEPTH):
        landed[l]["w_in"] = _unperm_w_in(landed[l]["w_in_p"])
    G = {}

    parts, shapes = [], []
    for l in range(DEPTH):
        for n in SMALL_PER_LAYER:
            parts.append(grads[l][n].astype(F32).reshape(-1))
            shapes.append(grads[l][n].shape)
    parts += [drel.reshape(-1), jnp.sum(sq).reshape(1) * (0.5 / D_MODEL)]
    shapes += [drel.shape, (1,)]
    red = _unpack(sum_devices(exchange([_pack(parts, 128, 8, F32)], "gather_small", True)[0], "sum_small"), shapes)
    k = 0
    per = {n: [] for n in SMALL_PER_LAYER}
    for l in range(DEPTH):
        for n in SMALL_PER_LAYER:
            per[n].append(red[k])
            k += 1
    for n in SMALL_PER_LAYER:
        G[n] = jnp.stack(per[n]).reshape((DEPTH, 3, 2 * D_FF) if n == "conv_w" else W[n].shape)
    G["rel_bias"] = red[k]
    loss = red[k + 1][0]
    G["conv_w"] = lax.dynamic_slice_in_dim(G["conv_w"], me * 704, 704, axis=2)

    delta, new_m, new_v = {}, {}, {}
    for n in WEIGHTS:
        shp = W[n].shape
        if n == "w_up":
            v3 = lambda a: jnp.swapaxes(a, 1, 2)
            back = v3
            g = [landed[l]["w_upT"] for l in range(DEPTH)]
        else:
            v3 = lambda a: a.reshape(shp if len(shp) == 3 else (1,) + shp)
            back = lambda a: a.reshape(shp)
            g = [landed[l][n] for l in range(DEPTH)] if n in SHARDED else v3(G[n])
        g, d, nm, nv = adamw(v3(W[n]), g, v3(M[n]), v3(V[n]), "adamw_" + n)
        G[n], delta[n], new_m[n], new_v[n] = back(g), back(d), back(nm), back(nv)
    return (loss, dx[None], *[G[n] for n in WEIGHTS], *[delta[n] for n in WEIGHTS],
            *[new_m[n] for n in WEIGHTS], *[new_v[n] for n in WEIGHTS])
```

```python
import functools
import math

import numpy as np
import jax
import jax.numpy as jnp
from jax import lax
from jax.experimental import pallas as pl
from jax.experimental.pallas import tpu as pltpu

F32 = jnp.float32
MXU = jnp.bfloat16

N_DEV = 8
DEPTH = 4
D_MODEL = 1024
HEAD_DIM = 64
WINDOW = 128
SWA_Q_HEADS = 8
REL_BUCKETS = 32
REL_MAX_DIST = 128
MLA_QK_DIM = 96
ROPE_DIM = 32
ROPE_THETA = 10000.0
D_FF = 2816
EPS = 1e-6
NEG = -1e30
IN_COLS = 1956
IN_COLS_P = 2048
C_QA, C_KA, C_VA = 0, 512, 640
C_QF, C_KF, C_VF = 768, 1024, 1280
C_CQ, C_CKV, C_MISC = 1536, 1792, 1920
ROPE_LANE0 = 64
ADAM_LR, ADAM_B1, ADAM_B2, ADAM_EPS, ADAM_WD, ADAM_STEP = 0.001, 0.9, 0.999, 1e-08, 0.01, 10

VMEM_LIMIT = 56 * 1024 * 1024
PACK_COLS = 1024
PACK_ROW_ALIGN = 16


def _cparams(sem=None):
    return pltpu.CompilerParams(dimension_semantics=sem, vmem_limit_bytes=VMEM_LIMIT)


def _tile(n, pref):
    if n <= pref:
        return n
    t = pref - pref % 128
    while t >= 128:
        if n % t == 0:
            return t
        t -= 128
    return n


def _dot(a, b):
    return jnp.dot(a.astype(MXU), b.astype(MXU), preferred_element_type=F32)


def _dot_nt(a, b):
    return lax.dot_general(a.astype(MXU), b.astype(MXU), (((1,), (1,)), ((), ())),
                           preferred_element_type=F32)


def _rms_fwd(x, g):
    return x * lax.rsqrt(jnp.mean(x * x, axis=-1, keepdims=True) + EPS) * g


def _rms_bwd(dy, x, g, n=None):
    r = lax.rsqrt(jnp.mean(x * x, axis=-1, keepdims=True) + EPS)
    xh = x * r
    dg = jnp.sum(dy * xh, axis=0, keepdims=True)
    dxh = dy * g
    dx = r * (dxh - xh * jnp.mean(dxh * xh, axis=-1, keepdims=True))
    return dx, dg


def _acc_out(ref, val, first):
    @pl.when(first)
    def _():
        ref[...] = val

    @pl.when(jnp.logical_not(first))
    def _():
        ref[...] += val


def norm_matmul(x, g, w, name, lo_tiles=0, tn_pref=512, w_transposed=False, h_transposed=True):
    T, K = x.shape
    N = w.shape[0] if w_transposed else w.shape[1]
    tm, tn = _tile(T, 1024), _tile(N, tn_pref)

    def body(x_ref, g_ref, w_ref, o_ref, hT_ref, *rest):
        h_sc = rest[-1]
        j = pl.program_id(1)

        @pl.when(j == 0)
        def _():
            h = _rms_fwd(x_ref[...], g_ref[...])
            h_sc[...] = h.astype(MXU)
            hT_ref[...] = (h.T if h_transposed else h).astype(MXU)

        r = (_dot_nt if w_transposed else _dot)(h_sc[...], w_ref[...])
        o_ref[...] = r
        if lo_tiles:
            @pl.when(j < lo_tiles)
            def _():
                rest[0][...] = r.astype(MXU)

    h_spec = pl.BlockSpec((K, tm), lambda i, j: (0, i)) if h_transposed else pl.BlockSpec((tm, K), lambda i, j: (i, 0))
    out_specs = [pl.BlockSpec((tm, tn), lambda i, j: (i, j)), h_spec]
    out_shape = [jax.ShapeDtypeStruct((T, N), F32), jax.ShapeDtypeStruct((K, T) if h_transposed else (T, K), MXU)]
    if lo_tiles:
        out_specs.append(pl.BlockSpec((tm, tn), lambda i, j: (i, jnp.minimum(j, lo_tiles - 1))))
        out_shape.append(jax.ShapeDtypeStruct((T, lo_tiles * tn), MXU))
    return pl.pallas_call(
        body, name=name, grid=(T // tm, N // tn),
        in_specs=[pl.BlockSpec((tm, K), lambda i, j: (i, 0)),
                  pl.BlockSpec((1, K), lambda i, j: (0, 0)),
                  pl.BlockSpec((tn, K), lambda i, j: (j, 0)) if w_transposed else
                  pl.BlockSpec((K, tn), lambda i, j: (0, j))],
        out_specs=out_specs, out_shape=out_shape,
        scratch_shapes=[pltpu.VMEM((tm, K), MXU)],
        compiler_params=_cparams(("parallel", "arbitrary")),
    )(x, g, w)


def matmul_nn(a, b, name, out_dtype=F32):
    M, K = a.shape
    N = b.shape[1]
    tm, tn, tk = _tile(M, 1408), _tile(N, 1536), _tile(K, 1024)
    nk = K // tk

    def body(a_ref, b_ref, o_ref, acc):
        k = pl.program_id(2)
        part = _dot(a_ref[...], b_ref[...])
        _acc_out(acc, part, k == 0)

        @pl.when(k == nk - 1)
        def _():
            o_ref[...] = acc[...].astype(out_dtype)

    return pl.pallas_call(
        body, name=name, grid=(M // tm, N // tn, nk),
        in_specs=[pl.BlockSpec((tm, tk), lambda i, j, k: (i, k)),
                  pl.BlockSpec((tk, tn), lambda i, j, k: (k, j))],
        out_specs=pl.BlockSpec((tm, tn), lambda i, j, k: (i, j)),
        out_shape=jax.ShapeDtypeStruct((M, N), out_dtype),
        scratch_shapes=[pltpu.VMEM((tm, tn), F32)],
        compiler_params=_cparams(("parallel", "parallel", "arbitrary")),
    )(a, b)


def matmul_nt_normbwd(dy, w, x, g, dres, name):
    T, N = dy.shape
    K = w.shape[0]
    tm, tn = _tile(T, 1024), _tile(N, 1536)
    nj = N // tn

    def body(dy_ref, w_ref, x_ref, g_ref, dres_ref, dx_ref, dg_ref, acc):
        i, j = pl.program_id(0), pl.program_id(1)
        _acc_out(acc, _dot_nt(dy_ref[...], w_ref[...]), j == 0)

        @pl.when(j == nj - 1)
        def _():
            dx, dg = _rms_bwd(acc[...], x_ref[...], g_ref[...])
            dx_ref[...] = dres_ref[...] + dx
            _acc_out(dg_ref, dg, i == 0)

    return pl.pallas_call(
        body, name=name, grid=(T // tm, nj),
        in_specs=[pl.BlockSpec((tm, tn), lambda i, j: (i, j)),
                  pl.BlockSpec((K, tn), lambda i, j: (0, j)),
                  pl.BlockSpec((tm, K), lambda i, j: (i, 0)),
                  pl.BlockSpec((1, K), lambda i, j: (0, 0)),
                  pl.BlockSpec((tm, K), lambda i, j: (i, 0))],
        out_specs=[pl.BlockSpec((tm, K), lambda i, j: (i, 0)),
                   pl.BlockSpec((1, K), lambda i, j: (0, 0))],
        out_shape=[jax.ShapeDtypeStruct((T, K), F32), jax.ShapeDtypeStruct((1, K), F32)],
        scratch_shapes=[pltpu.VMEM((tm, K), F32)],
        compiler_params=_cparams(("arbitrary", "arbitrary")),
    )(dy, w, x, g, dres)


def loss_kernel(y, tgt):
    T, D = y.shape
    tm = _tile(T, 512)

    def body(y_ref, t_ref, dy_ref, acc_ref):
        e = y_ref[...] - t_ref[...]
        dy_ref[...] = e * (1.0 / D)
        _acc_out(acc_ref, jnp.sum(e * e, axis=0, keepdims=True), pl.program_id(0) == 0)

    return pl.pallas_call(
        body, name="loss", grid=(T // tm,),
        in_specs=[pl.BlockSpec((tm, D), lambda i: (i, 0)), pl.BlockSpec((tm, D), lambda i: (i, 0))],
        out_specs=[pl.BlockSpec((tm, D), lambda i: (i, 0)), pl.BlockSpec((1, D), lambda i: (0, 0))],
        out_shape=[jax.ShapeDtypeStruct((T, D), F32), jax.ShapeDtypeStruct((1, D), F32)],
        compiler_params=_cparams(("arbitrary",)),
    )(y, tgt)


def _rope_partner(x):
    lane = lax.broadcasted_iota(jnp.int32, (1, 128), 1)
    return jnp.where(lane < ROPE_LANE0 + ROPE_DIM // 2, pltpu.roll(x, 128 - ROPE_DIM // 2, 1),
                     pltpu.roll(x, ROPE_DIM // 2, 1))


def _rope_apply(x, cos, sin_signed):
    return x * cos + _rope_partner(x) * sin_signed


def _rope_apply_bwd(dy, cos, sin_signed):
    lane = lax.broadcasted_iota(jnp.int32, (1, 128), 1)
    rotary = (lane >= ROPE_LANE0) & (lane < ROPE_LANE0 + ROPE_DIM)
    return dy * cos + jnp.where(rotary, _rope_partner(dy * sin_signed), 0.0)


def rope_tables(T):
    pos = jnp.arange(T, dtype=F32)
    inv_freq = ROPE_THETA ** (-(jnp.arange(ROPE_DIM // 2, dtype=F32) * 2.0 / ROPE_DIM))
    ang = pos[:, None] * inv_freq[None, :]
    cos, sin = jnp.cos(ang), jnp.sin(ang)
    z = jnp.zeros((T, ROPE_LANE0), F32)
    z2 = jnp.zeros((T, 128 - ROPE_LANE0 - ROPE_DIM), F32)
    cosr = jnp.concatenate([z, cos, cos, z2], axis=1)
    sinr = jnp.concatenate([z, -sin, sin, z2], axis=1)
    return cosr, sinr


def mla_prep(proj, gq, gkv, wuq, wukv, cosr, sinr):
    T = proj.shape[0]
    tm = _tile(T, 512)

    def body(cq_ref, ckv_ref, misc_ref, gq_ref, gkv_ref, wuq_ref, wukv_ref, cos_ref, sin_ref,
             q_ref, k_ref, v_ref, cqT_ref, ckvT_ref):
        lane = lax.broadcasted_iota(jnp.int32, (1, 128), 1)
        cosr_, sinr_ = cos_ref[...], sin_ref[...]
        cosq = cosr_ + jnp.where(lane < ROPE_LANE0, 1.0, 0.0)
        cqn = _rms_fwd(cq_ref[...], gq_ref[...])
        cqT_ref[...] = cqn.T.astype(MXU)
        qm = _dot(cqn, wuq_ref[...])
        q_ref[...] = jnp.concatenate(
            [_rope_apply(qm[:, 128 * h:128 * (h + 1)], cosq, sinr_) for h in range(4)], axis=1).astype(MXU)
        ckvn = _rms_fwd(ckv_ref[...], gkv_ref[...])
        ckvT_ref[...] = ckvn.T.astype(MXU)
        kv = _dot(ckvn, wukv_ref[...])
        kr = _rope_apply(misc_ref[...], cosr_, sinr_)
        k_ref[...] = jnp.concatenate(
            [kv[:, 128 * h:128 * (h + 1)] + kr for h in range(4)], axis=1).astype(MXU)
        v_ref[...] = kv[:, 512:768].astype(MXU)

    row = lambda i: (i, 0)
    const = lambda i: (0, 0)
    return pl.pallas_call(
        body, name="mla_prep", grid=(T // tm,),
        in_specs=[pl.BlockSpec((tm, 256), lambda i: (i, C_CQ // 256)),
                  pl.BlockSpec((tm, 128), lambda i: (i, C_CKV // 128)),
                  pl.BlockSpec((tm, 128), lambda i: (i, C_MISC // 128)),
                  pl.BlockSpec((1, 256), const), pl.BlockSpec((1, 128), const),
                  pl.BlockSpec((256, 512), const), pl.BlockSpec((128, 768), const),
                  pl.BlockSpec((tm, 128), row), pl.BlockSpec((tm, 128), row)],
        out_specs=[pl.BlockSpec((tm, 512), row), pl.BlockSpec((tm, 512), row), pl.BlockSpec((tm, 256), row),
                   pl.BlockSpec((256, tm), lambda i: (0, i)), pl.BlockSpec((128, tm), lambda i: (0, i))],
        out_shape=[jax.ShapeDtypeStruct((T, 512), MXU), jax.ShapeDtypeStruct((T, 512), MXU),
                   jax.ShapeDtypeStruct((T, 256), MXU),
                   jax.ShapeDtypeStruct((256, T), MXU), jax.ShapeDtypeStruct((128, T), MXU)],
        compiler_params=_cparams(("parallel",)),
    )(proj, proj, proj, gq, gkv, wuq, wukv, cosr, sinr)


def mla_prep_bwd(dq, dk, dv, proj, gq, gkv, wuq, wukv, cosr, sinr):
    T = proj.shape[0]
    tm = _tile(T, 512)

    def body(dq_ref, dk_ref, dv_ref, cq_ref, ckv_ref, gq_ref, gkv_ref, wuq_ref, wukv_ref, cos_ref, sin_ref,
             dqm_ref, dkv_ref, dcq_ref, dckv_ref, dmisc_ref, dgq_ref, dgkv_ref):
        first = pl.program_id(0) == 0
        lane = lax.broadcasted_iota(jnp.int32, (1, 128), 1)
        cosr_, sinr_ = cos_ref[...], sin_ref[...]
        cosq = cosr_ + jnp.where(lane < ROPE_LANE0, 1.0, 0.0)
        dqv = dq_ref[...]
        dqm = jnp.concatenate(
            [_rope_apply_bwd(dqv[:, 128 * h:128 * (h + 1)], cosq, sinr_) for h in range(4)], axis=1)
        dqm_ref[...] = dqm.astype(MXU)
        dcq, dgq = _rms_bwd(_dot_nt(dqm, wuq_ref[...]), cq_ref[...], gq_ref[...])
        dcq_ref[...] = dcq
        _acc_out(dgq_ref, dgq, first)
        dkv_ = dk_ref[...]
        heads = [dkv_[:, 128 * h:128 * (h + 1)] for h in range(4)]
        dkr = heads[0] + heads[1] + heads[2] + heads[3]
        dmisc_ref[...] = _rope_apply_bwd(dkr, cosr_, sinr_)
        dkvm = jnp.concatenate([jnp.where(lane < ROPE_LANE0, hd, 0.0) for hd in heads] + [dv_ref[...]], axis=1)
        dkv_ref[...] = dkvm.astype(MXU)
        dckv, dgkv = _rms_bwd(_dot_nt(dkvm, wukv_ref[...]), ckv_ref[...], gkv_ref[...])
        dckv_ref[...] = dckv
        _acc_out(dgkv_ref, dgkv, first)

    row = lambda i: (i, 0)
    const = lambda i: (0, 0)
    return pl.pallas_call(
        body, name="mla_prep_bwd", grid=(T // tm,),
        in_specs=[pl.BlockSpec((tm, 512), row), pl.BlockSpec((tm, 512), row), pl.BlockSpec((tm, 256), row),
                  pl.BlockSpec((tm, 256), lambda i: (i, C_CQ // 256)),
                  pl.BlockSpec((tm, 128), lambda i: (i, C_CKV // 128)),
                  pl.BlockSpec((1, 256), const), pl.BlockSpec((1, 128), const),
                  pl.BlockSpec((256, 512), const), pl.BlockSpec((128, 768), const),
                  pl.BlockSpec((tm, 128), row), pl.BlockSpec((tm, 128), row)],
        out_specs=[pl.BlockSpec((tm, 512), row), pl.BlockSpec((tm, 768), row), pl.BlockSpec((tm, 256), row),
                   pl.BlockSpec((tm, 128), row), pl.BlockSpec((tm, 128), row),
                   pl.BlockSpec((1, 256), const), pl.BlockSpec((1, 128), const)],
        out_shape=[jax.ShapeDtypeStruct((T, 512), MXU), jax.ShapeDtypeStruct((T, 768), MXU),
                   jax.ShapeDtypeStruct((T, 256), F32), jax.ShapeDtypeStruct((T, 128), F32),
                   jax.ShapeDtypeStruct((T, 128), F32),
                   jax.ShapeDtypeStruct((1, 256), F32), jax.ShapeDtypeStruct((1, 128), F32)],
        compiler_params=_cparams(("arbitrary",)),
    )(dq, dk, dv, proj, proj, gq, gkv, wuq, wukv, cosr, sinr)


def _split3(x):
    hi = x.astype(MXU)
    r1 = x - hi.astype(F32)
    mid = r1.astype(MXU)
    lo = (r1 - mid.astype(F32)).astype(MXU)
    return hi, mid, lo


def _tri_matmul(tri, x):
    hi, mid, lo = _split3(x)
    d = lambda p: jnp.dot(tri, p, preferred_element_type=F32)
    return d(hi) + d(mid) + d(lo)


def _log_sigmoid(z):
    return jnp.minimum(z, 0.0) - jnp.log(1.0 + jnp.exp(-jnp.abs(z)))


def fox_gate(proj, fbias):
    T = proj.shape[0]
    tb = _tile(T, 512)

    def body(misc_ref, b_ref, fc_ref, fr_ref, frep_ref, carry):
        @pl.when(pl.program_id(0) == 0)
        def _():
            carry[...] = jnp.zeros_like(carry)

        lane = lax.broadcasted_iota(jnp.int32, (1, 128), 1)
        lf = jnp.where(lane < 4, _log_sigmoid(misc_ref[...] + b_ref[...]), 0.0)
        r = lax.broadcasted_iota(jnp.int32, (tb, tb), 0)
        c = lax.broadcasted_iota(jnp.int32, (tb, tb), 1)
        tri = jnp.where(r >= c, 1.0, 0.0).astype(MXU)
        F = _tri_matmul(tri, lf) + carry[...]
        carry[...] = carry[...] + jnp.sum(lf, axis=0, keepdims=True)
        fc_ref[0] = F
        fc_ref[1] = pltpu.roll(F, 126, 1)
        ft = F.T[0:8, :]
        fr_ref[0] = ft
        fr_ref[1] = pltpu.roll(ft, 6, 0)
        for h in range(4):
            frep_ref[h] = jnp.broadcast_to(_lane_pick(F, h), (tb, 128))

    return pl.pallas_call(
        body, name="fox_gate", grid=(T // tb,),
        in_specs=[pl.BlockSpec((tb, 128), lambda i: (i, C_MISC // 128)), pl.BlockSpec((1, 128), lambda i: (0, 0))],
        out_specs=[pl.BlockSpec((2, tb, 128), lambda i: (0, i, 0)), pl.BlockSpec((2, 8, tb), lambda i: (0, 0, i)),
                   pl.BlockSpec((4, tb, 128), lambda i: (0, i, 0))],
        out_shape=[jax.ShapeDtypeStruct((2, T, 128), F32), jax.ShapeDtypeStruct((2, 8, T), F32),
                   jax.ShapeDtypeStruct((4, T, 128), F32)],
        scratch_shapes=[pltpu.VMEM((1, 128), F32)],
        compiler_params=_cparams(("arbitrary",)),
    )(proj, fbias)


def fox_gate_bwd(dFq, dFk, proj, fbias):
    T = proj.shape[0]
    tb = _tile(T, 512)
    nb = T // tb

    def body(dq_ref, dk_ref, misc_ref, b_ref, dm_ref, db_ref, carry):
        first = pl.program_id(0) == 0

        @pl.when(first)
        def _():
            carry[...] = jnp.zeros_like(carry)

        lane = lax.broadcasted_iota(jnp.int32, (1, 128), 1)
        dF = jnp.where(lane < 4, (dq_ref[0] + dk_ref[0]) + pltpu.roll(dq_ref[1] + dk_ref[1], 2, 1), 0.0)
        r = lax.broadcasted_iota(jnp.int32, (tb, tb), 0)
        c = lax.broadcasted_iota(jnp.int32, (tb, tb), 1)
        tri = jnp.where(r <= c, 1.0, 0.0).astype(MXU)
        dlf = _tri_matmul(tri, dF) + carry[...]
        carry[...] = carry[...] + jnp.sum(dF, axis=0, keepdims=True)
        z = misc_ref[...] + b_ref[...]
        dz = jnp.where(lane < 4, dlf * (1.0 / (1.0 + jnp.exp(z))), 0.0)
        dm_ref[...] = dz
        _acc_out(db_ref, jnp.sum(dz, axis=0, keepdims=True), first)

    return pl.pallas_call(
        body, name="fox_gate_bwd", grid=(nb,),
        in_specs=[pl.BlockSpec((2, tb, 128), lambda i: (0, nb - 1 - i, 0)),
                  pl.BlockSpec((2, tb, 128), lambda i: (0, nb - 1 - i, 0)),
                  pl.BlockSpec((tb, 128), lambda i: (nb - 1 - i, C_MISC // 128)),
                  pl.BlockSpec((1, 128), lambda i: (0, 0))],
        out_specs=[pl.BlockSpec((tb, 128), lambda i: (nb - 1 - i, 0)), pl.BlockSpec((1, 128), lambda i: (0, 0))],
        out_shape=[jax.ShapeDtypeStruct((T, 128), F32), jax.ShapeDtypeStruct((1, 128), F32)],
        scratch_shapes=[pltpu.VMEM((1, 128), F32)],
        compiler_params=_cparams(("arbitrary",)),
    )(dFq, dFk, proj, fbias)


FLASH_TILE = 512


def _row_stat_tile(a, b, n):
    at = jnp.broadcast_to(a, (n, 128)).T[0:8, :]
    bt = jnp.broadcast_to(b, (n, 128)).T[0:8, :]
    sub = lax.broadcasted_iota(jnp.int32, (8, 1), 0)
    return jnp.where(sub == 0, at, jnp.where(sub == 1, bt, 0.0))


def _col_stat_tile(a, b):
    lane = lax.broadcasted_iota(jnp.int32, (1, 128), 1)
    return jnp.where(lane == 0, a, jnp.where(lane == 1, b, 0.0))


def _lane_pick(x, h):
    lane = lax.broadcasted_iota(jnp.int32, (1, 128), 1)
    return jnp.sum(jnp.where(lane == h, x, 0.0), axis=1, keepdims=True)


def _half_mask(h):
    lane = lax.broadcasted_iota(jnp.int32, (1, 128), 1)
    return (lane // HEAD_DIM) == h


def _call_hosting(body, name, grid, args, in_specs, out_specs, out_shape, scratch, exch):
    n_out = len(out_shape)
    if exch is not None:
        body, (xargs, xin, xout, xshape, xscratch) = hosted_exchange(
            body, len(args), n_out, len(scratch), grid, *exch)
        args, in_specs, out_specs = args + xargs, in_specs + xin, out_specs + xout
        out_shape, scratch = out_shape + xshape, scratch + xscratch
    res = pl.pallas_call(
        body, name=name, grid=grid, in_specs=in_specs, out_specs=out_specs, out_shape=out_shape,
        scratch_shapes=scratch, compiler_params=_cparams(("arbitrary",) * len(grid)),
    )(*args)
    return res[:n_out], res[n_out:]


def flash_fwd(q, k, v, frep, frow, *, qblk, kblk, vblk, nq, scale, name, exch=None):
    T = q.shape[0]
    tk = _tile(T, FLASH_TILE)
    tq = _tile(T, 2 * FLASH_TILE)
    per_q = tq // tk
    wq = 128 * nq
    has_f = frep is not None

    def body(*refs):
        if has_f:
            q_ref, k_ref, v_ref, fk_ref, fr_ref, o_ref, lr_ref, vT_sc, m_sc, acc_sc = refs
        else:
            q_ref, k_ref, v_ref, o_ref, lr_ref, vT_sc, m_sc, acc_sc = refs
        i = pl.program_id(1)

        @pl.when(i == 0)
        def _():
            vT_sc[...] = v_ref[...].astype(F32).T.astype(MXU)

        key_row = lax.broadcasted_iota(jnp.int32, (tk, 1), 0)
        q_col = lax.broadcasted_iota(jnp.int32, (1, tq), 1)
        row_half = lax.broadcasted_iota(jnp.int32, (128, 1), 0) // HEAD_DIM
        qb = q_ref[...].astype(F32) * scale
        if nq == 1:
            qhs = [jnp.where(_half_mask(h), qb, 0).astype(MXU) for h in range(2)]
        else:
            qhs = [qb[:, 128 * h:128 * (h + 1)].astype(MXU) for h in range(2)]
        for h in range(2):
            m_sc[h] = jnp.full((1, tq), NEG, F32)
            acc_sc[h] = jnp.zeros((128, tq), F32)

        def make_step(diag_block):
            def step(j, carry):
                off = pl.multiple_of(j * tk, tk)
                ks = k_ref[pl.ds(off, tk), :]
                vT = vT_sc[:, pl.ds(off, tk)]
                for h in range(2):
                    kh = ks if nq == 1 else ks[:, 128 * h:128 * (h + 1)]
                    sT = _dot_nt(kh, qhs[h])
                    if has_f:
                        fk = fk_ref[h, pl.ds(off, tk), :]
                        sT = sT + (fr_ref[0, h:h + 1, :] - jnp.concatenate([fk] * (tq // 128), axis=1))
                    if diag_block is not None:
                        sT = jnp.where(key_row + diag_block * tk <= q_col, sT, NEG)
                    m_prev = m_sc[h]
                    m_new = jnp.maximum(m_prev, jnp.max(sT, axis=0, keepdims=True))
                    alpha = jnp.exp(m_prev - m_new)
                    pT = jnp.exp(sT - m_new)
                    vTh = jnp.where(row_half == h, vT, jnp.ones_like(vT))
                    acc_sc[h] = alpha * acc_sc[h] + _dot(vTh, pT)
                    m_sc[h] = m_new
                return carry
            return step

        lax.fori_loop(0, per_q * i, make_step(None), 0)
        for d in range(per_q):
            make_step(d)(per_q * i + d, 0)
        outs, lses = [], []
        for h in range(2):
            acc = acc_sc[h]
            outs.append(acc / pltpu.roll(acc, HEAD_DIM, 0))
            l = acc_sc[h, HEAD_DIM * (1 - h):HEAD_DIM * (1 - h) + 1, :]
            lses.append(m_sc[h] + jnp.log(l))
        o_ref[...] = jnp.where(row_half == 0, outs[0], outs[1]).T
        sub = lax.broadcasted_iota(jnp.int32, (8, 1), 0)
        lr_ref[0] = jnp.where(sub == 0, lses[0], jnp.where(sub == 1, lses[1], 0.0))

    in_specs = [pl.BlockSpec((tq, wq), lambda p, i: (i, qblk + p)),
                pl.BlockSpec((T, wq), lambda p, i: (0, kblk + p)),
                pl.BlockSpec((T, 128), lambda p, i: (0, vblk + p))]
    args = [q, k, v]
    if has_f:
        in_specs += [pl.BlockSpec((2, T, 128), lambda p, i: (p, 0, 0)),
                     pl.BlockSpec((1, 8, tq), lambda p, i: (p, 0, i))]
        args += [frep, frow]
    out_specs = [pl.BlockSpec((tq, 128), lambda p, i: (i, p)), pl.BlockSpec((1, 8, tq), lambda p, i: (p, 0, i))]
    out_shape = [jax.ShapeDtypeStruct((T, 256), F32), jax.ShapeDtypeStruct((2, 8, T), F32)]
    scratch = [pltpu.VMEM((128, T), MXU), pltpu.VMEM((2, 1, tq), F32), pltpu.VMEM((2, 128, tq), F32)]
    return _call_hosting(body, name, (2, T // tq), args, in_specs, out_specs, out_shape, scratch, exch)


def flash_bwd(q, k, v, do, o, lrow, fcol, frow, *, qblk, kblk, vblk, nq, scale, name, exch=None):
    T = q.shape[0]
    tq = tk = _tile(T, FLASH_TILE)
    wq = 128 * nq
    nqb = T // tq
    has_f = fcol is not None

    def body(*refs):
        if has_f:
            (q_ref, k_ref, v_ref, do_ref, o_ref, lr_ref, fc_ref, fr_ref,
             dq_ref, dk_ref, dv_ref, df_ref, dfq_ref, dk_sc, dv_sc, dqT_sc, d_sc, df_sc, dfq_sc) = refs
        else:
            q_ref, k_ref, v_ref, do_ref, o_ref, lr_ref, dq_ref, dk_ref, dv_ref, dk_sc, dv_sc, dqT_sc, d_sc = refs
        j = pl.program_id(1)
        diag = lax.broadcasted_iota(jnp.int32, (tk, 1), 0) <= lax.broadcasted_iota(jnp.int32, (1, tq), 1)
        hms = [_half_mask(h) for h in range(2)]

        @pl.when(j == 0)
        def _():
            dqT_sc[...] = jnp.zeros_like(dqT_sc)
            if has_f:
                dfq_sc[...] = jnp.zeros_like(dfq_sc)

            def delta(b, carry):
                off = pl.multiple_of(b * tq, tq)
                prod = do_ref[pl.ds(off, tq), :] * o_ref[pl.ds(off, tq), :]
                Ds = [jnp.sum(jnp.where(hms[h], prod, 0.0), axis=1, keepdims=True) for h in range(2)]
                d_sc[:, pl.ds(off, tq)] = _row_stat_tile(Ds[0], Ds[1], tq)
                return carry

            lax.fori_loop(0, nqb, delta, 0)

        kb = k_ref[...]
        vb = v_ref[...]
        if nq == 1:
            khs = [jnp.where(hms[h], kb, 0).astype(MXU) for h in range(2)]
        else:
            khs = [kb[:, 128 * h:128 * (h + 1)].astype(MXU) for h in range(2)]
        kTs = [kh.astype(F32).T.astype(MXU) for kh in khs]
        kss = [(kh.astype(F32) * scale).astype(MXU) for kh in khs]
        vhs = [jnp.where(hms[h], vb, 0).astype(MXU) for h in range(2)]
        fks = [_lane_pick(fc_ref[0], h) for h in range(2)] if has_f else None
        dv_sc[...] = jnp.zeros_like(dv_sc)
        dk_sc[...] = jnp.zeros_like(dk_sc)
        if has_f:
            df_sc[...] = jnp.zeros_like(df_sc)

        def make_step(masked):
            def step(i, carry):
                off = pl.multiple_of(i * tq, tq)
                qs = q_ref[pl.ds(off, tq), :]
                dos = do_ref[pl.ds(off, tq), :]
                for h in range(2):
                    qh = qs if nq == 1 else qs[:, 128 * h:128 * (h + 1)]
                    sT = _dot_nt(kss[h], qh)
                    if has_f:
                        sT = sT + (fr_ref[0, h:h + 1, pl.ds(off, tq)] - fks[h])
                    pT = jnp.exp(sT - lr_ref[0, h:h + 1, pl.ds(off, tq)])
                    if masked:
                        pT = jnp.where(diag, pT, 0.0)
                    dsT = pT * (_dot_nt(vhs[h], dos) - d_sc[h:h + 1, pl.ds(off, tq)])
                    dv_sc[...] += _dot(pT, jnp.where(hms[h], dos, 0))
                    qq = jnp.where(hms[h], qs, 0) if nq == 1 else qh
                    dk_sc[h if nq == 2 else 0] += _dot(dsT, qq)
                    dqT_sc[h if nq == 2 else 0, :, pl.ds(off, tq)] += _dot(kTs[h], dsT)
                    if has_f:
                        part = dsT[:, 0:128]
                        for c in range(1, tq // 128):
                            part = part + dsT[:, 128 * c:128 * (c + 1)]
                        df_sc[h] += part
                        dfq_sc[h:h + 1, pl.ds(off, tq)] += jnp.sum(dsT, axis=0, keepdims=True)
                return carry
            return step

        make_step(True)(j, 0)
        lax.fori_loop(j + 1, nqb, make_step(False), 0)
        if nq == 1:
            dk_ref[...] = dk_sc[0] * scale
        else:
            dk_ref[...] = jnp.concatenate([dk_sc[0], dk_sc[1]], axis=1) * scale
        dv_ref[...] = dv_sc[...]
        if has_f:
            df_ref[0] = _col_stat_tile(-jnp.sum(df_sc[0], axis=1, keepdims=True),
                                       -jnp.sum(df_sc[1], axis=1, keepdims=True))

        @pl.when(j == nqb - 1)
        def _():
            if nq == 1:
                dq_ref[...] = dqT_sc[0].T * scale
            else:
                dq_ref[...] = jnp.concatenate([dqT_sc[0].T, dqT_sc[1].T], axis=1) * scale
            if has_f:
                sub = lax.broadcasted_iota(jnp.int32, (128, 1), 0)
                rows = jnp.where(sub == 0, dfq_sc[0:1, :], jnp.where(sub == 1, dfq_sc[1:2, :], 0.0))
                dfq_ref[0] = rows.T

    in_specs = [pl.BlockSpec((T, wq), lambda p, j: (0, qblk + p)),
                pl.BlockSpec((tk, wq), lambda p, j: (j, kblk + p)),
                pl.BlockSpec((tk, 128), lambda p, j: (j, vblk + p)),
                pl.BlockSpec((T, 128), lambda p, j: (0, p)),
                pl.BlockSpec((T, 128), lambda p, j: (0, p)),
                pl.BlockSpec((1, 8, T), lambda p, j: (p, 0, 0))]
    args = [q, k, v, do, o, lrow]
    out_specs = [pl.BlockSpec((T, wq), lambda p, j: (0, p)),
                 pl.BlockSpec((tk, wq), lambda p, j: (j, p)), pl.BlockSpec((tk, 128), lambda p, j: (j, p))]
    out_shape = [jax.ShapeDtypeStruct((T, 2 * wq), F32), jax.ShapeDtypeStruct((T, 2 * wq), F32),
                 jax.ShapeDtypeStruct((T, 256), F32)]
    scratch = [pltpu.VMEM((nq, tk, 128), F32), pltpu.VMEM((tk, 128), F32), pltpu.VMEM((nq, 128, T), F32),
               pltpu.VMEM((8, T), F32)]
    if has_f:
        in_specs += [pl.BlockSpec((1, tk, 128), lambda p, j: (p, j, 0)),
                     pl.BlockSpec((1, 8, T), lambda p, j: (p, 0, 0))]
        args += [fcol, frow]
        out_specs += [pl.BlockSpec((1, tk, 128), lambda p, j: (p, j, 0)),
                      pl.BlockSpec((1, T, 128), lambda p, j: (p, 0, 0))]
        out_shape += [jax.ShapeDtypeStruct((2, T, 128), F32), jax.ShapeDtypeStruct((2, T, 128), F32)]
        scratch += [pltpu.VMEM((2, tk, 128), F32), pltpu.VMEM((8, T), F32)]
    return _call_hosting(body, name, (2, T // tk), args, in_specs, out_specs, out_shape, scratch, exch)


def _swa_align(pair, e, h):
    sel = jnp.where(_half_mask(e), pair, 0.0)
    if e == h:
        return sel
    return pltpu.roll(sel, HEAD_DIM, 1)


def _swa_mask(n):
    W = WINDOW
    qi = lax.broadcasted_iota(jnp.int32, (W, 2 * W), 0) + W
    kj = lax.broadcasted_iota(jnp.int32, (W, 2 * W), 1)
    dist = qi - kj
    return (dist >= 0) & (dist < W) & ((n > 0) | (kj >= W))


def swa_fwd(proj, bias, sinks):
    T = proj.shape[0]
    W = WINDOW
    nb = T // W
    scale = HEAD_DIM ** -0.5

    def body(sink_ref, q_ref, kp_ref, kc_ref, vp_ref, vc_ref, b_ref, o_ref, l_ref):
        n = pl.program_id(0)
        mask = _swa_mask(n)
        kband = jnp.concatenate([kp_ref[...], kc_ref[...]], axis=0).astype(MXU)
        vband = jnp.concatenate([vp_ref[...], vc_ref[...]], axis=0).astype(MXU)
        lane = lax.broadcasted_iota(jnp.int32, (1, 128), 1)
        lse_tile = jnp.zeros((W, 128), F32)
        pairs = []
        for h in range(2):
            full = []
            for g in range(4):
                hq = 4 * h + g
                qa = _swa_align(q_ref[:, 128 * (hq // 2):128 * (hq // 2 + 1)], hq % 2, h)
                s = _dot_nt(qa, kband) * scale + b_ref[hq]
                s = jnp.where(mask, s, NEG)
                sink = sink_ref[hq]
                m = jnp.maximum(jnp.max(s, axis=1, keepdims=True), sink)
                e = jnp.exp(s - m)
                l = jnp.sum(e, axis=1, keepdims=True) + jnp.exp(sink - m)
                r = jnp.where(_half_mask(h), _dot(e, vband), 0.0) / l
                full.append(r + pltpu.roll(r, HEAD_DIM, 1))
                lse_tile = jnp.where(lane == hq, m + jnp.log(l), lse_tile)
            pairs.append(jnp.where(_half_mask(0), full[0], full[1]))
            pairs.append(jnp.where(_half_mask(0), full[2], full[3]))
        o_ref[...] = jnp.concatenate(pairs, axis=1)
        l_ref[...] = lse_tile

    prev = lambda n: (jnp.maximum(n - 1, 0), C_KA // 128)
    cur = lambda n: (n, C_KA // 128)
    prev_v = lambda n: (jnp.maximum(n - 1, 0), C_VA // 128)
    cur_v = lambda n: (n, C_VA // 128)
    return pl.pallas_call(
        body, name="swa_fwd", grid=(nb,),
        in_specs=[pl.BlockSpec(memory_space=pltpu.SMEM),
                  pl.BlockSpec((W, 512), lambda n: (n, 0)),
                  pl.BlockSpec((W, 128), prev), pl.BlockSpec((W, 128), cur),
                  pl.BlockSpec((W, 128), prev_v), pl.BlockSpec((W, 128), cur_v),
                  pl.BlockSpec((8, W, 2 * W), lambda n: (0, 0, 0))],
        out_specs=[pl.BlockSpec((W, 512), lambda n: (n, 0)), pl.BlockSpec((W, 128), lambda n: (n, 0))],
        out_shape=[jax.ShapeDtypeStruct((T, 512), F32), jax.ShapeDtypeStruct((T, 128), F32)],
        compiler_params=_cparams(("parallel",)),
    )(sinks, proj, proj, proj, proj, proj, bias)


def swa_bwd(proj, bias, sinks, do, o, lse):
    T = proj.shape[0]
    W = WINDOW
    nb = T // W
    scale = HEAD_DIM ** -0.5

    def body(sink_ref, q_ref, kp_ref, kc_ref, vp_ref, vc_ref, b_ref, do_ref, o_ref, l_ref,
             dq_ref, dk_ref, dv_ref, db_ref, dsk_ref, ck, cv):
        n = pl.program_id(0)

        @pl.when(n == 0)
        def _():
            ck[...] = jnp.zeros_like(ck)
            cv[...] = jnp.zeros_like(cv)
            db_ref[...] = jnp.zeros_like(db_ref)
            dsk_ref[...] = jnp.zeros_like(dsk_ref)

        @pl.when(n < nb)
        def _():
            mask = _swa_mask(n)
            kb32 = jnp.concatenate([kp_ref[...], kc_ref[...]], axis=0)
            vb32 = jnp.concatenate([vp_ref[...], vc_ref[...]], axis=0)
            kband = kb32.astype(MXU)
            sub = lax.broadcasted_iota(jnp.int32, (8, 1), 0)
            dk_band = jnp.zeros((2 * W, 128), F32)
            dv_band = jnp.zeros((2 * W, 128), F32)
            dsk = jnp.zeros((8, 128), F32)
            dq_pairs = []
            mask4 = jnp.concatenate([mask] * 4, axis=0)
            for h in range(2):
                hm = _half_mask(h)
                km = jnp.where(hm, kb32, 0.0).astype(MXU)
                vm = jnp.where(hm, vb32, 0.0).astype(MXU)
                pbs = [slice(128 * ((4 * h + g) // 2), 128 * ((4 * h + g) // 2 + 1)) for g in range(4)]
                q4 = jnp.concatenate([_swa_align(q_ref[:, pbs[g]], g % 2, h) for g in range(4)], axis=0)
                do4 = jnp.concatenate([_swa_align(do_ref[:, pbs[g]], g % 2, h) for g in range(4)], axis=0)
                D4 = jnp.concatenate(
                    [jnp.sum(jnp.where(_half_mask(g % 2), do_ref[:, pbs[g]] * o_ref[:, pbs[g]], 0.0), axis=1,
                             keepdims=True) for g in range(4)], axis=0)
                lse4 = jnp.concatenate([_lane_pick(l_ref[...], 4 * h + g) for g in range(4)], axis=0)
                sink4 = jnp.concatenate([jnp.full((W, 1), sink_ref[4 * h + g], F32) for g in range(4)], axis=0)
                s = _dot_nt(q4, kband) * scale + b_ref[4 * h:4 * h + 4].reshape(4 * W, 2 * W)
                p = jnp.where(mask4, jnp.exp(s - lse4), 0.0)
                sd = jnp.exp(sink4 - lse4) * D4
                for g in range(4):
                    dsk = dsk + jnp.where(sub == 4 * h + g,
                                          -jnp.sum(sd[W * g:W * (g + 1)], axis=0, keepdims=True), 0.0)
                ds = p * (_dot_nt(do4, vm) - D4)
                db_ref[4 * h:4 * h + 4] += ds.reshape(4, W, 2 * W)
                dq = _dot(ds, km) * scale
                dq = dq + pltpu.roll(dq, HEAD_DIM, 1)
                dk_band = dk_band + _dot(ds.T, q4) * scale
                dv_band = dv_band + _dot(p.T, do4)
                dq_pairs.append(jnp.where(_half_mask(0), dq[0:W], dq[W:2 * W]))
                dq_pairs.append(jnp.where(_half_mask(0), dq[2 * W:3 * W], dq[3 * W:4 * W]))
            dq_ref[...] = jnp.concatenate(dq_pairs, axis=1)
            dsk_ref[...] += dsk
            dk_ref[...] = ck[...] + dk_band[0:W]
            dv_ref[...] = cv[...] + dv_band[0:W]
            ck[...] = dk_band[W:2 * W]
            cv[...] = dv_band[W:2 * W]

        @pl.when(n == nb)
        def _():
            dk_ref[...] = ck[...]
            dv_ref[...] = cv[...]

    cl = lambda n: jnp.minimum(n, nb - 1)
    pv = lambda n: jnp.maximum(jnp.minimum(n, nb - 1) - 1, 0)
    return pl.pallas_call(
        body, name="swa_bwd", grid=(nb + 1,),
        in_specs=[pl.BlockSpec(memory_space=pltpu.SMEM),
                  pl.BlockSpec((W, 512), lambda n: (cl(n), 0)),
                  pl.BlockSpec((W, 128), lambda n: (pv(n), C_KA // 128)),
                  pl.BlockSpec((W, 128), lambda n: (cl(n), C_KA // 128)),
                  pl.BlockSpec((W, 128), lambda n: (pv(n), C_VA // 128)),
                  pl.BlockSpec((W, 128), lambda n: (cl(n), C_VA // 128)),
                  pl.BlockSpec((8, W, 2 * W), lambda n: (0, 0, 0)),
                  pl.BlockSpec((W, 512), lambda n: (cl(n), 0)),
                  pl.BlockSpec((W, 512), lambda n: (cl(n), 0)),
                  pl.BlockSpec((W, 128), lambda n: (cl(n), 0))],
        out_specs=[pl.BlockSpec((W, 512), lambda n: (cl(n), 0)),
                   pl.BlockSpec((W, 128), lambda n: (jnp.maximum(n - 1, 0), 0)),
                   pl.BlockSpec((W, 128), lambda n: (jnp.maximum(n - 1, 0), 0)),
                   pl.BlockSpec((8, W, 2 * W), lambda n: (0, 0, 0)),
                   pl.BlockSpec((8, 128), lambda n: (0, 0))],
        out_shape=[jax.ShapeDtypeStruct((T, 512), F32), jax.ShapeDtypeStruct((T, 128), F32),
                   jax.ShapeDtypeStruct((T, 128), F32), jax.ShapeDtypeStruct((8, W, 2 * W), F32),
                   jax.ShapeDtypeStruct((8, 128), F32)],
        scratch_shapes=[pltpu.VMEM((W, 128), F32), pltpu.VMEM((W, 128), F32)],
        compiler_params=_cparams(("arbitrary",)),
    )(sinks, proj, proj, proj, proj, proj, bias, do, o, lse)


def swa_bias_table(rel_bias):
    W = WINDOW
    qi = jnp.arange(W, dtype=jnp.int32)[:, None] + W
    kj = jnp.arange(2 * W, dtype=jnp.int32)[None, :]
    dist = qi - kj
    max_exact = REL_BUCKETS // 2
    d = jnp.maximum(dist, 0)
    log_ratio = jnp.log(jnp.maximum(d, 1).astype(F32) / max_exact) / math.log(REL_MAX_DIST / max_exact)
    large = jnp.minimum(max_exact + (log_ratio * (REL_BUCKETS - max_exact)).astype(jnp.int32), REL_BUCKETS - 1)
    bucket = jnp.where(d < max_exact, d, large)
    bucket = bucket.reshape(-1)
    onehot = (bucket[None, :] == jnp.arange(REL_BUCKETS, dtype=jnp.int32)[:, None]).astype(F32)
    bias = jnp.dot(rel_bias.astype(F32).T, onehot, precision=lax.Precision.HIGHEST)
    return bias.reshape(SWA_Q_HEADS, W, 2 * W), bucket


def attn_out(oa, ob, oc, gn, wout, gpost, x):
    T = x.shape[0]
    tm = _tile(T, 512)

    def body(oa_ref, ob_ref, oc_ref, gn_ref, w_ref, gp_ref, x_ref, x2_ref, y_ref, mT_ref):
        g = gn_ref[...]
        mixed = jnp.concatenate([_rms_fwd(oa_ref[...], g[:, 0:512]), _rms_fwd(ob_ref[...], g[:, 512:768]),
                                 _rms_fwd(oc_ref[...], g[:, 768:1024])], axis=1)
        mT_ref[...] = mixed.T.astype(MXU)
        y = _dot(mixed, w_ref[...])
        y_ref[...] = y
        x2_ref[...] = x_ref[...] + _rms_fwd(y, gp_ref[...])

    row = lambda i: (i, 0)
    const = lambda i: (0, 0)
    return pl.pallas_call(
        body, name="attn_out", grid=(T // tm,),
        in_specs=[pl.BlockSpec((tm, 512), row), pl.BlockSpec((tm, 256), row), pl.BlockSpec((tm, 256), row),
                  pl.BlockSpec((1, 1024), const), pl.BlockSpec((1024, 1024), const), pl.BlockSpec((1, 1024), const),
                  pl.BlockSpec((tm, 1024), row)],
        out_specs=[pl.BlockSpec((tm, 1024), row), pl.BlockSpec((tm, 1024), row),
                   pl.BlockSpec((1024, tm), lambda i: (0, i))],
        out_shape=[jax.ShapeDtypeStruct((T, 1024), F32), jax.ShapeDtypeStruct((T, 1024), F32),
                   jax.ShapeDtypeStruct((1024, T), MXU)],
        compiler_params=_cparams(("parallel",)),
    )(oa, ob, oc, gn, wout, gpost, x)


def attn_out_bwd(dx2, y, oa, ob, oc, gn, wout, gpost):
    T = dx2.shape[0]
    tm = _tile(T, 512)

    def body(dx_ref, y_ref, oa_ref, ob_ref, oc_ref, gn_ref, w_ref, gp_ref,
             dy_ref, da_ref, db_ref, dc_ref, dgn_ref, dgp_ref):
        first = pl.program_id(0) == 0
        dy, dgp = _rms_bwd(dx_ref[...], y_ref[...], gp_ref[...])
        dy_ref[...] = dy.astype(MXU)
        _acc_out(dgp_ref, dgp, first)
        dm = _dot_nt(dy, w_ref[...])
        g = gn_ref[...]
        da, dga = _rms_bwd(dm[:, 0:512], oa_ref[...], g[:, 0:512])
        db, dgb = _rms_bwd(dm[:, 512:768], ob_ref[...], g[:, 512:768])
        dc, dgc = _rms_bwd(dm[:, 768:1024], oc_ref[...], g[:, 768:1024])
        da_ref[...] = da
        db_ref[...] = db
        dc_ref[...] = dc
        _acc_out(dgn_ref, jnp.concatenate([dga, dgb, dgc], axis=1), first)

    row = lambda i: (i, 0)
    const = lambda i: (0, 0)
    return pl.pallas_call(
        body, name="attn_out_bwd", grid=(T // tm,),
        in_specs=[pl.BlockSpec((tm, 1024), row), pl.BlockSpec((tm, 1024), row),
                  pl.BlockSpec((tm, 512), row), pl.BlockSpec((tm, 256), row), pl.BlockSpec((tm, 256), row),
                  pl.BlockSpec((1, 1024), const), pl.BlockSpec((1024, 1024), const), pl.BlockSpec((1, 1024), const)],
        out_specs=[pl.BlockSpec((tm, 1024), row), pl.BlockSpec((tm, 512), row), pl.BlockSpec((tm, 256), row),
                   pl.BlockSpec((tm, 256), row), pl.BlockSpec((1, 1024), const), pl.BlockSpec((1, 1024), const)],
        out_shape=[jax.ShapeDtypeStruct((T, 1024), MXU), jax.ShapeDtypeStruct((T, 512), F32),
                   jax.ShapeDtypeStruct((T, 256), F32), jax.ShapeDtypeStruct((T, 256), F32),
                   jax.ShapeDtypeStruct((1, 1024), F32), jax.ShapeDtypeStruct((1, 1024), F32)],
        compiler_params=_cparams(("arbitrary",)),
    )(dx2, y, oa, ob, oc, gn, wout, gpost)


FF_TILE = 256
_GELU_C = math.sqrt(2.0 / math.pi)


def _gelu(x):
    return 0.5 * x * (1.0 + jnp.tanh(_GELU_C * (x + 0.044715 * x * x * x)))


def _gelu_with_grad(x):
    x2 = x * x
    t = jnp.tanh(_GELU_C * x * (1.0 + 0.044715 * x2))
    h = 0.5 * (1.0 + t)
    return x * h, h + (0.5 * _GELU_C) * x * (1.0 - t * t) * (1.0 + (3 * 0.044715) * x2)


def _conv_taps(u, hal_ref, first):
    row = lax.broadcasted_iota(jnp.int32, (8, 1), 0)
    h6 = jnp.where(first, 0.0, hal_ref[6:7, :])
    h7 = jnp.where(first, 0.0, hal_ref[7:8, :])
    r1, r2 = pltpu.roll(u, 1, 0), pltpu.roll(u, 2, 0)
    r1 = jnp.concatenate([jnp.where(row == 0, h7, r1[0:8]), r1[8:]], axis=0)
    r2 = jnp.concatenate([jnp.where(row == 0, h6, jnp.where(row == 1, h7, r2[0:8])), r2[8:]], axis=0)
    return r1, r2


def ffn_fwd(u0, convw, convb, wdown, gpost, x2):
    T = x2.shape[0]
    tm, tn = _tile(T, 1024), FF_TILE
    nj = D_FF // tn

    def body(ug_ref, uu_ref, hg_ref, hu_ref, wg_ref, wu_ref, bg_ref, bu_ref, wd_ref, gp_ref, x_ref,
             x3_ref, y_ref, aT_ref, acc):
        i, j = pl.program_id(0), pl.program_id(1)
        first = i == 0

        def conv(u_ref, h_ref, w_ref, b_ref):
            u = u_ref[...]
            r1, r2 = _conv_taps(u, h_ref, first)
            return b_ref[...] + w_ref[0:1, :] * r2 + w_ref[1:2, :] * r1 + w_ref[2:3, :] * u

        a = _gelu(conv(ug_ref, hg_ref, wg_ref, bg_ref)) * conv(uu_ref, hu_ref, wu_ref, bu_ref)
        aT_ref[...] = a.T.astype(MXU)
        _acc_out(acc, _dot(a, wd_ref[...]), j == 0)

        @pl.when(j == nj - 1)
        def _():
            y = acc[...]
            y_ref[...] = y
            x3_ref[...] = x_ref[...] + _rms_fwd(y, gp_ref[...])

    halo = lambda off: (lambda i, j: (jnp.maximum(i * (tm // 8) - 1, 0), off + j))
    return pl.pallas_call(
        body, name="ffn_fwd", grid=(T // tm, nj),
        in_specs=[pl.BlockSpec((tm, tn), lambda i, j: (i, j)), pl.BlockSpec((tm, tn), lambda i, j: (i, nj + j)),
                  pl.BlockSpec((8, tn), halo(0)), pl.BlockSpec((8, tn), halo(nj)),
                  pl.BlockSpec((3, tn), lambda i, j: (0, j)), pl.BlockSpec((3, tn), lambda i, j: (0, nj + j)),
                  pl.BlockSpec((1, tn), lambda i, j: (0, j)), pl.BlockSpec((1, tn), lambda i, j: (0, nj + j)),
                  pl.BlockSpec((tn, 1024), lambda i, j: (j, 0)),
                  pl.BlockSpec((1, 1024), lambda i, j: (0, 0)),
                  pl.BlockSpec((tm, 1024), lambda i, j: (i, 0))],
        out_specs=[pl.BlockSpec((tm, 1024), lambda i, j: (i, 0)), pl.BlockSpec((tm, 1024), lambda i, j: (i, 0)),
                   pl.BlockSpec((tn, tm), lambda i, j: (j, i))],
        out_shape=[jax.ShapeDtypeStruct((T, 1024), F32), jax.ShapeDtypeStruct((T, 1024), F32),
                   jax.ShapeDtypeStruct((D_FF, T), MXU)],
        scratch_shapes=[pltpu.VMEM((tm, 1024), F32)],
        compiler_params=_cparams(("parallel", "arbitrary")),
    )(u0, u0, u0, u0, convw, convw, convb, convb, wdown, gpost, x2)


def ffn_bwd(dx3, y, u0, convw, convb, wdown, gpost, wupT, x2, gfpre):
    T = dx3.shape[0]
    tm, tn = _tile(T, 512), FF_TILE
    nj = D_FF // tn
    ni = T // tm

    def body(dx_ref, y_ref, ug_ref, uu_ref, hg_ref, hu_ref, wg_ref, wu_ref, bg_ref, bu_ref, wd_ref, gp_ref,
             wtg_ref, wtu_ref, x2_ref, gf_ref,
             dy_ref, dug_ref, duu_ref, dcg_ref, dcu_ref, dgp_ref, dx2_ref, dgf_ref, dy_sc, dh_sc, cg, cu, ag, au):
        s, j = pl.program_id(0), pl.program_id(1)
        i = ni - 1 - s
        first_tok = i == 0
        sub = lax.broadcasted_iota(jnp.int32, (8, 1), 0)

        @pl.when(j == 0)
        def _():
            dy, dgp = _rms_bwd(dx_ref[...], y_ref[...], gp_ref[...])
            dy_sc[...] = dy.astype(MXU)
            dy_ref[...] = dy.astype(MXU)
            _acc_out(dgp_ref, dgp, s == 0)
            dh_sc[...] = jnp.zeros_like(dh_sc)

        @pl.when(s == 0)
        def _():
            cg[j] = jnp.zeros((8, tn), F32)
            cu[j] = jnp.zeros((8, tn), F32)
            ag[j] = jnp.zeros((8, tn), F32)
            au[j] = jnp.zeros((8, tn), F32)

        da = _dot_nt(dy_sc[...], wd_ref[...])

        def conv(u_ref, h_ref, w_ref, b_ref):
            u = u_ref[...]
            r1, r2 = _conv_taps(u, h_ref, first_tok)
            return b_ref[...] + w_ref[0:1, :] * r2 + w_ref[1:2, :] * r1 + w_ref[2:3, :] * u, u, r1, r2

        gate, ugv, g1, g2 = conv(ug_ref, hg_ref, wg_ref, bg_ref)
        up, uuv, u1, u2 = conv(uu_ref, hu_ref, wu_ref, bu_ref)
        gl, dgl = _gelu_with_grad(gate)
        dup = da * gl
        dgate = da * up * dgl

        def conv_bwd(du, u, r1, r2, w_ref, c_ref, a_ref, duT_ref, wt_ref):
            nxt = c_ref[j]
            n0, n1 = nxt[0:1, :], nxt[1:2, :]
            f1, f2 = pltpu.roll(du, tm - 1, 0), pltpu.roll(du, tm - 2, 0)
            f1 = jnp.concatenate([f1[:tm - 8], jnp.where(sub == 7, n0, f1[tm - 8:])], axis=0)
            f2 = jnp.concatenate([f2[:tm - 8], jnp.where(sub == 7, n1, jnp.where(sub == 6, n0, f2[tm - 8:]))], axis=0)
            du0 = w_ref[2:3, :] * du + w_ref[1:2, :] * f1 + w_ref[0:1, :] * f2
            duT_ref[...] = du0.T.astype(MXU)
            dh_sc[...] += _dot(du0, wt_ref[...])
            c_ref[j] = du[0:8, :]
            red = lambda v: jnp.sum(v, axis=0, keepdims=True)
            part = jnp.where(sub == 0, red(du * r2), jnp.where(sub == 1, red(du * r1), jnp.where(
                sub == 2, red(du * u), jnp.where(sub == 3, red(du), 0.0))))
            a_ref[j] = a_ref[j] + part
            return a_ref[j]

        dcg_ref[0] = conv_bwd(dgate, ugv, g1, g2, wg_ref, cg, ag, dug_ref, wtg_ref)
        dcu_ref[0] = conv_bwd(dup, uuv, u1, u2, wu_ref, cu, au, duu_ref, wtu_ref)

        @pl.when(j == nj - 1)
        def _():
            dx, dgf = _rms_bwd(dh_sc[...], x2_ref[...], gf_ref[...])
            dx2_ref[...] = dx_ref[...] + dx
            _acc_out(dgf_ref, dgf, s == 0)

    rev = lambda s: ni - 1 - s
    halo = lambda off: (lambda s, j: (jnp.maximum(rev(s) * (tm // 8) - 1, 0), off + j))
    tok = pl.BlockSpec((tm, 1024), lambda s, j: (rev(s), 0))
    vec = pl.BlockSpec((1, 1024), lambda s, j: (0, 0))
    return pl.pallas_call(
        body, name="ffn_bwd", grid=(ni, nj),
        in_specs=[tok, tok,
                  pl.BlockSpec((tm, tn), lambda s, j: (rev(s), j)), pl.BlockSpec((tm, tn), lambda s, j: (rev(s), nj + j)),
                  pl.BlockSpec((8, tn), halo(0)), pl.BlockSpec((8, tn), halo(nj)),
                  pl.BlockSpec((3, tn), lambda s, j: (0, j)), pl.BlockSpec((3, tn), lambda s, j: (0, nj + j)),
                  pl.BlockSpec((1, tn), lambda s, j: (0, j)), pl.BlockSpec((1, tn), lambda s, j: (0, nj + j)),
                  pl.BlockSpec((tn, 1024), lambda s, j: (j, 0)), vec,
                  pl.BlockSpec((tn, 1024), lambda s, j: (j, 0)), pl.BlockSpec((tn, 1024), lambda s, j: (nj + j, 0)),
                  tok, vec],
        out_specs=[tok,
                   pl.BlockSpec((tn, tm), lambda s, j: (j, rev(s))), pl.BlockSpec((tn, tm), lambda s, j: (j, rev(s))),
                   pl.BlockSpec((1, 8, tn), lambda s, j: (s, 0, j)), pl.BlockSpec((1, 8, tn), lambda s, j: (s, 0, j)),
                   vec, tok, vec],
        out_shape=[jax.ShapeDtypeStruct((T, 1024), MXU), jax.ShapeDtypeStruct((D_FF, T), MXU),
                   jax.ShapeDtypeStruct((D_FF, T), MXU),
                   jax.ShapeDtypeStruct((ni, 8, D_FF), F32), jax.ShapeDtypeStruct((ni, 8, D_FF), F32),
                   jax.ShapeDtypeStruct((1, 1024), F32), jax.ShapeDtypeStruct((T, 1024), F32),
                   jax.ShapeDtypeStruct((1, 1024), F32)],
        scratch_shapes=[pltpu.VMEM((tm, 1024), MXU), pltpu.VMEM((tm, 1024), F32)] + [pltpu.VMEM((nj, 8, tn), F32)] * 4,
        compiler_params=_cparams(("arbitrary", "arbitrary")),
    )(dx3, y, u0, u0, u0, u0, convw, convw, convb, convb, wdown, gpost, wupT, wupT, x2, gfpre)


ELEMS_PER_BLOCK = 512 * 1024


def _row_block(R, C):
    if R * C <= ELEMS_PER_BLOCK or R % 8:
        return R
    best = 8
    for t in range(8, R + 1, 8):
        if R % t == 0 and t * C <= ELEMS_PER_BLOCK:
            best = t
    return best


def adamw(w, g, m, v, name):
    L, R, C = w.shape
    partials = isinstance(g, (list, tuple))
    tr = _row_block(R, 2 * C)
    c1 = 1.0 - ADAM_B1 ** ADAM_STEP
    c2 = 1.0 - ADAM_B2 ** ADAM_STEP

    def body(w_ref, *rest):
        g_refs, (m_ref, v_ref, g_out, d_ref, nm_ref, nv_ref) = rest[:-6], rest[-6:]

        def step(gv):
            g_out[0] = gv
            nm = ADAM_B1 * m_ref[0] + (1.0 - ADAM_B1) * gv
            nv = ADAM_B2 * v_ref[0] + (1.0 - ADAM_B2) * (gv * gv)
            nm_ref[0] = nm
            nv_ref[0] = nv
            d_ref[0] = -ADAM_LR * ((nm / c1) / (jnp.sqrt(nv / c2) + ADAM_EPS) + ADAM_WD * w_ref[0])

        if not partials:
            step(g_refs[0][0])
            return
        for k in range(L):
            @pl.when(pl.program_id(0) == k)
            def _(k=k):
                gv = g_refs[k][0].astype(F32)
                for d in range(1, N_DEV):
                    gv = gv + g_refs[k][d].astype(F32)
                step(gv)

    spec = pl.BlockSpec((1, tr, C), lambda l, i: (l, i, 0))
    if partials:
        gspecs = [pl.BlockSpec((N_DEV, tr, C), lambda l, i, k=k: (0, jnp.where(l == k, i, 0), 0)) for k in range(L)]
        gs = list(g)
    else:
        gspecs, gs = [spec], [g]
    return pl.pallas_call(
        body, name=name, grid=(L, R // tr), in_specs=[spec] + gspecs + [spec, spec], out_specs=[spec] * 4,
        out_shape=[jax.ShapeDtypeStruct((L, R, C), F32)] * 4,
        compiler_params=_cparams(("arbitrary", "arbitrary")),
    )(w, *gs, m, v)


def sum_devices(buf, name):
    _, R, C = buf.shape
    tr = _row_block(R, C * 4)

    def body(b_ref, o_ref):
        acc = b_ref[0].astype(F32)
        for d in range(1, N_DEV):
            acc = acc + b_ref[d].astype(F32)
        o_ref[...] = acc

    return pl.pallas_call(
        body, name=name, grid=(R // tr,),
        in_specs=[pl.BlockSpec((N_DEV, tr, C), lambda i: (0, i, 0))],
        out_specs=pl.BlockSpec((tr, C), lambda i: (i, 0)),
        out_shape=jax.ShapeDtypeStruct((R, C), F32),
        compiler_params=_cparams(("parallel",)),
    )(buf)


def _exchange_copies(src_refs, out_refs, send_sems, recv_sems, gather):
    x, y, c = lax.axis_index("x"), lax.axis_index("y"), lax.axis_index("c")
    me = 4 * x + 2 * y + c
    flip = lambda a, bit: 1 - a if bit else a
    part = lambda ref, d: ref if gather else ref.at[d]
    copies = []
    for k in range(1, N_DEV):
        px, py, pc = flip(x, (k >> 2) & 1), flip(y, (k >> 1) & 1), flip(c, k & 1)
        peer = 4 * px + 2 * py + pc
        for t in range(len(src_refs)):
            sem = t * (N_DEV - 1) + k - 1
            mk = lambda s, d: pltpu.make_async_remote_copy(
                src_ref=s, dst_ref=d, send_sem=send_sems.at[sem], recv_sem=recv_sems.at[sem],
                device_id=(px, py, pc), device_id_type=pl.DeviceIdType.MESH)
            copies.append((mk(part(src_refs[t], peer), out_refs[t].at[me]),
                           mk(part(src_refs[t], me), out_refs[t].at[peer])))
    return me, copies


def exchange(srcs, name, gather):
    n = len(srcs)
    shapes = [(N_DEV,) + s.shape if gather else s.shape for s in srcs]

    def body(*refs):
        src_refs, out_refs = refs[:n], refs[n:2 * n]
        send_sems, recv_sems, local_sems = refs[2 * n:]
        me, copies = _exchange_copies(src_refs, out_refs, send_sems, recv_sems, gather)
        for outgoing, _ in copies:
            outgoing.start()
        mine = [pltpu.make_async_copy(src_refs[t] if gather else src_refs[t].at[me], out_refs[t].at[me],
                                      local_sems.at[t]) for t in range(n)]
        for cp in mine:
            cp.start()
        for _, incoming in copies:
            incoming.wait_recv()
        for outgoing, _ in copies:
            outgoing.wait_send()
        for cp in mine:
            cp.wait()

    return pl.pallas_call(
        body, name=name,
        in_specs=[pl.BlockSpec(memory_space=pl.ANY)] * n, out_specs=[pl.BlockSpec(memory_space=pl.ANY)] * n,
        out_shape=[jax.ShapeDtypeStruct(shp, s.dtype) for shp, s in zip(shapes, srcs)],
        scratch_shapes=[pltpu.SemaphoreType.DMA((n * (N_DEV - 1),)), pltpu.SemaphoreType.DMA((n * (N_DEV - 1),)),
                        pltpu.SemaphoreType.DMA((n,))],
    )(*srcs)


def hosted_exchange(body, n_in, n_out, n_scratch, grid, srcs, gather):
    n = len(srcs)
    shapes = [(N_DEV,) + s.shape if gather else s.shape for s in srcs]

    def wrapped(*refs):
        ins, xin = refs[:n_in], refs[n_in:n_in + n]
        outs = refs[n_in + n:n_in + n + n_out]
        xout = refs[n_in + n + n_out:n_in + 2 * n + n_out]
        rest = refs[n_in + 2 * n + n_out:]
        scratch, (send_sems, recv_sems, local_sems) = rest[:n_scratch], rest[n_scratch:]
        ids = [pl.program_id(a) for a in range(len(grid))]
        first = functools.reduce(jnp.logical_and, [i == 0 for i in ids])
        last = functools.reduce(jnp.logical_and, [i == g - 1 for i, g in zip(ids, grid)])
        me, copies = _exchange_copies(xin, xout, send_sems, recv_sems, gather)
        mine = [pltpu.make_async_copy(xin[t] if gather else xin[t].at[me], xout[t].at[me], local_sems.at[t])
                for t in range(n)]

        @pl.when(first)
        def _():
            for outgoing, _ in copies:
                outgoing.start()
            for cp in mine:
                cp.start()

        body(*ins, *outs, *scratch)

        @pl.when(last)
        def _():
            for _, incoming in copies:
                incoming.wait_recv()
            for outgoing, _ in copies:
                outgoing.wait_send()
            for cp in mine:
                cp.wait()

    any_spec = pl.BlockSpec(memory_space=pl.ANY)
    return wrapped, (list(srcs), [any_spec] * n, [any_spec] * n,
                     [jax.ShapeDtypeStruct(shp, s.dtype) for shp, s in zip(shapes, srcs)],
                     [pltpu.SemaphoreType.DMA((n * (N_DEV - 1),)), pltpu.SemaphoreType.DMA((n * (N_DEV - 1),)),
                      pltpu.SemaphoreType.DMA((n,))])


def _pack(parts, cols, row_align, dtype):
    flat = jnp.concatenate([p.astype(dtype) for p in parts], axis=-1)
    n = flat.shape[-1]
    block = cols * row_align
    total = -(-n // block) * block
    flat = jnp.pad(flat, [(0, 0)] * (flat.ndim - 1) + [(0, total - n)])
    return flat.reshape(flat.shape[:-1] + (total // cols, cols))


def _unpack(buf, shapes):
    lead = buf.shape[:-2]
    flat = buf.reshape(lead + (-1,))
    out, off = [], 0
    for s in shapes:
        n = int(np.prod(s))
        out.append(flat[..., off:off + n].reshape(lead + tuple(s)))
        off += n
    return out


SHARD_SHAPES = [(128, IN_COLS), (256, 48), (128, 64), (128, 1024), (1024, 704), (352, 1024)]
SHARDED = ["w_in", "w_uq", "w_ukv", "w_out", "w_up", "w_down"]
ATTN_SENT = ["w_in_p", "w_uq", "w_ukv", "w_out"]
FFN_SIDE = ["w_upT", "conv_w", "w_down"]


def _full_from_shards(name, s):
    if name in ("w_in", "w_in_p", "w_out", "w_down", "w_upT"):
        return s.reshape((-1, s.shape[-1]))
    return s.transpose(1, 0, 2).reshape((s.shape[1], -1))


def _shards_from_full(name, f):
    if name in ("w_in", "w_in_p", "w_out", "w_down", "w_upT"):
        return f.reshape((N_DEV, -1, f.shape[-1]))
    return f.reshape((f.shape[0], N_DEV, -1)).transpose(1, 0, 2)


def _perm_w_in(w):
    z = lambda n: jnp.zeros(w.shape[:-1] + (n,), w.dtype)
    return jnp.concatenate([w[..., :1536], w[..., 1540:1924], w[..., 1536:1540], z(60), w[..., 1924:1956], z(32)],
                           axis=-1)


def _unperm_w_in(d):
    return jnp.concatenate([d[..., :1536], d[..., 1920:1924], d[..., 1536:1920], d[..., 1984:2016]], axis=-1)


def _perm_w_uq(w):
    return jnp.pad(w.reshape(256, 4, MLA_QK_DIM), ((0, 0), (0, 0), (0, 128 - MLA_QK_DIM))).reshape(256, 512)


def _unperm_w_uq(d):
    return d.reshape(256, 4, 128)[:, :, :MLA_QK_DIM].reshape(256, 4 * MLA_QK_DIM)


def _perm_w_ukv(w):
    w4 = w.reshape(128, 4, 128)
    k = jnp.pad(w4[:, :, :64], ((0, 0), (0, 0), (0, 64))).reshape(128, 512)
    return jnp.concatenate([k, w4[:, :, 64:].reshape(128, 256)], axis=1)


def _unperm_w_ukv(d):
    dk = d[:, :512].reshape(128, 4, 128)[:, :, :64]
    dv = d[:, 512:].reshape(128, 4, 64)
    return jnp.concatenate([dk, dv], axis=-1).reshape(128, 512)


def _row(v, width=None):
    v = v.reshape(1, -1).astype(F32)
    if width is not None and v.shape[1] < width:
        v = jnp.pad(v, ((0, 0), (0, width - v.shape[1])))
    return v


def _layer_fwd(x, P, shared, send=None, ffn_from=None):
    cosr, sinr, bias = shared
    ex = (lambda part: None) if send is None else (lambda part: (send[part], True))
    proj, hT, projb = norm_matmul(x, P["g_pre"], P["w_in_p"], "in_proj", lo_tiles=C_CQ // 512)
    qm, km, vm, cqT, ckvT = mla_prep(proj, P["gq"], P["gkv"], P["w_uq_p"], P["w_ukv_p"], cosr, sinr)
    fcol, frow, frep = fox_gate(proj, P["fbias"])
    oa, lse_a = swa_fwd(proj, bias, P["sinks"])
    (ob, lrb), got_a = flash_fwd(projb, projb, projb, frep, frow, qblk=C_QF // 128, kblk=C_KF // 128,
                                 vblk=C_VF // 128, nq=1, scale=HEAD_DIM ** -0.5, name="fox_fwd", exch=ex(0))
    (oc, lrc), got_b = flash_fwd(qm, km, vm, None, None, qblk=0, kblk=0, vblk=0, nq=2,
                                 scale=MLA_QK_DIM ** -0.5, name="mla_fwd", exch=ex(1))
    x2, y1, mT = attn_out(oa, ob, oc, P["gn"], P["w_out"], P["g_apost"], x)
    if ffn_from is not None:
        P = dict(P, **ffn_from(got_a, got_b))
    u0, h2 = norm_matmul(x2, P["g_fpre"], P["w_upT"], "up_proj", tn_pref=1536, w_transposed=True,
                         h_transposed=False)
    x3, y2, aT = ffn_fwd(u0, P["conv_w"], P["conv_b"], P["w_down"], P["g_fpost"], x2)
    S = dict(x=x, proj=proj, projb=projb, hT=hT, qm=qm, km=km, vm=vm, cqT=cqT, ckvT=ckvT, fcol=fcol, frow=frow,
             oa=oa, lse_a=lse_a, ob=ob, lrb=lrb, oc=oc, lrc=lrc,
             x2=x2, y1=y1, mT=mT, u0=u0, h2=h2, y2=y2, aT=aT)
    return x3, S, P, (got_a, got_b)


def _layer_bwd(dx3, P, S, shared, send_attn=None):
    cosr, sinr, bias = shared
    proj = S["proj"]
    G = {}
    dy2, dugT, duuT, dcg, dcu, G["ffn_post_norm"], dx2, G["ffn_pre_norm"] = ffn_bwd(
        dx3, S["y2"], S["u0"], P["conv_w"], P["conv_b"], P["w_down"], P["g_fpost"], P["w_upT"], S["x2"], P["g_fpre"])
    dconv = jnp.concatenate([dcg[-1], dcu[-1]], axis=1)
    G["conv_w"], G["conv_b"] = dconv[0:3], dconv[3]
    G["w_down"] = matmul_nn(S["aT"], dy2, "dw_down", MXU)
    G["w_upT"] = jnp.concatenate([matmul_nn(dugT, S["h2"], "dw_up_gate", MXU),
                                  matmul_nn(duuT, S["h2"], "dw_up_up", MXU)], axis=0)
    G["w_up"] = G["w_upT"].T
    dy1, doa, dob, doc, G["group_norm"], G["attn_post_norm"] = attn_out_bwd(
        dx2, S["y1"], S["oa"], S["ob"], S["oc"], P["gn"], P["w_out"], P["g_apost"])
    G["w_out"] = matmul_nn(S["mT"], dy1, "dw_out", MXU)
    dqa, dka, dva, dbias, dsk = swa_bwd(proj, bias, P["sinks"], doa, S["oa"], S["lse_a"])
    G["swa_sinks"] = dsk[:, 0]
    pb = S["projb"]
    if send_attn is None:
        ex = lambda part: None
    else:
        parts = ([_shards_from_full("w_down", G["w_down"])] + send_attn, [_shards_from_full("w_upT", G["w_upT"])])
        ex = lambda part: (parts[part], False)
    (dqf, dkf, dvf, dFk, dFq), got_a = flash_bwd(
        pb, pb, pb, dob, S["ob"], S["lrb"], S["fcol"], S["frow"], name="fox_bwd", qblk=C_QF // 128,
        kblk=C_KF // 128, vblk=C_VF // 128, nq=1, scale=HEAD_DIM ** -0.5, exch=ex(0))
    dmisc_f, dfb = fox_gate_bwd(dFq, dFk, proj, P["fbias"])
    G["forget_bias"] = dfb[0, 0:4]
    (dqm_, dkm_, dvm_), got_b = flash_bwd(
        S["qm"], S["km"], S["vm"], doc, S["oc"], S["lrc"], None, None, name="mla_bwd",
        qblk=0, kblk=0, vblk=0, nq=2, scale=MLA_QK_DIM ** -0.5, exch=ex(1))
    dqm, dkv, dcq, dckv, dmisc_r, G["q_latent_norm"], G["kv_latent_norm"] = mla_prep_bwd(
        dqm_, dkm_, dvm_, proj, P["gq"], P["gkv"], P["w_uq_p"], P["w_ukv_p"], cosr, sinr)
    G["w_uq"] = _unperm_w_uq(matmul_nn(S["cqT"], dqm, "dw_uq", MXU))
    G["w_ukv"] = _unperm_w_ukv(matmul_nn(S["ckvT"], dkv, "dw_ukv", MXU))
    dproj = jnp.concatenate([dqa, dka, dva, dqf, dkf, dvf, dcq, dckv, dmisc_f + dmisc_r], axis=1).astype(MXU)
    G["w_in_p"] = matmul_nn(S["hT"], dproj, "dw_in", MXU)
    G["w_in"] = _unperm_w_in(G["w_in_p"])
    dx, G["attn_pre_norm"] = matmul_nt_normbwd(dproj, P["w_in_p"], S["x"], P["g_pre"], dx2, "in_bwd")
    return dx, G, dbias, (got_a, got_b)


def _layer_params(l, full, small):
    return dict(
        g_pre=_row(small["attn_pre_norm"][l]),
        w_in_p=full["w_in_p"] if "w_in_p" in full else _perm_w_in(full["w_in"]),
        gq=_row(small["q_latent_norm"][l]), gkv=_row(small["kv_latent_norm"][l]),
        w_uq_p=_perm_w_uq(full["w_uq"]), w_ukv_p=_perm_w_ukv(full["w_ukv"]),
        fbias=_row(small["forget_bias"][l], 128), sinks=small["swa_sinks"][l].astype(F32),
        gn=_row(small["group_norm"][l]), w_out=full["w_out"], g_apost=_row(small["attn_post_norm"][l]),
        g_fpre=_row(small["ffn_pre_norm"][l]), conv_b=_row(small["conv_b"][l]),
        g_fpost=_row(small["ffn_post_norm"][l]),
        **{n: full[n] for n in FFN_SIDE if n in full},
        **({"w_upT": full["w_up"].T} if "w_up" in full else {}))


def _rel_bias_grad(dbias, bucket):
    flat = dbias.reshape(SWA_Q_HEADS, -1)
    hi = flat.astype(MXU)
    lo = (flat - hi.astype(F32)).astype(MXU)
    onehot = (bucket[:, None] == jnp.arange(128, dtype=jnp.int32)[None, :]).astype(MXU)
    r = matmul_nn(jnp.concatenate([hi, lo], axis=0), onehot, "rel_bias_grad")
    return (r[0:8] + r[8:16])[:, :REL_BUCKETS].T


def local_step(x, tgt, fulls, small, comm=None):
    T = x.shape[0]
    cosr, sinr = rope_tables(T)
    bias, bucket = swa_bias_table(small["rel_bias"])
    shared = (cosr, sinr, bias)
    Ps, Ss = [], []
    h, full = x, fulls[0]
    for l in range(DEPTH):
        P = _layer_params(l, full, small)
        if comm:
            h, S, P, got = _layer_fwd(h, P, shared, comm["weight_parts"](l), comm["ffn_from"])
            full = comm["attn_from"](got) if l + 1 < DEPTH else None
        else:
            h, S, P, _ = _layer_fwd(h, P, shared)
            full = fulls[l + 1] if l + 1 < DEPTH else None
        Ps.append(P)
        Ss.append(S)
    dh, sq = loss_kernel(h, tgt)
    grads = [None] * DEPTH
    dbias_sum = None
    pending = [] if comm else None
    for l in reversed(range(DEPTH)):
        dh, grads[l], dbias, got = _layer_bwd(dh, Ps[l], Ss[l], shared, pending)
        dbias_sum = dbias if dbias_sum is None else dbias_sum + dbias
        if comm:
            comm["landed"](l, ["w_down"], got[0][:1])
            comm["landed"](l, ["w_upT"], got[1])
            if pending:
                comm["landed"](l + 1, ATTN_SENT, got[0][1:])
            pending = [_shards_from_full(n, grads[l][n]) for n in ATTN_SENT]
    return sq, dh, grads, _rel_bias_grad(dbias_sum, bucket), pending


WEIGHTS = ['attn_pre_norm', 'w_in', 'forget_bias', 'swa_sinks', 'rel_bias', 'q_latent_norm', 'w_uq',
           'kv_latent_norm', 'w_ukv', 'group_norm', 'w_out', 'attn_post_norm', 'ffn_pre_norm', 'w_up', 'conv_w',
           'conv_b', 'w_down', 'ffn_post_norm']
SMALL_PER_LAYER = ['attn_pre_norm', 'forget_bias', 'swa_sinks', 'q_latent_norm', 'kv_latent_norm', 'group_norm',
                   'attn_post_norm', 'ffn_pre_norm', 'conv_b', 'ffn_post_norm', 'conv_w']


def kernel(x, attn_pre_norm, w_in, forget_bias, swa_sinks, rel_bias, q_latent_norm, w_uq, kv_latent_norm, w_ukv, group_norm, w_out, attn_post_norm, ffn_pre_norm, w_up, conv_w, conv_b, w_down, ffn_post_norm, loss_target, m_attn_pre_norm, m_w_in, m_forget_bias, m_swa_sinks, m_rel_bias, m_q_latent_norm, m_w_uq, m_kv_latent_norm, m_w_ukv, m_group_norm, m_w_out, m_attn_post_norm, m_ffn_pre_norm, m_w_up, m_conv_w, m_conv_b, m_w_down, m_ffn_post_norm, v_attn_pre_norm, v_w_in, v_forget_bias, v_swa_sinks, v_rel_bias, v_q_latent_norm, v_w_uq, v_kv_latent_norm, v_w_ukv, v_group_norm, v_w_out, v_attn_post_norm, v_ffn_pre_norm, v_w_up, v_conv_w, v_conv_b, v_w_down, v_ffn_post_norm):
    W = dict(attn_pre_norm=attn_pre_norm, w_in=w_in, forget_bias=forget_bias, swa_sinks=swa_sinks, rel_bias=rel_bias,
             q_latent_norm=q_latent_norm, w_uq=w_uq, kv_latent_norm=kv_latent_norm, w_ukv=w_ukv,
             group_norm=group_norm, w_out=w_out, attn_post_norm=attn_post_norm, ffn_pre_norm=ffn_pre_norm,
             w_up=w_up, conv_w=conv_w, conv_b=conv_b, w_down=w_down, ffn_post_norm=ffn_post_norm)
    M = dict(attn_pre_norm=m_attn_pre_norm, w_in=m_w_in, forget_bias=m_forget_bias, swa_sinks=m_swa_sinks,
             rel_bias=m_rel_bias, q_latent_norm=m_q_latent_norm, w_uq=m_w_uq, kv_latent_norm=m_kv_latent_norm,
             w_ukv=m_w_ukv, group_norm=m_group_norm, w_out=m_w_out, attn_post_norm=m_attn_post_norm,
             ffn_pre_norm=m_ffn_pre_norm, w_up=m_w_up, conv_w=m_conv_w, conv_b=m_conv_b, w_down=m_w_down,
             ffn_post_norm=m_ffn_post_norm)
    V = dict(attn_pre_norm=v_attn_pre_norm, w_in=v_w_in, forget_bias=v_forget_bias, swa_sinks=v_swa_sinks,
             rel_bias=v_rel_bias, q_latent_norm=v_q_latent_norm, w_uq=v_w_uq, kv_latent_norm=v_kv_latent_norm,
             w_ukv=v_w_ukv, group_norm=v_group_norm, w_out=v_w_out, attn_post_norm=v_attn_post_norm,
             ffn_pre_norm=v_ffn_pre_norm, w_up=v_w_up, conv_w=v_conv_w, conv_b=v_conv_b, w_down=v_w_down,
             ffn_post_norm=v_ffn_post_norm)
    me = 4 * lax.axis_index("x") + 2 * lax.axis_index("y") + lax.axis_index("c")

    def attn_shards(l):
        return [_perm_w_in(w_in[l].astype(MXU))] + [W[n][l].astype(MXU) for n in ATTN_SENT[1:]]

    def weight_parts(l):
        return ([W["w_down"][l].astype(MXU)] + (attn_shards(l + 1) if l + 1 < DEPTH else []),
                [jnp.swapaxes(W["w_up"][l], 0, 1).astype(MXU), conv_w[l]])

    def ffn_from(got_a, got_b):
        return dict(w_down=_full_from_shards("w_down", got_a[0]), w_upT=_full_from_shards("w_upT", got_b[0]),
                    conv_w=got_b[1].transpose(1, 0, 2).reshape(3, 2 * D_FF))

    def attn_from(got):
        return {n: _full_from_shards(n, s) for n, s in zip(ATTN_SENT, got[0][1:])}

    landed = [{} for _ in range(DEPTH)]

    def on_landed(l, names, arrays):
        landed[l].update(zip(names, arrays))

    comm = dict(weight_parts=weight_parts, ffn_from=ffn_from, attn_from=attn_from, landed=on_landed)
    full0 = dict(zip(ATTN_SENT, map(_full_from_shards, ATTN_SENT, exchange(attn_shards(0), "gather_weights", True))))
    sq, dx, grads, drel, last = local_step(x[0], loss_target[0], [full0], W, comm)
    on_landed(0, ATTN_SENT, exchange(last, "scatter_grads", False))
    for l in range(DEPTH):
        landed[l]["w_in"] = _unperm_w_in(landed[l]["w_in_p"])
    G = {}

    parts, shapes = [], []
    for l in range(DEPTH):
        for n in SMALL_PER_LAYER:
            parts.append(grads[l][n].astype(F32).reshape(-1))
            shapes.append(grads[l][n].shape)
    parts += [drel.reshape(-1), jnp.sum(sq).reshape(1) * (0.5 / D_MODEL)]
    shapes += [drel.shape, (1,)]
    red = _unpack(sum_devices(exchange([_pack(parts, 128, 8, F32)], "gather_small", True)[0], "sum_small"), shapes)
    k = 0
    per = {n: [] for n in SMALL_PER_LAYER}
    for l in range(DEPTH):
        for n in SMALL_PER_LAYER:
            per[n].append(red[k])
            k += 1
    for n in SMALL_PER_LAYER:
        G[n] = jnp.stack(per[n]).reshape((DEPTH, 3, 2 * D_FF) if n == "conv_w" else W[n].shape)
    G["rel_bias"] = red[k]
    loss = red[k + 1][0]
    G["conv_w"] = lax.dynamic_slice_in_dim(G["conv_w"], me * 704, 704, axis=2)

    delta, new_m, new_v = {}, {}, {}
    for n in WEIGHTS:
        shp = W[n].shape
        if n == "w_up":
            v3 = lambda a: jnp.swapaxes(a, 1, 2)
            back = v3
            g = [landed[l]["w_upT"] for l in range(DEPTH)]
        else:
            v3 = lambda a: a.reshape(shp if len(shp) == 3 else (1,) + shp)
            back = lambda a: a.reshape(shp)
            g = [landed[l][n] for l in range(DEPTH)] if n in SHARDED else v3(G[n])
        g, d, nm, nv = adamw(v3(W[n]), g, v3(M[n]), v3(V[n]), "adamw_" + n)
        G[n], delta[n], new_m[n], new_v[n] = back(g), back(d), back(nm), back(nv)
    return (loss, dx[None], *[G[n] for n in WEIGHTS], *[delta[n] for n in WEIGHTS],
            *[new_m[n] for n in WEIGHTS], *[new_v[n] for n in WEIGHTS])
```

```python
import functools
import math

import numpy as np
import jax
import jax.numpy as jnp
from jax import lax
from jax.experimental import pallas as pl
from jax.experimental.pallas import tpu as pltpu

F32 = jnp.float32
MXU = jnp.bfloat16

N_DEV = 8
DEPTH = 4
D_MODEL = 1024
HEAD_DIM = 64
WINDOW = 128
SWA_Q_HEADS = 8
REL_BUCKETS = 32
REL_MAX_DIST = 128
MLA_QK_DIM = 96
ROPE_DIM = 32
ROPE_THETA = 10000.0
D_FF = 2816
EPS = 1e-6
NEG = -1e30
IN_COLS = 1956
IN_COLS_P = 2048
C_QA, C_KA, C_VA = 0, 512, 640
C_QF, C_KF, C_VF = 768, 1024, 1280
C_CQ, C_CKV, C_MISC = 1536, 1792, 1920
ROPE_LANE0 = 64
ADAM_LR, ADAM_B1, ADAM_B2, ADAM_EPS, ADAM_WD, ADAM_STEP = 0.001, 0.9, 0.999, 1e-08, 0.01, 10

VMEM_LIMIT = 56 * 1024 * 1024
PACK_COLS = 1024
PACK_ROW_ALIGN = 16


def _cparams(sem=None):
    return pltpu.CompilerParams(dimension_semantics=sem, vmem_limit_bytes=VMEM_LIMIT)


def _tile(n, pref):
    if n <= pref:
        return n
    t = pref - pref % 128
    while t >= 128:
        if n % t == 0:
            return t
        t -= 128
    return n


def _dot(a, b):
    return jnp.dot(a.astype(MXU), b.astype(MXU), preferred_element_type=F32)


def _dot_nt(a, b):
    return lax.dot_general(a.astype(MXU), b.astype(MXU), (((1,), (1,)), ((), ())),
                           preferred_element_type=F32)


def _rms_fwd(x, g):
    return x * lax.rsqrt(jnp.mean(x * x, axis=-1, keepdims=True) + EPS) * g


def _rms_bwd(dy, x, g, n=None):
    r = lax.rsqrt(jnp.mean(x * x, axis=-1, keepdims=True) + EPS)
    xh = x * r
    dg = jnp.sum(dy * xh, axis=0, keepdims=True)
    dxh = dy * g
    dx = r * (dxh - xh * jnp.mean(dxh * xh, axis=-1, keepdims=True))
    return dx, dg


def _acc_out(ref, val, first):
    @pl.when(first)
    def _():
        ref[...] = val

    @pl.when(jnp.logical_not(first))
    def _():
        ref[...] += val


def norm_matmul(x, g, w, name, lo_tiles=0, tn_pref=512, w_transposed=False, h_transposed=True):
    T, K = x.shape
    N = w.shape[0] if w_transposed else w.shape[1]
    tm, tn = _tile(T, 1024), _tile(N, tn_pref)

    def body(x_ref, g_ref, w_ref, o_ref, hT_ref, *rest):
        h_sc = rest[-1]
        j = pl.program_id(1)

        @pl.when(j == 0)
        def _():
            h = _rms_fwd(x_ref[...], g_ref[...])
            h_sc[...] = h.astype(MXU)
            hT_ref[...] = (h.T if h_transposed else h).astype(MXU)

        r = (_dot_nt if w_transposed else _dot)(h_sc[...], w_ref[...])
        o_ref[...] = r
        if lo_tiles:
            @pl.when(j < lo_tiles)
            def _():
                rest[0][...] = r.astype(MXU)

    h_spec = pl.BlockSpec((K, tm), lambda i, j: (0, i)) if h_transposed else pl.BlockSpec((tm, K), lambda i, j: (i, 0))
    out_specs = [pl.BlockSpec((tm, tn), lambda i, j: (i, j)), h_spec]
    out_shape = [jax.ShapeDtypeStruct((T, N), F32), jax.ShapeDtypeStruct((K, T) if h_transposed else (T, K), MXU)]
    if lo_tiles:
        out_specs.append(pl.BlockSpec((tm, tn), lambda i, j: (i, jnp.minimum(j, lo_tiles - 1))))
        out_shape.append(jax.ShapeDtypeStruct((T, lo_tiles * tn), MXU))
    return pl.pallas_call(
        body, name=name, grid=(T // tm, N // tn),
        in_specs=[pl.BlockSpec((tm, K), lambda i, j: (i, 0)),
                  pl.BlockSpec((1, K), lambda i, j: (0, 0)),
                  pl.BlockSpec((tn, K), lambda i, j: (j, 0)) if w_transposed else
                  pl.BlockSpec((K, tn), lambda i, j: (0, j))],
        out_specs=out_specs, out_shape=out_shape,
        scratch_shapes=[pltpu.VMEM((tm, K), MXU)],
        compiler_params=_cparams(("parallel", "arbitrary")),
    )(x, g, w)


def matmul_nn(a, b, name, out_dtype=F32):
    M, K = a.shape
    N = b.shape[1]
    tm, tn, tk = _tile(M, 1408), _tile(N, 1536), _tile(K, 1024)
    nk = K // tk

    def body(a_ref, b_ref, o_ref, acc):
        k = pl.program_id(2)
        part = _dot(a_ref[...], b_ref[...])
        _acc_out(acc, part, k == 0)

        @pl.when(k == nk - 1)
        def _():
            o_ref[...] = acc[...].astype(out_dtype)

    return pl.pallas_call(
        body, name=name, grid=(M // tm, N // tn, nk),
        in_specs=[pl.BlockSpec((tm, tk), lambda i, j, k: (i, k)),
                  pl.BlockSpec((tk, tn), lambda i, j, k: (k, j))],
        out_specs=pl.BlockSpec((tm, tn), lambda i, j, k: (i, j)),
        out_shape=jax.ShapeDtypeStruct((M, N), out_dtype),
        scratch_shapes=[pltpu.VMEM((tm, tn), F32)],
        compiler_params=_cparams(("parallel", "parallel", "arbitrary")),
    )(a, b)


def matmul_nt_normbwd(dy, w, x, g, dres, name):
    T, N = dy.shape
    K = w.shape[0]
    tm, tn = _tile(T, 1024), _tile(N, 1536)
    nj = N // tn

    def body(dy_ref, w_ref, x_ref, g_ref, dres_ref, dx_ref, dg_ref, acc):
        i, j = pl.program_id(0), pl.program_id(1)
        _acc_out(acc, _dot_nt(dy_ref[...], w_ref[...]), j == 0)

        @pl.when(j == nj - 1)
        def _():
            dx, dg = _rms_bwd(acc[...], x_ref[...], g_ref[...])
            dx_ref[...] = dres_ref[...] + dx
            _acc_out(dg_ref, dg, i == 0)

    return pl.pallas_call(
        body, name=name, grid=(T // tm, nj),
        in_specs=[pl.BlockSpec((tm, tn), lambda i, j: (i, j)),
                  pl.BlockSpec((K, tn), lambda i, j: (0, j)),
                  pl.BlockSpec((tm, K), lambda i, j: (i, 0)),
                  pl.BlockSpec((1, K), lambda i, j: (0, 0)),
                  pl.BlockSpec((tm, K), lambda i, j: (i, 0))],
        out_specs=[pl.BlockSpec((tm, K), lambda i, j: (i, 0)),
                   pl.BlockSpec((1, K), lambda i, j: (0, 0))],
        out_shape=[jax.ShapeDtypeStruct((T, K), F32), jax.ShapeDtypeStruct((1, K), F32)],
        scratch_shapes=[pltpu.VMEM((tm, K), F32)],
        compiler_params=_cparams(("arbitrary", "arbitrary")),
    )(dy, w, x, g, dres)


def loss_kernel(y, tgt):
    T, D = y.shape
    tm = _tile(T, 512)

    def body(y_ref, t_ref, dy_ref, acc_ref):
        e = y_ref[...] - t_ref[...]
        dy_ref[...] = e * (1.0 / D)
        _acc_out(acc_ref, jnp.sum(e * e, axis=0, keepdims=True), pl.program_id(0) == 0)

    return pl.pallas_call(
        body, name="loss", grid=(T // tm,),
        in_specs=[pl.BlockSpec((tm, D), lambda i: (i, 0)), pl.BlockSpec((tm, D), lambda i: (i, 0))],
        out_specs=[pl.BlockSpec((tm, D), lambda i: (i, 0)), pl.BlockSpec((1, D), lambda i: (0, 0))],
        out_shape=[jax.ShapeDtypeStruct((T, D), F32), jax.ShapeDtypeStruct((1, D), F32)],
        compiler_params=_cparams(("arbitrary",)),
    )(y, tgt)


def _rope_partner(x):
    lane = lax.broadcasted_iota(jnp.int32, (1, 128), 1)
    return jnp.where(lane < ROPE_LANE0 + ROPE_DIM // 2, pltpu.roll(x, 128 - ROPE_DIM // 2, 1),
                     pltpu.roll(x, ROPE_DIM // 2, 1))


def _rope_apply(x, cos, sin_signed):
    return x * cos + _rope_partner(x) * sin_signed


def _rope_apply_bwd(dy, cos, sin_signed):
    lane = lax.broadcasted_iota(jnp.int32, (1, 128), 1)
    rotary = (lane >= ROPE_LANE0) & (lane < ROPE_LANE0 + ROPE_DIM)
    return dy * cos + jnp.where(rotary, _rope_partner(dy * sin_signed), 0.0)


def rope_tables(T):
    pos = jnp.arange(T, dtype=F32)
    inv_freq = ROPE_THETA ** (-(jnp.arange(ROPE_DIM // 2, dtype=F32) * 2.0 / ROPE_DIM))
    ang = pos[:, None] * inv_freq[None, :]
    cos, sin = jnp.cos(ang), jnp.sin(ang)
    z = jnp.zeros((T, ROPE_LANE0), F32)
    z2 = jnp.zeros((T, 128 - ROPE_LANE0 - ROPE_DIM), F32)
    cosr = jnp.concatenate([z, cos, cos, z2], axis=1)
    sinr = jnp.concatenate([z, -sin, sin, z2], axis=1)
    return cosr, sinr


def mla_prep(proj, gq, gkv, wuq, wukv, cosr, sinr):
    T = proj.shape[0]
    tm = _tile(T, 512)

    def body(cq_ref, ckv_ref, misc_ref, gq_ref, gkv_ref, wuq_ref, wukv_ref, cos_ref, sin_ref,
             q_ref, k_ref, v_ref, cqT_ref, ckvT_ref):
        lane = lax.broadcasted_iota(jnp.int32, (1, 128), 1)
        cosr_, sinr_ = cos_ref[...], sin_ref[...]
        cosq = cosr_ + jnp.where(lane < ROPE_LANE0, 1.0, 0.0)
        cqn = _rms_fwd(cq_ref[...], gq_ref[...])
        cqT_ref[...] = cqn.T.astype(MXU)
        qm = _dot(cqn, wuq_ref[...])
        q_ref[...] = jnp.concatenate(
            [_rope_apply(qm[:, 128 * h:128 * (h + 1)], cosq, sinr_) for h in range(4)], axis=1).astype(MXU)
        ckvn = _rms_fwd(ckv_ref[...], gkv_ref[...])
        ckvT_ref[...] = ckvn.T.astype(MXU)
        kv = _dot(ckvn, wukv_ref[...])
        kr = _rope_apply(misc_ref[...], cosr_, sinr_)
        k_ref[...] = jnp.concatenate(
            [kv[:, 128 * h:128 * (h + 1)] + kr for h in range(4)], axis=1).astype(MXU)
        v_ref[...] = kv[:, 512:768].astype(MXU)

    row = lambda i: (i, 0)
    const = lambda i: (0, 0)
    return pl.pallas_call(
        body, name="mla_prep", grid=(T // tm,),
        in_specs=[pl.BlockSpec((tm, 256), lambda i: (i, C_CQ // 256)),
                  pl.BlockSpec((tm, 128), lambda i: (i, C_CKV // 128)),
                  pl.BlockSpec((tm, 128), lambda i: (i, C_MISC // 128)),
                  pl.BlockSpec((1, 256), const), pl.BlockSpec((1, 128), const),
                  pl.BlockSpec((256, 512), const), pl.BlockSpec((128, 768), const),
                  pl.BlockSpec((tm, 128), row), pl.BlockSpec((tm, 128), row)],
        out_specs=[pl.BlockSpec((tm, 512), row), pl.BlockSpec((tm, 512), row), pl.BlockSpec((tm, 256), row),
                   pl.BlockSpec((256, tm), lambda i: (0, i)), pl.BlockSpec((128, tm), lambda i: (0, i))],
        out_shape=[jax.ShapeDtypeStruct((T, 512), MXU), jax.ShapeDtypeStruct((T, 512), MXU),
                   jax.ShapeDtypeStruct((T, 256), MXU),
                   jax.ShapeDtypeStruct((256, T), MXU), jax.ShapeDtypeStruct((128, T), MXU)],
        compiler_params=_cparams(("parallel",)),
    )(proj, proj, proj, gq, gkv, wuq, wukv, cosr, sinr)


def mla_prep_bwd(dq, dk, dv, proj, gq, gkv, wuq, wukv, cosr, sinr):
    T = proj.shape[0]
    tm = _tile(T, 512)

    def body(dq_ref, dk_ref, dv_ref, cq_ref, ckv_ref, gq_ref, gkv_ref, wuq_ref, wukv_ref, cos_ref, sin_ref,
             dqm_ref, dkv_ref, dcq_ref, dckv_ref, dmisc_ref, dgq_ref, dgkv_ref):
        first = pl.program_id(0) == 0
        lane = lax.broadcasted_iota(jnp.int32, (1, 128), 1)
        cosr_, sinr_ = cos_ref[...], sin_ref[...]
        cosq = cosr_ + jnp.where(lane < ROPE_LANE0, 1.0, 0.0)
        dqv = dq_ref[...]
        dqm = jnp.concatenate(
            [_rope_apply_bwd(dqv[:, 128 * h:128 * (h + 1)], cosq, sinr_) for h in range(4)], axis=1)
        dqm_ref[...] = dqm.astype(MXU)
        dcq, dgq = _rms_bwd(_dot_nt(dqm, wuq_ref[...]), cq_ref[...], gq_ref[...])
        dcq_ref[...] = dcq
        _acc_out(dgq_ref, dgq, first)
        dkv_ = dk_ref[...]
        heads = [dkv_[:, 128 * h:128 * (h + 1)] for h in range(4)]
        dkr = heads[0] + heads[1] + heads[2] + heads[3]
        dmisc_ref[...] = _rope_apply_bwd(dkr, cosr_, sinr_)
        dkvm = jnp.concatenate([jnp.where(lane < ROPE_LANE0, hd, 0.0) for hd in heads] + [dv_ref[...]], axis=1)
        dkv_ref[...] = dkvm.astype(MXU)
        dckv, dgkv = _rms_bwd(_dot_nt(dkvm, wukv_ref[...]), ckv_ref[...], gkv_ref[...])
        dckv_ref[...] = dckv
        _acc_out(dgkv_ref, dgkv, first)

    row = lambda i: (i, 0)
    const = lambda i: (0, 0)
    return pl.pallas_call(
        body, name="mla_prep_bwd", grid=(T // tm,),
        in_specs=[pl.BlockSpec((tm, 512), row), pl.BlockSpec((tm, 512), row), pl.BlockSpec((tm, 256), row),
                  pl.BlockSpec((tm, 256), lambda i: (i, C_CQ // 256)),
                  pl.BlockSpec((tm, 128), lambda i: (i, C_CKV // 128)),
                  pl.BlockSpec((1, 256), const), pl.BlockSpec((1, 128), const),
                  pl.BlockSpec((256, 512), const), pl.BlockSpec((128, 768), const),
                  pl.BlockSpec((tm, 128), row), pl.BlockSpec((tm, 128), row)],
        out_specs=[pl.BlockSpec((tm, 512), row), pl.BlockSpec((tm, 768), row), pl.BlockSpec((tm, 256), row),
                   pl.BlockSpec((tm, 128), row), pl.BlockSpec((tm, 128), row),
                   pl.BlockSpec((1, 256), const), pl.BlockSpec((1, 128), const)],
        out_shape=[jax.ShapeDtypeStruct((T, 512), MXU), jax.ShapeDtypeStruct((T, 768), MXU),
                   jax.ShapeDtypeStruct((T, 256), F32), jax.ShapeDtypeStruct((T, 128), F32),
                   jax.ShapeDtypeStruct((T, 128), F32),
                   jax.ShapeDtypeStruct((1, 256), F32), jax.ShapeDtypeStruct((1, 128), F32)],
        compiler_params=_cparams(("arbitrary",)),
    )(dq, dk, dv, proj, proj, gq, gkv, wuq, wukv, cosr, sinr)


def _split3(x):
    hi = x.astype(MXU)
    r1 = x - hi.astype(F32)
    mid = r1.astype(MXU)
    lo = (r1 - mid.astype(F32)).astype(MXU)
    return hi, mid, lo


def _tri_matmul(tri, x):
    hi, mid, lo = _split3(x)
    d = lambda p: jnp.dot(tri, p, preferred_element_type=F32)
    return d(hi) + d(mid) + d(lo)


def _log_sigmoid(z):
    return jnp.minimum(z, 0.0) - jnp.log(1.0 + jnp.exp(-jnp.abs(z)))


def fox_gate(proj, fbias):
    T = proj.shape[0]
    tb = _tile(T, 512)

    def body(misc_ref, b_ref, fc_ref, fr_ref, frep_ref, carry):
        @pl.when(pl.program_id(0) == 0)
        def _():
            carry[...] = jnp.zeros_like(carry)

        lane = lax.broadcasted_iota(jnp.int32, (1, 128), 1)
        lf = jnp.where(lane < 4, _log_sigmoid(misc_ref[...] + b_ref[...]), 0.0)
        r = lax.broadcasted_iota(jnp.int32, (tb, tb), 0)
        c = lax.broadcasted_iota(jnp.int32, (tb, tb), 1)
        tri = jnp.where(r >= c, 1.0, 0.0).astype(MXU)
        F = _tri_matmul(tri, lf) + carry[...]
        carry[...] = carry[...] + jnp.sum(lf, axis=0, keepdims=True)
        fc_ref[0] = F
        fc_ref[1] = pltpu.roll(F, 126, 1)
        ft = F.T[0:8, :]
        fr_ref[0] = ft
        fr_ref[1] = pltpu.roll(ft, 6, 0)
        for h in range(4):
            frep_ref[h] = jnp.broadcast_to(_lane_pick(F, h), (tb, 128))

    return pl.pallas_call(
        body, name="fox_gate", grid=(T // tb,),
        in_specs=[pl.BlockSpec((tb, 128), lambda i: (i, C_MISC // 128)), pl.BlockSpec((1, 128), lambda i: (0, 0))],
        out_specs=[pl.BlockSpec((2, tb, 128), lambda i: (0, i, 0)), pl.BlockSpec((2, 8, tb), lambda i: (0, 0, i)),
                   pl.BlockSpec((4, tb, 128), lambda i: (0, i, 0))],
        out_shape=[jax.ShapeDtypeStruct((2, T, 128), F32), jax.ShapeDtypeStruct((2, 8, T), F32),
                   jax.ShapeDtypeStruct((4, T, 128), F32)],
        scratch_shapes=[pltpu.VMEM((1, 128), F32)],
        compiler_params=_cparams(("arbitrary",)),
    )(proj, fbias)


def fox_gate_bwd(dFq, dFk, proj, fbias):
    T = proj.shape[0]
    tb = _tile(T, 512)
    nb = T // tb

    def body(dq_ref, dk_ref, misc_ref, b_ref, dm_ref, db_ref, carry):
        first = pl.program_id(0) == 0

        @pl.when(first)
        def _():
            carry[...] = jnp.zeros_like(carry)

        lane = lax.broadcasted_iota(jnp.int32, (1, 128), 1)
        dF = jnp.where(lane < 4, (dq_ref[0] + dk_ref[0]) + pltpu.roll(dq_ref[1] + dk_ref[1], 2, 1), 0.0)
        r = lax.broadcasted_iota(jnp.int32, (tb, tb), 0)
        c = lax.broadcasted_iota(jnp.int32, (tb, tb), 1)
        tri = jnp.where(r <= c, 1.0, 0.0).astype(MXU)
        dlf = _tri_matmul(tri, dF) + carry[...]
        carry[...] = carry[...] + jnp.sum(dF, axis=0, keepdims=True)
        z = misc_ref[...] + b_ref[...]
        dz = jnp.where(lane < 4, dlf * (1.0 / (1.0 + jnp.exp(z))), 0.0)
        dm_ref[...] = dz
        _acc_out(db_ref, jnp.sum(dz, axis=0, keepdims=True), first)

    return pl.pallas_call(
        body, name="fox_gate_bwd", grid=(nb,),
        in_specs=[pl.BlockSpec((2, tb, 128), lambda i: (0, nb - 1 - i, 0)),
                  pl.BlockSpec((2, tb, 128), lambda i: (0, nb - 1 - i, 0)),
                  pl.BlockSpec((tb, 128), lambda i: (nb - 1 - i, C_MISC // 128)),
                  pl.BlockSpec((1, 128), lambda i: (0, 0))],
        out_specs=[pl.BlockSpec((tb, 128), lambda i: (nb - 1 - i, 0)), pl.BlockSpec((1, 128), lambda i: (0, 0))],
        out_shape=[jax.ShapeDtypeStruct((T, 128), F32), jax.ShapeDtypeStruct((1, 128), F32)],
        scratch_shapes=[pltpu.VMEM((1, 128), F32)],
        compiler_params=_cparams(("arbitrary",)),
    )(dFq, dFk, proj, fbias)


FLASH_TILE = 512


def _row_stat_tile(a, b, n):
    at = jnp.broadcast_to(a, (n, 128)).T[0:8, :]
    bt = jnp.broadcast_to(b, (n, 128)).T[0:8, :]
    sub = lax.broadcasted_iota(jnp.int32, (8, 1), 0)
    return jnp.where(sub == 0, at, jnp.where(sub == 1, bt, 0.0))


def _col_stat_tile(a, b):
    lane = lax.broadcasted_iota(jnp.int32, (1, 128), 1)
    return jnp.where(lane == 0, a, jnp.where(lane == 1, b, 0.0))


def _lane_pick(x, h):
    lane = lax.broadcasted_iota(jnp.int32, (1, 128), 1)
    return jnp.sum(jnp.where(lane == h, x, 0.0), axis=1, keepdims=True)


def _half_mask(h):
    lane = lax.broadcasted_iota(jnp.int32, (1, 128), 1)
    return (lane // HEAD_DIM) == h


def _call_hosting(body, name, grid, args, in_specs, out_specs, out_shape, scratch, exch):
    n_out = len(out_shape)
    if exch is not None:
        body, (xargs, xin, xout, xshape, xscratch) = hosted_exchange(
            body, len(args), n_out, len(scratch), grid, *exch)
        args, in_specs, out_specs = args + xargs, in_specs + xin, out_specs + xout
        out_shape, scratch = out_shape + xshape, scratch + xscratch
    res = pl.pallas_call(
        body, name=name, grid=grid, in_specs=in_specs, out_specs=out_specs, out_shape=out_shape,
        scratch_shapes=scratch, compiler_params=_cparams(("arbitrary",) * len(grid)),
    )(*args)
    return res[:n_out], res[n_out:]


def flash_fwd(q, k, v, frep, frow, *, qblk, kblk, vblk, nq, scale, name, exch=None):
    T = q.shape[0]
    tk = _tile(T, FLASH_TILE)
    tq = _tile(T, 2 * FLASH_TILE)
    per_q = tq // tk
    wq = 128 * nq
    has_f = frep is not None

    def body(*refs):
        if has_f:
            q_ref, k_ref, v_ref, fk_ref, fr_ref, o_ref, lr_ref, vT_sc, m_sc, acc_sc = refs
        else:
            q_ref, k_ref, v_ref, o_ref, lr_ref, vT_sc, m_sc, acc_sc = refs
        i = pl.program_id(1)

        @pl.when(i == 0)
        def _():
            vT_sc[...] = v_ref[...].astype(F32).T.astype(MXU)

        key_row = lax.broadcasted_iota(jnp.int32, (tk, 1), 0)
        q_col = lax.broadcasted_iota(jnp.int32, (1, tq), 1)
        row_half = lax.broadcasted_iota(jnp.int32, (128, 1), 0) // HEAD_DIM
        qb = q_ref[...].astype(F32) * scale
        if nq == 1:
            qhs = [jnp.where(_half_mask(h), qb, 0).astype(MXU) for h in range(2)]
        else:
            qhs = [qb[:, 128 * h:128 * (h + 1)].astype(MXU) for h in range(2)]
        for h in range(2):
            m_sc[h] = jnp.full((1, tq), NEG, F32)
            acc_sc[h] = jnp.zeros((128, tq), F32)

        def make_step(diag_block):
            def step(j, carry):
                off = pl.multiple_of(j * tk, tk)
                ks = k_ref[pl.ds(off, tk), :]
                vT = vT_sc[:, pl.ds(off, tk)]
                for h in range(2):
                    kh = ks if nq == 1 else ks[:, 128 * h:128 * (h + 1)]
                    sT = _dot_nt(kh, qhs[h])
                    if has_f:
                        fk = fk_ref[h, pl.ds(off, tk), :]
                        sT = sT + (fr_ref[0, h:h + 1, :] - jnp.concatenate([fk] * (tq // 128), axis=1))
                    if diag_block is not None:
                        sT = jnp.where(key_row + diag_block * tk <= q_col, sT, NEG)
                    m_prev = m_sc[h]
                    m_new = jnp.maximum(m_prev, jnp.max(sT, axis=0, keepdims=True))
                    alpha = jnp.exp(m_prev - m_new)
                    pT = jnp.exp(sT - m_new)
                    vTh = jnp.where(row_half == h, vT, jnp.ones_like(vT))
                    acc_sc[h] = alpha * acc_sc[h] + _dot(vTh, pT)
                    m_sc[h] = m_new
                return carry
            return step

        lax.fori_loop(0, per_q * i, make_step(None), 0)
        for d in range(per_q):
            make_step(d)(per_q * i + d, 0)
        outs, lses = [], []
        for h in range(2):
            acc = acc_sc[h]
            outs.append(acc / pltpu.roll(acc, HEAD_DIM, 0))
            l = acc_sc[h, HEAD_DIM * (1 - h):HEAD_DIM * (1 - h) + 1, :]
            lses.append(m_sc[h] + jnp.log(l))
        o_ref[...] = jnp.where(row_half == 0, outs[0], outs[1]).T
        sub = lax.broadcasted_iota(jnp.int32, (8, 1), 0)
        lr_ref[0] = jnp.where(sub == 0, lses[0], jnp.where(sub == 1, lses[1], 0.0))

    in_specs = [pl.BlockSpec((tq, wq), lambda p, i: (i, qblk + p)),
                pl.BlockSpec((T, wq), lambda p, i: (0, kblk + p)),
                pl.BlockSpec((T, 128), lambda p, i: (0, vblk + p))]
    args = [q, k, v]
    if has_f:
        in_specs += [pl.BlockSpec((2, T, 128), lambda p, i: (p, 0, 0)),
                     pl.BlockSpec((1, 8, tq), lambda p, i: (p, 0, i))]
        args += [frep, frow]
    out_specs = [pl.BlockSpec((tq, 128), lambda p, i: (i, p)), pl.BlockSpec((1, 8, tq), lambda p, i: (p, 0, i))]
    out_shape = [jax.ShapeDtypeStruct((T, 256), F32), jax.ShapeDtypeStruct((2, 8, T), F32)]
    scratch = [pltpu.VMEM((128, T), MXU), pltpu.VMEM((2, 1, tq), F32), pltpu.VMEM((2, 128, tq), F32)]
    return _call_hosting(body, name, (2, T // tq), args, in_specs, out_specs, out_shape, scratch, exch)


def flash_bwd(q, k, v, do, o, lrow, fcol, frow, *, qblk, kblk, vblk, nq, scale, name, exch=None):
    T = q.shape[0]
    tq = tk = _tile(T, FLASH_TILE)
    wq = 128 * nq
    nqb = T // tq
    has_f = fcol is not None

    def body(*refs):
        if has_f:
            (q_ref, k_ref, v_ref, do_ref, o_ref, lr_ref, fc_ref, fr_ref,
             dq_ref, dk_ref, dv_ref, df_ref, dfq_ref, dk_sc, dv_sc, dqT_sc, d_sc, df_sc, dfq_sc) = refs
        else:
            q_ref, k_ref, v_ref, do_ref, o_ref, lr_ref, dq_ref, dk_ref, dv_ref, dk_sc, dv_sc, dqT_sc, d_sc = refs
        j = pl.program_id(1)
        diag = lax.broadcasted_iota(jnp.int32, (tk, 1), 0) <= lax.broadcasted_iota(jnp.int32, (1, tq), 1)
        hms = [_half_mask(h) for h in range(2)]

        @pl.when(j == 0)
        def _():
            dqT_sc[...] = jnp.zeros_like(dqT_sc)
            if has_f:
                dfq_sc[...] = jnp.zeros_like(dfq_sc)

            def delta(b, carry):
                off = pl.multiple_of(b * tq, tq)
                prod = do_ref[pl.ds(off, tq), :] * o_ref[pl.ds(off, tq), :]
                Ds = [jnp.sum(jnp.where(hms[h], prod, 0.0), axis=1, keepdims=True) for h in range(2)]
                d_sc[:, pl.ds(off, tq)] = _row_stat_tile(Ds[0], Ds[1], tq)
                return carry

            lax.fori_loop(0, nqb, delta, 0)

        kb = k_ref[...]
        vb = v_ref[...]
        if nq == 1:
            khs = [jnp.where(hms[h], kb, 0).astype(MXU) for h in range(2)]
        else:
            khs = [kb[:, 128 * h:128 * (h + 1)].astype(MXU) for h in range(2)]
        kTs = [kh.astype(F32).T.astype(MXU) for kh in khs]
        kss = [(kh.astype(F32) * scale).astype(MXU) for kh in khs]
        vhs = [jnp.where(hms[h], vb, 0).astype(MXU) for h in range(2)]
        fks = [_lane_pick(fc_ref[0], h) for h in range(2)] if has_f else None
        dv_sc[...] = jnp.zeros_like(dv_sc)
        dk_sc[...] = jnp.zeros_like(dk_sc)
        if has_f:
            df_sc[...] = jnp.zeros_like(df_sc)

        def make_step(masked):
            def step(i, carry):
                off = pl.multiple_of(i * tq, tq)
                qs = q_ref[pl.ds(off, tq), :]
                dos = do_ref[pl.ds(off, tq), :]
                for h in range(2):
                    qh = qs if nq == 1 else qs[:, 128 * h:128 * (h + 1)]
                    sT = _dot_nt(kss[h], qh)
                    if has_f:
                        sT = sT + (fr_ref[0, h:h + 1, pl.ds(off, tq)] - fks[h])
                    pT = jnp.exp(sT - lr_ref[0, h:h + 1, pl.ds(off, tq)])
                    if masked:
                        pT = jnp.where(diag, pT, 0.0)
                    dsT = pT * (_dot_nt(vhs[h], dos) - d_sc[h:h + 1, pl.ds(off, tq)])
                    dv_sc[...] += _dot(pT, jnp.where(hms[h], dos, 0))
                    qq = jnp.where(hms[h], qs, 0) if nq == 1 else qh
                    dk_sc[h if nq == 2 else 0] += _dot(dsT, qq)
                    dqT_sc[h if nq == 2 else 0, :, pl.ds(off, tq)] += _dot(kTs[h], dsT)
                    if has_f:
                        part = dsT[:, 0:128]
                        for c in range(1, tq // 128):
                            part = part + dsT[:, 128 * c:128 * (c + 1)]
                        df_sc[h] += part
                        dfq_sc[h:h + 1, pl.ds(off, tq)] += jnp.sum(dsT, axis=0, keepdims=True)
                return carry
            return step

        make_step(True)(j, 0)
        lax.fori_loop(j + 1, nqb, make_step(False), 0)
        if nq == 1:
            dk_ref[...] = dk_sc[0] * scale
        else:
            dk_ref[...] = jnp.concatenate([dk_sc[0], dk_sc[1]], axis=1) * scale
        dv_ref[...] = dv_sc[...]
        if has_f:
            df_ref[0] = _col_stat_tile(-jnp.sum(df_sc[0], axis=1, keepdims=True),
                                       -jnp.sum(df_sc[1], axis=1, keepdims=True))

        @pl.when(j == nqb - 1)
        def _():
            if nq == 1:
                dq_ref[...] = dqT_sc[0].T * scale
            else:
                dq_ref[...] = jnp.concatenate([dqT_sc[0].T, dqT_sc[1].T], axis=1) * scale
            if has_f:
                sub = lax.broadcasted_iota(jnp.int32, (128, 1), 0)
                rows = jnp.where(sub == 0, dfq_sc[0:1, :], jnp.where(sub == 1, dfq_sc[1:2, :], 0.0))
                dfq_ref[0] = rows.T

    in_specs = [pl.BlockSpec((T, wq), lambda p, j: (0, qblk + p)),
                pl.BlockSpec((tk, wq), lambda p, j: (j, kblk + p)),
                pl.BlockSpec((tk, 128), lambda p, j: (j, vblk + p)),
                pl.BlockSpec((T, 128), lambda p, j: (0, p)),
                pl.BlockSpec((T, 128), lambda p, j: (0, p)),
                pl.BlockSpec((1, 8, T), lambda p, j: (p, 0, 0))]
    args = [q, k, v, do, o, lrow]
    out_specs = [pl.BlockSpec((T, wq), lambda p, j: (0, p)),
                 pl.BlockSpec((tk, wq), lambda p, j: (j, p)), pl.BlockSpec((tk, 128), lambda p, j: (j, p))]
    out_shape = [jax.ShapeDtypeStruct((T, 2 * wq), F32), jax.ShapeDtypeStruct((T, 2 * wq), F32),
                 jax.ShapeDtypeStruct((T, 256), F32)]
    scratch = [pltpu.VMEM((nq, tk, 128), F32), pltpu.VMEM((tk, 128), F32), pltpu.VMEM((nq, 128, T), F32),
               pltpu.VMEM((8, T), F32)]
    if has_f:
        in_specs += [pl.BlockSpec((1, tk, 128), lambda p, j: (p, j, 0)),
                     pl.BlockSpec((1, 8, T), lambda p, j: (p, 0, 0))]
        args += [fcol, frow]
        out_specs += [pl.BlockSpec((1, tk, 128), lambda p, j: (p, j, 0)),
                      pl.BlockSpec((1, T, 128), lambda p, j: (p, 0, 0))]
        out_shape += [jax.ShapeDtypeStruct((2, T, 128), F32), jax.ShapeDtypeStruct((2, T, 128), F32)]
        scratch += [pltpu.VMEM((2, tk, 128), F32), pltpu.VMEM((8, T), F32)]
    return _call_hosting(body, name, (2, T // tk), args, in_specs, out_specs, out_shape, scratch, exch)


def _swa_align(pair, e, h):
    sel = jnp.where(_half_mask(e), pair, 0.0)
    if e == h:
        return sel
    return pltpu.roll(sel, HEAD_DIM, 1)


def _swa_mask(n):
    W = WINDOW
    qi = lax.broadcasted_iota(jnp.int32, (W, 2 * W), 0) + W
    kj = lax.broadcasted_iota(jnp.int32, (W, 2 * W), 1)
    dist = qi - kj
    return (dist >= 0) & (dist < W) & ((n > 0) | (kj >= W))


def swa_fwd(proj, bias, sinks, exch=None):
    T = proj.shape[0]
    W = WINDOW
    nb = T // W
    scale = HEAD_DIM ** -0.5

    def body(sink_ref, q_ref, kp_ref, kc_ref, vp_ref, vc_ref, b_ref, o_ref, l_ref):
        n = pl.program_id(0)
        mask = _swa_mask(n)
        kband = jnp.concatenate([kp_ref[...], kc_ref[...]], axis=0).astype(MXU)
        vband = jnp.concatenate([vp_ref[...], vc_ref[...]], axis=0).astype(MXU)
        lane = lax.broadcasted_iota(jnp.int32, (1, 128), 1)
        lse_tile = jnp.zeros((W, 128), F32)
        pairs = []
        for h in range(2):
            full = []
            for g in range(4):
                hq = 4 * h + g
                qa = _swa_align(q_ref[:, 128 * (hq // 2):128 * (hq // 2 + 1)], hq % 2, h)
                s = _dot_nt(qa, kband) * scale + b_ref[hq]
                s = jnp.where(mask, s, NEG)
                sink = sink_ref[hq]
                m = jnp.maximum(jnp.max(s, axis=1, keepdims=True), sink)
                e = jnp.exp(s - m)
                l = jnp.sum(e, axis=1, keepdims=True) + jnp.exp(sink - m)
                r = jnp.where(_half_mask(h), _dot(e, vband), 0.0) / l
                full.append(r + pltpu.roll(r, HEAD_DIM, 1))
                lse_tile = jnp.where(lane == hq, m + jnp.log(l), lse_tile)
            pairs.append(jnp.where(_half_mask(0), full[0], full[1]))
            pairs.append(jnp.where(_half_mask(0), full[2], full[3]))
        o_ref[...] = jnp.concatenate(pairs, axis=1)
        l_ref[...] = lse_tile

    prev = lambda n: (jnp.maximum(n - 1, 0), C_KA // 128)
    cur = lambda n: (n, C_KA // 128)
    prev_v = lambda n: (jnp.maximum(n - 1, 0), C_VA // 128)
    cur_v = lambda n: (n, C_VA // 128)
    return _call_hosting(
        body, "swa_fwd", (nb,), [sinks, proj, proj, proj, proj, proj, bias],
        [pl.BlockSpec(memory_space=pltpu.SMEM),
         pl.BlockSpec((W, 512), lambda n: (n, 0)),
         pl.BlockSpec((W, 128), prev), pl.BlockSpec((W, 128), cur),
         pl.BlockSpec((W, 128), prev_v), pl.BlockSpec((W, 128), cur_v),
         pl.BlockSpec((8, W, 2 * W), lambda n: (0, 0, 0))],
        [pl.BlockSpec((W, 512), lambda n: (n, 0)), pl.BlockSpec((W, 128), lambda n: (n, 0))],
        [jax.ShapeDtypeStruct((T, 512), F32), jax.ShapeDtypeStruct((T, 128), F32)], [], exch)


def swa_bwd(proj, bias, sinks, do, o, lse, exch=None):
    T = proj.shape[0]
    W = WINDOW
    nb = T // W
    scale = HEAD_DIM ** -0.5

    def body(sink_ref, q_ref, kp_ref, kc_ref, vp_ref, vc_ref, b_ref, do_ref, o_ref, l_ref,
             dq_ref, dk_ref, dv_ref, db_ref, dsk_ref, ck, cv):
        n = pl.program_id(0)

        @pl.when(n == 0)
        def _():
            ck[...] = jnp.zeros_like(ck)
            cv[...] = jnp.zeros_like(cv)
            db_ref[...] = jnp.zeros_like(db_ref)
            dsk_ref[...] = jnp.zeros_like(dsk_ref)

        @pl.when(n < nb)
        def _():
            mask = _swa_mask(n)
            kb32 = jnp.concatenate([kp_ref[...], kc_ref[...]], axis=0)
            vb32 = jnp.concatenate([vp_ref[...], vc_ref[...]], axis=0)
            kband = kb32.astype(MXU)
            sub = lax.broadcasted_iota(jnp.int32, (8, 1), 0)
            dk_band = jnp.zeros((2 * W, 128), F32)
            dv_band = jnp.zeros((2 * W, 128), F32)
            dsk = jnp.zeros((8, 128), F32)
            dq_pairs = []
            mask4 = jnp.concatenate([mask] * 4, axis=0)
            for h in range(2):
                hm = _half_mask(h)
                km = jnp.where(hm, kb32, 0.0).astype(MXU)
                vm = jnp.where(hm, vb32, 0.0).astype(MXU)
                pbs = [slice(128 * ((4 * h + g) // 2), 128 * ((4 * h + g) // 2 + 1)) for g in range(4)]
                q4 = jnp.concatenate([_swa_align(q_ref[:, pbs[g]], g % 2, h) for g in range(4)], axis=0)
                do4 = jnp.concatenate([_swa_align(do_ref[:, pbs[g]], g % 2, h) for g in range(4)], axis=0)
                D4 = jnp.concatenate(
                    [jnp.sum(jnp.where(_half_mask(g % 2), do_ref[:, pbs[g]] * o_ref[:, pbs[g]], 0.0), axis=1,
                             keepdims=True) for g in range(4)], axis=0)
                lse4 = jnp.concatenate([_lane_pick(l_ref[...], 4 * h + g) for g in range(4)], axis=0)
                sink4 = jnp.concatenate([jnp.full((W, 1), sink_ref[4 * h + g], F32) for g in range(4)], axis=0)
                s = _dot_nt(q4, kband) * scale + b_ref[4 * h:4 * h + 4].reshape(4 * W, 2 * W)
                p = jnp.where(mask4, jnp.exp(s - lse4), 0.0)
                sd = jnp.exp(sink4 - lse4) * D4
                for g in range(4):
                    dsk = dsk + jnp.where(sub == 4 * h + g,
                                          -jnp.sum(sd[W * g:W * (g + 1)], axis=0, keepdims=True), 0.0)
                ds = p * (_dot_nt(do4, vm) - D4)
                db_ref[4 * h:4 * h + 4] += ds.reshape(4, W, 2 * W)
                dq = _dot(ds, km) * scale
                dq = dq + pltpu.roll(dq, HEAD_DIM, 1)
                dk_band = dk_band + _dot(ds.T, q4) * scale
                dv_band = dv_band + _dot(p.T, do4)
                dq_pairs.append(jnp.where(_half_mask(0), dq[0:W], dq[W:2 * W]))
                dq_pairs.append(jnp.where(_half_mask(0), dq[2 * W:3 * W], dq[3 * W:4 * W]))
            dq_ref[...] = jnp.concatenate(dq_pairs, axis=1)
            dsk_ref[...] += dsk
            dk_ref[...] = ck[...] + dk_band[0:W]
            dv_ref[...] = cv[...] + dv_band[0:W]
            ck[...] = dk_band[W:2 * W]
            cv[...] = dv_band[W:2 * W]

        @pl.when(n == nb)
        def _():
            dk_ref[...] = ck[...]
            dv_ref[...] = cv[...]

    cl = lambda n: jnp.minimum(n, nb - 1)
    pv = lambda n: jnp.maximum(jnp.minimum(n, nb - 1) - 1, 0)
    return _call_hosting(
        body, "swa_bwd", (nb + 1,), [sinks, proj, proj, proj, proj, proj, bias, do, o, lse],
        [pl.BlockSpec(memory_space=pltpu.SMEM),
         pl.BlockSpec((W, 512), lambda n: (cl(n), 0)),
         pl.BlockSpec((W, 128), lambda n: (pv(n), C_KA // 128)),
         pl.BlockSpec((W, 128), lambda n: (cl(n), C_KA // 128)),
         pl.BlockSpec((W, 128), lambda n: (pv(n), C_VA // 128)),
         pl.BlockSpec((W, 128), lambda n: (cl(n), C_VA // 128)),
         pl.BlockSpec((8, W, 2 * W), lambda n: (0, 0, 0)),
         pl.BlockSpec((W, 512), lambda n: (cl(n), 0)),
         pl.BlockSpec((W, 512), lambda n: (cl(n), 0)),
         pl.BlockSpec((W, 128), lambda n: (cl(n), 0))],
        [pl.BlockSpec((W, 512), lambda n: (cl(n), 0)),
         pl.BlockSpec((W, 128), lambda n: (jnp.maximum(n - 1, 0), 0)),
         pl.BlockSpec((W, 128), lambda n: (jnp.maximum(n - 1, 0), 0)),
         pl.BlockSpec((8, W, 2 * W), lambda n: (0, 0, 0)),
         pl.BlockSpec((8, 128), lambda n: (0, 0))],
        [jax.ShapeDtypeStruct((T, 512), F32), jax.ShapeDtypeStruct((T, 128), F32),
         jax.ShapeDtypeStruct((T, 128), F32), jax.ShapeDtypeStruct((8, W, 2 * W), F32),
         jax.ShapeDtypeStruct((8, 128), F32)],
        [pltpu.VMEM((W, 128), F32), pltpu.VMEM((W, 128), F32)], exch)


def swa_bias_table(rel_bias):
    W = WINDOW
    qi = jnp.arange(W, dtype=jnp.int32)[:, None] + W
    kj = jnp.arange(2 * W, dtype=jnp.int32)[None, :]
    dist = qi - kj
    max_exact = REL_BUCKETS // 2
    d = jnp.maximum(dist, 0)
    log_ratio = jnp.log(jnp.maximum(d, 1).astype(F32) / max_exact) / math.log(REL_MAX_DIST / max_exact)
    large = jnp.minimum(max_exact + (log_ratio * (REL_BUCKETS - max_exact)).astype(jnp.int32), REL_BUCKETS - 1)
    bucket = jnp.where(d < max_exact, d, large)
    bucket = bucket.reshape(-1)
    onehot = (bucket[None, :] == jnp.arange(REL_BUCKETS, dtype=jnp.int32)[:, None]).astype(F32)
    bias = jnp.dot(rel_bias.astype(F32).T, onehot, precision=lax.Precision.HIGHEST)
    return bias.reshape(SWA_Q_HEADS, W, 2 * W), bucket


def attn_out(oa, ob, oc, gn, wout, gpost, x):
    T = x.shape[0]
    tm = _tile(T, 512)

    def body(oa_ref, ob_ref, oc_ref, gn_ref, w_ref, gp_ref, x_ref, x2_ref, y_ref, mT_ref):
        g = gn_ref[...]
        mixed = jnp.concatenate([_rms_fwd(oa_ref[...], g[:, 0:512]), _rms_fwd(ob_ref[...], g[:, 512:768]),
                                 _rms_fwd(oc_ref[...], g[:, 768:1024])], axis=1)
        mT_ref[...] = mixed.T.astype(MXU)
        y = _dot(mixed, w_ref[...])
        y_ref[...] = y
        x2_ref[...] = x_ref[...] + _rms_fwd(y, gp_ref[...])

    row = lambda i: (i, 0)
    const = lambda i: (0, 0)
    return pl.pallas_call(
        body, name="attn_out", grid=(T // tm,),
        in_specs=[pl.BlockSpec((tm, 512), row), pl.BlockSpec((tm, 256), row), pl.BlockSpec((tm, 256), row),
                  pl.BlockSpec((1, 1024), const), pl.BlockSpec((1024, 1024), const), pl.BlockSpec((1, 1024), const),
                  pl.BlockSpec((tm, 1024), row)],
        out_specs=[pl.BlockSpec((tm, 1024), row), pl.BlockSpec((tm, 1024), row),
                   pl.BlockSpec((1024, tm), lambda i: (0, i))],
        out_shape=[jax.ShapeDtypeStruct((T, 1024), F32), jax.ShapeDtypeStruct((T, 1024), F32),
                   jax.ShapeDtypeStruct((1024, T), MXU)],
        compiler_params=_cparams(("parallel",)),
    )(oa, ob, oc, gn, wout, gpost, x)


def attn_out_bwd(dx2, y, oa, ob, oc, gn, wout, gpost):
    T = dx2.shape[0]
    tm = _tile(T, 512)

    def body(dx_ref, y_ref, oa_ref, ob_ref, oc_ref, gn_ref, w_ref, gp_ref,
             dy_ref, da_ref, db_ref, dc_ref, dgn_ref, dgp_ref):
        first = pl.program_id(0) == 0
        dy, dgp = _rms_bwd(dx_ref[...], y_ref[...], gp_ref[...])
        dy_ref[...] = dy.astype(MXU)
        _acc_out(dgp_ref, dgp, first)
        dm = _dot_nt(dy, w_ref[...])
        g = gn_ref[...]
        da, dga = _rms_bwd(dm[:, 0:512], oa_ref[...], g[:, 0:512])
        db, dgb = _rms_bwd(dm[:, 512:768], ob_ref[...], g[:, 512:768])
        dc, dgc = _rms_bwd(dm[:, 768:1024], oc_ref[...], g[:, 768:1024])
        da_ref[...] = da
        db_ref[...] = db
        dc_ref[...] = dc
        _acc_out(dgn_ref, jnp.concatenate([dga, dgb, dgc], axis=1), first)

    row = lambda i: (i, 0)
    const = lambda i: (0, 0)
    return pl.pallas_call(
        body, name="attn_out_bwd", grid=(T // tm,),
        in_specs=[pl.BlockSpec((tm, 1024), row), pl.BlockSpec((tm, 1024), row),
                  pl.BlockSpec((tm, 512), row), pl.BlockSpec((tm, 256), row), pl.BlockSpec((tm, 256), row),
                  pl.BlockSpec((1, 1024), const), pl.BlockSpec((1024, 1024), const), pl.BlockSpec((1, 1024), const)],
        out_specs=[pl.BlockSpec((tm, 1024), row), pl.BlockSpec((tm, 512), row), pl.BlockSpec((tm, 256), row),
                   pl.BlockSpec((tm, 256), row), pl.BlockSpec((1, 1024), const), pl.BlockSpec((1, 1024), const)],
        out_shape=[jax.ShapeDtypeStruct((T, 1024), MXU), jax.ShapeDtypeStruct((T, 512), F32),
                   jax.ShapeDtypeStruct((T, 256), F32), jax.ShapeDtypeStruct((T, 256), F32),
                   jax.ShapeDtypeStruct((1, 1024), F32), jax.ShapeDtypeStruct((1, 1024), F32)],
        compiler_params=_cparams(("arbitrary",)),
    )(dx2, y, oa, ob, oc, gn, wout, gpost)


FF_TILE = 256
_GELU_C = math.sqrt(2.0 / math.pi)


def _gelu(x):
    return 0.5 * x * (1.0 + jnp.tanh(_GELU_C * (x + 0.044715 * x * x * x)))


def _gelu_with_grad(x):
    x2 = x * x
    t = jnp.tanh(_GELU_C * x * (1.0 + 0.044715 * x2))
    h = 0.5 * (1.0 + t)
    return x * h, h + (0.5 * _GELU_C) * x * (1.0 - t * t) * (1.0 + (3 * 0.044715) * x2)


def _conv_taps(u, hal_ref, first):
    row = lax.broadcasted_iota(jnp.int32, (8, 1), 0)
    h6 = jnp.where(first, 0.0, hal_ref[6:7, :])
    h7 = jnp.where(first, 0.0, hal_ref[7:8, :])
    r1, r2 = pltpu.roll(u, 1, 0), pltpu.roll(u, 2, 0)
    r1 = jnp.concatenate([jnp.where(row == 0, h7, r1[0:8]), r1[8:]], axis=0)
    r2 = jnp.concatenate([jnp.where(row == 0, h6, jnp.where(row == 1, h7, r2[0:8])), r2[8:]], axis=0)
    return r1, r2


def ffn_fwd(u0, convw, convb, wdown, gpost, x2, exch=None):
    T = x2.shape[0]
    tm, tn = _tile(T, 1024), FF_TILE
    nj = D_FF // tn

    def body(ug_ref, uu_ref, hg_ref, hu_ref, wg_ref, wu_ref, bg_ref, bu_ref, wd_ref, gp_ref, x_ref,
             x3_ref, y_ref, aT_ref, acc):
        i, j = pl.program_id(0), pl.program_id(1)
        first = i == 0

        def conv(u_ref, h_ref, w_ref, b_ref):
            u = u_ref[...]
            r1, r2 = _conv_taps(u, h_ref, first)
            return b_ref[...] + w_ref[0:1, :] * r2 + w_ref[1:2, :] * r1 + w_ref[2:3, :] * u

        a = _gelu(conv(ug_ref, hg_ref, wg_ref, bg_ref)) * conv(uu_ref, hu_ref, wu_ref, bu_ref)
        aT_ref[...] = a.T.astype(MXU)
        _acc_out(acc, _dot(a, wd_ref[...]), j == 0)

        @pl.when(j == nj - 1)
        def _():
            y = acc[...]
            y_ref[...] = y
            x3_ref[...] = x_ref[...] + _rms_fwd(y, gp_ref[...])

    halo = lambda off: (lambda i, j: (jnp.maximum(i * (tm // 8) - 1, 0), off + j))
    return _call_hosting(
        body, "ffn_fwd", (T // tm, nj), [u0, u0, u0, u0, convw, convw, convb, convb, wdown, gpost, x2],
        [pl.BlockSpec((tm, tn), lambda i, j: (i, j)), pl.BlockSpec((tm, tn), lambda i, j: (i, nj + j)),
         pl.BlockSpec((8, tn), halo(0)), pl.BlockSpec((8, tn), halo(nj)),
         pl.BlockSpec((3, tn), lambda i, j: (0, j)), pl.BlockSpec((3, tn), lambda i, j: (0, nj + j)),
         pl.BlockSpec((1, tn), lambda i, j: (0, j)), pl.BlockSpec((1, tn), lambda i, j: (0, nj + j)),
         pl.BlockSpec((tn, 1024), lambda i, j: (j, 0)),
         pl.BlockSpec((1, 1024), lambda i, j: (0, 0)),
         pl.BlockSpec((tm, 1024), lambda i, j: (i, 0))],
        [pl.BlockSpec((tm, 1024), lambda i, j: (i, 0)), pl.BlockSpec((tm, 1024), lambda i, j: (i, 0)),
         pl.BlockSpec((tn, tm), lambda i, j: (j, i))],
        [jax.ShapeDtypeStruct((T, 1024), F32), jax.ShapeDtypeStruct((T, 1024), F32),
         jax.ShapeDtypeStruct((D_FF, T), MXU)],
        [pltpu.VMEM((tm, 1024), F32)], exch)


def ffn_bwd(dx3, y, u0, convw, convb, wdown, gpost, wupT, x2, gfpre, exch=None):
    T = dx3.shape[0]
    tm, tn = _tile(T, 512), FF_TILE
    nj = D_FF // tn
    ni = T // tm

    def body(dx_ref, y_ref, ug_ref, uu_ref, hg_ref, hu_ref, wg_ref, wu_ref, bg_ref, bu_ref, wd_ref, gp_ref,
             wtg_ref, wtu_ref, x2_ref, gf_ref,
             dy_ref, dug_ref, duu_ref, dcg_ref, dcu_ref, dgp_ref, dx2_ref, dgf_ref, dy_sc, dh_sc, cg, cu, ag, au):
        s, j = pl.program_id(0), pl.program_id(1)
        i = ni - 1 - s
        first_tok = i == 0
        sub = lax.broadcasted_iota(jnp.int32, (8, 1), 0)

        @pl.when(j == 0)
        def _():
            dy, dgp = _rms_bwd(dx_ref[...], y_ref[...], gp_ref[...])
            dy_sc[...] = dy.astype(MXU)
            dy_ref[...] = dy.astype(MXU)
            _acc_out(dgp_ref, dgp, s == 0)
            dh_sc[...] = jnp.zeros_like(dh_sc)

        @pl.when(s == 0)
        def _():
            cg[j] = jnp.zeros((8, tn), F32)
            cu[j] = jnp.zeros((8, tn), F32)
            ag[j] = jnp.zeros((8, tn), F32)
            au[j] = jnp.zeros((8, tn), F32)

        da = _dot_nt(dy_sc[...], wd_ref[...])

        def conv(u_ref, h_ref, w_ref, b_ref):
            u = u_ref[...]
            r1, r2 = _conv_taps(u, h_ref, first_tok)
            return b_ref[...] + w_ref[0:1, :] * r2 + w_ref[1:2, :] * r1 + w_ref[2:3, :] * u, u, r1, r2

        gate, ugv, g1, g2 = conv(ug_ref, hg_ref, wg_ref, bg_ref)
        up, uuv, u1, u2 = conv(uu_ref, hu_ref, wu_ref, bu_ref)
        gl, dgl = _gelu_with_grad(gate)
        dup = da * gl
        dgate = da * up * dgl

        def conv_bwd(du, u, r1, r2, w_ref, c_ref, a_ref, duT_ref, wt_ref):
            nxt = c_ref[j]
            n0, n1 = nxt[0:1, :], nxt[1:2, :]
            f1, f2 = pltpu.roll(du, tm - 1, 0), pltpu.roll(du, tm - 2, 0)
            f1 = jnp.concatenate([f1[:tm - 8], jnp.where(sub == 7, n0, f1[tm - 8:])], axis=0)
            f2 = jnp.concatenate([f2[:tm - 8], jnp.where(sub == 7, n1, jnp.where(sub == 6, n0, f2[tm - 8:]))], axis=0)
            du0 = w_ref[2:3, :] * du + w_ref[1:2, :] * f1 + w_ref[0:1, :] * f2
            duT_ref[...] = du0.T.astype(MXU)
            dh_sc[...] += _dot(du0, wt_ref[...])
            c_ref[j] = du[0:8, :]
            red = lambda v: jnp.sum(v, axis=0, keepdims=True)
            part = jnp.where(sub == 0, red(du * r2), jnp.where(sub == 1, red(du * r1), jnp.where(
                sub == 2, red(du * u), jnp.where(sub == 3, red(du), 0.0))))
            a_ref[j] = a_ref[j] + part
            return a_ref[j]

        dcg_ref[0] = conv_bwd(dgate, ugv, g1, g2, wg_ref, cg, ag, dug_ref, wtg_ref)
        dcu_ref[0] = conv_bwd(dup, uuv, u1, u2, wu_ref, cu, au, duu_ref, wtu_ref)

        @pl.when(j == nj - 1)
        def _():
            dx, dgf = _rms_bwd(dh_sc[...], x2_ref[...], gf_ref[...])
            dx2_ref[...] = dx_ref[...] + dx
            _acc_out(dgf_ref, dgf, s == 0)

    rev = lambda s: ni - 1 - s
    halo = lambda off: (lambda s, j: (jnp.maximum(rev(s) * (tm // 8) - 1, 0), off + j))
    tok = pl.BlockSpec((tm, 1024), lambda s, j: (rev(s), 0))
    vec = pl.BlockSpec((1, 1024), lambda s, j: (0, 0))
    return _call_hosting(
        body, "ffn_bwd", (ni, nj),
        [dx3, y, u0, u0, u0, u0, convw, convw, convb, convb, wdown, gpost, wupT, wupT, x2, gfpre],
        [tok, tok,
         pl.BlockSpec((tm, tn), lambda s, j: (rev(s), j)), pl.BlockSpec((tm, tn), lambda s, j: (rev(s), nj + j)),
         pl.BlockSpec((8, tn), halo(0)), pl.BlockSpec((8, tn), halo(nj)),
         pl.BlockSpec((3, tn), lambda s, j: (0, j)), pl.BlockSpec((3, tn), lambda s, j: (0, nj + j)),
         pl.BlockSpec((1, tn), lambda s, j: (0, j)), pl.BlockSpec((1, tn), lambda s, j: (0, nj + j)),
         pl.BlockSpec((tn, 1024), lambda s, j: (j, 0)), vec,
         pl.BlockSpec((tn, 1024), lambda s, j: (j, 0)), pl.BlockSpec((tn, 1024), lambda s, j: (nj + j, 0)),
         tok, vec],
        [tok,
         pl.BlockSpec((tn, tm), lambda s, j: (j, rev(s))), pl.BlockSpec((tn, tm), lambda s, j: (j, rev(s))),
         pl.BlockSpec((1, 8, tn), lambda s, j: (s, 0, j)), pl.BlockSpec((1, 8, tn), lambda s, j: (s, 0, j)),
         vec, tok, vec],
        [jax.ShapeDtypeStruct((T, 1024), MXU), jax.ShapeDtypeStruct((D_FF, T), MXU),
         jax.ShapeDtypeStruct((D_FF, T), MXU),
         jax.ShapeDtypeStruct((ni, 8, D_FF), F32), jax.ShapeDtypeStruct((ni, 8, D_FF), F32),
         jax.ShapeDtypeStruct((1, 1024), F32), jax.ShapeDtypeStruct((T, 1024), F32),
         jax.ShapeDtypeStruct((1, 1024), F32)],
        [pltpu.VMEM((tm, 1024), MXU), pltpu.VMEM((tm, 1024), F32)] + [pltpu.VMEM((nj, 8, tn), F32)] * 4, exch)


ELEMS_PER_BLOCK = 512 * 1024


def _row_block(R, C):
    if R * C <= ELEMS_PER_BLOCK or R % 8:
        return R
    best = 8
    for t in range(8, R + 1, 8):
        if R % t == 0 and t * C <= ELEMS_PER_BLOCK:
            best = t
    return best


def adamw(w, g, m, v, name):
    L, R, C = w.shape
    partials = isinstance(g, (list, tuple))
    tr = _row_block(R, 2 * C)
    c1 = 1.0 - ADAM_B1 ** ADAM_STEP
    c2 = 1.0 - ADAM_B2 ** ADAM_STEP

    def body(w_ref, *rest):
        g_refs, (m_ref, v_ref, g_out, d_ref, nm_ref, nv_ref) = rest[:-6], rest[-6:]

        def step(gv):
            g_out[0] = gv
            nm = ADAM_B1 * m_ref[0] + (1.0 - ADAM_B1) * gv
            nv = ADAM_B2 * v_ref[0] + (1.0 - ADAM_B2) * (gv * gv)
            nm_ref[0] = nm
            nv_ref[0] = nv
            d_ref[0] = -ADAM_LR * ((nm / c1) / (jnp.sqrt(nv / c2) + ADAM_EPS) + ADAM_WD * w_ref[0])

        if not partials:
            step(g_refs[0][0])
            return
        for k in range(L):
            @pl.when(pl.program_id(0) == k)
            def _(k=k):
                gv = g_refs[k][0].astype(F32)
                for d in range(1, N_DEV):
                    gv = gv + g_refs[k][d].astype(F32)
                step(gv)

    spec = pl.BlockSpec((1, tr, C), lambda l, i: (l, i, 0))
    if partials:
        gspecs = [pl.BlockSpec((N_DEV, tr, C), lambda l, i, k=k: (0, jnp.where(l == k, i, 0), 0)) for k in range(L)]
        gs = list(g)
    else:
        gspecs, gs = [spec], [g]
    return pl.pallas_call(
        body, name=name, grid=(L, R // tr), in_specs=[spec] + gspecs + [spec, spec], out_specs=[spec] * 4,
        out_shape=[jax.ShapeDtypeStruct((L, R, C), F32)] * 4,
        compiler_params=_cparams(("arbitrary", "arbitrary")),
    )(w, *gs, m, v)


def sum_devices(buf, name):
    _, R, C = buf.shape
    tr = _row_block(R, C * 4)

    def body(b_ref, o_ref):
        acc = b_ref[0].astype(F32)
        for d in range(1, N_DEV):
            acc = acc + b_ref[d].astype(F32)
        o_ref[...] = acc

    return pl.pallas_call(
        body, name=name, grid=(R // tr,),
        in_specs=[pl.BlockSpec((N_DEV, tr, C), lambda i: (0, i, 0))],
        out_specs=pl.BlockSpec((tr, C), lambda i: (i, 0)),
        out_shape=jax.ShapeDtypeStruct((R, C), F32),
        compiler_params=_cparams(("parallel",)),
    )(buf)


def _exchange_copies(src_refs, out_refs, send_sems, recv_sems, gather):
    x, y, c = lax.axis_index("x"), lax.axis_index("y"), lax.axis_index("c")
    me = 4 * x + 2 * y + c
    flip = lambda a, bit: 1 - a if bit else a
    part = lambda ref, d: ref if gather else ref.at[d]
    copies = []
    for k in range(1, N_DEV):
        px, py, pc = flip(x, (k >> 2) & 1), flip(y, (k >> 1) & 1), flip(c, k & 1)
        peer = 4 * px + 2 * py + pc
        for t in range(len(src_refs)):
            sem = t * (N_DEV - 1) + k - 1
            mk = lambda s, d: pltpu.make_async_remote_copy(
                src_ref=s, dst_ref=d, send_sem=send_sems.at[sem], recv_sem=recv_sems.at[sem],
                device_id=(px, py, pc), device_id_type=pl.DeviceIdType.MESH)
            copies.append((mk(part(src_refs[t], peer), out_refs[t].at[me]),
                           mk(part(src_refs[t], me), out_refs[t].at[peer])))
    return me, copies


def exchange(srcs, name, gather):
    n = len(srcs)
    shapes = [(N_DEV,) + s.shape if gather else s.shape for s in srcs]

    def body(*refs):
        src_refs, out_refs = refs[:n], refs[n:2 * n]
        send_sems, recv_sems, local_sems = refs[2 * n:]
        me, copies = _exchange_copies(src_refs, out_refs, send_sems, recv_sems, gather)
        for outgoing, _ in copies:
            outgoing.start()
        mine = [pltpu.make_async_copy(src_refs[t] if gather else src_refs[t].at[me], out_refs[t].at[me],
                                      local_sems.at[t]) for t in range(n)]
        for cp in mine:
            cp.start()
        for _, incoming in copies:
            incoming.wait_recv()
        for outgoing, _ in copies:
            outgoing.wait_send()
        for cp in mine:
            cp.wait()

    return pl.pallas_call(
        body, name=name,
        in_specs=[pl.BlockSpec(memory_space=pl.ANY)] * n, out_specs=[pl.BlockSpec(memory_space=pl.ANY)] * n,
        out_shape=[jax.ShapeDtypeStruct(shp, s.dtype) for shp, s in zip(shapes, srcs)],
        scratch_shapes=[pltpu.SemaphoreType.DMA((n * (N_DEV - 1),)), pltpu.SemaphoreType.DMA((n * (N_DEV - 1),)),
                        pltpu.SemaphoreType.DMA((n,))],
    )(*srcs)


def hosted_exchange(body, n_in, n_out, n_scratch, grid, srcs, gather):
    n = len(srcs)
    shapes = [(N_DEV,) + s.shape if gather else s.shape for s in srcs]

    def wrapped(*refs):
        ins, xin = refs[:n_in], refs[n_in:n_in + n]
        outs = refs[n_in + n:n_in + n + n_out]
        xout = refs[n_in + n + n_out:n_in + 2 * n + n_out]
        rest = refs[n_in + 2 * n + n_out:]
        scratch, (send_sems, recv_sems, local_sems) = rest[:n_scratch], rest[n_scratch:]
        ids = [pl.program_id(a) for a in range(len(grid))]
        first = functools.reduce(jnp.logical_and, [i == 0 for i in ids])
        last = functools.reduce(jnp.logical_and, [i == g - 1 for i, g in zip(ids, grid)])
        me, copies = _exchange_copies(xin, xout, send_sems, recv_sems, gather)
        mine = [pltpu.make_async_copy(xin[t] if gather else xin[t].at[me], xout[t].at[me], local_sems.at[t])
                for t in range(n)]

        @pl.when(first)
        def _():
            for outgoing, _ in copies:
                outgoing.start()
            for cp in mine:
                cp.start()

        body(*ins, *outs, *scratch)

        @pl.when(last)
        def _():
            for _, incoming in copies:
                incoming.wait_recv()
            for outgoing, _ in copies:
                outgoing.wait_send()
            for cp in mine:
                cp.wait()

    any_spec = pl.BlockSpec(memory_space=pl.ANY)
    return wrapped, (list(srcs), [any_spec] * n, [any_spec] * n,
                     [jax.ShapeDtypeStruct(shp, s.dtype) for shp, s in zip(shapes, srcs)],
                     [pltpu.SemaphoreType.DMA((n * (N_DEV - 1),)), pltpu.SemaphoreType.DMA((n * (N_DEV - 1),)),
                      pltpu.SemaphoreType.DMA((n,))])


def _pack(parts, cols, row_align, dtype):
    flat = jnp.concatenate([p.astype(dtype) for p in parts], axis=-1)
    n = flat.shape[-1]
    block = cols * row_align
    total = -(-n // block) * block
    flat = jnp.pad(flat, [(0, 0)] * (flat.ndim - 1) + [(0, total - n)])
    return flat.reshape(flat.shape[:-1] + (total // cols, cols))


def _unpack(buf, shapes):
    lead = buf.shape[:-2]
    flat = buf.reshape(lead + (-1,))
    out, off = [], 0
    for s in shapes:
        n = int(np.prod(s))
        out.append(flat[..., off:off + n].reshape(lead + tuple(s)))
        off += n
    return out


SHARD_SHAPES = [(128, IN_COLS), (256, 48), (128, 64), (128, 1024), (1024, 704), (352, 1024)]
SHARDED = ["w_in", "w_uq", "w_ukv", "w_out", "w_up", "w_down"]
ATTN_SENT = ["w_in_p", "w_uq", "w_ukv", "w_out"]
UP_HALF = 352
FFN_SIDE = ["w_upT", "conv_w", "w_down"]


def _full_from_shards(name, s):
    if name in ("w_in", "w_in_p", "w_out", "w_down", "w_upT"):
        return s.reshape((-1, s.shape[-1]))
    return s.transpose(1, 0, 2).reshape((s.shape[1], -1))


def _shards_from_full(name, f):
    if name in ("w_in", "w_in_p", "w_out", "w_down", "w_upT"):
        return f.reshape((N_DEV, -1, f.shape[-1]))
    return f.reshape((f.shape[0], N_DEV, -1)).transpose(1, 0, 2)


def _perm_w_in(w):
    z = lambda n: jnp.zeros(w.shape[:-1] + (n,), w.dtype)
    return jnp.concatenate([w[..., :1536], w[..., 1540:1924], w[..., 1536:1540], z(60), w[..., 1924:1956], z(32)],
                           axis=-1)


def _unperm_w_in(d):
    return jnp.concatenate([d[..., :1536], d[..., 1920:1924], d[..., 1536:1920], d[..., 1984:2016]], axis=-1)


def _perm_w_uq(w):
    return jnp.pad(w.reshape(256, 4, MLA_QK_DIM), ((0, 0), (0, 0), (0, 128 - MLA_QK_DIM))).reshape(256, 512)


def _unperm_w_uq(d):
    return d.reshape(256, 4, 128)[:, :, :MLA_QK_DIM].reshape(256, 4 * MLA_QK_DIM)


def _perm_w_ukv(w):
    w4 = w.reshape(128, 4, 128)
    k = jnp.pad(w4[:, :, :64], ((0, 0), (0, 0), (0, 64))).reshape(128, 512)
    return jnp.concatenate([k, w4[:, :, 64:].reshape(128, 256)], axis=1)


def _unperm_w_ukv(d):
    dk = d[:, :512].reshape(128, 4, 128)[:, :, :64]
    dv = d[:, 512:].reshape(128, 4, 64)
    return jnp.concatenate([dk, dv], axis=-1).reshape(128, 512)


def _row(v, width=None):
    v = v.reshape(1, -1).astype(F32)
    if width is not None and v.shape[1] < width:
        v = jnp.pad(v, ((0, 0), (0, width - v.shape[1])))
    return v


def _layer_fwd(x, P, shared, send=None, ffn_from=None):
    cosr, sinr, bias = shared
    ex = lambda host: (send[host], True) if send is not None and send.get(host) else None
    proj, hT, projb = norm_matmul(x, P["g_pre"], P["w_in_p"], "in_proj", lo_tiles=C_CQ // 512)
    qm, km, vm, cqT, ckvT = mla_prep(proj, P["gq"], P["gkv"], P["w_uq_p"], P["w_ukv_p"], cosr, sinr)
    fcol, frow, frep = fox_gate(proj, P["fbias"])
    (oa, lse_a), got_swa = swa_fwd(proj, bias, P["sinks"], exch=ex("swa"))
    (ob, lrb), got_fox = flash_fwd(projb, projb, projb, frep, frow, qblk=C_QF // 128, kblk=C_KF // 128,
                                   vblk=C_VF // 128, nq=1, scale=HEAD_DIM ** -0.5, name="fox_fwd", exch=ex("fox"))
    (oc, lrc), got_mla = flash_fwd(qm, km, vm, None, None, qblk=0, kblk=0, vblk=0, nq=2,
                                   scale=MLA_QK_DIM ** -0.5, name="mla_fwd", exch=ex("mla"))
    x2, y1, mT = attn_out(oa, ob, oc, P["gn"], P["w_out"], P["g_apost"], x)
    if ffn_from is not None:
        P = dict(P, **ffn_from(got_swa, got_fox, got_mla))
    u0, h2 = norm_matmul(x2, P["g_fpre"], P["w_upT"], "up_proj", tn_pref=1536, w_transposed=True,
                         h_transposed=False)
    (x3, y2, aT), got_ffn = ffn_fwd(u0, P["conv_w"], P["conv_b"], P["w_down"], P["g_fpost"], x2, exch=ex("ffn"))
    S = dict(x=x, proj=proj, projb=projb, hT=hT, qm=qm, km=km, vm=vm, cqT=cqT, ckvT=ckvT, fcol=fcol, frow=frow,
             oa=oa, lse_a=lse_a, ob=ob, lrb=lrb, oc=oc, lrc=lrc,
             x2=x2, y1=y1, mT=mT, u0=u0, h2=h2, y2=y2, aT=aT)
    return x3, S, P, got_ffn


def _layer_bwd(dx3, P, S, shared, send_attn=None):
    cosr, sinr, bias = shared
    proj = S["proj"]
    G = {}
    got = {}
    ex = lambda arrays: (arrays, False) if send_attn is not None and arrays else None
    (dy2, dugT, duuT, dcg, dcu, G["ffn_post_norm"], dx2, G["ffn_pre_norm"]), got["ffn"] = ffn_bwd(
        dx3, S["y2"], S["u0"], P["conv_w"], P["conv_b"], P["w_down"], P["g_fpost"], P["w_upT"], S["x2"], P["g_fpre"],
        exch=ex(send_attn))
    dconv = jnp.concatenate([dcg[-1], dcu[-1]], axis=1)
    G["conv_w"], G["conv_b"] = dconv[0:3], dconv[3]
    G["w_down"] = matmul_nn(S["aT"], dy2, "dw_down", MXU)
    G["w_upT"] = jnp.concatenate([matmul_nn(dugT, S["h2"], "dw_up_gate", MXU),
                                  matmul_nn(duuT, S["h2"], "dw_up_up", MXU)], axis=0)
    G["w_up"] = G["w_upT"].T
    dy1, doa, dob, doc, G["group_norm"], G["attn_post_norm"] = attn_out_bwd(
        dx2, S["y1"], S["oa"], S["ob"], S["oc"], P["gn"], P["w_out"], P["g_apost"])
    G["w_out"] = matmul_nn(S["mT"], dy1, "dw_out", MXU)
    up_slices = _shards_from_full("w_upT", G["w_upT"])
    (dqa, dka, dva, dbias, dsk), got["swa"] = swa_bwd(proj, bias, P["sinks"], doa, S["oa"], S["lse_a"],
                                                      exch=ex([_shards_from_full("w_down", G["w_down"])]))
    G["swa_sinks"] = dsk[:, 0]
    pb = S["projb"]
    (dqf, dkf, dvf, dFk, dFq), got["fox"] = flash_bwd(
        pb, pb, pb, dob, S["ob"], S["lrb"], S["fcol"], S["frow"], name="fox_bwd", qblk=C_QF // 128,
        kblk=C_KF // 128, vblk=C_VF // 128, nq=1, scale=HEAD_DIM ** -0.5, exch=ex([up_slices[:, :UP_HALF]]))
    dmisc_f, dfb = fox_gate_bwd(dFq, dFk, proj, P["fbias"])
    G["forget_bias"] = dfb[0, 0:4]
    (dqm_, dkm_, dvm_), got["mla"] = flash_bwd(
        S["qm"], S["km"], S["vm"], doc, S["oc"], S["lrc"], None, None, name="mla_bwd",
        qblk=0, kblk=0, vblk=0, nq=2, scale=MLA_QK_DIM ** -0.5, exch=ex([up_slices[:, UP_HALF:]]))
    dqm, dkv, dcq, dckv, dmisc_r, G["q_latent_norm"], G["kv_latent_norm"] = mla_prep_bwd(
        dqm_, dkm_, dvm_, proj, P["gq"], P["gkv"], P["w_uq_p"], P["w_ukv_p"], cosr, sinr)
    G["w_uq"] = _unperm_w_uq(matmul_nn(S["cqT"], dqm, "dw_uq", MXU))
    G["w_ukv"] = _unperm_w_ukv(matmul_nn(S["ckvT"], dkv, "dw_ukv", MXU))
    dproj = jnp.concatenate([dqa, dka, dva, dqf, dkf, dvf, dcq, dckv, dmisc_f + dmisc_r], axis=1).astype(MXU)
    G["w_in_p"] = matmul_nn(S["hT"], dproj, "dw_in", MXU)
    G["w_in"] = _unperm_w_in(G["w_in_p"])
    dx, G["attn_pre_norm"] = matmul_nt_normbwd(dproj, P["w_in_p"], S["x"], P["g_pre"], dx2, "in_bwd")
    return dx, G, dbias, got


def _layer_params(l, full, small):
    return dict(
        g_pre=_row(small["attn_pre_norm"][l]),
        w_in_p=full["w_in_p"] if "w_in_p" in full else _perm_w_in(full["w_in"]),
        gq=_row(small["q_latent_norm"][l]), gkv=_row(small["kv_latent_norm"][l]),
        w_uq_p=_perm_w_uq(full["w_uq"]), w_ukv_p=_perm_w_ukv(full["w_ukv"]),
        fbias=_row(small["forget_bias"][l], 128), sinks=small["swa_sinks"][l].astype(F32),
        gn=_row(small["group_norm"][l]), w_out=full["w_out"], g_apost=_row(small["attn_post_norm"][l]),
        g_fpre=_row(small["ffn_pre_norm"][l]), conv_b=_row(small["conv_b"][l]),
        g_fpost=_row(small["ffn_post_norm"][l]),
        **{n: full[n] for n in FFN_SIDE if n in full},
        **({"w_upT": full["w_up"].T} if "w_up" in full else {}))


def _rel_bias_grad(dbias, bucket):
    flat = dbias.reshape(SWA_Q_HEADS, -1)
    hi = flat.astype(MXU)
    lo = (flat - hi.astype(F32)).astype(MXU)
    onehot = (bucket[:, None] == jnp.arange(128, dtype=jnp.int32)[None, :]).astype(MXU)
    r = matmul_nn(jnp.concatenate([hi, lo], axis=0), onehot, "rel_bias_grad")
    return (r[0:8] + r[8:16])[:, :REL_BUCKETS].T


def local_step(x, tgt, fulls, small, comm=None):
    T = x.shape[0]
    cosr, sinr = rope_tables(T)
    bias, bucket = swa_bias_table(small["rel_bias"])
    shared = (cosr, sinr, bias)
    Ps, Ss = [], []
    h, full = x, fulls[0]
    for l in range(DEPTH):
        P = _layer_params(l, full, small)
        if comm:
            h, S, P, got = _layer_fwd(h, P, shared, comm["weight_parts"](l), comm["ffn_from"])
            full = comm["attn_from"](got) if l + 1 < DEPTH else None
        else:
            h, S, P, _ = _layer_fwd(h, P, shared)
            full = fulls[l + 1] if l + 1 < DEPTH else None
        Ps.append(P)
        Ss.append(S)
    dh, sq = loss_kernel(h, tgt)
    grads = [None] * DEPTH
    dbias_sum = None
    pending = [] if comm else None
    for l in reversed(range(DEPTH)):
        dh, grads[l], dbias, got = _layer_bwd(dh, Ps[l], Ss[l], shared, pending)
        dbias_sum = dbias if dbias_sum is None else dbias_sum + dbias
        if comm:
            comm["landed"](l, ["w_down"], got["swa"])
            comm["landed"](l, ["w_upT"], [jnp.concatenate([got["fox"][0], got["mla"][0]], axis=1)])
            if pending:
                comm["landed"](l + 1, ATTN_SENT, got["ffn"])
            pending = [_shards_from_full(n, grads[l][n]) for n in ATTN_SENT]
    return sq, dh, grads, _rel_bias_grad(dbias_sum, bucket), pending


WEIGHTS = ['attn_pre_norm', 'w_in', 'forget_bias', 'swa_sinks', 'rel_bias', 'q_latent_norm', 'w_uq',
           'kv_latent_norm', 'w_ukv', 'group_norm', 'w_out', 'attn_post_norm', 'ffn_pre_norm', 'w_up', 'conv_w',
           'conv_b', 'w_down', 'ffn_post_norm']
SMALL_PER_LAYER = ['attn_pre_norm', 'forget_bias', 'swa_sinks', 'q_latent_norm', 'kv_latent_norm', 'group_norm',
                   'attn_post_norm', 'ffn_pre_norm', 'conv_b', 'ffn_post_norm', 'conv_w']


def kernel(x, attn_pre_norm, w_in, forget_bias, swa_sinks, rel_bias, q_latent_norm, w_uq, kv_latent_norm, w_ukv, group_norm, w_out, attn_post_norm, ffn_pre_norm, w_up, conv_w, conv_b, w_down, ffn_post_norm, loss_target, m_attn_pre_norm, m_w_in, m_forget_bias, m_swa_sinks, m_rel_bias, m_q_latent_norm, m_w_uq, m_kv_latent_norm, m_w_ukv, m_group_norm, m_w_out, m_attn_post_norm, m_ffn_pre_norm, m_w_up, m_conv_w, m_conv_b, m_w_down, m_ffn_post_norm, v_attn_pre_norm, v_w_in, v_forget_bias, v_swa_sinks, v_rel_bias, v_q_latent_norm, v_w_uq, v_kv_latent_norm, v_w_ukv, v_group_norm, v_w_out, v_attn_post_norm, v_ffn_pre_norm, v_w_up, v_conv_w, v_conv_b, v_w_down, v_ffn_post_norm):
    W = dict(attn_pre_norm=attn_pre_norm, w_in=w_in, forget_bias=forget_bias, swa_sinks=swa_sinks, rel_bias=rel_bias,
             q_latent_norm=q_latent_norm, w_uq=w_uq, kv_latent_norm=kv_latent_norm, w_ukv=w_ukv,
             group_norm=group_norm, w_out=w_out, attn_post_norm=attn_post_norm, ffn_pre_norm=ffn_pre_norm,
             w_up=w_up, conv_w=conv_w, conv_b=conv_b, w_down=w_down, ffn_post_norm=ffn_post_norm)
    M = dict(attn_pre_norm=m_attn_pre_norm, w_in=m_w_in, forget_bias=m_forget_bias, swa_sinks=m_swa_sinks,
             rel_bias=m_rel_bias, q_latent_norm=m_q_latent_norm, w_uq=m_w_uq, kv_latent_norm=m_kv_latent_norm,
             w_ukv=m_w_ukv, group_norm=m_group_norm, w_out=m_w_out, attn_post_norm=m_attn_post_norm,
             ffn_pre_norm=m_ffn_pre_norm, w_up=m_w_up, conv_w=m_conv_w, conv_b=m_conv_b, w_down=m_w_down,
             ffn_post_norm=m_ffn_post_norm)
    V = dict(attn_pre_norm=v_attn_pre_norm, w_in=v_w_in, forget_bias=v_forget_bias, swa_sinks=v_swa_sinks,
             rel_bias=v_rel_bias, q_latent_norm=v_q_latent_norm, w_uq=v_w_uq, kv_latent_norm=v_kv_latent_norm,
             w_ukv=v_w_ukv, group_norm=v_group_norm, w_out=v_w_out, attn_post_norm=v_attn_post_norm,
             ffn_pre_norm=v_ffn_pre_norm, w_up=v_w_up, conv_w=v_conv_w, conv_b=v_conv_b, w_down=v_w_down,
             ffn_post_norm=v_ffn_post_norm)
    me = 4 * lax.axis_index("x") + 2 * lax.axis_index("y") + lax.axis_index("c")

    def attn_shards(l):
        return [_perm_w_in(w_in[l].astype(MXU))] + [W[n][l].astype(MXU) for n in ATTN_SENT[1:]]

    def weight_parts(l):
        up = jnp.swapaxes(W["w_up"][l], 0, 1).astype(MXU)
        return dict(swa=[W["w_down"][l].astype(MXU)], fox=[up[:UP_HALF]], mla=[up[UP_HALF:], conv_w[l]],
                    ffn=attn_shards(l + 1) if l + 1 < DEPTH else [])

    def ffn_from(got_swa, got_fox, got_mla):
        return dict(w_down=_full_from_shards("w_down", got_swa[0]),
                    w_upT=_full_from_shards("w_upT", jnp.concatenate([got_fox[0], got_mla[0]], axis=1)),
                    conv_w=got_mla[1].transpose(1, 0, 2).reshape(3, 2 * D_FF))

    def attn_from(got_ffn):
        return {n: _full_from_shards(n, s) for n, s in zip(ATTN_SENT, got_ffn)}

    landed = [{} for _ in range(DEPTH)]

    def on_landed(l, names, arrays):
        landed[l].update(zip(names, arrays))

    comm = dict(weight_parts=weight_parts, ffn_from=ffn_from, attn_from=attn_from, landed=on_landed)
    full0 = dict(zip(ATTN_SENT, map(_full_from_shards, ATTN_SENT, exchange(attn_shards(0), "gather_weights", True))))
    sq, dx, grads, drel, last = local_step(x[0], loss_target[0], [full0], W, comm)
    on_landed(0, ATTN_SENT, exchange(last, "scatter_grads", False))
    for l in range(DEPTH):
        landed[l]["w_in"] = _unperm_w_in(landed[l]["w_in_p"])
    G = {}

    parts, shapes = [], []
    for l in range(DEPTH):
        for n in SMALL_PER_LAYER:
            parts.append(grads[l][n].astype(F32).reshape(-1))
            shapes.append(grads[l][n].shape)
    parts += [drel.reshape(-1), jnp.sum(sq).reshape(1) * (0.5 / D_MODEL)]
    shapes += [drel.shape, (1,)]
    red = _unpack(sum_devices(exchange([_pack(parts, 128, 8, F32)], "gather_small", True)[0], "sum_small"), shapes)
    k = 0
    per = {n: [] for n in SMALL_PER_LAYER}
    for l in range(DEPTH):
        for n in SMALL_PER_LAYER:
            per[n].append(red[k])
            k += 1
    for n in SMALL_PER_LAYER:
        G[n] = jnp.stack(per[n]).reshape((DEPTH, 3, 2 * D_FF) if n == "conv_w" else W[n].shape)
    G["rel_bias"] = red[k]
    loss = red[k + 1][0]
    G["conv_w"] = lax.dynamic_slice_in_dim(G["conv_w"], me * 704, 704, axis=2)

    delta, new_m, new_v = {}, {}, {}
    for n in WEIGHTS:
        shp = W[n].shape
        if n == "w_up":
            v3 = lambda a: jnp.swapaxes(a, 1, 2)
            back = v3
            g = [landed[l]["w_upT"] for l in range(DEPTH)]
        else:
            v3 = lambda a: a.reshape(shp if len(shp) == 3 else (1,) + shp)
            back = lambda a: a.reshape(shp)
            g = [landed[l][n] for l in range(DEPTH)] if n in SHARDED else v3(G[n])
        g, d, nm, nv = adamw(v3(W[n]), g, v3(M[n]), v3(V[n]), "adamw_" + n)
        G[n], delta[n], new_m[n], new_v[n] = back(g), back(d), back(nm), back(nv)
    return (loss, dx[None], *[G[n] for n in WEIGHTS], *[delta[n] for n in WEIGHTS],
            *[new_m[n] for n in WEIGHTS], *[new_v[n] for n in WEIGHTS])
```

```python
import functools
import math

import numpy as np
import jax
import jax.numpy as jnp
from jax import lax
from jax.experimental import pallas as pl
from jax.experimental.pallas import tpu as pltpu

F32 = jnp.float32
MXU = jnp.bfloat16

N_DEV = 8
DEPTH = 4
D_MODEL = 1024
HEAD_DIM = 64
WINDOW = 128
SWA_Q_HEADS = 8
REL_BUCKETS = 32
REL_MAX_DIST = 128
MLA_QK_DIM = 96
ROPE_DIM = 32
ROPE_THETA = 10000.0
D_FF = 2816
EPS = 1e-6
NEG = -1e30
IN_COLS = 1956
IN_COLS_P = 2048
C_QA, C_KA, C_VA = 0, 512, 640
C_QF, C_KF, C_VF = 768, 1024, 1280
C_CQ, C_CKV, C_MISC = 1536, 1792, 1920
ROPE_LANE0 = 64
ADAM_LR, ADAM_B1, ADAM_B2, ADAM_EPS, ADAM_WD, ADAM_STEP = 0.001, 0.9, 0.999, 1e-08, 0.01, 10

VMEM_LIMIT = 56 * 1024 * 1024
PACK_COLS = 1024
PACK_ROW_ALIGN = 16


def _cparams(sem=None):
    return pltpu.CompilerParams(dimension_semantics=sem, vmem_limit_bytes=VMEM_LIMIT)


def _tile(n, pref):
    if n <= pref:
        return n
    t = pref - pref % 128
    while t >= 128:
        if n % t == 0:
            return t
        t -= 128
    return n


def _dot(a, b):
    return jnp.dot(a.astype(MXU), b.astype(MXU), preferred_element_type=F32)


def _dot_nt(a, b):
    return lax.dot_general(a.astype(MXU), b.astype(MXU), (((1,), (1,)), ((), ())),
                           preferred_element_type=F32)


def _rms_fwd(x, g):
    return x * lax.rsqrt(jnp.mean(x * x, axis=-1, keepdims=True) + EPS) * g


def _rms_bwd(dy, x, g, n=None):
    r = lax.rsqrt(jnp.mean(x * x, axis=-1, keepdims=True) + EPS)
    xh = x * r
    dg = jnp.sum(dy * xh, axis=0, keepdims=True)
    dxh = dy * g
    dx = r * (dxh - xh * jnp.mean(dxh * xh, axis=-1, keepdims=True))
    return dx, dg


def _acc_out(ref, val, first):
    @pl.when(first)
    def _():
        ref[...] = val

    @pl.when(jnp.logical_not(first))
    def _():
        ref[...] += val


def norm_matmul(x, g, w, name, lo_tiles=0, tn_pref=512, w_transposed=False, h_transposed=True):
    T, K = x.shape
    N = w.shape[0] if w_transposed else w.shape[1]
    tm, tn = _tile(T, 1024), _tile(N, tn_pref)

    def body(x_ref, g_ref, w_ref, o_ref, hT_ref, *rest):
        h_sc = rest[-1]
        j = pl.program_id(1)

        @pl.when(j == 0)
        def _():
            h = _rms_fwd(x_ref[...], g_ref[...])
            h_sc[...] = h.astype(MXU)
            hT_ref[...] = (h.T if h_transposed else h).astype(MXU)

        r = (_dot_nt if w_transposed else _dot)(h_sc[...], w_ref[...])
        o_ref[...] = r
        if lo_tiles:
            @pl.when(j < lo_tiles)
            def _():
                rest[0][...] = r.astype(MXU)

    h_spec = pl.BlockSpec((K, tm), lambda i, j: (0, i)) if h_transposed else pl.BlockSpec((tm, K), lambda i, j: (i, 0))
    out_specs = [pl.BlockSpec((tm, tn), lambda i, j: (i, j)), h_spec]
    out_shape = [jax.ShapeDtypeStruct((T, N), F32), jax.ShapeDtypeStruct((K, T) if h_transposed else (T, K), MXU)]
    if lo_tiles:
        out_specs.append(pl.BlockSpec((tm, tn), lambda i, j: (i, jnp.minimum(j, lo_tiles - 1))))
        out_shape.append(jax.ShapeDtypeStruct((T, lo_tiles * tn), MXU))
    return pl.pallas_call(
        body, name=name, grid=(T // tm, N // tn),
        in_specs=[pl.BlockSpec((tm, K), lambda i, j: (i, 0)),
                  pl.BlockSpec((1, K), lambda i, j: (0, 0)),
                  pl.BlockSpec((tn, K), lambda i, j: (j, 0)) if w_transposed else
                  pl.BlockSpec((K, tn), lambda i, j: (0, j))],
        out_specs=out_specs, out_shape=out_shape,
        scratch_shapes=[pltpu.VMEM((tm, K), MXU)],
        compiler_params=_cparams(("parallel", "arbitrary")),
    )(x, g, w)


def matmul_nn(a, b, name, out_dtype=F32):
    M, K = a.shape
    N = b.shape[1]
    tm, tn, tk = _tile(M, 1408), _tile(N, 1536), _tile(K, 1024)
    nk = K // tk

    def body(a_ref, b_ref, o_ref, acc):
        k = pl.program_id(2)
        part = _dot(a_ref[...], b_ref[...])
        _acc_out(acc, part, k == 0)

        @pl.when(k == nk - 1)
        def _():
            o_ref[...] = acc[...].astype(out_dtype)

    return pl.pallas_call(
        body, name=name, grid=(M // tm, N // tn, nk),
        in_specs=[pl.BlockSpec((tm, tk), lambda i, j, k: (i, k)),
                  pl.BlockSpec((tk, tn), lambda i, j, k: (k, j))],
        out_specs=pl.BlockSpec((tm, tn), lambda i, j, k: (i, j)),
        out_shape=jax.ShapeDtypeStruct((M, N), out_dtype),
        scratch_shapes=[pltpu.VMEM((tm, tn), F32)],
        compiler_params=_cparams(("parallel", "parallel", "arbitrary")),
    )(a, b)


def matmul_nt_normbwd(dy, w, x, g, dres, name):
    T, N = dy.shape
    K = w.shape[0]
    tm, tn = _tile(T, 1024), _tile(N, 1536)
    nj = N // tn

    def body(dy_ref, w_ref, x_ref, g_ref, dres_ref, dx_ref, dg_ref, acc):
        i, j = pl.program_id(0), pl.program_id(1)
        _acc_out(acc, _dot_nt(dy_ref[...], w_ref[...]), j == 0)

        @pl.when(j == nj - 1)
        def _():
            dx, dg = _rms_bwd(acc[...], x_ref[...], g_ref[...])
            dx_ref[...] = dres_ref[...] + dx
            _acc_out(dg_ref, dg, i == 0)

    return pl.pallas_call(
        body, name=name, grid=(T // tm, nj),
        in_specs=[pl.BlockSpec((tm, tn), lambda i, j: (i, j)),
                  pl.BlockSpec((K, tn), lambda i, j: (0, j)),
                  pl.BlockSpec((tm, K), lambda i, j: (i, 0)),
                  pl.BlockSpec((1, K), lambda i, j: (0, 0)),
                  pl.BlockSpec((tm, K), lambda i, j: (i, 0))],
        out_specs=[pl.BlockSpec((tm, K), lambda i, j: (i, 0)),
                   pl.BlockSpec((1, K), lambda i, j: (0, 0))],
        out_shape=[jax.ShapeDtypeStruct((T, K), F32), jax.ShapeDtypeStruct((1, K), F32)],
        scratch_shapes=[pltpu.VMEM((tm, K), F32)],
        compiler_params=_cparams(("arbitrary", "arbitrary")),
    )(dy, w, x, g, dres)


def loss_kernel(y, tgt):
    T, D = y.shape
    tm = _tile(T, 512)

    def body(y_ref, t_ref, dy_ref, acc_ref):
        e = y_ref[...] - t_ref[...]
        dy_ref[...] = e * (1.0 / D)
        _acc_out(acc_ref, jnp.sum(e * e, axis=0, keepdims=True), pl.program_id(0) == 0)

    return pl.pallas_call(
        body, name="loss", grid=(T // tm,),
        in_specs=[pl.BlockSpec((tm, D), lambda i: (i, 0)), pl.BlockSpec((tm, D), lambda i: (i, 0))],
        out_specs=[pl.BlockSpec((tm, D), lambda i: (i, 0)), pl.BlockSpec((1, D), lambda i: (0, 0))],
        out_shape=[jax.ShapeDtypeStruct((T, D), F32), jax.ShapeDtypeStruct((1, D), F32)],
        compiler_params=_cparams(("arbitrary",)),
    )(y, tgt)


def _rope_partner(x):
    lane = lax.broadcasted_iota(jnp.int32, (1, 128), 1)
    return jnp.where(lane < ROPE_LANE0 + ROPE_DIM // 2, pltpu.roll(x, 128 - ROPE_DIM // 2, 1),
                     pltpu.roll(x, ROPE_DIM // 2, 1))


def _rope_apply(x, cos, sin_signed):
    return x * cos + _rope_partner(x) * sin_signed


def _rope_apply_bwd(dy, cos, sin_signed):
    lane = lax.broadcasted_iota(jnp.int32, (1, 128), 1)
    rotary = (lane >= ROPE_LANE0) & (lane < ROPE_LANE0 + ROPE_DIM)
    return dy * cos + jnp.where(rotary, _rope_partner(dy * sin_signed), 0.0)


def rope_tables(T):
    pos = jnp.arange(T, dtype=F32)
    inv_freq = ROPE_THETA ** (-(jnp.arange(ROPE_DIM // 2, dtype=F32) * 2.0 / ROPE_DIM))
    ang = pos[:, None] * inv_freq[None, :]
    cos, sin = jnp.cos(ang), jnp.sin(ang)
    z = jnp.zeros((T, ROPE_LANE0), F32)
    z2 = jnp.zeros((T, 128 - ROPE_LANE0 - ROPE_DIM), F32)
    cosr = jnp.concatenate([z, cos, cos, z2], axis=1)
    sinr = jnp.concatenate([z, -sin, sin, z2], axis=1)
    return cosr, sinr


def mla_prep(proj, gq, gkv, wuq, wukv, cosr, sinr):
    T = proj.shape[0]
    tm = _tile(T, 512)

    def body(cq_ref, ckv_ref, misc_ref, gq_ref, gkv_ref, wuq_ref, wukv_ref, cos_ref, sin_ref,
             q_ref, k_ref, v_ref, cqT_ref, ckvT_ref):
        lane = lax.broadcasted_iota(jnp.int32, (1, 128), 1)
        cosr_, sinr_ = cos_ref[...], sin_ref[...]
        cosq = cosr_ + jnp.where(lane < ROPE_LANE0, 1.0, 0.0)
        cqn = _rms_fwd(cq_ref[...], gq_ref[...])
        cqT_ref[...] = cqn.T.astype(MXU)
        qm = _dot(cqn, wuq_ref[...])
        q_ref[...] = jnp.concatenate(
            [_rope_apply(qm[:, 128 * h:128 * (h + 1)], cosq, sinr_) for h in range(4)], axis=1).astype(MXU)
        ckvn = _rms_fwd(ckv_ref[...], gkv_ref[...])
        ckvT_ref[...] = ckvn.T.astype(MXU)
        kv = _dot(ckvn, wukv_ref[...])
        kr = _rope_apply(misc_ref[...], cosr_, sinr_)
        k_ref[...] = jnp.concatenate(
            [kv[:, 128 * h:128 * (h + 1)] + kr for h in range(4)], axis=1).astype(MXU)
        v_ref[...] = kv[:, 512:768].astype(MXU)

    row = lambda i: (i, 0)
    const = lambda i: (0, 0)
    return pl.pallas_call(
        body, name="mla_prep", grid=(T // tm,),
        in_specs=[pl.BlockSpec((tm, 256), lambda i: (i, C_CQ // 256)),
                  pl.BlockSpec((tm, 128), lambda i: (i, C_CKV // 128)),
                  pl.BlockSpec((tm, 128), lambda i: (i, C_MISC // 128)),
                  pl.BlockSpec((1, 256), const), pl.BlockSpec((1, 128), const),
                  pl.BlockSpec((256, 512), const), pl.BlockSpec((128, 768), const),
                  pl.BlockSpec((tm, 128), row), pl.BlockSpec((tm, 128), row)],
        out_specs=[pl.BlockSpec((tm, 512), row), pl.BlockSpec((tm, 512), row), pl.BlockSpec((tm, 256), row),
                   pl.BlockSpec((256, tm), lambda i: (0, i)), pl.BlockSpec((128, tm), lambda i: (0, i))],
        out_shape=[jax.ShapeDtypeStruct((T, 512), MXU), jax.ShapeDtypeStruct((T, 512), MXU),
                   jax.ShapeDtypeStruct((T, 256), MXU),
                   jax.ShapeDtypeStruct((256, T), MXU), jax.ShapeDtypeStruct((128, T), MXU)],
        compiler_params=_cparams(("parallel",)),
    )(proj, proj, proj, gq, gkv, wuq, wukv, cosr, sinr)


def mla_prep_bwd(dq, dk, dv, proj, gq, gkv, wuq, wukv, cosr, sinr):
    T = proj.shape[0]
    tm = _tile(T, 512)

    def body(dq_ref, dk_ref, dv_ref, cq_ref, ckv_ref, gq_ref, gkv_ref, wuq_ref, wukv_ref, cos_ref, sin_ref,
             dqm_ref, dkv_ref, dcq_ref, dckv_ref, dmisc_ref, dgq_ref, dgkv_ref):
        first = pl.program_id(0) == 0
        lane = lax.broadcasted_iota(jnp.int32, (1, 128), 1)
        cosr_, sinr_ = cos_ref[...], sin_ref[...]
        cosq = cosr_ + jnp.where(lane < ROPE_LANE0, 1.0, 0.0)
        dqv = dq_ref[...]
        dqm = jnp.concatenate(
            [_rope_apply_bwd(dqv[:, 128 * h:128 * (h + 1)], cosq, sinr_) for h in range(4)], axis=1)
        dqm_ref[...] = dqm.astype(MXU)
        dcq, dgq = _rms_bwd(_dot_nt(dqm, wuq_ref[...]), cq_ref[...], gq_ref[...])
        dcq_ref[...] = dcq
        _acc_out(dgq_ref, dgq, first)
        dkv_ = dk_ref[...]
        heads = [dkv_[:, 128 * h:128 * (h + 1)] for h in range(4)]
        dkr = heads[0] + heads[1] + heads[2] + heads[3]
        dmisc_ref[...] = _rope_apply_bwd(dkr, cosr_, sinr_)
        dkvm = jnp.concatenate([jnp.where(lane < ROPE_LANE0, hd, 0.0) for hd in heads] + [dv_ref[...]], axis=1)
        dkv_ref[...] = dkvm.astype(MXU)
        dckv, dgkv = _rms_bwd(_dot_nt(dkvm, wukv_ref[...]), ckv_ref[...], gkv_ref[...])
        dckv_ref[...] = dckv
        _acc_out(dgkv_ref, dgkv, first)

    row = lambda i: (i, 0)
    const = lambda i: (0, 0)
    return pl.pallas_call(
        body, name="mla_prep_bwd", grid=(T // tm,),
        in_specs=[pl.BlockSpec((tm, 512), row), pl.BlockSpec((tm, 512), row), pl.BlockSpec((tm, 256), row),
                  pl.BlockSpec((tm, 256), lambda i: (i, C_CQ // 256)),
                  pl.BlockSpec((tm, 128), lambda i: (i, C_CKV // 128)),
                  pl.BlockSpec((1, 256), const), pl.BlockSpec((1, 128), const),
                  pl.BlockSpec((256, 512), const), pl.BlockSpec((128, 768), const),
                  pl.BlockSpec((tm, 128), row), pl.BlockSpec((tm, 128), row)],
        out_specs=[pl.BlockSpec((tm, 512), row), pl.BlockSpec((tm, 768), row), pl.BlockSpec((tm, 256), row),
                   pl.BlockSpec((tm, 128), row), pl.BlockSpec((tm, 128), row),
                   pl.BlockSpec((1, 256), const), pl.BlockSpec((1, 128), const)],
        out_shape=[jax.ShapeDtypeStruct((T, 512), MXU), jax.ShapeDtypeStruct((T, 768), MXU),
                   jax.ShapeDtypeStruct((T, 256), F32), jax.ShapeDtypeStruct((T, 128), F32),
                   jax.ShapeDtypeStruct((T, 128), F32),
                   jax.ShapeDtypeStruct((1, 256), F32), jax.ShapeDtypeStruct((1, 128), F32)],
        compiler_params=_cparams(("arbitrary",)),
    )(dq, dk, dv, proj, proj, gq, gkv, wuq, wukv, cosr, sinr)


def _split3(x):
    hi = x.astype(MXU)
    r1 = x - hi.astype(F32)
    mid = r1.astype(MXU)
    lo = (r1 - mid.astype(F32)).astype(MXU)
    return hi, mid, lo


def _tri_matmul(tri, x):
    hi, mid, lo = _split3(x)
    d = lambda p: jnp.dot(tri, p, preferred_element_type=F32)
    return d(hi) + d(mid) + d(lo)


def _log_sigmoid(z):
    return jnp.minimum(z, 0.0) - jnp.log(1.0 + jnp.exp(-jnp.abs(z)))


def fox_gate(proj, fbias):
    T = proj.shape[0]
    tb = _tile(T, 512)

    def body(misc_ref, b_ref, fc_ref, fr_ref, frep_ref, carry):
        @pl.when(pl.program_id(0) == 0)
        def _():
            carry[...] = jnp.zeros_like(carry)

        lane = lax.broadcasted_iota(jnp.int32, (1, 128), 1)
        lf = jnp.where(lane < 4, _log_sigmoid(misc_ref[...] + b_ref[...]), 0.0)
        r = lax.broadcasted_iota(jnp.int32, (tb, tb), 0)
        c = lax.broadcasted_iota(jnp.int32, (tb, tb), 1)
        tri = jnp.where(r >= c, 1.0, 0.0).astype(MXU)
        F = _tri_matmul(tri, lf) + carry[...]
        carry[...] = carry[...] + jnp.sum(lf, axis=0, keepdims=True)
        fc_ref[0] = F
        fc_ref[1] = pltpu.roll(F, 126, 1)
        ft = F.T[0:8, :]
        fr_ref[0] = ft
        fr_ref[1] = pltpu.roll(ft, 6, 0)
        for h in range(4):
            frep_ref[h] = jnp.broadcast_to(_lane_pick(F, h), (tb, 128))

    return pl.pallas_call(
        body, name="fox_gate", grid=(T // tb,),
        in_specs=[pl.BlockSpec((tb, 128), lambda i: (i, C_MISC // 128)), pl.BlockSpec((1, 128), lambda i: (0, 0))],
        out_specs=[pl.BlockSpec((2, tb, 128), lambda i: (0, i, 0)), pl.BlockSpec((2, 8, tb), lambda i: (0, 0, i)),
                   pl.BlockSpec((4, tb, 128), lambda i: (0, i, 0))],
        out_shape=[jax.ShapeDtypeStruct((2, T, 128), F32), jax.ShapeDtypeStruct((2, 8, T), F32),
                   jax.ShapeDtypeStruct((4, T, 128), F32)],
        scratch_shapes=[pltpu.VMEM((1, 128), F32)],
        compiler_params=_cparams(("arbitrary",)),
    )(proj, fbias)


def fox_gate_bwd(dFq, dFk, proj, fbias):
    T = proj.shape[0]
    tb = _tile(T, 512)
    nb = T // tb

    def body(dq_ref, dk_ref, misc_ref, b_ref, dm_ref, db_ref, carry):
        first = pl.program_id(0) == 0

        @pl.when(first)
        def _():
            carry[...] = jnp.zeros_like(carry)

        lane = lax.broadcasted_iota(jnp.int32, (1, 128), 1)
        dF = jnp.where(lane < 4, (dq_ref[0] + dk_ref[0]) + pltpu.roll(dq_ref[1] + dk_ref[1], 2, 1), 0.0)
        r = lax.broadcasted_iota(jnp.int32, (tb, tb), 0)
        c = lax.broadcasted_iota(jnp.int32, (tb, tb), 1)
        tri = jnp.where(r <= c, 1.0, 0.0).astype(MXU)
        dlf = _tri_matmul(tri, dF) + carry[...]
        carry[...] = carry[...] + jnp.sum(dF, axis=0, keepdims=True)
        z = misc_ref[...] + b_ref[...]
        dz = jnp.where(lane < 4, dlf * (1.0 / (1.0 + jnp.exp(z))), 0.0)
        dm_ref[...] = dz
        _acc_out(db_ref, jnp.sum(dz, axis=0, keepdims=True), first)

    return pl.pallas_call(
        body, name="fox_gate_bwd", grid=(nb,),
        in_specs=[pl.BlockSpec((2, tb, 128), lambda i: (0, nb - 1 - i, 0)),
                  pl.BlockSpec((2, tb, 128), lambda i: (0, nb - 1 - i, 0)),
                  pl.BlockSpec((tb, 128), lambda i: (nb - 1 - i, C_MISC // 128)),
                  pl.BlockSpec((1, 128), lambda i: (0, 0))],
        out_specs=[pl.BlockSpec((tb, 128), lambda i: (nb - 1 - i, 0)), pl.BlockSpec((1, 128), lambda i: (0, 0))],
        out_shape=[jax.ShapeDtypeStruct((T, 128), F32), jax.ShapeDtypeStruct((1, 128), F32)],
        scratch_shapes=[pltpu.VMEM((1, 128), F32)],
        compiler_params=_cparams(("arbitrary",)),
    )(dFq, dFk, proj, fbias)


FLASH_TILE = 512


def _row_stat_tile(a, b, n):
    at = jnp.broadcast_to(a, (n, 128)).T[0:8, :]
    bt = jnp.broadcast_to(b, (n, 128)).T[0:8, :]
    sub = lax.broadcasted_iota(jnp.int32, (8, 1), 0)
    return jnp.where(sub == 0, at, jnp.where(sub == 1, bt, 0.0))


def _col_stat_tile(a, b):
    lane = lax.broadcasted_iota(jnp.int32, (1, 128), 1)
    return jnp.where(lane == 0, a, jnp.where(lane == 1, b, 0.0))


def _lane_pick(x, h):
    lane = lax.broadcasted_iota(jnp.int32, (1, 128), 1)
    return jnp.sum(jnp.where(lane == h, x, 0.0), axis=1, keepdims=True)


def _half_mask(h):
    lane = lax.broadcasted_iota(jnp.int32, (1, 128), 1)
    return (lane // HEAD_DIM) == h


def _call_hosting(body, name, grid, args, in_specs, out_specs, out_shape, scratch, exch):
    n_out = len(out_shape)
    if exch is not None:
        body, (xargs, xin, xout, xshape, xscratch) = hosted_exchange(
            body, len(args), n_out, len(scratch), grid, *exch)
        args, in_specs, out_specs = args + xargs, in_specs + xin, out_specs + xout
        out_shape, scratch = out_shape + xshape, scratch + xscratch
    res = pl.pallas_call(
        body, name=name, grid=grid, in_specs=in_specs, out_specs=out_specs, out_shape=out_shape,
        scratch_shapes=scratch, compiler_params=_cparams(("arbitrary",) * len(grid)),
    )(*args)
    return res[:n_out], res[n_out:]


def flash_fwd(q, k, v, frep, frow, *, qblk, kblk, vblk, nq, scale, name, exch=None):
    T = q.shape[0]
    tk = _tile(T, FLASH_TILE)
    tq = _tile(T, 2 * FLASH_TILE)
    per_q = tq // tk
    wq = 128 * nq
    has_f = frep is not None

    def body(*refs):
        if has_f:
            q_ref, k_ref, v_ref, fk_ref, fr_ref, o_ref, lr_ref, vT_sc, m_sc, acc_sc = refs
        else:
            q_ref, k_ref, v_ref, o_ref, lr_ref, vT_sc, m_sc, acc_sc = refs
        i = pl.program_id(1)

        @pl.when(i == 0)
        def _():
            vT_sc[...] = v_ref[...].astype(F32).T.astype(MXU)

        key_row = lax.broadcasted_iota(jnp.int32, (tk, 1), 0)
        q_col = lax.broadcasted_iota(jnp.int32, (1, tq), 1)
        row_half = lax.broadcasted_iota(jnp.int32, (128, 1), 0) // HEAD_DIM
        qb = q_ref[...].astype(F32) * scale
        if nq == 1:
            qhs = [jnp.where(_half_mask(h), qb, 0).astype(MXU) for h in range(2)]
        else:
            qhs = [qb[:, 128 * h:128 * (h + 1)].astype(MXU) for h in range(2)]
        for h in range(2):
            m_sc[h] = jnp.full((1, tq), NEG, F32)
            acc_sc[h] = jnp.zeros((128, tq), F32)

        def make_step(diag_block):
            def step(j, carry):
                off = pl.multiple_of(j * tk, tk)
                ks = k_ref[pl.ds(off, tk), :]
                vT = vT_sc[:, pl.ds(off, tk)]
                for h in range(2):
                    kh = ks if nq == 1 else ks[:, 128 * h:128 * (h + 1)]
                    sT = _dot_nt(kh, qhs[h])
                    if has_f:
                        fk = fk_ref[h, pl.ds(off, tk), :]
                        sT = sT + (fr_ref[0, h:h + 1, :] - jnp.concatenate([fk] * (tq // 128), axis=1))
                    if diag_block is not None:
                        sT = jnp.where(key_row + diag_block * tk <= q_col, sT, NEG)
                    m_prev = m_sc[h]
                    m_new = jnp.maximum(m_prev, jnp.max(sT, axis=0, keepdims=True))
                    alpha = jnp.exp(m_prev - m_new)
                    pT = jnp.exp(sT - m_new)
                    vTh = jnp.where(row_half == h, vT, jnp.ones_like(vT))
                    acc_sc[h] = alpha * acc_sc[h] + _dot(vTh, pT)
                    m_sc[h] = m_new
                return carry
            return step

        lax.fori_loop(0, per_q * i, make_step(None), 0)
        for d in range(per_q):
            make_step(d)(per_q * i + d, 0)
        outs, lses = [], []
        for h in range(2):
            acc = acc_sc[h]
            outs.append(acc / pltpu.roll(acc, HEAD_DIM, 0))
            l = acc_sc[h, HEAD_DIM * (1 - h):HEAD_DIM * (1 - h) + 1, :]
            lses.append(m_sc[h] + jnp.log(l))
        o_ref[...] = jnp.where(row_half == 0, outs[0], outs[1]).T
        sub = lax.broadcasted_iota(jnp.int32, (8, 1), 0)
        lr_ref[0] = jnp.where(sub == 0, lses[0], jnp.where(sub == 1, lses[1], 0.0))

    in_specs = [pl.BlockSpec((tq, wq), lambda p, i: (i, qblk + p)),
                pl.BlockSpec((T, wq), lambda p, i: (0, kblk + p)),
                pl.BlockSpec((T, 128), lambda p, i: (0, vblk + p))]
    args = [q, k, v]
    if has_f:
        in_specs += [pl.BlockSpec((2, T, 128), lambda p, i: (p, 0, 0)),
                     pl.BlockSpec((1, 8, tq), lambda p, i: (p, 0, i))]
        args += [frep, frow]
    out_specs = [pl.BlockSpec((tq, 128), lambda p, i: (i, p)), pl.BlockSpec((1, 8, tq), lambda p, i: (p, 0, i))]
    out_shape = [jax.ShapeDtypeStruct((T, 256), F32), jax.ShapeDtypeStruct((2, 8, T), F32)]
    scratch = [pltpu.VMEM((128, T), MXU), pltpu.VMEM((2, 1, tq), F32), pltpu.VMEM((2, 128, tq), F32)]
    return _call_hosting(body, name, (2, T // tq), args, in_specs, out_specs, out_shape, scratch, exch)


def flash_bwd(q, k, v, do, o, lrow, fcol, frow, *, qblk, kblk, vblk, nq, scale, name, exch=None):
    T = q.shape[0]
    tq = tk = _tile(T, FLASH_TILE)
    wq = 128 * nq
    nqb = T // tq
    has_f = fcol is not None

    def body(*refs):
        if has_f:
            (q_ref, k_ref, v_ref, do_ref, o_ref, lr_ref, fc_ref, fr_ref,
             dq_ref, dk_ref, dv_ref, df_ref, dfq_ref, dk_sc, dv_sc, dqT_sc, d_sc, df_sc, dfq_sc) = refs
        else:
            q_ref, k_ref, v_ref, do_ref, o_ref, lr_ref, dq_ref, dk_ref, dv_ref, dk_sc, dv_sc, dqT_sc, d_sc = refs
        j = pl.program_id(1)
        diag = lax.broadcasted_iota(jnp.int32, (tk, 1), 0) <= lax.broadcasted_iota(jnp.int32, (1, tq), 1)
        hms = [_half_mask(h) for h in range(2)]

        @pl.when(j == 0)
        def _():
            dqT_sc[...] = jnp.zeros_like(dqT_sc)
            if has_f:
                dfq_sc[...] = jnp.zeros_like(dfq_sc)

            def delta(b, carry):
                off = pl.multiple_of(b * tq, tq)
                prod = do_ref[pl.ds(off, tq), :] * o_ref[pl.ds(off, tq), :]
                Ds = [jnp.sum(jnp.where(hms[h], prod, 0.0), axis=1, keepdims=True) for h in range(2)]
                d_sc[:, pl.ds(off, tq)] = _row_stat_tile(Ds[0], Ds[1], tq)
                return carry

            lax.fori_loop(0, nqb, delta, 0)

        kb = k_ref[...]
        vb = v_ref[...]
        if nq == 1:
            khs = [jnp.where(hms[h], kb, 0).astype(MXU) for h in range(2)]
        else:
            khs = [kb[:, 128 * h:128 * (h + 1)].astype(MXU) for h in range(2)]
        kTs = [kh.astype(F32).T.astype(MXU) for kh in khs]
        kss = [(kh.astype(F32) * scale).astype(MXU) for kh in khs]
        vhs = [jnp.where(hms[h], vb, 0).astype(MXU) for h in range(2)]
        fks = [_lane_pick(fc_ref[0], h) for h in range(2)] if has_f else None
        dv_sc[...] = jnp.zeros_like(dv_sc)
        dk_sc[...] = jnp.zeros_like(dk_sc)
        if has_f:
            df_sc[...] = jnp.zeros_like(df_sc)

        def make_step(masked):
            def step(i, carry):
                off = pl.multiple_of(i * tq, tq)
                qs = q_ref[pl.ds(off, tq), :]
                dos = do_ref[pl.ds(off, tq), :]
                for h in range(2):
                    qh = qs if nq == 1 else qs[:, 128 * h:128 * (h + 1)]
                    sT = _dot_nt(kss[h], qh)
                    if has_f:
                        sT = sT + (fr_ref[0, h:h + 1, pl.ds(off, tq)] - fks[h])
                    pT = jnp.exp(sT - lr_ref[0, h:h + 1, pl.ds(off, tq)])
                    if masked:
                        pT = jnp.where(diag, pT, 0.0)
                    dsT = pT * (_dot_nt(vhs[h], dos) - d_sc[h:h + 1, pl.ds(off, tq)])
                    dv_sc[...] += _dot(pT, jnp.where(hms[h], dos, 0))
                    qq = jnp.where(hms[h], qs, 0) if nq == 1 else qh
                    dk_sc[h if nq == 2 else 0] += _dot(dsT, qq)
                    dqT_sc[h if nq == 2 else 0, :, pl.ds(off, tq)] += _dot(kTs[h], dsT)
                    if has_f:
                        part = dsT[:, 0:128]
                        for c in range(1, tq // 128):
                            part = part + dsT[:, 128 * c:128 * (c + 1)]
                        df_sc[h] += part
                        dfq_sc[h:h + 1, pl.ds(off, tq)] += jnp.sum(dsT, axis=0, keepdims=True)
                return carry
            return step

        make_step(True)(j, 0)
        lax.fori_loop(j + 1, nqb, make_step(False), 0)
        if nq == 1:
            dk_ref[...] = dk_sc[0] * scale
        else:
            dk_ref[...] = jnp.concatenate([dk_sc[0], dk_sc[1]], axis=1) * scale
        dv_ref[...] = dv_sc[...]
        if has_f:
            df_ref[0] = _col_stat_tile(-jnp.sum(df_sc[0], axis=1, keepdims=True),
                                       -jnp.sum(df_sc[1], axis=1, keepdims=True))

        @pl.when(j == nqb - 1)
        def _():
            if nq == 1:
                dq_ref[...] = dqT_sc[0].T * scale
            else:
                dq_ref[...] = jnp.concatenate([dqT_sc[0].T, dqT_sc[1].T], axis=1) * scale
            if has_f:
                sub = lax.broadcasted_iota(jnp.int32, (128, 1), 0)
                rows = jnp.where(sub == 0, dfq_sc[0:1, :], jnp.where(sub == 1, dfq_sc[1:2, :], 0.0))
                dfq_ref[0] = rows.T

    in_specs = [pl.BlockSpec((T, wq), lambda p, j: (0, qblk + p)),
                pl.BlockSpec((tk, wq), lambda p, j: (j, kblk + p)),
                pl.BlockSpec((tk, 128), lambda p, j: (j, vblk + p)),
                pl.BlockSpec((T, 128), lambda p, j: (0, p)),
                pl.BlockSpec((T, 128), lambda p, j: (0, p)),
                pl.BlockSpec((1, 8, T), lambda p, j: (p, 0, 0))]
    args = [q, k, v, do, o, lrow]
    out_specs = [pl.BlockSpec((T, wq), lambda p, j: (0, p)),
                 pl.BlockSpec((tk, wq), lambda p, j: (j, p)), pl.BlockSpec((tk, 128), lambda p, j: (j, p))]
    out_shape = [jax.ShapeDtypeStruct((T, 2 * wq), F32), jax.ShapeDtypeStruct((T, 2 * wq), F32),
                 jax.ShapeDtypeStruct((T, 256), F32)]
    scratch = [pltpu.VMEM((nq, tk, 128), F32), pltpu.VMEM((tk, 128), F32), pltpu.VMEM((nq, 128, T), F32),
               pltpu.VMEM((8, T), F32)]
    if has_f:
        in_specs += [pl.BlockSpec((1, tk, 128), lambda p, j: (p, j, 0)),
                     pl.BlockSpec((1, 8, T), lambda p, j: (p, 0, 0))]
        args += [fcol, frow]
        out_specs += [pl.BlockSpec((1, tk, 128), lambda p, j: (p, j, 0)),
                      pl.BlockSpec((1, T, 128), lambda p, j: (p, 0, 0))]
        out_shape += [jax.ShapeDtypeStruct((2, T, 128), F32), jax.ShapeDtypeStruct((2, T, 128), F32)]
        scratch += [pltpu.VMEM((2, tk, 128), F32), pltpu.VMEM((8, T), F32)]
    return _call_hosting(body, name, (2, T // tk), args, in_specs, out_specs, out_shape, scratch, exch)


def _swa_align(pair, e, h):
    sel = jnp.where(_half_mask(e), pair, 0.0)
    if e == h:
        return sel
    return pltpu.roll(sel, HEAD_DIM, 1)


def _swa_mask(n):
    W = WINDOW
    qi = lax.broadcasted_iota(jnp.int32, (W, 2 * W), 0) + W
    kj = lax.broadcasted_iota(jnp.int32, (W, 2 * W), 1)
    dist = qi - kj
    return (dist >= 0) & (dist < W) & ((n > 0) | (kj >= W))


def swa_fwd(proj, bias, sinks, exch=None):
    T = proj.shape[0]
    W = WINDOW
    nb = T // W
    scale = HEAD_DIM ** -0.5

    def body(sink_ref, q_ref, kp_ref, kc_ref, vp_ref, vc_ref, b_ref, o_ref, l_ref):
        n = pl.program_id(0)
        mask = _swa_mask(n)
        kband = jnp.concatenate([kp_ref[...], kc_ref[...]], axis=0).astype(MXU)
        vband = jnp.concatenate([vp_ref[...], vc_ref[...]], axis=0).astype(MXU)
        lane = lax.broadcasted_iota(jnp.int32, (1, 128), 1)
        lse_tile = jnp.zeros((W, 128), F32)
        pairs = []
        for h in range(2):
            full = []
            for g in range(4):
                hq = 4 * h + g
                qa = _swa_align(q_ref[:, 128 * (hq // 2):128 * (hq // 2 + 1)], hq % 2, h)
                s = _dot_nt(qa, kband) * scale + b_ref[hq]
                s = jnp.where(mask, s, NEG)
                sink = sink_ref[hq]
                m = jnp.maximum(jnp.max(s, axis=1, keepdims=True), sink)
                e = jnp.exp(s - m)
                l = jnp.sum(e, axis=1, keepdims=True) + jnp.exp(sink - m)
                r = jnp.where(_half_mask(h), _dot(e, vband), 0.0) / l
                full.append(r + pltpu.roll(r, HEAD_DIM, 1))
                lse_tile = jnp.where(lane == hq, m + jnp.log(l), lse_tile)
            pairs.append(jnp.where(_half_mask(0), full[0], full[1]))
            pairs.append(jnp.where(_half_mask(0), full[2], full[3]))
        o_ref[...] = jnp.concatenate(pairs, axis=1)
        l_ref[...] = lse_tile

    prev = lambda n: (jnp.maximum(n - 1, 0), C_KA // 128)
    cur = lambda n: (n, C_KA // 128)
    prev_v = lambda n: (jnp.maximum(n - 1, 0), C_VA // 128)
    cur_v = lambda n: (n, C_VA // 128)
    return _call_hosting(
        body, "swa_fwd", (nb,), [sinks, proj, proj, proj, proj, proj, bias],
        [pl.BlockSpec(memory_space=pltpu.SMEM),
         pl.BlockSpec((W, 512), lambda n: (n, 0)),
         pl.BlockSpec((W, 128), prev), pl.BlockSpec((W, 128), cur),
         pl.BlockSpec((W, 128), prev_v), pl.BlockSpec((W, 128), cur_v),
         pl.BlockSpec((8, W, 2 * W), lambda n: (0, 0, 0))],
        [pl.BlockSpec((W, 512), lambda n: (n, 0)), pl.BlockSpec((W, 128), lambda n: (n, 0))],
        [jax.ShapeDtypeStruct((T, 512), F32), jax.ShapeDtypeStruct((T, 128), F32)], [], exch)


def swa_bwd(proj, bias, sinks, do, o, lse, exch=None):
    T = proj.shape[0]
    W = WINDOW
    nb = T // W
    scale = HEAD_DIM ** -0.5

    def body(sink_ref, q_ref, kp_ref, kc_ref, vp_ref, vc_ref, b_ref, do_ref, o_ref, l_ref,
             dq_ref, dk_ref, dv_ref, db_ref, dsk_ref, ck, cv):
        n = pl.program_id(0)

        @pl.when(n == 0)
        def _():
            ck[...] = jnp.zeros_like(ck)
            cv[...] = jnp.zeros_like(cv)
            db_ref[...] = jnp.zeros_like(db_ref)
            dsk_ref[...] = jnp.zeros_like(dsk_ref)

        @pl.when(n < nb)
        def _():
            mask = _swa_mask(n)
            kb32 = jnp.concatenate([kp_ref[...], kc_ref[...]], axis=0)
            vb32 = jnp.concatenate([vp_ref[...], vc_ref[...]], axis=0)
            kband = kb32.astype(MXU)
            sub = lax.broadcasted_iota(jnp.int32, (8, 1), 0)
            dk_band = jnp.zeros((2 * W, 128), F32)
            dv_band = jnp.zeros((2 * W, 128), F32)
            dsk = jnp.zeros((8, 128), F32)
            dq_pairs = []
            mask4 = jnp.concatenate([mask] * 4, axis=0)
            for h in range(2):
                hm = _half_mask(h)
                km = jnp.where(hm, kb32, 0.0).astype(MXU)
                vm = jnp.where(hm, vb32, 0.0).astype(MXU)
                pbs = [slice(128 * ((4 * h + g) // 2), 128 * ((4 * h + g) // 2 + 1)) for g in range(4)]
                q4 = jnp.concatenate([_swa_align(q_ref[:, pbs[g]], g % 2, h) for g in range(4)], axis=0)
                do4 = jnp.concatenate([_swa_align(do_ref[:, pbs[g]], g % 2, h) for g in range(4)], axis=0)
                D4 = jnp.concatenate(
                    [jnp.sum(jnp.where(_half_mask(g % 2), do_ref[:, pbs[g]] * o_ref[:, pbs[g]], 0.0), axis=1,
                             keepdims=True) for g in range(4)], axis=0)
                lse4 = jnp.concatenate([_lane_pick(l_ref[...], 4 * h + g) for g in range(4)], axis=0)
                sink4 = jnp.concatenate([jnp.full((W, 1), sink_ref[4 * h + g], F32) for g in range(4)], axis=0)
                s = _dot_nt(q4, kband) * scale + b_ref[4 * h:4 * h + 4].reshape(4 * W, 2 * W)
                p = jnp.where(mask4, jnp.exp(s - lse4), 0.0)
                sd = jnp.exp(sink4 - lse4) * D4
                for g in range(4):
                    dsk = dsk + jnp.where(sub == 4 * h + g,
                                          -jnp.sum(sd[W * g:W * (g + 1)], axis=0, keepdims=True), 0.0)
                ds = p * (_dot_nt(do4, vm) - D4)
                db_ref[4 * h:4 * h + 4] += ds.reshape(4, W, 2 * W)
                dq = _dot(ds, km) * scale
                dq = dq + pltpu.roll(dq, HEAD_DIM, 1)
                dk_band = dk_band + _dot(ds.T, q4) * scale
                dv_band = dv_band + _dot(p.T, do4)
                dq_pairs.append(jnp.where(_half_mask(0), dq[0:W], dq[W:2 * W]))
                dq_pairs.append(jnp.where(_half_mask(0), dq[2 * W:3 * W], dq[3 * W:4 * W]))
            dq_ref[...] = jnp.concatenate(dq_pairs, axis=1)
            dsk_ref[...] += dsk
            dk_ref[...] = ck[...] + dk_band[0:W]
            dv_ref[...] = cv[...] + dv_band[0:W]
            ck[...] = dk_band[W:2 * W]
            cv[...] = dv_band[W:2 * W]

        @pl.when(n == nb)
        def _():
            dk_ref[...] = ck[...]
            dv_ref[...] = cv[...]

    cl = lambda n: jnp.minimum(n, nb - 1)
    pv = lambda n: jnp.maximum(jnp.minimum(n, nb - 1) - 1, 0)
    return _call_hosting(
        body, "swa_bwd", (nb + 1,), [sinks, proj, proj, proj, proj, proj, bias, do, o, lse],
        [pl.BlockSpec(memory_space=pltpu.SMEM),
         pl.BlockSpec((W, 512), lambda n: (cl(n), 0)),
         pl.BlockSpec((W, 128), lambda n: (pv(n), C_KA // 128)),
         pl.BlockSpec((W, 128), lambda n: (cl(n), C_KA // 128)),
         pl.BlockSpec((W, 128), lambda n: (pv(n), C_VA // 128)),
         pl.BlockSpec((W, 128), lambda n: (cl(n), C_VA // 128)),
         pl.BlockSpec((8, W, 2 * W), lambda n: (0, 0, 0)),
         pl.BlockSpec((W, 512), lambda n: (cl(n), 0)),
         pl.BlockSpec((W, 512), lambda n: (cl(n), 0)),
         pl.BlockSpec((W, 128), lambda n: (cl(n), 0))],
        [pl.BlockSpec((W, 512), lambda n: (cl(n), 0)),
         pl.BlockSpec((W, 128), lambda n: (jnp.maximum(n - 1, 0), 0)),
         pl.BlockSpec((W, 128), lambda n: (jnp.maximum(n - 1, 0), 0)),
         pl.BlockSpec((8, W, 2 * W), lambda n: (0, 0, 0)),
         pl.BlockSpec((8, 128), lambda n: (0, 0))],
        [jax.ShapeDtypeStruct((T, 512), F32), jax.ShapeDtypeStruct((T, 128), F32),
         jax.ShapeDtypeStruct((T, 128), F32), jax.ShapeDtypeStruct((8, W, 2 * W), F32),
         jax.ShapeDtypeStruct((8, 128), F32)],
        [pltpu.VMEM((W, 128), F32), pltpu.VMEM((W, 128), F32)], exch)


def swa_bias_table(rel_bias):
    W = WINDOW
    qi = jnp.arange(W, dtype=jnp.int32)[:, None] + W
    kj = jnp.arange(2 * W, dtype=jnp.int32)[None, :]
    dist = qi - kj
    max_exact = REL_BUCKETS // 2
    d = jnp.maximum(dist, 0)
    log_ratio = jnp.log(jnp.maximum(d, 1).astype(F32) / max_exact) / math.log(REL_MAX_DIST / max_exact)
    large = jnp.minimum(max_exact + (log_ratio * (REL_BUCKETS - max_exact)).astype(jnp.int32), REL_BUCKETS - 1)
    bucket = jnp.where(d < max_exact, d, large)
    bucket = bucket.reshape(-1)
    onehot = (bucket[None, :] == jnp.arange(REL_BUCKETS, dtype=jnp.int32)[:, None]).astype(F32)
    bias = jnp.dot(rel_bias.astype(F32).T, onehot, precision=lax.Precision.HIGHEST)
    return bias.reshape(SWA_Q_HEADS, W, 2 * W), bucket


def attn_out(oa, ob, oc, gn, wout, gpost, x):
    T = x.shape[0]
    tm = _tile(T, 512)

    def body(oa_ref, ob_ref, oc_ref, gn_ref, w_ref, gp_ref, x_ref, x2_ref, y_ref, mT_ref):
        g = gn_ref[...]
        mixed = jnp.concatenate([_rms_fwd(oa_ref[...], g[:, 0:512]), _rms_fwd(ob_ref[...], g[:, 512:768]),
                                 _rms_fwd(oc_ref[...], g[:, 768:1024])], axis=1)
        mT_ref[...] = mixed.T.astype(MXU)
        y = _dot(mixed, w_ref[...])
        y_ref[...] = y
        x2_ref[...] = x_ref[...] + _rms_fwd(y, gp_ref[...])

    row = lambda i: (i, 0)
    const = lambda i: (0, 0)
    return pl.pallas_call(
        body, name="attn_out", grid=(T // tm,),
        in_specs=[pl.BlockSpec((tm, 512), row), pl.BlockSpec((tm, 256), row), pl.BlockSpec((tm, 256), row),
                  pl.BlockSpec((1, 1024), const), pl.BlockSpec((1024, 1024), const), pl.BlockSpec((1, 1024), const),
                  pl.BlockSpec((tm, 1024), row)],
        out_specs=[pl.BlockSpec((tm, 1024), row), pl.BlockSpec((tm, 1024), row),
                   pl.BlockSpec((1024, tm), lambda i: (0, i))],
        out_shape=[jax.ShapeDtypeStruct((T, 1024), F32), jax.ShapeDtypeStruct((T, 1024), F32),
                   jax.ShapeDtypeStruct((1024, T), MXU)],
        compiler_params=_cparams(("parallel",)),
    )(oa, ob, oc, gn, wout, gpost, x)


def attn_out_bwd(dx2, y, oa, ob, oc, gn, wout, gpost):
    T = dx2.shape[0]
    tm = _tile(T, 512)

    def body(dx_ref, y_ref, oa_ref, ob_ref, oc_ref, gn_ref, w_ref, gp_ref,
             dy_ref, da_ref, db_ref, dc_ref, dgn_ref, dgp_ref):
        first = pl.program_id(0) == 0
        dy, dgp = _rms_bwd(dx_ref[...], y_ref[...], gp_ref[...])
        dy_ref[...] = dy.astype(MXU)
        _acc_out(dgp_ref, dgp, first)
        dm = _dot_nt(dy, w_ref[...])
        g = gn_ref[...]
        da, dga = _rms_bwd(dm[:, 0:512], oa_ref[...], g[:, 0:512])
        db, dgb = _rms_bwd(dm[:, 512:768], ob_ref[...], g[:, 512:768])
        dc, dgc = _rms_bwd(dm[:, 768:1024], oc_ref[...], g[:, 768:1024])
        da_ref[...] = da
        db_ref[...] = db
        dc_ref[...] = dc
        _acc_out(dgn_ref, jnp.concatenate([dga, dgb, dgc], axis=1), first)

    row = lambda i: (i, 0)
    const = lambda i: (0, 0)
    return pl.pallas_call(
        body, name="attn_out_bwd", grid=(T // tm,),
        in_specs=[pl.BlockSpec((tm, 1024), row), pl.BlockSpec((tm, 1024), row),
                  pl.BlockSpec((tm, 512), row), pl.BlockSpec((tm, 256), row), pl.BlockSpec((tm, 256), row),
                  pl.BlockSpec((1, 1024), const), pl.BlockSpec((1024, 1024), const), pl.BlockSpec((1, 1024), const)],
        out_specs=[pl.BlockSpec((tm, 1024), row), pl.BlockSpec((tm, 512), row), pl.BlockSpec((tm, 256), row),
                   pl.BlockSpec((tm, 256), row), pl.BlockSpec((1, 1024), const), pl.BlockSpec((1, 1024), const)],
        out_shape=[jax.ShapeDtypeStruct((T, 1024), MXU), jax.ShapeDtypeStruct((T, 512), F32),
                   jax.ShapeDtypeStruct((T, 256), F32), jax.ShapeDtypeStruct((T, 256), F32),
                   jax.ShapeDtypeStruct((1, 1024), F32), jax.ShapeDtypeStruct((1, 1024), F32)],
        compiler_params=_cparams(("arbitrary",)),
    )(dx2, y, oa, ob, oc, gn, wout, gpost)


FF_TILE = 256
_GELU_C = math.sqrt(2.0 / math.pi)


def _gelu(x):
    return 0.5 * x * (1.0 + jnp.tanh(_GELU_C * (x + 0.044715 * x * x * x)))


def _gelu_with_grad(x):
    x2 = x * x
    t = jnp.tanh(_GELU_C * x * (1.0 + 0.044715 * x2))
    h = 0.5 * (1.0 + t)
    return x * h, h + (0.5 * _GELU_C) * x * (1.0 - t * t) * (1.0 + (3 * 0.044715) * x2)


def _conv_taps(u, hal_ref, first):
    row = lax.broadcasted_iota(jnp.int32, (8, 1), 0)
    h6 = jnp.where(first, 0.0, hal_ref[6:7, :])
    h7 = jnp.where(first, 0.0, hal_ref[7:8, :])
    r1, r2 = pltpu.roll(u, 1, 0), pltpu.roll(u, 2, 0)
    r1 = jnp.concatenate([jnp.where(row == 0, h7, r1[0:8]), r1[8:]], axis=0)
    r2 = jnp.concatenate([jnp.where(row == 0, h6, jnp.where(row == 1, h7, r2[0:8])), r2[8:]], axis=0)
    return r1, r2


def ffn_fwd(u0, convw, convb, wdown, gpost, x2, exch=None):
    T = x2.shape[0]
    tm, tn = _tile(T, 1024), FF_TILE
    nj = D_FF // tn

    def body(ug_ref, uu_ref, hg_ref, hu_ref, wg_ref, wu_ref, bg_ref, bu_ref, wd_ref, gp_ref, x_ref, wdp_ref,
             x3_ref, y_ref, aT_ref, acc, a_sc):
        i, j = pl.program_id(0), pl.program_id(1)
        first = i == 0

        @pl.when(j == 0)
        def _():
            acc[...] = jnp.zeros_like(acc)
            a_sc[...] = jnp.zeros_like(a_sc)

        acc[...] += _dot(a_sc[...], wdp_ref[...])

        def conv(u_ref, h_ref, w_ref, b_ref):
            u = u_ref[...]
            r1, r2 = _conv_taps(u, h_ref, first)
            return b_ref[...] + w_ref[0:1, :] * r2 + w_ref[1:2, :] * r1 + w_ref[2:3, :] * u

        a = _gelu(conv(ug_ref, hg_ref, wg_ref, bg_ref)) * conv(uu_ref, hu_ref, wu_ref, bu_ref)
        aT_ref[...] = a.T.astype(MXU)
        a_sc[...] = a.astype(MXU)

        @pl.when(j == nj - 1)
        def _():
            y = acc[...] + _dot(a_sc[...], wd_ref[...])
            y_ref[...] = y
            x3_ref[...] = x_ref[...] + _rms_fwd(y, gp_ref[...])

    halo = lambda off: (lambda i, j: (jnp.maximum(i * (tm // 8) - 1, 0), off + j))
    return _call_hosting(
        body, "ffn_fwd", (T // tm, nj), [u0, u0, u0, u0, convw, convw, convb, convb, wdown, gpost, x2, wdown],
        [pl.BlockSpec((tm, tn), lambda i, j: (i, j)), pl.BlockSpec((tm, tn), lambda i, j: (i, nj + j)),
         pl.BlockSpec((8, tn), halo(0)), pl.BlockSpec((8, tn), halo(nj)),
         pl.BlockSpec((3, tn), lambda i, j: (0, j)), pl.BlockSpec((3, tn), lambda i, j: (0, nj + j)),
         pl.BlockSpec((1, tn), lambda i, j: (0, j)), pl.BlockSpec((1, tn), lambda i, j: (0, nj + j)),
         pl.BlockSpec((tn, 1024), lambda i, j: (j, 0)),
         pl.BlockSpec((1, 1024), lambda i, j: (0, 0)),
         pl.BlockSpec((tm, 1024), lambda i, j: (i, 0)),
         pl.BlockSpec((tn, 1024), lambda i, j: (jnp.maximum(j - 1, 0), 0))],
        [pl.BlockSpec((tm, 1024), lambda i, j: (i, 0)), pl.BlockSpec((tm, 1024), lambda i, j: (i, 0)),
         pl.BlockSpec((tn, tm), lambda i, j: (j, i))],
        [jax.ShapeDtypeStruct((T, 1024), F32), jax.ShapeDtypeStruct((T, 1024), F32),
         jax.ShapeDtypeStruct((D_FF, T), MXU)],
        [pltpu.VMEM((tm, 1024), F32), pltpu.VMEM((tm, tn), MXU)], exch)


def ffn_bwd(dx3, y, u0, convw, convb, wdown, gpost, wupT, x2, gfpre, exch=None):
    T = dx3.shape[0]
    tm, tn = _tile(T, 512), FF_TILE
    nj = D_FF // tn
    ni = T // tm

    def body(dx_ref, y_ref, ug_ref, uu_ref, hg_ref, hu_ref, wg_ref, wu_ref, bg_ref, bu_ref, wd_ref, gp_ref,
             wtg_ref, wtu_ref, x2_ref, gf_ref, wdn_ref, wtgp_ref, wtup_ref,
             dy_ref, dug_ref, duu_ref, dcg_ref, dcu_ref, dgp_ref, dx2_ref, dgf_ref,
             dy_sc, dh_sc, da_sc, dug_sc, duu_sc, cg, cu, ag, au):
        s, j = pl.program_id(0), pl.program_id(1)
        i = ni - 1 - s
        first_tok = i == 0
        sub = lax.broadcasted_iota(jnp.int32, (8, 1), 0)
        slot = j % 2

        @pl.when(j == 0)
        def _():
            dy, dgp = _rms_bwd(dx_ref[...], y_ref[...], gp_ref[...])
            dy_sc[...] = dy.astype(MXU)
            dy_ref[...] = dy.astype(MXU)
            _acc_out(dgp_ref, dgp, s == 0)
            dh_sc[...] = jnp.zeros_like(dh_sc)
            da_sc[0] = _dot_nt(dy.astype(MXU), wd_ref[...])
            dug_sc[...] = jnp.zeros_like(dug_sc)
            duu_sc[...] = jnp.zeros_like(duu_sc)

        @pl.when(s == 0)
        def _():
            cg[j] = jnp.zeros((8, tn), F32)
            cu[j] = jnp.zeros((8, tn), F32)
            ag[j] = jnp.zeros((8, tn), F32)
            au[j] = jnp.zeros((8, tn), F32)

        da = da_sc[slot]
        da_sc[1 - slot] = _dot_nt(dy_sc[...], wdn_ref[...])
        dh_sc[...] += _dot(dug_sc[...], wtgp_ref[...]) + _dot(duu_sc[...], wtup_ref[...])

        def conv(u_ref, h_ref, w_ref, b_ref):
            u = u_ref[...]
            r1, r2 = _conv_taps(u, h_ref, first_tok)
            return b_ref[...] + w_ref[0:1, :] * r2 + w_ref[1:2, :] * r1 + w_ref[2:3, :] * u, u, r1, r2

        gate, ugv, g1, g2 = conv(ug_ref, hg_ref, wg_ref, bg_ref)
        up, uuv, u1, u2 = conv(uu_ref, hu_ref, wu_ref, bu_ref)
        gl, dgl = _gelu_with_grad(gate)
        dup = da * gl
        dgate = da * up * dgl

        def conv_bwd(du, u, r1, r2, w_ref, c_ref, a_ref, duT_ref, du_sc):
            nxt = c_ref[j]
            n0, n1 = nxt[0:1, :], nxt[1:2, :]
            f1, f2 = pltpu.roll(du, tm - 1, 0), pltpu.roll(du, tm - 2, 0)
            f1 = jnp.concatenate([f1[:tm - 8], jnp.where(sub == 7, n0, f1[tm - 8:])], axis=0)
            f2 = jnp.concatenate([f2[:tm - 8], jnp.where(sub == 7, n1, jnp.where(sub == 6, n0, f2[tm - 8:]))], axis=0)
            du0 = w_ref[2:3, :] * du + w_ref[1:2, :] * f1 + w_ref[0:1, :] * f2
            duT_ref[...] = du0.T.astype(MXU)
            du_sc[...] = du0.astype(MXU)
            c_ref[j] = du[0:8, :]
            red = lambda v: jnp.sum(v, axis=0, keepdims=True)
            part = jnp.where(sub == 0, red(du * r2), jnp.where(sub == 1, red(du * r1), jnp.where(
                sub == 2, red(du * u), jnp.where(sub == 3, red(du), 0.0))))
            a_ref[j] = a_ref[j] + part
            return a_ref[j]

        dcg_ref[0] = conv_bwd(dgate, ugv, g1, g2, wg_ref, cg, ag, dug_ref, dug_sc)
        dcu_ref[0] = conv_bwd(dup, uuv, u1, u2, wu_ref, cu, au, duu_ref, duu_sc)

        @pl.when(j == nj - 1)
        def _():
            dh = dh_sc[...] + _dot(dug_sc[...], wtg_ref[...]) + _dot(duu_sc[...], wtu_ref[...])
            dx, dgf = _rms_bwd(dh, x2_ref[...], gf_ref[...])
            dx2_ref[...] = dx_ref[...] + dx
            _acc_out(dgf_ref, dgf, s == 0)

    rev = lambda s: ni - 1 - s
    halo = lambda off: (lambda s, j: (jnp.maximum(rev(s) * (tm // 8) - 1, 0), off + j))
    tok = pl.BlockSpec((tm, 1024), lambda s, j: (rev(s), 0))
    vec = pl.BlockSpec((1, 1024), lambda s, j: (0, 0))
    return _call_hosting(
        body, "ffn_bwd", (ni, nj),
        [dx3, y, u0, u0, u0, u0, convw, convw, convb, convb, wdown, gpost, wupT, wupT, x2, gfpre,
         wdown, wupT, wupT],
        [tok, tok,
         pl.BlockSpec((tm, tn), lambda s, j: (rev(s), j)), pl.BlockSpec((tm, tn), lambda s, j: (rev(s), nj + j)),
         pl.BlockSpec((8, tn), halo(0)), pl.BlockSpec((8, tn), halo(nj)),
         pl.BlockSpec((3, tn), lambda s, j: (0, j)), pl.BlockSpec((3, tn), lambda s, j: (0, nj + j)),
         pl.BlockSpec((1, tn), lambda s, j: (0, j)), pl.BlockSpec((1, tn), lambda s, j: (0, nj + j)),
         pl.BlockSpec((tn, 1024), lambda s, j: (j, 0)), vec,
         pl.BlockSpec((tn, 1024), lambda s, j: (j, 0)), pl.BlockSpec((tn, 1024), lambda s, j: (nj + j, 0)),
         tok, vec,
         pl.BlockSpec((tn, 1024), lambda s, j: (jnp.minimum(j + 1, nj - 1), 0)),
         pl.BlockSpec((tn, 1024), lambda s, j: (jnp.maximum(j - 1, 0), 0)),
         pl.BlockSpec((tn, 1024), lambda s, j: (nj + jnp.maximum(j - 1, 0), 0))],
        [tok,
         pl.BlockSpec((tn, tm), lambda s, j: (j, rev(s))), pl.BlockSpec((tn, tm), lambda s, j: (j, rev(s))),
         pl.BlockSpec((1, 8, tn), lambda s, j: (s, 0, j)), pl.BlockSpec((1, 8, tn), lambda s, j: (s, 0, j)),
         vec, tok, vec],
        [jax.ShapeDtypeStruct((T, 1024), MXU), jax.ShapeDtypeStruct((D_FF, T), MXU),
         jax.ShapeDtypeStruct((D_FF, T), MXU),
         jax.ShapeDtypeStruct((ni, 8, D_FF), F32), jax.ShapeDtypeStruct((ni, 8, D_FF), F32),
         jax.ShapeDtypeStruct((1, 1024), F32), jax.ShapeDtypeStruct((T, 1024), F32),
         jax.ShapeDtypeStruct((1, 1024), F32)],
        [pltpu.VMEM((tm, 1024), MXU), pltpu.VMEM((tm, 1024), F32), pltpu.VMEM((2, tm, tn), F32),
         pltpu.VMEM((tm, tn), MXU), pltpu.VMEM((tm, tn), MXU)] + [pltpu.VMEM((nj, 8, tn), F32)] * 4, exch)


ELEMS_PER_BLOCK = 512 * 1024


def _row_block(R, C):
    if R * C <= ELEMS_PER_BLOCK or R % 8:
        return R
    best = 8
    for t in range(8, R + 1, 8):
        if R % t == 0 and t * C <= ELEMS_PER_BLOCK:
            best = t
    return best


def adamw(w, g, m, v, name):
    L, R, C = w.shape
    partials = isinstance(g, (list, tuple))
    tr = _row_block(R, 2 * C)
    c1 = 1.0 - ADAM_B1 ** ADAM_STEP
    c2 = 1.0 - ADAM_B2 ** ADAM_STEP

    def body(w_ref, *rest):
        g_refs, (m_ref, v_ref, g_out, d_ref, nm_ref, nv_ref) = rest[:-6], rest[-6:]

        def step(gv):
            g_out[0] = gv
            nm = ADAM_B1 * m_ref[0] + (1.0 - ADAM_B1) * gv
            nv = ADAM_B2 * v_ref[0] + (1.0 - ADAM_B2) * (gv * gv)
            nm_ref[0] = nm
            nv_ref[0] = nv
            d_ref[0] = -ADAM_LR * ((nm / c1) / (jnp.sqrt(nv / c2) + ADAM_EPS) + ADAM_WD * w_ref[0])

        if not partials:
            step(g_refs[0][0])
            return
        for k in range(L):
            @pl.when(pl.program_id(0) == k)
            def _(k=k):
                gv = g_refs[k][0].astype(F32)
                for d in range(1, N_DEV):
                    gv = gv + g_refs[k][d].astype(F32)
                step(gv)

    spec = pl.BlockSpec((1, tr, C), lambda l, i: (l, i, 0))
    if partials:
        gspecs = [pl.BlockSpec((N_DEV, tr, C), lambda l, i, k=k: (0, jnp.where(l == k, i, 0), 0)) for k in range(L)]
        gs = list(g)
    else:
        gspecs, gs = [spec], [g]
    return pl.pallas_call(
        body, name=name, grid=(L, R // tr), in_specs=[spec] + gspecs + [spec, spec], out_specs=[spec] * 4,
        out_shape=[jax.ShapeDtypeStruct((L, R, C), F32)] * 4,
        compiler_params=_cparams(("arbitrary", "arbitrary")),
    )(w, *gs, m, v)


def sum_devices(buf, name):
    _, R, C = buf.shape
    tr = _row_block(R, C * 4)

    def body(b_ref, o_ref):
        acc = b_ref[0].astype(F32)
        for d in range(1, N_DEV):
            acc = acc + b_ref[d].astype(F32)
        o_ref[...] = acc

    return pl.pallas_call(
        body, name=name, grid=(R // tr,),
        in_specs=[pl.BlockSpec((N_DEV, tr, C), lambda i: (0, i, 0))],
        out_specs=pl.BlockSpec((tr, C), lambda i: (i, 0)),
        out_shape=jax.ShapeDtypeStruct((R, C), F32),
        compiler_params=_cparams(("parallel",)),
    )(buf)


def _exchange_copies(src_refs, out_refs, send_sems, recv_sems, gather):
    x, y, c = lax.axis_index("x"), lax.axis_index("y"), lax.axis_index("c")
    me = 4 * x + 2 * y + c
    flip = lambda a, bit: 1 - a if bit else a
    part = lambda ref, d: ref if gather else ref.at[d]
    copies = []
    for k in range(1, N_DEV):
        px, py, pc = flip(x, (k >> 2) & 1), flip(y, (k >> 1) & 1), flip(c, k & 1)
        peer = 4 * px + 2 * py + pc
        for t in range(len(src_refs)):
            sem = t * (N_DEV - 1) + k - 1
            mk = lambda s, d: pltpu.make_async_remote_copy(
                src_ref=s, dst_ref=d, send_sem=send_sems.at[sem], recv_sem=recv_sems.at[sem],
                device_id=(px, py, pc), device_id_type=pl.DeviceIdType.MESH)
            copies.append((mk(part(src_refs[t], peer), out_refs[t].at[me]),
                           mk(part(src_refs[t], me), out_refs[t].at[peer])))
    return me, copies


def exchange(srcs, name, gather):
    n = len(srcs)
    shapes = [(N_DEV,) + s.shape if gather else s.shape for s in srcs]

    def body(*refs):
        src_refs, out_refs = refs[:n], refs[n:2 * n]
        send_sems, recv_sems, local_sems = refs[2 * n:]
        me, copies = _exchange_copies(src_refs, out_refs, send_sems, recv_sems, gather)
        for outgoing, _ in copies:
            outgoing.start()
        mine = [pltpu.make_async_copy(src_refs[t] if gather else src_refs[t].at[me], out_refs[t].at[me],
                                      local_sems.at[t]) for t in range(n)]
        for cp in mine:
            cp.start()
        for _, incoming in copies:
            incoming.wait_recv()
        for outgoing, _ in copies:
            outgoing.wait_send()
        for cp in mine:
            cp.wait()

    return pl.pallas_call(
        body, name=name,
        in_specs=[pl.BlockSpec(memory_space=pl.ANY)] * n, out_specs=[pl.BlockSpec(memory_space=pl.ANY)] * n,
        out_shape=[jax.ShapeDtypeStruct(shp, s.dtype) for shp, s in zip(shapes, srcs)],
        scratch_shapes=[pltpu.SemaphoreType.DMA((n * (N_DEV - 1),)), pltpu.SemaphoreType.DMA((n * (N_DEV - 1),)),
                        pltpu.SemaphoreType.DMA((n,))],
    )(*srcs)


def hosted_exchange(body, n_in, n_out, n_scratch, grid, srcs, gather):
    n = len(srcs)
    shapes = [(N_DEV,) + s.shape if gather else s.shape for s in srcs]

    def wrapped(*refs):
        ins, xin = refs[:n_in], refs[n_in:n_in + n]
        outs = refs[n_in + n:n_in + n + n_out]
        xout = refs[n_in + n + n_out:n_in + 2 * n + n_out]
        rest = refs[n_in + 2 * n + n_out:]
        scratch, (send_sems, recv_sems, local_sems) = rest[:n_scratch], rest[n_scratch:]
        ids = [pl.program_id(a) for a in range(len(grid))]
        first = functools.reduce(jnp.logical_and, [i == 0 for i in ids])
        last = functools.reduce(jnp.logical_and, [i == g - 1 for i, g in zip(ids, grid)])
        me, copies = _exchange_copies(xin, xout, send_sems, recv_sems, gather)
        mine = [pltpu.make_async_copy(xin[t] if gather else xin[t].at[me], xout[t].at[me], local_sems.at[t])
                for t in range(n)]

        @pl.when(first)
        def _():
            for outgoing, _ in copies:
                outgoing.start()
            for cp in mine:
                cp.start()

        body(*ins, *outs, *scratch)

        @pl.when(last)
        def _():
            for _, incoming in copies:
                incoming.wait_recv()
            for outgoing, _ in copies:
                outgoing.wait_send()
            for cp in mine:
                cp.wait()

    any_spec = pl.BlockSpec(memory_space=pl.ANY)
    return wrapped, (list(srcs), [any_spec] * n, [any_spec] * n,
                     [jax.ShapeDtypeStruct(shp, s.dtype) for shp, s in zip(shapes, srcs)],
                     [pltpu.SemaphoreType.DMA((n * (N_DEV - 1),)), pltpu.SemaphoreType.DMA((n * (N_DEV - 1),)),
                      pltpu.SemaphoreType.DMA((n,))])


def _pack(parts, cols, row_align, dtype):
    flat = jnp.concatenate([p.astype(dtype) for p in parts], axis=-1)
    n = flat.shape[-1]
    block = cols * row_align
    total = -(-n // block) * block
    flat = jnp.pad(flat, [(0, 0)] * (flat.ndim - 1) + [(0, total - n)])
    return flat.reshape(flat.shape[:-1] + (total // cols, cols))


def _unpack(buf, shapes):
    lead = buf.shape[:-2]
    flat = buf.reshape(lead + (-1,))
    out, off = [], 0
    for s in shapes:
        n = int(np.prod(s))
        out.append(flat[..., off:off + n].reshape(lead + tuple(s)))
        off += n
    return out


SHARD_SHAPES = [(128, IN_COLS), (256, 48), (128, 64), (128, 1024), (1024, 704), (352, 1024)]
SHARDED = ["w_in", "w_uq", "w_ukv", "w_out", "w_up", "w_down"]
ATTN_SENT = ["w_in_p", "w_uq", "w_ukv", "w_out"]
UP_HALF = 352
FFN_SIDE = ["w_upT", "conv_w", "w_down"]


def _full_from_shards(name, s):
    if name in ("w_in", "w_in_p", "w_out", "w_down", "w_upT"):
        return s.reshape((-1, s.shape[-1]))
    return s.transpose(1, 0, 2).reshape((s.shape[1], -1))


def _shards_from_full(name, f):
    if name in ("w_in", "w_in_p", "w_out", "w_down", "w_upT"):
        return f.reshape((N_DEV, -1, f.shape[-1]))
    return f.reshape((f.shape[0], N_DEV, -1)).transpose(1, 0, 2)


def _perm_w_in(w):
    z = lambda n: jnp.zeros(w.shape[:-1] + (n,), w.dtype)
    return jnp.concatenate([w[..., :1536], w[..., 1540:1924], w[..., 1536:1540], z(60), w[..., 1924:1956], z(32)],
                           axis=-1)


def _unperm_w_in(d):
    return jnp.concatenate([d[..., :1536], d[..., 1920:1924], d[..., 1536:1920], d[..., 1984:2016]], axis=-1)


def _perm_w_uq(w):
    return jnp.pad(w.reshape(256, 4, MLA_QK_DIM), ((0, 0), (0, 0), (0, 128 - MLA_QK_DIM))).reshape(256, 512)


def _unperm_w_uq(d):
    return d.reshape(256, 4, 128)[:, :, :MLA_QK_DIM].reshape(256, 4 * MLA_QK_DIM)


def _perm_w_ukv(w):
    w4 = w.reshape(128, 4, 128)
    k = jnp.pad(w4[:, :, :64], ((0, 0), (0, 0), (0, 64))).reshape(128, 512)
    return jnp.concatenate([k, w4[:, :, 64:].reshape(128, 256)], axis=1)


def _unperm_w_ukv(d):
    dk = d[:, :512].reshape(128, 4, 128)[:, :, :64]
    dv = d[:, 512:].reshape(128, 4, 64)
    return jnp.concatenate([dk, dv], axis=-1).reshape(128, 512)


def _row(v, width=None):
    v = v.reshape(1, -1).astype(F32)
    if width is not None and v.shape[1] < width:
        v = jnp.pad(v, ((0, 0), (0, width - v.shape[1])))
    return v


def _layer_fwd(x, P, shared, send=None, ffn_from=None):
    cosr, sinr, bias = shared
    ex = lambda host: (send[host], True) if send is not None and send.get(host) else None
    proj, hT, projb = norm_matmul(x, P["g_pre"], P["w_in_p"], "in_proj", lo_tiles=C_CQ // 512)
    qm, km, vm, cqT, ckvT = mla_prep(proj, P["gq"], P["gkv"], P["w_uq_p"], P["w_ukv_p"], cosr, sinr)
    fcol, frow, frep = fox_gate(proj, P["fbias"])
    (oa, lse_a), got_swa = swa_fwd(proj, bias, P["sinks"], exch=ex("swa"))
    (ob, lrb), got_fox = flash_fwd(projb, projb, projb, frep, frow, qblk=C_QF // 128, kblk=C_KF // 128,
                                   vblk=C_VF // 128, nq=1, scale=HEAD_DIM ** -0.5, name="fox_fwd", exch=ex("fox"))
    (oc, lrc), got_mla = flash_fwd(qm, km, vm, None, None, qblk=0, kblk=0, vblk=0, nq=2,
                                   scale=MLA_QK_DIM ** -0.5, name="mla_fwd", exch=ex("mla"))
    x2, y1, mT = attn_out(oa, ob, oc, P["gn"], P["w_out"], P["g_apost"], x)
    if ffn_from is not None:
        P = dict(P, **ffn_from(got_swa, got_fox, got_mla))
    u0, h2 = norm_matmul(x2, P["g_fpre"], P["w_upT"], "up_proj", tn_pref=1536, w_transposed=True,
                         h_transposed=False)
    (x3, y2, aT), got_ffn = ffn_fwd(u0, P["conv_w"], P["conv_b"], P["w_down"], P["g_fpost"], x2, exch=ex("ffn"))
    S = dict(x=x, proj=proj, projb=projb, hT=hT, qm=qm, km=km, vm=vm, cqT=cqT, ckvT=ckvT, fcol=fcol, frow=frow,
             oa=oa, lse_a=lse_a, ob=ob, lrb=lrb, oc=oc, lrc=lrc,
             x2=x2, y1=y1, mT=mT, u0=u0, h2=h2, y2=y2, aT=aT)
    return x3, S, P, got_ffn


def _layer_bwd(dx3, P, S, shared, send_attn=None):
    cosr, sinr, bias = shared
    proj = S["proj"]
    G = {}
    got = {}
    ex = lambda arrays: (arrays, False) if send_attn is not None and arrays else None
    (dy2, dugT, duuT, dcg, dcu, G["ffn_post_norm"], dx2, G["ffn_pre_norm"]), got["ffn"] = ffn_bwd(
        dx3, S["y2"], S["u0"], P["conv_w"], P["conv_b"], P["w_down"], P["g_fpost"], P["w_upT"], S["x2"], P["g_fpre"],
        exch=ex(send_attn))
    dconv = jnp.concatenate([dcg[-1], dcu[-1]], axis=1)
    G["conv_w"], G["conv_b"] = dconv[0:3], dconv[3]
    G["w_down"] = matmul_nn(S["aT"], dy2, "dw_down", MXU)
    G["w_upT"] = jnp.concatenate([matmul_nn(dugT, S["h2"], "dw_up_gate", MXU),
                                  matmul_nn(duuT, S["h2"], "dw_up_up", MXU)], axis=0)
    G["w_up"] = G["w_upT"].T
    dy1, doa, dob, doc, G["group_norm"], G["attn_post_norm"] = attn_out_bwd(
        dx2, S["y1"], S["oa"], S["ob"], S["oc"], P["gn"], P["w_out"], P["g_apost"])
    G["w_out"] = matmul_nn(S["mT"], dy1, "dw_out", MXU)
    up_slices = _shards_from_full("w_upT", G["w_upT"])
    (dqa, dka, dva, dbias, dsk), got["swa"] = swa_bwd(proj, bias, P["sinks"], doa, S["oa"], S["lse_a"],
                                                      exch=ex([_shards_from_full("w_down", G["w_down"])]))
    G["swa_sinks"] = dsk[:, 0]
    pb = S["projb"]
    (dqf, dkf, dvf, dFk, dFq), got["fox"] = flash_bwd(
        pb, pb, pb, dob, S["ob"], S["lrb"], S["fcol"], S["frow"], name="fox_bwd", qblk=C_QF // 128,
        kblk=C_KF // 128, vblk=C_VF // 128, nq=1, scale=HEAD_DIM ** -0.5, exch=ex([up_slices[:, :UP_HALF]]))
    dmisc_f, dfb = fox_gate_bwd(dFq, dFk, proj, P["fbias"])
    G["forget_bias"] = dfb[0, 0:4]
    (dqm_, dkm_, dvm_), got["mla"] = flash_bwd(
        S["qm"], S["km"], S["vm"], doc, S["oc"], S["lrc"], None, None, name="mla_bwd",
        qblk=0, kblk=0, vblk=0, nq=2, scale=MLA_QK_DIM ** -0.5, exch=ex([up_slices[:, UP_HALF:]]))
    dqm, dkv, dcq, dckv, dmisc_r, G["q_latent_norm"], G["kv_latent_norm"] = mla_prep_bwd(
        dqm_, dkm_, dvm_, proj, P["gq"], P["gkv"], P["w_uq_p"], P["w_ukv_p"], cosr, sinr)
    G["w_uq"] = _unperm_w_uq(matmul_nn(S["cqT"], dqm, "dw_uq", MXU))
    G["w_ukv"] = _unperm_w_ukv(matmul_nn(S["ckvT"], dkv, "dw_ukv", MXU))
    dproj = jnp.concatenate([dqa, dka, dva, dqf, dkf, dvf, dcq, dckv, dmisc_f + dmisc_r], axis=1).astype(MXU)
    G["w_in_p"] = matmul_nn(S["hT"], dproj, "dw_in", MXU)
    G["w_in"] = _unperm_w_in(G["w_in_p"])
    dx, G["attn_pre_norm"] = matmul_nt_normbwd(dproj, P["w_in_p"], S["x"], P["g_pre"], dx2, "in_bwd")
    return dx, G, dbias, got


def _layer_params(l, full, small):
    return dict(
        g_pre=_row(small["attn_pre_norm"][l]),
        w_in_p=full["w_in_p"] if "w_in_p" in full else _perm_w_in(full["w_in"]),
        gq=_row(small["q_latent_norm"][l]), gkv=_row(small["kv_latent_norm"][l]),
        w_uq_p=_perm_w_uq(full["w_uq"]), w_ukv_p=_perm_w_ukv(full["w_ukv"]),
        fbias=_row(small["forget_bias"][l], 128), sinks=small["swa_sinks"][l].astype(F32),
        gn=_row(small["group_norm"][l]), w_out=full["w_out"], g_apost=_row(small["attn_post_norm"][l]),
        g_fpre=_row(small["ffn_pre_norm"][l]), conv_b=_row(small["conv_b"][l]),
        g_fpost=_row(small["ffn_post_norm"][l]),
        **{n: full[n] for n in FFN_SIDE if n in full},
        **({"w_upT": full["w_up"].T} if "w_up" in full else {}))


def _rel_bias_grad(dbias, bucket):
    flat = dbias.reshape(SWA_Q_HEADS, -1)
    hi = flat.astype(MXU)
    lo = (flat - hi.astype(F32)).astype(MXU)
    onehot = (bucket[:, None] == jnp.arange(128, dtype=jnp.int32)[None, :]).astype(MXU)
    r = matmul_nn(jnp.concatenate([hi, lo], axis=0), onehot, "rel_bias_grad")
    return (r[0:8] + r[8:16])[:, :REL_BUCKETS].T


def local_step(x, tgt, fulls, small, comm=None):
    T = x.shape[0]
    cosr, sinr = rope_tables(T)
    bias, bucket = swa_bias_table(small["rel_bias"])
    shared = (cosr, sinr, bias)
    Ps, Ss = [], []
    h, full = x, fulls[0]
    for l in range(DEPTH):
        P = _layer_params(l, full, small)
        if comm:
            h, S, P, got = _layer_fwd(h, P, shared, comm["weight_parts"](l), comm["ffn_from"])
            full = comm["attn_from"](got) if l + 1 < DEPTH else None
        else:
            h, S, P, _ = _layer_fwd(h, P, shared)
            full = fulls[l + 1] if l + 1 < DEPTH else None
        Ps.append(P)
        Ss.append(S)
    dh, sq = loss_kernel(h, tgt)
    grads = [None] * DEPTH
    dbias_sum = None
    pending = [] if comm else None
    for l in reversed(range(DEPTH)):
        dh, grads[l], dbias, got = _layer_bwd(dh, Ps[l], Ss[l], shared, pending)
        dbias_sum = dbias if dbias_sum is None else dbias_sum + dbias
        if comm:
            comm["landed"](l, ["w_down"], got["swa"])
            comm["landed"](l, ["w_upT"], [jnp.concatenate([got["fox"][0], got["mla"][0]], axis=1)])
            if pending:
                comm["landed"](l + 1, ATTN_SENT, got["ffn"])
            pending = [_shards_from_full(n, grads[l][n]) for n in ATTN_SENT]
    return sq, dh, grads, _rel_bias_grad(dbias_sum, bucket), pending


WEIGHTS = ['attn_pre_norm', 'w_in', 'forget_bias', 'swa_sinks', 'rel_bias', 'q_latent_norm', 'w_uq',
           'kv_latent_norm', 'w_ukv', 'group_norm', 'w_out', 'attn_post_norm', 'ffn_pre_norm', 'w_up', 'conv_w',
           'conv_b', 'w_down', 'ffn_post_norm']
SMALL_PER_LAYER = ['attn_pre_norm', 'forget_bias', 'swa_sinks', 'q_latent_norm', 'kv_latent_norm', 'group_norm',
                   'attn_post_norm', 'ffn_pre_norm', 'conv_b', 'ffn_post_norm', 'conv_w']


def kernel(x, attn_pre_norm, w_in, forget_bias, swa_sinks, rel_bias, q_latent_norm, w_uq, kv_latent_norm, w_ukv, group_norm, w_out, attn_post_norm, ffn_pre_norm, w_up, conv_w, conv_b, w_down, ffn_post_norm, loss_target, m_attn_pre_norm, m_w_in, m_forget_bias, m_swa_sinks, m_rel_bias, m_q_latent_norm, m_w_uq, m_kv_latent_norm, m_w_ukv, m_group_norm, m_w_out, m_attn_post_norm, m_ffn_pre_norm, m_w_up, m_conv_w, m_conv_b, m_w_down, m_ffn_post_norm, v_attn_pre_norm, v_w_in, v_forget_bias, v_swa_sinks, v_rel_bias, v_q_latent_norm, v_w_uq, v_kv_latent_norm, v_w_ukv, v_group_norm, v_w_out, v_attn_post_norm, v_ffn_pre_norm, v_w_up, v_conv_w, v_conv_b, v_w_down, v_ffn_post_norm):
    W = dict(attn_pre_norm=attn_pre_norm, w_in=w_in, forget_bias=forget_bias, swa_sinks=swa_sinks, rel_bias=rel_bias,
             q_latent_norm=q_latent_norm, w_uq=w_uq, kv_latent_norm=kv_latent_norm, w_ukv=w_ukv,
             group_norm=group_norm, w_out=w_out, attn_post_norm=attn_post_norm, ffn_pre_norm=ffn_pre_norm,
             w_up=w_up, conv_w=conv_w, conv_b=conv_b, w_down=w_down, ffn_post_norm=ffn_post_norm)
    M = dict(attn_pre_norm=m_attn_pre_norm, w_in=m_w_in, forget_bias=m_forget_bias, swa_sinks=m_swa_sinks,
             rel_bias=m_rel_bias, q_latent_norm=m_q_latent_norm, w_uq=m_w_uq, kv_latent_norm=m_kv_latent_norm,
             w_ukv=m_w_ukv, group_norm=m_group_norm, w_out=m_w_out, attn_post_norm=m_attn_post_norm,
             ffn_pre_norm=m_ffn_pre_norm, w_up=m_w_up, conv_w=m_conv_w, conv_b=m_conv_b, w_down=m_w_down,
             ffn_post_norm=m_ffn_post_norm)
    V = dict(attn_pre_norm=v_attn_pre_norm, w_in=v_w_in, forget_bias=v_forget_bias, swa_sinks=v_swa_sinks,
             rel_bias=v_rel_bias, q_latent_norm=v_q_latent_norm, w_uq=v_w_uq, kv_latent_norm=v_kv_latent_norm,
             w_ukv=v_w_ukv, group_norm=v_group_norm, w_out=v_w_out, attn_post_norm=v_attn_post_norm,
             ffn_pre_norm=v_ffn_pre_norm, w_up=v_w_up, conv_w=v_conv_w, conv_b=v_conv_b, w_down=v_w_down,
             ffn_post_norm=v_ffn_post_norm)
    me = 4 * lax.axis_index("x") + 2 * lax.axis_index("y") + lax.axis_index("c")

    def attn_shards(l):
        return [_perm_w_in(w_in[l].astype(MXU))] + [W[n][l].astype(MXU) for n in ATTN_SENT[1:]]

    def weight_parts(l):
        up = jnp.swapaxes(W["w_up"][l], 0, 1).astype(MXU)
        return dict(swa=[W["w_down"][l].astype(MXU)], fox=[up[:UP_HALF]], mla=[up[UP_HALF:], conv_w[l]],
                    ffn=attn_shards(l + 1) if l + 1 < DEPTH else [])

    def ffn_from(got_swa, got_fox, got_mla):
        return dict(w_down=_full_from_shards("w_down", got_swa[0]),
                    w_upT=_full_from_shards("w_upT", jnp.concatenate([got_fox[0], got_mla[0]], axis=1)),
                    conv_w=got_mla[1].transpose(1, 0, 2).reshape(3, 2 * D_FF))

    def attn_from(got_ffn):
        return {n: _full_from_shards(n, s) for n, s in zip(ATTN_SENT, got_ffn)}

    landed = [{} for _ in range(DEPTH)]

    def on_landed(l, names, arrays):
        landed[l].update(zip(names, arrays))

    comm = dict(weight_parts=weight_parts, ffn_from=ffn_from, attn_from=attn_from, landed=on_landed)
    full0 = dict(zip(ATTN_SENT, map(_full_from_shards, ATTN_SENT, exchange(attn_shards(0), "gather_weights", True))))
    sq, dx, grads, drel, last = local_step(x[0], loss_target[0], [full0], W, comm)
    on_landed(0, ATTN_SENT, exchange(last, "scatter_grads", False))
    for l in range(DEPTH):
        landed[l]["w_in"] = _unperm_w_in(landed[l]["w_in_p"])
    G = {}

    parts, shapes = [], []
    for l in range(DEPTH):
        for n in SMALL_PER_LAYER:
            parts.append(grads[l][n].astype(F32).reshape(-1))
            shapes.append(grads[l][n].shape)
    parts += [drel.reshape(-1), jnp.sum(sq).reshape(1) * (0.5 / D_MODEL)]
    shapes += [drel.shape, (1,)]
    red = _unpack(sum_devices(exchange([_pack(parts, 128, 8, F32)], "gather_small", True)[0], "sum_small"), shapes)
    k = 0
    per = {n: [] for n in SMALL_PER_LAYER}
    for l in range(DEPTH):
        for n in SMALL_PER_LAYER:
            per[n].append(red[k])
            k += 1
    for n in SMALL_PER_LAYER:
        G[n] = jnp.stack(per[n]).reshape((DEPTH, 3, 2 * D_FF) if n == "conv_w" else W[n].shape)
    G["rel_bias"] = red[k]
    loss = red[k + 1][0]
    G["conv_w"] = lax.dynamic_slice_in_dim(G["conv_w"], me * 704, 704, axis=2)

    delta, new_m, new_v = {}, {}, {}
    for n in WEIGHTS:
        shp = W[n].shape
        if n == "w_up":
            v3 = lambda a: jnp.swapaxes(a, 1, 2)
            back = v3
            g = [landed[l]["w_upT"] for l in range(DEPTH)]
        else:
            v3 = lambda a: a.reshape(shp if len(shp) == 3 else (1,) + shp)
            back = lambda a: a.reshape(shp)
            g = [landed[l][n] for l in range(DEPTH)] if n in SHARDED else v3(G[n])
        g, d, nm, nv = adamw(v3(W[n]), g, v3(M[n]), v3(V[n]), "adamw_" + n)
        G[n], delta[n], new_m[n], new_v[n] = back(g), back(d), back(nm), back(nv)
    return (loss, dx[None], *[G[n] for n in WEIGHTS], *[delta[n] for n in WEIGHTS],
            *[new_m[n] for n in WEIGHTS], *[new_v[n] for n in WEIGHTS])
```

```python
import functools
import math

import numpy as np
import jax
import jax.numpy as jnp
from jax import lax
from jax.experimental import pallas as pl
from jax.experimental.pallas import tpu as pltpu

F32 = jnp.float32
MXU = jnp.bfloat16

N_DEV = 8
DEPTH = 4
D_MODEL = 1024
HEAD_DIM = 64
WINDOW = 128
SWA_Q_HEADS = 8
REL_BUCKETS = 32
REL_MAX_DIST = 128
MLA_QK_DIM = 96
ROPE_DIM = 32
ROPE_THETA = 10000.0
D_FF = 2816
EPS = 1e-6
NEG = -1e30
IN_COLS = 1956
IN_COLS_P = 2048
C_QA, C_KA, C_VA = 0, 512, 640
C_QF, C_KF, C_VF = 768, 1024, 1280
C_CQ, C_CKV, C_MISC = 1536, 1792, 1920
ROPE_LANE0 = 64
ADAM_LR, ADAM_B1, ADAM_B2, ADAM_EPS, ADAM_WD, ADAM_STEP = 0.001, 0.9, 0.999, 1e-08, 0.01, 10

VMEM_LIMIT = 56 * 1024 * 1024
PACK_COLS = 1024
PACK_ROW_ALIGN = 16


def _cparams(sem=None):
    return pltpu.CompilerParams(dimension_semantics=sem, vmem_limit_bytes=VMEM_LIMIT)


def _tile(n, pref):
    if n <= pref:
        return n
    t = pref - pref % 128
    while t >= 128:
        if n % t == 0:
            return t
        t -= 128
    return n


def _dot(a, b):
    return jnp.dot(a.astype(MXU), b.astype(MXU), preferred_element_type=F32)


def _dot_nt(a, b):
    return lax.dot_general(a.astype(MXU), b.astype(MXU), (((1,), (1,)), ((), ())),
                           preferred_element_type=F32)


def _rms_fwd(x, g):
    return x * lax.rsqrt(jnp.mean(x * x, axis=-1, keepdims=True) + EPS) * g


def _rms_bwd(dy, x, g, n=None):
    r = lax.rsqrt(jnp.mean(x * x, axis=-1, keepdims=True) + EPS)
    xh = x * r
    dg = jnp.sum(dy * xh, axis=0, keepdims=True)
    dxh = dy * g
    dx = r * (dxh - xh * jnp.mean(dxh * xh, axis=-1, keepdims=True))
    return dx, dg


def _acc_out(ref, val, first):
    @pl.when(first)
    def _():
        ref[...] = val

    @pl.when(jnp.logical_not(first))
    def _():
        ref[...] += val


def norm_matmul(x, g, w, name, lo_tiles=0, tn_pref=512, w_transposed=False, h_transposed=True):
    T, K = x.shape
    N = w.shape[0] if w_transposed else w.shape[1]
    tm, tn = _tile(T, 1024), _tile(N, tn_pref)

    def body(x_ref, g_ref, w_ref, o_ref, hT_ref, *rest):
        h_sc = rest[-1]
        j = pl.program_id(1)

        @pl.when(j == 0)
        def _():
            h = _rms_fwd(x_ref[...], g_ref[...])
            h_sc[...] = h.astype(MXU)
            hT_ref[...] = (h.T if h_transposed else h).astype(MXU)

        r = (_dot_nt if w_transposed else _dot)(h_sc[...], w_ref[...])
        o_ref[...] = r
        if lo_tiles:
            @pl.when(j < lo_tiles)
            def _():
                rest[0][...] = r.astype(MXU)

    h_spec = pl.BlockSpec((K, tm), lambda i, j: (0, i)) if h_transposed else pl.BlockSpec((tm, K), lambda i, j: (i, 0))
    out_specs = [pl.BlockSpec((tm, tn), lambda i, j: (i, j)), h_spec]
    out_shape = [jax.ShapeDtypeStruct((T, N), F32), jax.ShapeDtypeStruct((K, T) if h_transposed else (T, K), MXU)]
    if lo_tiles:
        out_specs.append(pl.BlockSpec((tm, tn), lambda i, j: (i, jnp.minimum(j, lo_tiles - 1))))
        out_shape.append(jax.ShapeDtypeStruct((T, lo_tiles * tn), MXU))
    return pl.pallas_call(
        body, name=name, grid=(T // tm, N // tn),
        in_specs=[pl.BlockSpec((tm, K), lambda i, j: (i, 0)),
                  pl.BlockSpec((1, K), lambda i, j: (0, 0)),
                  pl.BlockSpec((tn, K), lambda i, j: (j, 0)) if w_transposed else
                  pl.BlockSpec((K, tn), lambda i, j: (0, j))],
        out_specs=out_specs, out_shape=out_shape,
        scratch_shapes=[pltpu.VMEM((tm, K), MXU)],
        compiler_params=_cparams(("parallel", "arbitrary")),
    )(x, g, w)


def matmul_nn(a, b, name, out_dtype=F32):
    M, K = a.shape
    N = b.shape[1]
    tm, tn, tk = _tile(M, 1408), _tile(N, 1536), _tile(K, 1024)
    nk = K // tk

    def body(a_ref, b_ref, o_ref, acc):
        k = pl.program_id(2)
        part = _dot(a_ref[...], b_ref[...])
        _acc_out(acc, part, k == 0)

        @pl.when(k == nk - 1)
        def _():
            o_ref[...] = acc[...].astype(out_dtype)

    return pl.pallas_call(
        body, name=name, grid=(M // tm, N // tn, nk),
        in_specs=[pl.BlockSpec((tm, tk), lambda i, j, k: (i, k)),
                  pl.BlockSpec((tk, tn), lambda i, j, k: (k, j))],
        out_specs=pl.BlockSpec((tm, tn), lambda i, j, k: (i, j)),
        out_shape=jax.ShapeDtypeStruct((M, N), out_dtype),
        scratch_shapes=[pltpu.VMEM((tm, tn), F32)],
        compiler_params=_cparams(("parallel", "parallel", "arbitrary")),
    )(a, b)


def matmul_nt_normbwd(dy, w, x, g, dres, name):
    T, N = dy.shape
    K = w.shape[0]
    tm, tn = _tile(T, 1024), _tile(N, 1536)
    nj = N // tn

    def body(dy_ref, w_ref, x_ref, g_ref, dres_ref, dx_ref, dg_ref, acc):
        i, j = pl.program_id(0), pl.program_id(1)
        _acc_out(acc, _dot_nt(dy_ref[...], w_ref[...]), j == 0)

        @pl.when(j == nj - 1)
        def _():
            dx, dg = _rms_bwd(acc[...], x_ref[...], g_ref[...])
            dx_ref[...] = dres_ref[...] + dx
            _acc_out(dg_ref, dg, i == 0)

    return pl.pallas_call(
        body, name=name, grid=(T // tm, nj),
        in_specs=[pl.BlockSpec((tm, tn), lambda i, j: (i, j)),
                  pl.BlockSpec((K, tn), lambda i, j: (0, j)),
                  pl.BlockSpec((tm, K), lambda i, j: (i, 0)),
                  pl.BlockSpec((1, K), lambda i, j: (0, 0)),
                  pl.BlockSpec((tm, K), lambda i, j: (i, 0))],
        out_specs=[pl.BlockSpec((tm, K), lambda i, j: (i, 0)),
                   pl.BlockSpec((1, K), lambda i, j: (0, 0))],
        out_shape=[jax.ShapeDtypeStruct((T, K), F32), jax.ShapeDtypeStruct((1, K), F32)],
        scratch_shapes=[pltpu.VMEM((tm, K), F32)],
        compiler_params=_cparams(("arbitrary", "arbitrary")),
    )(dy, w, x, g, dres)


def loss_kernel(y, tgt):
    T, D = y.shape
    tm = _tile(T, 512)

    def body(y_ref, t_ref, dy_ref, acc_ref):
        e = y_ref[...] - t_ref[...]
        dy_ref[...] = e * (1.0 / D)
        _acc_out(acc_ref, jnp.sum(e * e, axis=0, keepdims=True), pl.program_id(0) == 0)

    return pl.pallas_call(
        body, name="loss", grid=(T // tm,),
        in_specs=[pl.BlockSpec((tm, D), lambda i: (i, 0)), pl.BlockSpec((tm, D), lambda i: (i, 0))],
        out_specs=[pl.BlockSpec((tm, D), lambda i: (i, 0)), pl.BlockSpec((1, D), lambda i: (0, 0))],
        out_shape=[jax.ShapeDtypeStruct((T, D), F32), jax.ShapeDtypeStruct((1, D), F32)],
        compiler_params=_cparams(("arbitrary",)),
    )(y, tgt)


def _rope_partner(x):
    lane = lax.broadcasted_iota(jnp.int32, (1, 128), 1)
    return jnp.where(lane < ROPE_LANE0 + ROPE_DIM // 2, pltpu.roll(x, 128 - ROPE_DIM // 2, 1),
                     pltpu.roll(x, ROPE_DIM // 2, 1))


def _rope_apply(x, cos, sin_signed):
    return x * cos + _rope_partner(x) * sin_signed


def _rope_apply_bwd(dy, cos, sin_signed):
    lane = lax.broadcasted_iota(jnp.int32, (1, 128), 1)
    rotary = (lane >= ROPE_LANE0) & (lane < ROPE_LANE0 + ROPE_DIM)
    return dy * cos + jnp.where(rotary, _rope_partner(dy * sin_signed), 0.0)


def rope_tables(T):
    pos = jnp.arange(T, dtype=F32)
    inv_freq = ROPE_THETA ** (-(jnp.arange(ROPE_DIM // 2, dtype=F32) * 2.0 / ROPE_DIM))
    ang = pos[:, None] * inv_freq[None, :]
    cos, sin = jnp.cos(ang), jnp.sin(ang)
    z = jnp.zeros((T, ROPE_LANE0), F32)
    z2 = jnp.zeros((T, 128 - ROPE_LANE0 - ROPE_DIM), F32)
    cosr = jnp.concatenate([z, cos, cos, z2], axis=1)
    sinr = jnp.concatenate([z, -sin, sin, z2], axis=1)
    return cosr, sinr


def mla_prep(proj, gq, gkv, wuq, wukv, cosr, sinr):
    T = proj.shape[0]
    tm = _tile(T, 512)

    def body(cq_ref, ckv_ref, misc_ref, gq_ref, gkv_ref, wuq_ref, wukv_ref, cos_ref, sin_ref,
             q_ref, k_ref, v_ref, cqT_ref, ckvT_ref):
        lane = lax.broadcasted_iota(jnp.int32, (1, 128), 1)
        cosr_, sinr_ = cos_ref[...], sin_ref[...]
        cosq = cosr_ + jnp.where(lane < ROPE_LANE0, 1.0, 0.0)
        cqn = _rms_fwd(cq_ref[...], gq_ref[...])
        cqT_ref[...] = cqn.T.astype(MXU)
        qm = _dot(cqn, wuq_ref[...])
        q_ref[...] = jnp.concatenate(
            [_rope_apply(qm[:, 128 * h:128 * (h + 1)], cosq, sinr_) for h in range(4)], axis=1).astype(MXU)
        ckvn = _rms_fwd(ckv_ref[...], gkv_ref[...])
        ckvT_ref[...] = ckvn.T.astype(MXU)
        kv = _dot(ckvn, wukv_ref[...])
        kr = _rope_apply(misc_ref[...], cosr_, sinr_)
        k_ref[...] = jnp.concatenate(
            [kv[:, 128 * h:128 * (h + 1)] + kr for h in range(4)], axis=1).astype(MXU)
        v_ref[...] = kv[:, 512:768].astype(MXU)

    row = lambda i: (i, 0)
    const = lambda i: (0, 0)
    return pl.pallas_call(
        body, name="mla_prep", grid=(T // tm,),
        in_specs=[pl.BlockSpec((tm, 256), lambda i: (i, C_CQ // 256)),
                  pl.BlockSpec((tm, 128), lambda i: (i, C_CKV // 128)),
                  pl.BlockSpec((tm, 128), lambda i: (i, C_MISC // 128)),
                  pl.BlockSpec((1, 256), const), pl.BlockSpec((1, 128), const),
                  pl.BlockSpec((256, 512), const), pl.BlockSpec((128, 768), const),
                  pl.BlockSpec((tm, 128), row), pl.BlockSpec((tm, 128), row)],
        out_specs=[pl.BlockSpec((tm, 512), row), pl.BlockSpec((tm, 512), row), pl.BlockSpec((tm, 256), row),
                   pl.BlockSpec((256, tm), lambda i: (0, i)), pl.BlockSpec((128, tm), lambda i: (0, i))],
        out_shape=[jax.ShapeDtypeStruct((T, 512), MXU), jax.ShapeDtypeStruct((T, 512), MXU),
                   jax.ShapeDtypeStruct((T, 256), MXU),
                   jax.ShapeDtypeStruct((256, T), MXU), jax.ShapeDtypeStruct((128, T), MXU)],
        compiler_params=_cparams(("parallel",)),
    )(proj, proj, proj, gq, gkv, wuq, wukv, cosr, sinr)


def mla_prep_bwd(dq, dk, dv, proj, gq, gkv, wuq, wukv, cosr, sinr):
    T = proj.shape[0]
    tm = _tile(T, 512)

    def body(dq_ref, dk_ref, dv_ref, cq_ref, ckv_ref, gq_ref, gkv_ref, wuq_ref, wukv_ref, cos_ref, sin_ref,
             dqm_ref, dkv_ref, dcq_ref, dckv_ref, dmisc_ref, dgq_ref, dgkv_ref):
        first = pl.program_id(0) == 0
        lane = lax.broadcasted_iota(jnp.int32, (1, 128), 1)
        cosr_, sinr_ = cos_ref[...], sin_ref[...]
        cosq = cosr_ + jnp.where(lane < ROPE_LANE0, 1.0, 0.0)
        dqv = dq_ref[...]
        dqm = jnp.concatenate(
            [_rope_apply_bwd(dqv[:, 128 * h:128 * (h + 1)], cosq, sinr_) for h in range(4)], axis=1)
        dqm_ref[...] = dqm.astype(MXU)
        dcq, dgq = _rms_bwd(_dot_nt(dqm, wuq_ref[...]), cq_ref[...], gq_ref[...])
        dcq_ref[...] = dcq
        _acc_out(dgq_ref, dgq, first)
        dkv_ = dk_ref[...]
        heads = [dkv_[:, 128 * h:128 * (h + 1)] for h in range(4)]
        dkr = heads[0] + heads[1] + heads[2] + heads[3]
        dmisc_ref[...] = _rope_apply_bwd(dkr, cosr_, sinr_)
        dkvm = jnp.concatenate([jnp.where(lane < ROPE_LANE0, hd, 0.0) for hd in heads] + [dv_ref[...]], axis=1)
        dkv_ref[...] = dkvm.astype(MXU)
        dckv, dgkv = _rms_bwd(_dot_nt(dkvm, wukv_ref[...]), ckv_ref[...], gkv_ref[...])
        dckv_ref[...] = dckv
        _acc_out(dgkv_ref, dgkv, first)

    row = lambda i: (i, 0)
    const = lambda i: (0, 0)
    return pl.pallas_call(
        body, name="mla_prep_bwd", grid=(T // tm,),
        in_specs=[pl.BlockSpec((tm, 512), row), pl.BlockSpec((tm, 512), row), pl.BlockSpec((tm, 256), row),
                  pl.BlockSpec((tm, 256), lambda i: (i, C_CQ // 256)),
                  pl.BlockSpec((tm, 128), lambda i: (i, C_CKV // 128)),
                  pl.BlockSpec((1, 256), const), pl.BlockSpec((1, 128), const),
                  pl.BlockSpec((256, 512), const), pl.BlockSpec((128, 768), const),
                  pl.BlockSpec((tm, 128), row), pl.BlockSpec((tm, 128), row)],
        out_specs=[pl.BlockSpec((tm, 512), row), pl.BlockSpec((tm, 768), row), pl.BlockSpec((tm, 256), row),
                   pl.BlockSpec((tm, 128), row), pl.BlockSpec((tm, 128), row),
                   pl.BlockSpec((1, 256), const), pl.BlockSpec((1, 128), const)],
        out_shape=[jax.ShapeDtypeStruct((T, 512), MXU), jax.ShapeDtypeStruct((T, 768), MXU),
                   jax.ShapeDtypeStruct((T, 256), F32), jax.ShapeDtypeStruct((T, 128), F32),
                   jax.ShapeDtypeStruct((T, 128), F32),
                   jax.ShapeDtypeStruct((1, 256), F32), jax.ShapeDtypeStruct((1, 128), F32)],
        compiler_params=_cparams(("arbitrary",)),
    )(dq, dk, dv, proj, proj, gq, gkv, wuq, wukv, cosr, sinr)


def _split3(x):
    hi = x.astype(MXU)
    r1 = x - hi.astype(F32)
    mid = r1.astype(MXU)
    lo = (r1 - mid.astype(F32)).astype(MXU)
    return hi, mid, lo


def _tri_matmul(tri, x):
    hi, mid, lo = _split3(x)
    d = lambda p: jnp.dot(tri, p, preferred_element_type=F32)
    return d(hi) + d(mid) + d(lo)


def _log_sigmoid(z):
    return jnp.minimum(z, 0.0) - jnp.log(1.0 + jnp.exp(-jnp.abs(z)))


def fox_gate(proj, fbias):
    T = proj.shape[0]
    tb = _tile(T, 512)

    def body(misc_ref, b_ref, fc_ref, fr_ref, frep_ref, carry):
        @pl.when(pl.program_id(0) == 0)
        def _():
            carry[...] = jnp.zeros_like(carry)

        lane = lax.broadcasted_iota(jnp.int32, (1, 128), 1)
        lf = jnp.where(lane < 4, _log_sigmoid(misc_ref[...] + b_ref[...]), 0.0)
        r = lax.broadcasted_iota(jnp.int32, (tb, tb), 0)
        c = lax.broadcasted_iota(jnp.int32, (tb, tb), 1)
        tri = jnp.where(r >= c, 1.0, 0.0).astype(MXU)
        F = _tri_matmul(tri, lf) + carry[...]
        carry[...] = carry[...] + jnp.sum(lf, axis=0, keepdims=True)
        fc_ref[0] = F
        fc_ref[1] = pltpu.roll(F, 126, 1)
        ft = F.T[0:8, :]
        fr_ref[0] = ft
        fr_ref[1] = pltpu.roll(ft, 6, 0)
        for h in range(4):
            frep_ref[h] = jnp.broadcast_to(_lane_pick(F, h), (tb, 128))

    return pl.pallas_call(
        body, name="fox_gate", grid=(T // tb,),
        in_specs=[pl.BlockSpec((tb, 128), lambda i: (i, C_MISC // 128)), pl.BlockSpec((1, 128), lambda i: (0, 0))],
        out_specs=[pl.BlockSpec((2, tb, 128), lambda i: (0, i, 0)), pl.BlockSpec((2, 8, tb), lambda i: (0, 0, i)),
                   pl.BlockSpec((4, tb, 128), lambda i: (0, i, 0))],
        out_shape=[jax.ShapeDtypeStruct((2, T, 128), F32), jax.ShapeDtypeStruct((2, 8, T), F32),
                   jax.ShapeDtypeStruct((4, T, 128), F32)],
        scratch_shapes=[pltpu.VMEM((1, 128), F32)],
        compiler_params=_cparams(("arbitrary",)),
    )(proj, fbias)


def fox_gate_bwd(dFq, dFk, proj, fbias):
    T = proj.shape[0]
    tb = _tile(T, 512)
    nb = T // tb

    def body(dq_ref, dk_ref, misc_ref, b_ref, dm_ref, db_ref, carry):
        first = pl.program_id(0) == 0

        @pl.when(first)
        def _():
            carry[...] = jnp.zeros_like(carry)

        lane = lax.broadcasted_iota(jnp.int32, (1, 128), 1)
        dF = jnp.where(lane < 4, (dq_ref[0] + dk_ref[0]) + pltpu.roll(dq_ref[1] + dk_ref[1], 2, 1), 0.0)
        r = lax.broadcasted_iota(jnp.int32, (tb, tb), 0)
        c = lax.broadcasted_iota(jnp.int32, (tb, tb), 1)
        tri = jnp.where(r <= c, 1.0, 0.0).astype(MXU)
        dlf = _tri_matmul(tri, dF) + carry[...]
        carry[...] = carry[...] + jnp.sum(dF, axis=0, keepdims=True)
        z = misc_ref[...] + b_ref[...]
        dz = jnp.where(lane < 4, dlf * (1.0 / (1.0 + jnp.exp(z))), 0.0)
        dm_ref[...] = dz
        _acc_out(db_ref, jnp.sum(dz, axis=0, keepdims=True), first)

    return pl.pallas_call(
        body, name="fox_gate_bwd", grid=(nb,),
        in_specs=[pl.BlockSpec((2, tb, 128), lambda i: (0, nb - 1 - i, 0)),
                  pl.BlockSpec((2, tb, 128), lambda i: (0, nb - 1 - i, 0)),
                  pl.BlockSpec((tb, 128), lambda i: (nb - 1 - i, C_MISC // 128)),
                  pl.BlockSpec((1, 128), lambda i: (0, 0))],
        out_specs=[pl.BlockSpec((tb, 128), lambda i: (nb - 1 - i, 0)), pl.BlockSpec((1, 128), lambda i: (0, 0))],
        out_shape=[jax.ShapeDtypeStruct((T, 128), F32), jax.ShapeDtypeStruct((1, 128), F32)],
        scratch_shapes=[pltpu.VMEM((1, 128), F32)],
        compiler_params=_cparams(("arbitrary",)),
    )(dFq, dFk, proj, fbias)


FLASH_TILE = 512


def _row_stat_tile(a, b, n):
    at = jnp.broadcast_to(a, (n, 128)).T[0:8, :]
    bt = jnp.broadcast_to(b, (n, 128)).T[0:8, :]
    sub = lax.broadcasted_iota(jnp.int32, (8, 1), 0)
    return jnp.where(sub == 0, at, jnp.where(sub == 1, bt, 0.0))


def _col_stat_tile(a, b):
    lane = lax.broadcasted_iota(jnp.int32, (1, 128), 1)
    return jnp.where(lane == 0, a, jnp.where(lane == 1, b, 0.0))


def _lane_pick(x, h):
    lane = lax.broadcasted_iota(jnp.int32, (1, 128), 1)
    return jnp.sum(jnp.where(lane == h, x, 0.0), axis=1, keepdims=True)


def _half_mask(h):
    lane = lax.broadcasted_iota(jnp.int32, (1, 128), 1)
    return (lane // HEAD_DIM) == h


def _call_hosting(body, name, grid, args, in_specs, out_specs, out_shape, scratch, exch):
    n_out = len(out_shape)
    if exch is not None:
        body, (xargs, xin, xout, xshape, xscratch) = hosted_exchange(
            body, len(args), n_out, len(scratch), grid, *exch)
        args, in_specs, out_specs = args + xargs, in_specs + xin, out_specs + xout
        out_shape, scratch = out_shape + xshape, scratch + xscratch
    res = pl.pallas_call(
        body, name=name, grid=grid, in_specs=in_specs, out_specs=out_specs, out_shape=out_shape,
        scratch_shapes=scratch, compiler_params=_cparams(("arbitrary",) * len(grid)),
    )(*args)
    return res[:n_out], res[n_out:]


def flash_fwd(q, k, v, frep, frow, *, qblk, kblk, vblk, nq, scale, name, exch=None):
    T = q.shape[0]
    tk = _tile(T, FLASH_TILE)
    tq = _tile(T, 2 * FLASH_TILE)
    per_q = tq // tk
    wq = 128 * nq
    has_f = frep is not None

    def body(*refs):
        if has_f:
            q_ref, k_ref, v_ref, fk_ref, fr_ref, o_ref, lr_ref, vT_sc, m_sc, acc_sc = refs
        else:
            q_ref, k_ref, v_ref, o_ref, lr_ref, vT_sc, m_sc, acc_sc = refs
        i = pl.program_id(1)

        @pl.when(i == 0)
        def _():
            vT_sc[...] = v_ref[...].astype(F32).T.astype(MXU)

        key_row = lax.broadcasted_iota(jnp.int32, (tk, 1), 0)
        q_col = lax.broadcasted_iota(jnp.int32, (1, tq), 1)
        row_half = lax.broadcasted_iota(jnp.int32, (128, 1), 0) // HEAD_DIM
        qb = q_ref[...].astype(F32) * scale
        if nq == 1:
            qhs = [jnp.where(_half_mask(h), qb, 0).astype(MXU) for h in range(2)]
        else:
            qhs = [qb[:, 128 * h:128 * (h + 1)].astype(MXU) for h in range(2)]
        for h in range(2):
            m_sc[h] = jnp.full((1, tq), NEG, F32)
            acc_sc[h] = jnp.zeros((128, tq), F32)

        def make_step(diag_block):
            def step(j, carry):
                off = pl.multiple_of(j * tk, tk)
                ks = k_ref[pl.ds(off, tk), :]
                vT = vT_sc[:, pl.ds(off, tk)]
                for h in range(2):
                    kh = ks if nq == 1 else ks[:, 128 * h:128 * (h + 1)]
                    sT = _dot_nt(kh, qhs[h])
                    if has_f:
                        fk = fk_ref[h, pl.ds(off, tk), :]
                        sT = sT + (fr_ref[0, h:h + 1, :] - jnp.concatenate([fk] * (tq // 128), axis=1))
                    if diag_block is not None:
                        sT = jnp.where(key_row + diag_block * tk <= q_col, sT, NEG)
                    m_prev = m_sc[h]
                    m_new = jnp.maximum(m_prev, jnp.max(sT, axis=0, keepdims=True))
                    alpha = jnp.exp(m_prev - m_new)
                    pT = jnp.exp(sT - m_new)
                    vTh = jnp.where(row_half == h, vT, jnp.ones_like(vT))
                    acc_sc[h] = alpha * acc_sc[h] + _dot(vTh, pT)
                    m_sc[h] = m_new
                return carry
            return step

        lax.fori_loop(0, per_q * i, make_step(None), 0)
        for d in range(per_q):
            make_step(d)(per_q * i + d, 0)
        outs, lses = [], []
        for h in range(2):
            acc = acc_sc[h]
            outs.append(acc / pltpu.roll(acc, HEAD_DIM, 0))
            l = acc_sc[h, HEAD_DIM * (1 - h):HEAD_DIM * (1 - h) + 1, :]
            lses.append(m_sc[h] + jnp.log(l))
        o_ref[...] = jnp.where(row_half == 0, outs[0], outs[1]).T
        sub = lax.broadcasted_iota(jnp.int32, (8, 1), 0)
        lr_ref[0] = jnp.where(sub == 0, lses[0], jnp.where(sub == 1, lses[1], 0.0))

    in_specs = [pl.BlockSpec((tq, wq), lambda p, i: (i, qblk + p)),
                pl.BlockSpec((T, wq), lambda p, i: (0, kblk + p)),
                pl.BlockSpec((T, 128), lambda p, i: (0, vblk + p))]
    args = [q, k, v]
    if has_f:
        in_specs += [pl.BlockSpec((2, T, 128), lambda p, i: (p, 0, 0)),
                     pl.BlockSpec((1, 8, tq), lambda p, i: (p, 0, i))]
        args += [frep, frow]
    out_specs = [pl.BlockSpec((tq, 128), lambda p, i: (i, p)), pl.BlockSpec((1, 8, tq), lambda p, i: (p, 0, i))]
    out_shape = [jax.ShapeDtypeStruct((T, 256), F32), jax.ShapeDtypeStruct((2, 8, T), F32)]
    scratch = [pltpu.VMEM((128, T), MXU), pltpu.VMEM((2, 1, tq), F32), pltpu.VMEM((2, 128, tq), F32)]
    return _call_hosting(body, name, (2, T // tq), args, in_specs, out_specs, out_shape, scratch, exch)


def flash_bwd(q, k, v, do, o, lrow, fcol, frow, *, qblk, kblk, vblk, nq, scale, name, exch=None):
    T = q.shape[0]
    tq = tk = _tile(T, FLASH_TILE)
    wq = 128 * nq
    nqb = T // tq
    has_f = fcol is not None

    def body(*refs):
        if has_f:
            (q_ref, k_ref, v_ref, do_ref, o_ref, lr_ref, fc_ref, fr_ref,
             dq_ref, dk_ref, dv_ref, df_ref, dfq_ref, dk_sc, dv_sc, dqT_sc, d_sc, df_sc, dfq_sc) = refs
        else:
            q_ref, k_ref, v_ref, do_ref, o_ref, lr_ref, dq_ref, dk_ref, dv_ref, dk_sc, dv_sc, dqT_sc, d_sc = refs
        j = pl.program_id(1)
        diag = lax.broadcasted_iota(jnp.int32, (tk, 1), 0) <= lax.broadcasted_iota(jnp.int32, (1, tq), 1)
        hms = [_half_mask(h) for h in range(2)]

        @pl.when(j == 0)
        def _():
            dqT_sc[...] = jnp.zeros_like(dqT_sc)
            if has_f:
                dfq_sc[...] = jnp.zeros_like(dfq_sc)

            def delta(b, carry):
                off = pl.multiple_of(b * tq, tq)
                prod = do_ref[pl.ds(off, tq), :] * o_ref[pl.ds(off, tq), :]
                Ds = [jnp.sum(jnp.where(hms[h], prod, 0.0), axis=1, keepdims=True) for h in range(2)]
                d_sc[:, pl.ds(off, tq)] = _row_stat_tile(Ds[0], Ds[1], tq)
                return carry

            lax.fori_loop(0, nqb, delta, 0)

        kb = k_ref[...]
        vb = v_ref[...]
        if nq == 1:
            khs = [jnp.where(hms[h], kb, 0).astype(MXU) for h in range(2)]
        else:
            khs = [kb[:, 128 * h:128 * (h + 1)].astype(MXU) for h in range(2)]
        kTs = [kh.astype(F32).T.astype(MXU) for kh in khs]
        kss = [(kh.astype(F32) * scale).astype(MXU) for kh in khs]
        vhs = [jnp.where(hms[h], vb, 0).astype(MXU) for h in range(2)]
        fks = [_lane_pick(fc_ref[0], h) for h in range(2)] if has_f else None
        dv_sc[...] = jnp.zeros_like(dv_sc)
        dk_sc[...] = jnp.zeros_like(dk_sc)
        if has_f:
            df_sc[...] = jnp.zeros_like(df_sc)

        def make_step(masked):
            def step(i, carry):
                off = pl.multiple_of(i * tq, tq)
                qs = q_ref[pl.ds(off, tq), :]
                dos = do_ref[pl.ds(off, tq), :]
                for h in range(2):
                    qh = qs if nq == 1 else qs[:, 128 * h:128 * (h + 1)]
                    sT = _dot_nt(kss[h], qh)
                    if has_f:
                        sT = sT + (fr_ref[0, h:h + 1, pl.ds(off, tq)] - fks[h])
                    pT = jnp.exp(sT - lr_ref[0, h:h + 1, pl.ds(off, tq)])
                    if masked:
                        pT = jnp.where(diag, pT, 0.0)
                    dsT = pT * (_dot_nt(vhs[h], dos) - d_sc[h:h + 1, pl.ds(off, tq)])
                    dv_sc[...] += _dot(pT, jnp.where(hms[h], dos, 0))
                    qq = jnp.where(hms[h], qs, 0) if nq == 1 else qh
                    dk_sc[h if nq == 2 else 0] += _dot(dsT, qq)
                    dqT_sc[h if nq == 2 else 0, :, pl.ds(off, tq)] += _dot(kTs[h], dsT)
                    if has_f:
                        part = dsT[:, 0:128]
                        for c in range(1, tq // 128):
                            part = part + dsT[:, 128 * c:128 * (c + 1)]
                        df_sc[h] += part
                        dfq_sc[h:h + 1, pl.ds(off, tq)] += jnp.sum(dsT, axis=0, keepdims=True)
                return carry
            return step

        make_step(True)(j, 0)
        lax.fori_loop(j + 1, nqb, make_step(False), 0)
        if nq == 1:
            dk_ref[...] = dk_sc[0] * scale
        else:
            dk_ref[...] = jnp.concatenate([dk_sc[0], dk_sc[1]], axis=1) * scale
        dv_ref[...] = dv_sc[...]
        if has_f:
            df_ref[0] = _col_stat_tile(-jnp.sum(df_sc[0], axis=1, keepdims=True),
                                       -jnp.sum(df_sc[1], axis=1, keepdims=True))

        @pl.when(j == nqb - 1)
        def _():
            if nq == 1:
                dq_ref[...] = dqT_sc[0].T * scale
            else:
                dq_ref[...] = jnp.concatenate([dqT_sc[0].T, dqT_sc[1].T], axis=1) * scale
            if has_f:
                sub = lax.broadcasted_iota(jnp.int32, (128, 1), 0)
                rows = jnp.where(sub == 0, dfq_sc[0:1, :], jnp.where(sub == 1, dfq_sc[1:2, :], 0.0))
                dfq_ref[0] = rows.T

    in_specs = [pl.BlockSpec((T, wq), lambda p, j: (0, qblk + p)),
                pl.BlockSpec((tk, wq), lambda p, j: (j, kblk + p)),
                pl.BlockSpec((tk, 128), lambda p, j: (j, vblk + p)),
                pl.BlockSpec((T, 128), lambda p, j: (0, p)),
                pl.BlockSpec((T, 128), lambda p, j: (0, p)),
                pl.BlockSpec((1, 8, T), lambda p, j: (p, 0, 0))]
    args = [q, k, v, do, o, lrow]
    out_specs = [pl.BlockSpec((T, wq), lambda p, j: (0, p)),
                 pl.BlockSpec((tk, wq), lambda p, j: (j, p)), pl.BlockSpec((tk, 128), lambda p, j: (j, p))]
    out_shape = [jax.ShapeDtypeStruct((T, 2 * wq), F32), jax.ShapeDtypeStruct((T, 2 * wq), F32),
                 jax.ShapeDtypeStruct((T, 256), F32)]
    scratch = [pltpu.VMEM((nq, tk, 128), F32), pltpu.VMEM((tk, 128), F32), pltpu.VMEM((nq, 128, T), F32),
               pltpu.VMEM((8, T), F32)]
    if has_f:
        in_specs += [pl.BlockSpec((1, tk, 128), lambda p, j: (p, j, 0)),
                     pl.BlockSpec((1, 8, T), lambda p, j: (p, 0, 0))]
        args += [fcol, frow]
        out_specs += [pl.BlockSpec((1, tk, 128), lambda p, j: (p, j, 0)),
                      pl.BlockSpec((1, T, 128), lambda p, j: (p, 0, 0))]
        out_shape += [jax.ShapeDtypeStruct((2, T, 128), F32), jax.ShapeDtypeStruct((2, T, 128), F32)]
        scratch += [pltpu.VMEM((2, tk, 128), F32), pltpu.VMEM((8, T), F32)]
    return _call_hosting(body, name, (2, T // tk), args, in_specs, out_specs, out_shape, scratch, exch)


def _swa_align(pair, e, h):
    sel = jnp.where(_half_mask(e), pair, 0.0)
    if e == h:
        return sel
    return pltpu.roll(sel, HEAD_DIM, 1)


def _swa_mask(n):
    W = WINDOW
    qi = lax.broadcasted_iota(jnp.int32, (W, 2 * W), 0) + W
    kj = lax.broadcasted_iota(jnp.int32, (W, 2 * W), 1)
    dist = qi - kj
    return (dist >= 0) & (dist < W) & ((n > 0) | (kj >= W))


def swa_fwd(proj, bias, sinks, exch=None):
    T = proj.shape[0]
    W = WINDOW
    nb = T // W
    scale = HEAD_DIM ** -0.5

    def body(sink_ref, q_ref, kp_ref, kc_ref, vp_ref, vc_ref, b_ref, o_ref, l_ref):
        n = pl.program_id(0)
        mask = _swa_mask(n)
        kband = jnp.concatenate([kp_ref[...], kc_ref[...]], axis=0).astype(MXU)
        vband = jnp.concatenate([vp_ref[...], vc_ref[...]], axis=0).astype(MXU)
        lane = lax.broadcasted_iota(jnp.int32, (1, 128), 1)
        lse_tile = jnp.zeros((W, 128), F32)
        pairs = []
        for h in range(2):
            full = []
            for g in range(4):
                hq = 4 * h + g
                qa = _swa_align(q_ref[:, 128 * (hq // 2):128 * (hq // 2 + 1)], hq % 2, h)
                s = _dot_nt(qa, kband) * scale + b_ref[hq]
                s = jnp.where(mask, s, NEG)
                sink = sink_ref[hq]
                m = jnp.maximum(jnp.max(s, axis=1, keepdims=True), sink)
                e = jnp.exp(s - m)
                l = jnp.sum(e, axis=1, keepdims=True) + jnp.exp(sink - m)
                r = jnp.where(_half_mask(h), _dot(e, vband), 0.0) / l
                full.append(r + pltpu.roll(r, HEAD_DIM, 1))
                lse_tile = jnp.where(lane == hq, m + jnp.log(l), lse_tile)
            pairs.append(jnp.where(_half_mask(0), full[0], full[1]))
            pairs.append(jnp.where(_half_mask(0), full[2], full[3]))
        o_ref[...] = jnp.concatenate(pairs, axis=1)
        l_ref[...] = lse_tile

    prev = lambda n: (jnp.maximum(n - 1, 0), C_KA // 128)
    cur = lambda n: (n, C_KA // 128)
    prev_v = lambda n: (jnp.maximum(n - 1, 0), C_VA // 128)
    cur_v = lambda n: (n, C_VA // 128)
    return _call_hosting(
        body, "swa_fwd", (nb,), [sinks, proj, proj, proj, proj, proj, bias],
        [pl.BlockSpec(memory_space=pltpu.SMEM),
         pl.BlockSpec((W, 512), lambda n: (n, 0)),
         pl.BlockSpec((W, 128), prev), pl.BlockSpec((W, 128), cur),
         pl.BlockSpec((W, 128), prev_v), pl.BlockSpec((W, 128), cur_v),
         pl.BlockSpec((8, W, 2 * W), lambda n: (0, 0, 0))],
        [pl.BlockSpec((W, 512), lambda n: (n, 0)), pl.BlockSpec((W, 128), lambda n: (n, 0))],
        [jax.ShapeDtypeStruct((T, 512), F32), jax.ShapeDtypeStruct((T, 128), F32)], [], exch)


def swa_bwd(proj, bias, sinks, do, o, lse, exch=None):
    T = proj.shape[0]
    W = WINDOW
    nb = T // W
    scale = HEAD_DIM ** -0.5

    def body(sink_ref, q_ref, kp_ref, kc_ref, vp_ref, vc_ref, b_ref, do_ref, o_ref, l_ref,
             dq_ref, dk_ref, dv_ref, db_ref, dsk_ref, ck, cv):
        n = pl.program_id(0)

        @pl.when(n == 0)
        def _():
            ck[...] = jnp.zeros_like(ck)
            cv[...] = jnp.zeros_like(cv)
            db_ref[...] = jnp.zeros_like(db_ref)
            dsk_ref[...] = jnp.zeros_like(dsk_ref)

        @pl.when(n < nb)
        def _():
            mask = _swa_mask(n)
            kb32 = jnp.concatenate([kp_ref[...], kc_ref[...]], axis=0)
            vb32 = jnp.concatenate([vp_ref[...], vc_ref[...]], axis=0)
            kband = kb32.astype(MXU)
            sub = lax.broadcasted_iota(jnp.int32, (8, 1), 0)
            dk_band = jnp.zeros((2 * W, 128), F32)
            dv_band = jnp.zeros((2 * W, 128), F32)
            dsk = jnp.zeros((8, 128), F32)
            dq_pairs = []
            mask4 = jnp.concatenate([mask] * 4, axis=0)
            for h in range(2):
                hm = _half_mask(h)
                km = jnp.where(hm, kb32, 0.0).astype(MXU)
                vm = jnp.where(hm, vb32, 0.0).astype(MXU)
                pbs = [slice(128 * ((4 * h + g) // 2), 128 * ((4 * h + g) // 2 + 1)) for g in range(4)]
                q4 = jnp.concatenate([_swa_align(q_ref[:, pbs[g]], g % 2, h) for g in range(4)], axis=0)
                do4 = jnp.concatenate([_swa_align(do_ref[:, pbs[g]], g % 2, h) for g in range(4)], axis=0)
                D4 = jnp.concatenate(
                    [jnp.sum(jnp.where(_half_mask(g % 2), do_ref[:, pbs[g]] * o_ref[:, pbs[g]], 0.0), axis=1,
                             keepdims=True) for g in range(4)], axis=0)
                lse4 = jnp.concatenate([_lane_pick(l_ref[...], 4 * h + g) for g in range(4)], axis=0)
                sink4 = jnp.concatenate([jnp.full((W, 1), sink_ref[4 * h + g], F32) for g in range(4)], axis=0)
                s = _dot_nt(q4, kband) * scale + b_ref[4 * h:4 * h + 4].reshape(4 * W, 2 * W)
                p = jnp.where(mask4, jnp.exp(s - lse4), 0.0)
                sd = jnp.exp(sink4 - lse4) * D4
                for g in range(4):
                    dsk = dsk + jnp.where(sub == 4 * h + g,
                                          -jnp.sum(sd[W * g:W * (g + 1)], axis=0, keepdims=True), 0.0)
                ds = p * (_dot_nt(do4, vm) - D4)
                db_ref[4 * h:4 * h + 4] += ds.reshape(4, W, 2 * W)
                dq = _dot(ds, km) * scale
                dq = dq + pltpu.roll(dq, HEAD_DIM, 1)
                dk_band = dk_band + _dot(ds.T, q4) * scale
                dv_band = dv_band + _dot(p.T, do4)
                dq_pairs.append(jnp.where(_half_mask(0), dq[0:W], dq[W:2 * W]))
                dq_pairs.append(jnp.where(_half_mask(0), dq[2 * W:3 * W], dq[3 * W:4 * W]))
            dq_ref[...] = jnp.concatenate(dq_pairs, axis=1)
            dsk_ref[...] += dsk
            dk_ref[...] = ck[...] + dk_band[0:W]
            dv_ref[...] = cv[...] + dv_band[0:W]
            ck[...] = dk_band[W:2 * W]
            cv[...] = dv_band[W:2 * W]

        @pl.when(n == nb)
        def _():
            dk_ref[...] = ck[...]
            dv_ref[...] = cv[...]

    cl = lambda n: jnp.minimum(n, nb - 1)
    pv = lambda n: jnp.maximum(jnp.minimum(n, nb - 1) - 1, 0)
    return _call_hosting(
        body, "swa_bwd", (nb + 1,), [sinks, proj, proj, proj, proj, proj, bias, do, o, lse],
        [pl.BlockSpec(memory_space=pltpu.SMEM),
         pl.BlockSpec((W, 512), lambda n: (cl(n), 0)),
         pl.BlockSpec((W, 128), lambda n: (pv(n), C_KA // 128)),
         pl.BlockSpec((W, 128), lambda n: (cl(n), C_KA // 128)),
         pl.BlockSpec((W, 128), lambda n: (pv(n), C_VA // 128)),
         pl.BlockSpec((W, 128), lambda n: (cl(n), C_VA // 128)),
         pl.BlockSpec((8, W, 2 * W), lambda n: (0, 0, 0)),
         pl.BlockSpec((W, 512), lambda n: (cl(n), 0)),
         pl.BlockSpec((W, 512), lambda n: (cl(n), 0)),
         pl.BlockSpec((W, 128), lambda n: (cl(n), 0))],
        [pl.BlockSpec((W, 512), lambda n: (cl(n), 0)),
         pl.BlockSpec((W, 128), lambda n: (jnp.maximum(n - 1, 0), 0)),
         pl.BlockSpec((W, 128), lambda n: (jnp.maximum(n - 1, 0), 0)),
         pl.BlockSpec((8, W, 2 * W), lambda n: (0, 0, 0)),
         pl.BlockSpec((8, 128), lambda n: (0, 0))],
        [jax.ShapeDtypeStruct((T, 512), F32), jax.ShapeDtypeStruct((T, 128), F32),
         jax.ShapeDtypeStruct((T, 128), F32), jax.ShapeDtypeStruct((8, W, 2 * W), F32),
         jax.ShapeDtypeStruct((8, 128), F32)],
        [pltpu.VMEM((W, 128), F32), pltpu.VMEM((W, 128), F32)], exch)


def swa_bias_table(rel_bias):
    W = WINDOW
    qi = jnp.arange(W, dtype=jnp.int32)[:, None] + W
    kj = jnp.arange(2 * W, dtype=jnp.int32)[None, :]
    dist = qi - kj
    max_exact = REL_BUCKETS // 2
    d = jnp.maximum(dist, 0)
    log_ratio = jnp.log(jnp.maximum(d, 1).astype(F32) / max_exact) / math.log(REL_MAX_DIST / max_exact)
    large = jnp.minimum(max_exact + (log_ratio * (REL_BUCKETS - max_exact)).astype(jnp.int32), REL_BUCKETS - 1)
    bucket = jnp.where(d < max_exact, d, large)
    bucket = bucket.reshape(-1)
    onehot = (bucket[None, :] == jnp.arange(REL_BUCKETS, dtype=jnp.int32)[:, None]).astype(F32)
    bias = jnp.dot(rel_bias.astype(F32).T, onehot, precision=lax.Precision.HIGHEST)
    return bias.reshape(SWA_Q_HEADS, W, 2 * W), bucket


def attn_out(oa, ob, oc, gn, wout, gpost, x):
    T = x.shape[0]
    tm = _tile(T, 512)

    def body(oa_ref, ob_ref, oc_ref, gn_ref, w_ref, gp_ref, x_ref, x2_ref, y_ref, mT_ref):
        g = gn_ref[...]
        mixed = jnp.concatenate([_rms_fwd(oa_ref[...], g[:, 0:512]), _rms_fwd(ob_ref[...], g[:, 512:768]),
                                 _rms_fwd(oc_ref[...], g[:, 768:1024])], axis=1)
        mT_ref[...] = mixed.T.astype(MXU)
        y = _dot(mixed, w_ref[...])
        y_ref[...] = y
        x2_ref[...] = x_ref[...] + _rms_fwd(y, gp_ref[...])

    row = lambda i: (i, 0)
    const = lambda i: (0, 0)
    return pl.pallas_call(
        body, name="attn_out", grid=(T // tm,),
        in_specs=[pl.BlockSpec((tm, 512), row), pl.BlockSpec((tm, 256), row), pl.BlockSpec((tm, 256), row),
                  pl.BlockSpec((1, 1024), const), pl.BlockSpec((1024, 1024), const), pl.BlockSpec((1, 1024), const),
                  pl.BlockSpec((tm, 1024), row)],
        out_specs=[pl.BlockSpec((tm, 1024), row), pl.BlockSpec((tm, 1024), row),
                   pl.BlockSpec((1024, tm), lambda i: (0, i))],
        out_shape=[jax.ShapeDtypeStruct((T, 1024), F32), jax.ShapeDtypeStruct((T, 1024), F32),
                   jax.ShapeDtypeStruct((1024, T), MXU)],
        compiler_params=_cparams(("parallel",)),
    )(oa, ob, oc, gn, wout, gpost, x)


def attn_out_bwd(dx2, y, oa, ob, oc, gn, wout, gpost):
    T = dx2.shape[0]
    tm = _tile(T, 512)

    def body(dx_ref, y_ref, oa_ref, ob_ref, oc_ref, gn_ref, w_ref, gp_ref,
             dy_ref, da_ref, db_ref, dc_ref, dgn_ref, dgp_ref):
        first = pl.program_id(0) == 0
        dy, dgp = _rms_bwd(dx_ref[...], y_ref[...], gp_ref[...])
        dy_ref[...] = dy.astype(MXU)
        _acc_out(dgp_ref, dgp, first)
        dm = _dot_nt(dy, w_ref[...])
        g = gn_ref[...]
        da, dga = _rms_bwd(dm[:, 0:512], oa_ref[...], g[:, 0:512])
        db, dgb = _rms_bwd(dm[:, 512:768], ob_ref[...], g[:, 512:768])
        dc, dgc = _rms_bwd(dm[:, 768:1024], oc_ref[...], g[:, 768:1024])
        da_ref[...] = da
        db_ref[...] = db
        dc_ref[...] = dc
        _acc_out(dgn_ref, jnp.concatenate([dga, dgb, dgc], axis=1), first)

    row = lambda i: (i, 0)
    const = lambda i: (0, 0)
    return pl.pallas_call(
        body, name="attn_out_bwd", grid=(T // tm,),
        in_specs=[pl.BlockSpec((tm, 1024), row), pl.BlockSpec((tm, 1024), row),
                  pl.BlockSpec((tm, 512), row), pl.BlockSpec((tm, 256), row), pl.BlockSpec((tm, 256), row),
                  pl.BlockSpec((1, 1024), const), pl.BlockSpec((1024, 1024), const), pl.BlockSpec((1, 1024), const)],
        out_specs=[pl.BlockSpec((tm, 1024), row), pl.BlockSpec((tm, 512), row), pl.BlockSpec((tm, 256), row),
                   pl.BlockSpec((tm, 256), row), pl.BlockSpec((1, 1024), const), pl.BlockSpec((1, 1024), const)],
        out_shape=[jax.ShapeDtypeStruct((T, 1024), MXU), jax.ShapeDtypeStruct((T, 512), F32),
                   jax.ShapeDtypeStruct((T, 256), F32), jax.ShapeDtypeStruct((T, 256), F32),
                   jax.ShapeDtypeStruct((1, 1024), F32), jax.ShapeDtypeStruct((1, 1024), F32)],
        compiler_params=_cparams(("arbitrary",)),
    )(dx2, y, oa, ob, oc, gn, wout, gpost)


FF_TILE = 256
_GELU_C = math.sqrt(2.0 / math.pi)


def _gelu(x):
    return 0.5 * x * (1.0 + jnp.tanh(_GELU_C * (x + 0.044715 * x * x * x)))


def _gelu_with_grad(x):
    x2 = x * x
    t = jnp.tanh(_GELU_C * x * (1.0 + 0.044715 * x2))
    h = 0.5 * (1.0 + t)
    return x * h, h + (0.5 * _GELU_C) * x * (1.0 - t * t) * (1.0 + (3 * 0.044715) * x2)


def _conv_taps(u, hal_ref, first):
    row = lax.broadcasted_iota(jnp.int32, (8, 1), 0)
    h6 = jnp.where(first, 0.0, hal_ref[6:7, :])
    h7 = jnp.where(first, 0.0, hal_ref[7:8, :])
    r1, r2 = pltpu.roll(u, 1, 0), pltpu.roll(u, 2, 0)
    r1 = jnp.concatenate([jnp.where(row == 0, h7, r1[0:8]), r1[8:]], axis=0)
    r2 = jnp.concatenate([jnp.where(row == 0, h6, jnp.where(row == 1, h7, r2[0:8])), r2[8:]], axis=0)
    return r1, r2


def ffn_fwd(u0, convw, convb, wdown, gpost, x2, exch=None):
    T = x2.shape[0]
    tm, tn = _tile(T, 1024), FF_TILE
    nj = D_FF // tn

    def body(ug_ref, uu_ref, hg_ref, hu_ref, wg_ref, wu_ref, bg_ref, bu_ref, wd_ref, gp_ref, x_ref, wdp_ref,
             x3_ref, y_ref, aT_ref, acc, a_sc):
        i, j = pl.program_id(0), pl.program_id(1)
        first = i == 0

        @pl.when(j == 0)
        def _():
            acc[...] = jnp.zeros_like(acc)
            a_sc[...] = jnp.zeros_like(a_sc)

        acc[...] += _dot(a_sc[...], wdp_ref[...])

        def conv(u_ref, h_ref, w_ref, b_ref):
            u = u_ref[...]
            r1, r2 = _conv_taps(u, h_ref, first)
            return b_ref[...] + w_ref[0:1, :] * r2 + w_ref[1:2, :] * r1 + w_ref[2:3, :] * u

        a = _gelu(conv(ug_ref, hg_ref, wg_ref, bg_ref)) * conv(uu_ref, hu_ref, wu_ref, bu_ref)
        aT_ref[...] = a.T.astype(MXU)
        a_sc[...] = a.astype(MXU)

        @pl.when(j == nj - 1)
        def _():
            y = acc[...] + _dot(a_sc[...], wd_ref[...])
            y_ref[...] = y
            x3_ref[...] = x_ref[...] + _rms_fwd(y, gp_ref[...])

    halo = lambda off: (lambda i, j: (jnp.maximum(i * (tm // 8) - 1, 0), off + j))
    return _call_hosting(
        body, "ffn_fwd", (T // tm, nj), [u0, u0, u0, u0, convw, convw, convb, convb, wdown, gpost, x2, wdown],
        [pl.BlockSpec((tm, tn), lambda i, j: (i, j)), pl.BlockSpec((tm, tn), lambda i, j: (i, nj + j)),
         pl.BlockSpec((8, tn), halo(0)), pl.BlockSpec((8, tn), halo(nj)),
         pl.BlockSpec((3, tn), lambda i, j: (0, j)), pl.BlockSpec((3, tn), lambda i, j: (0, nj + j)),
         pl.BlockSpec((1, tn), lambda i, j: (0, j)), pl.BlockSpec((1, tn), lambda i, j: (0, nj + j)),
         pl.BlockSpec((tn, 1024), lambda i, j: (j, 0)),
         pl.BlockSpec((1, 1024), lambda i, j: (0, 0)),
         pl.BlockSpec((tm, 1024), lambda i, j: (i, 0)),
         pl.BlockSpec((tn, 1024), lambda i, j: (jnp.maximum(j - 1, 0), 0))],
        [pl.BlockSpec((tm, 1024), lambda i, j: (i, 0)), pl.BlockSpec((tm, 1024), lambda i, j: (i, 0)),
         pl.BlockSpec((tn, tm), lambda i, j: (j, i))],
        [jax.ShapeDtypeStruct((T, 1024), F32), jax.ShapeDtypeStruct((T, 1024), F32),
         jax.ShapeDtypeStruct((D_FF, T), MXU)],
        [pltpu.VMEM((tm, 1024), F32), pltpu.VMEM((tm, tn), MXU)], exch)


def ffn_bwd(dx3, y, u0, convw, convb, wdown, gpost, wupT, x2, gfpre, exch=None):
    T = dx3.shape[0]
    tm, tn = _tile(T, 512), FF_TILE
    nj = D_FF // tn
    ni = T // tm

    def body(dx_ref, y_ref, ug_ref, uu_ref, hg_ref, hu_ref, wg_ref, wu_ref, bg_ref, bu_ref, wd_ref, gp_ref,
             wtg_ref, wtu_ref, x2_ref, gf_ref, wdn_ref, wtgp_ref, wtup_ref,
             dy_ref, dug_ref, duu_ref, dcg_ref, dcu_ref, dgp_ref, dx2_ref, dgf_ref,
             dy_sc, dh_sc, da_sc, dug_sc, duu_sc, cg, cu, ag, au):
        s, j = pl.program_id(0), pl.program_id(1)
        i = ni - 1 - s
        first_tok = i == 0
        sub = lax.broadcasted_iota(jnp.int32, (8, 1), 0)
        slot = j % 2

        @pl.when(j == 0)
        def _():
            dy, dgp = _rms_bwd(dx_ref[...], y_ref[...], gp_ref[...])
            dy_sc[...] = dy.astype(MXU)
            dy_ref[...] = dy.astype(MXU)
            _acc_out(dgp_ref, dgp, s == 0)
            dh_sc[...] = jnp.zeros_like(dh_sc)
            da_sc[0] = _dot_nt(dy.astype(MXU), wd_ref[...])
            dug_sc[...] = jnp.zeros_like(dug_sc)
            duu_sc[...] = jnp.zeros_like(duu_sc)

        @pl.when(s == 0)
        def _():
            cg[j] = jnp.zeros((8, tn), F32)
            cu[j] = jnp.zeros((8, tn), F32)
            ag[j] = jnp.zeros((8, tn), F32)
            au[j] = jnp.zeros((8, tn), F32)

        da = da_sc[slot]
        da_sc[1 - slot] = _dot_nt(dy_sc[...], wdn_ref[...])
        dh_sc[...] += _dot(dug_sc[...], wtgp_ref[...]) + _dot(duu_sc[...], wtup_ref[...])

        def conv(u_ref, h_ref, w_ref, b_ref):
            u = u_ref[...]
            r1, r2 = _conv_taps(u, h_ref, first_tok)
            return b_ref[...] + w_ref[0:1, :] * r2 + w_ref[1:2, :] * r1 + w_ref[2:3, :] * u, u, r1, r2

        gate, ugv, g1, g2 = conv(ug_ref, hg_ref, wg_ref, bg_ref)
        up, uuv, u1, u2 = conv(uu_ref, hu_ref, wu_ref, bu_ref)
        gl, dgl = _gelu_with_grad(gate)
        dup = da * gl
        dgate = da * up * dgl

        def conv_bwd(du, u, r1, r2, w_ref, c_ref, a_ref, duT_ref, du_sc):
            nxt = c_ref[j]
            n0, n1 = nxt[0:1, :], nxt[1:2, :]
            f1, f2 = pltpu.roll(du, tm - 1, 0), pltpu.roll(du, tm - 2, 0)
            f1 = jnp.concatenate([f1[:tm - 8], jnp.where(sub == 7, n0, f1[tm - 8:])], axis=0)
            f2 = jnp.concatenate([f2[:tm - 8], jnp.where(sub == 7, n1, jnp.where(sub == 6, n0, f2[tm - 8:]))], axis=0)
            du0 = w_ref[2:3, :] * du + w_ref[1:2, :] * f1 + w_ref[0:1, :] * f2
            duT_ref[...] = du0.T.astype(MXU)
            du_sc[...] = du0.astype(MXU)
            c_ref[j] = du[0:8, :]
            red = lambda v: jnp.sum(v, axis=0, keepdims=True)
            part = jnp.where(sub == 0, red(du * r2), jnp.where(sub == 1, red(du * r1), jnp.where(
                sub == 2, red(du * u), jnp.where(sub == 3, red(du), 0.0))))
            a_ref[j] = a_ref[j] + part
            return a_ref[j]

        dcg_ref[0] = conv_bwd(dgate, ugv, g1, g2, wg_ref, cg, ag, dug_ref, dug_sc)
        dcu_ref[0] = conv_bwd(dup, uuv, u1, u2, wu_ref, cu, au, duu_ref, duu_sc)

        @pl.when(j == nj - 1)
        def _():
            dh = dh_sc[...] + _dot(dug_sc[...], wtg_ref[...]) + _dot(duu_sc[...], wtu_ref[...])
            dx, dgf = _rms_bwd(dh, x2_ref[...], gf_ref[...])
            dx2_ref[...] = dx_ref[...] + dx
            _acc_out(dgf_ref, dgf, s == 0)

    rev = lambda s: ni - 1 - s
    halo = lambda off: (lambda s, j: (jnp.maximum(rev(s) * (tm // 8) - 1, 0), off + j))
    tok = pl.BlockSpec((tm, 1024), lambda s, j: (rev(s), 0))
    vec = pl.BlockSpec((1, 1024), lambda s, j: (0, 0))
    return _call_hosting(
        body, "ffn_bwd", (ni, nj),
        [dx3, y, u0, u0, u0, u0, convw, convw, convb, convb, wdown, gpost, wupT, wupT, x2, gfpre,
         wdown, wupT, wupT],
        [tok, tok,
         pl.BlockSpec((tm, tn), lambda s, j: (rev(s), j)), pl.BlockSpec((tm, tn), lambda s, j: (rev(s), nj + j)),
         pl.BlockSpec((8, tn), halo(0)), pl.BlockSpec((8, tn), halo(nj)),
         pl.BlockSpec((3, tn), lambda s, j: (0, j)), pl.BlockSpec((3, tn), lambda s, j: (0, nj + j)),
         pl.BlockSpec((1, tn), lambda s, j: (0, j)), pl.BlockSpec((1, tn), lambda s, j: (0, nj + j)),
         pl.BlockSpec((tn, 1024), lambda s, j: (j, 0)), vec,
         pl.BlockSpec((tn, 1024), lambda s, j: (j, 0)), pl.BlockSpec((tn, 1024), lambda s, j: (nj + j, 0)),
         tok, vec,
         pl.BlockSpec((tn, 1024), lambda s, j: (jnp.minimum(j + 1, nj - 1), 0)),
         pl.BlockSpec((tn, 1024), lambda s, j: (jnp.maximum(j - 1, 0), 0)),
         pl.BlockSpec((tn, 1024), lambda s, j: (nj + jnp.maximum(j - 1, 0), 0))],
        [tok,
         pl.BlockSpec((tn, tm), lambda s, j: (j, rev(s))), pl.BlockSpec((tn, tm), lambda s, j: (j, rev(s))),
         pl.BlockSpec((1, 8, tn), lambda s, j: (s, 0, j)), pl.BlockSpec((1, 8, tn), lambda s, j: (s, 0, j)),
         vec, tok, vec],
        [jax.ShapeDtypeStruct((T, 1024), MXU), jax.ShapeDtypeStruct((D_FF, T), MXU),
         jax.ShapeDtypeStruct((D_FF, T), MXU),
         jax.ShapeDtypeStruct((ni, 8, D_FF), F32), jax.ShapeDtypeStruct((ni, 8, D_FF), F32),
         jax.ShapeDtypeStruct((1, 1024), F32), jax.ShapeDtypeStruct((T, 1024), F32),
         jax.ShapeDtypeStruct((1, 1024), F32)],
        [pltpu.VMEM((tm, 1024), MXU), pltpu.VMEM((tm, 1024), F32), pltpu.VMEM((2, tm, tn), F32),
         pltpu.VMEM((tm, tn), MXU), pltpu.VMEM((tm, tn), MXU)] + [pltpu.VMEM((nj, 8, tn), F32)] * 4, exch)


ELEMS_PER_BLOCK = 512 * 1024


def _row_block(R, C):
    if R * C <= ELEMS_PER_BLOCK or R % 8:
        return R
    best = 8
    for t in range(8, R + 1, 8):
        if R % t == 0 and t * C <= ELEMS_PER_BLOCK:
            best = t
    return best


def adamw(w, g, m, v, name, exch=None):
    L, R, C = w.shape
    partials = isinstance(g, (list, tuple))
    tr = _row_block(R, 2 * C)
    c1 = 1.0 - ADAM_B1 ** ADAM_STEP
    c2 = 1.0 - ADAM_B2 ** ADAM_STEP

    def body(w_ref, *rest):
        g_refs, (m_ref, v_ref, g_out, d_ref, nm_ref, nv_ref) = rest[:-6], rest[-6:]

        def step(gv):
            g_out[0] = gv
            nm = ADAM_B1 * m_ref[0] + (1.0 - ADAM_B1) * gv
            nv = ADAM_B2 * v_ref[0] + (1.0 - ADAM_B2) * (gv * gv)
            nm_ref[0] = nm
            nv_ref[0] = nv
            d_ref[0] = -ADAM_LR * ((nm / c1) / (jnp.sqrt(nv / c2) + ADAM_EPS) + ADAM_WD * w_ref[0])

        if not partials:
            step(g_refs[0][0])
            return
        for k in range(L):
            @pl.when(pl.program_id(0) == k)
            def _(k=k):
                gv = g_refs[k][0].astype(F32)
                for d in range(1, N_DEV):
                    gv = gv + g_refs[k][d].astype(F32)
                step(gv)

    spec = pl.BlockSpec((1, tr, C), lambda l, i: (l, i, 0))
    if partials:
        gspecs = [pl.BlockSpec((N_DEV, tr, C), lambda l, i, k=k: (0, jnp.where(l == k, i, 0), 0)) for k in range(L)]
        gs = list(g)
    else:
        gspecs, gs = [spec], [g]
    return _call_hosting(body, name, (L, R // tr), [w] + gs + [m, v], [spec] + gspecs + [spec, spec], [spec] * 4,
                         [jax.ShapeDtypeStruct((L, R, C), F32)] * 4, [], exch)


def sum_devices(buf, name):
    _, R, C = buf.shape
    tr = _row_block(R, C * 4)

    def body(b_ref, o_ref):
        acc = b_ref[0].astype(F32)
        for d in range(1, N_DEV):
            acc = acc + b_ref[d].astype(F32)
        o_ref[...] = acc

    return pl.pallas_call(
        body, name=name, grid=(R // tr,),
        in_specs=[pl.BlockSpec((N_DEV, tr, C), lambda i: (0, i, 0))],
        out_specs=pl.BlockSpec((tr, C), lambda i: (i, 0)),
        out_shape=jax.ShapeDtypeStruct((R, C), F32),
        compiler_params=_cparams(("parallel",)),
    )(buf)


def _exchange_copies(src_refs, out_refs, send_sems, recv_sems, gather):
    x, y, c = lax.axis_index("x"), lax.axis_index("y"), lax.axis_index("c")
    me = 4 * x + 2 * y + c
    flip = lambda a, bit: 1 - a if bit else a
    part = lambda ref, d: ref if gather else ref.at[d]
    copies = []
    for k in range(1, N_DEV):
        px, py, pc = flip(x, (k >> 2) & 1), flip(y, (k >> 1) & 1), flip(c, k & 1)
        peer = 4 * px + 2 * py + pc
        for t in range(len(src_refs)):
            sem = t * (N_DEV - 1) + k - 1
            mk = lambda s, d: pltpu.make_async_remote_copy(
                src_ref=s, dst_ref=d, send_sem=send_sems.at[sem], recv_sem=recv_sems.at[sem],
                device_id=(px, py, pc), device_id_type=pl.DeviceIdType.MESH)
            copies.append((mk(part(src_refs[t], peer), out_refs[t].at[me]),
                           mk(part(src_refs[t], me), out_refs[t].at[peer])))
    return me, copies


def exchange(srcs, name, gather):
    n = len(srcs)
    shapes = [(N_DEV,) + s.shape if gather else s.shape for s in srcs]

    def body(*refs):
        src_refs, out_refs = refs[:n], refs[n:2 * n]
        send_sems, recv_sems, local_sems = refs[2 * n:]
        me, copies = _exchange_copies(src_refs, out_refs, send_sems, recv_sems, gather)
        for outgoing, _ in copies:
            outgoing.start()
        mine = [pltpu.make_async_copy(src_refs[t] if gather else src_refs[t].at[me], out_refs[t].at[me],
                                      local_sems.at[t]) for t in range(n)]
        for cp in mine:
            cp.start()
        for _, incoming in copies:
            incoming.wait_recv()
        for outgoing, _ in copies:
            outgoing.wait_send()
        for cp in mine:
            cp.wait()

    return pl.pallas_call(
        body, name=name,
        in_specs=[pl.BlockSpec(memory_space=pl.ANY)] * n, out_specs=[pl.BlockSpec(memory_space=pl.ANY)] * n,
        out_shape=[jax.ShapeDtypeStruct(shp, s.dtype) for shp, s in zip(shapes, srcs)],
        scratch_shapes=[pltpu.SemaphoreType.DMA((n * (N_DEV - 1),)), pltpu.SemaphoreType.DMA((n * (N_DEV - 1),)),
                        pltpu.SemaphoreType.DMA((n,))],
    )(*srcs)


def hosted_exchange(body, n_in, n_out, n_scratch, grid, srcs, gather):
    n = len(srcs)
    shapes = [(N_DEV,) + s.shape if gather else s.shape for s in srcs]

    def wrapped(*refs):
        ins, xin = refs[:n_in], refs[n_in:n_in + n]
        outs = refs[n_in + n:n_in + n + n_out]
        xout = refs[n_in + n + n_out:n_in + 2 * n + n_out]
        rest = refs[n_in + 2 * n + n_out:]
        scratch, (send_sems, recv_sems, local_sems) = rest[:n_scratch], rest[n_scratch:]
        ids = [pl.program_id(a) for a in range(len(grid))]
        first = functools.reduce(jnp.logical_and, [i == 0 for i in ids])
        last = functools.reduce(jnp.logical_and, [i == g - 1 for i, g in zip(ids, grid)])
        me, copies = _exchange_copies(xin, xout, send_sems, recv_sems, gather)
        mine = [pltpu.make_async_copy(xin[t] if gather else xin[t].at[me], xout[t].at[me], local_sems.at[t])
                for t in range(n)]

        @pl.when(first)
        def _():
            for outgoing, _ in copies:
                outgoing.start()
            for cp in mine:
                cp.start()

        body(*ins, *outs, *scratch)

        @pl.when(last)
        def _():
            for _, incoming in copies:
                incoming.wait_recv()
            for outgoing, _ in copies:
                outgoing.wait_send()
            for cp in mine:
                cp.wait()

    any_spec = pl.BlockSpec(memory_space=pl.ANY)
    return wrapped, (list(srcs), [any_spec] * n, [any_spec] * n,
                     [jax.ShapeDtypeStruct(shp, s.dtype) for shp, s in zip(shapes, srcs)],
                     [pltpu.SemaphoreType.DMA((n * (N_DEV - 1),)), pltpu.SemaphoreType.DMA((n * (N_DEV - 1),)),
                      pltpu.SemaphoreType.DMA((n,))])


def _pack(parts, cols, row_align, dtype):
    flat = jnp.concatenate([p.astype(dtype) for p in parts], axis=-1)
    n = flat.shape[-1]
    block = cols * row_align
    total = -(-n // block) * block
    flat = jnp.pad(flat, [(0, 0)] * (flat.ndim - 1) + [(0, total - n)])
    return flat.reshape(flat.shape[:-1] + (total // cols, cols))


def _unpack(buf, shapes):
    lead = buf.shape[:-2]
    flat = buf.reshape(lead + (-1,))
    out, off = [], 0
    for s in shapes:
        n = int(np.prod(s))
        out.append(flat[..., off:off + n].reshape(lead + tuple(s)))
        off += n
    return out


SHARD_SHAPES = [(128, IN_COLS), (256, 48), (128, 64), (128, 1024), (1024, 704), (352, 1024)]
SHARDED = ["w_in", "w_uq", "w_ukv", "w_out", "w_up", "w_down"]
ATTN_SENT = ["w_in_p", "w_uq", "w_ukv", "w_out"]
UP_HALF = 352
FFN_SIDE = ["w_upT", "conv_w", "w_down"]


def _full_from_shards(name, s):
    if name in ("w_in", "w_in_p", "w_out", "w_down", "w_upT"):
        return s.reshape((-1, s.shape[-1]))
    return s.transpose(1, 0, 2).reshape((s.shape[1], -1))


def _shards_from_full(name, f):
    if name in ("w_in", "w_in_p", "w_out", "w_down", "w_upT"):
        return f.reshape((N_DEV, -1, f.shape[-1]))
    return f.reshape((f.shape[0], N_DEV, -1)).transpose(1, 0, 2)


def _perm_w_in(w):
    z = lambda n: jnp.zeros(w.shape[:-1] + (n,), w.dtype)
    return jnp.concatenate([w[..., :1536], w[..., 1540:1924], w[..., 1536:1540], z(60), w[..., 1924:1956], z(32)],
                           axis=-1)


def _unperm_w_in(d):
    return jnp.concatenate([d[..., :1536], d[..., 1920:1924], d[..., 1536:1920], d[..., 1984:2016]], axis=-1)


def _perm_w_uq(w):
    return jnp.pad(w.reshape(256, 4, MLA_QK_DIM), ((0, 0), (0, 0), (0, 128 - MLA_QK_DIM))).reshape(256, 512)


def _unperm_w_uq(d):
    return d.reshape(256, 4, 128)[:, :, :MLA_QK_DIM].reshape(256, 4 * MLA_QK_DIM)


def _perm_w_ukv(w):
    w4 = w.reshape(128, 4, 128)
    k = jnp.pad(w4[:, :, :64], ((0, 0), (0, 0), (0, 64))).reshape(128, 512)
    return jnp.concatenate([k, w4[:, :, 64:].reshape(128, 256)], axis=1)


def _unperm_w_ukv(d):
    dk = d[:, :512].reshape(128, 4, 128)[:, :, :64]
    dv = d[:, 512:].reshape(128, 4, 64)
    return jnp.concatenate([dk, dv], axis=-1).reshape(128, 512)


def _row(v, width=None):
    v = v.reshape(1, -1).astype(F32)
    if width is not None and v.shape[1] < width:
        v = jnp.pad(v, ((0, 0), (0, width - v.shape[1])))
    return v


def _layer_fwd(x, P, shared, send=None, ffn_from=None):
    cosr, sinr, bias = shared
    ex = lambda host: (send[host], True) if send is not None and send.get(host) else None
    proj, hT, projb = norm_matmul(x, P["g_pre"], P["w_in_p"], "in_proj", lo_tiles=C_CQ // 512)
    qm, km, vm, cqT, ckvT = mla_prep(proj, P["gq"], P["gkv"], P["w_uq_p"], P["w_ukv_p"], cosr, sinr)
    fcol, frow, frep = fox_gate(proj, P["fbias"])
    (oa, lse_a), got_swa = swa_fwd(proj, bias, P["sinks"], exch=ex("swa"))
    (ob, lrb), got_fox = flash_fwd(projb, projb, projb, frep, frow, qblk=C_QF // 128, kblk=C_KF // 128,
                                   vblk=C_VF // 128, nq=1, scale=HEAD_DIM ** -0.5, name="fox_fwd", exch=ex("fox"))
    (oc, lrc), got_mla = flash_fwd(qm, km, vm, None, None, qblk=0, kblk=0, vblk=0, nq=2,
                                   scale=MLA_QK_DIM ** -0.5, name="mla_fwd", exch=ex("mla"))
    x2, y1, mT = attn_out(oa, ob, oc, P["gn"], P["w_out"], P["g_apost"], x)
    if ffn_from is not None:
        P = dict(P, **ffn_from(got_swa, got_fox, got_mla))
    u0, h2 = norm_matmul(x2, P["g_fpre"], P["w_upT"], "up_proj", tn_pref=1536, w_transposed=True,
                         h_transposed=False)
    (x3, y2, aT), got_ffn = ffn_fwd(u0, P["conv_w"], P["conv_b"], P["w_down"], P["g_fpost"], x2, exch=ex("ffn"))
    S = dict(x=x, proj=proj, projb=projb, hT=hT, qm=qm, km=km, vm=vm, cqT=cqT, ckvT=ckvT, fcol=fcol, frow=frow,
             oa=oa, lse_a=lse_a, ob=ob, lrb=lrb, oc=oc, lrc=lrc,
             x2=x2, y1=y1, mT=mT, u0=u0, h2=h2, y2=y2, aT=aT)
    return x3, S, P, got_ffn


def _layer_bwd(dx3, P, S, shared, send_attn=None):
    cosr, sinr, bias = shared
    proj = S["proj"]
    G = {}
    got = {}
    ex = lambda arrays: (arrays, False) if send_attn is not None and arrays else None
    (dy2, dugT, duuT, dcg, dcu, G["ffn_post_norm"], dx2, G["ffn_pre_norm"]), got["ffn"] = ffn_bwd(
        dx3, S["y2"], S["u0"], P["conv_w"], P["conv_b"], P["w_down"], P["g_fpost"], P["w_upT"], S["x2"], P["g_fpre"],
        exch=ex(send_attn))
    dconv = jnp.concatenate([dcg[-1], dcu[-1]], axis=1)
    G["conv_w"], G["conv_b"] = dconv[0:3], dconv[3]
    G["w_down"] = matmul_nn(S["aT"], dy2, "dw_down", MXU)
    G["w_upT"] = jnp.concatenate([matmul_nn(dugT, S["h2"], "dw_up_gate", MXU),
                                  matmul_nn(duuT, S["h2"], "dw_up_up", MXU)], axis=0)
    G["w_up"] = G["w_upT"].T
    dy1, doa, dob, doc, G["group_norm"], G["attn_post_norm"] = attn_out_bwd(
        dx2, S["y1"], S["oa"], S["ob"], S["oc"], P["gn"], P["w_out"], P["g_apost"])
    G["w_out"] = matmul_nn(S["mT"], dy1, "dw_out", MXU)
    up_slices = _shards_from_full("w_upT", G["w_upT"])
    (dqa, dka, dva, dbias, dsk), got["swa"] = swa_bwd(proj, bias, P["sinks"], doa, S["oa"], S["lse_a"],
                                                      exch=ex([_shards_from_full("w_down", G["w_down"])]))
    G["swa_sinks"] = dsk[:, 0]
    pb = S["projb"]
    (dqf, dkf, dvf, dFk, dFq), got["fox"] = flash_bwd(
        pb, pb, pb, dob, S["ob"], S["lrb"], S["fcol"], S["frow"], name="fox_bwd", qblk=C_QF // 128,
        kblk=C_KF // 128, vblk=C_VF // 128, nq=1, scale=HEAD_DIM ** -0.5, exch=ex([up_slices[:, :UP_HALF]]))
    dmisc_f, dfb = fox_gate_bwd(dFq, dFk, proj, P["fbias"])
    G["forget_bias"] = dfb[0, 0:4]
    (dqm_, dkm_, dvm_), got["mla"] = flash_bwd(
        S["qm"], S["km"], S["vm"], doc, S["oc"], S["lrc"], None, None, name="mla_bwd",
        qblk=0, kblk=0, vblk=0, nq=2, scale=MLA_QK_DIM ** -0.5, exch=ex([up_slices[:, UP_HALF:]]))
    dqm, dkv, dcq, dckv, dmisc_r, G["q_latent_norm"], G["kv_latent_norm"] = mla_prep_bwd(
        dqm_, dkm_, dvm_, proj, P["gq"], P["gkv"], P["w_uq_p"], P["w_ukv_p"], cosr, sinr)
    G["w_uq"] = _unperm_w_uq(matmul_nn(S["cqT"], dqm, "dw_uq", MXU))
    G["w_ukv"] = _unperm_w_ukv(matmul_nn(S["ckvT"], dkv, "dw_ukv", MXU))
    dproj = jnp.concatenate([dqa, dka, dva, dqf, dkf, dvf, dcq, dckv, dmisc_f + dmisc_r], axis=1).astype(MXU)
    G["w_in_p"] = matmul_nn(S["hT"], dproj, "dw_in", MXU)
    G["w_in"] = _unperm_w_in(G["w_in_p"])
    dx, G["attn_pre_norm"] = matmul_nt_normbwd(dproj, P["w_in_p"], S["x"], P["g_pre"], dx2, "in_bwd")
    return dx, G, dbias, got


def _layer_params(l, full, small):
    return dict(
        g_pre=_row(small["attn_pre_norm"][l]),
        w_in_p=full["w_in_p"] if "w_in_p" in full else _perm_w_in(full["w_in"]),
        gq=_row(small["q_latent_norm"][l]), gkv=_row(small["kv_latent_norm"][l]),
        w_uq_p=_perm_w_uq(full["w_uq"]), w_ukv_p=_perm_w_ukv(full["w_ukv"]),
        fbias=_row(small["forget_bias"][l], 128), sinks=small["swa_sinks"][l].astype(F32),
        gn=_row(small["group_norm"][l]), w_out=full["w_out"], g_apost=_row(small["attn_post_norm"][l]),
        g_fpre=_row(small["ffn_pre_norm"][l]), conv_b=_row(small["conv_b"][l]),
        g_fpost=_row(small["ffn_post_norm"][l]),
        **{n: full[n] for n in FFN_SIDE if n in full},
        **({"w_upT": full["w_up"].T} if "w_up" in full else {}))


def _rel_bias_grad(dbias, bucket):
    flat = dbias.reshape(SWA_Q_HEADS, -1)
    hi = flat.astype(MXU)
    lo = (flat - hi.astype(F32)).astype(MXU)
    onehot = (bucket[:, None] == jnp.arange(128, dtype=jnp.int32)[None, :]).astype(MXU)
    r = matmul_nn(jnp.concatenate([hi, lo], axis=0), onehot, "rel_bias_grad")
    return (r[0:8] + r[8:16])[:, :REL_BUCKETS].T


def local_step(x, tgt, fulls, small, comm=None):
    T = x.shape[0]
    cosr, sinr = rope_tables(T)
    bias, bucket = swa_bias_table(small["rel_bias"])
    shared = (cosr, sinr, bias)
    Ps, Ss = [], []
    h, full = x, fulls[0]
    for l in range(DEPTH):
        P = _layer_params(l, full, small)
        if comm:
            h, S, P, got = _layer_fwd(h, P, shared, comm["weight_parts"](l), comm["ffn_from"])
            full = comm["attn_from"](got) if l + 1 < DEPTH else None
        else:
            h, S, P, _ = _layer_fwd(h, P, shared)
            full = fulls[l + 1] if l + 1 < DEPTH else None
        Ps.append(P)
        Ss.append(S)
    dh, sq = loss_kernel(h, tgt)
    grads = [None] * DEPTH
    dbias_sum = None
    pending = [] if comm else None
    for l in reversed(range(DEPTH)):
        dh, grads[l], dbias, got = _layer_bwd(dh, Ps[l], Ss[l], shared, pending)
        dbias_sum = dbias if dbias_sum is None else dbias_sum + dbias
        if comm:
            comm["landed"](l, ["w_down"], got["swa"])
            comm["landed"](l, ["w_upT"], [jnp.concatenate([got["fox"][0], got["mla"][0]], axis=1)])
            if pending:
                comm["landed"](l + 1, ATTN_SENT, got["ffn"])
            pending = [_shards_from_full(n, grads[l][n]) for n in ATTN_SENT]
    return sq, dh, grads, _rel_bias_grad(dbias_sum, bucket), pending


WEIGHTS = ['attn_pre_norm', 'w_in', 'forget_bias', 'swa_sinks', 'rel_bias', 'q_latent_norm', 'w_uq',
           'kv_latent_norm', 'w_ukv', 'group_norm', 'w_out', 'attn_post_norm', 'ffn_pre_norm', 'w_up', 'conv_w',
           'conv_b', 'w_down', 'ffn_post_norm']
SMALL_PER_LAYER = ['attn_pre_norm', 'forget_bias', 'swa_sinks', 'q_latent_norm', 'kv_latent_norm', 'group_norm',
                   'attn_post_norm', 'ffn_pre_norm', 'conv_b', 'ffn_post_norm', 'conv_w']


def kernel(x, attn_pre_norm, w_in, forget_bias, swa_sinks, rel_bias, q_latent_norm, w_uq, kv_latent_norm, w_ukv, group_norm, w_out, attn_post_norm, ffn_pre_norm, w_up, conv_w, conv_b, w_down, ffn_post_norm, loss_target, m_attn_pre_norm, m_w_in, m_forget_bias, m_swa_sinks, m_rel_bias, m_q_latent_norm, m_w_uq, m_kv_latent_norm, m_w_ukv, m_group_norm, m_w_out, m_attn_post_norm, m_ffn_pre_norm, m_w_up, m_conv_w, m_conv_b, m_w_down, m_ffn_post_norm, v_attn_pre_norm, v_w_in, v_forget_bias, v_swa_sinks, v_rel_bias, v_q_latent_norm, v_w_uq, v_kv_latent_norm, v_w_ukv, v_group_norm, v_w_out, v_attn_post_norm, v_ffn_pre_norm, v_w_up, v_conv_w, v_conv_b, v_w_down, v_ffn_post_norm):
    W = dict(attn_pre_norm=attn_pre_norm, w_in=w_in, forget_bias=forget_bias, swa_sinks=swa_sinks, rel_bias=rel_bias,
             q_latent_norm=q_latent_norm, w_uq=w_uq, kv_latent_norm=kv_latent_norm, w_ukv=w_ukv,
             group_norm=group_norm, w_out=w_out, attn_post_norm=attn_post_norm, ffn_pre_norm=ffn_pre_norm,
             w_up=w_up, conv_w=conv_w, conv_b=conv_b, w_down=w_down, ffn_post_norm=ffn_post_norm)
    M = dict(attn_pre_norm=m_attn_pre_norm, w_in=m_w_in, forget_bias=m_forget_bias, swa_sinks=m_swa_sinks,
             rel_bias=m_rel_bias, q_latent_norm=m_q_latent_norm, w_uq=m_w_uq, kv_latent_norm=m_kv_latent_norm,
             w_ukv=m_w_ukv, group_norm=m_group_norm, w_out=m_w_out, attn_post_norm=m_attn_post_norm,
             ffn_pre_norm=m_ffn_pre_norm, w_up=m_w_up, conv_w=m_conv_w, conv_b=m_conv_b, w_down=m_w_down,
             ffn_post_norm=m_ffn_post_norm)
    V = dict(attn_pre_norm=v_attn_pre_norm, w_in=v_w_in, forget_bias=v_forget_bias, swa_sinks=v_swa_sinks,
             rel_bias=v_rel_bias, q_latent_norm=v_q_latent_norm, w_uq=v_w_uq, kv_latent_norm=v_kv_latent_norm,
             w_ukv=v_w_ukv, group_norm=v_group_norm, w_out=v_w_out, attn_post_norm=v_attn_post_norm,
             ffn_pre_norm=v_ffn_pre_norm, w_up=v_w_up, conv_w=v_conv_w, conv_b=v_conv_b, w_down=v_w_down,
             ffn_post_norm=v_ffn_post_norm)
    me = 4 * lax.axis_index("x") + 2 * lax.axis_index("y") + lax.axis_index("c")

    def attn_shards(l):
        return [_perm_w_in(w_in[l].astype(MXU))] + [W[n][l].astype(MXU) for n in ATTN_SENT[1:]]

    def weight_parts(l):
        up = jnp.swapaxes(W["w_up"][l], 0, 1).astype(MXU)
        return dict(swa=[W["w_down"][l].astype(MXU)], fox=[up[:UP_HALF]], mla=[up[UP_HALF:], conv_w[l]],
                    ffn=attn_shards(l + 1) if l + 1 < DEPTH else [])

    def ffn_from(got_swa, got_fox, got_mla):
        return dict(w_down=_full_from_shards("w_down", got_swa[0]),
                    w_upT=_full_from_shards("w_upT", jnp.concatenate([got_fox[0], got_mla[0]], axis=1)),
                    conv_w=got_mla[1].transpose(1, 0, 2).reshape(3, 2 * D_FF))

    def attn_from(got_ffn):
        return {n: _full_from_shards(n, s) for n, s in zip(ATTN_SENT, got_ffn)}

    landed = [{} for _ in range(DEPTH)]

    def on_landed(l, names, arrays):
        landed[l].update(zip(names, arrays))

    comm = dict(weight_parts=weight_parts, ffn_from=ffn_from, attn_from=attn_from, landed=on_landed)
    full0 = dict(zip(ATTN_SENT, map(_full_from_shards, ATTN_SENT, exchange(attn_shards(0), "gather_weights", True))))
    sq, dx, grads, drel, last = local_step(x[0], loss_target[0], [full0], W, comm)
    G, delta, new_m, new_v = {}, {}, {}, {}

    def update(n, exch=None):
        shp = W[n].shape
        if n == "w_up":
            v3 = lambda a: jnp.swapaxes(a, 1, 2)
            back = v3
            g = [landed[l]["w_upT"] for l in range(DEPTH)]
        else:
            v3 = lambda a: a.reshape(shp if len(shp) == 3 else (1,) + shp)
            back = lambda a: a.reshape(shp)
            g = [landed[l][n] for l in range(DEPTH)] if n in SHARDED else v3(G[n])
        (g, d, nm, nv), got = adamw(v3(W[n]), g, v3(M[n]), v3(V[n]), "adamw_" + n, exch)
        G[n], delta[n], new_m[n], new_v[n] = back(g), back(d), back(nm), back(nv)
        return got

    parts, shapes = [], []
    for l in range(DEPTH):
        for n in SMALL_PER_LAYER:
            parts.append(grads[l][n].astype(F32).reshape(-1))
            shapes.append(grads[l][n].shape)
    parts += [drel.reshape(-1), jnp.sum(sq).reshape(1) * (0.5 / D_MODEL)]
    shapes += [drel.shape, (1,)]
    on_landed(0, ATTN_SENT, update("w_up", (last, False)))
    for l in range(DEPTH):
        landed[l]["w_in"] = _unperm_w_in(landed[l]["w_in_p"])
    gathered = update("w_down", ([_pack(parts, 128, 8, F32)], True))[0]
    red = _unpack(sum_devices(gathered, "sum_small"), shapes)
    k = 0
    per = {n: [] for n in SMALL_PER_LAYER}
    for l in range(DEPTH):
        for n in SMALL_PER_LAYER:
            per[n].append(red[k])
            k += 1
    for n in SMALL_PER_LAYER:
        G[n] = jnp.stack(per[n]).reshape((DEPTH, 3, 2 * D_FF) if n == "conv_w" else W[n].shape)
    G["rel_bias"] = red[k]
    loss = red[k + 1][0]
    G["conv_w"] = lax.dynamic_slice_in_dim(G["conv_w"], me * 704, 704, axis=2)

    for n in WEIGHTS:
        if n not in ("w_up", "w_down"):
            update(n)
    return (loss, dx[None], *[G[n] for n in WEIGHTS], *[delta[n] for n in WEIGHTS],
            *[new_m[n] for n in WEIGHTS], *[new_v[n] for n in WEIGHTS])
```

```python
import functools
import math

import numpy as np
import jax
import jax.numpy as jnp
from jax import lax
from jax.experimental import pallas as pl
from jax.experimental.pallas import tpu as pltpu

F32 = jnp.float32
MXU = jnp.bfloat16

N_DEV = 8
DEPTH = 4
D_MODEL = 1024
HEAD_DIM = 64
WINDOW = 128
SWA_Q_HEADS = 8
REL_BUCKETS = 32
REL_MAX_DIST = 128
MLA_QK_DIM = 96
ROPE_DIM = 32
ROPE_THETA = 10000.0
D_FF = 2816
EPS = 1e-6
NEG = -1e30
IN_COLS = 1956
C_QA, C_KA, C_VA = 0, 512, 640
C_QF, C_KF, C_VF = 768, 1024, 1280
C_CQ, C_CKV, C_MISC = 1536, 1792, 1920
ROPE_LANE0 = 64
ADAM_LR, ADAM_B1, ADAM_B2, ADAM_EPS, ADAM_WD, ADAM_STEP = 0.001, 0.9, 0.999, 1e-08, 0.01, 10

VMEM_LIMIT = 56 * 1024 * 1024


def _cparams(sem=None):
    return pltpu.CompilerParams(dimension_semantics=sem, vmem_limit_bytes=VMEM_LIMIT)


def _tile(n, pref):
    if n <= pref:
        return n
    t = pref - pref % 128
    while t >= 128:
        if n % t == 0:
            return t
        t -= 128
    return n


def _dot(a, b):
    return jnp.dot(a.astype(MXU), b.astype(MXU), preferred_element_type=F32)


def _dot_nt(a, b):
    return lax.dot_general(a.astype(MXU), b.astype(MXU), (((1,), (1,)), ((), ())),
                           preferred_element_type=F32)


def _rms_fwd(x, g):
    return x * lax.rsqrt(jnp.mean(x * x, axis=-1, keepdims=True) + EPS) * g


def _rms_bwd(dy, x, g, n=None):
    r = lax.rsqrt(jnp.mean(x * x, axis=-1, keepdims=True) + EPS)
    xh = x * r
    dg = jnp.sum(dy * xh, axis=0, keepdims=True)
    dxh = dy * g
    dx = r * (dxh - xh * jnp.mean(dxh * xh, axis=-1, keepdims=True))
    return dx, dg


def _acc_out(ref, val, first):
    @pl.when(first)
    def _():
        ref[...] = val

    @pl.when(jnp.logical_not(first))
    def _():
        ref[...] += val


def norm_matmul(x, g, w, name, lo_tiles=0, tn_pref=512, w_transposed=False, h_transposed=True):
    T, K = x.shape
    N = w.shape[0] if w_transposed else w.shape[1]
    tm, tn = _tile(T, 1024), _tile(N, tn_pref)

    def body(x_ref, g_ref, w_ref, o_ref, hT_ref, *rest):
        h_sc = rest[-1]
        j = pl.program_id(1)

        @pl.when(j == 0)
        def _():
            h = _rms_fwd(x_ref[...], g_ref[...])
            h_sc[...] = h.astype(MXU)
            hT_ref[...] = (h.T if h_transposed else h).astype(MXU)

        r = (_dot_nt if w_transposed else _dot)(h_sc[...], w_ref[...])
        o_ref[...] = r
        if lo_tiles:
            @pl.when(j < lo_tiles)
            def _():
                rest[0][...] = r.astype(MXU)

    h_spec = pl.BlockSpec((K, tm), lambda i, j: (0, i)) if h_transposed else pl.BlockSpec((tm, K), lambda i, j: (i, 0))
    out_specs = [pl.BlockSpec((tm, tn), lambda i, j: (i, j)), h_spec]
    out_shape = [jax.ShapeDtypeStruct((T, N), F32), jax.ShapeDtypeStruct((K, T) if h_transposed else (T, K), MXU)]
    if lo_tiles:
        out_specs.append(pl.BlockSpec((tm, tn), lambda i, j: (i, jnp.minimum(j, lo_tiles - 1))))
        out_shape.append(jax.ShapeDtypeStruct((T, lo_tiles * tn), MXU))
    return pl.pallas_call(
        body, name=name, grid=(T // tm, N // tn),
        in_specs=[pl.BlockSpec((tm, K), lambda i, j: (i, 0)),
                  pl.BlockSpec((1, K), lambda i, j: (0, 0)),
                  pl.BlockSpec((tn, K), lambda i, j: (j, 0)) if w_transposed else
                  pl.BlockSpec((K, tn), lambda i, j: (0, j))],
        out_specs=out_specs, out_shape=out_shape,
        scratch_shapes=[pltpu.VMEM((tm, K), MXU)],
        compiler_params=_cparams(("parallel", "arbitrary")),
    )(x, g, w)


def matmul_nn(a, b, name, out_dtype=F32):
    M, K = a.shape
    N = b.shape[1]
    tm, tn, tk = _tile(M, 1408), _tile(N, 1536), _tile(K, 1024)
    nk = K // tk

    def body(a_ref, b_ref, o_ref, acc):
        k = pl.program_id(2)
        part = _dot(a_ref[...], b_ref[...])
        _acc_out(acc, part, k == 0)

        @pl.when(k == nk - 1)
        def _():
            o_ref[...] = acc[...].astype(out_dtype)

    return pl.pallas_call(
        body, name=name, grid=(M // tm, N // tn, nk),
        in_specs=[pl.BlockSpec((tm, tk), lambda i, j, k: (i, k)),
                  pl.BlockSpec((tk, tn), lambda i, j, k: (k, j))],
        out_specs=pl.BlockSpec((tm, tn), lambda i, j, k: (i, j)),
        out_shape=jax.ShapeDtypeStruct((M, N), out_dtype),
        scratch_shapes=[pltpu.VMEM((tm, tn), F32)],
        compiler_params=_cparams(("parallel", "parallel", "arbitrary")),
    )(a, b)


def matmul_nt_normbwd(dy, w, x, g, dres, name):
    T, N = dy.shape
    K = w.shape[0]
    tm, tn = _tile(T, 1024), _tile(N, 1536)
    nj = N // tn

    def body(dy_ref, w_ref, x_ref, g_ref, dres_ref, dx_ref, dg_ref, acc):
        i, j = pl.program_id(0), pl.program_id(1)
        _acc_out(acc, _dot_nt(dy_ref[...], w_ref[...]), j == 0)

        @pl.when(j == nj - 1)
        def _():
            dx, dg = _rms_bwd(acc[...], x_ref[...], g_ref[...])
            dx_ref[...] = dres_ref[...] + dx
            _acc_out(dg_ref, dg, i == 0)

    return pl.pallas_call(
        body, name=name, grid=(T // tm, nj),
        in_specs=[pl.BlockSpec((tm, tn), lambda i, j: (i, j)),
                  pl.BlockSpec((K, tn), lambda i, j: (0, j)),
                  pl.BlockSpec((tm, K), lambda i, j: (i, 0)),
                  pl.BlockSpec((1, K), lambda i, j: (0, 0)),
                  pl.BlockSpec((tm, K), lambda i, j: (i, 0))],
        out_specs=[pl.BlockSpec((tm, K), lambda i, j: (i, 0)),
                   pl.BlockSpec((1, K), lambda i, j: (0, 0))],
        out_shape=[jax.ShapeDtypeStruct((T, K), F32), jax.ShapeDtypeStruct((1, K), F32)],
        scratch_shapes=[pltpu.VMEM((tm, K), F32)],
        compiler_params=_cparams(("arbitrary", "arbitrary")),
    )(dy, w, x, g, dres)


def loss_kernel(y, tgt):
    T, D = y.shape
    tm = _tile(T, 512)

    def body(y_ref, t_ref, dy_ref, acc_ref):
        e = y_ref[...] - t_ref[...]
        dy_ref[...] = e * (1.0 / D)
        _acc_out(acc_ref, jnp.sum(e * e, axis=0, keepdims=True), pl.program_id(0) == 0)

    return pl.pallas_call(
        body, name="loss", grid=(T // tm,),
        in_specs=[pl.BlockSpec((tm, D), lambda i: (i, 0)), pl.BlockSpec((tm, D), lambda i: (i, 0))],
        out_specs=[pl.BlockSpec((tm, D), lambda i: (i, 0)), pl.BlockSpec((1, D), lambda i: (0, 0))],
        out_shape=[jax.ShapeDtypeStruct((T, D), F32), jax.ShapeDtypeStruct((1, D), F32)],
        compiler_params=_cparams(("arbitrary",)),
    )(y, tgt)


def _rope_partner(x):
    lane = lax.broadcasted_iota(jnp.int32, (1, 128), 1)
    return jnp.where(lane < ROPE_LANE0 + ROPE_DIM // 2, pltpu.roll(x, 128 - ROPE_DIM // 2, 1),
                     pltpu.roll(x, ROPE_DIM // 2, 1))


def _rope_apply(x, cos, sin_signed):
    return x * cos + _rope_partner(x) * sin_signed


def _rope_apply_bwd(dy, cos, sin_signed):
    lane = lax.broadcasted_iota(jnp.int32, (1, 128), 1)
    rotary = (lane >= ROPE_LANE0) & (lane < ROPE_LANE0 + ROPE_DIM)
    return dy * cos + jnp.where(rotary, _rope_partner(dy * sin_signed), 0.0)


def rope_tables(T):
    pos = jnp.arange(T, dtype=F32)
    inv_freq = ROPE_THETA ** (-(jnp.arange(ROPE_DIM // 2, dtype=F32) * 2.0 / ROPE_DIM))
    ang = pos[:, None] * inv_freq[None, :]
    cos, sin = jnp.cos(ang), jnp.sin(ang)
    z = jnp.zeros((T, ROPE_LANE0), F32)
    z2 = jnp.zeros((T, 128 - ROPE_LANE0 - ROPE_DIM), F32)
    cosr = jnp.concatenate([z, cos, cos, z2], axis=1)
    sinr = jnp.concatenate([z, -sin, sin, z2], axis=1)
    return cosr, sinr


def mla_prep(proj, gq, gkv, wuq, wukv, cosr, sinr):
    T = proj.shape[0]
    tm = _tile(T, 512)

    def body(cq_ref, ckv_ref, misc_ref, gq_ref, gkv_ref, wuq_ref, wukv_ref, cos_ref, sin_ref,
             q_ref, k_ref, v_ref, cqT_ref, ckvT_ref):
        lane = lax.broadcasted_iota(jnp.int32, (1, 128), 1)
        cosr_, sinr_ = cos_ref[...], sin_ref[...]
        cosq = cosr_ + jnp.where(lane < ROPE_LANE0, 1.0, 0.0)
        cqn = _rms_fwd(cq_ref[...], gq_ref[...])
        cqT_ref[...] = cqn.T.astype(MXU)
        qm = _dot(cqn, wuq_ref[...])
        q_ref[...] = jnp.concatenate(
            [_rope_apply(qm[:, 128 * h:128 * (h + 1)], cosq, sinr_) for h in range(4)], axis=1).astype(MXU)
        ckvn = _rms_fwd(ckv_ref[...], gkv_ref[...])
        ckvT_ref[...] = ckvn.T.astype(MXU)
        kv = _dot(ckvn, wukv_ref[...])
        kr = _rope_apply(misc_ref[...], cosr_, sinr_)
        k_ref[...] = jnp.concatenate(
            [kv[:, 128 * h:128 * (h + 1)] + kr for h in range(4)], axis=1).astype(MXU)
        v_ref[...] = kv[:, 512:768].astype(MXU)

    row = lambda i: (i, 0)
    const = lambda i: (0, 0)
    return pl.pallas_call(
        body, name="mla_prep", grid=(T // tm,),
        in_specs=[pl.BlockSpec((tm, 256), lambda i: (i, C_CQ // 256)),
                  pl.BlockSpec((tm, 128), lambda i: (i, C_CKV // 128)),
                  pl.BlockSpec((tm, 128), lambda i: (i, C_MISC // 128)),
                  pl.BlockSpec((1, 256), const), pl.BlockSpec((1, 128), const),
                  pl.BlockSpec((256, 512), const), pl.BlockSpec((128, 768), const),
                  pl.BlockSpec((tm, 128), row), pl.BlockSpec((tm, 128), row)],
        out_specs=[pl.BlockSpec((tm, 512), row), pl.BlockSpec((tm, 512), row), pl.BlockSpec((tm, 256), row),
                   pl.BlockSpec((256, tm), lambda i: (0, i)), pl.BlockSpec((128, tm), lambda i: (0, i))],
        out_shape=[jax.ShapeDtypeStruct((T, 512), MXU), jax.ShapeDtypeStruct((T, 512), MXU),
                   jax.ShapeDtypeStruct((T, 256), MXU),
                   jax.ShapeDtypeStruct((256, T), MXU), jax.ShapeDtypeStruct((128, T), MXU)],
        compiler_params=_cparams(("parallel",)),
    )(proj, proj, proj, gq, gkv, wuq, wukv, cosr, sinr)


def mla_prep_bwd(dq, dk, dv, proj, gq, gkv, wuq, wukv, cosr, sinr):
    T = proj.shape[0]
    tm = _tile(T, 512)

    def body(dq_ref, dk_ref, dv_ref, cq_ref, ckv_ref, gq_ref, gkv_ref, wuq_ref, wukv_ref, cos_ref, sin_ref,
             dqm_ref, dkv_ref, dcq_ref, dckv_ref, dmisc_ref, dgq_ref, dgkv_ref):
        first = pl.program_id(0) == 0
        lane = lax.broadcasted_iota(jnp.int32, (1, 128), 1)
        cosr_, sinr_ = cos_ref[...], sin_ref[...]
        cosq = cosr_ + jnp.where(lane < ROPE_LANE0, 1.0, 0.0)
        dqv = dq_ref[...]
        dqm = jnp.concatenate(
            [_rope_apply_bwd(dqv[:, 128 * h:128 * (h + 1)], cosq, sinr_) for h in range(4)], axis=1)
        dqm_ref[...] = dqm.astype(MXU)
        dcq, dgq = _rms_bwd(_dot_nt(dqm, wuq_ref[...]), cq_ref[...], gq_ref[...])
        dcq_ref[...] = dcq
        _acc_out(dgq_ref, dgq, first)
        dkv_ = dk_ref[...]
        heads = [dkv_[:, 128 * h:128 * (h + 1)] for h in range(4)]
        dkr = heads[0] + heads[1] + heads[2] + heads[3]
        dmisc_ref[...] = _rope_apply_bwd(dkr, cosr_, sinr_)
        dkvm = jnp.concatenate([jnp.where(lane < ROPE_LANE0, hd, 0.0) for hd in heads] + [dv_ref[...]], axis=1)
        dkv_ref[...] = dkvm.astype(MXU)
        dckv, dgkv = _rms_bwd(_dot_nt(dkvm, wukv_ref[...]), ckv_ref[...], gkv_ref[...])
        dckv_ref[...] = dckv
        _acc_out(dgkv_ref, dgkv, first)

    row = lambda i: (i, 0)
    const = lambda i: (0, 0)
    return pl.pallas_call(
        body, name="mla_prep_bwd", grid=(T // tm,),
        in_specs=[pl.BlockSpec((tm, 512), row), pl.BlockSpec((tm, 512), row), pl.BlockSpec((tm, 256), row),
                  pl.BlockSpec((tm, 256), lambda i: (i, C_CQ // 256)),
                  pl.BlockSpec((tm, 128), lambda i: (i, C_CKV // 128)),
                  pl.BlockSpec((1, 256), const), pl.BlockSpec((1, 128), const),
                  pl.BlockSpec((256, 512), const), pl.BlockSpec((128, 768), const),
                  pl.BlockSpec((tm, 128), row), pl.BlockSpec((tm, 128), row)],
        out_specs=[pl.BlockSpec((tm, 512), row), pl.BlockSpec((tm, 768), row), pl.BlockSpec((tm, 256), row),
                   pl.BlockSpec((tm, 128), row), pl.BlockSpec((tm, 128), row),
                   pl.BlockSpec((1, 256), const), pl.BlockSpec((1, 128), const)],
        out_shape=[jax.ShapeDtypeStruct((T, 512), MXU), jax.ShapeDtypeStruct((T, 768), MXU),
                   jax.ShapeDtypeStruct((T, 256), F32), jax.ShapeDtypeStruct((T, 128), F32),
                   jax.ShapeDtypeStruct((T, 128), F32),
                   jax.ShapeDtypeStruct((1, 256), F32), jax.ShapeDtypeStruct((1, 128), F32)],
        compiler_params=_cparams(("arbitrary",)),
    )(dq, dk, dv, proj, proj, gq, gkv, wuq, wukv, cosr, sinr)


def _split3(x):
    hi = x.astype(MXU)
    r1 = x - hi.astype(F32)
    mid = r1.astype(MXU)
    lo = (r1 - mid.astype(F32)).astype(MXU)
    return hi, mid, lo


def _tri_matmul(tri, x):
    hi, mid, lo = _split3(x)
    d = lambda p: jnp.dot(tri, p, preferred_element_type=F32)
    return d(hi) + d(mid) + d(lo)


def _log_sigmoid(z):
    return jnp.minimum(z, 0.0) - jnp.log(1.0 + jnp.exp(-jnp.abs(z)))


def fox_gate(proj, fbias):
    T = proj.shape[0]
    tb = _tile(T, 512)

    def body(misc_ref, b_ref, fc_ref, fr_ref, frep_ref, carry):
        @pl.when(pl.program_id(0) == 0)
        def _():
            carry[...] = jnp.zeros_like(carry)

        lane = lax.broadcasted_iota(jnp.int32, (1, 128), 1)
        lf = jnp.where(lane < 4, _log_sigmoid(misc_ref[...] + b_ref[...]), 0.0)
        r = lax.broadcasted_iota(jnp.int32, (tb, tb), 0)
        c = lax.broadcasted_iota(jnp.int32, (tb, tb), 1)
        tri = jnp.where(r >= c, 1.0, 0.0).astype(MXU)
        F = _tri_matmul(tri, lf) + carry[...]
        carry[...] = carry[...] + jnp.sum(lf, axis=0, keepdims=True)
        fc_ref[0] = F
        fc_ref[1] = pltpu.roll(F, 126, 1)
        ft = F.T[0:8, :]
        fr_ref[0] = ft
        fr_ref[1] = pltpu.roll(ft, 6, 0)
        for h in range(4):
            frep_ref[h] = jnp.broadcast_to(_lane_pick(F, h), (tb, 128))

    return pl.pallas_call(
        body, name="fox_gate", grid=(T // tb,),
        in_specs=[pl.BlockSpec((tb, 128), lambda i: (i, C_MISC // 128)), pl.BlockSpec((1, 128), lambda i: (0, 0))],
        out_specs=[pl.BlockSpec((2, tb, 128), lambda i: (0, i, 0)), pl.BlockSpec((2, 8, tb), lambda i: (0, 0, i)),
                   pl.BlockSpec((4, tb, 128), lambda i: (0, i, 0))],
        out_shape=[jax.ShapeDtypeStruct((2, T, 128), F32), jax.ShapeDtypeStruct((2, 8, T), F32),
                   jax.ShapeDtypeStruct((4, T, 128), F32)],
        scratch_shapes=[pltpu.VMEM((1, 128), F32)],
        compiler_params=_cparams(("arbitrary",)),
    )(proj, fbias)


def fox_gate_bwd(dFq, dFk, proj, fbias):
    T = proj.shape[0]
    tb = _tile(T, 512)
    nb = T // tb

    def body(dq_ref, dk_ref, misc_ref, b_ref, dm_ref, db_ref, carry):
        first = pl.program_id(0) == 0

        @pl.when(first)
        def _():
            carry[...] = jnp.zeros_like(carry)

        lane = lax.broadcasted_iota(jnp.int32, (1, 128), 1)
        dF = jnp.where(lane < 4, (dq_ref[0] + dk_ref[0]) + pltpu.roll(dq_ref[1] + dk_ref[1], 2, 1), 0.0)
        r = lax.broadcasted_iota(jnp.int32, (tb, tb), 0)
        c = lax.broadcasted_iota(jnp.int32, (tb, tb), 1)
        tri = jnp.where(r <= c, 1.0, 0.0).astype(MXU)
        dlf = _tri_matmul(tri, dF) + carry[...]
        carry[...] = carry[...] + jnp.sum(dF, axis=0, keepdims=True)
        z = misc_ref[...] + b_ref[...]
        dz = jnp.where(lane < 4, dlf * (1.0 / (1.0 + jnp.exp(z))), 0.0)
        dm_ref[...] = dz
        _acc_out(db_ref, jnp.sum(dz, axis=0, keepdims=True), first)

    return pl.pallas_call(
        body, name="fox_gate_bwd", grid=(nb,),
        in_specs=[pl.BlockSpec((2, tb, 128), lambda i: (0, nb - 1 - i, 0)),
                  pl.BlockSpec((2, tb, 128), lambda i: (0, nb - 1 - i, 0)),
                  pl.BlockSpec((tb, 128), lambda i: (nb - 1 - i, C_MISC // 128)),
                  pl.BlockSpec((1, 128), lambda i: (0, 0))],
        out_specs=[pl.BlockSpec((tb, 128), lambda i: (nb - 1 - i, 0)), pl.BlockSpec((1, 128), lambda i: (0, 0))],
        out_shape=[jax.ShapeDtypeStruct((T, 128), F32), jax.ShapeDtypeStruct((1, 128), F32)],
        scratch_shapes=[pltpu.VMEM((1, 128), F32)],
        compiler_params=_cparams(("arbitrary",)),
    )(dFq, dFk, proj, fbias)


FLASH_TILE = 512


def _row_stat_tile(a, b, n):
    at = jnp.broadcast_to(a, (n, 128)).T[0:8, :]
    bt = jnp.broadcast_to(b, (n, 128)).T[0:8, :]
    sub = lax.broadcasted_iota(jnp.int32, (8, 1), 0)
    return jnp.where(sub == 0, at, jnp.where(sub == 1, bt, 0.0))


def _col_stat_tile(a, b):
    lane = lax.broadcasted_iota(jnp.int32, (1, 128), 1)
    return jnp.where(lane == 0, a, jnp.where(lane == 1, b, 0.0))


def _lane_pick(x, h):
    lane = lax.broadcasted_iota(jnp.int32, (1, 128), 1)
    return jnp.sum(jnp.where(lane == h, x, 0.0), axis=1, keepdims=True)


def _half_mask(h):
    lane = lax.broadcasted_iota(jnp.int32, (1, 128), 1)
    return (lane // HEAD_DIM) == h


def _call_hosting(body, name, grid, args, in_specs, out_specs, out_shape, scratch, exch):
    n_out = len(out_shape)
    if exch is not None:
        body, (xargs, xin, xout, xshape, xscratch) = hosted_exchange(
            body, len(args), n_out, len(scratch), grid, *exch)
        args, in_specs, out_specs = args + xargs, in_specs + xin, out_specs + xout
        out_shape, scratch = out_shape + xshape, scratch + xscratch
    res = pl.pallas_call(
        body, name=name, grid=grid, in_specs=in_specs, out_specs=out_specs, out_shape=out_shape,
        scratch_shapes=scratch, compiler_params=_cparams(("arbitrary",) * len(grid)),
    )(*args)
    return res[:n_out], res[n_out:]


def flash_fwd(q, k, v, frep, frow, *, qblk, kblk, vblk, nq, scale, name, exch=None):
    T = q.shape[0]
    tk = _tile(T, FLASH_TILE)
    tq = _tile(T, 2 * FLASH_TILE)
    per_q = tq // tk
    wq = 128 * nq
    has_f = frep is not None

    def body(*refs):
        if has_f:
            q_ref, k_ref, v_ref, fk_ref, fr_ref, o_ref, lr_ref, vT_sc, m_sc, acc_sc = refs
        else:
            q_ref, k_ref, v_ref, o_ref, lr_ref, vT_sc, m_sc, acc_sc = refs
        i = pl.program_id(1)

        @pl.when(i == 0)
        def _():
            vT_sc[...] = v_ref[...].astype(F32).T.astype(MXU)

        key_row = lax.broadcasted_iota(jnp.int32, (tk, 1), 0)
        q_col = lax.broadcasted_iota(jnp.int32, (1, tq), 1)
        row_half = lax.broadcasted_iota(jnp.int32, (128, 1), 0) // HEAD_DIM
        qb = q_ref[...].astype(F32) * scale
        if nq == 1:
            qhs = [jnp.where(_half_mask(h), qb, 0).astype(MXU) for h in range(2)]
        else:
            qhs = [qb[:, 128 * h:128 * (h + 1)].astype(MXU) for h in range(2)]
        for h in range(2):
            m_sc[h] = jnp.full((1, tq), NEG, F32)
            acc_sc[h] = jnp.zeros((128, tq), F32)

        def make_step(diag_block):
            def step(j, carry):
                off = pl.multiple_of(j * tk, tk)
                ks = k_ref[pl.ds(off, tk), :]
                vT = vT_sc[:, pl.ds(off, tk)]
                for h in range(2):
                    kh = ks if nq == 1 else ks[:, 128 * h:128 * (h + 1)]
                    sT = _dot_nt(kh, qhs[h])
                    if has_f:
                        fk = fk_ref[h, pl.ds(off, tk), :]
                        sT = sT + (fr_ref[0, h:h + 1, :] - jnp.concatenate([fk] * (tq // 128), axis=1))
                    if diag_block is not None:
                        sT = jnp.where(key_row + diag_block * tk <= q_col, sT, NEG)
                    m_prev = m_sc[h]
                    m_new = jnp.maximum(m_prev, jnp.max(sT, axis=0, keepdims=True))
                    alpha = jnp.exp(m_prev - m_new)
                    pT = jnp.exp(sT - m_new)
                    vTh = jnp.where(row_half == h, vT, jnp.ones_like(vT))
                    acc_sc[h] = alpha * acc_sc[h] + _dot(vTh, pT)
                    m_sc[h] = m_new
                return carry
            return step

        lax.fori_loop(0, per_q * i, make_step(None), 0)
        for d in range(per_q):
            make_step(d)(per_q * i + d, 0)
        outs, lses = [], []
        for h in range(2):
            acc = acc_sc[h]
            outs.append(acc / pltpu.roll(acc, HEAD_DIM, 0))
            l = acc_sc[h, HEAD_DIM * (1 - h):HEAD_DIM * (1 - h) + 1, :]
            lses.append(m_sc[h] + jnp.log(l))
        o_ref[...] = jnp.where(row_half == 0, outs[0], outs[1]).T
        sub = lax.broadcasted_iota(jnp.int32, (8, 1), 0)
        lr_ref[0] = jnp.where(sub == 0, lses[0], jnp.where(sub == 1, lses[1], 0.0))

    in_specs = [pl.BlockSpec((tq, wq), lambda p, i: (i, qblk + p)),
                pl.BlockSpec((T, wq), lambda p, i: (0, kblk + p)),
                pl.BlockSpec((T, 128), lambda p, i: (0, vblk + p))]
    args = [q, k, v]
    if has_f:
        in_specs += [pl.BlockSpec((2, T, 128), lambda p, i: (p, 0, 0)),
                     pl.BlockSpec((1, 8, tq), lambda p, i: (p, 0, i))]
        args += [frep, frow]
    out_specs = [pl.BlockSpec((tq, 128), lambda p, i: (i, p)), pl.BlockSpec((1, 8, tq), lambda p, i: (p, 0, i))]
    out_shape = [jax.ShapeDtypeStruct((T, 256), F32), jax.ShapeDtypeStruct((2, 8, T), F32)]
    scratch = [pltpu.VMEM((128, T), MXU), pltpu.VMEM((2, 1, tq), F32), pltpu.VMEM((2, 128, tq), F32)]
    return _call_hosting(body, name, (2, T // tq), args, in_specs, out_specs, out_shape, scratch, exch)


def flash_bwd(q, k, v, do, o, lrow, fcol, frow, *, qblk, kblk, vblk, nq, scale, name, exch=None):
    T = q.shape[0]
    tq = tk = _tile(T, FLASH_TILE)
    wq = 128 * nq
    nqb = T // tq
    has_f = fcol is not None

    def body(*refs):
        if has_f:
            (q_ref, k_ref, v_ref, do_ref, o_ref, lr_ref, fc_ref, fr_ref,
             dq_ref, dk_ref, dv_ref, df_ref, dfq_ref, dk_sc, dv_sc, dqT_sc, d_sc, df_sc, dfq_sc) = refs
        else:
            q_ref, k_ref, v_ref, do_ref, o_ref, lr_ref, dq_ref, dk_ref, dv_ref, dk_sc, dv_sc, dqT_sc, d_sc = refs
        j = pl.program_id(1)
        diag = lax.broadcasted_iota(jnp.int32, (tk, 1), 0) <= lax.broadcasted_iota(jnp.int32, (1, tq), 1)
        hms = [_half_mask(h) for h in range(2)]

        @pl.when(j == 0)
        def _():
            dqT_sc[...] = jnp.zeros_like(dqT_sc)
            if has_f:
                dfq_sc[...] = jnp.zeros_like(dfq_sc)

            def delta(b, carry):
                off = pl.multiple_of(b * tq, tq)
                prod = do_ref[pl.ds(off, tq), :] * o_ref[pl.ds(off, tq), :]
                Ds = [jnp.sum(jnp.where(hms[h], prod, 0.0), axis=1, keepdims=True) for h in range(2)]
                d_sc[:, pl.ds(off, tq)] = _row_stat_tile(Ds[0], Ds[1], tq)
                return carry

            lax.fori_loop(0, nqb, delta, 0)

        kb = k_ref[...]
        vb = v_ref[...]
        if nq == 1:
            khs = [jnp.where(hms[h], kb, 0).astype(MXU) for h in range(2)]
        else:
            khs = [kb[:, 128 * h:128 * (h + 1)].astype(MXU) for h in range(2)]
        kTs = [kh.astype(F32).T.astype(MXU) for kh in khs]
        kss = [(kh.astype(F32) * scale).astype(MXU) for kh in khs]
        vhs = [jnp.where(hms[h], vb, 0).astype(MXU) for h in range(2)]
        fks = [_lane_pick(fc_ref[0], h) for h in range(2)] if has_f else None
        dv_sc[...] = jnp.zeros_like(dv_sc)
        dk_sc[...] = jnp.zeros_like(dk_sc)
        if has_f:
            df_sc[...] = jnp.zeros_like(df_sc)

        def make_step(masked):
            def step(i, carry):
                off = pl.multiple_of(i * tq, tq)
                qs = q_ref[pl.ds(off, tq), :]
                dos = do_ref[pl.ds(off, tq), :]
                for h in range(2):
                    qh = qs if nq == 1 else qs[:, 128 * h:128 * (h + 1)]
                    sT = _dot_nt(kss[h], qh)
                    if has_f:
                        sT = sT + (fr_ref[0, h:h + 1, pl.ds(off, tq)] - fks[h])
                    pT = jnp.exp(sT - lr_ref[0, h:h + 1, pl.ds(off, tq)])
                    if masked:
                        pT = jnp.where(diag, pT, 0.0)
                    dsT = pT * (_dot_nt(vhs[h], dos) - d_sc[h:h + 1, pl.ds(off, tq)])
                    dv_sc[...] += _dot(pT, jnp.where(hms[h], dos, 0))
                    qq = jnp.where(hms[h], qs, 0) if nq == 1 else qh
                    dk_sc[h if nq == 2 else 0] += _dot(dsT, qq)
                    dqT_sc[h if nq == 2 else 0, :, pl.ds(off, tq)] += _dot(kTs[h], dsT)
                    if has_f:
                        part = dsT[:, 0:128]
                        for c in range(1, tq // 128):
                            part = part + dsT[:, 128 * c:128 * (c + 1)]
                        df_sc[h] += part
                        dfq_sc[h:h + 1, pl.ds(off, tq)] += jnp.sum(dsT, axis=0, keepdims=True)
                return carry
            return step

        make_step(True)(j, 0)
        lax.fori_loop(j + 1, nqb, make_step(False), 0)
        if nq == 1:
            dk_ref[...] = dk_sc[0] * scale
        else:
            dk_ref[...] = jnp.concatenate([dk_sc[0], dk_sc[1]], axis=1) * scale
        dv_ref[...] = dv_sc[...]
        if has_f:
            df_ref[0] = _col_stat_tile(-jnp.sum(df_sc[0], axis=1, keepdims=True),
                                       -jnp.sum(df_sc[1], axis=1, keepdims=True))

        @pl.when(j == nqb - 1)
        def _():
            if nq == 1:
                dq_ref[...] = dqT_sc[0].T * scale
            else:
                dq_ref[...] = jnp.concatenate([dqT_sc[0].T, dqT_sc[1].T], axis=1) * scale
            if has_f:
                sub = lax.broadcasted_iota(jnp.int32, (128, 1), 0)
                rows = jnp.where(sub == 0, dfq_sc[0:1, :], jnp.where(sub == 1, dfq_sc[1:2, :], 0.0))
                dfq_ref[0] = rows.T

    in_specs = [pl.BlockSpec((T, wq), lambda p, j: (0, qblk + p)),
                pl.BlockSpec((tk, wq), lambda p, j: (j, kblk + p)),
                pl.BlockSpec((tk, 128), lambda p, j: (j, vblk + p)),
                pl.BlockSpec((T, 128), lambda p, j: (0, p)),
                pl.BlockSpec((T, 128), lambda p, j: (0, p)),
                pl.BlockSpec((1, 8, T), lambda p, j: (p, 0, 0))]
    args = [q, k, v, do, o, lrow]
    out_specs = [pl.BlockSpec((T, wq), lambda p, j: (0, p)),
                 pl.BlockSpec((tk, wq), lambda p, j: (j, p)), pl.BlockSpec((tk, 128), lambda p, j: (j, p))]
    out_shape = [jax.ShapeDtypeStruct((T, 2 * wq), F32), jax.ShapeDtypeStruct((T, 2 * wq), F32),
                 jax.ShapeDtypeStruct((T, 256), F32)]
    scratch = [pltpu.VMEM((nq, tk, 128), F32), pltpu.VMEM((tk, 128), F32), pltpu.VMEM((nq, 128, T), F32),
               pltpu.VMEM((8, T), F32)]
    if has_f:
        in_specs += [pl.BlockSpec((1, tk, 128), lambda p, j: (p, j, 0)),
                     pl.BlockSpec((1, 8, T), lambda p, j: (p, 0, 0))]
        args += [fcol, frow]
        out_specs += [pl.BlockSpec((1, tk, 128), lambda p, j: (p, j, 0)),
                      pl.BlockSpec((1, T, 128), lambda p, j: (p, 0, 0))]
        out_shape += [jax.ShapeDtypeStruct((2, T, 128), F32), jax.ShapeDtypeStruct((2, T, 128), F32)]
        scratch += [pltpu.VMEM((2, tk, 128), F32), pltpu.VMEM((8, T), F32)]
    return _call_hosting(body, name, (2, T // tk), args, in_specs, out_specs, out_shape, scratch, exch)


def _swa_align(pair, e, h):
    sel = jnp.where(_half_mask(e), pair, 0.0)
    if e == h:
        return sel
    return pltpu.roll(sel, HEAD_DIM, 1)


def _swa_mask(n):
    W = WINDOW
    qi = lax.broadcasted_iota(jnp.int32, (W, 2 * W), 0) + W
    kj = lax.broadcasted_iota(jnp.int32, (W, 2 * W), 1)
    dist = qi - kj
    return (dist >= 0) & (dist < W) & ((n > 0) | (kj >= W))


def swa_fwd(proj, bias, sinks, exch=None):
    T = proj.shape[0]
    W = WINDOW
    nb = T // W
    scale = HEAD_DIM ** -0.5

    def body(sink_ref, q_ref, kp_ref, kc_ref, vp_ref, vc_ref, b_ref, o_ref, l_ref):
        n = pl.program_id(0)
        mask = _swa_mask(n)
        kband = jnp.concatenate([kp_ref[...], kc_ref[...]], axis=0).astype(MXU)
        vband = jnp.concatenate([vp_ref[...], vc_ref[...]], axis=0).astype(MXU)
        lane = lax.broadcasted_iota(jnp.int32, (1, 128), 1)
        lse_tile = jnp.zeros((W, 128), F32)
        pairs = []
        for h in range(2):
            full = []
            for g in range(4):
                hq = 4 * h + g
                qa = _swa_align(q_ref[:, 128 * (hq // 2):128 * (hq // 2 + 1)], hq % 2, h)
                s = _dot_nt(qa, kband) * scale + b_ref[hq]
                s = jnp.where(mask, s, NEG)
                sink = sink_ref[hq]
                m = jnp.maximum(jnp.max(s, axis=1, keepdims=True), sink)
                e = jnp.exp(s - m)
                l = jnp.sum(e, axis=1, keepdims=True) + jnp.exp(sink - m)
                r = jnp.where(_half_mask(h), _dot(e, vband), 0.0) / l
                full.append(r + pltpu.roll(r, HEAD_DIM, 1))
                lse_tile = jnp.where(lane == hq, m + jnp.log(l), lse_tile)
            pairs.append(jnp.where(_half_mask(0), full[0], full[1]))
            pairs.append(jnp.where(_half_mask(0), full[2], full[3]))
        o_ref[...] = jnp.concatenate(pairs, axis=1)
        l_ref[...] = lse_tile

    prev = lambda n: (jnp.maximum(n - 1, 0), C_KA // 128)
    cur = lambda n: (n, C_KA // 128)
    prev_v = lambda n: (jnp.maximum(n - 1, 0), C_VA // 128)
    cur_v = lambda n: (n, C_VA // 128)
    return _call_hosting(
        body, "swa_fwd", (nb,), [sinks, proj, proj, proj, proj, proj, bias],
        [pl.BlockSpec(memory_space=pltpu.SMEM),
         pl.BlockSpec((W, 512), lambda n: (n, 0)),
         pl.BlockSpec((W, 128), prev), pl.BlockSpec((W, 128), cur),
         pl.BlockSpec((W, 128), prev_v), pl.BlockSpec((W, 128), cur_v),
         pl.BlockSpec((8, W, 2 * W), lambda n: (0, 0, 0))],
        [pl.BlockSpec((W, 512), lambda n: (n, 0)), pl.BlockSpec((W, 128), lambda n: (n, 0))],
        [jax.ShapeDtypeStruct((T, 512), F32), jax.ShapeDtypeStruct((T, 128), F32)], [], exch)


def swa_bwd(proj, bias, sinks, do, o, lse, exch=None):
    T = proj.shape[0]
    W = WINDOW
    nb = T // W
    scale = HEAD_DIM ** -0.5

    def body(sink_ref, q_ref, kp_ref, kc_ref, vp_ref, vc_ref, b_ref, do_ref, o_ref, l_ref,
             dq_ref, dk_ref, dv_ref, db_ref, dsk_ref, ck, cv):
        n = pl.program_id(0)

        @pl.when(n == 0)
        def _():
            ck[...] = jnp.zeros_like(ck)
            cv[...] = jnp.zeros_like(cv)
            db_ref[...] = jnp.zeros_like(db_ref)
            dsk_ref[...] = jnp.zeros_like(dsk_ref)

        @pl.when(n < nb)
        def _():
            mask = _swa_mask(n)
            kb32 = jnp.concatenate([kp_ref[...], kc_ref[...]], axis=0)
            vb32 = jnp.concatenate([vp_ref[...], vc_ref[...]], axis=0)
            kband = kb32.astype(MXU)
            sub = lax.broadcasted_iota(jnp.int32, (8, 1), 0)
            dk_band = jnp.zeros((2 * W, 128), F32)
            dv_band = jnp.zeros((2 * W, 128), F32)
            dsk = jnp.zeros((8, 128), F32)
            dq_pairs = []
            mask4 = jnp.concatenate([mask] * 4, axis=0)
            for h in range(2):
                hm = _half_mask(h)
                km = jnp.where(hm, kb32, 0.0).astype(MXU)
                vm = jnp.where(hm, vb32, 0.0).astype(MXU)
                pbs = [slice(128 * ((4 * h + g) // 2), 128 * ((4 * h + g) // 2 + 1)) for g in range(4)]
                q4 = jnp.concatenate([_swa_align(q_ref[:, pbs[g]], g % 2, h) for g in range(4)], axis=0)
                do4 = jnp.concatenate([_swa_align(do_ref[:, pbs[g]], g % 2, h) for g in range(4)], axis=0)
                D4 = jnp.concatenate(
                    [jnp.sum(jnp.where(_half_mask(g % 2), do_ref[:, pbs[g]] * o_ref[:, pbs[g]], 0.0), axis=1,
                             keepdims=True) for g in range(4)], axis=0)
                lse4 = jnp.concatenate([_lane_pick(l_ref[...], 4 * h + g) for g in range(4)], axis=0)
                sink4 = jnp.concatenate([jnp.full((W, 1), sink_ref[4 * h + g], F32) for g in range(4)], axis=0)
                s = _dot_nt(q4, kband) * scale + b_ref[4 * h:4 * h + 4].reshape(4 * W, 2 * W)
                p = jnp.where(mask4, jnp.exp(s - lse4), 0.0)
                sd = jnp.exp(sink4 - lse4) * D4
                for g in range(4):
                    dsk = dsk + jnp.where(sub == 4 * h + g,
                                          -jnp.sum(sd[W * g:W * (g + 1)], axis=0, keepdims=True), 0.0)
                ds = p * (_dot_nt(do4, vm) - D4)
                db_ref[4 * h:4 * h + 4] += ds.reshape(4, W, 2 * W)
                dq = _dot(ds, km) * scale
                dq = dq + pltpu.roll(dq, HEAD_DIM, 1)
                dk_band = dk_band + _dot(ds.T, q4) * scale
                dv_band = dv_band + _dot(p.T, do4)
                dq_pairs.append(jnp.where(_half_mask(0), dq[0:W], dq[W:2 * W]))
                dq_pairs.append(jnp.where(_half_mask(0), dq[2 * W:3 * W], dq[3 * W:4 * W]))
            dq_ref[...] = jnp.concatenate(dq_pairs, axis=1)
            dsk_ref[...] += dsk
            dk_ref[...] = ck[...] + dk_band[0:W]
            dv_ref[...] = cv[...] + dv_band[0:W]
            ck[...] = dk_band[W:2 * W]
            cv[...] = dv_band[W:2 * W]

        @pl.when(n == nb)
        def _():
            dk_ref[...] = ck[...]
            dv_ref[...] = cv[...]

    cl = lambda n: jnp.minimum(n, nb - 1)
    pv = lambda n: jnp.maximum(jnp.minimum(n, nb - 1) - 1, 0)
    return _call_hosting(
        body, "swa_bwd", (nb + 1,), [sinks, proj, proj, proj, proj, proj, bias, do, o, lse],
        [pl.BlockSpec(memory_space=pltpu.SMEM),
         pl.BlockSpec((W, 512), lambda n: (cl(n), 0)),
         pl.BlockSpec((W, 128), lambda n: (pv(n), C_KA // 128)),
         pl.BlockSpec((W, 128), lambda n: (cl(n), C_KA // 128)),
         pl.BlockSpec((W, 128), lambda n: (pv(n), C_VA // 128)),
         pl.BlockSpec((W, 128), lambda n: (cl(n), C_VA // 128)),
         pl.BlockSpec((8, W, 2 * W), lambda n: (0, 0, 0)),
         pl.BlockSpec((W, 512), lambda n: (cl(n), 0)),
         pl.BlockSpec((W, 512), lambda n: (cl(n), 0)),
         pl.BlockSpec((W, 128), lambda n: (cl(n), 0))],
        [pl.BlockSpec((W, 512), lambda n: (cl(n), 0)),
         pl.BlockSpec((W, 128), lambda n: (jnp.maximum(n - 1, 0), 0)),
         pl.BlockSpec((W, 128), lambda n: (jnp.maximum(n - 1, 0), 0)),
         pl.BlockSpec((8, W, 2 * W), lambda n: (0, 0, 0)),
         pl.BlockSpec((8, 128), lambda n: (0, 0))],
        [jax.ShapeDtypeStruct((T, 512), F32), jax.ShapeDtypeStruct((T, 128), F32),
         jax.ShapeDtypeStruct((T, 128), F32), jax.ShapeDtypeStruct((8, W, 2 * W), F32),
         jax.ShapeDtypeStruct((8, 128), F32)],
        [pltpu.VMEM((W, 128), F32), pltpu.VMEM((W, 128), F32)], exch)


def swa_bias_table(rel_bias):
    W = WINDOW
    qi = jnp.arange(W, dtype=jnp.int32)[:, None] + W
    kj = jnp.arange(2 * W, dtype=jnp.int32)[None, :]
    dist = qi - kj
    max_exact = REL_BUCKETS // 2
    d = jnp.maximum(dist, 0)
    log_ratio = jnp.log(jnp.maximum(d, 1).astype(F32) / max_exact) / math.log(REL_MAX_DIST / max_exact)
    large = jnp.minimum(max_exact + (log_ratio * (REL_BUCKETS - max_exact)).astype(jnp.int32), REL_BUCKETS - 1)
    bucket = jnp.where(d < max_exact, d, large)
    bucket = bucket.reshape(-1)
    onehot = (bucket[None, :] == jnp.arange(REL_BUCKETS, dtype=jnp.int32)[:, None]).astype(F32)
    bias = jnp.dot(rel_bias.astype(F32).T, onehot, precision=lax.Precision.HIGHEST)
    return bias.reshape(SWA_Q_HEADS, W, 2 * W), bucket


def attn_out(oa, ob, oc, gn, wout, gpost, x):
    T = x.shape[0]
    tm = _tile(T, 512)

    def body(oa_ref, ob_ref, oc_ref, gn_ref, w_ref, gp_ref, x_ref, x2_ref, y_ref, mT_ref):
        g = gn_ref[...]
        mixed = jnp.concatenate([_rms_fwd(oa_ref[...], g[:, 0:512]), _rms_fwd(ob_ref[...], g[:, 512:768]),
                                 _rms_fwd(oc_ref[...], g[:, 768:1024])], axis=1)
        mT_ref[...] = mixed.T.astype(MXU)
        y = _dot(mixed, w_ref[...])
        y_ref[...] = y
        x2_ref[...] = x_ref[...] + _rms_fwd(y, gp_ref[...])

    row = lambda i: (i, 0)
    const = lambda i: (0, 0)
    return pl.pallas_call(
        body, name="attn_out", grid=(T // tm,),
        in_specs=[pl.BlockSpec((tm, 512), row), pl.BlockSpec((tm, 256), row), pl.BlockSpec((tm, 256), row),
                  pl.BlockSpec((1, 1024), const), pl.BlockSpec((1024, 1024), const), pl.BlockSpec((1, 1024), const),
                  pl.BlockSpec((tm, 1024), row)],
        out_specs=[pl.BlockSpec((tm, 1024), row), pl.BlockSpec((tm, 1024), row),
                   pl.BlockSpec((1024, tm), lambda i: (0, i))],
        out_shape=[jax.ShapeDtypeStruct((T, 1024), F32), jax.ShapeDtypeStruct((T, 1024), F32),
                   jax.ShapeDtypeStruct((1024, T), MXU)],
        compiler_params=_cparams(("parallel",)),
    )(oa, ob, oc, gn, wout, gpost, x)


def attn_out_bwd(dx2, y, oa, ob, oc, gn, wout, gpost):
    T = dx2.shape[0]
    tm = _tile(T, 512)

    def body(dx_ref, y_ref, oa_ref, ob_ref, oc_ref, gn_ref, w_ref, gp_ref,
             dy_ref, da_ref, db_ref, dc_ref, dgn_ref, dgp_ref):
        first = pl.program_id(0) == 0
        dy, dgp = _rms_bwd(dx_ref[...], y_ref[...], gp_ref[...])
        dy_ref[...] = dy.astype(MXU)
        _acc_out(dgp_ref, dgp, first)
        dm = _dot_nt(dy, w_ref[...])
        g = gn_ref[...]
        da, dga = _rms_bwd(dm[:, 0:512], oa_ref[...], g[:, 0:512])
        db, dgb = _rms_bwd(dm[:, 512:768], ob_ref[...], g[:, 512:768])
        dc, dgc = _rms_bwd(dm[:, 768:1024], oc_ref[...], g[:, 768:1024])
        da_ref[...] = da
        db_ref[...] = db
        dc_ref[...] = dc
        _acc_out(dgn_ref, jnp.concatenate([dga, dgb, dgc], axis=1), first)

    row = lambda i: (i, 0)
    const = lambda i: (0, 0)
    return pl.pallas_call(
        body, name="attn_out_bwd", grid=(T // tm,),
        in_specs=[pl.BlockSpec((tm, 1024), row), pl.BlockSpec((tm, 1024), row),
                  pl.BlockSpec((tm, 512), row), pl.BlockSpec((tm, 256), row), pl.BlockSpec((tm, 256), row),
                  pl.BlockSpec((1, 1024), const), pl.BlockSpec((1024, 1024), const), pl.BlockSpec((1, 1024), const)],
        out_specs=[pl.BlockSpec((tm, 1024), row), pl.BlockSpec((tm, 512), row), pl.BlockSpec((tm, 256), row),
                   pl.BlockSpec((tm, 256), row), pl.BlockSpec((1, 1024), const), pl.BlockSpec((1, 1024), const)],
        out_shape=[jax.ShapeDtypeStruct((T, 1024), MXU), jax.ShapeDtypeStruct((T, 512), F32),
                   jax.ShapeDtypeStruct((T, 256), F32), jax.ShapeDtypeStruct((T, 256), F32),
                   jax.ShapeDtypeStruct((1, 1024), F32), jax.ShapeDtypeStruct((1, 1024), F32)],
        compiler_params=_cparams(("arbitrary",)),
    )(dx2, y, oa, ob, oc, gn, wout, gpost)


FF_TILE = 256
_GELU_C = math.sqrt(2.0 / math.pi)


def _gelu(x):
    return 0.5 * x * (1.0 + jnp.tanh(_GELU_C * (x + 0.044715 * x * x * x)))


def _gelu_with_grad(x):
    x2 = x * x
    t = jnp.tanh(_GELU_C * x * (1.0 + 0.044715 * x2))
    h = 0.5 * (1.0 + t)
    return x * h, h + (0.5 * _GELU_C) * x * (1.0 - t * t) * (1.0 + (3 * 0.044715) * x2)


def _conv_taps(u, hal_ref, first):
    row = lax.broadcasted_iota(jnp.int32, (8, 1), 0)
    h6 = jnp.where(first, 0.0, hal_ref[6:7, :])
    h7 = jnp.where(first, 0.0, hal_ref[7:8, :])
    r1, r2 = pltpu.roll(u, 1, 0), pltpu.roll(u, 2, 0)
    r1 = jnp.concatenate([jnp.where(row == 0, h7, r1[0:8]), r1[8:]], axis=0)
    r2 = jnp.concatenate([jnp.where(row == 0, h6, jnp.where(row == 1, h7, r2[0:8])), r2[8:]], axis=0)
    return r1, r2


def ffn_fwd(u0, convw, convb, wdown, gpost, x2, exch=None):
    T = x2.shape[0]
    tm, tn = _tile(T, 1024), FF_TILE
    nj = D_FF // tn

    def body(ug_ref, uu_ref, hg_ref, hu_ref, wg_ref, wu_ref, bg_ref, bu_ref, wd_ref, gp_ref, x_ref, wdp_ref,
             x3_ref, y_ref, aT_ref, acc, a_sc):
        i, j = pl.program_id(0), pl.program_id(1)
        first = i == 0

        @pl.when(j == 0)
        def _():
            acc[...] = jnp.zeros_like(acc)
            a_sc[...] = jnp.zeros_like(a_sc)

        acc[...] += _dot(a_sc[...], wdp_ref[...])

        def conv(u_ref, h_ref, w_ref, b_ref):
            u = u_ref[...]
            r1, r2 = _conv_taps(u, h_ref, first)
            return b_ref[...] + w_ref[0:1, :] * r2 + w_ref[1:2, :] * r1 + w_ref[2:3, :] * u

        a = _gelu(conv(ug_ref, hg_ref, wg_ref, bg_ref)) * conv(uu_ref, hu_ref, wu_ref, bu_ref)
        aT_ref[...] = a.T.astype(MXU)
        a_sc[...] = a.astype(MXU)

        @pl.when(j == nj - 1)
        def _():
            y = acc[...] + _dot(a_sc[...], wd_ref[...])
            y_ref[...] = y
            x3_ref[...] = x_ref[...] + _rms_fwd(y, gp_ref[...])

    halo = lambda off: (lambda i, j: (jnp.maximum(i * (tm // 8) - 1, 0), off + j))
    return _call_hosting(
        body, "ffn_fwd", (T // tm, nj), [u0, u0, u0, u0, convw, convw, convb, convb, wdown, gpost, x2, wdown],
        [pl.BlockSpec((tm, tn), lambda i, j: (i, j)), pl.BlockSpec((tm, tn), lambda i, j: (i, nj + j)),
         pl.BlockSpec((8, tn), halo(0)), pl.BlockSpec((8, tn), halo(nj)),
         pl.BlockSpec((3, tn), lambda i, j: (0, j)), pl.BlockSpec((3, tn), lambda i, j: (0, nj + j)),
         pl.BlockSpec((1, tn), lambda i, j: (0, j)), pl.BlockSpec((1, tn), lambda i, j: (0, nj + j)),
         pl.BlockSpec((tn, 1024), lambda i, j: (j, 0)),
         pl.BlockSpec((1, 1024), lambda i, j: (0, 0)),
         pl.BlockSpec((tm, 1024), lambda i, j: (i, 0)),
         pl.BlockSpec((tn, 1024), lambda i, j: (jnp.maximum(j - 1, 0), 0))],
        [pl.BlockSpec((tm, 1024), lambda i, j: (i, 0)), pl.BlockSpec((tm, 1024), lambda i, j: (i, 0)),
         pl.BlockSpec((tn, tm), lambda i, j: (j, i))],
        [jax.ShapeDtypeStruct((T, 1024), F32), jax.ShapeDtypeStruct((T, 1024), F32),
         jax.ShapeDtypeStruct((D_FF, T), MXU)],
        [pltpu.VMEM((tm, 1024), F32), pltpu.VMEM((tm, tn), MXU)], exch)


def ffn_bwd(dx3, y, u0, convw, convb, wdown, gpost, wupT, x2, gfpre, exch=None):
    T = dx3.shape[0]
    tm, tn = _tile(T, 512), FF_TILE
    nj = D_FF // tn
    ni = T // tm

    def body(dx_ref, y_ref, ug_ref, uu_ref, hg_ref, hu_ref, wg_ref, wu_ref, bg_ref, bu_ref, wd_ref, gp_ref,
             wtg_ref, wtu_ref, x2_ref, gf_ref, wdn_ref, wtgp_ref, wtup_ref,
             dy_ref, dug_ref, duu_ref, dcg_ref, dcu_ref, dgp_ref, dx2_ref, dgf_ref,
             dy_sc, dh_sc, da_sc, dug_sc, duu_sc, cg, cu, ag, au):
        s, j = pl.program_id(0), pl.program_id(1)
        i = ni - 1 - s
        first_tok = i == 0
        sub = lax.broadcasted_iota(jnp.int32, (8, 1), 0)
        slot = j % 2

        @pl.when(j == 0)
        def _():
            dy, dgp = _rms_bwd(dx_ref[...], y_ref[...], gp_ref[...])
            dy_sc[...] = dy.astype(MXU)
            dy_ref[...] = dy.astype(MXU)
            _acc_out(dgp_ref, dgp, s == 0)
            dh_sc[...] = jnp.zeros_like(dh_sc)
            da_sc[0] = _dot_nt(dy.astype(MXU), wd_ref[...])
            dug_sc[...] = jnp.zeros_like(dug_sc)
            duu_sc[...] = jnp.zeros_like(duu_sc)

        @pl.when(s == 0)
        def _():
            cg[j] = jnp.zeros((8, tn), F32)
            cu[j] = jnp.zeros((8, tn), F32)
            ag[j] = jnp.zeros((8, tn), F32)
            au[j] = jnp.zeros((8, tn), F32)

        da = da_sc[slot]
        da_sc[1 - slot] = _dot_nt(dy_sc[...], wdn_ref[...])
        dh_sc[...] += _dot(dug_sc[...], wtgp_ref[...]) + _dot(duu_sc[...], wtup_ref[...])

        def conv(u_ref, h_ref, w_ref, b_ref):
            u = u_ref[...]
            r1, r2 = _conv_taps(u, h_ref, first_tok)
            return b_ref[...] + w_ref[0:1, :] * r2 + w_ref[1:2, :] * r1 + w_ref[2:3, :] * u, u, r1, r2

        gate, ugv, g1, g2 = conv(ug_ref, hg_ref, wg_ref, bg_ref)
        up, uuv, u1, u2 = conv(uu_ref, hu_ref, wu_ref, bu_ref)
        gl, dgl = _gelu_with_grad(gate)
        dup = da * gl
        dgate = da * up * dgl

        def conv_bwd(du, u, r1, r2, w_ref, c_ref, a_ref, duT_ref, du_sc):
            nxt = c_ref[j]
            n0, n1 = nxt[0:1, :], nxt[1:2, :]
            f1, f2 = pltpu.roll(du, tm - 1, 0), pltpu.roll(du, tm - 2, 0)
            f1 = jnp.concatenate([f1[:tm - 8], jnp.where(sub == 7, n0, f1[tm - 8:])], axis=0)
            f2 = jnp.concatenate([f2[:tm - 8], jnp.where(sub == 7, n1, jnp.where(sub == 6, n0, f2[tm - 8:]))], axis=0)
            du0 = w_ref[2:3, :] * du + w_ref[1:2, :] * f1 + w_ref[0:1, :] * f2
            duT_ref[...] = du0.T.astype(MXU)
            du_sc[...] = du0.astype(MXU)
            c_ref[j] = du[0:8, :]
            red = lambda v: jnp.sum(v, axis=0, keepdims=True)
            part = jnp.where(sub == 0, red(du * r2), jnp.where(sub == 1, red(du * r1), jnp.where(
                sub == 2, red(du * u), jnp.where(sub == 3, red(du), 0.0))))
            a_ref[j] = a_ref[j] + part
            return a_ref[j]

        dcg_ref[0] = conv_bwd(dgate, ugv, g1, g2, wg_ref, cg, ag, dug_ref, dug_sc)
        dcu_ref[0] = conv_bwd(dup, uuv, u1, u2, wu_ref, cu, au, duu_ref, duu_sc)

        @pl.when(j == nj - 1)
        def _():
            dh = dh_sc[...] + _dot(dug_sc[...], wtg_ref[...]) + _dot(duu_sc[...], wtu_ref[...])
            dx, dgf = _rms_bwd(dh, x2_ref[...], gf_ref[...])
            dx2_ref[...] = dx_ref[...] + dx
            _acc_out(dgf_ref, dgf, s == 0)

    rev = lambda s: ni - 1 - s
    halo = lambda off: (lambda s, j: (jnp.maximum(rev(s) * (tm // 8) - 1, 0), off + j))
    tok = pl.BlockSpec((tm, 1024), lambda s, j: (rev(s), 0))
    vec = pl.BlockSpec((1, 1024), lambda s, j: (0, 0))
    return _call_hosting(
        body, "ffn_bwd", (ni, nj),
        [dx3, y, u0, u0, u0, u0, convw, convw, convb, convb, wdown, gpost, wupT, wupT, x2, gfpre,
         wdown, wupT, wupT],
        [tok, tok,
         pl.BlockSpec((tm, tn), lambda s, j: (rev(s), j)), pl.BlockSpec((tm, tn), lambda s, j: (rev(s), nj + j)),
         pl.BlockSpec((8, tn), halo(0)), pl.BlockSpec((8, tn), halo(nj)),
         pl.BlockSpec((3, tn), lambda s, j: (0, j)), pl.BlockSpec((3, tn), lambda s, j: (0, nj + j)),
         pl.BlockSpec((1, tn), lambda s, j: (0, j)), pl.BlockSpec((1, tn), lambda s, j: (0, nj + j)),
         pl.BlockSpec((tn, 1024), lambda s, j: (j, 0)), vec,
         pl.BlockSpec((tn, 1024), lambda s, j: (j, 0)), pl.BlockSpec((tn, 1024), lambda s, j: (nj + j, 0)),
         tok, vec,
         pl.BlockSpec((tn, 1024), lambda s, j: (jnp.minimum(j + 1, nj - 1), 0)),
         pl.BlockSpec((tn, 1024), lambda s, j: (jnp.maximum(j - 1, 0), 0)),
         pl.BlockSpec((tn, 1024), lambda s, j: (nj + jnp.maximum(j - 1, 0), 0))],
        [tok,
         pl.BlockSpec((tn, tm), lambda s, j: (j, rev(s))), pl.BlockSpec((tn, tm), lambda s, j: (j, rev(s))),
         pl.BlockSpec((1, 8, tn), lambda s, j: (s, 0, j)), pl.BlockSpec((1, 8, tn), lambda s, j: (s, 0, j)),
         vec, tok, vec],
        [jax.ShapeDtypeStruct((T, 1024), MXU), jax.ShapeDtypeStruct((D_FF, T), MXU),
         jax.ShapeDtypeStruct((D_FF, T), MXU),
         jax.ShapeDtypeStruct((ni, 8, D_FF), F32), jax.ShapeDtypeStruct((ni, 8, D_FF), F32),
         jax.ShapeDtypeStruct((1, 1024), F32), jax.ShapeDtypeStruct((T, 1024), F32),
         jax.ShapeDtypeStruct((1, 1024), F32)],
        [pltpu.VMEM((tm, 1024), MXU), pltpu.VMEM((tm, 1024), F32), pltpu.VMEM((2, tm, tn), F32),
         pltpu.VMEM((tm, tn), MXU), pltpu.VMEM((tm, tn), MXU)] + [pltpu.VMEM((nj, 8, tn), F32)] * 4, exch)


ELEMS_PER_BLOCK = 512 * 1024


def _row_block(R, C):
    if R * C <= ELEMS_PER_BLOCK or R % 8:
        return R
    best = 8
    for t in range(8, R + 1, 8):
        if R % t == 0 and t * C <= ELEMS_PER_BLOCK:
            best = t
    return best


def adamw(w, g, m, v, name, exch=None):
    L, R, C = w.shape
    partials = isinstance(g, (list, tuple))
    tr = _row_block(R, 2 * C)
    c1 = 1.0 - ADAM_B1 ** ADAM_STEP
    c2 = 1.0 - ADAM_B2 ** ADAM_STEP

    def body(w_ref, *rest):
        g_refs, (m_ref, v_ref, g_out, d_ref, nm_ref, nv_ref) = rest[:-6], rest[-6:]

        def step(gv):
            g_out[0] = gv
            nm = ADAM_B1 * m_ref[0] + (1.0 - ADAM_B1) * gv
            nv = ADAM_B2 * v_ref[0] + (1.0 - ADAM_B2) * (gv * gv)
            nm_ref[0] = nm
            nv_ref[0] = nv
            d_ref[0] = -ADAM_LR * ((nm / c1) / (jnp.sqrt(nv / c2) + ADAM_EPS) + ADAM_WD * w_ref[0])

        if not partials:
            step(g_refs[0][0])
            return
        for k in range(L):
            @pl.when(pl.program_id(0) == k)
            def _(k=k):
                gv = g_refs[k][0].astype(F32)
                for d in range(1, N_DEV):
                    gv = gv + g_refs[k][d].astype(F32)
                step(gv)

    spec = pl.BlockSpec((1, tr, C), lambda l, i: (l, i, 0))
    if partials:
        gspecs = [pl.BlockSpec((N_DEV, tr, C), lambda l, i, k=k: (0, jnp.where(l == k, i, 0), 0)) for k in range(L)]
        gs = list(g)
    else:
        gspecs, gs = [spec], [g]
    return _call_hosting(body, name, (L, R // tr), [w] + gs + [m, v], [spec] + gspecs + [spec, spec], [spec] * 4,
                         [jax.ShapeDtypeStruct((L, R, C), F32)] * 4, [], exch)


def sum_devices(buf, name):
    _, R, C = buf.shape
    tr = _row_block(R, C * 4)

    def body(b_ref, o_ref):
        acc = b_ref[0].astype(F32)
        for d in range(1, N_DEV):
            acc = acc + b_ref[d].astype(F32)
        o_ref[...] = acc

    return pl.pallas_call(
        body, name=name, grid=(R // tr,),
        in_specs=[pl.BlockSpec((N_DEV, tr, C), lambda i: (0, i, 0))],
        out_specs=pl.BlockSpec((tr, C), lambda i: (i, 0)),
        out_shape=jax.ShapeDtypeStruct((R, C), F32),
        compiler_params=_cparams(("parallel",)),
    )(buf)


def _exchange_copies(src_refs, out_refs, send_sems, recv_sems, gather):
    x, y, c = lax.axis_index("x"), lax.axis_index("y"), lax.axis_index("c")
    me = 4 * x + 2 * y + c
    flip = lambda a, bit: 1 - a if bit else a
    part = lambda ref, d: ref if gather else ref.at[d]
    copies = []
    for k in range(1, N_DEV):
        px, py, pc = flip(x, (k >> 2) & 1), flip(y, (k >> 1) & 1), flip(c, k & 1)
        peer = 4 * px + 2 * py + pc
        for t in range(len(src_refs)):
            sem = t * (N_DEV - 1) + k - 1
            mk = lambda s, d: pltpu.make_async_remote_copy(
                src_ref=s, dst_ref=d, send_sem=send_sems.at[sem], recv_sem=recv_sems.at[sem],
                device_id=(px, py, pc), device_id_type=pl.DeviceIdType.MESH)
            copies.append((mk(part(src_refs[t], peer), out_refs[t].at[me]),
                           mk(part(src_refs[t], me), out_refs[t].at[peer])))
    return me, copies


def exchange(srcs, name, gather):
    n = len(srcs)
    shapes = [(N_DEV,) + s.shape if gather else s.shape for s in srcs]

    def body(*refs):
        src_refs, out_refs = refs[:n], refs[n:2 * n]
        send_sems, recv_sems, local_sems = refs[2 * n:]
        me, copies = _exchange_copies(src_refs, out_refs, send_sems, recv_sems, gather)
        for outgoing, _ in copies:
            outgoing.start()
        mine = [pltpu.make_async_copy(src_refs[t] if gather else src_refs[t].at[me], out_refs[t].at[me],
                                      local_sems.at[t]) for t in range(n)]
        for cp in mine:
            cp.start()
        for _, incoming in copies:
            incoming.wait_recv()
        for outgoing, _ in copies:
            outgoing.wait_send()
        for cp in mine:
            cp.wait()

    return pl.pallas_call(
        body, name=name,
        in_specs=[pl.BlockSpec(memory_space=pl.ANY)] * n, out_specs=[pl.BlockSpec(memory_space=pl.ANY)] * n,
        out_shape=[jax.ShapeDtypeStruct(shp, s.dtype) for shp, s in zip(shapes, srcs)],
        scratch_shapes=[pltpu.SemaphoreType.DMA((n * (N_DEV - 1),)), pltpu.SemaphoreType.DMA((n * (N_DEV - 1),)),
                        pltpu.SemaphoreType.DMA((n,))],
    )(*srcs)


def hosted_exchange(body, n_in, n_out, n_scratch, grid, srcs, gather):
    n = len(srcs)
    shapes = [(N_DEV,) + s.shape if gather else s.shape for s in srcs]

    def wrapped(*refs):
        ins, xin = refs[:n_in], refs[n_in:n_in + n]
        outs = refs[n_in + n:n_in + n + n_out]
        xout = refs[n_in + n + n_out:n_in + 2 * n + n_out]
        rest = refs[n_in + 2 * n + n_out:]
        scratch, (send_sems, recv_sems, local_sems) = rest[:n_scratch], rest[n_scratch:]
        ids = [pl.program_id(a) for a in range(len(grid))]
        first = functools.reduce(jnp.logical_and, [i == 0 for i in ids])
        last = functools.reduce(jnp.logical_and, [i == g - 1 for i, g in zip(ids, grid)])
        me, copies = _exchange_copies(xin, xout, send_sems, recv_sems, gather)
        mine = [pltpu.make_async_copy(xin[t] if gather else xin[t].at[me], xout[t].at[me], local_sems.at[t])
                for t in range(n)]

        @pl.when(first)
        def _():
            for outgoing, _ in copies:
                outgoing.start()
            for cp in mine:
                cp.start()

        body(*ins, *outs, *scratch)

        @pl.when(last)
        def _():
            for _, incoming in copies:
                incoming.wait_recv()
            for outgoing, _ in copies:
                outgoing.wait_send()
            for cp in mine:
                cp.wait()

    any_spec = pl.BlockSpec(memory_space=pl.ANY)
    return wrapped, (list(srcs), [any_spec] * n, [any_spec] * n,
                     [jax.ShapeDtypeStruct(shp, s.dtype) for shp, s in zip(shapes, srcs)],
                     [pltpu.SemaphoreType.DMA((n * (N_DEV - 1),)), pltpu.SemaphoreType.DMA((n * (N_DEV - 1),)),
                      pltpu.SemaphoreType.DMA((n,))])


def _pack(parts, cols, row_align, dtype):
    flat = jnp.concatenate([p.astype(dtype) for p in parts], axis=-1)
    n = flat.shape[-1]
    block = cols * row_align
    total = -(-n // block) * block
    flat = jnp.pad(flat, [(0, 0)] * (flat.ndim - 1) + [(0, total - n)])
    return flat.reshape(flat.shape[:-1] + (total // cols, cols))


def _unpack(buf, shapes):
    lead = buf.shape[:-2]
    flat = buf.reshape(lead + (-1,))
    out, off = [], 0
    for s in shapes:
        n = int(np.prod(s))
        out.append(flat[..., off:off + n].reshape(lead + tuple(s)))
        off += n
    return out


SHARDED = ["w_in", "w_uq", "w_ukv", "w_out", "w_up", "w_down"]
ATTN_SENT = ["w_in_p", "w_uq", "w_ukv", "w_out"]
UP_HALF = 352
FFN_SIDE = ["w_upT", "conv_w", "w_down"]


def _full_from_shards(name, s):
    if name in ("w_in", "w_in_p", "w_out", "w_down", "w_upT"):
        return s.reshape((-1, s.shape[-1]))
    return s.transpose(1, 0, 2).reshape((s.shape[1], -1))


def _shards_from_full(name, f):
    if name in ("w_in", "w_in_p", "w_out", "w_down", "w_upT"):
        return f.reshape((N_DEV, -1, f.shape[-1]))
    return f.reshape((f.shape[0], N_DEV, -1)).transpose(1, 0, 2)


def _perm_w_in(w):
    z = lambda n: jnp.zeros(w.shape[:-1] + (n,), w.dtype)
    return jnp.concatenate([w[..., :1536], w[..., 1540:1924], w[..., 1536:1540], z(60), w[..., 1924:1956], z(32)],
                           axis=-1)


def _unperm_w_in(d):
    return jnp.concatenate([d[..., :1536], d[..., 1920:1924], d[..., 1536:1920], d[..., 1984:2016]], axis=-1)


def _perm_w_uq(w):
    return jnp.pad(w.reshape(256, 4, MLA_QK_DIM), ((0, 0), (0, 0), (0, 128 - MLA_QK_DIM))).reshape(256, 512)


def _unperm_w_uq(d):
    return d.reshape(256, 4, 128)[:, :, :MLA_QK_DIM].reshape(256, 4 * MLA_QK_DIM)


def _perm_w_ukv(w):
    w4 = w.reshape(128, 4, 128)
    k = jnp.pad(w4[:, :, :64], ((0, 0), (0, 0), (0, 64))).reshape(128, 512)
    return jnp.concatenate([k, w4[:, :, 64:].reshape(128, 256)], axis=1)


def _unperm_w_ukv(d):
    dk = d[:, :512].reshape(128, 4, 128)[:, :, :64]
    dv = d[:, 512:].reshape(128, 4, 64)
    return jnp.concatenate([dk, dv], axis=-1).reshape(128, 512)


def _row(v, width=None):
    v = v.reshape(1, -1).astype(F32)
    if width is not None and v.shape[1] < width:
        v = jnp.pad(v, ((0, 0), (0, width - v.shape[1])))
    return v


def _layer_fwd(x, P, shared, send=None, ffn_from=None):
    cosr, sinr, bias = shared
    ex = lambda host: (send[host], True) if send is not None and send.get(host) else None
    proj, hT, projb = norm_matmul(x, P["g_pre"], P["w_in_p"], "in_proj", lo_tiles=2, tn_pref=1024)
    qm, km, vm, cqT, ckvT = mla_prep(proj, P["gq"], P["gkv"], P["w_uq_p"], P["w_ukv_p"], cosr, sinr)
    fcol, frow, frep = fox_gate(proj, P["fbias"])
    (oa, lse_a), got_swa = swa_fwd(proj, bias, P["sinks"], exch=ex("swa"))
    (ob, lrb), got_fox = flash_fwd(projb, projb, projb, frep, frow, qblk=C_QF // 128, kblk=C_KF // 128,
                                   vblk=C_VF // 128, nq=1, scale=HEAD_DIM ** -0.5, name="fox_fwd", exch=ex("fox"))
    (oc, lrc), got_mla = flash_fwd(qm, km, vm, None, None, qblk=0, kblk=0, vblk=0, nq=2,
                                   scale=MLA_QK_DIM ** -0.5, name="mla_fwd", exch=ex("mla"))
    x2, y1, mT = attn_out(oa, ob, oc, P["gn"], P["w_out"], P["g_apost"], x)
    if ffn_from is not None:
        P = dict(P, **ffn_from(got_swa, got_fox, got_mla))
    u0, h2 = norm_matmul(x2, P["g_fpre"], P["w_upT"], "up_proj", tn_pref=1536, w_transposed=True,
                         h_transposed=False)
    (x3, y2, aT), got_ffn = ffn_fwd(u0, P["conv_w"], P["conv_b"], P["w_down"], P["g_fpost"], x2, exch=ex("ffn"))
    S = dict(x=x, proj=proj, projb=projb, hT=hT, qm=qm, km=km, vm=vm, cqT=cqT, ckvT=ckvT, fcol=fcol, frow=frow,
             oa=oa, lse_a=lse_a, ob=ob, lrb=lrb, oc=oc, lrc=lrc,
             x2=x2, y1=y1, mT=mT, u0=u0, h2=h2, y2=y2, aT=aT)
    return x3, S, P, got_ffn


def _layer_bwd(dx3, P, S, shared, send_attn=None):
    cosr, sinr, bias = shared
    proj = S["proj"]
    G = {}
    got = {}
    ex = lambda arrays: (arrays, False) if send_attn is not None and arrays else None
    (dy2, dugT, duuT, dcg, dcu, G["ffn_post_norm"], dx2, G["ffn_pre_norm"]), got["ffn"] = ffn_bwd(
        dx3, S["y2"], S["u0"], P["conv_w"], P["conv_b"], P["w_down"], P["g_fpost"], P["w_upT"], S["x2"], P["g_fpre"],
        exch=ex(send_attn))
    dconv = jnp.concatenate([dcg[-1], dcu[-1]], axis=1)
    G["conv_w"], G["conv_b"] = dconv[0:3], dconv[3]
    G["w_down"] = matmul_nn(S["aT"], dy2, "dw_down", MXU)
    G["w_upT"] = jnp.concatenate([matmul_nn(dugT, S["h2"], "dw_up_gate", MXU),
                                  matmul_nn(duuT, S["h2"], "dw_up_up", MXU)], axis=0)
    G["w_up"] = G["w_upT"].T
    dy1, doa, dob, doc, G["group_norm"], G["attn_post_norm"] = attn_out_bwd(
        dx2, S["y1"], S["oa"], S["ob"], S["oc"], P["gn"], P["w_out"], P["g_apost"])
    G["w_out"] = matmul_nn(S["mT"], dy1, "dw_out", MXU)
    up_slices = _shards_from_full("w_upT", G["w_upT"])
    (dqa, dka, dva, dbias, dsk), got["swa"] = swa_bwd(proj, bias, P["sinks"], doa, S["oa"], S["lse_a"],
                                                      exch=ex([_shards_from_full("w_down", G["w_down"])]))
    G["swa_sinks"] = dsk[:, 0]
    pb = S["projb"]
    (dqf, dkf, dvf, dFk, dFq), got["fox"] = flash_bwd(
        pb, pb, pb, dob, S["ob"], S["lrb"], S["fcol"], S["frow"], name="fox_bwd", qblk=C_QF // 128,
        kblk=C_KF // 128, vblk=C_VF // 128, nq=1, scale=HEAD_DIM ** -0.5, exch=ex([up_slices[:, :UP_HALF]]))
    dmisc_f, dfb = fox_gate_bwd(dFq, dFk, proj, P["fbias"])
    G["forget_bias"] = dfb[0, 0:4]
    (dqm_, dkm_, dvm_), got["mla"] = flash_bwd(
        S["qm"], S["km"], S["vm"], doc, S["oc"], S["lrc"], None, None, name="mla_bwd",
        qblk=0, kblk=0, vblk=0, nq=2, scale=MLA_QK_DIM ** -0.5, exch=ex([up_slices[:, UP_HALF:]]))
    dqm, dkv, dcq, dckv, dmisc_r, G["q_latent_norm"], G["kv_latent_norm"] = mla_prep_bwd(
        dqm_, dkm_, dvm_, proj, P["gq"], P["gkv"], P["w_uq_p"], P["w_ukv_p"], cosr, sinr)
    G["w_uq"] = _unperm_w_uq(matmul_nn(S["cqT"], dqm, "dw_uq", MXU))
    G["w_ukv"] = _unperm_w_ukv(matmul_nn(S["ckvT"], dkv, "dw_ukv", MXU))
    dproj = jnp.concatenate([dqa, dka, dva, dqf, dkf, dvf, dcq, dckv, dmisc_f + dmisc_r], axis=1).astype(MXU)
    G["w_in_p"] = matmul_nn(S["hT"], dproj, "dw_in", MXU)
    G["w_in"] = _unperm_w_in(G["w_in_p"])
    dx, G["attn_pre_norm"] = matmul_nt_normbwd(dproj, P["w_in_p"], S["x"], P["g_pre"], dx2, "in_bwd")
    return dx, G, dbias, got


def _layer_params(l, full, small):
    return dict(
        g_pre=_row(small["attn_pre_norm"][l]),
        w_in_p=full["w_in_p"] if "w_in_p" in full else _perm_w_in(full["w_in"]),
        gq=_row(small["q_latent_norm"][l]), gkv=_row(small["kv_latent_norm"][l]),
        w_uq_p=_perm_w_uq(full["w_uq"]), w_ukv_p=_perm_w_ukv(full["w_ukv"]),
        fbias=_row(small["forget_bias"][l], 128), sinks=small["swa_sinks"][l].astype(F32),
        gn=_row(small["group_norm"][l]), w_out=full["w_out"], g_apost=_row(small["attn_post_norm"][l]),
        g_fpre=_row(small["ffn_pre_norm"][l]), conv_b=_row(small["conv_b"][l]),
        g_fpost=_row(small["ffn_post_norm"][l]),
        **{n: full[n] for n in FFN_SIDE if n in full},
        **({"w_upT": full["w_up"].T} if "w_up" in full else {}))


def _rel_bias_grad(dbias, bucket):
    flat = dbias.reshape(SWA_Q_HEADS, -1)
    hi = flat.astype(MXU)
    lo = (flat - hi.astype(F32)).astype(MXU)
    onehot = (bucket[:, None] == jnp.arange(128, dtype=jnp.int32)[None, :]).astype(MXU)
    r = matmul_nn(jnp.concatenate([hi, lo], axis=0), onehot, "rel_bias_grad")
    return (r[0:8] + r[8:16])[:, :REL_BUCKETS].T


def local_step(x, tgt, fulls, small, comm=None):
    T = x.shape[0]
    cosr, sinr = rope_tables(T)
    bias, bucket = swa_bias_table(small["rel_bias"])
    shared = (cosr, sinr, bias)
    Ps, Ss = [], []
    h, full = x, fulls[0]
    for l in range(DEPTH):
        P = _layer_params(l, full, small)
        if comm:
            h, S, P, got = _layer_fwd(h, P, shared, comm["weight_parts"](l), comm["ffn_from"])
            full = comm["attn_from"](got) if l + 1 < DEPTH else None
        else:
            h, S, P, _ = _layer_fwd(h, P, shared)
            full = fulls[l + 1] if l + 1 < DEPTH else None
        Ps.append(P)
        Ss.append(S)
    dh, sq = loss_kernel(h, tgt)
    grads = [None] * DEPTH
    dbias_sum = None
    pending = [] if comm else None
    for l in reversed(range(DEPTH)):
        dh, grads[l], dbias, got = _layer_bwd(dh, Ps[l], Ss[l], shared, pending)
        dbias_sum = dbias if dbias_sum is None else dbias_sum + dbias
        if comm:
            comm["landed"](l, ["w_down"], got["swa"])
            comm["landed"](l, ["w_upT"], [jnp.concatenate([got["fox"][0], got["mla"][0]], axis=1)])
            if pending:
                comm["landed"](l + 1, ATTN_SENT, got["ffn"])
            pending = [_shards_from_full(n, grads[l][n]) for n in ATTN_SENT]
    return sq, dh, grads, _rel_bias_grad(dbias_sum, bucket), pending


WEIGHTS = ['attn_pre_norm', 'w_in', 'forget_bias', 'swa_sinks', 'rel_bias', 'q_latent_norm', 'w_uq',
           'kv_latent_norm', 'w_ukv', 'group_norm', 'w_out', 'attn_post_norm', 'ffn_pre_norm', 'w_up', 'conv_w',
           'conv_b', 'w_down', 'ffn_post_norm']
SMALL_PER_LAYER = ['attn_pre_norm', 'forget_bias', 'swa_sinks', 'q_latent_norm', 'kv_latent_norm', 'group_norm',
                   'attn_post_norm', 'ffn_pre_norm', 'conv_b', 'ffn_post_norm', 'conv_w']


def kernel(x, attn_pre_norm, w_in, forget_bias, swa_sinks, rel_bias, q_latent_norm, w_uq, kv_latent_norm, w_ukv, group_norm, w_out, attn_post_norm, ffn_pre_norm, w_up, conv_w, conv_b, w_down, ffn_post_norm, loss_target, m_attn_pre_norm, m_w_in, m_forget_bias, m_swa_sinks, m_rel_bias, m_q_latent_norm, m_w_uq, m_kv_latent_norm, m_w_ukv, m_group_norm, m_w_out, m_attn_post_norm, m_ffn_pre_norm, m_w_up, m_conv_w, m_conv_b, m_w_down, m_ffn_post_norm, v_attn_pre_norm, v_w_in, v_forget_bias, v_swa_sinks, v_rel_bias, v_q_latent_norm, v_w_uq, v_kv_latent_norm, v_w_ukv, v_group_norm, v_w_out, v_attn_post_norm, v_ffn_pre_norm, v_w_up, v_conv_w, v_conv_b, v_w_down, v_ffn_post_norm):
    W = dict(attn_pre_norm=attn_pre_norm, w_in=w_in, forget_bias=forget_bias, swa_sinks=swa_sinks, rel_bias=rel_bias,
             q_latent_norm=q_latent_norm, w_uq=w_uq, kv_latent_norm=kv_latent_norm, w_ukv=w_ukv,
             group_norm=group_norm, w_out=w_out, attn_post_norm=attn_post_norm, ffn_pre_norm=ffn_pre_norm,
             w_up=w_up, conv_w=conv_w, conv_b=conv_b, w_down=w_down, ffn_post_norm=ffn_post_norm)
    M = dict(attn_pre_norm=m_attn_pre_norm, w_in=m_w_in, forget_bias=m_forget_bias, swa_sinks=m_swa_sinks,
             rel_bias=m_rel_bias, q_latent_norm=m_q_latent_norm, w_uq=m_w_uq, kv_latent_norm=m_kv_latent_norm,
             w_ukv=m_w_ukv, group_norm=m_group_norm, w_out=m_w_out, attn_post_norm=m_attn_post_norm,
             ffn_pre_norm=m_ffn_pre_norm, w_up=m_w_up, conv_w=m_conv_w, conv_b=m_conv_b, w_down=m_w_down,
             ffn_post_norm=m_ffn_post_norm)
    V = dict(attn_pre_norm=v_attn_pre_norm, w_in=v_w_in, forget_bias=v_forget_bias, swa_sinks=v_swa_sinks,
             rel_bias=v_rel_bias, q_latent_norm=v_q_latent_norm, w_uq=v_w_uq, kv_latent_norm=v_kv_latent_norm,
             w_ukv=v_w_ukv, group_norm=v_group_norm, w_out=v_w_out, attn_post_norm=v_attn_post_norm,
             ffn_pre_norm=v_ffn_pre_norm, w_up=v_w_up, conv_w=v_conv_w, conv_b=v_conv_b, w_down=v_w_down,
             ffn_post_norm=v_ffn_post_norm)
    me = 4 * lax.axis_index("x") + 2 * lax.axis_index("y") + lax.axis_index("c")

    def attn_shards(l):
        return [_perm_w_in(w_in[l].astype(MXU))] + [W[n][l].astype(MXU) for n in ATTN_SENT[1:]]

    def weight_parts(l):
        up = jnp.swapaxes(W["w_up"][l], 0, 1).astype(MXU)
        return dict(swa=[W["w_down"][l].astype(MXU)], fox=[up[:UP_HALF]], mla=[up[UP_HALF:], conv_w[l]],
                    ffn=attn_shards(l + 1) if l + 1 < DEPTH else [])

    def ffn_from(got_swa, got_fox, got_mla):
        return dict(w_down=_full_from_shards("w_down", got_swa[0]),
                    w_upT=_full_from_shards("w_upT", jnp.concatenate([got_fox[0], got_mla[0]], axis=1)),
                    conv_w=got_mla[1].transpose(1, 0, 2).reshape(3, 2 * D_FF))

    def attn_from(got_ffn):
        return {n: _full_from_shards(n, s) for n, s in zip(ATTN_SENT, got_ffn)}

    landed = [{} for _ in range(DEPTH)]

    def on_landed(l, names, arrays):
        landed[l].update(zip(names, arrays))

    comm = dict(weight_parts=weight_parts, ffn_from=ffn_from, attn_from=attn_from, landed=on_landed)
    full0 = dict(zip(ATTN_SENT, map(_full_from_shards, ATTN_SENT, exchange(attn_shards(0), "gather_weights", True))))
    sq, dx, grads, drel, last = local_step(x[0], loss_target[0], [full0], W, comm)
    G, delta, new_m, new_v = {}, {}, {}, {}

    def update(n, exch=None):
        shp = W[n].shape
        if n == "w_up":
            v3 = lambda a: jnp.swapaxes(a, 1, 2)
            back = v3
            g = [landed[l]["w_upT"] for l in range(DEPTH)]
        else:
            v3 = lambda a: a.reshape(shp if len(shp) == 3 else (1,) + shp)
            back = lambda a: a.reshape(shp)
            g = [landed[l][n] for l in range(DEPTH)] if n in SHARDED else v3(G[n])
        (g, d, nm, nv), got = adamw(v3(W[n]), g, v3(M[n]), v3(V[n]), "adamw_" + n, exch)
        G[n], delta[n], new_m[n], new_v[n] = back(g), back(d), back(nm), back(nv)
        return got

    parts, shapes = [], []
    for l in range(DEPTH):
        for n in SMALL_PER_LAYER:
            parts.append(grads[l][n].astype(F32).reshape(-1))
            shapes.append(grads[l][n].shape)
    parts += [drel.reshape(-1), jnp.sum(sq).reshape(1) * (0.5 / D_MODEL)]
    shapes += [drel.shape, (1,)]
    on_landed(0, ATTN_SENT, update("w_up", (last, False)))
    for l in range(DEPTH):
        landed[l]["w_in"] = _unperm_w_in(landed[l]["w_in_p"])
    gathered = update("w_down", ([_pack(parts, 128, 8, F32)], True))[0]
    red = _unpack(sum_devices(gathered, "sum_small"), shapes)
    k = 0
    per = {n: [] for n in SMALL_PER_LAYER}
    for l in range(DEPTH):
        for n in SMALL_PER_LAYER:
            per[n].append(red[k])
            k += 1
    for n in SMALL_PER_LAYER:
        G[n] = jnp.stack(per[n]).reshape((DEPTH, 3, 2 * D_FF) if n == "conv_w" else W[n].shape)
    G["rel_bias"] = red[k]
    loss = red[k + 1][0]
    G["conv_w"] = lax.dynamic_slice_in_dim(G["conv_w"], me * 704, 704, axis=2)

    for n in WEIGHTS:
        if n not in ("w_up", "w_down"):
            update(n)
    return (loss, dx[None], *[G[n] for n in WEIGHTS], *[delta[n] for n in WEIGHTS],
            *[new_m[n] for n in WEIGHTS], *[new_v[n] for n in WEIGHTS])
```

```python
import functools
import math

import numpy as np
import jax
import jax.numpy as jnp
from jax import lax
from jax.experimental import pallas as pl
from jax.experimental.pallas import tpu as pltpu

F32 = jnp.float32
MXU = jnp.bfloat16

N_DEV = 8
DEPTH = 4
D_MODEL = 1024
HEAD_DIM = 64
WINDOW = 128
SWA_Q_HEADS = 8
REL_BUCKETS = 32
REL_MAX_DIST = 128
MLA_QK_DIM = 96
ROPE_DIM = 32
ROPE_THETA = 10000.0
D_FF = 2816
EPS = 1e-6
NEG = -1e30
IN_COLS = 1956
C_QA, C_KA, C_VA = 0, 512, 640
C_QF, C_KF, C_VF = 768, 1024, 1280
C_CQ, C_CKV, C_MISC = 1536, 1792, 1920
ROPE_LANE0 = 64
ADAM_LR, ADAM_B1, ADAM_B2, ADAM_EPS, ADAM_WD, ADAM_STEP = 0.001, 0.9, 0.999, 1e-08, 0.01, 10

VMEM_LIMIT = 56 * 1024 * 1024


def _cparams(sem=None):
    return pltpu.CompilerParams(dimension_semantics=sem, vmem_limit_bytes=VMEM_LIMIT)


def _tile(n, pref):
    if n <= pref:
        return n
    t = pref - pref % 128
    while t >= 128:
        if n % t == 0:
            return t
        t -= 128
    return n


def _dot(a, b):
    return jnp.dot(a.astype(MXU), b.astype(MXU), preferred_element_type=F32)


def _dot_nt(a, b):
    return lax.dot_general(a.astype(MXU), b.astype(MXU), (((1,), (1,)), ((), ())),
                           preferred_element_type=F32)


def _rms_fwd(x, g):
    return x * lax.rsqrt(jnp.mean(x * x, axis=-1, keepdims=True) + EPS) * g


def _rms_bwd(dy, x, g, n=None):
    r = lax.rsqrt(jnp.mean(x * x, axis=-1, keepdims=True) + EPS)
    xh = x * r
    dg = jnp.sum(dy * xh, axis=0, keepdims=True)
    dxh = dy * g
    dx = r * (dxh - xh * jnp.mean(dxh * xh, axis=-1, keepdims=True))
    return dx, dg


def _acc_out(ref, val, first):
    @pl.when(first)
    def _():
        ref[...] = val

    @pl.when(jnp.logical_not(first))
    def _():
        ref[...] += val


def norm_matmul(x, g, w, name, lo_tiles=0, tn_pref=512, w_transposed=False, h_transposed=True):
    T, K = x.shape
    N = w.shape[0] if w_transposed else w.shape[1]
    tm, tn = _tile(T, 1024), _tile(N, tn_pref)

    def body(x_ref, g_ref, w_ref, o_ref, hT_ref, *rest):
        h_sc = rest[-1]
        j = pl.program_id(1)

        @pl.when(j == 0)
        def _():
            h = _rms_fwd(x_ref[...], g_ref[...])
            h_sc[...] = h.astype(MXU)
            hT_ref[...] = (h.T if h_transposed else h).astype(MXU)

        r = (_dot_nt if w_transposed else _dot)(h_sc[...], w_ref[...])
        o_ref[...] = r
        if lo_tiles:
            @pl.when(j < lo_tiles)
            def _():
                rest[0][...] = r.astype(MXU)

    h_spec = pl.BlockSpec((K, tm), lambda i, j: (0, i)) if h_transposed else pl.BlockSpec((tm, K), lambda i, j: (i, 0))
    out_specs = [pl.BlockSpec((tm, tn), lambda i, j: (i, j)), h_spec]
    out_shape = [jax.ShapeDtypeStruct((T, N), F32), jax.ShapeDtypeStruct((K, T) if h_transposed else (T, K), MXU)]
    if lo_tiles:
        out_specs.append(pl.BlockSpec((tm, tn), lambda i, j: (i, jnp.minimum(j, lo_tiles - 1))))
        out_shape.append(jax.ShapeDtypeStruct((T, lo_tiles * tn), MXU))
    return pl.pallas_call(
        body, name=name, grid=(T // tm, N // tn),
        in_specs=[pl.BlockSpec((tm, K), lambda i, j: (i, 0)),
                  pl.BlockSpec((1, K), lambda i, j: (0, 0)),
                  pl.BlockSpec((tn, K), lambda i, j: (j, 0)) if w_transposed else
                  pl.BlockSpec((K, tn), lambda i, j: (0, j))],
        out_specs=out_specs, out_shape=out_shape,
        scratch_shapes=[pltpu.VMEM((tm, K), MXU)],
        compiler_params=_cparams(("parallel", "arbitrary")),
    )(x, g, w)


def matmul_nn(a, b, name, out_dtype=F32):
    M, K = a.shape
    N = b.shape[1]
    tm, tn, tk = _tile(M, 1408), _tile(N, 1536), _tile(K, 1024)
    nk = K // tk

    def body(a_ref, b_ref, o_ref, acc):
        k = pl.program_id(2)
        part = _dot(a_ref[...], b_ref[...])
        _acc_out(acc, part, k == 0)

        @pl.when(k == nk - 1)
        def _():
            o_ref[...] = acc[...].astype(out_dtype)

    return pl.pallas_call(
        body, name=name, grid=(M // tm, N // tn, nk),
        in_specs=[pl.BlockSpec((tm, tk), lambda i, j, k: (i, k)),
                  pl.BlockSpec((tk, tn), lambda i, j, k: (k, j))],
        out_specs=pl.BlockSpec((tm, tn), lambda i, j, k: (i, j)),
        out_shape=jax.ShapeDtypeStruct((M, N), out_dtype),
        scratch_shapes=[pltpu.VMEM((tm, tn), F32)],
        compiler_params=_cparams(("parallel", "parallel", "arbitrary")),
    )(a, b)


def matmul_nt_normbwd(dy, w, x, g, dres, name):
    T, N = dy.shape
    K = w.shape[0]
    tm, tn = _tile(T, 1024), _tile(N, 1536)
    nj = N // tn

    def body(dy_ref, w_ref, x_ref, g_ref, dres_ref, dx_ref, dg_ref, acc):
        i, j = pl.program_id(0), pl.program_id(1)
        _acc_out(acc, _dot_nt(dy_ref[...], w_ref[...]), j == 0)

        @pl.when(j == nj - 1)
        def _():
            dx, dg = _rms_bwd(acc[...], x_ref[...], g_ref[...])
            dx_ref[...] = dres_ref[...] + dx
            _acc_out(dg_ref, dg, i == 0)

    return pl.pallas_call(
        body, name=name, grid=(T // tm, nj),
        in_specs=[pl.BlockSpec((tm, tn), lambda i, j: (i, j)),
                  pl.BlockSpec((K, tn), lambda i, j: (0, j)),
                  pl.BlockSpec((tm, K), lambda i, j: (i, 0)),
                  pl.BlockSpec((1, K), lambda i, j: (0, 0)),
                  pl.BlockSpec((tm, K), lambda i, j: (i, 0))],
        out_specs=[pl.BlockSpec((tm, K), lambda i, j: (i, 0)),
                   pl.BlockSpec((1, K), lambda i, j: (0, 0))],
        out_shape=[jax.ShapeDtypeStruct((T, K), F32), jax.ShapeDtypeStruct((1, K), F32)],
        scratch_shapes=[pltpu.VMEM((tm, K), F32)],
        compiler_params=_cparams(("arbitrary", "arbitrary")),
    )(dy, w, x, g, dres)


def loss_kernel(y, tgt):
    T, D = y.shape
    tm = _tile(T, 512)

    def body(y_ref, t_ref, dy_ref, acc_ref):
        e = y_ref[...] - t_ref[...]
        dy_ref[...] = e * (1.0 / D)
        _acc_out(acc_ref, jnp.sum(e * e, axis=0, keepdims=True), pl.program_id(0) == 0)

    return pl.pallas_call(
        body, name="loss", grid=(T // tm,),
        in_specs=[pl.BlockSpec((tm, D), lambda i: (i, 0)), pl.BlockSpec((tm, D), lambda i: (i, 0))],
        out_specs=[pl.BlockSpec((tm, D), lambda i: (i, 0)), pl.BlockSpec((1, D), lambda i: (0, 0))],
        out_shape=[jax.ShapeDtypeStruct((T, D), F32), jax.ShapeDtypeStruct((1, D), F32)],
        compiler_params=_cparams(("arbitrary",)),
    )(y, tgt)


def _rope_partner(x):
    lane = lax.broadcasted_iota(jnp.int32, (1, 128), 1)
    return jnp.where(lane < ROPE_LANE0 + ROPE_DIM // 2, pltpu.roll(x, 128 - ROPE_DIM // 2, 1),
                     pltpu.roll(x, ROPE_DIM // 2, 1))


def _rope_apply(x, cos, sin_signed):
    return x * cos + _rope_partner(x) * sin_signed


def _rope_apply_bwd(dy, cos, sin_signed):
    lane = lax.broadcasted_iota(jnp.int32, (1, 128), 1)
    rotary = (lane >= ROPE_LANE0) & (lane < ROPE_LANE0 + ROPE_DIM)
    return dy * cos + jnp.where(rotary, _rope_partner(dy * sin_signed), 0.0)


def rope_tables(T):
    pos = jnp.arange(T, dtype=F32)
    inv_freq = ROPE_THETA ** (-(jnp.arange(ROPE_DIM // 2, dtype=F32) * 2.0 / ROPE_DIM))
    ang = pos[:, None] * inv_freq[None, :]
    cos, sin = jnp.cos(ang), jnp.sin(ang)
    z = jnp.zeros((T, ROPE_LANE0), F32)
    z2 = jnp.zeros((T, 128 - ROPE_LANE0 - ROPE_DIM), F32)
    cosr = jnp.concatenate([z, cos, cos, z2], axis=1)
    sinr = jnp.concatenate([z, -sin, sin, z2], axis=1)
    return cosr, sinr


def mla_prep(proj, gq, gkv, wuq, wukv, cosr, sinr):
    T = proj.shape[0]
    tm = _tile(T, 512)

    def body(cq_ref, ckv_ref, misc_ref, gq_ref, gkv_ref, wuq_ref, wukv_ref, cos_ref, sin_ref,
             q_ref, k_ref, v_ref, cqT_ref, ckvT_ref):
        lane = lax.broadcasted_iota(jnp.int32, (1, 128), 1)
        cosr_, sinr_ = cos_ref[...], sin_ref[...]
        cosq = cosr_ + jnp.where(lane < ROPE_LANE0, 1.0, 0.0)
        cqn = _rms_fwd(cq_ref[...], gq_ref[...])
        cqT_ref[...] = cqn.T.astype(MXU)
        qm = _dot(cqn, wuq_ref[...])
        q_ref[...] = jnp.concatenate(
            [_rope_apply(qm[:, 128 * h:128 * (h + 1)], cosq, sinr_) for h in range(4)], axis=1).astype(MXU)
        ckvn = _rms_fwd(ckv_ref[...], gkv_ref[...])
        ckvT_ref[...] = ckvn.T.astype(MXU)
        kv = _dot(ckvn, wukv_ref[...])
        kr = _rope_apply(misc_ref[...], cosr_, sinr_)
        k_ref[...] = jnp.concatenate(
            [kv[:, 128 * h:128 * (h + 1)] + kr for h in range(4)], axis=1).astype(MXU)
        v_ref[...] = kv[:, 512:768].astype(MXU)

    row = lambda i: (i, 0)
    const = lambda i: (0, 0)
    return pl.pallas_call(
        body, name="mla_prep", grid=(T // tm,),
        in_specs=[pl.BlockSpec((tm, 256), lambda i: (i, C_CQ // 256)),
                  pl.BlockSpec((tm, 128), lambda i: (i, C_CKV // 128)),
                  pl.BlockSpec((tm, 128), lambda i: (i, C_MISC // 128)),
                  pl.BlockSpec((1, 256), const), pl.BlockSpec((1, 128), const),
                  pl.BlockSpec((256, 512), const), pl.BlockSpec((128, 768), const),
                  pl.BlockSpec((tm, 128), row), pl.BlockSpec((tm, 128), row)],
        out_specs=[pl.BlockSpec((tm, 512), row), pl.BlockSpec((tm, 512), row), pl.BlockSpec((tm, 256), row),
                   pl.BlockSpec((256, tm), lambda i: (0, i)), pl.BlockSpec((128, tm), lambda i: (0, i))],
        out_shape=[jax.ShapeDtypeStruct((T, 512), MXU), jax.ShapeDtypeStruct((T, 512), MXU),
                   jax.ShapeDtypeStruct((T, 256), MXU),
                   jax.ShapeDtypeStruct((256, T), MXU), jax.ShapeDtypeStruct((128, T), MXU)],
        compiler_params=_cparams(("parallel",)),
    )(proj, proj, proj, gq, gkv, wuq, wukv, cosr, sinr)


def mla_prep_bwd(dq, dk, dv, proj, gq, gkv, wuq, wukv, cosr, sinr):
    T = proj.shape[0]
    tm = _tile(T, 512)

    def body(dq_ref, dk_ref, dv_ref, cq_ref, ckv_ref, gq_ref, gkv_ref, wuq_ref, wukv_ref, cos_ref, sin_ref,
             dqm_ref, dkv_ref, dcq_ref, dckv_ref, dmisc_ref, dgq_ref, dgkv_ref):
        first = pl.program_id(0) == 0
        lane = lax.broadcasted_iota(jnp.int32, (1, 128), 1)
        cosr_, sinr_ = cos_ref[...], sin_ref[...]
        cosq = cosr_ + jnp.where(lane < ROPE_LANE0, 1.0, 0.0)
        dqv = dq_ref[...]
        dqm = jnp.concatenate(
            [_rope_apply_bwd(dqv[:, 128 * h:128 * (h + 1)], cosq, sinr_) for h in range(4)], axis=1)
        dqm_ref[...] = dqm.astype(MXU)
        dcq, dgq = _rms_bwd(_dot_nt(dqm, wuq_ref[...]), cq_ref[...], gq_ref[...])
        dcq_ref[...] = dcq
        _acc_out(dgq_ref, dgq, first)
        dkv_ = dk_ref[...]
        heads = [dkv_[:, 128 * h:128 * (h + 1)] for h in range(4)]
        dkr = heads[0] + heads[1] + heads[2] + heads[3]
        dmisc_ref[...] = _rope_apply_bwd(dkr, cosr_, sinr_)
        dkvm = jnp.concatenate([jnp.where(lane < ROPE_LANE0, hd, 0.0) for hd in heads] + [dv_ref[...]], axis=1)
        dkv_ref[...] = dkvm.astype(MXU)
        dckv, dgkv = _rms_bwd(_dot_nt(dkvm, wukv_ref[...]), ckv_ref[...], gkv_ref[...])
        dckv_ref[...] = dckv
        _acc_out(dgkv_ref, dgkv, first)

    row = lambda i: (i, 0)
    const = lambda i: (0, 0)
    return pl.pallas_call(
        body, name="mla_prep_bwd", grid=(T // tm,),
        in_specs=[pl.BlockSpec((tm, 512), row), pl.BlockSpec((tm, 512), row), pl.BlockSpec((tm, 256), row),
                  pl.BlockSpec((tm, 256), lambda i: (i, C_CQ // 256)),
                  pl.BlockSpec((tm, 128), lambda i: (i, C_CKV // 128)),
                  pl.BlockSpec((1, 256), const), pl.BlockSpec((1, 128), const),
                  pl.BlockSpec((256, 512), const), pl.BlockSpec((128, 768), const),
                  pl.BlockSpec((tm, 128), row), pl.BlockSpec((tm, 128), row)],
        out_specs=[pl.BlockSpec((tm, 512), row), pl.BlockSpec((tm, 768), row), pl.BlockSpec((tm, 256), row),
                   pl.BlockSpec((tm, 128), row), pl.BlockSpec((tm, 128), row),
                   pl.BlockSpec((1, 256), const), pl.BlockSpec((1, 128), const)],
        out_shape=[jax.ShapeDtypeStruct((T, 512), MXU), jax.ShapeDtypeStruct((T, 768), MXU),
                   jax.ShapeDtypeStruct((T, 256), F32), jax.ShapeDtypeStruct((T, 128), F32),
                   jax.ShapeDtypeStruct((T, 128), F32),
                   jax.ShapeDtypeStruct((1, 256), F32), jax.ShapeDtypeStruct((1, 128), F32)],
        compiler_params=_cparams(("arbitrary",)),
    )(dq, dk, dv, proj, proj, gq, gkv, wuq, wukv, cosr, sinr)


def _split3(x):
    hi = x.astype(MXU)
    r1 = x - hi.astype(F32)
    mid = r1.astype(MXU)
    lo = (r1 - mid.astype(F32)).astype(MXU)
    return hi, mid, lo


def _tri_matmul(tri, x):
    hi, mid, lo = _split3(x)
    d = lambda p: jnp.dot(tri, p, preferred_element_type=F32)
    return d(hi) + d(mid) + d(lo)


def _log_sigmoid(z):
    return jnp.minimum(z, 0.0) - jnp.log(1.0 + jnp.exp(-jnp.abs(z)))


def fox_gate(proj, fbias):
    T = proj.shape[0]
    tb = _tile(T, 512)

    def body(misc_ref, b_ref, fc_ref, fr_ref, frep_ref, carry):
        @pl.when(pl.program_id(0) == 0)
        def _():
            carry[...] = jnp.zeros_like(carry)

        lane = lax.broadcasted_iota(jnp.int32, (1, 128), 1)
        lf = jnp.where(lane < 4, _log_sigmoid(misc_ref[...] + b_ref[...]), 0.0)
        r = lax.broadcasted_iota(jnp.int32, (tb, tb), 0)
        c = lax.broadcasted_iota(jnp.int32, (tb, tb), 1)
        tri = jnp.where(r >= c, 1.0, 0.0).astype(MXU)
        F = _tri_matmul(tri, lf) + carry[...]
        carry[...] = carry[...] + jnp.sum(lf, axis=0, keepdims=True)
        fc_ref[0] = F
        fc_ref[1] = pltpu.roll(F, 126, 1)
        ft = F.T[0:8, :]
        fr_ref[0] = ft
        fr_ref[1] = pltpu.roll(ft, 6, 0)
        for h in range(4):
            frep_ref[h] = jnp.broadcast_to(_lane_pick(F, h), (tb, 128))

    return pl.pallas_call(
        body, name="fox_gate", grid=(T // tb,),
        in_specs=[pl.BlockSpec((tb, 128), lambda i: (i, C_MISC // 128)), pl.BlockSpec((1, 128), lambda i: (0, 0))],
        out_specs=[pl.BlockSpec((2, tb, 128), lambda i: (0, i, 0)), pl.BlockSpec((2, 8, tb), lambda i: (0, 0, i)),
                   pl.BlockSpec((4, tb, 128), lambda i: (0, i, 0))],
        out_shape=[jax.ShapeDtypeStruct((2, T, 128), F32), jax.ShapeDtypeStruct((2, 8, T), F32),
                   jax.ShapeDtypeStruct((4, T, 128), F32)],
        scratch_shapes=[pltpu.VMEM((1, 128), F32)],
        compiler_params=_cparams(("arbitrary",)),
    )(proj, fbias)


def fox_gate_bwd(dFq, dFk, proj, fbias):
    T = proj.shape[0]
    tb = _tile(T, 512)
    nb = T // tb

    def body(dq_ref, dk_ref, misc_ref, b_ref, dm_ref, db_ref, carry):
        first = pl.program_id(0) == 0

        @pl.when(first)
        def _():
            carry[...] = jnp.zeros_like(carry)

        lane = lax.broadcasted_iota(jnp.int32, (1, 128), 1)
        dF = jnp.where(lane < 4, (dq_ref[0] + dk_ref[0]) + pltpu.roll(dq_ref[1] + dk_ref[1], 2, 1), 0.0)
        r = lax.broadcasted_iota(jnp.int32, (tb, tb), 0)
        c = lax.broadcasted_iota(jnp.int32, (tb, tb), 1)
        tri = jnp.where(r <= c, 1.0, 0.0).astype(MXU)
        dlf = _tri_matmul(tri, dF) + carry[...]
        carry[...] = carry[...] + jnp.sum(dF, axis=0, keepdims=True)
        z = misc_ref[...] + b_ref[...]
        dz = jnp.where(lane < 4, dlf * (1.0 / (1.0 + jnp.exp(z))), 0.0)
        dm_ref[...] = dz
        _acc_out(db_ref, jnp.sum(dz, axis=0, keepdims=True), first)

    return pl.pallas_call(
        body, name="fox_gate_bwd", grid=(nb,),
        in_specs=[pl.BlockSpec((2, tb, 128), lambda i: (0, nb - 1 - i, 0)),
                  pl.BlockSpec((2, tb, 128), lambda i: (0, nb - 1 - i, 0)),
                  pl.BlockSpec((tb, 128), lambda i: (nb - 1 - i, C_MISC // 128)),
                  pl.BlockSpec((1, 128), lambda i: (0, 0))],
        out_specs=[pl.BlockSpec((tb, 128), lambda i: (nb - 1 - i, 0)), pl.BlockSpec((1, 128), lambda i: (0, 0))],
        out_shape=[jax.ShapeDtypeStruct((T, 128), F32), jax.ShapeDtypeStruct((1, 128), F32)],
        scratch_shapes=[pltpu.VMEM((1, 128), F32)],
        compiler_params=_cparams(("arbitrary",)),
    )(dFq, dFk, proj, fbias)


FLASH_TILE = 512


def _row_stat_tile(a, b, n):
    at = jnp.broadcast_to(a, (n, 128)).T[0:8, :]
    bt = jnp.broadcast_to(b, (n, 128)).T[0:8, :]
    sub = lax.broadcasted_iota(jnp.int32, (8, 1), 0)
    return jnp.where(sub == 0, at, jnp.where(sub == 1, bt, 0.0))


def _col_stat_tile(a, b):
    lane = lax.broadcasted_iota(jnp.int32, (1, 128), 1)
    return jnp.where(lane == 0, a, jnp.where(lane == 1, b, 0.0))


def _lane_pick(x, h):
    lane = lax.broadcasted_iota(jnp.int32, (1, 128), 1)
    return jnp.sum(jnp.where(lane == h, x, 0.0), axis=1, keepdims=True)


def _half_mask(h):
    lane = lax.broadcasted_iota(jnp.int32, (1, 128), 1)
    return (lane // HEAD_DIM) == h


def _call_hosting(body, name, grid, args, in_specs, out_specs, out_shape, scratch, exch):
    n_out = len(out_shape)
    if exch is not None:
        body, (xargs, xin, xout, xshape, xscratch) = hosted_exchange(
            body, len(args), n_out, len(scratch), grid, *exch)
        args, in_specs, out_specs = args + xargs, in_specs + xin, out_specs + xout
        out_shape, scratch = out_shape + xshape, scratch + xscratch
    res = pl.pallas_call(
        body, name=name, grid=grid, in_specs=in_specs, out_specs=out_specs, out_shape=out_shape,
        scratch_shapes=scratch, compiler_params=_cparams(("arbitrary",) * len(grid)),
    )(*args)
    return res[:n_out], res[n_out:]


def flash_fwd(q, k, v, frep, frow, *, qblk, kblk, vblk, nq, scale, name, exch=None):
    T = q.shape[0]
    tk = _tile(T, FLASH_TILE)
    tq = _tile(T, 2 * FLASH_TILE)
    per_q = tq // tk
    wq = 128 * nq
    has_f = frep is not None

    def body(*refs):
        if has_f:
            q_ref, k_ref, v_ref, fk_ref, fr_ref, o_ref, lr_ref, vT_sc, m_sc, acc_sc = refs
        else:
            q_ref, k_ref, v_ref, o_ref, lr_ref, vT_sc, m_sc, acc_sc = refs
        i = pl.program_id(1)

        @pl.when(i == 0)
        def _():
            vT_sc[...] = v_ref[...].astype(F32).T.astype(MXU)

        key_row = lax.broadcasted_iota(jnp.int32, (tk, 1), 0)
        q_col = lax.broadcasted_iota(jnp.int32, (1, tq), 1)
        row_half = lax.broadcasted_iota(jnp.int32, (128, 1), 0) // HEAD_DIM
        qb = q_ref[...].astype(F32) * scale
        if nq == 1:
            qhs = [jnp.where(_half_mask(h), qb, 0).astype(MXU) for h in range(2)]
        else:
            qhs = [qb[:, 128 * h:128 * (h + 1)].astype(MXU) for h in range(2)]
        for h in range(2):
            m_sc[h] = jnp.full((1, tq), NEG, F32)
            acc_sc[h] = jnp.zeros((128, tq), F32)

        def make_step(diag_block):
            def step(j, carry):
                off = pl.multiple_of(j * tk, tk)
                ks = k_ref[pl.ds(off, tk), :]
                vT = vT_sc[:, pl.ds(off, tk)]
                for h in range(2):
                    kh = ks if nq == 1 else ks[:, 128 * h:128 * (h + 1)]
                    sT = _dot_nt(kh, qhs[h])
                    if has_f:
                        fk = fk_ref[h, pl.ds(off, tk), :]
                        sT = sT + (fr_ref[0, h:h + 1, :] - jnp.concatenate([fk] * (tq // 128), axis=1))
                    if diag_block is not None:
                        sT = jnp.where(key_row + diag_block * tk <= q_col, sT, NEG)
                    m_prev = m_sc[h]
                    m_new = jnp.maximum(m_prev, jnp.max(sT, axis=0, keepdims=True))
                    alpha = jnp.exp(m_prev - m_new)
                    pT = jnp.exp(sT - m_new)
                    vTh = jnp.where(row_half == h, vT, jnp.ones_like(vT))
                    acc_sc[h] = alpha * acc_sc[h] + _dot(vTh, pT)
                    m_sc[h] = m_new
                return carry
            return step

        lax.fori_loop(0, per_q * i, make_step(None), 0)
        for d in range(per_q):
            make_step(d)(per_q * i + d, 0)
        outs, lses = [], []
        for h in range(2):
            acc = acc_sc[h]
            outs.append(acc / pltpu.roll(acc, HEAD_DIM, 0))
            l = acc_sc[h, HEAD_DIM * (1 - h):HEAD_DIM * (1 - h) + 1, :]
            lses.append(m_sc[h] + jnp.log(l))
        o_ref[...] = jnp.where(row_half == 0, outs[0], outs[1]).T
        sub = lax.broadcasted_iota(jnp.int32, (8, 1), 0)
        lr_ref[0] = jnp.where(sub == 0, lses[0], jnp.where(sub == 1, lses[1], 0.0))

    in_specs = [pl.BlockSpec((tq, wq), lambda p, i: (i, qblk + p)),
                pl.BlockSpec((T, wq), lambda p, i: (0, kblk + p)),
                pl.BlockSpec((T, 128), lambda p, i: (0, vblk + p))]
    args = [q, k, v]
    if has_f:
        in_specs += [pl.BlockSpec((2, T, 128), lambda p, i: (p, 0, 0)),
                     pl.BlockSpec((1, 8, tq), lambda p, i: (p, 0, i))]
        args += [frep, frow]
    out_specs = [pl.BlockSpec((tq, 128), lambda p, i: (i, p)), pl.BlockSpec((1, 8, tq), lambda p, i: (p, 0, i))]
    out_shape = [jax.ShapeDtypeStruct((T, 256), F32), jax.ShapeDtypeStruct((2, 8, T), F32)]
    scratch = [pltpu.VMEM((128, T), MXU), pltpu.VMEM((2, 1, tq), F32), pltpu.VMEM((2, 128, tq), F32)]
    return _call_hosting(body, name, (2, T // tq), args, in_specs, out_specs, out_shape, scratch, exch)


def flash_bwd(q, k, v, do, o, lrow, fcol, frow, *, qblk, kblk, vblk, nq, scale, name, exch=None):
    T = q.shape[0]
    tq = tk = _tile(T, FLASH_TILE)
    wq = 128 * nq
    nqb = T // tq
    has_f = fcol is not None

    def body(*refs):
        if has_f:
            (q_ref, k_ref, v_ref, do_ref, o_ref, lr_ref, fc_ref, fr_ref,
             dq_ref, dk_ref, dv_ref, df_ref, dfq_ref, dk_sc, dv_sc, dqT_sc, d_sc, df_sc, dfq_sc) = refs
        else:
            q_ref, k_ref, v_ref, do_ref, o_ref, lr_ref, dq_ref, dk_ref, dv_ref, dk_sc, dv_sc, dqT_sc, d_sc = refs
        j = pl.program_id(1)
        diag = lax.broadcasted_iota(jnp.int32, (tk, 1), 0) <= lax.broadcasted_iota(jnp.int32, (1, tq), 1)
        hms = [_half_mask(h) for h in range(2)]

        @pl.when(j == 0)
        def _():
            dqT_sc[...] = jnp.zeros_like(dqT_sc)
            if has_f:
                dfq_sc[...] = jnp.zeros_like(dfq_sc)

            def delta(b, carry):
                off = pl.multiple_of(b * tq, tq)
                prod = do_ref[pl.ds(off, tq), :] * o_ref[pl.ds(off, tq), :]
                Ds = [jnp.sum(jnp.where(hms[h], prod, 0.0), axis=1, keepdims=True) for h in range(2)]
                d_sc[:, pl.ds(off, tq)] = _row_stat_tile(Ds[0], Ds[1], tq)
                return carry

            lax.fori_loop(0, nqb, delta, 0)

        kb = k_ref[...]
        vb = v_ref[...]
        if nq == 1:
            khs = [jnp.where(hms[h], kb, 0).astype(MXU) for h in range(2)]
        else:
            khs = [kb[:, 128 * h:128 * (h + 1)].astype(MXU) for h in range(2)]
        kTs = [kh.astype(F32).T.astype(MXU) for kh in khs]
        kss = [(kh.astype(F32) * scale).astype(MXU) for kh in khs]
        vhs = [jnp.where(hms[h], vb, 0).astype(MXU) for h in range(2)]
        fks = [_lane_pick(fc_ref[0], h) for h in range(2)] if has_f else None
        dv_sc[...] = jnp.zeros_like(dv_sc)
        dk_sc[...] = jnp.zeros_like(dk_sc)
        if has_f:
            df_sc[...] = jnp.zeros_like(df_sc)

        def make_step(masked):
            def step(i, carry):
                off = pl.multiple_of(i * tq, tq)
                qs = q_ref[pl.ds(off, tq), :]
                dos = do_ref[pl.ds(off, tq), :]
                for h in range(2):
                    qh = qs if nq == 1 else qs[:, 128 * h:128 * (h + 1)]
                    sT = _dot_nt(kss[h], qh)
                    if has_f:
                        sT = sT + (fr_ref[0, h:h + 1, pl.ds(off, tq)] - fks[h])
                    pT = jnp.exp(sT - lr_ref[0, h:h + 1, pl.ds(off, tq)])
                    if masked:
                        pT = jnp.where(diag, pT, 0.0)
                    dsT = pT * (_dot_nt(vhs[h], dos) - d_sc[h:h + 1, pl.ds(off, tq)])
                    dv_sc[...] += _dot(pT, jnp.where(hms[h], dos, 0))
                    qq = jnp.where(hms[h], qs, 0) if nq == 1 else qh
                    dk_sc[h if nq == 2 else 0] += _dot(dsT, qq)
                    dqT_sc[h if nq == 2 else 0, :, pl.ds(off, tq)] += _dot(kTs[h], dsT)
                    if has_f:
                        part = dsT[:, 0:128]
                        for c in range(1, tq // 128):
                            part = part + dsT[:, 128 * c:128 * (c + 1)]
                        df_sc[h] += part
                        dfq_sc[h:h + 1, pl.ds(off, tq)] += jnp.sum(dsT, axis=0, keepdims=True)
                return carry
            return step

        make_step(True)(j, 0)
        lax.fori_loop(j + 1, nqb, make_step(False), 0)
        if nq == 1:
            dk_ref[...] = dk_sc[0] * scale
        else:
            dk_ref[...] = jnp.concatenate([dk_sc[0], dk_sc[1]], axis=1) * scale
        dv_ref[...] = dv_sc[...]
        if has_f:
            df_ref[0] = _col_stat_tile(-jnp.sum(df_sc[0], axis=1, keepdims=True),
                                       -jnp.sum(df_sc[1], axis=1, keepdims=True))

        @pl.when(j == nqb - 1)
        def _():
            if nq == 1:
                dq_ref[...] = dqT_sc[0].T * scale
            else:
                dq_ref[...] = jnp.concatenate([dqT_sc[0].T, dqT_sc[1].T], axis=1) * scale
            if has_f:
                sub = lax.broadcasted_iota(jnp.int32, (128, 1), 0)
                rows = jnp.where(sub == 0, dfq_sc[0:1, :], jnp.where(sub == 1, dfq_sc[1:2, :], 0.0))
                dfq_ref[0] = rows.T

    in_specs = [pl.BlockSpec((T, wq), lambda p, j: (0, qblk + p)),
                pl.BlockSpec((tk, wq), lambda p, j: (j, kblk + p)),
                pl.BlockSpec((tk, 128), lambda p, j: (j, vblk + p)),
                pl.BlockSpec((T, 128), lambda p, j: (0, p)),
                pl.BlockSpec((T, 128), lambda p, j: (0, p)),
                pl.BlockSpec((1, 8, T), lambda p, j: (p, 0, 0))]
    args = [q, k, v, do, o, lrow]
    out_specs = [pl.BlockSpec((T, wq), lambda p, j: (0, p)),
                 pl.BlockSpec((tk, wq), lambda p, j: (j, p)), pl.BlockSpec((tk, 128), lambda p, j: (j, p))]
    out_shape = [jax.ShapeDtypeStruct((T, 2 * wq), F32), jax.ShapeDtypeStruct((T, 2 * wq), F32),
                 jax.ShapeDtypeStruct((T, 256), F32)]
    scratch = [pltpu.VMEM((nq, tk, 128), F32), pltpu.VMEM((tk, 128), F32), pltpu.VMEM((nq, 128, T), F32),
               pltpu.VMEM((8, T), F32)]
    if has_f:
        in_specs += [pl.BlockSpec((1, tk, 128), lambda p, j: (p, j, 0)),
                     pl.BlockSpec((1, 8, T), lambda p, j: (p, 0, 0))]
        args += [fcol, frow]
        out_specs += [pl.BlockSpec((1, tk, 128), lambda p, j: (p, j, 0)),
                      pl.BlockSpec((1, T, 128), lambda p, j: (p, 0, 0))]
        out_shape += [jax.ShapeDtypeStruct((2, T, 128), F32), jax.ShapeDtypeStruct((2, T, 128), F32)]
        scratch += [pltpu.VMEM((2, tk, 128), F32), pltpu.VMEM((8, T), F32)]
    return _call_hosting(body, name, (2, T // tk), args, in_specs, out_specs, out_shape, scratch, exch)


def _swa_align(pair, e, h):
    sel = jnp.where(_half_mask(e), pair, 0.0)
    if e == h:
        return sel
    return pltpu.roll(sel, HEAD_DIM, 1)


def _swa_mask(n):
    W = WINDOW
    qi = lax.broadcasted_iota(jnp.int32, (W, 2 * W), 0) + W
    kj = lax.broadcasted_iota(jnp.int32, (W, 2 * W), 1)
    dist = qi - kj
    return (dist >= 0) & (dist < W) & ((n > 0) | (kj >= W))


def swa_fwd(proj, bias, sinks, exch=None):
    T = proj.shape[0]
    W = WINDOW
    nb = T // W
    scale = HEAD_DIM ** -0.5

    def body(sink_ref, q_ref, kp_ref, kc_ref, vp_ref, vc_ref, b_ref, o_ref, l_ref):
        n = pl.program_id(0)
        mask = _swa_mask(n)
        kband = jnp.concatenate([kp_ref[...], kc_ref[...]], axis=0).astype(MXU)
        vband = jnp.concatenate([vp_ref[...], vc_ref[...]], axis=0).astype(MXU)
        lane = lax.broadcasted_iota(jnp.int32, (1, 128), 1)
        lse_tile = jnp.zeros((W, 128), F32)
        pairs = []
        for h in range(2):
            full = []
            for g in range(4):
                hq = 4 * h + g
                qa = _swa_align(q_ref[:, 128 * (hq // 2):128 * (hq // 2 + 1)], hq % 2, h)
                s = _dot_nt(qa, kband) * scale + b_ref[hq]
                s = jnp.where(mask, s, NEG)
                sink = sink_ref[hq]
                m = jnp.maximum(jnp.max(s, axis=1, keepdims=True), sink)
                e = jnp.exp(s - m)
                l = jnp.sum(e, axis=1, keepdims=True) + jnp.exp(sink - m)
                r = jnp.where(_half_mask(h), _dot(e, vband), 0.0) / l
                full.append(r + pltpu.roll(r, HEAD_DIM, 1))
                lse_tile = jnp.where(lane == hq, m + jnp.log(l), lse_tile)
            pairs.append(jnp.where(_half_mask(0), full[0], full[1]))
            pairs.append(jnp.where(_half_mask(0), full[2], full[3]))
        o_ref[...] = jnp.concatenate(pairs, axis=1)
        l_ref[...] = lse_tile

    prev = lambda n: (jnp.maximum(n - 1, 0), C_KA // 128)
    cur = lambda n: (n, C_KA // 128)
    prev_v = lambda n: (jnp.maximum(n - 1, 0), C_VA // 128)
    cur_v = lambda n: (n, C_VA // 128)
    return _call_hosting(
        body, "swa_fwd", (nb,), [sinks, proj, proj, proj, proj, proj, bias],
        [pl.BlockSpec(memory_space=pltpu.SMEM),
         pl.BlockSpec((W, 512), lambda n: (n, 0)),
         pl.BlockSpec((W, 128), prev), pl.BlockSpec((W, 128), cur),
         pl.BlockSpec((W, 128), prev_v), pl.BlockSpec((W, 128), cur_v),
         pl.BlockSpec((8, W, 2 * W), lambda n: (0, 0, 0))],
        [pl.BlockSpec((W, 512), lambda n: (n, 0)), pl.BlockSpec((W, 128), lambda n: (n, 0))],
        [jax.ShapeDtypeStruct((T, 512), F32), jax.ShapeDtypeStruct((T, 128), F32)], [], exch)


def swa_bwd(proj, bias, sinks, do, o, lse, exch=None):
    T = proj.shape[0]
    W = WINDOW
    nb = T // W
    scale = HEAD_DIM ** -0.5

    def body(sink_ref, q_ref, kp_ref, kc_ref, vp_ref, vc_ref, b_ref, do_ref, o_ref, l_ref,
             dq_ref, dk_ref, dv_ref, db_ref, dsk_ref, ck, cv):
        n = pl.program_id(0)

        @pl.when(n == 0)
        def _():
            ck[...] = jnp.zeros_like(ck)
            cv[...] = jnp.zeros_like(cv)
            db_ref[...] = jnp.zeros_like(db_ref)
            dsk_ref[...] = jnp.zeros_like(dsk_ref)

        @pl.when(n < nb)
        def _():
            mask = _swa_mask(n)
            kb32 = jnp.concatenate([kp_ref[...], kc_ref[...]], axis=0)
            vb32 = jnp.concatenate([vp_ref[...], vc_ref[...]], axis=0)
            kband = kb32.astype(MXU)
            sub = lax.broadcasted_iota(jnp.int32, (8, 1), 0)
            dk_band = jnp.zeros((2 * W, 128), F32)
            dv_band = jnp.zeros((2 * W, 128), F32)
            dsk = jnp.zeros((8, 128), F32)
            dq_pairs = []
            mask4 = jnp.concatenate([mask] * 4, axis=0)
            for h in range(2):
                hm = _half_mask(h)
                km = jnp.where(hm, kb32, 0.0).astype(MXU)
                vm = jnp.where(hm, vb32, 0.0).astype(MXU)
                pbs = [slice(128 * ((4 * h + g) // 2), 128 * ((4 * h + g) // 2 + 1)) for g in range(4)]
                q4 = jnp.concatenate([_swa_align(q_ref[:, pbs[g]], g % 2, h) for g in range(4)], axis=0)
                do4 = jnp.concatenate([_swa_align(do_ref[:, pbs[g]], g % 2, h) for g in range(4)], axis=0)
                D4 = jnp.concatenate(
                    [jnp.sum(jnp.where(_half_mask(g % 2), do_ref[:, pbs[g]] * o_ref[:, pbs[g]], 0.0), axis=1,
                             keepdims=True) for g in range(4)], axis=0)
                lse4 = jnp.concatenate([_lane_pick(l_ref[...], 4 * h + g) for g in range(4)], axis=0)
                sink4 = jnp.concatenate([jnp.full((W, 1), sink_ref[4 * h + g], F32) for g in range(4)], axis=0)
                s = _dot_nt(q4, kband) * scale + b_ref[4 * h:4 * h + 4].reshape(4 * W, 2 * W)
                p = jnp.where(mask4, jnp.exp(s - lse4), 0.0)
                sd = jnp.exp(sink4 - lse4) * D4
                for g in range(4):
                    dsk = dsk + jnp.where(sub == 4 * h + g,
                                          -jnp.sum(sd[W * g:W * (g + 1)], axis=0, keepdims=True), 0.0)
                ds = p * (_dot_nt(do4, vm) - D4)
                db_ref[4 * h:4 * h + 4] += ds.reshape(4, W, 2 * W)
                dq = _dot(ds, km) * scale
                dq = dq + pltpu.roll(dq, HEAD_DIM, 1)
                dk_band = dk_band + _dot(ds.T, q4) * scale
                dv_band = dv_band + _dot(p.T, do4)
                dq_pairs.append(jnp.where(_half_mask(0), dq[0:W], dq[W:2 * W]))
                dq_pairs.append(jnp.where(_half_mask(0), dq[2 * W:3 * W], dq[3 * W:4 * W]))
            dq_ref[...] = jnp.concatenate(dq_pairs, axis=1)
            dsk_ref[...] += dsk
            dk_ref[...] = ck[...] + dk_band[0:W]
            dv_ref[...] = cv[...] + dv_band[0:W]
            ck[...] = dk_band[W:2 * W]
            cv[...] = dv_band[W:2 * W]

        @pl.when(n == nb)
        def _():
            dk_ref[...] = ck[...]
            dv_ref[...] = cv[...]

    cl = lambda n: jnp.minimum(n, nb - 1)
    pv = lambda n: jnp.maximum(jnp.minimum(n, nb - 1) - 1, 0)
    return _call_hosting(
        body, "swa_bwd", (nb + 1,), [sinks, proj, proj, proj, proj, proj, bias, do, o, lse],
        [pl.BlockSpec(memory_space=pltpu.SMEM),
         pl.BlockSpec((W, 512), lambda n: (cl(n), 0)),
         pl.BlockSpec((W, 128), lambda n: (pv(n), C_KA // 128)),
         pl.BlockSpec((W, 128), lambda n: (cl(n), C_KA // 128)),
         pl.BlockSpec((W, 128), lambda n: (pv(n), C_VA // 128)),
         pl.BlockSpec((W, 128), lambda n: (cl(n), C_VA // 128)),
         pl.BlockSpec((8, W, 2 * W), lambda n: (0, 0, 0)),
         pl.BlockSpec((W, 512), lambda n: (cl(n), 0)),
         pl.BlockSpec((W, 512), lambda n: (cl(n), 0)),
         pl.BlockSpec((W, 128), lambda n: (cl(n), 0))],
        [pl.BlockSpec((W, 512), lambda n: (cl(n), 0)),
         pl.BlockSpec((W, 128), lambda n: (jnp.maximum(n - 1, 0), 0)),
         pl.BlockSpec((W, 128), lambda n: (jnp.maximum(n - 1, 0), 0)),
         pl.BlockSpec((8, W, 2 * W), lambda n: (0, 0, 0)),
         pl.BlockSpec((8, 128), lambda n: (0, 0))],
        [jax.ShapeDtypeStruct((T, 512), F32), jax.ShapeDtypeStruct((T, 128), F32),
         jax.ShapeDtypeStruct((T, 128), F32), jax.ShapeDtypeStruct((8, W, 2 * W), F32),
         jax.ShapeDtypeStruct((8, 128), F32)],
        [pltpu.VMEM((W, 128), F32), pltpu.VMEM((W, 128), F32)], exch)


def swa_bias_table(rel_bias):
    W = WINDOW
    qi = jnp.arange(W, dtype=jnp.int32)[:, None] + W
    kj = jnp.arange(2 * W, dtype=jnp.int32)[None, :]
    dist = qi - kj
    max_exact = REL_BUCKETS // 2
    d = jnp.maximum(dist, 0)
    log_ratio = jnp.log(jnp.maximum(d, 1).astype(F32) / max_exact) / math.log(REL_MAX_DIST / max_exact)
    large = jnp.minimum(max_exact + (log_ratio * (REL_BUCKETS - max_exact)).astype(jnp.int32), REL_BUCKETS - 1)
    bucket = jnp.where(d < max_exact, d, large)
    bucket = bucket.reshape(-1)
    onehot = (bucket[None, :] == jnp.arange(REL_BUCKETS, dtype=jnp.int32)[:, None]).astype(F32)
    bias = jnp.dot(rel_bias.astype(F32).T, onehot, precision=lax.Precision.HIGHEST)
    return bias.reshape(SWA_Q_HEADS, W, 2 * W), bucket


def attn_out(oa, ob, oc, gn, wout, gpost, x):
    T = x.shape[0]
    tm = _tile(T, 512)

    def body(oa_ref, ob_ref, oc_ref, gn_ref, w_ref, gp_ref, x_ref, x2_ref, y_ref, mT_ref):
        g = gn_ref[...]
        mixed = jnp.concatenate([_rms_fwd(oa_ref[...], g[:, 0:512]), _rms_fwd(ob_ref[...], g[:, 512:768]),
                                 _rms_fwd(oc_ref[...], g[:, 768:1024])], axis=1)
        mT_ref[...] = mixed.T.astype(MXU)
        y = _dot(mixed, w_ref[...])
        y_ref[...] = y
        x2_ref[...] = x_ref[...] + _rms_fwd(y, gp_ref[...])

    row = lambda i: (i, 0)
    const = lambda i: (0, 0)
    return pl.pallas_call(
        body, name="attn_out", grid=(T // tm,),
        in_specs=[pl.BlockSpec((tm, 512), row), pl.BlockSpec((tm, 256), row), pl.BlockSpec((tm, 256), row),
                  pl.BlockSpec((1, 1024), const), pl.BlockSpec((1024, 1024), const), pl.BlockSpec((1, 1024), const),
                  pl.BlockSpec((tm, 1024), row)],
        out_specs=[pl.BlockSpec((tm, 1024), row), pl.BlockSpec((tm, 1024), row),
                   pl.BlockSpec((1024, tm), lambda i: (0, i))],
        out_shape=[jax.ShapeDtypeStruct((T, 1024), F32), jax.ShapeDtypeStruct((T, 1024), F32),
                   jax.ShapeDtypeStruct((1024, T), MXU)],
        compiler_params=_cparams(("parallel",)),
    )(oa, ob, oc, gn, wout, gpost, x)


def attn_out_bwd(dx2, y, oa, ob, oc, gn, wout, gpost):
    T = dx2.shape[0]
    tm = _tile(T, 512)

    def body(dx_ref, y_ref, oa_ref, ob_ref, oc_ref, gn_ref, w_ref, gp_ref,
             dy_ref, da_ref, db_ref, dc_ref, dgn_ref, dgp_ref):
        first = pl.program_id(0) == 0
        dy, dgp = _rms_bwd(dx_ref[...], y_ref[...], gp_ref[...])
        dy_ref[...] = dy.astype(MXU)
        _acc_out(dgp_ref, dgp, first)
        dm = _dot_nt(dy, w_ref[...])
        g = gn_ref[...]
        da, dga = _rms_bwd(dm[:, 0:512], oa_ref[...], g[:, 0:512])
        db, dgb = _rms_bwd(dm[:, 512:768], ob_ref[...], g[:, 512:768])
        dc, dgc = _rms_bwd(dm[:, 768:1024], oc_ref[...], g[:, 768:1024])
        da_ref[...] = da
        db_ref[...] = db
        dc_ref[...] = dc
        _acc_out(dgn_ref, jnp.concatenate([dga, dgb, dgc], axis=1), first)

    row = lambda i: (i, 0)
    const = lambda i: (0, 0)
    return pl.pallas_call(
        body, name="attn_out_bwd", grid=(T // tm,),
        in_specs=[pl.BlockSpec((tm, 1024), row), pl.BlockSpec((tm, 1024), row),
                  pl.BlockSpec((tm, 512), row), pl.BlockSpec((tm, 256), row), pl.BlockSpec((tm, 256), row),
                  pl.BlockSpec((1, 1024), const), pl.BlockSpec((1024, 1024), const), pl.BlockSpec((1, 1024), const)],
        out_specs=[pl.BlockSpec((tm, 1024), row), pl.BlockSpec((tm, 512), row), pl.BlockSpec((tm, 256), row),
                   pl.BlockSpec((tm, 256), row), pl.BlockSpec((1, 1024), const), pl.BlockSpec((1, 1024), const)],
        out_shape=[jax.ShapeDtypeStruct((T, 1024), MXU), jax.ShapeDtypeStruct((T, 512), F32),
                   jax.ShapeDtypeStruct((T, 256), F32), jax.ShapeDtypeStruct((T, 256), F32),
                   jax.ShapeDtypeStruct((1, 1024), F32), jax.ShapeDtypeStruct((1, 1024), F32)],
        compiler_params=_cparams(("arbitrary",)),
    )(dx2, y, oa, ob, oc, gn, wout, gpost)


FF_TILE = 256
_GELU_C = math.sqrt(2.0 / math.pi)


def _gelu(x):
    return 0.5 * x * (1.0 + jnp.tanh(_GELU_C * (x + 0.044715 * x * x * x)))


def _gelu_with_grad(x):
    x2 = x * x
    t = jnp.tanh(_GELU_C * x * (1.0 + 0.044715 * x2))
    h = 0.5 * (1.0 + t)
    return x * h, h + (0.5 * _GELU_C) * x * (1.0 - t * t) * (1.0 + (3 * 0.044715) * x2)


def _conv_taps(u, hal_ref, first):
    row = lax.broadcasted_iota(jnp.int32, (8, 1), 0)
    h6 = jnp.where(first, 0.0, hal_ref[6:7, :])
    h7 = jnp.where(first, 0.0, hal_ref[7:8, :])
    r1, r2 = pltpu.roll(u, 1, 0), pltpu.roll(u, 2, 0)
    r1 = jnp.concatenate([jnp.where(row == 0, h7, r1[0:8]), r1[8:]], axis=0)
    r2 = jnp.concatenate([jnp.where(row == 0, h6, jnp.where(row == 1, h7, r2[0:8])), r2[8:]], axis=0)
    return r1, r2


def ffn_fwd(u0, convw, convb, wdown, gpost, x2, exch=None):
    T = x2.shape[0]
    tm, tn = _tile(T, 1024), FF_TILE
    nj = D_FF // tn

    def body(ug_ref, uu_ref, hg_ref, hu_ref, wg_ref, wu_ref, bg_ref, bu_ref, wd_ref, gp_ref, x_ref, wdp_ref,
             x3_ref, y_ref, aT_ref, gate_ref, up_ref, acc, a_sc):
        i, j = pl.program_id(0), pl.program_id(1)
        first = i == 0

        @pl.when(j == 0)
        def _():
            acc[...] = jnp.zeros_like(acc)
            a_sc[...] = jnp.zeros_like(a_sc)

        acc[...] += _dot(a_sc[...], wdp_ref[...])

        def conv(u_ref, h_ref, w_ref, b_ref):
            u = u_ref[...]
            r1, r2 = _conv_taps(u, h_ref, first)
            return b_ref[...] + w_ref[0:1, :] * r2 + w_ref[1:2, :] * r1 + w_ref[2:3, :] * u

        gate, up = conv(ug_ref, hg_ref, wg_ref, bg_ref), conv(uu_ref, hu_ref, wu_ref, bu_ref)
        gate_ref[...] = gate
        up_ref[...] = up
        a = _gelu(gate) * up
        aT_ref[...] = a.T.astype(MXU)
        a_sc[...] = a.astype(MXU)

        @pl.when(j == nj - 1)
        def _():
            y = acc[...] + _dot(a_sc[...], wd_ref[...])
            y_ref[...] = y
            x3_ref[...] = x_ref[...] + _rms_fwd(y, gp_ref[...])

    halo = lambda off: (lambda i, j: (jnp.maximum(i * (tm // 8) - 1, 0), off + j))
    return _call_hosting(
        body, "ffn_fwd", (T // tm, nj), [u0, u0, u0, u0, convw, convw, convb, convb, wdown, gpost, x2, wdown],
        [pl.BlockSpec((tm, tn), lambda i, j: (i, j)), pl.BlockSpec((tm, tn), lambda i, j: (i, nj + j)),
         pl.BlockSpec((8, tn), halo(0)), pl.BlockSpec((8, tn), halo(nj)),
         pl.BlockSpec((3, tn), lambda i, j: (0, j)), pl.BlockSpec((3, tn), lambda i, j: (0, nj + j)),
         pl.BlockSpec((1, tn), lambda i, j: (0, j)), pl.BlockSpec((1, tn), lambda i, j: (0, nj + j)),
         pl.BlockSpec((tn, 1024), lambda i, j: (j, 0)),
         pl.BlockSpec((1, 1024), lambda i, j: (0, 0)),
         pl.BlockSpec((tm, 1024), lambda i, j: (i, 0)),
         pl.BlockSpec((tn, 1024), lambda i, j: (jnp.maximum(j - 1, 0), 0))],
        [pl.BlockSpec((tm, 1024), lambda i, j: (i, 0)), pl.BlockSpec((tm, 1024), lambda i, j: (i, 0)),
         pl.BlockSpec((tn, tm), lambda i, j: (j, i)),
         pl.BlockSpec((tm, tn), lambda i, j: (i, j)), pl.BlockSpec((tm, tn), lambda i, j: (i, j))],
        [jax.ShapeDtypeStruct((T, 1024), F32), jax.ShapeDtypeStruct((T, 1024), F32),
         jax.ShapeDtypeStruct((D_FF, T), MXU),
         jax.ShapeDtypeStruct((T, D_FF), F32), jax.ShapeDtypeStruct((T, D_FF), F32)],
        [pltpu.VMEM((tm, 1024), F32), pltpu.VMEM((tm, tn), MXU)], exch)


def ffn_bwd(dx3, y, u0, gate, up, convw, wdown, gpost, wupT, x2, gfpre, exch=None):
    T = dx3.shape[0]
    tm, tn = _tile(T, 512), FF_TILE
    nj = D_FF // tn
    ni = T // tm

    def body(dx_ref, y_ref, ug_ref, uu_ref, gate_ref, up_ref, wg_ref, wu_ref, wd_ref, gp_ref,
             wtg_ref, wtu_ref, x2_ref, gf_ref, wdn_ref, wtgp_ref, wtup_ref,
             dy_ref, dug_ref, duu_ref, dcg_ref, dcu_ref, dgp_ref, dx2_ref, dgf_ref,
             dy_sc, dh_sc, da_sc, dug_sc, duu_sc, cg, cu, ag, au):
        s, j = pl.program_id(0), pl.program_id(1)
        sub = lax.broadcasted_iota(jnp.int32, (8, 1), 0)
        slot = j % 2

        @pl.when(j == 0)
        def _():
            dy, dgp = _rms_bwd(dx_ref[...], y_ref[...], gp_ref[...])
            dy_sc[...] = dy.astype(MXU)
            dy_ref[...] = dy.astype(MXU)
            _acc_out(dgp_ref, dgp, s == 0)
            dh_sc[...] = jnp.zeros_like(dh_sc)
            da_sc[0] = _dot_nt(dy.astype(MXU), wd_ref[...])
            dug_sc[...] = jnp.zeros_like(dug_sc)
            duu_sc[...] = jnp.zeros_like(duu_sc)

        @pl.when(s == 0)
        def _():
            cg[j] = jnp.zeros((8, tn), F32)
            cu[j] = jnp.zeros((8, tn), F32)
            ag[j] = jnp.zeros((8, tn), F32)
            au[j] = jnp.zeros((8, tn), F32)

        da = da_sc[slot]
        da_sc[1 - slot] = _dot_nt(dy_sc[...], wdn_ref[...])
        dh_sc[...] += _dot(dug_sc[...], wtgp_ref[...]) + _dot(duu_sc[...], wtup_ref[...])

        gl, dgl = _gelu_with_grad(gate_ref[...])
        dup = da * gl
        dgate = da * up_ref[...] * dgl

        def conv_bwd(du, u_ref, w_ref, c_ref, a_ref, duT_ref, du_sc):
            nxt = c_ref[j]
            n0, n1 = nxt[0:1, :], nxt[1:2, :]
            f1, f2 = pltpu.roll(du, tm - 1, 0), pltpu.roll(du, tm - 2, 0)
            f1 = jnp.concatenate([f1[:tm - 8], jnp.where(sub == 7, n0, f1[tm - 8:])], axis=0)
            f2 = jnp.concatenate([f2[:tm - 8], jnp.where(sub == 7, n1, jnp.where(sub == 6, n0, f2[tm - 8:]))], axis=0)
            du0 = w_ref[2:3, :] * du + w_ref[1:2, :] * f1 + w_ref[0:1, :] * f2
            duT_ref[...] = du0.T.astype(MXU)
            du_sc[...] = du0.astype(MXU)
            c_ref[j] = du[0:8, :]
            red = lambda v: jnp.sum(v, axis=0, keepdims=True)
            u = u_ref[...]
            part = jnp.where(sub == 0, red(f2 * u), jnp.where(sub == 1, red(f1 * u), jnp.where(
                sub == 2, red(du * u), jnp.where(sub == 3, red(du), 0.0))))
            a_ref[j] = a_ref[j] + part
            return a_ref[j]

        dcg_ref[0] = conv_bwd(dgate, ug_ref, wg_ref, cg, ag, dug_ref, dug_sc)
        dcu_ref[0] = conv_bwd(dup, uu_ref, wu_ref, cu, au, duu_ref, duu_sc)

        @pl.when(j == nj - 1)
        def _():
            dh = dh_sc[...] + _dot(dug_sc[...], wtg_ref[...]) + _dot(duu_sc[...], wtu_ref[...])
            dx, dgf = _rms_bwd(dh, x2_ref[...], gf_ref[...])
            dx2_ref[...] = dx_ref[...] + dx
            _acc_out(dgf_ref, dgf, s == 0)

    rev = lambda s: ni - 1 - s
    tok = pl.BlockSpec((tm, 1024), lambda s, j: (rev(s), 0))
    vec = pl.BlockSpec((1, 1024), lambda s, j: (0, 0))
    tile = pl.BlockSpec((tm, tn), lambda s, j: (rev(s), j))
    return _call_hosting(
        body, "ffn_bwd", (ni, nj),
        [dx3, y, u0, u0, gate, up, convw, convw, wdown, gpost, wupT, wupT, x2, gfpre, wdown, wupT, wupT],
        [tok, tok,
         tile, pl.BlockSpec((tm, tn), lambda s, j: (rev(s), nj + j)), tile, tile,
         pl.BlockSpec((3, tn), lambda s, j: (0, j)), pl.BlockSpec((3, tn), lambda s, j: (0, nj + j)),
         pl.BlockSpec((tn, 1024), lambda s, j: (j, 0)), vec,
         pl.BlockSpec((tn, 1024), lambda s, j: (j, 0)), pl.BlockSpec((tn, 1024), lambda s, j: (nj + j, 0)),
         tok, vec,
         pl.BlockSpec((tn, 1024), lambda s, j: (jnp.minimum(j + 1, nj - 1), 0)),
         pl.BlockSpec((tn, 1024), lambda s, j: (jnp.maximum(j - 1, 0), 0)),
         pl.BlockSpec((tn, 1024), lambda s, j: (nj + jnp.maximum(j - 1, 0), 0))],
        [tok,
         pl.BlockSpec((tn, tm), lambda s, j: (j, rev(s))), pl.BlockSpec((tn, tm), lambda s, j: (j, rev(s))),
         pl.BlockSpec((1, 8, tn), lambda s, j: (s, 0, j)), pl.BlockSpec((1, 8, tn), lambda s, j: (s, 0, j)),
         vec, tok, vec],
        [jax.ShapeDtypeStruct((T, 1024), MXU), jax.ShapeDtypeStruct((D_FF, T), MXU),
         jax.ShapeDtypeStruct((D_FF, T), MXU),
         jax.ShapeDtypeStruct((ni, 8, D_FF), F32), jax.ShapeDtypeStruct((ni, 8, D_FF), F32),
         jax.ShapeDtypeStruct((1, 1024), F32), jax.ShapeDtypeStruct((T, 1024), F32),
         jax.ShapeDtypeStruct((1, 1024), F32)],
        [pltpu.VMEM((tm, 1024), MXU), pltpu.VMEM((tm, 1024), F32), pltpu.VMEM((2, tm, tn), F32),
         pltpu.VMEM((tm, tn), MXU), pltpu.VMEM((tm, tn), MXU)] + [pltpu.VMEM((nj, 8, tn), F32)] * 4, exch)


ELEMS_PER_BLOCK = 512 * 1024


def _row_block(R, C):
    if R * C <= ELEMS_PER_BLOCK or R % 8:
        return R
    best = 8
    for t in range(8, R + 1, 8):
        if R % t == 0 and t * C <= ELEMS_PER_BLOCK:
            best = t
    return best


def adamw(w, g, m, v, name, exch=None):
    L, R, C = w.shape
    partials = isinstance(g, (list, tuple))
    tr = _row_block(R, 2 * C)
    c1 = 1.0 - ADAM_B1 ** ADAM_STEP
    c2 = 1.0 - ADAM_B2 ** ADAM_STEP

    def body(w_ref, *rest):
        g_refs, (m_ref, v_ref, g_out, d_ref, nm_ref, nv_ref) = rest[:-6], rest[-6:]

        def step(gv):
            g_out[0] = gv
            nm = ADAM_B1 * m_ref[0] + (1.0 - ADAM_B1) * gv
            nv = ADAM_B2 * v_ref[0] + (1.0 - ADAM_B2) * (gv * gv)
            nm_ref[0] = nm
            nv_ref[0] = nv
            d_ref[0] = -ADAM_LR * ((nm / c1) / (jnp.sqrt(nv / c2) + ADAM_EPS) + ADAM_WD * w_ref[0])

        if not partials:
            step(g_refs[0][0])
            return
        for k in range(L):
            @pl.when(pl.program_id(0) == k)
            def _(k=k):
                gv = g_refs[k][0].astype(F32)
                for d in range(1, N_DEV):
                    gv = gv + g_refs[k][d].astype(F32)
                step(gv)

    spec = pl.BlockSpec((1, tr, C), lambda l, i: (l, i, 0))
    if partials:
        gspecs = [pl.BlockSpec((N_DEV, tr, C), lambda l, i, k=k: (0, jnp.where(l == k, i, 0), 0)) for k in range(L)]
        gs = list(g)
    else:
        gspecs, gs = [spec], [g]
    return _call_hosting(body, name, (L, R // tr), [w] + gs + [m, v], [spec] + gspecs + [spec, spec], [spec] * 4,
                         [jax.ShapeDtypeStruct((L, R, C), F32)] * 4, [], exch)


def sum_devices(buf, name):
    _, R, C = buf.shape
    tr = _row_block(R, C * 4)

    def body(b_ref, o_ref):
        acc = b_ref[0].astype(F32)
        for d in range(1, N_DEV):
            acc = acc + b_ref[d].astype(F32)
        o_ref[...] = acc

    return pl.pallas_call(
        body, name=name, grid=(R // tr,),
        in_specs=[pl.BlockSpec((N_DEV, tr, C), lambda i: (0, i, 0))],
        out_specs=pl.BlockSpec((tr, C), lambda i: (i, 0)),
        out_shape=jax.ShapeDtypeStruct((R, C), F32),
        compiler_params=_cparams(("parallel",)),
    )(buf)


def _exchange_copies(src_refs, out_refs, send_sems, recv_sems, gather):
    x, y, c = lax.axis_index("x"), lax.axis_index("y"), lax.axis_index("c")
    me = 4 * x + 2 * y + c
    flip = lambda a, bit: 1 - a if bit else a
    part = lambda ref, d: ref if gather else ref.at[d]
    copies = []
    for k in range(1, N_DEV):
        px, py, pc = flip(x, (k >> 2) & 1), flip(y, (k >> 1) & 1), flip(c, k & 1)
        peer = 4 * px + 2 * py + pc
        for t in range(len(src_refs)):
            sem = t * (N_DEV - 1) + k - 1
            mk = lambda s, d: pltpu.make_async_remote_copy(
                src_ref=s, dst_ref=d, send_sem=send_sems.at[sem], recv_sem=recv_sems.at[sem],
                device_id=(px, py, pc), device_id_type=pl.DeviceIdType.MESH)
            copies.append((mk(part(src_refs[t], peer), out_refs[t].at[me]),
                           mk(part(src_refs[t], me), out_refs[t].at[peer])))
    return me, copies


def exchange(srcs, name, gather):
    n = len(srcs)
    shapes = [(N_DEV,) + s.shape if gather else s.shape for s in srcs]

    def body(*refs):
        src_refs, out_refs = refs[:n], refs[n:2 * n]
        send_sems, recv_sems, local_sems = refs[2 * n:]
        me, copies = _exchange_copies(src_refs, out_refs, send_sems, recv_sems, gather)
        for outgoing, _ in copies:
            outgoing.start()
        mine = [pltpu.make_async_copy(src_refs[t] if gather else src_refs[t].at[me], out_refs[t].at[me],
                                      local_sems.at[t]) for t in range(n)]
        for cp in mine:
            cp.start()
        for _, incoming in copies:
            incoming.wait_recv()
        for outgoing, _ in copies:
            outgoing.wait_send()
        for cp in mine:
            cp.wait()

    return pl.pallas_call(
        body, name=name,
        in_specs=[pl.BlockSpec(memory_space=pl.ANY)] * n, out_specs=[pl.BlockSpec(memory_space=pl.ANY)] * n,
        out_shape=[jax.ShapeDtypeStruct(shp, s.dtype) for shp, s in zip(shapes, srcs)],
        scratch_shapes=[pltpu.SemaphoreType.DMA((n * (N_DEV - 1),)), pltpu.SemaphoreType.DMA((n * (N_DEV - 1),)),
                        pltpu.SemaphoreType.DMA((n,))],
    )(*srcs)


def hosted_exchange(body, n_in, n_out, n_scratch, grid, srcs, gather):
    n = len(srcs)
    shapes = [(N_DEV,) + s.shape if gather else s.shape for s in srcs]

    def wrapped(*refs):
        ins, xin = refs[:n_in], refs[n_in:n_in + n]
        outs = refs[n_in + n:n_in + n + n_out]
        xout = refs[n_in + n + n_out:n_in + 2 * n + n_out]
        rest = refs[n_in + 2 * n + n_out:]
        scratch, (send_sems, recv_sems, local_sems) = rest[:n_scratch], rest[n_scratch:]
        ids = [pl.program_id(a) for a in range(len(grid))]
        first = functools.reduce(jnp.logical_and, [i == 0 for i in ids])
        last = functools.reduce(jnp.logical_and, [i == g - 1 for i, g in zip(ids, grid)])
        me, copies = _exchange_copies(xin, xout, send_sems, recv_sems, gather)
        mine = [pltpu.make_async_copy(xin[t] if gather else xin[t].at[me], xout[t].at[me], local_sems.at[t])
                for t in range(n)]

        @pl.when(first)
        def _():
            for outgoing, _ in copies:
                outgoing.start()
            for cp in mine:
                cp.start()

        body(*ins, *outs, *scratch)

        @pl.when(last)
        def _():
            for _, incoming in copies:
                incoming.wait_recv()
            for outgoing, _ in copies:
                outgoing.wait_send()
            for cp in mine:
                cp.wait()

    any_spec = pl.BlockSpec(memory_space=pl.ANY)
    return wrapped, (list(srcs), [any_spec] * n, [any_spec] * n,
                     [jax.ShapeDtypeStruct(shp, s.dtype) for shp, s in zip(shapes, srcs)],
                     [pltpu.SemaphoreType.DMA((n * (N_DEV - 1),)), pltpu.SemaphoreType.DMA((n * (N_DEV - 1),)),
                      pltpu.SemaphoreType.DMA((n,))])


def _pack(parts, cols, row_align, dtype):
    flat = jnp.concatenate([p.astype(dtype) for p in parts], axis=-1)
    n = flat.shape[-1]
    block = cols * row_align
    total = -(-n // block) * block
    flat = jnp.pad(flat, [(0, 0)] * (flat.ndim - 1) + [(0, total - n)])
    return flat.reshape(flat.shape[:-1] + (total // cols, cols))


def _unpack(buf, shapes):
    lead = buf.shape[:-2]
    flat = buf.reshape(lead + (-1,))
    out, off = [], 0
    for s in shapes:
        n = int(np.prod(s))
        out.append(flat[..., off:off + n].reshape(lead + tuple(s)))
        off += n
    return out


SHARDED = ["w_in", "w_uq", "w_ukv", "w_out", "w_up", "w_down"]
ATTN_SENT = ["w_in_p", "w_uq", "w_ukv", "w_out"]
UP_HALF = 352
FFN_SIDE = ["w_upT", "conv_w", "w_down"]


def _full_from_shards(name, s):
    if name in ("w_in", "w_in_p", "w_out", "w_down", "w_upT"):
        return s.reshape((-1, s.shape[-1]))
    return s.transpose(1, 0, 2).reshape((s.shape[1], -1))


def _shards_from_full(name, f):
    if name in ("w_in", "w_in_p", "w_out", "w_down", "w_upT"):
        return f.reshape((N_DEV, -1, f.shape[-1]))
    return f.reshape((f.shape[0], N_DEV, -1)).transpose(1, 0, 2)


def _perm_w_in(w):
    z = lambda n: jnp.zeros(w.shape[:-1] + (n,), w.dtype)
    return jnp.concatenate([w[..., :1536], w[..., 1540:1924], w[..., 1536:1540], z(60), w[..., 1924:1956], z(32)],
                           axis=-1)


def _unperm_w_in(d):
    return jnp.concatenate([d[..., :1536], d[..., 1920:1924], d[..., 1536:1920], d[..., 1984:2016]], axis=-1)


def _perm_w_uq(w):
    return jnp.pad(w.reshape(256, 4, MLA_QK_DIM), ((0, 0), (0, 0), (0, 128 - MLA_QK_DIM))).reshape(256, 512)


def _unperm_w_uq(d):
    return d.reshape(256, 4, 128)[:, :, :MLA_QK_DIM].reshape(256, 4 * MLA_QK_DIM)


def _perm_w_ukv(w):
    w4 = w.reshape(128, 4, 128)
    k = jnp.pad(w4[:, :, :64], ((0, 0), (0, 0), (0, 64))).reshape(128, 512)
    return jnp.concatenate([k, w4[:, :, 64:].reshape(128, 256)], axis=1)


def _unperm_w_ukv(d):
    dk = d[:, :512].reshape(128, 4, 128)[:, :, :64]
    dv = d[:, 512:].reshape(128, 4, 64)
    return jnp.concatenate([dk, dv], axis=-1).reshape(128, 512)


def _row(v, width=None):
    v = v.reshape(1, -1).astype(F32)
    if width is not None and v.shape[1] < width:
        v = jnp.pad(v, ((0, 0), (0, width - v.shape[1])))
    return v


def _layer_fwd(x, P, shared, send=None, ffn_from=None):
    cosr, sinr, bias = shared
    ex = lambda host: (send[host], True) if send is not None and send.get(host) else None
    proj, hT, projb = norm_matmul(x, P["g_pre"], P["w_in_p"], "in_proj", lo_tiles=2, tn_pref=1024)
    qm, km, vm, cqT, ckvT = mla_prep(proj, P["gq"], P["gkv"], P["w_uq_p"], P["w_ukv_p"], cosr, sinr)
    fcol, frow, frep = fox_gate(proj, P["fbias"])
    (oa, lse_a), got_swa = swa_fwd(proj, bias, P["sinks"], exch=ex("swa"))
    (ob, lrb), got_fox = flash_fwd(projb, projb, projb, frep, frow, qblk=C_QF // 128, kblk=C_KF // 128,
                                   vblk=C_VF // 128, nq=1, scale=HEAD_DIM ** -0.5, name="fox_fwd", exch=ex("fox"))
    (oc, lrc), got_mla = flash_fwd(qm, km, vm, None, None, qblk=0, kblk=0, vblk=0, nq=2,
                                   scale=MLA_QK_DIM ** -0.5, name="mla_fwd", exch=ex("mla"))
    x2, y1, mT = attn_out(oa, ob, oc, P["gn"], P["w_out"], P["g_apost"], x)
    if ffn_from is not None:
        P = dict(P, **ffn_from(got_swa, got_fox, got_mla))
    u0, h2 = norm_matmul(x2, P["g_fpre"], P["w_upT"], "up_proj", tn_pref=1536, w_transposed=True,
                         h_transposed=False)
    (x3, y2, aT, gate, up), got_ffn = ffn_fwd(u0, P["conv_w"], P["conv_b"], P["w_down"], P["g_fpost"], x2,
                                              exch=ex("ffn"))
    S = dict(x=x, proj=proj, projb=projb, hT=hT, qm=qm, km=km, vm=vm, cqT=cqT, ckvT=ckvT, fcol=fcol, frow=frow,
             oa=oa, lse_a=lse_a, ob=ob, lrb=lrb, oc=oc, lrc=lrc,
             x2=x2, y1=y1, mT=mT, u0=u0, h2=h2, y2=y2, aT=aT, gate=gate, up=up)
    return x3, S, P, got_ffn


def _layer_bwd(dx3, P, S, shared, send_attn=None):
    cosr, sinr, bias = shared
    proj = S["proj"]
    G = {}
    got = {}
    ex = lambda arrays: (arrays, False) if send_attn is not None and arrays else None
    (dy2, dugT, duuT, dcg, dcu, G["ffn_post_norm"], dx2, G["ffn_pre_norm"]), got["ffn"] = ffn_bwd(
        dx3, S["y2"], S["u0"], S["gate"], S["up"], P["conv_w"], P["w_down"], P["g_fpost"], P["w_upT"], S["x2"], P["g_fpre"],
        exch=ex(send_attn))
    dconv = jnp.concatenate([dcg[-1], dcu[-1]], axis=1)
    G["conv_w"], G["conv_b"] = dconv[0:3], dconv[3]
    G["w_down"] = matmul_nn(S["aT"], dy2, "dw_down", MXU)
    G["w_upT"] = jnp.concatenate([matmul_nn(dugT, S["h2"], "dw_up_gate", MXU),
                                  matmul_nn(duuT, S["h2"], "dw_up_up", MXU)], axis=0)
    G["w_up"] = G["w_upT"].T
    dy1, doa, dob, doc, G["group_norm"], G["attn_post_norm"] = attn_out_bwd(
        dx2, S["y1"], S["oa"], S["ob"], S["oc"], P["gn"], P["w_out"], P["g_apost"])
    G["w_out"] = matmul_nn(S["mT"], dy1, "dw_out", MXU)
    up_slices = _shards_from_full("w_upT", G["w_upT"])
    (dqa, dka, dva, dbias, dsk), got["swa"] = swa_bwd(proj, bias, P["sinks"], doa, S["oa"], S["lse_a"],
                                                      exch=ex([_shards_from_full("w_down", G["w_down"])]))
    G["swa_sinks"] = dsk[:, 0]
    pb = S["projb"]
    (dqf, dkf, dvf, dFk, dFq), got["fox"] = flash_bwd(
        pb, pb, pb, dob, S["ob"], S["lrb"], S["fcol"], S["frow"], name="fox_bwd", qblk=C_QF // 128,
        kblk=C_KF // 128, vblk=C_VF // 128, nq=1, scale=HEAD_DIM ** -0.5, exch=ex([up_slices[:, :UP_HALF]]))
    dmisc_f, dfb = fox_gate_bwd(dFq, dFk, proj, P["fbias"])
    G["forget_bias"] = dfb[0, 0:4]
    (dqm_, dkm_, dvm_), got["mla"] = flash_bwd(
        S["qm"], S["km"], S["vm"], doc, S["oc"], S["lrc"], None, None, name="mla_bwd",
        qblk=0, kblk=0, vblk=0, nq=2, scale=MLA_QK_DIM ** -0.5, exch=ex([up_slices[:, UP_HALF:]]))
    dqm, dkv, dcq, dckv, dmisc_r, G["q_latent_norm"], G["kv_latent_norm"] = mla_prep_bwd(
        dqm_, dkm_, dvm_, proj, P["gq"], P["gkv"], P["w_uq_p"], P["w_ukv_p"], cosr, sinr)
    G["w_uq"] = _unperm_w_uq(matmul_nn(S["cqT"], dqm, "dw_uq", MXU))
    G["w_ukv"] = _unperm_w_ukv(matmul_nn(S["ckvT"], dkv, "dw_ukv", MXU))
    dproj = jnp.concatenate([dqa, dka, dva, dqf, dkf, dvf, dcq, dckv, dmisc_f + dmisc_r], axis=1).astype(MXU)
    G["w_in_p"] = matmul_nn(S["hT"], dproj, "dw_in", MXU)
    G["w_in"] = _unperm_w_in(G["w_in_p"])
    dx, G["attn_pre_norm"] = matmul_nt_normbwd(dproj, P["w_in_p"], S["x"], P["g_pre"], dx2, "in_bwd")
    return dx, G, dbias, got


def _layer_params(l, full, small):
    return dict(
        g_pre=_row(small["attn_pre_norm"][l]),
        w_in_p=full["w_in_p"] if "w_in_p" in full else _perm_w_in(full["w_in"]),
        gq=_row(small["q_latent_norm"][l]), gkv=_row(small["kv_latent_norm"][l]),
        w_uq_p=_perm_w_uq(full["w_uq"]), w_ukv_p=_perm_w_ukv(full["w_ukv"]),
        fbias=_row(small["forget_bias"][l], 128), sinks=small["swa_sinks"][l].astype(F32),
        gn=_row(small["group_norm"][l]), w_out=full["w_out"], g_apost=_row(small["attn_post_norm"][l]),
        g_fpre=_row(small["ffn_pre_norm"][l]), conv_b=_row(small["conv_b"][l]),
        g_fpost=_row(small["ffn_post_norm"][l]),
        **{n: full[n] for n in FFN_SIDE if n in full},
        **({"w_upT": full["w_up"].T} if "w_up" in full else {}))


def _rel_bias_grad(dbias, bucket):
    flat = dbias.reshape(SWA_Q_HEADS, -1)
    hi = flat.astype(MXU)
    lo = (flat - hi.astype(F32)).astype(MXU)
    onehot = (bucket[:, None] == jnp.arange(128, dtype=jnp.int32)[None, :]).astype(MXU)
    r = matmul_nn(jnp.concatenate([hi, lo], axis=0), onehot, "rel_bias_grad")
    return (r[0:8] + r[8:16])[:, :REL_BUCKETS].T


def local_step(x, tgt, fulls, small, comm=None):
    T = x.shape[0]
    cosr, sinr = rope_tables(T)
    bias, bucket = swa_bias_table(small["rel_bias"])
    shared = (cosr, sinr, bias)
    Ps, Ss = [], []
    h, full = x, fulls[0]
    for l in range(DEPTH):
        P = _layer_params(l, full, small)
        if comm:
            h, S, P, got = _layer_fwd(h, P, shared, comm["weight_parts"](l), comm["ffn_from"])
            full = comm["attn_from"](got) if l + 1 < DEPTH else None
        else:
            h, S, P, _ = _layer_fwd(h, P, shared)
            full = fulls[l + 1] if l + 1 < DEPTH else None
        Ps.append(P)
        Ss.append(S)
    dh, sq = loss_kernel(h, tgt)
    grads = [None] * DEPTH
    dbias_sum = None
    pending = [] if comm else None
    for l in reversed(range(DEPTH)):
        dh, grads[l], dbias, got = _layer_bwd(dh, Ps[l], Ss[l], shared, pending)
        dbias_sum = dbias if dbias_sum is None else dbias_sum + dbias
        if comm:
            comm["landed"](l, ["w_down"], got["swa"])
            comm["landed"](l, ["w_upT"], [jnp.concatenate([got["fox"][0], got["mla"][0]], axis=1)])
            if pending:
                comm["landed"](l + 1, ATTN_SENT, got["ffn"])
            pending = [_shards_from_full(n, grads[l][n]) for n in ATTN_SENT]
    return sq, dh, grads, _rel_bias_grad(dbias_sum, bucket), pending


WEIGHTS = ['attn_pre_norm', 'w_in', 'forget_bias', 'swa_sinks', 'rel_bias', 'q_latent_norm', 'w_uq',
           'kv_latent_norm', 'w_ukv', 'group_norm', 'w_out', 'attn_post_norm', 'ffn_pre_norm', 'w_up', 'conv_w',
           'conv_b', 'w_down', 'ffn_post_norm']
SMALL_PER_LAYER = ['attn_pre_norm', 'forget_bias', 'swa_sinks', 'q_latent_norm', 'kv_latent_norm', 'group_norm',
                   'attn_post_norm', 'ffn_pre_norm', 'conv_b', 'ffn_post_norm', 'conv_w']


def kernel(x, attn_pre_norm, w_in, forget_bias, swa_sinks, rel_bias, q_latent_norm, w_uq, kv_latent_norm, w_ukv, group_norm, w_out, attn_post_norm, ffn_pre_norm, w_up, conv_w, conv_b, w_down, ffn_post_norm, loss_target, m_attn_pre_norm, m_w_in, m_forget_bias, m_swa_sinks, m_rel_bias, m_q_latent_norm, m_w_uq, m_kv_latent_norm, m_w_ukv, m_group_norm, m_w_out, m_attn_post_norm, m_ffn_pre_norm, m_w_up, m_conv_w, m_conv_b, m_w_down, m_ffn_post_norm, v_attn_pre_norm, v_w_in, v_forget_bias, v_swa_sinks, v_rel_bias, v_q_latent_norm, v_w_uq, v_kv_latent_norm, v_w_ukv, v_group_norm, v_w_out, v_attn_post_norm, v_ffn_pre_norm, v_w_up, v_conv_w, v_conv_b, v_w_down, v_ffn_post_norm):
    W = dict(attn_pre_norm=attn_pre_norm, w_in=w_in, forget_bias=forget_bias, swa_sinks=swa_sinks, rel_bias=rel_bias,
             q_latent_norm=q_latent_norm, w_uq=w_uq, kv_latent_norm=kv_latent_norm, w_ukv=w_ukv,
             group_norm=group_norm, w_out=w_out, attn_post_norm=attn_post_norm, ffn_pre_norm=ffn_pre_norm,
             w_up=w_up, conv_w=conv_w, conv_b=conv_b, w_down=w_down, ffn_post_norm=ffn_post_norm)
    M = dict(attn_pre_norm=m_attn_pre_norm, w_in=m_w_in, forget_bias=m_forget_bias, swa_sinks=m_swa_sinks,
             rel_bias=m_rel_bias, q_latent_norm=m_q_latent_norm, w_uq=m_w_uq, kv_latent_norm=m_kv_latent_norm,
             w_ukv=m_w_ukv, group_norm=m_group_norm, w_out=m_w_out, attn_post_norm=m_attn_post_norm,
             ffn_pre_norm=m_ffn_pre_norm, w_up=m_w_up, conv_w=m_conv_w, conv_b=m_conv_b, w_down=m_w_down,
             ffn_post_norm=m_ffn_post_norm)
    V = dict(attn_pre_norm=v_attn_pre_norm, w_in=v_w_in, forget_bias=v_forget_bias, swa_sinks=v_swa_sinks,
             rel_bias=v_rel_bias, q_latent_norm=v_q_latent_norm, w_uq=v_w_uq, kv_latent_norm=v_kv_latent_norm,
             w_ukv=v_w_ukv, group_norm=v_group_norm, w_out=v_w_out, attn_post_norm=v_attn_post_norm,
             ffn_pre_norm=v_ffn_pre_norm, w_up=v_w_up, conv_w=v_conv_w, conv_b=v_conv_b, w_down=v_w_down,
             ffn_post_norm=v_ffn_post_norm)
    me = 4 * lax.axis_index("x") + 2 * lax.axis_index("y") + lax.axis_index("c")

    def attn_shards(l):
        return [_perm_w_in(w_in[l].astype(MXU))] + [W[n][l].astype(MXU) for n in ATTN_SENT[1:]]

    def weight_parts(l):
        up = jnp.swapaxes(W["w_up"][l], 0, 1).astype(MXU)
        return dict(swa=[W["w_down"][l].astype(MXU)], fox=[up[:UP_HALF]], mla=[up[UP_HALF:], conv_w[l]],
                    ffn=attn_shards(l + 1) if l + 1 < DEPTH else [])

    def ffn_from(got_swa, got_fox, got_mla):
        return dict(w_down=_full_from_shards("w_down", got_swa[0]),
                    w_upT=_full_from_shards("w_upT", jnp.concatenate([got_fox[0], got_mla[0]], axis=1)),
                    conv_w=got_mla[1].transpose(1, 0, 2).reshape(3, 2 * D_FF))

    def attn_from(got_ffn):
        return {n: _full_from_shards(n, s) for n, s in zip(ATTN_SENT, got_ffn)}

    landed = [{} for _ in range(DEPTH)]

    def on_landed(l, names, arrays):
        landed[l].update(zip(names, arrays))

    comm = dict(weight_parts=weight_parts, ffn_from=ffn_from, attn_from=attn_from, landed=on_landed)
    full0 = dict(zip(ATTN_SENT, map(_full_from_shards, ATTN_SENT, exchange(attn_shards(0), "gather_weights", True))))
    sq, dx, grads, drel, last = local_step(x[0], loss_target[0], [full0], W, comm)
    G, delta, new_m, new_v = {}, {}, {}, {}

    def update(n, exch=None):
        shp = W[n].shape
        if n == "w_up":
            v3 = lambda a: jnp.swapaxes(a, 1, 2)
            back = v3
            g = [landed[l]["w_upT"] for l in range(DEPTH)]
        else:
            v3 = lambda a: a.reshape(shp if len(shp) == 3 else (1,) + shp)
            back = lambda a: a.reshape(shp)
            g = [landed[l][n] for l in range(DEPTH)] if n in SHARDED else v3(G[n])
        (g, d, nm, nv), got = adamw(v3(W[n]), g, v3(M[n]), v3(V[n]), "adamw_" + n, exch)
        G[n], delta[n], new_m[n], new_v[n] = back(g), back(d), back(nm), back(nv)
        return got

    parts, shapes = [], []
    for l in range(DEPTH):
        for n in SMALL_PER_LAYER:
            parts.append(grads[l][n].astype(F32).reshape(-1))
            shapes.append(grads[l][n].shape)
    parts += [drel.reshape(-1), jnp.sum(sq).reshape(1) * (0.5 / D_MODEL)]
    shapes += [drel.shape, (1,)]
    on_landed(0, ATTN_SENT, update("w_up", (last, False)))
    for l in range(DEPTH):
        landed[l]["w_in"] = _unperm_w_in(landed[l]["w_in_p"])
    gathered = update("w_down", ([_pack(parts, 128, 8, F32)], True))[0]
    red = _unpack(sum_devices(gathered, "sum_small"), shapes)
    k = 0
    per = {n: [] for n in SMALL_PER_LAYER}
    for l in range(DEPTH):
        for n in SMALL_PER_LAYER:
            per[n].append(red[k])
            k += 1
    for n in SMALL_PER_LAYER:
        G[n] = jnp.stack(per[n]).reshape((DEPTH, 3, 2 * D_FF) if n == "conv_w" else W[n].shape)
    G["rel_bias"] = red[k]
    loss = red[k + 1][0]
    G["conv_w"] = lax.dynamic_slice_in_dim(G["conv_w"], me * 704, 704, axis=2)

    for n in WEIGHTS:
        if n not in ("w_up", "w_down"):
            update(n)
    return (loss, dx[None], *[G[n] for n in WEIGHTS], *[delta[n] for n in WEIGHTS],
            *[new_m[n] for n in WEIGHTS], *[new_v[n] for n in WEIGHTS])
```

```python
import functools
import math

import numpy as np
import jax
import jax.numpy as jnp
from jax import lax
from jax.experimental import pallas as pl
from jax.experimental.pallas import tpu as pltpu

F32 = jnp.float32
MXU = jnp.bfloat16

N_DEV = 8
DEPTH = 4
D_MODEL = 1024
HEAD_DIM = 64
WINDOW = 128
SWA_Q_HEADS = 8
REL_BUCKETS = 32
REL_MAX_DIST = 128
MLA_QK_DIM = 96
ROPE_DIM = 32
ROPE_THETA = 10000.0
D_FF = 2816
EPS = 1e-6
NEG = -1e30
IN_COLS = 1956
C_QA, C_KA, C_VA = 0, 512, 640
C_QF, C_KF, C_VF = 768, 1024, 1280
C_CQ, C_CKV, C_MISC = 1536, 1792, 1920
ROPE_LANE0 = 64
ADAM_LR, ADAM_B1, ADAM_B2, ADAM_EPS, ADAM_WD, ADAM_STEP = 0.001, 0.9, 0.999, 1e-08, 0.01, 10

VMEM_LIMIT = 56 * 1024 * 1024


def _cparams(sem=None):
    return pltpu.CompilerParams(dimension_semantics=sem, vmem_limit_bytes=VMEM_LIMIT)


def _tile(n, pref):
    if n <= pref:
        return n
    t = pref - pref % 128
    while t >= 128:
        if n % t == 0:
            return t
        t -= 128
    return n


def _dot(a, b):
    return jnp.dot(a.astype(MXU), b.astype(MXU), preferred_element_type=F32)


def _dot_nt(a, b):
    return lax.dot_general(a.astype(MXU), b.astype(MXU), (((1,), (1,)), ((), ())),
                           preferred_element_type=F32)


def _rms_fwd(x, g):
    return x * lax.rsqrt(jnp.mean(x * x, axis=-1, keepdims=True) + EPS) * g


def _rms_bwd(dy, x, g, n=None):
    r = lax.rsqrt(jnp.mean(x * x, axis=-1, keepdims=True) + EPS)
    xh = x * r
    dg = jnp.sum(dy * xh, axis=0, keepdims=True)
    dxh = dy * g
    dx = r * (dxh - xh * jnp.mean(dxh * xh, axis=-1, keepdims=True))
    return dx, dg


def _acc_out(ref, val, first):
    @pl.when(first)
    def _():
        ref[...] = val

    @pl.when(jnp.logical_not(first))
    def _():
        ref[...] += val


def norm_matmul(x, g, w, name, lo_tiles=0, tn_pref=512, w_transposed=False, h_transposed=True, exch=None):
    T, K = x.shape
    N = w.shape[0] if w_transposed else w.shape[1]
    tm, tn = _tile(T, 1024), _tile(N, tn_pref)

    def body(x_ref, g_ref, w_ref, o_ref, hT_ref, *rest):
        h_sc = rest[-1]
        j = pl.program_id(1)

        @pl.when(j == 0)
        def _():
            h = _rms_fwd(x_ref[...], g_ref[...])
            h_sc[...] = h.astype(MXU)
            hT_ref[...] = (h.T if h_transposed else h).astype(MXU)

        r = (_dot_nt if w_transposed else _dot)(h_sc[...], w_ref[...])
        o_ref[...] = r
        if lo_tiles:
            @pl.when(j < lo_tiles)
            def _():
                rest[0][...] = r.astype(MXU)

    h_spec = pl.BlockSpec((K, tm), lambda i, j: (0, i)) if h_transposed else pl.BlockSpec((tm, K), lambda i, j: (i, 0))
    out_specs = [pl.BlockSpec((tm, tn), lambda i, j: (i, j)), h_spec]
    out_shape = [jax.ShapeDtypeStruct((T, N), F32), jax.ShapeDtypeStruct((K, T) if h_transposed else (T, K), MXU)]
    if lo_tiles:
        out_specs.append(pl.BlockSpec((tm, tn), lambda i, j: (i, jnp.minimum(j, lo_tiles - 1))))
        out_shape.append(jax.ShapeDtypeStruct((T, lo_tiles * tn), MXU))
    in_specs = [pl.BlockSpec((tm, K), lambda i, j: (i, 0)),
                pl.BlockSpec((1, K), lambda i, j: (0, 0)),
                pl.BlockSpec((tn, K), lambda i, j: (j, 0)) if w_transposed else
                pl.BlockSpec((K, tn), lambda i, j: (0, j))]
    outs, landed = _call_hosting(body, name, (T // tm, N // tn), [x, g, w], in_specs, out_specs, out_shape,
                                 [pltpu.VMEM((tm, K), MXU)], exch)
    return outs if exch is None else (outs, landed)


def matmul_nn(a, b, name, out_dtype=F32):
    M, K = a.shape
    N = b.shape[1]
    tm, tn, tk = _tile(M, 1408), _tile(N, 1536), _tile(K, 1024)
    nk = K // tk

    def body(a_ref, b_ref, o_ref, acc):
        k = pl.program_id(2)
        part = _dot(a_ref[...], b_ref[...])
        _acc_out(acc, part, k == 0)

        @pl.when(k == nk - 1)
        def _():
            o_ref[...] = acc[...].astype(out_dtype)

    return pl.pallas_call(
        body, name=name, grid=(M // tm, N // tn, nk),
        in_specs=[pl.BlockSpec((tm, tk), lambda i, j, k: (i, k)),
                  pl.BlockSpec((tk, tn), lambda i, j, k: (k, j))],
        out_specs=pl.BlockSpec((tm, tn), lambda i, j, k: (i, j)),
        out_shape=jax.ShapeDtypeStruct((M, N), out_dtype),
        scratch_shapes=[pltpu.VMEM((tm, tn), F32)],
        compiler_params=_cparams(("parallel", "parallel", "arbitrary")),
    )(a, b)


def matmul_nt_normbwd(dy, w, x, g, dres, name):
    T, N = dy.shape
    K = w.shape[0]
    tm, tn = _tile(T, 1024), _tile(N, 1536)
    nj = N // tn

    def body(dy_ref, w_ref, x_ref, g_ref, dres_ref, dx_ref, dg_ref, acc):
        i, j = pl.program_id(0), pl.program_id(1)
        _acc_out(acc, _dot_nt(dy_ref[...], w_ref[...]), j == 0)

        @pl.when(j == nj - 1)
        def _():
            dx, dg = _rms_bwd(acc[...], x_ref[...], g_ref[...])
            dx_ref[...] = dres_ref[...] + dx
            _acc_out(dg_ref, dg, i == 0)

    return pl.pallas_call(
        body, name=name, grid=(T // tm, nj),
        in_specs=[pl.BlockSpec((tm, tn), lambda i, j: (i, j)),
                  pl.BlockSpec((K, tn), lambda i, j: (0, j)),
                  pl.BlockSpec((tm, K), lambda i, j: (i, 0)),
                  pl.BlockSpec((1, K), lambda i, j: (0, 0)),
                  pl.BlockSpec((tm, K), lambda i, j: (i, 0))],
        out_specs=[pl.BlockSpec((tm, K), lambda i, j: (i, 0)),
                   pl.BlockSpec((1, K), lambda i, j: (0, 0))],
        out_shape=[jax.ShapeDtypeStruct((T, K), F32), jax.ShapeDtypeStruct((1, K), F32)],
        scratch_shapes=[pltpu.VMEM((tm, K), F32)],
        compiler_params=_cparams(("arbitrary", "arbitrary")),
    )(dy, w, x, g, dres)


def loss_kernel(y, tgt):
    T, D = y.shape
    tm = _tile(T, 512)

    def body(y_ref, t_ref, dy_ref, acc_ref):
        e = y_ref[...] - t_ref[...]
        dy_ref[...] = e * (1.0 / D)
        _acc_out(acc_ref, jnp.sum(e * e, axis=0, keepdims=True), pl.program_id(0) == 0)

    return pl.pallas_call(
        body, name="loss", grid=(T // tm,),
        in_specs=[pl.BlockSpec((tm, D), lambda i: (i, 0)), pl.BlockSpec((tm, D), lambda i: (i, 0))],
        out_specs=[pl.BlockSpec((tm, D), lambda i: (i, 0)), pl.BlockSpec((1, D), lambda i: (0, 0))],
        out_shape=[jax.ShapeDtypeStruct((T, D), F32), jax.ShapeDtypeStruct((1, D), F32)],
        compiler_params=_cparams(("arbitrary",)),
    )(y, tgt)


def _rope_partner(x):
    lane = lax.broadcasted_iota(jnp.int32, (1, 128), 1)
    return jnp.where(lane < ROPE_LANE0 + ROPE_DIM // 2, pltpu.roll(x, 128 - ROPE_DIM // 2, 1),
                     pltpu.roll(x, ROPE_DIM // 2, 1))


def _rope_apply(x, cos, sin_signed):
    return x * cos + _rope_partner(x) * sin_signed


def _rope_apply_bwd(dy, cos, sin_signed):
    lane = lax.broadcasted_iota(jnp.int32, (1, 128), 1)
    rotary = (lane >= ROPE_LANE0) & (lane < ROPE_LANE0 + ROPE_DIM)
    return dy * cos + jnp.where(rotary, _rope_partner(dy * sin_signed), 0.0)


def rope_tables(T):
    pos = jnp.arange(T, dtype=F32)
    inv_freq = ROPE_THETA ** (-(jnp.arange(ROPE_DIM // 2, dtype=F32) * 2.0 / ROPE_DIM))
    ang = pos[:, None] * inv_freq[None, :]
    cos, sin = jnp.cos(ang), jnp.sin(ang)
    z = jnp.zeros((T, ROPE_LANE0), F32)
    z2 = jnp.zeros((T, 128 - ROPE_LANE0 - ROPE_DIM), F32)
    cosr = jnp.concatenate([z, cos, cos, z2], axis=1)
    sinr = jnp.concatenate([z, -sin, sin, z2], axis=1)
    return cosr, sinr


def mla_prep(proj, gq, gkv, wuq, wukv, cosr, sinr):
    T = proj.shape[0]
    tm = _tile(T, 512)

    def body(cq_ref, ckv_ref, misc_ref, gq_ref, gkv_ref, wuq_ref, wukv_ref, cos_ref, sin_ref,
             q_ref, k_ref, v_ref, cqT_ref, ckvT_ref):
        lane = lax.broadcasted_iota(jnp.int32, (1, 128), 1)
        cosr_, sinr_ = cos_ref[...], sin_ref[...]
        cosq = cosr_ + jnp.where(lane < ROPE_LANE0, 1.0, 0.0)
        cqn = _rms_fwd(cq_ref[...], gq_ref[...])
        cqT_ref[...] = cqn.T.astype(MXU)
        qm = _dot(cqn, wuq_ref[...])
        q_ref[...] = jnp.concatenate(
            [_rope_apply(qm[:, 128 * h:128 * (h + 1)], cosq, sinr_) for h in range(4)], axis=1).astype(MXU)
        ckvn = _rms_fwd(ckv_ref[...], gkv_ref[...])
        ckvT_ref[...] = ckvn.T.astype(MXU)
        kv = _dot(ckvn, wukv_ref[...])
        kr = _rope_apply(misc_ref[...], cosr_, sinr_)
        k_ref[...] = jnp.concatenate(
            [kv[:, 128 * h:128 * (h + 1)] + kr for h in range(4)], axis=1).astype(MXU)
        v_ref[...] = kv[:, 512:768].astype(MXU)

    row = lambda i: (i, 0)
    const = lambda i: (0, 0)
    return pl.pallas_call(
        body, name="mla_prep", grid=(T // tm,),
        in_specs=[pl.BlockSpec((tm, 256), lambda i: (i, C_CQ // 256)),
                  pl.BlockSpec((tm, 128), lambda i: (i, C_CKV // 128)),
                  pl.BlockSpec((tm, 128), lambda i: (i, C_MISC // 128)),
                  pl.BlockSpec((1, 256), const), pl.BlockSpec((1, 128), const),
                  pl.BlockSpec((256, 512), const), pl.BlockSpec((128, 768), const),
                  pl.BlockSpec((tm, 128), row), pl.BlockSpec((tm, 128), row)],
        out_specs=[pl.BlockSpec((tm, 512), row), pl.BlockSpec((tm, 512), row), pl.BlockSpec((tm, 256), row),
                   pl.BlockSpec((256, tm), lambda i: (0, i)), pl.BlockSpec((128, tm), lambda i: (0, i))],
        out_shape=[jax.ShapeDtypeStruct((T, 512), MXU), jax.ShapeDtypeStruct((T, 512), MXU),
                   jax.ShapeDtypeStruct((T, 256), MXU),
                   jax.ShapeDtypeStruct((256, T), MXU), jax.ShapeDtypeStruct((128, T), MXU)],
        compiler_params=_cparams(("parallel",)),
    )(proj, proj, proj, gq, gkv, wuq, wukv, cosr, sinr)


def mla_prep_bwd(dq, dk, dv, proj, gq, gkv, wuq, wukv, cosr, sinr):
    T = proj.shape[0]
    tm = _tile(T, 512)

    def body(dq_ref, dk_ref, dv_ref, cq_ref, ckv_ref, gq_ref, gkv_ref, wuq_ref, wukv_ref, cos_ref, sin_ref,
             dqm_ref, dkv_ref, dcq_ref, dckv_ref, dmisc_ref, dgq_ref, dgkv_ref):
        first = pl.program_id(0) == 0
        lane = lax.broadcasted_iota(jnp.int32, (1, 128), 1)
        cosr_, sinr_ = cos_ref[...], sin_ref[...]
        cosq = cosr_ + jnp.where(lane < ROPE_LANE0, 1.0, 0.0)
        dqv = dq_ref[...]
        dqm = jnp.concatenate(
            [_rope_apply_bwd(dqv[:, 128 * h:128 * (h + 1)], cosq, sinr_) for h in range(4)], axis=1)
        dqm_ref[...] = dqm.astype(MXU)
        dcq, dgq = _rms_bwd(_dot_nt(dqm, wuq_ref[...]), cq_ref[...], gq_ref[...])
        dcq_ref[...] = dcq
        _acc_out(dgq_ref, dgq, first)
        dkv_ = dk_ref[...]
        heads = [dkv_[:, 128 * h:128 * (h + 1)] for h in range(4)]
        dkr = heads[0] + heads[1] + heads[2] + heads[3]
        dmisc_ref[...] = _rope_apply_bwd(dkr, cosr_, sinr_)
        dkvm = jnp.concatenate([jnp.where(lane < ROPE_LANE0, hd, 0.0) for hd in heads] + [dv_ref[...]], axis=1)
        dkv_ref[...] = dkvm.astype(MXU)
        dckv, dgkv = _rms_bwd(_dot_nt(dkvm, wukv_ref[...]), ckv_ref[...], gkv_ref[...])
        dckv_ref[...] = dckv
        _acc_out(dgkv_ref, dgkv, first)

    row = lambda i: (i, 0)
    const = lambda i: (0, 0)
    return pl.pallas_call(
        body, name="mla_prep_bwd", grid=(T // tm,),
        in_specs=[pl.BlockSpec((tm, 512), row), pl.BlockSpec((tm, 512), row), pl.BlockSpec((tm, 256), row),
                  pl.BlockSpec((tm, 256), lambda i: (i, C_CQ // 256)),
                  pl.BlockSpec((tm, 128), lambda i: (i, C_CKV // 128)),
                  pl.BlockSpec((1, 256), const), pl.BlockSpec((1, 128), const),
                  pl.BlockSpec((256, 512), const), pl.BlockSpec((128, 768), const),
                  pl.BlockSpec((tm, 128), row), pl.BlockSpec((tm, 128), row)],
        out_specs=[pl.BlockSpec((tm, 512), row), pl.BlockSpec((tm, 768), row), pl.BlockSpec((tm, 256), row),
                   pl.BlockSpec((tm, 128), row), pl.BlockSpec((tm, 128), row),
                   pl.BlockSpec((1, 256), const), pl.BlockSpec((1, 128), const)],
        out_shape=[jax.ShapeDtypeStruct((T, 512), MXU), jax.ShapeDtypeStruct((T, 768), MXU),
                   jax.ShapeDtypeStruct((T, 256), F32), jax.ShapeDtypeStruct((T, 128), F32),
                   jax.ShapeDtypeStruct((T, 128), F32),
                   jax.ShapeDtypeStruct((1, 256), F32), jax.ShapeDtypeStruct((1, 128), F32)],
        compiler_params=_cparams(("arbitrary",)),
    )(dq, dk, dv, proj, proj, gq, gkv, wuq, wukv, cosr, sinr)


def _split3(x):
    hi = x.astype(MXU)
    r1 = x - hi.astype(F32)
    mid = r1.astype(MXU)
    lo = (r1 - mid.astype(F32)).astype(MXU)
    return hi, mid, lo


def _tri_matmul(tri, x):
    hi, mid, lo = _split3(x)
    d = lambda p: jnp.dot(tri, p, preferred_element_type=F32)
    return d(hi) + d(mid) + d(lo)


def _log_sigmoid(z):
    return jnp.minimum(z, 0.0) - jnp.log(1.0 + jnp.exp(-jnp.abs(z)))


def fox_gate(proj, fbias):
    T = proj.shape[0]
    tb = _tile(T, 512)

    def body(misc_ref, b_ref, fc_ref, fr_ref, frep_ref, carry):
        @pl.when(pl.program_id(0) == 0)
        def _():
            carry[...] = jnp.zeros_like(carry)

        lane = lax.broadcasted_iota(jnp.int32, (1, 128), 1)
        lf = jnp.where(lane < 4, _log_sigmoid(misc_ref[...] + b_ref[...]), 0.0)
        r = lax.broadcasted_iota(jnp.int32, (tb, tb), 0)
        c = lax.broadcasted_iota(jnp.int32, (tb, tb), 1)
        tri = jnp.where(r >= c, 1.0, 0.0).astype(MXU)
        F = _tri_matmul(tri, lf) + carry[...]
        carry[...] = carry[...] + jnp.sum(lf, axis=0, keepdims=True)
        fc_ref[0] = F
        fc_ref[1] = pltpu.roll(F, 126, 1)
        ft = F.T[0:8, :]
        fr_ref[0] = ft
        fr_ref[1] = pltpu.roll(ft, 6, 0)
        for h in range(4):
            frep_ref[h] = jnp.broadcast_to(_lane_pick(F, h), (tb, 128))

    return pl.pallas_call(
        body, name="fox_gate", grid=(T // tb,),
        in_specs=[pl.BlockSpec((tb, 128), lambda i: (i, C_MISC // 128)), pl.BlockSpec((1, 128), lambda i: (0, 0))],
        out_specs=[pl.BlockSpec((2, tb, 128), lambda i: (0, i, 0)), pl.BlockSpec((2, 8, tb), lambda i: (0, 0, i)),
                   pl.BlockSpec((4, tb, 128), lambda i: (0, i, 0))],
        out_shape=[jax.ShapeDtypeStruct((2, T, 128), F32), jax.ShapeDtypeStruct((2, 8, T), F32),
                   jax.ShapeDtypeStruct((4, T, 128), F32)],
        scratch_shapes=[pltpu.VMEM((1, 128), F32)],
        compiler_params=_cparams(("arbitrary",)),
    )(proj, fbias)


def fox_gate_bwd(dFq, dFk, proj, fbias):
    T = proj.shape[0]
    tb = _tile(T, 512)
    nb = T // tb

    def body(dq_ref, dk_ref, misc_ref, b_ref, dm_ref, db_ref, carry):
        first = pl.program_id(0) == 0

        @pl.when(first)
        def _():
            carry[...] = jnp.zeros_like(carry)

        lane = lax.broadcasted_iota(jnp.int32, (1, 128), 1)
        dF = jnp.where(lane < 4, (dq_ref[0] + dk_ref[0]) + pltpu.roll(dq_ref[1] + dk_ref[1], 2, 1), 0.0)
        r = lax.broadcasted_iota(jnp.int32, (tb, tb), 0)
        c = lax.broadcasted_iota(jnp.int32, (tb, tb), 1)
        tri = jnp.where(r <= c, 1.0, 0.0).astype(MXU)
        dlf = _tri_matmul(tri, dF) + carry[...]
        carry[...] = carry[...] + jnp.sum(dF, axis=0, keepdims=True)
        z = misc_ref[...] + b_ref[...]
        dz = jnp.where(lane < 4, dlf * (1.0 / (1.0 + jnp.exp(z))), 0.0)
        dm_ref[...] = dz
        _acc_out(db_ref, jnp.sum(dz, axis=0, keepdims=True), first)

    return pl.pallas_call(
        body, name="fox_gate_bwd", grid=(nb,),
        in_specs=[pl.BlockSpec((2, tb, 128), lambda i: (0, nb - 1 - i, 0)),
                  pl.BlockSpec((2, tb, 128), lambda i: (0, nb - 1 - i, 0)),
                  pl.BlockSpec((tb, 128), lambda i: (nb - 1 - i, C_MISC // 128)),
                  pl.BlockSpec((1, 128), lambda i: (0, 0))],
        out_specs=[pl.BlockSpec((tb, 128), lambda i: (nb - 1 - i, 0)), pl.BlockSpec((1, 128), lambda i: (0, 0))],
        out_shape=[jax.ShapeDtypeStruct((T, 128), F32), jax.ShapeDtypeStruct((1, 128), F32)],
        scratch_shapes=[pltpu.VMEM((1, 128), F32)],
        compiler_params=_cparams(("arbitrary",)),
    )(dFq, dFk, proj, fbias)


FLASH_TILE = 512


def _row_stat_tile(a, b, n):
    at = jnp.broadcast_to(a, (n, 128)).T[0:8, :]
    bt = jnp.broadcast_to(b, (n, 128)).T[0:8, :]
    sub = lax.broadcasted_iota(jnp.int32, (8, 1), 0)
    return jnp.where(sub == 0, at, jnp.where(sub == 1, bt, 0.0))


def _col_stat_tile(a, b):
    lane = lax.broadcasted_iota(jnp.int32, (1, 128), 1)
    return jnp.where(lane == 0, a, jnp.where(lane == 1, b, 0.0))


def _lane_pick(x, h):
    lane = lax.broadcasted_iota(jnp.int32, (1, 128), 1)
    return jnp.sum(jnp.where(lane == h, x, 0.0), axis=1, keepdims=True)


def _half_mask(h):
    lane = lax.broadcasted_iota(jnp.int32, (1, 128), 1)
    return (lane // HEAD_DIM) == h


def _call_hosting(body, name, grid, args, in_specs, out_specs, out_shape, scratch, exch):
    n_out = len(out_shape)
    if exch is not None:
        body, (xargs, xin, xout, xshape, xscratch) = hosted_exchange(
            body, len(args), n_out, len(scratch), grid, *exch)
        args, in_specs, out_specs = args + xargs, in_specs + xin, out_specs + xout
        out_shape, scratch = out_shape + xshape, scratch + xscratch
    res = pl.pallas_call(
        body, name=name, grid=grid, in_specs=in_specs, out_specs=out_specs, out_shape=out_shape,
        scratch_shapes=scratch, compiler_params=_cparams(("arbitrary",) * len(grid)),
    )(*args)
    return res[:n_out], res[n_out:]


def flash_fwd(q, k, v, frep, frow, *, qblk, kblk, vblk, nq, scale, name, exch=None):
    T = q.shape[0]
    tk = _tile(T, FLASH_TILE)
    tq = _tile(T, 2 * FLASH_TILE)
    per_q = tq // tk
    wq = 128 * nq
    has_f = frep is not None

    def body(*refs):
        if has_f:
            q_ref, k_ref, v_ref, fk_ref, fr_ref, o_ref, lr_ref, vT_sc, m_sc, acc_sc = refs
        else:
            q_ref, k_ref, v_ref, o_ref, lr_ref, vT_sc, m_sc, acc_sc = refs
        i = pl.program_id(1)

        @pl.when(i == 0)
        def _():
            vT_sc[...] = v_ref[...].astype(F32).T.astype(MXU)

        key_row = lax.broadcasted_iota(jnp.int32, (tk, 1), 0)
        q_col = lax.broadcasted_iota(jnp.int32, (1, tq), 1)
        row_half = lax.broadcasted_iota(jnp.int32, (128, 1), 0) // HEAD_DIM
        qb = q_ref[...].astype(F32) * scale
        if nq == 1:
            qhs = [jnp.where(_half_mask(h), qb, 0).astype(MXU) for h in range(2)]
        else:
            qhs = [qb[:, 128 * h:128 * (h + 1)].astype(MXU) for h in range(2)]
        for h in range(2):
            m_sc[h] = jnp.full((1, tq), NEG, F32)
            acc_sc[h] = jnp.zeros((128, tq), F32)

        def make_step(diag_block):
            def step(j, carry):
                off = pl.multiple_of(j * tk, tk)
                ks = k_ref[pl.ds(off, tk), :]
                vT = vT_sc[:, pl.ds(off, tk)]
                for h in range(2):
                    kh = ks if nq == 1 else ks[:, 128 * h:128 * (h + 1)]
                    sT = _dot_nt(kh, qhs[h])
                    if has_f:
                        fk = fk_ref[h, pl.ds(off, tk), :]
                        sT = sT + (fr_ref[0, h:h + 1, :] - jnp.concatenate([fk] * (tq // 128), axis=1))
                    if diag_block is not None:
                        sT = jnp.where(key_row + diag_block * tk <= q_col, sT, NEG)
                    m_prev = m_sc[h]
                    m_new = jnp.maximum(m_prev, jnp.max(sT, axis=0, keepdims=True))
                    alpha = jnp.exp(m_prev - m_new)
                    pT = jnp.exp(sT - m_new)
                    vTh = jnp.where(row_half == h, vT, jnp.ones_like(vT))
                    acc_sc[h] = alpha * acc_sc[h] + _dot(vTh, pT)
                    m_sc[h] = m_new
                return carry
            return step

        lax.fori_loop(0, per_q * i, make_step(None), 0)
        for d in range(per_q):
            make_step(d)(per_q * i + d, 0)
        outs, lses = [], []
        for h in range(2):
            acc = acc_sc[h]
            outs.append(acc / pltpu.roll(acc, HEAD_DIM, 0))
            l = acc_sc[h, HEAD_DIM * (1 - h):HEAD_DIM * (1 - h) + 1, :]
            lses.append(m_sc[h] + jnp.log(l))
        o_ref[...] = jnp.where(row_half == 0, outs[0], outs[1]).T
        sub = lax.broadcasted_iota(jnp.int32, (8, 1), 0)
        lr_ref[0] = jnp.where(sub == 0, lses[0], jnp.where(sub == 1, lses[1], 0.0))

    in_specs = [pl.BlockSpec((tq, wq), lambda p, i: (i, qblk + p)),
                pl.BlockSpec((T, wq), lambda p, i: (0, kblk + p)),
                pl.BlockSpec((T, 128), lambda p, i: (0, vblk + p))]
    args = [q, k, v]
    if has_f:
        in_specs += [pl.BlockSpec((2, T, 128), lambda p, i: (p, 0, 0)),
                     pl.BlockSpec((1, 8, tq), lambda p, i: (p, 0, i))]
        args += [frep, frow]
    out_specs = [pl.BlockSpec((tq, 128), lambda p, i: (i, p)), pl.BlockSpec((1, 8, tq), lambda p, i: (p, 0, i))]
    out_shape = [jax.ShapeDtypeStruct((T, 256), F32), jax.ShapeDtypeStruct((2, 8, T), F32)]
    scratch = [pltpu.VMEM((128, T), MXU), pltpu.VMEM((2, 1, tq), F32), pltpu.VMEM((2, 128, tq), F32)]
    return _call_hosting(body, name, (2, T // tq), args, in_specs, out_specs, out_shape, scratch, exch)


def flash_bwd(q, k, v, do, o, lrow, fcol, frow, *, qblk, kblk, vblk, nq, scale, name, exch=None):
    T = q.shape[0]
    tq = tk = _tile(T, FLASH_TILE)
    wq = 128 * nq
    nqb = T // tq
    has_f = fcol is not None

    def body(*refs):
        if has_f:
            (q_ref, k_ref, v_ref, do_ref, o_ref, lr_ref, fc_ref, fr_ref,
             dq_ref, dk_ref, dv_ref, df_ref, dfq_ref, dk_sc, dv_sc, dqT_sc, d_sc, df_sc, dfq_sc) = refs
        else:
            q_ref, k_ref, v_ref, do_ref, o_ref, lr_ref, dq_ref, dk_ref, dv_ref, dk_sc, dv_sc, dqT_sc, d_sc = refs
        j = pl.program_id(1)
        diag = lax.broadcasted_iota(jnp.int32, (tk, 1), 0) <= lax.broadcasted_iota(jnp.int32, (1, tq), 1)
        hms = [_half_mask(h) for h in range(2)]

        @pl.when(j == 0)
        def _():
            dqT_sc[...] = jnp.zeros_like(dqT_sc)
            if has_f:
                dfq_sc[...] = jnp.zeros_like(dfq_sc)

            def delta(b, carry):
                off = pl.multiple_of(b * tq, tq)
                prod = do_ref[pl.ds(off, tq), :] * o_ref[pl.ds(off, tq), :]
                Ds = [jnp.sum(jnp.where(hms[h], prod, 0.0), axis=1, keepdims=True) for h in range(2)]
                d_sc[:, pl.ds(off, tq)] = _row_stat_tile(Ds[0], Ds[1], tq)
                return carry

            lax.fori_loop(0, nqb, delta, 0)

        kb = k_ref[...]
        vb = v_ref[...]
        if nq == 1:
            khs = [jnp.where(hms[h], kb, 0).astype(MXU) for h in range(2)]
        else:
            khs = [kb[:, 128 * h:128 * (h + 1)].astype(MXU) for h in range(2)]
        kTs = [kh.astype(F32).T.astype(MXU) for kh in khs]
        kss = [(kh.astype(F32) * scale).astype(MXU) for kh in khs]
        vhs = [jnp.where(hms[h], vb, 0).astype(MXU) for h in range(2)]
        fks = [_lane_pick(fc_ref[0], h) for h in range(2)] if has_f else None
        dv_sc[...] = jnp.zeros_like(dv_sc)
        dk_sc[...] = jnp.zeros_like(dk_sc)
        if has_f:
            df_sc[...] = jnp.zeros_like(df_sc)

        def make_step(masked):
            def step(i, carry):
                off = pl.multiple_of(i * tq, tq)
                qs = q_ref[pl.ds(off, tq), :]
                dos = do_ref[pl.ds(off, tq), :]
                for h in range(2):
                    qh = qs if nq == 1 else qs[:, 128 * h:128 * (h + 1)]
                    sT = _dot_nt(kss[h], qh)
                    if has_f:
                        sT = sT + (fr_ref[0, h:h + 1, pl.ds(off, tq)] - fks[h])
                    pT = jnp.exp(sT - lr_ref[0, h:h + 1, pl.ds(off, tq)])
                    if masked:
                        pT = jnp.where(diag, pT, 0.0)
                    dsT = pT * (_dot_nt(vhs[h], dos) - d_sc[h:h + 1, pl.ds(off, tq)])
                    dv_sc[...] += _dot(pT, jnp.where(hms[h], dos, 0))
                    qq = jnp.where(hms[h], qs, 0) if nq == 1 else qh
                    dk_sc[h if nq == 2 else 0] += _dot(dsT, qq)
                    dqT_sc[h if nq == 2 else 0, :, pl.ds(off, tq)] += _dot(kTs[h], dsT)
                    if has_f:
                        part = dsT[:, 0:128]
                        for c in range(1, tq // 128):
                            part = part + dsT[:, 128 * c:128 * (c + 1)]
                        df_sc[h] += part
                        dfq_sc[h:h + 1, pl.ds(off, tq)] += jnp.sum(dsT, axis=0, keepdims=True)
                return carry
            return step

        make_step(True)(j, 0)
        lax.fori_loop(j + 1, nqb, make_step(False), 0)
        if nq == 1:
            dk_ref[...] = dk_sc[0] * scale
        else:
            dk_ref[...] = jnp.concatenate([dk_sc[0], dk_sc[1]], axis=1) * scale
        dv_ref[...] = dv_sc[...]
        if has_f:
            df_ref[0] = _col_stat_tile(-jnp.sum(df_sc[0], axis=1, keepdims=True),
                                       -jnp.sum(df_sc[1], axis=1, keepdims=True))

        @pl.when(j == nqb - 1)
        def _():
            if nq == 1:
                dq_ref[...] = dqT_sc[0].T * scale
            else:
                dq_ref[...] = jnp.concatenate([dqT_sc[0].T, dqT_sc[1].T], axis=1) * scale
            if has_f:
                sub = lax.broadcasted_iota(jnp.int32, (128, 1), 0)
                rows = jnp.where(sub == 0, dfq_sc[0:1, :], jnp.where(sub == 1, dfq_sc[1:2, :], 0.0))
                dfq_ref[0] = rows.T

    in_specs = [pl.BlockSpec((T, wq), lambda p, j: (0, qblk + p)),
                pl.BlockSpec((tk, wq), lambda p, j: (j, kblk + p)),
                pl.BlockSpec((tk, 128), lambda p, j: (j, vblk + p)),
                pl.BlockSpec((T, 128), lambda p, j: (0, p)),
                pl.BlockSpec((T, 128), lambda p, j: (0, p)),
                pl.BlockSpec((1, 8, T), lambda p, j: (p, 0, 0))]
    args = [q, k, v, do, o, lrow]
    out_specs = [pl.BlockSpec((T, wq), lambda p, j: (0, p)),
                 pl.BlockSpec((tk, wq), lambda p, j: (j, p)), pl.BlockSpec((tk, 128), lambda p, j: (j, p))]
    out_shape = [jax.ShapeDtypeStruct((T, 2 * wq), F32), jax.ShapeDtypeStruct((T, 2 * wq), F32),
                 jax.ShapeDtypeStruct((T, 256), F32)]
    scratch = [pltpu.VMEM((nq, tk, 128), F32), pltpu.VMEM((tk, 128), F32), pltpu.VMEM((nq, 128, T), F32),
               pltpu.VMEM((8, T), F32)]
    if has_f:
        in_specs += [pl.BlockSpec((1, tk, 128), lambda p, j: (p, j, 0)),
                     pl.BlockSpec((1, 8, T), lambda p, j: (p, 0, 0))]
        args += [fcol, frow]
        out_specs += [pl.BlockSpec((1, tk, 128), lambda p, j: (p, j, 0)),
                      pl.BlockSpec((1, T, 128), lambda p, j: (p, 0, 0))]
        out_shape += [jax.ShapeDtypeStruct((2, T, 128), F32), jax.ShapeDtypeStruct((2, T, 128), F32)]
        scratch += [pltpu.VMEM((2, tk, 128), F32), pltpu.VMEM((8, T), F32)]
    return _call_hosting(body, name, (2, T // tk), args, in_specs, out_specs, out_shape, scratch, exch)


def _swa_align(pair, e, h):
    sel = jnp.where(_half_mask(e), pair, 0.0)
    if e == h:
        return sel
    return pltpu.roll(sel, HEAD_DIM, 1)


def _swa_mask(n):
    W = WINDOW
    qi = lax.broadcasted_iota(jnp.int32, (W, 2 * W), 0) + W
    kj = lax.broadcasted_iota(jnp.int32, (W, 2 * W), 1)
    dist = qi - kj
    return (dist >= 0) & (dist < W) & ((n > 0) | (kj >= W))


def swa_fwd(proj, bias, sinks, exch=None):
    T = proj.shape[0]
    W = WINDOW
    nb = T // W
    scale = HEAD_DIM ** -0.5

    def body(sink_ref, q_ref, kp_ref, kc_ref, vp_ref, vc_ref, b_ref, o_ref, l_ref):
        n = pl.program_id(0)
        mask = _swa_mask(n)
        kband = jnp.concatenate([kp_ref[...], kc_ref[...]], axis=0).astype(MXU)
        vband = jnp.concatenate([vp_ref[...], vc_ref[...]], axis=0).astype(MXU)
        lane = lax.broadcasted_iota(jnp.int32, (1, 128), 1)
        lse_tile = jnp.zeros((W, 128), F32)
        pairs = []
        for h in range(2):
            full = []
            for g in range(4):
                hq = 4 * h + g
                qa = _swa_align(q_ref[:, 128 * (hq // 2):128 * (hq // 2 + 1)], hq % 2, h)
                s = _dot_nt(qa, kband) * scale + b_ref[hq]
                s = jnp.where(mask, s, NEG)
                sink = sink_ref[hq]
                m = jnp.maximum(jnp.max(s, axis=1, keepdims=True), sink)
                e = jnp.exp(s - m)
                l = jnp.sum(e, axis=1, keepdims=True) + jnp.exp(sink - m)
                r = jnp.where(_half_mask(h), _dot(e, vband), 0.0) / l
                full.append(r + pltpu.roll(r, HEAD_DIM, 1))
                lse_tile = jnp.where(lane == hq, m + jnp.log(l), lse_tile)
            pairs.append(jnp.where(_half_mask(0), full[0], full[1]))
            pairs.append(jnp.where(_half_mask(0), full[2], full[3]))
        o_ref[...] = jnp.concatenate(pairs, axis=1)
        l_ref[...] = lse_tile

    prev = lambda n: (jnp.maximum(n - 1, 0), C_KA // 128)
    cur = lambda n: (n, C_KA // 128)
    prev_v = lambda n: (jnp.maximum(n - 1, 0), C_VA // 128)
    cur_v = lambda n: (n, C_VA // 128)
    return _call_hosting(
        body, "swa_fwd", (nb,), [sinks, proj, proj, proj, proj, proj, bias],
        [pl.BlockSpec(memory_space=pltpu.SMEM),
         pl.BlockSpec((W, 512), lambda n: (n, 0)),
         pl.BlockSpec((W, 128), prev), pl.BlockSpec((W, 128), cur),
         pl.BlockSpec((W, 128), prev_v), pl.BlockSpec((W, 128), cur_v),
         pl.BlockSpec((8, W, 2 * W), lambda n: (0, 0, 0))],
        [pl.BlockSpec((W, 512), lambda n: (n, 0)), pl.BlockSpec((W, 128), lambda n: (n, 0))],
        [jax.ShapeDtypeStruct((T, 512), F32), jax.ShapeDtypeStruct((T, 128), F32)], [], exch)


def swa_bwd(proj, bias, sinks, do, o, lse, exch=None):
    T = proj.shape[0]
    W = WINDOW
    nb = T // W
    scale = HEAD_DIM ** -0.5

    def body(sink_ref, q_ref, kp_ref, kc_ref, vp_ref, vc_ref, b_ref, do_ref, o_ref, l_ref,
             dq_ref, dk_ref, dv_ref, db_ref, dsk_ref, ck, cv):
        n = pl.program_id(0)

        @pl.when(n == 0)
        def _():
            ck[...] = jnp.zeros_like(ck)
            cv[...] = jnp.zeros_like(cv)
            db_ref[...] = jnp.zeros_like(db_ref)
            dsk_ref[...] = jnp.zeros_like(dsk_ref)

        @pl.when(n < nb)
        def _():
            mask = _swa_mask(n)
            kb32 = jnp.concatenate([kp_ref[...], kc_ref[...]], axis=0)
            vb32 = jnp.concatenate([vp_ref[...], vc_ref[...]], axis=0)
            kband = kb32.astype(MXU)
            sub = lax.broadcasted_iota(jnp.int32, (8, 1), 0)
            dk_band = jnp.zeros((2 * W, 128), F32)
            dv_band = jnp.zeros((2 * W, 128), F32)
            dsk = jnp.zeros((8, 128), F32)
            dq_pairs = []
            mask4 = jnp.concatenate([mask] * 4, axis=0)
            for h in range(2):
                hm = _half_mask(h)
                km = jnp.where(hm, kb32, 0.0).astype(MXU)
                vm = jnp.where(hm, vb32, 0.0).astype(MXU)
                pbs = [slice(128 * ((4 * h + g) // 2), 128 * ((4 * h + g) // 2 + 1)) for g in range(4)]
                q4 = jnp.concatenate([_swa_align(q_ref[:, pbs[g]], g % 2, h) for g in range(4)], axis=0)
                do4 = jnp.concatenate([_swa_align(do_ref[:, pbs[g]], g % 2, h) for g in range(4)], axis=0)
                D4 = jnp.concatenate(
                    [jnp.sum(jnp.where(_half_mask(g % 2), do_ref[:, pbs[g]] * o_ref[:, pbs[g]], 0.0), axis=1,
                             keepdims=True) for g in range(4)], axis=0)
                lse4 = jnp.concatenate([_lane_pick(l_ref[...], 4 * h + g) for g in range(4)], axis=0)
                sink4 = jnp.concatenate([jnp.full((W, 1), sink_ref[4 * h + g], F32) for g in range(4)], axis=0)
                s = _dot_nt(q4, kband) * scale + b_ref[4 * h:4 * h + 4].reshape(4 * W, 2 * W)
                p = jnp.where(mask4, jnp.exp(s - lse4), 0.0)
                sd = jnp.exp(sink4 - lse4) * D4
                for g in range(4):
                    dsk = dsk + jnp.where(sub == 4 * h + g,
                                          -jnp.sum(sd[W * g:W * (g + 1)], axis=0, keepdims=True), 0.0)
                ds = p * (_dot_nt(do4, vm) - D4)
                db_ref[4 * h:4 * h + 4] += ds.reshape(4, W, 2 * W)
                dq = _dot(ds, km) * scale
                dq = dq + pltpu.roll(dq, HEAD_DIM, 1)
                dk_band = dk_band + _dot(ds.T, q4) * scale
                dv_band = dv_band + _dot(p.T, do4)
                dq_pairs.append(jnp.where(_half_mask(0), dq[0:W], dq[W:2 * W]))
                dq_pairs.append(jnp.where(_half_mask(0), dq[2 * W:3 * W], dq[3 * W:4 * W]))
            dq_ref[...] = jnp.concatenate(dq_pairs, axis=1)
            dsk_ref[...] += dsk
            dk_ref[...] = ck[...] + dk_band[0:W]
            dv_ref[...] = cv[...] + dv_band[0:W]
            ck[...] = dk_band[W:2 * W]
            cv[...] = dv_band[W:2 * W]

        @pl.when(n == nb)
        def _():
            dk_ref[...] = ck[...]
            dv_ref[...] = cv[...]

    cl = lambda n: jnp.minimum(n, nb - 1)
    pv = lambda n: jnp.maximum(jnp.minimum(n, nb - 1) - 1, 0)
    return _call_hosting(
        body, "swa_bwd", (nb + 1,), [sinks, proj, proj, proj, proj, proj, bias, do, o, lse],
        [pl.BlockSpec(memory_space=pltpu.SMEM),
         pl.BlockSpec((W, 512), lambda n: (cl(n), 0)),
         pl.BlockSpec((W, 128), lambda n: (pv(n), C_KA // 128)),
         pl.BlockSpec((W, 128), lambda n: (cl(n), C_KA // 128)),
         pl.BlockSpec((W, 128), lambda n: (pv(n), C_VA // 128)),
         pl.BlockSpec((W, 128), lambda n: (cl(n), C_VA // 128)),
         pl.BlockSpec((8, W, 2 * W), lambda n: (0, 0, 0)),
         pl.BlockSpec((W, 512), lambda n: (cl(n), 0)),
         pl.BlockSpec((W, 512), lambda n: (cl(n), 0)),
         pl.BlockSpec((W, 128), lambda n: (cl(n), 0))],
        [pl.BlockSpec((W, 512), lambda n: (cl(n), 0)),
         pl.BlockSpec((W, 128), lambda n: (jnp.maximum(n - 1, 0), 0)),
         pl.BlockSpec((W, 128), lambda n: (jnp.maximum(n - 1, 0), 0)),
         pl.BlockSpec((8, W, 2 * W), lambda n: (0, 0, 0)),
         pl.BlockSpec((8, 128), lambda n: (0, 0))],
        [jax.ShapeDtypeStruct((T, 512), F32), jax.ShapeDtypeStruct((T, 128), F32),
         jax.ShapeDtypeStruct((T, 128), F32), jax.ShapeDtypeStruct((8, W, 2 * W), F32),
         jax.ShapeDtypeStruct((8, 128), F32)],
        [pltpu.VMEM((W, 128), F32), pltpu.VMEM((W, 128), F32)], exch)


def swa_bias_table(rel_bias):
    W = WINDOW
    qi = jnp.arange(W, dtype=jnp.int32)[:, None] + W
    kj = jnp.arange(2 * W, dtype=jnp.int32)[None, :]
    dist = qi - kj
    max_exact = REL_BUCKETS // 2
    d = jnp.maximum(dist, 0)
    log_ratio = jnp.log(jnp.maximum(d, 1).astype(F32) / max_exact) / math.log(REL_MAX_DIST / max_exact)
    large = jnp.minimum(max_exact + (log_ratio * (REL_BUCKETS - max_exact)).astype(jnp.int32), REL_BUCKETS - 1)
    bucket = jnp.where(d < max_exact, d, large)
    bucket = bucket.reshape(-1)
    onehot = (bucket[None, :] == jnp.arange(REL_BUCKETS, dtype=jnp.int32)[:, None]).astype(F32)
    bias = jnp.dot(rel_bias.astype(F32).T, onehot, precision=lax.Precision.HIGHEST)
    return bias.reshape(SWA_Q_HEADS, W, 2 * W), bucket


def attn_out(oa, ob, oc, gn, wout, gpost, x):
    T = x.shape[0]
    tm = _tile(T, 512)

    def body(oa_ref, ob_ref, oc_ref, gn_ref, w_ref, gp_ref, x_ref, x2_ref, y_ref, mT_ref):
        g = gn_ref[...]
        mixed = jnp.concatenate([_rms_fwd(oa_ref[...], g[:, 0:512]), _rms_fwd(ob_ref[...], g[:, 512:768]),
                                 _rms_fwd(oc_ref[...], g[:, 768:1024])], axis=1)
        mT_ref[...] = mixed.T.astype(MXU)
        y = _dot(mixed, w_ref[...])
        y_ref[...] = y
        x2_ref[...] = x_ref[...] + _rms_fwd(y, gp_ref[...])

    row = lambda i: (i, 0)
    const = lambda i: (0, 0)
    return pl.pallas_call(
        body, name="attn_out", grid=(T // tm,),
        in_specs=[pl.BlockSpec((tm, 512), row), pl.BlockSpec((tm, 256), row), pl.BlockSpec((tm, 256), row),
                  pl.BlockSpec((1, 1024), const), pl.BlockSpec((1024, 1024), const), pl.BlockSpec((1, 1024), const),
                  pl.BlockSpec((tm, 1024), row)],
        out_specs=[pl.BlockSpec((tm, 1024), row), pl.BlockSpec((tm, 1024), row),
                   pl.BlockSpec((1024, tm), lambda i: (0, i))],
        out_shape=[jax.ShapeDtypeStruct((T, 1024), F32), jax.ShapeDtypeStruct((T, 1024), F32),
                   jax.ShapeDtypeStruct((1024, T), MXU)],
        compiler_params=_cparams(("parallel",)),
    )(oa, ob, oc, gn, wout, gpost, x)


def attn_out_bwd(dx2, y, oa, ob, oc, gn, wout, gpost):
    T = dx2.shape[0]
    tm = _tile(T, 512)

    def body(dx_ref, y_ref, oa_ref, ob_ref, oc_ref, gn_ref, w_ref, gp_ref,
             dy_ref, da_ref, db_ref, dc_ref, dgn_ref, dgp_ref):
        first = pl.program_id(0) == 0
        dy, dgp = _rms_bwd(dx_ref[...], y_ref[...], gp_ref[...])
        dy_ref[...] = dy.astype(MXU)
        _acc_out(dgp_ref, dgp, first)
        dm = _dot_nt(dy, w_ref[...])
        g = gn_ref[...]
        da, dga = _rms_bwd(dm[:, 0:512], oa_ref[...], g[:, 0:512])
        db, dgb = _rms_bwd(dm[:, 512:768], ob_ref[...], g[:, 512:768])
        dc, dgc = _rms_bwd(dm[:, 768:1024], oc_ref[...], g[:, 768:1024])
        da_ref[...] = da
        db_ref[...] = db
        dc_ref[...] = dc
        _acc_out(dgn_ref, jnp.concatenate([dga, dgb, dgc], axis=1), first)

    row = lambda i: (i, 0)
    const = lambda i: (0, 0)
    return pl.pallas_call(
        body, name="attn_out_bwd", grid=(T // tm,),
        in_specs=[pl.BlockSpec((tm, 1024), row), pl.BlockSpec((tm, 1024), row),
                  pl.BlockSpec((tm, 512), row), pl.BlockSpec((tm, 256), row), pl.BlockSpec((tm, 256), row),
                  pl.BlockSpec((1, 1024), const), pl.BlockSpec((1024, 1024), const), pl.BlockSpec((1, 1024), const)],
        out_specs=[pl.BlockSpec((tm, 1024), row), pl.BlockSpec((tm, 512), row), pl.BlockSpec((tm, 256), row),
                   pl.BlockSpec((tm, 256), row), pl.BlockSpec((1, 1024), const), pl.BlockSpec((1, 1024), const)],
        out_shape=[jax.ShapeDtypeStruct((T, 1024), MXU), jax.ShapeDtypeStruct((T, 512), F32),
                   jax.ShapeDtypeStruct((T, 256), F32), jax.ShapeDtypeStruct((T, 256), F32),
                   jax.ShapeDtypeStruct((1, 1024), F32), jax.ShapeDtypeStruct((1, 1024), F32)],
        compiler_params=_cparams(("arbitrary",)),
    )(dx2, y, oa, ob, oc, gn, wout, gpost)


FF_TILE = 256
_GELU_C = math.sqrt(2.0 / math.pi)


def _gelu(x):
    return 0.5 * x * (1.0 + jnp.tanh(_GELU_C * (x + 0.044715 * x * x * x)))


def _gelu_with_grad(x):
    x2 = x * x
    t = jnp.tanh(_GELU_C * x * (1.0 + 0.044715 * x2))
    h = 0.5 * (1.0 + t)
    return x * h, h + (0.5 * _GELU_C) * x * (1.0 - t * t) * (1.0 + (3 * 0.044715) * x2)


def _conv_taps(u, hal_ref, first):
    row = lax.broadcasted_iota(jnp.int32, (8, 1), 0)
    h6 = jnp.where(first, 0.0, hal_ref[6:7, :])
    h7 = jnp.where(first, 0.0, hal_ref[7:8, :])
    r1, r2 = pltpu.roll(u, 1, 0), pltpu.roll(u, 2, 0)
    r1 = jnp.concatenate([jnp.where(row == 0, h7, r1[0:8]), r1[8:]], axis=0)
    r2 = jnp.concatenate([jnp.where(row == 0, h6, jnp.where(row == 1, h7, r2[0:8])), r2[8:]], axis=0)
    return r1, r2


def ffn_fwd(u0, convw, convb, wdown, gpost, x2, exch=None):
    T = x2.shape[0]
    tm, tn = _tile(T, 1024), FF_TILE
    nj = D_FF // tn

    def body(ug_ref, uu_ref, hg_ref, hu_ref, wg_ref, wu_ref, bg_ref, bu_ref, wd_ref, gp_ref, x_ref, wdp_ref,
             x3_ref, y_ref, aT_ref, acc, a_sc):
        i, j = pl.program_id(0), pl.program_id(1)
        first = i == 0

        @pl.when(j == 0)
        def _():
            acc[...] = jnp.zeros_like(acc)
            a_sc[...] = jnp.zeros_like(a_sc)

        acc[...] += _dot(a_sc[...], wdp_ref[...])

        def conv(u_ref, h_ref, w_ref, b_ref):
            u = u_ref[...]
            r1, r2 = _conv_taps(u, h_ref, first)
            return b_ref[...] + w_ref[0:1, :] * r2 + w_ref[1:2, :] * r1 + w_ref[2:3, :] * u

        a = _gelu(conv(ug_ref, hg_ref, wg_ref, bg_ref)) * conv(uu_ref, hu_ref, wu_ref, bu_ref)
        aT_ref[...] = a.T.astype(MXU)
        a_sc[...] = a.astype(MXU)

        @pl.when(j == nj - 1)
        def _():
            y = acc[...] + _dot(a_sc[...], wd_ref[...])
            y_ref[...] = y
            x3_ref[...] = x_ref[...] + _rms_fwd(y, gp_ref[...])

    halo = lambda off: (lambda i, j: (jnp.maximum(i * (tm // 8) - 1, 0), off + j))
    return _call_hosting(
        body, "ffn_fwd", (T // tm, nj), [u0, u0, u0, u0, convw, convw, convb, convb, wdown, gpost, x2, wdown],
        [pl.BlockSpec((tm, tn), lambda i, j: (i, j)), pl.BlockSpec((tm, tn), lambda i, j: (i, nj + j)),
         pl.BlockSpec((8, tn), halo(0)), pl.BlockSpec((8, tn), halo(nj)),
         pl.BlockSpec((3, tn), lambda i, j: (0, j)), pl.BlockSpec((3, tn), lambda i, j: (0, nj + j)),
         pl.BlockSpec((1, tn), lambda i, j: (0, j)), pl.BlockSpec((1, tn), lambda i, j: (0, nj + j)),
         pl.BlockSpec((tn, 1024), lambda i, j: (j, 0)),
         pl.BlockSpec((1, 1024), lambda i, j: (0, 0)),
         pl.BlockSpec((tm, 1024), lambda i, j: (i, 0)),
         pl.BlockSpec((tn, 1024), lambda i, j: (jnp.maximum(j - 1, 0), 0))],
        [pl.BlockSpec((tm, 1024), lambda i, j: (i, 0)), pl.BlockSpec((tm, 1024), lambda i, j: (i, 0)),
         pl.BlockSpec((tn, tm), lambda i, j: (j, i))],
        [jax.ShapeDtypeStruct((T, 1024), F32), jax.ShapeDtypeStruct((T, 1024), F32),
         jax.ShapeDtypeStruct((D_FF, T), MXU)],
        [pltpu.VMEM((tm, 1024), F32), pltpu.VMEM((tm, tn), MXU)], exch)


def ffn_bwd(dx3, y, u0, convw, convb, wdown, gpost, wupT, x2, gfpre, exch=None):
    T = dx3.shape[0]
    tm, tn = _tile(T, 512), FF_TILE
    nj = D_FF // tn
    ni = T // tm

    def body(dx_ref, y_ref, ug_ref, uu_ref, hg_ref, hu_ref, wg_ref, wu_ref, bg_ref, bu_ref, wd_ref, gp_ref,
             wtg_ref, wtu_ref, x2_ref, gf_ref, wdn_ref, wtgp_ref, wtup_ref,
             dy_ref, dug_ref, duu_ref, dcg_ref, dcu_ref, dgp_ref, dx2_ref, dgf_ref,
             dy_sc, dh_sc, da_sc, dug_sc, duu_sc, cg, cu, ag, au):
        s, j = pl.program_id(0), pl.program_id(1)
        i = ni - 1 - s
        first_tok = i == 0
        sub = lax.broadcasted_iota(jnp.int32, (8, 1), 0)
        slot = j % 2

        @pl.when(j == 0)
        def _():
            dy, dgp = _rms_bwd(dx_ref[...], y_ref[...], gp_ref[...])
            dy_sc[...] = dy.astype(MXU)
            dy_ref[...] = dy.astype(MXU)
            _acc_out(dgp_ref, dgp, s == 0)
            dh_sc[...] = jnp.zeros_like(dh_sc)
            da_sc[0] = _dot_nt(dy.astype(MXU), wd_ref[...])
            dug_sc[...] = jnp.zeros_like(dug_sc)
            duu_sc[...] = jnp.zeros_like(duu_sc)

        @pl.when(s == 0)
        def _():
            cg[j] = jnp.zeros((8, tn), F32)
            cu[j] = jnp.zeros((8, tn), F32)
            ag[j] = jnp.zeros((8, tn), F32)
            au[j] = jnp.zeros((8, tn), F32)

        da = da_sc[slot]
        da_sc[1 - slot] = _dot_nt(dy_sc[...], wdn_ref[...])
        dh_sc[...] += _dot(dug_sc[...], wtgp_ref[...]) + _dot(duu_sc[...], wtup_ref[...])

        def conv(u_ref, h_ref, w_ref, b_ref):
            u = u_ref[...]
            r1, r2 = _conv_taps(u, h_ref, first_tok)
            return b_ref[...] + w_ref[0:1, :] * r2 + w_ref[1:2, :] * r1 + w_ref[2:3, :] * u, u, r1, r2

        gate, ugv, g1, g2 = conv(ug_ref, hg_ref, wg_ref, bg_ref)
        up, uuv, u1, u2 = conv(uu_ref, hu_ref, wu_ref, bu_ref)
        gl, dgl = _gelu_with_grad(gate)
        dup = da * gl
        dgate = da * up * dgl

        def conv_bwd(du, u, r1, r2, w_ref, c_ref, a_ref, duT_ref, du_sc):
            nxt = c_ref[j]
            n0, n1 = nxt[0:1, :], nxt[1:2, :]
            f1, f2 = pltpu.roll(du, tm - 1, 0), pltpu.roll(du, tm - 2, 0)
            f1 = jnp.concatenate([f1[:tm - 8], jnp.where(sub == 7, n0, f1[tm - 8:])], axis=0)
            f2 = jnp.concatenate([f2[:tm - 8], jnp.where(sub == 7, n1, jnp.where(sub == 6, n0, f2[tm - 8:]))], axis=0)
            du0 = w_ref[2:3, :] * du + w_ref[1:2, :] * f1 + w_ref[0:1, :] * f2
            duT_ref[...] = du0.T.astype(MXU)
            du_sc[...] = du0.astype(MXU)
            c_ref[j] = du[0:8, :]
            red = lambda v: jnp.sum(v, axis=0, keepdims=True)
            part = jnp.where(sub == 0, red(du * r2), jnp.where(sub == 1, red(du * r1), jnp.where(
                sub == 2, red(du * u), jnp.where(sub == 3, red(du), 0.0))))
            a_ref[j] = a_ref[j] + part
            return a_ref[j]

        dcg_ref[0] = conv_bwd(dgate, ugv, g1, g2, wg_ref, cg, ag, dug_ref, dug_sc)
        dcu_ref[0] = conv_bwd(dup, uuv, u1, u2, wu_ref, cu, au, duu_ref, duu_sc)

        @pl.when(j == nj - 1)
        def _():
            dh = dh_sc[...] + _dot(dug_sc[...], wtg_ref[...]) + _dot(duu_sc[...], wtu_ref[...])
            dx, dgf = _rms_bwd(dh, x2_ref[...], gf_ref[...])
            dx2_ref[...] = dx_ref[...] + dx
            _acc_out(dgf_ref, dgf, s == 0)

    rev = lambda s: ni - 1 - s
    halo = lambda off: (lambda s, j: (jnp.maximum(rev(s) * (tm // 8) - 1, 0), off + j))
    tok = pl.BlockSpec((tm, 1024), lambda s, j: (rev(s), 0))
    vec = pl.BlockSpec((1, 1024), lambda s, j: (0, 0))
    return _call_hosting(
        body, "ffn_bwd", (ni, nj),
        [dx3, y, u0, u0, u0, u0, convw, convw, convb, convb, wdown, gpost, wupT, wupT, x2, gfpre,
         wdown, wupT, wupT],
        [tok, tok,
         pl.BlockSpec((tm, tn), lambda s, j: (rev(s), j)), pl.BlockSpec((tm, tn), lambda s, j: (rev(s), nj + j)),
         pl.BlockSpec((8, tn), halo(0)), pl.BlockSpec((8, tn), halo(nj)),
         pl.BlockSpec((3, tn), lambda s, j: (0, j)), pl.BlockSpec((3, tn), lambda s, j: (0, nj + j)),
         pl.BlockSpec((1, tn), lambda s, j: (0, j)), pl.BlockSpec((1, tn), lambda s, j: (0, nj + j)),
         pl.BlockSpec((tn, 1024), lambda s, j: (j, 0)), vec,
         pl.BlockSpec((tn, 1024), lambda s, j: (j, 0)), pl.BlockSpec((tn, 1024), lambda s, j: (nj + j, 0)),
         tok, vec,
         pl.BlockSpec((tn, 1024), lambda s, j: (jnp.minimum(j + 1, nj - 1), 0)),
         pl.BlockSpec((tn, 1024), lambda s, j: (jnp.maximum(j - 1, 0), 0)),
         pl.BlockSpec((tn, 1024), lambda s, j: (nj + jnp.maximum(j - 1, 0), 0))],
        [tok,
         pl.BlockSpec((tn, tm), lambda s, j: (j, rev(s))), pl.BlockSpec((tn, tm), lambda s, j: (j, rev(s))),
         pl.BlockSpec((1, 8, tn), lambda s, j: (s, 0, j)), pl.BlockSpec((1, 8, tn), lambda s, j: (s, 0, j)),
         vec, tok, vec],
        [jax.ShapeDtypeStruct((T, 1024), MXU), jax.ShapeDtypeStruct((D_FF, T), MXU),
         jax.ShapeDtypeStruct((D_FF, T), MXU),
         jax.ShapeDtypeStruct((ni, 8, D_FF), F32), jax.ShapeDtypeStruct((ni, 8, D_FF), F32),
         jax.ShapeDtypeStruct((1, 1024), F32), jax.ShapeDtypeStruct((T, 1024), F32),
         jax.ShapeDtypeStruct((1, 1024), F32)],
        [pltpu.VMEM((tm, 1024), MXU), pltpu.VMEM((tm, 1024), F32), pltpu.VMEM((2, tm, tn), F32),
         pltpu.VMEM((tm, tn), MXU), pltpu.VMEM((tm, tn), MXU)] + [pltpu.VMEM((nj, 8, tn), F32)] * 4, exch)


ELEMS_PER_BLOCK = 512 * 1024


def _row_block(R, C):
    if R * C <= ELEMS_PER_BLOCK or R % 8:
        return R
    best = 8
    for t in range(8, R + 1, 8):
        if R % t == 0 and t * C <= ELEMS_PER_BLOCK:
            best = t
    return best


def adamw(w, g, m, v, name, exch=None):
    L, R, C = w.shape
    partials = isinstance(g, (list, tuple))
    tr = _row_block(R, 2 * C)
    c1 = 1.0 - ADAM_B1 ** ADAM_STEP
    c2 = 1.0 - ADAM_B2 ** ADAM_STEP

    def body(w_ref, *rest):
        g_refs, (m_ref, v_ref, g_out, d_ref, nm_ref, nv_ref) = rest[:-6], rest[-6:]

        def step(gv):
            g_out[0] = gv
            nm = ADAM_B1 * m_ref[0] + (1.0 - ADAM_B1) * gv
            nv = ADAM_B2 * v_ref[0] + (1.0 - ADAM_B2) * (gv * gv)
            nm_ref[0] = nm
            nv_ref[0] = nv
            d_ref[0] = -ADAM_LR * ((nm / c1) / (jnp.sqrt(nv / c2) + ADAM_EPS) + ADAM_WD * w_ref[0])

        if not partials:
            step(g_refs[0][0])
            return
        for k in range(L):
            @pl.when(pl.program_id(0) == k)
            def _(k=k):
                gv = g_refs[k][0].astype(F32)
                for d in range(1, N_DEV):
                    gv = gv + g_refs[k][d].astype(F32)
                step(gv)

    spec = pl.BlockSpec((1, tr, C), lambda l, i: (l, i, 0))
    if partials:
        gspecs = [pl.BlockSpec((N_DEV, tr, C), lambda l, i, k=k: (0, jnp.where(l == k, i, 0), 0)) for k in range(L)]
        gs = list(g)
    else:
        gspecs, gs = [spec], [g]
    return _call_hosting(body, name, (L, R // tr), [w] + gs + [m, v], [spec] + gspecs + [spec, spec], [spec] * 4,
                         [jax.ShapeDtypeStruct((L, R, C), F32)] * 4, [], exch)


def sum_devices(buf, name):
    _, R, C = buf.shape
    tr = _row_block(R, C * 4)

    def body(b_ref, o_ref):
        acc = b_ref[0].astype(F32)
        for d in range(1, N_DEV):
            acc = acc + b_ref[d].astype(F32)
        o_ref[...] = acc

    return pl.pallas_call(
        body, name=name, grid=(R // tr,),
        in_specs=[pl.BlockSpec((N_DEV, tr, C), lambda i: (0, i, 0))],
        out_specs=pl.BlockSpec((tr, C), lambda i: (i, 0)),
        out_shape=jax.ShapeDtypeStruct((R, C), F32),
        compiler_params=_cparams(("parallel",)),
    )(buf)


def _exchange_copies(src_refs, out_refs, send_sems, recv_sems, gather):
    x, y, c = lax.axis_index("x"), lax.axis_index("y"), lax.axis_index("c")
    me = 4 * x + 2 * y + c
    flip = lambda a, bit: 1 - a if bit else a
    part = lambda ref, d: ref if gather else ref.at[d]
    copies = []
    for k in range(1, N_DEV):
        px, py, pc = flip(x, (k >> 2) & 1), flip(y, (k >> 1) & 1), flip(c, k & 1)
        peer = 4 * px + 2 * py + pc
        for t in range(len(src_refs)):
            sem = t * (N_DEV - 1) + k - 1
            mk = lambda s, d: pltpu.make_async_remote_copy(
                src_ref=s, dst_ref=d, send_sem=send_sems.at[sem], recv_sem=recv_sems.at[sem],
                device_id=(px, py, pc), device_id_type=pl.DeviceIdType.MESH)
            copies.append((mk(part(src_refs[t], peer), out_refs[t].at[me]),
                           mk(part(src_refs[t], me), out_refs[t].at[peer])))
    return me, copies


def exchange(srcs, name, gather):
    n = len(srcs)
    shapes = [(N_DEV,) + s.shape if gather else s.shape for s in srcs]

    def body(*refs):
        src_refs, out_refs = refs[:n], refs[n:2 * n]
        send_sems, recv_sems, local_sems = refs[2 * n:]
        me, copies = _exchange_copies(src_refs, out_refs, send_sems, recv_sems, gather)
        for outgoing, _ in copies:
            outgoing.start()
        mine = [pltpu.make_async_copy(src_refs[t] if gather else src_refs[t].at[me], out_refs[t].at[me],
                                      local_sems.at[t]) for t in range(n)]
        for cp in mine:
            cp.start()
        for _, incoming in copies:
            incoming.wait_recv()
        for outgoing, _ in copies:
            outgoing.wait_send()
        for cp in mine:
            cp.wait()

    return pl.pallas_call(
        body, name=name,
        in_specs=[pl.BlockSpec(memory_space=pl.ANY)] * n, out_specs=[pl.BlockSpec(memory_space=pl.ANY)] * n,
        out_shape=[jax.ShapeDtypeStruct(shp, s.dtype) for shp, s in zip(shapes, srcs)],
        scratch_shapes=[pltpu.SemaphoreType.DMA((n * (N_DEV - 1),)), pltpu.SemaphoreType.DMA((n * (N_DEV - 1),)),
                        pltpu.SemaphoreType.DMA((n,))],
    )(*srcs)


def hosted_exchange(body, n_in, n_out, n_scratch, grid, srcs, gather):
    n = len(srcs)
    shapes = [(N_DEV,) + s.shape if gather else s.shape for s in srcs]

    def wrapped(*refs):
        ins, xin = refs[:n_in], refs[n_in:n_in + n]
        outs = refs[n_in + n:n_in + n + n_out]
        xout = refs[n_in + n + n_out:n_in + 2 * n + n_out]
        rest = refs[n_in + 2 * n + n_out:]
        scratch, (send_sems, recv_sems, local_sems) = rest[:n_scratch], rest[n_scratch:]
        ids = [pl.program_id(a) for a in range(len(grid))]
        first = functools.reduce(jnp.logical_and, [i == 0 for i in ids])
        last = functools.reduce(jnp.logical_and, [i == g - 1 for i, g in zip(ids, grid)])
        me, copies = _exchange_copies(xin, xout, send_sems, recv_sems, gather)
        mine = [pltpu.make_async_copy(xin[t] if gather else xin[t].at[me], xout[t].at[me], local_sems.at[t])
                for t in range(n)]

        @pl.when(first)
        def _():
            for outgoing, _ in copies:
                outgoing.start()
            for cp in mine:
                cp.start()

        body(*ins, *outs, *scratch)

        @pl.when(last)
        def _():
            for _, incoming in copies:
                incoming.wait_recv()
            for outgoing, _ in copies:
                outgoing.wait_send()
            for cp in mine:
                cp.wait()

    any_spec = pl.BlockSpec(memory_space=pl.ANY)
    return wrapped, (list(srcs), [any_spec] * n, [any_spec] * n,
                     [jax.ShapeDtypeStruct(shp, s.dtype) for shp, s in zip(shapes, srcs)],
                     [pltpu.SemaphoreType.DMA((n * (N_DEV - 1),)), pltpu.SemaphoreType.DMA((n * (N_DEV - 1),)),
                      pltpu.SemaphoreType.DMA((n,))])


def _pack(parts, cols, row_align, dtype):
    flat = jnp.concatenate([p.astype(dtype) for p in parts], axis=-1)
    n = flat.shape[-1]
    block = cols * row_align
    total = -(-n // block) * block
    flat = jnp.pad(flat, [(0, 0)] * (flat.ndim - 1) + [(0, total - n)])
    return flat.reshape(flat.shape[:-1] + (total // cols, cols))


def _unpack(buf, shapes):
    lead = buf.shape[:-2]
    flat = buf.reshape(lead + (-1,))
    out, off = [], 0
    for s in shapes:
        n = int(np.prod(s))
        out.append(flat[..., off:off + n].reshape(lead + tuple(s)))
        off += n
    return out


SHARDED = ["w_in", "w_uq", "w_ukv", "w_out", "w_up", "w_down"]
ATTN_SENT = ["w_in_p", "w_uq", "w_ukv", "w_out"]
UP_HALF = 352
FFN_SIDE = ["w_upT", "conv_w", "w_down"]


def _full_from_shards(name, s):
    if name in ("w_in", "w_in_p", "w_out", "w_down", "w_upT"):
        return s.reshape((-1, s.shape[-1]))
    return s.transpose(1, 0, 2).reshape((s.shape[1], -1))


def _shards_from_full(name, f):
    if name in ("w_in", "w_in_p", "w_out", "w_down", "w_upT"):
        return f.reshape((N_DEV, -1, f.shape[-1]))
    return f.reshape((f.shape[0], N_DEV, -1)).transpose(1, 0, 2)


def _perm_w_in(w):
    z = lambda n: jnp.zeros(w.shape[:-1] + (n,), w.dtype)
    return jnp.concatenate([w[..., :1536], w[..., 1540:1924], w[..., 1536:1540], z(60), w[..., 1924:1956], z(32)],
                           axis=-1)


def _unperm_w_in(d):
    return jnp.concatenate([d[..., :1536], d[..., 1920:1924], d[..., 1536:1920], d[..., 1984:2016]], axis=-1)


def _perm_w_uq(w):
    return jnp.pad(w.reshape(256, 4, MLA_QK_DIM), ((0, 0), (0, 0), (0, 128 - MLA_QK_DIM))).reshape(256, 512)


def _unperm_w_uq(d):
    return d.reshape(256, 4, 128)[:, :, :MLA_QK_DIM].reshape(256, 4 * MLA_QK_DIM)


def _perm_w_ukv(w):
    w4 = w.reshape(128, 4, 128)
    k = jnp.pad(w4[:, :, :64], ((0, 0), (0, 0), (0, 64))).reshape(128, 512)
    return jnp.concatenate([k, w4[:, :, 64:].reshape(128, 256)], axis=1)


def _unperm_w_ukv(d):
    dk = d[:, :512].reshape(128, 4, 128)[:, :, :64]
    dv = d[:, 512:].reshape(128, 4, 64)
    return jnp.concatenate([dk, dv], axis=-1).reshape(128, 512)


def _row(v, width=None):
    v = v.reshape(1, -1).astype(F32)
    if width is not None and v.shape[1] < width:
        v = jnp.pad(v, ((0, 0), (0, width - v.shape[1])))
    return v


def _layer_fwd(x, P, shared, send=None, ffn_from=None):
    cosr, sinr, bias = shared
    ex = lambda host: (send[host], True) if send is not None and send.get(host) else None
    proj, hT, projb = norm_matmul(x, P["g_pre"], P["w_in_p"], "in_proj", lo_tiles=2, tn_pref=1024)
    qm, km, vm, cqT, ckvT = mla_prep(proj, P["gq"], P["gkv"], P["w_uq_p"], P["w_ukv_p"], cosr, sinr)
    fcol, frow, frep = fox_gate(proj, P["fbias"])
    (oa, lse_a), _ = swa_fwd(proj, bias, P["sinks"])
    (ob, lrb), got_fox = flash_fwd(projb, projb, projb, frep, frow, qblk=C_QF // 128, kblk=C_KF // 128,
                                   vblk=C_VF // 128, nq=1, scale=HEAD_DIM ** -0.5, name="fox_fwd", exch=ex("fox"))
    (oc, lrc), got_mla = flash_fwd(qm, km, vm, None, None, qblk=0, kblk=0, vblk=0, nq=2,
                                   scale=MLA_QK_DIM ** -0.5, name="mla_fwd", exch=ex("mla"))
    x2, y1, mT = attn_out(oa, ob, oc, P["gn"], P["w_out"], P["g_apost"], x)
    if ffn_from is not None:
        P = dict(P, **ffn_from(None, got_fox, got_mla))
    up = norm_matmul(x2, P["g_fpre"], P["w_upT"], "up_proj", tn_pref=1536, w_transposed=True,
                     h_transposed=False, exch=ex("up"))
    (u0, h2), got_up = up if ex("up") is not None else (up, None)
    if ffn_from is not None:
        P = dict(P, **ffn_from(got_up, None, None))
    (x3, y2, aT), got_ffn = ffn_fwd(u0, P["conv_w"], P["conv_b"], P["w_down"], P["g_fpost"], x2, exch=ex("ffn"))
    S = dict(x=x, proj=proj, projb=projb, hT=hT, qm=qm, km=km, vm=vm, cqT=cqT, ckvT=ckvT, fcol=fcol, frow=frow,
             oa=oa, lse_a=lse_a, ob=ob, lrb=lrb, oc=oc, lrc=lrc,
             x2=x2, y1=y1, mT=mT, u0=u0, h2=h2, y2=y2, aT=aT)
    return x3, S, P, got_ffn


def _layer_bwd(dx3, P, S, shared, send_attn=None):
    cosr, sinr, bias = shared
    proj = S["proj"]
    G = {}
    got = {}
    ex = lambda arrays: (arrays, False) if send_attn is not None and arrays else None
    (dy2, dugT, duuT, dcg, dcu, G["ffn_post_norm"], dx2, G["ffn_pre_norm"]), got["ffn"] = ffn_bwd(
        dx3, S["y2"], S["u0"], P["conv_w"], P["conv_b"], P["w_down"], P["g_fpost"], P["w_upT"], S["x2"], P["g_fpre"],
        exch=ex(send_attn))
    dconv = jnp.concatenate([dcg[-1], dcu[-1]], axis=1)
    G["conv_w"], G["conv_b"] = dconv[0:3], dconv[3]
    G["w_down"] = matmul_nn(S["aT"], dy2, "dw_down", MXU)
    G["w_upT"] = jnp.concatenate([matmul_nn(dugT, S["h2"], "dw_up_gate", MXU),
                                  matmul_nn(duuT, S["h2"], "dw_up_up", MXU)], axis=0)
    G["w_up"] = G["w_upT"].T
    dy1, doa, dob, doc, G["group_norm"], G["attn_post_norm"] = attn_out_bwd(
        dx2, S["y1"], S["oa"], S["ob"], S["oc"], P["gn"], P["w_out"], P["g_apost"])
    G["w_out"] = matmul_nn(S["mT"], dy1, "dw_out", MXU)
    up_slices = _shards_from_full("w_upT", G["w_upT"])
    (dqa, dka, dva, dbias, dsk), got["swa"] = swa_bwd(proj, bias, P["sinks"], doa, S["oa"], S["lse_a"],
                                                      exch=ex([_shards_from_full("w_down", G["w_down"])]))
    G["swa_sinks"] = dsk[:, 0]
    pb = S["projb"]
    (dqf, dkf, dvf, dFk, dFq), got["fox"] = flash_bwd(
        pb, pb, pb, dob, S["ob"], S["lrb"], S["fcol"], S["frow"], name="fox_bwd", qblk=C_QF // 128,
        kblk=C_KF // 128, vblk=C_VF // 128, nq=1, scale=HEAD_DIM ** -0.5, exch=ex([up_slices[:, :UP_HALF]]))
    dmisc_f, dfb = fox_gate_bwd(dFq, dFk, proj, P["fbias"])
    G["forget_bias"] = dfb[0, 0:4]
    (dqm_, dkm_, dvm_), got["mla"] = flash_bwd(
        S["qm"], S["km"], S["vm"], doc, S["oc"], S["lrc"], None, None, name="mla_bwd",
        qblk=0, kblk=0, vblk=0, nq=2, scale=MLA_QK_DIM ** -0.5, exch=ex([up_slices[:, UP_HALF:]]))
    dqm, dkv, dcq, dckv, dmisc_r, G["q_latent_norm"], G["kv_latent_norm"] = mla_prep_bwd(
        dqm_, dkm_, dvm_, proj, P["gq"], P["gkv"], P["w_uq_p"], P["w_ukv_p"], cosr, sinr)
    G["w_uq"] = _unperm_w_uq(matmul_nn(S["cqT"], dqm, "dw_uq", MXU))
    G["w_ukv"] = _unperm_w_ukv(matmul_nn(S["ckvT"], dkv, "dw_ukv", MXU))
    dproj = jnp.concatenate([dqa, dka, dva, dqf, dkf, dvf, dcq, dckv, dmisc_f + dmisc_r], axis=1).astype(MXU)
    G["w_in_p"] = matmul_nn(S["hT"], dproj, "dw_in", MXU)
    G["w_in"] = _unperm_w_in(G["w_in_p"])
    dx, G["attn_pre_norm"] = matmul_nt_normbwd(dproj, P["w_in_p"], S["x"], P["g_pre"], dx2, "in_bwd")
    return dx, G, dbias, got


def _layer_params(l, full, small):
    return dict(
        g_pre=_row(small["attn_pre_norm"][l]),
        w_in_p=full["w_in_p"] if "w_in_p" in full else _perm_w_in(full["w_in"]),
        gq=_row(small["q_latent_norm"][l]), gkv=_row(small["kv_latent_norm"][l]),
        w_uq_p=_perm_w_uq(full["w_uq"]), w_ukv_p=_perm_w_ukv(full["w_ukv"]),
        fbias=_row(small["forget_bias"][l], 128), sinks=small["swa_sinks"][l].astype(F32),
        gn=_row(small["group_norm"][l]), w_out=full["w_out"], g_apost=_row(small["attn_post_norm"][l]),
        g_fpre=_row(small["ffn_pre_norm"][l]), conv_b=_row(small["conv_b"][l]),
        g_fpost=_row(small["ffn_post_norm"][l]),
        **{n: full[n] for n in FFN_SIDE if n in full},
        **({"w_upT": full["w_up"].T} if "w_up" in full else {}))


def _rel_bias_grad(dbias, bucket):
    flat = dbias.reshape(SWA_Q_HEADS, -1)
    hi = flat.astype(MXU)
    lo = (flat - hi.astype(F32)).astype(MXU)
    onehot = (bucket[:, None] == jnp.arange(128, dtype=jnp.int32)[None, :]).astype(MXU)
    r = matmul_nn(jnp.concatenate([hi, lo], axis=0), onehot, "rel_bias_grad")
    return (r[0:8] + r[8:16])[:, :REL_BUCKETS].T


def local_step(x, tgt, fulls, small, comm=None):
    T = x.shape[0]
    cosr, sinr = rope_tables(T)
    bias, bucket = swa_bias_table(small["rel_bias"])
    shared = (cosr, sinr, bias)
    Ps, Ss = [], []
    h, full = x, fulls[0]
    for l in range(DEPTH):
        P = _layer_params(l, full, small)
        if comm:
            h, S, P, got = _layer_fwd(h, P, shared, comm["weight_parts"](l), comm["ffn_from"])
            full = comm["attn_from"](got) if l + 1 < DEPTH else None
        else:
            h, S, P, _ = _layer_fwd(h, P, shared)
            full = fulls[l + 1] if l + 1 < DEPTH else None
        Ps.append(P)
        Ss.append(S)
    dh, sq = loss_kernel(h, tgt)
    grads = [None] * DEPTH
    dbias_sum = None
    pending = [] if comm else None
    for l in reversed(range(DEPTH)):
        dh, grads[l], dbias, got = _layer_bwd(dh, Ps[l], Ss[l], shared, pending)
        dbias_sum = dbias if dbias_sum is None else dbias_sum + dbias
        if comm:
            comm["landed"](l, ["w_down"], got["swa"])
            comm["landed"](l, ["w_upT"], [jnp.concatenate([got["fox"][0], got["mla"][0]], axis=1)])
            if pending:
                comm["landed"](l + 1, ATTN_SENT, got["ffn"])
            pending = [_shards_from_full(n, grads[l][n]) for n in ATTN_SENT]
    return sq, dh, grads, _rel_bias_grad(dbias_sum, bucket), pending


WEIGHTS = ['attn_pre_norm', 'w_in', 'forget_bias', 'swa_sinks', 'rel_bias', 'q_latent_norm', 'w_uq',
           'kv_latent_norm', 'w_ukv', 'group_norm', 'w_out', 'attn_post_norm', 'ffn_pre_norm', 'w_up', 'conv_w',
           'conv_b', 'w_down', 'ffn_post_norm']
SMALL_PER_LAYER = ['attn_pre_norm', 'forget_bias', 'swa_sinks', 'q_latent_norm', 'kv_latent_norm', 'group_norm',
                   'attn_post_norm', 'ffn_pre_norm', 'conv_b', 'ffn_post_norm', 'conv_w']


def kernel(x, attn_pre_norm, w_in, forget_bias, swa_sinks, rel_bias, q_latent_norm, w_uq, kv_latent_norm, w_ukv, group_norm, w_out, attn_post_norm, ffn_pre_norm, w_up, conv_w, conv_b, w_down, ffn_post_norm, loss_target, m_attn_pre_norm, m_w_in, m_forget_bias, m_swa_sinks, m_rel_bias, m_q_latent_norm, m_w_uq, m_kv_latent_norm, m_w_ukv, m_group_norm, m_w_out, m_attn_post_norm, m_ffn_pre_norm, m_w_up, m_conv_w, m_conv_b, m_w_down, m_ffn_post_norm, v_attn_pre_norm, v_w_in, v_forget_bias, v_swa_sinks, v_rel_bias, v_q_latent_norm, v_w_uq, v_kv_latent_norm, v_w_ukv, v_group_norm, v_w_out, v_attn_post_norm, v_ffn_pre_norm, v_w_up, v_conv_w, v_conv_b, v_w_down, v_ffn_post_norm):
    W = dict(attn_pre_norm=attn_pre_norm, w_in=w_in, forget_bias=forget_bias, swa_sinks=swa_sinks, rel_bias=rel_bias,
             q_latent_norm=q_latent_norm, w_uq=w_uq, kv_latent_norm=kv_latent_norm, w_ukv=w_ukv,
             group_norm=group_norm, w_out=w_out, attn_post_norm=attn_post_norm, ffn_pre_norm=ffn_pre_norm,
             w_up=w_up, conv_w=conv_w, conv_b=conv_b, w_down=w_down, ffn_post_norm=ffn_post_norm)
    M = dict(attn_pre_norm=m_attn_pre_norm, w_in=m_w_in, forget_bias=m_forget_bias, swa_sinks=m_swa_sinks,
             rel_bias=m_rel_bias, q_latent_norm=m_q_latent_norm, w_uq=m_w_uq, kv_latent_norm=m_kv_latent_norm,
             w_ukv=m_w_ukv, group_norm=m_group_norm, w_out=m_w_out, attn_post_norm=m_attn_post_norm,
             ffn_pre_norm=m_ffn_pre_norm, w_up=m_w_up, conv_w=m_conv_w, conv_b=m_conv_b, w_down=m_w_down,
             ffn_post_norm=m_ffn_post_norm)
    V = dict(attn_pre_norm=v_attn_pre_norm, w_in=v_w_in, forget_bias=v_forget_bias, swa_sinks=v_swa_sinks,
             rel_bias=v_rel_bias, q_latent_norm=v_q_latent_norm, w_uq=v_w_uq, kv_latent_norm=v_kv_latent_norm,
             w_ukv=v_w_ukv, group_norm=v_group_norm, w_out=v_w_out, attn_post_norm=v_attn_post_norm,
             ffn_pre_norm=v_ffn_pre_norm, w_up=v_w_up, conv_w=v_conv_w, conv_b=v_conv_b, w_down=v_w_down,
             ffn_post_norm=v_ffn_post_norm)
    me = 4 * lax.axis_index("x") + 2 * lax.axis_index("y") + lax.axis_index("c")

    def attn_shards(l):
        return [_perm_w_in(w_in[l].astype(MXU))] + [W[n][l].astype(MXU) for n in ATTN_SENT[1:]]

    def weight_parts(l):
        up = jnp.swapaxes(W["w_up"][l], 0, 1).astype(MXU)
        return dict(up=[W["w_down"][l].astype(MXU)], fox=[up[:UP_HALF]], mla=[up[UP_HALF:], conv_w[l]],
                    ffn=attn_shards(l + 1) if l + 1 < DEPTH else [])

    def ffn_from(got_up, got_fox, got_mla):
        if got_up is not None:
            return dict(w_down=_full_from_shards("w_down", got_up[0]))
        return dict(w_upT=_full_from_shards("w_upT", jnp.concatenate([got_fox[0], got_mla[0]], axis=1)),
                    conv_w=got_mla[1].transpose(1, 0, 2).reshape(3, 2 * D_FF))

    def attn_from(got_ffn):
        return {n: _full_from_shards(n, s) for n, s in zip(ATTN_SENT, got_ffn)}

    landed = [{} for _ in range(DEPTH)]

    def on_landed(l, names, arrays):
        landed[l].update(zip(names, arrays))

    comm = dict(weight_parts=weight_parts, ffn_from=ffn_from, attn_from=attn_from, landed=on_landed)
    full0 = dict(zip(ATTN_SENT, map(_full_from_shards, ATTN_SENT, exchange(attn_shards(0), "gather_weights", True))))
    sq, dx, grads, drel, last = local_step(x[0], loss_target[0], [full0], W, comm)
    G, delta, new_m, new_v = {}, {}, {}, {}

    def update(n, exch=None):
        shp = W[n].shape
        if n == "w_up":
            v3 = lambda a: jnp.swapaxes(a, 1, 2)
            back = v3
            g = [landed[l]["w_upT"] for l in range(DEPTH)]
        else:
            v3 = lambda a: a.reshape(shp if len(shp) == 3 else (1,) + shp)
            back = lambda a: a.reshape(shp)
            g = [landed[l][n] for l in range(DEPTH)] if n in SHARDED else v3(G[n])
        (g, d, nm, nv), got = adamw(v3(W[n]), g, v3(M[n]), v3(V[n]), "adamw_" + n, exch)
        G[n], delta[n], new_m[n], new_v[n] = back(g), back(d), back(nm), back(nv)
        return got

    parts, shapes = [], []
    for l in range(DEPTH):
        for n in SMALL_PER_LAYER:
            parts.append(grads[l][n].astype(F32).reshape(-1))
            shapes.append(grads[l][n].shape)
    parts += [drel.reshape(-1), jnp.sum(sq).reshape(1) * (0.5 / D_MODEL)]
    shapes += [drel.shape, (1,)]
    on_landed(0, ATTN_SENT, update("w_up", (last, False)))
    for l in range(DEPTH):
        landed[l]["w_in"] = _unperm_w_in(landed[l]["w_in_p"])
    gathered = update("w_down", ([_pack(parts, 128, 8, F32)], True))[0]
    red = _unpack(sum_devices(gathered, "sum_small"), shapes)
    k = 0
    per = {n: [] for n in SMALL_PER_LAYER}
    for l in range(DEPTH):
        for n in SMALL_PER_LAYER:
            per[n].append(red[k])
            k += 1
    for n in SMALL_PER_LAYER:
        G[n] = jnp.stack(per[n]).reshape((DEPTH, 3, 2 * D_FF) if n == "conv_w" else W[n].shape)
    G["rel_bias"] = red[k]
    loss = red[k + 1][0]
    G["conv_w"] = lax.dynamic_slice_in_dim(G["conv_w"], me * 704, 704, axis=2)

    for n in WEIGHTS:
        if n not in ("w_up", "w_down"):
            update(n)
    return (loss, dx[None], *[G[n] for n in WEIGHTS], *[delta[n] for n in WEIGHTS],
            *[new_m[n] for n in WEIGHTS], *[new_v[n] for n in WEIGHTS])
```

```python
import functools
import math

import numpy as np
import jax
import jax.numpy as jnp
from jax import lax
from jax.experimental import pallas as pl
from jax.experimental.pallas import tpu as pltpu

F32 = jnp.float32
MXU = jnp.bfloat16

N_DEV = 8
DEPTH = 4
D_MODEL = 1024
HEAD_DIM = 64
WINDOW = 128
SWA_Q_HEADS = 8
REL_BUCKETS = 32
REL_MAX_DIST = 128
MLA_QK_DIM = 96
ROPE_DIM = 32
ROPE_THETA = 10000.0
D_FF = 2816
EPS = 1e-6
NEG = -1e30
IN_COLS = 1956
C_QA, C_KA, C_VA = 0, 512, 640
C_QF, C_KF, C_VF = 768, 1024, 1280
C_CQ, C_CKV, C_MISC = 1536, 1792, 1920
ROPE_LANE0 = 64
ADAM_LR, ADAM_B1, ADAM_B2, ADAM_EPS, ADAM_WD, ADAM_STEP = 0.001, 0.9, 0.999, 1e-08, 0.01, 10

VMEM_LIMIT = 56 * 1024 * 1024


def _cparams(sem=None):
    return pltpu.CompilerParams(dimension_semantics=sem, vmem_limit_bytes=VMEM_LIMIT)


def _tile(n, pref):
    if n <= pref:
        return n
    t = pref - pref % 128
    while t >= 128:
        if n % t == 0:
            return t
        t -= 128
    return n


def _dot(a, b):
    return jnp.dot(a.astype(MXU), b.astype(MXU), preferred_element_type=F32)


def _dot_nt(a, b):
    return lax.dot_general(a.astype(MXU), b.astype(MXU), (((1,), (1,)), ((), ())),
                           preferred_element_type=F32)


def _rms_fwd(x, g):
    return x * lax.rsqrt(jnp.mean(x * x, axis=-1, keepdims=True) + EPS) * g


def _rms_bwd(dy, x, g, n=None):
    r = lax.rsqrt(jnp.mean(x * x, axis=-1, keepdims=True) + EPS)
    xh = x * r
    dg = jnp.sum(dy * xh, axis=0, keepdims=True)
    dxh = dy * g
    dx = r * (dxh - xh * jnp.mean(dxh * xh, axis=-1, keepdims=True))
    return dx, dg


def _acc_out(ref, val, first):
    @pl.when(first)
    def _():
        ref[...] = val

    @pl.when(jnp.logical_not(first))
    def _():
        ref[...] += val


def norm_matmul(x, g, w, name, lo_tiles=0, tn_pref=512, w_transposed=False, h_transposed=True, exch=None):
    T, K = x.shape
    N = w.shape[0] if w_transposed else w.shape[1]
    tm, tn = _tile(T, 1024), _tile(N, tn_pref)

    def body(x_ref, g_ref, w_ref, o_ref, hT_ref, *rest):
        h_sc = rest[-1]
        j = pl.program_id(1)

        @pl.when(j == 0)
        def _():
            h = _rms_fwd(x_ref[...], g_ref[...])
            h_sc[...] = h.astype(MXU)
            hT_ref[...] = (h.T if h_transposed else h).astype(MXU)

        r = (_dot_nt if w_transposed else _dot)(h_sc[...], w_ref[...])
        o_ref[...] = r
        if lo_tiles:
            @pl.when(j < lo_tiles)
            def _():
                rest[0][...] = r.astype(MXU)

    h_spec = pl.BlockSpec((K, tm), lambda i, j: (0, i)) if h_transposed else pl.BlockSpec((tm, K), lambda i, j: (i, 0))
    out_specs = [pl.BlockSpec((tm, tn), lambda i, j: (i, j)), h_spec]
    out_shape = [jax.ShapeDtypeStruct((T, N), F32), jax.ShapeDtypeStruct((K, T) if h_transposed else (T, K), MXU)]
    if lo_tiles:
        out_specs.append(pl.BlockSpec((tm, tn), lambda i, j: (i, jnp.minimum(j, lo_tiles - 1))))
        out_shape.append(jax.ShapeDtypeStruct((T, lo_tiles * tn), MXU))
    in_specs = [pl.BlockSpec((tm, K), lambda i, j: (i, 0)),
                pl.BlockSpec((1, K), lambda i, j: (0, 0)),
                pl.BlockSpec((tn, K), lambda i, j: (j, 0)) if w_transposed else
                pl.BlockSpec((K, tn), lambda i, j: (0, j))]
    outs, landed = _call_hosting(body, name, (T // tm, N // tn), [x, g, w], in_specs, out_specs, out_shape,
                                 [pltpu.VMEM((tm, K), MXU)], exch)
    return outs if exch is None else (outs, landed)


def matmul_nn(a, b, name, out_dtype=F32):
    M, K = a.shape
    N = b.shape[1]
    tm, tn, tk = _tile(M, 1408), _tile(N, 1536), _tile(K, 1024)
    nk = K // tk

    def body(a_ref, b_ref, o_ref, acc):
        k = pl.program_id(2)
        part = _dot(a_ref[...], b_ref[...])
        _acc_out(acc, part, k == 0)

        @pl.when(k == nk - 1)
        def _():
            o_ref[...] = acc[...].astype(out_dtype)

    return pl.pallas_call(
        body, name=name, grid=(M // tm, N // tn, nk),
        in_specs=[pl.BlockSpec((tm, tk), lambda i, j, k: (i, k)),
                  pl.BlockSpec((tk, tn), lambda i, j, k: (k, j))],
        out_specs=pl.BlockSpec((tm, tn), lambda i, j, k: (i, j)),
        out_shape=jax.ShapeDtypeStruct((M, N), out_dtype),
        scratch_shapes=[pltpu.VMEM((tm, tn), F32)],
        compiler_params=_cparams(("parallel", "parallel", "arbitrary")),
    )(a, b)


def matmul_nt_normbwd(dy, w, x, g, dres, name):
    T, N = dy.shape
    K = w.shape[0]
    tm, tn = _tile(T, 1024), _tile(N, 1536)
    nj = N // tn

    def body(dy_ref, w_ref, x_ref, g_ref, dres_ref, dx_ref, dg_ref, acc):
        i, j = pl.program_id(0), pl.program_id(1)
        _acc_out(acc, _dot_nt(dy_ref[...], w_ref[...]), j == 0)

        @pl.when(j == nj - 1)
        def _():
            dx, dg = _rms_bwd(acc[...], x_ref[...], g_ref[...])
            dx_ref[...] = dres_ref[...] + dx
            _acc_out(dg_ref, dg, i == 0)

    return pl.pallas_call(
        body, name=name, grid=(T // tm, nj),
        in_specs=[pl.BlockSpec((tm, tn), lambda i, j: (i, j)),
                  pl.BlockSpec((K, tn), lambda i, j: (0, j)),
                  pl.BlockSpec((tm, K), lambda i, j: (i, 0)),
                  pl.BlockSpec((1, K), lambda i, j: (0, 0)),
                  pl.BlockSpec((tm, K), lambda i, j: (i, 0))],
        out_specs=[pl.BlockSpec((tm, K), lambda i, j: (i, 0)),
                   pl.BlockSpec((1, K), lambda i, j: (0, 0))],
        out_shape=[jax.ShapeDtypeStruct((T, K), F32), jax.ShapeDtypeStruct((1, K), F32)],
        scratch_shapes=[pltpu.VMEM((tm, K), F32)],
        compiler_params=_cparams(("arbitrary", "arbitrary")),
    )(dy, w, x, g, dres)


def loss_kernel(y, tgt):
    T, D = y.shape
    tm = _tile(T, 512)

    def body(y_ref, t_ref, dy_ref, acc_ref):
        e = y_ref[...] - t_ref[...]
        dy_ref[...] = e * (1.0 / D)
        _acc_out(acc_ref, jnp.sum(e * e, axis=0, keepdims=True), pl.program_id(0) == 0)

    return pl.pallas_call(
        body, name="loss", grid=(T // tm,),
        in_specs=[pl.BlockSpec((tm, D), lambda i: (i, 0)), pl.BlockSpec((tm, D), lambda i: (i, 0))],
        out_specs=[pl.BlockSpec((tm, D), lambda i: (i, 0)), pl.BlockSpec((1, D), lambda i: (0, 0))],
        out_shape=[jax.ShapeDtypeStruct((T, D), F32), jax.ShapeDtypeStruct((1, D), F32)],
        compiler_params=_cparams(("arbitrary",)),
    )(y, tgt)


def _rope_partner(x):
    lane = lax.broadcasted_iota(jnp.int32, (1, 128), 1)
    return jnp.where(lane < ROPE_LANE0 + ROPE_DIM // 2, pltpu.roll(x, 128 - ROPE_DIM // 2, 1),
                     pltpu.roll(x, ROPE_DIM // 2, 1))


def _rope_apply(x, cos, sin_signed):
    return x * cos + _rope_partner(x) * sin_signed


def _rope_apply_bwd(dy, cos, sin_signed):
    lane = lax.broadcasted_iota(jnp.int32, (1, 128), 1)
    rotary = (lane >= ROPE_LANE0) & (lane < ROPE_LANE0 + ROPE_DIM)
    return dy * cos + jnp.where(rotary, _rope_partner(dy * sin_signed), 0.0)


def rope_tables(T):
    pos = jnp.arange(T, dtype=F32)
    inv_freq = ROPE_THETA ** (-(jnp.arange(ROPE_DIM // 2, dtype=F32) * 2.0 / ROPE_DIM))
    ang = pos[:, None] * inv_freq[None, :]
    cos, sin = jnp.cos(ang), jnp.sin(ang)
    z = jnp.zeros((T, ROPE_LANE0), F32)
    z2 = jnp.zeros((T, 128 - ROPE_LANE0 - ROPE_DIM), F32)
    cosr = jnp.concatenate([z, cos, cos, z2], axis=1)
    sinr = jnp.concatenate([z, -sin, sin, z2], axis=1)
    return cosr, sinr


def mla_prep(proj, gq, gkv, wuq, wukv, cosr, sinr):
    T = proj.shape[0]
    tm = _tile(T, 512)

    def body(cq_ref, ckv_ref, misc_ref, gq_ref, gkv_ref, wuq_ref, wukv_ref, cos_ref, sin_ref,
             q_ref, k_ref, v_ref, cqT_ref, ckvT_ref):
        lane = lax.broadcasted_iota(jnp.int32, (1, 128), 1)
        cosr_, sinr_ = cos_ref[...], sin_ref[...]
        cosq = cosr_ + jnp.where(lane < ROPE_LANE0, 1.0, 0.0)
        cqn = _rms_fwd(cq_ref[...], gq_ref[...])
        cqT_ref[...] = cqn.T.astype(MXU)
        qm = _dot(cqn, wuq_ref[...])
        q_ref[...] = jnp.concatenate(
            [_rope_apply(qm[:, 128 * h:128 * (h + 1)], cosq, sinr_) for h in range(4)], axis=1).astype(MXU)
        ckvn = _rms_fwd(ckv_ref[...], gkv_ref[...])
        ckvT_ref[...] = ckvn.T.astype(MXU)
        kv = _dot(ckvn, wukv_ref[...])
        kr = _rope_apply(misc_ref[...], cosr_, sinr_)
        k_ref[...] = jnp.concatenate(
            [kv[:, 128 * h:128 * (h + 1)] + kr for h in range(4)], axis=1).astype(MXU)
        v_ref[...] = kv[:, 512:768].astype(MXU)

    row = lambda i: (i, 0)
    const = lambda i: (0, 0)
    return pl.pallas_call(
        body, name="mla_prep", grid=(T // tm,),
        in_specs=[pl.BlockSpec((tm, 256), lambda i: (i, C_CQ // 256)),
                  pl.BlockSpec((tm, 128), lambda i: (i, C_CKV // 128)),
                  pl.BlockSpec((tm, 128), lambda i: (i, C_MISC // 128)),
                  pl.BlockSpec((1, 256), const), pl.BlockSpec((1, 128), const),
                  pl.BlockSpec((256, 512), const), pl.BlockSpec((128, 768), const),
                  pl.BlockSpec((tm, 128), row), pl.BlockSpec((tm, 128), row)],
        out_specs=[pl.BlockSpec((tm, 512), row), pl.BlockSpec((tm, 512), row), pl.BlockSpec((tm, 256), row),
                   pl.BlockSpec((256, tm), lambda i: (0, i)), pl.BlockSpec((128, tm), lambda i: (0, i))],
        out_shape=[jax.ShapeDtypeStruct((T, 512), MXU), jax.ShapeDtypeStruct((T, 512), MXU),
                   jax.ShapeDtypeStruct((T, 256), MXU),
                   jax.ShapeDtypeStruct((256, T), MXU), jax.ShapeDtypeStruct((128, T), MXU)],
        compiler_params=_cparams(("parallel",)),
    )(proj, proj, proj, gq, gkv, wuq, wukv, cosr, sinr)


def mla_prep_bwd(dq, dk, dv, proj, gq, gkv, wuq, wukv, cosr, sinr):
    T = proj.shape[0]
    tm = _tile(T, 512)

    def body(dq_ref, dk_ref, dv_ref, cq_ref, ckv_ref, gq_ref, gkv_ref, wuq_ref, wukv_ref, cos_ref, sin_ref,
             dqm_ref, dkv_ref, dcq_ref, dckv_ref, dmisc_ref, dgq_ref, dgkv_ref):
        first = pl.program_id(0) == 0
        lane = lax.broadcasted_iota(jnp.int32, (1, 128), 1)
        cosr_, sinr_ = cos_ref[...], sin_ref[...]
        cosq = cosr_ + jnp.where(lane < ROPE_LANE0, 1.0, 0.0)
        dqv = dq_ref[...]
        dqm = jnp.concatenate(
            [_rope_apply_bwd(dqv[:, 128 * h:128 * (h + 1)], cosq, sinr_) for h in range(4)], axis=1)
        dqm_ref[...] = dqm.astype(MXU)
        dcq, dgq = _rms_bwd(_dot_nt(dqm, wuq_ref[...]), cq_ref[...], gq_ref[...])
        dcq_ref[...] = dcq
        _acc_out(dgq_ref, dgq, first)
        dkv_ = dk_ref[...]
        heads = [dkv_[:, 128 * h:128 * (h + 1)] for h in range(4)]
        dkr = heads[0] + heads[1] + heads[2] + heads[3]
        dmisc_ref[...] = _rope_apply_bwd(dkr, cosr_, sinr_)
        dkvm = jnp.concatenate([jnp.where(lane < ROPE_LANE0, hd, 0.0) for hd in heads] + [dv_ref[...]], axis=1)
        dkv_ref[...] = dkvm.astype(MXU)
        dckv, dgkv = _rms_bwd(_dot_nt(dkvm, wukv_ref[...]), ckv_ref[...], gkv_ref[...])
        dckv_ref[...] = dckv
        _acc_out(dgkv_ref, dgkv, first)

    row = lambda i: (i, 0)
    const = lambda i: (0, 0)
    return pl.pallas_call(
        body, name="mla_prep_bwd", grid=(T // tm,),
        in_specs=[pl.BlockSpec((tm, 512), row), pl.BlockSpec((tm, 512), row), pl.BlockSpec((tm, 256), row),
                  pl.BlockSpec((tm, 256), lambda i: (i, C_CQ // 256)),
                  pl.BlockSpec((tm, 128), lambda i: (i, C_CKV // 128)),
                  pl.BlockSpec((1, 256), const), pl.BlockSpec((1, 128), const),
                  pl.BlockSpec((256, 512), const), pl.BlockSpec((128, 768), const),
                  pl.BlockSpec((tm, 128), row), pl.BlockSpec((tm, 128), row)],
        out_specs=[pl.BlockSpec((tm, 512), row), pl.BlockSpec((tm, 768), row), pl.BlockSpec((tm, 256), row),
                   pl.BlockSpec((tm, 128), row), pl.BlockSpec((tm, 128), row),
                   pl.BlockSpec((1, 256), const), pl.BlockSpec((1, 128), const)],
        out_shape=[jax.ShapeDtypeStruct((T, 512), MXU), jax.ShapeDtypeStruct((T, 768), MXU),
                   jax.ShapeDtypeStruct((T, 256), F32), jax.ShapeDtypeStruct((T, 128), F32),
                   jax.ShapeDtypeStruct((T, 128), F32),
                   jax.ShapeDtypeStruct((1, 256), F32), jax.ShapeDtypeStruct((1, 128), F32)],
        compiler_params=_cparams(("arbitrary",)),
    )(dq, dk, dv, proj, proj, gq, gkv, wuq, wukv, cosr, sinr)


def _split3(x):
    hi = x.astype(MXU)
    r1 = x - hi.astype(F32)
    mid = r1.astype(MXU)
    lo = (r1 - mid.astype(F32)).astype(MXU)
    return hi, mid, lo


def _tri_matmul(tri, x):
    hi, mid, lo = _split3(x)
    d = lambda p: jnp.dot(tri, p, preferred_element_type=F32)
    return d(hi) + d(mid) + d(lo)


def _log_sigmoid(z):
    return jnp.minimum(z, 0.0) - jnp.log(1.0 + jnp.exp(-jnp.abs(z)))


def fox_gate(proj, fbias):
    T = proj.shape[0]
    tb = _tile(T, 512)

    def body(misc_ref, b_ref, fc_ref, fr_ref, frep_ref, carry):
        @pl.when(pl.program_id(0) == 0)
        def _():
            carry[...] = jnp.zeros_like(carry)

        lane = lax.broadcasted_iota(jnp.int32, (1, 128), 1)
        lf = jnp.where(lane < 4, _log_sigmoid(misc_ref[...] + b_ref[...]), 0.0)
        r = lax.broadcasted_iota(jnp.int32, (tb, tb), 0)
        c = lax.broadcasted_iota(jnp.int32, (tb, tb), 1)
        tri = jnp.where(r >= c, 1.0, 0.0).astype(MXU)
        F = _tri_matmul(tri, lf) + carry[...]
        carry[...] = carry[...] + jnp.sum(lf, axis=0, keepdims=True)
        fc_ref[0] = F
        fc_ref[1] = pltpu.roll(F, 126, 1)
        ft = F.T[0:8, :]
        fr_ref[0] = ft
        fr_ref[1] = pltpu.roll(ft, 6, 0)
        for h in range(4):
            frep_ref[h] = jnp.broadcast_to(_lane_pick(F, h), (tb, 128))

    return pl.pallas_call(
        body, name="fox_gate", grid=(T // tb,),
        in_specs=[pl.BlockSpec((tb, 128), lambda i: (i, C_MISC // 128)), pl.BlockSpec((1, 128), lambda i: (0, 0))],
        out_specs=[pl.BlockSpec((2, tb, 128), lambda i: (0, i, 0)), pl.BlockSpec((2, 8, tb), lambda i: (0, 0, i)),
                   pl.BlockSpec((4, tb, 128), lambda i: (0, i, 0))],
        out_shape=[jax.ShapeDtypeStruct((2, T, 128), F32), jax.ShapeDtypeStruct((2, 8, T), F32),
                   jax.ShapeDtypeStruct((4, T, 128), F32)],
        scratch_shapes=[pltpu.VMEM((1, 128), F32)],
        compiler_params=_cparams(("arbitrary",)),
    )(proj, fbias)


def fox_gate_bwd(dFq, dFk, proj, fbias):
    T = proj.shape[0]
    tb = _tile(T, 512)
    nb = T // tb

    def body(dq_ref, dk_ref, misc_ref, b_ref, dm_ref, db_ref, carry):
        first = pl.program_id(0) == 0

        @pl.when(first)
        def _():
            carry[...] = jnp.zeros_like(carry)

        lane = lax.broadcasted_iota(jnp.int32, (1, 128), 1)
        dF = jnp.where(lane < 4, (dq_ref[0] + dk_ref[0]) + pltpu.roll(dq_ref[1] + dk_ref[1], 2, 1), 0.0)
        r = lax.broadcasted_iota(jnp.int32, (tb, tb), 0)
        c = lax.broadcasted_iota(jnp.int32, (tb, tb), 1)
        tri = jnp.where(r <= c, 1.0, 0.0).astype(MXU)
        dlf = _tri_matmul(tri, dF) + carry[...]
        carry[...] = carry[...] + jnp.sum(dF, axis=0, keepdims=True)
        z = misc_ref[...] + b_ref[...]
        dz = jnp.where(lane < 4, dlf * (1.0 / (1.0 + jnp.exp(z))), 0.0)
        dm_ref[...] = dz
        _acc_out(db_ref, jnp.sum(dz, axis=0, keepdims=True), first)

    return pl.pallas_call(
        body, name="fox_gate_bwd", grid=(nb,),
        in_specs=[pl.BlockSpec((2, tb, 128), lambda i: (0, nb - 1 - i, 0)),
                  pl.BlockSpec((2, tb, 128), lambda i: (0, nb - 1 - i, 0)),
                  pl.BlockSpec((tb, 128), lambda i: (nb - 1 - i, C_MISC // 128)),
                  pl.BlockSpec((1, 128), lambda i: (0, 0))],
        out_specs=[pl.BlockSpec((tb, 128), lambda i: (nb - 1 - i, 0)), pl.BlockSpec((1, 128), lambda i: (0, 0))],
        out_shape=[jax.ShapeDtypeStruct((T, 128), F32), jax.ShapeDtypeStruct((1, 128), F32)],
        scratch_shapes=[pltpu.VMEM((1, 128), F32)],
        compiler_params=_cparams(("arbitrary",)),
    )(dFq, dFk, proj, fbias)


FLASH_TILE = 512


def _row_stat_tile(a, b, n):
    at = jnp.broadcast_to(a, (n, 128)).T[0:8, :]
    bt = jnp.broadcast_to(b, (n, 128)).T[0:8, :]
    sub = lax.broadcasted_iota(jnp.int32, (8, 1), 0)
    return jnp.where(sub == 0, at, jnp.where(sub == 1, bt, 0.0))


def _col_stat_tile(a, b):
    lane = lax.broadcasted_iota(jnp.int32, (1, 128), 1)
    return jnp.where(lane == 0, a, jnp.where(lane == 1, b, 0.0))


def _lane_pick(x, h):
    lane = lax.broadcasted_iota(jnp.int32, (1, 128), 1)
    return jnp.sum(jnp.where(lane == h, x, 0.0), axis=1, keepdims=True)


def _half_mask(h):
    lane = lax.broadcasted_iota(jnp.int32, (1, 128), 1)
    return (lane // HEAD_DIM) == h


def _call_hosting(body, name, grid, args, in_specs, out_specs, out_shape, scratch, exch):
    n_out = len(out_shape)
    if exch is not None:
        body, (xargs, xin, xout, xshape, xscratch) = hosted_exchange(
            body, len(args), n_out, len(scratch), grid, *exch)
        args, in_specs, out_specs = args + xargs, in_specs + xin, out_specs + xout
        out_shape, scratch = out_shape + xshape, scratch + xscratch
    res = pl.pallas_call(
        body, name=name, grid=grid, in_specs=in_specs, out_specs=out_specs, out_shape=out_shape,
        scratch_shapes=scratch, compiler_params=_cparams(("arbitrary",) * len(grid)),
    )(*args)
    return res[:n_out], res[n_out:]


def flash_fwd(q, k, v, frep, frow, *, qblk, kblk, vblk, nq, scale, name, exch=None):
    T = q.shape[0]
    tk = _tile(T, FLASH_TILE)
    tq = _tile(T, 2 * FLASH_TILE)
    per_q = tq // tk
    wq = 128 * nq
    has_f = frep is not None

    def body(*refs):
        if has_f:
            q_ref, k_ref, v_ref, fk_ref, fr_ref, o_ref, lr_ref, vT_sc, m_sc, acc_sc = refs
        else:
            q_ref, k_ref, v_ref, o_ref, lr_ref, vT_sc, m_sc, acc_sc = refs
        i = pl.program_id(1)

        @pl.when(i == 0)
        def _():
            vT_sc[...] = v_ref[...].astype(F32).T.astype(MXU)

        key_row = lax.broadcasted_iota(jnp.int32, (tk, 1), 0)
        q_col = lax.broadcasted_iota(jnp.int32, (1, tq), 1)
        row_half = lax.broadcasted_iota(jnp.int32, (128, 1), 0) // HEAD_DIM
        qb = q_ref[...].astype(F32) * scale
        if nq == 1:
            qhs = [jnp.where(_half_mask(h), qb, 0).astype(MXU) for h in range(2)]
        else:
            qhs = [qb[:, 128 * h:128 * (h + 1)].astype(MXU) for h in range(2)]
        for h in range(2):
            m_sc[h] = jnp.full((1, tq), NEG, F32)
            acc_sc[h] = jnp.zeros((128, tq), F32)

        def make_step(diag_block):
            def step(j, carry):
                off = pl.multiple_of(j * tk, tk)
                ks = k_ref[pl.ds(off, tk), :]
                vT = vT_sc[:, pl.ds(off, tk)]
                for h in range(2):
                    kh = ks if nq == 1 else ks[:, 128 * h:128 * (h + 1)]
                    sT = _dot_nt(kh, qhs[h])
                    if has_f:
                        fk = fk_ref[h, pl.ds(off, tk), :]
                        sT = sT + (fr_ref[0, h:h + 1, :] - jnp.concatenate([fk] * (tq // 128), axis=1))
                    if diag_block is not None:
                        sT = jnp.where(key_row + diag_block * tk <= q_col, sT, NEG)
                    m_prev = m_sc[h]
                    m_new = jnp.maximum(m_prev, jnp.max(sT, axis=0, keepdims=True))
                    alpha = jnp.exp(m_prev - m_new)
                    pT = jnp.exp(sT - m_new)
                    vTh = jnp.where(row_half == h, vT, jnp.ones_like(vT))
                    acc_sc[h] = alpha * acc_sc[h] + _dot(vTh, pT)
                    m_sc[h] = m_new
                return carry
            return step

        lax.fori_loop(0, per_q * i, make_step(None), 0)
        for d in range(per_q):
            make_step(d)(per_q * i + d, 0)
        outs, lses = [], []
        for h in range(2):
            acc = acc_sc[h]
            outs.append(acc / pltpu.roll(acc, HEAD_DIM, 0))
            l = acc_sc[h, HEAD_DIM * (1 - h):HEAD_DIM * (1 - h) + 1, :]
            lses.append(m_sc[h] + jnp.log(l))
        o_ref[...] = jnp.where(row_half == 0, outs[0], outs[1]).T
        sub = lax.broadcasted_iota(jnp.int32, (8, 1), 0)
        lr_ref[0] = jnp.where(sub == 0, lses[0], jnp.where(sub == 1, lses[1], 0.0))

    in_specs = [pl.BlockSpec((tq, wq), lambda p, i: (i, qblk + p)),
                pl.BlockSpec((T, wq), lambda p, i: (0, kblk + p)),
                pl.BlockSpec((T, 128), lambda p, i: (0, vblk + p))]
    args = [q, k, v]
    if has_f:
        in_specs += [pl.BlockSpec((2, T, 128), lambda p, i: (p, 0, 0)),
                     pl.BlockSpec((1, 8, tq), lambda p, i: (p, 0, i))]
        args += [frep, frow]
    out_specs = [pl.BlockSpec((tq, 128), lambda p, i: (i, p)), pl.BlockSpec((1, 8, tq), lambda p, i: (p, 0, i))]
    out_shape = [jax.ShapeDtypeStruct((T, 256), F32), jax.ShapeDtypeStruct((2, 8, T), F32)]
    scratch = [pltpu.VMEM((128, T), MXU), pltpu.VMEM((2, 1, tq), F32), pltpu.VMEM((2, 128, tq), F32)]
    return _call_hosting(body, name, (2, T // tq), args, in_specs, out_specs, out_shape, scratch, exch)


def flash_bwd(q, k, v, do, o, lrow, fcol, frow, *, qblk, kblk, vblk, nq, scale, name, exch=None):
    T = q.shape[0]
    tq = tk = _tile(T, FLASH_TILE)
    wq = 128 * nq
    nqb = T // tq
    has_f = fcol is not None

    def body(*refs):
        if has_f:
            (q_ref, k_ref, v_ref, do_ref, o_ref, lr_ref, fc_ref, fr_ref,
             dq_ref, dk_ref, dv_ref, df_ref, dfq_ref, dk_sc, dv_sc, dqT_sc, d_sc, df_sc, dfq_sc) = refs
        else:
            q_ref, k_ref, v_ref, do_ref, o_ref, lr_ref, dq_ref, dk_ref, dv_ref, dk_sc, dv_sc, dqT_sc, d_sc = refs
        j = pl.program_id(1)
        diag = lax.broadcasted_iota(jnp.int32, (tk, 1), 0) <= lax.broadcasted_iota(jnp.int32, (1, tq), 1)
        hms = [_half_mask(h) for h in range(2)]

        @pl.when(j == 0)
        def _():
            dqT_sc[...] = jnp.zeros_like(dqT_sc)
            if has_f:
                dfq_sc[...] = jnp.zeros_like(dfq_sc)

            def delta(b, carry):
                off = pl.multiple_of(b * tq, tq)
                prod = do_ref[pl.ds(off, tq), :] * o_ref[pl.ds(off, tq), :]
                Ds = [jnp.sum(jnp.where(hms[h], prod, 0.0), axis=1, keepdims=True) for h in range(2)]
                d_sc[:, pl.ds(off, tq)] = _row_stat_tile(Ds[0], Ds[1], tq)
                return carry

            lax.fori_loop(0, nqb, delta, 0)

        kb = k_ref[...]
        vb = v_ref[...]
        if nq == 1:
            khs = [jnp.where(hms[h], kb, 0).astype(MXU) for h in range(2)]
        else:
            khs = [kb[:, 128 * h:128 * (h + 1)].astype(MXU) for h in range(2)]
        kTs = [kh.astype(F32).T.astype(MXU) for kh in khs]
        kss = [(kh.astype(F32) * scale).astype(MXU) for kh in khs]
        vhs = [jnp.where(hms[h], vb, 0).astype(MXU) for h in range(2)]
        fks = [_lane_pick(fc_ref[0], h) for h in range(2)] if has_f else None
        dv_sc[...] = jnp.zeros_like(dv_sc)
        dk_sc[...] = jnp.zeros_like(dk_sc)
        if has_f:
            df_sc[...] = jnp.zeros_like(df_sc)

        def make_step(masked):
            def step(i, carry):
                off = pl.multiple_of(i * tq, tq)
                qs = q_ref[pl.ds(off, tq), :]
                dos = do_ref[pl.ds(off, tq), :]
                for h in range(2):
                    qh = qs if nq == 1 else qs[:, 128 * h:128 * (h + 1)]
                    sT = _dot_nt(kss[h], qh)
                    if has_f:
                        sT = sT + (fr_ref[0, h:h + 1, pl.ds(off, tq)] - fks[h])
                    pT = jnp.exp(sT - lr_ref[0, h:h + 1, pl.ds(off, tq)])
                    if masked:
                        pT = jnp.where(diag, pT, 0.0)
                    dsT = pT * (_dot_nt(vhs[h], dos) - d_sc[h:h + 1, pl.ds(off, tq)])
                    dv_sc[...] += _dot(pT, jnp.where(hms[h], dos, 0))
                    qq = jnp.where(hms[h], qs, 0) if nq == 1 else qh
                    dk_sc[h if nq == 2 else 0] += _dot(dsT, qq)
                    dqT_sc[h if nq == 2 else 0, :, pl.ds(off, tq)] += _dot(kTs[h], dsT)
                    if has_f:
                        part = dsT[:, 0:128]
                        for c in range(1, tq // 128):
                            part = part + dsT[:, 128 * c:128 * (c + 1)]
                        df_sc[h] += part
                        dfq_sc[h:h + 1, pl.ds(off, tq)] += jnp.sum(dsT, axis=0, keepdims=True)
                return carry
            return step

        make_step(True)(j, 0)
        lax.fori_loop(j + 1, nqb, make_step(False), 0)
        if nq == 1:
            dk_ref[...] = dk_sc[0] * scale
        else:
            dk_ref[...] = jnp.concatenate([dk_sc[0], dk_sc[1]], axis=1) * scale
        dv_ref[...] = dv_sc[...]
        if has_f:
            df_ref[0] = _col_stat_tile(-jnp.sum(df_sc[0], axis=1, keepdims=True),
                                       -jnp.sum(df_sc[1], axis=1, keepdims=True))

        @pl.when(j == nqb - 1)
        def _():
            if nq == 1:
                dq_ref[...] = dqT_sc[0].T * scale
            else:
                dq_ref[...] = jnp.concatenate([dqT_sc[0].T, dqT_sc[1].T], axis=1) * scale
            if has_f:
                sub = lax.broadcasted_iota(jnp.int32, (128, 1), 0)
                rows = jnp.where(sub == 0, dfq_sc[0:1, :], jnp.where(sub == 1, dfq_sc[1:2, :], 0.0))
                dfq_ref[0] = rows.T

    in_specs = [pl.BlockSpec((T, wq), lambda p, j: (0, qblk + p)),
                pl.BlockSpec((tk, wq), lambda p, j: (j, kblk + p)),
                pl.BlockSpec((tk, 128), lambda p, j: (j, vblk + p)),
                pl.BlockSpec((T, 128), lambda p, j: (0, p)),
                pl.BlockSpec((T, 128), lambda p, j: (0, p)),
                pl.BlockSpec((1, 8, T), lambda p, j: (p, 0, 0))]
    args = [q, k, v, do, o, lrow]
    out_specs = [pl.BlockSpec((T, wq), lambda p, j: (0, p)),
                 pl.BlockSpec((tk, wq), lambda p, j: (j, p)), pl.BlockSpec((tk, 128), lambda p, j: (j, p))]
    out_shape = [jax.ShapeDtypeStruct((T, 2 * wq), F32), jax.ShapeDtypeStruct((T, 2 * wq), F32),
                 jax.ShapeDtypeStruct((T, 256), F32)]
    scratch = [pltpu.VMEM((nq, tk, 128), F32), pltpu.VMEM((tk, 128), F32), pltpu.VMEM((nq, 128, T), F32),
               pltpu.VMEM((8, T), F32)]
    if has_f:
        in_specs += [pl.BlockSpec((1, tk, 128), lambda p, j: (p, j, 0)),
                     pl.BlockSpec((1, 8, T), lambda p, j: (p, 0, 0))]
        args += [fcol, frow]
        out_specs += [pl.BlockSpec((1, tk, 128), lambda p, j: (p, j, 0)),
                      pl.BlockSpec((1, T, 128), lambda p, j: (p, 0, 0))]
        out_shape += [jax.ShapeDtypeStruct((2, T, 128), F32), jax.ShapeDtypeStruct((2, T, 128), F32)]
        scratch += [pltpu.VMEM((2, tk, 128), F32), pltpu.VMEM((8, T), F32)]
    return _call_hosting(body, name, (2, T // tk), args, in_specs, out_specs, out_shape, scratch, exch)


def _swa_align(pair, e, h):
    sel = jnp.where(_half_mask(e), pair, 0.0)
    if e == h:
        return sel
    return pltpu.roll(sel, HEAD_DIM, 1)


def _swa_mask(n):
    W = WINDOW
    qi = lax.broadcasted_iota(jnp.int32, (W, 2 * W), 0) + W
    kj = lax.broadcasted_iota(jnp.int32, (W, 2 * W), 1)
    dist = qi - kj
    return (dist >= 0) & (dist < W) & ((n > 0) | (kj >= W))


def swa_fwd(proj, bias, sinks, exch=None):
    T = proj.shape[0]
    W = WINDOW
    nb = T // W
    scale = HEAD_DIM ** -0.5

    def body(sink_ref, q_ref, kp_ref, kc_ref, vp_ref, vc_ref, b_ref, o_ref, l_ref):
        n = pl.program_id(0)
        mask = _swa_mask(n)
        kband = jnp.concatenate([kp_ref[...], kc_ref[...]], axis=0).astype(MXU)
        vband = jnp.concatenate([vp_ref[...], vc_ref[...]], axis=0).astype(MXU)
        lane = lax.broadcasted_iota(jnp.int32, (1, 128), 1)
        lse_tile = jnp.zeros((W, 128), F32)
        pairs = []
        for h in range(2):
            full = []
            for g in range(4):
                hq = 4 * h + g
                qa = _swa_align(q_ref[:, 128 * (hq // 2):128 * (hq // 2 + 1)], hq % 2, h)
                s = _dot_nt(qa, kband) * scale + b_ref[hq]
                s = jnp.where(mask, s, NEG)
                sink = sink_ref[hq]
                m = jnp.maximum(jnp.max(s, axis=1, keepdims=True), sink)
                e = jnp.exp(s - m)
                l = jnp.sum(e, axis=1, keepdims=True) + jnp.exp(sink - m)
                r = jnp.where(_half_mask(h), _dot(e, vband), 0.0) / l
                full.append(r + pltpu.roll(r, HEAD_DIM, 1))
                lse_tile = jnp.where(lane == hq, m + jnp.log(l), lse_tile)
            pairs.append(jnp.where(_half_mask(0), full[0], full[1]))
            pairs.append(jnp.where(_half_mask(0), full[2], full[3]))
        o_ref[...] = jnp.concatenate(pairs, axis=1)
        l_ref[...] = lse_tile

    prev = lambda n: (jnp.maximum(n - 1, 0), C_KA // 128)
    cur = lambda n: (n, C_KA // 128)
    prev_v = lambda n: (jnp.maximum(n - 1, 0), C_VA // 128)
    cur_v = lambda n: (n, C_VA // 128)
    return _call_hosting(
        body, "swa_fwd", (nb,), [sinks, proj, proj, proj, proj, proj, bias],
        [pl.BlockSpec(memory_space=pltpu.SMEM),
         pl.BlockSpec((W, 512), lambda n: (n, 0)),
         pl.BlockSpec((W, 128), prev), pl.BlockSpec((W, 128), cur),
         pl.BlockSpec((W, 128), prev_v), pl.BlockSpec((W, 128), cur_v),
         pl.BlockSpec((8, W, 2 * W), lambda n: (0, 0, 0))],
        [pl.BlockSpec((W, 512), lambda n: (n, 0)), pl.BlockSpec((W, 128), lambda n: (n, 0))],
        [jax.ShapeDtypeStruct((T, 512), F32), jax.ShapeDtypeStruct((T, 128), F32)], [], exch)


def swa_bwd(proj, bias, sinks, do, o, lse, exch=None):
    T = proj.shape[0]
    W = WINDOW
    nb = T // W
    scale = HEAD_DIM ** -0.5

    def body(sink_ref, q_ref, kp_ref, kc_ref, vp_ref, vc_ref, b_ref, do_ref, o_ref, l_ref,
             dq_ref, dk_ref, dv_ref, db_ref, dsk_ref, ck, cv):
        n = pl.program_id(0)

        @pl.when(n == 0)
        def _():
            ck[...] = jnp.zeros_like(ck)
            cv[...] = jnp.zeros_like(cv)
            db_ref[...] = jnp.zeros_like(db_ref)
            dsk_ref[...] = jnp.zeros_like(dsk_ref)

        @pl.when(n < nb)
        def _():
            mask = _swa_mask(n)
            kb32 = jnp.concatenate([kp_ref[...], kc_ref[...]], axis=0)
            vb32 = jnp.concatenate([vp_ref[...], vc_ref[...]], axis=0)
            kband = kb32.astype(MXU)
            sub = lax.broadcasted_iota(jnp.int32, (8, 1), 0)
            dk_band = jnp.zeros((2 * W, 128), F32)
            dv_band = jnp.zeros((2 * W, 128), F32)
            dsk = jnp.zeros((8, 128), F32)
            dq_pairs = []
            mask4 = jnp.concatenate([mask] * 4, axis=0)
            for h in range(2):
                hm = _half_mask(h)
                km = jnp.where(hm, kb32, 0.0).astype(MXU)
                vm = jnp.where(hm, vb32, 0.0).astype(MXU)
                pbs = [slice(128 * ((4 * h + g) // 2), 128 * ((4 * h + g) // 2 + 1)) for g in range(4)]
                q4 = jnp.concatenate([_swa_align(q_ref[:, pbs[g]], g % 2, h) for g in range(4)], axis=0)
                do4 = jnp.concatenate([_swa_align(do_ref[:, pbs[g]], g % 2, h) for g in range(4)], axis=0)
                D4 = jnp.concatenate(
                    [jnp.sum(jnp.where(_half_mask(g % 2), do_ref[:, pbs[g]] * o_ref[:, pbs[g]], 0.0), axis=1,
                             keepdims=True) for g in range(4)], axis=0)
                lse4 = jnp.concatenate([_lane_pick(l_ref[...], 4 * h + g) for g in range(4)], axis=0)
                sink4 = jnp.concatenate([jnp.full((W, 1), sink_ref[4 * h + g], F32) for g in range(4)], axis=0)
                s = _dot_nt(q4, kband) * scale + b_ref[4 * h:4 * h + 4].reshape(4 * W, 2 * W)
                p = jnp.where(mask4, jnp.exp(s - lse4), 0.0)
                sd = jnp.exp(sink4 - lse4) * D4
                for g in range(4):
                    dsk = dsk + jnp.where(sub == 4 * h + g,
                                          -jnp.sum(sd[W * g:W * (g + 1)], axis=0, keepdims=True), 0.0)
                ds = p * (_dot_nt(do4, vm) - D4)
                db_ref[4 * h:4 * h + 4] += ds.reshape(4, W, 2 * W)
                dq = _dot(ds, km) * scale
                dq = dq + pltpu.roll(dq, HEAD_DIM, 1)
                dk_band = dk_band + _dot(ds.T, q4) * scale
                dv_band = dv_band + _dot(p.T, do4)
                dq_pairs.append(jnp.where(_half_mask(0), dq[0:W], dq[W:2 * W]))
                dq_pairs.append(jnp.where(_half_mask(0), dq[2 * W:3 * W], dq[3 * W:4 * W]))
            dq_ref[...] = jnp.concatenate(dq_pairs, axis=1)
            dsk_ref[...] += dsk
            dk_ref[...] = ck[...] + dk_band[0:W]
            dv_ref[...] = cv[...] + dv_band[0:W]
            ck[...] = dk_band[W:2 * W]
            cv[...] = dv_band[W:2 * W]

        @pl.when(n == nb)
        def _():
            dk_ref[...] = ck[...]
            dv_ref[...] = cv[...]

    cl = lambda n: jnp.minimum(n, nb - 1)
    pv = lambda n: jnp.maximum(jnp.minimum(n, nb - 1) - 1, 0)
    return _call_hosting(
        body, "swa_bwd", (nb + 1,), [sinks, proj, proj, proj, proj, proj, bias, do, o, lse],
        [pl.BlockSpec(memory_space=pltpu.SMEM),
         pl.BlockSpec((W, 512), lambda n: (cl(n), 0)),
         pl.BlockSpec((W, 128), lambda n: (pv(n), C_KA // 128)),
         pl.BlockSpec((W, 128), lambda n: (cl(n), C_KA // 128)),
         pl.BlockSpec((W, 128), lambda n: (pv(n), C_VA // 128)),
         pl.BlockSpec((W, 128), lambda n: (cl(n), C_VA // 128)),
         pl.BlockSpec((8, W, 2 * W), lambda n: (0, 0, 0)),
         pl.BlockSpec((W, 512), lambda n: (cl(n), 0)),
         pl.BlockSpec((W, 512), lambda n: (cl(n), 0)),
         pl.BlockSpec((W, 128), lambda n: (cl(n), 0))],
        [pl.BlockSpec((W, 512), lambda n: (cl(n), 0)),
         pl.BlockSpec((W, 128), lambda n: (jnp.maximum(n - 1, 0), 0)),
         pl.BlockSpec((W, 128), lambda n: (jnp.maximum(n - 1, 0), 0)),
         pl.BlockSpec((8, W, 2 * W), lambda n: (0, 0, 0)),
         pl.BlockSpec((8, 128), lambda n: (0, 0))],
        [jax.ShapeDtypeStruct((T, 512), F32), jax.ShapeDtypeStruct((T, 128), F32),
         jax.ShapeDtypeStruct((T, 128), F32), jax.ShapeDtypeStruct((8, W, 2 * W), F32),
         jax.ShapeDtypeStruct((8, 128), F32)],
        [pltpu.VMEM((W, 128), F32), pltpu.VMEM((W, 128), F32)], exch)


def swa_bias_table(rel_bias):
    W = WINDOW
    qi = jnp.arange(W, dtype=jnp.int32)[:, None] + W
    kj = jnp.arange(2 * W, dtype=jnp.int32)[None, :]
    dist = qi - kj
    max_exact = REL_BUCKETS // 2
    d = jnp.maximum(dist, 0)
    log_ratio = jnp.log(jnp.maximum(d, 1).astype(F32) / max_exact) / math.log(REL_MAX_DIST / max_exact)
    large = jnp.minimum(max_exact + (log_ratio * (REL_BUCKETS - max_exact)).astype(jnp.int32), REL_BUCKETS - 1)
    bucket = jnp.where(d < max_exact, d, large)
    bucket = bucket.reshape(-1)
    onehot = (bucket[None, :] == jnp.arange(REL_BUCKETS, dtype=jnp.int32)[:, None]).astype(F32)
    bias = jnp.dot(rel_bias.astype(F32).T, onehot, precision=lax.Precision.HIGHEST)
    return bias.reshape(SWA_Q_HEADS, W, 2 * W), bucket


def attn_out(oa, ob, oc, gn, wout, gpost, x):
    T = x.shape[0]
    tm = _tile(T, 512)

    def body(oa_ref, ob_ref, oc_ref, gn_ref, w_ref, gp_ref, x_ref, x2_ref, y_ref, mT_ref):
        g = gn_ref[...]
        mixed = jnp.concatenate([_rms_fwd(oa_ref[...], g[:, 0:512]), _rms_fwd(ob_ref[...], g[:, 512:768]),
                                 _rms_fwd(oc_ref[...], g[:, 768:1024])], axis=1)
        mT_ref[...] = mixed.T.astype(MXU)
        y = _dot(mixed, w_ref[...])
        y_ref[...] = y
        x2_ref[...] = x_ref[...] + _rms_fwd(y, gp_ref[...])

    row = lambda i: (i, 0)
    const = lambda i: (0, 0)
    return pl.pallas_call(
        body, name="attn_out", grid=(T // tm,),
        in_specs=[pl.BlockSpec((tm, 512), row), pl.BlockSpec((tm, 256), row), pl.BlockSpec((tm, 256), row),
                  pl.BlockSpec((1, 1024), const), pl.BlockSpec((1024, 1024), const), pl.BlockSpec((1, 1024), const),
                  pl.BlockSpec((tm, 1024), row)],
        out_specs=[pl.BlockSpec((tm, 1024), row), pl.BlockSpec((tm, 1024), row),
                   pl.BlockSpec((1024, tm), lambda i: (0, i))],
        out_shape=[jax.ShapeDtypeStruct((T, 1024), F32), jax.ShapeDtypeStruct((T, 1024), F32),
                   jax.ShapeDtypeStruct((1024, T), MXU)],
        compiler_params=_cparams(("parallel",)),
    )(oa, ob, oc, gn, wout, gpost, x)


def attn_out_bwd(dx2, y, oa, ob, oc, gn, wout, gpost):
    T = dx2.shape[0]
    tm = _tile(T, 512)

    def body(dx_ref, y_ref, oa_ref, ob_ref, oc_ref, gn_ref, w_ref, gp_ref,
             dy_ref, da_ref, db_ref, dc_ref, dgn_ref, dgp_ref):
        first = pl.program_id(0) == 0
        dy, dgp = _rms_bwd(dx_ref[...], y_ref[...], gp_ref[...])
        dy_ref[...] = dy.astype(MXU)
        _acc_out(dgp_ref, dgp, first)
        dm = _dot_nt(dy, w_ref[...])
        g = gn_ref[...]
        da, dga = _rms_bwd(dm[:, 0:512], oa_ref[...], g[:, 0:512])
        db, dgb = _rms_bwd(dm[:, 512:768], ob_ref[...], g[:, 512:768])
        dc, dgc = _rms_bwd(dm[:, 768:1024], oc_ref[...], g[:, 768:1024])
        da_ref[...] = da
        db_ref[...] = db
        dc_ref[...] = dc
        _acc_out(dgn_ref, jnp.concatenate([dga, dgb, dgc], axis=1), first)

    row = lambda i: (i, 0)
    const = lambda i: (0, 0)
    return pl.pallas_call(
        body, name="attn_out_bwd", grid=(T // tm,),
        in_specs=[pl.BlockSpec((tm, 1024), row), pl.BlockSpec((tm, 1024), row),
                  pl.BlockSpec((tm, 512), row), pl.BlockSpec((tm, 256), row), pl.BlockSpec((tm, 256), row),
                  pl.BlockSpec((1, 1024), const), pl.BlockSpec((1024, 1024), const), pl.BlockSpec((1, 1024), const)],
        out_specs=[pl.BlockSpec((tm, 1024), row), pl.BlockSpec((tm, 512), row), pl.BlockSpec((tm, 256), row),
                   pl.BlockSpec((tm, 256), row), pl.BlockSpec((1, 1024), const), pl.BlockSpec((1, 1024), const)],
        out_shape=[jax.ShapeDtypeStruct((T, 1024), MXU), jax.ShapeDtypeStruct((T, 512), F32),
                   jax.ShapeDtypeStruct((T, 256), F32), jax.ShapeDtypeStruct((T, 256), F32),
                   jax.ShapeDtypeStruct((1, 1024), F32), jax.ShapeDtypeStruct((1, 1024), F32)],
        compiler_params=_cparams(("arbitrary",)),
    )(dx2, y, oa, ob, oc, gn, wout, gpost)


FF_TILE = 256
_GELU_C = math.sqrt(2.0 / math.pi)


def _gelu(x):
    return 0.5 * x * (1.0 + jnp.tanh(_GELU_C * (x + 0.044715 * x * x * x)))


def _gelu_with_grad(x):
    x2 = x * x
    t = jnp.tanh(_GELU_C * x * (1.0 + 0.044715 * x2))
    h = 0.5 * (1.0 + t)
    return x * h, h + (0.5 * _GELU_C) * x * (1.0 - t * t) * (1.0 + (3 * 0.044715) * x2)


def _conv_taps(u, hal_ref, first):
    row = lax.broadcasted_iota(jnp.int32, (8, 1), 0)
    h6 = jnp.where(first, 0.0, hal_ref[6:7, :])
    h7 = jnp.where(first, 0.0, hal_ref[7:8, :])
    r1, r2 = pltpu.roll(u, 1, 0), pltpu.roll(u, 2, 0)
    r1 = jnp.concatenate([jnp.where(row == 0, h7, r1[0:8]), r1[8:]], axis=0)
    r2 = jnp.concatenate([jnp.where(row == 0, h6, jnp.where(row == 1, h7, r2[0:8])), r2[8:]], axis=0)
    return r1, r2


def ffn_fwd(u0, convw, convb, wdown, gpost, x2, exch=None):
    T = x2.shape[0]
    tm, tn = _tile(T, 1024), FF_TILE
    nj = D_FF // tn

    def body(ug_ref, uu_ref, hg_ref, hu_ref, wg_ref, wu_ref, bg_ref, bu_ref, wd_ref, gp_ref, x_ref, wdp_ref,
             x3_ref, y_ref, aT_ref, acc, a_sc):
        i, j = pl.program_id(0), pl.program_id(1)
        first = i == 0

        @pl.when(j == 0)
        def _():
            acc[...] = jnp.zeros_like(acc)
            a_sc[...] = jnp.zeros_like(a_sc)

        acc[...] += _dot(a_sc[...], wdp_ref[...])

        def conv(u_ref, h_ref, w_ref, b_ref):
            u = u_ref[...]
            r1, r2 = _conv_taps(u, h_ref, first)
            return b_ref[...] + w_ref[0:1, :] * r2 + w_ref[1:2, :] * r1 + w_ref[2:3, :] * u

        a = _gelu(conv(ug_ref, hg_ref, wg_ref, bg_ref)) * conv(uu_ref, hu_ref, wu_ref, bu_ref)
        aT_ref[...] = a.T.astype(MXU)
        a_sc[...] = a.astype(MXU)

        @pl.when(j == nj - 1)
        def _():
            y = acc[...] + _dot(a_sc[...], wd_ref[...])
            y_ref[...] = y
            x3_ref[...] = x_ref[...] + _rms_fwd(y, gp_ref[...])

    halo = lambda off: (lambda i, j: (jnp.maximum(i * (tm // 8) - 1, 0), off + j))
    return _call_hosting(
        body, "ffn_fwd", (T // tm, nj), [u0, u0, u0, u0, convw, convw, convb, convb, wdown, gpost, x2, wdown],
        [pl.BlockSpec((tm, tn), lambda i, j: (i, j)), pl.BlockSpec((tm, tn), lambda i, j: (i, nj + j)),
         pl.BlockSpec((8, tn), halo(0)), pl.BlockSpec((8, tn), halo(nj)),
         pl.BlockSpec((3, tn), lambda i, j: (0, j)), pl.BlockSpec((3, tn), lambda i, j: (0, nj + j)),
         pl.BlockSpec((1, tn), lambda i, j: (0, j)), pl.BlockSpec((1, tn), lambda i, j: (0, nj + j)),
         pl.BlockSpec((tn, 1024), lambda i, j: (j, 0)),
         pl.BlockSpec((1, 1024), lambda i, j: (0, 0)),
         pl.BlockSpec((tm, 1024), lambda i, j: (i, 0)),
         pl.BlockSpec((tn, 1024), lambda i, j: (jnp.maximum(j - 1, 0), 0))],
        [pl.BlockSpec((tm, 1024), lambda i, j: (i, 0)), pl.BlockSpec((tm, 1024), lambda i, j: (i, 0)),
         pl.BlockSpec((tn, tm), lambda i, j: (j, i))],
        [jax.ShapeDtypeStruct((T, 1024), F32), jax.ShapeDtypeStruct((T, 1024), F32),
         jax.ShapeDtypeStruct((D_FF, T), MXU)],
        [pltpu.VMEM((tm, 1024), F32), pltpu.VMEM((tm, tn), MXU)], exch)


def ffn_bwd(dx3, y, u0, convw, convb, wdown, gpost, wupT, x2, gfpre, exch=None):
    T = dx3.shape[0]
    tm, tn = _tile(T, 512), FF_TILE
    nj = D_FF // tn
    ni = T // tm

    def body(dx_ref, y_ref, ug_ref, uu_ref, hg_ref, hu_ref, wg_ref, wu_ref, bg_ref, bu_ref, wd_ref, gp_ref,
             wtg_ref, wtu_ref, x2_ref, gf_ref, wdn_ref, wtgp_ref, wtup_ref,
             dy_ref, dug_ref, duu_ref, dcg_ref, dcu_ref, dgp_ref, dx2_ref, dgf_ref,
             dy_sc, dh_sc, da_sc, dug_sc, duu_sc, cg, cu, ag, au):
        s, j = pl.program_id(0), pl.program_id(1)
        i = ni - 1 - s
        first_tok = i == 0
        sub = lax.broadcasted_iota(jnp.int32, (8, 1), 0)
        slot = j % 2

        @pl.when(j == 0)
        def _():
            dy, dgp = _rms_bwd(dx_ref[...], y_ref[...], gp_ref[...])
            dy_sc[...] = dy.astype(MXU)
            dy_ref[...] = dy.astype(MXU)
            _acc_out(dgp_ref, dgp, s == 0)
            dh_sc[...] = jnp.zeros_like(dh_sc)
            da_sc[0] = _dot_nt(dy.astype(MXU), wd_ref[...])
            dug_sc[...] = jnp.zeros_like(dug_sc)
            duu_sc[...] = jnp.zeros_like(duu_sc)

        @pl.when(s == 0)
        def _():
            cg[j] = jnp.zeros((8, tn), F32)
            cu[j] = jnp.zeros((8, tn), F32)
            ag[j] = jnp.zeros((8, tn), F32)
            au[j] = jnp.zeros((8, tn), F32)

        da = da_sc[slot]
        da_sc[1 - slot] = _dot_nt(dy_sc[...], wdn_ref[...])
        dh_sc[...] += _dot(dug_sc[...], wtgp_ref[...]) + _dot(duu_sc[...], wtup_ref[...])

        def conv(u_ref, h_ref, w_ref, b_ref):
            u = u_ref[...]
            r1, r2 = _conv_taps(u, h_ref, first_tok)
            return b_ref[...] + w_ref[0:1, :] * r2 + w_ref[1:2, :] * r1 + w_ref[2:3, :] * u, u, r1, r2

        gate, ugv, g1, g2 = conv(ug_ref, hg_ref, wg_ref, bg_ref)
        up, uuv, u1, u2 = conv(uu_ref, hu_ref, wu_ref, bu_ref)
        gl, dgl = _gelu_with_grad(gate)
        dup = da * gl
        dgate = da * up * dgl

        def conv_bwd(du, u, r1, r2, w_ref, c_ref, a_ref, duT_ref, du_sc):
            nxt = c_ref[j]
            n0, n1 = nxt[0:1, :], nxt[1:2, :]
            f1, f2 = pltpu.roll(du, tm - 1, 0), pltpu.roll(du, tm - 2, 0)
            f1 = jnp.concatenate([f1[:tm - 8], jnp.where(sub == 7, n0, f1[tm - 8:])], axis=0)
            f2 = jnp.concatenate([f2[:tm - 8], jnp.where(sub == 7, n1, jnp.where(sub == 6, n0, f2[tm - 8:]))], axis=0)
            du0 = w_ref[2:3, :] * du + w_ref[1:2, :] * f1 + w_ref[0:1, :] * f2
            duT_ref[...] = du0.T.astype(MXU)
            du_sc[...] = du0.astype(MXU)
            c_ref[j] = du[0:8, :]
            red = lambda v: jnp.sum(v, axis=0, keepdims=True)
            part = jnp.where(sub == 0, red(du * r2), jnp.where(sub == 1, red(du * r1), jnp.where(
                sub == 2, red(du * u), jnp.where(sub == 3, red(du), 0.0))))
            a_ref[j] = a_ref[j] + part
            return a_ref[j]

        dcg_ref[0] = conv_bwd(dgate, ugv, g1, g2, wg_ref, cg, ag, dug_ref, dug_sc)
        dcu_ref[0] = conv_bwd(dup, uuv, u1, u2, wu_ref, cu, au, duu_ref, duu_sc)

        @pl.when(j == nj - 1)
        def _():
            dh = dh_sc[...] + _dot(dug_sc[...], wtg_ref[...]) + _dot(duu_sc[...], wtu_ref[...])
            dx, dgf = _rms_bwd(dh, x2_ref[...], gf_ref[...])
            dx2_ref[...] = dx_ref[...] + dx
            _acc_out(dgf_ref, dgf, s == 0)

    rev = lambda s: ni - 1 - s
    halo = lambda off: (lambda s, j: (jnp.maximum(rev(s) * (tm // 8) - 1, 0), off + j))
    tok = pl.BlockSpec((tm, 1024), lambda s, j: (rev(s), 0))
    vec = pl.BlockSpec((1, 1024), lambda s, j: (0, 0))
    return _call_hosting(
        body, "ffn_bwd", (ni, nj),
        [dx3, y, u0, u0, u0, u0, convw, convw, convb, convb, wdown, gpost, wupT, wupT, x2, gfpre,
         wdown, wupT, wupT],
        [tok, tok,
         pl.BlockSpec((tm, tn), lambda s, j: (rev(s), j)), pl.BlockSpec((tm, tn), lambda s, j: (rev(s), nj + j)),
         pl.BlockSpec((8, tn), halo(0)), pl.BlockSpec((8, tn), halo(nj)),
         pl.BlockSpec((3, tn), lambda s, j: (0, j)), pl.BlockSpec((3, tn), lambda s, j: (0, nj + j)),
         pl.BlockSpec((1, tn), lambda s, j: (0, j)), pl.BlockSpec((1, tn), lambda s, j: (0, nj + j)),
         pl.BlockSpec((tn, 1024), lambda s, j: (j, 0)), vec,
         pl.BlockSpec((tn, 1024), lambda s, j: (j, 0)), pl.BlockSpec((tn, 1024), lambda s, j: (nj + j, 0)),
         tok, vec,
         pl.BlockSpec((tn, 1024), lambda s, j: (jnp.minimum(j + 1, nj - 1), 0)),
         pl.BlockSpec((tn, 1024), lambda s, j: (jnp.maximum(j - 1, 0), 0)),
         pl.BlockSpec((tn, 1024), lambda s, j: (nj + jnp.maximum(j - 1, 0), 0))],
        [tok,
         pl.BlockSpec((tn, tm), lambda s, j: (j, rev(s))), pl.BlockSpec((tn, tm), lambda s, j: (j, rev(s))),
         pl.BlockSpec((1, 8, tn), lambda s, j: (s, 0, j)), pl.BlockSpec((1, 8, tn), lambda s, j: (s, 0, j)),
         vec, tok, vec],
        [jax.ShapeDtypeStruct((T, 1024), MXU), jax.ShapeDtypeStruct((D_FF, T), MXU),
         jax.ShapeDtypeStruct((D_FF, T), MXU),
         jax.ShapeDtypeStruct((ni, 8, D_FF), F32), jax.ShapeDtypeStruct((ni, 8, D_FF), F32),
         jax.ShapeDtypeStruct((1, 1024), F32), jax.ShapeDtypeStruct((T, 1024), F32),
         jax.ShapeDtypeStruct((1, 1024), F32)],
        [pltpu.VMEM((tm, 1024), MXU), pltpu.VMEM((tm, 1024), F32), pltpu.VMEM((2, tm, tn), F32),
         pltpu.VMEM((tm, tn), MXU), pltpu.VMEM((tm, tn), MXU)] + [pltpu.VMEM((nj, 8, tn), F32)] * 4, exch)


ELEMS_PER_BLOCK = 512 * 1024


def _row_block(R, C):
    if R * C <= ELEMS_PER_BLOCK or R % 8:
        return R
    best = 8
    for t in range(8, R + 1, 8):
        if R % t == 0 and t * C <= ELEMS_PER_BLOCK:
            best = t
    return best


def adamw(w, g, m, v, name, exch=None):
    L, R, C = w.shape
    partials = isinstance(g, (list, tuple))
    tr = _row_block(R, 2 * C)
    c1 = 1.0 - ADAM_B1 ** ADAM_STEP
    c2 = 1.0 - ADAM_B2 ** ADAM_STEP

    def body(w_ref, *rest):
        g_refs, (m_ref, v_ref, g_out, d_ref, nm_ref, nv_ref) = rest[:-6], rest[-6:]

        def step(gv):
            g_out[0] = gv
            nm = ADAM_B1 * m_ref[0] + (1.0 - ADAM_B1) * gv
            nv = ADAM_B2 * v_ref[0] + (1.0 - ADAM_B2) * (gv * gv)
            nm_ref[0] = nm
            nv_ref[0] = nv
            d_ref[0] = -ADAM_LR * ((nm / c1) / (jnp.sqrt(nv / c2) + ADAM_EPS) + ADAM_WD * w_ref[0])

        if not partials:
            step(g_refs[0][0])
            return
        for k in range(L):
            @pl.when(pl.program_id(0) == k)
            def _(k=k):
                gv = g_refs[k][0].astype(F32)
                for d in range(1, N_DEV):
                    gv = gv + g_refs[k][d].astype(F32)
                step(gv)

    spec = pl.BlockSpec((1, tr, C), lambda l, i: (l, i, 0))
    if partials:
        gspecs = [pl.BlockSpec((N_DEV, tr, C), lambda l, i, k=k: (0, jnp.where(l == k, i, 0), 0)) for k in range(L)]
        gs = list(g)
    else:
        gspecs, gs = [spec], [g]
    return _call_hosting(body, name, (L, R // tr), [w] + gs + [m, v], [spec] + gspecs + [spec, spec], [spec] * 4,
                         [jax.ShapeDtypeStruct((L, R, C), F32)] * 4, [], exch)


def sum_devices(buf, name):
    _, R, C = buf.shape
    tr = _row_block(R, C * 4)

    def body(b_ref, o_ref):
        acc = b_ref[0].astype(F32)
        for d in range(1, N_DEV):
            acc = acc + b_ref[d].astype(F32)
        o_ref[...] = acc

    return pl.pallas_call(
        body, name=name, grid=(R // tr,),
        in_specs=[pl.BlockSpec((N_DEV, tr, C), lambda i: (0, i, 0))],
        out_specs=pl.BlockSpec((tr, C), lambda i: (i, 0)),
        out_shape=jax.ShapeDtypeStruct((R, C), F32),
        compiler_params=_cparams(("parallel",)),
    )(buf)


def _exchange_copies(src_refs, out_refs, send_sems, recv_sems, gather):
    x, y, c = lax.axis_index("x"), lax.axis_index("y"), lax.axis_index("c")
    me = 4 * x + 2 * y + c
    flip = lambda a, bit: 1 - a if bit else a
    part = lambda ref, d: ref if gather else ref.at[d]
    copies = []
    for k in range(1, N_DEV):
        px, py, pc = flip(x, (k >> 2) & 1), flip(y, (k >> 1) & 1), flip(c, k & 1)
        peer = 4 * px + 2 * py + pc
        for t in range(len(src_refs)):
            sem = t * (N_DEV - 1) + k - 1
            mk = lambda s, d: pltpu.make_async_remote_copy(
                src_ref=s, dst_ref=d, send_sem=send_sems.at[sem], recv_sem=recv_sems.at[sem],
                device_id=(px, py, pc), device_id_type=pl.DeviceIdType.MESH)
            copies.append((mk(part(src_refs[t], peer), out_refs[t].at[me]),
                           mk(part(src_refs[t], me), out_refs[t].at[peer])))
    return me, copies


def exchange(srcs, name, gather):
    n = len(srcs)
    shapes = [(N_DEV,) + s.shape if gather else s.shape for s in srcs]

    def body(*refs):
        src_refs, out_refs = refs[:n], refs[n:2 * n]
        send_sems, recv_sems, local_sems = refs[2 * n:]
        me, copies = _exchange_copies(src_refs, out_refs, send_sems, recv_sems, gather)
        for outgoing, _ in copies:
            outgoing.start()
        mine = [pltpu.make_async_copy(src_refs[t] if gather else src_refs[t].at[me], out_refs[t].at[me],
                                      local_sems.at[t]) for t in range(n)]
        for cp in mine:
            cp.start()
        for _, incoming in copies:
            incoming.wait_recv()
        for outgoing, _ in copies:
            outgoing.wait_send()
        for cp in mine:
            cp.wait()

    return pl.pallas_call(
        body, name=name,
        in_specs=[pl.BlockSpec(memory_space=pl.ANY)] * n, out_specs=[pl.BlockSpec(memory_space=pl.ANY)] * n,
        out_shape=[jax.ShapeDtypeStruct(shp, s.dtype) for shp, s in zip(shapes, srcs)],
        scratch_shapes=[pltpu.SemaphoreType.DMA((n * (N_DEV - 1),)), pltpu.SemaphoreType.DMA((n * (N_DEV - 1),)),
                        pltpu.SemaphoreType.DMA((n,))],
    )(*srcs)


def hosted_exchange(body, n_in, n_out, n_scratch, grid, srcs, gather):
    n = len(srcs)
    shapes = [(N_DEV,) + s.shape if gather else s.shape for s in srcs]

    def wrapped(*refs):
        ins, xin = refs[:n_in], refs[n_in:n_in + n]
        outs = refs[n_in + n:n_in + n + n_out]
        xout = refs[n_in + n + n_out:n_in + 2 * n + n_out]
        rest = refs[n_in + 2 * n + n_out:]
        scratch, (send_sems, recv_sems, local_sems) = rest[:n_scratch], rest[n_scratch:]
        ids = [pl.program_id(a) for a in range(len(grid))]
        first = functools.reduce(jnp.logical_and, [i == 0 for i in ids])
        last = functools.reduce(jnp.logical_and, [i == g - 1 for i, g in zip(ids, grid)])
        me, copies = _exchange_copies(xin, xout, send_sems, recv_sems, gather)
        mine = [pltpu.make_async_copy(xin[t] if gather else xin[t].at[me], xout[t].at[me], local_sems.at[t])
                for t in range(n)]

        @pl.when(first)
        def _():
            for outgoing, _ in copies:
                outgoing.start()
            for cp in mine:
                cp.start()

        body(*ins, *outs, *scratch)

        @pl.when(last)
        def _():
            for _, incoming in copies:
                incoming.wait_recv()
            for outgoing, _ in copies:
                outgoing.wait_send()
            for cp in mine:
                cp.wait()

    any_spec = pl.BlockSpec(memory_space=pl.ANY)
    return wrapped, (list(srcs), [any_spec] * n, [any_spec] * n,
                     [jax.ShapeDtypeStruct(shp, s.dtype) for shp, s in zip(shapes, srcs)],
                     [pltpu.SemaphoreType.DMA((n * (N_DEV - 1),)), pltpu.SemaphoreType.DMA((n * (N_DEV - 1),)),
                      pltpu.SemaphoreType.DMA((n,))])


def _pack(parts, cols, row_align, dtype):
    flat = jnp.concatenate([p.astype(dtype) for p in parts], axis=-1)
    n = flat.shape[-1]
    block = cols * row_align
    total = -(-n // block) * block
    flat = jnp.pad(flat, [(0, 0)] * (flat.ndim - 1) + [(0, total - n)])
    return flat.reshape(flat.shape[:-1] + (total // cols, cols))


def _unpack(buf, shapes):
    lead = buf.shape[:-2]
    flat = buf.reshape(lead + (-1,))
    out, off = [], 0
    for s in shapes:
        n = int(np.prod(s))
        out.append(flat[..., off:off + n].reshape(lead + tuple(s)))
        off += n
    return out


SHARDED = ["w_in", "w_uq", "w_ukv", "w_out", "w_up", "w_down"]
ATTN_SENT = ["w_in_p", "w_uq", "w_ukv", "w_out"]
UP_HALF = 352
FFN_SIDE = ["w_upT", "conv_w", "w_down"]


def _full_from_shards(name, s):
    if name in ("w_in", "w_in_p", "w_out", "w_down", "w_upT"):
        return s.reshape((-1, s.shape[-1]))
    return s.transpose(1, 0, 2).reshape((s.shape[1], -1))


def _shards_from_full(name, f):
    if name in ("w_in", "w_in_p", "w_out", "w_down", "w_upT"):
        return f.reshape((N_DEV, -1, f.shape[-1]))
    return f.reshape((f.shape[0], N_DEV, -1)).transpose(1, 0, 2)


def _perm_w_in(w):
    z = lambda n: jnp.zeros(w.shape[:-1] + (n,), w.dtype)
    return jnp.concatenate([w[..., :1536], w[..., 1540:1924], w[..., 1536:1540], z(60), w[..., 1924:1956], z(32)],
                           axis=-1)


def _unperm_w_in(d):
    return jnp.concatenate([d[..., :1536], d[..., 1920:1924], d[..., 1536:1920], d[..., 1984:2016]], axis=-1)


def _perm_w_uq(w):
    return jnp.pad(w.reshape(256, 4, MLA_QK_DIM), ((0, 0), (0, 0), (0, 128 - MLA_QK_DIM))).reshape(256, 512)


def _unperm_w_uq(d):
    return d.reshape(256, 4, 128)[:, :, :MLA_QK_DIM].reshape(256, 4 * MLA_QK_DIM)


def _perm_w_ukv(w):
    w4 = w.reshape(128, 4, 128)
    k = jnp.pad(w4[:, :, :64], ((0, 0), (0, 0), (0, 64))).reshape(128, 512)
    return jnp.concatenate([k, w4[:, :, 64:].reshape(128, 256)], axis=1)


def _unperm_w_ukv(d):
    dk = d[:, :512].reshape(128, 4, 128)[:, :, :64]
    dv = d[:, 512:].reshape(128, 4, 64)
    return jnp.concatenate([dk, dv], axis=-1).reshape(128, 512)


def _row(v, width=None):
    v = v.reshape(1, -1).astype(F32)
    if width is not None and v.shape[1] < width:
        v = jnp.pad(v, ((0, 0), (0, width - v.shape[1])))
    return v


def _layer_fwd(x, P, shared, send=None, ffn_from=None):
    cosr, sinr, bias = shared
    ex = lambda host: (send[host], True) if send is not None and send.get(host) else None
    proj, hT, projb = norm_matmul(x, P["g_pre"], P["w_in_p"], "in_proj", lo_tiles=2, tn_pref=1024)
    qm, km, vm, cqT, ckvT = mla_prep(proj, P["gq"], P["gkv"], P["w_uq_p"], P["w_ukv_p"], cosr, sinr)
    fcol, frow, frep = fox_gate(proj, P["fbias"])
    (oa, lse_a), _ = swa_fwd(proj, bias, P["sinks"])
    (ob, lrb), got_fox = flash_fwd(projb, projb, projb, frep, frow, qblk=C_QF // 128, kblk=C_KF // 128,
                                   vblk=C_VF // 128, nq=1, scale=HEAD_DIM ** -0.5, name="fox_fwd", exch=ex("fox"))
    (oc, lrc), got_mla = flash_fwd(qm, km, vm, None, None, qblk=0, kblk=0, vblk=0, nq=2,
                                   scale=MLA_QK_DIM ** -0.5, name="mla_fwd", exch=ex("mla"))
    x2, y1, mT = attn_out(oa, ob, oc, P["gn"], P["w_out"], P["g_apost"], x)
    if ffn_from is not None:
        P = dict(P, **ffn_from(None, got_fox, got_mla))
    up = norm_matmul(x2, P["g_fpre"], P["w_upT"], "up_proj", tn_pref=1536, w_transposed=True,
                     h_transposed=False, exch=ex("up"))
    (u0, h2), got_up = up if ex("up") is not None else (up, None)
    if ffn_from is not None:
        P = dict(P, **ffn_from(got_up, None, None))
    (x3, y2, aT), got_ffn = ffn_fwd(u0, P["conv_w"], P["conv_b"], P["w_down"], P["g_fpost"], x2, exch=ex("ffn"))
    S = dict(x=x, proj=proj, projb=projb, hT=hT, qm=qm, km=km, vm=vm, cqT=cqT, ckvT=ckvT, fcol=fcol, frow=frow,
             oa=oa, lse_a=lse_a, ob=ob, lrb=lrb, oc=oc, lrc=lrc,
             x2=x2, y1=y1, mT=mT, u0=u0, h2=h2, y2=y2, aT=aT)
    return x3, S, P, got_ffn


def _layer_bwd(dx3, P, S, shared, send_attn=None):
    cosr, sinr, bias = shared
    proj = S["proj"]
    G = {}
    got = {}
    ex = lambda arrays: (arrays, False) if send_attn is not None and arrays else None
    (dy2, dugT, duuT, dcg, dcu, G["ffn_post_norm"], dx2, G["ffn_pre_norm"]), got["ffn"] = ffn_bwd(
        dx3, S["y2"], S["u0"], P["conv_w"], P["conv_b"], P["w_down"], P["g_fpost"], P["w_upT"], S["x2"], P["g_fpre"],
        exch=ex(send_attn))
    dconv = jnp.concatenate([dcg[-1], dcu[-1]], axis=1)
    G["conv_w"], G["conv_b"] = dconv[0:3], dconv[3]
    G["w_down"] = matmul_nn(S["aT"], dy2, "dw_down", MXU)
    G["w_upT"] = jnp.concatenate([matmul_nn(dugT, S["h2"], "dw_up_gate", MXU),
                                  matmul_nn(duuT, S["h2"], "dw_up_up", MXU)], axis=0)
    G["w_up"] = G["w_upT"].T
    dy1, doa, dob, doc, G["group_norm"], G["attn_post_norm"] = attn_out_bwd(
        dx2, S["y1"], S["oa"], S["ob"], S["oc"], P["gn"], P["w_out"], P["g_apost"])
    G["w_out"] = matmul_nn(S["mT"], dy1, "dw_out", MXU)
    up_slices = _shards_from_full("w_upT", G["w_upT"])
    (dqa, dka, dva, dbias, dsk), _ = swa_bwd(proj, bias, P["sinks"], doa, S["oa"], S["lse_a"])
    G["swa_sinks"] = dsk[:, 0]
    pb = S["projb"]
    (dqf, dkf, dvf, dFk, dFq), got["fox"] = flash_bwd(
        pb, pb, pb, dob, S["ob"], S["lrb"], S["fcol"], S["frow"], name="fox_bwd", qblk=C_QF // 128,
        kblk=C_KF // 128, vblk=C_VF // 128, nq=1, scale=HEAD_DIM ** -0.5,
        exch=ex([up_slices[:, :UP_HALF], _shards_from_full("w_down", G["w_down"])]))
    dmisc_f, dfb = fox_gate_bwd(dFq, dFk, proj, P["fbias"])
    G["forget_bias"] = dfb[0, 0:4]
    (dqm_, dkm_, dvm_), got["mla"] = flash_bwd(
        S["qm"], S["km"], S["vm"], doc, S["oc"], S["lrc"], None, None, name="mla_bwd",
        qblk=0, kblk=0, vblk=0, nq=2, scale=MLA_QK_DIM ** -0.5, exch=ex([up_slices[:, UP_HALF:]]))
    dqm, dkv, dcq, dckv, dmisc_r, G["q_latent_norm"], G["kv_latent_norm"] = mla_prep_bwd(
        dqm_, dkm_, dvm_, proj, P["gq"], P["gkv"], P["w_uq_p"], P["w_ukv_p"], cosr, sinr)
    G["w_uq"] = _unperm_w_uq(matmul_nn(S["cqT"], dqm, "dw_uq", MXU))
    G["w_ukv"] = _unperm_w_ukv(matmul_nn(S["ckvT"], dkv, "dw_ukv", MXU))
    dproj = jnp.concatenate([dqa, dka, dva, dqf, dkf, dvf, dcq, dckv, dmisc_f + dmisc_r], axis=1).astype(MXU)
    G["w_in_p"] = matmul_nn(S["hT"], dproj, "dw_in", MXU)
    G["w_in"] = _unperm_w_in(G["w_in_p"])
    dx, G["attn_pre_norm"] = matmul_nt_normbwd(dproj, P["w_in_p"], S["x"], P["g_pre"], dx2, "in_bwd")
    return dx, G, dbias, got


def _layer_params(l, full, small):
    return dict(
        g_pre=_row(small["attn_pre_norm"][l]),
        w_in_p=full["w_in_p"] if "w_in_p" in full else _perm_w_in(full["w_in"]),
        gq=_row(small["q_latent_norm"][l]), gkv=_row(small["kv_latent_norm"][l]),
        w_uq_p=_perm_w_uq(full["w_uq"]), w_ukv_p=_perm_w_ukv(full["w_ukv"]),
        fbias=_row(small["forget_bias"][l], 128), sinks=small["swa_sinks"][l].astype(F32),
        gn=_row(small["group_norm"][l]), w_out=full["w_out"], g_apost=_row(small["attn_post_norm"][l]),
        g_fpre=_row(small["ffn_pre_norm"][l]), conv_b=_row(small["conv_b"][l]),
        g_fpost=_row(small["ffn_post_norm"][l]),
        **{n: full[n] for n in FFN_SIDE if n in full},
        **({"w_upT": full["w_up"].T} if "w_up" in full else {}))


def _rel_bias_grad(dbias, bucket):
    flat = dbias.reshape(SWA_Q_HEADS, -1)
    hi = flat.astype(MXU)
    lo = (flat - hi.astype(F32)).astype(MXU)
    onehot = (bucket[:, None] == jnp.arange(128, dtype=jnp.int32)[None, :]).astype(MXU)
    r = matmul_nn(jnp.concatenate([hi, lo], axis=0), onehot, "rel_bias_grad")
    return (r[0:8] + r[8:16])[:, :REL_BUCKETS].T


def local_step(x, tgt, fulls, small, comm=None):
    T = x.shape[0]
    cosr, sinr = rope_tables(T)
    bias, bucket = swa_bias_table(small["rel_bias"])
    shared = (cosr, sinr, bias)
    Ps, Ss = [], []
    h, full = x, fulls[0]
    for l in range(DEPTH):
        P = _layer_params(l, full, small)
        if comm:
            h, S, P, got = _layer_fwd(h, P, shared, comm["weight_parts"](l), comm["ffn_from"])
            full = comm["attn_from"](got) if l + 1 < DEPTH else None
        else:
            h, S, P, _ = _layer_fwd(h, P, shared)
            full = fulls[l + 1] if l + 1 < DEPTH else None
        Ps.append(P)
        Ss.append(S)
    dh, sq = loss_kernel(h, tgt)
    grads = [None] * DEPTH
    dbias_sum = None
    pending = [] if comm else None
    for l in reversed(range(DEPTH)):
        dh, grads[l], dbias, got = _layer_bwd(dh, Ps[l], Ss[l], shared, pending)
        dbias_sum = dbias if dbias_sum is None else dbias_sum + dbias
        if comm:
            comm["landed"](l, ["w_down"], got["fox"][1:])
            comm["landed"](l, ["w_upT"], [jnp.concatenate([got["fox"][0], got["mla"][0]], axis=1)])
            if pending:
                comm["landed"](l + 1, ATTN_SENT, got["ffn"])
            pending = [_shards_from_full(n, grads[l][n]) for n in ATTN_SENT]
    return sq, dh, grads, _rel_bias_grad(dbias_sum, bucket), pending


WEIGHTS = ['attn_pre_norm', 'w_in', 'forget_bias', 'swa_sinks', 'rel_bias', 'q_latent_norm', 'w_uq',
           'kv_latent_norm', 'w_ukv', 'group_norm', 'w_out', 'attn_post_norm', 'ffn_pre_norm', 'w_up', 'conv_w',
           'conv_b', 'w_down', 'ffn_post_norm']
SMALL_PER_LAYER = ['attn_pre_norm', 'forget_bias', 'swa_sinks', 'q_latent_norm', 'kv_latent_norm', 'group_norm',
                   'attn_post_norm', 'ffn_pre_norm', 'conv_b', 'ffn_post_norm', 'conv_w']


def kernel(x, attn_pre_norm, w_in, forget_bias, swa_sinks, rel_bias, q_latent_norm, w_uq, kv_latent_norm, w_ukv, group_norm, w_out, attn_post_norm, ffn_pre_norm, w_up, conv_w, conv_b, w_down, ffn_post_norm, loss_target, m_attn_pre_norm, m_w_in, m_forget_bias, m_swa_sinks, m_rel_bias, m_q_latent_norm, m_w_uq, m_kv_latent_norm, m_w_ukv, m_group_norm, m_w_out, m_attn_post_norm, m_ffn_pre_norm, m_w_up, m_conv_w, m_conv_b, m_w_down, m_ffn_post_norm, v_attn_pre_norm, v_w_in, v_forget_bias, v_swa_sinks, v_rel_bias, v_q_latent_norm, v_w_uq, v_kv_latent_norm, v_w_ukv, v_group_norm, v_w_out, v_attn_post_norm, v_ffn_pre_norm, v_w_up, v_conv_w, v_conv_b, v_w_down, v_ffn_post_norm):
    W = dict(attn_pre_norm=attn_pre_norm, w_in=w_in, forget_bias=forget_bias, swa_sinks=swa_sinks, rel_bias=rel_bias,
             q_latent_norm=q_latent_norm, w_uq=w_uq, kv_latent_norm=kv_latent_norm, w_ukv=w_ukv,
             group_norm=group_norm, w_out=w_out, attn_post_norm=attn_post_norm, ffn_pre_norm=ffn_pre_norm,
             w_up=w_up, conv_w=conv_w, conv_b=conv_b, w_down=w_down, ffn_post_norm=ffn_post_norm)
    M = dict(attn_pre_norm=m_attn_pre_norm, w_in=m_w_in, forget_bias=m_forget_bias, swa_sinks=m_swa_sinks,
             rel_bias=m_rel_bias, q_latent_norm=m_q_latent_norm, w_uq=m_w_uq, kv_latent_norm=m_kv_latent_norm,
             w_ukv=m_w_ukv, group_norm=m_group_norm, w_out=m_w_out, attn_post_norm=m_attn_post_norm,
             ffn_pre_norm=m_ffn_pre_norm, w_up=m_w_up, conv_w=m_conv_w, conv_b=m_conv_b, w_down=m_w_down,
             ffn_post_norm=m_ffn_post_norm)
    V = dict(attn_pre_norm=v_attn_pre_norm, w_in=v_w_in, forget_bias=v_forget_bias, swa_sinks=v_swa_sinks,
             rel_bias=v_rel_bias, q_latent_norm=v_q_latent_norm, w_uq=v_w_uq, kv_latent_norm=v_kv_latent_norm,
             w_ukv=v_w_ukv, group_norm=v_group_norm, w_out=v_w_out, attn_post_norm=v_attn_post_norm,
             ffn_pre_norm=v_ffn_pre_norm, w_up=v_w_up, conv_w=v_conv_w, conv_b=v_conv_b, w_down=v_w_down,
             ffn_post_norm=v_ffn_post_norm)
    me = 4 * lax.axis_index("x") + 2 * lax.axis_index("y") + lax.axis_index("c")

    def attn_shards(l):
        return [_perm_w_in(w_in[l].astype(MXU))] + [W[n][l].astype(MXU) for n in ATTN_SENT[1:]]

    def weight_parts(l):
        up = jnp.swapaxes(W["w_up"][l], 0, 1).astype(MXU)
        return dict(up=[W["w_down"][l].astype(MXU)], fox=[up[:UP_HALF]], mla=[up[UP_HALF:], conv_w[l]],
                    ffn=attn_shards(l + 1) if l + 1 < DEPTH else [])

    def ffn_from(got_up, got_fox, got_mla):
        if got_up is not None:
            return dict(w_down=_full_from_shards("w_down", got_up[0]))
        return dict(w_upT=_full_from_shards("w_upT", jnp.concatenate([got_fox[0], got_mla[0]], axis=1)),
                    conv_w=got_mla[1].transpose(1, 0, 2).reshape(3, 2 * D_FF))

    def attn_from(got_ffn):
        return {n: _full_from_shards(n, s) for n, s in zip(ATTN_SENT, got_ffn)}

    landed = [{} for _ in range(DEPTH)]

    def on_landed(l, names, arrays):
        landed[l].update(zip(names, arrays))

    comm = dict(weight_parts=weight_parts, ffn_from=ffn_from, attn_from=attn_from, landed=on_landed)
    full0 = dict(zip(ATTN_SENT, map(_full_from_shards, ATTN_SENT, exchange(attn_shards(0), "gather_weights", True))))
    sq, dx, grads, drel, last = local_step(x[0], loss_target[0], [full0], W, comm)
    G, delta, new_m, new_v = {}, {}, {}, {}

    def update(n, exch=None):
        shp = W[n].shape
        if n == "w_up":
            v3 = lambda a: jnp.swapaxes(a, 1, 2)
            back = v3
            g = [landed[l]["w_upT"] for l in range(DEPTH)]
        else:
            v3 = lambda a: a.reshape(shp if len(shp) == 3 else (1,) + shp)
            back = lambda a: a.reshape(shp)
            g = [landed[l][n] for l in range(DEPTH)] if n in SHARDED else v3(G[n])
        (g, d, nm, nv), got = adamw(v3(W[n]), g, v3(M[n]), v3(V[n]), "adamw_" + n, exch)
        G[n], delta[n], new_m[n], new_v[n] = back(g), back(d), back(nm), back(nv)
        return got

    parts, shapes = [], []
    for l in range(DEPTH):
        for n in SMALL_PER_LAYER:
            parts.append(grads[l][n].astype(F32).reshape(-1))
            shapes.append(grads[l][n].shape)
    parts += [drel.reshape(-1), jnp.sum(sq).reshape(1) * (0.5 / D_MODEL)]
    shapes += [drel.shape, (1,)]
    on_landed(0, ATTN_SENT, update("w_up", (last, False)))
    for l in range(DEPTH):
        landed[l]["w_in"] = _unperm_w_in(landed[l]["w_in_p"])
    gathered = update("w_down", ([_pack(parts, 128, 8, F32)], True))[0]
    red = _unpack(sum_devices(gathered, "sum_small"), shapes)
    k = 0
    per = {n: [] for n in SMALL_PER_LAYER}
    for l in range(DEPTH):
        for n in SMALL_PER_LAYER:
            per[n].append(red[k])
            k += 1
    for n in SMALL_PER_LAYER:
        G[n] = jnp.stack(per[n]).reshape((DEPTH, 3, 2 * D_FF) if n == "conv_w" else W[n].shape)
    G["rel_bias"] = red[k]
    loss = red[k + 1][0]
    G["conv_w"] = lax.dynamic_slice_in_dim(G["conv_w"], me * 704, 704, axis=2)

    for n in WEIGHTS:
        if n not in ("w_up", "w_down"):
            update(n)
    return (loss, dx[None], *[G[n] for n in WEIGHTS], *[delta[n] for n in WEIGHTS],
            *[new_m[n] for n in WEIGHTS], *[new_v[n] for n in WEIGHTS])
```

```python
import functools
import math

import numpy as np
import jax
import jax.numpy as jnp
from jax import lax
from jax.experimental import pallas as pl
from jax.experimental.pallas import tpu as pltpu

F32 = jnp.float32
MXU = jnp.bfloat16

N_DEV = 8
DEPTH = 4
D_MODEL = 1024
HEAD_DIM = 64
WINDOW = 128
SWA_Q_HEADS = 8
REL_BUCKETS = 32
REL_MAX_DIST = 128
MLA_QK_DIM = 96
ROPE_DIM = 32
ROPE_THETA = 10000.0
D_FF = 2816
EPS = 1e-6
NEG = -1e30
IN_COLS = 1956
C_QA, C_KA, C_VA = 0, 512, 640
C_QF, C_KF, C_VF = 768, 1024, 1280
C_CQ, C_CKV, C_MISC = 1536, 1792, 1920
ROPE_LANE0 = 64
ADAM_LR, ADAM_B1, ADAM_B2, ADAM_EPS, ADAM_WD, ADAM_STEP = 0.001, 0.9, 0.999, 1e-08, 0.01, 10

VMEM_LIMIT = 56 * 1024 * 1024


def _cparams(sem=None):
    return pltpu.CompilerParams(dimension_semantics=sem, vmem_limit_bytes=VMEM_LIMIT)


def _tile(n, pref):
    if n <= pref:
        return n
    t = pref - pref % 128
    while t >= 128:
        if n % t == 0:
            return t
        t -= 128
    return n


def _dot(a, b):
    return jnp.dot(a.astype(MXU), b.astype(MXU), preferred_element_type=F32)


def _dot_nt(a, b):
    return lax.dot_general(a.astype(MXU), b.astype(MXU), (((1,), (1,)), ((), ())),
                           preferred_element_type=F32)


def _rms_fwd(x, g):
    return x * lax.rsqrt(jnp.mean(x * x, axis=-1, keepdims=True) + EPS) * g


def _rms_bwd(dy, x, g, n=None):
    r = lax.rsqrt(jnp.mean(x * x, axis=-1, keepdims=True) + EPS)
    xh = x * r
    dg = jnp.sum(dy * xh, axis=0, keepdims=True)
    dxh = dy * g
    dx = r * (dxh - xh * jnp.mean(dxh * xh, axis=-1, keepdims=True))
    return dx, dg


def _acc_out(ref, val, first):
    @pl.when(first)
    def _():
        ref[...] = val

    @pl.when(jnp.logical_not(first))
    def _():
        ref[...] += val


def norm_matmul(x, g, w, name, lo_tiles=0, tn_pref=512, w_transposed=False, h_transposed=True, exch=None):
    T, K = x.shape
    N = w.shape[0] if w_transposed else w.shape[1]
    tm, tn = _tile(T, 1024), _tile(N, tn_pref)

    def body(x_ref, g_ref, w_ref, o_ref, hT_ref, *rest):
        h_sc = rest[-1]
        j = pl.program_id(1)

        @pl.when(j == 0)
        def _():
            h = _rms_fwd(x_ref[...], g_ref[...])
            h_sc[...] = h.astype(MXU)
            hT_ref[...] = (h.T if h_transposed else h).astype(MXU)

        r = (_dot_nt if w_transposed else _dot)(h_sc[...], w_ref[...])
        o_ref[...] = r
        if lo_tiles:
            @pl.when(j < lo_tiles)
            def _():
                rest[0][...] = r.astype(MXU)

    h_spec = pl.BlockSpec((K, tm), lambda i, j: (0, i)) if h_transposed else pl.BlockSpec((tm, K), lambda i, j: (i, 0))
    out_specs = [pl.BlockSpec((tm, tn), lambda i, j: (i, j)), h_spec]
    out_shape = [jax.ShapeDtypeStruct((T, N), F32), jax.ShapeDtypeStruct((K, T) if h_transposed else (T, K), MXU)]
    if lo_tiles:
        out_specs.append(pl.BlockSpec((tm, tn), lambda i, j: (i, jnp.minimum(j, lo_tiles - 1))))
        out_shape.append(jax.ShapeDtypeStruct((T, lo_tiles * tn), MXU))
    in_specs = [pl.BlockSpec((tm, K), lambda i, j: (i, 0)),
                pl.BlockSpec((1, K), lambda i, j: (0, 0)),
                pl.BlockSpec((tn, K), lambda i, j: (j, 0)) if w_transposed else
                pl.BlockSpec((K, tn), lambda i, j: (0, j))]
    outs, landed = _call_hosting(body, name, (T // tm, N // tn), [x, g, w], in_specs, out_specs, out_shape,
                                 [pltpu.VMEM((tm, K), MXU)], exch)
    return outs if exch is None else (outs, landed)


def matmul_nn(a, b, name, out_dtype=F32, exch=None):
    M, K = a.shape
    N = b.shape[1]
    tm, tn, tk = _tile(M, 1408), _tile(N, 1536), _tile(K, 1024)
    nk = K // tk

    def body(a_ref, b_ref, o_ref, acc):
        k = pl.program_id(2)
        part = _dot(a_ref[...], b_ref[...])
        _acc_out(acc, part, k == 0)

        @pl.when(k == nk - 1)
        def _():
            o_ref[...] = acc[...].astype(out_dtype)

    outs, landed = _call_hosting(
        body, name, (M // tm, N // tn, nk), [a, b],
        [pl.BlockSpec((tm, tk), lambda i, j, k: (i, k)), pl.BlockSpec((tk, tn), lambda i, j, k: (k, j))],
        [pl.BlockSpec((tm, tn), lambda i, j, k: (i, j))], [jax.ShapeDtypeStruct((M, N), out_dtype)],
        [pltpu.VMEM((tm, tn), F32)], exch)
    return outs[0] if exch is None else (outs[0], landed)


def matmul_nt_normbwd(dy, w, x, g, dres, name):
    T, N = dy.shape
    K = w.shape[0]
    tm, tn = _tile(T, 1024), _tile(N, 1536)
    nj = N // tn

    def body(dy_ref, w_ref, x_ref, g_ref, dres_ref, dx_ref, dg_ref, acc):
        i, j = pl.program_id(0), pl.program_id(1)
        _acc_out(acc, _dot_nt(dy_ref[...], w_ref[...]), j == 0)

        @pl.when(j == nj - 1)
        def _():
            dx, dg = _rms_bwd(acc[...], x_ref[...], g_ref[...])
            dx_ref[...] = dres_ref[...] + dx
            _acc_out(dg_ref, dg, i == 0)

    return pl.pallas_call(
        body, name=name, grid=(T // tm, nj),
        in_specs=[pl.BlockSpec((tm, tn), lambda i, j: (i, j)),
                  pl.BlockSpec((K, tn), lambda i, j: (0, j)),
                  pl.BlockSpec((tm, K), lambda i, j: (i, 0)),
                  pl.BlockSpec((1, K), lambda i, j: (0, 0)),
                  pl.BlockSpec((tm, K), lambda i, j: (i, 0))],
        out_specs=[pl.BlockSpec((tm, K), lambda i, j: (i, 0)),
                   pl.BlockSpec((1, K), lambda i, j: (0, 0))],
        out_shape=[jax.ShapeDtypeStruct((T, K), F32), jax.ShapeDtypeStruct((1, K), F32)],
        scratch_shapes=[pltpu.VMEM((tm, K), F32)],
        compiler_params=_cparams(("arbitrary", "arbitrary")),
    )(dy, w, x, g, dres)


def loss_kernel(y, tgt):
    T, D = y.shape
    tm = _tile(T, 512)

    def body(y_ref, t_ref, dy_ref, acc_ref):
        e = y_ref[...] - t_ref[...]
        dy_ref[...] = e * (1.0 / D)
        _acc_out(acc_ref, jnp.sum(e * e, axis=0, keepdims=True), pl.program_id(0) == 0)

    return pl.pallas_call(
        body, name="loss", grid=(T // tm,),
        in_specs=[pl.BlockSpec((tm, D), lambda i: (i, 0)), pl.BlockSpec((tm, D), lambda i: (i, 0))],
        out_specs=[pl.BlockSpec((tm, D), lambda i: (i, 0)), pl.BlockSpec((1, D), lambda i: (0, 0))],
        out_shape=[jax.ShapeDtypeStruct((T, D), F32), jax.ShapeDtypeStruct((1, D), F32)],
        compiler_params=_cparams(("arbitrary",)),
    )(y, tgt)


def _rope_partner(x):
    lane = lax.broadcasted_iota(jnp.int32, (1, 128), 1)
    return jnp.where(lane < ROPE_LANE0 + ROPE_DIM // 2, pltpu.roll(x, 128 - ROPE_DIM // 2, 1),
                     pltpu.roll(x, ROPE_DIM // 2, 1))


def _rope_apply(x, cos, sin_signed):
    return x * cos + _rope_partner(x) * sin_signed


def _rope_apply_bwd(dy, cos, sin_signed):
    lane = lax.broadcasted_iota(jnp.int32, (1, 128), 1)
    rotary = (lane >= ROPE_LANE0) & (lane < ROPE_LANE0 + ROPE_DIM)
    return dy * cos + jnp.where(rotary, _rope_partner(dy * sin_signed), 0.0)


def rope_tables(T):
    pos = jnp.arange(T, dtype=F32)
    inv_freq = ROPE_THETA ** (-(jnp.arange(ROPE_DIM // 2, dtype=F32) * 2.0 / ROPE_DIM))
    ang = pos[:, None] * inv_freq[None, :]
    cos, sin = jnp.cos(ang), jnp.sin(ang)
    z = jnp.zeros((T, ROPE_LANE0), F32)
    z2 = jnp.zeros((T, 128 - ROPE_LANE0 - ROPE_DIM), F32)
    cosr = jnp.concatenate([z, cos, cos, z2], axis=1)
    sinr = jnp.concatenate([z, -sin, sin, z2], axis=1)
    return cosr, sinr


def mla_prep(proj, gq, gkv, wuq, wukv, cosr, sinr):
    T = proj.shape[0]
    tm = _tile(T, 512)

    def body(cq_ref, ckv_ref, misc_ref, gq_ref, gkv_ref, wuq_ref, wukv_ref, cos_ref, sin_ref,
             q_ref, k_ref, v_ref, cqT_ref, ckvT_ref):
        lane = lax.broadcasted_iota(jnp.int32, (1, 128), 1)
        cosr_, sinr_ = cos_ref[...], sin_ref[...]
        cosq = cosr_ + jnp.where(lane < ROPE_LANE0, 1.0, 0.0)
        cqn = _rms_fwd(cq_ref[...], gq_ref[...])
        cqT_ref[...] = cqn.T.astype(MXU)
        qm = _dot(cqn, wuq_ref[...])
        q_ref[...] = jnp.concatenate(
            [_rope_apply(qm[:, 128 * h:128 * (h + 1)], cosq, sinr_) for h in range(4)], axis=1).astype(MXU)
        ckvn = _rms_fwd(ckv_ref[...], gkv_ref[...])
        ckvT_ref[...] = ckvn.T.astype(MXU)
        kv = _dot(ckvn, wukv_ref[...])
        kr = _rope_apply(misc_ref[...], cosr_, sinr_)
        k_ref[...] = jnp.concatenate(
            [kv[:, 128 * h:128 * (h + 1)] + kr for h in range(4)], axis=1).astype(MXU)
        v_ref[...] = kv[:, 512:768].astype(MXU)

    row = lambda i: (i, 0)
    const = lambda i: (0, 0)
    return pl.pallas_call(
        body, name="mla_prep", grid=(T // tm,),
        in_specs=[pl.BlockSpec((tm, 256), lambda i: (i, C_CQ // 256)),
                  pl.BlockSpec((tm, 128), lambda i: (i, C_CKV // 128)),
                  pl.BlockSpec((tm, 128), lambda i: (i, C_MISC // 128)),
                  pl.BlockSpec((1, 256), const), pl.BlockSpec((1, 128), const),
                  pl.BlockSpec((256, 512), const), pl.BlockSpec((128, 768), const),
                  pl.BlockSpec((tm, 128), row), pl.BlockSpec((tm, 128), row)],
        out_specs=[pl.BlockSpec((tm, 512), row), pl.BlockSpec((tm, 512), row), pl.BlockSpec((tm, 256), row),
                   pl.BlockSpec((256, tm), lambda i: (0, i)), pl.BlockSpec((128, tm), lambda i: (0, i))],
        out_shape=[jax.ShapeDtypeStruct((T, 512), MXU), jax.ShapeDtypeStruct((T, 512), MXU),
                   jax.ShapeDtypeStruct((T, 256), MXU),
                   jax.ShapeDtypeStruct((256, T), MXU), jax.ShapeDtypeStruct((128, T), MXU)],
        compiler_params=_cparams(("parallel",)),
    )(proj, proj, proj, gq, gkv, wuq, wukv, cosr, sinr)


def mla_prep_bwd(dq, dk, dv, proj, gq, gkv, wuq, wukv, cosr, sinr):
    T = proj.shape[0]
    tm = _tile(T, 512)

    def body(dq_ref, dk_ref, dv_ref, cq_ref, ckv_ref, gq_ref, gkv_ref, wuq_ref, wukv_ref, cos_ref, sin_ref,
             dqm_ref, dkv_ref, dcq_ref, dckv_ref, dmisc_ref, dgq_ref, dgkv_ref):
        first = pl.program_id(0) == 0
        lane = lax.broadcasted_iota(jnp.int32, (1, 128), 1)
        cosr_, sinr_ = cos_ref[...], sin_ref[...]
        cosq = cosr_ + jnp.where(lane < ROPE_LANE0, 1.0, 0.0)
        dqv = dq_ref[...]
        dqm = jnp.concatenate(
            [_rope_apply_bwd(dqv[:, 128 * h:128 * (h + 1)], cosq, sinr_) for h in range(4)], axis=1)
        dqm_ref[...] = dqm.astype(MXU)
        dcq, dgq = _rms_bwd(_dot_nt(dqm, wuq_ref[...]), cq_ref[...], gq_ref[...])
        dcq_ref[...] = dcq
        _acc_out(dgq_ref, dgq, first)
        dkv_ = dk_ref[...]
        heads = [dkv_[:, 128 * h:128 * (h + 1)] for h in range(4)]
        dkr = heads[0] + heads[1] + heads[2] + heads[3]
        dmisc_ref[...] = _rope_apply_bwd(dkr, cosr_, sinr_)
        dkvm = jnp.concatenate([jnp.where(lane < ROPE_LANE0, hd, 0.0) for hd in heads] + [dv_ref[...]], axis=1)
        dkv_ref[...] = dkvm.astype(MXU)
        dckv, dgkv = _rms_bwd(_dot_nt(dkvm, wukv_ref[...]), ckv_ref[...], gkv_ref[...])
        dckv_ref[...] = dckv
        _acc_out(dgkv_ref, dgkv, first)

    row = lambda i: (i, 0)
    const = lambda i: (0, 0)
    return pl.pallas_call(
        body, name="mla_prep_bwd", grid=(T // tm,),
        in_specs=[pl.BlockSpec((tm, 512), row), pl.BlockSpec((tm, 512), row), pl.BlockSpec((tm, 256), row),
                  pl.BlockSpec((tm, 256), lambda i: (i, C_CQ // 256)),
                  pl.BlockSpec((tm, 128), lambda i: (i, C_CKV // 128)),
                  pl.BlockSpec((1, 256), const), pl.BlockSpec((1, 128), const),
                  pl.BlockSpec((256, 512), const), pl.BlockSpec((128, 768), const),
                  pl.BlockSpec((tm, 128), row), pl.BlockSpec((tm, 128), row)],
        out_specs=[pl.BlockSpec((tm, 512), row), pl.BlockSpec((tm, 768), row), pl.BlockSpec((tm, 256), row),
                   pl.BlockSpec((tm, 128), row), pl.BlockSpec((tm, 128), row),
                   pl.BlockSpec((1, 256), const), pl.BlockSpec((1, 128), const)],
        out_shape=[jax.ShapeDtypeStruct((T, 512), MXU), jax.ShapeDtypeStruct((T, 768), MXU),
                   jax.ShapeDtypeStruct((T, 256), F32), jax.ShapeDtypeStruct((T, 128), F32),
                   jax.ShapeDtypeStruct((T, 128), F32),
                   jax.ShapeDtypeStruct((1, 256), F32), jax.ShapeDtypeStruct((1, 128), F32)],
        compiler_params=_cparams(("arbitrary",)),
    )(dq, dk, dv, proj, proj, gq, gkv, wuq, wukv, cosr, sinr)


def _split3(x):
    hi = x.astype(MXU)
    r1 = x - hi.astype(F32)
    mid = r1.astype(MXU)
    lo = (r1 - mid.astype(F32)).astype(MXU)
    return hi, mid, lo


def _tri_matmul(tri, x):
    hi, mid, lo = _split3(x)
    d = lambda p: jnp.dot(tri, p, preferred_element_type=F32)
    return d(hi) + d(mid) + d(lo)


def _log_sigmoid(z):
    return jnp.minimum(z, 0.0) - jnp.log(1.0 + jnp.exp(-jnp.abs(z)))


def fox_gate(proj, fbias):
    T = proj.shape[0]
    tb = _tile(T, 512)

    def body(misc_ref, b_ref, fc_ref, fr_ref, frep_ref, carry):
        @pl.when(pl.program_id(0) == 0)
        def _():
            carry[...] = jnp.zeros_like(carry)

        lane = lax.broadcasted_iota(jnp.int32, (1, 128), 1)
        lf = jnp.where(lane < 4, _log_sigmoid(misc_ref[...] + b_ref[...]), 0.0)
        r = lax.broadcasted_iota(jnp.int32, (tb, tb), 0)
        c = lax.broadcasted_iota(jnp.int32, (tb, tb), 1)
        tri = jnp.where(r >= c, 1.0, 0.0).astype(MXU)
        F = _tri_matmul(tri, lf) + carry[...]
        carry[...] = carry[...] + jnp.sum(lf, axis=0, keepdims=True)
        fc_ref[0] = F
        fc_ref[1] = pltpu.roll(F, 126, 1)
        ft = F.T[0:8, :]
        fr_ref[0] = ft
        fr_ref[1] = pltpu.roll(ft, 6, 0)
        for h in range(4):
            frep_ref[h] = jnp.broadcast_to(_lane_pick(F, h), (tb, 128))

    return pl.pallas_call(
        body, name="fox_gate", grid=(T // tb,),
        in_specs=[pl.BlockSpec((tb, 128), lambda i: (i, C_MISC // 128)), pl.BlockSpec((1, 128), lambda i: (0, 0))],
        out_specs=[pl.BlockSpec((2, tb, 128), lambda i: (0, i, 0)), pl.BlockSpec((2, 8, tb), lambda i: (0, 0, i)),
                   pl.BlockSpec((4, tb, 128), lambda i: (0, i, 0))],
        out_shape=[jax.ShapeDtypeStruct((2, T, 128), F32), jax.ShapeDtypeStruct((2, 8, T), F32),
                   jax.ShapeDtypeStruct((4, T, 128), F32)],
        scratch_shapes=[pltpu.VMEM((1, 128), F32)],
        compiler_params=_cparams(("arbitrary",)),
    )(proj, fbias)


def fox_gate_bwd(dFq, dFk, proj, fbias):
    T = proj.shape[0]
    tb = _tile(T, 512)
    nb = T // tb

    def body(dq_ref, dk_ref, misc_ref, b_ref, dm_ref, db_ref, carry):
        first = pl.program_id(0) == 0

        @pl.when(first)
        def _():
            carry[...] = jnp.zeros_like(carry)

        lane = lax.broadcasted_iota(jnp.int32, (1, 128), 1)
        dF = jnp.where(lane < 4, (dq_ref[0] + dk_ref[0]) + pltpu.roll(dq_ref[1] + dk_ref[1], 2, 1), 0.0)
        r = lax.broadcasted_iota(jnp.int32, (tb, tb), 0)
        c = lax.broadcasted_iota(jnp.int32, (tb, tb), 1)
        tri = jnp.where(r <= c, 1.0, 0.0).astype(MXU)
        dlf = _tri_matmul(tri, dF) + carry[...]
        carry[...] = carry[...] + jnp.sum(dF, axis=0, keepdims=True)
        z = misc_ref[...] + b_ref[...]
        dz = jnp.where(lane < 4, dlf * (1.0 / (1.0 + jnp.exp(z))), 0.0)
        dm_ref[...] = dz
        _acc_out(db_ref, jnp.sum(dz, axis=0, keepdims=True), first)

    return pl.pallas_call(
        body, name="fox_gate_bwd", grid=(nb,),
        in_specs=[pl.BlockSpec((2, tb, 128), lambda i: (0, nb - 1 - i, 0)),
                  pl.BlockSpec((2, tb, 128), lambda i: (0, nb - 1 - i, 0)),
                  pl.BlockSpec((tb, 128), lambda i: (nb - 1 - i, C_MISC // 128)),
                  pl.BlockSpec((1, 128), lambda i: (0, 0))],
        out_specs=[pl.BlockSpec((tb, 128), lambda i: (nb - 1 - i, 0)), pl.BlockSpec((1, 128), lambda i: (0, 0))],
        out_shape=[jax.ShapeDtypeStruct((T, 128), F32), jax.ShapeDtypeStruct((1, 128), F32)],
        scratch_shapes=[pltpu.VMEM((1, 128), F32)],
        compiler_params=_cparams(("arbitrary",)),
    )(dFq, dFk, proj, fbias)


FLASH_TILE = 512


def _row_stat_tile(a, b, n):
    at = jnp.broadcast_to(a, (n, 128)).T[0:8, :]
    bt = jnp.broadcast_to(b, (n, 128)).T[0:8, :]
    sub = lax.broadcasted_iota(jnp.int32, (8, 1), 0)
    return jnp.where(sub == 0, at, jnp.where(sub == 1, bt, 0.0))


def _col_stat_tile(a, b):
    lane = lax.broadcasted_iota(jnp.int32, (1, 128), 1)
    return jnp.where(lane == 0, a, jnp.where(lane == 1, b, 0.0))


def _lane_pick(x, h):
    lane = lax.broadcasted_iota(jnp.int32, (1, 128), 1)
    return jnp.sum(jnp.where(lane == h, x, 0.0), axis=1, keepdims=True)


def _half_mask(h):
    lane = lax.broadcasted_iota(jnp.int32, (1, 128), 1)
    return (lane // HEAD_DIM) == h


def _call_hosting(body, name, grid, args, in_specs, out_specs, out_shape, scratch, exch):
    n_out = len(out_shape)
    if exch is not None:
        body, (xargs, xin, xout, xshape, xscratch) = hosted_exchange(
            body, len(args), n_out, len(scratch), grid, *exch)
        args, in_specs, out_specs = args + xargs, in_specs + xin, out_specs + xout
        out_shape, scratch = out_shape + xshape, scratch + xscratch
    res = pl.pallas_call(
        body, name=name, grid=grid, in_specs=in_specs, out_specs=out_specs, out_shape=out_shape,
        scratch_shapes=scratch, compiler_params=_cparams(("arbitrary",) * len(grid)),
    )(*args)
    return res[:n_out], res[n_out:]


def flash_fwd(q, k, v, frep, frow, *, qblk, kblk, vblk, nq, scale, name, exch=None):
    T = q.shape[0]
    tk = _tile(T, FLASH_TILE)
    tq = _tile(T, 2 * FLASH_TILE)
    per_q = tq // tk
    wq = 128 * nq
    has_f = frep is not None

    def body(*refs):
        if has_f:
            q_ref, k_ref, v_ref, fk_ref, fr_ref, o_ref, lr_ref, vT_sc, m_sc, acc_sc = refs
        else:
            q_ref, k_ref, v_ref, o_ref, lr_ref, vT_sc, m_sc, acc_sc = refs
        i = pl.program_id(1)

        @pl.when(i == 0)
        def _():
            vT_sc[...] = v_ref[...].astype(F32).T.astype(MXU)

        key_row = lax.broadcasted_iota(jnp.int32, (tk, 1), 0)
        q_col = lax.broadcasted_iota(jnp.int32, (1, tq), 1)
        row_half = lax.broadcasted_iota(jnp.int32, (128, 1), 0) // HEAD_DIM
        qb = q_ref[...].astype(F32) * scale
        if nq == 1:
            qhs = [jnp.where(_half_mask(h), qb, 0).astype(MXU) for h in range(2)]
        else:
            qhs = [qb[:, 128 * h:128 * (h + 1)].astype(MXU) for h in range(2)]
        for h in range(2):
            m_sc[h] = jnp.full((1, tq), NEG, F32)
            acc_sc[h] = jnp.zeros((128, tq), F32)

        def make_step(diag_block):
            def step(j, carry):
                off = pl.multiple_of(j * tk, tk)
                ks = k_ref[pl.ds(off, tk), :]
                vT = vT_sc[:, pl.ds(off, tk)]
                for h in range(2):
                    kh = ks if nq == 1 else ks[:, 128 * h:128 * (h + 1)]
                    sT = _dot_nt(kh, qhs[h])
                    if has_f:
                        fk = fk_ref[h, pl.ds(off, tk), :]
                        sT = sT + (fr_ref[0, h:h + 1, :] - jnp.concatenate([fk] * (tq // 128), axis=1))
                    if diag_block is not None:
                        sT = jnp.where(key_row + diag_block * tk <= q_col, sT, NEG)
                    m_prev = m_sc[h]
                    m_new = jnp.maximum(m_prev, jnp.max(sT, axis=0, keepdims=True))
                    alpha = jnp.exp(m_prev - m_new)
                    pT = jnp.exp(sT - m_new)
                    vTh = jnp.where(row_half == h, vT, jnp.ones_like(vT))
                    acc_sc[h] = alpha * acc_sc[h] + _dot(vTh, pT)
                    m_sc[h] = m_new
                return carry
            return step

        lax.fori_loop(0, per_q * i, make_step(None), 0)
        for d in range(per_q):
            make_step(d)(per_q * i + d, 0)
        outs, lses = [], []
        for h in range(2):
            acc = acc_sc[h]
            outs.append(acc / pltpu.roll(acc, HEAD_DIM, 0))
            l = acc_sc[h, HEAD_DIM * (1 - h):HEAD_DIM * (1 - h) + 1, :]
            lses.append(m_sc[h] + jnp.log(l))
        o_ref[...] = jnp.where(row_half == 0, outs[0], outs[1]).T
        sub = lax.broadcasted_iota(jnp.int32, (8, 1), 0)
        lr_ref[0] = jnp.where(sub == 0, lses[0], jnp.where(sub == 1, lses[1], 0.0))

    in_specs = [pl.BlockSpec((tq, wq), lambda p, i: (i, qblk + p)),
                pl.BlockSpec((T, wq), lambda p, i: (0, kblk + p)),
                pl.BlockSpec((T, 128), lambda p, i: (0, vblk + p))]
    args = [q, k, v]
    if has_f:
        in_specs += [pl.BlockSpec((2, T, 128), lambda p, i: (p, 0, 0)),
                     pl.BlockSpec((1, 8, tq), lambda p, i: (p, 0, i))]
        args += [frep, frow]
    out_specs = [pl.BlockSpec((tq, 128), lambda p, i: (i, p)), pl.BlockSpec((1, 8, tq), lambda p, i: (p, 0, i))]
    out_shape = [jax.ShapeDtypeStruct((T, 256), F32), jax.ShapeDtypeStruct((2, 8, T), F32)]
    scratch = [pltpu.VMEM((128, T), MXU), pltpu.VMEM((2, 1, tq), F32), pltpu.VMEM((2, 128, tq), F32)]
    return _call_hosting(body, name, (2, T // tq), args, in_specs, out_specs, out_shape, scratch, exch)


def flash_bwd(q, k, v, do, o, lrow, fcol, frow, *, qblk, kblk, vblk, nq, scale, name, exch=None):
    T = q.shape[0]
    tq = tk = _tile(T, FLASH_TILE)
    wq = 128 * nq
    nqb = T // tq
    has_f = fcol is not None

    def body(*refs):
        if has_f:
            (q_ref, k_ref, v_ref, do_ref, o_ref, lr_ref, fc_ref, fr_ref,
             dq_ref, dk_ref, dv_ref, df_ref, dfq_ref, dk_sc, dv_sc, dqT_sc, d_sc, df_sc, dfq_sc) = refs
        else:
            q_ref, k_ref, v_ref, do_ref, o_ref, lr_ref, dq_ref, dk_ref, dv_ref, dk_sc, dv_sc, dqT_sc, d_sc = refs
        j = pl.program_id(1)
        diag = lax.broadcasted_iota(jnp.int32, (tk, 1), 0) <= lax.broadcasted_iota(jnp.int32, (1, tq), 1)
        hms = [_half_mask(h) for h in range(2)]

        @pl.when(j == 0)
        def _():
            dqT_sc[...] = jnp.zeros_like(dqT_sc)
            if has_f:
                dfq_sc[...] = jnp.zeros_like(dfq_sc)

            def delta(b, carry):
                off = pl.multiple_of(b * tq, tq)
                prod = do_ref[pl.ds(off, tq), :] * o_ref[pl.ds(off, tq), :]
                Ds = [jnp.sum(jnp.where(hms[h], prod, 0.0), axis=1, keepdims=True) for h in range(2)]
                d_sc[:, pl.ds(off, tq)] = _row_stat_tile(Ds[0], Ds[1], tq)
                return carry

            lax.fori_loop(0, nqb, delta, 0)

        kb = k_ref[...]
        vb = v_ref[...]
        if nq == 1:
            khs = [jnp.where(hms[h], kb, 0).astype(MXU) for h in range(2)]
        else:
            khs = [kb[:, 128 * h:128 * (h + 1)].astype(MXU) for h in range(2)]
        kTs = [kh.astype(F32).T.astype(MXU) for kh in khs]
        kss = [(kh.astype(F32) * scale).astype(MXU) for kh in khs]
        vhs = [jnp.where(hms[h], vb, 0).astype(MXU) for h in range(2)]
        fks = [_lane_pick(fc_ref[0], h) for h in range(2)] if has_f else None
        dv_sc[...] = jnp.zeros_like(dv_sc)
        dk_sc[...] = jnp.zeros_like(dk_sc)
        if has_f:
            df_sc[...] = jnp.zeros_like(df_sc)

        def make_step(masked):
            def step(i, carry):
                off = pl.multiple_of(i * tq, tq)
                qs = q_ref[pl.ds(off, tq), :]
                dos = do_ref[pl.ds(off, tq), :]
                for h in range(2):
                    qh = qs if nq == 1 else qs[:, 128 * h:128 * (h + 1)]
                    sT = _dot_nt(kss[h], qh)
                    if has_f:
                        sT = sT + (fr_ref[0, h:h + 1, pl.ds(off, tq)] - fks[h])
                    pT = jnp.exp(sT - lr_ref[0, h:h + 1, pl.ds(off, tq)])
                    if masked:
                        pT = jnp.where(diag, pT, 0.0)
                    dsT = pT * (_dot_nt(vhs[h], dos) - d_sc[h:h + 1, pl.ds(off, tq)])
                    dv_sc[...] += _dot(pT, jnp.where(hms[h], dos, 0))
                    qq = jnp.where(hms[h], qs, 0) if nq == 1 else qh
                    dk_sc[h if nq == 2 else 0] += _dot(dsT, qq)
                    dqT_sc[h if nq == 2 else 0, :, pl.ds(off, tq)] += _dot(kTs[h], dsT)
                    if has_f:
                        part = dsT[:, 0:128]
                        for c in range(1, tq // 128):
                            part = part + dsT[:, 128 * c:128 * (c + 1)]
                        df_sc[h] += part
                        dfq_sc[h:h + 1, pl.ds(off, tq)] += jnp.sum(dsT, axis=0, keepdims=True)
                return carry
            return step

        make_step(True)(j, 0)
        lax.fori_loop(j + 1, nqb, make_step(False), 0)
        if nq == 1:
            dk_ref[...] = dk_sc[0] * scale
        else:
            dk_ref[...] = jnp.concatenate([dk_sc[0], dk_sc[1]], axis=1) * scale
        dv_ref[...] = dv_sc[...]
        if has_f:
            df_ref[0] = _col_stat_tile(-jnp.sum(df_sc[0], axis=1, keepdims=True),
                                       -jnp.sum(df_sc[1], axis=1, keepdims=True))

        @pl.when(j == nqb - 1)
        def _():
            if nq == 1:
                dq_ref[...] = dqT_sc[0].T * scale
            else:
                dq_ref[...] = jnp.concatenate([dqT_sc[0].T, dqT_sc[1].T], axis=1) * scale
            if has_f:
                sub = lax.broadcasted_iota(jnp.int32, (128, 1), 0)
                rows = jnp.where(sub == 0, dfq_sc[0:1, :], jnp.where(sub == 1, dfq_sc[1:2, :], 0.0))
                dfq_ref[0] = rows.T

    in_specs = [pl.BlockSpec((T, wq), lambda p, j: (0, qblk + p)),
                pl.BlockSpec((tk, wq), lambda p, j: (j, kblk + p)),
                pl.BlockSpec((tk, 128), lambda p, j: (j, vblk + p)),
                pl.BlockSpec((T, 128), lambda p, j: (0, p)),
                pl.BlockSpec((T, 128), lambda p, j: (0, p)),
                pl.BlockSpec((1, 8, T), lambda p, j: (p, 0, 0))]
    args = [q, k, v, do, o, lrow]
    out_specs = [pl.BlockSpec((T, wq), lambda p, j: (0, p)),
                 pl.BlockSpec((tk, wq), lambda p, j: (j, p)), pl.BlockSpec((tk, 128), lambda p, j: (j, p))]
    out_shape = [jax.ShapeDtypeStruct((T, 2 * wq), F32), jax.ShapeDtypeStruct((T, 2 * wq), F32),
                 jax.ShapeDtypeStruct((T, 256), F32)]
    scratch = [pltpu.VMEM((nq, tk, 128), F32), pltpu.VMEM((tk, 128), F32), pltpu.VMEM((nq, 128, T), F32),
               pltpu.VMEM((8, T), F32)]
    if has_f:
        in_specs += [pl.BlockSpec((1, tk, 128), lambda p, j: (p, j, 0)),
                     pl.BlockSpec((1, 8, T), lambda p, j: (p, 0, 0))]
        args += [fcol, frow]
        out_specs += [pl.BlockSpec((1, tk, 128), lambda p, j: (p, j, 0)),
                      pl.BlockSpec((1, T, 128), lambda p, j: (p, 0, 0))]
        out_shape += [jax.ShapeDtypeStruct((2, T, 128), F32), jax.ShapeDtypeStruct((2, T, 128), F32)]
        scratch += [pltpu.VMEM((2, tk, 128), F32), pltpu.VMEM((8, T), F32)]
    return _call_hosting(body, name, (2, T // tk), args, in_specs, out_specs, out_shape, scratch, exch)


def _swa_align(pair, e, h):
    sel = jnp.where(_half_mask(e), pair, 0.0)
    if e == h:
        return sel
    return pltpu.roll(sel, HEAD_DIM, 1)


def _swa_mask(n):
    W = WINDOW
    qi = lax.broadcasted_iota(jnp.int32, (W, 2 * W), 0) + W
    kj = lax.broadcasted_iota(jnp.int32, (W, 2 * W), 1)
    dist = qi - kj
    return (dist >= 0) & (dist < W) & ((n > 0) | (kj >= W))


def swa_fwd(proj, bias, sinks, exch=None):
    T = proj.shape[0]
    W = WINDOW
    nb = T // W
    scale = HEAD_DIM ** -0.5

    def body(sink_ref, q_ref, kp_ref, kc_ref, vp_ref, vc_ref, b_ref, o_ref, l_ref):
        n = pl.program_id(0)
        mask = _swa_mask(n)
        kband = jnp.concatenate([kp_ref[...], kc_ref[...]], axis=0).astype(MXU)
        vband = jnp.concatenate([vp_ref[...], vc_ref[...]], axis=0).astype(MXU)
        lane = lax.broadcasted_iota(jnp.int32, (1, 128), 1)
        lse_tile = jnp.zeros((W, 128), F32)
        pairs = []
        for h in range(2):
            full = []
            for g in range(4):
                hq = 4 * h + g
                qa = _swa_align(q_ref[:, 128 * (hq // 2):128 * (hq // 2 + 1)], hq % 2, h)
                s = _dot_nt(qa, kband) * scale + b_ref[hq]
                s = jnp.where(mask, s, NEG)
                sink = sink_ref[hq]
                m = jnp.maximum(jnp.max(s, axis=1, keepdims=True), sink)
                e = jnp.exp(s - m)
                l = jnp.sum(e, axis=1, keepdims=True) + jnp.exp(sink - m)
                r = jnp.where(_half_mask(h), _dot(e, vband), 0.0) / l
                full.append(r + pltpu.roll(r, HEAD_DIM, 1))
                lse_tile = jnp.where(lane == hq, m + jnp.log(l), lse_tile)
            pairs.append(jnp.where(_half_mask(0), full[0], full[1]))
            pairs.append(jnp.where(_half_mask(0), full[2], full[3]))
        o_ref[...] = jnp.concatenate(pairs, axis=1)
        l_ref[...] = lse_tile

    prev = lambda n: (jnp.maximum(n - 1, 0), C_KA // 128)
    cur = lambda n: (n, C_KA // 128)
    prev_v = lambda n: (jnp.maximum(n - 1, 0), C_VA // 128)
    cur_v = lambda n: (n, C_VA // 128)
    return _call_hosting(
        body, "swa_fwd", (nb,), [sinks, proj, proj, proj, proj, proj, bias],
        [pl.BlockSpec(memory_space=pltpu.SMEM),
         pl.BlockSpec((W, 512), lambda n: (n, 0)),
         pl.BlockSpec((W, 128), prev), pl.BlockSpec((W, 128), cur),
         pl.BlockSpec((W, 128), prev_v), pl.BlockSpec((W, 128), cur_v),
         pl.BlockSpec((8, W, 2 * W), lambda n: (0, 0, 0))],
        [pl.BlockSpec((W, 512), lambda n: (n, 0)), pl.BlockSpec((W, 128), lambda n: (n, 0))],
        [jax.ShapeDtypeStruct((T, 512), F32), jax.ShapeDtypeStruct((T, 128), F32)], [], exch)


def swa_bwd(proj, bias, sinks, do, o, lse, exch=None):
    T = proj.shape[0]
    W = WINDOW
    nb = T // W
    scale = HEAD_DIM ** -0.5

    def body(sink_ref, q_ref, kp_ref, kc_ref, vp_ref, vc_ref, b_ref, do_ref, o_ref, l_ref,
             dq_ref, dk_ref, dv_ref, db_ref, dsk_ref, ck, cv):
        n = pl.program_id(0)

        @pl.when(n == 0)
        def _():
            ck[...] = jnp.zeros_like(ck)
            cv[...] = jnp.zeros_like(cv)
            db_ref[...] = jnp.zeros_like(db_ref)
            dsk_ref[...] = jnp.zeros_like(dsk_ref)

        @pl.when(n < nb)
        def _():
            mask = _swa_mask(n)
            kb32 = jnp.concatenate([kp_ref[...], kc_ref[...]], axis=0)
            vb32 = jnp.concatenate([vp_ref[...], vc_ref[...]], axis=0)
            kband = kb32.astype(MXU)
            sub = lax.broadcasted_iota(jnp.int32, (8, 1), 0)
            dk_band = jnp.zeros((2 * W, 128), F32)
            dv_band = jnp.zeros((2 * W, 128), F32)
            dsk = jnp.zeros((8, 128), F32)
            dq_pairs = []
            mask4 = jnp.concatenate([mask] * 4, axis=0)
            for h in range(2):
                hm = _half_mask(h)
                km = jnp.where(hm, kb32, 0.0).astype(MXU)
                vm = jnp.where(hm, vb32, 0.0).astype(MXU)
                pbs = [slice(128 * ((4 * h + g) // 2), 128 * ((4 * h + g) // 2 + 1)) for g in range(4)]
                q4 = jnp.concatenate([_swa_align(q_ref[:, pbs[g]], g % 2, h) for g in range(4)], axis=0)
                do4 = jnp.concatenate([_swa_align(do_ref[:, pbs[g]], g % 2, h) for g in range(4)], axis=0)
                D4 = jnp.concatenate(
                    [jnp.sum(jnp.where(_half_mask(g % 2), do_ref[:, pbs[g]] * o_ref[:, pbs[g]], 0.0), axis=1,
                             keepdims=True) for g in range(4)], axis=0)
                lse4 = jnp.concatenate([_lane_pick(l_ref[...], 4 * h + g) for g in range(4)], axis=0)
                sink4 = jnp.concatenate([jnp.full((W, 1), sink_ref[4 * h + g], F32) for g in range(4)], axis=0)
                s = _dot_nt(q4, kband) * scale + b_ref[4 * h:4 * h + 4].reshape(4 * W, 2 * W)
                p = jnp.where(mask4, jnp.exp(s - lse4), 0.0)
                sd = jnp.exp(sink4 - lse4) * D4
                for g in range(4):
                    dsk = dsk + jnp.where(sub == 4 * h + g,
                                          -jnp.sum(sd[W * g:W * (g + 1)], axis=0, keepdims=True), 0.0)
                ds = p * (_dot_nt(do4, vm) - D4)
                db_ref[4 * h:4 * h + 4] += ds.reshape(4, W, 2 * W)
                dq = _dot(ds, km) * scale
                dq = dq + pltpu.roll(dq, HEAD_DIM, 1)
                dk_band = dk_band + _dot(ds.T, q4) * scale
                dv_band = dv_band + _dot(p.T, do4)
                dq_pairs.append(jnp.where(_half_mask(0), dq[0:W], dq[W:2 * W]))
                dq_pairs.append(jnp.where(_half_mask(0), dq[2 * W:3 * W], dq[3 * W:4 * W]))
            dq_ref[...] = jnp.concatenate(dq_pairs, axis=1)
            dsk_ref[...] += dsk
            dk_ref[...] = ck[...] + dk_band[0:W]
            dv_ref[...] = cv[...] + dv_band[0:W]
            ck[...] = dk_band[W:2 * W]
            cv[...] = dv_band[W:2 * W]

        @pl.when(n == nb)
        def _():
            dk_ref[...] = ck[...]
            dv_ref[...] = cv[...]

    cl = lambda n: jnp.minimum(n, nb - 1)
    pv = lambda n: jnp.maximum(jnp.minimum(n, nb - 1) - 1, 0)
    return _call_hosting(
        body, "swa_bwd", (nb + 1,), [sinks, proj, proj, proj, proj, proj, bias, do, o, lse],
        [pl.BlockSpec(memory_space=pltpu.SMEM),
         pl.BlockSpec((W, 512), lambda n: (cl(n), 0)),
         pl.BlockSpec((W, 128), lambda n: (pv(n), C_KA // 128)),
         pl.BlockSpec((W, 128), lambda n: (cl(n), C_KA // 128)),
         pl.BlockSpec((W, 128), lambda n: (pv(n), C_VA // 128)),
         pl.BlockSpec((W, 128), lambda n: (cl(n), C_VA // 128)),
         pl.BlockSpec((8, W, 2 * W), lambda n: (0, 0, 0)),
         pl.BlockSpec((W, 512), lambda n: (cl(n), 0)),
         pl.BlockSpec((W, 512), lambda n: (cl(n), 0)),
         pl.BlockSpec((W, 128), lambda n: (cl(n), 0))],
        [pl.BlockSpec((W, 512), lambda n: (cl(n), 0)),
         pl.BlockSpec((W, 128), lambda n: (jnp.maximum(n - 1, 0), 0)),
         pl.BlockSpec((W, 128), lambda n: (jnp.maximum(n - 1, 0), 0)),
         pl.BlockSpec((8, W, 2 * W), lambda n: (0, 0, 0)),
         pl.BlockSpec((8, 128), lambda n: (0, 0))],
        [jax.ShapeDtypeStruct((T, 512), F32), jax.ShapeDtypeStruct((T, 128), F32),
         jax.ShapeDtypeStruct((T, 128), F32), jax.ShapeDtypeStruct((8, W, 2 * W), F32),
         jax.ShapeDtypeStruct((8, 128), F32)],
        [pltpu.VMEM((W, 128), F32), pltpu.VMEM((W, 128), F32)], exch)


def swa_bias_table(rel_bias):
    W = WINDOW
    qi = jnp.arange(W, dtype=jnp.int32)[:, None] + W
    kj = jnp.arange(2 * W, dtype=jnp.int32)[None, :]
    dist = qi - kj
    max_exact = REL_BUCKETS // 2
    d = jnp.maximum(dist, 0)
    log_ratio = jnp.log(jnp.maximum(d, 1).astype(F32) / max_exact) / math.log(REL_MAX_DIST / max_exact)
    large = jnp.minimum(max_exact + (log_ratio * (REL_BUCKETS - max_exact)).astype(jnp.int32), REL_BUCKETS - 1)
    bucket = jnp.where(d < max_exact, d, large)
    bucket = bucket.reshape(-1)
    onehot = (bucket[None, :] == jnp.arange(REL_BUCKETS, dtype=jnp.int32)[:, None]).astype(F32)
    bias = jnp.dot(rel_bias.astype(F32).T, onehot, precision=lax.Precision.HIGHEST)
    return bias.reshape(SWA_Q_HEADS, W, 2 * W), bucket


def attn_out(oa, ob, oc, gn, wout, gpost, x):
    T = x.shape[0]
    tm = _tile(T, 512)

    def body(oa_ref, ob_ref, oc_ref, gn_ref, w_ref, gp_ref, x_ref, x2_ref, y_ref, mT_ref):
        g = gn_ref[...]
        mixed = jnp.concatenate([_rms_fwd(oa_ref[...], g[:, 0:512]), _rms_fwd(ob_ref[...], g[:, 512:768]),
                                 _rms_fwd(oc_ref[...], g[:, 768:1024])], axis=1)
        mT_ref[...] = mixed.T.astype(MXU)
        y = _dot(mixed, w_ref[...])
        y_ref[...] = y
        x2_ref[...] = x_ref[...] + _rms_fwd(y, gp_ref[...])

    row = lambda i: (i, 0)
    const = lambda i: (0, 0)
    return pl.pallas_call(
        body, name="attn_out", grid=(T // tm,),
        in_specs=[pl.BlockSpec((tm, 512), row), pl.BlockSpec((tm, 256), row), pl.BlockSpec((tm, 256), row),
                  pl.BlockSpec((1, 1024), const), pl.BlockSpec((1024, 1024), const), pl.BlockSpec((1, 1024), const),
                  pl.BlockSpec((tm, 1024), row)],
        out_specs=[pl.BlockSpec((tm, 1024), row), pl.BlockSpec((tm, 1024), row),
                   pl.BlockSpec((1024, tm), lambda i: (0, i))],
        out_shape=[jax.ShapeDtypeStruct((T, 1024), F32), jax.ShapeDtypeStruct((T, 1024), F32),
                   jax.ShapeDtypeStruct((1024, T), MXU)],
        compiler_params=_cparams(("parallel",)),
    )(oa, ob, oc, gn, wout, gpost, x)


def attn_out_bwd(dx2, y, oa, ob, oc, gn, wout, gpost):
    T = dx2.shape[0]
    tm = _tile(T, 512)

    def body(dx_ref, y_ref, oa_ref, ob_ref, oc_ref, gn_ref, w_ref, gp_ref,
             dy_ref, da_ref, db_ref, dc_ref, dgn_ref, dgp_ref):
        first = pl.program_id(0) == 0
        dy, dgp = _rms_bwd(dx_ref[...], y_ref[...], gp_ref[...])
        dy_ref[...] = dy.astype(MXU)
        _acc_out(dgp_ref, dgp, first)
        dm = _dot_nt(dy, w_ref[...])
        g = gn_ref[...]
        da, dga = _rms_bwd(dm[:, 0:512], oa_ref[...], g[:, 0:512])
        db, dgb = _rms_bwd(dm[:, 512:768], ob_ref[...], g[:, 512:768])
        dc, dgc = _rms_bwd(dm[:, 768:1024], oc_ref[...], g[:, 768:1024])
        da_ref[...] = da
        db_ref[...] = db
        dc_ref[...] = dc
        _acc_out(dgn_ref, jnp.concatenate([dga, dgb, dgc], axis=1), first)

    row = lambda i: (i, 0)
    const = lambda i: (0, 0)
    return pl.pallas_call(
        body, name="attn_out_bwd", grid=(T // tm,),
        in_specs=[pl.BlockSpec((tm, 1024), row), pl.BlockSpec((tm, 1024), row),
                  pl.BlockSpec((tm, 512), row), pl.BlockSpec((tm, 256), row), pl.BlockSpec((tm, 256), row),
                  pl.BlockSpec((1, 1024), const), pl.BlockSpec((1024, 1024), const), pl.BlockSpec((1, 1024), const)],
        out_specs=[pl.BlockSpec((tm, 1024), row), pl.BlockSpec((tm, 512), row), pl.BlockSpec((tm, 256), row),
                   pl.BlockSpec((tm, 256), row), pl.BlockSpec((1, 1024), const), pl.BlockSpec((1, 1024), const)],
        out_shape=[jax.ShapeDtypeStruct((T, 1024), MXU), jax.ShapeDtypeStruct((T, 512), F32),
                   jax.ShapeDtypeStruct((T, 256), F32), jax.ShapeDtypeStruct((T, 256), F32),
                   jax.ShapeDtypeStruct((1, 1024), F32), jax.ShapeDtypeStruct((1, 1024), F32)],
        compiler_params=_cparams(("arbitrary",)),
    )(dx2, y, oa, ob, oc, gn, wout, gpost)


FF_TILE = 256
_GELU_C = math.sqrt(2.0 / math.pi)


def _gelu(x):
    return 0.5 * x * (1.0 + jnp.tanh(_GELU_C * (x + 0.044715 * x * x * x)))


def _gelu_with_grad(x):
    x2 = x * x
    t = jnp.tanh(_GELU_C * x * (1.0 + 0.044715 * x2))
    h = 0.5 * (1.0 + t)
    return x * h, h + (0.5 * _GELU_C) * x * (1.0 - t * t) * (1.0 + (3 * 0.044715) * x2)


def _conv_taps(u, hal_ref, first):
    row = lax.broadcasted_iota(jnp.int32, (8, 1), 0)
    h6 = jnp.where(first, 0.0, hal_ref[6:7, :])
    h7 = jnp.where(first, 0.0, hal_ref[7:8, :])
    r1, r2 = pltpu.roll(u, 1, 0), pltpu.roll(u, 2, 0)
    r1 = jnp.concatenate([jnp.where(row == 0, h7, r1[0:8]), r1[8:]], axis=0)
    r2 = jnp.concatenate([jnp.where(row == 0, h6, jnp.where(row == 1, h7, r2[0:8])), r2[8:]], axis=0)
    return r1, r2


def ffn_fwd(u0, convw, convb, wdown, gpost, x2, exch=None):
    T = x2.shape[0]
    tm, tn = _tile(T, 1024), FF_TILE
    nj = D_FF // tn

    def body(ug_ref, uu_ref, hg_ref, hu_ref, wg_ref, wu_ref, bg_ref, bu_ref, wd_ref, gp_ref, x_ref, wdp_ref,
             x3_ref, y_ref, aT_ref, acc, a_sc):
        i, j = pl.program_id(0), pl.program_id(1)
        first = i == 0

        @pl.when(j == 0)
        def _():
            acc[...] = jnp.zeros_like(acc)
            a_sc[...] = jnp.zeros_like(a_sc)

        acc[...] += _dot(a_sc[...], wdp_ref[...])

        def conv(u_ref, h_ref, w_ref, b_ref):
            u = u_ref[...]
            r1, r2 = _conv_taps(u, h_ref, first)
            return b_ref[...] + w_ref[0:1, :] * r2 + w_ref[1:2, :] * r1 + w_ref[2:3, :] * u

        a = _gelu(conv(ug_ref, hg_ref, wg_ref, bg_ref)) * conv(uu_ref, hu_ref, wu_ref, bu_ref)
        aT_ref[...] = a.T.astype(MXU)
        a_sc[...] = a.astype(MXU)

        @pl.when(j == nj - 1)
        def _():
            y = acc[...] + _dot(a_sc[...], wd_ref[...])
            y_ref[...] = y
            x3_ref[...] = x_ref[...] + _rms_fwd(y, gp_ref[...])

    halo = lambda off: (lambda i, j: (jnp.maximum(i * (tm // 8) - 1, 0), off + j))
    return _call_hosting(
        body, "ffn_fwd", (T // tm, nj), [u0, u0, u0, u0, convw, convw, convb, convb, wdown, gpost, x2, wdown],
        [pl.BlockSpec((tm, tn), lambda i, j: (i, j)), pl.BlockSpec((tm, tn), lambda i, j: (i, nj + j)),
         pl.BlockSpec((8, tn), halo(0)), pl.BlockSpec((8, tn), halo(nj)),
         pl.BlockSpec((3, tn), lambda i, j: (0, j)), pl.BlockSpec((3, tn), lambda i, j: (0, nj + j)),
         pl.BlockSpec((1, tn), lambda i, j: (0, j)), pl.BlockSpec((1, tn), lambda i, j: (0, nj + j)),
         pl.BlockSpec((tn, 1024), lambda i, j: (j, 0)),
         pl.BlockSpec((1, 1024), lambda i, j: (0, 0)),
         pl.BlockSpec((tm, 1024), lambda i, j: (i, 0)),
         pl.BlockSpec((tn, 1024), lambda i, j: (jnp.maximum(j - 1, 0), 0))],
        [pl.BlockSpec((tm, 1024), lambda i, j: (i, 0)), pl.BlockSpec((tm, 1024), lambda i, j: (i, 0)),
         pl.BlockSpec((tn, tm), lambda i, j: (j, i))],
        [jax.ShapeDtypeStruct((T, 1024), F32), jax.ShapeDtypeStruct((T, 1024), F32),
         jax.ShapeDtypeStruct((D_FF, T), MXU)],
        [pltpu.VMEM((tm, 1024), F32), pltpu.VMEM((tm, tn), MXU)], exch)


def ffn_bwd(dx3, y, u0, convw, convb, wdown, gpost, wupT, x2, gfpre, exch=None):
    T = dx3.shape[0]
    tm, tn = _tile(T, 512), FF_TILE
    nj = D_FF // tn
    ni = T // tm

    def body(dx_ref, y_ref, ug_ref, uu_ref, hg_ref, hu_ref, wg_ref, wu_ref, bg_ref, bu_ref, wd_ref, gp_ref,
             wtg_ref, wtu_ref, x2_ref, gf_ref, wdn_ref, wtgp_ref, wtup_ref,
             dy_ref, dug_ref, duu_ref, dcg_ref, dcu_ref, dgp_ref, dx2_ref, dgf_ref,
             dy_sc, dh_sc, da_sc, dug_sc, duu_sc, cg, cu, ag, au):
        s, j = pl.program_id(0), pl.program_id(1)
        i = ni - 1 - s
        first_tok = i == 0
        sub = lax.broadcasted_iota(jnp.int32, (8, 1), 0)
        slot = j % 2

        @pl.when(j == 0)
        def _():
            dy, dgp = _rms_bwd(dx_ref[...], y_ref[...], gp_ref[...])
            dy_sc[...] = dy.astype(MXU)
            dy_ref[...] = dy.astype(MXU)
            _acc_out(dgp_ref, dgp, s == 0)
            dh_sc[...] = jnp.zeros_like(dh_sc)
            da_sc[0] = _dot_nt(dy.astype(MXU), wd_ref[...])
            dug_sc[...] = jnp.zeros_like(dug_sc)
            duu_sc[...] = jnp.zeros_like(duu_sc)

        @pl.when(s == 0)
        def _():
            cg[j] = jnp.zeros((8, tn), F32)
            cu[j] = jnp.zeros((8, tn), F32)
            ag[j] = jnp.zeros((8, tn), F32)
            au[j] = jnp.zeros((8, tn), F32)

        da = da_sc[slot]
        da_sc[1 - slot] = _dot_nt(dy_sc[...], wdn_ref[...])
        dh_sc[...] += _dot(dug_sc[...], wtgp_ref[...]) + _dot(duu_sc[...], wtup_ref[...])

        def conv(u_ref, h_ref, w_ref, b_ref):
            u = u_ref[...]
            r1, r2 = _conv_taps(u, h_ref, first_tok)
            return b_ref[...] + w_ref[0:1, :] * r2 + w_ref[1:2, :] * r1 + w_ref[2:3, :] * u, u, r1, r2

        gate, ugv, g1, g2 = conv(ug_ref, hg_ref, wg_ref, bg_ref)
        up, uuv, u1, u2 = conv(uu_ref, hu_ref, wu_ref, bu_ref)
        gl, dgl = _gelu_with_grad(gate)
        dup = da * gl
        dgate = da * up * dgl

        def conv_bwd(du, u, r1, r2, w_ref, c_ref, a_ref, duT_ref, du_sc):
            nxt = c_ref[j]
            n0, n1 = nxt[0:1, :], nxt[1:2, :]
            f1, f2 = pltpu.roll(du, tm - 1, 0), pltpu.roll(du, tm - 2, 0)
            f1 = jnp.concatenate([f1[:tm - 8], jnp.where(sub == 7, n0, f1[tm - 8:])], axis=0)
            f2 = jnp.concatenate([f2[:tm - 8], jnp.where(sub == 7, n1, jnp.where(sub == 6, n0, f2[tm - 8:]))], axis=0)
            du0 = w_ref[2:3, :] * du + w_ref[1:2, :] * f1 + w_ref[0:1, :] * f2
            duT_ref[...] = du0.T.astype(MXU)
            du_sc[...] = du0.astype(MXU)
            c_ref[j] = du[0:8, :]
            red = lambda v: jnp.sum(v, axis=0, keepdims=True)
            part = jnp.where(sub == 0, red(du * r2), jnp.where(sub == 1, red(du * r1), jnp.where(
                sub == 2, red(du * u), jnp.where(sub == 3, red(du), 0.0))))
            a_ref[j] = a_ref[j] + part
            return a_ref[j]

        dcg_ref[0] = conv_bwd(dgate, ugv, g1, g2, wg_ref, cg, ag, dug_ref, dug_sc)
        dcu_ref[0] = conv_bwd(dup, uuv, u1, u2, wu_ref, cu, au, duu_ref, duu_sc)

        @pl.when(j == nj - 1)
        def _():
            dh = dh_sc[...] + _dot(dug_sc[...], wtg_ref[...]) + _dot(duu_sc[...], wtu_ref[...])
            dx, dgf = _rms_bwd(dh, x2_ref[...], gf_ref[...])
            dx2_ref[...] = dx_ref[...] + dx
            _acc_out(dgf_ref, dgf, s == 0)

    rev = lambda s: ni - 1 - s
    halo = lambda off: (lambda s, j: (jnp.maximum(rev(s) * (tm // 8) - 1, 0), off + j))
    tok = pl.BlockSpec((tm, 1024), lambda s, j: (rev(s), 0))
    vec = pl.BlockSpec((1, 1024), lambda s, j: (0, 0))
    return _call_hosting(
        body, "ffn_bwd", (ni, nj),
        [dx3, y, u0, u0, u0, u0, convw, convw, convb, convb, wdown, gpost, wupT, wupT, x2, gfpre,
         wdown, wupT, wupT],
        [tok, tok,
         pl.BlockSpec((tm, tn), lambda s, j: (rev(s), j)), pl.BlockSpec((tm, tn), lambda s, j: (rev(s), nj + j)),
         pl.BlockSpec((8, tn), halo(0)), pl.BlockSpec((8, tn), halo(nj)),
         pl.BlockSpec((3, tn), lambda s, j: (0, j)), pl.BlockSpec((3, tn), lambda s, j: (0, nj + j)),
         pl.BlockSpec((1, tn), lambda s, j: (0, j)), pl.BlockSpec((1, tn), lambda s, j: (0, nj + j)),
         pl.BlockSpec((tn, 1024), lambda s, j: (j, 0)), vec,
         pl.BlockSpec((tn, 1024), lambda s, j: (j, 0)), pl.BlockSpec((tn, 1024), lambda s, j: (nj + j, 0)),
         tok, vec,
         pl.BlockSpec((tn, 1024), lambda s, j: (jnp.minimum(j + 1, nj - 1), 0)),
         pl.BlockSpec((tn, 1024), lambda s, j: (jnp.maximum(j - 1, 0), 0)),
         pl.BlockSpec((tn, 1024), lambda s, j: (nj + jnp.maximum(j - 1, 0), 0))],
        [tok,
         pl.BlockSpec((tn, tm), lambda s, j: (j, rev(s))), pl.BlockSpec((tn, tm), lambda s, j: (j, rev(s))),
         pl.BlockSpec((1, 8, tn), lambda s, j: (s, 0, j)), pl.BlockSpec((1, 8, tn), lambda s, j: (s, 0, j)),
         vec, tok, vec],
        [jax.ShapeDtypeStruct((T, 1024), MXU), jax.ShapeDtypeStruct((D_FF, T), MXU),
         jax.ShapeDtypeStruct((D_FF, T), MXU),
         jax.ShapeDtypeStruct((ni, 8, D_FF), F32), jax.ShapeDtypeStruct((ni, 8, D_FF), F32),
         jax.ShapeDtypeStruct((1, 1024), F32), jax.ShapeDtypeStruct((T, 1024), F32),
         jax.ShapeDtypeStruct((1, 1024), F32)],
        [pltpu.VMEM((tm, 1024), MXU), pltpu.VMEM((tm, 1024), F32), pltpu.VMEM((2, tm, tn), F32),
         pltpu.VMEM((tm, tn), MXU), pltpu.VMEM((tm, tn), MXU)] + [pltpu.VMEM((nj, 8, tn), F32)] * 4, exch)


ELEMS_PER_BLOCK = 512 * 1024


def _row_block(R, C):
    if R * C <= ELEMS_PER_BLOCK or R % 8:
        return R
    best = 8
    for t in range(8, R + 1, 8):
        if R % t == 0 and t * C <= ELEMS_PER_BLOCK:
            best = t
    return best


def adamw(w, g, m, v, name, exch=None):
    L, R, C = w.shape
    partials = isinstance(g, (list, tuple))
    tr = _row_block(R, 2 * C)
    c1 = 1.0 - ADAM_B1 ** ADAM_STEP
    c2 = 1.0 - ADAM_B2 ** ADAM_STEP

    def body(w_ref, *rest):
        g_refs, (m_ref, v_ref, g_out, d_ref, nm_ref, nv_ref) = rest[:-6], rest[-6:]

        def step(gv):
            g_out[0] = gv
            nm = ADAM_B1 * m_ref[0] + (1.0 - ADAM_B1) * gv
            nv = ADAM_B2 * v_ref[0] + (1.0 - ADAM_B2) * (gv * gv)
            nm_ref[0] = nm
            nv_ref[0] = nv
            d_ref[0] = -ADAM_LR * ((nm / c1) / (jnp.sqrt(nv / c2) + ADAM_EPS) + ADAM_WD * w_ref[0])

        if not partials:
            step(g_refs[0][0])
            return
        for k in range(L):
            @pl.when(pl.program_id(0) == k)
            def _(k=k):
                gv = g_refs[k][0].astype(F32)
                for d in range(1, N_DEV):
                    gv = gv + g_refs[k][d].astype(F32)
                step(gv)

    spec = pl.BlockSpec((1, tr, C), lambda l, i: (l, i, 0))
    if partials:
        gspecs = [pl.BlockSpec((N_DEV, tr, C), lambda l, i, k=k: (0, jnp.where(l == k, i, 0), 0)) for k in range(L)]
        gs = list(g)
    else:
        gspecs, gs = [spec], [g]
    return _call_hosting(body, name, (L, R // tr), [w] + gs + [m, v], [spec] + gspecs + [spec, spec], [spec] * 4,
                         [jax.ShapeDtypeStruct((L, R, C), F32)] * 4, [], exch)


def sum_devices(buf, name):
    _, R, C = buf.shape
    tr = _row_block(R, C * 4)

    def body(b_ref, o_ref):
        acc = b_ref[0].astype(F32)
        for d in range(1, N_DEV):
            acc = acc + b_ref[d].astype(F32)
        o_ref[...] = acc

    return pl.pallas_call(
        body, name=name, grid=(R // tr,),
        in_specs=[pl.BlockSpec((N_DEV, tr, C), lambda i: (0, i, 0))],
        out_specs=pl.BlockSpec((tr, C), lambda i: (i, 0)),
        out_shape=jax.ShapeDtypeStruct((R, C), F32),
        compiler_params=_cparams(("parallel",)),
    )(buf)


def _exchange_copies(src_refs, out_refs, send_sems, recv_sems, gather):
    x, y, c = lax.axis_index("x"), lax.axis_index("y"), lax.axis_index("c")
    me = 4 * x + 2 * y + c
    flip = lambda a, bit: 1 - a if bit else a
    part = lambda ref, d: ref if gather else ref.at[d]
    copies = []
    for k in range(1, N_DEV):
        px, py, pc = flip(x, (k >> 2) & 1), flip(y, (k >> 1) & 1), flip(c, k & 1)
        peer = 4 * px + 2 * py + pc
        for t in range(len(src_refs)):
            sem = t * (N_DEV - 1) + k - 1
            mk = lambda s, d: pltpu.make_async_remote_copy(
                src_ref=s, dst_ref=d, send_sem=send_sems.at[sem], recv_sem=recv_sems.at[sem],
                device_id=(px, py, pc), device_id_type=pl.DeviceIdType.MESH)
            copies.append((mk(part(src_refs[t], peer), out_refs[t].at[me]),
                           mk(part(src_refs[t], me), out_refs[t].at[peer])))
    return me, copies


def exchange(srcs, name, gather):
    n = len(srcs)
    shapes = [(N_DEV,) + s.shape if gather else s.shape for s in srcs]

    def body(*refs):
        src_refs, out_refs = refs[:n], refs[n:2 * n]
        send_sems, recv_sems, local_sems = refs[2 * n:]
        me, copies = _exchange_copies(src_refs, out_refs, send_sems, recv_sems, gather)
        for outgoing, _ in copies:
            outgoing.start()
        mine = [pltpu.make_async_copy(src_refs[t] if gather else src_refs[t].at[me], out_refs[t].at[me],
                                      local_sems.at[t]) for t in range(n)]
        for cp in mine:
            cp.start()
        for _, incoming in copies:
            incoming.wait_recv()
        for outgoing, _ in copies:
            outgoing.wait_send()
        for cp in mine:
            cp.wait()

    return pl.pallas_call(
        body, name=name,
        in_specs=[pl.BlockSpec(memory_space=pl.ANY)] * n, out_specs=[pl.BlockSpec(memory_space=pl.ANY)] * n,
        out_shape=[jax.ShapeDtypeStruct(shp, s.dtype) for shp, s in zip(shapes, srcs)],
        scratch_shapes=[pltpu.SemaphoreType.DMA((n * (N_DEV - 1),)), pltpu.SemaphoreType.DMA((n * (N_DEV - 1),)),
                        pltpu.SemaphoreType.DMA((n,))],
    )(*srcs)


def hosted_exchange(body, n_in, n_out, n_scratch, grid, srcs, gather):
    n = len(srcs)
    shapes = [(N_DEV,) + s.shape if gather else s.shape for s in srcs]

    def wrapped(*refs):
        ins, xin = refs[:n_in], refs[n_in:n_in + n]
        outs = refs[n_in + n:n_in + n + n_out]
        xout = refs[n_in + n + n_out:n_in + 2 * n + n_out]
        rest = refs[n_in + 2 * n + n_out:]
        scratch, (send_sems, recv_sems, local_sems) = rest[:n_scratch], rest[n_scratch:]
        ids = [pl.program_id(a) for a in range(len(grid))]
        first = functools.reduce(jnp.logical_and, [i == 0 for i in ids])
        last = functools.reduce(jnp.logical_and, [i == g - 1 for i, g in zip(ids, grid)])
        me, copies = _exchange_copies(xin, xout, send_sems, recv_sems, gather)
        mine = [pltpu.make_async_copy(xin[t] if gather else xin[t].at[me], xout[t].at[me], local_sems.at[t])
                for t in range(n)]

        @pl.when(first)
        def _():
            for outgoing, _ in copies:
                outgoing.start()
            for cp in mine:
                cp.start()

        body(*ins, *outs, *scratch)

        @pl.when(last)
        def _():
            for _, incoming in copies:
                incoming.wait_recv()
            for outgoing, _ in copies:
                outgoing.wait_send()
            for cp in mine:
                cp.wait()

    any_spec = pl.BlockSpec(memory_space=pl.ANY)
    return wrapped, (list(srcs), [any_spec] * n, [any_spec] * n,
                     [jax.ShapeDtypeStruct(shp, s.dtype) for shp, s in zip(shapes, srcs)],
                     [pltpu.SemaphoreType.DMA((n * (N_DEV - 1),)), pltpu.SemaphoreType.DMA((n * (N_DEV - 1),)),
                      pltpu.SemaphoreType.DMA((n,))])


def _pack(parts, cols, row_align, dtype):
    flat = jnp.concatenate([p.astype(dtype) for p in parts], axis=-1)
    n = flat.shape[-1]
    block = cols * row_align
    total = -(-n // block) * block
    flat = jnp.pad(flat, [(0, 0)] * (flat.ndim - 1) + [(0, total - n)])
    return flat.reshape(flat.shape[:-1] + (total // cols, cols))


def _unpack(buf, shapes):
    lead = buf.shape[:-2]
    flat = buf.reshape(lead + (-1,))
    out, off = [], 0
    for s in shapes:
        n = int(np.prod(s))
        out.append(flat[..., off:off + n].reshape(lead + tuple(s)))
        off += n
    return out


SHARDED = ["w_in", "w_uq", "w_ukv", "w_out", "w_up", "w_down"]
ATTN_SENT = ["w_in_p", "w_uq", "w_ukv", "w_out"]
UP_HALF = 352
FFN_SIDE = ["w_upT", "conv_w", "w_down"]


def _full_from_shards(name, s):
    if name in ("w_in", "w_in_p", "w_out", "w_down", "w_upT"):
        return s.reshape((-1, s.shape[-1]))
    return s.transpose(1, 0, 2).reshape((s.shape[1], -1))


def _shards_from_full(name, f):
    if name in ("w_in", "w_in_p", "w_out", "w_down", "w_upT"):
        return f.reshape((N_DEV, -1, f.shape[-1]))
    return f.reshape((f.shape[0], N_DEV, -1)).transpose(1, 0, 2)


def _perm_w_in(w):
    z = lambda n: jnp.zeros(w.shape[:-1] + (n,), w.dtype)
    return jnp.concatenate([w[..., :1536], w[..., 1540:1924], w[..., 1536:1540], z(60), w[..., 1924:1956], z(32)],
                           axis=-1)


def _unperm_w_in(d):
    return jnp.concatenate([d[..., :1536], d[..., 1920:1924], d[..., 1536:1920], d[..., 1984:2016]], axis=-1)


def _perm_w_uq(w):
    return jnp.pad(w.reshape(256, 4, MLA_QK_DIM), ((0, 0), (0, 0), (0, 128 - MLA_QK_DIM))).reshape(256, 512)


def _unperm_w_uq(d):
    return d.reshape(256, 4, 128)[:, :, :MLA_QK_DIM].reshape(256, 4 * MLA_QK_DIM)


def _perm_w_ukv(w):
    w4 = w.reshape(128, 4, 128)
    k = jnp.pad(w4[:, :, :64], ((0, 0), (0, 0), (0, 64))).reshape(128, 512)
    return jnp.concatenate([k, w4[:, :, 64:].reshape(128, 256)], axis=1)


def _unperm_w_ukv(d):
    dk = d[:, :512].reshape(128, 4, 128)[:, :, :64]
    dv = d[:, 512:].reshape(128, 4, 64)
    return jnp.concatenate([dk, dv], axis=-1).reshape(128, 512)


def _row(v, width=None):
    v = v.reshape(1, -1).astype(F32)
    if width is not None and v.shape[1] < width:
        v = jnp.pad(v, ((0, 0), (0, width - v.shape[1])))
    return v


def _layer_fwd(x, P, shared, send=None, ffn_from=None):
    cosr, sinr, bias = shared
    ex = lambda host: (send[host], True) if send is not None and send.get(host) else None
    proj, hT, projb = norm_matmul(x, P["g_pre"], P["w_in_p"], "in_proj", lo_tiles=2, tn_pref=1024)
    qm, km, vm, cqT, ckvT = mla_prep(proj, P["gq"], P["gkv"], P["w_uq_p"], P["w_ukv_p"], cosr, sinr)
    fcol, frow, frep = fox_gate(proj, P["fbias"])
    (oa, lse_a), _ = swa_fwd(proj, bias, P["sinks"])
    (ob, lrb), got_fox = flash_fwd(projb, projb, projb, frep, frow, qblk=C_QF // 128, kblk=C_KF // 128,
                                   vblk=C_VF // 128, nq=1, scale=HEAD_DIM ** -0.5, name="fox_fwd", exch=ex("fox"))
    (oc, lrc), got_mla = flash_fwd(qm, km, vm, None, None, qblk=0, kblk=0, vblk=0, nq=2,
                                   scale=MLA_QK_DIM ** -0.5, name="mla_fwd", exch=ex("mla"))
    x2, y1, mT = attn_out(oa, ob, oc, P["gn"], P["w_out"], P["g_apost"], x)
    if ffn_from is not None:
        P = dict(P, **ffn_from(None, got_fox, got_mla))
    up = norm_matmul(x2, P["g_fpre"], P["w_upT"], "up_proj", tn_pref=1536, w_transposed=True,
                     h_transposed=False, exch=ex("up"))
    (u0, h2), got_up = up if ex("up") is not None else (up, None)
    if ffn_from is not None:
        P = dict(P, **ffn_from(got_up, None, None))
    (x3, y2, aT), got_ffn = ffn_fwd(u0, P["conv_w"], P["conv_b"], P["w_down"], P["g_fpost"], x2, exch=ex("ffn"))
    S = dict(x=x, proj=proj, projb=projb, hT=hT, qm=qm, km=km, vm=vm, cqT=cqT, ckvT=ckvT, fcol=fcol, frow=frow,
             oa=oa, lse_a=lse_a, ob=ob, lrb=lrb, oc=oc, lrc=lrc,
             x2=x2, y1=y1, mT=mT, u0=u0, h2=h2, y2=y2, aT=aT)
    return x3, S, P, got_ffn


def _layer_bwd(dx3, P, S, shared, send_attn=None, own_small=False):
    cosr, sinr, bias = shared
    proj = S["proj"]
    G = {}
    got = {}
    ex = lambda arrays: (arrays, False) if send_attn is not None and arrays else None
    (dy2, dugT, duuT, dcg, dcu, G["ffn_post_norm"], dx2, G["ffn_pre_norm"]), got["ffn"] = ffn_bwd(
        dx3, S["y2"], S["u0"], P["conv_w"], P["conv_b"], P["w_down"], P["g_fpost"], P["w_upT"], S["x2"], P["g_fpre"],
        exch=ex(send_attn))
    dconv = jnp.concatenate([dcg[-1], dcu[-1]], axis=1)
    G["conv_w"], G["conv_b"] = dconv[0:3], dconv[3]
    G["w_down"] = matmul_nn(S["aT"], dy2, "dw_down", MXU)
    G["w_upT"] = jnp.concatenate([matmul_nn(dugT, S["h2"], "dw_up_gate", MXU),
                                  matmul_nn(duuT, S["h2"], "dw_up_up", MXU)], axis=0)
    G["w_up"] = G["w_upT"].T
    dy1, doa, dob, doc, G["group_norm"], G["attn_post_norm"] = attn_out_bwd(
        dx2, S["y1"], S["oa"], S["ob"], S["oc"], P["gn"], P["w_out"], P["g_apost"])
    G["w_out"] = matmul_nn(S["mT"], dy1, "dw_out", MXU)
    up_slices = _shards_from_full("w_upT", G["w_upT"])
    (dqa, dka, dva, dbias, dsk), _ = swa_bwd(proj, bias, P["sinks"], doa, S["oa"], S["lse_a"])
    G["swa_sinks"] = dsk[:, 0]
    pb = S["projb"]
    (dqf, dkf, dvf, dFk, dFq), got["fox"] = flash_bwd(
        pb, pb, pb, dob, S["ob"], S["lrb"], S["fcol"], S["frow"], name="fox_bwd", qblk=C_QF // 128,
        kblk=C_KF // 128, vblk=C_VF // 128, nq=1, scale=HEAD_DIM ** -0.5,
        exch=ex([up_slices[:, :UP_HALF], _shards_from_full("w_down", G["w_down"])]))
    dmisc_f, dfb = fox_gate_bwd(dFq, dFk, proj, P["fbias"])
    G["forget_bias"] = dfb[0, 0:4]
    (dqm_, dkm_, dvm_), got["mla"] = flash_bwd(
        S["qm"], S["km"], S["vm"], doc, S["oc"], S["lrc"], None, None, name="mla_bwd",
        qblk=0, kblk=0, vblk=0, nq=2, scale=MLA_QK_DIM ** -0.5, exch=ex([up_slices[:, UP_HALF:]]))
    dqm, dkv, dcq, dckv, dmisc_r, G["q_latent_norm"], G["kv_latent_norm"] = mla_prep_bwd(
        dqm_, dkm_, dvm_, proj, P["gq"], P["gkv"], P["w_uq_p"], P["w_ukv_p"], cosr, sinr)
    G["w_uq"] = _unperm_w_uq(matmul_nn(S["cqT"], dqm, "dw_uq", MXU))
    G["w_ukv"] = _unperm_w_ukv(matmul_nn(S["ckvT"], dkv, "dw_ukv", MXU))
    dproj = jnp.concatenate([dqa, dka, dva, dqf, dkf, dvf, dcq, dckv, dmisc_f + dmisc_r], axis=1).astype(MXU)
    if own_small and send_attn is not None:
        G["w_in_p"], got["dw_in"] = matmul_nn(
            S["hT"], dproj, "dw_in", MXU, exch=([_shards_from_full(n, G[n]) for n in ATTN_SENT[1:]], False))
    else:
        G["w_in_p"] = matmul_nn(S["hT"], dproj, "dw_in", MXU)
    G["w_in"] = _unperm_w_in(G["w_in_p"])
    dx, G["attn_pre_norm"] = matmul_nt_normbwd(dproj, P["w_in_p"], S["x"], P["g_pre"], dx2, "in_bwd")
    return dx, G, dbias, got


def _layer_params(l, full, small):
    return dict(
        g_pre=_row(small["attn_pre_norm"][l]),
        w_in_p=full["w_in_p"] if "w_in_p" in full else _perm_w_in(full["w_in"]),
        gq=_row(small["q_latent_norm"][l]), gkv=_row(small["kv_latent_norm"][l]),
        w_uq_p=_perm_w_uq(full["w_uq"]), w_ukv_p=_perm_w_ukv(full["w_ukv"]),
        fbias=_row(small["forget_bias"][l], 128), sinks=small["swa_sinks"][l].astype(F32),
        gn=_row(small["group_norm"][l]), w_out=full["w_out"], g_apost=_row(small["attn_post_norm"][l]),
        g_fpre=_row(small["ffn_pre_norm"][l]), conv_b=_row(small["conv_b"][l]),
        g_fpost=_row(small["ffn_post_norm"][l]),
        **{n: full[n] for n in FFN_SIDE if n in full},
        **({"w_upT": full["w_up"].T} if "w_up" in full else {}))


def _rel_bias_grad(dbias, bucket):
    flat = dbias.reshape(SWA_Q_HEADS, -1)
    hi = flat.astype(MXU)
    lo = (flat - hi.astype(F32)).astype(MXU)
    onehot = (bucket[:, None] == jnp.arange(128, dtype=jnp.int32)[None, :]).astype(MXU)
    r = matmul_nn(jnp.concatenate([hi, lo], axis=0), onehot, "rel_bias_grad")
    return (r[0:8] + r[8:16])[:, :REL_BUCKETS].T


def local_step(x, tgt, fulls, small, comm=None):
    T = x.shape[0]
    cosr, sinr = rope_tables(T)
    bias, bucket = swa_bias_table(small["rel_bias"])
    shared = (cosr, sinr, bias)
    Ps, Ss = [], []
    h, full = x, fulls[0]
    for l in range(DEPTH):
        P = _layer_params(l, full, small)
        if comm:
            h, S, P, got = _layer_fwd(h, P, shared, comm["weight_parts"](l), comm["ffn_from"])
            full = comm["attn_from"](got) if l + 1 < DEPTH else None
        else:
            h, S, P, _ = _layer_fwd(h, P, shared)
            full = fulls[l + 1] if l + 1 < DEPTH else None
        Ps.append(P)
        Ss.append(S)
    dh, sq = loss_kernel(h, tgt)
    grads = [None] * DEPTH
    dbias_sum = None
    pending = [] if comm else None
    for l in reversed(range(DEPTH)):
        dh, grads[l], dbias, got = _layer_bwd(dh, Ps[l], Ss[l], shared, pending, own_small=l == 0)
        dbias_sum = dbias if dbias_sum is None else dbias_sum + dbias
        if comm:
            comm["landed"](l, ["w_down"], got["fox"][1:])
            comm["landed"](l, ["w_upT"], [jnp.concatenate([got["fox"][0], got["mla"][0]], axis=1)])
            if pending:
                comm["landed"](l + 1, ATTN_SENT, got["ffn"])
            if l == 0:
                comm["landed"](0, ATTN_SENT[1:], got["dw_in"])
                pending = [_shards_from_full("w_in_p", grads[0]["w_in_p"])]
            else:
                pending = [_shards_from_full(n, grads[l][n]) for n in ATTN_SENT]
    return sq, dh, grads, _rel_bias_grad(dbias_sum, bucket), pending


WEIGHTS = ['attn_pre_norm', 'w_in', 'forget_bias', 'swa_sinks', 'rel_bias', 'q_latent_norm', 'w_uq',
           'kv_latent_norm', 'w_ukv', 'group_norm', 'w_out', 'attn_post_norm', 'ffn_pre_norm', 'w_up', 'conv_w',
           'conv_b', 'w_down', 'ffn_post_norm']
SMALL_PER_LAYER = ['attn_pre_norm', 'forget_bias', 'swa_sinks', 'q_latent_norm', 'kv_latent_norm', 'group_norm',
                   'attn_post_norm', 'ffn_pre_norm', 'conv_b', 'ffn_post_norm', 'conv_w']


def kernel(x, attn_pre_norm, w_in, forget_bias, swa_sinks, rel_bias, q_latent_norm, w_uq, kv_latent_norm, w_ukv, group_norm, w_out, attn_post_norm, ffn_pre_norm, w_up, conv_w, conv_b, w_down, ffn_post_norm, loss_target, m_attn_pre_norm, m_w_in, m_forget_bias, m_swa_sinks, m_rel_bias, m_q_latent_norm, m_w_uq, m_kv_latent_norm, m_w_ukv, m_group_norm, m_w_out, m_attn_post_norm, m_ffn_pre_norm, m_w_up, m_conv_w, m_conv_b, m_w_down, m_ffn_post_norm, v_attn_pre_norm, v_w_in, v_forget_bias, v_swa_sinks, v_rel_bias, v_q_latent_norm, v_w_uq, v_kv_latent_norm, v_w_ukv, v_group_norm, v_w_out, v_attn_post_norm, v_ffn_pre_norm, v_w_up, v_conv_w, v_conv_b, v_w_down, v_ffn_post_norm):
    W = dict(attn_pre_norm=attn_pre_norm, w_in=w_in, forget_bias=forget_bias, swa_sinks=swa_sinks, rel_bias=rel_bias,
             q_latent_norm=q_latent_norm, w_uq=w_uq, kv_latent_norm=kv_latent_norm, w_ukv=w_ukv,
             group_norm=group_norm, w_out=w_out, attn_post_norm=attn_post_norm, ffn_pre_norm=ffn_pre_norm,
             w_up=w_up, conv_w=conv_w, conv_b=conv_b, w_down=w_down, ffn_post_norm=ffn_post_norm)
    M = dict(attn_pre_norm=m_attn_pre_norm, w_in=m_w_in, forget_bias=m_forget_bias, swa_sinks=m_swa_sinks,
             rel_bias=m_rel_bias, q_latent_norm=m_q_latent_norm, w_uq=m_w_uq, kv_latent_norm=m_kv_latent_norm,
             w_ukv=m_w_ukv, group_norm=m_group_norm, w_out=m_w_out, attn_post_norm=m_attn_post_norm,
             ffn_pre_norm=m_ffn_pre_norm, w_up=m_w_up, conv_w=m_conv_w, conv_b=m_conv_b, w_down=m_w_down,
             ffn_post_norm=m_ffn_post_norm)
    V = dict(attn_pre_norm=v_attn_pre_norm, w_in=v_w_in, forget_bias=v_forget_bias, swa_sinks=v_swa_sinks,
             rel_bias=v_rel_bias, q_latent_norm=v_q_latent_norm, w_uq=v_w_uq, kv_latent_norm=v_kv_latent_norm,
             w_ukv=v_w_ukv, group_norm=v_group_norm, w_out=v_w_out, attn_post_norm=v_attn_post_norm,
             ffn_pre_norm=v_ffn_pre_norm, w_up=v_w_up, conv_w=v_conv_w, conv_b=v_conv_b, w_down=v_w_down,
             ffn_post_norm=v_ffn_post_norm)
    me = 4 * lax.axis_index("x") + 2 * lax.axis_index("y") + lax.axis_index("c")

    def attn_shards(l):
        return [_perm_w_in(w_in[l].astype(MXU))] + [W[n][l].astype(MXU) for n in ATTN_SENT[1:]]

    def weight_parts(l):
        up = jnp.swapaxes(W["w_up"][l], 0, 1).astype(MXU)
        return dict(up=[W["w_down"][l].astype(MXU)], fox=[up[:UP_HALF]], mla=[up[UP_HALF:], conv_w[l]],
                    ffn=attn_shards(l + 1) if l + 1 < DEPTH else [])

    def ffn_from(got_up, got_fox, got_mla):
        if got_up is not None:
            return dict(w_down=_full_from_shards("w_down", got_up[0]))
        return dict(w_upT=_full_from_shards("w_upT", jnp.concatenate([got_fox[0], got_mla[0]], axis=1)),
                    conv_w=got_mla[1].transpose(1, 0, 2).reshape(3, 2 * D_FF))

    def attn_from(got_ffn):
        return {n: _full_from_shards(n, s) for n, s in zip(ATTN_SENT, got_ffn)}

    landed = [{} for _ in range(DEPTH)]

    def on_landed(l, names, arrays):
        landed[l].update(zip(names, arrays))

    comm = dict(weight_parts=weight_parts, ffn_from=ffn_from, attn_from=attn_from, landed=on_landed)
    full0 = dict(zip(ATTN_SENT, map(_full_from_shards, ATTN_SENT, exchange(attn_shards(0), "gather_weights", True))))
    sq, dx, grads, drel, last = local_step(x[0], loss_target[0], [full0], W, comm)
    G, delta, new_m, new_v = {}, {}, {}, {}

    def update(n, exch=None):
        shp = W[n].shape
        if n == "w_up":
            v3 = lambda a: jnp.swapaxes(a, 1, 2)
            back = v3
            g = [landed[l]["w_upT"] for l in range(DEPTH)]
        else:
            v3 = lambda a: a.reshape(shp if len(shp) == 3 else (1,) + shp)
            back = lambda a: a.reshape(shp)
            g = [landed[l][n] for l in range(DEPTH)] if n in SHARDED else v3(G[n])
        (g, d, nm, nv), got = adamw(v3(W[n]), g, v3(M[n]), v3(V[n]), "adamw_" + n, exch)
        G[n], delta[n], new_m[n], new_v[n] = back(g), back(d), back(nm), back(nv)
        return got

    parts, shapes = [], []
    for l in range(DEPTH):
        for n in SMALL_PER_LAYER:
            parts.append(grads[l][n].astype(F32).reshape(-1))
            shapes.append(grads[l][n].shape)
    parts += [drel.reshape(-1), jnp.sum(sq).reshape(1) * (0.5 / D_MODEL)]
    shapes += [drel.shape, (1,)]
    on_landed(0, ATTN_SENT[:1], update("w_up", (last, False)))
    for l in range(DEPTH):
        landed[l]["w_in"] = _unperm_w_in(landed[l]["w_in_p"])
    gathered = update("w_down", ([_pack(parts, 128, 8, F32)], True))[0]
    red = _unpack(sum_devices(gathered, "sum_small"), shapes)
    k = 0
    per = {n: [] for n in SMALL_PER_LAYER}
    for l in range(DEPTH):
        for n in SMALL_PER_LAYER:
            per[n].append(red[k])
            k += 1
    for n in SMALL_PER_LAYER:
        G[n] = jnp.stack(per[n]).reshape((DEPTH, 3, 2 * D_FF) if n == "conv_w" else W[n].shape)
    G["rel_bias"] = red[k]
    loss = red[k + 1][0]
    G["conv_w"] = lax.dynamic_slice_in_dim(G["conv_w"], me * 704, 704, axis=2)

    for n in WEIGHTS:
        if n not in ("w_up", "w_down"):
            update(n)
    return (loss, dx[None], *[G[n] for n in WEIGHTS], *[delta[n] for n in WEIGHTS],
            *[new_m[n] for n in WEIGHTS], *[new_v[n] for n in WEIGHTS])
```

```python
import functools
import math

import numpy as np
import jax
import jax.numpy as jnp
from jax import lax
from jax.experimental import pallas as pl
from jax.experimental.pallas import tpu as pltpu

F32 = jnp.float32
MXU = jnp.bfloat16

N_DEV = 8
DEPTH = 4
D_MODEL = 1024
HEAD_DIM = 64
WINDOW = 128
SWA_Q_HEADS = 8
REL_BUCKETS = 32
REL_MAX_DIST = 128
MLA_QK_DIM = 96
ROPE_DIM = 32
ROPE_THETA = 10000.0
D_FF = 2816
EPS = 1e-6
NEG = -1e30
IN_COLS = 1956
C_QA, C_KA, C_VA = 0, 512, 640
C_QF, C_KF, C_VF = 768, 1024, 1280
C_CQ, C_CKV, C_MISC = 1536, 1792, 1920
ROPE_LANE0 = 64
ADAM_LR, ADAM_B1, ADAM_B2, ADAM_EPS, ADAM_WD, ADAM_STEP = 0.001, 0.9, 0.999, 1e-08, 0.01, 10

VMEM_LIMIT = 56 * 1024 * 1024


def _cparams(sem=None):
    return pltpu.CompilerParams(dimension_semantics=sem, vmem_limit_bytes=VMEM_LIMIT)


def _tile(n, pref):
    if n <= pref:
        return n
    t = pref - pref % 128
    while t >= 128:
        if n % t == 0:
            return t
        t -= 128
    return n


def _dot(a, b):
    return jnp.dot(a.astype(MXU), b.astype(MXU), preferred_element_type=F32)


def _dot_nt(a, b):
    return lax.dot_general(a.astype(MXU), b.astype(MXU), (((1,), (1,)), ((), ())),
                           preferred_element_type=F32)


def _rms_fwd(x, g):
    return x * lax.rsqrt(jnp.mean(x * x, axis=-1, keepdims=True) + EPS) * g


def _rms_bwd(dy, x, g, n=None):
    r = lax.rsqrt(jnp.mean(x * x, axis=-1, keepdims=True) + EPS)
    xh = x * r
    dg = jnp.sum(dy * xh, axis=0, keepdims=True)
    dxh = dy * g
    dx = r * (dxh - xh * jnp.mean(dxh * xh, axis=-1, keepdims=True))
    return dx, dg


def _acc_out(ref, val, first):
    @pl.when(first)
    def _():
        ref[...] = val

    @pl.when(jnp.logical_not(first))
    def _():
        ref[...] += val


def norm_matmul(x, g, w, name, lo_tiles=0, tn_pref=512, w_transposed=False, h_transposed=True, exch=None):
    T, K = x.shape
    N = w.shape[0] if w_transposed else w.shape[1]
    tm, tn = _tile(T, 1024), _tile(N, tn_pref)

    def body(x_ref, g_ref, w_ref, o_ref, hT_ref, *rest):
        h_sc = rest[-1]
        j = pl.program_id(1)

        @pl.when(j == 0)
        def _():
            h = _rms_fwd(x_ref[...], g_ref[...])
            h_sc[...] = h.astype(MXU)
            hT_ref[...] = (h.T if h_transposed else h).astype(MXU)

        r = (_dot_nt if w_transposed else _dot)(h_sc[...], w_ref[...])
        o_ref[...] = r
        if lo_tiles:
            @pl.when(j < lo_tiles)
            def _():
                rest[0][...] = r.astype(MXU)

    h_spec = pl.BlockSpec((K, tm), lambda i, j: (0, i)) if h_transposed else pl.BlockSpec((tm, K), lambda i, j: (i, 0))
    out_specs = [pl.BlockSpec((tm, tn), lambda i, j: (i, j)), h_spec]
    out_shape = [jax.ShapeDtypeStruct((T, N), F32), jax.ShapeDtypeStruct((K, T) if h_transposed else (T, K), MXU)]
    if lo_tiles:
        out_specs.append(pl.BlockSpec((tm, tn), lambda i, j: (i, jnp.minimum(j, lo_tiles - 1))))
        out_shape.append(jax.ShapeDtypeStruct((T, lo_tiles * tn), MXU))
    in_specs = [pl.BlockSpec((tm, K), lambda i, j: (i, 0)),
                pl.BlockSpec((1, K), lambda i, j: (0, 0)),
                pl.BlockSpec((tn, K), lambda i, j: (j, 0)) if w_transposed else
                pl.BlockSpec((K, tn), lambda i, j: (0, j))]
    outs, landed = _call_hosting(body, name, (T // tm, N // tn), [x, g, w], in_specs, out_specs, out_shape,
                                 [pltpu.VMEM((tm, K), MXU)], exch)
    return outs if exch is None else (outs, landed)


def matmul_nn(a, b, name, out_dtype=F32, exch=None):
    M, K = a.shape
    N = b.shape[1]
    tm, tn, tk = _tile(M, 1408), _tile(N, 1536), _tile(K, 1024)
    nk = K // tk

    def body(a_ref, b_ref, o_ref, acc):
        k = pl.program_id(2)
        part = _dot(a_ref[...], b_ref[...])
        _acc_out(acc, part, k == 0)

        @pl.when(k == nk - 1)
        def _():
            o_ref[...] = acc[...].astype(out_dtype)

    outs, landed = _call_hosting(
        body, name, (M // tm, N // tn, nk), [a, b],
        [pl.BlockSpec((tm, tk), lambda i, j, k: (i, k)), pl.BlockSpec((tk, tn), lambda i, j, k: (k, j))],
        [pl.BlockSpec((tm, tn), lambda i, j, k: (i, j))], [jax.ShapeDtypeStruct((M, N), out_dtype)],
        [pltpu.VMEM((tm, tn), F32)], exch)
    return outs[0] if exch is None else (outs[0], landed)


def matmul_nt_normbwd(dy, w, x, g, dres, name):
    T, N = dy.shape
    K = w.shape[0]
    tm, tn = _tile(T, 1024), _tile(N, 1536)
    nj = N // tn

    def body(dy_ref, w_ref, x_ref, g_ref, dres_ref, dx_ref, dg_ref, acc):
        i, j = pl.program_id(0), pl.program_id(1)
        _acc_out(acc, _dot_nt(dy_ref[...], w_ref[...]), j == 0)

        @pl.when(j == nj - 1)
        def _():
            dx, dg = _rms_bwd(acc[...], x_ref[...], g_ref[...])
            dx_ref[...] = dres_ref[...] + dx
            _acc_out(dg_ref, dg, i == 0)

    return pl.pallas_call(
        body, name=name, grid=(T // tm, nj),
        in_specs=[pl.BlockSpec((tm, tn), lambda i, j: (i, j)),
                  pl.BlockSpec((K, tn), lambda i, j: (0, j)),
                  pl.BlockSpec((tm, K), lambda i, j: (i, 0)),
                  pl.BlockSpec((1, K), lambda i, j: (0, 0)),
                  pl.BlockSpec((tm, K), lambda i, j: (i, 0))],
        out_specs=[pl.BlockSpec((tm, K), lambda i, j: (i, 0)),
                   pl.BlockSpec((1, K), lambda i, j: (0, 0))],
        out_shape=[jax.ShapeDtypeStruct((T, K), F32), jax.ShapeDtypeStruct((1, K), F32)],
        scratch_shapes=[pltpu.VMEM((tm, K), F32)],
        compiler_params=_cparams(("arbitrary", "arbitrary")),
    )(dy, w, x, g, dres)


def loss_kernel(y, tgt):
    T, D = y.shape
    tm = _tile(T, 512)

    def body(y_ref, t_ref, dy_ref, acc_ref):
        e = y_ref[...] - t_ref[...]
        dy_ref[...] = e * (1.0 / D)
        _acc_out(acc_ref, jnp.sum(e * e, axis=0, keepdims=True), pl.program_id(0) == 0)

    return pl.pallas_call(
        body, name="loss", grid=(T // tm,),
        in_specs=[pl.BlockSpec((tm, D), lambda i: (i, 0)), pl.BlockSpec((tm, D), lambda i: (i, 0))],
        out_specs=[pl.BlockSpec((tm, D), lambda i: (i, 0)), pl.BlockSpec((1, D), lambda i: (0, 0))],
        out_shape=[jax.ShapeDtypeStruct((T, D), F32), jax.ShapeDtypeStruct((1, D), F32)],
        compiler_params=_cparams(("arbitrary",)),
    )(y, tgt)


def _rope_partner(x):
    lane = lax.broadcasted_iota(jnp.int32, (1, 128), 1)
    return jnp.where(lane < ROPE_LANE0 + ROPE_DIM // 2, pltpu.roll(x, 128 - ROPE_DIM // 2, 1),
                     pltpu.roll(x, ROPE_DIM // 2, 1))


def _rope_apply(x, cos, sin_signed):
    return x * cos + _rope_partner(x) * sin_signed


def _rope_apply_bwd(dy, cos, sin_signed):
    lane = lax.broadcasted_iota(jnp.int32, (1, 128), 1)
    rotary = (lane >= ROPE_LANE0) & (lane < ROPE_LANE0 + ROPE_DIM)
    return dy * cos + jnp.where(rotary, _rope_partner(dy * sin_signed), 0.0)


def rope_tables(T):
    pos = jnp.arange(T, dtype=F32)
    inv_freq = ROPE_THETA ** (-(jnp.arange(ROPE_DIM // 2, dtype=F32) * 2.0 / ROPE_DIM))
    ang = pos[:, None] * inv_freq[None, :]
    cos, sin = jnp.cos(ang), jnp.sin(ang)
    z = jnp.zeros((T, ROPE_LANE0), F32)
    z2 = jnp.zeros((T, 128 - ROPE_LANE0 - ROPE_DIM), F32)
    cosr = jnp.concatenate([z, cos, cos, z2], axis=1)
    sinr = jnp.concatenate([z, -sin, sin, z2], axis=1)
    return cosr, sinr


def mla_prep(proj, gq, gkv, wuq, wukv, cosr, sinr):
    T = proj.shape[0]
    tm = _tile(T, 512)

    def body(cq_ref, ckv_ref, misc_ref, gq_ref, gkv_ref, wuq_ref, wukv_ref, cos_ref, sin_ref,
             q_ref, k_ref, v_ref, cqT_ref, ckvT_ref):
        lane = lax.broadcasted_iota(jnp.int32, (1, 128), 1)
        cosr_, sinr_ = cos_ref[...], sin_ref[...]
        cosq = cosr_ + jnp.where(lane < ROPE_LANE0, 1.0, 0.0)
        cqn = _rms_fwd(cq_ref[...], gq_ref[...])
        cqT_ref[...] = cqn.T.astype(MXU)
        qm = _dot(cqn, wuq_ref[...])
        q_ref[...] = jnp.concatenate(
            [_rope_apply(qm[:, 128 * h:128 * (h + 1)], cosq, sinr_) for h in range(4)], axis=1).astype(MXU)
        ckvn = _rms_fwd(ckv_ref[...], gkv_ref[...])
        ckvT_ref[...] = ckvn.T.astype(MXU)
        kv = _dot(ckvn, wukv_ref[...])
        kr = _rope_apply(misc_ref[...], cosr_, sinr_)
        k_ref[...] = jnp.concatenate(
            [kv[:, 128 * h:128 * (h + 1)] + kr for h in range(4)], axis=1).astype(MXU)
        v_ref[...] = kv[:, 512:768].astype(MXU)

    row = lambda i: (i, 0)
    const = lambda i: (0, 0)
    return pl.pallas_call(
        body, name="mla_prep", grid=(T // tm,),
        in_specs=[pl.BlockSpec((tm, 256), lambda i: (i, C_CQ // 256)),
                  pl.BlockSpec((tm, 128), lambda i: (i, C_CKV // 128)),
                  pl.BlockSpec((tm, 128), lambda i: (i, C_MISC // 128)),
                  pl.BlockSpec((1, 256), const), pl.BlockSpec((1, 128), const),
                  pl.BlockSpec((256, 512), const), pl.BlockSpec((128, 768), const),
                  pl.BlockSpec((tm, 128), row), pl.BlockSpec((tm, 128), row)],
        out_specs=[pl.BlockSpec((tm, 512), row), pl.BlockSpec((tm, 512), row), pl.BlockSpec((tm, 256), row),
                   pl.BlockSpec((256, tm), lambda i: (0, i)), pl.BlockSpec((128, tm), lambda i: (0, i))],
        out_shape=[jax.ShapeDtypeStruct((T, 512), MXU), jax.ShapeDtypeStruct((T, 512), MXU),
                   jax.ShapeDtypeStruct((T, 256), MXU),
                   jax.ShapeDtypeStruct((256, T), MXU), jax.ShapeDtypeStruct((128, T), MXU)],
        compiler_params=_cparams(("parallel",)),
    )(proj, proj, proj, gq, gkv, wuq, wukv, cosr, sinr)


def mla_prep_bwd(dq, dk, dv, proj, gq, gkv, wuq, wukv, cosr, sinr):
    T = proj.shape[0]
    tm = _tile(T, 512)

    def body(dq_ref, dk_ref, dv_ref, cq_ref, ckv_ref, gq_ref, gkv_ref, wuq_ref, wukv_ref, cos_ref, sin_ref,
             dqm_ref, dkv_ref, dcq_ref, dckv_ref, dmisc_ref, dgq_ref, dgkv_ref):
        first = pl.program_id(0) == 0
        lane = lax.broadcasted_iota(jnp.int32, (1, 128), 1)
        cosr_, sinr_ = cos_ref[...], sin_ref[...]
        cosq = cosr_ + jnp.where(lane < ROPE_LANE0, 1.0, 0.0)
        dqv = dq_ref[...]
        dqm = jnp.concatenate(
            [_rope_apply_bwd(dqv[:, 128 * h:128 * (h + 1)], cosq, sinr_) for h in range(4)], axis=1)
        dqm_ref[...] = dqm.astype(MXU)
        dcq, dgq = _rms_bwd(_dot_nt(dqm, wuq_ref[...]), cq_ref[...], gq_ref[...])
        dcq_ref[...] = dcq
        _acc_out(dgq_ref, dgq, first)
        dkv_ = dk_ref[...]
        heads = [dkv_[:, 128 * h:128 * (h + 1)] for h in range(4)]
        dkr = heads[0] + heads[1] + heads[2] + heads[3]
        dmisc_ref[...] = _rope_apply_bwd(dkr, cosr_, sinr_)
        dkvm = jnp.concatenate([jnp.where(lane < ROPE_LANE0, hd, 0.0) for hd in heads] + [dv_ref[...]], axis=1)
        dkv_ref[...] = dkvm.astype(MXU)
        dckv, dgkv = _rms_bwd(_dot_nt(dkvm, wukv_ref[...]), ckv_ref[...], gkv_ref[...])
        dckv_ref[...] = dckv
        _acc_out(dgkv_ref, dgkv, first)

    row = lambda i: (i, 0)
    const = lambda i: (0, 0)
    return pl.pallas_call(
        body, name="mla_prep_bwd", grid=(T // tm,),
        in_specs=[pl.BlockSpec((tm, 512), row), pl.BlockSpec((tm, 512), row), pl.BlockSpec((tm, 256), row),
                  pl.BlockSpec((tm, 256), lambda i: (i, C_CQ // 256)),
                  pl.BlockSpec((tm, 128), lambda i: (i, C_CKV // 128)),
                  pl.BlockSpec((1, 256), const), pl.BlockSpec((1, 128), const),
                  pl.BlockSpec((256, 512), const), pl.BlockSpec((128, 768), const),
                  pl.BlockSpec((tm, 128), row), pl.BlockSpec((tm, 128), row)],
        out_specs=[pl.BlockSpec((tm, 512), row), pl.BlockSpec((tm, 768), row), pl.BlockSpec((tm, 256), row),
                   pl.BlockSpec((tm, 128), row), pl.BlockSpec((tm, 128), row),
                   pl.BlockSpec((1, 256), const), pl.BlockSpec((1, 128), const)],
        out_shape=[jax.ShapeDtypeStruct((T, 512), MXU), jax.ShapeDtypeStruct((T, 768), MXU),
                   jax.ShapeDtypeStruct((T, 256), F32), jax.ShapeDtypeStruct((T, 128), F32),
                   jax.ShapeDtypeStruct((T, 128), F32),
                   jax.ShapeDtypeStruct((1, 256), F32), jax.ShapeDtypeStruct((1, 128), F32)],
        compiler_params=_cparams(("arbitrary",)),
    )(dq, dk, dv, proj, proj, gq, gkv, wuq, wukv, cosr, sinr)


def _split3(x):
    hi = x.astype(MXU)
    r1 = x - hi.astype(F32)
    mid = r1.astype(MXU)
    lo = (r1 - mid.astype(F32)).astype(MXU)
    return hi, mid, lo


def _tri_matmul(tri, x):
    hi, mid, lo = _split3(x)
    d = lambda p: jnp.dot(tri, p, preferred_element_type=F32)
    return d(hi) + d(mid) + d(lo)


def _log_sigmoid(z):
    return jnp.minimum(z, 0.0) - jnp.log(1.0 + jnp.exp(-jnp.abs(z)))


def fox_gate(proj, fbias):
    T = proj.shape[0]
    tb = _tile(T, 512)

    def body(misc_ref, b_ref, fc_ref, fr_ref, frep_ref, carry):
        @pl.when(pl.program_id(0) == 0)
        def _():
            carry[...] = jnp.zeros_like(carry)

        lane = lax.broadcasted_iota(jnp.int32, (1, 128), 1)
        lf = jnp.where(lane < 4, _log_sigmoid(misc_ref[...] + b_ref[...]), 0.0)
        r = lax.broadcasted_iota(jnp.int32, (tb, tb), 0)
        c = lax.broadcasted_iota(jnp.int32, (tb, tb), 1)
        tri = jnp.where(r >= c, 1.0, 0.0).astype(MXU)
        F = _tri_matmul(tri, lf) + carry[...]
        carry[...] = carry[...] + jnp.sum(lf, axis=0, keepdims=True)
        fc_ref[0] = F
        fc_ref[1] = pltpu.roll(F, 126, 1)
        ft = F.T[0:8, :]
        fr_ref[0] = ft
        fr_ref[1] = pltpu.roll(ft, 6, 0)
        for h in range(4):
            frep_ref[h] = jnp.broadcast_to(_lane_pick(F, h), (tb, 128))

    return pl.pallas_call(
        body, name="fox_gate", grid=(T // tb,),
        in_specs=[pl.BlockSpec((tb, 128), lambda i: (i, C_MISC // 128)), pl.BlockSpec((1, 128), lambda i: (0, 0))],
        out_specs=[pl.BlockSpec((2, tb, 128), lambda i: (0, i, 0)), pl.BlockSpec((2, 8, tb), lambda i: (0, 0, i)),
                   pl.BlockSpec((4, tb, 128), lambda i: (0, i, 0))],
        out_shape=[jax.ShapeDtypeStruct((2, T, 128), F32), jax.ShapeDtypeStruct((2, 8, T), F32),
                   jax.ShapeDtypeStruct((4, T, 128), F32)],
        scratch_shapes=[pltpu.VMEM((1, 128), F32)],
        compiler_params=_cparams(("arbitrary",)),
    )(proj, fbias)


def fox_gate_bwd(dFq, dFk, proj, fbias):
    T = proj.shape[0]
    tb = _tile(T, 512)
    nb = T // tb

    def body(dq_ref, dk_ref, misc_ref, b_ref, dm_ref, db_ref, carry):
        first = pl.program_id(0) == 0

        @pl.when(first)
        def _():
            carry[...] = jnp.zeros_like(carry)

        lane = lax.broadcasted_iota(jnp.int32, (1, 128), 1)
        dF = jnp.where(lane < 4, (dq_ref[0] + dk_ref[0]) + pltpu.roll(dq_ref[1] + dk_ref[1], 2, 1), 0.0)
        r = lax.broadcasted_iota(jnp.int32, (tb, tb), 0)
        c = lax.broadcasted_iota(jnp.int32, (tb, tb), 1)
        tri = jnp.where(r <= c, 1.0, 0.0).astype(MXU)
        dlf = _tri_matmul(tri, dF) + carry[...]
        carry[...] = carry[...] + jnp.sum(dF, axis=0, keepdims=True)
        z = misc_ref[...] + b_ref[...]
        dz = jnp.where(lane < 4, dlf * (1.0 / (1.0 + jnp.exp(z))), 0.0)
        dm_ref[...] = dz
        _acc_out(db_ref, jnp.sum(dz, axis=0, keepdims=True), first)

    return pl.pallas_call(
        body, name="fox_gate_bwd", grid=(nb,),
        in_specs=[pl.BlockSpec((2, tb, 128), lambda i: (0, nb - 1 - i, 0)),
                  pl.BlockSpec((2, tb, 128), lambda i: (0, nb - 1 - i, 0)),
                  pl.BlockSpec((tb, 128), lambda i: (nb - 1 - i, C_MISC // 128)),
                  pl.BlockSpec((1, 128), lambda i: (0, 0))],
        out_specs=[pl.BlockSpec((tb, 128), lambda i: (nb - 1 - i, 0)), pl.BlockSpec((1, 128), lambda i: (0, 0))],
        out_shape=[jax.ShapeDtypeStruct((T, 128), F32), jax.ShapeDtypeStruct((1, 128), F32)],
        scratch_shapes=[pltpu.VMEM((1, 128), F32)],
        compiler_params=_cparams(("arbitrary",)),
    )(dFq, dFk, proj, fbias)


FLASH_TILE = 512


def _row_stat_tile(a, b, n):
    at = jnp.broadcast_to(a, (n, 128)).T[0:8, :]
    bt = jnp.broadcast_to(b, (n, 128)).T[0:8, :]
    sub = lax.broadcasted_iota(jnp.int32, (8, 1), 0)
    return jnp.where(sub == 0, at, jnp.where(sub == 1, bt, 0.0))


def _col_stat_tile(a, b):
    lane = lax.broadcasted_iota(jnp.int32, (1, 128), 1)
    return jnp.where(lane == 0, a, jnp.where(lane == 1, b, 0.0))


def _lane_pick(x, h):
    lane = lax.broadcasted_iota(jnp.int32, (1, 128), 1)
    return jnp.sum(jnp.where(lane == h, x, 0.0), axis=1, keepdims=True)


def _half_mask(h):
    lane = lax.broadcasted_iota(jnp.int32, (1, 128), 1)
    return (lane // HEAD_DIM) == h


def _call_hosting(body, name, grid, args, in_specs, out_specs, out_shape, scratch, exch):
    n_out = len(out_shape)
    if exch is not None:
        body, (xargs, xin, xout, xshape, xscratch) = hosted_exchange(
            body, len(args), n_out, len(scratch), grid, *exch)
        args, in_specs, out_specs = args + xargs, in_specs + xin, out_specs + xout
        out_shape, scratch = out_shape + xshape, scratch + xscratch
    res = pl.pallas_call(
        body, name=name, grid=grid, in_specs=in_specs, out_specs=out_specs, out_shape=out_shape,
        scratch_shapes=scratch, compiler_params=_cparams(("arbitrary",) * len(grid)),
    )(*args)
    return res[:n_out], res[n_out:]


def flash_fwd(q, k, v, frep, frow, *, qblk, kblk, vblk, nq, scale, name, exch=None):
    T = q.shape[0]
    tk = _tile(T, FLASH_TILE)
    tq = _tile(T, 2 * FLASH_TILE)
    per_q = tq // tk
    wq = 128 * nq
    has_f = frep is not None

    def body(*refs):
        if has_f:
            q_ref, k_ref, v_ref, fk_ref, fr_ref, o_ref, lr_ref, vT_sc, m_sc, acc_sc = refs
        else:
            q_ref, k_ref, v_ref, o_ref, lr_ref, vT_sc, m_sc, acc_sc = refs
        i = pl.program_id(1)

        @pl.when(i == 0)
        def _():
            vT_sc[...] = v_ref[...].astype(F32).T.astype(MXU)

        key_row = lax.broadcasted_iota(jnp.int32, (tk, 1), 0)
        q_col = lax.broadcasted_iota(jnp.int32, (1, tq), 1)
        row_half = lax.broadcasted_iota(jnp.int32, (128, 1), 0) // HEAD_DIM
        qb = q_ref[...].astype(F32) * scale
        if nq == 1:
            qhs = [jnp.where(_half_mask(h), qb, 0).astype(MXU) for h in range(2)]
        else:
            qhs = [qb[:, 128 * h:128 * (h + 1)].astype(MXU) for h in range(2)]
        for h in range(2):
            m_sc[h] = jnp.full((1, tq), NEG, F32)
            acc_sc[h] = jnp.zeros((128, tq), F32)

        def make_step(diag_block):
            def step(j, carry):
                off = pl.multiple_of(j * tk, tk)
                ks = k_ref[pl.ds(off, tk), :]
                vT = vT_sc[:, pl.ds(off, tk)]
                for h in range(2):
                    kh = ks if nq == 1 else ks[:, 128 * h:128 * (h + 1)]
                    sT = _dot_nt(kh, qhs[h])
                    if has_f:
                        fk = fk_ref[h, pl.ds(off, tk), :]
                        sT = sT + (fr_ref[0, h:h + 1, :] - jnp.concatenate([fk] * (tq // 128), axis=1))
                    if diag_block is not None:
                        sT = jnp.where(key_row + diag_block * tk <= q_col, sT, NEG)
                    m_prev = m_sc[h]
                    m_new = jnp.maximum(m_prev, jnp.max(sT, axis=0, keepdims=True))
                    alpha = jnp.exp(m_prev - m_new)
                    pT = jnp.exp(sT - m_new)
                    vTh = jnp.where(row_half == h, vT, jnp.ones_like(vT))
                    acc_sc[h] = alpha * acc_sc[h] + _dot(vTh, pT)
                    m_sc[h] = m_new
                return carry
            return step

        lax.fori_loop(0, per_q * i, make_step(None), 0)
        for d in range(per_q):
            make_step(d)(per_q * i + d, 0)
        outs, lses = [], []
        for h in range(2):
            acc = acc_sc[h]
            outs.append(acc / pltpu.roll(acc, HEAD_DIM, 0))
            l = acc_sc[h, HEAD_DIM * (1 - h):HEAD_DIM * (1 - h) + 1, :]
            lses.append(m_sc[h] + jnp.log(l))
        o_ref[...] = jnp.where(row_half == 0, outs[0], outs[1]).T
        sub = lax.broadcasted_iota(jnp.int32, (8, 1), 0)
        lr_ref[0] = jnp.where(sub == 0, lses[0], jnp.where(sub == 1, lses[1], 0.0))

    in_specs = [pl.BlockSpec((tq, wq), lambda p, i: (i, qblk + p)),
                pl.BlockSpec((T, wq), lambda p, i: (0, kblk + p)),
                pl.BlockSpec((T, 128), lambda p, i: (0, vblk + p))]
    args = [q, k, v]
    if has_f:
        in_specs += [pl.BlockSpec((2, T, 128), lambda p, i: (p, 0, 0)),
                     pl.BlockSpec((1, 8, tq), lambda p, i: (p, 0, i))]
        args += [frep, frow]
    out_specs = [pl.BlockSpec((tq, 128), lambda p, i: (i, p)), pl.BlockSpec((1, 8, tq), lambda p, i: (p, 0, i))]
    out_shape = [jax.ShapeDtypeStruct((T, 256), F32), jax.ShapeDtypeStruct((2, 8, T), F32)]
    scratch = [pltpu.VMEM((128, T), MXU), pltpu.VMEM((2, 1, tq), F32), pltpu.VMEM((2, 128, tq), F32)]
    return _call_hosting(body, name, (2, T // tq), args, in_specs, out_specs, out_shape, scratch, exch)


def flash_bwd(q, k, v, do, o, lrow, fcol, frow, *, qblk, kblk, vblk, nq, scale, name, exch=None):
    T = q.shape[0]
    tk = _tile(T, FLASH_TILE)
    tq = _tile(T, 2 * FLASH_TILE)
    per_q = tq // tk
    wq = 128 * nq
    nqb, nkb = T // tq, T // tk
    has_f = fcol is not None

    def body(*refs):
        if has_f:
            (q_ref, k_ref, v_ref, do_ref, o_ref, lr_ref, fc_ref, fr_ref,
             dq_ref, dk_ref, dv_ref, df_ref, dfq_ref, dk_sc, dv_sc, dqT_sc, d_sc, df_sc, dfq_sc) = refs
        else:
            q_ref, k_ref, v_ref, do_ref, o_ref, lr_ref, dq_ref, dk_ref, dv_ref, dk_sc, dv_sc, dqT_sc, d_sc = refs
        j = pl.program_id(1)
        diag = (lax.broadcasted_iota(jnp.int32, (tk, 1), 0) + (j % per_q) * tk
                <= lax.broadcasted_iota(jnp.int32, (1, tq), 1))
        hms = [_half_mask(h) for h in range(2)]

        @pl.when(j == 0)
        def _():
            dqT_sc[...] = jnp.zeros_like(dqT_sc)
            if has_f:
                dfq_sc[...] = jnp.zeros_like(dfq_sc)

            def delta(b, carry):
                off = pl.multiple_of(b * tq, tq)
                prod = do_ref[pl.ds(off, tq), :] * o_ref[pl.ds(off, tq), :]
                Ds = [jnp.sum(jnp.where(hms[h], prod, 0.0), axis=1, keepdims=True) for h in range(2)]
                d_sc[:, pl.ds(off, tq)] = _row_stat_tile(Ds[0], Ds[1], tq)
                return carry

            lax.fori_loop(0, nqb, delta, 0)

        kb = k_ref[...]
        vb = v_ref[...]
        if nq == 1:
            khs = [jnp.where(hms[h], kb, 0).astype(MXU) for h in range(2)]
        else:
            khs = [kb[:, 128 * h:128 * (h + 1)].astype(MXU) for h in range(2)]
        kTs = [kh.astype(F32).T.astype(MXU) for kh in khs]
        kss = [(kh.astype(F32) * scale).astype(MXU) for kh in khs]
        vhs = [jnp.where(hms[h], vb, 0).astype(MXU) for h in range(2)]
        fks = [_lane_pick(fc_ref[0], h) for h in range(2)] if has_f else None
        dv_sc[...] = jnp.zeros_like(dv_sc)
        dk_sc[...] = jnp.zeros_like(dk_sc)
        if has_f:
            df_sc[...] = jnp.zeros_like(df_sc)

        def make_step(masked):
            def step(i, carry):
                off = pl.multiple_of(i * tq, tq)
                qs = q_ref[pl.ds(off, tq), :]
                dos = do_ref[pl.ds(off, tq), :]
                for h in range(2):
                    qh = qs if nq == 1 else qs[:, 128 * h:128 * (h + 1)]
                    sT = _dot_nt(kss[h], qh)
                    if has_f:
                        sT = sT + (fr_ref[0, h:h + 1, pl.ds(off, tq)] - fks[h])
                    pT = jnp.exp(sT - lr_ref[0, h:h + 1, pl.ds(off, tq)])
                    if masked:
                        pT = jnp.where(diag, pT, 0.0)
                    dsT = pT * (_dot_nt(vhs[h], dos) - d_sc[h:h + 1, pl.ds(off, tq)])
                    dv_sc[...] += _dot(pT, jnp.where(hms[h], dos, 0))
                    qq = jnp.where(hms[h], qs, 0) if nq == 1 else qh
                    dk_sc[h if nq == 2 else 0] += _dot(dsT, qq)
                    dqT_sc[h if nq == 2 else 0, :, pl.ds(off, tq)] += _dot(kTs[h], dsT)
                    if has_f:
                        part = dsT[:, 0:128]
                        for c in range(1, tq // 128):
                            part = part + dsT[:, 128 * c:128 * (c + 1)]
                        df_sc[h] += part
                        dfq_sc[h:h + 1, pl.ds(off, tq)] += jnp.sum(dsT, axis=0, keepdims=True)
                return carry
            return step

        make_step(True)(j // per_q, 0)
        lax.fori_loop(j // per_q + 1, nqb, make_step(False), 0)
        if nq == 1:
            dk_ref[...] = dk_sc[0] * scale
        else:
            dk_ref[...] = jnp.concatenate([dk_sc[0], dk_sc[1]], axis=1) * scale
        dv_ref[...] = dv_sc[...]
        if has_f:
            df_ref[0] = _col_stat_tile(-jnp.sum(df_sc[0], axis=1, keepdims=True),
                                       -jnp.sum(df_sc[1], axis=1, keepdims=True))

        @pl.when(j == nkb - 1)
        def _():
            if nq == 1:
                dq_ref[...] = dqT_sc[0].T * scale
            else:
                dq_ref[...] = jnp.concatenate([dqT_sc[0].T, dqT_sc[1].T], axis=1) * scale
            if has_f:
                sub = lax.broadcasted_iota(jnp.int32, (128, 1), 0)
                rows = jnp.where(sub == 0, dfq_sc[0:1, :], jnp.where(sub == 1, dfq_sc[1:2, :], 0.0))
                dfq_ref[0] = rows.T

    in_specs = [pl.BlockSpec((T, wq), lambda p, j: (0, qblk + p)),
                pl.BlockSpec((tk, wq), lambda p, j: (j, kblk + p)),
                pl.BlockSpec((tk, 128), lambda p, j: (j, vblk + p)),
                pl.BlockSpec((T, 128), lambda p, j: (0, p)),
                pl.BlockSpec((T, 128), lambda p, j: (0, p)),
                pl.BlockSpec((1, 8, T), lambda p, j: (p, 0, 0))]
    args = [q, k, v, do, o, lrow]
    out_specs = [pl.BlockSpec((T, wq), lambda p, j: (0, p)),
                 pl.BlockSpec((tk, wq), lambda p, j: (j, p)), pl.BlockSpec((tk, 128), lambda p, j: (j, p))]
    out_shape = [jax.ShapeDtypeStruct((T, 2 * wq), F32), jax.ShapeDtypeStruct((T, 2 * wq), F32),
                 jax.ShapeDtypeStruct((T, 256), F32)]
    scratch = [pltpu.VMEM((nq, tk, 128), F32), pltpu.VMEM((tk, 128), F32), pltpu.VMEM((nq, 128, T), F32),
               pltpu.VMEM((8, T), F32)]
    if has_f:
        in_specs += [pl.BlockSpec((1, tk, 128), lambda p, j: (p, j, 0)),
                     pl.BlockSpec((1, 8, T), lambda p, j: (p, 0, 0))]
        args += [fcol, frow]
        out_specs += [pl.BlockSpec((1, tk, 128), lambda p, j: (p, j, 0)),
                      pl.BlockSpec((1, T, 128), lambda p, j: (p, 0, 0))]
        out_shape += [jax.ShapeDtypeStruct((2, T, 128), F32), jax.ShapeDtypeStruct((2, T, 128), F32)]
        scratch += [pltpu.VMEM((2, tk, 128), F32), pltpu.VMEM((8, T), F32)]
    return _call_hosting(body, name, (2, T // tk), args, in_specs, out_specs, out_shape, scratch, exch)


def _swa_align(pair, e, h):
    sel = jnp.where(_half_mask(e), pair, 0.0)
    if e == h:
        return sel
    return pltpu.roll(sel, HEAD_DIM, 1)


def _swa_mask(n):
    W = WINDOW
    qi = lax.broadcasted_iota(jnp.int32, (W, 2 * W), 0) + W
    kj = lax.broadcasted_iota(jnp.int32, (W, 2 * W), 1)
    dist = qi - kj
    return (dist >= 0) & (dist < W) & ((n > 0) | (kj >= W))


def swa_fwd(proj, bias, sinks, exch=None):
    T = proj.shape[0]
    W = WINDOW
    nb = T // W
    scale = HEAD_DIM ** -0.5

    def body(sink_ref, q_ref, kp_ref, kc_ref, vp_ref, vc_ref, b_ref, o_ref, l_ref):
        n = pl.program_id(0)
        mask = _swa_mask(n)
        kband = jnp.concatenate([kp_ref[...], kc_ref[...]], axis=0).astype(MXU)
        vband = jnp.concatenate([vp_ref[...], vc_ref[...]], axis=0).astype(MXU)
        lane = lax.broadcasted_iota(jnp.int32, (1, 128), 1)
        lse_tile = jnp.zeros((W, 128), F32)
        pairs = []
        for h in range(2):
            full = []
            for g in range(4):
                hq = 4 * h + g
                qa = _swa_align(q_ref[:, 128 * (hq // 2):128 * (hq // 2 + 1)], hq % 2, h)
                s = _dot_nt(qa, kband) * scale + b_ref[hq]
                s = jnp.where(mask, s, NEG)
                sink = sink_ref[hq]
                m = jnp.maximum(jnp.max(s, axis=1, keepdims=True), sink)
                e = jnp.exp(s - m)
                l = jnp.sum(e, axis=1, keepdims=True) + jnp.exp(sink - m)
                r = jnp.where(_half_mask(h), _dot(e, vband), 0.0) / l
                full.append(r + pltpu.roll(r, HEAD_DIM, 1))
                lse_tile = jnp.where(lane == hq, m + jnp.log(l), lse_tile)
            pairs.append(jnp.where(_half_mask(0), full[0], full[1]))
            pairs.append(jnp.where(_half_mask(0), full[2], full[3]))
        o_ref[...] = jnp.concatenate(pairs, axis=1)
        l_ref[...] = lse_tile

    prev = lambda n: (jnp.maximum(n - 1, 0), C_KA // 128)
    cur = lambda n: (n, C_KA // 128)
    prev_v = lambda n: (jnp.maximum(n - 1, 0), C_VA // 128)
    cur_v = lambda n: (n, C_VA // 128)
    return _call_hosting(
        body, "swa_fwd", (nb,), [sinks, proj, proj, proj, proj, proj, bias],
        [pl.BlockSpec(memory_space=pltpu.SMEM),
         pl.BlockSpec((W, 512), lambda n: (n, 0)),
         pl.BlockSpec((W, 128), prev), pl.BlockSpec((W, 128), cur),
         pl.BlockSpec((W, 128), prev_v), pl.BlockSpec((W, 128), cur_v),
         pl.BlockSpec((8, W, 2 * W), lambda n: (0, 0, 0))],
        [pl.BlockSpec((W, 512), lambda n: (n, 0)), pl.BlockSpec((W, 128), lambda n: (n, 0))],
        [jax.ShapeDtypeStruct((T, 512), F32), jax.ShapeDtypeStruct((T, 128), F32)], [], exch)


def swa_bwd(proj, bias, sinks, do, o, lse, exch=None):
    T = proj.shape[0]
    W = WINDOW
    nb = T // W
    scale = HEAD_DIM ** -0.5

    def body(sink_ref, q_ref, kp_ref, kc_ref, vp_ref, vc_ref, b_ref, do_ref, o_ref, l_ref,
             dq_ref, dk_ref, dv_ref, db_ref, dsk_ref, ck, cv):
        n = pl.program_id(0)

        @pl.when(n == 0)
        def _():
            ck[...] = jnp.zeros_like(ck)
            cv[...] = jnp.zeros_like(cv)
            db_ref[...] = jnp.zeros_like(db_ref)
            dsk_ref[...] = jnp.zeros_like(dsk_ref)

        @pl.when(n < nb)
        def _():
            mask = _swa_mask(n)
            kb32 = jnp.concatenate([kp_ref[...], kc_ref[...]], axis=0)
            vb32 = jnp.concatenate([vp_ref[...], vc_ref[...]], axis=0)
            kband = kb32.astype(MXU)
            sub = lax.broadcasted_iota(jnp.int32, (8, 1), 0)
            dk_band = jnp.zeros((2 * W, 128), F32)
            dv_band = jnp.zeros((2 * W, 128), F32)
            dsk = jnp.zeros((8, 128), F32)
            dq_pairs = []
            mask4 = jnp.concatenate([mask] * 4, axis=0)
            for h in range(2):
                hm = _half_mask(h)
                km = jnp.where(hm, kb32, 0.0).astype(MXU)
                vm = jnp.where(hm, vb32, 0.0).astype(MXU)
                pbs = [slice(128 * ((4 * h + g) // 2), 128 * ((4 * h + g) // 2 + 1)) for g in range(4)]
                q4 = jnp.concatenate([_swa_align(q_ref[:, pbs[g]], g % 2, h) for g in range(4)], axis=0)
                do4 = jnp.concatenate([_swa_align(do_ref[:, pbs[g]], g % 2, h) for g in range(4)], axis=0)
                D4 = jnp.concatenate(
                    [jnp.sum(jnp.where(_half_mask(g % 2), do_ref[:, pbs[g]] * o_ref[:, pbs[g]], 0.0), axis=1,
                             keepdims=True) for g in range(4)], axis=0)
                lse4 = jnp.concatenate([_lane_pick(l_ref[...], 4 * h + g) for g in range(4)], axis=0)
                sink4 = jnp.concatenate([jnp.full((W, 1), sink_ref[4 * h + g], F32) for g in range(4)], axis=0)
                s = _dot_nt(q4, kband) * scale + b_ref[4 * h:4 * h + 4].reshape(4 * W, 2 * W)
                p = jnp.where(mask4, jnp.exp(s - lse4), 0.0)
                sd = jnp.exp(sink4 - lse4) * D4
                for g in range(4):
                    dsk = dsk + jnp.where(sub == 4 * h + g,
                                          -jnp.sum(sd[W * g:W * (g + 1)], axis=0, keepdims=True), 0.0)
                ds = p * (_dot_nt(do4, vm) - D4)
                db_ref[4 * h:4 * h + 4] += ds.reshape(4, W, 2 * W)
                dq = _dot(ds, km) * scale
                dq = dq + pltpu.roll(dq, HEAD_DIM, 1)
                dk_band = dk_band + _dot(ds.T, q4) * scale
                dv_band = dv_band + _dot(p.T, do4)
                dq_pairs.append(jnp.where(_half_mask(0), dq[0:W], dq[W:2 * W]))
                dq_pairs.append(jnp.where(_half_mask(0), dq[2 * W:3 * W], dq[3 * W:4 * W]))
            dq_ref[...] = jnp.concatenate(dq_pairs, axis=1)
            dsk_ref[...] += dsk
            dk_ref[...] = ck[...] + dk_band[0:W]
            dv_ref[...] = cv[...] + dv_band[0:W]
            ck[...] = dk_band[W:2 * W]
            cv[...] = dv_band[W:2 * W]

        @pl.when(n == nb)
        def _():
            dk_ref[...] = ck[...]
            dv_ref[...] = cv[...]

    cl = lambda n: jnp.minimum(n, nb - 1)
    pv = lambda n: jnp.maximum(jnp.minimum(n, nb - 1) - 1, 0)
    return _call_hosting(
        body, "swa_bwd", (nb + 1,), [sinks, proj, proj, proj, proj, proj, bias, do, o, lse],
        [pl.BlockSpec(memory_space=pltpu.SMEM),
         pl.BlockSpec((W, 512), lambda n: (cl(n), 0)),
         pl.BlockSpec((W, 128), lambda n: (pv(n), C_KA // 128)),
         pl.BlockSpec((W, 128), lambda n: (cl(n), C_KA // 128)),
         pl.BlockSpec((W, 128), lambda n: (pv(n), C_VA // 128)),
         pl.BlockSpec((W, 128), lambda n: (cl(n), C_VA // 128)),
         pl.BlockSpec((8, W, 2 * W), lambda n: (0, 0, 0)),
         pl.BlockSpec((W, 512), lambda n: (cl(n), 0)),
         pl.BlockSpec((W, 512), lambda n: (cl(n), 0)),
         pl.BlockSpec((W, 128), lambda n: (cl(n), 0))],
        [pl.BlockSpec((W, 512), lambda n: (cl(n), 0)),
         pl.BlockSpec((W, 128), lambda n: (jnp.maximum(n - 1, 0), 0)),
         pl.BlockSpec((W, 128), lambda n: (jnp.maximum(n - 1, 0), 0)),
         pl.BlockSpec((8, W, 2 * W), lambda n: (0, 0, 0)),
         pl.BlockSpec((8, 128), lambda n: (0, 0))],
        [jax.ShapeDtypeStruct((T, 512), F32), jax.ShapeDtypeStruct((T, 128), F32),
         jax.ShapeDtypeStruct((T, 128), F32), jax.ShapeDtypeStruct((8, W, 2 * W), F32),
         jax.ShapeDtypeStruct((8, 128), F32)],
        [pltpu.VMEM((W, 128), F32), pltpu.VMEM((W, 128), F32)], exch)


def swa_bias_table(rel_bias):
    W = WINDOW
    qi = jnp.arange(W, dtype=jnp.int32)[:, None] + W
    kj = jnp.arange(2 * W, dtype=jnp.int32)[None, :]
    dist = qi - kj
    max_exact = REL_BUCKETS // 2
    d = jnp.maximum(dist, 0)
    log_ratio = jnp.log(jnp.maximum(d, 1).astype(F32) / max_exact) / math.log(REL_MAX_DIST / max_exact)
    large = jnp.minimum(max_exact + (log_ratio * (REL_BUCKETS - max_exact)).astype(jnp.int32), REL_BUCKETS - 1)
    bucket = jnp.where(d < max_exact, d, large)
    bucket = bucket.reshape(-1)
    onehot = (bucket[None, :] == jnp.arange(REL_BUCKETS, dtype=jnp.int32)[:, None]).astype(F32)
    bias = jnp.dot(rel_bias.astype(F32).T, onehot, precision=lax.Precision.HIGHEST)
    return bias.reshape(SWA_Q_HEADS, W, 2 * W), bucket


def attn_out(oa, ob, oc, gn, wout, gpost, x):
    T = x.shape[0]
    tm = _tile(T, 512)

    def body(oa_ref, ob_ref, oc_ref, gn_ref, w_ref, gp_ref, x_ref, x2_ref, y_ref, mT_ref):
        g = gn_ref[...]
        mixed = jnp.concatenate([_rms_fwd(oa_ref[...], g[:, 0:512]), _rms_fwd(ob_ref[...], g[:, 512:768]),
                                 _rms_fwd(oc_ref[...], g[:, 768:1024])], axis=1)
        mT_ref[...] = mixed.T.astype(MXU)
        y = _dot(mixed, w_ref[...])
        y_ref[...] = y
        x2_ref[...] = x_ref[...] + _rms_fwd(y, gp_ref[...])

    row = lambda i: (i, 0)
    const = lambda i: (0, 0)
    return pl.pallas_call(
        body, name="attn_out", grid=(T // tm,),
        in_specs=[pl.BlockSpec((tm, 512), row), pl.BlockSpec((tm, 256), row), pl.BlockSpec((tm, 256), row),
                  pl.BlockSpec((1, 1024), const), pl.BlockSpec((1024, 1024), const), pl.BlockSpec((1, 1024), const),
                  pl.BlockSpec((tm, 1024), row)],
        out_specs=[pl.BlockSpec((tm, 1024), row), pl.BlockSpec((tm, 1024), row),
                   pl.BlockSpec((1024, tm), lambda i: (0, i))],
        out_shape=[jax.ShapeDtypeStruct((T, 1024), F32), jax.ShapeDtypeStruct((T, 1024), F32),
                   jax.ShapeDtypeStruct((1024, T), MXU)],
        compiler_params=_cparams(("parallel",)),
    )(oa, ob, oc, gn, wout, gpost, x)


def attn_out_bwd(dx2, y, oa, ob, oc, gn, wout, gpost):
    T = dx2.shape[0]
    tm = _tile(T, 512)

    def body(dx_ref, y_ref, oa_ref, ob_ref, oc_ref, gn_ref, w_ref, gp_ref,
             dy_ref, da_ref, db_ref, dc_ref, dgn_ref, dgp_ref):
        first = pl.program_id(0) == 0
        dy, dgp = _rms_bwd(dx_ref[...], y_ref[...], gp_ref[...])
        dy_ref[...] = dy.astype(MXU)
        _acc_out(dgp_ref, dgp, first)
        dm = _dot_nt(dy, w_ref[...])
        g = gn_ref[...]
        da, dga = _rms_bwd(dm[:, 0:512], oa_ref[...], g[:, 0:512])
        db, dgb = _rms_bwd(dm[:, 512:768], ob_ref[...], g[:, 512:768])
        dc, dgc = _rms_bwd(dm[:, 768:1024], oc_ref[...], g[:, 768:1024])
        da_ref[...] = da
        db_ref[...] = db
        dc_ref[...] = dc
        _acc_out(dgn_ref, jnp.concatenate([dga, dgb, dgc], axis=1), first)

    row = lambda i: (i, 0)
    const = lambda i: (0, 0)
    return pl.pallas_call(
        body, name="attn_out_bwd", grid=(T // tm,),
        in_specs=[pl.BlockSpec((tm, 1024), row), pl.BlockSpec((tm, 1024), row),
                  pl.BlockSpec((tm, 512), row), pl.BlockSpec((tm, 256), row), pl.BlockSpec((tm, 256), row),
                  pl.BlockSpec((1, 1024), const), pl.BlockSpec((1024, 1024), const), pl.BlockSpec((1, 1024), const)],
        out_specs=[pl.BlockSpec((tm, 1024), row), pl.BlockSpec((tm, 512), row), pl.BlockSpec((tm, 256), row),
                   pl.BlockSpec((tm, 256), row), pl.BlockSpec((1, 1024), const), pl.BlockSpec((1, 1024), const)],
        out_shape=[jax.ShapeDtypeStruct((T, 1024), MXU), jax.ShapeDtypeStruct((T, 512), F32),
                   jax.ShapeDtypeStruct((T, 256), F32), jax.ShapeDtypeStruct((T, 256), F32),
                   jax.ShapeDtypeStruct((1, 1024), F32), jax.ShapeDtypeStruct((1, 1024), F32)],
        compiler_params=_cparams(("arbitrary",)),
    )(dx2, y, oa, ob, oc, gn, wout, gpost)


FF_TILE = 256
_GELU_C = math.sqrt(2.0 / math.pi)


def _gelu(x):
    return 0.5 * x * (1.0 + jnp.tanh(_GELU_C * (x + 0.044715 * x * x * x)))


def _gelu_with_grad(x):
    x2 = x * x
    t = jnp.tanh(_GELU_C * x * (1.0 + 0.044715 * x2))
    h = 0.5 * (1.0 + t)
    return x * h, h + (0.5 * _GELU_C) * x * (1.0 - t * t) * (1.0 + (3 * 0.044715) * x2)


def _conv_taps(u, hal_ref, first):
    row = lax.broadcasted_iota(jnp.int32, (8, 1), 0)
    h6 = jnp.where(first, 0.0, hal_ref[6:7, :])
    h7 = jnp.where(first, 0.0, hal_ref[7:8, :])
    r1, r2 = pltpu.roll(u, 1, 0), pltpu.roll(u, 2, 0)
    r1 = jnp.concatenate([jnp.where(row == 0, h7, r1[0:8]), r1[8:]], axis=0)
    r2 = jnp.concatenate([jnp.where(row == 0, h6, jnp.where(row == 1, h7, r2[0:8])), r2[8:]], axis=0)
    return r1, r2


def ffn_fwd(u0, convw, convb, wdown, gpost, x2, exch=None):
    T = x2.shape[0]
    tm, tn = _tile(T, 1024), FF_TILE
    nj = D_FF // tn

    def body(ug_ref, uu_ref, hg_ref, hu_ref, wg_ref, wu_ref, bg_ref, bu_ref, wd_ref, gp_ref, x_ref, wdp_ref,
             x3_ref, y_ref, aT_ref, acc, a_sc):
        i, j = pl.program_id(0), pl.program_id(1)
        first = i == 0

        @pl.when(j == 0)
        def _():
            acc[...] = jnp.zeros_like(acc)
            a_sc[...] = jnp.zeros_like(a_sc)

        acc[...] += _dot(a_sc[...], wdp_ref[...])

        def conv(u_ref, h_ref, w_ref, b_ref):
            u = u_ref[...]
            r1, r2 = _conv_taps(u, h_ref, first)
            return b_ref[...] + w_ref[0:1, :] * r2 + w_ref[1:2, :] * r1 + w_ref[2:3, :] * u

        a = _gelu(conv(ug_ref, hg_ref, wg_ref, bg_ref)) * conv(uu_ref, hu_ref, wu_ref, bu_ref)
        aT_ref[...] = a.T.astype(MXU)
        a_sc[...] = a.astype(MXU)

        @pl.when(j == nj - 1)
        def _():
            y = acc[...] + _dot(a_sc[...], wd_ref[...])
            y_ref[...] = y
            x3_ref[...] = x_ref[...] + _rms_fwd(y, gp_ref[...])

    halo = lambda off: (lambda i, j: (jnp.maximum(i * (tm // 8) - 1, 0), off + j))
    return _call_hosting(
        body, "ffn_fwd", (T // tm, nj), [u0, u0, u0, u0, convw, convw, convb, convb, wdown, gpost, x2, wdown],
        [pl.BlockSpec((tm, tn), lambda i, j: (i, j)), pl.BlockSpec((tm, tn), lambda i, j: (i, nj + j)),
         pl.BlockSpec((8, tn), halo(0)), pl.BlockSpec((8, tn), halo(nj)),
         pl.BlockSpec((3, tn), lambda i, j: (0, j)), pl.BlockSpec((3, tn), lambda i, j: (0, nj + j)),
         pl.BlockSpec((1, tn), lambda i, j: (0, j)), pl.BlockSpec((1, tn), lambda i, j: (0, nj + j)),
         pl.BlockSpec((tn, 1024), lambda i, j: (j, 0)),
         pl.BlockSpec((1, 1024), lambda i, j: (0, 0)),
         pl.BlockSpec((tm, 1024), lambda i, j: (i, 0)),
         pl.BlockSpec((tn, 1024), lambda i, j: (jnp.maximum(j - 1, 0), 0))],
        [pl.BlockSpec((tm, 1024), lambda i, j: (i, 0)), pl.BlockSpec((tm, 1024), lambda i, j: (i, 0)),
         pl.BlockSpec((tn, tm), lambda i, j: (j, i))],
        [jax.ShapeDtypeStruct((T, 1024), F32), jax.ShapeDtypeStruct((T, 1024), F32),
         jax.ShapeDtypeStruct((D_FF, T), MXU)],
        [pltpu.VMEM((tm, 1024), F32), pltpu.VMEM((tm, tn), MXU)], exch)


def ffn_bwd(dx3, y, u0, convw, convb, wdown, gpost, wupT, x2, gfpre, exch=None):
    T = dx3.shape[0]
    tm, tn = _tile(T, 512), FF_TILE
    nj = D_FF // tn
    ni = T // tm

    def body(dx_ref, y_ref, ug_ref, uu_ref, hg_ref, hu_ref, wg_ref, wu_ref, bg_ref, bu_ref, wd_ref, gp_ref,
             wtg_ref, wtu_ref, x2_ref, gf_ref, wdn_ref, wtgp_ref, wtup_ref,
             dy_ref, dug_ref, duu_ref, dcg_ref, dcu_ref, dgp_ref, dx2_ref, dgf_ref,
             dy_sc, dh_sc, da_sc, dug_sc, duu_sc, cg, cu, ag, au):
        s, j = pl.program_id(0), pl.program_id(1)
        i = ni - 1 - s
        first_tok = i == 0
        sub = lax.broadcasted_iota(jnp.int32, (8, 1), 0)
        slot = j % 2

        @pl.when(j == 0)
        def _():
            dy, dgp = _rms_bwd(dx_ref[...], y_ref[...], gp_ref[...])
            dy_sc[...] = dy.astype(MXU)
            dy_ref[...] = dy.astype(MXU)
            _acc_out(dgp_ref, dgp, s == 0)
            dh_sc[...] = jnp.zeros_like(dh_sc)
            da_sc[0] = _dot_nt(dy.astype(MXU), wd_ref[...])
            dug_sc[...] = jnp.zeros_like(dug_sc)
            duu_sc[...] = jnp.zeros_like(duu_sc)

        @pl.when(s == 0)
        def _():
            cg[j] = jnp.zeros((8, tn), F32)
            cu[j] = jnp.zeros((8, tn), F32)
            ag[j] = jnp.zeros((8, tn), F32)
            au[j] = jnp.zeros((8, tn), F32)

        da = da_sc[slot]
        da_sc[1 - slot] = _dot_nt(dy_sc[...], wdn_ref[...])
        dh_sc[...] += _dot(dug_sc[...], wtgp_ref[...]) + _dot(duu_sc[...], wtup_ref[...])

        def conv(u_ref, h_ref, w_ref, b_ref):
            u = u_ref[...]
            r1, r2 = _conv_taps(u, h_ref, first_tok)
            return b_ref[...] + w_ref[0:1, :] * r2 + w_ref[1:2, :] * r1 + w_ref[2:3, :] * u, u, r1, r2

        gate, ugv, g1, g2 = conv(ug_ref, hg_ref, wg_ref, bg_ref)
        up, uuv, u1, u2 = conv(uu_ref, hu_ref, wu_ref, bu_ref)
        gl, dgl = _gelu_with_grad(gate)
        dup = da * gl
        dgate = da * up * dgl

        def conv_bwd(du, u, r1, r2, w_ref, c_ref, a_ref, duT_ref, du_sc):
            nxt = c_ref[j]
            n0, n1 = nxt[0:1, :], nxt[1:2, :]
            f1, f2 = pltpu.roll(du, tm - 1, 0), pltpu.roll(du, tm - 2, 0)
            f1 = jnp.concatenate([f1[:tm - 8], jnp.where(sub == 7, n0, f1[tm - 8:])], axis=0)
            f2 = jnp.concatenate([f2[:tm - 8], jnp.where(sub == 7, n1, jnp.where(sub == 6, n0, f2[tm - 8:]))], axis=0)
            du0 = w_ref[2:3, :] * du + w_ref[1:2, :] * f1 + w_ref[0:1, :] * f2
            duT_ref[...] = du0.T.astype(MXU)
            du_sc[...] = du0.astype(MXU)
            c_ref[j] = du[0:8, :]
            red = lambda v: jnp.sum(v, axis=0, keepdims=True)
            part = jnp.where(sub == 0, red(du * r2), jnp.where(sub == 1, red(du * r1), jnp.where(
                sub == 2, red(du * u), jnp.where(sub == 3, red(du), 0.0))))
            a_ref[j] = a_ref[j] + part
            return a_ref[j]

        dcg_ref[0] = conv_bwd(dgate, ugv, g1, g2, wg_ref, cg, ag, dug_ref, dug_sc)
        dcu_ref[0] = conv_bwd(dup, uuv, u1, u2, wu_ref, cu, au, duu_ref, duu_sc)

        @pl.when(j == nj - 1)
        def _():
            dh = dh_sc[...] + _dot(dug_sc[...], wtg_ref[...]) + _dot(duu_sc[...], wtu_ref[...])
            dx, dgf = _rms_bwd(dh, x2_ref[...], gf_ref[...])
            dx2_ref[...] = dx_ref[...] + dx
            _acc_out(dgf_ref, dgf, s == 0)

    rev = lambda s: ni - 1 - s
    halo = lambda off: (lambda s, j: (jnp.maximum(rev(s) * (tm // 8) - 1, 0), off + j))
    tok = pl.BlockSpec((tm, 1024), lambda s, j: (rev(s), 0))
    vec = pl.BlockSpec((1, 1024), lambda s, j: (0, 0))
    return _call_hosting(
        body, "ffn_bwd", (ni, nj),
        [dx3, y, u0, u0, u0, u0, convw, convw, convb, convb, wdown, gpost, wupT, wupT, x2, gfpre,
         wdown, wupT, wupT],
        [tok, tok,
         pl.BlockSpec((tm, tn), lambda s, j: (rev(s), j)), pl.BlockSpec((tm, tn), lambda s, j: (rev(s), nj + j)),
         pl.BlockSpec((8, tn), halo(0)), pl.BlockSpec((8, tn), halo(nj)),
         pl.BlockSpec((3, tn), lambda s, j: (0, j)), pl.BlockSpec((3, tn), lambda s, j: (0, nj + j)),
         pl.BlockSpec((1, tn), lambda s, j: (0, j)), pl.BlockSpec((1, tn), lambda s, j: (0, nj + j)),
         pl.BlockSpec((tn, 1024), lambda s, j: (j, 0)), vec,
         pl.BlockSpec((tn, 1024), lambda s, j: (j, 0)), pl.BlockSpec((tn, 1024), lambda s, j: (nj + j, 0)),
         tok, vec,
         pl.BlockSpec((tn, 1024), lambda s, j: (jnp.minimum(j + 1, nj - 1), 0)),
         pl.BlockSpec((tn, 1024), lambda s, j: (jnp.maximum(j - 1, 0), 0)),
         pl.BlockSpec((tn, 1024), lambda s, j: (nj + jnp.maximum(j - 1, 0), 0))],
        [tok,
         pl.BlockSpec((tn, tm), lambda s, j: (j, rev(s))), pl.BlockSpec((tn, tm), lambda s, j: (j, rev(s))),
         pl.BlockSpec((1, 8, tn), lambda s, j: (s, 0, j)), pl.BlockSpec((1, 8, tn), lambda s, j: (s, 0, j)),
         vec, tok, vec],
        [jax.ShapeDtypeStruct((T, 1024), MXU), jax.ShapeDtypeStruct((D_FF, T), MXU),
         jax.ShapeDtypeStruct((D_FF, T), MXU),
         jax.ShapeDtypeStruct((ni, 8, D_FF), F32), jax.ShapeDtypeStruct((ni, 8, D_FF), F32),
         jax.ShapeDtypeStruct((1, 1024), F32), jax.ShapeDtypeStruct((T, 1024), F32),
         jax.ShapeDtypeStruct((1, 1024), F32)],
        [pltpu.VMEM((tm, 1024), MXU), pltpu.VMEM((tm, 1024), F32), pltpu.VMEM((2, tm, tn), F32),
         pltpu.VMEM((tm, tn), MXU), pltpu.VMEM((tm, tn), MXU)] + [pltpu.VMEM((nj, 8, tn), F32)] * 4, exch)


ELEMS_PER_BLOCK = 512 * 1024


def _row_block(R, C):
    if R * C <= ELEMS_PER_BLOCK or R % 8:
        return R
    best = 8
    for t in range(8, R + 1, 8):
        if R % t == 0 and t * C <= ELEMS_PER_BLOCK:
            best = t
    return best


def adamw(w, g, m, v, name, exch=None):
    L, R, C = w.shape
    partials = isinstance(g, (list, tuple))
    tr = _row_block(R, 2 * C)
    c1 = 1.0 - ADAM_B1 ** ADAM_STEP
    c2 = 1.0 - ADAM_B2 ** ADAM_STEP

    def body(w_ref, *rest):
        g_refs, (m_ref, v_ref, g_out, d_ref, nm_ref, nv_ref) = rest[:-6], rest[-6:]

        def step(gv):
            g_out[0] = gv
            nm = ADAM_B1 * m_ref[0] + (1.0 - ADAM_B1) * gv
            nv = ADAM_B2 * v_ref[0] + (1.0 - ADAM_B2) * (gv * gv)
            nm_ref[0] = nm
            nv_ref[0] = nv
            d_ref[0] = -ADAM_LR * ((nm / c1) / (jnp.sqrt(nv / c2) + ADAM_EPS) + ADAM_WD * w_ref[0])

        if not partials:
            step(g_refs[0][0])
            return
        for k in range(L):
            @pl.when(pl.program_id(0) == k)
            def _(k=k):
                gv = g_refs[k][0].astype(F32)
                for d in range(1, N_DEV):
                    gv = gv + g_refs[k][d].astype(F32)
                step(gv)

    spec = pl.BlockSpec((1, tr, C), lambda l, i: (l, i, 0))
    if partials:
        gspecs = [pl.BlockSpec((N_DEV, tr, C), lambda l, i, k=k: (0, jnp.where(l == k, i, 0), 0)) for k in range(L)]
        gs = list(g)
    else:
        gspecs, gs = [spec], [g]
    return _call_hosting(body, name, (L, R // tr), [w] + gs + [m, v], [spec] + gspecs + [spec, spec], [spec] * 4,
                         [jax.ShapeDtypeStruct((L, R, C), F32)] * 4, [], exch)


def sum_devices(buf, name):
    _, R, C = buf.shape
    tr = _row_block(R, C * 4)

    def body(b_ref, o_ref):
        acc = b_ref[0].astype(F32)
        for d in range(1, N_DEV):
            acc = acc + b_ref[d].astype(F32)
        o_ref[...] = acc

    return pl.pallas_call(
        body, name=name, grid=(R // tr,),
        in_specs=[pl.BlockSpec((N_DEV, tr, C), lambda i: (0, i, 0))],
        out_specs=pl.BlockSpec((tr, C), lambda i: (i, 0)),
        out_shape=jax.ShapeDtypeStruct((R, C), F32),
        compiler_params=_cparams(("parallel",)),
    )(buf)


def _exchange_copies(src_refs, out_refs, send_sems, recv_sems, gather):
    x, y, c = lax.axis_index("x"), lax.axis_index("y"), lax.axis_index("c")
    me = 4 * x + 2 * y + c
    flip = lambda a, bit: 1 - a if bit else a
    part = lambda ref, d: ref if gather else ref.at[d]
    copies = []
    for k in range(1, N_DEV):
        px, py, pc = flip(x, (k >> 2) & 1), flip(y, (k >> 1) & 1), flip(c, k & 1)
        peer = 4 * px + 2 * py + pc
        for t in range(len(src_refs)):
            sem = t * (N_DEV - 1) + k - 1
            mk = lambda s, d: pltpu.make_async_remote_copy(
                src_ref=s, dst_ref=d, send_sem=send_sems.at[sem], recv_sem=recv_sems.at[sem],
                device_id=(px, py, pc), device_id_type=pl.DeviceIdType.MESH)
            copies.append((mk(part(src_refs[t], peer), out_refs[t].at[me]),
                           mk(part(src_refs[t], me), out_refs[t].at[peer])))
    return me, copies


def exchange(srcs, name, gather):
    n = len(srcs)
    shapes = [(N_DEV,) + s.shape if gather else s.shape for s in srcs]

    def body(*refs):
        src_refs, out_refs = refs[:n], refs[n:2 * n]
        send_sems, recv_sems, local_sems = refs[2 * n:]
        me, copies = _exchange_copies(src_refs, out_refs, send_sems, recv_sems, gather)
        for outgoing, _ in copies:
            outgoing.start()
        mine = [pltpu.make_async_copy(src_refs[t] if gather else src_refs[t].at[me], out_refs[t].at[me],
                                      local_sems.at[t]) for t in range(n)]
        for cp in mine:
            cp.start()
        for _, incoming in copies:
            incoming.wait_recv()
        for outgoing, _ in copies:
            outgoing.wait_send()
        for cp in mine:
            cp.wait()

    return pl.pallas_call(
        body, name=name,
        in_specs=[pl.BlockSpec(memory_space=pl.ANY)] * n, out_specs=[pl.BlockSpec(memory_space=pl.ANY)] * n,
        out_shape=[jax.ShapeDtypeStruct(shp, s.dtype) for shp, s in zip(shapes, srcs)],
        scratch_shapes=[pltpu.SemaphoreType.DMA((n * (N_DEV - 1),)), pltpu.SemaphoreType.DMA((n * (N_DEV - 1),)),
                        pltpu.SemaphoreType.DMA((n,))],
    )(*srcs)


def hosted_exchange(body, n_in, n_out, n_scratch, grid, srcs, gather):
    n = len(srcs)
    shapes = [(N_DEV,) + s.shape if gather else s.shape for s in srcs]

    def wrapped(*refs):
        ins, xin = refs[:n_in], refs[n_in:n_in + n]
        outs = refs[n_in + n:n_in + n + n_out]
        xout = refs[n_in + n + n_out:n_in + 2 * n + n_out]
        rest = refs[n_in + 2 * n + n_out:]
        scratch, (send_sems, recv_sems, local_sems) = rest[:n_scratch], rest[n_scratch:]
        ids = [pl.program_id(a) for a in range(len(grid))]
        first = functools.reduce(jnp.logical_and, [i == 0 for i in ids])
        last = functools.reduce(jnp.logical_and, [i == g - 1 for i, g in zip(ids, grid)])
        me, copies = _exchange_copies(xin, xout, send_sems, recv_sems, gather)
        mine = [pltpu.make_async_copy(xin[t] if gather else xin[t].at[me], xout[t].at[me], local_sems.at[t])
                for t in range(n)]

        @pl.when(first)
        def _():
            for outgoing, _ in copies:
                outgoing.start()
            for cp in mine:
                cp.start()

        body(*ins, *outs, *scratch)

        @pl.when(last)
        def _():
            for _, incoming in copies:
                incoming.wait_recv()
            for outgoing, _ in copies:
                outgoing.wait_send()
            for cp in mine:
                cp.wait()

    any_spec = pl.BlockSpec(memory_space=pl.ANY)
    return wrapped, (list(srcs), [any_spec] * n, [any_spec] * n,
                     [jax.ShapeDtypeStruct(shp, s.dtype) for shp, s in zip(shapes, srcs)],
                     [pltpu.SemaphoreType.DMA((n * (N_DEV - 1),)), pltpu.SemaphoreType.DMA((n * (N_DEV - 1),)),
                      pltpu.SemaphoreType.DMA((n,))])


def _pack(parts, cols, row_align, dtype):
    flat = jnp.concatenate([p.astype(dtype) for p in parts], axis=-1)
    n = flat.shape[-1]
    block = cols * row_align
    total = -(-n // block) * block
    flat = jnp.pad(flat, [(0, 0)] * (flat.ndim - 1) + [(0, total - n)])
    return flat.reshape(flat.shape[:-1] + (total // cols, cols))


def _unpack(buf, shapes):
    lead = buf.shape[:-2]
    flat = buf.reshape(lead + (-1,))
    out, off = [], 0
    for s in shapes:
        n = int(np.prod(s))
        out.append(flat[..., off:off + n].reshape(lead + tuple(s)))
        off += n
    return out


SHARDED = ["w_in", "w_uq", "w_ukv", "w_out", "w_up", "w_down"]
ATTN_SENT = ["w_in_p", "w_uq", "w_ukv", "w_out"]
UP_HALF = 352
FFN_SIDE = ["w_upT", "conv_w", "w_down"]


def _full_from_shards(name, s):
    if name in ("w_in", "w_in_p", "w_out", "w_down", "w_upT"):
        return s.reshape((-1, s.shape[-1]))
    return s.transpose(1, 0, 2).reshape((s.shape[1], -1))


def _shards_from_full(name, f):
    if name in ("w_in", "w_in_p", "w_out", "w_down", "w_upT"):
        return f.reshape((N_DEV, -1, f.shape[-1]))
    return f.reshape((f.shape[0], N_DEV, -1)).transpose(1, 0, 2)


def _perm_w_in(w):
    z = lambda n: jnp.zeros(w.shape[:-1] + (n,), w.dtype)
    return jnp.concatenate([w[..., :1536], w[..., 1540:1924], w[..., 1536:1540], z(60), w[..., 1924:1956], z(32)],
                           axis=-1)


def _unperm_w_in(d):
    return jnp.concatenate([d[..., :1536], d[..., 1920:1924], d[..., 1536:1920], d[..., 1984:2016]], axis=-1)


def _perm_w_uq(w):
    return jnp.pad(w.reshape(256, 4, MLA_QK_DIM), ((0, 0), (0, 0), (0, 128 - MLA_QK_DIM))).reshape(256, 512)


def _unperm_w_uq(d):
    return d.reshape(256, 4, 128)[:, :, :MLA_QK_DIM].reshape(256, 4 * MLA_QK_DIM)


def _perm_w_ukv(w):
    w4 = w.reshape(128, 4, 128)
    k = jnp.pad(w4[:, :, :64], ((0, 0), (0, 0), (0, 64))).reshape(128, 512)
    return jnp.concatenate([k, w4[:, :, 64:].reshape(128, 256)], axis=1)


def _unperm_w_ukv(d):
    dk = d[:, :512].reshape(128, 4, 128)[:, :, :64]
    dv = d[:, 512:].reshape(128, 4, 64)
    return jnp.concatenate([dk, dv], axis=-1).reshape(128, 512)


def _row(v, width=None):
    v = v.reshape(1, -1).astype(F32)
    if width is not None and v.shape[1] < width:
        v = jnp.pad(v, ((0, 0), (0, width - v.shape[1])))
    return v


def _layer_fwd(x, P, shared, send=None, ffn_from=None):
    cosr, sinr, bias = shared
    ex = lambda host: (send[host], True) if send is not None and send.get(host) else None
    proj, hT, projb = norm_matmul(x, P["g_pre"], P["w_in_p"], "in_proj", lo_tiles=2, tn_pref=1024)
    qm, km, vm, cqT, ckvT = mla_prep(proj, P["gq"], P["gkv"], P["w_uq_p"], P["w_ukv_p"], cosr, sinr)
    fcol, frow, frep = fox_gate(proj, P["fbias"])
    (oa, lse_a), _ = swa_fwd(proj, bias, P["sinks"])
    (ob, lrb), got_fox = flash_fwd(projb, projb, projb, frep, frow, qblk=C_QF // 128, kblk=C_KF // 128,
                                   vblk=C_VF // 128, nq=1, scale=HEAD_DIM ** -0.5, name="fox_fwd", exch=ex("fox"))
    (oc, lrc), got_mla = flash_fwd(qm, km, vm, None, None, qblk=0, kblk=0, vblk=0, nq=2,
                                   scale=MLA_QK_DIM ** -0.5, name="mla_fwd", exch=ex("mla"))
    x2, y1, mT = attn_out(oa, ob, oc, P["gn"], P["w_out"], P["g_apost"], x)
    if ffn_from is not None:
        P = dict(P, **ffn_from(None, got_fox, got_mla))
    up = norm_matmul(x2, P["g_fpre"], P["w_upT"], "up_proj", tn_pref=1536, w_transposed=True,
                     h_transposed=False, exch=ex("up"))
    (u0, h2), got_up = up if ex("up") is not None else (up, None)
    if ffn_from is not None:
        P = dict(P, **ffn_from(got_up, None, None))
    (x3, y2, aT), got_ffn = ffn_fwd(u0, P["conv_w"], P["conv_b"], P["w_down"], P["g_fpost"], x2, exch=ex("ffn"))
    S = dict(x=x, proj=proj, projb=projb, hT=hT, qm=qm, km=km, vm=vm, cqT=cqT, ckvT=ckvT, fcol=fcol, frow=frow,
             oa=oa, lse_a=lse_a, ob=ob, lrb=lrb, oc=oc, lrc=lrc,
             x2=x2, y1=y1, mT=mT, u0=u0, h2=h2, y2=y2, aT=aT)
    return x3, S, P, got_ffn


def _layer_bwd(dx3, P, S, shared, send_attn=None, own_small=False):
    cosr, sinr, bias = shared
    proj = S["proj"]
    G = {}
    got = {}
    ex = lambda arrays: (arrays, False) if send_attn is not None and arrays else None
    (dy2, dugT, duuT, dcg, dcu, G["ffn_post_norm"], dx2, G["ffn_pre_norm"]), got["ffn"] = ffn_bwd(
        dx3, S["y2"], S["u0"], P["conv_w"], P["conv_b"], P["w_down"], P["g_fpost"], P["w_upT"], S["x2"], P["g_fpre"],
        exch=ex(send_attn))
    dconv = jnp.concatenate([dcg[-1], dcu[-1]], axis=1)
    G["conv_w"], G["conv_b"] = dconv[0:3], dconv[3]
    G["w_down"] = matmul_nn(S["aT"], dy2, "dw_down", MXU)
    G["w_upT"] = jnp.concatenate([matmul_nn(dugT, S["h2"], "dw_up_gate", MXU),
                                  matmul_nn(duuT, S["h2"], "dw_up_up", MXU)], axis=0)
    G["w_up"] = G["w_upT"].T
    dy1, doa, dob, doc, G["group_norm"], G["attn_post_norm"] = attn_out_bwd(
        dx2, S["y1"], S["oa"], S["ob"], S["oc"], P["gn"], P["w_out"], P["g_apost"])
    G["w_out"] = matmul_nn(S["mT"], dy1, "dw_out", MXU)
    up_slices = _shards_from_full("w_upT", G["w_upT"])
    (dqa, dka, dva, dbias, dsk), _ = swa_bwd(proj, bias, P["sinks"], doa, S["oa"], S["lse_a"])
    G["swa_sinks"] = dsk[:, 0]
    pb = S["projb"]
    (dqf, dkf, dvf, dFk, dFq), got["fox"] = flash_bwd(
        pb, pb, pb, dob, S["ob"], S["lrb"], S["fcol"], S["frow"], name="fox_bwd", qblk=C_QF // 128,
        kblk=C_KF // 128, vblk=C_VF // 128, nq=1, scale=HEAD_DIM ** -0.5,
        exch=ex([up_slices[:, :UP_HALF], _shards_from_full("w_down", G["w_down"])]))
    dmisc_f, dfb = fox_gate_bwd(dFq, dFk, proj, P["fbias"])
    G["forget_bias"] = dfb[0, 0:4]
    (dqm_, dkm_, dvm_), got["mla"] = flash_bwd(
        S["qm"], S["km"], S["vm"], doc, S["oc"], S["lrc"], None, None, name="mla_bwd",
        qblk=0, kblk=0, vblk=0, nq=2, scale=MLA_QK_DIM ** -0.5, exch=ex([up_slices[:, UP_HALF:]]))
    dqm, dkv, dcq, dckv, dmisc_r, G["q_latent_norm"], G["kv_latent_norm"] = mla_prep_bwd(
        dqm_, dkm_, dvm_, proj, P["gq"], P["gkv"], P["w_uq_p"], P["w_ukv_p"], cosr, sinr)
    G["w_uq"] = _unperm_w_uq(matmul_nn(S["cqT"], dqm, "dw_uq", MXU))
    G["w_ukv"] = _unperm_w_ukv(matmul_nn(S["ckvT"], dkv, "dw_ukv", MXU))
    dproj = jnp.concatenate([dqa, dka, dva, dqf, dkf, dvf, dcq, dckv, dmisc_f + dmisc_r], axis=1).astype(MXU)
    if own_small and send_attn is not None:
        G["w_in_p"], got["dw_in"] = matmul_nn(
            S["hT"], dproj, "dw_in", MXU, exch=([_shards_from_full(n, G[n]) for n in ATTN_SENT[1:]], False))
    else:
        G["w_in_p"] = matmul_nn(S["hT"], dproj, "dw_in", MXU)
    G["w_in"] = _unperm_w_in(G["w_in_p"])
    dx, G["attn_pre_norm"] = matmul_nt_normbwd(dproj, P["w_in_p"], S["x"], P["g_pre"], dx2, "in_bwd")
    return dx, G, dbias, got


def _layer_params(l, full, small):
    return dict(
        g_pre=_row(small["attn_pre_norm"][l]),
        w_in_p=full["w_in_p"] if "w_in_p" in full else _perm_w_in(full["w_in"]),
        gq=_row(small["q_latent_norm"][l]), gkv=_row(small["kv_latent_norm"][l]),
        w_uq_p=_perm_w_uq(full["w_uq"]), w_ukv_p=_perm_w_ukv(full["w_ukv"]),
        fbias=_row(small["forget_bias"][l], 128), sinks=small["swa_sinks"][l].astype(F32),
        gn=_row(small["group_norm"][l]), w_out=full["w_out"], g_apost=_row(small["attn_post_norm"][l]),
        g_fpre=_row(small["ffn_pre_norm"][l]), conv_b=_row(small["conv_b"][l]),
        g_fpost=_row(small["ffn_post_norm"][l]),
        **{n: full[n] for n in FFN_SIDE if n in full},
        **({"w_upT": full["w_up"].T} if "w_up" in full else {}))


def _rel_bias_grad(dbias, bucket):
    flat = dbias.reshape(SWA_Q_HEADS, -1)
    hi = flat.astype(MXU)
    lo = (flat - hi.astype(F32)).astype(MXU)
    onehot = (bucket[:, None] == jnp.arange(128, dtype=jnp.int32)[None, :]).astype(MXU)
    r = matmul_nn(jnp.concatenate([hi, lo], axis=0), onehot, "rel_bias_grad")
    return (r[0:8] + r[8:16])[:, :REL_BUCKETS].T


def local_step(x, tgt, fulls, small, comm=None):
    T = x.shape[0]
    cosr, sinr = rope_tables(T)
    bias, bucket = swa_bias_table(small["rel_bias"])
    shared = (cosr, sinr, bias)
    Ps, Ss = [], []
    h, full = x, fulls[0]
    for l in range(DEPTH):
        P = _layer_params(l, full, small)
        if comm:
            h, S, P, got = _layer_fwd(h, P, shared, comm["weight_parts"](l), comm["ffn_from"])
            full = comm["attn_from"](got) if l + 1 < DEPTH else None
        else:
            h, S, P, _ = _layer_fwd(h, P, shared)
            full = fulls[l + 1] if l + 1 < DEPTH else None
        Ps.append(P)
        Ss.append(S)
    dh, sq = loss_kernel(h, tgt)
    grads = [None] * DEPTH
    dbias_sum = None
    pending = [] if comm else None
    for l in reversed(range(DEPTH)):
        dh, grads[l], dbias, got = _layer_bwd(dh, Ps[l], Ss[l], shared, pending, own_small=l == 0)
        dbias_sum = dbias if dbias_sum is None else dbias_sum + dbias
        if comm:
            comm["landed"](l, ["w_down"], got["fox"][1:])
            comm["landed"](l, ["w_upT"], [jnp.concatenate([got["fox"][0], got["mla"][0]], axis=1)])
            if pending:
                comm["landed"](l + 1, ATTN_SENT, got["ffn"])
            if l == 0:
                comm["landed"](0, ATTN_SENT[1:], got["dw_in"])
                pending = [_shards_from_full("w_in_p", grads[0]["w_in_p"])]
            else:
                pending = [_shards_from_full(n, grads[l][n]) for n in ATTN_SENT]
    return sq, dh, grads, _rel_bias_grad(dbias_sum, bucket), pending


WEIGHTS = ['attn_pre_norm', 'w_in', 'forget_bias', 'swa_sinks', 'rel_bias', 'q_latent_norm', 'w_uq',
           'kv_latent_norm', 'w_ukv', 'group_norm', 'w_out', 'attn_post_norm', 'ffn_pre_norm', 'w_up', 'conv_w',
           'conv_b', 'w_down', 'ffn_post_norm']
SMALL_PER_LAYER = ['attn_pre_norm', 'forget_bias', 'swa_sinks', 'q_latent_norm', 'kv_latent_norm', 'group_norm',
                   'attn_post_norm', 'ffn_pre_norm', 'conv_b', 'ffn_post_norm', 'conv_w']


def kernel(x, attn_pre_norm, w_in, forget_bias, swa_sinks, rel_bias, q_latent_norm, w_uq, kv_latent_norm, w_ukv, group_norm, w_out, attn_post_norm, ffn_pre_norm, w_up, conv_w, conv_b, w_down, ffn_post_norm, loss_target, m_attn_pre_norm, m_w_in, m_forget_bias, m_swa_sinks, m_rel_bias, m_q_latent_norm, m_w_uq, m_kv_latent_norm, m_w_ukv, m_group_norm, m_w_out, m_attn_post_norm, m_ffn_pre_norm, m_w_up, m_conv_w, m_conv_b, m_w_down, m_ffn_post_norm, v_attn_pre_norm, v_w_in, v_forget_bias, v_swa_sinks, v_rel_bias, v_q_latent_norm, v_w_uq, v_kv_latent_norm, v_w_ukv, v_group_norm, v_w_out, v_attn_post_norm, v_ffn_pre_norm, v_w_up, v_conv_w, v_conv_b, v_w_down, v_ffn_post_norm):
    W = dict(attn_pre_norm=attn_pre_norm, w_in=w_in, forget_bias=forget_bias, swa_sinks=swa_sinks, rel_bias=rel_bias,
             q_latent_norm=q_latent_norm, w_uq=w_uq, kv_latent_norm=kv_latent_norm, w_ukv=w_ukv,
             group_norm=group_norm, w_out=w_out, attn_post_norm=attn_post_norm, ffn_pre_norm=ffn_pre_norm,
             w_up=w_up, conv_w=conv_w, conv_b=conv_b, w_down=w_down, ffn_post_norm=ffn_post_norm)
    M = dict(attn_pre_norm=m_attn_pre_norm, w_in=m_w_in, forget_bias=m_forget_bias, swa_sinks=m_swa_sinks,
             rel_bias=m_rel_bias, q_latent_norm=m_q_latent_norm, w_uq=m_w_uq, kv_latent_norm=m_kv_latent_norm,
             w_ukv=m_w_ukv, group_norm=m_group_norm, w_out=m_w_out, attn_post_norm=m_attn_post_norm,
             ffn_pre_norm=m_ffn_pre_norm, w_up=m_w_up, conv_w=m_conv_w, conv_b=m_conv_b, w_down=m_w_down,
             ffn_post_norm=m_ffn_post_norm)
    V = dict(attn_pre_norm=v_attn_pre_norm, w_in=v_w_in, forget_bias=v_forget_bias, swa_sinks=v_swa_sinks,
             rel_bias=v_rel_bias, q_latent_norm=v_q_latent_norm, w_uq=v_w_uq, kv_latent_norm=v_kv_latent_norm,
             w_ukv=v_w_ukv, group_norm=v_group_norm, w_out=v_w_out, attn_post_norm=v_attn_post_norm,
             ffn_pre_norm=v_ffn_pre_norm, w_up=v_w_up, conv_w=v_conv_w, conv_b=v_conv_b, w_down=v_w_down,
             ffn_post_norm=v_ffn_post_norm)
    me = 4 * lax.axis_index("x") + 2 * lax.axis_index("y") + lax.axis_index("c")

    def attn_shards(l):
        return [_perm_w_in(w_in[l].astype(MXU))] + [W[n][l].astype(MXU) for n in ATTN_SENT[1:]]

    def weight_parts(l):
        up = jnp.swapaxes(W["w_up"][l], 0, 1).astype(MXU)
        return dict(up=[W["w_down"][l].astype(MXU)], fox=[up[:UP_HALF]], mla=[up[UP_HALF:], conv_w[l]],
                    ffn=attn_shards(l + 1) if l + 1 < DEPTH else [])

    def ffn_from(got_up, got_fox, got_mla):
        if got_up is not None:
            return dict(w_down=_full_from_shards("w_down", got_up[0]))
        return dict(w_upT=_full_from_shards("w_upT", jnp.concatenate([got_fox[0], got_mla[0]], axis=1)),
                    conv_w=got_mla[1].transpose(1, 0, 2).reshape(3, 2 * D_FF))

    def attn_from(got_ffn):
        return {n: _full_from_shards(n, s) for n, s in zip(ATTN_SENT, got_ffn)}

    landed = [{} for _ in range(DEPTH)]

    def on_landed(l, names, arrays):
        landed[l].update(zip(names, arrays))

    comm = dict(weight_parts=weight_parts, ffn_from=ffn_from, attn_from=attn_from, landed=on_landed)
    full0 = dict(zip(ATTN_SENT, map(_full_from_shards, ATTN_SENT, exchange(attn_shards(0), "gather_weights", True))))
    sq, dx, grads, drel, last = local_step(x[0], loss_target[0], [full0], W, comm)
    G, delta, new_m, new_v = {}, {}, {}, {}

    def update(n, exch=None):
        shp = W[n].shape
        if n == "w_up":
            v3 = lambda a: jnp.swapaxes(a, 1, 2)
            back = v3
            g = [landed[l]["w_upT"] for l in range(DEPTH)]
        else:
            v3 = lambda a: a.reshape(shp if len(shp) == 3 else (1,) + shp)
            back = lambda a: a.reshape(shp)
            g = [landed[l][n] for l in range(DEPTH)] if n in SHARDED else v3(G[n])
        (g, d, nm, nv), got = adamw(v3(W[n]), g, v3(M[n]), v3(V[n]), "adamw_" + n, exch)
        G[n], delta[n], new_m[n], new_v[n] = back(g), back(d), back(nm), back(nv)
        return got

    parts, shapes = [], []
    for l in range(DEPTH):
        for n in SMALL_PER_LAYER:
            parts.append(grads[l][n].astype(F32).reshape(-1))
            shapes.append(grads[l][n].shape)
    parts += [drel.reshape(-1), jnp.sum(sq).reshape(1) * (0.5 / D_MODEL)]
    shapes += [drel.shape, (1,)]
    on_landed(0, ATTN_SENT[:1], update("w_up", (last, False)))
    for l in range(DEPTH):
        landed[l]["w_in"] = _unperm_w_in(landed[l]["w_in_p"])
    gathered = update("w_down", ([_pack(parts, 128, 8, F32)], True))[0]
    red = _unpack(sum_devices(gathered, "sum_small"), shapes)
    k = 0
    per = {n: [] for n in SMALL_PER_LAYER}
    for l in range(DEPTH):
        for n in SMALL_PER_LAYER:
            per[n].append(red[k])
            k += 1
    for n in SMALL_PER_LAYER:
        G[n] = jnp.stack(per[n]).reshape((DEPTH, 3, 2 * D_FF) if n == "conv_w" else W[n].shape)
    G["rel_bias"] = red[k]
    loss = red[k + 1][0]
    G["conv_w"] = lax.dynamic_slice_in_dim(G["conv_w"], me * 704, 704, axis=2)

    for n in WEIGHTS:
        if n not in ("w_up", "w_down"):
            update(n)
    return (loss, dx[None], *[G[n] for n in WEIGHTS], *[delta[n] for n in WEIGHTS],
            *[new_m[n] for n in WEIGHTS], *[new_v[n] for n in WEIGHTS])
```

```python
import functools
import math

import numpy as np
import jax
import jax.numpy as jnp
from jax import lax
from jax.experimental import pallas as pl
from jax.experimental.pallas import tpu as pltpu

F32 = jnp.float32
MXU = jnp.bfloat16

N_DEV = 8
DEPTH = 4
D_MODEL = 1024
HEAD_DIM = 64
WINDOW = 128
SWA_Q_HEADS = 8
REL_BUCKETS = 32
REL_MAX_DIST = 128
MLA_QK_DIM = 96
ROPE_DIM = 32
ROPE_THETA = 10000.0
D_FF = 2816
EPS = 1e-6
NEG = -1e30
IN_COLS = 1956
C_QA, C_KA, C_VA = 0, 512, 640
C_QF, C_KF, C_VF = 768, 1024, 1280
C_CQ, C_CKV, C_MISC = 1536, 1792, 1920
ROPE_LANE0 = 64
ADAM_LR, ADAM_B1, ADAM_B2, ADAM_EPS, ADAM_WD, ADAM_STEP = 0.001, 0.9, 0.999, 1e-08, 0.01, 10

VMEM_LIMIT = 56 * 1024 * 1024


def _cparams(sem=None):
    return pltpu.CompilerParams(dimension_semantics=sem, vmem_limit_bytes=VMEM_LIMIT)


def _tile(n, pref):
    if n <= pref:
        return n
    t = pref - pref % 128
    while t >= 128:
        if n % t == 0:
            return t
        t -= 128
    return n


def _dot(a, b):
    return jnp.dot(a.astype(MXU), b.astype(MXU), preferred_element_type=F32)


def _dot_nt(a, b):
    return lax.dot_general(a.astype(MXU), b.astype(MXU), (((1,), (1,)), ((), ())),
                           preferred_element_type=F32)


def _rms_fwd(x, g):
    return x * lax.rsqrt(jnp.mean(x * x, axis=-1, keepdims=True) + EPS) * g


def _rms_bwd(dy, x, g, n=None):
    r = lax.rsqrt(jnp.mean(x * x, axis=-1, keepdims=True) + EPS)
    xh = x * r
    dg = jnp.sum(dy * xh, axis=0, keepdims=True)
    dxh = dy * g
    dx = r * (dxh - xh * jnp.mean(dxh * xh, axis=-1, keepdims=True))
    return dx, dg


def _acc_out(ref, val, first):
    @pl.when(first)
    def _():
        ref[...] = val

    @pl.when(jnp.logical_not(first))
    def _():
        ref[...] += val


def norm_matmul(x, g, w, name, lo_tiles=0, tn_pref=512, w_transposed=False, h_transposed=True, exch=None):
    T, K = x.shape
    N = w.shape[0] if w_transposed else w.shape[1]
    tm, tn = _tile(T, 1024), _tile(N, tn_pref)

    def body(x_ref, g_ref, w_ref, o_ref, hT_ref, *rest):
        h_sc = rest[-1]
        j = pl.program_id(1)

        @pl.when(j == 0)
        def _():
            h = _rms_fwd(x_ref[...], g_ref[...])
            h_sc[...] = h.astype(MXU)
            hT_ref[...] = (h.T if h_transposed else h).astype(MXU)

        r = (_dot_nt if w_transposed else _dot)(h_sc[...], w_ref[...])
        o_ref[...] = r
        if lo_tiles:
            @pl.when(j < lo_tiles)
            def _():
                rest[0][...] = r.astype(MXU)

    h_spec = pl.BlockSpec((K, tm), lambda i, j: (0, i)) if h_transposed else pl.BlockSpec((tm, K), lambda i, j: (i, 0))
    out_specs = [pl.BlockSpec((tm, tn), lambda i, j: (i, j)), h_spec]
    out_shape = [jax.ShapeDtypeStruct((T, N), F32), jax.ShapeDtypeStruct((K, T) if h_transposed else (T, K), MXU)]
    if lo_tiles:
        out_specs.append(pl.BlockSpec((tm, tn), lambda i, j: (i, jnp.minimum(j, lo_tiles - 1))))
        out_shape.append(jax.ShapeDtypeStruct((T, lo_tiles * tn), MXU))
    in_specs = [pl.BlockSpec((tm, K), lambda i, j: (i, 0)),
                pl.BlockSpec((1, K), lambda i, j: (0, 0)),
                pl.BlockSpec((tn, K), lambda i, j: (j, 0)) if w_transposed else
                pl.BlockSpec((K, tn), lambda i, j: (0, j))]
    outs, landed = _call_hosting(body, name, (T // tm, N // tn), [x, g, w], in_specs, out_specs, out_shape,
                                 [pltpu.VMEM((tm, K), MXU)], exch)
    return outs if exch is None else (outs, landed)


def matmul_nn(a, b, name, out_dtype=F32, exch=None):
    M, K = a.shape
    N = b.shape[1]
    tm, tn, tk = _tile(M, 1408), _tile(N, 1536), _tile(K, 1024)
    nk = K // tk

    def body(a_ref, b_ref, o_ref, acc):
        k = pl.program_id(2)
        part = _dot(a_ref[...], b_ref[...])
        _acc_out(acc, part, k == 0)

        @pl.when(k == nk - 1)
        def _():
            o_ref[...] = acc[...].astype(out_dtype)

    outs, landed = _call_hosting(
        body, name, (M // tm, N // tn, nk), [a, b],
        [pl.BlockSpec((tm, tk), lambda i, j, k: (i, k)), pl.BlockSpec((tk, tn), lambda i, j, k: (k, j))],
        [pl.BlockSpec((tm, tn), lambda i, j, k: (i, j))], [jax.ShapeDtypeStruct((M, N), out_dtype)],
        [pltpu.VMEM((tm, tn), F32)], exch)
    return outs[0] if exch is None else (outs[0], landed)


def matmul_nt_normbwd(dy, w, x, g, dres, name):
    T, N = dy.shape
    K = w.shape[0]
    tm, tn = _tile(T, 1024), _tile(N, 1536)
    nj = N // tn

    def body(dy_ref, w_ref, x_ref, g_ref, dres_ref, dx_ref, dg_ref, acc):
        i, j = pl.program_id(0), pl.program_id(1)
        _acc_out(acc, _dot_nt(dy_ref[...], w_ref[...]), j == 0)

        @pl.when(j == nj - 1)
        def _():
            dx, dg = _rms_bwd(acc[...], x_ref[...], g_ref[...])
            dx_ref[...] = dres_ref[...] + dx
            _acc_out(dg_ref, dg, i == 0)

    return pl.pallas_call(
        body, name=name, grid=(T // tm, nj),
        in_specs=[pl.BlockSpec((tm, tn), lambda i, j: (i, j)),
                  pl.BlockSpec((K, tn), lambda i, j: (0, j)),
                  pl.BlockSpec((tm, K), lambda i, j: (i, 0)),
                  pl.BlockSpec((1, K), lambda i, j: (0, 0)),
                  pl.BlockSpec((tm, K), lambda i, j: (i, 0))],
        out_specs=[pl.BlockSpec((tm, K), lambda i, j: (i, 0)),
                   pl.BlockSpec((1, K), lambda i, j: (0, 0))],
        out_shape=[jax.ShapeDtypeStruct((T, K), F32), jax.ShapeDtypeStruct((1, K), F32)],
        scratch_shapes=[pltpu.VMEM((tm, K), F32)],
        compiler_params=_cparams(("arbitrary", "arbitrary")),
    )(dy, w, x, g, dres)


def loss_kernel(y, tgt):
    T, D = y.shape
    tm = _tile(T, 512)

    def body(y_ref, t_ref, dy_ref, acc_ref):
        e = y_ref[...] - t_ref[...]
        dy_ref[...] = e * (1.0 / D)
        _acc_out(acc_ref, jnp.sum(e * e, axis=0, keepdims=True), pl.program_id(0) == 0)

    return pl.pallas_call(
        body, name="loss", grid=(T // tm,),
        in_specs=[pl.BlockSpec((tm, D), lambda i: (i, 0)), pl.BlockSpec((tm, D), lambda i: (i, 0))],
        out_specs=[pl.BlockSpec((tm, D), lambda i: (i, 0)), pl.BlockSpec((1, D), lambda i: (0, 0))],
        out_shape=[jax.ShapeDtypeStruct((T, D), F32), jax.ShapeDtypeStruct((1, D), F32)],
        compiler_params=_cparams(("arbitrary",)),
    )(y, tgt)


def _rope_partner(x):
    lane = lax.broadcasted_iota(jnp.int32, (1, 128), 1)
    return jnp.where(lane < ROPE_LANE0 + ROPE_DIM // 2, pltpu.roll(x, 128 - ROPE_DIM // 2, 1),
                     pltpu.roll(x, ROPE_DIM // 2, 1))


def _rope_apply(x, cos, sin_signed):
    return x * cos + _rope_partner(x) * sin_signed


def _rope_apply_bwd(dy, cos, sin_signed):
    lane = lax.broadcasted_iota(jnp.int32, (1, 128), 1)
    rotary = (lane >= ROPE_LANE0) & (lane < ROPE_LANE0 + ROPE_DIM)
    return dy * cos + jnp.where(rotary, _rope_partner(dy * sin_signed), 0.0)


def rope_tables(T):
    pos = jnp.arange(T, dtype=F32)
    inv_freq = ROPE_THETA ** (-(jnp.arange(ROPE_DIM // 2, dtype=F32) * 2.0 / ROPE_DIM))
    ang = pos[:, None] * inv_freq[None, :]
    cos, sin = jnp.cos(ang), jnp.sin(ang)
    z = jnp.zeros((T, ROPE_LANE0), F32)
    z2 = jnp.zeros((T, 128 - ROPE_LANE0 - ROPE_DIM), F32)
    cosr = jnp.concatenate([z, cos, cos, z2], axis=1)
    sinr = jnp.concatenate([z, -sin, sin, z2], axis=1)
    return cosr, sinr


def mla_prep(proj, gq, gkv, wuq, wukv, cosr, sinr):
    T = proj.shape[0]
    tm = _tile(T, 512)

    def body(cq_ref, ckv_ref, misc_ref, gq_ref, gkv_ref, wuq_ref, wukv_ref, cos_ref, sin_ref,
             q_ref, k_ref, v_ref, cqT_ref, ckvT_ref):
        lane = lax.broadcasted_iota(jnp.int32, (1, 128), 1)
        cosr_, sinr_ = cos_ref[...], sin_ref[...]
        cosq = cosr_ + jnp.where(lane < ROPE_LANE0, 1.0, 0.0)
        cqn = _rms_fwd(cq_ref[...], gq_ref[...])
        cqT_ref[...] = cqn.T.astype(MXU)
        qm = _dot(cqn, wuq_ref[...])
        q_ref[...] = jnp.concatenate(
            [_rope_apply(qm[:, 128 * h:128 * (h + 1)], cosq, sinr_) for h in range(4)], axis=1).astype(MXU)
        ckvn = _rms_fwd(ckv_ref[...], gkv_ref[...])
        ckvT_ref[...] = ckvn.T.astype(MXU)
        kv = _dot(ckvn, wukv_ref[...])
        kr = _rope_apply(misc_ref[...], cosr_, sinr_)
        k_ref[...] = jnp.concatenate(
            [kv[:, 128 * h:128 * (h + 1)] + kr for h in range(4)], axis=1).astype(MXU)
        v_ref[...] = kv[:, 512:768].astype(MXU)

    row = lambda i: (i, 0)
    const = lambda i: (0, 0)
    return pl.pallas_call(
        body, name="mla_prep", grid=(T // tm,),
        in_specs=[pl.BlockSpec((tm, 256), lambda i: (i, C_CQ // 256)),
                  pl.BlockSpec((tm, 128), lambda i: (i, C_CKV // 128)),
                  pl.BlockSpec((tm, 128), lambda i: (i, C_MISC // 128)),
                  pl.BlockSpec((1, 256), const), pl.BlockSpec((1, 128), const),
                  pl.BlockSpec((256, 512), const), pl.BlockSpec((128, 768), const),
                  pl.BlockSpec((tm, 128), row), pl.BlockSpec((tm, 128), row)],
        out_specs=[pl.BlockSpec((tm, 512), row), pl.BlockSpec((tm, 512), row), pl.BlockSpec((tm, 256), row),
                   pl.BlockSpec((256, tm), lambda i: (0, i)), pl.BlockSpec((128, tm), lambda i: (0, i))],
        out_shape=[jax.ShapeDtypeStruct((T, 512), MXU), jax.ShapeDtypeStruct((T, 512), MXU),
                   jax.ShapeDtypeStruct((T, 256), MXU),
                   jax.ShapeDtypeStruct((256, T), MXU), jax.ShapeDtypeStruct((128, T), MXU)],
        compiler_params=_cparams(("parallel",)),
    )(proj, proj, proj, gq, gkv, wuq, wukv, cosr, sinr)


def mla_prep_bwd(dq, dk, dv, proj, gq, gkv, wuq, wukv, cosr, sinr):
    T = proj.shape[0]
    tm = _tile(T, 512)

    def body(dq_ref, dk_ref, dv_ref, cq_ref, ckv_ref, gq_ref, gkv_ref, wuq_ref, wukv_ref, cos_ref, sin_ref,
             dqm_ref, dkv_ref, dcq_ref, dckv_ref, dmisc_ref, dgq_ref, dgkv_ref):
        first = pl.program_id(0) == 0
        lane = lax.broadcasted_iota(jnp.int32, (1, 128), 1)
        cosr_, sinr_ = cos_ref[...], sin_ref[...]
        cosq = cosr_ + jnp.where(lane < ROPE_LANE0, 1.0, 0.0)
        dqv = dq_ref[...]
        dqm = jnp.concatenate(
            [_rope_apply_bwd(dqv[:, 128 * h:128 * (h + 1)], cosq, sinr_) for h in range(4)], axis=1)
        dqm_ref[...] = dqm.astype(MXU)
        dcq, dgq = _rms_bwd(_dot_nt(dqm, wuq_ref[...]), cq_ref[...], gq_ref[...])
        dcq_ref[...] = dcq
        _acc_out(dgq_ref, dgq, first)
        dkv_ = dk_ref[...]
        heads = [dkv_[:, 128 * h:128 * (h + 1)] for h in range(4)]
        dkr = heads[0] + heads[1] + heads[2] + heads[3]
        dmisc_ref[...] = _rope_apply_bwd(dkr, cosr_, sinr_)
        dkvm = jnp.concatenate([jnp.where(lane < ROPE_LANE0, hd, 0.0) for hd in heads] + [dv_ref[...]], axis=1)
        dkv_ref[...] = dkvm.astype(MXU)
        dckv, dgkv = _rms_bwd(_dot_nt(dkvm, wukv_ref[...]), ckv_ref[...], gkv_ref[...])
        dckv_ref[...] = dckv
        _acc_out(dgkv_ref, dgkv, first)

    row = lambda i: (i, 0)
    const = lambda i: (0, 0)
    return pl.pallas_call(
        body, name="mla_prep_bwd", grid=(T // tm,),
        in_specs=[pl.BlockSpec((tm, 512), row), pl.BlockSpec((tm, 512), row), pl.BlockSpec((tm, 256), row),
                  pl.BlockSpec((tm, 256), lambda i: (i, C_CQ // 256)),
                  pl.BlockSpec((tm, 128), lambda i: (i, C_CKV // 128)),
                  pl.BlockSpec((1, 256), const), pl.BlockSpec((1, 128), const),
                  pl.BlockSpec((256, 512), const), pl.BlockSpec((128, 768), const),
                  pl.BlockSpec((tm, 128), row), pl.BlockSpec((tm, 128), row)],
        out_specs=[pl.BlockSpec((tm, 512), row), pl.BlockSpec((tm, 768), row), pl.BlockSpec((tm, 256), row),
                   pl.BlockSpec((tm, 128), row), pl.BlockSpec((tm, 128), row),
                   pl.BlockSpec((1, 256), const), pl.BlockSpec((1, 128), const)],
        out_shape=[jax.ShapeDtypeStruct((T, 512), MXU), jax.ShapeDtypeStruct((T, 768), MXU),
                   jax.ShapeDtypeStruct((T, 256), F32), jax.ShapeDtypeStruct((T, 128), F32),
                   jax.ShapeDtypeStruct((T, 128), F32),
                   jax.ShapeDtypeStruct((1, 256), F32), jax.ShapeDtypeStruct((1, 128), F32)],
        compiler_params=_cparams(("arbitrary",)),
    )(dq, dk, dv, proj, proj, gq, gkv, wuq, wukv, cosr, sinr)


def _split3(x):
    hi = x.astype(MXU)
    r1 = x - hi.astype(F32)
    mid = r1.astype(MXU)
    lo = (r1 - mid.astype(F32)).astype(MXU)
    return hi, mid, lo


def _tri_matmul(tri, x):
    hi, mid, lo = _split3(x)
    d = lambda p: jnp.dot(tri, p, preferred_element_type=F32)
    return d(hi) + d(mid) + d(lo)


def _log_sigmoid(z):
    return jnp.minimum(z, 0.0) - jnp.log(1.0 + jnp.exp(-jnp.abs(z)))


def fox_gate(proj, fbias):
    T = proj.shape[0]
    tb = _tile(T, 512)

    def body(misc_ref, b_ref, fc_ref, fr_ref, frep_ref, carry):
        @pl.when(pl.program_id(0) == 0)
        def _():
            carry[...] = jnp.zeros_like(carry)

        lane = lax.broadcasted_iota(jnp.int32, (1, 128), 1)
        lf = jnp.where(lane < 4, _log_sigmoid(misc_ref[...] + b_ref[...]), 0.0)
        r = lax.broadcasted_iota(jnp.int32, (tb, tb), 0)
        c = lax.broadcasted_iota(jnp.int32, (tb, tb), 1)
        tri = jnp.where(r >= c, 1.0, 0.0).astype(MXU)
        F = _tri_matmul(tri, lf) + carry[...]
        carry[...] = carry[...] + jnp.sum(lf, axis=0, keepdims=True)
        fc_ref[0] = F
        fc_ref[1] = pltpu.roll(F, 126, 1)
        ft = F.T[0:8, :]
        fr_ref[0] = ft
        fr_ref[1] = pltpu.roll(ft, 6, 0)
        for h in range(4):
            frep_ref[h] = jnp.broadcast_to(_lane_pick(F, h), (tb, 128))

    return pl.pallas_call(
        body, name="fox_gate", grid=(T // tb,),
        in_specs=[pl.BlockSpec((tb, 128), lambda i: (i, C_MISC // 128)), pl.BlockSpec((1, 128), lambda i: (0, 0))],
        out_specs=[pl.BlockSpec((2, tb, 128), lambda i: (0, i, 0)), pl.BlockSpec((2, 8, tb), lambda i: (0, 0, i)),
                   pl.BlockSpec((4, tb, 128), lambda i: (0, i, 0))],
        out_shape=[jax.ShapeDtypeStruct((2, T, 128), F32), jax.ShapeDtypeStruct((2, 8, T), F32),
                   jax.ShapeDtypeStruct((4, T, 128), F32)],
        scratch_shapes=[pltpu.VMEM((1, 128), F32)],
        compiler_params=_cparams(("arbitrary",)),
    )(proj, fbias)


def fox_gate_bwd(dFq, dFk, proj, fbias):
    T = proj.shape[0]
    tb = _tile(T, 512)
    nb = T // tb

    def body(dq_ref, dk_ref, misc_ref, b_ref, dm_ref, db_ref, carry):
        first = pl.program_id(0) == 0

        @pl.when(first)
        def _():
            carry[...] = jnp.zeros_like(carry)

        lane = lax.broadcasted_iota(jnp.int32, (1, 128), 1)
        dF = jnp.where(lane < 4, (dq_ref[0] + dk_ref[0]) + pltpu.roll(dq_ref[1] + dk_ref[1], 2, 1), 0.0)
        r = lax.broadcasted_iota(jnp.int32, (tb, tb), 0)
        c = lax.broadcasted_iota(jnp.int32, (tb, tb), 1)
        tri = jnp.where(r <= c, 1.0, 0.0).astype(MXU)
        dlf = _tri_matmul(tri, dF) + carry[...]
        carry[...] = carry[...] + jnp.sum(dF, axis=0, keepdims=True)
        z = misc_ref[...] + b_ref[...]
        dz = jnp.where(lane < 4, dlf * (1.0 / (1.0 + jnp.exp(z))), 0.0)
        dm_ref[...] = dz
        _acc_out(db_ref, jnp.sum(dz, axis=0, keepdims=True), first)

    return pl.pallas_call(
        body, name="fox_gate_bwd", grid=(nb,),
        in_specs=[pl.BlockSpec((2, tb, 128), lambda i: (0, nb - 1 - i, 0)),
                  pl.BlockSpec((2, tb, 128), lambda i: (0, nb - 1 - i, 0)),
                  pl.BlockSpec((tb, 128), lambda i: (nb - 1 - i, C_MISC // 128)),
                  pl.BlockSpec((1, 128), lambda i: (0, 0))],
        out_specs=[pl.BlockSpec((tb, 128), lambda i: (nb - 1 - i, 0)), pl.BlockSpec((1, 128), lambda i: (0, 0))],
        out_shape=[jax.ShapeDtypeStruct((T, 128), F32), jax.ShapeDtypeStruct((1, 128), F32)],
        scratch_shapes=[pltpu.VMEM((1, 128), F32)],
        compiler_params=_cparams(("arbitrary",)),
    )(dFq, dFk, proj, fbias)


FLASH_TILE = 512


def _row_stat_tile(a, b, n):
    at = jnp.broadcast_to(a, (n, 128)).T[0:8, :]
    bt = jnp.broadcast_to(b, (n, 128)).T[0:8, :]
    sub = lax.broadcasted_iota(jnp.int32, (8, 1), 0)
    return jnp.where(sub == 0, at, jnp.where(sub == 1, bt, 0.0))


def _col_stat_tile(a, b):
    lane = lax.broadcasted_iota(jnp.int32, (1, 128), 1)
    return jnp.where(lane == 0, a, jnp.where(lane == 1, b, 0.0))


def _lane_pick(x, h):
    lane = lax.broadcasted_iota(jnp.int32, (1, 128), 1)
    return jnp.sum(jnp.where(lane == h, x, 0.0), axis=1, keepdims=True)


def _half_mask(h):
    lane = lax.broadcasted_iota(jnp.int32, (1, 128), 1)
    return (lane // HEAD_DIM) == h


def _call_hosting(body, name, grid, args, in_specs, out_specs, out_shape, scratch, exch):
    n_out = len(out_shape)
    if exch is not None:
        body, (xargs, xin, xout, xshape, xscratch) = hosted_exchange(
            body, len(args), n_out, len(scratch), grid, *exch)
        args, in_specs, out_specs = args + xargs, in_specs + xin, out_specs + xout
        out_shape, scratch = out_shape + xshape, scratch + xscratch
    res = pl.pallas_call(
        body, name=name, grid=grid, in_specs=in_specs, out_specs=out_specs, out_shape=out_shape,
        scratch_shapes=scratch, compiler_params=_cparams(("arbitrary",) * len(grid)),
    )(*args)
    return res[:n_out], res[n_out:]


def flash_fwd(q, k, v, frep, frow, *, qblk, kblk, vblk, nq, scale, name, exch=None):
    T = q.shape[0]
    tk = _tile(T, FLASH_TILE)
    tq = _tile(T, 2 * FLASH_TILE)
    per_q = tq // tk
    wq = 128 * nq
    has_f = frep is not None

    def body(*refs):
        if has_f:
            q_ref, k_ref, v_ref, fk_ref, fr_ref, o_ref, lr_ref, vT_sc, m_sc, acc_sc = refs
        else:
            q_ref, k_ref, v_ref, o_ref, lr_ref, vT_sc, m_sc, acc_sc = refs
        i = pl.program_id(1)

        @pl.when(i == 0)
        def _():
            vT_sc[...] = v_ref[...].astype(F32).T.astype(MXU)

        key_row = lax.broadcasted_iota(jnp.int32, (tk, 1), 0)
        q_col = lax.broadcasted_iota(jnp.int32, (1, tq), 1)
        row_half = lax.broadcasted_iota(jnp.int32, (128, 1), 0) // HEAD_DIM
        qb = q_ref[...].astype(F32) * scale
        if nq == 1:
            qhs = [jnp.where(_half_mask(h), qb, 0).astype(MXU) for h in range(2)]
        else:
            qhs = [qb[:, 128 * h:128 * (h + 1)].astype(MXU) for h in range(2)]
        for h in range(2):
            m_sc[h] = jnp.full((1, tq), NEG, F32)
            acc_sc[h] = jnp.zeros((128, tq), F32)

        def make_step(diag_block):
            def step(j, carry):
                off = pl.multiple_of(j * tk, tk)
                ks = k_ref[pl.ds(off, tk), :]
                vT = vT_sc[:, pl.ds(off, tk)]
                for h in range(2):
                    kh = ks if nq == 1 else ks[:, 128 * h:128 * (h + 1)]
                    sT = _dot_nt(kh, qhs[h])
                    if has_f:
                        fk = fk_ref[h, pl.ds(off, tk), :]
                        sT = sT + (fr_ref[0, h:h + 1, :] - jnp.concatenate([fk] * (tq // 128), axis=1))
                    if diag_block is not None:
                        sT = jnp.where(key_row + diag_block * tk <= q_col, sT, NEG)
                    m_prev = m_sc[h]
                    m_new = jnp.maximum(m_prev, jnp.max(sT, axis=0, keepdims=True))
                    alpha = jnp.exp(m_prev - m_new)
                    pT = jnp.exp(sT - m_new)
                    vTh = jnp.where(row_half == h, vT, jnp.ones_like(vT))
                    acc_sc[h] = alpha * acc_sc[h] + _dot(vTh, pT)
                    m_sc[h] = m_new
                return carry
            return step

        lax.fori_loop(0, per_q * i, make_step(None), 0)
        for d in range(per_q):
            make_step(d)(per_q * i + d, 0)
        outs, lses = [], []
        for h in range(2):
            acc = acc_sc[h]
            outs.append(acc / pltpu.roll(acc, HEAD_DIM, 0))
            l = acc_sc[h, HEAD_DIM * (1 - h):HEAD_DIM * (1 - h) + 1, :]
            lses.append(m_sc[h] + jnp.log(l))
        o_ref[...] = jnp.where(row_half == 0, outs[0], outs[1]).T
        sub = lax.broadcasted_iota(jnp.int32, (8, 1), 0)
        lr_ref[0] = jnp.where(sub == 0, lses[0], jnp.where(sub == 1, lses[1], 0.0))

    in_specs = [pl.BlockSpec((tq, wq), lambda p, i: (i, qblk + p)),
                pl.BlockSpec((T, wq), lambda p, i: (0, kblk + p)),
                pl.BlockSpec((T, 128), lambda p, i: (0, vblk + p))]
    args = [q, k, v]
    if has_f:
        in_specs += [pl.BlockSpec((2, T, 128), lambda p, i: (p, 0, 0)),
                     pl.BlockSpec((1, 8, tq), lambda p, i: (p, 0, i))]
        args += [frep, frow]
    out_specs = [pl.BlockSpec((tq, 128), lambda p, i: (i, p)), pl.BlockSpec((1, 8, tq), lambda p, i: (p, 0, i))]
    out_shape = [jax.ShapeDtypeStruct((T, 256), F32), jax.ShapeDtypeStruct((2, 8, T), F32)]
    scratch = [pltpu.VMEM((128, T), MXU), pltpu.VMEM((2, 1, tq), F32), pltpu.VMEM((2, 128, tq), F32)]
    return _call_hosting(body, name, (2, T // tq), args, in_specs, out_specs, out_shape, scratch, exch)


def flash_bwd(q, k, v, do, o, lrow, fcol, frow, *, qblk, kblk, vblk, nq, scale, name, exch=None):
    T = q.shape[0]
    tq = tk = _tile(T, FLASH_TILE)
    wq = 128 * nq
    nqb = T // tq
    has_f = fcol is not None

    def body(*refs):
        if has_f:
            (q_ref, k_ref, v_ref, do_ref, o_ref, lr_ref, fc_ref, fr_ref,
             dq_ref, dk_ref, dv_ref, df_ref, dfq_ref, dk_sc, dv_sc, dqT_sc, d_sc, df_sc, dfq_sc) = refs
        else:
            q_ref, k_ref, v_ref, do_ref, o_ref, lr_ref, dq_ref, dk_ref, dv_ref, dk_sc, dv_sc, dqT_sc, d_sc = refs
        j = pl.program_id(1)
        diag = lax.broadcasted_iota(jnp.int32, (tk, 1), 0) <= lax.broadcasted_iota(jnp.int32, (1, tq), 1)
        hms = [_half_mask(h) for h in range(2)]

        @pl.when(j == 0)
        def _():
            dqT_sc[...] = jnp.zeros_like(dqT_sc)
            if has_f:
                dfq_sc[...] = jnp.zeros_like(dfq_sc)

            def delta(b, carry):
                off = pl.multiple_of(b * tq, tq)
                prod = do_ref[pl.ds(off, tq), :] * o_ref[pl.ds(off, tq), :]
                Ds = [jnp.sum(jnp.where(hms[h], prod, 0.0), axis=1, keepdims=True) for h in range(2)]
                d_sc[:, pl.ds(off, tq)] = _row_stat_tile(Ds[0], Ds[1], tq)
                return carry

            lax.fori_loop(0, nqb, delta, 0)

        kb = k_ref[...]
        vb = v_ref[...]
        if nq == 1:
            khs = [jnp.where(hms[h], kb, 0).astype(MXU) for h in range(2)]
        else:
            khs = [kb[:, 128 * h:128 * (h + 1)].astype(MXU) for h in range(2)]
        kTs = [kh.astype(F32).T.astype(MXU) for kh in khs]
        kss = [(kh.astype(F32) * scale).astype(MXU) for kh in khs]
        vhs = [jnp.where(hms[h], vb, 0).astype(MXU) for h in range(2)]
        fks = [_lane_pick(fc_ref[0], h) for h in range(2)] if has_f else None
        dv_sc[...] = jnp.zeros_like(dv_sc)
        dk_sc[...] = jnp.zeros_like(dk_sc)
        if has_f:
            df_sc[...] = jnp.zeros_like(df_sc)

        def make_step(masked):
            def step(i, carry):
                off = pl.multiple_of(i * tq, tq)
                qs = q_ref[pl.ds(off, tq), :]
                dos = do_ref[pl.ds(off, tq), :]
                for h in range(2):
                    qh = qs if nq == 1 else qs[:, 128 * h:128 * (h + 1)]
                    sT = _dot_nt(kss[h], qh)
                    if has_f:
                        sT = sT + (fr_ref[0, h:h + 1, pl.ds(off, tq)] - fks[h])
                    pT = jnp.exp(sT - lr_ref[0, h:h + 1, pl.ds(off, tq)])
                    if masked:
                        pT = jnp.where(diag, pT, 0.0)
                    dsT = pT * (_dot_nt(vhs[h], dos) - d_sc[h:h + 1, pl.ds(off, tq)])
                    dv_sc[...] += _dot(pT, jnp.where(hms[h], dos, 0))
                    qq = jnp.where(hms[h], qs, 0) if nq == 1 else qh
                    dk_sc[h if nq == 2 else 0] += _dot(dsT, qq)
                    dqT_sc[h if nq == 2 else 0, :, pl.ds(off, tq)] += _dot(kTs[h], dsT)
                    if has_f:
                        part = dsT[:, 0:128]
                        for c in range(1, tq // 128):
                            part = part + dsT[:, 128 * c:128 * (c + 1)]
                        df_sc[h] += part
                        dfq_sc[h:h + 1, pl.ds(off, tq)] += jnp.sum(dsT, axis=0, keepdims=True)
                return carry
            return step

        make_step(True)(j, 0)
        lax.fori_loop(j + 1, nqb, make_step(False), 0)
        if nq == 1:
            dk_ref[...] = dk_sc[0] * scale
        else:
            dk_ref[...] = jnp.concatenate([dk_sc[0], dk_sc[1]], axis=1) * scale
        dv_ref[...] = dv_sc[...]
        if has_f:
            df_ref[0] = _col_stat_tile(-jnp.sum(df_sc[0], axis=1, keepdims=True),
                                       -jnp.sum(df_sc[1], axis=1, keepdims=True))

        @pl.when(j == nqb - 1)
        def _():
            if nq == 1:
                dq_ref[...] = dqT_sc[0].T * scale
            else:
                dq_ref[...] = jnp.concatenate([dqT_sc[0].T, dqT_sc[1].T], axis=1) * scale
            if has_f:
                sub = lax.broadcasted_iota(jnp.int32, (128, 1), 0)
                rows = jnp.where(sub == 0, dfq_sc[0:1, :], jnp.where(sub == 1, dfq_sc[1:2, :], 0.0))
                dfq_ref[0] = rows.T

    in_specs = [pl.BlockSpec((T, wq), lambda p, j: (0, qblk + p)),
                pl.BlockSpec((tk, wq), lambda p, j: (j, kblk + p)),
                pl.BlockSpec((tk, 128), lambda p, j: (j, vblk + p)),
                pl.BlockSpec((T, 128), lambda p, j: (0, p)),
                pl.BlockSpec((T, 128), lambda p, j: (0, p)),
                pl.BlockSpec((1, 8, T), lambda p, j: (p, 0, 0))]
    args = [q, k, v, do, o, lrow]
    out_specs = [pl.BlockSpec((T, wq), lambda p, j: (0, p)),
                 pl.BlockSpec((tk, wq), lambda p, j: (j, p)), pl.BlockSpec((tk, 128), lambda p, j: (j, p))]
    out_shape = [jax.ShapeDtypeStruct((T, 2 * wq), F32), jax.ShapeDtypeStruct((T, 2 * wq), F32),
                 jax.ShapeDtypeStruct((T, 256), F32)]
    scratch = [pltpu.VMEM((nq, tk, 128), F32), pltpu.VMEM((tk, 128), F32), pltpu.VMEM((nq, 128, T), F32),
               pltpu.VMEM((8, T), F32)]
    if has_f:
        in_specs += [pl.BlockSpec((1, tk, 128), lambda p, j: (p, j, 0)),
                     pl.BlockSpec((1, 8, T), lambda p, j: (p, 0, 0))]
        args += [fcol, frow]
        out_specs += [pl.BlockSpec((1, tk, 128), lambda p, j: (p, j, 0)),
                      pl.BlockSpec((1, T, 128), lambda p, j: (p, 0, 0))]
        out_shape += [jax.ShapeDtypeStruct((2, T, 128), F32), jax.ShapeDtypeStruct((2, T, 128), F32)]
        scratch += [pltpu.VMEM((2, tk, 128), F32), pltpu.VMEM((8, T), F32)]
    return _call_hosting(body, name, (2, T // tk), args, in_specs, out_specs, out_shape, scratch, exch)


def _swa_align(pair, e, h):
    sel = jnp.where(_half_mask(e), pair, 0.0)
    if e == h:
        return sel
    return pltpu.roll(sel, HEAD_DIM, 1)


def _swa_mask(n):
    W = WINDOW
    qi = lax.broadcasted_iota(jnp.int32, (W, 2 * W), 0) + W
    kj = lax.broadcasted_iota(jnp.int32, (W, 2 * W), 1)
    dist = qi - kj
    return (dist >= 0) & (dist < W) & ((n > 0) | (kj >= W))


def swa_fwd(proj, bias, sinks, exch=None):
    T = proj.shape[0]
    W = WINDOW
    nb = T // W
    scale = HEAD_DIM ** -0.5

    def body(sink_ref, q_ref, kp_ref, kc_ref, vp_ref, vc_ref, b_ref, o_ref, l_ref):
        n = pl.program_id(0)
        mask = _swa_mask(n)
        kband = jnp.concatenate([kp_ref[...], kc_ref[...]], axis=0).astype(MXU)
        vband = jnp.concatenate([vp_ref[...], vc_ref[...]], axis=0).astype(MXU)
        lane = lax.broadcasted_iota(jnp.int32, (1, 128), 1)
        lse_tile = jnp.zeros((W, 128), F32)
        pairs = []
        for h in range(2):
            full = []
            for g in range(4):
                hq = 4 * h + g
                qa = _swa_align(q_ref[:, 128 * (hq // 2):128 * (hq // 2 + 1)], hq % 2, h)
                s = _dot_nt(qa, kband) * scale + b_ref[hq]
                s = jnp.where(mask, s, NEG)
                sink = sink_ref[hq]
                m = jnp.maximum(jnp.max(s, axis=1, keepdims=True), sink)
                e = jnp.exp(s - m)
                l = jnp.sum(e, axis=1, keepdims=True) + jnp.exp(sink - m)
                r = jnp.where(_half_mask(h), _dot(e, vband), 0.0) / l
                full.append(r + pltpu.roll(r, HEAD_DIM, 1))
                lse_tile = jnp.where(lane == hq, m + jnp.log(l), lse_tile)
            pairs.append(jnp.where(_half_mask(0), full[0], full[1]))
            pairs.append(jnp.where(_half_mask(0), full[2], full[3]))
        o_ref[...] = jnp.concatenate(pairs, axis=1)
        l_ref[...] = lse_tile

    prev = lambda n: (jnp.maximum(n - 1, 0), C_KA // 128)
    cur = lambda n: (n, C_KA // 128)
    prev_v = lambda n: (jnp.maximum(n - 1, 0), C_VA // 128)
    cur_v = lambda n: (n, C_VA // 128)
    return _call_hosting(
        body, "swa_fwd", (nb,), [sinks, proj, proj, proj, proj, proj, bias],
        [pl.BlockSpec(memory_space=pltpu.SMEM),
         pl.BlockSpec((W, 512), lambda n: (n, 0)),
         pl.BlockSpec((W, 128), prev), pl.BlockSpec((W, 128), cur),
         pl.BlockSpec((W, 128), prev_v), pl.BlockSpec((W, 128), cur_v),
         pl.BlockSpec((8, W, 2 * W), lambda n: (0, 0, 0))],
        [pl.BlockSpec((W, 512), lambda n: (n, 0)), pl.BlockSpec((W, 128), lambda n: (n, 0))],
        [jax.ShapeDtypeStruct((T, 512), F32), jax.ShapeDtypeStruct((T, 128), F32)], [], exch)


def swa_bwd(proj, bias, sinks, do, o, lse, exch=None):
    T = proj.shape[0]
    W = WINDOW
    nb = T // W
    scale = HEAD_DIM ** -0.5

    def body(sink_ref, q_ref, kp_ref, kc_ref, vp_ref, vc_ref, b_ref, do_ref, o_ref, l_ref,
             dq_ref, dk_ref, dv_ref, db_ref, dsk_ref, ck, cv):
        n = pl.program_id(0)

        @pl.when(n == 0)
        def _():
            ck[...] = jnp.zeros_like(ck)
            cv[...] = jnp.zeros_like(cv)
            db_ref[...] = jnp.zeros_like(db_ref)
            dsk_ref[...] = jnp.zeros_like(dsk_ref)

        @pl.when(n < nb)
        def _():
            mask = _swa_mask(n)
            kb32 = jnp.concatenate([kp_ref[...], kc_ref[...]], axis=0)
            vb32 = jnp.concatenate([vp_ref[...], vc_ref[...]], axis=0)
            kband = kb32.astype(MXU)
            sub = lax.broadcasted_iota(jnp.int32, (8, 1), 0)
            dk_band = jnp.zeros((2 * W, 128), F32)
            dv_band = jnp.zeros((2 * W, 128), F32)
            dsk = jnp.zeros((8, 128), F32)
            dq_pairs = []
            mask4 = jnp.concatenate([mask] * 4, axis=0)
            for h in range(2):
                hm = _half_mask(h)
                km = jnp.where(hm, kb32, 0.0).astype(MXU)
                vm = jnp.where(hm, vb32, 0.0).astype(MXU)
                pbs = [slice(128 * ((4 * h + g) // 2), 128 * ((4 * h + g) // 2 + 1)) for g in range(4)]
                q4 = jnp.concatenate([_swa_align(q_ref[:, pbs[g]], g % 2, h) for g in range(4)], axis=0)
                do4 = jnp.concatenate([_swa_align(do_ref[:, pbs[g]], g % 2, h) for g in range(4)], axis=0)
                D4 = jnp.concatenate(
                    [jnp.sum(jnp.where(_half_mask(g % 2), do_ref[:, pbs[g]] * o_ref[:, pbs[g]], 0.0), axis=1,
                             keepdims=True) for g in range(4)], axis=0)
                lse4 = jnp.concatenate([_lane_pick(l_ref[...], 4 * h + g) for g in range(4)], axis=0)
                sink4 = jnp.concatenate([jnp.full((W, 1), sink_ref[4 * h + g], F32) for g in range(4)], axis=0)
                s = _dot_nt(q4, kband) * scale + b_ref[4 * h:4 * h + 4].reshape(4 * W, 2 * W)
                p = jnp.where(mask4, jnp.exp(s - lse4), 0.0)
                sd = jnp.exp(sink4 - lse4) * D4
                for g in range(4):
                    dsk = dsk + jnp.where(sub == 4 * h + g,
                                          -jnp.sum(sd[W * g:W * (g + 1)], axis=0, keepdims=True), 0.0)
                ds = p * (_dot_nt(do4, vm) - D4)
                db_ref[4 * h:4 * h + 4] += ds.reshape(4, W, 2 * W)
                dq = _dot(ds, km) * scale
                dq = dq + pltpu.roll(dq, HEAD_DIM, 1)
                dk_band = dk_band + _dot(ds.T, q4) * scale
                dv_band = dv_band + _dot(p.T, do4)
                dq_pairs.append(jnp.where(_half_mask(0), dq[0:W], dq[W:2 * W]))
                dq_pairs.append(jnp.where(_half_mask(0), dq[2 * W:3 * W], dq[3 * W:4 * W]))
            dq_ref[...] = jnp.concatenate(dq_pairs, axis=1)
            dsk_ref[...] += dsk
            dk_ref[...] = ck[...] + dk_band[0:W]
            dv_ref[...] = cv[...] + dv_band[0:W]
            ck[...] = dk_band[W:2 * W]
            cv[...] = dv_band[W:2 * W]

        @pl.when(n == nb)
        def _():
            dk_ref[...] = ck[...]
            dv_ref[...] = cv[...]

    cl = lambda n: jnp.minimum(n, nb - 1)
    pv = lambda n: jnp.maximum(jnp.minimum(n, nb - 1) - 1, 0)
    return _call_hosting(
        body, "swa_bwd", (nb + 1,), [sinks, proj, proj, proj, proj, proj, bias, do, o, lse],
        [pl.BlockSpec(memory_space=pltpu.SMEM),
         pl.BlockSpec((W, 512), lambda n: (cl(n), 0)),
         pl.BlockSpec((W, 128), lambda n: (pv(n), C_KA // 128)),
         pl.BlockSpec((W, 128), lambda n: (cl(n), C_KA // 128)),
         pl.BlockSpec((W, 128), lambda n: (pv(n), C_VA // 128)),
         pl.BlockSpec((W, 128), lambda n: (cl(n), C_VA // 128)),
         pl.BlockSpec((8, W, 2 * W), lambda n: (0, 0, 0)),
         pl.BlockSpec((W, 512), lambda n: (cl(n), 0)),
         pl.BlockSpec((W, 512), lambda n: (cl(n), 0)),
         pl.BlockSpec((W, 128), lambda n: (cl(n), 0))],
        [pl.BlockSpec((W, 512), lambda n: (cl(n), 0)),
         pl.BlockSpec((W, 128), lambda n: (jnp.maximum(n - 1, 0), 0)),
         pl.BlockSpec((W, 128), lambda n: (jnp.maximum(n - 1, 0), 0)),
         pl.BlockSpec((8, W, 2 * W), lambda n: (0, 0, 0)),
         pl.BlockSpec((8, 128), lambda n: (0, 0))],
        [jax.ShapeDtypeStruct((T, 512), F32), jax.ShapeDtypeStruct((T, 128), F32),
         jax.ShapeDtypeStruct((T, 128), F32), jax.ShapeDtypeStruct((8, W, 2 * W), F32),
         jax.ShapeDtypeStruct((8, 128), F32)],
        [pltpu.VMEM((W, 128), F32), pltpu.VMEM((W, 128), F32)], exch)


def swa_bias_table(rel_bias):
    W = WINDOW
    qi = jnp.arange(W, dtype=jnp.int32)[:, None] + W
    kj = jnp.arange(2 * W, dtype=jnp.int32)[None, :]
    dist = qi - kj
    max_exact = REL_BUCKETS // 2
    d = jnp.maximum(dist, 0)
    log_ratio = jnp.log(jnp.maximum(d, 1).astype(F32) / max_exact) / math.log(REL_MAX_DIST / max_exact)
    large = jnp.minimum(max_exact + (log_ratio * (REL_BUCKETS - max_exact)).astype(jnp.int32), REL_BUCKETS - 1)
    bucket = jnp.where(d < max_exact, d, large)
    bucket = bucket.reshape(-1)
    onehot = (bucket[None, :] == jnp.arange(REL_BUCKETS, dtype=jnp.int32)[:, None]).astype(F32)
    bias = jnp.dot(rel_bias.astype(F32).T, onehot, precision=lax.Precision.HIGHEST)
    return bias.reshape(SWA_Q_HEADS, W, 2 * W), bucket


def attn_out(oa, ob, oc, gn, wout, gpost, x):
    T = x.shape[0]
    tm = _tile(T, 512)

    def body(oa_ref, ob_ref, oc_ref, gn_ref, w_ref, gp_ref, x_ref, x2_ref, y_ref, mT_ref):
        g = gn_ref[...]
        mixed = jnp.concatenate([_rms_fwd(oa_ref[...], g[:, 0:512]), _rms_fwd(ob_ref[...], g[:, 512:768]),
                                 _rms_fwd(oc_ref[...], g[:, 768:1024])], axis=1)
        mT_ref[...] = mixed.T.astype(MXU)
        y = _dot(mixed, w_ref[...])
        y_ref[...] = y
        x2_ref[...] = x_ref[...] + _rms_fwd(y, gp_ref[...])

    row = lambda i: (i, 0)
    const = lambda i: (0, 0)
    return pl.pallas_call(
        body, name="attn_out", grid=(T // tm,),
        in_specs=[pl.BlockSpec((tm, 512), row), pl.BlockSpec((tm, 256), row), pl.BlockSpec((tm, 256), row),
                  pl.BlockSpec((1, 1024), const), pl.BlockSpec((1024, 1024), const), pl.BlockSpec((1, 1024), const),
                  pl.BlockSpec((tm, 1024), row)],
        out_specs=[pl.BlockSpec((tm, 1024), row), pl.BlockSpec((tm, 1024), row),
                   pl.BlockSpec((1024, tm), lambda i: (0, i))],
        out_shape=[jax.ShapeDtypeStruct((T, 1024), F32), jax.ShapeDtypeStruct((T, 1024), F32),
                   jax.ShapeDtypeStruct((1024, T), MXU)],
        compiler_params=_cparams(("parallel",)),
    )(oa, ob, oc, gn, wout, gpost, x)


def attn_out_bwd(dx2, y, oa, ob, oc, gn, wout, gpost):
    T = dx2.shape[0]
    tm = _tile(T, 512)

    def body(dx_ref, y_ref, oa_ref, ob_ref, oc_ref, gn_ref, w_ref, gp_ref,
             dy_ref, da_ref, db_ref, dc_ref, dgn_ref, dgp_ref):
        first = pl.program_id(0) == 0
        dy, dgp = _rms_bwd(dx_ref[...], y_ref[...], gp_ref[...])
        dy_ref[...] = dy.astype(MXU)
        _acc_out(dgp_ref, dgp, first)
        dm = _dot_nt(dy, w_ref[...])
        g = gn_ref[...]
        da, dga = _rms_bwd(dm[:, 0:512], oa_ref[...], g[:, 0:512])
        db, dgb = _rms_bwd(dm[:, 512:768], ob_ref[...], g[:, 512:768])
        dc, dgc = _rms_bwd(dm[:, 768:1024], oc_ref[...], g[:, 768:1024])
        da_ref[...] = da
        db_ref[...] = db
        dc_ref[...] = dc
        _acc_out(dgn_ref, jnp.concatenate([dga, dgb, dgc], axis=1), first)

    row = lambda i: (i, 0)
    const = lambda i: (0, 0)
    return pl.pallas_call(
        body, name="attn_out_bwd", grid=(T // tm,),
        in_specs=[pl.BlockSpec((tm, 1024), row), pl.BlockSpec((tm, 1024), row),
                  pl.BlockSpec((tm, 512), row), pl.BlockSpec((tm, 256), row), pl.BlockSpec((tm, 256), row),
                  pl.BlockSpec((1, 1024), const), pl.BlockSpec((1024, 1024), const), pl.BlockSpec((1, 1024), const)],
        out_specs=[pl.BlockSpec((tm, 1024), row), pl.BlockSpec((tm, 512), row), pl.BlockSpec((tm, 256), row),
                   pl.BlockSpec((tm, 256), row), pl.BlockSpec((1, 1024), const), pl.BlockSpec((1, 1024), const)],
        out_shape=[jax.ShapeDtypeStruct((T, 1024), MXU), jax.ShapeDtypeStruct((T, 512), F32),
                   jax.ShapeDtypeStruct((T, 256), F32), jax.ShapeDtypeStruct((T, 256), F32),
                   jax.ShapeDtypeStruct((1, 1024), F32), jax.ShapeDtypeStruct((1, 1024), F32)],
        compiler_params=_cparams(("arbitrary",)),
    )(dx2, y, oa, ob, oc, gn, wout, gpost)


FF_TILE = 256
FF_CHUNK = 32
_GELU_C = math.sqrt(2.0 / math.pi)


def _gelu(x):
    return 0.5 * x * (1.0 + jnp.tanh(_GELU_C * (x + 0.044715 * x * x * x)))


def _gelu_with_grad(x):
    x2 = x * x
    t = jnp.tanh(_GELU_C * x * (1.0 + 0.044715 * x2))
    h = 0.5 * (1.0 + t)
    return x * h, h + (0.5 * _GELU_C) * x * (1.0 - t * t) * (1.0 + (3 * 0.044715) * x2)


def _conv_taps(u, hal_ref, first):
    row = lax.broadcasted_iota(jnp.int32, (8, 1), 0)
    h6 = jnp.where(first, 0.0, hal_ref[6:7, :])
    h7 = jnp.where(first, 0.0, hal_ref[7:8, :])
    r1, r2 = pltpu.roll(u, 1, 0), pltpu.roll(u, 2, 0)
    r1 = jnp.concatenate([jnp.where(row == 0, h7, r1[0:8]), r1[8:]], axis=0)
    r2 = jnp.concatenate([jnp.where(row == 0, h6, jnp.where(row == 1, h7, r2[0:8])), r2[8:]], axis=0)
    return r1, r2


def ffn_fwd(u0, convw, convb, wdown, gpost, x2, exch=None):
    T = x2.shape[0]
    tm, tn = _tile(T, 1024), FF_TILE
    nj = D_FF // tn

    def body(ug_ref, uu_ref, hg_ref, hu_ref, wg_ref, wu_ref, bg_ref, bu_ref, wd_ref, gp_ref, x_ref, wdp_ref,
             x3_ref, y_ref, aT_ref, acc, a_sc, af_sc):
        i, j = pl.program_id(0), pl.program_id(1)
        first = i == 0

        @pl.when(j == 0)
        def _():
            acc[...] = jnp.zeros_like(acc)
            a_sc[...] = jnp.zeros_like(a_sc)

        acc[...] += _dot(a_sc[...], wdp_ref[...])

        taps = lambda ref: (ref[0:1, :], ref[1:2, :], ref[2:3, :])
        wg, wu, bg, bu = taps(wg_ref), taps(wu_ref), bg_ref[...], bu_ref[...]

        def conv(win, wk, b):
            return b + wk[0] * pltpu.roll(win, 2, 0)[8:] + wk[1] * pltpu.roll(win, 1, 0)[8:] + wk[2] * win[8:]

        def chunk(win_g, win_u, row0):
            af_sc[pl.ds(row0, FF_CHUNK), :] = _gelu(conv(win_g, wg, bg)) * conv(win_u, wu, bu)

        before = lambda h_ref: jnp.where(first, 0.0, h_ref[...])
        chunk(jnp.concatenate([before(hg_ref), ug_ref[0:FF_CHUNK, :]], axis=0),
              jnp.concatenate([before(hu_ref), uu_ref[0:FF_CHUNK, :]], axis=0), 0)

        def later(c, carry):
            start = pl.multiple_of(c * FF_CHUNK - 8, 8)
            chunk(ug_ref[pl.ds(start, FF_CHUNK + 8), :], uu_ref[pl.ds(start, FF_CHUNK + 8), :],
                  pl.multiple_of(c * FF_CHUNK, 8))
            return carry

        lax.fori_loop(1, tm // FF_CHUNK, later, 0)
        a = af_sc[...]
        aT_ref[...] = a.T.astype(MXU)
        a_sc[...] = a.astype(MXU)

        @pl.when(j == nj - 1)
        def _():
            y = acc[...] + _dot(a_sc[...], wd_ref[...])
            y_ref[...] = y
            x3_ref[...] = x_ref[...] + _rms_fwd(y, gp_ref[...])

    halo = lambda off: (lambda i, j: (jnp.maximum(i * (tm // 8) - 1, 0), off + j))
    return _call_hosting(
        body, "ffn_fwd", (T // tm, nj), [u0, u0, u0, u0, convw, convw, convb, convb, wdown, gpost, x2, wdown],
        [pl.BlockSpec((tm, tn), lambda i, j: (i, j)), pl.BlockSpec((tm, tn), lambda i, j: (i, nj + j)),
         pl.BlockSpec((8, tn), halo(0)), pl.BlockSpec((8, tn), halo(nj)),
         pl.BlockSpec((3, tn), lambda i, j: (0, j)), pl.BlockSpec((3, tn), lambda i, j: (0, nj + j)),
         pl.BlockSpec((1, tn), lambda i, j: (0, j)), pl.BlockSpec((1, tn), lambda i, j: (0, nj + j)),
         pl.BlockSpec((tn, 1024), lambda i, j: (j, 0)),
         pl.BlockSpec((1, 1024), lambda i, j: (0, 0)),
         pl.BlockSpec((tm, 1024), lambda i, j: (i, 0)),
         pl.BlockSpec((tn, 1024), lambda i, j: (jnp.maximum(j - 1, 0), 0))],
        [pl.BlockSpec((tm, 1024), lambda i, j: (i, 0)), pl.BlockSpec((tm, 1024), lambda i, j: (i, 0)),
         pl.BlockSpec((tn, tm), lambda i, j: (j, i))],
        [jax.ShapeDtypeStruct((T, 1024), F32), jax.ShapeDtypeStruct((T, 1024), F32),
         jax.ShapeDtypeStruct((D_FF, T), MXU)],
        [pltpu.VMEM((tm, 1024), F32), pltpu.VMEM((tm, tn), MXU), pltpu.VMEM((tm, tn), F32)], exch)


def ffn_bwd(dx3, y, u0, convw, convb, wdown, gpost, wupT, x2, gfpre, exch=None):
    T = dx3.shape[0]
    tm, tn = _tile(T, 512), FF_TILE
    nj = D_FF // tn
    ni = T // tm

    def body(dx_ref, y_ref, ug_ref, uu_ref, hg_ref, hu_ref, wg_ref, wu_ref, bg_ref, bu_ref, wd_ref, gp_ref,
             wtg_ref, wtu_ref, x2_ref, gf_ref, wdn_ref, wtgp_ref, wtup_ref,
             dy_ref, dug_ref, duu_ref, dcg_ref, dcu_ref, dgp_ref, dx2_ref, dgf_ref,
             dy_sc, dh_sc, da_sc, dug_sc, duu_sc, cg, cu, ag, au):
        s, j = pl.program_id(0), pl.program_id(1)
        i = ni - 1 - s
        first_tok = i == 0
        sub = lax.broadcasted_iota(jnp.int32, (8, 1), 0)
        slot = j % 2

        @pl.when(j == 0)
        def _():
            dy, dgp = _rms_bwd(dx_ref[...], y_ref[...], gp_ref[...])
            dy_sc[...] = dy.astype(MXU)
            dy_ref[...] = dy.astype(MXU)
            _acc_out(dgp_ref, dgp, s == 0)
            dh_sc[...] = jnp.zeros_like(dh_sc)
            da_sc[0] = _dot_nt(dy.astype(MXU), wd_ref[...])
            dug_sc[...] = jnp.zeros_like(dug_sc)
            duu_sc[...] = jnp.zeros_like(duu_sc)

        @pl.when(s == 0)
        def _():
            cg[j] = jnp.zeros((8, tn), F32)
            cu[j] = jnp.zeros((8, tn), F32)
            ag[j] = jnp.zeros((8, tn), F32)
            au[j] = jnp.zeros((8, tn), F32)

        da = da_sc[slot]
        da_sc[1 - slot] = _dot_nt(dy_sc[...], wdn_ref[...])
        dh_sc[...] += _dot(dug_sc[...], wtgp_ref[...]) + _dot(duu_sc[...], wtup_ref[...])

        def conv(u_ref, h_ref, w_ref, b_ref):
            u = u_ref[...]
            r1, r2 = _conv_taps(u, h_ref, first_tok)
            return b_ref[...] + w_ref[0:1, :] * r2 + w_ref[1:2, :] * r1 + w_ref[2:3, :] * u, u, r1, r2

        gate, ugv, g1, g2 = conv(ug_ref, hg_ref, wg_ref, bg_ref)
        up, uuv, u1, u2 = conv(uu_ref, hu_ref, wu_ref, bu_ref)
        gl, dgl = _gelu_with_grad(gate)
        dup = da * gl
        dgate = da * up * dgl

        def conv_bwd(du, u, r1, r2, w_ref, c_ref, a_ref, duT_ref, du_sc):
            nxt = c_ref[j]
            n0, n1 = nxt[0:1, :], nxt[1:2, :]
            f1, f2 = pltpu.roll(du, tm - 1, 0), pltpu.roll(du, tm - 2, 0)
            f1 = jnp.concatenate([f1[:tm - 8], jnp.where(sub == 7, n0, f1[tm - 8:])], axis=0)
            f2 = jnp.concatenate([f2[:tm - 8], jnp.where(sub == 7, n1, jnp.where(sub == 6, n0, f2[tm - 8:]))], axis=0)
            du0 = w_ref[2:3, :] * du + w_ref[1:2, :] * f1 + w_ref[0:1, :] * f2
            duT_ref[...] = du0.T.astype(MXU)
            du_sc[...] = du0.astype(MXU)
            c_ref[j] = du[0:8, :]
            red = lambda v: jnp.sum(v, axis=0, keepdims=True)
            part = jnp.where(sub == 0, red(du * r2), jnp.where(sub == 1, red(du * r1), jnp.where(
                sub == 2, red(du * u), jnp.where(sub == 3, red(du), 0.0))))
            a_ref[j] = a_ref[j] + part
            return a_ref[j]

        dcg_ref[0] = conv_bwd(dgate, ugv, g1, g2, wg_ref, cg, ag, dug_ref, dug_sc)
        dcu_ref[0] = conv_bwd(dup, uuv, u1, u2, wu_ref, cu, au, duu_ref, duu_sc)

        @pl.when(j == nj - 1)
        def _():
            dh = dh_sc[...] + _dot(dug_sc[...], wtg_ref[...]) + _dot(duu_sc[...], wtu_ref[...])
            dx, dgf = _rms_bwd(dh, x2_ref[...], gf_ref[...])
            dx2_ref[...] = dx_ref[...] + dx
            _acc_out(dgf_ref, dgf, s == 0)

    rev = lambda s: ni - 1 - s
    halo = lambda off: (lambda s, j: (jnp.maximum(rev(s) * (tm // 8) - 1, 0), off + j))
    tok = pl.BlockSpec((tm, 1024), lambda s, j: (rev(s), 0))
    vec = pl.BlockSpec((1, 1024), lambda s, j: (0, 0))
    return _call_hosting(
        body, "ffn_bwd", (ni, nj),
        [dx3, y, u0, u0, u0, u0, convw, convw, convb, convb, wdown, gpost, wupT, wupT, x2, gfpre,
         wdown, wupT, wupT],
        [tok, tok,
         pl.BlockSpec((tm, tn), lambda s, j: (rev(s), j)), pl.BlockSpec((tm, tn), lambda s, j: (rev(s), nj + j)),
         pl.BlockSpec((8, tn), halo(0)), pl.BlockSpec((8, tn), halo(nj)),
         pl.BlockSpec((3, tn), lambda s, j: (0, j)), pl.BlockSpec((3, tn), lambda s, j: (0, nj + j)),
         pl.BlockSpec((1, tn), lambda s, j: (0, j)), pl.BlockSpec((1, tn), lambda s, j: (0, nj + j)),
         pl.BlockSpec((tn, 1024), lambda s, j: (j, 0)), vec,
         pl.BlockSpec((tn, 1024), lambda s, j: (j, 0)), pl.BlockSpec((tn, 1024), lambda s, j: (nj + j, 0)),
         tok, vec,
         pl.BlockSpec((tn, 1024), lambda s, j: (jnp.minimum(j + 1, nj - 1), 0)),
         pl.BlockSpec((tn, 1024), lambda s, j: (jnp.maximum(j - 1, 0), 0)),
         pl.BlockSpec((tn, 1024), lambda s, j: (nj + jnp.maximum(j - 1, 0), 0))],
        [tok,
         pl.BlockSpec((tn, tm), lambda s, j: (j, rev(s))), pl.BlockSpec((tn, tm), lambda s, j: (j, rev(s))),
         pl.BlockSpec((1, 8, tn), lambda s, j: (s, 0, j)), pl.BlockSpec((1, 8, tn), lambda s, j: (s, 0, j)),
         vec, tok, vec],
        [jax.ShapeDtypeStruct((T, 1024), MXU), jax.ShapeDtypeStruct((D_FF, T), MXU),
         jax.ShapeDtypeStruct((D_FF, T), MXU),
         jax.ShapeDtypeStruct((ni, 8, D_FF), F32), jax.ShapeDtypeStruct((ni, 8, D_FF), F32),
         jax.ShapeDtypeStruct((1, 1024), F32), jax.ShapeDtypeStruct((T, 1024), F32),
         jax.ShapeDtypeStruct((1, 1024), F32)],
        [pltpu.VMEM((tm, 1024), MXU), pltpu.VMEM((tm, 1024), F32), pltpu.VMEM((2, tm, tn), F32),
         pltpu.VMEM((tm, tn), MXU), pltpu.VMEM((tm, tn), MXU)] + [pltpu.VMEM((nj, 8, tn), F32)] * 4, exch)


ELEMS_PER_BLOCK = 512 * 1024


def _row_block(R, C):
    if R * C <= ELEMS_PER_BLOCK or R % 8:
        return R
    best = 8
    for t in range(8, R + 1, 8):
        if R % t == 0 and t * C <= ELEMS_PER_BLOCK:
            best = t
    return best


def adamw(w, g, m, v, name, exch=None):
    L, R, C = w.shape
    partials = isinstance(g, (list, tuple))
    tr = _row_block(R, 2 * C)
    c1 = 1.0 - ADAM_B1 ** ADAM_STEP
    c2 = 1.0 - ADAM_B2 ** ADAM_STEP

    def body(w_ref, *rest):
        g_refs, (m_ref, v_ref, g_out, d_ref, nm_ref, nv_ref) = rest[:-6], rest[-6:]

        def step(gv):
            g_out[0] = gv
            nm = ADAM_B1 * m_ref[0] + (1.0 - ADAM_B1) * gv
            nv = ADAM_B2 * v_ref[0] + (1.0 - ADAM_B2) * (gv * gv)
            nm_ref[0] = nm
            nv_ref[0] = nv
            d_ref[0] = -ADAM_LR * ((nm / c1) / (jnp.sqrt(nv / c2) + ADAM_EPS) + ADAM_WD * w_ref[0])

        if not partials:
            step(g_refs[0][0])
            return
        for k in range(L):
            @pl.when(pl.program_id(0) == k)
            def _(k=k):
                gv = g_refs[k][0].astype(F32)
                for d in range(1, N_DEV):
                    gv = gv + g_refs[k][d].astype(F32)
                step(gv)

    spec = pl.BlockSpec((1, tr, C), lambda l, i: (l, i, 0))
    if partials:
        gspecs = [pl.BlockSpec((N_DEV, tr, C), lambda l, i, k=k: (0, jnp.where(l == k, i, 0), 0)) for k in range(L)]
        gs = list(g)
    else:
        gspecs, gs = [spec], [g]
    return _call_hosting(body, name, (L, R // tr), [w] + gs + [m, v], [spec] + gspecs + [spec, spec], [spec] * 4,
                         [jax.ShapeDtypeStruct((L, R, C), F32)] * 4, [], exch)


def sum_devices(buf, name):
    _, R, C = buf.shape
    tr = _row_block(R, C * 4)

    def body(b_ref, o_ref):
        acc = b_ref[0].astype(F32)
        for d in range(1, N_DEV):
            acc = acc + b_ref[d].astype(F32)
        o_ref[...] = acc

    return pl.pallas_call(
        body, name=name, grid=(R // tr,),
        in_specs=[pl.BlockSpec((N_DEV, tr, C), lambda i: (0, i, 0))],
        out_specs=pl.BlockSpec((tr, C), lambda i: (i, 0)),
        out_shape=jax.ShapeDtypeStruct((R, C), F32),
        compiler_params=_cparams(("parallel",)),
    )(buf)


def _exchange_copies(src_refs, out_refs, send_sems, recv_sems, gather):
    x, y, c = lax.axis_index("x"), lax.axis_index("y"), lax.axis_index("c")
    me = 4 * x + 2 * y + c
    flip = lambda a, bit: 1 - a if bit else a
    part = lambda ref, d: ref if gather else ref.at[d]
    copies = []
    for k in range(1, N_DEV):
        px, py, pc = flip(x, (k >> 2) & 1), flip(y, (k >> 1) & 1), flip(c, k & 1)
        peer = 4 * px + 2 * py + pc
        for t in range(len(src_refs)):
            sem = t * (N_DEV - 1) + k - 1
            mk = lambda s, d: pltpu.make_async_remote_copy(
                src_ref=s, dst_ref=d, send_sem=send_sems.at[sem], recv_sem=recv_sems.at[sem],
                device_id=(px, py, pc), device_id_type=pl.DeviceIdType.MESH)
            copies.append((mk(part(src_refs[t], peer), out_refs[t].at[me]),
                           mk(part(src_refs[t], me), out_refs[t].at[peer])))
    return me, copies


def exchange(srcs, name, gather):
    n = len(srcs)
    shapes = [(N_DEV,) + s.shape if gather else s.shape for s in srcs]

    def body(*refs):
        src_refs, out_refs = refs[:n], refs[n:2 * n]
        send_sems, recv_sems, local_sems = refs[2 * n:]
        me, copies = _exchange_copies(src_refs, out_refs, send_sems, recv_sems, gather)
        for outgoing, _ in copies:
            outgoing.start()
        mine = [pltpu.make_async_copy(src_refs[t] if gather else src_refs[t].at[me], out_refs[t].at[me],
                                      local_sems.at[t]) for t in range(n)]
        for cp in mine:
            cp.start()
        for _, incoming in copies:
            incoming.wait_recv()
        for outgoing, _ in copies:
            outgoing.wait_send()
        for cp in mine:
            cp.wait()

    return pl.pallas_call(
        body, name=name,
        in_specs=[pl.BlockSpec(memory_space=pl.ANY)] * n, out_specs=[pl.BlockSpec(memory_space=pl.ANY)] * n,
        out_shape=[jax.ShapeDtypeStruct(shp, s.dtype) for shp, s in zip(shapes, srcs)],
        scratch_shapes=[pltpu.SemaphoreType.DMA((n * (N_DEV - 1),)), pltpu.SemaphoreType.DMA((n * (N_DEV - 1),)),
                        pltpu.SemaphoreType.DMA((n,))],
    )(*srcs)


def hosted_exchange(body, n_in, n_out, n_scratch, grid, srcs, gather):
    n = len(srcs)
    shapes = [(N_DEV,) + s.shape if gather else s.shape for s in srcs]

    def wrapped(*refs):
        ins, xin = refs[:n_in], refs[n_in:n_in + n]
        outs = refs[n_in + n:n_in + n + n_out]
        xout = refs[n_in + n + n_out:n_in + 2 * n + n_out]
        rest = refs[n_in + 2 * n + n_out:]
        scratch, (send_sems, recv_sems, local_sems) = rest[:n_scratch], rest[n_scratch:]
        ids = [pl.program_id(a) for a in range(len(grid))]
        first = functools.reduce(jnp.logical_and, [i == 0 for i in ids])
        last = functools.reduce(jnp.logical_and, [i == g - 1 for i, g in zip(ids, grid)])
        me, copies = _exchange_copies(xin, xout, send_sems, recv_sems, gather)
        mine = [pltpu.make_async_copy(xin[t] if gather else xin[t].at[me], xout[t].at[me], local_sems.at[t])
                for t in range(n)]

        @pl.when(first)
        def _():
            for outgoing, _ in copies:
                outgoing.start()
            for cp in mine:
                cp.start()

        body(*ins, *outs, *scratch)

        @pl.when(last)
        def _():
            for _, incoming in copies:
                incoming.wait_recv()
            for outgoing, _ in copies:
                outgoing.wait_send()
            for cp in mine:
                cp.wait()

    any_spec = pl.BlockSpec(memory_space=pl.ANY)
    return wrapped, (list(srcs), [any_spec] * n, [any_spec] * n,
                     [jax.ShapeDtypeStruct(shp, s.dtype) for shp, s in zip(shapes, srcs)],
                     [pltpu.SemaphoreType.DMA((n * (N_DEV - 1),)), pltpu.SemaphoreType.DMA((n * (N_DEV - 1),)),
                      pltpu.SemaphoreType.DMA((n,))])


def _pack(parts, cols, row_align, dtype):
    flat = jnp.concatenate([p.astype(dtype) for p in parts], axis=-1)
    n = flat.shape[-1]
    block = cols * row_align
    total = -(-n // block) * block
    flat = jnp.pad(flat, [(0, 0)] * (flat.ndim - 1) + [(0, total - n)])
    return flat.reshape(flat.shape[:-1] + (total // cols, cols))


def _unpack(buf, shapes):
    lead = buf.shape[:-2]
    flat = buf.reshape(lead + (-1,))
    out, off = [], 0
    for s in shapes:
        n = int(np.prod(s))
        out.append(flat[..., off:off + n].reshape(lead + tuple(s)))
        off += n
    return out


SHARDED = ["w_in", "w_uq", "w_ukv", "w_out", "w_up", "w_down"]
ATTN_SENT = ["w_in_p", "w_uq", "w_ukv", "w_out"]
UP_HALF = 352
FFN_SIDE = ["w_upT", "conv_w", "w_down"]


def _full_from_shards(name, s):
    if name in ("w_in", "w_in_p", "w_out", "w_down", "w_upT"):
        return s.reshape((-1, s.shape[-1]))
    return s.transpose(1, 0, 2).reshape((s.shape[1], -1))


def _shards_from_full(name, f):
    if name in ("w_in", "w_in_p", "w_out", "w_down", "w_upT"):
        return f.reshape((N_DEV, -1, f.shape[-1]))
    return f.reshape((f.shape[0], N_DEV, -1)).transpose(1, 0, 2)


def _perm_w_in(w):
    z = lambda n: jnp.zeros(w.shape[:-1] + (n,), w.dtype)
    return jnp.concatenate([w[..., :1536], w[..., 1540:1924], w[..., 1536:1540], z(60), w[..., 1924:1956], z(32)],
                           axis=-1)


def _unperm_w_in(d):
    return jnp.concatenate([d[..., :1536], d[..., 1920:1924], d[..., 1536:1920], d[..., 1984:2016]], axis=-1)


def _perm_w_uq(w):
    return jnp.pad(w.reshape(256, 4, MLA_QK_DIM), ((0, 0), (0, 0), (0, 128 - MLA_QK_DIM))).reshape(256, 512)


def _unperm_w_uq(d):
    return d.reshape(256, 4, 128)[:, :, :MLA_QK_DIM].reshape(256, 4 * MLA_QK_DIM)


def _perm_w_ukv(w):
    w4 = w.reshape(128, 4, 128)
    k = jnp.pad(w4[:, :, :64], ((0, 0), (0, 0), (0, 64))).reshape(128, 512)
    return jnp.concatenate([k, w4[:, :, 64:].reshape(128, 256)], axis=1)


def _unperm_w_ukv(d):
    dk = d[:, :512].reshape(128, 4, 128)[:, :, :64]
    dv = d[:, 512:].reshape(128, 4, 64)
    return jnp.concatenate([dk, dv], axis=-1).reshape(128, 512)


def _row(v, width=None):
    v = v.reshape(1, -1).astype(F32)
    if width is not None and v.shape[1] < width:
        v = jnp.pad(v, ((0, 0), (0, width - v.shape[1])))
    return v


def _layer_fwd(x, P, shared, send=None, ffn_from=None):
    cosr, sinr, bias = shared
    ex = lambda host: (send[host], True) if send is not None and send.get(host) else None
    proj, hT, projb = norm_matmul(x, P["g_pre"], P["w_in_p"], "in_proj", lo_tiles=2, tn_pref=1024)
    qm, km, vm, cqT, ckvT = mla_prep(proj, P["gq"], P["gkv"], P["w_uq_p"], P["w_ukv_p"], cosr, sinr)
    fcol, frow, frep = fox_gate(proj, P["fbias"])
    (oa, lse_a), _ = swa_fwd(proj, bias, P["sinks"])
    (ob, lrb), got_fox = flash_fwd(projb, projb, projb, frep, frow, qblk=C_QF // 128, kblk=C_KF // 128,
                                   vblk=C_VF // 128, nq=1, scale=HEAD_DIM ** -0.5, name="fox_fwd", exch=ex("fox"))
    (oc, lrc), got_mla = flash_fwd(qm, km, vm, None, None, qblk=0, kblk=0, vblk=0, nq=2,
                                   scale=MLA_QK_DIM ** -0.5, name="mla_fwd", exch=ex("mla"))
    x2, y1, mT = attn_out(oa, ob, oc, P["gn"], P["w_out"], P["g_apost"], x)
    if ffn_from is not None:
        P = dict(P, **ffn_from(None, got_fox, got_mla))
    up = norm_matmul(x2, P["g_fpre"], P["w_upT"], "up_proj", tn_pref=1536, w_transposed=True,
                     h_transposed=False, exch=ex("up"))
    (u0, h2), got_up = up if ex("up") is not None else (up, None)
    if ffn_from is not None:
        P = dict(P, **ffn_from(got_up, None, None))
    (x3, y2, aT), got_ffn = ffn_fwd(u0, P["conv_w"], P["conv_b"], P["w_down"], P["g_fpost"], x2, exch=ex("ffn"))
    S = dict(x=x, proj=proj, projb=projb, hT=hT, qm=qm, km=km, vm=vm, cqT=cqT, ckvT=ckvT, fcol=fcol, frow=frow,
             oa=oa, lse_a=lse_a, ob=ob, lrb=lrb, oc=oc, lrc=lrc,
             x2=x2, y1=y1, mT=mT, u0=u0, h2=h2, y2=y2, aT=aT)
    return x3, S, P, got_ffn


def _layer_bwd(dx3, P, S, shared, send_attn=None, own_small=False):
    cosr, sinr, bias = shared
    proj = S["proj"]
    G = {}
    got = {}
    ex = lambda arrays: (arrays, False) if send_attn is not None and arrays else None
    (dy2, dugT, duuT, dcg, dcu, G["ffn_post_norm"], dx2, G["ffn_pre_norm"]), got["ffn"] = ffn_bwd(
        dx3, S["y2"], S["u0"], P["conv_w"], P["conv_b"], P["w_down"], P["g_fpost"], P["w_upT"], S["x2"], P["g_fpre"],
        exch=ex(send_attn))
    dconv = jnp.concatenate([dcg[-1], dcu[-1]], axis=1)
    G["conv_w"], G["conv_b"] = dconv[0:3], dconv[3]
    G["w_down"] = matmul_nn(S["aT"], dy2, "dw_down", MXU)
    G["w_upT"] = jnp.concatenate([matmul_nn(dugT, S["h2"], "dw_up_gate", MXU),
                                  matmul_nn(duuT, S["h2"], "dw_up_up", MXU)], axis=0)
    G["w_up"] = G["w_upT"].T
    dy1, doa, dob, doc, G["group_norm"], G["attn_post_norm"] = attn_out_bwd(
        dx2, S["y1"], S["oa"], S["ob"], S["oc"], P["gn"], P["w_out"], P["g_apost"])
    G["w_out"] = matmul_nn(S["mT"], dy1, "dw_out", MXU)
    up_slices = _shards_from_full("w_upT", G["w_upT"])
    (dqa, dka, dva, dbias, dsk), _ = swa_bwd(proj, bias, P["sinks"], doa, S["oa"], S["lse_a"])
    G["swa_sinks"] = dsk[:, 0]
    pb = S["projb"]
    (dqf, dkf, dvf, dFk, dFq), got["fox"] = flash_bwd(
        pb, pb, pb, dob, S["ob"], S["lrb"], S["fcol"], S["frow"], name="fox_bwd", qblk=C_QF // 128,
        kblk=C_KF // 128, vblk=C_VF // 128, nq=1, scale=HEAD_DIM ** -0.5,
        exch=ex([up_slices[:, :UP_HALF], _shards_from_full("w_down", G["w_down"])]))
    dmisc_f, dfb = fox_gate_bwd(dFq, dFk, proj, P["fbias"])
    G["forget_bias"] = dfb[0, 0:4]
    (dqm_, dkm_, dvm_), got["mla"] = flash_bwd(
        S["qm"], S["km"], S["vm"], doc, S["oc"], S["lrc"], None, None, name="mla_bwd",
        qblk=0, kblk=0, vblk=0, nq=2, scale=MLA_QK_DIM ** -0.5, exch=ex([up_slices[:, UP_HALF:]]))
    dqm, dkv, dcq, dckv, dmisc_r, G["q_latent_norm"], G["kv_latent_norm"] = mla_prep_bwd(
        dqm_, dkm_, dvm_, proj, P["gq"], P["gkv"], P["w_uq_p"], P["w_ukv_p"], cosr, sinr)
    G["w_uq"] = _unperm_w_uq(matmul_nn(S["cqT"], dqm, "dw_uq", MXU))
    G["w_ukv"] = _unperm_w_ukv(matmul_nn(S["ckvT"], dkv, "dw_ukv", MXU))
    dproj = jnp.concatenate([dqa, dka, dva, dqf, dkf, dvf, dcq, dckv, dmisc_f + dmisc_r], axis=1).astype(MXU)
    if own_small and send_attn is not None:
        G["w_in_p"], got["dw_in"] = matmul_nn(
            S["hT"], dproj, "dw_in", MXU, exch=([_shards_from_full(n, G[n]) for n in ATTN_SENT[1:]], False))
    else:
        G["w_in_p"] = matmul_nn(S["hT"], dproj, "dw_in", MXU)
    G["w_in"] = _unperm_w_in(G["w_in_p"])
    dx, G["attn_pre_norm"] = matmul_nt_normbwd(dproj, P["w_in_p"], S["x"], P["g_pre"], dx2, "in_bwd")
    return dx, G, dbias, got


def _layer_params(l, full, small):
    return dict(
        g_pre=_row(small["attn_pre_norm"][l]),
        w_in_p=full["w_in_p"] if "w_in_p" in full else _perm_w_in(full["w_in"]),
        gq=_row(small["q_latent_norm"][l]), gkv=_row(small["kv_latent_norm"][l]),
        w_uq_p=_perm_w_uq(full["w_uq"]), w_ukv_p=_perm_w_ukv(full["w_ukv"]),
        fbias=_row(small["forget_bias"][l], 128), sinks=small["swa_sinks"][l].astype(F32),
        gn=_row(small["group_norm"][l]), w_out=full["w_out"], g_apost=_row(small["attn_post_norm"][l]),
        g_fpre=_row(small["ffn_pre_norm"][l]), conv_b=_row(small["conv_b"][l]),
        g_fpost=_row(small["ffn_post_norm"][l]),
        **{n: full[n] for n in FFN_SIDE if n in full},
        **({"w_upT": full["w_up"].T} if "w_up" in full else {}))


def _rel_bias_grad(dbias, bucket):
    flat = dbias.reshape(SWA_Q_HEADS, -1)
    hi = flat.astype(MXU)
    lo = (flat - hi.astype(F32)).astype(MXU)
    onehot = (bucket[:, None] == jnp.arange(128, dtype=jnp.int32)[None, :]).astype(MXU)
    r = matmul_nn(jnp.concatenate([hi, lo], axis=0), onehot, "rel_bias_grad")
    return (r[0:8] + r[8:16])[:, :REL_BUCKETS].T


def local_step(x, tgt, fulls, small, comm=None):
    T = x.shape[0]
    cosr, sinr = rope_tables(T)
    bias, bucket = swa_bias_table(small["rel_bias"])
    shared = (cosr, sinr, bias)
    Ps, Ss = [], []
    h, full = x, fulls[0]
    for l in range(DEPTH):
        P = _layer_params(l, full, small)
        if comm:
            h, S, P, got = _layer_fwd(h, P, shared, comm["weight_parts"](l), comm["ffn_from"])
            full = comm["attn_from"](got) if l + 1 < DEPTH else None
        else:
            h, S, P, _ = _layer_fwd(h, P, shared)
            full = fulls[l + 1] if l + 1 < DEPTH else None
        Ps.append(P)
        Ss.append(S)
    dh, sq = loss_kernel(h, tgt)
    grads = [None] * DEPTH
    dbias_sum = None
    pending = [] if comm else None
    for l in reversed(range(DEPTH)):
        dh, grads[l], dbias, got = _layer_bwd(dh, Ps[l], Ss[l], shared, pending, own_small=l == 0)
        dbias_sum = dbias if dbias_sum is None else dbias_sum + dbias
        if comm:
            comm["landed"](l, ["w_down"], got["fox"][1:])
            comm["landed"](l, ["w_upT"], [jnp.concatenate([got["fox"][0], got["mla"][0]], axis=1)])
            if pending:
                comm["landed"](l + 1, ATTN_SENT, got["ffn"])
            if l == 0:
                comm["landed"](0, ATTN_SENT[1:], got["dw_in"])
                pending = [_shards_from_full("w_in_p", grads[0]["w_in_p"])]
            else:
                pending = [_shards_from_full(n, grads[l][n]) for n in ATTN_SENT]
    return sq, dh, grads, _rel_bias_grad(dbias_sum, bucket), pending


WEIGHTS = ['attn_pre_norm', 'w_in', 'forget_bias', 'swa_sinks', 'rel_bias', 'q_latent_norm', 'w_uq',
           'kv_latent_norm', 'w_ukv', 'group_norm', 'w_out', 'attn_post_norm', 'ffn_pre_norm', 'w_up', 'conv_w',
           'conv_b', 'w_down', 'ffn_post_norm']
SMALL_PER_LAYER = ['attn_pre_norm', 'forget_bias', 'swa_sinks', 'q_latent_norm', 'kv_latent_norm', 'group_norm',
                   'attn_post_norm', 'ffn_pre_norm', 'conv_b', 'ffn_post_norm', 'conv_w']


def kernel(x, attn_pre_norm, w_in, forget_bias, swa_sinks, rel_bias, q_latent_norm, w_uq, kv_latent_norm, w_ukv, group_norm, w_out, attn_post_norm, ffn_pre_norm, w_up, conv_w, conv_b, w_down, ffn_post_norm, loss_target, m_attn_pre_norm, m_w_in, m_forget_bias, m_swa_sinks, m_rel_bias, m_q_latent_norm, m_w_uq, m_kv_latent_norm, m_w_ukv, m_group_norm, m_w_out, m_attn_post_norm, m_ffn_pre_norm, m_w_up, m_conv_w, m_conv_b, m_w_down, m_ffn_post_norm, v_attn_pre_norm, v_w_in, v_forget_bias, v_swa_sinks, v_rel_bias, v_q_latent_norm, v_w_uq, v_kv_latent_norm, v_w_ukv, v_group_norm, v_w_out, v_attn_post_norm, v_ffn_pre_norm, v_w_up, v_conv_w, v_conv_b, v_w_down, v_ffn_post_norm):
    W = dict(attn_pre_norm=attn_pre_norm, w_in=w_in, forget_bias=forget_bias, swa_sinks=swa_sinks, rel_bias=rel_bias,
             q_latent_norm=q_latent_norm, w_uq=w_uq, kv_latent_norm=kv_latent_norm, w_ukv=w_ukv,
             group_norm=group_norm, w_out=w_out, attn_post_norm=attn_post_norm, ffn_pre_norm=ffn_pre_norm,
             w_up=w_up, conv_w=conv_w, conv_b=conv_b, w_down=w_down, ffn_post_norm=ffn_post_norm)
    M = dict(attn_pre_norm=m_attn_pre_norm, w_in=m_w_in, forget_bias=m_forget_bias, swa_sinks=m_swa_sinks,
             rel_bias=m_rel_bias, q_latent_norm=m_q_latent_norm, w_uq=m_w_uq, kv_latent_norm=m_kv_latent_norm,
             w_ukv=m_w_ukv, group_norm=m_group_norm, w_out=m_w_out, attn_post_norm=m_attn_post_norm,
             ffn_pre_norm=m_ffn_pre_norm, w_up=m_w_up, conv_w=m_conv_w, conv_b=m_conv_b, w_down=m_w_down,
             ffn_post_norm=m_ffn_post_norm)
    V = dict(attn_pre_norm=v_attn_pre_norm, w_in=v_w_in, forget_bias=v_forget_bias, swa_sinks=v_swa_sinks,
             rel_bias=v_rel_bias, q_latent_norm=v_q_latent_norm, w_uq=v_w_uq, kv_latent_norm=v_kv_latent_norm,
             w_ukv=v_w_ukv, group_norm=v_group_norm, w_out=v_w_out, attn_post_norm=v_attn_post_norm,
             ffn_pre_norm=v_ffn_pre_norm, w_up=v_w_up, conv_w=v_conv_w, conv_b=v_conv_b, w_down=v_w_down,
             ffn_post_norm=v_ffn_post_norm)
    me = 4 * lax.axis_index("x") + 2 * lax.axis_index("y") + lax.axis_index("c")

    def attn_shards(l):
        return [_perm_w_in(w_in[l].astype(MXU))] + [W[n][l].astype(MXU) for n in ATTN_SENT[1:]]

    def weight_parts(l):
        up = jnp.swapaxes(W["w_up"][l], 0, 1).astype(MXU)
        return dict(up=[W["w_down"][l].astype(MXU)], fox=[up[:UP_HALF]], mla=[up[UP_HALF:], conv_w[l]],
                    ffn=attn_shards(l + 1) if l + 1 < DEPTH else [])

    def ffn_from(got_up, got_fox, got_mla):
        if got_up is not None:
            return dict(w_down=_full_from_shards("w_down", got_up[0]))
        return dict(w_upT=_full_from_shards("w_upT", jnp.concatenate([got_fox[0], got_mla[0]], axis=1)),
                    conv_w=got_mla[1].transpose(1, 0, 2).reshape(3, 2 * D_FF))

    def attn_from(got_ffn):
        return {n: _full_from_shards(n, s) for n, s in zip(ATTN_SENT, got_ffn)}

    landed = [{} for _ in range(DEPTH)]

    def on_landed(l, names, arrays):
        landed[l].update(zip(names, arrays))

    comm = dict(weight_parts=weight_parts, ffn_from=ffn_from, attn_from=attn_from, landed=on_landed)
    full0 = dict(zip(ATTN_SENT, map(_full_from_shards, ATTN_SENT, exchange(attn_shards(0), "gather_weights", True))))
    sq, dx, grads, drel, last = local_step(x[0], loss_target[0], [full0], W, comm)
    G, delta, new_m, new_v = {}, {}, {}, {}

    def update(n, exch=None):
        shp = W[n].shape
        if n == "w_up":
            v3 = lambda a: jnp.swapaxes(a, 1, 2)
            back = v3
            g = [landed[l]["w_upT"] for l in range(DEPTH)]
        else:
            v3 = lambda a: a.reshape(shp if len(shp) == 3 else (1,) + shp)
            back = lambda a: a.reshape(shp)
            g = [landed[l][n] for l in range(DEPTH)] if n in SHARDED else v3(G[n])
        (g, d, nm, nv), got = adamw(v3(W[n]), g, v3(M[n]), v3(V[n]), "adamw_" + n, exch)
        G[n], delta[n], new_m[n], new_v[n] = back(g), back(d), back(nm), back(nv)
        return got

    parts, shapes = [], []
    for l in range(DEPTH):
        for n in SMALL_PER_LAYER:
            parts.append(grads[l][n].astype(F32).reshape(-1))
            shapes.append(grads[l][n].shape)
    parts += [drel.reshape(-1), jnp.sum(sq).reshape(1) * (0.5 / D_MODEL)]
    shapes += [drel.shape, (1,)]
    on_landed(0, ATTN_SENT[:1], update("w_up", (last, False)))
    for l in range(DEPTH):
        landed[l]["w_in"] = _unperm_w_in(landed[l]["w_in_p"])
    gathered = update("w_down", ([_pack(parts, 128, 8, F32)], True))[0]
    red = _unpack(sum_devices(gathered, "sum_small"), shapes)
    k = 0
    per = {n: [] for n in SMALL_PER_LAYER}
    for l in range(DEPTH):
        for n in SMALL_PER_LAYER:
            per[n].append(red[k])
            k += 1
    for n in SMALL_PER_LAYER:
        G[n] = jnp.stack(per[n]).reshape((DEPTH, 3, 2 * D_FF) if n == "conv_w" else W[n].shape)
    G["rel_bias"] = red[k]
    loss = red[k + 1][0]
    G["conv_w"] = lax.dynamic_slice_in_dim(G["conv_w"], me * 704, 704, axis=2)

    for n in WEIGHTS:
        if n not in ("w_up", "w_down"):
            update(n)
    return (loss, dx[None], *[G[n] for n in WEIGHTS], *[delta[n] for n in WEIGHTS],
            *[new_m[n] for n in WEIGHTS], *[new_v[n] for n in WEIGHTS])
```

```python
import functools
import math

import numpy as np
import jax
import jax.numpy as jnp
from jax import lax
from jax.experimental import pallas as pl
from jax.experimental.pallas import tpu as pltpu

F32 = jnp.float32
MXU = jnp.bfloat16

N_DEV = 8
DEPTH = 4
D_MODEL = 1024
HEAD_DIM = 64
WINDOW = 128
SWA_Q_HEADS = 8
REL_BUCKETS = 32
REL_MAX_DIST = 128
MLA_QK_DIM = 96
ROPE_DIM = 32
ROPE_THETA = 10000.0
D_FF = 2816
EPS = 1e-6
NEG = -1e30
IN_COLS = 1956
C_QA, C_KA, C_VA = 0, 512, 640
C_QF, C_KF, C_VF = 768, 1024, 1280
C_CQ, C_CKV, C_MISC = 1536, 1792, 1920
ROPE_LANE0 = 64
ADAM_LR, ADAM_B1, ADAM_B2, ADAM_EPS, ADAM_WD, ADAM_STEP = 0.001, 0.9, 0.999, 1e-08, 0.01, 10

VMEM_LIMIT = 56 * 1024 * 1024


def _cparams(sem=None):
    return pltpu.CompilerParams(dimension_semantics=sem, vmem_limit_bytes=VMEM_LIMIT)


def _tile(n, pref):
    if n <= pref:
        return n
    t = pref - pref % 128
    while t >= 128:
        if n % t == 0:
            return t
        t -= 128
    return n


def _dot(a, b):
    return jnp.dot(a.astype(MXU), b.astype(MXU), preferred_element_type=F32)


def _dot_nt(a, b):
    return lax.dot_general(a.astype(MXU), b.astype(MXU), (((1,), (1,)), ((), ())),
                           preferred_element_type=F32)


def _rms_fwd(x, g):
    return x * lax.rsqrt(jnp.mean(x * x, axis=-1, keepdims=True) + EPS) * g


def _rms_bwd(dy, x, g, n=None):
    r = lax.rsqrt(jnp.mean(x * x, axis=-1, keepdims=True) + EPS)
    xh = x * r
    dg = jnp.sum(dy * xh, axis=0, keepdims=True)
    dxh = dy * g
    dx = r * (dxh - xh * jnp.mean(dxh * xh, axis=-1, keepdims=True))
    return dx, dg


def _acc_out(ref, val, first):
    @pl.when(first)
    def _():
        ref[...] = val

    @pl.when(jnp.logical_not(first))
    def _():
        ref[...] += val


def norm_matmul(x, g, w, name, lo_tiles=0, tn_pref=512, w_transposed=False, h_transposed=True, exch=None):
    T, K = x.shape
    N = w.shape[0] if w_transposed else w.shape[1]
    tm, tn = _tile(T, 1024), _tile(N, tn_pref)

    def body(x_ref, g_ref, w_ref, o_ref, hT_ref, *rest):
        h_sc = rest[-1]
        j = pl.program_id(1)

        @pl.when(j == 0)
        def _():
            h = _rms_fwd(x_ref[...], g_ref[...])
            h_sc[...] = h.astype(MXU)
            hT_ref[...] = (h.T if h_transposed else h).astype(MXU)

        r = (_dot_nt if w_transposed else _dot)(h_sc[...], w_ref[...])
        o_ref[...] = r
        if lo_tiles:
            @pl.when(j < lo_tiles)
            def _():
                rest[0][...] = r.astype(MXU)

    h_spec = pl.BlockSpec((K, tm), lambda i, j: (0, i)) if h_transposed else pl.BlockSpec((tm, K), lambda i, j: (i, 0))
    out_specs = [pl.BlockSpec((tm, tn), lambda i, j: (i, j)), h_spec]
    out_shape = [jax.ShapeDtypeStruct((T, N), F32), jax.ShapeDtypeStruct((K, T) if h_transposed else (T, K), MXU)]
    if lo_tiles:
        out_specs.append(pl.BlockSpec((tm, tn), lambda i, j: (i, jnp.minimum(j, lo_tiles - 1))))
        out_shape.append(jax.ShapeDtypeStruct((T, lo_tiles * tn), MXU))
    in_specs = [pl.BlockSpec((tm, K), lambda i, j: (i, 0)),
                pl.BlockSpec((1, K), lambda i, j: (0, 0)),
                pl.BlockSpec((tn, K), lambda i, j: (j, 0)) if w_transposed else
                pl.BlockSpec((K, tn), lambda i, j: (0, j))]
    outs, landed = _call_hosting(body, name, (T // tm, N // tn), [x, g, w], in_specs, out_specs, out_shape,
                                 [pltpu.VMEM((tm, K), MXU)], exch)
    return outs if exch is None else (outs, landed)


def matmul_nn(a, b, name, out_dtype=F32, exch=None):
    M, K = a.shape
    N = b.shape[1]
    tm, tn, tk = _tile(M, 1408), _tile(N, 1536), _tile(K, 1024)
    nk = K // tk

    def body(a_ref, b_ref, o_ref, acc):
        k = pl.program_id(2)
        part = _dot(a_ref[...], b_ref[...])
        _acc_out(acc, part, k == 0)

        @pl.when(k == nk - 1)
        def _():
            o_ref[...] = acc[...].astype(out_dtype)

    outs, landed = _call_hosting(
        body, name, (M // tm, N // tn, nk), [a, b],
        [pl.BlockSpec((tm, tk), lambda i, j, k: (i, k)), pl.BlockSpec((tk, tn), lambda i, j, k: (k, j))],
        [pl.BlockSpec((tm, tn), lambda i, j, k: (i, j))], [jax.ShapeDtypeStruct((M, N), out_dtype)],
        [pltpu.VMEM((tm, tn), F32)], exch)
    return outs[0] if exch is None else (outs[0], landed)


def matmul_nt_normbwd(dy, w, x, g, dres, name):
    T, N = dy.shape
    K = w.shape[0]
    tm, tn = _tile(T, 1024), _tile(N, 1536)
    nj = N // tn

    def body(dy_ref, w_ref, x_ref, g_ref, dres_ref, dx_ref, dg_ref, acc):
        i, j = pl.program_id(0), pl.program_id(1)
        _acc_out(acc, _dot_nt(dy_ref[...], w_ref[...]), j == 0)

        @pl.when(j == nj - 1)
        def _():
            dx, dg = _rms_bwd(acc[...], x_ref[...], g_ref[...])
            dx_ref[...] = dres_ref[...] + dx
            _acc_out(dg_ref, dg, i == 0)

    return pl.pallas_call(
        body, name=name, grid=(T // tm, nj),
        in_specs=[pl.BlockSpec((tm, tn), lambda i, j: (i, j)),
                  pl.BlockSpec((K, tn), lambda i, j: (0, j)),
                  pl.BlockSpec((tm, K), lambda i, j: (i, 0)),
                  pl.BlockSpec((1, K), lambda i, j: (0, 0)),
                  pl.BlockSpec((tm, K), lambda i, j: (i, 0))],
        out_specs=[pl.BlockSpec((tm, K), lambda i, j: (i, 0)),
                   pl.BlockSpec((1, K), lambda i, j: (0, 0))],
        out_shape=[jax.ShapeDtypeStruct((T, K), F32), jax.ShapeDtypeStruct((1, K), F32)],
        scratch_shapes=[pltpu.VMEM((tm, K), F32)],
        compiler_params=_cparams(("arbitrary", "arbitrary")),
    )(dy, w, x, g, dres)


def loss_kernel(y, tgt):
    T, D = y.shape
    tm = _tile(T, 512)

    def body(y_ref, t_ref, dy_ref, acc_ref):
        e = y_ref[...] - t_ref[...]
        dy_ref[...] = e * (1.0 / D)
        _acc_out(acc_ref, jnp.sum(e * e, axis=0, keepdims=True), pl.program_id(0) == 0)

    return pl.pallas_call(
        body, name="loss", grid=(T // tm,),
        in_specs=[pl.BlockSpec((tm, D), lambda i: (i, 0)), pl.BlockSpec((tm, D), lambda i: (i, 0))],
        out_specs=[pl.BlockSpec((tm, D), lambda i: (i, 0)), pl.BlockSpec((1, D), lambda i: (0, 0))],
        out_shape=[jax.ShapeDtypeStruct((T, D), F32), jax.ShapeDtypeStruct((1, D), F32)],
        compiler_params=_cparams(("arbitrary",)),
    )(y, tgt)


def _rope_partner(x):
    lane = lax.broadcasted_iota(jnp.int32, (1, 128), 1)
    return jnp.where(lane < ROPE_LANE0 + ROPE_DIM // 2, pltpu.roll(x, 128 - ROPE_DIM // 2, 1),
                     pltpu.roll(x, ROPE_DIM // 2, 1))


def _rope_apply(x, cos, sin_signed):
    return x * cos + _rope_partner(x) * sin_signed


def _rope_apply_bwd(dy, cos, sin_signed):
    lane = lax.broadcasted_iota(jnp.int32, (1, 128), 1)
    rotary = (lane >= ROPE_LANE0) & (lane < ROPE_LANE0 + ROPE_DIM)
    return dy * cos + jnp.where(rotary, _rope_partner(dy * sin_signed), 0.0)


def rope_tables(T):
    pos = jnp.arange(T, dtype=F32)
    inv_freq = ROPE_THETA ** (-(jnp.arange(ROPE_DIM // 2, dtype=F32) * 2.0 / ROPE_DIM))
    ang = pos[:, None] * inv_freq[None, :]
    cos, sin = jnp.cos(ang), jnp.sin(ang)
    z = jnp.zeros((T, ROPE_LANE0), F32)
    z2 = jnp.zeros((T, 128 - ROPE_LANE0 - ROPE_DIM), F32)
    cosr = jnp.concatenate([z, cos, cos, z2], axis=1)
    sinr = jnp.concatenate([z, -sin, sin, z2], axis=1)
    return cosr, sinr


def mla_prep(proj, gq, gkv, wuq, wukv, cosr, sinr):
    T = proj.shape[0]
    tm = _tile(T, 512)

    def body(cq_ref, ckv_ref, misc_ref, gq_ref, gkv_ref, wuq_ref, wukv_ref, cos_ref, sin_ref,
             q_ref, k_ref, v_ref, cqT_ref, ckvT_ref):
        lane = lax.broadcasted_iota(jnp.int32, (1, 128), 1)
        cosr_, sinr_ = cos_ref[...], sin_ref[...]
        cosq = cosr_ + jnp.where(lane < ROPE_LANE0, 1.0, 0.0)
        cqn = _rms_fwd(cq_ref[...], gq_ref[...])
        cqT_ref[...] = cqn.T.astype(MXU)
        qm = _dot(cqn, wuq_ref[...])
        q_ref[...] = jnp.concatenate(
            [_rope_apply(qm[:, 128 * h:128 * (h + 1)], cosq, sinr_) for h in range(4)], axis=1).astype(MXU)
        ckvn = _rms_fwd(ckv_ref[...], gkv_ref[...])
        ckvT_ref[...] = ckvn.T.astype(MXU)
        kv = _dot(ckvn, wukv_ref[...])
        kr = _rope_apply(misc_ref[...], cosr_, sinr_)
        k_ref[...] = jnp.concatenate(
            [kv[:, 128 * h:128 * (h + 1)] + kr for h in range(4)], axis=1).astype(MXU)
        v_ref[...] = kv[:, 512:768].astype(MXU)

    row = lambda i: (i, 0)
    const = lambda i: (0, 0)
    return pl.pallas_call(
        body, name="mla_prep", grid=(T // tm,),
        in_specs=[pl.BlockSpec((tm, 256), lambda i: (i, C_CQ // 256)),
                  pl.BlockSpec((tm, 128), lambda i: (i, C_CKV // 128)),
                  pl.BlockSpec((tm, 128), lambda i: (i, C_MISC // 128)),
                  pl.BlockSpec((1, 256), const), pl.BlockSpec((1, 128), const),
                  pl.BlockSpec((256, 512), const), pl.BlockSpec((128, 768), const),
                  pl.BlockSpec((tm, 128), row), pl.BlockSpec((tm, 128), row)],
        out_specs=[pl.BlockSpec((tm, 512), row), pl.BlockSpec((tm, 512), row), pl.BlockSpec((tm, 256), row),
                   pl.BlockSpec((256, tm), lambda i: (0, i)), pl.BlockSpec((128, tm), lambda i: (0, i))],
        out_shape=[jax.ShapeDtypeStruct((T, 512), MXU), jax.ShapeDtypeStruct((T, 512), MXU),
                   jax.ShapeDtypeStruct((T, 256), MXU),
                   jax.ShapeDtypeStruct((256, T), MXU), jax.ShapeDtypeStruct((128, T), MXU)],
        compiler_params=_cparams(("parallel",)),
    )(proj, proj, proj, gq, gkv, wuq, wukv, cosr, sinr)


def mla_prep_bwd(dq, dk, dv, proj, gq, gkv, wuq, wukv, cosr, sinr):
    T = proj.shape[0]
    tm = _tile(T, 512)

    def body(dq_ref, dk_ref, dv_ref, cq_ref, ckv_ref, gq_ref, gkv_ref, wuq_ref, wukv_ref, cos_ref, sin_ref,
             dqm_ref, dkv_ref, dcq_ref, dckv_ref, dmisc_ref, dgq_ref, dgkv_ref):
        first = pl.program_id(0) == 0
        lane = lax.broadcasted_iota(jnp.int32, (1, 128), 1)
        cosr_, sinr_ = cos_ref[...], sin_ref[...]
        cosq = cosr_ + jnp.where(lane < ROPE_LANE0, 1.0, 0.0)
        dqv = dq_ref[...]
        dqm = jnp.concatenate(
            [_rope_apply_bwd(dqv[:, 128 * h:128 * (h + 1)], cosq, sinr_) for h in range(4)], axis=1)
        dqm_ref[...] = dqm.astype(MXU)
        dcq, dgq = _rms_bwd(_dot_nt(dqm, wuq_ref[...]), cq_ref[...], gq_ref[...])
        dcq_ref[...] = dcq
        _acc_out(dgq_ref, dgq, first)
        dkv_ = dk_ref[...]
        heads = [dkv_[:, 128 * h:128 * (h + 1)] for h in range(4)]
        dkr = heads[0] + heads[1] + heads[2] + heads[3]
        dmisc_ref[...] = _rope_apply_bwd(dkr, cosr_, sinr_)
        dkvm = jnp.concatenate([jnp.where(lane < ROPE_LANE0, hd, 0.0) for hd in heads] + [dv_ref[...]], axis=1)
        dkv_ref[...] = dkvm.astype(MXU)
        dckv, dgkv = _rms_bwd(_dot_nt(dkvm, wukv_ref[...]), ckv_ref[...], gkv_ref[...])
        dckv_ref[...] = dckv
        _acc_out(dgkv_ref, dgkv, first)

    row = lambda i: (i, 0)
    const = lambda i: (0, 0)
    return pl.pallas_call(
        body, name="mla_prep_bwd", grid=(T // tm,),
        in_specs=[pl.BlockSpec((tm, 512), row), pl.BlockSpec((tm, 512), row), pl.BlockSpec((tm, 256), row),
                  pl.BlockSpec((tm, 256), lambda i: (i, C_CQ // 256)),
                  pl.BlockSpec((tm, 128), lambda i: (i, C_CKV // 128)),
                  pl.BlockSpec((1, 256), const), pl.BlockSpec((1, 128), const),
                  pl.BlockSpec((256, 512), const), pl.BlockSpec((128, 768), const),
                  pl.BlockSpec((tm, 128), row), pl.BlockSpec((tm, 128), row)],
        out_specs=[pl.BlockSpec((tm, 512), row), pl.BlockSpec((tm, 768), row), pl.BlockSpec((tm, 256), row),
                   pl.BlockSpec((tm, 128), row), pl.BlockSpec((tm, 128), row),
                   pl.BlockSpec((1, 256), const), pl.BlockSpec((1, 128), const)],
        out_shape=[jax.ShapeDtypeStruct((T, 512), MXU), jax.ShapeDtypeStruct((T, 768), MXU),
                   jax.ShapeDtypeStruct((T, 256), F32), jax.ShapeDtypeStruct((T, 128), F32),
                   jax.ShapeDtypeStruct((T, 128), F32),
                   jax.ShapeDtypeStruct((1, 256), F32), jax.ShapeDtypeStruct((1, 128), F32)],
        compiler_params=_cparams(("arbitrary",)),
    )(dq, dk, dv, proj, proj, gq, gkv, wuq, wukv, cosr, sinr)


def _split3(x):
    hi = x.astype(MXU)
    r1 = x - hi.astype(F32)
    mid = r1.astype(MXU)
    lo = (r1 - mid.astype(F32)).astype(MXU)
    return hi, mid, lo


def _tri_matmul(tri, x):
    hi, mid, lo = _split3(x)
    d = lambda p: jnp.dot(tri, p, preferred_element_type=F32)
    return d(hi) + d(mid) + d(lo)


def _log_sigmoid(z):
    return jnp.minimum(z, 0.0) - jnp.log(1.0 + jnp.exp(-jnp.abs(z)))


def fox_gate(proj, fbias):
    T = proj.shape[0]
    tb = _tile(T, 512)

    def body(misc_ref, b_ref, fc_ref, fr_ref, frep_ref, carry):
        @pl.when(pl.program_id(0) == 0)
        def _():
            carry[...] = jnp.zeros_like(carry)

        lane = lax.broadcasted_iota(jnp.int32, (1, 128), 1)
        lf = jnp.where(lane < 4, _log_sigmoid(misc_ref[...] + b_ref[...]), 0.0)
        r = lax.broadcasted_iota(jnp.int32, (tb, tb), 0)
        c = lax.broadcasted_iota(jnp.int32, (tb, tb), 1)
        tri = jnp.where(r >= c, 1.0, 0.0).astype(MXU)
        F = _tri_matmul(tri, lf) + carry[...]
        carry[...] = carry[...] + jnp.sum(lf, axis=0, keepdims=True)
        fc_ref[0] = F
        fc_ref[1] = pltpu.roll(F, 126, 1)
        ft = F.T[0:8, :]
        fr_ref[0] = ft
        fr_ref[1] = pltpu.roll(ft, 6, 0)
        for h in range(4):
            frep_ref[h] = jnp.broadcast_to(_lane_pick(F, h), (tb, 128))

    return pl.pallas_call(
        body, name="fox_gate", grid=(T // tb,),
        in_specs=[pl.BlockSpec((tb, 128), lambda i: (i, C_MISC // 128)), pl.BlockSpec((1, 128), lambda i: (0, 0))],
        out_specs=[pl.BlockSpec((2, tb, 128), lambda i: (0, i, 0)), pl.BlockSpec((2, 8, tb), lambda i: (0, 0, i)),
                   pl.BlockSpec((4, tb, 128), lambda i: (0, i, 0))],
        out_shape=[jax.ShapeDtypeStruct((2, T, 128), F32), jax.ShapeDtypeStruct((2, 8, T), F32),
                   jax.ShapeDtypeStruct((4, T, 128), F32)],
        scratch_shapes=[pltpu.VMEM((1, 128), F32)],
        compiler_params=_cparams(("arbitrary",)),
    )(proj, fbias)


def fox_gate_bwd(dFq, dFk, proj, fbias):
    T = proj.shape[0]
    tb = _tile(T, 512)
    nb = T // tb

    def body(dq_ref, dk_ref, misc_ref, b_ref, dm_ref, db_ref, carry):
        first = pl.program_id(0) == 0

        @pl.when(first)
        def _():
            carry[...] = jnp.zeros_like(carry)

        lane = lax.broadcasted_iota(jnp.int32, (1, 128), 1)
        dF = jnp.where(lane < 4, (dq_ref[0] + dk_ref[0]) + pltpu.roll(dq_ref[1] + dk_ref[1], 2, 1), 0.0)
        r = lax.broadcasted_iota(jnp.int32, (tb, tb), 0)
        c = lax.broadcasted_iota(jnp.int32, (tb, tb), 1)
        tri = jnp.where(r <= c, 1.0, 0.0).astype(MXU)
        dlf = _tri_matmul(tri, dF) + carry[...]
        carry[...] = carry[...] + jnp.sum(dF, axis=0, keepdims=True)
        z = misc_ref[...] + b_ref[...]
        dz = jnp.where(lane < 4, dlf * (1.0 / (1.0 + jnp.exp(z))), 0.0)
        dm_ref[...] = dz
        _acc_out(db_ref, jnp.sum(dz, axis=0, keepdims=True), first)

    return pl.pallas_call(
        body, name="fox_gate_bwd", grid=(nb,),
        in_specs=[pl.BlockSpec((2, tb, 128), lambda i: (0, nb - 1 - i, 0)),
                  pl.BlockSpec((2, tb, 128), lambda i: (0, nb - 1 - i, 0)),
                  pl.BlockSpec((tb, 128), lambda i: (nb - 1 - i, C_MISC // 128)),
                  pl.BlockSpec((1, 128), lambda i: (0, 0))],
        out_specs=[pl.BlockSpec((tb, 128), lambda i: (nb - 1 - i, 0)), pl.BlockSpec((1, 128), lambda i: (0, 0))],
        out_shape=[jax.ShapeDtypeStruct((T, 128), F32), jax.ShapeDtypeStruct((1, 128), F32)],
        scratch_shapes=[pltpu.VMEM((1, 128), F32)],
        compiler_params=_cparams(("arbitrary",)),
    )(dFq, dFk, proj, fbias)


FLASH_TILE = 512


def _row_stat_tile(a, b, n):
    at = jnp.broadcast_to(a, (n, 128)).T[0:8, :]
    bt = jnp.broadcast_to(b, (n, 128)).T[0:8, :]
    sub = lax.broadcasted_iota(jnp.int32, (8, 1), 0)
    return jnp.where(sub == 0, at, jnp.where(sub == 1, bt, 0.0))


def _col_stat_tile(a, b):
    lane = lax.broadcasted_iota(jnp.int32, (1, 128), 1)
    return jnp.where(lane == 0, a, jnp.where(lane == 1, b, 0.0))


def _lane_pick(x, h):
    lane = lax.broadcasted_iota(jnp.int32, (1, 128), 1)
    return jnp.sum(jnp.where(lane == h, x, 0.0), axis=1, keepdims=True)


def _half_mask(h):
    lane = lax.broadcasted_iota(jnp.int32, (1, 128), 1)
    return (lane // HEAD_DIM) == h


def _call_hosting(body, name, grid, args, in_specs, out_specs, out_shape, scratch, exch):
    n_out = len(out_shape)
    if exch is not None:
        body, (xargs, xin, xout, xshape, xscratch) = hosted_exchange(
            body, len(args), n_out, len(scratch), grid, *exch)
        args, in_specs, out_specs = args + xargs, in_specs + xin, out_specs + xout
        out_shape, scratch = out_shape + xshape, scratch + xscratch
    res = pl.pallas_call(
        body, name=name, grid=grid, in_specs=in_specs, out_specs=out_specs, out_shape=out_shape,
        scratch_shapes=scratch, compiler_params=_cparams(("arbitrary",) * len(grid)),
    )(*args)
    return res[:n_out], res[n_out:]


def flash_fwd(q, k, v, frep, frow, *, qblk, kblk, vblk, nq, scale, name, exch=None):
    T = q.shape[0]
    tk = _tile(T, FLASH_TILE)
    tq = _tile(T, 2 * FLASH_TILE)
    per_q = tq // tk
    wq = 128 * nq
    has_f = frep is not None

    def body(*refs):
        if has_f:
            q_ref, k_ref, v_ref, fk_ref, fr_ref, o_ref, lr_ref, vT_sc, m_sc, acc_sc = refs
        else:
            q_ref, k_ref, v_ref, o_ref, lr_ref, vT_sc, m_sc, acc_sc = refs
        i = pl.program_id(1)

        @pl.when(i == 0)
        def _():
            vT_sc[...] = v_ref[...].astype(F32).T.astype(MXU)

        key_row = lax.broadcasted_iota(jnp.int32, (tk, 1), 0)
        q_col = lax.broadcasted_iota(jnp.int32, (1, tq), 1)
        row_half = lax.broadcasted_iota(jnp.int32, (128, 1), 0) // HEAD_DIM
        qb = q_ref[...].astype(F32) * scale
        if nq == 1:
            qhs = [jnp.where(_half_mask(h), qb, 0).astype(MXU) for h in range(2)]
        else:
            qhs = [qb[:, 128 * h:128 * (h + 1)].astype(MXU) for h in range(2)]
        for h in range(2):
            m_sc[h] = jnp.full((1, tq), NEG, F32)
            acc_sc[h] = jnp.zeros((128, tq), F32)

        def make_step(diag_block):
            def step(j, carry):
                off = pl.multiple_of(j * tk, tk)
                ks = k_ref[pl.ds(off, tk), :]
                vT = vT_sc[:, pl.ds(off, tk)]
                for h in range(2):
                    kh = ks if nq == 1 else ks[:, 128 * h:128 * (h + 1)]
                    sT = _dot_nt(kh, qhs[h])
                    if has_f:
                        fk = fk_ref[h, pl.ds(off, tk), :]
                        sT = sT + (fr_ref[0, h:h + 1, :] - jnp.concatenate([fk] * (tq // 128), axis=1))
                    if diag_block is not None:
                        sT = jnp.where(key_row + diag_block * tk <= q_col, sT, NEG)
                    m_prev = m_sc[h]
                    m_new = jnp.maximum(m_prev, jnp.max(sT, axis=0, keepdims=True))
                    alpha = jnp.exp(m_prev - m_new)
                    pT = jnp.exp(sT - m_new)
                    vTh = jnp.where(row_half == h, vT, jnp.ones_like(vT))
                    acc_sc[h] = alpha * acc_sc[h] + _dot(vTh, pT)
                    m_sc[h] = m_new
                return carry
            return step

        lax.fori_loop(0, per_q * i, make_step(None), 0)
        for d in range(per_q):
            make_step(d)(per_q * i + d, 0)
        outs, lses = [], []
        for h in range(2):
            acc = acc_sc[h]
            outs.append(acc / pltpu.roll(acc, HEAD_DIM, 0))
            l = acc_sc[h, HEAD_DIM * (1 - h):HEAD_DIM * (1 - h) + 1, :]
            lses.append(m_sc[h] + jnp.log(l))
        o_ref[...] = jnp.where(row_half == 0, outs[0], outs[1]).T
        sub = lax.broadcasted_iota(jnp.int32, (8, 1), 0)
        lr_ref[0] = jnp.where(sub == 0, lses[0], jnp.where(sub == 1, lses[1], 0.0))

    in_specs = [pl.BlockSpec((tq, wq), lambda p, i: (i, qblk + p)),
                pl.BlockSpec((T, wq), lambda p, i: (0, kblk + p)),
                pl.BlockSpec((T, 128), lambda p, i: (0, vblk + p))]
    args = [q, k, v]
    if has_f:
        in_specs += [pl.BlockSpec((2, T, 128), lambda p, i: (p, 0, 0)),
                     pl.BlockSpec((1, 8, tq), lambda p, i: (p, 0, i))]
        args += [frep, frow]
    out_specs = [pl.BlockSpec((tq, 128), lambda p, i: (i, p)), pl.BlockSpec((1, 8, tq), lambda p, i: (p, 0, i))]
    out_shape = [jax.ShapeDtypeStruct((T, 256), F32), jax.ShapeDtypeStruct((2, 8, T), F32)]
    scratch = [pltpu.VMEM((128, T), MXU), pltpu.VMEM((2, 1, tq), F32), pltpu.VMEM((2, 128, tq), F32)]
    return _call_hosting(body, name, (2, T // tq), args, in_specs, out_specs, out_shape, scratch, exch)


def flash_bwd(q, k, v, do, o, lrow, fcol, frow, *, qblk, kblk, vblk, nq, scale, name, exch=None):
    T = q.shape[0]
    tq = tk = _tile(T, FLASH_TILE)
    wq = 128 * nq
    nqb = T // tq
    has_f = fcol is not None

    def body(*refs):
        if has_f:
            (q_ref, k_ref, v_ref, do_ref, o_ref, lr_ref, fc_ref, fr_ref,
             dq_ref, dk_ref, dv_ref, df_ref, dfq_ref, dk_sc, dv_sc, dqT_sc, d_sc, df_sc, dfq_sc) = refs
        else:
            q_ref, k_ref, v_ref, do_ref, o_ref, lr_ref, dq_ref, dk_ref, dv_ref, dk_sc, dv_sc, dqT_sc, d_sc = refs
        j = pl.program_id(1)
        diag = lax.broadcasted_iota(jnp.int32, (tk, 1), 0) <= lax.broadcasted_iota(jnp.int32, (1, tq), 1)
        hms = [_half_mask(h) for h in range(2)]

        @pl.when(j == 0)
        def _():
            dqT_sc[...] = jnp.zeros_like(dqT_sc)
            if has_f:
                dfq_sc[...] = jnp.zeros_like(dfq_sc)

            def delta(b, carry):
                off = pl.multiple_of(b * tq, tq)
                prod = do_ref[pl.ds(off, tq), :] * o_ref[pl.ds(off, tq), :]
                Ds = [jnp.sum(jnp.where(hms[h], prod, 0.0), axis=1, keepdims=True) for h in range(2)]
                d_sc[:, pl.ds(off, tq)] = _row_stat_tile(Ds[0], Ds[1], tq)
                return carry

            lax.fori_loop(0, nqb, delta, 0)

        kb = k_ref[...]
        vb = v_ref[...]
        if nq == 1:
            khs = [jnp.where(hms[h], kb, 0).astype(MXU) for h in range(2)]
        else:
            khs = [kb[:, 128 * h:128 * (h + 1)].astype(MXU) for h in range(2)]
        kTs = [kh.astype(F32).T.astype(MXU) for kh in khs]
        kss = [(kh.astype(F32) * scale).astype(MXU) for kh in khs]
        vhs = [jnp.where(hms[h], vb, 0).astype(MXU) for h in range(2)]
        fks = [_lane_pick(fc_ref[0], h) for h in range(2)] if has_f else None
        dv_sc[...] = jnp.zeros_like(dv_sc)
        dk_sc[...] = jnp.zeros_like(dk_sc)
        if has_f:
            df_sc[...] = jnp.zeros_like(df_sc)

        def make_step(masked):
            def step(i, carry):
                off = pl.multiple_of(i * tq, tq)
                qs = q_ref[pl.ds(off, tq), :]
                dos = do_ref[pl.ds(off, tq), :]
                for h in range(2):
                    qh = qs if nq == 1 else qs[:, 128 * h:128 * (h + 1)]
                    sT = _dot_nt(kss[h], qh)
                    if has_f:
                        sT = sT + (fr_ref[0, h:h + 1, pl.ds(off, tq)] - fks[h])
                    pT = jnp.exp(sT - lr_ref[0, h:h + 1, pl.ds(off, tq)])
                    if masked:
                        pT = jnp.where(diag, pT, 0.0)
                    dsT = pT * (_dot_nt(vhs[h], dos) - d_sc[h:h + 1, pl.ds(off, tq)])
                    dv_sc[...] += _dot(pT, jnp.where(hms[h], dos, 0))
                    qq = jnp.where(hms[h], qs, 0) if nq == 1 else qh
                    dk_sc[h if nq == 2 else 0] += _dot(dsT, qq)
                    dqT_sc[h if nq == 2 else 0, :, pl.ds(off, tq)] += _dot(kTs[h], dsT)
                    if has_f:
                        part = dsT[:, 0:128]
                        for c in range(1, tq // 128):
                            part = part + dsT[:, 128 * c:128 * (c + 1)]
                        df_sc[h] += part
                        dfq_sc[h:h + 1, pl.ds(off, tq)] += jnp.sum(dsT, axis=0, keepdims=True)
                return carry
            return step

        make_step(True)(j, 0)
        lax.fori_loop(j + 1, nqb, make_step(False), 0)
        if nq == 1:
            dk_ref[...] = dk_sc[0] * scale
        else:
            dk_ref[...] = jnp.concatenate([dk_sc[0], dk_sc[1]], axis=1) * scale
        dv_ref[...] = dv_sc[...]
        if has_f:
            df_ref[0] = _col_stat_tile(-jnp.sum(df_sc[0], axis=1, keepdims=True),
                                       -jnp.sum(df_sc[1], axis=1, keepdims=True))

        @pl.when(j == nqb - 1)
        def _():
            if nq == 1:
                dq_ref[...] = dqT_sc[0].T * scale
            else:
                dq_ref[...] = jnp.concatenate([dqT_sc[0].T, dqT_sc[1].T], axis=1) * scale
            if has_f:
                sub = lax.broadcasted_iota(jnp.int32, (128, 1), 0)
                rows = jnp.where(sub == 0, dfq_sc[0:1, :], jnp.where(sub == 1, dfq_sc[1:2, :], 0.0))
                dfq_ref[0] = rows.T

    in_specs = [pl.BlockSpec((T, wq), lambda p, j: (0, qblk + p)),
                pl.BlockSpec((tk, wq), lambda p, j: (j, kblk + p)),
                pl.BlockSpec((tk, 128), lambda p, j: (j, vblk + p)),
                pl.BlockSpec((T, 128), lambda p, j: (0, p)),
                pl.BlockSpec((T, 128), lambda p, j: (0, p)),
                pl.BlockSpec((1, 8, T), lambda p, j: (p, 0, 0))]
    args = [q, k, v, do, o, lrow]
    out_specs = [pl.BlockSpec((T, wq), lambda p, j: (0, p)),
                 pl.BlockSpec((tk, wq), lambda p, j: (j, p)), pl.BlockSpec((tk, 128), lambda p, j: (j, p))]
    out_shape = [jax.ShapeDtypeStruct((T, 2 * wq), F32), jax.ShapeDtypeStruct((T, 2 * wq), F32),
                 jax.ShapeDtypeStruct((T, 256), F32)]
    scratch = [pltpu.VMEM((nq, tk, 128), F32), pltpu.VMEM((tk, 128), F32), pltpu.VMEM((nq, 128, T), F32),
               pltpu.VMEM((8, T), F32)]
    if has_f:
        in_specs += [pl.BlockSpec((1, tk, 128), lambda p, j: (p, j, 0)),
                     pl.BlockSpec((1, 8, T), lambda p, j: (p, 0, 0))]
        args += [fcol, frow]
        out_specs += [pl.BlockSpec((1, tk, 128), lambda p, j: (p, j, 0)),
                      pl.BlockSpec((1, T, 128), lambda p, j: (p, 0, 0))]
        out_shape += [jax.ShapeDtypeStruct((2, T, 128), F32), jax.ShapeDtypeStruct((2, T, 128), F32)]
        scratch += [pltpu.VMEM((2, tk, 128), F32), pltpu.VMEM((8, T), F32)]
    return _call_hosting(body, name, (2, T // tk), args, in_specs, out_specs, out_shape, scratch, exch)


def _swa_align(pair, e, h):
    sel = jnp.where(_half_mask(e), pair, 0.0)
    if e == h:
        return sel
    return pltpu.roll(sel, HEAD_DIM, 1)


def _swa_mask(n):
    W = WINDOW
    qi = lax.broadcasted_iota(jnp.int32, (W, 2 * W), 0) + W
    kj = lax.broadcasted_iota(jnp.int32, (W, 2 * W), 1)
    dist = qi - kj
    return (dist >= 0) & (dist < W) & ((n > 0) | (kj >= W))


def swa_fwd(proj, bias, sinks, exch=None):
    T = proj.shape[0]
    W = WINDOW
    nb = T // W
    scale = HEAD_DIM ** -0.5

    def body(sink_ref, q_ref, kp_ref, kc_ref, vp_ref, vc_ref, b_ref, o_ref, l_ref):
        n = pl.program_id(0)
        mask = _swa_mask(n)
        kband = jnp.concatenate([kp_ref[...], kc_ref[...]], axis=0).astype(MXU)
        vband = jnp.concatenate([vp_ref[...], vc_ref[...]], axis=0).astype(MXU)
        lane = lax.broadcasted_iota(jnp.int32, (1, 128), 1)
        lse_tile = jnp.zeros((W, 128), F32)
        pairs = []
        for h in range(2):
            full = []
            for g in range(4):
                hq = 4 * h + g
                qa = _swa_align(q_ref[:, 128 * (hq // 2):128 * (hq // 2 + 1)], hq % 2, h)
                s = _dot_nt(qa, kband) * scale + b_ref[hq]
                s = jnp.where(mask, s, NEG)
                sink = sink_ref[hq]
                m = jnp.maximum(jnp.max(s, axis=1, keepdims=True), sink)
                e = jnp.exp(s - m)
                l = jnp.sum(e, axis=1, keepdims=True) + jnp.exp(sink - m)
                r = jnp.where(_half_mask(h), _dot(e, vband), 0.0) / l
                full.append(r + pltpu.roll(r, HEAD_DIM, 1))
                lse_tile = jnp.where(lane == hq, m + jnp.log(l), lse_tile)
            pairs.append(jnp.where(_half_mask(0), full[0], full[1]))
            pairs.append(jnp.where(_half_mask(0), full[2], full[3]))
        o_ref[...] = jnp.concatenate(pairs, axis=1)
        l_ref[...] = lse_tile

    prev = lambda n: (jnp.maximum(n - 1, 0), C_KA // 128)
    cur = lambda n: (n, C_KA // 128)
    prev_v = lambda n: (jnp.maximum(n - 1, 0), C_VA // 128)
    cur_v = lambda n: (n, C_VA // 128)
    return _call_hosting(
        body, "swa_fwd", (nb,), [sinks, proj, proj, proj, proj, proj, bias],
        [pl.BlockSpec(memory_space=pltpu.SMEM),
         pl.BlockSpec((W, 512), lambda n: (n, 0)),
         pl.BlockSpec((W, 128), prev), pl.BlockSpec((W, 128), cur),
         pl.BlockSpec((W, 128), prev_v), pl.BlockSpec((W, 128), cur_v),
         pl.BlockSpec((8, W, 2 * W), lambda n: (0, 0, 0))],
        [pl.BlockSpec((W, 512), lambda n: (n, 0)), pl.BlockSpec((W, 128), lambda n: (n, 0))],
        [jax.ShapeDtypeStruct((T, 512), F32), jax.ShapeDtypeStruct((T, 128), F32)], [], exch)


def swa_bwd(proj, bias, sinks, do, o, lse, exch=None):
    T = proj.shape[0]
    W = WINDOW
    nb = T // W
    scale = HEAD_DIM ** -0.5

    def body(sink_ref, q_ref, kp_ref, kc_ref, vp_ref, vc_ref, b_ref, do_ref, o_ref, l_ref,
             dq_ref, dk_ref, dv_ref, db_ref, dsk_ref, ck, cv):
        n = pl.program_id(0)

        @pl.when(n == 0)
        def _():
            ck[...] = jnp.zeros_like(ck)
            cv[...] = jnp.zeros_like(cv)
            db_ref[...] = jnp.zeros_like(db_ref)
            dsk_ref[...] = jnp.zeros_like(dsk_ref)

        @pl.when(n < nb)
        def _():
            mask = _swa_mask(n)
            kb32 = jnp.concatenate([kp_ref[...], kc_ref[...]], axis=0)
            vb32 = jnp.concatenate([vp_ref[...], vc_ref[...]], axis=0)
            kband = kb32.astype(MXU)
            sub = lax.broadcasted_iota(jnp.int32, (8, 1), 0)
            dk_band = jnp.zeros((2 * W, 128), F32)
            dv_band = jnp.zeros((2 * W, 128), F32)
            dsk = jnp.zeros((8, 128), F32)
            dq_pairs = []
            mask4 = jnp.concatenate([mask] * 4, axis=0)
            for h in range(2):
                hm = _half_mask(h)
                km = jnp.where(hm, kb32, 0.0).astype(MXU)
                vm = jnp.where(hm, vb32, 0.0).astype(MXU)
                pbs = [slice(128 * ((4 * h + g) // 2), 128 * ((4 * h + g) // 2 + 1)) for g in range(4)]
                q4 = jnp.concatenate([_swa_align(q_ref[:, pbs[g]], g % 2, h) for g in range(4)], axis=0)
                do4 = jnp.concatenate([_swa_align(do_ref[:, pbs[g]], g % 2, h) for g in range(4)], axis=0)
                D4 = jnp.concatenate(
                    [jnp.sum(jnp.where(_half_mask(g % 2), do_ref[:, pbs[g]] * o_ref[:, pbs[g]], 0.0), axis=1,
                             keepdims=True) for g in range(4)], axis=0)
                lse4 = jnp.concatenate([_lane_pick(l_ref[...], 4 * h + g) for g in range(4)], axis=0)
                sink4 = jnp.concatenate([jnp.full((W, 1), sink_ref[4 * h + g], F32) for g in range(4)], axis=0)
                s = _dot_nt(q4, kband) * scale + b_ref[4 * h:4 * h + 4].reshape(4 * W, 2 * W)
                p = jnp.where(mask4, jnp.exp(s - lse4), 0.0)
                sd = jnp.exp(sink4 - lse4) * D4
                for g in range(4):
                    dsk = dsk + jnp.where(sub == 4 * h + g,
                                          -jnp.sum(sd[W * g:W * (g + 1)], axis=0, keepdims=True), 0.0)
                ds = p * (_dot_nt(do4, vm) - D4)
                db_ref[4 * h:4 * h + 4] += ds.reshape(4, W, 2 * W)
                dq = _dot(ds, km) * scale
                dq = dq + pltpu.roll(dq, HEAD_DIM, 1)
                dk_band = dk_band + _dot(ds.T, q4) * scale
                dv_band = dv_band + _dot(p.T, do4)
                dq_pairs.append(jnp.where(_half_mask(0), dq[0:W], dq[W:2 * W]))
                dq_pairs.append(jnp.where(_half_mask(0), dq[2 * W:3 * W], dq[3 * W:4 * W]))
            dq_ref[...] = jnp.concatenate(dq_pairs, axis=1)
            dsk_ref[...] += dsk
            dk_ref[...] = ck[...] + dk_band[0:W]
            dv_ref[...] = cv[...] + dv_band[0:W]
            ck[...] = dk_band[W:2 * W]
            cv[...] = dv_band[W:2 * W]

        @pl.when(n == nb)
        def _():
            dk_ref[...] = ck[...]
            dv_ref[...] = cv[...]

    cl = lambda n: jnp.minimum(n, nb - 1)
    pv = lambda n: jnp.maximum(jnp.minimum(n, nb - 1) - 1, 0)
    return _call_hosting(
        body, "swa_bwd", (nb + 1,), [sinks, proj, proj, proj, proj, proj, bias, do, o, lse],
        [pl.BlockSpec(memory_space=pltpu.SMEM),
         pl.BlockSpec((W, 512), lambda n: (cl(n), 0)),
         pl.BlockSpec((W, 128), lambda n: (pv(n), C_KA // 128)),
         pl.BlockSpec((W, 128), lambda n: (cl(n), C_KA // 128)),
         pl.BlockSpec((W, 128), lambda n: (pv(n), C_VA // 128)),
         pl.BlockSpec((W, 128), lambda n: (cl(n), C_VA // 128)),
         pl.BlockSpec((8, W, 2 * W), lambda n: (0, 0, 0)),
         pl.BlockSpec((W, 512), lambda n: (cl(n), 0)),
         pl.BlockSpec((W, 512), lambda n: (cl(n), 0)),
         pl.BlockSpec((W, 128), lambda n: (cl(n), 0))],
        [pl.BlockSpec((W, 512), lambda n: (cl(n), 0)),
         pl.BlockSpec((W, 128), lambda n: (jnp.maximum(n - 1, 0), 0)),
         pl.BlockSpec((W, 128), lambda n: (jnp.maximum(n - 1, 0), 0)),
         pl.BlockSpec((8, W, 2 * W), lambda n: (0, 0, 0)),
         pl.BlockSpec((8, 128), lambda n: (0, 0))],
        [jax.ShapeDtypeStruct((T, 512), F32), jax.ShapeDtypeStruct((T, 128), F32),
         jax.ShapeDtypeStruct((T, 128), F32), jax.ShapeDtypeStruct((8, W, 2 * W), F32),
         jax.ShapeDtypeStruct((8, 128), F32)],
        [pltpu.VMEM((W, 128), F32), pltpu.VMEM((W, 128), F32)], exch)


def swa_bias_table(rel_bias):
    W = WINDOW
    qi = jnp.arange(W, dtype=jnp.int32)[:, None] + W
    kj = jnp.arange(2 * W, dtype=jnp.int32)[None, :]
    dist = qi - kj
    max_exact = REL_BUCKETS // 2
    d = jnp.maximum(dist, 0)
    log_ratio = jnp.log(jnp.maximum(d, 1).astype(F32) / max_exact) / math.log(REL_MAX_DIST / max_exact)
    large = jnp.minimum(max_exact + (log_ratio * (REL_BUCKETS - max_exact)).astype(jnp.int32), REL_BUCKETS - 1)
    bucket = jnp.where(d < max_exact, d, large)
    bucket = bucket.reshape(-1)
    onehot = (bucket[None, :] == jnp.arange(REL_BUCKETS, dtype=jnp.int32)[:, None]).astype(F32)
    bias = jnp.dot(rel_bias.astype(F32).T, onehot, precision=lax.Precision.HIGHEST)
    return bias.reshape(SWA_Q_HEADS, W, 2 * W), bucket


def attn_out(oa, ob, oc, gn, wout, gpost, x):
    T = x.shape[0]
    tm = _tile(T, 512)

    def body(oa_ref, ob_ref, oc_ref, gn_ref, w_ref, gp_ref, x_ref, x2_ref, y_ref, mT_ref):
        g = gn_ref[...]
        mixed = jnp.concatenate([_rms_fwd(oa_ref[...], g[:, 0:512]), _rms_fwd(ob_ref[...], g[:, 512:768]),
                                 _rms_fwd(oc_ref[...], g[:, 768:1024])], axis=1)
        mT_ref[...] = mixed.T.astype(MXU)
        y = _dot(mixed, w_ref[...])
        y_ref[...] = y
        x2_ref[...] = x_ref[...] + _rms_fwd(y, gp_ref[...])

    row = lambda i: (i, 0)
    const = lambda i: (0, 0)
    return pl.pallas_call(
        body, name="attn_out", grid=(T // tm,),
        in_specs=[pl.BlockSpec((tm, 512), row), pl.BlockSpec((tm, 256), row), pl.BlockSpec((tm, 256), row),
                  pl.BlockSpec((1, 1024), const), pl.BlockSpec((1024, 1024), const), pl.BlockSpec((1, 1024), const),
                  pl.BlockSpec((tm, 1024), row)],
        out_specs=[pl.BlockSpec((tm, 1024), row), pl.BlockSpec((tm, 1024), row),
                   pl.BlockSpec((1024, tm), lambda i: (0, i))],
        out_shape=[jax.ShapeDtypeStruct((T, 1024), F32), jax.ShapeDtypeStruct((T, 1024), F32),
                   jax.ShapeDtypeStruct((1024, T), MXU)],
        compiler_params=_cparams(("parallel",)),
    )(oa, ob, oc, gn, wout, gpost, x)


def attn_out_bwd(dx2, y, oa, ob, oc, gn, wout, gpost):
    T = dx2.shape[0]
    tm = _tile(T, 512)

    def body(dx_ref, y_ref, oa_ref, ob_ref, oc_ref, gn_ref, w_ref, gp_ref,
             dy_ref, da_ref, db_ref, dc_ref, dgn_ref, dgp_ref):
        first = pl.program_id(0) == 0
        dy, dgp = _rms_bwd(dx_ref[...], y_ref[...], gp_ref[...])
        dy_ref[...] = dy.astype(MXU)
        _acc_out(dgp_ref, dgp, first)
        dm = _dot_nt(dy, w_ref[...])
        g = gn_ref[...]
        da, dga = _rms_bwd(dm[:, 0:512], oa_ref[...], g[:, 0:512])
        db, dgb = _rms_bwd(dm[:, 512:768], ob_ref[...], g[:, 512:768])
        dc, dgc = _rms_bwd(dm[:, 768:1024], oc_ref[...], g[:, 768:1024])
        da_ref[...] = da
        db_ref[...] = db
        dc_ref[...] = dc
        _acc_out(dgn_ref, jnp.concatenate([dga, dgb, dgc], axis=1), first)

    row = lambda i: (i, 0)
    const = lambda i: (0, 0)
    return pl.pallas_call(
        body, name="attn_out_bwd", grid=(T // tm,),
        in_specs=[pl.BlockSpec((tm, 1024), row), pl.BlockSpec((tm, 1024), row),
                  pl.BlockSpec((tm, 512), row), pl.BlockSpec((tm, 256), row), pl.BlockSpec((tm, 256), row),
                  pl.BlockSpec((1, 1024), const), pl.BlockSpec((1024, 1024), const), pl.BlockSpec((1, 1024), const)],
        out_specs=[pl.BlockSpec((tm, 1024), row), pl.BlockSpec((tm, 512), row), pl.BlockSpec((tm, 256), row),
                   pl.BlockSpec((tm, 256), row), pl.BlockSpec((1, 1024), const), pl.BlockSpec((1, 1024), const)],
        out_shape=[jax.ShapeDtypeStruct((T, 1024), MXU), jax.ShapeDtypeStruct((T, 512), F32),
                   jax.ShapeDtypeStruct((T, 256), F32), jax.ShapeDtypeStruct((T, 256), F32),
                   jax.ShapeDtypeStruct((1, 1024), F32), jax.ShapeDtypeStruct((1, 1024), F32)],
        compiler_params=_cparams(("arbitrary",)),
    )(dx2, y, oa, ob, oc, gn, wout, gpost)


FF_TILE = 256
_GELU_C = math.sqrt(2.0 / math.pi)


def _gelu(x):
    return 0.5 * x * (1.0 + jnp.tanh(_GELU_C * (x + 0.044715 * x * x * x)))


def _gelu_with_grad(x):
    x2 = x * x
    t = jnp.tanh(_GELU_C * x * (1.0 + 0.044715 * x2))
    h = 0.5 * (1.0 + t)
    return x * h, h + (0.5 * _GELU_C) * x * (1.0 - t * t) * (1.0 + (3 * 0.044715) * x2)


def _conv_taps(u, hal_ref, first):
    row = lax.broadcasted_iota(jnp.int32, (8, 1), 0)
    h6 = jnp.where(first, 0.0, hal_ref[6:7, :])
    h7 = jnp.where(first, 0.0, hal_ref[7:8, :])
    r1, r2 = pltpu.roll(u, 1, 0), pltpu.roll(u, 2, 0)
    r1 = jnp.concatenate([jnp.where(row == 0, h7, r1[0:8]), r1[8:]], axis=0)
    r2 = jnp.concatenate([jnp.where(row == 0, h6, jnp.where(row == 1, h7, r2[0:8])), r2[8:]], axis=0)
    return r1, r2


def ffn_fwd(u0, convw, convb, wdown, gpost, x2, exch=None):
    T = x2.shape[0]
    tm, tn = _tile(T, 1024), FF_TILE
    nj = D_FF // tn

    def body(ug_ref, uu_ref, hg_ref, hu_ref, wg_ref, wu_ref, bg_ref, bu_ref, wd_ref, gp_ref, x_ref, wdp_ref,
             x3_ref, y_ref, aT_ref, acc, a_sc):
        i, j = pl.program_id(0), pl.program_id(1)
        first = i == 0

        @pl.when(j == 0)
        def _():
            acc[...] = jnp.zeros_like(acc)
            a_sc[...] = jnp.zeros_like(a_sc)

        acc[...] += _dot(a_sc[...], wdp_ref[...])

        half = tm // 2

        def conv(u_ref, h_ref, w_ref, b_ref, lo):
            u = u_ref[lo:lo + half, :]
            if lo == 0:
                r1, r2 = _conv_taps(u, h_ref, first)
            else:
                r1, r2 = _conv_taps(u, u_ref.at[lo - 8:lo], False)
            return b_ref[...] + w_ref[0:1, :] * r2 + w_ref[1:2, :] * r1 + w_ref[2:3, :] * u

        a = jnp.concatenate(
            [_gelu(conv(ug_ref, hg_ref, wg_ref, bg_ref, lo)) * conv(uu_ref, hu_ref, wu_ref, bu_ref, lo)
             for lo in (0, half)], axis=0)
        aT_ref[...] = a.T.astype(MXU)
        a_sc[...] = a.astype(MXU)

        @pl.when(j == nj - 1)
        def _():
            y = acc[...] + _dot(a_sc[...], wd_ref[...])
            y_ref[...] = y
            x3_ref[...] = x_ref[...] + _rms_fwd(y, gp_ref[...])

    halo = lambda off: (lambda i, j: (jnp.maximum(i * (tm // 8) - 1, 0), off + j))
    return _call_hosting(
        body, "ffn_fwd", (T // tm, nj), [u0, u0, u0, u0, convw, convw, convb, convb, wdown, gpost, x2, wdown],
        [pl.BlockSpec((tm, tn), lambda i, j: (i, j)), pl.BlockSpec((tm, tn), lambda i, j: (i, nj + j)),
         pl.BlockSpec((8, tn), halo(0)), pl.BlockSpec((8, tn), halo(nj)),
         pl.BlockSpec((3, tn), lambda i, j: (0, j)), pl.BlockSpec((3, tn), lambda i, j: (0, nj + j)),
         pl.BlockSpec((1, tn), lambda i, j: (0, j)), pl.BlockSpec((1, tn), lambda i, j: (0, nj + j)),
         pl.BlockSpec((tn, 1024), lambda i, j: (j, 0)),
         pl.BlockSpec((1, 1024), lambda i, j: (0, 0)),
         pl.BlockSpec((tm, 1024), lambda i, j: (i, 0)),
         pl.BlockSpec((tn, 1024), lambda i, j: (jnp.maximum(j - 1, 0), 0))],
        [pl.BlockSpec((tm, 1024), lambda i, j: (i, 0)), pl.BlockSpec((tm, 1024), lambda i, j: (i, 0)),
         pl.BlockSpec((tn, tm), lambda i, j: (j, i))],
        [jax.ShapeDtypeStruct((T, 1024), F32), jax.ShapeDtypeStruct((T, 1024), F32),
         jax.ShapeDtypeStruct((D_FF, T), MXU)],
        [pltpu.VMEM((tm, 1024), F32), pltpu.VMEM((tm, tn), MXU)], exch)


def ffn_bwd(dx3, y, u0, convw, convb, wdown, gpost, wupT, x2, gfpre, exch=None):
    T = dx3.shape[0]
    tm, tn = _tile(T, 512), FF_TILE
    nj = D_FF // tn
    ni = T // tm

    def body(dx_ref, y_ref, ug_ref, uu_ref, hg_ref, hu_ref, wg_ref, wu_ref, bg_ref, bu_ref, wd_ref, gp_ref,
             wtg_ref, wtu_ref, x2_ref, gf_ref, wdn_ref, wtgp_ref, wtup_ref,
             dy_ref, dug_ref, duu_ref, dcg_ref, dcu_ref, dgp_ref, dx2_ref, dgf_ref,
             dy_sc, dh_sc, da_sc, dug_sc, duu_sc, cg, cu, ag, au):
        s, j = pl.program_id(0), pl.program_id(1)
        i = ni - 1 - s
        first_tok = i == 0
        sub = lax.broadcasted_iota(jnp.int32, (8, 1), 0)
        slot = j % 2

        @pl.when(j == 0)
        def _():
            dy, dgp = _rms_bwd(dx_ref[...], y_ref[...], gp_ref[...])
            dy_sc[...] = dy.astype(MXU)
            dy_ref[...] = dy.astype(MXU)
            _acc_out(dgp_ref, dgp, s == 0)
            dh_sc[...] = jnp.zeros_like(dh_sc)
            da_sc[0] = _dot_nt(dy.astype(MXU), wd_ref[...])
            dug_sc[...] = jnp.zeros_like(dug_sc)
            duu_sc[...] = jnp.zeros_like(duu_sc)

        @pl.when(s == 0)
        def _():
            cg[j] = jnp.zeros((8, tn), F32)
            cu[j] = jnp.zeros((8, tn), F32)
            ag[j] = jnp.zeros((8, tn), F32)
            au[j] = jnp.zeros((8, tn), F32)

        da = da_sc[slot]
        da_sc[1 - slot] = _dot_nt(dy_sc[...], wdn_ref[...])
        dh_sc[...] += _dot(dug_sc[...], wtgp_ref[...]) + _dot(duu_sc[...], wtup_ref[...])

        def conv(u_ref, h_ref, w_ref, b_ref):
            u = u_ref[...]
            r1, r2 = _conv_taps(u, h_ref, first_tok)
            return b_ref[...] + w_ref[0:1, :] * r2 + w_ref[1:2, :] * r1 + w_ref[2:3, :] * u, u, r1, r2

        gate, ugv, g1, g2 = conv(ug_ref, hg_ref, wg_ref, bg_ref)
        up, uuv, u1, u2 = conv(uu_ref, hu_ref, wu_ref, bu_ref)
        gl, dgl = _gelu_with_grad(gate)
        dup = da * gl
        dgate = da * up * dgl

        def conv_bwd(du, u, r1, r2, w_ref, c_ref, a_ref, duT_ref, du_sc):
            nxt = c_ref[j]
            n0, n1 = nxt[0:1, :], nxt[1:2, :]
            f1, f2 = pltpu.roll(du, tm - 1, 0), pltpu.roll(du, tm - 2, 0)
            f1 = jnp.concatenate([f1[:tm - 8], jnp.where(sub == 7, n0, f1[tm - 8:])], axis=0)
            f2 = jnp.concatenate([f2[:tm - 8], jnp.where(sub == 7, n1, jnp.where(sub == 6, n0, f2[tm - 8:]))], axis=0)
            du0 = w_ref[2:3, :] * du + w_ref[1:2, :] * f1 + w_ref[0:1, :] * f2
            duT_ref[...] = du0.T.astype(MXU)
            du_sc[...] = du0.astype(MXU)
            c_ref[j] = du[0:8, :]
            red = lambda v: jnp.sum(v, axis=0, keepdims=True)
            part = jnp.where(sub == 0, red(du * r2), jnp.where(sub == 1, red(du * r1), jnp.where(
                sub == 2, red(du * u), jnp.where(sub == 3, red(du), 0.0))))
            a_ref[j] = a_ref[j] + part
            return a_ref[j]

        dcg_ref[0] = conv_bwd(dgate, ugv, g1, g2, wg_ref, cg, ag, dug_ref, dug_sc)
        dcu_ref[0] = conv_bwd(dup, uuv, u1, u2, wu_ref, cu, au, duu_ref, duu_sc)

        @pl.when(j == nj - 1)
        def _():
            dh = dh_sc[...] + _dot(dug_sc[...], wtg_ref[...]) + _dot(duu_sc[...], wtu_ref[...])
            dx, dgf = _rms_bwd(dh, x2_ref[...], gf_ref[...])
            dx2_ref[...] = dx_ref[...] + dx
            _acc_out(dgf_ref, dgf, s == 0)

    rev = lambda s: ni - 1 - s
    halo = lambda off: (lambda s, j: (jnp.maximum(rev(s) * (tm // 8) - 1, 0), off + j))
    tok = pl.BlockSpec((tm, 1024), lambda s, j: (rev(s), 0))
    vec = pl.BlockSpec((1, 1024), lambda s, j: (0, 0))
    return _call_hosting(
        body, "ffn_bwd", (ni, nj),
        [dx3, y, u0, u0, u0, u0, convw, convw, convb, convb, wdown, gpost, wupT, wupT, x2, gfpre,
         wdown, wupT, wupT],
        [tok, tok,
         pl.BlockSpec((tm, tn), lambda s, j: (rev(s), j)), pl.BlockSpec((tm, tn), lambda s, j: (rev(s), nj + j)),
         pl.BlockSpec((8, tn), halo(0)), pl.BlockSpec((8, tn), halo(nj)),
         pl.BlockSpec((3, tn), lambda s, j: (0, j)), pl.BlockSpec((3, tn), lambda s, j: (0, nj + j)),
         pl.BlockSpec((1, tn), lambda s, j: (0, j)), pl.BlockSpec((1, tn), lambda s, j: (0, nj + j)),
         pl.BlockSpec((tn, 1024), lambda s, j: (j, 0)), vec,
         pl.BlockSpec((tn, 1024), lambda s, j: (j, 0)), pl.BlockSpec((tn, 1024), lambda s, j: (nj + j, 0)),
         tok, vec,
         pl.BlockSpec((tn, 1024), lambda s, j: (jnp.minimum(j + 1, nj - 1), 0)),
         pl.BlockSpec((tn, 1024), lambda s, j: (jnp.maximum(j - 1, 0), 0)),
         pl.BlockSpec((tn, 1024), lambda s, j: (nj + jnp.maximum(j - 1, 0), 0))],
        [tok,
         pl.BlockSpec((tn, tm), lambda s, j: (j, rev(s))), pl.BlockSpec((tn, tm), lambda s, j: (j, rev(s))),
         pl.BlockSpec((1, 8, tn), lambda s, j: (s, 0, j)), pl.BlockSpec((1, 8, tn), lambda s, j: (s, 0, j)),
         vec, tok, vec],
        [jax.ShapeDtypeStruct((T, 1024), MXU), jax.ShapeDtypeStruct((D_FF, T), MXU),
         jax.ShapeDtypeStruct((D_FF, T), MXU),
         jax.ShapeDtypeStruct((ni, 8, D_FF), F32), jax.ShapeDtypeStruct((ni, 8, D_FF), F32),
         jax.ShapeDtypeStruct((1, 1024), F32), jax.ShapeDtypeStruct((T, 1024), F32),
         jax.ShapeDtypeStruct((1, 1024), F32)],
        [pltpu.VMEM((tm, 1024), MXU), pltpu.VMEM((tm, 1024), F32), pltpu.VMEM((2, tm, tn), F32),
         pltpu.VMEM((tm, tn), MXU), pltpu.VMEM((tm, tn), MXU)] + [pltpu.VMEM((nj, 8, tn), F32)] * 4, exch)


ELEMS_PER_BLOCK = 512 * 1024


def _row_block(R, C):
    if R * C <= ELEMS_PER_BLOCK or R % 8:
        return R
    best = 8
    for t in range(8, R + 1, 8):
        if R % t == 0 and t * C <= ELEMS_PER_BLOCK:
            best = t
    return best


def adamw(w, g, m, v, name, exch=None):
    L, R, C = w.shape
    partials = isinstance(g, (list, tuple))
    tr = _row_block(R, 2 * C)
    c1 = 1.0 - ADAM_B1 ** ADAM_STEP
    c2 = 1.0 - ADAM_B2 ** ADAM_STEP

    def body(w_ref, *rest):
        g_refs, (m_ref, v_ref, g_out, d_ref, nm_ref, nv_ref) = rest[:-6], rest[-6:]

        def step(gv):
            g_out[0] = gv
            nm = ADAM_B1 * m_ref[0] + (1.0 - ADAM_B1) * gv
            nv = ADAM_B2 * v_ref[0] + (1.0 - ADAM_B2) * (gv * gv)
            nm_ref[0] = nm
            nv_ref[0] = nv
            d_ref[0] = -ADAM_LR * ((nm / c1) / (jnp.sqrt(nv / c2) + ADAM_EPS) + ADAM_WD * w_ref[0])

        if not partials:
            step(g_refs[0][0])
            return
        for k in range(L):
            @pl.when(pl.program_id(0) == k)
            def _(k=k):
                gv = g_refs[k][0].astype(F32)
                for d in range(1, N_DEV):
                    gv = gv + g_refs[k][d].astype(F32)
                step(gv)

    spec = pl.BlockSpec((1, tr, C), lambda l, i: (l, i, 0))
    if partials:
        gspecs = [pl.BlockSpec((N_DEV, tr, C), lambda l, i, k=k: (0, jnp.where(l == k, i, 0), 0)) for k in range(L)]
        gs = list(g)
    else:
        gspecs, gs = [spec], [g]
    return _call_hosting(body, name, (L, R // tr), [w] + gs + [m, v], [spec] + gspecs + [spec, spec], [spec] * 4,
                         [jax.ShapeDtypeStruct((L, R, C), F32)] * 4, [], exch)


def sum_devices(buf, name):
    _, R, C = buf.shape
    tr = _row_block(R, C * 4)

    def body(b_ref, o_ref):
        acc = b_ref[0].astype(F32)
        for d in range(1, N_DEV):
            acc = acc + b_ref[d].astype(F32)
        o_ref[...] = acc

    return pl.pallas_call(
        body, name=name, grid=(R // tr,),
        in_specs=[pl.BlockSpec((N_DEV, tr, C), lambda i: (0, i, 0))],
        out_specs=pl.BlockSpec((tr, C), lambda i: (i, 0)),
        out_shape=jax.ShapeDtypeStruct((R, C), F32),
        compiler_params=_cparams(("parallel",)),
    )(buf)


def _exchange_copies(src_refs, out_refs, send_sems, recv_sems, gather):
    x, y, c = lax.axis_index("x"), lax.axis_index("y"), lax.axis_index("c")
    me = 4 * x + 2 * y + c
    flip = lambda a, bit: 1 - a if bit else a
    part = lambda ref, d: ref if gather else ref.at[d]
    copies = []
    for k in range(1, N_DEV):
        px, py, pc = flip(x, (k >> 2) & 1), flip(y, (k >> 1) & 1), flip(c, k & 1)
        peer = 4 * px + 2 * py + pc
        for t in range(len(src_refs)):
            sem = t * (N_DEV - 1) + k - 1
            mk = lambda s, d: pltpu.make_async_remote_copy(
                src_ref=s, dst_ref=d, send_sem=send_sems.at[sem], recv_sem=recv_sems.at[sem],
                device_id=(px, py, pc), device_id_type=pl.DeviceIdType.MESH)
            copies.append((mk(part(src_refs[t], peer), out_refs[t].at[me]),
                           mk(part(src_refs[t], me), out_refs[t].at[peer])))
    return me, copies


def exchange(srcs, name, gather):
    n = len(srcs)
    shapes = [(N_DEV,) + s.shape if gather else s.shape for s in srcs]

    def body(*refs):
        src_refs, out_refs = refs[:n], refs[n:2 * n]
        send_sems, recv_sems, local_sems = refs[2 * n:]
        me, copies = _exchange_copies(src_refs, out_refs, send_sems, recv_sems, gather)
        for outgoing, _ in copies:
            outgoing.start()
        mine = [pltpu.make_async_copy(src_refs[t] if gather else src_refs[t].at[me], out_refs[t].at[me],
                                      local_sems.at[t]) for t in range(n)]
        for cp in mine:
            cp.start()
        for _, incoming in copies:
            incoming.wait_recv()
        for outgoing, _ in copies:
            outgoing.wait_send()
        for cp in mine:
            cp.wait()

    return pl.pallas_call(
        body, name=name,
        in_specs=[pl.BlockSpec(memory_space=pl.ANY)] * n, out_specs=[pl.BlockSpec(memory_space=pl.ANY)] * n,
        out_shape=[jax.ShapeDtypeStruct(shp, s.dtype) for shp, s in zip(shapes, srcs)],
        scratch_shapes=[pltpu.SemaphoreType.DMA((n * (N_DEV - 1),)), pltpu.SemaphoreType.DMA((n * (N_DEV - 1),)),
                        pltpu.SemaphoreType.DMA((n,))],
    )(*srcs)


def hosted_exchange(body, n_in, n_out, n_scratch, grid, srcs, gather):
    n = len(srcs)
    shapes = [(N_DEV,) + s.shape if gather else s.shape for s in srcs]

    def wrapped(*refs):
        ins, xin = refs[:n_in], refs[n_in:n_in + n]
        outs = refs[n_in + n:n_in + n + n_out]
        xout = refs[n_in + n + n_out:n_in + 2 * n + n_out]
        rest = refs[n_in + 2 * n + n_out:]
        scratch, (send_sems, recv_sems, local_sems) = rest[:n_scratch], rest[n_scratch:]
        ids = [pl.program_id(a) for a in range(len(grid))]
        first = functools.reduce(jnp.logical_and, [i == 0 for i in ids])
        last = functools.reduce(jnp.logical_and, [i == g - 1 for i, g in zip(ids, grid)])
        me, copies = _exchange_copies(xin, xout, send_sems, recv_sems, gather)
        mine = [pltpu.make_async_copy(xin[t] if gather else xin[t].at[me], xout[t].at[me], local_sems.at[t])
                for t in range(n)]

        @pl.when(first)
        def _():
            for outgoing, _ in copies:
                outgoing.start()
            for cp in mine:
                cp.start()

        body(*ins, *outs, *scratch)

        @pl.when(last)
        def _():
            for _, incoming in copies:
                incoming.wait_recv()
            for outgoing, _ in copies:
                outgoing.wait_send()
            for cp in mine:
                cp.wait()

    any_spec = pl.BlockSpec(memory_space=pl.ANY)
    return wrapped, (list(srcs), [any_spec] * n, [any_spec] * n,
                     [jax.ShapeDtypeStruct(shp, s.dtype) for shp, s in zip(shapes, srcs)],
                     [pltpu.SemaphoreType.DMA((n * (N_DEV - 1),)), pltpu.SemaphoreType.DMA((n * (N_DEV - 1),)),
                      pltpu.SemaphoreType.DMA((n,))])


def _pack(parts, cols, row_align, dtype):
    flat = jnp.concatenate([p.astype(dtype) for p in parts], axis=-1)
    n = flat.shape[-1]
    block = cols * row_align
    total = -(-n // block) * block
    flat = jnp.pad(flat, [(0, 0)] * (flat.ndim - 1) + [(0, total - n)])
    return flat.reshape(flat.shape[:-1] + (total // cols, cols))


def _unpack(buf, shapes):
    lead = buf.shape[:-2]
    flat = buf.reshape(lead + (-1,))
    out, off = [], 0
    for s in shapes:
        n = int(np.prod(s))
        out.append(flat[..., off:off + n].reshape(lead + tuple(s)))
        off += n
    return out


SHARDED = ["w_in", "w_uq", "w_ukv", "w_out", "w_up", "w_down"]
ATTN_SENT = ["w_in_p", "w_uq", "w_ukv", "w_out"]
UP_HALF = 352
FFN_SIDE = ["w_upT", "conv_w", "w_down"]


def _full_from_shards(name, s):
    if name in ("w_in", "w_in_p", "w_out", "w_down", "w_upT"):
        return s.reshape((-1, s.shape[-1]))
    return s.transpose(1, 0, 2).reshape((s.shape[1], -1))


def _shards_from_full(name, f):
    if name in ("w_in", "w_in_p", "w_out", "w_down", "w_upT"):
        return f.reshape((N_DEV, -1, f.shape[-1]))
    return f.reshape((f.shape[0], N_DEV, -1)).transpose(1, 0, 2)


def _perm_w_in(w):
    z = lambda n: jnp.zeros(w.shape[:-1] + (n,), w.dtype)
    return jnp.concatenate([w[..., :1536], w[..., 1540:1924], w[..., 1536:1540], z(60), w[..., 1924:1956], z(32)],
                           axis=-1)


def _unperm_w_in(d):
    return jnp.concatenate([d[..., :1536], d[..., 1920:1924], d[..., 1536:1920], d[..., 1984:2016]], axis=-1)


def _perm_w_uq(w):
    return jnp.pad(w.reshape(256, 4, MLA_QK_DIM), ((0, 0), (0, 0), (0, 128 - MLA_QK_DIM))).reshape(256, 512)


def _unperm_w_uq(d):
    return d.reshape(256, 4, 128)[:, :, :MLA_QK_DIM].reshape(256, 4 * MLA_QK_DIM)


def _perm_w_ukv(w):
    w4 = w.reshape(128, 4, 128)
    k = jnp.pad(w4[:, :, :64], ((0, 0), (0, 0), (0, 64))).reshape(128, 512)
    return jnp.concatenate([k, w4[:, :, 64:].reshape(128, 256)], axis=1)


def _unperm_w_ukv(d):
    dk = d[:, :512].reshape(128, 4, 128)[:, :, :64]
    dv = d[:, 512:].reshape(128, 4, 64)
    return jnp.concatenate([dk, dv], axis=-1).reshape(128, 512)


def _row(v, width=None):
    v = v.reshape(1, -1).astype(F32)
    if width is not None and v.shape[1] < width:
        v = jnp.pad(v, ((0, 0), (0, width - v.shape[1])))
    return v


def _layer_fwd(x, P, shared, send=None, ffn_from=None):
    cosr, sinr, bias = shared
    ex = lambda host: (send[host], True) if send is not None and send.get(host) else None
    proj, hT, projb = norm_matmul(x, P["g_pre"], P["w_in_p"], "in_proj", lo_tiles=2, tn_pref=1024)
    qm, km, vm, cqT, ckvT = mla_prep(proj, P["gq"], P["gkv"], P["w_uq_p"], P["w_ukv_p"], cosr, sinr)
    fcol, frow, frep = fox_gate(proj, P["fbias"])
    (oa, lse_a), _ = swa_fwd(proj, bias, P["sinks"])
    (ob, lrb), got_fox = flash_fwd(projb, projb, projb, frep, frow, qblk=C_QF // 128, kblk=C_KF // 128,
                                   vblk=C_VF // 128, nq=1, scale=HEAD_DIM ** -0.5, name="fox_fwd", exch=ex("fox"))
    (oc, lrc), got_mla = flash_fwd(qm, km, vm, None, None, qblk=0, kblk=0, vblk=0, nq=2,
                                   scale=MLA_QK_DIM ** -0.5, name="mla_fwd", exch=ex("mla"))
    x2, y1, mT = attn_out(oa, ob, oc, P["gn"], P["w_out"], P["g_apost"], x)
    if ffn_from is not None:
        P = dict(P, **ffn_from(None, got_fox, got_mla))
    up = norm_matmul(x2, P["g_fpre"], P["w_upT"], "up_proj", tn_pref=1536, w_transposed=True,
                     h_transposed=False, exch=ex("up"))
    (u0, h2), got_up = up if ex("up") is not None else (up, None)
    if ffn_from is not None:
        P = dict(P, **ffn_from(got_up, None, None))
    (x3, y2, aT), got_ffn = ffn_fwd(u0, P["conv_w"], P["conv_b"], P["w_down"], P["g_fpost"], x2, exch=ex("ffn"))
    S = dict(x=x, proj=proj, projb=projb, hT=hT, qm=qm, km=km, vm=vm, cqT=cqT, ckvT=ckvT, fcol=fcol, frow=frow,
             oa=oa, lse_a=lse_a, ob=ob, lrb=lrb, oc=oc, lrc=lrc,
             x2=x2, y1=y1, mT=mT, u0=u0, h2=h2, y2=y2, aT=aT)
    return x3, S, P, got_ffn


def _layer_bwd(dx3, P, S, shared, send_attn=None, own_small=False):
    cosr, sinr, bias = shared
    proj = S["proj"]
    G = {}
    got = {}
    ex = lambda arrays: (arrays, False) if send_attn is not None and arrays else None
    (dy2, dugT, duuT, dcg, dcu, G["ffn_post_norm"], dx2, G["ffn_pre_norm"]), got["ffn"] = ffn_bwd(
        dx3, S["y2"], S["u0"], P["conv_w"], P["conv_b"], P["w_down"], P["g_fpost"], P["w_upT"], S["x2"], P["g_fpre"],
        exch=ex(send_attn))
    dconv = jnp.concatenate([dcg[-1], dcu[-1]], axis=1)
    G["conv_w"], G["conv_b"] = dconv[0:3], dconv[3]
    G["w_down"] = matmul_nn(S["aT"], dy2, "dw_down", MXU)
    G["w_upT"] = jnp.concatenate([matmul_nn(dugT, S["h2"], "dw_up_gate", MXU),
                                  matmul_nn(duuT, S["h2"], "dw_up_up", MXU)], axis=0)
    G["w_up"] = G["w_upT"].T
    dy1, doa, dob, doc, G["group_norm"], G["attn_post_norm"] = attn_out_bwd(
        dx2, S["y1"], S["oa"], S["ob"], S["oc"], P["gn"], P["w_out"], P["g_apost"])
    G["w_out"] = matmul_nn(S["mT"], dy1, "dw_out", MXU)
    up_slices = _shards_from_full("w_upT", G["w_upT"])
    (dqa, dka, dva, dbias, dsk), _ = swa_bwd(proj, bias, P["sinks"], doa, S["oa"], S["lse_a"])
    G["swa_sinks"] = dsk[:, 0]
    pb = S["projb"]
    (dqf, dkf, dvf, dFk, dFq), got["fox"] = flash_bwd(
        pb, pb, pb, dob, S["ob"], S["lrb"], S["fcol"], S["frow"], name="fox_bwd", qblk=C_QF // 128,
        kblk=C_KF // 128, vblk=C_VF // 128, nq=1, scale=HEAD_DIM ** -0.5,
        exch=ex([up_slices[:, :UP_HALF], _shards_from_full("w_down", G["w_down"])]))
    dmisc_f, dfb = fox_gate_bwd(dFq, dFk, proj, P["fbias"])
    G["forget_bias"] = dfb[0, 0:4]
    (dqm_, dkm_, dvm_), got["mla"] = flash_bwd(
        S["qm"], S["km"], S["vm"], doc, S["oc"], S["lrc"], None, None, name="mla_bwd",
        qblk=0, kblk=0, vblk=0, nq=2, scale=MLA_QK_DIM ** -0.5, exch=ex([up_slices[:, UP_HALF:]]))
    dqm, dkv, dcq, dckv, dmisc_r, G["q_latent_norm"], G["kv_latent_norm"] = mla_prep_bwd(
        dqm_, dkm_, dvm_, proj, P["gq"], P["gkv"], P["w_uq_p"], P["w_ukv_p"], cosr, sinr)
    G["w_uq"] = _unperm_w_uq(matmul_nn(S["cqT"], dqm, "dw_uq", MXU))
    G["w_ukv"] = _unperm_w_ukv(matmul_nn(S["ckvT"], dkv, "dw_ukv", MXU))
    dproj = jnp.concatenate([dqa, dka, dva, dqf, dkf, dvf, dcq, dckv, dmisc_f + dmisc_r], axis=1).astype(MXU)
    if own_small and send_attn is not None:
        G["w_in_p"], got["dw_in"] = matmul_nn(
            S["hT"], dproj, "dw_in", MXU, exch=([_shards_from_full(n, G[n]) for n in ATTN_SENT[1:]], False))
    else:
        G["w_in_p"] = matmul_nn(S["hT"], dproj, "dw_in", MXU)
    G["w_in"] = _unperm_w_in(G["w_in_p"])
    dx, G["attn_pre_norm"] = matmul_nt_normbwd(dproj, P["w_in_p"], S["x"], P["g_pre"], dx2, "in_bwd")
    return dx, G, dbias, got


def _layer_params(l, full, small):
    return dict(
        g_pre=_row(small["attn_pre_norm"][l]),
        w_in_p=full["w_in_p"] if "w_in_p" in full else _perm_w_in(full["w_in"]),
        gq=_row(small["q_latent_norm"][l]), gkv=_row(small["kv_latent_norm"][l]),
        w_uq_p=_perm_w_uq(full["w_uq"]), w_ukv_p=_perm_w_ukv(full["w_ukv"]),
        fbias=_row(small["forget_bias"][l], 128), sinks=small["swa_sinks"][l].astype(F32),
        gn=_row(small["group_norm"][l]), w_out=full["w_out"], g_apost=_row(small["attn_post_norm"][l]),
        g_fpre=_row(small["ffn_pre_norm"][l]), conv_b=_row(small["conv_b"][l]),
        g_fpost=_row(small["ffn_post_norm"][l]),
        **{n: full[n] for n in FFN_SIDE if n in full},
        **({"w_upT": full["w_up"].T} if "w_up" in full else {}))


def _rel_bias_grad(dbias, bucket):
    flat = dbias.reshape(SWA_Q_HEADS, -1)
    hi = flat.astype(MXU)
    lo = (flat - hi.astype(F32)).astype(MXU)
    onehot = (bucket[:, None] == jnp.arange(128, dtype=jnp.int32)[None, :]).astype(MXU)
    r = matmul_nn(jnp.concatenate([hi, lo], axis=0), onehot, "rel_bias_grad")
    return (r[0:8] + r[8:16])[:, :REL_BUCKETS].T


def local_step(x, tgt, fulls, small, comm=None):
    T = x.shape[0]
    cosr, sinr = rope_tables(T)
    bias, bucket = swa_bias_table(small["rel_bias"])
    shared = (cosr, sinr, bias)
    Ps, Ss = [], []
    h, full = x, fulls[0]
    for l in range(DEPTH):
        P = _layer_params(l, full, small)
        if comm:
            h, S, P, got = _layer_fwd(h, P, shared, comm["weight_parts"](l), comm["ffn_from"])
            full = comm["attn_from"](got) if l + 1 < DEPTH else None
        else:
            h, S, P, _ = _layer_fwd(h, P, shared)
            full = fulls[l + 1] if l + 1 < DEPTH else None
        Ps.append(P)
        Ss.append(S)
    dh, sq = loss_kernel(h, tgt)
    grads = [None] * DEPTH
    dbias_sum = None
    pending = [] if comm else None
    for l in reversed(range(DEPTH)):
        dh, grads[l], dbias, got = _layer_bwd(dh, Ps[l], Ss[l], shared, pending, own_small=l == 0)
        dbias_sum = dbias if dbias_sum is None else dbias_sum + dbias
        if comm:
            comm["landed"](l, ["w_down"], got["fox"][1:])
            comm["landed"](l, ["w_upT"], [jnp.concatenate([got["fox"][0], got["mla"][0]], axis=1)])
            if pending:
                comm["landed"](l + 1, ATTN_SENT, got["ffn"])
            if l == 0:
                comm["landed"](0, ATTN_SENT[1:], got["dw_in"])
                pending = [_shards_from_full("w_in_p", grads[0]["w_in_p"])]
            else:
                pending = [_shards_from_full(n, grads[l][n]) for n in ATTN_SENT]
    return sq, dh, grads, _rel_bias_grad(dbias_sum, bucket), pending


WEIGHTS = ['attn_pre_norm', 'w_in', 'forget_bias', 'swa_sinks', 'rel_bias', 'q_latent_norm', 'w_uq',
           'kv_latent_norm', 'w_ukv', 'group_norm', 'w_out', 'attn_post_norm', 'ffn_pre_norm', 'w_up', 'conv_w',
           'conv_b', 'w_down', 'ffn_post_norm']
SMALL_PER_LAYER = ['attn_pre_norm', 'forget_bias', 'swa_sinks', 'q_latent_norm', 'kv_latent_norm', 'group_norm',
                   'attn_post_norm', 'ffn_pre_norm', 'conv_b', 'ffn_post_norm', 'conv_w']


def kernel(x, attn_pre_norm, w_in, forget_bias, swa_sinks, rel_bias, q_latent_norm, w_uq, kv_latent_norm, w_ukv, group_norm, w_out, attn_post_norm, ffn_pre_norm, w_up, conv_w, conv_b, w_down, ffn_post_norm, loss_target, m_attn_pre_norm, m_w_in, m_forget_bias, m_swa_sinks, m_rel_bias, m_q_latent_norm, m_w_uq, m_kv_latent_norm, m_w_ukv, m_group_norm, m_w_out, m_attn_post_norm, m_ffn_pre_norm, m_w_up, m_conv_w, m_conv_b, m_w_down, m_ffn_post_norm, v_attn_pre_norm, v_w_in, v_forget_bias, v_swa_sinks, v_rel_bias, v_q_latent_norm, v_w_uq, v_kv_latent_norm, v_w_ukv, v_group_norm, v_w_out, v_attn_post_norm, v_ffn_pre_norm, v_w_up, v_conv_w, v_conv_b, v_w_down, v_ffn_post_norm):
    W = dict(attn_pre_norm=attn_pre_norm, w_in=w_in, forget_bias=forget_bias, swa_sinks=swa_sinks, rel_bias=rel_bias,
             q_latent_norm=q_latent_norm, w_uq=w_uq, kv_latent_norm=kv_latent_norm, w_ukv=w_ukv,
             group_norm=group_norm, w_out=w_out, attn_post_norm=attn_post_norm, ffn_pre_norm=ffn_pre_norm,
             w_up=w_up, conv_w=conv_w, conv_b=conv_b, w_down=w_down, ffn_post_norm=ffn_post_norm)
    M = dict(attn_pre_norm=m_attn_pre_norm, w_in=m_w_in, forget_bias=m_forget_bias, swa_sinks=m_swa_sinks,
             rel_bias=m_rel_bias, q_latent_norm=m_q_latent_norm, w_uq=m_w_uq, kv_latent_norm=m_kv_latent_norm,
             w_ukv=m_w_ukv, group_norm=m_group_norm, w_out=m_w_out, attn_post_norm=m_attn_post_norm,
             ffn_pre_norm=m_ffn_pre_norm, w_up=m_w_up, conv_w=m_conv_w, conv_b=m_conv_b, w_down=m_w_down,
             ffn_post_norm=m_ffn_post_norm)
    V = dict(attn_pre_norm=v_attn_pre_norm, w_in=v_w_in, forget_bias=v_forget_bias, swa_sinks=v_swa_sinks,
             rel_bias=v_rel_bias, q_latent_norm=v_q_latent_norm, w_uq=v_w_uq, kv_latent_norm=v_kv_latent_norm,
             w_ukv=v_w_ukv, group_norm=v_group_norm, w_out=v_w_out, attn_post_norm=v_attn_post_norm,
             ffn_pre_norm=v_ffn_pre_norm, w_up=v_w_up, conv_w=v_conv_w, conv_b=v_conv_b, w_down=v_w_down,
             ffn_post_norm=v_ffn_post_norm)
    me = 4 * lax.axis_index("x") + 2 * lax.axis_index("y") + lax.axis_index("c")

    def attn_shards(l):
        return [_perm_w_in(w_in[l].astype(MXU))] + [W[n][l].astype(MXU) for n in ATTN_SENT[1:]]

    def weight_parts(l):
        up = jnp.swapaxes(W["w_up"][l], 0, 1).astype(MXU)
        return dict(up=[W["w_down"][l].astype(MXU)], fox=[up[:UP_HALF]], mla=[up[UP_HALF:], conv_w[l]],
                    ffn=attn_shards(l + 1) if l + 1 < DEPTH else [])

    def ffn_from(got_up, got_fox, got_mla):
        if got_up is not None:
            return dict(w_down=_full_from_shards("w_down", got_up[0]))
        return dict(w_upT=_full_from_shards("w_upT", jnp.concatenate([got_fox[0], got_mla[0]], axis=1)),
                    conv_w=got_mla[1].transpose(1, 0, 2).reshape(3, 2 * D_FF))

    def attn_from(got_ffn):
        return {n: _full_from_shards(n, s) for n, s in zip(ATTN_SENT, got_ffn)}

    landed = [{} for _ in range(DEPTH)]

    def on_landed(l, names, arrays):
        landed[l].update(zip(names, arrays))

    comm = dict(weight_parts=weight_parts, ffn_from=ffn_from, attn_from=attn_from, landed=on_landed)
    full0 = dict(zip(ATTN_SENT, map(_full_from_shards, ATTN_SENT, exchange(attn_shards(0), "gather_weights", True))))
    sq, dx, grads, drel, last = local_step(x[0], loss_target[0], [full0], W, comm)
    G, delta, new_m, new_v = {}, {}, {}, {}

    def update(n, exch=None):
        shp = W[n].shape
        if n == "w_up":
            v3 = lambda a: jnp.swapaxes(a, 1, 2)
            back = v3
            g = [landed[l]["w_upT"] for l in range(DEPTH)]
        else:
            v3 = lambda a: a.reshape(shp if len(shp) == 3 else (1,) + shp)
            back = lambda a: a.reshape(shp)
            g = [landed[l][n] for l in range(DEPTH)] if n in SHARDED else v3(G[n])
        (g, d, nm, nv), got = adamw(v3(W[n]), g, v3(M[n]), v3(V[n]), "adamw_" + n, exch)
        G[n], delta[n], new_m[n], new_v[n] = back(g), back(d), back(nm), back(nv)
        return got

    parts, shapes = [], []
    for l in range(DEPTH):
        for n in SMALL_PER_LAYER:
            parts.append(grads[l][n].astype(F32).reshape(-1))
            shapes.append(grads[l][n].shape)
    parts += [drel.reshape(-1), jnp.sum(sq).reshape(1) * (0.5 / D_MODEL)]
    shapes += [drel.shape, (1,)]
    on_landed(0, ATTN_SENT[:1], update("w_up", (last, False)))
    for l in range(DEPTH):
        landed[l]["w_in"] = _unperm_w_in(landed[l]["w_in_p"])
    gathered = update("w_down", ([_pack(parts, 128, 8, F32)], True))[0]
    red = _unpack(sum_devices(gathered, "sum_small"), shapes)
    k = 0
    per = {n: [] for n in SMALL_PER_LAYER}
    for l in range(DEPTH):
        for n in SMALL_PER_LAYER:
            per[n].append(red[k])
            k += 1
    for n in SMALL_PER_LAYER:
        G[n] = jnp.stack(per[n]).reshape((DEPTH, 3, 2 * D_FF) if n == "conv_w" else W[n].shape)
    G["rel_bias"] = red[k]
    loss = red[k + 1][0]
    G["conv_w"] = lax.dynamic_slice_in_dim(G["conv_w"], me * 704, 704, axis=2)

    for n in WEIGHTS:
        if n not in ("w_up", "w_down"):
            update(n)
    return (loss, dx[None], *[G[n] for n in WEIGHTS], *[delta[n] for n in WEIGHTS],
            *[new_m[n] for n in WEIGHTS], *[new_v[n] for n in WEIGHTS])
```
